```python
import jax, jax.numpy as jnp
from jax import lax
import numpy as np

D_MODEL = 1024
BATCH = 8
SEQ = 2048
DEPTH = 2

D_MIX = D_MODEL
D_POOL = D_MIX // 2
D_SGU = D_MIX - D_POOL
POOL_WINDOWS = (2, 4, 8, 16)
N_POOL_GROUPS = len(POOL_WINDOWS)
POOL_GROUP = D_POOL // N_POOL_GROUPS
N_SGU_HEADS = 4
SGU_HEAD = D_SGU // N_SGU_HEADS
CHUNK = 128
SPLITS = (D_POOL, 2 * D_POOL, 2 * D_POOL + D_SGU, 2 * D_POOL + 2 * D_SGU)
D_IN_PROJ = 2 * D_POOL + 3 * D_SGU
DEEPNORM_ALPHA = (2.0 * DEPTH) ** 0.25
DEEPNORM_BETA = (8.0 * DEPTH) ** -0.25
LN_EPS = 1e-5

kernel_name = "hybrid_pool_sgu_deepnorm_adaln"


def _layer_norm(x):
    xf = x.astype(jnp.float32)
    mu = jnp.mean(xf, axis=-1, keepdims=True)
    var = jnp.mean(jnp.square(xf - mu), axis=-1, keepdims=True)
    return ((xf - mu) * lax.rsqrt(var + LN_EPS)).astype(x.dtype)


def _pool_mixer(xa, w_pool, pool_scale):
    B, S, _ = xa.shape
    xg = xa.reshape(B, S, N_POOL_GROUPS, POOL_GROUP)
    cs = jnp.cumsum(xg.astype(jnp.float32), axis=1)
    cs = jnp.pad(cs, ((0, 0), (1, 0), (0, 0), (0, 0)))
    t = jnp.arange(S)
    pooled = []
    for g, w in enumerate(POOL_WINDOWS):
        lo = jnp.maximum(t + 1 - w, 0)
        cs_g = cs[:, :, g, :]
        win_sum = cs_g[:, 1:] - cs_g[:, lo]
        count = (t + 1 - lo).astype(jnp.float32)
        pooled.append(win_sum / count[None, :, None])
    pooled = jnp.stack(pooled, axis=2).astype(xa.dtype) - xg
    y = jnp.einsum('bsgc,gcd->bsgd', pooled, w_pool)
    return y.reshape(B, S, D_POOL) * pool_scale


def _sgu_mixer(u, v, ln_g, ln_b, w_s, b_s):
    B, S, _ = u.shape
    n_chunks = S // CHUNK
    v = v.reshape(B, n_chunks, CHUNK, N_SGU_HEADS, SGU_HEAD)
    v = _layer_norm(v) * ln_g + ln_b
    causal = jnp.tril(jnp.ones((CHUNK, CHUNK), dtype=bool))
    w = jnp.where(causal[None], w_s, 0)
    mixed = jnp.einsum('hts,bnshd->bnthd', w, v) + jnp.transpose(b_s)[:, :, None]
    return u * mixed.reshape(B, S, D_SGU)


def _hybrid_layer(x, c, w_ada, b_ada, w_in, w_pool, pool_scale,
                  sgu_ln_g, sgu_ln_b, w_sgu, b_sgu, w_out, ln_g, ln_b):
    mod = jax.nn.silu(c) @ w_ada + b_ada
    shift, scale, gate = jnp.split(mod, 3, axis=-1)
    h = _layer_norm(x) * (1 + scale[:, None]) + shift[:, None]
    proj = h @ w_in
    xa, ga, u, v, gb = jnp.split(proj, SPLITS, axis=-1)
    ya = _pool_mixer(xa, w_pool, pool_scale) * jax.nn.silu(ga)
    yb = _sgu_mixer(jax.nn.gelu(u, approximate=False), jax.nn.gelu(v, approximate=False),
                    sgu_ln_g, sgu_ln_b, w_sgu, b_sgu) * jax.nn.silu(gb)
    y = jnp.concatenate([ya, yb], axis=-1) @ w_out
    z = DEEPNORM_ALPHA * x + gate[:, None] * y
    return _layer_norm(z) * ln_g + ln_b


def _fwd_setup_inputs(seed: int = 0) -> dict:
    key = jax.random.key(seed)
    ks = jax.random.split(key, 16)
    f32 = jnp.float32
    x = jax.random.normal(ks[0], (BATCH, SEQ, D_MODEL), f32)
    c = jax.random.normal(ks[1], (BATCH, D_MODEL), f32)
    w_ada = jax.random.normal(ks[2], (DEPTH, D_MODEL, 3 * D_MODEL), f32) * (0.5 * D_MODEL ** -0.5)
    b_ada = jax.random.normal(ks[3], (DEPTH, 3 * D_MODEL), f32) * 0.02
    w_in = jax.random.normal(ks[4], (DEPTH, D_MODEL, D_IN_PROJ), f32) * D_MODEL ** -0.5
    w_pool = jax.random.normal(ks[5], (DEPTH, N_POOL_GROUPS, POOL_GROUP, POOL_GROUP), f32) * POOL_GROUP ** -0.5
    pool_scale = 1.0 + 0.1 * jax.random.normal(ks[6], (DEPTH, D_POOL), f32)
    sgu_ln_g = 1.0 + 0.1 * jax.random.normal(ks[7], (DEPTH, N_SGU_HEADS, SGU_HEAD), f32)
    sgu_ln_b = 0.02 * jax.random.normal(ks[8], (DEPTH, N_SGU_HEADS, SGU_HEAD), f32)
    w_sgu = jax.random.normal(ks[9], (DEPTH, N_SGU_HEADS, CHUNK, CHUNK), f32) * CHUNK ** -0.5
    b_sgu = 1.0 + 0.1 * jax.random.normal(ks[10], (DEPTH, N_SGU_HEADS, CHUNK), f32)
    w_out = jax.random.normal(ks[11], (DEPTH, D_MIX, D_MODEL), f32) * (D_MIX ** -0.5 * DEEPNORM_BETA)
    ln_g = 1.0 + 0.1 * jax.random.normal(ks[12], (DEPTH, D_MODEL), f32)
    ln_b = 0.02 * jax.random.normal(ks[13], (DEPTH, D_MODEL), f32)
    return {"x": x, "c": c, "w_ada": w_ada, "b_ada": b_ada, "w_in": w_in,
            "w_pool": w_pool, "pool_scale": pool_scale, "sgu_ln_g": sgu_ln_g,
            "sgu_ln_b": sgu_ln_b, "w_sgu": w_sgu, "b_sgu": b_sgu, "w_out": w_out,
            "ln_g": ln_g, "ln_b": ln_b}


def _fwd_reference(x, c, w_ada, b_ada, w_in, w_pool, pool_scale, sgu_ln_g, sgu_ln_b,
              w_sgu, b_sgu, w_out, ln_g, ln_b):
    for l in range(DEPTH):
        x = _hybrid_layer(x, c, w_ada[l], b_ada[l], w_in[l], w_pool[l], pool_scale[l],
                          sgu_ln_g[l], sgu_ln_b[l], w_sgu[l], b_sgu[l], w_out[l],
                          ln_g[l], ln_b[l])
    return x


import jax as _jax
import jax.numpy as _jnp

TWIN_FORMAT = 'train_step'
FWD_PARAMS = ['x', 'c', 'w_ada', 'b_ada', 'w_in', 'w_pool', 'pool_scale', 'sgu_ln_g', 'sgu_ln_b', 'w_sgu', 'b_sgu', 'w_out', 'ln_g', 'ln_b']
TWIN_WEIGHTS = ['w_ada', 'b_ada', 'w_in', 'w_pool', 'pool_scale', 'sgu_ln_g', 'sgu_ln_b', 'w_sgu', 'b_sgu', 'w_out', 'ln_g', 'ln_b']
TWIN_DIFF_INPUT = 'x'
TWIN_INPUTS = ['x', 'c', 'w_ada', 'b_ada', 'w_in', 'w_pool', 'pool_scale', 'sgu_ln_g', 'sgu_ln_b', 'w_sgu', 'b_sgu', 'w_out', 'ln_g', 'ln_b', 'loss_target', 'm_w_ada', 'm_b_ada', 'm_w_in', 'm_w_pool', 'm_pool_scale', 'm_sgu_ln_g', 'm_sgu_ln_b', 'm_w_sgu', 'm_b_sgu', 'm_w_out', 'm_ln_g', 'm_ln_b', 'v_w_ada', 'v_b_ada', 'v_w_in', 'v_w_pool', 'v_pool_scale', 'v_sgu_ln_g', 'v_sgu_ln_b', 'v_w_sgu', 'v_b_sgu', 'v_w_out', 'v_ln_g', 'v_ln_b']
TWIN_OUTPUTS = ['loss', 'grad_x', 'grad_w_ada', 'grad_b_ada', 'grad_w_in', 'grad_w_pool', 'grad_pool_scale', 'grad_sgu_ln_g', 'grad_sgu_ln_b', 'grad_w_sgu', 'grad_b_sgu', 'grad_w_out', 'grad_ln_g', 'grad_ln_b', 'delta_w_ada', 'delta_b_ada', 'delta_w_in', 'delta_w_pool', 'delta_pool_scale', 'delta_sgu_ln_g', 'delta_sgu_ln_b', 'delta_w_sgu', 'delta_b_sgu', 'delta_w_out', 'delta_ln_g', 'delta_ln_b', 'new_m_w_ada', 'new_m_b_ada', 'new_m_w_in', 'new_m_w_pool', 'new_m_pool_scale', 'new_m_sgu_ln_g', 'new_m_sgu_ln_b', 'new_m_w_sgu', 'new_m_b_sgu', 'new_m_w_out', 'new_m_ln_g', 'new_m_ln_b', 'new_v_w_ada', 'new_v_b_ada', 'new_v_w_in', 'new_v_w_pool', 'new_v_pool_scale', 'new_v_sgu_ln_g', 'new_v_sgu_ln_b', 'new_v_w_sgu', 'new_v_b_sgu', 'new_v_w_out', 'new_v_ln_g', 'new_v_ln_b']
TWIN_LEAF_KINDS = {'loss': 'loss', 'grad_x': 'grad_x', 'grad_w_ada': 'grad_w', 'grad_b_ada': 'grad_w', 'grad_w_in': 'grad_w', 'grad_w_pool': 'grad_w', 'grad_pool_scale': 'grad_w', 'grad_sgu_ln_g': 'grad_w', 'grad_sgu_ln_b': 'grad_w', 'grad_w_sgu': 'grad_w', 'grad_b_sgu': 'grad_w', 'grad_w_out': 'grad_w', 'grad_ln_g': 'grad_w', 'grad_ln_b': 'grad_w', 'delta_w_ada': 'delta_w', 'delta_b_ada': 'delta_w', 'delta_w_in': 'delta_w', 'delta_w_pool': 'delta_w', 'delta_pool_scale': 'delta_w', 'delta_sgu_ln_g': 'delta_w', 'delta_sgu_ln_b': 'delta_w', 'delta_w_sgu': 'delta_w', 'delta_b_sgu': 'delta_w', 'delta_w_out': 'delta_w', 'delta_ln_g': 'delta_w', 'delta_ln_b': 'delta_w', 'new_m_w_ada': 'new_m', 'new_m_b_ada': 'new_m', 'new_m_w_in': 'new_m', 'new_m_w_pool': 'new_m', 'new_m_pool_scale': 'new_m', 'new_m_sgu_ln_g': 'new_m', 'new_m_sgu_ln_b': 'new_m', 'new_m_w_sgu': 'new_m', 'new_m_b_sgu': 'new_m', 'new_m_w_out': 'new_m', 'new_m_ln_g': 'new_m', 'new_m_ln_b': 'new_m', 'new_v_w_ada': 'new_v', 'new_v_b_ada': 'new_v', 'new_v_w_in': 'new_v', 'new_v_w_pool': 'new_v', 'new_v_pool_scale': 'new_v', 'new_v_sgu_ln_g': 'new_v', 'new_v_sgu_ln_b': 'new_v', 'new_v_w_sgu': 'new_v', 'new_v_b_sgu': 'new_v', 'new_v_w_out': 'new_v', 'new_v_ln_g': 'new_v', 'new_v_ln_b': 'new_v'}


def _forward(args):
    return _fwd_reference(*[args[k] for k in FWD_PARAMS])


def _output_shape():
    out = _jax.eval_shape(lambda: _forward(_fwd_setup_inputs(0)))
    return out.shape, out.dtype

N_MICROBATCH = 1
ADAM_LR = 0.001
ADAM_B1 = 0.9
ADAM_B2 = 0.999
ADAM_EPS = 1e-08
ADAM_WD = 0.01
ADAM_STEP = 10
PER_EXAMPLE_BATCH_AXIS = {'x': 0, 'c': 0, 'loss_target': 0}
SHARED_INPUTS = []
_WEIGHT_DTYPES = {'w_ada': _jnp.float32, 'b_ada': _jnp.float32, 'w_in': _jnp.float32, 'w_pool': _jnp.float32, 'pool_scale': _jnp.float32, 'sgu_ln_g': _jnp.float32, 'sgu_ln_b': _jnp.float32, 'w_sgu': _jnp.float32, 'b_sgu': _jnp.float32, 'w_out': _jnp.float32, 'ln_g': _jnp.float32, 'ln_b': _jnp.float32}
MOMENT_SCALE = {'w_ada': 1.206592e-02, 'b_ada': 2.027636e-02, 'w_in': 7.920873e-03, 'w_pool': 8.474274e-03, 'pool_scale': 9.133966e-03, 'sgu_ln_g': 4.732930e-03, 'sgu_ln_b': 4.874592e-03, 'w_sgu': 4.692024e-03, 'b_sgu': 6.727247e-03, 'w_out': 1.665893e-02, 'ln_g': 1.178810e+01, 'ln_b': 3.533053e-01}


def _to_microbatches(a, axis):
    t = _jnp.moveaxis(a, axis, 0)
    t = t.reshape((N_MICROBATCH, t.shape[0] // N_MICROBATCH) + t.shape[1:])
    return _jnp.moveaxis(t, 1, axis + 1)


def setup_inputs(seed: int = 0) -> dict:
    inp = _fwd_setup_inputs(seed)
    key = _jax.random.fold_in(_jax.random.key(seed), 7919)
    shape, _ = _output_shape()
    out = dict(inp)
    out["loss_target"] = _jax.random.normal(_jax.random.fold_in(key, 0), shape, _jnp.float32)
    for i, name in enumerate(TWIN_WEIGHTS):
        w = inp[name].astype(_jnp.float32)
        if MOMENT_SCALE is None:
            s = _jnp.sqrt(_jnp.mean(_jnp.square(w)) + 1e-30)
        else:
            s = MOMENT_SCALE[name]
        km, kv = _jax.random.split(_jax.random.fold_in(key, i + 1))
        out[name] = w
        out["m_" + name] = s * _jax.random.normal(km, w.shape, _jnp.float32)
        out["v_" + name] = (s * s) * _jax.random.uniform(kv, w.shape, _jnp.float32, 0.5, 1.5)
    if N_MICROBATCH > 1:
        for name, axis in PER_EXAMPLE_BATCH_AXIS.items():
            out[name] = _to_microbatches(out[name], axis)
    return {'x': out['x'], 'c': out['c'], 'w_ada': out['w_ada'], 'b_ada': out['b_ada'], 'w_in': out['w_in'], 'w_pool': out['w_pool'], 'pool_scale': out['pool_scale'], 'sgu_ln_g': out['sgu_ln_g'], 'sgu_ln_b': out['sgu_ln_b'], 'w_sgu': out['w_sgu'], 'b_sgu': out['b_sgu'], 'w_out': out['w_out'], 'ln_g': out['ln_g'], 'ln_b': out['ln_b'], 'loss_target': out['loss_target'], 'm_w_ada': out['m_w_ada'], 'm_b_ada': out['m_b_ada'], 'm_w_in': out['m_w_in'], 'm_w_pool': out['m_w_pool'], 'm_pool_scale': out['m_pool_scale'], 'm_sgu_ln_g': out['m_sgu_ln_g'], 'm_sgu_ln_b': out['m_sgu_ln_b'], 'm_w_sgu': out['m_w_sgu'], 'm_b_sgu': out['m_b_sgu'], 'm_w_out': out['m_w_out'], 'm_ln_g': out['m_ln_g'], 'm_ln_b': out['m_ln_b'], 'v_w_ada': out['v_w_ada'], 'v_b_ada': out['v_b_ada'], 'v_w_in': out['v_w_in'], 'v_w_pool': out['v_w_pool'], 'v_pool_scale': out['v_pool_scale'], 'v_sgu_ln_g': out['v_sgu_ln_g'], 'v_sgu_ln_b': out['v_sgu_ln_b'], 'v_w_sgu': out['v_w_sgu'], 'v_b_sgu': out['v_b_sgu'], 'v_w_out': out['v_w_out'], 'v_ln_g': out['v_ln_g'], 'v_ln_b': out['v_ln_b']}


def _loss(weights, diff, rest, loss_target):
    with _jax.named_scope("forward"):
        args = {**rest, TWIN_DIFF_INPUT: diff, **{k: w.astype(_WEIGHT_DTYPES[k]) for k, w in weights.items()}}
        y = _forward(args)
    with _jax.named_scope("loss_head"):
        err = _jnp.square(y.astype(_jnp.float32) - loss_target)
        return 0.5 * _jnp.sum(_jnp.mean(err, axis=-1)) if err.ndim else 0.5 * err


def _adamw(w, g, m, v):
    m = ADAM_B1 * m + (1.0 - ADAM_B1) * g
    v = ADAM_B2 * v + (1.0 - ADAM_B2) * _jnp.square(g)
    m_hat = m / (1.0 - ADAM_B1 ** ADAM_STEP)
    v_hat = v / (1.0 - ADAM_B2 ** ADAM_STEP)
    delta = -ADAM_LR * (m_hat / (_jnp.sqrt(v_hat) + ADAM_EPS) + ADAM_WD * w)
    return delta, m, v


def reference(x, c, w_ada, b_ada, w_in, w_pool, pool_scale, sgu_ln_g, sgu_ln_b, w_sgu, b_sgu, w_out, ln_g, ln_b, loss_target, m_w_ada, m_b_ada, m_w_in, m_w_pool, m_pool_scale, m_sgu_ln_g, m_sgu_ln_b, m_w_sgu, m_b_sgu, m_w_out, m_ln_g, m_ln_b, v_w_ada, v_b_ada, v_w_in, v_w_pool, v_pool_scale, v_sgu_ln_g, v_sgu_ln_b, v_w_sgu, v_b_sgu, v_w_out, v_ln_g, v_ln_b):
    given = dict(x=x, c=c, w_ada=w_ada, b_ada=b_ada, w_in=w_in, w_pool=w_pool, pool_scale=pool_scale, sgu_ln_g=sgu_ln_g, sgu_ln_b=sgu_ln_b, w_sgu=w_sgu, b_sgu=b_sgu, w_out=w_out, ln_g=ln_g, ln_b=ln_b, loss_target=loss_target, m_w_ada=m_w_ada, m_b_ada=m_b_ada, m_w_in=m_w_in, m_w_pool=m_w_pool, m_pool_scale=m_pool_scale, m_sgu_ln_g=m_sgu_ln_g, m_sgu_ln_b=m_sgu_ln_b, m_w_sgu=m_w_sgu, m_b_sgu=m_b_sgu, m_w_out=m_w_out, m_ln_g=m_ln_g, m_ln_b=m_ln_b, v_w_ada=v_w_ada, v_b_ada=v_b_ada, v_w_in=v_w_in, v_w_pool=v_w_pool, v_pool_scale=v_pool_scale, v_sgu_ln_g=v_sgu_ln_g, v_sgu_ln_b=v_sgu_ln_b, v_w_sgu=v_w_sgu, v_b_sgu=v_b_sgu, v_w_out=v_w_out, v_ln_g=v_ln_g, v_ln_b=v_ln_b)
    weights = {n: given[n] for n in TWIN_WEIGHTS}
    shared = {n: given[n] for n in SHARED_INPUTS}
    per_example = {n: given[n] for n in ['x', 'c']}
    grad_fn = _jax.value_and_grad(_loss, argnums=(0, 1))

    def one_microbatch(ex, loss_target):
        ex = dict(ex)
        diff = ex.pop(TWIN_DIFF_INPUT)
        return grad_fn(weights, diff, {**shared, **ex}, loss_target)

    if N_MICROBATCH == 1:
        loss, (grad_w, grad_x) = one_microbatch(per_example, given["loss_target"])
    else:
        def body(carry, xs):
            loss_sum, grad_sum = carry
            l_k, (gw_k, gx_k) = one_microbatch(xs[0], xs[1])
            with _jax.named_scope("update"):
                return (loss_sum + l_k, _jax.tree.map(_jnp.add, grad_sum, gw_k)), gx_k

        init = (_jnp.zeros((), _jnp.float32), _jax.tree.map(_jnp.zeros_like, weights))
        (loss, grad_w), grad_x = _jax.lax.scan(body, init, (per_example, given["loss_target"]))
    with _jax.named_scope("update"):
        delta_w, new_m, new_v = {}, {}, {}
        for n in TWIN_WEIGHTS:
            delta_w[n], new_m[n], new_v[n] = _adamw(weights[n], grad_w[n], given["m_" + n], given["v_" + n])
    return (loss, grad_x, *[grad_w[n] for n in TWIN_WEIGHTS], *[delta_w[n] for n in TWIN_WEIGHTS],
            *[new_m[n] for n in TWIN_WEIGHTS], *[new_v[n] for n in TWIN_WEIGHTS])
```

```python
import functools
import math

import jax
import jax.numpy as jnp
from jax import lax
from jax.experimental import pallas as pl
from jax.experimental.pallas import tpu as pltpu

F32 = jnp.float32
BF16 = jnp.bfloat16

D_MODEL = 1024
SEQ = 2048
DEPTH = 2
D_POOL = 512
D_SGU = 512
D_IN = 2560
N_GROUPS = 4
GROUP = 128
N_HEADS = 4
HEAD = 128
CHUNK = 128
WINDOWS = (2, 4, 8, 16)
ALPHA = (2.0 * DEPTH) ** 0.25
LN_EPS = 1e-5
N_DEV = 8

ADAM_LR = 0.001
ADAM_B1 = 0.9
ADAM_B2 = 0.999
ADAM_EPS = 1e-08
ADAM_WD = 0.01
ADAM_STEP = 10

TM = 256
HALO = 16
N_TILES = SEQ // TM
TK = 512
VMEM_LIMIT = 56 * 1024 * 1024

ROW_WPOOL = 0
ROW_WSGU = 512
ROW_PSCALE = 1024
ROW_SLNG = 1028
ROW_SLNB = 1032
ROW_BSGU = 1036
ROW_LNG = 1040
ROW_LNB = 1048
PACK_ROWS = 1088

SQRT_HALF = 0.7071067811865476
INV_SQRT_2PI = 0.3989422804014327


def _ln(x):
    mu = jnp.mean(x, axis=-1, keepdims=True)
    xc = x - mu
    var = jnp.mean(xc * xc, axis=-1, keepdims=True)
    rstd = lax.rsqrt(var + LN_EPS)
    return xc * rstd, rstd


def _ln_bwd(dxn, xn, rstd):
    m1 = jnp.mean(dxn, axis=-1, keepdims=True)
    m2 = jnp.mean(dxn * xn, axis=-1, keepdims=True)
    return rstd * (dxn - m1 - xn * m2)


def _gelu(x):
    return 0.5 * x * (1.0 + lax.erf(x * SQRT_HALF))


def _gelu_grad(x):
    cdf = 0.5 * (1.0 + lax.erf(x * SQRT_HALF))
    return cdf + x * (INV_SQRT_2PI * jnp.exp(-0.5 * x * x))


def _silu_parts(x):
    s = jax.nn.sigmoid(x)
    return x * s, s * (1.0 + x * (1.0 - s))


def _dot(a, b):
    return lax.dot_general(a, b, (((1,), (0,)), ((), ())), preferred_element_type=F32)


def _dot_nt(a, b):
    return lax.dot_general(a, b, (((1,), (1,)), ((), ())), preferred_element_type=F32)


def _dot_tn(a, b):
    return lax.dot_general(a, b, (((0,), (0,)), ((), ())), preferred_element_type=F32)


def _row_index(tile):
    return tile * TM + lax.broadcasted_iota(jnp.int32, (TM, 1), 0)


def _window_sums(ext, forward):
    n = TM + HALO
    cur = ext
    outs = []
    for g in range(N_GROUPS):
        step = 1 << g
        cur = cur + pltpu.roll(cur, step if forward else n - step, 0)
        rows = cur[HALO:, :GROUP] if forward else cur[:TM, :GROUP]
        outs.append(rows)
        cur = cur[:, GROUP:] if g + 1 < N_GROUPS else None
    return outs


def _counts(rows):
    return [jnp.minimum(rows + 1, w).astype(F32) for w in WINDOWS]


def _tril_bf16(w):
    t = lax.broadcasted_iota(jnp.int32, (CHUNK, CHUNK), 0)
    s = lax.broadcasted_iota(jnp.int32, (CHUNK, CHUNK), 1)
    return jnp.where(t >= s, w, 0.0).astype(BF16)


def _mix_forward(proj, halo, tile, wpool_ref, pscale, slng, slnb, wsgu_ref, bsgut_ref, keep):
    rows = _row_index(tile)
    counts = _counts(rows)
    xa = proj[:, 0:D_POOL]
    ga = proj[:, D_POOL:2 * D_POOL]
    sums = _window_sums(jnp.concatenate([halo, xa], axis=0), True)
    ga_act, ga_grad = _silu_parts(ga)
    pooled, pw, ya = [], [], []
    for g in range(N_GROUPS):
        sl = slice(g * GROUP, (g + 1) * GROUP)
        p = (sums[g] / counts[g] - xa[:, sl]).astype(BF16)
        q = _dot(p, wpool_ref[g].astype(BF16))
        pooled.append(p)
        pw.append(q)
        ya.append(q * pscale[:, sl] * ga_act[:, sl])

    u = proj[:, 2 * D_POOL:2 * D_POOL + D_SGU]
    v = proj[:, 2 * D_POOL + D_SGU:2 * D_POOL + 2 * D_SGU]
    gb = proj[:, 2 * D_POOL + 2 * D_SGU:]
    gb_act, gb_grad = _silu_parts(gb)
    u_act = _gelu(u)
    v_act = _gelu(v)
    vn, vrstd, vln, mixed, yb = [], [], [], [], []
    for h in range(N_HEADS):
        sl = slice(h * HEAD, (h + 1) * HEAD)
        n_h, r_h = _ln(v_act[:, sl])
        l_h = (n_h * slng[:, sl] + slnb[:, sl]).astype(BF16)
        w_h = _tril_bf16(wsgu_ref[h])
        bias = bsgut_ref[:, h:h + 1]
        m_h = jnp.concatenate(
            [_dot(w_h, l_h[k * CHUNK:(k + 1) * CHUNK]) + bias for k in range(TM // CHUNK)], axis=0)
        vn.append(n_h)
        vrstd.append(r_h)
        vln.append(l_h)
        mixed.append(m_h)
        yb.append(u_act[:, sl] * m_h * gb_act[:, sl])
    cat = jnp.concatenate(ya + yb, axis=1)
    if not keep:
        return cat, None
    return cat, dict(counts=counts, ga_act=ga_act, ga_grad=ga_grad, pooled=pooled, pw=pw, u=u, v=v, u_act=u_act,
                     gb_act=gb_act, gb_grad=gb_grad, vn=vn, vrstd=vrstd, vln=vln, mixed=mixed)


def _const_spec(shape):
    nd = len(shape)
    return pl.BlockSpec(shape, lambda i: (0,) * nd)


def _layer_forward(x, mod_l, w_int, w_outf, w_pool, pscale, slng, slnb, w_sgu, bsgut, ln_g, ln_b, name):
    def body(x_ref, mod_ref, wint_ref, wout_ref, wpool_ref, pscale_ref, slng_ref, slnb_ref, wsgu_ref, bsgut_ref,
             lng_ref, lnb_ref, out_ref, proj_ref, y_ref, halo_ref):
        tile = pl.program_id(0)

        @pl.when(tile == 0)
        def _():
            halo_ref[...] = jnp.zeros_like(halo_ref)

        xt = x_ref[...]
        shift = mod_ref[:, 0:D_MODEL]
        scale = mod_ref[:, D_MODEL:2 * D_MODEL]
        gate = mod_ref[:, 2 * D_MODEL:]
        xn, _ = _ln(xt)
        h = (xn * (1.0 + scale) + shift).astype(BF16)
        proj = _dot_nt(h, wint_ref[...])
        proj_ref[...] = proj
        cat, _ = _mix_forward(proj, halo_ref[...], tile, wpool_ref, pscale_ref[...], slng_ref[...], slnb_ref[...],
                              wsgu_ref, bsgut_ref, False)
        halo_ref[...] = proj[TM - HALO:, 0:D_POOL]
        y = _dot(cat.astype(BF16), wout_ref[...])
        y_ref[...] = y
        zn, _ = _ln(ALPHA * xt + gate * y)
        out_ref[...] = zn * lng_ref[...] + lnb_ref[...]

    row = lambda w: pl.BlockSpec((TM, w), lambda i: (i, 0))
    return pl.pallas_call(
        body,
        name=name,
        grid=(N_TILES,),
        in_specs=[row(D_MODEL), _const_spec((1, 3 * D_MODEL)), _const_spec((D_IN, D_MODEL)),
                  _const_spec((D_MODEL, D_MODEL)), _const_spec((N_GROUPS, GROUP, GROUP)), _const_spec((1, D_POOL)),
                  _const_spec((1, D_SGU)), _const_spec((1, D_SGU)), _const_spec((N_HEADS, CHUNK, CHUNK)),
                  _const_spec((CHUNK, N_HEADS)), _const_spec((1, D_MODEL)), _const_spec((1, D_MODEL))],
        out_specs=[row(D_MODEL), row(D_IN), row(D_MODEL)],
        out_shape=[jax.ShapeDtypeStruct((SEQ, D_MODEL), F32), jax.ShapeDtypeStruct((SEQ, D_IN), F32),
                   jax.ShapeDtypeStruct((SEQ, D_MODEL), F32)],
        scratch_shapes=[pltpu.VMEM((HALO, D_POOL), F32)],
        compiler_params=pltpu.CompilerParams(dimension_semantics=("arbitrary",), vmem_limit_bytes=VMEM_LIMIT),
    )(x, mod_l, w_int, w_outf, w_pool, pscale, slng, slnb, w_sgu, bsgut, ln_g, ln_b)


VEC_LNG, VEC_LNB, VEC_POOL, VEC_SGU, VEC_SHIFT, VEC_SCALE, VEC_GATE, VEC_LOSS = range(8)


def _layer_backward(a, b, x, proj, y, mod_l, w_int, w_outf, w_pool, pscale, slng, slnb, w_sgu, bsgut, ln_g, is_last,
                    name):
    def body(a_ref, b_ref, x_ref, proj_ref, prev_ref, y_ref, mod_ref, wint_ref, wout_ref, wpool_ref, pscale_ref,
             slng_ref, slnb_ref, wsgu_ref, bsgut_ref, lng_ref,
             dx_ref, dproj_ref, h_ref, cat_ref, dy_ref, small_ref, dmod_ref, loss_ref,
             vec_ref, dmix_ref, halo_ref):
        step = pl.program_id(0)
        tile = N_TILES - 1 - step

        @pl.when(step == 0)
        def _():
            small_ref[...] = jnp.zeros_like(small_ref)
            vec_ref[...] = jnp.zeros_like(vec_ref)
            dmix_ref[...] = jnp.zeros_like(dmix_ref)
            halo_ref[...] = jnp.zeros_like(halo_ref)

        def acc(row, lo, val):
            hi = lo + val.shape[1]
            vec_ref[row:row + 1, lo:hi] += jnp.sum(val, axis=0, keepdims=True)

        xt = x_ref[...]
        yt = y_ref[...]
        shift = mod_ref[:, 0:D_MODEL]
        scale = mod_ref[:, D_MODEL:2 * D_MODEL]
        gate = mod_ref[:, 2 * D_MODEL:]

        zn, zrstd = _ln(ALPHA * xt + gate * yt)
        if is_last:
            diff = a_ref[...] - b_ref[...]
            acc(VEC_LOSS, 0, diff * diff)
            dout = diff * (1.0 / D_MODEL)
        else:
            dout = a_ref[...]
        acc(VEC_LNG, 0, dout * zn)
        acc(VEC_LNB, 0, dout)
        dz = _ln_bwd(dout * lng_ref[...], zn, zrstd)
        acc(VEC_GATE, 0, dz * yt)
        dy = (dz * gate).astype(BF16)
        dy_ref[...] = dy
        dcat = _dot_nt(dy, wout_ref[...])

        proj = proj_ref[...]
        prev = jnp.where(tile > 0, prev_ref[...], 0.0)
        cat, k = _mix_forward(proj, prev, tile, wpool_ref, pscale_ref[...], slng_ref[...], slnb_ref[...],
                              wsgu_ref, bsgut_ref, True)
        cat_ref[...] = cat.astype(BF16)
        pscale = pscale_ref[...]
        slng = slng_ref[...]

        dga, dq = [], []
        for g in range(N_GROUPS):
            sl = slice(g * GROUP, (g + 1) * GROUP)
            dya = dcat[:, sl]
            dyp = dya * k["ga_act"][:, sl]
            dga.append(dya * k["pw"][g] * pscale[:, sl] * k["ga_grad"][:, sl])
            acc(VEC_POOL, g * GROUP, dyp * k["pw"][g])
            dpw = (dyp * pscale[:, sl]).astype(BF16)
            small_ref[ROW_WPOOL + g * GROUP:ROW_WPOOL + (g + 1) * GROUP, :] += _dot_tn(k["pooled"][g], dpw)
            dq.append(_dot_nt(dpw, wpool_ref[g].astype(BF16)))
        dpooled = jnp.concatenate(dq, axis=1)
        scaled = jnp.concatenate([dq[g] / k["counts"][g] for g in range(N_GROUPS)], axis=1)
        sums = _window_sums(jnp.concatenate([scaled, halo_ref[...]], axis=0), False)
        halo_ref[...] = scaled[0:HALO]
        dxa = jnp.concatenate(sums, axis=1) - dpooled

        du, dv, dgb = [], [], []
        for h in range(N_HEADS):
            sl = slice(h * HEAD, (h + 1) * HEAD)
            dyb = dcat[:, D_POOL + h * HEAD:D_POOL + (h + 1) * HEAD]
            m_h = k["mixed"][h]
            ug = k["u_act"][:, sl] * dyb
            du.append(dyb * m_h * k["gb_act"][:, sl] * _gelu_grad(k["u"][:, sl]))
            dgb.append(ug * m_h * k["gb_grad"][:, sl])
            dmixed = ug * k["gb_act"][:, sl]
            dmixed_bf = dmixed.astype(BF16)
            w_h = _tril_bf16(wsgu_ref[h])
            dvln_parts = []
            dmix_sum = dmix_ref[h]
            dws = small_ref[ROW_WSGU + h * CHUNK:ROW_WSGU + (h + 1) * CHUNK, :]
            for c in range(TM // CHUNK):
                cs = slice(c * CHUNK, (c + 1) * CHUNK)
                dmix_sum = dmix_sum + dmixed[cs]
                dws = dws + _dot_nt(dmixed_bf[cs], k["vln"][h][cs])
                dvln_parts.append(_dot_tn(w_h, dmixed_bf[cs]))
            dmix_ref[h] = dmix_sum
            small_ref[ROW_WSGU + h * CHUNK:ROW_WSGU + (h + 1) * CHUNK, :] = dws
            dvln = jnp.concatenate(dvln_parts, axis=0)
            acc(VEC_SGU, h * HEAD, dvln * k["vn"][h])
            acc(VEC_SGU, D_SGU + h * HEAD, dvln)
            dvv = _ln_bwd(dvln * slng[:, sl], k["vn"][h], k["vrstd"][h])
            dv.append(dvv * _gelu_grad(k["v"][:, sl]))

        dproj = jnp.concatenate([dxa] + dga + du + dv + dgb, axis=1).astype(BF16)
        dproj_ref[...] = dproj
        dh = _dot(dproj, wint_ref[...])

        xn, xrstd = _ln(xt)
        h_ref[...] = (xn * (1.0 + scale) + shift).astype(BF16)
        acc(VEC_SCALE, 0, dh * xn)
        acc(VEC_SHIFT, 0, dh)
        dx_ref[...] = _ln_bwd(dh * (1.0 + scale), xn, xrstd) + ALPHA * dz

        @pl.when(step == N_TILES - 1)
        def _():
            def put(row0, vec_row, lo, n):
                for r in range(n):
                    small_ref[row0 + r:row0 + r + 1, :] = vec_ref[vec_row:vec_row + 1, lo + r * 128:lo + (r + 1) * 128]

            put(ROW_PSCALE, VEC_POOL, 0, 4)
            put(ROW_SLNG, VEC_SGU, 0, 4)
            put(ROW_SLNB, VEC_SGU, D_SGU, 4)
            put(ROW_LNG, VEC_LNG, 0, 8)
            put(ROW_LNB, VEC_LNB, 0, 8)
            ones = jnp.ones((8, HEAD), F32)
            t = lax.broadcasted_iota(jnp.int32, (CHUNK, CHUNK), 0)
            s = lax.broadcasted_iota(jnp.int32, (CHUNK, CHUNK), 1)
            for h in range(N_HEADS):
                bias_rows = lax.dot_general(ones, dmix_ref[h], (((1,), (1,)), ((), ())),
                                            preferred_element_type=F32, precision=lax.Precision.HIGHEST)
                small_ref[ROW_BSGU + h:ROW_BSGU + h + 1, :] = bias_rows[0:1]
                blk = small_ref[ROW_WSGU + h * CHUNK:ROW_WSGU + (h + 1) * CHUNK, :]
                small_ref[ROW_WSGU + h * CHUNK:ROW_WSGU + (h + 1) * CHUNK, :] = jnp.where(t >= s, blk, 0.0)
            dmod_ref[:, 0:D_MODEL] = vec_ref[VEC_SHIFT:VEC_SHIFT + 1, :]
            dmod_ref[:, D_MODEL:2 * D_MODEL] = vec_ref[VEC_SCALE:VEC_SCALE + 1, :]
            dmod_ref[:, 2 * D_MODEL:] = vec_ref[VEC_GATE:VEC_GATE + 1, :]
            loss_ref[...] = vec_ref[VEC_LOSS:VEC_LOSS + 1, :]

    rev = lambda w: pl.BlockSpec((TM, w), lambda i: (N_TILES - 1 - i, 0))
    prev_spec = pl.BlockSpec(
        (HALO, D_POOL), lambda i: (jnp.maximum((N_TILES - 1 - i) * (TM // HALO) - 1, 0), 0))
    return pl.pallas_call(
        body,
        name=name,
        grid=(N_TILES,),
        in_specs=[rev(D_MODEL), rev(D_MODEL) if is_last else pl.BlockSpec((TM, D_MODEL), lambda i: (0, 0)),
                  rev(D_MODEL), rev(D_IN), prev_spec, rev(D_MODEL),
                  _const_spec((1, 3 * D_MODEL)), _const_spec((D_IN, D_MODEL)), _const_spec((D_MODEL, D_MODEL)),
                  _const_spec((N_GROUPS, GROUP, GROUP)), _const_spec((1, D_POOL)), _const_spec((1, D_SGU)),
                  _const_spec((1, D_SGU)), _const_spec((N_HEADS, CHUNK, CHUNK)), _const_spec((CHUNK, N_HEADS)),
                  _const_spec((1, D_MODEL))],
        out_specs=[rev(D_MODEL), rev(D_IN), rev(D_MODEL), rev(D_MODEL), rev(D_MODEL),
                   _const_spec((PACK_ROWS, 128)), _const_spec((1, 3 * D_MODEL)), _const_spec((1, D_MODEL))],
        out_shape=[jax.ShapeDtypeStruct((SEQ, D_MODEL), F32), jax.ShapeDtypeStruct((SEQ, D_IN), BF16),
                   jax.ShapeDtypeStruct((SEQ, D_MODEL), BF16), jax.ShapeDtypeStruct((SEQ, D_MODEL), BF16),
                   jax.ShapeDtypeStruct((SEQ, D_MODEL), BF16), jax.ShapeDtypeStruct((PACK_ROWS, 128), F32),
                   jax.ShapeDtypeStruct((1, 3 * D_MODEL), F32), jax.ShapeDtypeStruct((1, D_MODEL), F32)],
        scratch_shapes=[pltpu.VMEM((8, D_MODEL), F32), pltpu.VMEM((N_HEADS, CHUNK, HEAD), F32),
                        pltpu.VMEM((HALO, D_POOL), F32)],
        compiler_params=pltpu.CompilerParams(dimension_semantics=("arbitrary",), vmem_limit_bytes=VMEM_LIMIT),
    )(a, b, x, proj, proj, y, mod_l, w_int, w_outf, w_pool, pscale, slng, slnb, w_sgu, bsgut, ln_g)


def _grad_matmul(lhs, rhs, block_cols, name):
    m, n = lhs.shape[1], rhs.shape[1]
    k_steps = SEQ // TK

    def body(lhs_ref, rhs_ref, out_ref, acc_ref):
        k = pl.program_id(1)
        part = _dot_tn(lhs_ref[...], rhs_ref[...])

        @pl.when(k == 0)
        def _():
            acc_ref[...] = part

        @pl.when(k > 0)
        def _():
            acc_ref[...] += part

        @pl.when(k == k_steps - 1)
        def _():
            out_ref[...] = acc_ref[...].astype(BF16)

    return pl.pallas_call(
        body,
        name=name,
        grid=(m // block_cols, k_steps),
        in_specs=[pl.BlockSpec((TK, block_cols), lambda j, k: (k, j)), pl.BlockSpec((TK, n), lambda j, k: (k, 0))],
        out_specs=pl.BlockSpec((block_cols, n), lambda j, k: (j, 0)),
        out_shape=jax.ShapeDtypeStruct((m, n), BF16),
        scratch_shapes=[pltpu.VMEM((block_cols, n), F32)],
        compiler_params=pltpu.CompilerParams(dimension_semantics=("arbitrary", "arbitrary"),
                                             vmem_limit_bytes=VMEM_LIMIT),
    )(lhs, rhs)


def _adamw_math(w, g, m, v):
    m = ADAM_B1 * m + (1.0 - ADAM_B1) * g
    v = ADAM_B2 * v + (1.0 - ADAM_B2) * (g * g)
    m_hat = m / (1.0 - ADAM_B1 ** ADAM_STEP)
    v_hat = v / (1.0 - ADAM_B2 ** ADAM_STEP)
    delta = -ADAM_LR * (m_hat / (jnp.sqrt(v_hat) + ADAM_EPS) + ADAM_WD * w)
    return delta, m, v


def _adamw(w, g, m, v, block_rows, name):
    rows, cols = w.shape

    def body(w_ref, g_ref, m_ref, v_ref, d_ref, nm_ref, nv_ref):
        d_ref[...], nm_ref[...], nv_ref[...] = _adamw_math(w_ref[...], g_ref[...], m_ref[...], v_ref[...])

    spec = pl.BlockSpec((block_rows, cols), lambda i: (i, 0))
    return pl.pallas_call(
        body,
        name=name,
        grid=(rows // block_rows,),
        in_specs=[spec] * 4,
        out_specs=[spec] * 3,
        out_shape=[jax.ShapeDtypeStruct(w.shape, F32)] * 3,
        compiler_params=pltpu.CompilerParams(dimension_semantics=("arbitrary",), vmem_limit_bytes=VMEM_LIMIT),
    )(w, g, m, v)


def _adamw_ada(w, m, v, act_t, dmod_cols, name):
    cols = w.shape[2]

    def body(w_ref, m_ref, v_ref, act_ref, dmod_ref, g_ref, d_ref, nm_ref, nv_ref):
        act = act_ref[...]
        dm = dmod_ref[0]
        g = act[:, 0:1] * dm[0:1, :]
        for b in range(1, N_DEV):
            g = g + act[:, b:b + 1] * dm[b:b + 1, :]
        g_ref[0] = g
        d_ref[0], nm_ref[0], nv_ref[0] = _adamw_math(w_ref[0], g, m_ref[0], v_ref[0])

    spec = pl.BlockSpec((1, D_MODEL, cols), lambda l: (l, 0, 0))
    return pl.pallas_call(
        body,
        name=name,
        grid=(DEPTH,),
        in_specs=[spec, spec, spec, pl.BlockSpec((D_MODEL, N_DEV), lambda l: (0, 0)),
                  pl.BlockSpec((1, N_DEV, cols), lambda l: (l, 0, 0))],
        out_specs=[spec] * 4,
        out_shape=[jax.ShapeDtypeStruct(w.shape, F32)] * 4,
        compiler_params=pltpu.CompilerParams(dimension_semantics=("arbitrary",), vmem_limit_bytes=VMEM_LIMIT),
    )(w, m, v, act_t, dmod_cols)


def _adamw_bias(w, m, v, dmod_all, name):
    def body(w_ref, m_ref, v_ref, dmod_ref, g_ref, d_ref, nm_ref, nv_ref):
        g = dmod_ref[0]
        for b in range(1, N_DEV):
            g = g + dmod_ref[b]
        g_ref[...] = g
        d_ref[...], nm_ref[...], nv_ref[...] = _adamw_math(w_ref[...], g, m_ref[...], v_ref[...])

    return pl.pallas_call(
        body,
        name=name,
        out_shape=[jax.ShapeDtypeStruct(w.shape, F32)] * 4,
        compiler_params=pltpu.CompilerParams(vmem_limit_bytes=VMEM_LIMIT),
    )(w, m, v, dmod_all)


MESH = pl.DeviceIdType.MESH
SIBLING = 1
CHIP_RELATIONS = (2, 4, 6)
ANY = pl.BlockSpec(memory_space=pl.ANY)
VMEM = pl.BlockSpec(memory_space=pltpu.VMEM)


def _me():
    return lax.axis_index("x"), lax.axis_index("y"), lax.axis_index("c")


def _peer(r):
    x, y, c = _me()
    return (1 - x if r & 4 else x, 1 - y if r & 2 else y, 1 - c if r & 1 else c)


def _index(dev):
    return 4 * dev[0] + 2 * dev[1] + dev[2]


def _remote(src, dst, send_sem, recv_sem, dev):
    return pltpu.make_async_remote_copy(src_ref=src, dst_ref=dst, send_sem=send_sem, recv_sem=recv_sem,
                                        device_id=dev, device_id_type=MESH)


class _TwoLevelGather:
    def __init__(self, out, send_sems, recv_sems, src=None):
        self.out, self.send_sems, self.recv_sems, self.src = out, send_sems, recv_sems, src
        self.started = []

    def _copy(self, k, block, to, src=None):
        slot = self.out.at[_index(block)]
        return _remote(slot if src is None else src, slot, self.send_sems.at[k], self.recv_sems.at[k], to)

    def send_mine(self):
        me = _me()
        src = self.out.at[_index(me)] if self.src is None else self.src
        copies = [self._copy(0, me, _peer(SIBLING), src)]
        copies += [self._copy(1 + j, me, _peer(r), src) for j, r in enumerate(CHIP_RELATIONS)]
        for cp in copies:
            cp.start()
        self.started += copies

    def pass_on(self):
        for j, r in enumerate(CHIP_RELATIONS):
            self._copy(1 + j, _peer(r), _me()).wait_recv()
            cp = self._copy(4 + j, _peer(r), _peer(SIBLING))
            cp.start()
            self.started.append(cp)

    def wait_rest(self):
        self._copy(0, _peer(SIBLING), _me()).wait_recv()
        for j, r in enumerate(CHIP_RELATIONS):
            self._copy(4 + j, _peer(r ^ SIBLING), _me()).wait_recv()

    def wait_sends(self):
        for cp in self.started:
            cp.wait_send()


def _direct_exchange(src_of, dst_of, send_sems, recv_sems):
    me = _me()
    copies = [_remote(src_of(_peer(r)), dst_of(me), send_sems.at[r - 1], recv_sems.at[r - 1], _peer(r))
              for r in range(1, N_DEV)]
    for cp in copies:
        cp.start()
    return copies


def _wait_direct(copies):
    for cp in copies:
        cp.wait_recv()
    for cp in copies:
        cp.wait_send()


def _prep(c_row, w_ada, b_ada, wint_loc, wout_loc):
    cols = w_ada.shape[2]

    def body(c_ref, wada_ref, bada_ref, wint_ref, wout_ref, act_all, mod_ref, wint_all, wout_all,
             act_src, part, mod_recv, w_send, w_recv, w_local, a_send, a_recv, m_send, m_recv):
        me = _me()
        mine = _index(me)

        cval = c_ref[...]
        act_src[...] = jnp.zeros_like(act_src)
        act_src[0:1, :] = cval * jax.nn.sigmoid(cval)
        act_all[mine] = act_src[...]
        act_copies = _direct_exchange(lambda p: act_src, lambda m: act_all.at[_index(m)], a_send, a_recv)

        gathers, locals_ = [], []
        for layer in range(DEPTH):
            for a, (loc, full) in enumerate(((wint_ref, wint_all), (wout_ref, wout_all))):
                n = 2 * layer + a
                own = pltpu.make_async_copy(loc.at[layer], full.at[layer, mine], w_local.at[n])
                own.start()
                locals_.append(own)
                g = _TwoLevelGather(full.at[layer], w_send.at[n], w_recv.at[n], src=loc.at[layer])
                g.send_mine()
                gathers.append(g)

        _wait_direct(act_copies)
        acts = jnp.concatenate([act_all[j, 0:1, :] for j in range(N_DEV)], axis=0)
        part[...] = jnp.zeros_like(part)
        for layer in range(DEPTH):
            res = lax.dot_general(acts, wada_ref[layer], (((1,), (0,)), ((), ())), preferred_element_type=F32,
                                  precision=lax.Precision.HIGHEST)
            for b in range(N_DEV):
                part[b, layer:layer + 1, :] = res[b:b + 1, :]
        mod_recv[mine] = part[mine]
        mod_copies = _direct_exchange(lambda p: part.at[_index(p)], lambda m: mod_recv.at[_index(m)], m_send, m_recv)

        for g in gathers:
            g.pass_on()
        for g in gathers:
            g.wait_rest()
        _wait_direct(mod_copies)
        for layer in range(DEPTH):
            for j in range(N_DEV):
                sl = slice(j * cols, (j + 1) * cols)
                mod_ref[layer:layer + 1, sl] = mod_recv[j, layer:layer + 1, :] + bada_ref[layer:layer + 1, sl]
        for g in gathers:
            g.wait_sends()
        for own in locals_:
            own.wait()

    return pl.pallas_call(
        body,
        name="prep_gather",
        in_specs=[VMEM, VMEM, VMEM, ANY, ANY],
        out_specs=[VMEM, VMEM, ANY, ANY],
        out_shape=[jax.ShapeDtypeStruct((N_DEV, 8, D_MODEL), F32), jax.ShapeDtypeStruct((DEPTH, 3 * D_MODEL), F32),
                   jax.ShapeDtypeStruct((DEPTH, N_DEV) + wint_loc.shape[1:], BF16),
                   jax.ShapeDtypeStruct((DEPTH, N_DEV) + wout_loc.shape[1:], BF16)],
        scratch_shapes=[pltpu.VMEM((8, D_MODEL), F32), pltpu.VMEM((N_DEV, 8, cols), F32),
                        pltpu.VMEM((N_DEV, 8, cols), F32),
                        pltpu.SemaphoreType.DMA((2 * DEPTH, 7)), pltpu.SemaphoreType.DMA((2 * DEPTH, 7)),
                        pltpu.SemaphoreType.DMA((2 * DEPTH,)),
                        pltpu.SemaphoreType.DMA((7,)), pltpu.SemaphoreType.DMA((7,)),
                        pltpu.SemaphoreType.DMA((7,)), pltpu.SemaphoreType.DMA((7,))],
        compiler_params=pltpu.CompilerParams(vmem_limit_bytes=VMEM_LIMIT),
    )(c_row, w_ada, b_ada, wint_loc, wout_loc)


def _reduce(dwin, dwout, small, dmod8):
    n_arr = 2 * DEPTH + 1

    def body(dwin0_ref, dwin1_ref, dwout0_ref, dwout1_ref, small_ref, dmod_ref, gwin_ref, gwout_ref, stot_ref,
             dmod_all, sib_win, sib_wout, sib_small, snd_win, snd_wout, snd_small, rcv_win, rcv_wout, rcv_small,
             d_send, d_recv, i_send, i_recv, m_send, m_recv, g_send, g_recv):
        dwin_refs = (dwin0_ref, dwin1_ref)
        dwout_refs = (dwout0_ref, dwout1_ref)
        me = _me()
        x, y, c = me
        mine = _index(me)
        other = 1 - c
        sib = _peer(SIBLING)

        dmod_all[mine] = dmod_ref[...]
        dmod_copies = _direct_exchange(lambda p: dmod_ref, lambda m: dmod_all.at[_index(m)], m_send, m_recv)

        pairs = []
        for layer in range(DEPTH):
            pairs.append((dwin_refs[layer].at[pl.ds(0, 4), other], sib_win.at[layer]))
            pairs.append((dwout_refs[layer].at[pl.ds(0, 4), other], sib_wout.at[layer]))
        pairs.append((small_ref.at[pl.ds(0, 4), other], sib_small))
        first = [_remote(s, d, d_send.at[n], d_recv.at[n], sib) for n, (s, d) in enumerate(pairs)]
        for cp in first:
            cp.start()
        for cp in first:
            cp.wait_recv()

        second = []
        for j, r in enumerate(CHIP_RELATIONS):
            peer = _peer(r)
            chip = 2 * peer[0] + peer[1]
            for layer in range(DEPTH):
                snd_win[layer, j] = (dwin_refs[layer][chip, c].astype(F32)
                                     + sib_win[layer, chip].astype(F32)).astype(BF16)
                snd_wout[layer, j] = (dwout_refs[layer][chip, c].astype(F32)
                                      + sib_wout[layer, chip].astype(F32)).astype(BF16)
            snd_small[j] = small_ref[chip, c] + sib_small[chip]
            triples = []
            for layer in range(DEPTH):
                triples.append((snd_win.at[layer, j], rcv_win.at[layer, j]))
                triples.append((snd_wout.at[layer, j], rcv_wout.at[layer, j]))
            triples.append((snd_small.at[j], rcv_small.at[j]))
            for n, (s, d) in enumerate(triples):
                cp = _remote(s, d, i_send.at[j, n], i_recv.at[j, n], peer)
                cp.start()
                second.append(cp)

        home = 2 * x + y
        for layer in range(DEPTH):
            gwin_ref[layer] = dwin_refs[layer][home, c].astype(F32) + sib_win[layer, home].astype(F32)
            gwout_ref[layer] = dwout_refs[layer][home, c].astype(F32) + sib_wout[layer, home].astype(F32)
        total = small_ref[home, c] + sib_small[home]
        for cp in second:
            cp.wait_recv()
        for j in range(len(CHIP_RELATIONS)):
            for layer in range(DEPTH):
                gwin_ref[layer] += rcv_win[layer, j].astype(F32)
                gwout_ref[layer] += rcv_wout[layer, j].astype(F32)
            total = total + rcv_small[j]
        stot_ref[mine] = total

        gather = _TwoLevelGather(stot_ref, g_send, g_recv)
        gather.send_mine()
        gather.pass_on()
        gather.wait_rest()
        gather.wait_sends()
        _wait_direct(dmod_copies)
        for cp in first + second:
            cp.wait_send()

    shard_in, shard_out, shard_small = dwin[0].shape[2], dwout[0].shape[2], small.shape[2]
    n_rel = len(CHIP_RELATIONS)
    return pl.pallas_call(
        body,
        name="grad_reduce",
        in_specs=[VMEM] * 6,
        out_specs=[VMEM] * 4,
        out_shape=[jax.ShapeDtypeStruct((DEPTH, shard_in, D_MODEL), F32),
                   jax.ShapeDtypeStruct((DEPTH, shard_out, D_MODEL), F32),
                   jax.ShapeDtypeStruct((N_DEV, shard_small, 128), F32),
                   jax.ShapeDtypeStruct((N_DEV,) + dmod8.shape, F32)],
        scratch_shapes=[pltpu.VMEM((DEPTH, 4, shard_in, D_MODEL), BF16), pltpu.VMEM((DEPTH, 4, shard_out, D_MODEL), BF16),
                        pltpu.VMEM((4, shard_small, 128), F32),
                        pltpu.VMEM((DEPTH, n_rel, shard_in, D_MODEL), BF16),
                        pltpu.VMEM((DEPTH, n_rel, shard_out, D_MODEL), BF16),
                        pltpu.VMEM((n_rel, shard_small, 128), F32),
                        pltpu.VMEM((DEPTH, n_rel, shard_in, D_MODEL), BF16),
                        pltpu.VMEM((DEPTH, n_rel, shard_out, D_MODEL), BF16),
                        pltpu.VMEM((n_rel, shard_small, 128), F32),
                        pltpu.SemaphoreType.DMA((n_arr,)), pltpu.SemaphoreType.DMA((n_arr,)),
                        pltpu.SemaphoreType.DMA((n_rel, n_arr)), pltpu.SemaphoreType.DMA((n_rel, n_arr)),
                        pltpu.SemaphoreType.DMA((7,)), pltpu.SemaphoreType.DMA((7,)),
                        pltpu.SemaphoreType.DMA((7,)), pltpu.SemaphoreType.DMA((7,))],
        compiler_params=pltpu.CompilerParams(vmem_limit_bytes=VMEM_LIMIT),
    )(*dwin, *dwout, small, dmod8)


SMALL_NAMES = ("w_pool", "w_sgu", "pool_scale", "sgu_ln_g", "sgu_ln_b", "b_sgu", "ln_g", "ln_b")
SMALL_ROWS = (512, 512, 4, 4, 4, 4, 8, 8)


def _pack_small(parts):
    pieces = [parts[n].reshape(DEPTH, r, 128) for n, r in zip(SMALL_NAMES, SMALL_ROWS)]
    pieces.append(jnp.zeros((DEPTH, PACK_ROWS - sum(SMALL_ROWS), 128), F32))
    return jnp.concatenate(pieces, axis=1).reshape(DEPTH * PACK_ROWS, 128)


def _unpack_small(packed, shapes):
    packed = packed.reshape(DEPTH, PACK_ROWS, 128)
    out, row = {}, 0
    for n, r in zip(SMALL_NAMES, SMALL_ROWS):
        out[n] = packed[:, row:row + r, :].reshape(shapes[n])
        row += r
    return out


def kernel(x, c, w_ada, b_ada, w_in, w_pool, pool_scale, sgu_ln_g, sgu_ln_b, w_sgu, b_sgu, w_out, ln_g, ln_b, loss_target, m_w_ada, m_b_ada, m_w_in, m_w_pool, m_pool_scale, m_sgu_ln_g, m_sgu_ln_b, m_w_sgu, m_b_sgu, m_w_out, m_ln_g, m_ln_b, v_w_ada, v_b_ada, v_w_in, v_w_pool, v_pool_scale, v_sgu_ln_g, v_sgu_ln_b, v_w_sgu, v_b_sgu, v_w_out, v_ln_g, v_ln_b):
    axes = ("x", "y", "c")
    mine = _index(_me())
    small_w = dict(w_pool=w_pool, w_sgu=w_sgu, pool_scale=pool_scale, sgu_ln_g=sgu_ln_g, sgu_ln_b=sgu_ln_b,
                   b_sgu=b_sgu, ln_g=ln_g, ln_b=ln_b)
    small_m = dict(w_pool=m_w_pool, w_sgu=m_w_sgu, pool_scale=m_pool_scale, sgu_ln_g=m_sgu_ln_g,
                   sgu_ln_b=m_sgu_ln_b, b_sgu=m_b_sgu, ln_g=m_ln_g, ln_b=m_ln_b)
    small_v = dict(w_pool=v_w_pool, w_sgu=v_w_sgu, pool_scale=v_pool_scale, sgu_ln_g=v_sgu_ln_g,
                   sgu_ln_b=v_sgu_ln_b, b_sgu=v_b_sgu, ln_g=v_ln_g, ln_b=v_ln_b)
    small_shapes = {n: a.shape for n, a in small_w.items()}

    wint_loc = jnp.transpose(w_in, (0, 2, 1)).astype(BF16)
    act_slots, mod, wint_all, wout_all = _prep(c, w_ada, b_ada, wint_loc, w_out.astype(BF16))
    w_int = wint_all.reshape(DEPTH, D_IN, D_MODEL)
    w_outf = wout_all.reshape(DEPTH, D_MODEL, D_MODEL)
    act_all = act_slots[:, 0, :]

    def layer_args(l):
        return (w_int[l], w_outf[l], w_pool[l], pool_scale[l].reshape(1, D_POOL), sgu_ln_g[l].reshape(1, D_SGU),
                sgu_ln_b[l].reshape(1, D_SGU), w_sgu[l], jnp.transpose(b_sgu[l]), ln_g[l].reshape(1, D_MODEL))

    acts, cur = [], x[0]
    for l in range(DEPTH):
        out, proj, y = _layer_forward(cur, mod[l:l + 1], *layer_args(l), ln_b[l].reshape(1, D_MODEL), f"layer_fwd_{l}")
        acts.append((cur, proj, y))
        cur = out

    a, b = cur, loss_target[0]
    dwin, dwout, smalls, dmods, loss_lanes = [None] * DEPTH, [None] * DEPTH, [None] * DEPTH, [None] * DEPTH, None
    for l in reversed(range(DEPTH)):
        xin, proj, y = acts[l]
        dx, dproj, h, cat, dy, small, dmod, lanes = _layer_backward(
            a, b, xin, proj, y, mod[l:l + 1], *layer_args(l), l == DEPTH - 1, f"layer_bwd_{l}")
        if l == DEPTH - 1:
            loss_lanes = lanes
        dwin[l] = _grad_matmul(dproj, h, 640, f"grad_w_in_{l}")
        dwout[l] = _grad_matmul(cat, dy, 512, f"grad_w_out_{l}")
        smalls[l], dmods[l] = small, dmod
        a = b = dx
    grad_x = a[None]
    loss = lax.psum((0.5 / D_MODEL) * jnp.sum(loss_lanes), axes)

    shard_in, shard_out = D_IN // N_DEV, D_MODEL // N_DEV
    gwin_t, gwout, small_tot, dmod_slots = _reduce(
        [t.reshape(4, 2, shard_in, D_MODEL) for t in dwin],
        [t.reshape(4, 2, shard_out, D_MODEL) for t in dwout],
        jnp.concatenate(smalls, axis=0).reshape(4, 2, DEPTH * PACK_ROWS // N_DEV, 128),
        jnp.concatenate(dmods, axis=0).reshape(8, 3 * D_MODEL * DEPTH // 8))
    dmod_all = dmod_slots.reshape(N_DEV, DEPTH, 3 * D_MODEL)
    g_w_in = jnp.transpose(gwin_t, (0, 2, 1))

    cols_ada = w_ada.shape[2]
    dmod_cols = jnp.transpose(lax.dynamic_slice_in_dim(dmod_all, mine * cols_ada, cols_ada, axis=2), (1, 0, 2))
    g_w_ada, d_w_ada, nm_w_ada, nv_w_ada = _adamw_ada(w_ada, m_w_ada, v_w_ada, jnp.transpose(act_all), dmod_cols,
                                                      "adamw_w_ada")
    g_b_ada, d_b_ada, nm_b_ada, nv_b_ada = _adamw_bias(b_ada, m_b_ada, v_b_ada, dmod_all, "adamw_b_ada")
    flat = lambda t: t.reshape(-1, t.shape[-1])
    d_w_in, nm_w_in, nv_w_in = [t.reshape(w_in.shape) for t in
                                _adamw(flat(w_in), flat(g_w_in), flat(m_w_in), flat(v_w_in), 512, "adamw_w_in")]
    d_w_out, nm_w_out, nv_w_out = [t.reshape(w_out.shape) for t in
                                   _adamw(flat(w_out), flat(gwout), flat(m_w_out), flat(v_w_out), 128, "adamw_w_out")]
    g_small = small_tot.reshape(DEPTH * PACK_ROWS, 128)
    d_small, nm_small, nv_small = _adamw(_pack_small(small_w), g_small, _pack_small(small_m), _pack_small(small_v),
                                         DEPTH * PACK_ROWS // 2, "adamw_small")
    gs, ds, ms, vs = [_unpack_small(t, small_shapes) for t in (g_small, d_small, nm_small, nv_small)]

    def ordered(w_ada_, b_ada_, w_in_, small, w_out_):
        return (w_ada_, b_ada_, w_in_, small["w_pool"], small["pool_scale"], small["sgu_ln_g"], small["sgu_ln_b"],
                small["w_sgu"], small["b_sgu"], w_out_, small["ln_g"], small["ln_b"])

    return (loss, grad_x,
            *ordered(g_w_ada, g_b_ada, g_w_in, gs, gwout),
            *ordered(d_w_ada, d_b_ada, d_w_in, ds, d_w_out),
            *ordered(nm_w_ada, nm_b_ada, nm_w_in, ms, nm_w_out),
            *ordered(nv_w_ada, nv_b_ada, nv_w_in, vs, nv_w_out))
```

```python
import functools
import math

import jax
import jax.numpy as jnp
from jax import lax
from jax.experimental import pallas as pl
from jax.experimental.pallas import tpu as pltpu

F32 = jnp.float32
BF16 = jnp.bfloat16

D_MODEL = 1024
SEQ = 2048
DEPTH = 2
D_POOL = 512
D_SGU = 512
D_IN = 2560
N_GROUPS = 4
GROUP = 128
N_HEADS = 4
HEAD = 128
CHUNK = 128
WINDOWS = (2, 4, 8, 16)
ALPHA = (2.0 * DEPTH) ** 0.25
LN_EPS = 1e-5
N_DEV = 8

ADAM_LR = 0.001
ADAM_B1 = 0.9
ADAM_B2 = 0.999
ADAM_EPS = 1e-08
ADAM_WD = 0.01
ADAM_STEP = 10

TM = 256
HALO = 16
N_TILES = SEQ // TM
VMEM_LIMIT = 60 * 1024 * 1024

ROW_WPOOL = 0
ROW_WSGU = 512
ROW_PSCALE = 1024
ROW_SLNG = 1028
ROW_SLNB = 1032
ROW_BSGU = 1036
ROW_LNG = 1040
ROW_LNB = 1048
PACK_ROWS = 1088

SQRT_HALF = 0.7071067811865476
INV_SQRT_2PI = 0.3989422804014327


def _ln(x):
    mu = jnp.mean(x, axis=-1, keepdims=True)
    xc = x - mu
    var = jnp.mean(xc * xc, axis=-1, keepdims=True)
    rstd = lax.rsqrt(var + LN_EPS)
    return xc * rstd, rstd


def _ln_bwd(dxn, xn, rstd):
    m1 = jnp.mean(dxn, axis=-1, keepdims=True)
    m2 = jnp.mean(dxn * xn, axis=-1, keepdims=True)
    return rstd * (dxn - m1 - xn * m2)


def _gelu_parts(x, with_grad):
    cdf = 0.5 * (1.0 + lax.erf(x * SQRT_HALF))
    if not with_grad:
        return x * cdf, None
    return x * cdf, cdf + x * (INV_SQRT_2PI * jnp.exp(-0.5 * x * x))


def _silu_parts(x):
    s = jax.nn.sigmoid(x)
    return x * s, s * (1.0 + x * (1.0 - s))


def _dot(a, b):
    return lax.dot_general(a, b, (((1,), (0,)), ((), ())), preferred_element_type=F32)


def _dot_nt(a, b):
    return lax.dot_general(a, b, (((1,), (1,)), ((), ())), preferred_element_type=F32)


def _dot_tn(a, b):
    return lax.dot_general(a, b, (((0,), (0,)), ((), ())), preferred_element_type=F32)


def _row_index(tile):
    return tile * TM + lax.broadcasted_iota(jnp.int32, (TM, 1), 0)


def _window_sums(ext, forward):
    n = TM + HALO
    cur = ext
    outs = []
    for g in range(N_GROUPS):
        step = 1 << g
        cur = cur + pltpu.roll(cur, step if forward else n - step, 0)
        rows = cur[HALO:, :GROUP] if forward else cur[:TM, :GROUP]
        outs.append(rows)
        cur = cur[:, GROUP:] if g + 1 < N_GROUPS else None
    return outs


def _counts(rows):
    return [jnp.minimum(rows + 1, w).astype(F32) for w in WINDOWS]


def _tril_bf16(w):
    t = lax.broadcasted_iota(jnp.int32, (CHUNK, CHUNK), 0)
    s = lax.broadcasted_iota(jnp.int32, (CHUNK, CHUNK), 1)
    return jnp.where(t >= s, w, 0.0).astype(BF16)


def _mix_forward(proj, halo, tile, wpool_ref, pscale, slng, slnb, wsgu_ref, bsgut_ref, keep):
    rows = _row_index(tile)
    counts = _counts(rows)
    xa = proj[:, 0:D_POOL]
    ga = proj[:, D_POOL:2 * D_POOL]
    sums = _window_sums(jnp.concatenate([halo, xa], axis=0), True)
    ga_act, ga_grad = _silu_parts(ga)
    pooled, pw, ya = [], [], []
    for g in range(N_GROUPS):
        sl = slice(g * GROUP, (g + 1) * GROUP)
        p = (sums[g] / counts[g] - xa[:, sl]).astype(BF16)
        q = _dot(p, wpool_ref[g].astype(BF16))
        pooled.append(p)
        pw.append(q)
        ya.append(q * pscale[:, sl] * ga_act[:, sl])

    u = proj[:, 2 * D_POOL:2 * D_POOL + D_SGU]
    v = proj[:, 2 * D_POOL + D_SGU:2 * D_POOL + 2 * D_SGU]
    gb = proj[:, 2 * D_POOL + 2 * D_SGU:]
    gb_act, gb_grad = _silu_parts(gb)
    u_act, u_grad = _gelu_parts(u, keep)
    v_act, v_grad = _gelu_parts(v, keep)
    vn, vrstd, vln, mixed, yb = [], [], [], [], []
    for h in range(N_HEADS):
        sl = slice(h * HEAD, (h + 1) * HEAD)
        n_h, r_h = _ln(v_act[:, sl])
        l_h = (n_h * slng[:, sl] + slnb[:, sl]).astype(BF16)
        w_h = _tril_bf16(wsgu_ref[h])
        bias = bsgut_ref[:, h:h + 1]
        m_h = jnp.concatenate(
            [_dot(w_h, l_h[k * CHUNK:(k + 1) * CHUNK]) + bias for k in range(TM // CHUNK)], axis=0)
        vn.append(n_h)
        vrstd.append(r_h)
        vln.append(l_h)
        mixed.append(m_h)
        yb.append(u_act[:, sl] * m_h * gb_act[:, sl])
    cat = jnp.concatenate(ya + yb, axis=1)
    if not keep:
        return cat, None
    return cat, dict(counts=counts, ga_act=ga_act, ga_grad=ga_grad, pooled=pooled, pw=pw, u_grad=u_grad,
                     v_grad=v_grad, u_act=u_act, gb_act=gb_act, gb_grad=gb_grad, vn=vn, vrstd=vrstd, vln=vln,
                     mixed=mixed)


def _const_spec(shape):
    nd = len(shape)
    return pl.BlockSpec(shape, lambda i: (0,) * nd)


def _layer_forward(x, mod_l, w_int, w_outf, w_pool, pscale, slng, slnb, w_sgu, bsgut, ln_g, ln_b, name, gather=()):
    n_gather = len(gather)

    def body(x_ref, mod_ref, wint_ref, wout_ref, wpool_ref, pscale_ref, slng_ref, slnb_ref, wsgu_ref, bsgut_ref,
             lng_ref, lnb_ref, *rest):
        loc_refs, rest = rest[:n_gather], rest[n_gather:]
        out_ref, proj_ref, y_ref = rest[:3]
        full_refs, rest = rest[3:3 + n_gather], rest[3 + n_gather:]
        halo_ref = rest[0]
        tile = pl.program_id(0)

        def gathers():
            g_send, g_recv, _ = rest[1:]
            return [_TwoLevelGather(full_refs[n], g_send.at[n], g_recv.at[n], src=loc_refs[n])
                    for n in range(n_gather)]

        def own_copies():
            mine = _index(_me())
            return [pltpu.make_async_copy(loc_refs[n], full_refs[n].at[mine], rest[3].at[n]) for n in range(n_gather)]

        @pl.when(tile == 0)
        def _():
            halo_ref[...] = jnp.zeros_like(halo_ref)
            if n_gather:
                for cp in own_copies():
                    cp.start()
                for g in gathers():
                    g.send_mine()

        if n_gather:
            @pl.when(tile == N_TILES - 2)
            def _():
                for g in gathers():
                    g.pass_on()

        xt = x_ref[...]
        shift = mod_ref[:, 0:D_MODEL]
        scale = mod_ref[:, D_MODEL:2 * D_MODEL]
        gate = mod_ref[:, 2 * D_MODEL:]
        xn, _ = _ln(xt)
        h = (xn * (1.0 + scale) + shift).astype(BF16)
        proj = _dot_nt(h, wint_ref[...])
        proj_ref[...] = proj
        cat, _ = _mix_forward(proj, halo_ref[...], tile, wpool_ref, pscale_ref[...], slng_ref[...], slnb_ref[...],
                              wsgu_ref, bsgut_ref, False)
        halo_ref[...] = proj[TM - HALO:, 0:D_POOL]
        y = _dot(cat.astype(BF16), wout_ref[...])
        y_ref[...] = y
        zn, _ = _ln(ALPHA * xt + gate * y)
        out_ref[...] = zn * lng_ref[...] + lnb_ref[...]

        if n_gather:
            @pl.when(tile == N_TILES - 1)
            def _():
                for g in gathers():
                    g.wait_rest()
                for g in gathers():
                    g.wait_sends()
                for cp in own_copies():
                    cp.wait()

    row = lambda w: pl.BlockSpec((TM, w), lambda i: (i, 0))
    comm_scratch = [pltpu.SemaphoreType.DMA((n_gather, 7)), pltpu.SemaphoreType.DMA((n_gather, 7)),
                    pltpu.SemaphoreType.DMA((n_gather,))] if n_gather else []
    return pl.pallas_call(
        body,
        name=name,
        grid=(N_TILES,),
        in_specs=[row(D_MODEL), _const_spec((1, 3 * D_MODEL)), _const_spec((D_IN, D_MODEL)),
                  _const_spec((D_MODEL, D_MODEL)), _const_spec((N_GROUPS, GROUP, GROUP)), _const_spec((1, D_POOL)),
                  _const_spec((1, D_SGU)), _const_spec((1, D_SGU)), _const_spec((N_HEADS, CHUNK, CHUNK)),
                  _const_spec((CHUNK, N_HEADS)), _const_spec((1, D_MODEL)), _const_spec((1, D_MODEL))]
                 + [ANY] * n_gather,
        out_specs=[row(D_MODEL), row(D_IN), row(D_MODEL)] + [ANY] * n_gather,
        out_shape=[jax.ShapeDtypeStruct((SEQ, D_MODEL), F32), jax.ShapeDtypeStruct((SEQ, D_IN), F32),
                   jax.ShapeDtypeStruct((SEQ, D_MODEL), F32)]
                  + [jax.ShapeDtypeStruct((N_DEV,) + g.shape, g.dtype) for g in gather],
        scratch_shapes=[pltpu.VMEM((HALO, D_POOL), F32)] + comm_scratch,
        compiler_params=pltpu.CompilerParams(dimension_semantics=("arbitrary",), vmem_limit_bytes=VMEM_LIMIT),
    )(x, mod_l, w_int, w_outf, w_pool, pscale, slng, slnb, w_sgu, bsgut, ln_g, ln_b, *gather)


VEC_LNG, VEC_LNB, VEC_POOL, VEC_SGU, VEC_SHIFT, VEC_SCALE, VEC_GATE, VEC_LOSS = range(8)


def _layer_backward(a, b, x, proj, y, mod_l, w_int, w_outf, w_pool, pscale, slng, slnb, w_sgu, bsgut, ln_g, is_last,
                    name, reduce=()):
    n_red = len(reduce)

    def body(a_ref, b_ref, x_ref, proj_ref, prev_ref, y_ref, mod_ref, wint_ref, wout_ref, wpool_ref, pscale_ref,
             slng_ref, slnb_ref, wsgu_ref, bsgut_ref, lng_ref, *rest):
        part_refs, rest = rest[:n_red], rest[n_red:]
        dx_ref, dproj_ref, h_ref, cat_ref, dy_ref, small_ref, dmod_ref, loss_ref = rest[:8]
        shard_refs, rest = rest[8:8 + n_red], rest[8 + n_red:]
        vec_ref, dmix_ref, halo_ref = rest[:3]
        step = pl.program_id(0)
        tile = N_TILES - 1 - step

        def scatter():
            bufs, sems = rest[3:3 + 4 * n_red], rest[3 + 4 * n_red:]
            arrays = [dict(part=part_refs[n], out=shard_refs[n], stage=bufs[4 * n], sib=bufs[4 * n + 1],
                           snd=bufs[4 * n + 2], rcv=bufs[4 * n + 3]) for n in range(n_red)]
            return _ChipReduceScatter(arrays, *sems)

        @pl.when(step == 0)
        def _():
            small_ref[...] = jnp.zeros_like(small_ref)
            vec_ref[...] = jnp.zeros_like(vec_ref)
            dmix_ref[...] = jnp.zeros_like(dmix_ref)
            halo_ref[...] = jnp.zeros_like(halo_ref)
            if n_red:
                scatter().start()

        if n_red:
            @pl.when(step == 1)
            def _():
                scatter().exchange()

        def acc(row, lo, val):
            hi = lo + val.shape[1]
            vec_ref[row:row + 1, lo:hi] += jnp.sum(val, axis=0, keepdims=True)

        xt = x_ref[...]
        yt = y_ref[...]
        shift = mod_ref[:, 0:D_MODEL]
        scale = mod_ref[:, D_MODEL:2 * D_MODEL]
        gate = mod_ref[:, 2 * D_MODEL:]

        zn, zrstd = _ln(ALPHA * xt + gate * yt)
        if is_last:
            diff = a_ref[...] - b_ref[...]
            acc(VEC_LOSS, 0, diff * diff)
            dout = diff * (1.0 / D_MODEL)
        else:
            dout = a_ref[...]
        acc(VEC_LNG, 0, dout * zn)
        acc(VEC_LNB, 0, dout)
        dz = _ln_bwd(dout * lng_ref[...], zn, zrstd)
        acc(VEC_GATE, 0, dz * yt)
        dy = (dz * gate).astype(BF16)
        dy_ref[...] = dy
        dcat = _dot_nt(dy, wout_ref[...])

        proj = proj_ref[...]
        prev = jnp.where(tile > 0, prev_ref[...], 0.0)
        cat, k = _mix_forward(proj, prev, tile, wpool_ref, pscale_ref[...], slng_ref[...], slnb_ref[...],
                              wsgu_ref, bsgut_ref, True)
        cat_ref[...] = cat.astype(BF16)
        pscale = pscale_ref[...]
        slng = slng_ref[...]

        dga, dq = [], []
        for g in range(N_GROUPS):
            sl = slice(g * GROUP, (g + 1) * GROUP)
            dya = dcat[:, sl]
            dyp = dya * k["ga_act"][:, sl]
            dga.append(dya * k["pw"][g] * pscale[:, sl] * k["ga_grad"][:, sl])
            acc(VEC_POOL, g * GROUP, dyp * k["pw"][g])
            dpw = (dyp * pscale[:, sl]).astype(BF16)
            small_ref[ROW_WPOOL + g * GROUP:ROW_WPOOL + (g + 1) * GROUP, :] += _dot_tn(k["pooled"][g], dpw)
            dq.append(_dot_nt(dpw, wpool_ref[g].astype(BF16)))
        dpooled = jnp.concatenate(dq, axis=1)
        scaled = jnp.concatenate([dq[g] / k["counts"][g] for g in range(N_GROUPS)], axis=1)
        sums = _window_sums(jnp.concatenate([scaled, halo_ref[...]], axis=0), False)
        halo_ref[...] = scaled[0:HALO]
        dxa = jnp.concatenate(sums, axis=1) - dpooled

        du, dv, dgb = [], [], []
        for h in range(N_HEADS):
            sl = slice(h * HEAD, (h + 1) * HEAD)
            dyb = dcat[:, D_POOL + h * HEAD:D_POOL + (h + 1) * HEAD]
            m_h = k["mixed"][h]
            ug = k["u_act"][:, sl] * dyb
            du.append(dyb * m_h * k["gb_act"][:, sl] * k["u_grad"][:, sl])
            dgb.append(ug * m_h * k["gb_grad"][:, sl])
            dmixed = ug * k["gb_act"][:, sl]
            dmixed_bf = dmixed.astype(BF16)
            w_h = _tril_bf16(wsgu_ref[h])
            dvln_parts = []
            dmix_sum = dmix_ref[h]
            dws = small_ref[ROW_WSGU + h * CHUNK:ROW_WSGU + (h + 1) * CHUNK, :]
            for c in range(TM // CHUNK):
                cs = slice(c * CHUNK, (c + 1) * CHUNK)
                dmix_sum = dmix_sum + dmixed[cs]
                dws = dws + _dot_nt(dmixed_bf[cs], k["vln"][h][cs])
                dvln_parts.append(_dot_tn(w_h, dmixed_bf[cs]))
            dmix_ref[h] = dmix_sum
            small_ref[ROW_WSGU + h * CHUNK:ROW_WSGU + (h + 1) * CHUNK, :] = dws
            dvln = jnp.concatenate(dvln_parts, axis=0)
            acc(VEC_SGU, h * HEAD, dvln * k["vn"][h])
            acc(VEC_SGU, D_SGU + h * HEAD, dvln)
            dvv = _ln_bwd(dvln * slng[:, sl], k["vn"][h], k["vrstd"][h])
            dv.append(dvv * k["v_grad"][:, sl])

        dproj = jnp.concatenate([dxa] + dga + du + dv + dgb, axis=1).astype(BF16)
        dproj_ref[...] = dproj
        dh = _dot(dproj, wint_ref[...])

        xn, xrstd = _ln(xt)
        h_ref[...] = (xn * (1.0 + scale) + shift).astype(BF16)
        acc(VEC_SCALE, 0, dh * xn)
        acc(VEC_SHIFT, 0, dh)
        dx_ref[...] = _ln_bwd(dh * (1.0 + scale), xn, xrstd) + ALPHA * dz

        @pl.when(step == N_TILES - 1)
        def _():
            def put(row0, vec_row, lo, n):
                for r in range(n):
                    small_ref[row0 + r:row0 + r + 1, :] = vec_ref[vec_row:vec_row + 1, lo + r * 128:lo + (r + 1) * 128]

            put(ROW_PSCALE, VEC_POOL, 0, 4)
            put(ROW_SLNG, VEC_SGU, 0, 4)
            put(ROW_SLNB, VEC_SGU, D_SGU, 4)
            put(ROW_LNG, VEC_LNG, 0, 8)
            put(ROW_LNB, VEC_LNB, 0, 8)
            ones = jnp.ones((8, HEAD), F32)
            t = lax.broadcasted_iota(jnp.int32, (CHUNK, CHUNK), 0)
            s = lax.broadcasted_iota(jnp.int32, (CHUNK, CHUNK), 1)
            for h in range(N_HEADS):
                bias_rows = lax.dot_general(ones, dmix_ref[h], (((1,), (1,)), ((), ())),
                                            preferred_element_type=F32, precision=lax.Precision.HIGHEST)
                small_ref[ROW_BSGU + h:ROW_BSGU + h + 1, :] = bias_rows[0:1]
                blk = small_ref[ROW_WSGU + h * CHUNK:ROW_WSGU + (h + 1) * CHUNK, :]
                small_ref[ROW_WSGU + h * CHUNK:ROW_WSGU + (h + 1) * CHUNK, :] = jnp.where(t >= s, blk, 0.0)
            dmod_ref[:, 0:D_MODEL] = vec_ref[VEC_SHIFT:VEC_SHIFT + 1, :]
            dmod_ref[:, D_MODEL:2 * D_MODEL] = vec_ref[VEC_SCALE:VEC_SCALE + 1, :]
            dmod_ref[:, 2 * D_MODEL:] = vec_ref[VEC_GATE:VEC_GATE + 1, :]
            loss_ref[...] = vec_ref[VEC_LOSS:VEC_LOSS + 1, :]
            if n_red:
                scatter().finish()
                scatter().wait_sends()

    rev = lambda w: pl.BlockSpec((TM, w), lambda i: (N_TILES - 1 - i, 0))
    prev_spec = pl.BlockSpec(
        (HALO, D_POOL), lambda i: (jnp.maximum((N_TILES - 1 - i) * (TM // HALO) - 1, 0), 0))
    comm_scratch = []
    for p in reduce:
        comm_scratch += _ChipReduceScatter.buffers(p.shape[2], p.shape[3], p.dtype)
    if n_red:
        comm_scratch += _ChipReduceScatter.semaphores(n_red)
    return pl.pallas_call(
        body,
        name=name,
        grid=(N_TILES,),
        in_specs=[rev(D_MODEL), rev(D_MODEL) if is_last else pl.BlockSpec((TM, D_MODEL), lambda i: (0, 0)),
                  rev(D_MODEL), rev(D_IN), prev_spec, rev(D_MODEL),
                  _const_spec((1, 3 * D_MODEL)), _const_spec((D_IN, D_MODEL)), _const_spec((D_MODEL, D_MODEL)),
                  _const_spec((N_GROUPS, GROUP, GROUP)), _const_spec((1, D_POOL)), _const_spec((1, D_SGU)),
                  _const_spec((1, D_SGU)), _const_spec((N_HEADS, CHUNK, CHUNK)), _const_spec((CHUNK, N_HEADS)),
                  _const_spec((1, D_MODEL))] + [ANY] * n_red,
        out_specs=[rev(D_MODEL), rev(D_IN), rev(D_MODEL), rev(D_MODEL), rev(D_MODEL),
                   _const_spec((PACK_ROWS, 128)), _const_spec((1, 3 * D_MODEL)), _const_spec((1, D_MODEL))]
                  + [_const_spec(p.shape[2:]) for p in reduce],
        out_shape=[jax.ShapeDtypeStruct((SEQ, D_MODEL), F32), jax.ShapeDtypeStruct((SEQ, D_IN), BF16),
                   jax.ShapeDtypeStruct((SEQ, D_MODEL), BF16), jax.ShapeDtypeStruct((SEQ, D_MODEL), BF16),
                   jax.ShapeDtypeStruct((SEQ, D_MODEL), BF16), jax.ShapeDtypeStruct((PACK_ROWS, 128), F32),
                   jax.ShapeDtypeStruct((1, 3 * D_MODEL), F32), jax.ShapeDtypeStruct((1, D_MODEL), F32)]
                  + [jax.ShapeDtypeStruct(p.shape[2:], F32) for p in reduce],
        scratch_shapes=[pltpu.VMEM((8, D_MODEL), F32), pltpu.VMEM((N_HEADS, CHUNK, HEAD), F32),
                        pltpu.VMEM((HALO, D_POOL), F32)] + comm_scratch,
        compiler_params=pltpu.CompilerParams(dimension_semantics=("arbitrary",), vmem_limit_bytes=VMEM_LIMIT),
    )(a, b, x, proj, proj, y, mod_l, w_int, w_outf, w_pool, pscale, slng, slnb, w_sgu, bsgut, ln_g, *reduce)


def _grad_matmul(lhs, rhs, block_cols, name):
    m, n = lhs.shape[1], rhs.shape[1]

    def body(lhs_ref, rhs_ref, out_ref):
        out_ref[...] = _dot_tn(lhs_ref[...], rhs_ref[...]).astype(BF16)

    return pl.pallas_call(
        body,
        name=name,
        grid=(m // block_cols,),
        in_specs=[pl.BlockSpec((SEQ, block_cols), lambda j: (0, j)), pl.BlockSpec((SEQ, n), lambda j: (0, 0))],
        out_specs=pl.BlockSpec((block_cols, n), lambda j: (j, 0)),
        out_shape=jax.ShapeDtypeStruct((m, n), BF16),
        compiler_params=pltpu.CompilerParams(dimension_semantics=("arbitrary",), vmem_limit_bytes=VMEM_LIMIT),
    )(lhs, rhs)


def _adamw_math(w, g, m, v):
    m = ADAM_B1 * m + (1.0 - ADAM_B1) * g
    v = ADAM_B2 * v + (1.0 - ADAM_B2) * (g * g)
    m_hat = m / (1.0 - ADAM_B1 ** ADAM_STEP)
    v_hat = v / (1.0 - ADAM_B2 ** ADAM_STEP)
    delta = -ADAM_LR * (m_hat / (jnp.sqrt(v_hat) + ADAM_EPS) + ADAM_WD * w)
    return delta, m, v


def _adamw(w, g, m, v, block_rows, name):
    rows, cols = w.shape

    def body(w_ref, g_ref, m_ref, v_ref, d_ref, nm_ref, nv_ref):
        d_ref[...], nm_ref[...], nv_ref[...] = _adamw_math(w_ref[...], g_ref[...], m_ref[...], v_ref[...])

    spec = pl.BlockSpec((block_rows, cols), lambda i: (i, 0))
    return pl.pallas_call(
        body,
        name=name,
        grid=(rows // block_rows,),
        in_specs=[spec] * 4,
        out_specs=[spec] * 3,
        out_shape=[jax.ShapeDtypeStruct(w.shape, F32)] * 3,
        compiler_params=pltpu.CompilerParams(dimension_semantics=("arbitrary",), vmem_limit_bytes=VMEM_LIMIT),
    )(w, g, m, v)


def _adamw_ada(w, m, v, act_t, dmod_cols, name):
    cols = w.shape[2]

    def body(w_ref, m_ref, v_ref, act_ref, dmod_ref, g_ref, d_ref, nm_ref, nv_ref):
        act = act_ref[...]
        dm = dmod_ref[0]
        g = act[:, 0:1] * dm[0:1, :]
        for b in range(1, N_DEV):
            g = g + act[:, b:b + 1] * dm[b:b + 1, :]
        g_ref[0] = g
        d_ref[0], nm_ref[0], nv_ref[0] = _adamw_math(w_ref[0], g, m_ref[0], v_ref[0])

    spec = pl.BlockSpec((1, D_MODEL, cols), lambda l: (l, 0, 0))
    return pl.pallas_call(
        body,
        name=name,
        grid=(DEPTH,),
        in_specs=[spec, spec, spec, pl.BlockSpec((D_MODEL, N_DEV), lambda l: (0, 0)),
                  pl.BlockSpec((1, N_DEV, cols), lambda l: (l, 0, 0))],
        out_specs=[spec] * 4,
        out_shape=[jax.ShapeDtypeStruct(w.shape, F32)] * 4,
        compiler_params=pltpu.CompilerParams(dimension_semantics=("arbitrary",), vmem_limit_bytes=VMEM_LIMIT),
    )(w, m, v, act_t, dmod_cols)


def _adamw_bias(w, m, v, dmod_all, name):
    def body(w_ref, m_ref, v_ref, dmod_ref, g_ref, d_ref, nm_ref, nv_ref):
        g = dmod_ref[0]
        for b in range(1, N_DEV):
            g = g + dmod_ref[b]
        g_ref[...] = g
        d_ref[...], nm_ref[...], nv_ref[...] = _adamw_math(w_ref[...], g, m_ref[...], v_ref[...])

    return pl.pallas_call(
        body,
        name=name,
        out_shape=[jax.ShapeDtypeStruct(w.shape, F32)] * 4,
        compiler_params=pltpu.CompilerParams(vmem_limit_bytes=VMEM_LIMIT),
    )(w, m, v, dmod_all)


MESH = pl.DeviceIdType.MESH
SIBLING = 1
CHIP_RELATIONS = (2, 4, 6)
ANY = pl.BlockSpec(memory_space=pl.ANY)
VMEM = pl.BlockSpec(memory_space=pltpu.VMEM)


def _me():
    return lax.axis_index("x"), lax.axis_index("y"), lax.axis_index("c")


def _peer(r):
    x, y, c = _me()
    return (1 - x if r & 4 else x, 1 - y if r & 2 else y, 1 - c if r & 1 else c)


def _index(dev):
    return 4 * dev[0] + 2 * dev[1] + dev[2]


def _remote(src, dst, send_sem, recv_sem, dev):
    return pltpu.make_async_remote_copy(src_ref=src, dst_ref=dst, send_sem=send_sem, recv_sem=recv_sem,
                                        device_id=dev, device_id_type=MESH)


class _TwoLevelGather:
    def __init__(self, out, send_sems, recv_sems, src=None):
        self.out, self.send_sems, self.recv_sems, self.src = out, send_sems, recv_sems, src

    def _copy(self, k, block, to, src=None):
        slot = self.out.at[_index(block)]
        return _remote(slot if src is None else src, slot, self.send_sems.at[k], self.recv_sems.at[k], to)

    def _mine(self):
        me = _me()
        src = self.out.at[_index(me)] if self.src is None else self.src
        return [self._copy(0, me, _peer(SIBLING), src)] + [
            self._copy(1 + j, me, _peer(r), src) for j, r in enumerate(CHIP_RELATIONS)]

    def _passed(self):
        return [self._copy(4 + j, _peer(r), _peer(SIBLING)) for j, r in enumerate(CHIP_RELATIONS)]

    def send_mine(self):
        for cp in self._mine():
            cp.start()

    def pass_on(self):
        passed = self._passed()
        for j, r in enumerate(CHIP_RELATIONS):
            self._copy(1 + j, _peer(r), _me()).wait_recv()
            passed[j].start()

    def wait_rest(self):
        self._copy(0, _peer(SIBLING), _me()).wait_recv()
        for j, r in enumerate(CHIP_RELATIONS):
            self._copy(4 + j, _peer(r ^ SIBLING), _me()).wait_recv()

    def wait_sends(self):
        for cp in self._mine() + self._passed():
            cp.wait_send()


class _ChipReduceScatter:
    def __init__(self, arrays, l_sem, d_send, d_recv, i_send, i_recv):
        self.arrays = arrays
        self.l_sem, self.d_send, self.d_recv, self.i_send, self.i_recv = l_sem, d_send, d_recv, i_send, i_recv

    @staticmethod
    def buffers(rows, cols, dtype):
        n_rel = len(CHIP_RELATIONS)
        return [pltpu.VMEM((4, rows, cols), dtype), pltpu.VMEM((4, rows, cols), dtype),
                pltpu.VMEM((n_rel, rows, cols), dtype), pltpu.VMEM((n_rel, rows, cols), dtype)]

    @staticmethod
    def semaphores(n):
        n_rel = len(CHIP_RELATIONS)
        return [pltpu.SemaphoreType.DMA((n,)), pltpu.SemaphoreType.DMA((n,)), pltpu.SemaphoreType.DMA((n,)),
                pltpu.SemaphoreType.DMA((n_rel, n)), pltpu.SemaphoreType.DMA((n_rel, n))]

    def _staging(self):
        c = _me()[2]
        return [pltpu.make_async_copy(a["part"].at[pl.ds(0, 4), c], a["stage"], self.l_sem.at[n])
                for n, a in enumerate(self.arrays)]

    def _first(self):
        other = 1 - _me()[2]
        return [_remote(a["part"].at[pl.ds(0, 4), other], a["sib"], self.d_send.at[n], self.d_recv.at[n],
                        _peer(SIBLING)) for n, a in enumerate(self.arrays)]

    def _second(self, j):
        return [_remote(a["snd"].at[j], a["rcv"].at[j], self.i_send.at[j, n], self.i_recv.at[j, n],
                        _peer(CHIP_RELATIONS[j])) for n, a in enumerate(self.arrays)]

    def start(self):
        for cp in self._staging() + self._first():
            cp.start()

    def exchange(self):
        for cp in self._staging():
            cp.wait()
        for cp in self._first():
            cp.wait_recv()
        for j, r in enumerate(CHIP_RELATIONS):
            peer = _peer(r)
            chip = 2 * peer[0] + peer[1]
            for a in self.arrays:
                a["snd"][j] = (a["stage"][chip].astype(F32) + a["sib"][chip].astype(F32)).astype(a["snd"].dtype)
            for cp in self._second(j):
                cp.start()

    def finish(self):
        x, y, _ = _me()
        home = 2 * x + y
        for a in self.arrays:
            a["out"][...] = a["stage"][home].astype(F32) + a["sib"][home].astype(F32)
        for j in range(len(CHIP_RELATIONS)):
            for cp in self._second(j):
                cp.wait_recv()
            for a in self.arrays:
                a["out"][...] += a["rcv"][j].astype(F32)

    def wait_sends(self):
        for cp in self._first():
            cp.wait_send()
        for j in range(len(CHIP_RELATIONS)):
            for cp in self._second(j):
                cp.wait_send()


def _direct_exchange(src_of, dst_of, send_sems, recv_sems):
    me = _me()
    copies = [_remote(src_of(_peer(r)), dst_of(me), send_sems.at[r - 1], recv_sems.at[r - 1], _peer(r))
              for r in range(1, N_DEV)]
    for cp in copies:
        cp.start()
    return copies


def _wait_direct(copies):
    for cp in copies:
        cp.wait_recv()
    for cp in copies:
        cp.wait_send()


def _prep(c_row, w_ada, b_ada, blocks):
    cols = w_ada.shape[2]
    n_blocks = len(blocks)

    def body(c_ref, wada_ref, bada_ref, *rest):
        loc_refs, rest = rest[:n_blocks], rest[n_blocks:]
        act_all, mod_ref = rest[:2]
        full_refs, rest = rest[2:2 + n_blocks], rest[2 + n_blocks:]
        act_src, part, mod_recv, w_send, w_recv, w_local, a_send, a_recv, m_send, m_recv = rest
        me = _me()
        mine = _index(me)

        cval = c_ref[...]
        act_src[...] = jnp.zeros_like(act_src)
        act_src[0:1, :] = cval * jax.nn.sigmoid(cval)
        act_all[mine] = act_src[...]
        act_copies = _direct_exchange(lambda p: act_src, lambda m: act_all.at[_index(m)], a_send, a_recv)

        gathers, locals_ = [], []
        for n in range(n_blocks):
            own = pltpu.make_async_copy(loc_refs[n], full_refs[n].at[mine], w_local.at[n])
            own.start()
            locals_.append(own)
            g = _TwoLevelGather(full_refs[n], w_send.at[n], w_recv.at[n], src=loc_refs[n])
            g.send_mine()
            gathers.append(g)

        _wait_direct(act_copies)
        acts = jnp.concatenate([act_all[j, 0:1, :] for j in range(N_DEV)], axis=0)
        part[...] = jnp.zeros_like(part)
        for layer in range(DEPTH):
            res = lax.dot_general(acts, wada_ref[layer], (((1,), (0,)), ((), ())), preferred_element_type=F32,
                                  precision=lax.Precision.HIGHEST)
            for b in range(N_DEV):
                part[b, layer:layer + 1, :] = res[b:b + 1, :]
        mod_recv[mine] = part[mine]
        mod_copies = _direct_exchange(lambda p: part.at[_index(p)], lambda m: mod_recv.at[_index(m)], m_send, m_recv)

        for g in gathers:
            g.pass_on()
        for g in gathers:
            g.wait_rest()
        _wait_direct(mod_copies)
        for layer in range(DEPTH):
            for j in range(N_DEV):
                sl = slice(j * cols, (j + 1) * cols)
                mod_ref[layer:layer + 1, sl] = mod_recv[j, layer:layer + 1, :] + bada_ref[layer:layer + 1, sl]
        for g in gathers:
            g.wait_sends()
        for own in locals_:
            own.wait()

    return pl.pallas_call(
        body,
        name="prep_gather",
        in_specs=[VMEM, VMEM, VMEM] + [ANY] * n_blocks,
        out_specs=[VMEM, VMEM] + [ANY] * n_blocks,
        out_shape=[jax.ShapeDtypeStruct((N_DEV, 8, D_MODEL), F32), jax.ShapeDtypeStruct((DEPTH, 3 * D_MODEL), F32)]
                  + [jax.ShapeDtypeStruct((N_DEV,) + blk.shape, blk.dtype) for blk in blocks],
        scratch_shapes=[pltpu.VMEM((8, D_MODEL), F32), pltpu.VMEM((N_DEV, 8, cols), F32),
                        pltpu.VMEM((N_DEV, 8, cols), F32),
                        pltpu.SemaphoreType.DMA((n_blocks, 7)), pltpu.SemaphoreType.DMA((n_blocks, 7)),
                        pltpu.SemaphoreType.DMA((n_blocks,)),
                        pltpu.SemaphoreType.DMA((7,)), pltpu.SemaphoreType.DMA((7,)),
                        pltpu.SemaphoreType.DMA((7,)), pltpu.SemaphoreType.DMA((7,))],
        compiler_params=pltpu.CompilerParams(vmem_limit_bytes=VMEM_LIMIT),
    )(c_row, w_ada, b_ada, *blocks)


def _tail_reduce(parts, small, dmod8, loss_lanes):
    n_parts = len(parts)
    n_arr = n_parts + 1

    def body(*refs):
        part_refs, refs = refs[:n_arr], refs[n_arr:]
        dmod_ref, lanes_ref = refs[:2]
        shard_refs, refs = refs[2:2 + n_arr], refs[2 + n_arr:]
        stot_ref, dmod_all, loss_ref = refs[:3]
        bufs, refs = refs[3:3 + 4 * n_arr], refs[3 + 4 * n_arr:]
        loss_src, loss_all = refs[:2]
        rs_sems, (m_send, m_recv, g_send, g_recv, s_send, s_recv) = refs[2:7], refs[7:]
        mine = _index(_me())

        arrays = [dict(part=part_refs[n], out=shard_refs[n], stage=bufs[4 * n], sib=bufs[4 * n + 1],
                       snd=bufs[4 * n + 2], rcv=bufs[4 * n + 3]) for n in range(n_arr)]
        scatter = _ChipReduceScatter(arrays, *rs_sems)
        scatter.start()

        dmod_all[mine] = dmod_ref[...]
        dmod_copies = _direct_exchange(lambda p: dmod_ref, lambda m: dmod_all.at[_index(m)], m_send, m_recv)
        loss_src[...] = jnp.full(loss_src.shape, (0.5 / D_MODEL) * jnp.sum(lanes_ref[...]), F32)
        loss_all[mine] = loss_src[...]
        loss_copies = _direct_exchange(lambda p: loss_src, lambda m: loss_all.at[_index(m)], s_send, s_recv)

        scatter.exchange()
        scatter.finish()
        stot_ref[mine] = shard_refs[n_parts][...]
        gather = _TwoLevelGather(stot_ref, g_send, g_recv)
        gather.send_mine()
        gather.pass_on()
        gather.wait_rest()
        gather.wait_sends()
        _wait_direct(dmod_copies)
        _wait_direct(loss_copies)
        total = loss_all[0]
        for j in range(1, N_DEV):
            total = total + loss_all[j]
        loss_ref[...] = total
        scatter.wait_sends()

    everything = list(parts) + [small]
    comm_scratch = []
    for p in everything:
        comm_scratch += _ChipReduceScatter.buffers(p.shape[2], p.shape[3], p.dtype)
    comm_scratch += [pltpu.VMEM((8, 128), F32), pltpu.VMEM((N_DEV, 8, 128), F32)]
    comm_scratch += _ChipReduceScatter.semaphores(n_arr)
    comm_scratch += [pltpu.SemaphoreType.DMA((7,)) for _ in range(6)]
    return pl.pallas_call(
        body,
        name="grad_reduce",
        in_specs=[ANY] * n_arr + [VMEM, VMEM],
        out_specs=[VMEM] * (n_arr + 3),
        out_shape=[jax.ShapeDtypeStruct(p.shape[2:], F32) for p in everything]
                  + [jax.ShapeDtypeStruct((N_DEV,) + small.shape[2:], F32),
                     jax.ShapeDtypeStruct((N_DEV,) + dmod8.shape, F32), jax.ShapeDtypeStruct((8, 128), F32)],
        scratch_shapes=comm_scratch,
        compiler_params=pltpu.CompilerParams(vmem_limit_bytes=VMEM_LIMIT),
    )(*everything, dmod8, loss_lanes)


SMALL_NAMES = ("w_pool", "w_sgu", "pool_scale", "sgu_ln_g", "sgu_ln_b", "b_sgu", "ln_g", "ln_b")
SMALL_ROWS = (512, 512, 4, 4, 4, 4, 8, 8)


def _adamw_small(g_packed, ws, ms, vs, name):
    n = len(SMALL_NAMES)

    def body(g_ref, *refs):
        w_refs, m_refs, v_refs = refs[:n], refs[n:2 * n], refs[2 * n:3 * n]
        outs = refs[3 * n:]
        row = 0
        for p, r in enumerate(SMALL_ROWS):
            for layer in range(DEPTH):
                dst = slice(layer * r, (layer + 1) * r)
                g = g_ref[layer * PACK_ROWS + row:layer * PACK_ROWS + row + r, :]
                delta, new_m, new_v = _adamw_math(w_refs[p][dst, :], g, m_refs[p][dst, :], v_refs[p][dst, :])
                outs[p][dst, :] = g
                outs[n + p][dst, :] = delta
                outs[2 * n + p][dst, :] = new_m
                outs[3 * n + p][dst, :] = new_v
            row += r

    res = pl.pallas_call(
        body,
        name=name,
        out_shape=[jax.ShapeDtypeStruct(w.shape, F32) for w in ws] * 4,
        compiler_params=pltpu.CompilerParams(vmem_limit_bytes=VMEM_LIMIT),
    )(g_packed, *ws, *ms, *vs)
    return res[:n], res[n:2 * n], res[2 * n:3 * n], res[3 * n:]


def kernel(x, c, w_ada, b_ada, w_in, w_pool, pool_scale, sgu_ln_g, sgu_ln_b, w_sgu, b_sgu, w_out, ln_g, ln_b, loss_target, m_w_ada, m_b_ada, m_w_in, m_w_pool, m_pool_scale, m_sgu_ln_g, m_sgu_ln_b, m_w_sgu, m_b_sgu, m_w_out, m_ln_g, m_ln_b, v_w_ada, v_b_ada, v_w_in, v_w_pool, v_pool_scale, v_sgu_ln_g, v_sgu_ln_b, v_w_sgu, v_b_sgu, v_w_out, v_ln_g, v_ln_b):
    mine = _index(_me())
    small_w = dict(w_pool=w_pool, w_sgu=w_sgu, pool_scale=pool_scale, sgu_ln_g=sgu_ln_g, sgu_ln_b=sgu_ln_b,
                   b_sgu=b_sgu, ln_g=ln_g, ln_b=ln_b)
    small_m = dict(w_pool=m_w_pool, w_sgu=m_w_sgu, pool_scale=m_pool_scale, sgu_ln_g=m_sgu_ln_g,
                   sgu_ln_b=m_sgu_ln_b, b_sgu=m_b_sgu, ln_g=m_ln_g, ln_b=m_ln_b)
    small_v = dict(w_pool=v_w_pool, w_sgu=v_w_sgu, pool_scale=v_pool_scale, sgu_ln_g=v_sgu_ln_g,
                   sgu_ln_b=v_sgu_ln_b, b_sgu=v_b_sgu, ln_g=v_ln_g, ln_b=v_ln_b)

    wint_loc = jnp.transpose(w_in, (0, 2, 1)).astype(BF16)
    wout_loc = w_out.astype(BF16)
    act_slots, mod, wint0, wout0 = _prep(c, w_ada, b_ada, [wint_loc[0], wout_loc[0]])
    act_all = act_slots[:, 0, :]
    w_int, w_outf = [wint0.reshape(D_IN, D_MODEL)], [wout0.reshape(D_MODEL, D_MODEL)]

    def layer_args(l):
        return (w_int[l], w_outf[l], w_pool[l], pool_scale[l].reshape(1, D_POOL), sgu_ln_g[l].reshape(1, D_SGU),
                sgu_ln_b[l].reshape(1, D_SGU), w_sgu[l], jnp.transpose(b_sgu[l]), ln_g[l].reshape(1, D_MODEL))

    acts, cur = [], x[0]
    for l in range(DEPTH):
        nxt = [wint_loc[l + 1], wout_loc[l + 1]] if l + 1 < DEPTH else []
        out, proj, y, *gathered = _layer_forward(cur, mod[l:l + 1], *layer_args(l), ln_b[l].reshape(1, D_MODEL),
                                                 f"layer_fwd_{l}", gather=nxt)
        if gathered:
            w_int.append(gathered[0].reshape(D_IN, D_MODEL))
            w_outf.append(gathered[1].reshape(D_MODEL, D_MODEL))
        acts.append((cur, proj, y))
        cur = out

    shard_in, shard_out = D_IN // N_DEV, D_MODEL // N_DEV
    a, b = cur, loss_target[0]
    smalls, dmods, loss_lanes = [None] * DEPTH, [None] * DEPTH, None
    g_w_in_t, g_w_out = [None] * DEPTH, [None] * DEPTH
    pending = []
    for l in reversed(range(DEPTH)):
        xin, proj, y = acts[l]
        dx, dproj, h, cat, dy, small, dmod, lanes, *shards = _layer_backward(
            a, b, xin, proj, y, mod[l:l + 1], *layer_args(l), l == DEPTH - 1, f"layer_bwd_{l}", reduce=pending)
        if shards:
            g_w_in_t[l + 1], g_w_out[l + 1] = shards
        if l == DEPTH - 1:
            loss_lanes = lanes
        pending = [_grad_matmul(dproj, h, 640, f"grad_w_in_{l}").reshape(4, 2, shard_in, D_MODEL),
                   _grad_matmul(cat, dy, 512, f"grad_w_out_{l}").reshape(4, 2, shard_out, D_MODEL)]
        smalls[l], dmods[l] = small, dmod
        a = b = dx
    grad_x = a[None]

    g_w_in_t[0], g_w_out[0], _, small_tot, dmod_slots, loss_tile = _tail_reduce(
        pending, jnp.concatenate(smalls, axis=0).reshape(4, 2, DEPTH * PACK_ROWS // N_DEV, 128),
        jnp.concatenate(dmods, axis=0).reshape(8, 3 * D_MODEL * DEPTH // 8), loss_lanes)
    loss = loss_tile[0, 0]
    dmod_all = dmod_slots.reshape(N_DEV, DEPTH, 3 * D_MODEL)
    g_w_in = jnp.transpose(jnp.stack(g_w_in_t), (0, 2, 1))
    gwout = jnp.stack(g_w_out)

    cols_ada = w_ada.shape[2]
    dmod_cols = jnp.transpose(lax.dynamic_slice_in_dim(dmod_all, mine * cols_ada, cols_ada, axis=2), (1, 0, 2))
    g_w_ada, d_w_ada, nm_w_ada, nv_w_ada = _adamw_ada(w_ada, m_w_ada, v_w_ada, jnp.transpose(act_all), dmod_cols,
                                                      "adamw_w_ada")
    g_b_ada, d_b_ada, nm_b_ada, nv_b_ada = _adamw_bias(b_ada, m_b_ada, v_b_ada, dmod_all, "adamw_b_ada")
    flat = lambda t: t.reshape(-1, t.shape[-1])
    d_w_in, nm_w_in, nv_w_in = [t.reshape(w_in.shape) for t in
                                _adamw(flat(w_in), flat(g_w_in), flat(m_w_in), flat(v_w_in), 512, "adamw_w_in")]
    d_w_out, nm_w_out, nv_w_out = [t.reshape(w_out.shape) for t in
                                   _adamw(flat(w_out), flat(gwout), flat(m_w_out), flat(v_w_out), 128, "adamw_w_out")]
    rows128 = lambda t: t.reshape(-1, 128)
    small_out = _adamw_small(small_tot.reshape(DEPTH * PACK_ROWS, 128), [rows128(small_w[n]) for n in SMALL_NAMES],
                             [rows128(small_m[n]) for n in SMALL_NAMES], [rows128(small_v[n]) for n in SMALL_NAMES],
                             "adamw_small")
    gs, ds, ms, vs = [{n: t.reshape(small_w[n].shape) for n, t in zip(SMALL_NAMES, group)} for group in small_out]

    def ordered(w_ada_, b_ada_, w_in_, small, w_out_):
        return (w_ada_, b_ada_, w_in_, small["w_pool"], small["pool_scale"], small["sgu_ln_g"], small["sgu_ln_b"],
                small["w_sgu"], small["b_sgu"], w_out_, small["ln_g"], small["ln_b"])

    return (loss, grad_x,
            *ordered(g_w_ada, g_b_ada, g_w_in, gs, gwout),
            *ordered(d_w_ada, d_b_ada, d_w_in, ds, d_w_out),
            *ordered(nm_w_ada, nm_b_ada, nm_w_in, ms, nm_w_out),
            *ordered(nv_w_ada, nv_b_ada, nv_w_in, vs, nv_w_out))
```

```python
import jax
import jax.numpy as jnp
from jax import lax
from jax.experimental import pallas as pl
from jax.experimental.pallas import tpu as pltpu

F32 = jnp.float32
BF16 = jnp.bfloat16

D_MODEL = 1024
SEQ = 2048
DEPTH = 2
D_POOL = 512
D_SGU = 512
D_IN = 2560
N_GROUPS = 4
GROUP = 128
N_HEADS = 4
HEAD = 128
CHUNK = 128
WINDOWS = (2, 4, 8, 16)
ALPHA = (2.0 * DEPTH) ** 0.25
LN_EPS = 1e-5
N_DEV = 8

ADAM_LR = 0.001
ADAM_B1 = 0.9
ADAM_B2 = 0.999
ADAM_EPS = 1e-08
ADAM_WD = 0.01
ADAM_STEP = 10

TM = 256
HALO = 16
N_TILES = SEQ // TM
VMEM_LIMIT = 60 * 1024 * 1024

ROW_WPOOL = 0
ROW_WSGU = 512
ROW_PSCALE = 1024
ROW_SLNG = 1028
ROW_SLNB = 1032
ROW_BSGU = 1036
ROW_LNG = 1040
ROW_LNB = 1048
PACK_ROWS = 1088

SQRT_HALF = 0.7071067811865476
INV_SQRT_2PI = 0.3989422804014327


def _ln(x):
    mu = jnp.mean(x, axis=-1, keepdims=True)
    xc = x - mu
    var = jnp.mean(xc * xc, axis=-1, keepdims=True)
    rstd = lax.rsqrt(var + LN_EPS)
    return xc * rstd, rstd


def _ln_bwd(dxn, xn, rstd):
    m1 = jnp.mean(dxn, axis=-1, keepdims=True)
    m2 = jnp.mean(dxn * xn, axis=-1, keepdims=True)
    return rstd * (dxn - m1 - xn * m2)


def _gelu_parts(x, with_grad):
    cdf = 0.5 * (1.0 + lax.erf(x * SQRT_HALF))
    if not with_grad:
        return x * cdf, None
    return x * cdf, cdf + x * (INV_SQRT_2PI * jnp.exp(-0.5 * x * x))


def _silu_parts(x):
    s = jax.nn.sigmoid(x)
    return x * s, s * (1.0 + x * (1.0 - s))


def _dot(a, b):
    return lax.dot_general(a, b, (((1,), (0,)), ((), ())), preferred_element_type=F32)


def _dot_nt(a, b):
    return lax.dot_general(a, b, (((1,), (1,)), ((), ())), preferred_element_type=F32)


def _dot_tn(a, b):
    return lax.dot_general(a, b, (((0,), (0,)), ((), ())), preferred_element_type=F32)


def _row_index(tile):
    return tile * TM + lax.broadcasted_iota(jnp.int32, (TM, 1), 0)


def _window_sums(ext, forward):
    n = TM + HALO
    cur = ext
    outs = []
    for g in range(N_GROUPS):
        step = 1 << g
        cur = cur + pltpu.roll(cur, step if forward else n - step, 0)
        rows = cur[HALO:, :GROUP] if forward else cur[:TM, :GROUP]
        outs.append(rows)
        cur = cur[:, GROUP:] if g + 1 < N_GROUPS else None
    return outs


def _inverse_counts(rows):
    return [1.0 / jnp.minimum(rows + 1, w).astype(F32) for w in WINDOWS]


def _tril_bf16(w):
    t = lax.broadcasted_iota(jnp.int32, (CHUNK, CHUNK), 0)
    s = lax.broadcasted_iota(jnp.int32, (CHUNK, CHUNK), 1)
    return jnp.where(t >= s, w, 0.0).astype(BF16)


def _mix_forward(proj, halo, tile, wpool_ref, pscale, slng, slnb, wsgu_ref, bsgut_ref, keep):
    rows = _row_index(tile)
    inv_counts = _inverse_counts(rows)
    xa = proj[:, 0:D_POOL]
    ga = proj[:, D_POOL:2 * D_POOL]
    sums = _window_sums(jnp.concatenate([halo, xa], axis=0), True)
    ga_act, ga_grad = _silu_parts(ga)
    pooled, pw, ya = [], [], []
    for g in range(N_GROUPS):
        sl = slice(g * GROUP, (g + 1) * GROUP)
        p = (sums[g] * inv_counts[g] - xa[:, sl]).astype(BF16)
        q = _dot(p, wpool_ref[g].astype(BF16))
        pooled.append(p)
        pw.append(q)
        ya.append(q * pscale[:, sl] * ga_act[:, sl])

    u = proj[:, 2 * D_POOL:2 * D_POOL + D_SGU]
    v = proj[:, 2 * D_POOL + D_SGU:2 * D_POOL + 2 * D_SGU]
    gb = proj[:, 2 * D_POOL + 2 * D_SGU:]
    gb_act, gb_grad = _silu_parts(gb)
    u_act, u_grad = _gelu_parts(u, keep)
    v_act, v_grad = _gelu_parts(v, keep)
    vn, vrstd, vln, mixed, yb = [], [], [], [], []
    for h in range(N_HEADS):
        sl = slice(h * HEAD, (h + 1) * HEAD)
        n_h, r_h = _ln(v_act[:, sl])
        l_h = (n_h * slng[:, sl] + slnb[:, sl]).astype(BF16)
        w_h = _tril_bf16(wsgu_ref[h])
        bias = bsgut_ref[:, h:h + 1]
        m_h = jnp.concatenate(
            [_dot(w_h, l_h[k * CHUNK:(k + 1) * CHUNK]) + bias for k in range(TM // CHUNK)], axis=0)
        vn.append(n_h)
        vrstd.append(r_h)
        vln.append(l_h)
        mixed.append(m_h)
        yb.append(u_act[:, sl] * m_h * gb_act[:, sl])
    cat = jnp.concatenate(ya + yb, axis=1)
    if not keep:
        return cat, None
    return cat, dict(inv_counts=inv_counts, ga_act=ga_act, ga_grad=ga_grad, pooled=pooled, pw=pw, u_grad=u_grad,
                     v_grad=v_grad, u_act=u_act, gb_act=gb_act, gb_grad=gb_grad, vn=vn, vrstd=vrstd, vln=vln,
                     mixed=mixed)


def _const_spec(shape):
    nd = len(shape)
    return pl.BlockSpec(shape, lambda i: (0,) * nd)


def _layer_forward(x, mod_l, w_int, w_outf, w_pool, pscale, slng, slnb, w_sgu, bsgut, ln_g, ln_b, name, gather=()):
    n_gather = len(gather)

    def body(x_ref, mod_ref, wint_ref, wout_ref, wpool_ref, pscale_ref, slng_ref, slnb_ref, wsgu_ref, bsgut_ref,
             lng_ref, lnb_ref, *rest):
        loc_refs, rest = rest[:n_gather], rest[n_gather:]
        out_ref, proj_ref, y_ref = rest[:3]
        full_refs, rest = rest[3:3 + n_gather], rest[3 + n_gather:]
        halo_ref = rest[0]
        tile = pl.program_id(0)

        def gathers():
            g_send, g_recv, _ = rest[1:]
            return [_TwoLevelGather(full_refs[n], g_send.at[n], g_recv.at[n], src=loc_refs[n])
                    for n in range(n_gather)]

        def own_copies():
            mine = _index(_me())
            return [pltpu.make_async_copy(loc_refs[n], full_refs[n].at[mine], rest[3].at[n]) for n in range(n_gather)]

        @pl.when(tile == 0)
        def _():
            halo_ref[...] = jnp.zeros_like(halo_ref)
            if n_gather:
                for cp in own_copies():
                    cp.start()
                for g in gathers():
                    g.send_mine()

        if n_gather:
            @pl.when(tile == N_TILES - 2)
            def _():
                for g in gathers():
                    g.pass_on()

        xt = x_ref[...]
        shift = mod_ref[:, 0:D_MODEL]
        scale = mod_ref[:, D_MODEL:2 * D_MODEL]
        gate = mod_ref[:, 2 * D_MODEL:]
        xn, _ = _ln(xt)
        h = (xn * (1.0 + scale) + shift).astype(BF16)
        proj = _dot_nt(h, wint_ref[...])
        proj_ref[...] = proj
        cat, _ = _mix_forward(proj, halo_ref[...], tile, wpool_ref, pscale_ref[...], slng_ref[...], slnb_ref[...],
                              wsgu_ref, bsgut_ref, False)
        halo_ref[...] = proj[TM - HALO:, 0:D_POOL]
        y = _dot(cat.astype(BF16), wout_ref[...])
        y_ref[...] = y
        zn, _ = _ln(ALPHA * xt + gate * y)
        out_ref[...] = zn * lng_ref[...] + lnb_ref[...]

        if n_gather:
            @pl.when(tile == N_TILES - 1)
            def _():
                for g in gathers():
                    g.wait_rest()
                for g in gathers():
                    g.wait_sends()
                for cp in own_copies():
                    cp.wait()

    row = lambda w: pl.BlockSpec((TM, w), lambda i: (i, 0))
    comm_scratch = [pltpu.SemaphoreType.DMA((n_gather, 7)), pltpu.SemaphoreType.DMA((n_gather, 7)),
                    pltpu.SemaphoreType.DMA((n_gather,))] if n_gather else []
    return pl.pallas_call(
        body,
        name=name,
        grid=(N_TILES,),
        in_specs=[row(D_MODEL), _const_spec((1, 3 * D_MODEL)), _const_spec((D_IN, D_MODEL)),
                  _const_spec((D_MODEL, D_MODEL)), _const_spec((N_GROUPS, GROUP, GROUP)), _const_spec((1, D_POOL)),
                  _const_spec((1, D_SGU)), _const_spec((1, D_SGU)), _const_spec((N_HEADS, CHUNK, CHUNK)),
                  _const_spec((CHUNK, N_HEADS)), _const_spec((1, D_MODEL)), _const_spec((1, D_MODEL))]
                 + [ANY] * n_gather,
        out_specs=[row(D_MODEL), row(D_IN), row(D_MODEL)] + [ANY] * n_gather,
        out_shape=[jax.ShapeDtypeStruct((SEQ, D_MODEL), F32), jax.ShapeDtypeStruct((SEQ, D_IN), F32),
                   jax.ShapeDtypeStruct((SEQ, D_MODEL), F32)]
                  + [jax.ShapeDtypeStruct((N_DEV,) + g.shape, g.dtype) for g in gather],
        scratch_shapes=[pltpu.VMEM((HALO, D_POOL), F32)] + comm_scratch,
        compiler_params=pltpu.CompilerParams(dimension_semantics=("arbitrary",), vmem_limit_bytes=VMEM_LIMIT),
    )(x, mod_l, w_int, w_outf, w_pool, pscale, slng, slnb, w_sgu, bsgut, ln_g, ln_b, *gather)


VEC_LNG, VEC_LNB, VEC_POOL, VEC_SGU, VEC_SHIFT, VEC_SCALE, VEC_GATE, VEC_LOSS = range(8)


def _layer_backward(a, b, x, proj, y, mod_l, w_int, w_outf, w_pool, pscale, slng, slnb, w_sgu, bsgut, ln_g, is_last,
                    name, reduce=()):
    n_red = len(reduce)

    def body(a_ref, b_ref, x_ref, proj_ref, prev_ref, y_ref, mod_ref, wint_ref, wout_ref, wpool_ref, pscale_ref,
             slng_ref, slnb_ref, wsgu_ref, bsgut_ref, lng_ref, *rest):
        part_refs, rest = rest[:n_red], rest[n_red:]
        dx_ref, dproj_ref, h_ref, cat_ref, dy_ref, small_ref, dmod_ref, loss_ref = rest[:8]
        shard_refs, rest = rest[8:8 + n_red], rest[8 + n_red:]
        vec_ref, dmix_ref, halo_ref = rest[:3]
        step = pl.program_id(0)
        tile = N_TILES - 1 - step

        def scatter():
            bufs, sems = rest[3:3 + 4 * n_red], rest[3 + 4 * n_red:]
            arrays = [dict(part=part_refs[n], out=shard_refs[n], stage=bufs[4 * n], sib=bufs[4 * n + 1],
                           snd=bufs[4 * n + 2], rcv=bufs[4 * n + 3]) for n in range(n_red)]
            return _ChipReduceScatter(arrays, *sems)

        @pl.when(step == 0)
        def _():
            small_ref[...] = jnp.zeros_like(small_ref)
            vec_ref[...] = jnp.zeros_like(vec_ref)
            dmix_ref[...] = jnp.zeros_like(dmix_ref)
            halo_ref[...] = jnp.zeros_like(halo_ref)
            if n_red:
                scatter().start()

        if n_red:
            @pl.when(step == 1)
            def _():
                scatter().exchange()

        def acc(row, lo, val):
            hi = lo + val.shape[1]
            vec_ref[row:row + 1, lo:hi] += jnp.sum(val, axis=0, keepdims=True)

        xt = x_ref[...]
        yt = y_ref[...]
        shift = mod_ref[:, 0:D_MODEL]
        scale = mod_ref[:, D_MODEL:2 * D_MODEL]
        gate = mod_ref[:, 2 * D_MODEL:]

        zn, zrstd = _ln(ALPHA * xt + gate * yt)
        if is_last:
            diff = a_ref[...] - b_ref[...]
            acc(VEC_LOSS, 0, diff * diff)
            dout = diff * (1.0 / D_MODEL)
        else:
            dout = a_ref[...]
        acc(VEC_LNG, 0, dout * zn)
        acc(VEC_LNB, 0, dout)
        dz = _ln_bwd(dout * lng_ref[...], zn, zrstd)
        acc(VEC_GATE, 0, dz * yt)
        dy = (dz * gate).astype(BF16)
        dy_ref[...] = dy
        dcat = _dot_nt(dy, wout_ref[...])

        proj = proj_ref[...]
        prev = jnp.where(tile > 0, prev_ref[...], 0.0)
        cat, k = _mix_forward(proj, prev, tile, wpool_ref, pscale_ref[...], slng_ref[...], slnb_ref[...],
                              wsgu_ref, bsgut_ref, True)
        cat_ref[...] = cat.astype(BF16)
        pscale = pscale_ref[...]
        slng = slng_ref[...]

        dga, dq = [], []
        for g in range(N_GROUPS):
            sl = slice(g * GROUP, (g + 1) * GROUP)
            dya = dcat[:, sl]
            dyp = dya * k["ga_act"][:, sl]
            dga.append(dya * k["pw"][g] * pscale[:, sl] * k["ga_grad"][:, sl])
            acc(VEC_POOL, g * GROUP, dyp * k["pw"][g])
            dpw = (dyp * pscale[:, sl]).astype(BF16)
            small_ref[ROW_WPOOL + g * GROUP:ROW_WPOOL + (g + 1) * GROUP, :] += _dot_tn(k["pooled"][g], dpw)
            dq.append(_dot_nt(dpw, wpool_ref[g].astype(BF16)))
        dpooled = jnp.concatenate(dq, axis=1)
        scaled = jnp.concatenate([dq[g] * k["inv_counts"][g] for g in range(N_GROUPS)], axis=1)
        sums = _window_sums(jnp.concatenate([scaled, halo_ref[...]], axis=0), False)
        halo_ref[...] = scaled[0:HALO]
        dxa = jnp.concatenate(sums, axis=1) - dpooled

        du, dv, dgb = [], [], []
        for h in range(N_HEADS):
            sl = slice(h * HEAD, (h + 1) * HEAD)
            dyb = dcat[:, D_POOL + h * HEAD:D_POOL + (h + 1) * HEAD]
            m_h = k["mixed"][h]
            ug = k["u_act"][:, sl] * dyb
            du.append(dyb * m_h * k["gb_act"][:, sl] * k["u_grad"][:, sl])
            dgb.append(ug * m_h * k["gb_grad"][:, sl])
            dmixed = ug * k["gb_act"][:, sl]
            dmixed_bf = dmixed.astype(BF16)
            w_h = _tril_bf16(wsgu_ref[h])
            dvln_parts = []
            dmix_sum = dmix_ref[h]
            dws = small_ref[ROW_WSGU + h * CHUNK:ROW_WSGU + (h + 1) * CHUNK, :]
            for c in range(TM // CHUNK):
                cs = slice(c * CHUNK, (c + 1) * CHUNK)
                dmix_sum = dmix_sum + dmixed[cs]
                dws = dws + _dot_nt(dmixed_bf[cs], k["vln"][h][cs])
                dvln_parts.append(_dot_tn(w_h, dmixed_bf[cs]))
            dmix_ref[h] = dmix_sum
            small_ref[ROW_WSGU + h * CHUNK:ROW_WSGU + (h + 1) * CHUNK, :] = dws
            dvln = jnp.concatenate(dvln_parts, axis=0)
            acc(VEC_SGU, h * HEAD, dvln * k["vn"][h])
            acc(VEC_SGU, D_SGU + h * HEAD, dvln)
            dvv = _ln_bwd(dvln * slng[:, sl], k["vn"][h], k["vrstd"][h])
            dv.append(dvv * k["v_grad"][:, sl])

        dproj = jnp.concatenate([dxa] + dga + du + dv + dgb, axis=1).astype(BF16)
        dproj_ref[...] = dproj
        dh = _dot(dproj, wint_ref[...])

        xn, xrstd = _ln(xt)
        h_ref[...] = (xn * (1.0 + scale) + shift).astype(BF16)
        acc(VEC_SCALE, 0, dh * xn)
        acc(VEC_SHIFT, 0, dh)
        dx_ref[...] = _ln_bwd(dh * (1.0 + scale), xn, xrstd) + ALPHA * dz

        @pl.when(step == N_TILES - 1)
        def _():
            def put(row0, vec_row, lo, n):
                for r in range(n):
                    small_ref[row0 + r:row0 + r + 1, :] = vec_ref[vec_row:vec_row + 1, lo + r * 128:lo + (r + 1) * 128]

            put(ROW_PSCALE, VEC_POOL, 0, 4)
            put(ROW_SLNG, VEC_SGU, 0, 4)
            put(ROW_SLNB, VEC_SGU, D_SGU, 4)
            put(ROW_LNG, VEC_LNG, 0, 8)
            put(ROW_LNB, VEC_LNB, 0, 8)
            ones = jnp.ones((8, HEAD), F32)
            t = lax.broadcasted_iota(jnp.int32, (CHUNK, CHUNK), 0)
            s = lax.broadcasted_iota(jnp.int32, (CHUNK, CHUNK), 1)
            for h in range(N_HEADS):
                bias_rows = lax.dot_general(ones, dmix_ref[h], (((1,), (1,)), ((), ())),
                                            preferred_element_type=F32, precision=lax.Precision.HIGHEST)
                small_ref[ROW_BSGU + h:ROW_BSGU + h + 1, :] = bias_rows[0:1]
                blk = small_ref[ROW_WSGU + h * CHUNK:ROW_WSGU + (h + 1) * CHUNK, :]
                small_ref[ROW_WSGU + h * CHUNK:ROW_WSGU + (h + 1) * CHUNK, :] = jnp.where(t >= s, blk, 0.0)
            dmod_ref[:, 0:D_MODEL] = vec_ref[VEC_SHIFT:VEC_SHIFT + 1, :]
            dmod_ref[:, D_MODEL:2 * D_MODEL] = vec_ref[VEC_SCALE:VEC_SCALE + 1, :]
            dmod_ref[:, 2 * D_MODEL:] = vec_ref[VEC_GATE:VEC_GATE + 1, :]
            loss_ref[...] = vec_ref[VEC_LOSS:VEC_LOSS + 1, :]
            if n_red:
                scatter().finish()
                scatter().wait_sends()

    rev = lambda w: pl.BlockSpec((TM, w), lambda i: (N_TILES - 1 - i, 0))
    prev_spec = pl.BlockSpec(
        (HALO, D_POOL), lambda i: (jnp.maximum((N_TILES - 1 - i) * (TM // HALO) - 1, 0), 0))
    comm_scratch = []
    for p in reduce:
        comm_scratch += _ChipReduceScatter.buffers(p.shape[2], p.shape[3], p.dtype)
    if n_red:
        comm_scratch += _ChipReduceScatter.semaphores(n_red)
    return pl.pallas_call(
        body,
        name=name,
        grid=(N_TILES,),
        in_specs=[rev(D_MODEL), rev(D_MODEL) if is_last else pl.BlockSpec((TM, D_MODEL), lambda i: (0, 0)),
                  rev(D_MODEL), rev(D_IN), prev_spec, rev(D_MODEL),
                  _const_spec((1, 3 * D_MODEL)), _const_spec((D_IN, D_MODEL)), _const_spec((D_MODEL, D_MODEL)),
                  _const_spec((N_GROUPS, GROUP, GROUP)), _const_spec((1, D_POOL)), _const_spec((1, D_SGU)),
                  _const_spec((1, D_SGU)), _const_spec((N_HEADS, CHUNK, CHUNK)), _const_spec((CHUNK, N_HEADS)),
                  _const_spec((1, D_MODEL))] + [ANY] * n_red,
        out_specs=[rev(D_MODEL), rev(D_IN), rev(D_MODEL), rev(D_MODEL), rev(D_MODEL),
                   _const_spec((PACK_ROWS, 128)), _const_spec((1, 3 * D_MODEL)), _const_spec((1, D_MODEL))]
                  + [_const_spec(p.shape[2:]) for p in reduce],
        out_shape=[jax.ShapeDtypeStruct((SEQ, D_MODEL), F32), jax.ShapeDtypeStruct((SEQ, D_IN), BF16),
                   jax.ShapeDtypeStruct((SEQ, D_MODEL), BF16), jax.ShapeDtypeStruct((SEQ, D_MODEL), BF16),
                   jax.ShapeDtypeStruct((SEQ, D_MODEL), BF16), jax.ShapeDtypeStruct((PACK_ROWS, 128), F32),
                   jax.ShapeDtypeStruct((1, 3 * D_MODEL), F32), jax.ShapeDtypeStruct((1, D_MODEL), F32)]
                  + [jax.ShapeDtypeStruct(p.shape[2:], F32) for p in reduce],
        scratch_shapes=[pltpu.VMEM((8, D_MODEL), F32), pltpu.VMEM((N_HEADS, CHUNK, HEAD), F32),
                        pltpu.VMEM((HALO, D_POOL), F32)] + comm_scratch,
        compiler_params=pltpu.CompilerParams(dimension_semantics=("arbitrary",), vmem_limit_bytes=VMEM_LIMIT),
    )(a, b, x, proj, proj, y, mod_l, w_int, w_outf, w_pool, pscale, slng, slnb, w_sgu, bsgut, ln_g, *reduce)


def _grad_matmul(lhs, rhs, block_cols, name):
    m, n = lhs.shape[1], rhs.shape[1]

    def body(lhs_ref, rhs_ref, out_ref):
        out_ref[...] = _dot_tn(lhs_ref[...], rhs_ref[...]).astype(BF16)

    return pl.pallas_call(
        body,
        name=name,
        grid=(m // block_cols,),
        in_specs=[pl.BlockSpec((SEQ, block_cols), lambda j: (0, j)), pl.BlockSpec((SEQ, n), lambda j: (0, 0))],
        out_specs=pl.BlockSpec((block_cols, n), lambda j: (j, 0)),
        out_shape=jax.ShapeDtypeStruct((m, n), BF16),
        compiler_params=pltpu.CompilerParams(dimension_semantics=("arbitrary",), vmem_limit_bytes=VMEM_LIMIT),
    )(lhs, rhs)


def _adamw_math(w, g, m, v):
    m = ADAM_B1 * m + (1.0 - ADAM_B1) * g
    v = ADAM_B2 * v + (1.0 - ADAM_B2) * (g * g)
    m_hat = m / (1.0 - ADAM_B1 ** ADAM_STEP)
    v_hat = v / (1.0 - ADAM_B2 ** ADAM_STEP)
    delta = -ADAM_LR * (m_hat / (jnp.sqrt(v_hat) + ADAM_EPS) + ADAM_WD * w)
    return delta, m, v


def _adamw(w, g, m, v, block_rows, name):
    rows, cols = w.shape

    def body(w_ref, g_ref, m_ref, v_ref, d_ref, nm_ref, nv_ref):
        d_ref[...], nm_ref[...], nv_ref[...] = _adamw_math(w_ref[...], g_ref[...], m_ref[...], v_ref[...])

    spec = pl.BlockSpec((block_rows, cols), lambda i: (i, 0))
    return pl.pallas_call(
        body,
        name=name,
        grid=(rows // block_rows,),
        in_specs=[spec] * 4,
        out_specs=[spec] * 3,
        out_shape=[jax.ShapeDtypeStruct(w.shape, F32)] * 3,
        compiler_params=pltpu.CompilerParams(dimension_semantics=("arbitrary",), vmem_limit_bytes=VMEM_LIMIT),
    )(w, g, m, v)


def _adamw_ada(w, m, v, act_t, dmod_cols, name):
    cols = w.shape[2]

    rows = 256

    def body(w_ref, m_ref, v_ref, act_ref, dmod_ref, g_ref, d_ref, nm_ref, nv_ref):
        act = act_ref[...]
        dm = dmod_ref[0]
        g = act[:, 0:1] * dm[0:1, :]
        for b in range(1, N_DEV):
            g = g + act[:, b:b + 1] * dm[b:b + 1, :]
        g_ref[0] = g
        d_ref[0], nm_ref[0], nv_ref[0] = _adamw_math(w_ref[0], g, m_ref[0], v_ref[0])

    spec = pl.BlockSpec((1, rows, cols), lambda l, i: (l, i, 0))
    return pl.pallas_call(
        body,
        name=name,
        grid=(DEPTH, D_MODEL // rows),
        in_specs=[spec, spec, spec, pl.BlockSpec((rows, N_DEV), lambda l, i: (i, 0)),
                  pl.BlockSpec((1, N_DEV, cols), lambda l, i: (l, 0, 0))],
        out_specs=[spec] * 4,
        out_shape=[jax.ShapeDtypeStruct(w.shape, F32)] * 4,
        compiler_params=pltpu.CompilerParams(dimension_semantics=("arbitrary", "arbitrary"),
                                             vmem_limit_bytes=VMEM_LIMIT),
    )(w, m, v, act_t, dmod_cols)


def _adamw_bias(w, m, v, dmod_all, name):
    def body(w_ref, m_ref, v_ref, dmod_ref, g_ref, d_ref, nm_ref, nv_ref):
        g = dmod_ref[0]
        for b in range(1, N_DEV):
            g = g + dmod_ref[b]
        g_ref[...] = g
        d_ref[...], nm_ref[...], nv_ref[...] = _adamw_math(w_ref[...], g, m_ref[...], v_ref[...])

    return pl.pallas_call(
        body,
        name=name,
        out_shape=[jax.ShapeDtypeStruct(w.shape, F32)] * 4,
        compiler_params=pltpu.CompilerParams(vmem_limit_bytes=VMEM_LIMIT),
    )(w, m, v, dmod_all)


MESH = pl.DeviceIdType.MESH
SIBLING = 1
CHIP_RELATIONS = (2, 4, 6)
ANY = pl.BlockSpec(memory_space=pl.ANY)
VMEM = pl.BlockSpec(memory_space=pltpu.VMEM)


def _me():
    return lax.axis_index("x"), lax.axis_index("y"), lax.axis_index("c")


def _peer(r):
    x, y, c = _me()
    return (1 - x if r & 4 else x, 1 - y if r & 2 else y, 1 - c if r & 1 else c)


def _index(dev):
    return 4 * dev[0] + 2 * dev[1] + dev[2]


def _remote(src, dst, send_sem, recv_sem, dev):
    return pltpu.make_async_remote_copy(src_ref=src, dst_ref=dst, send_sem=send_sem, recv_sem=recv_sem,
                                        device_id=dev, device_id_type=MESH)


class _TwoLevelGather:
    def __init__(self, out, send_sems, recv_sems, src=None):
        self.out, self.send_sems, self.recv_sems, self.src = out, send_sems, recv_sems, src

    def _copy(self, k, block, to, src=None):
        slot = self.out.at[_index(block)]
        return _remote(slot if src is None else src, slot, self.send_sems.at[k], self.recv_sems.at[k], to)

    def _mine(self):
        me = _me()
        src = self.out.at[_index(me)] if self.src is None else self.src
        return [self._copy(0, me, _peer(SIBLING), src)] + [
            self._copy(1 + j, me, _peer(r), src) for j, r in enumerate(CHIP_RELATIONS)]

    def _passed(self):
        return [self._copy(4 + j, _peer(r), _peer(SIBLING)) for j, r in enumerate(CHIP_RELATIONS)]

    def send_mine(self):
        for cp in self._mine():
            cp.start()

    def pass_on(self):
        passed = self._passed()
        for j, r in enumerate(CHIP_RELATIONS):
            self._copy(1 + j, _peer(r), _me()).wait_recv()
            passed[j].start()

    def wait_rest(self):
        self._copy(0, _peer(SIBLING), _me()).wait_recv()
        for j, r in enumerate(CHIP_RELATIONS):
            self._copy(4 + j, _peer(r ^ SIBLING), _me()).wait_recv()

    def wait_sends(self):
        for cp in self._mine() + self._passed():
            cp.wait_send()


class _ChipReduceScatter:
    def __init__(self, arrays, l_sem, d_send, d_recv, i_send, i_recv):
        self.arrays = arrays
        self.l_sem, self.d_send, self.d_recv, self.i_send, self.i_recv = l_sem, d_send, d_recv, i_send, i_recv

    @staticmethod
    def buffers(rows, cols, dtype):
        n_rel = len(CHIP_RELATIONS)
        return [pltpu.VMEM((4, rows, cols), dtype), pltpu.VMEM((4, rows, cols), dtype),
                pltpu.VMEM((n_rel, rows, cols), dtype), pltpu.VMEM((n_rel, rows, cols), dtype)]

    @staticmethod
    def semaphores(n):
        n_rel = len(CHIP_RELATIONS)
        return [pltpu.SemaphoreType.DMA((n,)), pltpu.SemaphoreType.DMA((n,)), pltpu.SemaphoreType.DMA((n,)),
                pltpu.SemaphoreType.DMA((n_rel, n)), pltpu.SemaphoreType.DMA((n_rel, n))]

    def _staging(self):
        c = _me()[2]
        return [pltpu.make_async_copy(a["part"].at[pl.ds(0, 4), c], a["stage"], self.l_sem.at[n])
                for n, a in enumerate(self.arrays)]

    def _first(self):
        other = 1 - _me()[2]
        return [_remote(a["part"].at[pl.ds(0, 4), other], a["sib"], self.d_send.at[n], self.d_recv.at[n],
                        _peer(SIBLING)) for n, a in enumerate(self.arrays)]

    def _second(self, j):
        return [_remote(a["snd"].at[j], a["rcv"].at[j], self.i_send.at[j, n], self.i_recv.at[j, n],
                        _peer(CHIP_RELATIONS[j])) for n, a in enumerate(self.arrays)]

    def start(self):
        for cp in self._staging() + self._first():
            cp.start()

    def exchange(self):
        for cp in self._staging():
            cp.wait()
        for cp in self._first():
            cp.wait_recv()
        for j, r in enumerate(CHIP_RELATIONS):
            peer = _peer(r)
            chip = 2 * peer[0] + peer[1]
            for a in self.arrays:
                a["snd"][j] = (a["stage"][chip].astype(F32) + a["sib"][chip].astype(F32)).astype(a["snd"].dtype)
            for cp in self._second(j):
                cp.start()

    def finish(self):
        x, y, _ = _me()
        home = 2 * x + y
        for a in self.arrays:
            a["out"][...] = a["stage"][home].astype(F32) + a["sib"][home].astype(F32)
        for j in range(len(CHIP_RELATIONS)):
            for cp in self._second(j):
                cp.wait_recv()
            for a in self.arrays:
                a["out"][...] += a["rcv"][j].astype(F32)

    def wait_sends(self):
        for cp in self._first():
            cp.wait_send()
        for j in range(len(CHIP_RELATIONS)):
            for cp in self._second(j):
                cp.wait_send()


def _direct_exchange(src_of, dst_of, send_sems, recv_sems):
    me = _me()
    copies = [_remote(src_of(_peer(r)), dst_of(me), send_sems.at[r - 1], recv_sems.at[r - 1], _peer(r))
              for r in range(1, N_DEV)]
    for cp in copies:
        cp.start()
    return copies


def _wait_direct(copies):
    for cp in copies:
        cp.wait_recv()
    for cp in copies:
        cp.wait_send()


def _prep(c_row, w_ada, b_ada, blocks):
    cols = w_ada.shape[2]
    n_blocks = len(blocks)

    def body(c_ref, wada_ref, bada_ref, *rest):
        loc_refs, rest = rest[:n_blocks], rest[n_blocks:]
        act_all, mod_ref = rest[:2]
        full_refs, rest = rest[2:2 + n_blocks], rest[2 + n_blocks:]
        act_src, part, mod_recv, w_send, w_recv, w_local, a_send, a_recv, m_send, m_recv = rest
        me = _me()
        mine = _index(me)

        cval = c_ref[...]
        act_src[...] = jnp.zeros_like(act_src)
        act_src[0:1, :] = cval * jax.nn.sigmoid(cval)
        act_all[mine] = act_src[...]
        act_copies = _direct_exchange(lambda p: act_src, lambda m: act_all.at[_index(m)], a_send, a_recv)

        gathers, locals_ = [], []
        for n in range(n_blocks):
            own = pltpu.make_async_copy(loc_refs[n], full_refs[n].at[mine], w_local.at[n])
            own.start()
            locals_.append(own)
            g = _TwoLevelGather(full_refs[n], w_send.at[n], w_recv.at[n], src=loc_refs[n])
            g.send_mine()
            gathers.append(g)

        _wait_direct(act_copies)
        acts = jnp.concatenate([act_all[j, 0:1, :] for j in range(N_DEV)], axis=0)
        part[...] = jnp.zeros_like(part)
        for layer in range(DEPTH):
            res = lax.dot_general(acts, wada_ref[layer], (((1,), (0,)), ((), ())), preferred_element_type=F32,
                                  precision=lax.Precision.HIGHEST)
            for b in range(N_DEV):
                part[b, layer:layer + 1, :] = res[b:b + 1, :]
        mod_recv[mine] = part[mine]
        mod_copies = _direct_exchange(lambda p: part.at[_index(p)], lambda m: mod_recv.at[_index(m)], m_send, m_recv)

        for g in gathers:
            g.pass_on()
        for g in gathers:
            g.wait_rest()
        _wait_direct(mod_copies)
        for layer in range(DEPTH):
            for j in range(N_DEV):
                sl = slice(j * cols, (j + 1) * cols)
                mod_ref[layer:layer + 1, sl] = mod_recv[j, layer:layer + 1, :] + bada_ref[layer:layer + 1, sl]
        for g in gathers:
            g.wait_sends()
        for own in locals_:
            own.wait()

    return pl.pallas_call(
        body,
        name="prep_gather",
        in_specs=[VMEM, VMEM, VMEM] + [ANY] * n_blocks,
        out_specs=[VMEM, VMEM] + [ANY] * n_blocks,
        out_shape=[jax.ShapeDtypeStruct((N_DEV, 8, D_MODEL), F32), jax.ShapeDtypeStruct((DEPTH, 3 * D_MODEL), F32)]
                  + [jax.ShapeDtypeStruct((N_DEV,) + blk.shape, blk.dtype) for blk in blocks],
        scratch_shapes=[pltpu.VMEM((8, D_MODEL), F32), pltpu.VMEM((N_DEV, 8, cols), F32),
                        pltpu.VMEM((N_DEV, 8, cols), F32),
                        pltpu.SemaphoreType.DMA((n_blocks, 7)), pltpu.SemaphoreType.DMA((n_blocks, 7)),
                        pltpu.SemaphoreType.DMA((n_blocks,)),
                        pltpu.SemaphoreType.DMA((7,)), pltpu.SemaphoreType.DMA((7,)),
                        pltpu.SemaphoreType.DMA((7,)), pltpu.SemaphoreType.DMA((7,))],
        compiler_params=pltpu.CompilerParams(vmem_limit_bytes=VMEM_LIMIT),
    )(c_row, w_ada, b_ada, *blocks)


def _tail_reduce(parts, small, dmod8, loss_lanes):
    n_parts = len(parts)
    n_arr = n_parts + 1

    def body(*refs):
        part_refs, refs = refs[:n_arr], refs[n_arr:]
        dmod_ref, lanes_ref = refs[:2]
        shard_refs, refs = refs[2:2 + n_arr], refs[2 + n_arr:]
        stot_ref, dmod_all, loss_ref = refs[:3]
        bufs, refs = refs[3:3 + 4 * n_arr], refs[3 + 4 * n_arr:]
        loss_src, loss_all = refs[:2]
        rs_sems, (m_send, m_recv, g_send, g_recv, s_send, s_recv) = refs[2:7], refs[7:]
        mine = _index(_me())

        arrays = [dict(part=part_refs[n], out=shard_refs[n], stage=bufs[4 * n], sib=bufs[4 * n + 1],
                       snd=bufs[4 * n + 2], rcv=bufs[4 * n + 3]) for n in range(n_arr)]
        scatter = _ChipReduceScatter(arrays, *rs_sems)
        scatter.start()

        dmod_all[mine] = dmod_ref[...]
        dmod_copies = _direct_exchange(lambda p: dmod_ref, lambda m: dmod_all.at[_index(m)], m_send, m_recv)
        loss_src[...] = jnp.full(loss_src.shape, (0.5 / D_MODEL) * jnp.sum(lanes_ref[...]), F32)
        loss_all[mine] = loss_src[...]
        loss_copies = _direct_exchange(lambda p: loss_src, lambda m: loss_all.at[_index(m)], s_send, s_recv)

        scatter.exchange()
        scatter.finish()
        stot_ref[mine] = shard_refs[n_parts][...]
        gather = _TwoLevelGather(stot_ref, g_send, g_recv)
        gather.send_mine()
        gather.pass_on()
        gather.wait_rest()
        gather.wait_sends()
        _wait_direct(dmod_copies)
        _wait_direct(loss_copies)
        total = loss_all[0]
        for j in range(1, N_DEV):
            total = total + loss_all[j]
        loss_ref[...] = total
        scatter.wait_sends()

    everything = list(parts) + [small]
    comm_scratch = []
    for p in everything:
        comm_scratch += _ChipReduceScatter.buffers(p.shape[2], p.shape[3], p.dtype)
    comm_scratch += [pltpu.VMEM((8, 128), F32), pltpu.VMEM((N_DEV, 8, 128), F32)]
    comm_scratch += _ChipReduceScatter.semaphores(n_arr)
    comm_scratch += [pltpu.SemaphoreType.DMA((7,)) for _ in range(6)]
    return pl.pallas_call(
        body,
        name="grad_reduce",
        in_specs=[ANY] * n_arr + [VMEM, VMEM],
        out_specs=[VMEM] * (n_arr + 3),
        out_shape=[jax.ShapeDtypeStruct(p.shape[2:], F32) for p in everything]
                  + [jax.ShapeDtypeStruct((N_DEV,) + small.shape[2:], F32),
                     jax.ShapeDtypeStruct((N_DEV,) + dmod8.shape, F32), jax.ShapeDtypeStruct((8, 128), F32)],
        scratch_shapes=comm_scratch,
        compiler_params=pltpu.CompilerParams(vmem_limit_bytes=VMEM_LIMIT),
    )(*everything, dmod8, loss_lanes)


SMALL_NAMES = ("w_pool", "w_sgu", "pool_scale", "sgu_ln_g", "sgu_ln_b", "b_sgu", "ln_g", "ln_b")
SMALL_ROWS = (512, 512, 4, 4, 4, 4, 8, 8)


def _adamw_small(g_packed, ws, ms, vs, name):
    n = len(SMALL_NAMES)

    def body(g_ref, *refs):
        w_refs, m_refs, v_refs = refs[:n], refs[n:2 * n], refs[2 * n:3 * n]
        outs = refs[3 * n:]
        row = 0
        for p, r in enumerate(SMALL_ROWS):
            for layer in range(DEPTH):
                dst = slice(layer * r, (layer + 1) * r)
                g = g_ref[layer * PACK_ROWS + row:layer * PACK_ROWS + row + r, :]
                delta, new_m, new_v = _adamw_math(w_refs[p][dst, :], g, m_refs[p][dst, :], v_refs[p][dst, :])
                outs[p][dst, :] = g
                outs[n + p][dst, :] = delta
                outs[2 * n + p][dst, :] = new_m
                outs[3 * n + p][dst, :] = new_v
            row += r

    res = pl.pallas_call(
        body,
        name=name,
        out_shape=[jax.ShapeDtypeStruct(w.shape, F32) for w in ws] * 4,
        compiler_params=pltpu.CompilerParams(vmem_limit_bytes=VMEM_LIMIT),
    )(g_packed, *ws, *ms, *vs)
    return res[:n], res[n:2 * n], res[2 * n:3 * n], res[3 * n:]


def kernel(x, c, w_ada, b_ada, w_in, w_pool, pool_scale, sgu_ln_g, sgu_ln_b, w_sgu, b_sgu, w_out, ln_g, ln_b, loss_target, m_w_ada, m_b_ada, m_w_in, m_w_pool, m_pool_scale, m_sgu_ln_g, m_sgu_ln_b, m_w_sgu, m_b_sgu, m_w_out, m_ln_g, m_ln_b, v_w_ada, v_b_ada, v_w_in, v_w_pool, v_pool_scale, v_sgu_ln_g, v_sgu_ln_b, v_w_sgu, v_b_sgu, v_w_out, v_ln_g, v_ln_b):
    mine = _index(_me())
    small_w = dict(w_pool=w_pool, w_sgu=w_sgu, pool_scale=pool_scale, sgu_ln_g=sgu_ln_g, sgu_ln_b=sgu_ln_b,
                   b_sgu=b_sgu, ln_g=ln_g, ln_b=ln_b)
    small_m = dict(w_pool=m_w_pool, w_sgu=m_w_sgu, pool_scale=m_pool_scale, sgu_ln_g=m_sgu_ln_g,
                   sgu_ln_b=m_sgu_ln_b, b_sgu=m_b_sgu, ln_g=m_ln_g, ln_b=m_ln_b)
    small_v = dict(w_pool=v_w_pool, w_sgu=v_w_sgu, pool_scale=v_pool_scale, sgu_ln_g=v_sgu_ln_g,
                   sgu_ln_b=v_sgu_ln_b, b_sgu=v_b_sgu, ln_g=v_ln_g, ln_b=v_ln_b)

    wint_loc = jnp.transpose(w_in, (0, 2, 1)).astype(BF16)
    wout_loc = w_out.astype(BF16)
    act_slots, mod, wint0, wout0 = _prep(c, w_ada, b_ada, [wint_loc[0], wout_loc[0]])
    act_all = act_slots[:, 0, :]
    w_int, w_outf = [wint0.reshape(D_IN, D_MODEL)], [wout0.reshape(D_MODEL, D_MODEL)]

    def layer_args(l):
        return (w_int[l], w_outf[l], w_pool[l], pool_scale[l].reshape(1, D_POOL), sgu_ln_g[l].reshape(1, D_SGU),
                sgu_ln_b[l].reshape(1, D_SGU), w_sgu[l], jnp.transpose(b_sgu[l]), ln_g[l].reshape(1, D_MODEL))

    acts, cur = [], x[0]
    for l in range(DEPTH):
        nxt = [wint_loc[l + 1], wout_loc[l + 1]] if l + 1 < DEPTH else []
        out, proj, y, *gathered = _layer_forward(cur, mod[l:l + 1], *layer_args(l), ln_b[l].reshape(1, D_MODEL),
                                                 f"layer_fwd_{l}", gather=nxt)
        if gathered:
            w_int.append(gathered[0].reshape(D_IN, D_MODEL))
            w_outf.append(gathered[1].reshape(D_MODEL, D_MODEL))
        acts.append((cur, proj, y))
        cur = out

    shard_in, shard_out = D_IN // N_DEV, D_MODEL // N_DEV
    a, b = cur, loss_target[0]
    smalls, dmods, loss_lanes = [None] * DEPTH, [None] * DEPTH, None
    g_w_in_t, g_w_out = [None] * DEPTH, [None] * DEPTH
    pending = []
    for l in reversed(range(DEPTH)):
        xin, proj, y = acts[l]
        dx, dproj, h, cat, dy, small, dmod, lanes, *shards = _layer_backward(
            a, b, xin, proj, y, mod[l:l + 1], *layer_args(l), l == DEPTH - 1, f"layer_bwd_{l}", reduce=pending)
        if shards:
            g_w_in_t[l + 1], g_w_out[l + 1] = shards
        if l == DEPTH - 1:
            loss_lanes = lanes
        pending = [_grad_matmul(dproj, h, 640, f"grad_w_in_{l}").reshape(4, 2, shard_in, D_MODEL),
                   _grad_matmul(cat, dy, 512, f"grad_w_out_{l}").reshape(4, 2, shard_out, D_MODEL)]
        smalls[l], dmods[l] = small, dmod
        a = b = dx
    grad_x = a[None]

    g_w_in_t[0], g_w_out[0], _, small_tot, dmod_slots, loss_tile = _tail_reduce(
        pending, jnp.concatenate(smalls, axis=0).reshape(4, 2, DEPTH * PACK_ROWS // N_DEV, 128),
        jnp.concatenate(dmods, axis=0).reshape(8, 3 * D_MODEL * DEPTH // 8), loss_lanes)
    loss = loss_tile[0, 0]
    dmod_all = dmod_slots.reshape(N_DEV, DEPTH, 3 * D_MODEL)
    gwin_t = jnp.stack(g_w_in_t)
    gwout = jnp.stack(g_w_out)

    cols_ada = w_ada.shape[2]
    dmod_cols = jnp.transpose(lax.dynamic_slice_in_dim(dmod_all, mine * cols_ada, cols_ada, axis=2), (1, 0, 2))
    g_w_ada, d_w_ada, nm_w_ada, nv_w_ada = _adamw_ada(w_ada, m_w_ada, v_w_ada, jnp.transpose(act_all), dmod_cols,
                                                      "adamw_w_ada")
    g_b_ada, d_b_ada, nm_b_ada, nv_b_ada = _adamw_bias(b_ada, m_b_ada, v_b_ada, dmod_all, "adamw_b_ada")
    flat = lambda t: t.reshape(-1, t.shape[-1])
    to_t = lambda t: flat(jnp.transpose(t, (0, 2, 1)))
    from_t = lambda t: jnp.transpose(t.reshape(DEPTH, shard_in, D_MODEL), (0, 2, 1))
    d_w_in, nm_w_in, nv_w_in = [from_t(t) for t in
                                _adamw(to_t(w_in), flat(gwin_t), to_t(m_w_in), to_t(v_w_in), 128, "adamw_w_in")]
    g_w_in = from_t(gwin_t)
    d_w_out, nm_w_out, nv_w_out = [t.reshape(w_out.shape) for t in
                                   _adamw(flat(w_out), flat(gwout), flat(m_w_out), flat(v_w_out), 128, "adamw_w_out")]
    rows128 = lambda t: t.reshape(-1, 128)
    small_out = _adamw_small(small_tot.reshape(DEPTH * PACK_ROWS, 128), [rows128(small_w[n]) for n in SMALL_NAMES],
                             [rows128(small_m[n]) for n in SMALL_NAMES], [rows128(small_v[n]) for n in SMALL_NAMES],
                             "adamw_small")
    gs, ds, ms, vs = [{n: t.reshape(small_w[n].shape) for n, t in zip(SMALL_NAMES, group)} for group in small_out]

    def ordered(w_ada_, b_ada_, w_in_, small, w_out_):
        return (w_ada_, b_ada_, w_in_, small["w_pool"], small["pool_scale"], small["sgu_ln_g"], small["sgu_ln_b"],
                small["w_sgu"], small["b_sgu"], w_out_, small["ln_g"], small["ln_b"])

    return (loss, grad_x,
            *ordered(g_w_ada, g_b_ada, g_w_in, gs, gwout),
            *ordered(d_w_ada, d_b_ada, d_w_in, ds, d_w_out),
            *ordered(nm_w_ada, nm_b_ada, nm_w_in, ms, nm_w_out),
            *ordered(nv_w_ada, nv_b_ada, nv_w_in, vs, nv_w_out))
```

```python
import jax
import jax.numpy as jnp
from jax import lax
from jax.experimental import pallas as pl
from jax.experimental.pallas import tpu as pltpu

F32 = jnp.float32
BF16 = jnp.bfloat16

D_MODEL = 1024
SEQ = 2048
DEPTH = 2
D_POOL = 512
D_SGU = 512
D_IN = 2560
N_GROUPS = 4
GROUP = 128
N_HEADS = 4
HEAD = 128
CHUNK = 128
WINDOWS = (2, 4, 8, 16)
ALPHA = (2.0 * DEPTH) ** 0.25
LN_EPS = 1e-5
N_DEV = 8

ADAM_LR = 0.001
ADAM_B1 = 0.9
ADAM_B2 = 0.999
ADAM_EPS = 1e-08
ADAM_WD = 0.01
ADAM_STEP = 10

TM = 256
HALO = 16
N_TILES = SEQ // TM
VMEM_LIMIT = 60 * 1024 * 1024

ROW_WPOOL = 0
ROW_WSGU = 512
ROW_PSCALE = 1024
ROW_SLNG = 1028
ROW_SLNB = 1032
ROW_BSGU = 1036
ROW_LNG = 1040
ROW_LNB = 1048
PACK_ROWS = 1088

SQRT_HALF = 0.7071067811865476
INV_SQRT_2PI = 0.3989422804014327


def _ln(x):
    mu = jnp.mean(x, axis=-1, keepdims=True)
    xc = x - mu
    var = jnp.mean(xc * xc, axis=-1, keepdims=True)
    rstd = lax.rsqrt(var + LN_EPS)
    return xc * rstd, rstd


def _ln_bwd(dxn, xn, rstd):
    m1 = jnp.mean(dxn, axis=-1, keepdims=True)
    m2 = jnp.mean(dxn * xn, axis=-1, keepdims=True)
    return rstd * (dxn - m1 - xn * m2)


def _gelu_parts(x, with_grad):
    cdf = 0.5 * (1.0 + lax.erf(x * SQRT_HALF))
    if not with_grad:
        return x * cdf, None
    return x * cdf, cdf + x * (INV_SQRT_2PI * jnp.exp(-0.5 * x * x))


def _silu_parts(x):
    s = jax.nn.sigmoid(x)
    return x * s, s * (1.0 + x * (1.0 - s))


def _dot(a, b):
    return lax.dot_general(a, b, (((1,), (0,)), ((), ())), preferred_element_type=F32)


def _dot_nt(a, b):
    return lax.dot_general(a, b, (((1,), (1,)), ((), ())), preferred_element_type=F32)


def _dot_tn(a, b):
    return lax.dot_general(a, b, (((0,), (0,)), ((), ())), preferred_element_type=F32)


def _row_index(tile):
    return tile * TM + lax.broadcasted_iota(jnp.int32, (TM, 1), 0)


def _window_sums(ext, forward):
    n = TM + HALO
    cur = ext
    outs = []
    for g in range(N_GROUPS):
        step = 1 << g
        cur = cur + pltpu.roll(cur, step if forward else n - step, 0)
        rows = cur[HALO:, :GROUP] if forward else cur[:TM, :GROUP]
        outs.append(rows)
        cur = cur[:, GROUP:] if g + 1 < N_GROUPS else None
    return outs


def _inverse_counts(rows):
    return [1.0 / jnp.minimum(rows + 1, w).astype(F32) for w in WINDOWS]


def _tril_bf16(w):
    t = lax.broadcasted_iota(jnp.int32, (CHUNK, CHUNK), 0)
    s = lax.broadcasted_iota(jnp.int32, (CHUNK, CHUNK), 1)
    return jnp.where(t >= s, w, 0.0).astype(BF16)


def _mix_forward(proj, halo, tile, wpool_ref, pscale, slng, slnb, wsgu_ref, bsgut_ref, keep):
    rows = _row_index(tile)
    inv_counts = _inverse_counts(rows)
    xa = proj[:, 0:D_POOL]
    ga = proj[:, D_POOL:2 * D_POOL]
    sums = _window_sums(jnp.concatenate([halo, xa], axis=0), True)
    ga_act, ga_grad = _silu_parts(ga)
    pooled, pw, ya = [], [], []
    for g in range(N_GROUPS):
        sl = slice(g * GROUP, (g + 1) * GROUP)
        p = (sums[g] * inv_counts[g] - xa[:, sl]).astype(BF16)
        q = _dot(p, wpool_ref[g].astype(BF16))
        pooled.append(p)
        pw.append(q)
        ya.append(q * pscale[:, sl] * ga_act[:, sl])

    u = proj[:, 2 * D_POOL:2 * D_POOL + D_SGU]
    v = proj[:, 2 * D_POOL + D_SGU:2 * D_POOL + 2 * D_SGU]
    gb = proj[:, 2 * D_POOL + 2 * D_SGU:]
    gb_act, gb_grad = _silu_parts(gb)
    u_act, u_grad = _gelu_parts(u, keep)
    v_act, v_grad = _gelu_parts(v, keep)
    vn, vrstd, vln, mixed, yb = [], [], [], [], []
    for h in range(N_HEADS):
        sl = slice(h * HEAD, (h + 1) * HEAD)
        n_h, r_h = _ln(v_act[:, sl])
        l_h = (n_h * slng[:, sl] + slnb[:, sl]).astype(BF16)
        w_h = _tril_bf16(wsgu_ref[h])
        bias = bsgut_ref[:, h:h + 1]
        m_h = jnp.concatenate(
            [_dot(w_h, l_h[k * CHUNK:(k + 1) * CHUNK]) + bias for k in range(TM // CHUNK)], axis=0)
        vn.append(n_h)
        vrstd.append(r_h)
        vln.append(l_h)
        mixed.append(m_h)
        yb.append(u_act[:, sl] * m_h * gb_act[:, sl])
    cat = jnp.concatenate(ya + yb, axis=1)
    if not keep:
        return cat, None
    return cat, dict(inv_counts=inv_counts, ga_act=ga_act, ga_grad=ga_grad, pooled=pooled, pw=pw, u_grad=u_grad,
                     v_grad=v_grad, u_act=u_act, gb_act=gb_act, gb_grad=gb_grad, vn=vn, vrstd=vrstd, vln=vln,
                     mixed=mixed)


def _const_spec(shape):
    nd = len(shape)
    return pl.BlockSpec(shape, lambda i: (0,) * nd)


def _layer_forward(x, mod_l, w_int, w_outf, w_pool, pscale, slng, slnb, w_sgu, bsgut, ln_g, ln_b, name, gather=()):
    n_gather = len(gather)

    def body(x_ref, mod_ref, wint_ref, wout_ref, wpool_ref, pscale_ref, slng_ref, slnb_ref, wsgu_ref, bsgut_ref,
             lng_ref, lnb_ref, *rest):
        loc_refs, rest = rest[:n_gather], rest[n_gather:]
        out_ref, proj_ref, y_ref = rest[:3]
        full_refs, rest = rest[3:3 + n_gather], rest[3 + n_gather:]
        halo_ref = rest[0]
        tile = pl.program_id(0)

        def gathers():
            g_send, g_recv, _ = rest[1:]
            return [_TwoLevelGather(full_refs[n], g_send.at[n], g_recv.at[n], src=loc_refs[n])
                    for n in range(n_gather)]

        def own_copies():
            mine = _index(_me())
            return [pltpu.make_async_copy(loc_refs[n], full_refs[n].at[mine], rest[3].at[n]) for n in range(n_gather)]

        @pl.when(tile == 0)
        def _():
            halo_ref[...] = jnp.zeros_like(halo_ref)
            if n_gather:
                for cp in own_copies():
                    cp.start()
                for g in gathers():
                    g.send_mine()

        if n_gather:
            @pl.when(tile == N_TILES // 2)
            def _():
                for g in gathers():
                    g.relay()

        xt = x_ref[...]
        shift = mod_ref[:, 0:D_MODEL]
        scale = mod_ref[:, D_MODEL:2 * D_MODEL]
        gate = mod_ref[:, 2 * D_MODEL:]
        xn, _ = _ln(xt)
        h = (xn * (1.0 + scale) + shift).astype(BF16)
        proj = _dot_nt(h, wint_ref[...])
        proj_ref[...] = proj
        cat, _ = _mix_forward(proj, halo_ref[...], tile, wpool_ref, pscale_ref[...], slng_ref[...], slnb_ref[...],
                              wsgu_ref, bsgut_ref, False)
        halo_ref[...] = proj[TM - HALO:, 0:D_POOL]
        y = _dot(cat.astype(BF16), wout_ref[...])
        y_ref[...] = y
        zn, _ = _ln(ALPHA * xt + gate * y)
        out_ref[...] = zn * lng_ref[...] + lnb_ref[...]

        if n_gather:
            @pl.when(tile == N_TILES - 1)
            def _():
                for g in gathers():
                    g.pass_on()
                for g in gathers():
                    g.wait_rest()
                for g in gathers():
                    g.wait_sends()
                for cp in own_copies():
                    cp.wait()

    row = lambda w: pl.BlockSpec((TM, w), lambda i: (i, 0))
    comm_scratch = [pltpu.SemaphoreType.DMA((n_gather, GATHER_SEMS)), pltpu.SemaphoreType.DMA((n_gather, GATHER_SEMS)),
                    pltpu.SemaphoreType.DMA((n_gather,))] if n_gather else []
    return pl.pallas_call(
        body,
        name=name,
        grid=(N_TILES,),
        in_specs=[row(D_MODEL), _const_spec((1, 3 * D_MODEL)), _const_spec((D_IN, D_MODEL)),
                  _const_spec((D_MODEL, D_MODEL)), _const_spec((N_GROUPS, GROUP, GROUP)), _const_spec((1, D_POOL)),
                  _const_spec((1, D_SGU)), _const_spec((1, D_SGU)), _const_spec((N_HEADS, CHUNK, CHUNK)),
                  _const_spec((CHUNK, N_HEADS)), _const_spec((1, D_MODEL)), _const_spec((1, D_MODEL))]
                 + [ANY] * n_gather,
        out_specs=[row(D_MODEL), row(D_IN), row(D_MODEL)] + [ANY] * n_gather,
        out_shape=[jax.ShapeDtypeStruct((SEQ, D_MODEL), F32), jax.ShapeDtypeStruct((SEQ, D_IN), F32),
                   jax.ShapeDtypeStruct((SEQ, D_MODEL), F32)]
                  + [jax.ShapeDtypeStruct((N_DEV,) + g.shape, g.dtype) for g in gather],
        scratch_shapes=[pltpu.VMEM((HALO, D_POOL), F32)] + comm_scratch,
        compiler_params=pltpu.CompilerParams(dimension_semantics=("arbitrary",), vmem_limit_bytes=VMEM_LIMIT),
    )(x, mod_l, w_int, w_outf, w_pool, pscale, slng, slnb, w_sgu, bsgut, ln_g, ln_b, *gather)


VEC_LNG, VEC_LNB, VEC_POOL, VEC_SGU, VEC_SHIFT, VEC_SCALE, VEC_GATE, VEC_LOSS = range(8)


def _layer_backward(a, b, x, proj, y, mod_l, w_int, w_outf, w_pool, pscale, slng, slnb, w_sgu, bsgut, ln_g, is_last,
                    name, reduce=()):
    n_red = len(reduce)

    def body(a_ref, b_ref, x_ref, proj_ref, prev_ref, y_ref, mod_ref, wint_ref, wout_ref, wpool_ref, pscale_ref,
             slng_ref, slnb_ref, wsgu_ref, bsgut_ref, lng_ref, *rest):
        part_refs, rest = rest[:n_red], rest[n_red:]
        dx_ref, dproj_ref, h_ref, cat_ref, dy_ref, small_ref, dmod_ref, loss_ref = rest[:8]
        shard_refs, rest = rest[8:8 + n_red], rest[8 + n_red:]
        vec_ref, dmix_ref, halo_ref = rest[:3]
        step = pl.program_id(0)
        tile = N_TILES - 1 - step

        def scatter():
            bufs, sems = rest[3:3 + 4 * n_red], rest[3 + 4 * n_red:]
            arrays = [dict(part=part_refs[n], out=shard_refs[n], stage=bufs[4 * n], sib=bufs[4 * n + 1],
                           snd=bufs[4 * n + 2], rcv=bufs[4 * n + 3]) for n in range(n_red)]
            return _ChipReduceScatter(arrays, *sems)

        @pl.when(step == 0)
        def _():
            small_ref[...] = jnp.zeros_like(small_ref)
            vec_ref[...] = jnp.zeros_like(vec_ref)
            dmix_ref[...] = jnp.zeros_like(dmix_ref)
            halo_ref[...] = jnp.zeros_like(halo_ref)
            if n_red:
                scatter().start()

        if n_red:
            @pl.when(step == 1)
            def _():
                scatter().exchange()

        def acc(row, lo, val):
            hi = lo + val.shape[1]
            vec_ref[row:row + 1, lo:hi] += jnp.sum(val, axis=0, keepdims=True)

        xt = x_ref[...]
        yt = y_ref[...]
        shift = mod_ref[:, 0:D_MODEL]
        scale = mod_ref[:, D_MODEL:2 * D_MODEL]
        gate = mod_ref[:, 2 * D_MODEL:]

        zn, zrstd = _ln(ALPHA * xt + gate * yt)
        if is_last:
            diff = a_ref[...] - b_ref[...]
            acc(VEC_LOSS, 0, diff * diff)
            dout = diff * (1.0 / D_MODEL)
        else:
            dout = a_ref[...]
        acc(VEC_LNG, 0, dout * zn)
        acc(VEC_LNB, 0, dout)
        dz = _ln_bwd(dout * lng_ref[...], zn, zrstd)
        acc(VEC_GATE, 0, dz * yt)
        dy = (dz * gate).astype(BF16)
        dy_ref[...] = dy
        dcat = _dot_nt(dy, wout_ref[...])

        proj = proj_ref[...]
        prev = jnp.where(tile > 0, prev_ref[...], 0.0)
        cat, k = _mix_forward(proj, prev, tile, wpool_ref, pscale_ref[...], slng_ref[...], slnb_ref[...],
                              wsgu_ref, bsgut_ref, True)
        cat_ref[...] = cat.astype(BF16)
        pscale = pscale_ref[...]
        slng = slng_ref[...]

        dga, dq = [], []
        for g in range(N_GROUPS):
            sl = slice(g * GROUP, (g + 1) * GROUP)
            dya = dcat[:, sl]
            dyp = dya * k["ga_act"][:, sl]
            dga.append(dya * k["pw"][g] * pscale[:, sl] * k["ga_grad"][:, sl])
            acc(VEC_POOL, g * GROUP, dyp * k["pw"][g])
            dpw = (dyp * pscale[:, sl]).astype(BF16)
            small_ref[ROW_WPOOL + g * GROUP:ROW_WPOOL + (g + 1) * GROUP, :] += _dot_tn(k["pooled"][g], dpw)
            dq.append(_dot_nt(dpw, wpool_ref[g].astype(BF16)))
        dpooled = jnp.concatenate(dq, axis=1)
        scaled = jnp.concatenate([dq[g] * k["inv_counts"][g] for g in range(N_GROUPS)], axis=1)
        sums = _window_sums(jnp.concatenate([scaled, halo_ref[...]], axis=0), False)
        halo_ref[...] = scaled[0:HALO]
        dxa = jnp.concatenate(sums, axis=1) - dpooled

        du, dv, dgb = [], [], []
        for h in range(N_HEADS):
            sl = slice(h * HEAD, (h + 1) * HEAD)
            dyb = dcat[:, D_POOL + h * HEAD:D_POOL + (h + 1) * HEAD]
            m_h = k["mixed"][h]
            ug = k["u_act"][:, sl] * dyb
            du.append(dyb * m_h * k["gb_act"][:, sl] * k["u_grad"][:, sl])
            dgb.append(ug * m_h * k["gb_grad"][:, sl])
            dmixed = ug * k["gb_act"][:, sl]
            dmixed_bf = dmixed.astype(BF16)
            w_h = _tril_bf16(wsgu_ref[h])
            dvln_parts = []
            dmix_sum = dmix_ref[h]
            dws = small_ref[ROW_WSGU + h * CHUNK:ROW_WSGU + (h + 1) * CHUNK, :]
            for c in range(TM // CHUNK):
                cs = slice(c * CHUNK, (c + 1) * CHUNK)
                dmix_sum = dmix_sum + dmixed[cs]
                dws = dws + _dot_nt(dmixed_bf[cs], k["vln"][h][cs])
                dvln_parts.append(_dot_tn(w_h, dmixed_bf[cs]))
            dmix_ref[h] = dmix_sum
            small_ref[ROW_WSGU + h * CHUNK:ROW_WSGU + (h + 1) * CHUNK, :] = dws
            dvln = jnp.concatenate(dvln_parts, axis=0)
            acc(VEC_SGU, h * HEAD, dvln * k["vn"][h])
            acc(VEC_SGU, D_SGU + h * HEAD, dvln)
            dvv = _ln_bwd(dvln * slng[:, sl], k["vn"][h], k["vrstd"][h])
            dv.append(dvv * k["v_grad"][:, sl])

        dproj = jnp.concatenate([dxa] + dga + du + dv + dgb, axis=1).astype(BF16)
        dproj_ref[...] = dproj
        dh = _dot(dproj, wint_ref[...])

        xn, xrstd = _ln(xt)
        h_ref[...] = (xn * (1.0 + scale) + shift).astype(BF16)
        acc(VEC_SCALE, 0, dh * xn)
        acc(VEC_SHIFT, 0, dh)
        dx_ref[...] = _ln_bwd(dh * (1.0 + scale), xn, xrstd) + ALPHA * dz

        @pl.when(step == N_TILES - 1)
        def _():
            def put(row0, vec_row, lo, n):
                for r in range(n):
                    small_ref[row0 + r:row0 + r + 1, :] = vec_ref[vec_row:vec_row + 1, lo + r * 128:lo + (r + 1) * 128]

            put(ROW_PSCALE, VEC_POOL, 0, 4)
            put(ROW_SLNG, VEC_SGU, 0, 4)
            put(ROW_SLNB, VEC_SGU, D_SGU, 4)
            put(ROW_LNG, VEC_LNG, 0, 8)
            put(ROW_LNB, VEC_LNB, 0, 8)
            ones = jnp.ones((8, HEAD), F32)
            t = lax.broadcasted_iota(jnp.int32, (CHUNK, CHUNK), 0)
            s = lax.broadcasted_iota(jnp.int32, (CHUNK, CHUNK), 1)
            for h in range(N_HEADS):
                bias_rows = lax.dot_general(ones, dmix_ref[h], (((1,), (1,)), ((), ())),
                                            preferred_element_type=F32, precision=lax.Precision.HIGHEST)
                small_ref[ROW_BSGU + h:ROW_BSGU + h + 1, :] = bias_rows[0:1]
                blk = small_ref[ROW_WSGU + h * CHUNK:ROW_WSGU + (h + 1) * CHUNK, :]
                small_ref[ROW_WSGU + h * CHUNK:ROW_WSGU + (h + 1) * CHUNK, :] = jnp.where(t >= s, blk, 0.0)
            dmod_ref[:, 0:D_MODEL] = vec_ref[VEC_SHIFT:VEC_SHIFT + 1, :]
            dmod_ref[:, D_MODEL:2 * D_MODEL] = vec_ref[VEC_SCALE:VEC_SCALE + 1, :]
            dmod_ref[:, 2 * D_MODEL:] = vec_ref[VEC_GATE:VEC_GATE + 1, :]
            loss_ref[...] = vec_ref[VEC_LOSS:VEC_LOSS + 1, :]
            if n_red:
                scatter().finish()
                scatter().wait_sends()

    rev = lambda w: pl.BlockSpec((TM, w), lambda i: (N_TILES - 1 - i, 0))
    prev_spec = pl.BlockSpec(
        (HALO, D_POOL), lambda i: (jnp.maximum((N_TILES - 1 - i) * (TM // HALO) - 1, 0), 0))
    comm_scratch = []
    for p in reduce:
        comm_scratch += _ChipReduceScatter.buffers(p.shape[2], p.shape[3], p.dtype)
    if n_red:
        comm_scratch += _ChipReduceScatter.semaphores(n_red)
    return pl.pallas_call(
        body,
        name=name,
        grid=(N_TILES,),
        in_specs=[rev(D_MODEL), rev(D_MODEL) if is_last else pl.BlockSpec((TM, D_MODEL), lambda i: (0, 0)),
                  rev(D_MODEL), rev(D_IN), prev_spec, rev(D_MODEL),
                  _const_spec((1, 3 * D_MODEL)), _const_spec((D_IN, D_MODEL)), _const_spec((D_MODEL, D_MODEL)),
                  _const_spec((N_GROUPS, GROUP, GROUP)), _const_spec((1, D_POOL)), _const_spec((1, D_SGU)),
                  _const_spec((1, D_SGU)), _const_spec((N_HEADS, CHUNK, CHUNK)), _const_spec((CHUNK, N_HEADS)),
                  _const_spec((1, D_MODEL))] + [ANY] * n_red,
        out_specs=[rev(D_MODEL), rev(D_IN), rev(D_MODEL), rev(D_MODEL), rev(D_MODEL),
                   _const_spec((PACK_ROWS, 128)), _const_spec((1, 3 * D_MODEL)), _const_spec((1, D_MODEL))]
                  + [_const_spec(p.shape[2:]) for p in reduce],
        out_shape=[jax.ShapeDtypeStruct((SEQ, D_MODEL), F32), jax.ShapeDtypeStruct((SEQ, D_IN), BF16),
                   jax.ShapeDtypeStruct((SEQ, D_MODEL), BF16), jax.ShapeDtypeStruct((SEQ, D_MODEL), BF16),
                   jax.ShapeDtypeStruct((SEQ, D_MODEL), BF16), jax.ShapeDtypeStruct((PACK_ROWS, 128), F32),
                   jax.ShapeDtypeStruct((1, 3 * D_MODEL), F32), jax.ShapeDtypeStruct((1, D_MODEL), F32)]
                  + [jax.ShapeDtypeStruct(p.shape[2:], F32) for p in reduce],
        scratch_shapes=[pltpu.VMEM((8, D_MODEL), F32), pltpu.VMEM((N_HEADS, CHUNK, HEAD), F32),
                        pltpu.VMEM((HALO, D_POOL), F32)] + comm_scratch,
        compiler_params=pltpu.CompilerParams(dimension_semantics=("arbitrary",), vmem_limit_bytes=VMEM_LIMIT),
    )(a, b, x, proj, proj, y, mod_l, w_int, w_outf, w_pool, pscale, slng, slnb, w_sgu, bsgut, ln_g, *reduce)


def _grad_matmul(lhs, rhs, block_cols, name):
    m, n = lhs.shape[1], rhs.shape[1]

    def body(lhs_ref, rhs_ref, out_ref):
        out_ref[...] = _dot_tn(lhs_ref[...], rhs_ref[...]).astype(BF16)

    return pl.pallas_call(
        body,
        name=name,
        grid=(m // block_cols,),
        in_specs=[pl.BlockSpec((SEQ, block_cols), lambda j: (0, j)), pl.BlockSpec((SEQ, n), lambda j: (0, 0))],
        out_specs=pl.BlockSpec((block_cols, n), lambda j: (j, 0)),
        out_shape=jax.ShapeDtypeStruct((m, n), BF16),
        compiler_params=pltpu.CompilerParams(dimension_semantics=("arbitrary",), vmem_limit_bytes=VMEM_LIMIT),
    )(lhs, rhs)


def _adamw_math(w, g, m, v):
    m = ADAM_B1 * m + (1.0 - ADAM_B1) * g
    v = ADAM_B2 * v + (1.0 - ADAM_B2) * (g * g)
    m_hat = m / (1.0 - ADAM_B1 ** ADAM_STEP)
    v_hat = v / (1.0 - ADAM_B2 ** ADAM_STEP)
    delta = -ADAM_LR * (m_hat / (jnp.sqrt(v_hat) + ADAM_EPS) + ADAM_WD * w)
    return delta, m, v


def _adamw(w, g, m, v, block_rows, name):
    rows, cols = w.shape

    def body(w_ref, g_ref, m_ref, v_ref, d_ref, nm_ref, nv_ref):
        d_ref[...], nm_ref[...], nv_ref[...] = _adamw_math(w_ref[...], g_ref[...], m_ref[...], v_ref[...])

    spec = pl.BlockSpec((block_rows, cols), lambda i: (i, 0))
    return pl.pallas_call(
        body,
        name=name,
        grid=(rows // block_rows,),
        in_specs=[spec] * 4,
        out_specs=[spec] * 3,
        out_shape=[jax.ShapeDtypeStruct(w.shape, F32)] * 3,
        compiler_params=pltpu.CompilerParams(dimension_semantics=("arbitrary",), vmem_limit_bytes=VMEM_LIMIT),
    )(w, g, m, v)


def _adamw_ada(w, m, v, act_t, dmod_cols, name):
    cols = w.shape[2]

    rows = 256

    def body(w_ref, m_ref, v_ref, act_ref, dmod_ref, g_ref, d_ref, nm_ref, nv_ref):
        act = act_ref[...]
        dm = dmod_ref[0]
        g = act[:, 0:1] * dm[0:1, :]
        for b in range(1, N_DEV):
            g = g + act[:, b:b + 1] * dm[b:b + 1, :]
        g_ref[0] = g
        d_ref[0], nm_ref[0], nv_ref[0] = _adamw_math(w_ref[0], g, m_ref[0], v_ref[0])

    spec = pl.BlockSpec((1, rows, cols), lambda l, i: (l, i, 0))
    return pl.pallas_call(
        body,
        name=name,
        grid=(DEPTH, D_MODEL // rows),
        in_specs=[spec, spec, spec, pl.BlockSpec((rows, N_DEV), lambda l, i: (i, 0)),
                  pl.BlockSpec((1, N_DEV, cols), lambda l, i: (l, 0, 0))],
        out_specs=[spec] * 4,
        out_shape=[jax.ShapeDtypeStruct(w.shape, F32)] * 4,
        compiler_params=pltpu.CompilerParams(dimension_semantics=("arbitrary", "arbitrary"),
                                             vmem_limit_bytes=VMEM_LIMIT),
    )(w, m, v, act_t, dmod_cols)


def _adamw_bias(w, m, v, dmod_all, name):
    def body(w_ref, m_ref, v_ref, dmod_ref, g_ref, d_ref, nm_ref, nv_ref):
        g = dmod_ref[0]
        for b in range(1, N_DEV):
            g = g + dmod_ref[b]
        g_ref[...] = g
        d_ref[...], nm_ref[...], nv_ref[...] = _adamw_math(w_ref[...], g, m_ref[...], v_ref[...])

    return pl.pallas_call(
        body,
        name=name,
        out_shape=[jax.ShapeDtypeStruct(w.shape, F32)] * 4,
        compiler_params=pltpu.CompilerParams(vmem_limit_bytes=VMEM_LIMIT),
    )(w, m, v, dmod_all)


MESH = pl.DeviceIdType.MESH
SIBLING = 1
CHIP_RELATIONS = (2, 4, 6)
ANY = pl.BlockSpec(memory_space=pl.ANY)
VMEM = pl.BlockSpec(memory_space=pltpu.VMEM)


def _me():
    return lax.axis_index("x"), lax.axis_index("y"), lax.axis_index("c")


def _peer(r):
    x, y, c = _me()
    return (1 - x if r & 4 else x, 1 - y if r & 2 else y, 1 - c if r & 1 else c)


def _index(dev):
    return 4 * dev[0] + 2 * dev[1] + dev[2]


def _remote(src, dst, send_sem, recv_sem, dev):
    return pltpu.make_async_remote_copy(src_ref=src, dst_ref=dst, send_sem=send_sem, recv_sem=recv_sem,
                                        device_id=dev, device_id_type=MESH)


ACROSS_X, ACROSS_Y, ACROSS_BOTH = 4, 2, 6
GATHER_SEMS = 11


class _TwoLevelGather:
    def __init__(self, out, send_sems, recv_sems, src=None):
        self.out, self.send_sems, self.recv_sems, self.src = out, send_sems, recv_sems, src
        self.half = out.shape[1] // 2

    def _copy(self, k, block, part, to, src=None):
        slot = self.out.at[_index(block)]
        if part is not None:
            rows = pl.ds(part * self.half, self.half)
            slot = slot.at[rows]
            src = None if src is None else src.at[rows]
        return _remote(slot if src is None else src, slot, self.send_sems.at[k], self.recv_sems.at[k], to)

    def _mine(self):
        me = _me()
        src = self.out.at[_index(me)] if self.src is None else self.src
        x, y = _peer(ACROSS_X), _peer(ACROSS_Y)
        return [self._copy(1, me, 0, x, src), self._copy(3, me, 1, y, src), self._copy(2, me, 1, x, src),
                self._copy(4, me, 0, y, src), self._copy(0, me, None, _peer(SIBLING), src)]

    def _relayed(self):
        return [self._copy(5, _peer(ACROSS_X), 0, _peer(ACROSS_Y)), self._copy(6, _peer(ACROSS_Y), 1, _peer(ACROSS_X))]

    def _passed(self):
        sib, far = _peer(SIBLING), _peer(ACROSS_BOTH)
        return [self._copy(7, _peer(ACROSS_X), None, sib), self._copy(8, _peer(ACROSS_Y), None, sib),
                self._copy(9, far, 0, sib), self._copy(10, far, 1, sib)]

    def _arrival(self, k, r, part):
        return self._copy(k, _peer(r), part, _me())

    def send_mine(self):
        for cp in self._mine():
            cp.start()

    def relay(self):
        relayed = self._relayed()
        self._arrival(1, ACROSS_X, 0).wait_recv()
        relayed[0].start()
        self._arrival(3, ACROSS_Y, 1).wait_recv()
        relayed[1].start()

    def pass_on(self):
        passed = self._passed()
        self._arrival(2, ACROSS_X, 1).wait_recv()
        passed[0].start()
        self._arrival(4, ACROSS_Y, 0).wait_recv()
        passed[1].start()
        self._arrival(5, ACROSS_BOTH, 0).wait_recv()
        passed[2].start()
        self._arrival(6, ACROSS_BOTH, 1).wait_recv()
        passed[3].start()

    def wait_rest(self):
        self._arrival(0, SIBLING, None).wait_recv()
        self._arrival(7, ACROSS_X ^ SIBLING, None).wait_recv()
        self._arrival(8, ACROSS_Y ^ SIBLING, None).wait_recv()
        self._arrival(9, ACROSS_BOTH ^ SIBLING, 0).wait_recv()
        self._arrival(10, ACROSS_BOTH ^ SIBLING, 1).wait_recv()

    def wait_sends(self):
        for cp in self._mine() + self._relayed() + self._passed():
            cp.wait_send()


class _ChipReduceScatter:
    def __init__(self, arrays, l_sem, d_send, d_recv, i_send, i_recv):
        self.arrays = arrays
        self.l_sem, self.d_send, self.d_recv, self.i_send, self.i_recv = l_sem, d_send, d_recv, i_send, i_recv

    @staticmethod
    def buffers(rows, cols, dtype):
        n_rel = len(CHIP_RELATIONS)
        return [pltpu.VMEM((4, rows, cols), dtype), pltpu.VMEM((4, rows, cols), dtype),
                pltpu.VMEM((n_rel, rows, cols), dtype), pltpu.VMEM((n_rel, rows, cols), dtype)]

    @staticmethod
    def semaphores(n):
        n_rel = len(CHIP_RELATIONS)
        return [pltpu.SemaphoreType.DMA((n,)), pltpu.SemaphoreType.DMA((n,)), pltpu.SemaphoreType.DMA((n,)),
                pltpu.SemaphoreType.DMA((n_rel, n)), pltpu.SemaphoreType.DMA((n_rel, n))]

    def _staging(self):
        c = _me()[2]
        return [pltpu.make_async_copy(a["part"].at[pl.ds(0, 4), c], a["stage"], self.l_sem.at[n])
                for n, a in enumerate(self.arrays)]

    def _first(self):
        other = 1 - _me()[2]
        return [_remote(a["part"].at[pl.ds(0, 4), other], a["sib"], self.d_send.at[n], self.d_recv.at[n],
                        _peer(SIBLING)) for n, a in enumerate(self.arrays)]

    def _second(self, j):
        return [_remote(a["snd"].at[j], a["rcv"].at[j], self.i_send.at[j, n], self.i_recv.at[j, n],
                        _peer(CHIP_RELATIONS[j])) for n, a in enumerate(self.arrays)]

    def start(self):
        for cp in self._staging() + self._first():
            cp.start()

    def exchange(self):
        for cp in self._staging():
            cp.wait()
        for cp in self._first():
            cp.wait_recv()
        for j, r in enumerate(CHIP_RELATIONS):
            peer = _peer(r)
            chip = 2 * peer[0] + peer[1]
            for a in self.arrays:
                a["snd"][j] = (a["stage"][chip].astype(F32) + a["sib"][chip].astype(F32)).astype(a["snd"].dtype)
            for cp in self._second(j):
                cp.start()

    def finish(self):
        x, y, _ = _me()
        home = 2 * x + y
        for a in self.arrays:
            a["out"][...] = a["stage"][home].astype(F32) + a["sib"][home].astype(F32)
        for j in range(len(CHIP_RELATIONS)):
            for cp in self._second(j):
                cp.wait_recv()
            for a in self.arrays:
                a["out"][...] += a["rcv"][j].astype(F32)

    def wait_sends(self):
        for cp in self._first():
            cp.wait_send()
        for j in range(len(CHIP_RELATIONS)):
            for cp in self._second(j):
                cp.wait_send()


def _direct_exchange(src_of, dst_of, send_sems, recv_sems):
    me = _me()
    copies = [_remote(src_of(_peer(r)), dst_of(me), send_sems.at[r - 1], recv_sems.at[r - 1], _peer(r))
              for r in range(1, N_DEV)]
    for cp in copies:
        cp.start()
    return copies


def _wait_direct(copies):
    for cp in copies:
        cp.wait_recv()
    for cp in copies:
        cp.wait_send()


def _prep(c_row, w_ada, b_ada, blocks):
    cols = w_ada.shape[2]
    n_blocks = len(blocks)

    def body(c_ref, wada_ref, bada_ref, *rest):
        loc_refs, rest = rest[:n_blocks], rest[n_blocks:]
        act_all, mod_ref = rest[:2]
        full_refs, rest = rest[2:2 + n_blocks], rest[2 + n_blocks:]
        act_src, part, mod_recv, w_send, w_recv, w_local, a_send, a_recv, m_send, m_recv = rest
        me = _me()
        mine = _index(me)

        cval = c_ref[...]
        act_src[...] = jnp.zeros_like(act_src)
        act_src[0:1, :] = cval * jax.nn.sigmoid(cval)
        act_all[mine] = act_src[...]
        act_copies = _direct_exchange(lambda p: act_src, lambda m: act_all.at[_index(m)], a_send, a_recv)

        gathers, locals_ = [], []
        for n in range(n_blocks):
            own = pltpu.make_async_copy(loc_refs[n], full_refs[n].at[mine], w_local.at[n])
            own.start()
            locals_.append(own)
            g = _TwoLevelGather(full_refs[n], w_send.at[n], w_recv.at[n], src=loc_refs[n])
            g.send_mine()
            gathers.append(g)

        _wait_direct(act_copies)
        acts = jnp.concatenate([act_all[j, 0:1, :] for j in range(N_DEV)], axis=0)
        part[...] = jnp.zeros_like(part)
        for layer in range(DEPTH):
            res = lax.dot_general(acts, wada_ref[layer], (((1,), (0,)), ((), ())), preferred_element_type=F32,
                                  precision=lax.Precision.HIGHEST)
            for b in range(N_DEV):
                part[b, layer:layer + 1, :] = res[b:b + 1, :]
        mod_recv[mine] = part[mine]
        mod_copies = _direct_exchange(lambda p: part.at[_index(p)], lambda m: mod_recv.at[_index(m)], m_send, m_recv)

        for g in gathers:
            g.relay()
        for g in gathers:
            g.pass_on()
        for g in gathers:
            g.wait_rest()
        _wait_direct(mod_copies)
        for layer in range(DEPTH):
            for j in range(N_DEV):
                sl = slice(j * cols, (j + 1) * cols)
                mod_ref[layer:layer + 1, sl] = mod_recv[j, layer:layer + 1, :] + bada_ref[layer:layer + 1, sl]
        for g in gathers:
            g.wait_sends()
        for own in locals_:
            own.wait()

    return pl.pallas_call(
        body,
        name="prep_gather",
        in_specs=[VMEM, VMEM, VMEM] + [ANY] * n_blocks,
        out_specs=[VMEM, VMEM] + [ANY] * n_blocks,
        out_shape=[jax.ShapeDtypeStruct((N_DEV, 8, D_MODEL), F32), jax.ShapeDtypeStruct((DEPTH, 3 * D_MODEL), F32)]
                  + [jax.ShapeDtypeStruct((N_DEV,) + blk.shape, blk.dtype) for blk in blocks],
        scratch_shapes=[pltpu.VMEM((8, D_MODEL), F32), pltpu.VMEM((N_DEV, 8, cols), F32),
                        pltpu.VMEM((N_DEV, 8, cols), F32),
                        pltpu.SemaphoreType.DMA((n_blocks, GATHER_SEMS)),
                        pltpu.SemaphoreType.DMA((n_blocks, GATHER_SEMS)),
                        pltpu.SemaphoreType.DMA((n_blocks,)),
                        pltpu.SemaphoreType.DMA((7,)), pltpu.SemaphoreType.DMA((7,)),
                        pltpu.SemaphoreType.DMA((7,)), pltpu.SemaphoreType.DMA((7,))],
        compiler_params=pltpu.CompilerParams(vmem_limit_bytes=VMEM_LIMIT),
    )(c_row, w_ada, b_ada, *blocks)


def _tail_reduce(parts, small, dmod8, loss_lanes):
    n_parts = len(parts)
    n_arr = n_parts + 1

    def body(*refs):
        part_refs, refs = refs[:n_arr], refs[n_arr:]
        dmod_ref, lanes_ref = refs[:2]
        shard_refs, refs = refs[2:2 + n_arr], refs[2 + n_arr:]
        stot_ref, dmod_all, loss_ref = refs[:3]
        bufs, refs = refs[3:3 + 4 * n_arr], refs[3 + 4 * n_arr:]
        loss_src, loss_all = refs[:2]
        rs_sems, (m_send, m_recv, g_send, g_recv, s_send, s_recv) = refs[2:7], refs[7:]
        mine = _index(_me())

        arrays = [dict(part=part_refs[n], out=shard_refs[n], stage=bufs[4 * n], sib=bufs[4 * n + 1],
                       snd=bufs[4 * n + 2], rcv=bufs[4 * n + 3]) for n in range(n_arr)]
        scatter = _ChipReduceScatter(arrays, *rs_sems)
        scatter.start()

        dmod_all[mine] = dmod_ref[...]
        dmod_copies = _direct_exchange(lambda p: dmod_ref, lambda m: dmod_all.at[_index(m)], m_send, m_recv)
        loss_src[...] = jnp.full(loss_src.shape, (0.5 / D_MODEL) * jnp.sum(lanes_ref[...]), F32)
        loss_all[mine] = loss_src[...]
        loss_copies = _direct_exchange(lambda p: loss_src, lambda m: loss_all.at[_index(m)], s_send, s_recv)

        scatter.exchange()
        scatter.finish()
        stot_ref[mine] = shard_refs[n_parts][...]
        gather = _TwoLevelGather(stot_ref, g_send, g_recv)
        gather.send_mine()
        gather.relay()
        gather.pass_on()
        gather.wait_rest()
        gather.wait_sends()
        _wait_direct(dmod_copies)
        _wait_direct(loss_copies)
        total = loss_all[0]
        for j in range(1, N_DEV):
            total = total + loss_all[j]
        loss_ref[...] = total
        scatter.wait_sends()

    everything = list(parts) + [small]
    comm_scratch = []
    for p in everything:
        comm_scratch += _ChipReduceScatter.buffers(p.shape[2], p.shape[3], p.dtype)
    comm_scratch += [pltpu.VMEM((8, 128), F32), pltpu.VMEM((N_DEV, 8, 128), F32)]
    comm_scratch += _ChipReduceScatter.semaphores(n_arr)
    comm_scratch += [pltpu.SemaphoreType.DMA((n,)) for n in (7, 7, GATHER_SEMS, GATHER_SEMS, 7, 7)]
    return pl.pallas_call(
        body,
        name="grad_reduce",
        in_specs=[ANY] * n_arr + [VMEM, VMEM],
        out_specs=[VMEM] * (n_arr + 3),
        out_shape=[jax.ShapeDtypeStruct(p.shape[2:], F32) for p in everything]
                  + [jax.ShapeDtypeStruct((N_DEV,) + small.shape[2:], F32),
                     jax.ShapeDtypeStruct((N_DEV,) + dmod8.shape, F32), jax.ShapeDtypeStruct((8, 128), F32)],
        scratch_shapes=comm_scratch,
        compiler_params=pltpu.CompilerParams(vmem_limit_bytes=VMEM_LIMIT),
    )(*everything, dmod8, loss_lanes)


SMALL_NAMES = ("w_pool", "w_sgu", "pool_scale", "sgu_ln_g", "sgu_ln_b", "b_sgu", "ln_g", "ln_b")
SMALL_ROWS = (512, 512, 4, 4, 4, 4, 8, 8)


def _adamw_small(g_packed, ws, ms, vs, name):
    n = len(SMALL_NAMES)

    def body(g_ref, *refs):
        w_refs, m_refs, v_refs = refs[:n], refs[n:2 * n], refs[2 * n:3 * n]
        outs = refs[3 * n:]
        row = 0
        for p, r in enumerate(SMALL_ROWS):
            for layer in range(DEPTH):
                dst = slice(layer * r, (layer + 1) * r)
                g = g_ref[layer * PACK_ROWS + row:layer * PACK_ROWS + row + r, :]
                delta, new_m, new_v = _adamw_math(w_refs[p][dst, :], g, m_refs[p][dst, :], v_refs[p][dst, :])
                outs[p][dst, :] = g
                outs[n + p][dst, :] = delta
                outs[2 * n + p][dst, :] = new_m
                outs[3 * n + p][dst, :] = new_v
            row += r

    res = pl.pallas_call(
        body,
        name=name,
        out_shape=[jax.ShapeDtypeStruct(w.shape, F32) for w in ws] * 4,
        compiler_params=pltpu.CompilerParams(vmem_limit_bytes=VMEM_LIMIT),
    )(g_packed, *ws, *ms, *vs)
    return res[:n], res[n:2 * n], res[2 * n:3 * n], res[3 * n:]


def kernel(x, c, w_ada, b_ada, w_in, w_pool, pool_scale, sgu_ln_g, sgu_ln_b, w_sgu, b_sgu, w_out, ln_g, ln_b, loss_target, m_w_ada, m_b_ada, m_w_in, m_w_pool, m_pool_scale, m_sgu_ln_g, m_sgu_ln_b, m_w_sgu, m_b_sgu, m_w_out, m_ln_g, m_ln_b, v_w_ada, v_b_ada, v_w_in, v_w_pool, v_pool_scale, v_sgu_ln_g, v_sgu_ln_b, v_w_sgu, v_b_sgu, v_w_out, v_ln_g, v_ln_b):
    mine = _index(_me())
    small_w = dict(w_pool=w_pool, w_sgu=w_sgu, pool_scale=pool_scale, sgu_ln_g=sgu_ln_g, sgu_ln_b=sgu_ln_b,
                   b_sgu=b_sgu, ln_g=ln_g, ln_b=ln_b)
    small_m = dict(w_pool=m_w_pool, w_sgu=m_w_sgu, pool_scale=m_pool_scale, sgu_ln_g=m_sgu_ln_g,
                   sgu_ln_b=m_sgu_ln_b, b_sgu=m_b_sgu, ln_g=m_ln_g, ln_b=m_ln_b)
    small_v = dict(w_pool=v_w_pool, w_sgu=v_w_sgu, pool_scale=v_pool_scale, sgu_ln_g=v_sgu_ln_g,
                   sgu_ln_b=v_sgu_ln_b, b_sgu=v_b_sgu, ln_g=v_ln_g, ln_b=v_ln_b)

    wint_loc = jnp.transpose(w_in, (0, 2, 1)).astype(BF16)
    wout_loc = w_out.astype(BF16)
    act_slots, mod, wint0, wout0 = _prep(c, w_ada, b_ada, [wint_loc[0], wout_loc[0]])
    act_all = act_slots[:, 0, :]
    w_int, w_outf = [wint0.reshape(D_IN, D_MODEL)], [wout0.reshape(D_MODEL, D_MODEL)]

    def layer_args(l):
        return (w_int[l], w_outf[l], w_pool[l], pool_scale[l].reshape(1, D_POOL), sgu_ln_g[l].reshape(1, D_SGU),
                sgu_ln_b[l].reshape(1, D_SGU), w_sgu[l], jnp.transpose(b_sgu[l]), ln_g[l].reshape(1, D_MODEL))

    acts, cur = [], x[0]
    for l in range(DEPTH):
        nxt = [wint_loc[l + 1], wout_loc[l + 1]] if l + 1 < DEPTH else []
        out, proj, y, *gathered = _layer_forward(cur, mod[l:l + 1], *layer_args(l), ln_b[l].reshape(1, D_MODEL),
                                                 f"layer_fwd_{l}", gather=nxt)
        if gathered:
            w_int.append(gathered[0].reshape(D_IN, D_MODEL))
            w_outf.append(gathered[1].reshape(D_MODEL, D_MODEL))
        acts.append((cur, proj, y))
        cur = out

    shard_in, shard_out = D_IN // N_DEV, D_MODEL // N_DEV
    a, b = cur, loss_target[0]
    smalls, dmods, loss_lanes = [None] * DEPTH, [None] * DEPTH, None
    g_w_in_t, g_w_out = [None] * DEPTH, [None] * DEPTH
    pending = []
    for l in reversed(range(DEPTH)):
        xin, proj, y = acts[l]
        dx, dproj, h, cat, dy, small, dmod, lanes, *shards = _layer_backward(
            a, b, xin, proj, y, mod[l:l + 1], *layer_args(l), l == DEPTH - 1, f"layer_bwd_{l}", reduce=pending)
        if shards:
            g_w_in_t[l + 1], g_w_out[l + 1] = shards
        if l == DEPTH - 1:
            loss_lanes = lanes
        pending = [_grad_matmul(dproj, h, 640, f"grad_w_in_{l}").reshape(4, 2, shard_in, D_MODEL),
                   _grad_matmul(cat, dy, 512, f"grad_w_out_{l}").reshape(4, 2, shard_out, D_MODEL)]
        smalls[l], dmods[l] = small, dmod
        a = b = dx
    grad_x = a[None]

    g_w_in_t[0], g_w_out[0], _, small_tot, dmod_slots, loss_tile = _tail_reduce(
        pending, jnp.concatenate(smalls, axis=0).reshape(4, 2, DEPTH * PACK_ROWS // N_DEV, 128),
        jnp.concatenate(dmods, axis=0).reshape(8, 3 * D_MODEL * DEPTH // 8), loss_lanes)
    loss = loss_tile[0, 0]
    dmod_all = dmod_slots.reshape(N_DEV, DEPTH, 3 * D_MODEL)
    gwin_t = jnp.stack(g_w_in_t)
    gwout = jnp.stack(g_w_out)

    cols_ada = w_ada.shape[2]
    dmod_cols = jnp.transpose(lax.dynamic_slice_in_dim(dmod_all, mine * cols_ada, cols_ada, axis=2), (1, 0, 2))
    g_w_ada, d_w_ada, nm_w_ada, nv_w_ada = _adamw_ada(w_ada, m_w_ada, v_w_ada, jnp.transpose(act_all), dmod_cols,
                                                      "adamw_w_ada")
    g_b_ada, d_b_ada, nm_b_ada, nv_b_ada = _adamw_bias(b_ada, m_b_ada, v_b_ada, dmod_all, "adamw_b_ada")
    flat = lambda t: t.reshape(-1, t.shape[-1])
    to_t = lambda t: flat(jnp.transpose(t, (0, 2, 1)))
    from_t = lambda t: jnp.transpose(t.reshape(DEPTH, shard_in, D_MODEL), (0, 2, 1))
    d_w_in, nm_w_in, nv_w_in = [from_t(t) for t in
                                _adamw(to_t(w_in), flat(gwin_t), to_t(m_w_in), to_t(v_w_in), 128, "adamw_w_in")]
    g_w_in = from_t(gwin_t)
    d_w_out, nm_w_out, nv_w_out = [t.reshape(w_out.shape) for t in
                                   _adamw(flat(w_out), flat(gwout), flat(m_w_out), flat(v_w_out), 128, "adamw_w_out")]
    rows128 = lambda t: t.reshape(-1, 128)
    small_out = _adamw_small(small_tot.reshape(DEPTH * PACK_ROWS, 128), [rows128(small_w[n]) for n in SMALL_NAMES],
                             [rows128(small_m[n]) for n in SMALL_NAMES], [rows128(small_v[n]) for n in SMALL_NAMES],
                             "adamw_small")
    gs, ds, ms, vs = [{n: t.reshape(small_w[n].shape) for n, t in zip(SMALL_NAMES, group)} for group in small_out]

    def ordered(w_ada_, b_ada_, w_in_, small, w_out_):
        return (w_ada_, b_ada_, w_in_, small["w_pool"], small["pool_scale"], small["sgu_ln_g"], small["sgu_ln_b"],
                small["w_sgu"], small["b_sgu"], w_out_, small["ln_g"], small["ln_b"])

    return (loss, grad_x,
            *ordered(g_w_ada, g_b_ada, g_w_in, gs, gwout),
            *ordered(d_w_ada, d_b_ada, d_w_in, ds, d_w_out),
            *ordered(nm_w_ada, nm_b_ada, nm_w_in, ms, nm_w_out),
            *ordered(nv_w_ada, nv_b_ada, nv_w_in, vs, nv_w_out))
```

```python
import jax
import jax.numpy as jnp
from jax import lax
from jax.experimental import pallas as pl
from jax.experimental.pallas import tpu as pltpu

F32 = jnp.float32
BF16 = jnp.bfloat16

D_MODEL = 1024
SEQ = 2048
DEPTH = 2
D_POOL = 512
D_SGU = 512
D_IN = 2560
N_GROUPS = 4
GROUP = 128
N_HEADS = 4
HEAD = 128
CHUNK = 128
WINDOWS = (2, 4, 8, 16)
ALPHA = (2.0 * DEPTH) ** 0.25
LN_EPS = 1e-5
N_DEV = 8

ADAM_LR = 0.001
ADAM_B1 = 0.9
ADAM_B2 = 0.999
ADAM_EPS = 1e-08
ADAM_WD = 0.01
ADAM_STEP = 10

TM = 256
HALO = 16
N_TILES = SEQ // TM
VMEM_LIMIT = 60 * 1024 * 1024

ROW_WPOOL = 0
ROW_WSGU = 512
ROW_PSCALE = 1024
ROW_SLNG = 1028
ROW_SLNB = 1032
ROW_BSGU = 1036
ROW_LNG = 1040
ROW_LNB = 1048
PACK_ROWS = 1088

SQRT_HALF = 0.7071067811865476
INV_SQRT_2PI = 0.3989422804014327


def _ln(x):
    mu = jnp.mean(x, axis=-1, keepdims=True)
    xc = x - mu
    var = jnp.mean(xc * xc, axis=-1, keepdims=True)
    rstd = lax.rsqrt(var + LN_EPS)
    return xc * rstd, rstd


def _ln_bwd(dxn, xn, rstd):
    m1 = jnp.mean(dxn, axis=-1, keepdims=True)
    m2 = jnp.mean(dxn * xn, axis=-1, keepdims=True)
    return rstd * (dxn - m1 - xn * m2)


def _gelu_parts(x, with_grad):
    cdf = 0.5 * (1.0 + lax.erf(x * SQRT_HALF))
    if not with_grad:
        return x * cdf, None
    return x * cdf, cdf + x * (INV_SQRT_2PI * jnp.exp(-0.5 * x * x))


def _silu_parts(x):
    s = jax.nn.sigmoid(x)
    return x * s, s * (1.0 + x * (1.0 - s))


def _dot(a, b):
    return lax.dot_general(a, b, (((1,), (0,)), ((), ())), preferred_element_type=F32)


def _dot_nt(a, b):
    return lax.dot_general(a, b, (((1,), (1,)), ((), ())), preferred_element_type=F32)


def _dot_tn(a, b):
    return lax.dot_general(a, b, (((0,), (0,)), ((), ())), preferred_element_type=F32)


def _row_index(tile):
    return tile * TM + lax.broadcasted_iota(jnp.int32, (TM, 1), 0)


def _window_sums(ext, forward):
    n = TM + HALO
    cur = ext
    outs = []
    for g in range(N_GROUPS):
        step = 1 << g
        cur = cur + pltpu.roll(cur, step if forward else n - step, 0)
        rows = cur[HALO:, :GROUP] if forward else cur[:TM, :GROUP]
        outs.append(rows)
        cur = cur[:, GROUP:] if g + 1 < N_GROUPS else None
    return outs


def _inverse_counts(rows):
    return [1.0 / jnp.minimum(rows + 1, w).astype(F32) for w in WINDOWS]


def _tril_bf16(w):
    t = lax.broadcasted_iota(jnp.int32, (CHUNK, CHUNK), 0)
    s = lax.broadcasted_iota(jnp.int32, (CHUNK, CHUNK), 1)
    return jnp.where(t >= s, w, 0.0).astype(BF16)


def _mix_forward(proj, halo, tile, wpool_ref, pscale, slng, slnb, wsgu_ref, bsgut_ref, keep):
    rows = _row_index(tile)
    inv_counts = _inverse_counts(rows)
    xa = proj[:, 0:D_POOL]
    ga = proj[:, D_POOL:2 * D_POOL]
    sums = _window_sums(jnp.concatenate([halo, xa], axis=0), True)
    ga_act, ga_grad = _silu_parts(ga)
    pooled, pw, ya = [], [], []
    for g in range(N_GROUPS):
        sl = slice(g * GROUP, (g + 1) * GROUP)
        p = (sums[g] * inv_counts[g] - xa[:, sl]).astype(BF16)
        q = _dot(p, wpool_ref[g].astype(BF16))
        pooled.append(p)
        pw.append(q)
        ya.append(q * pscale[:, sl] * ga_act[:, sl])

    u = proj[:, 2 * D_POOL:2 * D_POOL + D_SGU]
    v = proj[:, 2 * D_POOL + D_SGU:2 * D_POOL + 2 * D_SGU]
    gb = proj[:, 2 * D_POOL + 2 * D_SGU:]
    gb_act, gb_grad = _silu_parts(gb)
    u_act, u_grad = _gelu_parts(u, keep)
    v_act, v_grad = _gelu_parts(v, keep)
    vn, vrstd, vln, mixed, yb = [], [], [], [], []
    for h in range(N_HEADS):
        sl = slice(h * HEAD, (h + 1) * HEAD)
        n_h, r_h = _ln(v_act[:, sl])
        l_h = (n_h * slng[:, sl] + slnb[:, sl]).astype(BF16)
        w_h = _tril_bf16(wsgu_ref[h])
        bias = bsgut_ref[:, h:h + 1]
        m_h = jnp.concatenate(
            [_dot(w_h, l_h[k * CHUNK:(k + 1) * CHUNK]) + bias for k in range(TM // CHUNK)], axis=0)
        vn.append(n_h)
        vrstd.append(r_h)
        vln.append(l_h)
        mixed.append(m_h)
        yb.append(u_act[:, sl] * m_h * gb_act[:, sl])
    cat = jnp.concatenate(ya + yb, axis=1)
    if not keep:
        return cat, None
    return cat, dict(inv_counts=inv_counts, ga_act=ga_act, ga_grad=ga_grad, pooled=pooled, pw=pw, u_grad=u_grad,
                     v_grad=v_grad, u_act=u_act, gb_act=gb_act, gb_grad=gb_grad, vn=vn, vrstd=vrstd, vln=vln,
                     mixed=mixed)


def _const_spec(shape):
    nd = len(shape)
    return pl.BlockSpec(shape, lambda i: (0,) * nd)


def _layer_forward(x, mod_l, w_int, w_outf, w_pool, pscale, slng, slnb, w_sgu, bsgut, ln_g, ln_b, name, gather=()):
    n_gather = len(gather)

    def body(x_ref, mod_ref, wint_ref, wout_ref, wpool_ref, pscale_ref, slng_ref, slnb_ref, wsgu_ref, bsgut_ref,
             lng_ref, lnb_ref, *rest):
        loc_refs, rest = rest[:n_gather], rest[n_gather:]
        out_ref, proj_ref, y_ref = rest[:3]
        full_refs, rest = rest[3:3 + n_gather], rest[3 + n_gather:]
        halo_ref = rest[0]
        tile = pl.program_id(0)

        def gathers():
            g_send, g_recv, _ = rest[1:]
            return [_TwoLevelGather(full_refs[n], g_send.at[n], g_recv.at[n], src=loc_refs[n])
                    for n in range(n_gather)]

        def own_copies():
            mine = _index(_me())
            return [pltpu.make_async_copy(loc_refs[n], full_refs[n].at[mine], rest[3].at[n]) for n in range(n_gather)]

        @pl.when(tile == 0)
        def _():
            halo_ref[...] = jnp.zeros_like(halo_ref)
            if n_gather:
                for cp in own_copies():
                    cp.start()
                for g in gathers():
                    g.send_mine()

        if n_gather:
            @pl.when(tile == N_TILES // 2)
            def _():
                for g in gathers():
                    g.relay()

        xt = x_ref[...]
        shift = mod_ref[:, 0:D_MODEL]
        scale = mod_ref[:, D_MODEL:2 * D_MODEL]
        gate = mod_ref[:, 2 * D_MODEL:]
        xn, _ = _ln(xt)
        h = (xn * (1.0 + scale) + shift).astype(BF16)
        proj = _dot_nt(h, wint_ref[...])
        proj_ref[...] = proj
        cat, _ = _mix_forward(proj, halo_ref[...], tile, wpool_ref, pscale_ref[...], slng_ref[...], slnb_ref[...],
                              wsgu_ref, bsgut_ref, False)
        halo_ref[...] = proj[TM - HALO:, 0:D_POOL]
        y = _dot(cat.astype(BF16), wout_ref[...])
        y_ref[...] = y
        zn, _ = _ln(ALPHA * xt + gate * y)
        out_ref[...] = zn * lng_ref[...] + lnb_ref[...]

        if n_gather:
            @pl.when(tile == N_TILES - 1)
            def _():
                for g in gathers():
                    g.pass_on()
                for g in gathers():
                    g.wait_rest()
                for g in gathers():
                    g.wait_sends()
                for cp in own_copies():
                    cp.wait()

    row = lambda w: pl.BlockSpec((TM, w), lambda i: (i, 0))
    comm_scratch = [pltpu.SemaphoreType.DMA((n_gather, GATHER_SEMS)), pltpu.SemaphoreType.DMA((n_gather, GATHER_SEMS)),
                    pltpu.SemaphoreType.DMA((n_gather,))] if n_gather else []
    return pl.pallas_call(
        body,
        name=name,
        grid=(N_TILES,),
        in_specs=[row(D_MODEL), _const_spec((1, 3 * D_MODEL)), _const_spec((D_IN, D_MODEL)),
                  _const_spec((D_MODEL, D_MODEL)), _const_spec((N_GROUPS, GROUP, GROUP)), _const_spec((1, D_POOL)),
                  _const_spec((1, D_SGU)), _const_spec((1, D_SGU)), _const_spec((N_HEADS, CHUNK, CHUNK)),
                  _const_spec((CHUNK, N_HEADS)), _const_spec((1, D_MODEL)), _const_spec((1, D_MODEL))]
                 + [ANY] * n_gather,
        out_specs=[row(D_MODEL), row(D_IN), row(D_MODEL)] + [ANY] * n_gather,
        out_shape=[jax.ShapeDtypeStruct((SEQ, D_MODEL), F32), jax.ShapeDtypeStruct((SEQ, D_IN), F32),
                   jax.ShapeDtypeStruct((SEQ, D_MODEL), F32)]
                  + [jax.ShapeDtypeStruct((N_DEV,) + g.shape, g.dtype) for g in gather],
        scratch_shapes=[pltpu.VMEM((HALO, D_POOL), F32)] + comm_scratch,
        compiler_params=pltpu.CompilerParams(dimension_semantics=("arbitrary",), vmem_limit_bytes=VMEM_LIMIT),
    )(x, mod_l, w_int, w_outf, w_pool, pscale, slng, slnb, w_sgu, bsgut, ln_g, ln_b, *gather)


VEC_LNG, VEC_LNB, VEC_POOL, VEC_SGU, VEC_SHIFT, VEC_SCALE, VEC_GATE, VEC_LOSS = range(8)


def _layer_backward(a, b, x, proj, y, mod_l, w_int, w_outf, w_pool, pscale, slng, slnb, w_sgu, bsgut, ln_g, is_last,
                    name, reduce=()):
    n_red = len(reduce)

    def body(a_ref, b_ref, x_ref, proj_ref, prev_ref, y_ref, mod_ref, wint_ref, wout_ref, wpool_ref, pscale_ref,
             slng_ref, slnb_ref, wsgu_ref, bsgut_ref, lng_ref, *rest):
        part_refs, rest = rest[:n_red], rest[n_red:]
        dx_ref, dproj_ref, h_ref, cat_ref, dy_ref, small_ref, dmod_ref, loss_ref = rest[:8]
        shard_refs, rest = rest[8:8 + n_red], rest[8 + n_red:]
        vec_ref, dmix_ref, halo_ref = rest[:3]
        step = pl.program_id(0)
        tile = N_TILES - 1 - step

        def scatter():
            bufs, sems = rest[3:3 + 5 * n_red], rest[3 + 5 * n_red:]
            arrays = [dict(part=part_refs[n], out=shard_refs[n], staged=True, stage=bufs[5 * n], sib=bufs[5 * n + 1],
                           snd=bufs[5 * n + 2], rcv=bufs[5 * n + 3], relay=bufs[5 * n + 4]) for n in range(n_red)]
            return _ChipReduceScatter(arrays, *sems)

        @pl.when(step == 0)
        def _():
            small_ref[...] = jnp.zeros_like(small_ref)
            vec_ref[...] = jnp.zeros_like(vec_ref)
            dmix_ref[...] = jnp.zeros_like(dmix_ref)
            halo_ref[...] = jnp.zeros_like(halo_ref)
            if n_red:
                scatter().start()

        if n_red:
            @pl.when(step == 1)
            def _():
                scatter().exchange()

        def acc(row, lo, val):
            hi = lo + val.shape[1]
            vec_ref[row:row + 1, lo:hi] += jnp.sum(val, axis=0, keepdims=True)

        xt = x_ref[...]
        yt = y_ref[...]
        shift = mod_ref[:, 0:D_MODEL]
        scale = mod_ref[:, D_MODEL:2 * D_MODEL]
        gate = mod_ref[:, 2 * D_MODEL:]

        zn, zrstd = _ln(ALPHA * xt + gate * yt)
        if is_last:
            diff = a_ref[...] - b_ref[...]
            acc(VEC_LOSS, 0, diff * diff)
            dout = diff * (1.0 / D_MODEL)
        else:
            dout = a_ref[...]
        acc(VEC_LNG, 0, dout * zn)
        acc(VEC_LNB, 0, dout)
        dz = _ln_bwd(dout * lng_ref[...], zn, zrstd)
        acc(VEC_GATE, 0, dz * yt)
        dy = (dz * gate).astype(BF16)
        dy_ref[...] = dy
        dcat = _dot_nt(dy, wout_ref[...])

        proj = proj_ref[...]
        prev = jnp.where(tile > 0, prev_ref[...], 0.0)
        cat, k = _mix_forward(proj, prev, tile, wpool_ref, pscale_ref[...], slng_ref[...], slnb_ref[...],
                              wsgu_ref, bsgut_ref, True)
        cat_ref[...] = cat.astype(BF16)
        pscale = pscale_ref[...]
        slng = slng_ref[...]

        dga, dq = [], []
        for g in range(N_GROUPS):
            sl = slice(g * GROUP, (g + 1) * GROUP)
            dya = dcat[:, sl]
            dyp = dya * k["ga_act"][:, sl]
            dga.append(dya * k["pw"][g] * pscale[:, sl] * k["ga_grad"][:, sl])
            acc(VEC_POOL, g * GROUP, dyp * k["pw"][g])
            dpw = (dyp * pscale[:, sl]).astype(BF16)
            small_ref[ROW_WPOOL + g * GROUP:ROW_WPOOL + (g + 1) * GROUP, :] += _dot_tn(k["pooled"][g], dpw)
            dq.append(_dot_nt(dpw, wpool_ref[g].astype(BF16)))
        dpooled = jnp.concatenate(dq, axis=1)
        scaled = jnp.concatenate([dq[g] * k["inv_counts"][g] for g in range(N_GROUPS)], axis=1)
        sums = _window_sums(jnp.concatenate([scaled, halo_ref[...]], axis=0), False)
        halo_ref[...] = scaled[0:HALO]
        dxa = jnp.concatenate(sums, axis=1) - dpooled

        du, dv, dgb = [], [], []
        for h in range(N_HEADS):
            sl = slice(h * HEAD, (h + 1) * HEAD)
            dyb = dcat[:, D_POOL + h * HEAD:D_POOL + (h + 1) * HEAD]
            m_h = k["mixed"][h]
            ug = k["u_act"][:, sl] * dyb
            du.append(dyb * m_h * k["gb_act"][:, sl] * k["u_grad"][:, sl])
            dgb.append(ug * m_h * k["gb_grad"][:, sl])
            dmixed = ug * k["gb_act"][:, sl]
            dmixed_bf = dmixed.astype(BF16)
            w_h = _tril_bf16(wsgu_ref[h])
            dvln_parts = []
            dmix_sum = dmix_ref[h]
            dws = small_ref[ROW_WSGU + h * CHUNK:ROW_WSGU + (h + 1) * CHUNK, :]
            for c in range(TM // CHUNK):
                cs = slice(c * CHUNK, (c + 1) * CHUNK)
                dmix_sum = dmix_sum + dmixed[cs]
                dws = dws + _dot_nt(dmixed_bf[cs], k["vln"][h][cs])
                dvln_parts.append(_dot_tn(w_h, dmixed_bf[cs]))
            dmix_ref[h] = dmix_sum
            small_ref[ROW_WSGU + h * CHUNK:ROW_WSGU + (h + 1) * CHUNK, :] = dws
            dvln = jnp.concatenate(dvln_parts, axis=0)
            acc(VEC_SGU, h * HEAD, dvln * k["vn"][h])
            acc(VEC_SGU, D_SGU + h * HEAD, dvln)
            dvv = _ln_bwd(dvln * slng[:, sl], k["vn"][h], k["vrstd"][h])
            dv.append(dvv * k["v_grad"][:, sl])

        dproj = jnp.concatenate([dxa] + dga + du + dv + dgb, axis=1).astype(BF16)
        dproj_ref[...] = dproj
        dh = _dot(dproj, wint_ref[...])

        xn, xrstd = _ln(xt)
        h_ref[...] = (xn * (1.0 + scale) + shift).astype(BF16)
        acc(VEC_SCALE, 0, dh * xn)
        acc(VEC_SHIFT, 0, dh)
        dx_ref[...] = _ln_bwd(dh * (1.0 + scale), xn, xrstd) + ALPHA * dz

        @pl.when(step == N_TILES - 1)
        def _():
            def put(row0, vec_row, lo, n):
                for r in range(n):
                    small_ref[row0 + r:row0 + r + 1, :] = vec_ref[vec_row:vec_row + 1, lo + r * 128:lo + (r + 1) * 128]

            put(ROW_PSCALE, VEC_POOL, 0, 4)
            put(ROW_SLNG, VEC_SGU, 0, 4)
            put(ROW_SLNB, VEC_SGU, D_SGU, 4)
            put(ROW_LNG, VEC_LNG, 0, 8)
            put(ROW_LNB, VEC_LNB, 0, 8)
            ones = jnp.ones((8, HEAD), F32)
            t = lax.broadcasted_iota(jnp.int32, (CHUNK, CHUNK), 0)
            s = lax.broadcasted_iota(jnp.int32, (CHUNK, CHUNK), 1)
            for h in range(N_HEADS):
                bias_rows = lax.dot_general(ones, dmix_ref[h], (((1,), (1,)), ((), ())),
                                            preferred_element_type=F32, precision=lax.Precision.HIGHEST)
                small_ref[ROW_BSGU + h:ROW_BSGU + h + 1, :] = bias_rows[0:1]
                blk = small_ref[ROW_WSGU + h * CHUNK:ROW_WSGU + (h + 1) * CHUNK, :]
                small_ref[ROW_WSGU + h * CHUNK:ROW_WSGU + (h + 1) * CHUNK, :] = jnp.where(t >= s, blk, 0.0)
            dmod_ref[:, 0:D_MODEL] = vec_ref[VEC_SHIFT:VEC_SHIFT + 1, :]
            dmod_ref[:, D_MODEL:2 * D_MODEL] = vec_ref[VEC_SCALE:VEC_SCALE + 1, :]
            dmod_ref[:, 2 * D_MODEL:] = vec_ref[VEC_GATE:VEC_GATE + 1, :]
            loss_ref[...] = vec_ref[VEC_LOSS:VEC_LOSS + 1, :]
            if n_red:
                scatter().finish()
                scatter().wait_sends()

    rev = lambda w: pl.BlockSpec((TM, w), lambda i: (N_TILES - 1 - i, 0))
    prev_spec = pl.BlockSpec(
        (HALO, D_POOL), lambda i: (jnp.maximum((N_TILES - 1 - i) * (TM // HALO) - 1, 0), 0))
    comm_scratch = []
    for p in reduce:
        comm_scratch += _ChipReduceScatter.buffers(p.shape[2], p.shape[3], p.dtype)
    if n_red:
        comm_scratch += _ChipReduceScatter.semaphores(n_red)
    return pl.pallas_call(
        body,
        name=name,
        grid=(N_TILES,),
        in_specs=[rev(D_MODEL), rev(D_MODEL) if is_last else pl.BlockSpec((TM, D_MODEL), lambda i: (0, 0)),
                  rev(D_MODEL), rev(D_IN), prev_spec, rev(D_MODEL),
                  _const_spec((1, 3 * D_MODEL)), _const_spec((D_IN, D_MODEL)), _const_spec((D_MODEL, D_MODEL)),
                  _const_spec((N_GROUPS, GROUP, GROUP)), _const_spec((1, D_POOL)), _const_spec((1, D_SGU)),
                  _const_spec((1, D_SGU)), _const_spec((N_HEADS, CHUNK, CHUNK)), _const_spec((CHUNK, N_HEADS)),
                  _const_spec((1, D_MODEL))] + [ANY] * n_red,
        out_specs=[rev(D_MODEL), rev(D_IN), rev(D_MODEL), rev(D_MODEL), rev(D_MODEL),
                   _const_spec((PACK_ROWS, 128)), _const_spec((1, 3 * D_MODEL)), _const_spec((1, D_MODEL))]
                  + [_const_spec(p.shape[2:]) for p in reduce],
        out_shape=[jax.ShapeDtypeStruct((SEQ, D_MODEL), F32), jax.ShapeDtypeStruct((SEQ, D_IN), BF16),
                   jax.ShapeDtypeStruct((SEQ, D_MODEL), BF16), jax.ShapeDtypeStruct((SEQ, D_MODEL), BF16),
                   jax.ShapeDtypeStruct((SEQ, D_MODEL), BF16), jax.ShapeDtypeStruct((PACK_ROWS, 128), F32),
                   jax.ShapeDtypeStruct((1, 3 * D_MODEL), F32), jax.ShapeDtypeStruct((1, D_MODEL), F32)]
                  + [jax.ShapeDtypeStruct(p.shape[2:], F32) for p in reduce],
        scratch_shapes=[pltpu.VMEM((8, D_MODEL), F32), pltpu.VMEM((N_HEADS, CHUNK, HEAD), F32),
                        pltpu.VMEM((HALO, D_POOL), F32)] + comm_scratch,
        compiler_params=pltpu.CompilerParams(dimension_semantics=("arbitrary",), vmem_limit_bytes=VMEM_LIMIT),
    )(a, b, x, proj, proj, y, mod_l, w_int, w_outf, w_pool, pscale, slng, slnb, w_sgu, bsgut, ln_g, *reduce)


def _grad_matmul(lhs, rhs, block_cols, name):
    m, n = lhs.shape[1], rhs.shape[1]

    def body(lhs_ref, rhs_ref, out_ref):
        out_ref[...] = _dot_tn(lhs_ref[...], rhs_ref[...]).astype(BF16)

    return pl.pallas_call(
        body,
        name=name,
        grid=(m // block_cols,),
        in_specs=[pl.BlockSpec((SEQ, block_cols), lambda j: (0, j)), pl.BlockSpec((SEQ, n), lambda j: (0, 0))],
        out_specs=pl.BlockSpec((block_cols, n), lambda j: (j, 0)),
        out_shape=jax.ShapeDtypeStruct((m, n), BF16),
        compiler_params=pltpu.CompilerParams(dimension_semantics=("arbitrary",), vmem_limit_bytes=VMEM_LIMIT),
    )(lhs, rhs)


def _adamw_math(w, g, m, v):
    m = ADAM_B1 * m + (1.0 - ADAM_B1) * g
    v = ADAM_B2 * v + (1.0 - ADAM_B2) * (g * g)
    m_hat = m / (1.0 - ADAM_B1 ** ADAM_STEP)
    v_hat = v / (1.0 - ADAM_B2 ** ADAM_STEP)
    delta = -ADAM_LR * (m_hat / (jnp.sqrt(v_hat) + ADAM_EPS) + ADAM_WD * w)
    return delta, m, v


def _adamw(w, g, m, v, block_rows, name):
    rows, cols = w.shape

    def body(w_ref, g_ref, m_ref, v_ref, d_ref, nm_ref, nv_ref):
        d_ref[...], nm_ref[...], nv_ref[...] = _adamw_math(w_ref[...], g_ref[...], m_ref[...], v_ref[...])

    spec = pl.BlockSpec((block_rows, cols), lambda i: (i, 0))
    return pl.pallas_call(
        body,
        name=name,
        grid=(rows // block_rows,),
        in_specs=[spec] * 4,
        out_specs=[spec] * 3,
        out_shape=[jax.ShapeDtypeStruct(w.shape, F32)] * 3,
        compiler_params=pltpu.CompilerParams(dimension_semantics=("arbitrary",), vmem_limit_bytes=VMEM_LIMIT),
    )(w, g, m, v)


def _adamw_ada(w, m, v, act_t, dmod_cols, name):
    cols = w.shape[2]

    rows = 256

    def body(w_ref, m_ref, v_ref, act_ref, dmod_ref, g_ref, d_ref, nm_ref, nv_ref):
        act = act_ref[...]
        dm = dmod_ref[0]
        g = act[:, 0:1] * dm[0:1, :]
        for b in range(1, N_DEV):
            g = g + act[:, b:b + 1] * dm[b:b + 1, :]
        g_ref[0] = g
        d_ref[0], nm_ref[0], nv_ref[0] = _adamw_math(w_ref[0], g, m_ref[0], v_ref[0])

    spec = pl.BlockSpec((1, rows, cols), lambda l, i: (l, i, 0))
    return pl.pallas_call(
        body,
        name=name,
        grid=(DEPTH, D_MODEL // rows),
        in_specs=[spec, spec, spec, pl.BlockSpec((rows, N_DEV), lambda l, i: (i, 0)),
                  pl.BlockSpec((1, N_DEV, cols), lambda l, i: (l, 0, 0))],
        out_specs=[spec] * 4,
        out_shape=[jax.ShapeDtypeStruct(w.shape, F32)] * 4,
        compiler_params=pltpu.CompilerParams(dimension_semantics=("arbitrary", "arbitrary"),
                                             vmem_limit_bytes=VMEM_LIMIT),
    )(w, m, v, act_t, dmod_cols)


def _adamw_bias(w, m, v, dmod_all, name):
    def body(w_ref, m_ref, v_ref, dmod_ref, g_ref, d_ref, nm_ref, nv_ref):
        g = dmod_ref[0]
        for b in range(1, N_DEV):
            g = g + dmod_ref[b]
        g_ref[...] = g
        d_ref[...], nm_ref[...], nv_ref[...] = _adamw_math(w_ref[...], g, m_ref[...], v_ref[...])

    return pl.pallas_call(
        body,
        name=name,
        out_shape=[jax.ShapeDtypeStruct(w.shape, F32)] * 4,
        compiler_params=pltpu.CompilerParams(vmem_limit_bytes=VMEM_LIMIT),
    )(w, m, v, dmod_all)


MESH = pl.DeviceIdType.MESH
SIBLING = 1
ANY = pl.BlockSpec(memory_space=pl.ANY)
VMEM = pl.BlockSpec(memory_space=pltpu.VMEM)


def _me():
    return lax.axis_index("x"), lax.axis_index("y"), lax.axis_index("c")


def _peer(r):
    x, y, c = _me()
    return (1 - x if r & 4 else x, 1 - y if r & 2 else y, 1 - c if r & 1 else c)


def _index(dev):
    return 4 * dev[0] + 2 * dev[1] + dev[2]


def _remote(src, dst, send_sem, recv_sem, dev):
    return pltpu.make_async_remote_copy(src_ref=src, dst_ref=dst, send_sem=send_sem, recv_sem=recv_sem,
                                        device_id=dev, device_id_type=MESH)


ACROSS_X, ACROSS_Y, ACROSS_BOTH = 4, 2, 6
GATHER_SEMS = 11


class _TwoLevelGather:
    def __init__(self, out, send_sems, recv_sems, src=None):
        self.out, self.send_sems, self.recv_sems, self.src = out, send_sems, recv_sems, src
        self.half = out.shape[1] // 2

    def _copy(self, k, block, part, to, src=None):
        slot = self.out.at[_index(block)]
        if part is not None:
            rows = pl.ds(part * self.half, self.half)
            slot = slot.at[rows]
            src = None if src is None else src.at[rows]
        return _remote(slot if src is None else src, slot, self.send_sems.at[k], self.recv_sems.at[k], to)

    def _mine(self):
        me = _me()
        src = self.out.at[_index(me)] if self.src is None else self.src
        x, y = _peer(ACROSS_X), _peer(ACROSS_Y)
        return [self._copy(1, me, 0, x, src), self._copy(3, me, 1, y, src), self._copy(2, me, 1, x, src),
                self._copy(4, me, 0, y, src), self._copy(0, me, None, _peer(SIBLING), src)]

    def _relayed(self):
        return [self._copy(5, _peer(ACROSS_X), 0, _peer(ACROSS_Y)), self._copy(6, _peer(ACROSS_Y), 1, _peer(ACROSS_X))]

    def _passed(self):
        sib, far = _peer(SIBLING), _peer(ACROSS_BOTH)
        return [self._copy(7, _peer(ACROSS_X), None, sib), self._copy(8, _peer(ACROSS_Y), None, sib),
                self._copy(9, far, 0, sib), self._copy(10, far, 1, sib)]

    def _arrival(self, k, r, part):
        return self._copy(k, _peer(r), part, _me())

    def send_mine(self):
        for cp in self._mine():
            cp.start()

    def relay(self):
        relayed = self._relayed()
        self._arrival(1, ACROSS_X, 0).wait_recv()
        relayed[0].start()
        self._arrival(3, ACROSS_Y, 1).wait_recv()
        relayed[1].start()

    def pass_on(self):
        passed = self._passed()
        self._arrival(2, ACROSS_X, 1).wait_recv()
        passed[0].start()
        self._arrival(4, ACROSS_Y, 0).wait_recv()
        passed[1].start()
        self._arrival(5, ACROSS_BOTH, 0).wait_recv()
        passed[2].start()
        self._arrival(6, ACROSS_BOTH, 1).wait_recv()
        passed[3].start()

    def wait_rest(self):
        self._arrival(0, SIBLING, None).wait_recv()
        self._arrival(7, ACROSS_X ^ SIBLING, None).wait_recv()
        self._arrival(8, ACROSS_Y ^ SIBLING, None).wait_recv()
        self._arrival(9, ACROSS_BOTH ^ SIBLING, 0).wait_recv()
        self._arrival(10, ACROSS_BOTH ^ SIBLING, 1).wait_recv()

    def wait_sends(self):
        for cp in self._mine() + self._relayed() + self._passed():
            cp.wait_send()


class _ChipReduceScatter:
    SLOTS = 6

    def __init__(self, arrays, l_sem, d_send, d_recv, i_send, i_recv):
        self.arrays = arrays
        self.l_sem, self.d_send, self.d_recv, self.i_send, self.i_recv = l_sem, d_send, d_recv, i_send, i_recv

    @staticmethod
    def buffers(rows, cols, dtype, staged=True):
        stage = [pltpu.VMEM((4, rows, cols), dtype)] if staged else []
        return stage + [pltpu.VMEM((4, rows, cols), dtype), pltpu.VMEM((3, rows, cols), dtype),
                        pltpu.VMEM((2, rows, cols), dtype), pltpu.VMEM((2, rows // 2, cols), dtype)]

    @classmethod
    def semaphores(cls, n):
        return [pltpu.SemaphoreType.DMA((n,)), pltpu.SemaphoreType.DMA((n,)), pltpu.SemaphoreType.DMA((n,)),
                pltpu.SemaphoreType.DMA((n, cls.SLOTS)), pltpu.SemaphoreType.DMA((n, cls.SLOTS))]

    def _pick(self, which):
        return list(enumerate(self.arrays)) if which is None else [(n, self.arrays[n]) for n in which]

    def _staging(self, which):
        c = _me()[2]
        return [pltpu.make_async_copy(a["part"].at[pl.ds(0, 4), c], a["stage"], self.l_sem.at[n])
                for n, a in self._pick(which) if a["staged"]]

    def _first(self, which):
        other = 1 - _me()[2]
        return [_remote(a["part"].at[pl.ds(0, 4), other], a["sib"], self.d_send.at[n], self.d_recv.at[n],
                        _peer(SIBLING)) for n, a in self._pick(which)]

    @staticmethod
    def _halves(a):
        half = a["rcv"].shape[1] // 2
        return pl.ds(0, half), pl.ds(half, half)

    def _hops(self, n, a):
        h0, h1 = self._halves(a)
        x, y = _peer(ACROSS_X), _peer(ACROSS_Y)
        snd, rcv, relay = a["snd"], a["rcv"], a["relay"]
        pairs = [(snd.at[2, h0], relay.at[0], x), (snd.at[2, h1], relay.at[1], y),
                 (snd.at[0, h0], rcv.at[0, h0], x), (snd.at[0, h1], rcv.at[0, h1], x),
                 (snd.at[1, h1], rcv.at[1, h1], y), (snd.at[1, h0], rcv.at[1, h0], y)]
        return [_remote(s, d, self.i_send.at[n, k], self.i_recv.at[n, k], to) for k, (s, d, to) in enumerate(pairs)]

    def _mine(self, a, chip, rows=None):
        src = a["stage"].at[chip] if a["staged"] else a["part"].at[chip, _me()[2]]
        mine, sib = (src[...], a["sib"][chip]) if rows is None else (src[rows, :], a["sib"][chip, rows, :])
        return mine.astype(F32) + sib.astype(F32)

    def start(self, which=None):
        for cp in self._staging(which) + self._first(which):
            cp.start()

    def exchange(self, which=None):
        for cp in self._staging(which):
            cp.wait()
        for cp in self._first(which):
            cp.wait_recv()
        chip = lambda dev: 2 * dev[0] + dev[1]
        across_x, across_y, far = chip(_peer(ACROSS_X)), chip(_peer(ACROSS_Y)), chip(_peer(ACROSS_BOTH))
        picked = self._pick(which)
        for n, a in picked:
            h0, h1 = self._halves(a)
            hops = self._hops(n, a)
            dtype = a["snd"].dtype
            a["snd"][2] = self._mine(a, far).astype(dtype)
            hops[0].start()
            hops[1].start()
            a["snd"][0, h0, :] = self._mine(a, across_x, h0).astype(dtype)
            hops[2].start()
            a["snd"][1, h1, :] = self._mine(a, across_y, h1).astype(dtype)
            hops[4].start()
        for n, a in picked:
            h0, h1 = self._halves(a)
            hops = self._hops(n, a)
            dtype = a["snd"].dtype
            hops[1].wait_recv()
            a["snd"][0, h1, :] = (self._mine(a, across_x, h1) + a["relay"][1].astype(F32)).astype(dtype)
            hops[3].start()
            hops[0].wait_recv()
            a["snd"][1, h0, :] = (self._mine(a, across_y, h0) + a["relay"][0].astype(F32)).astype(dtype)
            hops[5].start()

    def finish(self, which=None):
        x, y, _ = _me()
        home = 2 * x + y
        for n, a in self._pick(which):
            hops = self._hops(n, a)
            a["out"][...] = self._mine(a, home)
            hops[2].wait_recv()
            hops[3].wait_recv()
            a["out"][...] += a["rcv"][0].astype(F32)
            hops[4].wait_recv()
            hops[5].wait_recv()
            a["out"][...] += a["rcv"][1].astype(F32)

    def wait_sends(self, which=None):
        for cp in self._first(which):
            cp.wait_send()
        for n, a in self._pick(which):
            for cp in self._hops(n, a):
                cp.wait_send()


def _direct_exchange(src_of, dst_of, send_sems, recv_sems):
    me = _me()
    copies = [_remote(src_of(_peer(r)), dst_of(me), send_sems.at[r - 1], recv_sems.at[r - 1], _peer(r))
              for r in range(1, N_DEV)]
    for cp in copies:
        cp.start()
    return copies


def _wait_direct(copies):
    for cp in copies:
        cp.wait_recv()
    for cp in copies:
        cp.wait_send()


def _prep(c_row, w_ada, b_ada, blocks):
    cols = w_ada.shape[2]
    n_blocks = len(blocks)

    def body(c_ref, wada_ref, bada_ref, *rest):
        loc_refs, rest = rest[:n_blocks], rest[n_blocks:]
        act_all, mod_ref = rest[:2]
        full_refs, rest = rest[2:2 + n_blocks], rest[2 + n_blocks:]
        act_src, part, mod_recv, w_send, w_recv, w_local, a_send, a_recv, m_send, m_recv = rest
        me = _me()
        mine = _index(me)

        cval = c_ref[...]
        act_src[...] = jnp.zeros_like(act_src)
        act_src[0:1, :] = cval * jax.nn.sigmoid(cval)
        act_all[mine] = act_src[...]
        act_copies = _direct_exchange(lambda p: act_src, lambda m: act_all.at[_index(m)], a_send, a_recv)

        gathers, locals_ = [], []
        for n in range(n_blocks):
            own = pltpu.make_async_copy(loc_refs[n], full_refs[n].at[mine], w_local.at[n])
            own.start()
            locals_.append(own)
            g = _TwoLevelGather(full_refs[n], w_send.at[n], w_recv.at[n], src=loc_refs[n])
            g.send_mine()
            gathers.append(g)

        _wait_direct(act_copies)
        acts = jnp.concatenate([act_all[j, 0:1, :] for j in range(N_DEV)], axis=0)
        part[...] = jnp.zeros_like(part)
        for layer in range(DEPTH):
            res = lax.dot_general(acts, wada_ref[layer], (((1,), (0,)), ((), ())), preferred_element_type=F32,
                                  precision=lax.Precision.HIGHEST)
            for b in range(N_DEV):
                part[b, layer:layer + 1, :] = res[b:b + 1, :]
        mod_recv[mine] = part[mine]
        mod_copies = _direct_exchange(lambda p: part.at[_index(p)], lambda m: mod_recv.at[_index(m)], m_send, m_recv)

        for g in gathers:
            g.relay()
        for g in gathers:
            g.pass_on()
        for g in gathers:
            g.wait_rest()
        _wait_direct(mod_copies)
        for layer in range(DEPTH):
            for j in range(N_DEV):
                sl = slice(j * cols, (j + 1) * cols)
                mod_ref[layer:layer + 1, sl] = mod_recv[j, layer:layer + 1, :] + bada_ref[layer:layer + 1, sl]
        for g in gathers:
            g.wait_sends()
        for own in locals_:
            own.wait()

    return pl.pallas_call(
        body,
        name="prep_gather",
        in_specs=[VMEM, VMEM, VMEM] + [ANY] * n_blocks,
        out_specs=[VMEM, VMEM] + [ANY] * n_blocks,
        out_shape=[jax.ShapeDtypeStruct((N_DEV, 8, D_MODEL), F32), jax.ShapeDtypeStruct((DEPTH, 3 * D_MODEL), F32)]
                  + [jax.ShapeDtypeStruct((N_DEV,) + blk.shape, blk.dtype) for blk in blocks],
        scratch_shapes=[pltpu.VMEM((8, D_MODEL), F32), pltpu.VMEM((N_DEV, 8, cols), F32),
                        pltpu.VMEM((N_DEV, 8, cols), F32),
                        pltpu.SemaphoreType.DMA((n_blocks, GATHER_SEMS)),
                        pltpu.SemaphoreType.DMA((n_blocks, GATHER_SEMS)),
                        pltpu.SemaphoreType.DMA((n_blocks,)),
                        pltpu.SemaphoreType.DMA((7,)), pltpu.SemaphoreType.DMA((7,)),
                        pltpu.SemaphoreType.DMA((7,)), pltpu.SemaphoreType.DMA((7,))],
        compiler_params=pltpu.CompilerParams(vmem_limit_bytes=VMEM_LIMIT),
    )(c_row, w_ada, b_ada, *blocks)


def _grad_tail(dproj, h, cat, dy, small, dmod8, loss_lanes):
    shard_in, shard_out, shard_small = D_IN // N_DEV, D_MODEL // N_DEV, small.shape[2]
    W_IN, W_OUT, SMALL = 0, 1, 2

    def body(dproj_hbm, h_hbm, cat_hbm, dy_hbm, small_hbm, dmod_ref, lanes_ref,
             gwin_ref, gwout_ref, stot_ref, dmod_all, loss_ref,
             dproj_v, h_v, cat_v, dy_v, part_in, part_out, own_small, loss_src, loss_all, *rest):
        bufs, rest = rest[:13], rest[13:]
        load_sems, rs_sems = rest[0], rest[1:6]
        m_send, m_recv, g_send, g_recv, s_send, s_recv = rest[6:]
        mine = _index(_me())

        loads = [pltpu.make_async_copy(s, d, load_sems.at[n]) for n, (s, d) in enumerate(
            ((cat_hbm, cat_v), (dy_hbm, dy_v), (dproj_hbm, dproj_v), (h_hbm, h_v)))]
        for cp in loads:
            cp.start()
        arrays = [dict(part=part_in, out=gwin_ref, staged=False, sib=bufs[0], snd=bufs[1], rcv=bufs[2], relay=bufs[3]),
                  dict(part=part_out, out=gwout_ref, staged=False, sib=bufs[4], snd=bufs[5], rcv=bufs[6],
                       relay=bufs[7]),
                  dict(part=small_hbm, out=own_small, staged=True, stage=bufs[8], sib=bufs[9], snd=bufs[10],
                       rcv=bufs[11], relay=bufs[12])]
        scatter = _ChipReduceScatter(arrays, *rs_sems)
        scatter.start([SMALL])
        dmod_all[mine] = dmod_ref[...]
        dmod_copies = _direct_exchange(lambda p: dmod_ref, lambda m: dmod_all.at[_index(m)], m_send, m_recv)
        loss_src[...] = jnp.full(loss_src.shape, (0.5 / D_MODEL) * jnp.sum(lanes_ref[...]), F32)
        loss_all[mine] = loss_src[...]
        loss_copies = _direct_exchange(lambda p: loss_src, lambda m: loss_all.at[_index(m)], s_send, s_recv)

        loads[0].wait()
        loads[1].wait()
        for blk in range(2):
            res = _dot_tn(cat_v[:, blk * 512:(blk + 1) * 512], dy_v[...]).astype(BF16)
            for s in range(4):
                part_out[2 * blk + s // 2, s % 2] = res[s * shard_out:(s + 1) * shard_out]
        scatter.start([W_OUT])
        scatter.exchange([SMALL])

        gather = _TwoLevelGather(stot_ref, g_send, g_recv)
        loads[2].wait()
        loads[3].wait()
        for chip in range(4):
            res = _dot_tn(dproj_v[:, chip * 2 * shard_in:(chip + 1) * 2 * shard_in], h_v[...]).astype(BF16)
            part_in[chip, 0] = res[:shard_in]
            part_in[chip, 1] = res[shard_in:]
            if chip == 0:
                scatter.exchange([W_OUT])
            if chip == 1:
                scatter.finish([SMALL])
                stot_ref[mine] = own_small[...]
                gather.send_mine()
            if chip == 2:
                gather.relay()
        scatter.start([W_IN])
        scatter.finish([W_OUT])
        scatter.exchange([W_IN])
        gather.pass_on()
        gather.wait_rest()
        _wait_direct(dmod_copies)
        _wait_direct(loss_copies)
        total = loss_all[0]
        for j in range(1, N_DEV):
            total = total + loss_all[j]
        loss_ref[...] = total
        scatter.finish([W_IN])
        gather.wait_sends()
        scatter.wait_sends()

    buffers = _ChipReduceScatter.buffers
    comm_scratch = (buffers(shard_in, D_MODEL, BF16, staged=False) + buffers(shard_out, D_MODEL, BF16, staged=False)
                    + buffers(shard_small, 128, F32))
    comm_scratch += [pltpu.SemaphoreType.DMA((4,))] + _ChipReduceScatter.semaphores(3)
    comm_scratch += [pltpu.SemaphoreType.DMA((n,)) for n in (7, 7, GATHER_SEMS, GATHER_SEMS, 7, 7)]
    return pl.pallas_call(
        body,
        name="grad_tail",
        in_specs=[ANY] * 5 + [VMEM, VMEM],
        out_specs=[VMEM] * 5,
        out_shape=[jax.ShapeDtypeStruct((shard_in, D_MODEL), F32), jax.ShapeDtypeStruct((shard_out, D_MODEL), F32),
                   jax.ShapeDtypeStruct((N_DEV, shard_small, 128), F32),
                   jax.ShapeDtypeStruct((N_DEV,) + dmod8.shape, F32), jax.ShapeDtypeStruct((8, 128), F32)],
        scratch_shapes=[pltpu.VMEM(dproj.shape, BF16), pltpu.VMEM(h.shape, BF16), pltpu.VMEM(cat.shape, BF16),
                        pltpu.VMEM(dy.shape, BF16), pltpu.VMEM((4, 2, shard_in, D_MODEL), BF16),
                        pltpu.VMEM((4, 2, shard_out, D_MODEL), BF16), pltpu.VMEM((shard_small, 128), F32),
                        pltpu.VMEM((8, 128), F32), pltpu.VMEM((N_DEV, 8, 128), F32)] + comm_scratch,
        compiler_params=pltpu.CompilerParams(vmem_limit_bytes=VMEM_LIMIT),
    )(dproj, h, cat, dy, small, dmod8, loss_lanes)


SMALL_NAMES = ("w_pool", "w_sgu", "pool_scale", "sgu_ln_g", "sgu_ln_b", "b_sgu", "ln_g", "ln_b")
SMALL_ROWS = (512, 512, 4, 4, 4, 4, 8, 8)


def _adamw_small(g_packed, ws, ms, vs, name):
    n = len(SMALL_NAMES)

    def body(g_ref, *refs):
        w_refs, m_refs, v_refs = refs[:n], refs[n:2 * n], refs[2 * n:3 * n]
        outs = refs[3 * n:]
        row = 0
        for p, r in enumerate(SMALL_ROWS):
            for layer in range(DEPTH):
                dst = slice(layer * r, (layer + 1) * r)
                g = g_ref[layer * PACK_ROWS + row:layer * PACK_ROWS + row + r, :]
                delta, new_m, new_v = _adamw_math(w_refs[p][dst, :], g, m_refs[p][dst, :], v_refs[p][dst, :])
                outs[p][dst, :] = g
                outs[n + p][dst, :] = delta
                outs[2 * n + p][dst, :] = new_m
                outs[3 * n + p][dst, :] = new_v
            row += r

    res = pl.pallas_call(
        body,
        name=name,
        out_shape=[jax.ShapeDtypeStruct(w.shape, F32) for w in ws] * 4,
        compiler_params=pltpu.CompilerParams(vmem_limit_bytes=VMEM_LIMIT),
    )(g_packed, *ws, *ms, *vs)
    return res[:n], res[n:2 * n], res[2 * n:3 * n], res[3 * n:]


def kernel(x, c, w_ada, b_ada, w_in, w_pool, pool_scale, sgu_ln_g, sgu_ln_b, w_sgu, b_sgu, w_out, ln_g, ln_b, loss_target, m_w_ada, m_b_ada, m_w_in, m_w_pool, m_pool_scale, m_sgu_ln_g, m_sgu_ln_b, m_w_sgu, m_b_sgu, m_w_out, m_ln_g, m_ln_b, v_w_ada, v_b_ada, v_w_in, v_w_pool, v_pool_scale, v_sgu_ln_g, v_sgu_ln_b, v_w_sgu, v_b_sgu, v_w_out, v_ln_g, v_ln_b):
    mine = _index(_me())
    small_w = dict(w_pool=w_pool, w_sgu=w_sgu, pool_scale=pool_scale, sgu_ln_g=sgu_ln_g, sgu_ln_b=sgu_ln_b,
                   b_sgu=b_sgu, ln_g=ln_g, ln_b=ln_b)
    small_m = dict(w_pool=m_w_pool, w_sgu=m_w_sgu, pool_scale=m_pool_scale, sgu_ln_g=m_sgu_ln_g,
                   sgu_ln_b=m_sgu_ln_b, b_sgu=m_b_sgu, ln_g=m_ln_g, ln_b=m_ln_b)
    small_v = dict(w_pool=v_w_pool, w_sgu=v_w_sgu, pool_scale=v_pool_scale, sgu_ln_g=v_sgu_ln_g,
                   sgu_ln_b=v_sgu_ln_b, b_sgu=v_b_sgu, ln_g=v_ln_g, ln_b=v_ln_b)

    wint_loc = jnp.transpose(w_in, (0, 2, 1)).astype(BF16)
    wout_loc = w_out.astype(BF16)
    act_slots, mod, wint0, wout0 = _prep(c, w_ada, b_ada, [wint_loc[0], wout_loc[0]])
    act_all = act_slots[:, 0, :]
    w_int, w_outf = [wint0.reshape(D_IN, D_MODEL)], [wout0.reshape(D_MODEL, D_MODEL)]

    def layer_args(l):
        return (w_int[l], w_outf[l], w_pool[l], pool_scale[l].reshape(1, D_POOL), sgu_ln_g[l].reshape(1, D_SGU),
                sgu_ln_b[l].reshape(1, D_SGU), w_sgu[l], jnp.transpose(b_sgu[l]), ln_g[l].reshape(1, D_MODEL))

    acts, cur = [], x[0]
    for l in range(DEPTH):
        nxt = [wint_loc[l + 1], wout_loc[l + 1]] if l + 1 < DEPTH else []
        out, proj, y, *gathered = _layer_forward(cur, mod[l:l + 1], *layer_args(l), ln_b[l].reshape(1, D_MODEL),
                                                 f"layer_fwd_{l}", gather=nxt)
        if gathered:
            w_int.append(gathered[0].reshape(D_IN, D_MODEL))
            w_outf.append(gathered[1].reshape(D_MODEL, D_MODEL))
        acts.append((cur, proj, y))
        cur = out

    shard_in, shard_out = D_IN // N_DEV, D_MODEL // N_DEV
    a, b = cur, loss_target[0]
    smalls, dmods, loss_lanes = [None] * DEPTH, [None] * DEPTH, None
    g_w_in_t, g_w_out = [None] * DEPTH, [None] * DEPTH
    pending = []
    for l in reversed(range(DEPTH)):
        xin, proj, y = acts[l]
        dx, dproj, h, cat, dy, small, dmod, lanes, *shards = _layer_backward(
            a, b, xin, proj, y, mod[l:l + 1], *layer_args(l), l == DEPTH - 1, f"layer_bwd_{l}", reduce=pending)
        if shards:
            g_w_in_t[l + 1], g_w_out[l + 1] = shards
        if l == DEPTH - 1:
            loss_lanes = lanes
        if l > 0:
            pending = [_grad_matmul(dproj, h, 640, f"grad_w_in_{l}").reshape(4, 2, shard_in, D_MODEL),
                       _grad_matmul(cat, dy, 512, f"grad_w_out_{l}").reshape(4, 2, shard_out, D_MODEL)]
        smalls[l], dmods[l] = small, dmod
        a = b = dx
    grad_x = a[None]

    g_w_in_t[0], g_w_out[0], small_tot, dmod_slots, loss_tile = _grad_tail(
        dproj, h, cat, dy, jnp.concatenate(smalls, axis=0).reshape(4, 2, DEPTH * PACK_ROWS // N_DEV, 128),
        jnp.concatenate(dmods, axis=0).reshape(8, 3 * D_MODEL * DEPTH // 8), loss_lanes)
    loss = loss_tile[0, 0]
    dmod_all = dmod_slots.reshape(N_DEV, DEPTH, 3 * D_MODEL)
    gwin_t = jnp.stack(g_w_in_t)
    gwout = jnp.stack(g_w_out)

    cols_ada = w_ada.shape[2]
    dmod_cols = jnp.transpose(lax.dynamic_slice_in_dim(dmod_all, mine * cols_ada, cols_ada, axis=2), (1, 0, 2))
    g_w_ada, d_w_ada, nm_w_ada, nv_w_ada = _adamw_ada(w_ada, m_w_ada, v_w_ada, jnp.transpose(act_all), dmod_cols,
                                                      "adamw_w_ada")
    g_b_ada, d_b_ada, nm_b_ada, nv_b_ada = _adamw_bias(b_ada, m_b_ada, v_b_ada, dmod_all, "adamw_b_ada")
    flat = lambda t: t.reshape(-1, t.shape[-1])
    to_t = lambda t: flat(jnp.transpose(t, (0, 2, 1)))
    from_t = lambda t: jnp.transpose(t.reshape(DEPTH, shard_in, D_MODEL), (0, 2, 1))
    d_w_in, nm_w_in, nv_w_in = [from_t(t) for t in
                                _adamw(to_t(w_in), flat(gwin_t), to_t(m_w_in), to_t(v_w_in), 128, "adamw_w_in")]
    g_w_in = from_t(gwin_t)
    d_w_out, nm_w_out, nv_w_out = [t.reshape(w_out.shape) for t in
                                   _adamw(flat(w_out), flat(gwout), flat(m_w_out), flat(v_w_out), 128, "adamw_w_out")]
    rows128 = lambda t: t.reshape(-1, 128)
    small_out = _adamw_small(small_tot.reshape(DEPTH * PACK_ROWS, 128), [rows128(small_w[n]) for n in SMALL_NAMES],
                             [rows128(small_m[n]) for n in SMALL_NAMES], [rows128(small_v[n]) for n in SMALL_NAMES],
                             "adamw_small")
    gs, ds, ms, vs = [{n: t.reshape(small_w[n].shape) for n, t in zip(SMALL_NAMES, group)} for group in small_out]

    def ordered(w_ada_, b_ada_, w_in_, small, w_out_):
        return (w_ada_, b_ada_, w_in_, small["w_pool"], small["pool_scale"], small["sgu_ln_g"], small["sgu_ln_b"],
                small["w_sgu"], small["b_sgu"], w_out_, small["ln_g"], small["ln_b"])

    return (loss, grad_x,
            *ordered(g_w_ada, g_b_ada, g_w_in, gs, gwout),
            *ordered(d_w_ada, d_b_ada, d_w_in, ds, d_w_out),
            *ordered(nm_w_ada, nm_b_ada, nm_w_in, ms, nm_w_out),
            *ordered(nv_w_ada, nv_b_ada, nv_w_in, vs, nv_w_out))
```

```python
import jax
import jax.numpy as jnp
from jax import lax
from jax.experimental import pallas as pl
from jax.experimental.pallas import tpu as pltpu

F32 = jnp.float32
BF16 = jnp.bfloat16

D_MODEL = 1024
SEQ = 2048
DEPTH = 2
D_POOL = 512
D_SGU = 512
D_IN = 2560
N_GROUPS = 4
GROUP = 128
N_HEADS = 4
HEAD = 128
CHUNK = 128
WINDOWS = (2, 4, 8, 16)
ALPHA = (2.0 * DEPTH) ** 0.25
LN_EPS = 1e-5
N_DEV = 8

ADAM_LR = 0.001
ADAM_B1 = 0.9
ADAM_B2 = 0.999
ADAM_EPS = 1e-08
ADAM_WD = 0.01
ADAM_STEP = 10

TM = 256
HALO = 16
N_TILES = SEQ // TM
VMEM_LIMIT = 60 * 1024 * 1024

ROW_WPOOL = 0
ROW_WSGU = 512
ROW_PSCALE = 1024
ROW_SLNG = 1028
ROW_SLNB = 1032
ROW_BSGU = 1036
ROW_LNG = 1040
ROW_LNB = 1048
PACK_ROWS = 1088

SQRT_HALF = 0.7071067811865476
INV_SQRT_2PI = 0.3989422804014327


def _ln(x):
    mu = jnp.mean(x, axis=-1, keepdims=True)
    xc = x - mu
    var = jnp.mean(xc * xc, axis=-1, keepdims=True)
    rstd = lax.rsqrt(var + LN_EPS)
    return xc * rstd, rstd


def _ln_bwd(dxn, xn, rstd):
    m1 = jnp.mean(dxn, axis=-1, keepdims=True)
    m2 = jnp.mean(dxn * xn, axis=-1, keepdims=True)
    return rstd * (dxn - m1 - xn * m2)


def _gelu_parts(x, with_grad):
    cdf = 0.5 * (1.0 + lax.erf(x * SQRT_HALF))
    if not with_grad:
        return x * cdf, None
    return x * cdf, cdf + x * (INV_SQRT_2PI * jnp.exp(-0.5 * x * x))


def _silu_parts(x):
    s = jax.nn.sigmoid(x)
    return x * s, s * (1.0 + x * (1.0 - s))


def _dot(a, b):
    return lax.dot_general(a, b, (((1,), (0,)), ((), ())), preferred_element_type=F32)


def _dot_nt(a, b):
    return lax.dot_general(a, b, (((1,), (1,)), ((), ())), preferred_element_type=F32)


def _dot_tn(a, b):
    return lax.dot_general(a, b, (((0,), (0,)), ((), ())), preferred_element_type=F32)


def _row_index(tile):
    return tile * TM + lax.broadcasted_iota(jnp.int32, (TM, 1), 0)


def _window_sums(ext, forward):
    n = TM + HALO
    cur = ext
    outs = []
    for g in range(N_GROUPS):
        step = 1 << g
        cur = cur + pltpu.roll(cur, step if forward else n - step, 0)
        rows = cur[HALO:, :GROUP] if forward else cur[:TM, :GROUP]
        outs.append(rows)
        cur = cur[:, GROUP:] if g + 1 < N_GROUPS else None
    return outs


def _inverse_counts(rows):
    return [1.0 / jnp.minimum(rows + 1, w).astype(F32) for w in WINDOWS]


def _tril_bf16(w):
    t = lax.broadcasted_iota(jnp.int32, (CHUNK, CHUNK), 0)
    s = lax.broadcasted_iota(jnp.int32, (CHUNK, CHUNK), 1)
    return jnp.where(t >= s, w, 0.0).astype(BF16)


def _mix_forward(proj, halo, tile, wpool_ref, pscale, slng, slnb, wsgu_ref, bsgut_ref, keep):
    rows = _row_index(tile)
    inv_counts = _inverse_counts(rows)
    xa = proj[:, 0:D_POOL]
    ga = proj[:, D_POOL:2 * D_POOL]
    sums = _window_sums(jnp.concatenate([halo, xa], axis=0), True)
    ga_act, ga_grad = _silu_parts(ga)
    pooled, pw, ya = [], [], []
    for g in range(N_GROUPS):
        sl = slice(g * GROUP, (g + 1) * GROUP)
        p = (sums[g] * inv_counts[g] - xa[:, sl]).astype(BF16)
        q = _dot(p, wpool_ref[g].astype(BF16))
        pooled.append(p)
        pw.append(q)
        ya.append(q * pscale[:, sl] * ga_act[:, sl])

    u = proj[:, 2 * D_POOL:2 * D_POOL + D_SGU]
    v = proj[:, 2 * D_POOL + D_SGU:2 * D_POOL + 2 * D_SGU]
    gb = proj[:, 2 * D_POOL + 2 * D_SGU:]
    gb_act, gb_grad = _silu_parts(gb)
    u_act, u_grad = _gelu_parts(u, keep)
    v_act, v_grad = _gelu_parts(v, keep)
    vn, vrstd, vln, mixed, yb = [], [], [], [], []
    for h in range(N_HEADS):
        sl = slice(h * HEAD, (h + 1) * HEAD)
        n_h, r_h = _ln(v_act[:, sl])
        l_h = (n_h * slng[:, sl] + slnb[:, sl]).astype(BF16)
        w_h = _tril_bf16(wsgu_ref[h])
        bias = bsgut_ref[:, h:h + 1]
        m_h = jnp.concatenate(
            [_dot(w_h, l_h[k * CHUNK:(k + 1) * CHUNK]) + bias for k in range(TM // CHUNK)], axis=0)
        vn.append(n_h)
        vrstd.append(r_h)
        vln.append(l_h)
        mixed.append(m_h)
        yb.append(u_act[:, sl] * m_h * gb_act[:, sl])
    cat = jnp.concatenate(ya + yb, axis=1)
    if not keep:
        return cat, None
    return cat, dict(inv_counts=inv_counts, ga_act=ga_act, ga_grad=ga_grad, pooled=pooled, pw=pw, u_grad=u_grad,
                     v_grad=v_grad, u_act=u_act, gb_act=gb_act, gb_grad=gb_grad, vn=vn, vrstd=vrstd, vln=vln,
                     mixed=mixed)


def _const_spec(shape):
    nd = len(shape)
    return pl.BlockSpec(shape, lambda i: (0,) * nd)


def _layer_forward(x, mod_l, w_int, w_outf, w_pool, pscale, slng, slnb, w_sgu, bsgut, ln_g, ln_b, name, gather=()):
    n_gather = len(gather)

    def body(x_ref, mod_ref, wint_ref, wout_ref, wpool_ref, pscale_ref, slng_ref, slnb_ref, wsgu_ref, bsgut_ref,
             lng_ref, lnb_ref, *rest):
        loc_refs, rest = rest[:n_gather], rest[n_gather:]
        out_ref, proj_ref, y_ref = rest[:3]
        full_refs, rest = rest[3:3 + n_gather], rest[3 + n_gather:]
        halo_ref = rest[0]
        tile = pl.program_id(0)

        def gathers():
            g_send, g_recv, _ = rest[1:]
            return [_TwoLevelGather(full_refs[n], g_send.at[n], g_recv.at[n], src=loc_refs[n])
                    for n in range(n_gather)]

        def own_copies():
            mine = _index(_me())
            return [pltpu.make_async_copy(loc_refs[n], full_refs[n].at[mine], rest[3].at[n]) for n in range(n_gather)]

        @pl.when(tile == 0)
        def _():
            halo_ref[...] = jnp.zeros_like(halo_ref)
            if n_gather:
                for cp in own_copies():
                    cp.start()
                for g in gathers():
                    g.send_mine()

        if n_gather:
            @pl.when(tile == N_TILES // 2 - 1)
            def _():
                for g in gathers():
                    g.relay()

            @pl.when(tile == N_TILES - 2)
            def _():
                for g in gathers():
                    g.pass_near()

        xt = x_ref[...]
        shift = mod_ref[:, 0:D_MODEL]
        scale = mod_ref[:, D_MODEL:2 * D_MODEL]
        gate = mod_ref[:, 2 * D_MODEL:]
        xn, _ = _ln(xt)
        h = (xn * (1.0 + scale) + shift).astype(BF16)
        proj = _dot_nt(h, wint_ref[...])
        proj_ref[...] = proj
        cat, _ = _mix_forward(proj, halo_ref[...], tile, wpool_ref, pscale_ref[...], slng_ref[...], slnb_ref[...],
                              wsgu_ref, bsgut_ref, False)
        halo_ref[...] = proj[TM - HALO:, 0:D_POOL]
        y = _dot(cat.astype(BF16), wout_ref[...])
        y_ref[...] = y
        zn, _ = _ln(ALPHA * xt + gate * y)
        out_ref[...] = zn * lng_ref[...] + lnb_ref[...]

        if n_gather:
            @pl.when(tile == N_TILES - 1)
            def _():
                for g in gathers():
                    g.pass_far()
                for g in gathers():
                    g.wait_rest()
                for g in gathers():
                    g.wait_sends()
                for cp in own_copies():
                    cp.wait()

    row = lambda w: pl.BlockSpec((TM, w), lambda i: (i, 0))
    comm_scratch = [pltpu.SemaphoreType.DMA((n_gather, GATHER_SEMS)), pltpu.SemaphoreType.DMA((n_gather, GATHER_SEMS)),
                    pltpu.SemaphoreType.DMA((n_gather,))] if n_gather else []
    return pl.pallas_call(
        body,
        name=name,
        grid=(N_TILES,),
        in_specs=[row(D_MODEL), _const_spec((1, 3 * D_MODEL)), _const_spec((D_IN, D_MODEL)),
                  _const_spec((D_MODEL, D_MODEL)), _const_spec((N_GROUPS, GROUP, GROUP)), _const_spec((1, D_POOL)),
                  _const_spec((1, D_SGU)), _const_spec((1, D_SGU)), _const_spec((N_HEADS, CHUNK, CHUNK)),
                  _const_spec((CHUNK, N_HEADS)), _const_spec((1, D_MODEL)), _const_spec((1, D_MODEL))]
                 + [ANY] * n_gather,
        out_specs=[row(D_MODEL), row(D_IN), row(D_MODEL)] + [ANY] * n_gather,
        out_shape=[jax.ShapeDtypeStruct((SEQ, D_MODEL), F32), jax.ShapeDtypeStruct((SEQ, D_IN), F32),
                   jax.ShapeDtypeStruct((SEQ, D_MODEL), F32)]
                  + [jax.ShapeDtypeStruct((N_DEV,) + g.shape, g.dtype) for g in gather],
        scratch_shapes=[pltpu.VMEM((HALO, D_POOL), F32)] + comm_scratch,
        compiler_params=pltpu.CompilerParams(dimension_semantics=("arbitrary",), vmem_limit_bytes=VMEM_LIMIT),
    )(x, mod_l, w_int, w_outf, w_pool, pscale, slng, slnb, w_sgu, bsgut, ln_g, ln_b, *gather)


VEC_LNG, VEC_LNB, VEC_POOL, VEC_SGU, VEC_SHIFT, VEC_SCALE, VEC_GATE, VEC_LOSS = range(8)


def _layer_backward(a, b, x, proj, y, mod_l, w_int, w_outf, w_pool, pscale, slng, slnb, w_sgu, bsgut, ln_g, is_last,
                    name, reduce=()):
    n_red = len(reduce)

    def body(a_ref, b_ref, x_ref, proj_ref, prev_ref, y_ref, mod_ref, wint_ref, wout_ref, wpool_ref, pscale_ref,
             slng_ref, slnb_ref, wsgu_ref, bsgut_ref, lng_ref, *rest):
        part_refs, rest = rest[:n_red], rest[n_red:]
        dx_ref, dproj_ref, h_ref, cat_ref, dy_ref, small_ref, dmod_ref, loss_ref = rest[:8]
        shard_refs, rest = rest[8:8 + n_red], rest[8 + n_red:]
        vec_ref, dmix_ref, halo_ref = rest[:3]
        step = pl.program_id(0)
        tile = N_TILES - 1 - step

        def scatter():
            bufs, sems = rest[3:3 + 5 * n_red], rest[3 + 5 * n_red:]
            arrays = [dict(part=part_refs[n], out=shard_refs[n], staged=True, stage=bufs[5 * n], sib=bufs[5 * n + 1],
                           snd=bufs[5 * n + 2], rcv=bufs[5 * n + 3], relay=bufs[5 * n + 4]) for n in range(n_red)]
            return _ChipReduceScatter(arrays, *sems)

        @pl.when(step == 0)
        def _():
            small_ref[...] = jnp.zeros_like(small_ref)
            vec_ref[...] = jnp.zeros_like(vec_ref)
            dmix_ref[...] = jnp.zeros_like(dmix_ref)
            halo_ref[...] = jnp.zeros_like(halo_ref)
            if n_red:
                scatter().start()

        if n_red:
            @pl.when(step == 1)
            def _():
                scatter().exchange()

            @pl.when(step == N_TILES // 2)
            def _():
                scatter().fold()

        def acc(row, lo, val):
            hi = lo + val.shape[1]
            vec_ref[row:row + 1, lo:hi] += jnp.sum(val, axis=0, keepdims=True)

        xt = x_ref[...]
        yt = y_ref[...]
        shift = mod_ref[:, 0:D_MODEL]
        scale = mod_ref[:, D_MODEL:2 * D_MODEL]
        gate = mod_ref[:, 2 * D_MODEL:]

        zn, zrstd = _ln(ALPHA * xt + gate * yt)
        if is_last:
            diff = a_ref[...] - b_ref[...]
            acc(VEC_LOSS, 0, diff * diff)
            dout = diff * (1.0 / D_MODEL)
        else:
            dout = a_ref[...]
        acc(VEC_LNG, 0, dout * zn)
        acc(VEC_LNB, 0, dout)
        dz = _ln_bwd(dout * lng_ref[...], zn, zrstd)
        acc(VEC_GATE, 0, dz * yt)
        dy = (dz * gate).astype(BF16)
        dy_ref[...] = dy
        dcat = _dot_nt(dy, wout_ref[...])

        proj = proj_ref[...]
        prev = jnp.where(tile > 0, prev_ref[...], 0.0)
        cat, k = _mix_forward(proj, prev, tile, wpool_ref, pscale_ref[...], slng_ref[...], slnb_ref[...],
                              wsgu_ref, bsgut_ref, True)
        cat_ref[...] = cat.astype(BF16)
        pscale = pscale_ref[...]
        slng = slng_ref[...]

        dga, dq = [], []
        for g in range(N_GROUPS):
            sl = slice(g * GROUP, (g + 1) * GROUP)
            dya = dcat[:, sl]
            dyp = dya * k["ga_act"][:, sl]
            dga.append(dya * k["pw"][g] * pscale[:, sl] * k["ga_grad"][:, sl])
            acc(VEC_POOL, g * GROUP, dyp * k["pw"][g])
            dpw = (dyp * pscale[:, sl]).astype(BF16)
            small_ref[ROW_WPOOL + g * GROUP:ROW_WPOOL + (g + 1) * GROUP, :] += _dot_tn(k["pooled"][g], dpw)
            dq.append(_dot_nt(dpw, wpool_ref[g].astype(BF16)))
        dpooled = jnp.concatenate(dq, axis=1)
        scaled = jnp.concatenate([dq[g] * k["inv_counts"][g] for g in range(N_GROUPS)], axis=1)
        sums = _window_sums(jnp.concatenate([scaled, halo_ref[...]], axis=0), False)
        halo_ref[...] = scaled[0:HALO]
        dxa = jnp.concatenate(sums, axis=1) - dpooled

        du, dv, dgb = [], [], []
        for h in range(N_HEADS):
            sl = slice(h * HEAD, (h + 1) * HEAD)
            dyb = dcat[:, D_POOL + h * HEAD:D_POOL + (h + 1) * HEAD]
            m_h = k["mixed"][h]
            ug = k["u_act"][:, sl] * dyb
            du.append(dyb * m_h * k["gb_act"][:, sl] * k["u_grad"][:, sl])
            dgb.append(ug * m_h * k["gb_grad"][:, sl])
            dmixed = ug * k["gb_act"][:, sl]
            dmixed_bf = dmixed.astype(BF16)
            w_h = _tril_bf16(wsgu_ref[h])
            dvln_parts = []
            dmix_sum = dmix_ref[h]
            dws = small_ref[ROW_WSGU + h * CHUNK:ROW_WSGU + (h + 1) * CHUNK, :]
            for c in range(TM // CHUNK):
                cs = slice(c * CHUNK, (c + 1) * CHUNK)
                dmix_sum = dmix_sum + dmixed[cs]
                dws = dws + _dot_nt(dmixed_bf[cs], k["vln"][h][cs])
                dvln_parts.append(_dot_tn(w_h, dmixed_bf[cs]))
            dmix_ref[h] = dmix_sum
            small_ref[ROW_WSGU + h * CHUNK:ROW_WSGU + (h + 1) * CHUNK, :] = dws
            dvln = jnp.concatenate(dvln_parts, axis=0)
            acc(VEC_SGU, h * HEAD, dvln * k["vn"][h])
            acc(VEC_SGU, D_SGU + h * HEAD, dvln)
            dvv = _ln_bwd(dvln * slng[:, sl], k["vn"][h], k["vrstd"][h])
            dv.append(dvv * k["v_grad"][:, sl])

        dproj = jnp.concatenate([dxa] + dga + du + dv + dgb, axis=1).astype(BF16)
        dproj_ref[...] = dproj
        dh = _dot(dproj, wint_ref[...])

        xn, xrstd = _ln(xt)
        h_ref[...] = (xn * (1.0 + scale) + shift).astype(BF16)
        acc(VEC_SCALE, 0, dh * xn)
        acc(VEC_SHIFT, 0, dh)
        dx_ref[...] = _ln_bwd(dh * (1.0 + scale), xn, xrstd) + ALPHA * dz

        @pl.when(step == N_TILES - 1)
        def _():
            def put(row0, vec_row, lo, n):
                for r in range(n):
                    small_ref[row0 + r:row0 + r + 1, :] = vec_ref[vec_row:vec_row + 1, lo + r * 128:lo + (r + 1) * 128]

            put(ROW_PSCALE, VEC_POOL, 0, 4)
            put(ROW_SLNG, VEC_SGU, 0, 4)
            put(ROW_SLNB, VEC_SGU, D_SGU, 4)
            put(ROW_LNG, VEC_LNG, 0, 8)
            put(ROW_LNB, VEC_LNB, 0, 8)
            ones = jnp.ones((8, HEAD), F32)
            t = lax.broadcasted_iota(jnp.int32, (CHUNK, CHUNK), 0)
            s = lax.broadcasted_iota(jnp.int32, (CHUNK, CHUNK), 1)
            for h in range(N_HEADS):
                bias_rows = lax.dot_general(ones, dmix_ref[h], (((1,), (1,)), ((), ())),
                                            preferred_element_type=F32, precision=lax.Precision.HIGHEST)
                small_ref[ROW_BSGU + h:ROW_BSGU + h + 1, :] = bias_rows[0:1]
                blk = small_ref[ROW_WSGU + h * CHUNK:ROW_WSGU + (h + 1) * CHUNK, :]
                small_ref[ROW_WSGU + h * CHUNK:ROW_WSGU + (h + 1) * CHUNK, :] = jnp.where(t >= s, blk, 0.0)
            dmod_ref[:, 0:D_MODEL] = vec_ref[VEC_SHIFT:VEC_SHIFT + 1, :]
            dmod_ref[:, D_MODEL:2 * D_MODEL] = vec_ref[VEC_SCALE:VEC_SCALE + 1, :]
            dmod_ref[:, 2 * D_MODEL:] = vec_ref[VEC_GATE:VEC_GATE + 1, :]
            loss_ref[...] = vec_ref[VEC_LOSS:VEC_LOSS + 1, :]
            if n_red:
                scatter().finish()
                scatter().wait_sends()

    rev = lambda w: pl.BlockSpec((TM, w), lambda i: (N_TILES - 1 - i, 0))
    prev_spec = pl.BlockSpec(
        (HALO, D_POOL), lambda i: (jnp.maximum((N_TILES - 1 - i) * (TM // HALO) - 1, 0), 0))
    comm_scratch = []
    for p in reduce:
        comm_scratch += _ChipReduceScatter.buffers(p.shape[2], p.shape[3], p.dtype)
    if n_red:
        comm_scratch += _ChipReduceScatter.semaphores(n_red)
    return pl.pallas_call(
        body,
        name=name,
        grid=(N_TILES,),
        in_specs=[rev(D_MODEL), rev(D_MODEL) if is_last else pl.BlockSpec((TM, D_MODEL), lambda i: (0, 0)),
                  rev(D_MODEL), rev(D_IN), prev_spec, rev(D_MODEL),
                  _const_spec((1, 3 * D_MODEL)), _const_spec((D_IN, D_MODEL)), _const_spec((D_MODEL, D_MODEL)),
                  _const_spec((N_GROUPS, GROUP, GROUP)), _const_spec((1, D_POOL)), _const_spec((1, D_SGU)),
                  _const_spec((1, D_SGU)), _const_spec((N_HEADS, CHUNK, CHUNK)), _const_spec((CHUNK, N_HEADS)),
                  _const_spec((1, D_MODEL))] + [ANY] * n_red,
        out_specs=[rev(D_MODEL), rev(D_IN), rev(D_MODEL), rev(D_MODEL), rev(D_MODEL),
                   _const_spec((PACK_ROWS, 128)), _const_spec((1, 3 * D_MODEL)), _const_spec((1, D_MODEL))]
                  + [_const_spec(p.shape[2:]) for p in reduce],
        out_shape=[jax.ShapeDtypeStruct((SEQ, D_MODEL), F32), jax.ShapeDtypeStruct((SEQ, D_IN), BF16),
                   jax.ShapeDtypeStruct((SEQ, D_MODEL), BF16), jax.ShapeDtypeStruct((SEQ, D_MODEL), BF16),
                   jax.ShapeDtypeStruct((SEQ, D_MODEL), BF16), jax.ShapeDtypeStruct((PACK_ROWS, 128), F32),
                   jax.ShapeDtypeStruct((1, 3 * D_MODEL), F32), jax.ShapeDtypeStruct((1, D_MODEL), F32)]
                  + [jax.ShapeDtypeStruct(p.shape[2:], F32) for p in reduce],
        scratch_shapes=[pltpu.VMEM((8, D_MODEL), F32), pltpu.VMEM((N_HEADS, CHUNK, HEAD), F32),
                        pltpu.VMEM((HALO, D_POOL), F32)] + comm_scratch,
        compiler_params=pltpu.CompilerParams(dimension_semantics=("arbitrary",), vmem_limit_bytes=VMEM_LIMIT),
    )(a, b, x, proj, proj, y, mod_l, w_int, w_outf, w_pool, pscale, slng, slnb, w_sgu, bsgut, ln_g, *reduce)


def _grad_matmul(lhs, rhs, block_cols, name):
    m, n = lhs.shape[1], rhs.shape[1]

    def body(lhs_ref, rhs_ref, out_ref):
        out_ref[...] = _dot_tn(lhs_ref[...], rhs_ref[...]).astype(BF16)

    return pl.pallas_call(
        body,
        name=name,
        grid=(m // block_cols,),
        in_specs=[pl.BlockSpec((SEQ, block_cols), lambda j: (0, j)), pl.BlockSpec((SEQ, n), lambda j: (0, 0))],
        out_specs=pl.BlockSpec((block_cols, n), lambda j: (j, 0)),
        out_shape=jax.ShapeDtypeStruct((m, n), BF16),
        compiler_params=pltpu.CompilerParams(dimension_semantics=("arbitrary",), vmem_limit_bytes=VMEM_LIMIT),
    )(lhs, rhs)


def _adamw_math(w, g, m, v):
    m = ADAM_B1 * m + (1.0 - ADAM_B1) * g
    v = ADAM_B2 * v + (1.0 - ADAM_B2) * (g * g)
    m_hat = m / (1.0 - ADAM_B1 ** ADAM_STEP)
    v_hat = v / (1.0 - ADAM_B2 ** ADAM_STEP)
    delta = -ADAM_LR * (m_hat / (jnp.sqrt(v_hat) + ADAM_EPS) + ADAM_WD * w)
    return delta, m, v


def _adamw(w, g, m, v, block_rows, name):
    rows, cols = w.shape

    def body(w_ref, g_ref, m_ref, v_ref, d_ref, nm_ref, nv_ref):
        d_ref[...], nm_ref[...], nv_ref[...] = _adamw_math(w_ref[...], g_ref[...], m_ref[...], v_ref[...])

    spec = pl.BlockSpec((block_rows, cols), lambda i: (i, 0))
    return pl.pallas_call(
        body,
        name=name,
        grid=(rows // block_rows,),
        in_specs=[spec] * 4,
        out_specs=[spec] * 3,
        out_shape=[jax.ShapeDtypeStruct(w.shape, F32)] * 3,
        compiler_params=pltpu.CompilerParams(dimension_semantics=("arbitrary",), vmem_limit_bytes=VMEM_LIMIT),
    )(w, g, m, v)


def _adamw_ada(w, m, v, act_t, dmod_cols, name):
    cols = w.shape[2]

    rows = 256

    def body(w_ref, m_ref, v_ref, act_ref, dmod_ref, g_ref, d_ref, nm_ref, nv_ref):
        act = act_ref[...]
        dm = dmod_ref[0]
        g = act[:, 0:1] * dm[0:1, :]
        for b in range(1, N_DEV):
            g = g + act[:, b:b + 1] * dm[b:b + 1, :]
        g_ref[0] = g
        d_ref[0], nm_ref[0], nv_ref[0] = _adamw_math(w_ref[0], g, m_ref[0], v_ref[0])

    spec = pl.BlockSpec((1, rows, cols), lambda l, i: (l, i, 0))
    return pl.pallas_call(
        body,
        name=name,
        grid=(DEPTH, D_MODEL // rows),
        in_specs=[spec, spec, spec, pl.BlockSpec((rows, N_DEV), lambda l, i: (i, 0)),
                  pl.BlockSpec((1, N_DEV, cols), lambda l, i: (l, 0, 0))],
        out_specs=[spec] * 4,
        out_shape=[jax.ShapeDtypeStruct(w.shape, F32)] * 4,
        compiler_params=pltpu.CompilerParams(dimension_semantics=("arbitrary", "arbitrary"),
                                             vmem_limit_bytes=VMEM_LIMIT),
    )(w, m, v, act_t, dmod_cols)


def _adamw_bias(w, m, v, dmod_all, name):
    def body(w_ref, m_ref, v_ref, dmod_ref, g_ref, d_ref, nm_ref, nv_ref):
        g = dmod_ref[0]
        for b in range(1, N_DEV):
            g = g + dmod_ref[b]
        g_ref[...] = g
        d_ref[...], nm_ref[...], nv_ref[...] = _adamw_math(w_ref[...], g, m_ref[...], v_ref[...])

    return pl.pallas_call(
        body,
        name=name,
        out_shape=[jax.ShapeDtypeStruct(w.shape, F32)] * 4,
        compiler_params=pltpu.CompilerParams(vmem_limit_bytes=VMEM_LIMIT),
    )(w, m, v, dmod_all)


MESH = pl.DeviceIdType.MESH
SIBLING = 1
ANY = pl.BlockSpec(memory_space=pl.ANY)
VMEM = pl.BlockSpec(memory_space=pltpu.VMEM)


def _me():
    return lax.axis_index("x"), lax.axis_index("y"), lax.axis_index("c")


def _peer(r):
    x, y, c = _me()
    return (1 - x if r & 4 else x, 1 - y if r & 2 else y, 1 - c if r & 1 else c)


def _index(dev):
    return 4 * dev[0] + 2 * dev[1] + dev[2]


def _remote(src, dst, send_sem, recv_sem, dev):
    return pltpu.make_async_remote_copy(src_ref=src, dst_ref=dst, send_sem=send_sem, recv_sem=recv_sem,
                                        device_id=dev, device_id_type=MESH)


ACROSS_X, ACROSS_Y, ACROSS_BOTH = 4, 2, 6
GATHER_SEMS = 11


class _TwoLevelGather:
    def __init__(self, out, send_sems, recv_sems, src=None):
        self.out, self.send_sems, self.recv_sems, self.src = out, send_sems, recv_sems, src
        self.half = out.shape[1] // 2

    def _copy(self, k, block, part, to, src=None):
        slot = self.out.at[_index(block)]
        if part is not None:
            rows = pl.ds(part * self.half, self.half)
            slot = slot.at[rows]
            src = None if src is None else src.at[rows]
        return _remote(slot if src is None else src, slot, self.send_sems.at[k], self.recv_sems.at[k], to)

    def _mine(self):
        me = _me()
        src = self.out.at[_index(me)] if self.src is None else self.src
        x, y = _peer(ACROSS_X), _peer(ACROSS_Y)
        return [self._copy(1, me, 0, x, src), self._copy(3, me, 1, y, src), self._copy(2, me, 1, x, src),
                self._copy(4, me, 0, y, src), self._copy(0, me, None, _peer(SIBLING), src)]

    def _relayed(self):
        return [self._copy(5, _peer(ACROSS_X), 0, _peer(ACROSS_Y)), self._copy(6, _peer(ACROSS_Y), 1, _peer(ACROSS_X))]

    def _passed(self):
        sib, far = _peer(SIBLING), _peer(ACROSS_BOTH)
        return [self._copy(7, _peer(ACROSS_X), None, sib), self._copy(8, _peer(ACROSS_Y), None, sib),
                self._copy(9, far, 0, sib), self._copy(10, far, 1, sib)]

    def _arrival(self, k, r, part):
        return self._copy(k, _peer(r), part, _me())

    def send_mine(self):
        for cp in self._mine():
            cp.start()

    def relay(self):
        relayed = self._relayed()
        self._arrival(1, ACROSS_X, 0).wait_recv()
        relayed[0].start()
        self._arrival(3, ACROSS_Y, 1).wait_recv()
        relayed[1].start()

    def pass_near(self):
        passed = self._passed()
        self._arrival(2, ACROSS_X, 1).wait_recv()
        passed[0].start()
        self._arrival(4, ACROSS_Y, 0).wait_recv()
        passed[1].start()

    def pass_far(self):
        passed = self._passed()
        self._arrival(5, ACROSS_BOTH, 0).wait_recv()
        passed[2].start()
        self._arrival(6, ACROSS_BOTH, 1).wait_recv()
        passed[3].start()

    def pass_on(self):
        self.pass_near()
        self.pass_far()

    def wait_rest(self):
        self._arrival(0, SIBLING, None).wait_recv()
        self._arrival(7, ACROSS_X ^ SIBLING, None).wait_recv()
        self._arrival(8, ACROSS_Y ^ SIBLING, None).wait_recv()
        self._arrival(9, ACROSS_BOTH ^ SIBLING, 0).wait_recv()
        self._arrival(10, ACROSS_BOTH ^ SIBLING, 1).wait_recv()

    def wait_sends(self):
        for cp in self._mine() + self._relayed() + self._passed():
            cp.wait_send()


class _ChipReduceScatter:
    SLOTS = 6

    def __init__(self, arrays, l_sem, d_send, d_recv, i_send, i_recv):
        self.arrays = arrays
        self.l_sem, self.d_send, self.d_recv, self.i_send, self.i_recv = l_sem, d_send, d_recv, i_send, i_recv

    @staticmethod
    def buffers(rows, cols, dtype, staged=True):
        stage = [pltpu.VMEM((4, rows, cols), dtype)] if staged else []
        return stage + [pltpu.VMEM((4, rows, cols), dtype), pltpu.VMEM((3, rows, cols), dtype),
                        pltpu.VMEM((2, rows, cols), dtype), pltpu.VMEM((2, rows // 2, cols), dtype)]

    @classmethod
    def semaphores(cls, n):
        return [pltpu.SemaphoreType.DMA((n,)), pltpu.SemaphoreType.DMA((n,)), pltpu.SemaphoreType.DMA((n,)),
                pltpu.SemaphoreType.DMA((n, cls.SLOTS)), pltpu.SemaphoreType.DMA((n, cls.SLOTS))]

    def _pick(self, which):
        return list(enumerate(self.arrays)) if which is None else [(n, self.arrays[n]) for n in which]

    def _staging(self, which):
        c = _me()[2]
        return [pltpu.make_async_copy(a["part"].at[pl.ds(0, 4), c], a["stage"], self.l_sem.at[n])
                for n, a in self._pick(which) if a["staged"]]

    def _first(self, which):
        other = 1 - _me()[2]
        return [_remote(a["part"].at[pl.ds(0, 4), other], a["sib"], self.d_send.at[n], self.d_recv.at[n],
                        _peer(SIBLING)) for n, a in self._pick(which)]

    @staticmethod
    def _halves(a):
        half = a["rcv"].shape[1] // 2
        return pl.ds(0, half), pl.ds(half, half)

    def _hops(self, n, a):
        h0, h1 = self._halves(a)
        x, y = _peer(ACROSS_X), _peer(ACROSS_Y)
        snd, rcv, relay = a["snd"], a["rcv"], a["relay"]
        pairs = [(snd.at[2, h0], relay.at[0], x), (snd.at[2, h1], relay.at[1], y),
                 (snd.at[0, h0], rcv.at[0, h0], x), (snd.at[0, h1], rcv.at[0, h1], x),
                 (snd.at[1, h1], rcv.at[1, h1], y), (snd.at[1, h0], rcv.at[1, h0], y)]
        return [_remote(s, d, self.i_send.at[n, k], self.i_recv.at[n, k], to) for k, (s, d, to) in enumerate(pairs)]

    def _mine(self, a, chip, rows=None):
        src = a["stage"].at[chip] if a["staged"] else a["part"].at[chip, _me()[2]]
        mine, sib = (src[...], a["sib"][chip]) if rows is None else (src[rows, :], a["sib"][chip, rows, :])
        return mine.astype(F32) + sib.astype(F32)

    def start(self, which=None):
        for cp in self._staging(which) + self._first(which):
            cp.start()

    def exchange(self, which=None):
        for cp in self._staging(which):
            cp.wait()
        for cp in self._first(which):
            cp.wait_recv()
        chip = lambda dev: 2 * dev[0] + dev[1]
        across_x, across_y, far = chip(_peer(ACROSS_X)), chip(_peer(ACROSS_Y)), chip(_peer(ACROSS_BOTH))
        picked = self._pick(which)
        for n, a in picked:
            hops = self._hops(n, a)
            a["snd"][2] = self._mine(a, far).astype(a["snd"].dtype)
            hops[0].start()
            hops[1].start()
        for n, a in picked:
            h0, h1 = self._halves(a)
            hops = self._hops(n, a)
            dtype = a["snd"].dtype
            a["snd"][0, h0, :] = self._mine(a, across_x, h0).astype(dtype)
            hops[2].start()
            a["snd"][1, h1, :] = self._mine(a, across_y, h1).astype(dtype)
            hops[4].start()

    def fold(self, which=None):
        chip = lambda dev: 2 * dev[0] + dev[1]
        across_x, across_y = chip(_peer(ACROSS_X)), chip(_peer(ACROSS_Y))
        for n, a in self._pick(which):
            h0, h1 = self._halves(a)
            hops = self._hops(n, a)
            dtype = a["snd"].dtype
            hops[1].wait_recv()
            a["snd"][0, h1, :] = (self._mine(a, across_x, h1) + a["relay"][1].astype(F32)).astype(dtype)
            hops[3].start()
            hops[0].wait_recv()
            a["snd"][1, h0, :] = (self._mine(a, across_y, h0) + a["relay"][0].astype(F32)).astype(dtype)
            hops[5].start()

    def finish(self, which=None):
        x, y, _ = _me()
        home = 2 * x + y
        for n, a in self._pick(which):
            hops = self._hops(n, a)
            a["out"][...] = self._mine(a, home)
            hops[2].wait_recv()
            hops[3].wait_recv()
            a["out"][...] += a["rcv"][0].astype(F32)
            hops[4].wait_recv()
            hops[5].wait_recv()
            a["out"][...] += a["rcv"][1].astype(F32)

    def wait_sends(self, which=None):
        for cp in self._first(which):
            cp.wait_send()
        for n, a in self._pick(which):
            for cp in self._hops(n, a):
                cp.wait_send()


def _direct_exchange(src_of, dst_of, send_sems, recv_sems):
    me = _me()
    copies = [_remote(src_of(_peer(r)), dst_of(me), send_sems.at[r - 1], recv_sems.at[r - 1], _peer(r))
              for r in range(1, N_DEV)]
    for cp in copies:
        cp.start()
    return copies


def _wait_direct(copies):
    for cp in copies:
        cp.wait_recv()
    for cp in copies:
        cp.wait_send()


def _prep(c_row, w_ada, b_ada, blocks):
    cols = w_ada.shape[2]
    n_blocks = len(blocks)

    def body(c_ref, wada_ref, bada_ref, *rest):
        loc_refs, rest = rest[:n_blocks], rest[n_blocks:]
        act_all, mod_ref = rest[:2]
        full_refs, rest = rest[2:2 + n_blocks], rest[2 + n_blocks:]
        act_src, part, mod_recv, w_send, w_recv, w_local, a_send, a_recv, m_send, m_recv = rest
        me = _me()
        mine = _index(me)

        cval = c_ref[...]
        act_src[...] = jnp.zeros_like(act_src)
        act_src[0:1, :] = cval * jax.nn.sigmoid(cval)
        act_all[mine] = act_src[...]
        act_copies = _direct_exchange(lambda p: act_src, lambda m: act_all.at[_index(m)], a_send, a_recv)

        gathers, locals_ = [], []
        for n in range(n_blocks):
            own = pltpu.make_async_copy(loc_refs[n], full_refs[n].at[mine], w_local.at[n])
            own.start()
            locals_.append(own)
            g = _TwoLevelGather(full_refs[n], w_send.at[n], w_recv.at[n], src=loc_refs[n])
            g.send_mine()
            gathers.append(g)

        _wait_direct(act_copies)
        acts = jnp.concatenate([act_all[j, 0:1, :] for j in range(N_DEV)], axis=0)
        part[...] = jnp.zeros_like(part)
        for layer in range(DEPTH):
            res = lax.dot_general(acts, wada_ref[layer], (((1,), (0,)), ((), ())), preferred_element_type=F32,
                                  precision=lax.Precision.HIGHEST)
            for b in range(N_DEV):
                part[b, layer:layer + 1, :] = res[b:b + 1, :]
        mod_recv[mine] = part[mine]
        mod_copies = _direct_exchange(lambda p: part.at[_index(p)], lambda m: mod_recv.at[_index(m)], m_send, m_recv)

        for g in gathers:
            g.relay()
        for g in gathers:
            g.pass_on()
        for g in gathers:
            g.wait_rest()
        _wait_direct(mod_copies)
        for layer in range(DEPTH):
            for j in range(N_DEV):
                sl = slice(j * cols, (j + 1) * cols)
                mod_ref[layer:layer + 1, sl] = mod_recv[j, layer:layer + 1, :] + bada_ref[layer:layer + 1, sl]
        for g in gathers:
            g.wait_sends()
        for own in locals_:
            own.wait()

    return pl.pallas_call(
        body,
        name="prep_gather",
        in_specs=[VMEM, VMEM, VMEM] + [ANY] * n_blocks,
        out_specs=[VMEM, VMEM] + [ANY] * n_blocks,
        out_shape=[jax.ShapeDtypeStruct((N_DEV, 8, D_MODEL), F32), jax.ShapeDtypeStruct((DEPTH, 3 * D_MODEL), F32)]
                  + [jax.ShapeDtypeStruct((N_DEV,) + blk.shape, blk.dtype) for blk in blocks],
        scratch_shapes=[pltpu.VMEM((8, D_MODEL), F32), pltpu.VMEM((N_DEV, 8, cols), F32),
                        pltpu.VMEM((N_DEV, 8, cols), F32),
                        pltpu.SemaphoreType.DMA((n_blocks, GATHER_SEMS)),
                        pltpu.SemaphoreType.DMA((n_blocks, GATHER_SEMS)),
                        pltpu.SemaphoreType.DMA((n_blocks,)),
                        pltpu.SemaphoreType.DMA((7,)), pltpu.SemaphoreType.DMA((7,)),
                        pltpu.SemaphoreType.DMA((7,)), pltpu.SemaphoreType.DMA((7,))],
        compiler_params=pltpu.CompilerParams(vmem_limit_bytes=VMEM_LIMIT),
    )(c_row, w_ada, b_ada, *blocks)


def _grad_tail(dproj, h, cat, dy, small, dmod8, loss_lanes):
    shard_in, shard_out, shard_small = D_IN // N_DEV, D_MODEL // N_DEV, small.shape[2]
    W_IN, W_OUT, SMALL = 0, 1, 2

    def body(dproj_hbm, h_hbm, cat_hbm, dy_hbm, small_hbm, dmod_ref, lanes_ref,
             gwin_ref, gwout_ref, stot_ref, dmod_all, loss_ref,
             dproj_v, h_v, cat_v, dy_v, part_in, part_out, own_small, loss_src, loss_all, *rest):
        bufs, rest = rest[:13], rest[13:]
        load_sems, rs_sems = rest[0], rest[1:6]
        m_send, m_recv, g_send, g_recv, s_send, s_recv = rest[6:]
        mine = _index(_me())

        loads = [pltpu.make_async_copy(s, d, load_sems.at[n]) for n, (s, d) in enumerate(
            ((cat_hbm, cat_v), (dy_hbm, dy_v), (dproj_hbm, dproj_v), (h_hbm, h_v)))]
        for cp in loads:
            cp.start()
        arrays = [dict(part=part_in, out=gwin_ref, staged=False, sib=bufs[0], snd=bufs[1], rcv=bufs[2], relay=bufs[3]),
                  dict(part=part_out, out=gwout_ref, staged=False, sib=bufs[4], snd=bufs[5], rcv=bufs[6],
                       relay=bufs[7]),
                  dict(part=small_hbm, out=own_small, staged=True, stage=bufs[8], sib=bufs[9], snd=bufs[10],
                       rcv=bufs[11], relay=bufs[12])]
        scatter = _ChipReduceScatter(arrays, *rs_sems)
        scatter.start([SMALL])
        dmod_all[mine] = dmod_ref[...]
        dmod_copies = _direct_exchange(lambda p: dmod_ref, lambda m: dmod_all.at[_index(m)], m_send, m_recv)
        loss_src[...] = jnp.full(loss_src.shape, (0.5 / D_MODEL) * jnp.sum(lanes_ref[...]), F32)
        loss_all[mine] = loss_src[...]
        loss_copies = _direct_exchange(lambda p: loss_src, lambda m: loss_all.at[_index(m)], s_send, s_recv)

        loads[0].wait()
        loads[1].wait()
        for blk in range(2):
            res = _dot_tn(cat_v[:, blk * 512:(blk + 1) * 512], dy_v[...]).astype(BF16)
            for s in range(4):
                part_out[2 * blk + s // 2, s % 2] = res[s * shard_out:(s + 1) * shard_out]
        scatter.start([W_OUT])
        scatter.exchange([SMALL])

        gather = _TwoLevelGather(stot_ref, g_send, g_recv)
        loads[2].wait()
        loads[3].wait()
        for chip in range(4):
            res = _dot_tn(dproj_v[:, chip * 2 * shard_in:(chip + 1) * 2 * shard_in], h_v[...]).astype(BF16)
            part_in[chip, 0] = res[:shard_in]
            part_in[chip, 1] = res[shard_in:]
            if chip == 0:
                scatter.exchange([W_OUT])
                scatter.fold([SMALL])
            if chip == 1:
                scatter.fold([W_OUT])
                scatter.finish([SMALL])
                stot_ref[mine] = own_small[...]
                gather.send_mine()
            if chip == 2:
                gather.relay()
        scatter.start([W_IN])
        scatter.finish([W_OUT])
        scatter.exchange([W_IN])
        gather.pass_on()
        scatter.fold([W_IN])
        gather.wait_rest()
        _wait_direct(dmod_copies)
        _wait_direct(loss_copies)
        total = loss_all[0]
        for j in range(1, N_DEV):
            total = total + loss_all[j]
        loss_ref[...] = total
        scatter.finish([W_IN])
        gather.wait_sends()
        scatter.wait_sends()

    buffers = _ChipReduceScatter.buffers
    comm_scratch = (buffers(shard_in, D_MODEL, BF16, staged=False) + buffers(shard_out, D_MODEL, BF16, staged=False)
                    + buffers(shard_small, 128, F32))
    comm_scratch += [pltpu.SemaphoreType.DMA((4,))] + _ChipReduceScatter.semaphores(3)
    comm_scratch += [pltpu.SemaphoreType.DMA((n,)) for n in (7, 7, GATHER_SEMS, GATHER_SEMS, 7, 7)]
    return pl.pallas_call(
        body,
        name="grad_tail",
        in_specs=[ANY] * 5 + [VMEM, VMEM],
        out_specs=[VMEM] * 5,
        out_shape=[jax.ShapeDtypeStruct((shard_in, D_MODEL), F32), jax.ShapeDtypeStruct((shard_out, D_MODEL), F32),
                   jax.ShapeDtypeStruct((N_DEV, shard_small, 128), F32),
                   jax.ShapeDtypeStruct((N_DEV,) + dmod8.shape, F32), jax.ShapeDtypeStruct((8, 128), F32)],
        scratch_shapes=[pltpu.VMEM(dproj.shape, BF16), pltpu.VMEM(h.shape, BF16), pltpu.VMEM(cat.shape, BF16),
                        pltpu.VMEM(dy.shape, BF16), pltpu.VMEM((4, 2, shard_in, D_MODEL), BF16),
                        pltpu.VMEM((4, 2, shard_out, D_MODEL), BF16), pltpu.VMEM((shard_small, 128), F32),
                        pltpu.VMEM((8, 128), F32), pltpu.VMEM((N_DEV, 8, 128), F32)] + comm_scratch,
        compiler_params=pltpu.CompilerParams(vmem_limit_bytes=VMEM_LIMIT),
    )(dproj, h, cat, dy, small, dmod8, loss_lanes)


SMALL_NAMES = ("w_pool", "w_sgu", "pool_scale", "sgu_ln_g", "sgu_ln_b", "b_sgu", "ln_g", "ln_b")
SMALL_ROWS = (512, 512, 4, 4, 4, 4, 8, 8)


def _adamw_small(g_packed, ws, ms, vs, name):
    n = len(SMALL_NAMES)

    def body(g_ref, *refs):
        w_refs, m_refs, v_refs = refs[:n], refs[n:2 * n], refs[2 * n:3 * n]
        outs = refs[3 * n:]
        row = 0
        for p, r in enumerate(SMALL_ROWS):
            for layer in range(DEPTH):
                dst = slice(layer * r, (layer + 1) * r)
                g = g_ref[layer * PACK_ROWS + row:layer * PACK_ROWS + row + r, :]
                delta, new_m, new_v = _adamw_math(w_refs[p][dst, :], g, m_refs[p][dst, :], v_refs[p][dst, :])
                outs[p][dst, :] = g
                outs[n + p][dst, :] = delta
                outs[2 * n + p][dst, :] = new_m
                outs[3 * n + p][dst, :] = new_v
            row += r

    res = pl.pallas_call(
        body,
        name=name,
        out_shape=[jax.ShapeDtypeStruct(w.shape, F32) for w in ws] * 4,
        compiler_params=pltpu.CompilerParams(vmem_limit_bytes=VMEM_LIMIT),
    )(g_packed, *ws, *ms, *vs)
    return res[:n], res[n:2 * n], res[2 * n:3 * n], res[3 * n:]


def kernel(x, c, w_ada, b_ada, w_in, w_pool, pool_scale, sgu_ln_g, sgu_ln_b, w_sgu, b_sgu, w_out, ln_g, ln_b, loss_target, m_w_ada, m_b_ada, m_w_in, m_w_pool, m_pool_scale, m_sgu_ln_g, m_sgu_ln_b, m_w_sgu, m_b_sgu, m_w_out, m_ln_g, m_ln_b, v_w_ada, v_b_ada, v_w_in, v_w_pool, v_pool_scale, v_sgu_ln_g, v_sgu_ln_b, v_w_sgu, v_b_sgu, v_w_out, v_ln_g, v_ln_b):
    mine = _index(_me())
    small_w = dict(w_pool=w_pool, w_sgu=w_sgu, pool_scale=pool_scale, sgu_ln_g=sgu_ln_g, sgu_ln_b=sgu_ln_b,
                   b_sgu=b_sgu, ln_g=ln_g, ln_b=ln_b)
    small_m = dict(w_pool=m_w_pool, w_sgu=m_w_sgu, pool_scale=m_pool_scale, sgu_ln_g=m_sgu_ln_g,
                   sgu_ln_b=m_sgu_ln_b, b_sgu=m_b_sgu, ln_g=m_ln_g, ln_b=m_ln_b)
    small_v = dict(w_pool=v_w_pool, w_sgu=v_w_sgu, pool_scale=v_pool_scale, sgu_ln_g=v_sgu_ln_g,
                   sgu_ln_b=v_sgu_ln_b, b_sgu=v_b_sgu, ln_g=v_ln_g, ln_b=v_ln_b)

    wint_loc = jnp.transpose(w_in, (0, 2, 1)).astype(BF16)
    wout_loc = w_out.astype(BF16)
    act_slots, mod, wint0, wout0 = _prep(c, w_ada, b_ada, [wint_loc[0], wout_loc[0]])
    act_all = act_slots[:, 0, :]
    w_int, w_outf = [wint0.reshape(D_IN, D_MODEL)], [wout0.reshape(D_MODEL, D_MODEL)]

    def layer_args(l):
        return (w_int[l], w_outf[l], w_pool[l], pool_scale[l].reshape(1, D_POOL), sgu_ln_g[l].reshape(1, D_SGU),
                sgu_ln_b[l].reshape(1, D_SGU), w_sgu[l], jnp.transpose(b_sgu[l]), ln_g[l].reshape(1, D_MODEL))

    acts, cur = [], x[0]
    for l in range(DEPTH):
        nxt = [wint_loc[l + 1], wout_loc[l + 1]] if l + 1 < DEPTH else []
        out, proj, y, *gathered = _layer_forward(cur, mod[l:l + 1], *layer_args(l), ln_b[l].reshape(1, D_MODEL),
                                                 f"layer_fwd_{l}", gather=nxt)
        if gathered:
            w_int.append(gathered[0].reshape(D_IN, D_MODEL))
            w_outf.append(gathered[1].reshape(D_MODEL, D_MODEL))
        acts.append((cur, proj, y))
        cur = out

    shard_in, shard_out = D_IN // N_DEV, D_MODEL // N_DEV
    a, b = cur, loss_target[0]
    smalls, dmods, loss_lanes = [None] * DEPTH, [None] * DEPTH, None
    g_w_in_t, g_w_out = [None] * DEPTH, [None] * DEPTH
    pending = []
    for l in reversed(range(DEPTH)):
        xin, proj, y = acts[l]
        dx, dproj, h, cat, dy, small, dmod, lanes, *shards = _layer_backward(
            a, b, xin, proj, y, mod[l:l + 1], *layer_args(l), l == DEPTH - 1, f"layer_bwd_{l}", reduce=pending)
        if shards:
            g_w_in_t[l + 1], g_w_out[l + 1] = shards
        if l == DEPTH - 1:
            loss_lanes = lanes
        if l > 0:
            pending = [_grad_matmul(dproj, h, 640, f"grad_w_in_{l}").reshape(4, 2, shard_in, D_MODEL),
                       _grad_matmul(cat, dy, 512, f"grad_w_out_{l}").reshape(4, 2, shard_out, D_MODEL)]
        smalls[l], dmods[l] = small, dmod
        a = b = dx
    grad_x = a[None]

    g_w_in_t[0], g_w_out[0], small_tot, dmod_slots, loss_tile = _grad_tail(
        dproj, h, cat, dy, jnp.concatenate(smalls, axis=0).reshape(4, 2, DEPTH * PACK_ROWS // N_DEV, 128),
        jnp.concatenate(dmods, axis=0).reshape(8, 3 * D_MODEL * DEPTH // 8), loss_lanes)
    loss = loss_tile[0, 0]
    dmod_all = dmod_slots.reshape(N_DEV, DEPTH, 3 * D_MODEL)
    gwin_t = jnp.stack(g_w_in_t)
    gwout = jnp.stack(g_w_out)

    cols_ada = w_ada.shape[2]
    dmod_cols = jnp.transpose(lax.dynamic_slice_in_dim(dmod_all, mine * cols_ada, cols_ada, axis=2), (1, 0, 2))
    g_w_ada, d_w_ada, nm_w_ada, nv_w_ada = _adamw_ada(w_ada, m_w_ada, v_w_ada, jnp.transpose(act_all), dmod_cols,
                                                      "adamw_w_ada")
    g_b_ada, d_b_ada, nm_b_ada, nv_b_ada = _adamw_bias(b_ada, m_b_ada, v_b_ada, dmod_all, "adamw_b_ada")
    flat = lambda t: t.reshape(-1, t.shape[-1])
    to_t = lambda t: flat(jnp.transpose(t, (0, 2, 1)))
    from_t = lambda t: jnp.transpose(t.reshape(DEPTH, shard_in, D_MODEL), (0, 2, 1))
    d_w_in, nm_w_in, nv_w_in = [from_t(t) for t in
                                _adamw(to_t(w_in), flat(gwin_t), to_t(m_w_in), to_t(v_w_in), 128, "adamw_w_in")]
    g_w_in = from_t(gwin_t)
    d_w_out, nm_w_out, nv_w_out = [t.reshape(w_out.shape) for t in
                                   _adamw(flat(w_out), flat(gwout), flat(m_w_out), flat(v_w_out), 128, "adamw_w_out")]
    rows128 = lambda t: t.reshape(-1, 128)
    small_out = _adamw_small(small_tot.reshape(DEPTH * PACK_ROWS, 128), [rows128(small_w[n]) for n in SMALL_NAMES],
                             [rows128(small_m[n]) for n in SMALL_NAMES], [rows128(small_v[n]) for n in SMALL_NAMES],
                             "adamw_small")
    gs, ds, ms, vs = [{n: t.reshape(small_w[n].shape) for n, t in zip(SMALL_NAMES, group)} for group in small_out]

    def ordered(w_ada_, b_ada_, w_in_, small, w_out_):
        return (w_ada_, b_ada_, w_in_, small["w_pool"], small["pool_scale"], small["sgu_ln_g"], small["sgu_ln_b"],
                small["w_sgu"], small["b_sgu"], w_out_, small["ln_g"], small["ln_b"])

    return (loss, grad_x,
            *ordered(g_w_ada, g_b_ada, g_w_in, gs, gwout),
            *ordered(d_w_ada, d_b_ada, d_w_in, ds, d_w_out),
            *ordered(nm_w_ada, nm_b_ada, nm_w_in, ms, nm_w_out),
            *ordered(nv_w_ada, nv_b_ada, nv_w_in, vs, nv_w_out))
```

```python
import jax
import jax.numpy as jnp
from jax import lax
from jax.experimental import pallas as pl
from jax.experimental.pallas import tpu as pltpu

F32 = jnp.float32
BF16 = jnp.bfloat16

D_MODEL = 1024
SEQ = 2048
DEPTH = 2
D_POOL = 512
D_SGU = 512
D_IN = 2560
N_GROUPS = 4
GROUP = 128
N_HEADS = 4
HEAD = 128
CHUNK = 128
WINDOWS = (2, 4, 8, 16)
ALPHA = (2.0 * DEPTH) ** 0.25
LN_EPS = 1e-5
N_DEV = 8

ADAM_LR = 0.001
ADAM_B1 = 0.9
ADAM_B2 = 0.999
ADAM_EPS = 1e-08
ADAM_WD = 0.01
ADAM_STEP = 10

TM = 256
HALO = 16
N_TILES = SEQ // TM
VMEM_LIMIT = 60 * 1024 * 1024

ROW_WPOOL = 0
ROW_WSGU = 512
ROW_PSCALE = 1024
ROW_SLNG = 1028
ROW_SLNB = 1032
ROW_BSGU = 1036
ROW_LNG = 1040
ROW_LNB = 1048
PACK_ROWS = 1088
DMOD_COLS = DEPTH * 3 * D_MODEL // 8

SQRT_HALF = 0.7071067811865476
INV_SQRT_2PI = 0.3989422804014327


def _ln(x):
    mu = jnp.mean(x, axis=-1, keepdims=True)
    xc = x - mu
    var = jnp.mean(xc * xc, axis=-1, keepdims=True)
    rstd = lax.rsqrt(var + LN_EPS)
    return xc * rstd, rstd


def _ln_bwd(dxn, xn, rstd):
    m1 = jnp.mean(dxn, axis=-1, keepdims=True)
    m2 = jnp.mean(dxn * xn, axis=-1, keepdims=True)
    return rstd * (dxn - m1 - xn * m2)


def _gelu_parts(x, with_grad):
    cdf = 0.5 * (1.0 + lax.erf(x * SQRT_HALF))
    if not with_grad:
        return x * cdf, None
    return x * cdf, cdf + x * (INV_SQRT_2PI * jnp.exp(-0.5 * x * x))


def _silu_parts(x):
    s = jax.nn.sigmoid(x)
    return x * s, s * (1.0 + x * (1.0 - s))


def _dot(a, b):
    return lax.dot_general(a, b, (((1,), (0,)), ((), ())), preferred_element_type=F32)


def _dot_nt(a, b):
    return lax.dot_general(a, b, (((1,), (1,)), ((), ())), preferred_element_type=F32)


def _dot_tn(a, b):
    return lax.dot_general(a, b, (((0,), (0,)), ((), ())), preferred_element_type=F32)


def _row_index(tile):
    return tile * TM + lax.broadcasted_iota(jnp.int32, (TM, 1), 0)


def _window_sums(ext, forward):
    n = TM + HALO
    cur = ext
    outs = []
    for g in range(N_GROUPS):
        step = 1 << g
        cur = cur + pltpu.roll(cur, step if forward else n - step, 0)
        rows = cur[HALO:, :GROUP] if forward else cur[:TM, :GROUP]
        outs.append(rows)
        cur = cur[:, GROUP:] if g + 1 < N_GROUPS else None
    return outs


def _inverse_counts(rows):
    return [1.0 / jnp.minimum(rows + 1, w).astype(F32) for w in WINDOWS]


def _tril_bf16(w):
    t = lax.broadcasted_iota(jnp.int32, (CHUNK, CHUNK), 0)
    s = lax.broadcasted_iota(jnp.int32, (CHUNK, CHUNK), 1)
    return jnp.where(t >= s, w, 0.0).astype(BF16)


class _MixWeights:
    def __init__(self, layer, wpool_ref, pscale_ref, slng_ref, slnb_ref, wsgu_ref, bsgut_ref):
        self.layer = layer
        self.wpool_ref, self.pscale_ref, self.slng_ref, self.slnb_ref = wpool_ref, pscale_ref, slng_ref, slnb_ref
        self.wsgu_ref, self.bsgut_ref = wsgu_ref, bsgut_ref

    def pool(self, g):
        return self.wpool_ref[self.layer, g].astype(BF16)

    def pool_scale(self, g):
        return self.pscale_ref[self.layer:self.layer + 1, g * GROUP:(g + 1) * GROUP]

    def ln_gain(self, h):
        return self.slng_ref[self.layer, h:h + 1, :]

    def ln_bias(self, h):
        return self.slnb_ref[self.layer, h:h + 1, :]

    def mix(self, h):
        return _tril_bf16(self.wsgu_ref[self.layer, h])

    def mix_bias(self, h):
        return self.bsgut_ref[self.layer, :, h:h + 1]


SMALL_SPECS = ((DEPTH, N_GROUPS, GROUP, GROUP), (DEPTH, D_POOL), (DEPTH, N_HEADS, HEAD), (DEPTH, N_HEADS, HEAD),
               (DEPTH, N_HEADS, CHUNK, CHUNK), (DEPTH, CHUNK, N_HEADS))


def _mix_forward(proj, halo, tile, w, keep):
    rows = _row_index(tile)
    inv_counts = _inverse_counts(rows)
    xa = proj[:, 0:D_POOL]
    ga = proj[:, D_POOL:2 * D_POOL]
    sums = _window_sums(jnp.concatenate([halo, xa], axis=0), True)
    ga_act, ga_grad = _silu_parts(ga)
    pooled, pw, ya = [], [], []
    for g in range(N_GROUPS):
        sl = slice(g * GROUP, (g + 1) * GROUP)
        p = (sums[g] * inv_counts[g] - xa[:, sl]).astype(BF16)
        q = _dot(p, w.pool(g))
        pooled.append(p)
        pw.append(q)
        ya.append(q * w.pool_scale(g) * ga_act[:, sl])

    u = proj[:, 2 * D_POOL:2 * D_POOL + D_SGU]
    v = proj[:, 2 * D_POOL + D_SGU:2 * D_POOL + 2 * D_SGU]
    gb = proj[:, 2 * D_POOL + 2 * D_SGU:]
    gb_act, gb_grad = _silu_parts(gb)
    u_act, u_grad = _gelu_parts(u, keep)
    v_act, v_grad = _gelu_parts(v, keep)
    vn, vrstd, vln, mixed, yb = [], [], [], [], []
    for h in range(N_HEADS):
        sl = slice(h * HEAD, (h + 1) * HEAD)
        n_h, r_h = _ln(v_act[:, sl])
        l_h = (n_h * w.ln_gain(h) + w.ln_bias(h)).astype(BF16)
        w_h = w.mix(h)
        bias = w.mix_bias(h)
        m_h = jnp.concatenate(
            [_dot(w_h, l_h[k * CHUNK:(k + 1) * CHUNK]) + bias for k in range(TM // CHUNK)], axis=0)
        vn.append(n_h)
        vrstd.append(r_h)
        vln.append(l_h)
        mixed.append(m_h)
        yb.append(u_act[:, sl] * m_h * gb_act[:, sl])
    cat = jnp.concatenate(ya + yb, axis=1)
    if not keep:
        return cat, None
    return cat, dict(inv_counts=inv_counts, ga_act=ga_act, ga_grad=ga_grad, pooled=pooled, pw=pw, u_grad=u_grad,
                     v_grad=v_grad, u_act=u_act, gb_act=gb_act, gb_grad=gb_grad, vn=vn, vrstd=vrstd, vln=vln,
                     mixed=mixed)


def _const_spec(shape):
    nd = len(shape)
    return pl.BlockSpec(shape, lambda i: (0,) * nd)


def _layer_forward(layer, x, mod, w_int, w_outf, small, ln_g, ln_b, name, gather=()):
    n_gather = len(gather)

    def body(x_ref, mod_ref, wint_ref, wout_ref, wpool_ref, pscale_ref, slng_ref, slnb_ref, wsgu_ref, bsgut_ref,
             lng_ref, lnb_ref, *rest):
        weights = _MixWeights(layer, wpool_ref, pscale_ref, slng_ref, slnb_ref, wsgu_ref, bsgut_ref)
        loc_refs, rest = rest[:n_gather], rest[n_gather:]
        out_ref, proj_ref, y_ref = rest[:3]
        full_refs, rest = rest[3:3 + n_gather], rest[3 + n_gather:]
        halo_ref = rest[0]
        tile = pl.program_id(0)

        def gathers():
            g_send, g_recv, _ = rest[1:]
            return [_TwoLevelGather(full_refs[n], g_send.at[n], g_recv.at[n], src=loc_refs[n])
                    for n in range(n_gather)]

        def own_copies():
            mine = _index(_me())
            return [pltpu.make_async_copy(loc_refs[n], full_refs[n].at[mine], rest[3].at[n]) for n in range(n_gather)]

        @pl.when(tile == 0)
        def _():
            halo_ref[...] = jnp.zeros_like(halo_ref)
            if n_gather:
                for cp in own_copies():
                    cp.start()
                for g in gathers():
                    g.send_mine()

        if n_gather:
            @pl.when(tile == N_TILES // 2 - 1)
            def _():
                for g in gathers():
                    g.relay()

            @pl.when(tile == N_TILES - 2)
            def _():
                for g in gathers():
                    g.pass_near()

        xt = x_ref[...]
        shift = mod_ref[layer:layer + 1, 0:D_MODEL]
        scale = mod_ref[layer:layer + 1, D_MODEL:2 * D_MODEL]
        gate = mod_ref[layer:layer + 1, 2 * D_MODEL:]
        xn, _ = _ln(xt)
        h = (xn * (1.0 + scale) + shift).astype(BF16)
        proj = _dot_nt(h, wint_ref[...])
        proj_ref[...] = proj
        cat, _ = _mix_forward(proj, halo_ref[...], tile, weights, False)
        halo_ref[...] = proj[TM - HALO:, 0:D_POOL]
        y = _dot(cat.astype(BF16), wout_ref[...])
        y_ref[...] = y
        zn, _ = _ln(ALPHA * xt + gate * y)
        out_ref[...] = zn * lng_ref[layer:layer + 1, :] + lnb_ref[layer:layer + 1, :]

        if n_gather:
            @pl.when(tile == N_TILES - 1)
            def _():
                for g in gathers():
                    g.pass_far()
                for g in gathers():
                    g.wait_rest()
                for g in gathers():
                    g.wait_sends()
                for cp in own_copies():
                    cp.wait()

    row = lambda w: pl.BlockSpec((TM, w), lambda i: (i, 0))
    comm_scratch = [pltpu.SemaphoreType.DMA((n_gather, GATHER_SEMS)), pltpu.SemaphoreType.DMA((n_gather, GATHER_SEMS)),
                    pltpu.SemaphoreType.DMA((n_gather,))] if n_gather else []
    return pl.pallas_call(
        body,
        name=name,
        grid=(N_TILES,),
        in_specs=[row(D_MODEL), _const_spec((DEPTH, 3 * D_MODEL)), _const_spec((D_IN, D_MODEL)),
                  _const_spec((D_MODEL, D_MODEL))] + [_const_spec(s) for s in SMALL_SPECS]
                 + [_const_spec((DEPTH, D_MODEL)), _const_spec((DEPTH, D_MODEL))] + [ANY] * n_gather,
        out_specs=[row(D_MODEL), row(D_IN), row(D_MODEL)] + [ANY] * n_gather,
        out_shape=[jax.ShapeDtypeStruct((SEQ, D_MODEL), F32), jax.ShapeDtypeStruct((SEQ, D_IN), F32),
                   jax.ShapeDtypeStruct((SEQ, D_MODEL), F32)]
                  + [jax.ShapeDtypeStruct((N_DEV,) + g.shape, g.dtype) for g in gather],
        scratch_shapes=[pltpu.VMEM((HALO, D_POOL), F32)] + comm_scratch,
        compiler_params=pltpu.CompilerParams(dimension_semantics=("arbitrary",), vmem_limit_bytes=VMEM_LIMIT),
    )(x, mod, w_int, w_outf, *small, ln_g, ln_b, *gather)


VEC_LNG, VEC_LNB, VEC_POOL, VEC_SGU, VEC_SHIFT, VEC_SCALE, VEC_GATE, VEC_LOSS = range(8)


def _layer_backward(layer, a, b, x, proj, y, mod, w_int, w_outf, small, ln_g, is_last, name, carry=(), reduce=()):
    n_red, n_carry = len(reduce), len(carry)
    base = layer * PACK_ROWS

    def body(a_ref, b_ref, x_ref, proj_ref, prev_ref, y_ref, mod_ref, wint_ref, wout_ref, wpool_ref, pscale_ref,
             slng_ref, slnb_ref, wsgu_ref, bsgut_ref, lng_ref, *rest):
        weights = _MixWeights(layer, wpool_ref, pscale_ref, slng_ref, slnb_ref, wsgu_ref, bsgut_ref)
        carry_refs, rest = rest[:n_carry], rest[n_carry:]
        part_refs, rest = rest[:n_red], rest[n_red:]
        dx_ref, dproj_ref, h_ref, cat_ref, dy_ref, small_ref, dmod_ref, loss_ref = rest[:8]
        shard_refs, rest = rest[8:8 + n_red], rest[8 + n_red:]
        vec_ref, dmix_ref, halo_ref = rest[:3]
        step = pl.program_id(0)
        tile = N_TILES - 1 - step

        def scatter():
            bufs, sems = rest[3:3 + 5 * n_red], rest[3 + 5 * n_red:]
            arrays = [dict(part=part_refs[n], out=shard_refs[n], staged=True, stage=bufs[5 * n], sib=bufs[5 * n + 1],
                           snd=bufs[5 * n + 2], rcv=bufs[5 * n + 3], relay=bufs[5 * n + 4]) for n in range(n_red)]
            return _ChipReduceScatter(arrays, *sems)

        @pl.when(step == 0)
        def _():
            small_ref[...] = jnp.zeros_like(small_ref)
            dmod_ref[...] = jnp.zeros_like(dmod_ref)
            vec_ref[...] = jnp.zeros_like(vec_ref)
            dmix_ref[...] = jnp.zeros_like(dmix_ref)
            halo_ref[...] = jnp.zeros_like(halo_ref)
            if n_red:
                scatter().start()

        if n_red:
            @pl.when(step == 1)
            def _():
                scatter().exchange()

            @pl.when(step == N_TILES // 2)
            def _():
                scatter().fold()

        def acc(row, lo, val):
            hi = lo + val.shape[1]
            vec_ref[row:row + 1, lo:hi] += jnp.sum(val, axis=0, keepdims=True)

        xt = x_ref[...]
        yt = y_ref[...]
        shift = mod_ref[layer:layer + 1, 0:D_MODEL]
        scale = mod_ref[layer:layer + 1, D_MODEL:2 * D_MODEL]
        gate = mod_ref[layer:layer + 1, 2 * D_MODEL:]
        ln_gain = lng_ref[layer:layer + 1, :]

        zn, zrstd = _ln(ALPHA * xt + gate * yt)
        if is_last:
            diff = a_ref[...] - b_ref[...]
            acc(VEC_LOSS, 0, diff * diff)
            dout = diff * (1.0 / D_MODEL)
        else:
            dout = a_ref[...]
        acc(VEC_LNG, 0, dout * zn)
        acc(VEC_LNB, 0, dout)
        dz = _ln_bwd(dout * ln_gain, zn, zrstd)
        acc(VEC_GATE, 0, dz * yt)
        dy = (dz * gate).astype(BF16)
        dy_ref[...] = dy
        dcat = _dot_nt(dy, wout_ref[...])

        proj = proj_ref[...]
        prev = jnp.where(tile > 0, prev_ref[...], 0.0)
        cat, k = _mix_forward(proj, prev, tile, weights, True)
        cat_ref[...] = cat.astype(BF16)

        dga, dq = [], []
        for g in range(N_GROUPS):
            sl = slice(g * GROUP, (g + 1) * GROUP)
            pscale = weights.pool_scale(g)
            dya = dcat[:, sl]
            dyp = dya * k["ga_act"][:, sl]
            dga.append(dya * k["pw"][g] * pscale * k["ga_grad"][:, sl])
            acc(VEC_POOL, g * GROUP, dyp * k["pw"][g])
            dpw = (dyp * pscale).astype(BF16)
            rows = pl.ds(base + ROW_WPOOL + g * GROUP, GROUP)
            small_ref[rows, :] += _dot_tn(k["pooled"][g], dpw)
            dq.append(_dot_nt(dpw, weights.pool(g)))
        dpooled = jnp.concatenate(dq, axis=1)
        scaled = jnp.concatenate([dq[g] * k["inv_counts"][g] for g in range(N_GROUPS)], axis=1)
        sums = _window_sums(jnp.concatenate([scaled, halo_ref[...]], axis=0), False)
        halo_ref[...] = scaled[0:HALO]
        dxa = jnp.concatenate(sums, axis=1) - dpooled

        du, dv, dgb = [], [], []
        for h in range(N_HEADS):
            sl = slice(h * HEAD, (h + 1) * HEAD)
            dyb = dcat[:, D_POOL + h * HEAD:D_POOL + (h + 1) * HEAD]
            m_h = k["mixed"][h]
            ug = k["u_act"][:, sl] * dyb
            du.append(dyb * m_h * k["gb_act"][:, sl] * k["u_grad"][:, sl])
            dgb.append(ug * m_h * k["gb_grad"][:, sl])
            dmixed = ug * k["gb_act"][:, sl]
            dmixed_bf = dmixed.astype(BF16)
            w_h = weights.mix(h)
            dvln_parts = []
            dmix_sum = dmix_ref[h]
            wsgu_rows = pl.ds(base + ROW_WSGU + h * CHUNK, CHUNK)
            dws = small_ref[wsgu_rows, :]
            for c in range(TM // CHUNK):
                cs = slice(c * CHUNK, (c + 1) * CHUNK)
                dmix_sum = dmix_sum + dmixed[cs]
                dws = dws + _dot_nt(dmixed_bf[cs], k["vln"][h][cs])
                dvln_parts.append(_dot_tn(w_h, dmixed_bf[cs]))
            dmix_ref[h] = dmix_sum
            small_ref[wsgu_rows, :] = dws
            dvln = jnp.concatenate(dvln_parts, axis=0)
            acc(VEC_SGU, h * HEAD, dvln * k["vn"][h])
            acc(VEC_SGU, D_SGU + h * HEAD, dvln)
            dvv = _ln_bwd(dvln * weights.ln_gain(h), k["vn"][h], k["vrstd"][h])
            dv.append(dvv * k["v_grad"][:, sl])

        dproj = jnp.concatenate([dxa] + dga + du + dv + dgb, axis=1).astype(BF16)
        dproj_ref[...] = dproj
        dh = _dot(dproj, wint_ref[...])

        xn, xrstd = _ln(xt)
        h_ref[...] = (xn * (1.0 + scale) + shift).astype(BF16)
        acc(VEC_SCALE, 0, dh * xn)
        acc(VEC_SHIFT, 0, dh)
        dx_ref[...] = _ln_bwd(dh * (1.0 + scale), xn, xrstd) + ALPHA * dz

        @pl.when(step == N_TILES - 1)
        def _():
            def put(row0, vec_row, lo, n):
                for r in range(n):
                    small_ref[base + row0 + r:base + row0 + r + 1, :] = (
                        vec_ref[vec_row:vec_row + 1, lo + r * 128:lo + (r + 1) * 128])

            put(ROW_PSCALE, VEC_POOL, 0, 4)
            put(ROW_SLNG, VEC_SGU, 0, 4)
            put(ROW_SLNB, VEC_SGU, D_SGU, 4)
            put(ROW_LNG, VEC_LNG, 0, 8)
            put(ROW_LNB, VEC_LNB, 0, 8)
            ones = jnp.ones((8, HEAD), F32)
            t = lax.broadcasted_iota(jnp.int32, (CHUNK, CHUNK), 0)
            s = lax.broadcasted_iota(jnp.int32, (CHUNK, CHUNK), 1)
            for h in range(N_HEADS):
                bias_rows = lax.dot_general(ones, dmix_ref[h], (((1,), (1,)), ((), ())),
                                            preferred_element_type=F32, precision=lax.Precision.HIGHEST)
                small_ref[base + ROW_BSGU + h:base + ROW_BSGU + h + 1, :] = bias_rows[0:1]
                rows = pl.ds(base + ROW_WSGU + h * CHUNK, CHUNK)
                small_ref[rows, :] = jnp.where(t >= s, small_ref[rows, :], 0.0)
            pieces = ((0, VEC_SHIFT, 0, 768),
                      (1, VEC_SHIFT, 768, 256), (1, VEC_SCALE, 0, 512),
                      (2, VEC_SCALE, 512, 512), (2, VEC_GATE, 0, 256),
                      (3, VEC_GATE, 256, 768))
            filled = [0] * 4
            for q, vec_row, lo, n in pieces:
                row = 4 * layer + q
                dmod_ref[row:row + 1, filled[q]:filled[q] + n] = vec_ref[vec_row:vec_row + 1, lo:lo + n]
                filled[q] += n
            if n_carry:
                for other in range(layer + 1, DEPTH):
                    rows = pl.ds(other * PACK_ROWS, PACK_ROWS)
                    small_ref[rows, :] = carry_refs[0][rows, :]
                    dmod_ref[4 * other:4 * other + 4, :] = carry_refs[1][4 * other:4 * other + 4, :]
            loss_ref[...] = vec_ref[VEC_LOSS:VEC_LOSS + 1, :]
            if n_red:
                scatter().finish()
                scatter().wait_sends()

    rev = lambda w: pl.BlockSpec((TM, w), lambda i: (N_TILES - 1 - i, 0))
    prev_spec = pl.BlockSpec(
        (HALO, D_POOL), lambda i: (jnp.maximum((N_TILES - 1 - i) * (TM // HALO) - 1, 0), 0))
    comm_scratch = []
    for p in reduce:
        comm_scratch += _ChipReduceScatter.buffers(p.shape[2], p.shape[3], p.dtype)
    if n_red:
        comm_scratch += _ChipReduceScatter.semaphores(n_red)
    return pl.pallas_call(
        body,
        name=name,
        grid=(N_TILES,),
        in_specs=[rev(D_MODEL), rev(D_MODEL) if is_last else pl.BlockSpec((TM, D_MODEL), lambda i: (0, 0)),
                  rev(D_MODEL), rev(D_IN), prev_spec, rev(D_MODEL),
                  _const_spec((DEPTH, 3 * D_MODEL)), _const_spec((D_IN, D_MODEL)), _const_spec((D_MODEL, D_MODEL))]
                 + [_const_spec(s) for s in SMALL_SPECS] + [_const_spec((DEPTH, D_MODEL))]
                 + [_const_spec(c.shape) for c in carry] + [ANY] * n_red,
        out_specs=[rev(D_MODEL), rev(D_IN), rev(D_MODEL), rev(D_MODEL), rev(D_MODEL),
                   _const_spec((DEPTH * PACK_ROWS, 128)), _const_spec((8, DMOD_COLS)), _const_spec((1, D_MODEL))]
                  + [_const_spec(p.shape[2:]) for p in reduce],
        out_shape=[jax.ShapeDtypeStruct((SEQ, D_MODEL), F32), jax.ShapeDtypeStruct((SEQ, D_IN), BF16),
                   jax.ShapeDtypeStruct((SEQ, D_MODEL), BF16), jax.ShapeDtypeStruct((SEQ, D_MODEL), BF16),
                   jax.ShapeDtypeStruct((SEQ, D_MODEL), BF16), jax.ShapeDtypeStruct((DEPTH * PACK_ROWS, 128), F32),
                   jax.ShapeDtypeStruct((8, DMOD_COLS), F32), jax.ShapeDtypeStruct((1, D_MODEL), F32)]
                  + [jax.ShapeDtypeStruct(p.shape[2:], F32) for p in reduce],
        scratch_shapes=[pltpu.VMEM((8, D_MODEL), F32), pltpu.VMEM((N_HEADS, CHUNK, HEAD), F32),
                        pltpu.VMEM((HALO, D_POOL), F32)] + comm_scratch,
        compiler_params=pltpu.CompilerParams(dimension_semantics=("arbitrary",), vmem_limit_bytes=VMEM_LIMIT),
    )(a, b, x, proj, proj, y, mod, w_int, w_outf, *small, ln_g, *carry, *reduce)


def _grad_matmul(lhs, rhs, block_cols, name):
    m, n = lhs.shape[1], rhs.shape[1]

    def body(lhs_ref, rhs_ref, out_ref):
        out_ref[...] = _dot_tn(lhs_ref[...], rhs_ref[...]).astype(BF16)

    return pl.pallas_call(
        body,
        name=name,
        grid=(m // block_cols,),
        in_specs=[pl.BlockSpec((SEQ, block_cols), lambda j: (0, j)), pl.BlockSpec((SEQ, n), lambda j: (0, 0))],
        out_specs=pl.BlockSpec((block_cols, n), lambda j: (j, 0)),
        out_shape=jax.ShapeDtypeStruct((m, n), BF16),
        compiler_params=pltpu.CompilerParams(dimension_semantics=("arbitrary",), vmem_limit_bytes=VMEM_LIMIT),
    )(lhs, rhs)


def _adamw_math(w, g, m, v):
    m = ADAM_B1 * m + (1.0 - ADAM_B1) * g
    v = ADAM_B2 * v + (1.0 - ADAM_B2) * (g * g)
    m_hat = m / (1.0 - ADAM_B1 ** ADAM_STEP)
    v_hat = v / (1.0 - ADAM_B2 ** ADAM_STEP)
    delta = -ADAM_LR * (m_hat / (jnp.sqrt(v_hat) + ADAM_EPS) + ADAM_WD * w)
    return delta, m, v


def _adamw(w, grads, m, v, block_rows, name):
    rows, cols = grads[0].shape
    blocks = rows // block_rows

    def body(w_ref, m_ref, v_ref, *rest):
        g_refs, (g_ref, d_ref, nm_ref, nv_ref) = rest[:DEPTH], rest[DEPTH:]
        for layer in range(DEPTH):
            @pl.when(pl.program_id(0) == layer)
            def _():
                g = g_refs[layer][...]
                g_ref[...] = g
                d_ref[...], nm_ref[...], nv_ref[...] = _adamw_math(w_ref[...], g, m_ref[...], v_ref[...])

    def grad_spec(layer):
        return pl.BlockSpec((block_rows, cols),
                            lambda l, i: (jnp.where(l == layer, i, jnp.where(l < layer, 0, blocks - 1)), 0))

    spec = pl.BlockSpec((block_rows, cols), lambda l, i: (l * blocks + i, 0))
    return pl.pallas_call(
        body,
        name=name,
        grid=(DEPTH, blocks),
        in_specs=[spec] * 3 + [grad_spec(layer) for layer in range(DEPTH)],
        out_specs=[spec] * 4,
        out_shape=[jax.ShapeDtypeStruct(w.shape, F32)] * 4,
        compiler_params=pltpu.CompilerParams(dimension_semantics=("arbitrary", "arbitrary"),
                                             vmem_limit_bytes=VMEM_LIMIT),
    )(w, m, v, *grads)


def _adamw_ada(w, m, v, act_t, dmod_cols, name):
    cols = w.shape[2]

    rows = 256

    def body(w_ref, m_ref, v_ref, act_ref, dmod_ref, g_ref, d_ref, nm_ref, nv_ref):
        act = act_ref[...]
        dm = dmod_ref[0]
        g = act[:, 0:1] * dm[0:1, :]
        for b in range(1, N_DEV):
            g = g + act[:, b:b + 1] * dm[b:b + 1, :]
        g_ref[0] = g
        d_ref[0], nm_ref[0], nv_ref[0] = _adamw_math(w_ref[0], g, m_ref[0], v_ref[0])

    spec = pl.BlockSpec((1, rows, cols), lambda l, i: (l, i, 0))
    return pl.pallas_call(
        body,
        name=name,
        grid=(DEPTH, D_MODEL // rows),
        in_specs=[spec, spec, spec, pl.BlockSpec((rows, N_DEV), lambda l, i: (i, 0)),
                  pl.BlockSpec((1, N_DEV, cols), lambda l, i: (l, 0, 0))],
        out_specs=[spec] * 4,
        out_shape=[jax.ShapeDtypeStruct(w.shape, F32)] * 4,
        compiler_params=pltpu.CompilerParams(dimension_semantics=("arbitrary", "arbitrary"),
                                             vmem_limit_bytes=VMEM_LIMIT),
    )(w, m, v, act_t, dmod_cols)


def _adamw_bias(w, m, v, dmod_all, name):
    def body(w_ref, m_ref, v_ref, dmod_ref, g_ref, d_ref, nm_ref, nv_ref):
        g = dmod_ref[0]
        for b in range(1, N_DEV):
            g = g + dmod_ref[b]
        g_ref[...] = g
        d_ref[...], nm_ref[...], nv_ref[...] = _adamw_math(w_ref[...], g, m_ref[...], v_ref[...])

    return pl.pallas_call(
        body,
        name=name,
        out_shape=[jax.ShapeDtypeStruct(w.shape, F32)] * 4,
        compiler_params=pltpu.CompilerParams(vmem_limit_bytes=VMEM_LIMIT),
    )(w, m, v, dmod_all)


MESH = pl.DeviceIdType.MESH
SIBLING = 1
ANY = pl.BlockSpec(memory_space=pl.ANY)
VMEM = pl.BlockSpec(memory_space=pltpu.VMEM)


def _me():
    return lax.axis_index("x"), lax.axis_index("y"), lax.axis_index("c")


def _peer(r):
    x, y, c = _me()
    return (1 - x if r & 4 else x, 1 - y if r & 2 else y, 1 - c if r & 1 else c)


def _index(dev):
    return 4 * dev[0] + 2 * dev[1] + dev[2]


def _remote(src, dst, send_sem, recv_sem, dev):
    return pltpu.make_async_remote_copy(src_ref=src, dst_ref=dst, send_sem=send_sem, recv_sem=recv_sem,
                                        device_id=dev, device_id_type=MESH)


ACROSS_X, ACROSS_Y, ACROSS_BOTH = 4, 2, 6
GATHER_SEMS = 11


class _TwoLevelGather:
    def __init__(self, out, send_sems, recv_sems, src=None):
        self.out, self.send_sems, self.recv_sems, self.src = out, send_sems, recv_sems, src
        self.half = out.shape[1] // 2

    def _copy(self, k, block, part, to, src=None):
        slot = self.out.at[_index(block)]
        if part is not None:
            rows = pl.ds(part * self.half, self.half)
            slot = slot.at[rows]
            src = None if src is None else src.at[rows]
        return _remote(slot if src is None else src, slot, self.send_sems.at[k], self.recv_sems.at[k], to)

    def _mine(self):
        me = _me()
        src = self.out.at[_index(me)] if self.src is None else self.src
        x, y = _peer(ACROSS_X), _peer(ACROSS_Y)
        return [self._copy(1, me, 0, x, src), self._copy(3, me, 1, y, src), self._copy(2, me, 1, x, src),
                self._copy(4, me, 0, y, src), self._copy(0, me, None, _peer(SIBLING), src)]

    def _relayed(self):
        return [self._copy(5, _peer(ACROSS_X), 0, _peer(ACROSS_Y)), self._copy(6, _peer(ACROSS_Y), 1, _peer(ACROSS_X))]

    def _passed(self):
        sib, far = _peer(SIBLING), _peer(ACROSS_BOTH)
        return [self._copy(7, _peer(ACROSS_X), None, sib), self._copy(8, _peer(ACROSS_Y), None, sib),
                self._copy(9, far, 0, sib), self._copy(10, far, 1, sib)]

    def _arrival(self, k, r, part):
        return self._copy(k, _peer(r), part, _me())

    def send_mine(self):
        for cp in self._mine():
            cp.start()

    def relay(self):
        relayed = self._relayed()
        self._arrival(1, ACROSS_X, 0).wait_recv()
        relayed[0].start()
        self._arrival(3, ACROSS_Y, 1).wait_recv()
        relayed[1].start()

    def pass_near(self):
        passed = self._passed()
        self._arrival(2, ACROSS_X, 1).wait_recv()
        passed[0].start()
        self._arrival(4, ACROSS_Y, 0).wait_recv()
        passed[1].start()

    def pass_far(self):
        passed = self._passed()
        self._arrival(5, ACROSS_BOTH, 0).wait_recv()
        passed[2].start()
        self._arrival(6, ACROSS_BOTH, 1).wait_recv()
        passed[3].start()

    def pass_on(self):
        self.pass_near()
        self.pass_far()

    def wait_rest(self):
        self._arrival(0, SIBLING, None).wait_recv()
        self._arrival(7, ACROSS_X ^ SIBLING, None).wait_recv()
        self._arrival(8, ACROSS_Y ^ SIBLING, None).wait_recv()
        self._arrival(9, ACROSS_BOTH ^ SIBLING, 0).wait_recv()
        self._arrival(10, ACROSS_BOTH ^ SIBLING, 1).wait_recv()

    def wait_sends(self):
        for cp in self._mine() + self._relayed() + self._passed():
            cp.wait_send()


class _ChipReduceScatter:
    SLOTS = 6

    def __init__(self, arrays, l_sem, d_send, d_recv, i_send, i_recv):
        self.arrays = arrays
        self.l_sem, self.d_send, self.d_recv, self.i_send, self.i_recv = l_sem, d_send, d_recv, i_send, i_recv

    @staticmethod
    def buffers(rows, cols, dtype, staged=True):
        stage = [pltpu.VMEM((4, rows, cols), dtype)] if staged else []
        return stage + [pltpu.VMEM((4, rows, cols), dtype), pltpu.VMEM((3, rows, cols), dtype),
                        pltpu.VMEM((2, rows, cols), dtype), pltpu.VMEM((2, rows // 2, cols), dtype)]

    @classmethod
    def semaphores(cls, n):
        return [pltpu.SemaphoreType.DMA((n,)), pltpu.SemaphoreType.DMA((n,)), pltpu.SemaphoreType.DMA((n,)),
                pltpu.SemaphoreType.DMA((n, cls.SLOTS)), pltpu.SemaphoreType.DMA((n, cls.SLOTS))]

    def _pick(self, which):
        return list(enumerate(self.arrays)) if which is None else [(n, self.arrays[n]) for n in which]

    def _staging(self, which):
        c = _me()[2]
        return [pltpu.make_async_copy(a["part"].at[pl.ds(0, 4), c], a["stage"], self.l_sem.at[n])
                for n, a in self._pick(which) if a["staged"]]

    def _first(self, which):
        other = 1 - _me()[2]
        return [_remote(a["part"].at[pl.ds(0, 4), other], a["sib"], self.d_send.at[n], self.d_recv.at[n],
                        _peer(SIBLING)) for n, a in self._pick(which)]

    @staticmethod
    def _halves(a):
        half = a["rcv"].shape[1] // 2
        return pl.ds(0, half), pl.ds(half, half)

    def _hops(self, n, a):
        h0, h1 = self._halves(a)
        x, y = _peer(ACROSS_X), _peer(ACROSS_Y)
        snd, rcv, relay = a["snd"], a["rcv"], a["relay"]
        pairs = [(snd.at[2, h0], relay.at[0], x), (snd.at[2, h1], relay.at[1], y),
                 (snd.at[0, h0], rcv.at[0, h0], x), (snd.at[0, h1], rcv.at[0, h1], x),
                 (snd.at[1, h1], rcv.at[1, h1], y), (snd.at[1, h0], rcv.at[1, h0], y)]
        return [_remote(s, d, self.i_send.at[n, k], self.i_recv.at[n, k], to) for k, (s, d, to) in enumerate(pairs)]

    def _mine(self, a, chip, rows=None):
        src = a["stage"].at[chip] if a["staged"] else a["part"].at[chip, _me()[2]]
        mine, sib = (src[...], a["sib"][chip]) if rows is None else (src[rows, :], a["sib"][chip, rows, :])
        return mine.astype(F32) + sib.astype(F32)

    def start(self, which=None):
        for cp in self._staging(which) + self._first(which):
            cp.start()

    def exchange(self, which=None):
        for cp in self._staging(which):
            cp.wait()
        for cp in self._first(which):
            cp.wait_recv()
        chip = lambda dev: 2 * dev[0] + dev[1]
        across_x, across_y, far = chip(_peer(ACROSS_X)), chip(_peer(ACROSS_Y)), chip(_peer(ACROSS_BOTH))
        picked = self._pick(which)
        for n, a in picked:
            hops = self._hops(n, a)
            a["snd"][2] = self._mine(a, far).astype(a["snd"].dtype)
            hops[0].start()
            hops[1].start()
        for n, a in picked:
            h0, h1 = self._halves(a)
            hops = self._hops(n, a)
            dtype = a["snd"].dtype
            a["snd"][0, h0, :] = self._mine(a, across_x, h0).astype(dtype)
            hops[2].start()
            a["snd"][1, h1, :] = self._mine(a, across_y, h1).astype(dtype)
            hops[4].start()

    def fold(self, which=None):
        chip = lambda dev: 2 * dev[0] + dev[1]
        across_x, across_y = chip(_peer(ACROSS_X)), chip(_peer(ACROSS_Y))
        for n, a in self._pick(which):
            h0, h1 = self._halves(a)
            hops = self._hops(n, a)
            dtype = a["snd"].dtype
            hops[1].wait_recv()
            a["snd"][0, h1, :] = (self._mine(a, across_x, h1) + a["relay"][1].astype(F32)).astype(dtype)
            hops[3].start()
            hops[0].wait_recv()
            a["snd"][1, h0, :] = (self._mine(a, across_y, h0) + a["relay"][0].astype(F32)).astype(dtype)
            hops[5].start()

    def finish(self, which=None):
        x, y, _ = _me()
        home = 2 * x + y
        for n, a in self._pick(which):
            hops = self._hops(n, a)
            a["out"][...] = self._mine(a, home)
            hops[2].wait_recv()
            hops[3].wait_recv()
            a["out"][...] += a["rcv"][0].astype(F32)
            hops[4].wait_recv()
            hops[5].wait_recv()
            a["out"][...] += a["rcv"][1].astype(F32)

    def wait_sends(self, which=None):
        for cp in self._first(which):
            cp.wait_send()
        for n, a in self._pick(which):
            for cp in self._hops(n, a):
                cp.wait_send()


def _direct_exchange(src_of, dst_of, send_sems, recv_sems):
    me = _me()
    copies = [_remote(src_of(_peer(r)), dst_of(me), send_sems.at[r - 1], recv_sems.at[r - 1], _peer(r))
              for r in range(1, N_DEV)]
    for cp in copies:
        cp.start()
    return copies


def _wait_direct(copies):
    for cp in copies:
        cp.wait_recv()
    for cp in copies:
        cp.wait_send()


def _prep(c_row, w_ada, b_ada, blocks):
    cols = w_ada.shape[2]
    n_blocks = len(blocks)

    def body(c_ref, wada_ref, bada_ref, *rest):
        loc_refs, rest = rest[:n_blocks], rest[n_blocks:]
        act_all, mod_ref = rest[:2]
        full_refs, rest = rest[2:2 + n_blocks], rest[2 + n_blocks:]
        act_src, part, mod_recv, w_send, w_recv, w_local, a_send, a_recv, m_send, m_recv = rest
        me = _me()
        mine = _index(me)

        cval = c_ref[...]
        act_src[...] = jnp.zeros_like(act_src)
        act_src[0:1, :] = cval * jax.nn.sigmoid(cval)
        act_all[mine] = act_src[...]
        act_copies = _direct_exchange(lambda p: act_src, lambda m: act_all.at[_index(m)], a_send, a_recv)

        gathers, locals_ = [], []
        for n in range(n_blocks):
            own = pltpu.make_async_copy(loc_refs[n], full_refs[n].at[mine], w_local.at[n])
            own.start()
            locals_.append(own)
            g = _TwoLevelGather(full_refs[n], w_send.at[n], w_recv.at[n], src=loc_refs[n])
            g.send_mine()
            gathers.append(g)

        _wait_direct(act_copies)
        acts = jnp.concatenate([act_all[j, 0:1, :] for j in range(N_DEV)], axis=0)
        part[...] = jnp.zeros_like(part)
        for layer in range(DEPTH):
            res = lax.dot_general(acts, wada_ref[layer], (((1,), (0,)), ((), ())), preferred_element_type=F32,
                                  precision=lax.Precision.HIGHEST)
            for b in range(N_DEV):
                part[b, layer:layer + 1, :] = res[b:b + 1, :]
        mod_recv[mine] = part[mine]
        mod_copies = _direct_exchange(lambda p: part.at[_index(p)], lambda m: mod_recv.at[_index(m)], m_send, m_recv)

        for g in gathers:
            g.relay()
        for g in gathers:
            g.pass_on()
        for g in gathers:
            g.wait_rest()
        _wait_direct(mod_copies)
        for layer in range(DEPTH):
            for j in range(N_DEV):
                sl = slice(j * cols, (j + 1) * cols)
                mod_ref[layer:layer + 1, sl] = mod_recv[j, layer:layer + 1, :] + bada_ref[layer:layer + 1, sl]
        for g in gathers:
            g.wait_sends()
        for own in locals_:
            own.wait()

    return pl.pallas_call(
        body,
        name="prep_gather",
        in_specs=[VMEM, VMEM, VMEM] + [ANY] * n_blocks,
        out_specs=[VMEM, VMEM] + [ANY] * n_blocks,
        out_shape=[jax.ShapeDtypeStruct((N_DEV, 8, D_MODEL), F32), jax.ShapeDtypeStruct((DEPTH, 3 * D_MODEL), F32)]
                  + [jax.ShapeDtypeStruct((N_DEV,) + blk.shape, blk.dtype) for blk in blocks],
        scratch_shapes=[pltpu.VMEM((8, D_MODEL), F32), pltpu.VMEM((N_DEV, 8, cols), F32),
                        pltpu.VMEM((N_DEV, 8, cols), F32),
                        pltpu.SemaphoreType.DMA((n_blocks, GATHER_SEMS)),
                        pltpu.SemaphoreType.DMA((n_blocks, GATHER_SEMS)),
                        pltpu.SemaphoreType.DMA((n_blocks,)),
                        pltpu.SemaphoreType.DMA((7,)), pltpu.SemaphoreType.DMA((7,)),
                        pltpu.SemaphoreType.DMA((7,)), pltpu.SemaphoreType.DMA((7,))],
        compiler_params=pltpu.CompilerParams(vmem_limit_bytes=VMEM_LIMIT),
    )(c_row, w_ada, b_ada, *blocks)


def _grad_tail(dproj, h, cat, dy, small, dmod8, loss_lanes):
    shard_in, shard_out, shard_small = D_IN // N_DEV, D_MODEL // N_DEV, small.shape[2]
    W_IN, W_OUT, SMALL = 0, 1, 2

    def body(dproj_hbm, h_hbm, cat_hbm, dy_hbm, small_hbm, dmod_ref, lanes_ref,
             gwin_ref, gwout_ref, stot_ref, dmod_all, loss_ref,
             dproj_v, h_v, cat_v, dy_v, part_in, part_out, own_small, loss_src, loss_all, *rest):
        bufs, rest = rest[:13], rest[13:]
        load_sems, rs_sems = rest[0], rest[1:6]
        m_send, m_recv, g_send, g_recv, s_send, s_recv = rest[6:]
        mine = _index(_me())

        loads = [pltpu.make_async_copy(s, d, load_sems.at[n]) for n, (s, d) in enumerate(
            ((cat_hbm, cat_v), (dy_hbm, dy_v), (dproj_hbm, dproj_v), (h_hbm, h_v)))]
        for cp in loads:
            cp.start()
        arrays = [dict(part=part_in, out=gwin_ref, staged=False, sib=bufs[0], snd=bufs[1], rcv=bufs[2], relay=bufs[3]),
                  dict(part=part_out, out=gwout_ref, staged=False, sib=bufs[4], snd=bufs[5], rcv=bufs[6],
                       relay=bufs[7]),
                  dict(part=small_hbm, out=own_small, staged=True, stage=bufs[8], sib=bufs[9], snd=bufs[10],
                       rcv=bufs[11], relay=bufs[12])]
        scatter = _ChipReduceScatter(arrays, *rs_sems)
        scatter.start([SMALL])
        dmod_all[mine] = dmod_ref[...]
        dmod_copies = _direct_exchange(lambda p: dmod_ref, lambda m: dmod_all.at[_index(m)], m_send, m_recv)
        loss_src[...] = jnp.full(loss_src.shape, (0.5 / D_MODEL) * jnp.sum(lanes_ref[...]), F32)
        loss_all[mine] = loss_src[...]
        loss_copies = _direct_exchange(lambda p: loss_src, lambda m: loss_all.at[_index(m)], s_send, s_recv)

        loads[0].wait()
        loads[1].wait()
        for blk in range(2):
            res = _dot_tn(cat_v[:, blk * 512:(blk + 1) * 512], dy_v[...]).astype(BF16)
            for s in range(4):
                part_out[2 * blk + s // 2, s % 2] = res[s * shard_out:(s + 1) * shard_out]
        scatter.start([W_OUT])
        scatter.exchange([SMALL])

        gather = _TwoLevelGather(stot_ref, g_send, g_recv)
        loads[2].wait()
        loads[3].wait()
        for chip in range(4):
            res = _dot_tn(dproj_v[:, chip * 2 * shard_in:(chip + 1) * 2 * shard_in], h_v[...]).astype(BF16)
            part_in[chip, 0] = res[:shard_in]
            part_in[chip, 1] = res[shard_in:]
            if chip == 0:
                scatter.exchange([W_OUT])
                scatter.fold([SMALL])
            if chip == 1:
                scatter.fold([W_OUT])
                scatter.finish([SMALL])
                stot_ref[mine] = own_small[...]
                gather.send_mine()
            if chip == 2:
                gather.relay()
        scatter.start([W_IN])
        scatter.finish([W_OUT])
        scatter.exchange([W_IN])
        gather.pass_on()
        scatter.fold([W_IN])
        gather.wait_rest()
        _wait_direct(dmod_copies)
        _wait_direct(loss_copies)
        total = loss_all[0]
        for j in range(1, N_DEV):
            total = total + loss_all[j]
        loss_ref[...] = total
        scatter.finish([W_IN])
        gather.wait_sends()
        scatter.wait_sends()

    buffers = _ChipReduceScatter.buffers
    comm_scratch = (buffers(shard_in, D_MODEL, BF16, staged=False) + buffers(shard_out, D_MODEL, BF16, staged=False)
                    + buffers(shard_small, 128, F32))
    comm_scratch += [pltpu.SemaphoreType.DMA((4,))] + _ChipReduceScatter.semaphores(3)
    comm_scratch += [pltpu.SemaphoreType.DMA((n,)) for n in (7, 7, GATHER_SEMS, GATHER_SEMS, 7, 7)]
    return pl.pallas_call(
        body,
        name="grad_tail",
        in_specs=[ANY] * 5 + [VMEM, VMEM],
        out_specs=[VMEM] * 5,
        out_shape=[jax.ShapeDtypeStruct((shard_in, D_MODEL), F32), jax.ShapeDtypeStruct((shard_out, D_MODEL), F32),
                   jax.ShapeDtypeStruct((N_DEV, shard_small, 128), F32),
                   jax.ShapeDtypeStruct((N_DEV,) + dmod8.shape, F32), jax.ShapeDtypeStruct((8, 128), F32)],
        scratch_shapes=[pltpu.VMEM(dproj.shape, BF16), pltpu.VMEM(h.shape, BF16), pltpu.VMEM(cat.shape, BF16),
                        pltpu.VMEM(dy.shape, BF16), pltpu.VMEM((4, 2, shard_in, D_MODEL), BF16),
                        pltpu.VMEM((4, 2, shard_out, D_MODEL), BF16), pltpu.VMEM((shard_small, 128), F32),
                        pltpu.VMEM((8, 128), F32), pltpu.VMEM((N_DEV, 8, 128), F32)] + comm_scratch,
        compiler_params=pltpu.CompilerParams(vmem_limit_bytes=VMEM_LIMIT),
    )(dproj, h, cat, dy, small, dmod8, loss_lanes)


SMALL_NAMES = ("w_pool", "w_sgu", "pool_scale", "sgu_ln_g", "sgu_ln_b", "b_sgu", "ln_g", "ln_b")
SMALL_ROWS = (512, 512, 4, 4, 4, 4, 8, 8)


def _adamw_small(g_packed, ws, ms, vs, name):
    n = len(SMALL_NAMES)

    def body(g_ref, *refs):
        w_refs, m_refs, v_refs = refs[:n], refs[n:2 * n], refs[2 * n:3 * n]
        outs = refs[3 * n:]

        def update(p, at, g):
            delta, new_m, new_v = _adamw_math(w_refs[p][at], g, m_refs[p][at], v_refs[p][at])
            outs[p][at] = g
            outs[n + p][at] = delta
            outs[2 * n + p][at] = new_m
            outs[3 * n + p][at] = new_v

        row = 0
        for p, r in enumerate(SMALL_ROWS):
            shape = ws[p].shape
            for layer in range(DEPTH):
                first = layer * PACK_ROWS + row
                if len(shape) == 4:
                    for k in range(shape[1]):
                        update(p, (layer, k), g_ref[first + k * shape[2]:first + (k + 1) * shape[2], :])
                elif len(shape) == 3:
                    update(p, (layer,), g_ref[first:first + r, :])
                else:
                    g = jnp.concatenate([g_ref[first + k:first + k + 1, :] for k in range(r)], axis=1)
                    update(p, (slice(layer, layer + 1), slice(None)), g)
            row += r

    res = pl.pallas_call(
        body,
        name=name,
        out_shape=[jax.ShapeDtypeStruct(w.shape, F32) for w in ws] * 4,
        compiler_params=pltpu.CompilerParams(vmem_limit_bytes=VMEM_LIMIT),
    )(g_packed, *ws, *ms, *vs)
    return res[:n], res[n:2 * n], res[2 * n:3 * n], res[3 * n:]


def kernel(x, c, w_ada, b_ada, w_in, w_pool, pool_scale, sgu_ln_g, sgu_ln_b, w_sgu, b_sgu, w_out, ln_g, ln_b, loss_target, m_w_ada, m_b_ada, m_w_in, m_w_pool, m_pool_scale, m_sgu_ln_g, m_sgu_ln_b, m_w_sgu, m_b_sgu, m_w_out, m_ln_g, m_ln_b, v_w_ada, v_b_ada, v_w_in, v_w_pool, v_pool_scale, v_sgu_ln_g, v_sgu_ln_b, v_w_sgu, v_b_sgu, v_w_out, v_ln_g, v_ln_b):
    mine = _index(_me())
    small_w = dict(w_pool=w_pool, w_sgu=w_sgu, pool_scale=pool_scale, sgu_ln_g=sgu_ln_g, sgu_ln_b=sgu_ln_b,
                   b_sgu=b_sgu, ln_g=ln_g, ln_b=ln_b)
    small_m = dict(w_pool=m_w_pool, w_sgu=m_w_sgu, pool_scale=m_pool_scale, sgu_ln_g=m_sgu_ln_g,
                   sgu_ln_b=m_sgu_ln_b, b_sgu=m_b_sgu, ln_g=m_ln_g, ln_b=m_ln_b)
    small_v = dict(w_pool=v_w_pool, w_sgu=v_w_sgu, pool_scale=v_pool_scale, sgu_ln_g=v_sgu_ln_g,
                   sgu_ln_b=v_sgu_ln_b, b_sgu=v_b_sgu, ln_g=v_ln_g, ln_b=v_ln_b)

    wint_loc = jnp.transpose(w_in, (0, 2, 1)).astype(BF16)
    wout_loc = w_out.astype(BF16)
    act_slots, mod, wint0, wout0 = _prep(c, w_ada, b_ada, [wint_loc[0], wout_loc[0]])
    act_all = act_slots[:, 0, :]
    w_int, w_outf = [wint0.reshape(D_IN, D_MODEL)], [wout0.reshape(D_MODEL, D_MODEL)]
    small = (w_pool, pool_scale, sgu_ln_g, sgu_ln_b, w_sgu, jnp.transpose(b_sgu, (0, 2, 1)))

    acts, cur = [], x[0]
    for l in range(DEPTH):
        nxt = [wint_loc[l + 1], wout_loc[l + 1]] if l + 1 < DEPTH else []
        out, proj, y, *gathered = _layer_forward(l, cur, mod, w_int[l], w_outf[l], small, ln_g, ln_b,
                                                 f"layer_fwd_{l}", gather=nxt)
        if gathered:
            w_int.append(gathered[0].reshape(D_IN, D_MODEL))
            w_outf.append(gathered[1].reshape(D_MODEL, D_MODEL))
        acts.append((cur, proj, y))
        cur = out

    shard_in, shard_out = D_IN // N_DEV, D_MODEL // N_DEV
    a, b = cur, loss_target[0]
    loss_lanes, carry, pending = None, (), []
    g_w_in_t, g_w_out = [None] * DEPTH, [None] * DEPTH
    for l in reversed(range(DEPTH)):
        xin, proj, y = acts[l]
        dx, dproj, h, cat, dy, small_grads, dmod8, lanes, *shards = _layer_backward(
            l, a, b, xin, proj, y, mod, w_int[l], w_outf[l], small, ln_g, l == DEPTH - 1, f"layer_bwd_{l}",
            carry=carry, reduce=pending)
        if shards:
            g_w_in_t[l + 1], g_w_out[l + 1] = shards
        if l == DEPTH - 1:
            loss_lanes = lanes
        if l > 0:
            pending = [_grad_matmul(dproj, h, 640, f"grad_w_in_{l}").reshape(4, 2, shard_in, D_MODEL),
                       _grad_matmul(cat, dy, 512, f"grad_w_out_{l}").reshape(4, 2, shard_out, D_MODEL)]
        carry = (small_grads, dmod8)
        a = b = dx
    grad_x = a[None]

    g_w_in_t[0], g_w_out[0], small_tot, dmod_slots, loss_tile = _grad_tail(
        dproj, h, cat, dy, small_grads.reshape(4, 2, DEPTH * PACK_ROWS // N_DEV, 128), dmod8, loss_lanes)
    loss = loss_tile[0, 0]
    dmod_all = dmod_slots.reshape(N_DEV, DEPTH, 3 * D_MODEL)

    cols_ada = w_ada.shape[2]
    dmod_cols = jnp.transpose(lax.dynamic_slice_in_dim(dmod_all, mine * cols_ada, cols_ada, axis=2), (1, 0, 2))
    g_w_ada, d_w_ada, nm_w_ada, nv_w_ada = _adamw_ada(w_ada, m_w_ada, v_w_ada, jnp.transpose(act_all), dmod_cols,
                                                      "adamw_w_ada")
    g_b_ada, d_b_ada, nm_b_ada, nv_b_ada = _adamw_bias(b_ada, m_b_ada, v_b_ada, dmod_all, "adamw_b_ada")
    flat = lambda t: t.reshape(-1, t.shape[-1])
    to_t = lambda t: flat(jnp.transpose(t, (0, 2, 1)))
    from_t = lambda t: jnp.transpose(t.reshape(DEPTH, shard_in, D_MODEL), (0, 2, 1))
    g_w_in, d_w_in, nm_w_in, nv_w_in = [from_t(t) for t in _adamw(to_t(w_in), g_w_in_t, to_t(m_w_in), to_t(v_w_in),
                                                                  shard_in // 2, "adamw_w_in")]
    gwout, d_w_out, nm_w_out, nv_w_out = [t.reshape(w_out.shape) for t in _adamw(
        flat(w_out), g_w_out, flat(m_w_out), flat(v_w_out), shard_out, "adamw_w_out")]
    small_out = _adamw_small(small_tot.reshape(DEPTH * PACK_ROWS, 128), [small_w[n] for n in SMALL_NAMES],
                             [small_m[n] for n in SMALL_NAMES], [small_v[n] for n in SMALL_NAMES], "adamw_small")
    gs, ds, ms, vs = [dict(zip(SMALL_NAMES, group)) for group in small_out]

    def ordered(w_ada_, b_ada_, w_in_, small, w_out_):
        return (w_ada_, b_ada_, w_in_, small["w_pool"], small["pool_scale"], small["sgu_ln_g"], small["sgu_ln_b"],
                small["w_sgu"], small["b_sgu"], w_out_, small["ln_g"], small["ln_b"])

    return (loss, grad_x,
            *ordered(g_w_ada, g_b_ada, g_w_in, gs, gwout),
            *ordered(d_w_ada, d_b_ada, d_w_in, ds, d_w_out),
            *ordered(nm_w_ada, nm_b_ada, nm_w_in, ms, nm_w_out),
            *ordered(nv_w_ada, nv_b_ada, nv_w_in, vs, nv_w_out))
```

```python
import jax
import jax.numpy as jnp
from jax import lax
from jax.experimental import pallas as pl
from jax.experimental.pallas import tpu as pltpu

F32 = jnp.float32
BF16 = jnp.bfloat16

D_MODEL = 1024
SEQ = 2048
DEPTH = 2
D_POOL = 512
D_SGU = 512
D_IN = 2560
N_GROUPS = 4
GROUP = 128
N_HEADS = 4
HEAD = 128
CHUNK = 128
WINDOWS = (2, 4, 8, 16)
ALPHA = (2.0 * DEPTH) ** 0.25
LN_EPS = 1e-5
N_DEV = 8

ADAM_LR = 0.001
ADAM_B1 = 0.9
ADAM_B2 = 0.999
ADAM_EPS = 1e-08
ADAM_WD = 0.01
ADAM_STEP = 10

TM = 256
HALO = 16
N_TILES = SEQ // TM
VMEM_LIMIT = 60 * 1024 * 1024

ROW_WPOOL = 0
ROW_WSGU = 512
ROW_PSCALE = 1024
ROW_SLNG = 1028
ROW_SLNB = 1032
ROW_BSGU = 1036
ROW_LNG = 1040
ROW_LNB = 1048
PACK_ROWS = 1088
DMOD_COLS = DEPTH * 3 * D_MODEL // 8

SQRT_HALF = 0.7071067811865476
INV_SQRT_2PI = 0.3989422804014327


def _ln(x):
    mu = jnp.mean(x, axis=-1, keepdims=True)
    xc = x - mu
    var = jnp.mean(xc * xc, axis=-1, keepdims=True)
    rstd = lax.rsqrt(var + LN_EPS)
    return xc * rstd, rstd


def _ln_bwd(dxn, xn, rstd):
    m1 = jnp.mean(dxn, axis=-1, keepdims=True)
    m2 = jnp.mean(dxn * xn, axis=-1, keepdims=True)
    return rstd * (dxn - m1 - xn * m2)


def _normal_cdf(x):
    return 0.5 * (1.0 + lax.erf(x * SQRT_HALF))


def _gelu_parts(x, cdf, with_grad):
    if not with_grad:
        return x * cdf, None
    return x * cdf, cdf + x * (INV_SQRT_2PI * jnp.exp(-0.5 * x * x))


def _silu_parts(x):
    s = jax.nn.sigmoid(x)
    return x * s, s * (1.0 + x * (1.0 - s))


def _dot(a, b):
    return lax.dot_general(a, b, (((1,), (0,)), ((), ())), preferred_element_type=F32)


def _dot_nt(a, b):
    return lax.dot_general(a, b, (((1,), (1,)), ((), ())), preferred_element_type=F32)


def _dot_tn(a, b):
    return lax.dot_general(a, b, (((0,), (0,)), ((), ())), preferred_element_type=F32)


def _row_index(tile):
    return tile * TM + lax.broadcasted_iota(jnp.int32, (TM, 1), 0)


def _window_sums(ext, forward):
    n = TM + HALO
    cur = ext
    outs = []
    for g in range(N_GROUPS):
        step = 1 << g
        cur = cur + pltpu.roll(cur, step if forward else n - step, 0)
        rows = cur[HALO:, :GROUP] if forward else cur[:TM, :GROUP]
        outs.append(rows)
        cur = cur[:, GROUP:] if g + 1 < N_GROUPS else None
    return outs


def _inverse_counts(rows):
    return [1.0 / jnp.minimum(rows + 1, w).astype(F32) for w in WINDOWS]


def _tril_bf16(w):
    t = lax.broadcasted_iota(jnp.int32, (CHUNK, CHUNK), 0)
    s = lax.broadcasted_iota(jnp.int32, (CHUNK, CHUNK), 1)
    return jnp.where(t >= s, w, 0.0).astype(BF16)


class _MixWeights:
    def __init__(self, layer, wpool_ref, pscale_ref, slng_ref, slnb_ref, wsgu_ref, bsgut_ref):
        self.layer = layer
        self.wpool_ref, self.pscale_ref, self.slng_ref, self.slnb_ref = wpool_ref, pscale_ref, slng_ref, slnb_ref
        self.wsgu_ref, self.bsgut_ref = wsgu_ref, bsgut_ref

    def pool(self, g):
        return self.wpool_ref[self.layer, g].astype(BF16)

    def pool_scale(self, g):
        return self.pscale_ref[self.layer:self.layer + 1, g * GROUP:(g + 1) * GROUP]

    def ln_gain(self, h):
        return self.slng_ref[self.layer, h:h + 1, :]

    def ln_bias(self, h):
        return self.slnb_ref[self.layer, h:h + 1, :]

    def mix(self, h):
        return _tril_bf16(self.wsgu_ref[self.layer, h])

    def mix_bias(self, h):
        return self.bsgut_ref[self.layer, :, h:h + 1]


SMALL_SPECS = ((DEPTH, N_GROUPS, GROUP, GROUP), (DEPTH, D_POOL), (DEPTH, N_HEADS, HEAD), (DEPTH, N_HEADS, HEAD),
               (DEPTH, N_HEADS, CHUNK, CHUNK), (DEPTH, CHUNK, N_HEADS))


def _mix_forward(proj, halo, tile, w, cdf=None):
    keep = cdf is not None
    rows = _row_index(tile)
    inv_counts = _inverse_counts(rows)
    xa = proj[:, 0:D_POOL]
    ga = proj[:, D_POOL:2 * D_POOL]
    sums = _window_sums(jnp.concatenate([halo, xa], axis=0), True)
    ga_act, ga_grad = _silu_parts(ga)
    pooled, pw, ya = [], [], []
    for g in range(N_GROUPS):
        sl = slice(g * GROUP, (g + 1) * GROUP)
        p = (sums[g] * inv_counts[g] - xa[:, sl]).astype(BF16)
        q = _dot(p, w.pool(g))
        pooled.append(p)
        pw.append(q)
        ya.append(q * w.pool_scale(g) * ga_act[:, sl])

    u = proj[:, 2 * D_POOL:2 * D_POOL + D_SGU]
    v = proj[:, 2 * D_POOL + D_SGU:2 * D_POOL + 2 * D_SGU]
    gb = proj[:, 2 * D_POOL + 2 * D_SGU:]
    gb_act, gb_grad = _silu_parts(gb)
    if cdf is None:
        cdf = jnp.concatenate([_normal_cdf(u), _normal_cdf(v)], axis=1)
    u_act, u_grad = _gelu_parts(u, cdf[:, :D_SGU], keep)
    v_act, v_grad = _gelu_parts(v, cdf[:, D_SGU:], keep)
    vn, vrstd, vln, mixed, yb = [], [], [], [], []
    for h in range(N_HEADS):
        sl = slice(h * HEAD, (h + 1) * HEAD)
        n_h, r_h = _ln(v_act[:, sl])
        l_h = (n_h * w.ln_gain(h) + w.ln_bias(h)).astype(BF16)
        w_h = w.mix(h)
        bias = w.mix_bias(h)
        m_h = jnp.concatenate(
            [_dot(w_h, l_h[k * CHUNK:(k + 1) * CHUNK]) + bias for k in range(TM // CHUNK)], axis=0)
        vn.append(n_h)
        vrstd.append(r_h)
        vln.append(l_h)
        mixed.append(m_h)
        yb.append(u_act[:, sl] * m_h * gb_act[:, sl])
    cat = jnp.concatenate(ya + yb, axis=1)
    if not keep:
        return cat, cdf
    return cat, dict(inv_counts=inv_counts, ga_act=ga_act, ga_grad=ga_grad, pooled=pooled, pw=pw, u_grad=u_grad,
                     v_grad=v_grad, u_act=u_act, gb_act=gb_act, gb_grad=gb_grad, vn=vn, vrstd=vrstd, vln=vln,
                     mixed=mixed)


def _const_spec(shape):
    nd = len(shape)
    return pl.BlockSpec(shape, lambda i: (0,) * nd)


def _layer_forward(layer, x, mod, w_int, w_outf, small, ln_g, ln_b, name, gather=()):
    n_gather = len(gather)

    def body(x_ref, mod_ref, wint_ref, wout_ref, wpool_ref, pscale_ref, slng_ref, slnb_ref, wsgu_ref, bsgut_ref,
             lng_ref, lnb_ref, *rest):
        weights = _MixWeights(layer, wpool_ref, pscale_ref, slng_ref, slnb_ref, wsgu_ref, bsgut_ref)
        loc_refs, rest = rest[:n_gather], rest[n_gather:]
        out_ref, proj_ref, y_ref, cdf_ref = rest[:4]
        full_refs, rest = rest[4:4 + n_gather], rest[4 + n_gather:]
        halo_ref = rest[0]
        tile = pl.program_id(0)

        def gathers():
            g_send, g_recv, _ = rest[1:]
            return [_TwoLevelGather(full_refs[n], g_send.at[n], g_recv.at[n], src=loc_refs[n])
                    for n in range(n_gather)]

        def own_copies():
            mine = _index(_me())
            return [pltpu.make_async_copy(loc_refs[n], full_refs[n].at[mine], rest[3].at[n]) for n in range(n_gather)]

        @pl.when(tile == 0)
        def _():
            halo_ref[...] = jnp.zeros_like(halo_ref)
            if n_gather:
                for cp in own_copies():
                    cp.start()
                for g in gathers():
                    g.send_mine()

        if n_gather:
            @pl.when(tile == N_TILES // 2 - 1)
            def _():
                for g in gathers():
                    g.relay()

            @pl.when(tile == N_TILES - 2)
            def _():
                for g in gathers():
                    g.pass_near()

        xt = x_ref[...]
        shift = mod_ref[layer:layer + 1, 0:D_MODEL]
        scale = mod_ref[layer:layer + 1, D_MODEL:2 * D_MODEL]
        gate = mod_ref[layer:layer + 1, 2 * D_MODEL:]
        xn, _ = _ln(xt)
        h = (xn * (1.0 + scale) + shift).astype(BF16)
        proj = _dot_nt(h, wint_ref[...])
        proj_ref[...] = proj
        cat, cdf_ref[...] = _mix_forward(proj, halo_ref[...], tile, weights)
        halo_ref[...] = proj[TM - HALO:, 0:D_POOL]
        y = _dot(cat.astype(BF16), wout_ref[...])
        y_ref[...] = y
        zn, _ = _ln(ALPHA * xt + gate * y)
        out_ref[...] = zn * lng_ref[layer:layer + 1, :] + lnb_ref[layer:layer + 1, :]

        if n_gather:
            @pl.when(tile == N_TILES - 1)
            def _():
                for g in gathers():
                    g.pass_far()
                for g in gathers():
                    g.wait_rest()
                for g in gathers():
                    g.wait_sends()
                for cp in own_copies():
                    cp.wait()

    row = lambda w: pl.BlockSpec((TM, w), lambda i: (i, 0))
    comm_scratch = [pltpu.SemaphoreType.DMA((n_gather, GATHER_SEMS)), pltpu.SemaphoreType.DMA((n_gather, GATHER_SEMS)),
                    pltpu.SemaphoreType.DMA((n_gather,))] if n_gather else []
    return pl.pallas_call(
        body,
        name=name,
        grid=(N_TILES,),
        in_specs=[row(D_MODEL), _const_spec((DEPTH, 3 * D_MODEL)), _const_spec((D_IN, D_MODEL)),
                  _const_spec((D_MODEL, D_MODEL))] + [_const_spec(s) for s in SMALL_SPECS]
                 + [_const_spec((DEPTH, D_MODEL)), _const_spec((DEPTH, D_MODEL))] + [ANY] * n_gather,
        out_specs=[row(D_MODEL), row(D_IN), row(D_MODEL), row(2 * D_SGU)] + [ANY] * n_gather,
        out_shape=[jax.ShapeDtypeStruct((SEQ, D_MODEL), F32), jax.ShapeDtypeStruct((SEQ, D_IN), F32),
                   jax.ShapeDtypeStruct((SEQ, D_MODEL), F32), jax.ShapeDtypeStruct((SEQ, 2 * D_SGU), F32)]
                  + [jax.ShapeDtypeStruct((N_DEV,) + g.shape, g.dtype) for g in gather],
        scratch_shapes=[pltpu.VMEM((HALO, D_POOL), F32)] + comm_scratch,
        compiler_params=pltpu.CompilerParams(dimension_semantics=("arbitrary",), vmem_limit_bytes=VMEM_LIMIT),
    )(x, mod, w_int, w_outf, *small, ln_g, ln_b, *gather)


VEC_LNG, VEC_LNB, VEC_POOL, VEC_SGU, VEC_SHIFT, VEC_SCALE, VEC_GATE, VEC_LOSS = range(8)


def _layer_backward(layer, a, b, x, proj, y, cdf, mod, w_int, w_outf, small, ln_g, is_last, name, carry=(),
                    reduce=()):
    n_red, n_carry = len(reduce), len(carry)
    base = layer * PACK_ROWS

    def body(a_ref, b_ref, x_ref, proj_ref, prev_ref, y_ref, cdf_ref, mod_ref, wint_ref, wout_ref, wpool_ref,
             pscale_ref, slng_ref, slnb_ref, wsgu_ref, bsgut_ref, lng_ref, *rest):
        weights = _MixWeights(layer, wpool_ref, pscale_ref, slng_ref, slnb_ref, wsgu_ref, bsgut_ref)
        carry_refs, rest = rest[:n_carry], rest[n_carry:]
        part_refs, rest = rest[:n_red], rest[n_red:]
        dx_ref, dproj_ref, h_ref, cat_ref, dy_ref, small_ref, dmod_ref, loss_ref = rest[:8]
        shard_refs, rest = rest[8:8 + n_red], rest[8 + n_red:]
        vec_ref, dmix_ref, halo_ref = rest[:3]
        step = pl.program_id(0)
        tile = N_TILES - 1 - step

        def scatter():
            bufs, sems = rest[3:3 + 5 * n_red], rest[3 + 5 * n_red:]
            arrays = [dict(part=part_refs[n], out=shard_refs[n], staged=True, stage=bufs[5 * n], sib=bufs[5 * n + 1],
                           snd=bufs[5 * n + 2], rcv=bufs[5 * n + 3], relay=bufs[5 * n + 4]) for n in range(n_red)]
            return _ChipReduceScatter(arrays, *sems)

        @pl.when(step == 0)
        def _():
            small_ref[...] = jnp.zeros_like(small_ref)
            dmod_ref[...] = jnp.zeros_like(dmod_ref)
            vec_ref[...] = jnp.zeros_like(vec_ref)
            dmix_ref[...] = jnp.zeros_like(dmix_ref)
            halo_ref[...] = jnp.zeros_like(halo_ref)
            if n_red:
                scatter().start()

        if n_red:
            @pl.when(step == 1)
            def _():
                scatter().exchange()

            @pl.when(step == N_TILES // 2)
            def _():
                scatter().fold()

        def acc(row, lo, val):
            hi = lo + val.shape[1]
            vec_ref[row:row + 1, lo:hi] += jnp.sum(val, axis=0, keepdims=True)

        xt = x_ref[...]
        yt = y_ref[...]
        shift = mod_ref[layer:layer + 1, 0:D_MODEL]
        scale = mod_ref[layer:layer + 1, D_MODEL:2 * D_MODEL]
        gate = mod_ref[layer:layer + 1, 2 * D_MODEL:]
        ln_gain = lng_ref[layer:layer + 1, :]

        zn, zrstd = _ln(ALPHA * xt + gate * yt)
        if is_last:
            diff = a_ref[...] - b_ref[...]
            acc(VEC_LOSS, 0, diff * diff)
            dout = diff * (1.0 / D_MODEL)
        else:
            dout = a_ref[...]
        acc(VEC_LNG, 0, dout * zn)
        acc(VEC_LNB, 0, dout)
        dz = _ln_bwd(dout * ln_gain, zn, zrstd)
        acc(VEC_GATE, 0, dz * yt)
        dy = (dz * gate).astype(BF16)
        dy_ref[...] = dy
        dcat = _dot_nt(dy, wout_ref[...])

        proj = proj_ref[...]
        prev = jnp.where(tile > 0, prev_ref[...], 0.0)
        cat, k = _mix_forward(proj, prev, tile, weights, cdf_ref[...])
        cat_ref[...] = cat.astype(BF16)

        dga, dq = [], []
        for g in range(N_GROUPS):
            sl = slice(g * GROUP, (g + 1) * GROUP)
            pscale = weights.pool_scale(g)
            dya = dcat[:, sl]
            dyp = dya * k["ga_act"][:, sl]
            dga.append(dya * k["pw"][g] * pscale * k["ga_grad"][:, sl])
            acc(VEC_POOL, g * GROUP, dyp * k["pw"][g])
            dpw = (dyp * pscale).astype(BF16)
            rows = pl.ds(base + ROW_WPOOL + g * GROUP, GROUP)
            small_ref[rows, :] += _dot_tn(k["pooled"][g], dpw)
            dq.append(_dot_nt(dpw, weights.pool(g)))
        dpooled = jnp.concatenate(dq, axis=1)
        scaled = jnp.concatenate([dq[g] * k["inv_counts"][g] for g in range(N_GROUPS)], axis=1)
        sums = _window_sums(jnp.concatenate([scaled, halo_ref[...]], axis=0), False)
        halo_ref[...] = scaled[0:HALO]
        dxa = jnp.concatenate(sums, axis=1) - dpooled

        du, dv, dgb = [], [], []
        for h in range(N_HEADS):
            sl = slice(h * HEAD, (h + 1) * HEAD)
            dyb = dcat[:, D_POOL + h * HEAD:D_POOL + (h + 1) * HEAD]
            m_h = k["mixed"][h]
            ug = k["u_act"][:, sl] * dyb
            du.append(dyb * m_h * k["gb_act"][:, sl] * k["u_grad"][:, sl])
            dgb.append(ug * m_h * k["gb_grad"][:, sl])
            dmixed = ug * k["gb_act"][:, sl]
            dmixed_bf = dmixed.astype(BF16)
            w_h = weights.mix(h)
            dvln_parts = []
            dmix_sum = dmix_ref[h]
            wsgu_rows = pl.ds(base + ROW_WSGU + h * CHUNK, CHUNK)
            dws = small_ref[wsgu_rows, :]
            for c in range(TM // CHUNK):
                cs = slice(c * CHUNK, (c + 1) * CHUNK)
                dmix_sum = dmix_sum + dmixed[cs]
                dws = dws + _dot_nt(dmixed_bf[cs], k["vln"][h][cs])
                dvln_parts.append(_dot_tn(w_h, dmixed_bf[cs]))
            dmix_ref[h] = dmix_sum
            small_ref[wsgu_rows, :] = dws
            dvln = jnp.concatenate(dvln_parts, axis=0)
            acc(VEC_SGU, h * HEAD, dvln * k["vn"][h])
            acc(VEC_SGU, D_SGU + h * HEAD, dvln)
            dvv = _ln_bwd(dvln * weights.ln_gain(h), k["vn"][h], k["vrstd"][h])
            dv.append(dvv * k["v_grad"][:, sl])

        dproj = jnp.concatenate([dxa] + dga + du + dv + dgb, axis=1).astype(BF16)
        dproj_ref[...] = dproj
        dh = _dot(dproj, wint_ref[...])

        xn, xrstd = _ln(xt)
        h_ref[...] = (xn * (1.0 + scale) + shift).astype(BF16)
        acc(VEC_SCALE, 0, dh * xn)
        acc(VEC_SHIFT, 0, dh)
        dx_ref[...] = _ln_bwd(dh * (1.0 + scale), xn, xrstd) + ALPHA * dz

        @pl.when(step == N_TILES - 1)
        def _():
            def put(row0, vec_row, lo, n):
                for r in range(n):
                    small_ref[base + row0 + r:base + row0 + r + 1, :] = (
                        vec_ref[vec_row:vec_row + 1, lo + r * 128:lo + (r + 1) * 128])

            put(ROW_PSCALE, VEC_POOL, 0, 4)
            put(ROW_SLNG, VEC_SGU, 0, 4)
            put(ROW_SLNB, VEC_SGU, D_SGU, 4)
            put(ROW_LNG, VEC_LNG, 0, 8)
            put(ROW_LNB, VEC_LNB, 0, 8)
            ones = jnp.ones((8, HEAD), F32)
            t = lax.broadcasted_iota(jnp.int32, (CHUNK, CHUNK), 0)
            s = lax.broadcasted_iota(jnp.int32, (CHUNK, CHUNK), 1)
            for h in range(N_HEADS):
                bias_rows = lax.dot_general(ones, dmix_ref[h], (((1,), (1,)), ((), ())),
                                            preferred_element_type=F32, precision=lax.Precision.HIGHEST)
                small_ref[base + ROW_BSGU + h:base + ROW_BSGU + h + 1, :] = bias_rows[0:1]
                rows = pl.ds(base + ROW_WSGU + h * CHUNK, CHUNK)
                small_ref[rows, :] = jnp.where(t >= s, small_ref[rows, :], 0.0)
            pieces = ((0, VEC_SHIFT, 0, 768),
                      (1, VEC_SHIFT, 768, 256), (1, VEC_SCALE, 0, 512),
                      (2, VEC_SCALE, 512, 512), (2, VEC_GATE, 0, 256),
                      (3, VEC_GATE, 256, 768))
            filled = [0] * 4
            for q, vec_row, lo, n in pieces:
                row = 4 * layer + q
                dmod_ref[row:row + 1, filled[q]:filled[q] + n] = vec_ref[vec_row:vec_row + 1, lo:lo + n]
                filled[q] += n
            if n_carry:
                for other in range(layer + 1, DEPTH):
                    rows = pl.ds(other * PACK_ROWS, PACK_ROWS)
                    small_ref[rows, :] = carry_refs[0][rows, :]
                    dmod_ref[4 * other:4 * other + 4, :] = carry_refs[1][4 * other:4 * other + 4, :]
            loss_ref[...] = vec_ref[VEC_LOSS:VEC_LOSS + 1, :]
            if n_red:
                scatter().finish()
                scatter().wait_sends()

    rev = lambda w: pl.BlockSpec((TM, w), lambda i: (N_TILES - 1 - i, 0))
    prev_spec = pl.BlockSpec(
        (HALO, D_POOL), lambda i: (jnp.maximum((N_TILES - 1 - i) * (TM // HALO) - 1, 0), 0))
    comm_scratch = []
    for p in reduce:
        comm_scratch += _ChipReduceScatter.buffers(p.shape[2], p.shape[3], p.dtype)
    if n_red:
        comm_scratch += _ChipReduceScatter.semaphores(n_red)
    return pl.pallas_call(
        body,
        name=name,
        grid=(N_TILES,),
        in_specs=[rev(D_MODEL), rev(D_MODEL) if is_last else pl.BlockSpec((TM, D_MODEL), lambda i: (0, 0)),
                  rev(D_MODEL), rev(D_IN), prev_spec, rev(D_MODEL), rev(2 * D_SGU),
                  _const_spec((DEPTH, 3 * D_MODEL)), _const_spec((D_IN, D_MODEL)), _const_spec((D_MODEL, D_MODEL))]
                 + [_const_spec(s) for s in SMALL_SPECS] + [_const_spec((DEPTH, D_MODEL))]
                 + [_const_spec(c.shape) for c in carry] + [ANY] * n_red,
        out_specs=[rev(D_MODEL), rev(D_IN), rev(D_MODEL), rev(D_MODEL), rev(D_MODEL),
                   _const_spec((DEPTH * PACK_ROWS, 128)), _const_spec((8, DMOD_COLS)), _const_spec((1, D_MODEL))]
                  + [_const_spec(p.shape[2:]) for p in reduce],
        out_shape=[jax.ShapeDtypeStruct((SEQ, D_MODEL), F32), jax.ShapeDtypeStruct((SEQ, D_IN), BF16),
                   jax.ShapeDtypeStruct((SEQ, D_MODEL), BF16), jax.ShapeDtypeStruct((SEQ, D_MODEL), BF16),
                   jax.ShapeDtypeStruct((SEQ, D_MODEL), BF16), jax.ShapeDtypeStruct((DEPTH * PACK_ROWS, 128), F32),
                   jax.ShapeDtypeStruct((8, DMOD_COLS), F32), jax.ShapeDtypeStruct((1, D_MODEL), F32)]
                  + [jax.ShapeDtypeStruct(p.shape[2:], F32) for p in reduce],
        scratch_shapes=[pltpu.VMEM((8, D_MODEL), F32), pltpu.VMEM((N_HEADS, CHUNK, HEAD), F32),
                        pltpu.VMEM((HALO, D_POOL), F32)] + comm_scratch,
        compiler_params=pltpu.CompilerParams(dimension_semantics=("arbitrary",), vmem_limit_bytes=VMEM_LIMIT),
    )(a, b, x, proj, proj, y, cdf, mod, w_int, w_outf, *small, ln_g, *carry, *reduce)


def _grad_matmul(lhs, rhs, block_cols, name):
    m, n = lhs.shape[1], rhs.shape[1]

    def body(lhs_ref, rhs_ref, out_ref):
        out_ref[...] = _dot_tn(lhs_ref[...], rhs_ref[...]).astype(BF16)

    return pl.pallas_call(
        body,
        name=name,
        grid=(m // block_cols,),
        in_specs=[pl.BlockSpec((SEQ, block_cols), lambda j: (0, j)), pl.BlockSpec((SEQ, n), lambda j: (0, 0))],
        out_specs=pl.BlockSpec((block_cols, n), lambda j: (j, 0)),
        out_shape=jax.ShapeDtypeStruct((m, n), BF16),
        compiler_params=pltpu.CompilerParams(dimension_semantics=("arbitrary",), vmem_limit_bytes=VMEM_LIMIT),
    )(lhs, rhs)


def _adamw_math(w, g, m, v):
    m = ADAM_B1 * m + (1.0 - ADAM_B1) * g
    v = ADAM_B2 * v + (1.0 - ADAM_B2) * (g * g)
    m_hat = m / (1.0 - ADAM_B1 ** ADAM_STEP)
    v_hat = v / (1.0 - ADAM_B2 ** ADAM_STEP)
    delta = -ADAM_LR * (m_hat / (jnp.sqrt(v_hat) + ADAM_EPS) + ADAM_WD * w)
    return delta, m, v


def _adamw(w, grads, m, v, block_rows, name):
    rows, cols = grads[0].shape
    blocks = rows // block_rows

    def body(w_ref, m_ref, v_ref, *rest):
        g_refs, (g_ref, d_ref, nm_ref, nv_ref) = rest[:DEPTH], rest[DEPTH:]
        for layer in range(DEPTH):
            @pl.when(pl.program_id(0) == layer)
            def _():
                g = g_refs[layer][...]
                g_ref[...] = g
                d_ref[...], nm_ref[...], nv_ref[...] = _adamw_math(w_ref[...], g, m_ref[...], v_ref[...])

    def grad_spec(layer):
        return pl.BlockSpec((block_rows, cols),
                            lambda l, i: (jnp.where(l == layer, i, jnp.where(l < layer, 0, blocks - 1)), 0))

    spec = pl.BlockSpec((block_rows, cols), lambda l, i: (l * blocks + i, 0))
    return pl.pallas_call(
        body,
        name=name,
        grid=(DEPTH, blocks),
        in_specs=[spec] * 3 + [grad_spec(layer) for layer in range(DEPTH)],
        out_specs=[spec] * 4,
        out_shape=[jax.ShapeDtypeStruct(w.shape, F32)] * 4,
        compiler_params=pltpu.CompilerParams(dimension_semantics=("arbitrary", "arbitrary"),
                                             vmem_limit_bytes=VMEM_LIMIT),
    )(w, m, v, *grads)


def _adamw_ada(w, m, v, act_t, dmod_cols, name):
    cols = w.shape[2]

    rows = 256

    def body(w_ref, m_ref, v_ref, act_ref, dmod_ref, g_ref, d_ref, nm_ref, nv_ref):
        act = act_ref[...]
        dm = dmod_ref[0]
        g = act[:, 0:1] * dm[0:1, :]
        for b in range(1, N_DEV):
            g = g + act[:, b:b + 1] * dm[b:b + 1, :]
        g_ref[0] = g
        d_ref[0], nm_ref[0], nv_ref[0] = _adamw_math(w_ref[0], g, m_ref[0], v_ref[0])

    spec = pl.BlockSpec((1, rows, cols), lambda l, i: (l, i, 0))
    return pl.pallas_call(
        body,
        name=name,
        grid=(DEPTH, D_MODEL // rows),
        in_specs=[spec, spec, spec, pl.BlockSpec((rows, N_DEV), lambda l, i: (i, 0)),
                  pl.BlockSpec((1, N_DEV, cols), lambda l, i: (l, 0, 0))],
        out_specs=[spec] * 4,
        out_shape=[jax.ShapeDtypeStruct(w.shape, F32)] * 4,
        compiler_params=pltpu.CompilerParams(dimension_semantics=("arbitrary", "arbitrary"),
                                             vmem_limit_bytes=VMEM_LIMIT),
    )(w, m, v, act_t, dmod_cols)


def _adamw_bias(w, m, v, dmod_all, name):
    def body(w_ref, m_ref, v_ref, dmod_ref, g_ref, d_ref, nm_ref, nv_ref):
        g = dmod_ref[0]
        for b in range(1, N_DEV):
            g = g + dmod_ref[b]
        g_ref[...] = g
        d_ref[...], nm_ref[...], nv_ref[...] = _adamw_math(w_ref[...], g, m_ref[...], v_ref[...])

    return pl.pallas_call(
        body,
        name=name,
        out_shape=[jax.ShapeDtypeStruct(w.shape, F32)] * 4,
        compiler_params=pltpu.CompilerParams(vmem_limit_bytes=VMEM_LIMIT),
    )(w, m, v, dmod_all)


MESH = pl.DeviceIdType.MESH
SIBLING = 1
ANY = pl.BlockSpec(memory_space=pl.ANY)
VMEM = pl.BlockSpec(memory_space=pltpu.VMEM)


def _me():
    return lax.axis_index("x"), lax.axis_index("y"), lax.axis_index("c")


def _peer(r):
    x, y, c = _me()
    return (1 - x if r & 4 else x, 1 - y if r & 2 else y, 1 - c if r & 1 else c)


def _index(dev):
    return 4 * dev[0] + 2 * dev[1] + dev[2]


def _remote(src, dst, send_sem, recv_sem, dev):
    return pltpu.make_async_remote_copy(src_ref=src, dst_ref=dst, send_sem=send_sem, recv_sem=recv_sem,
                                        device_id=dev, device_id_type=MESH)


ACROSS_X, ACROSS_Y, ACROSS_BOTH = 4, 2, 6
GATHER_SEMS = 11


class _TwoLevelGather:
    def __init__(self, out, send_sems, recv_sems, src=None):
        self.out, self.send_sems, self.recv_sems, self.src = out, send_sems, recv_sems, src
        self.half = out.shape[1] // 2

    def _copy(self, k, block, part, to, src=None):
        slot = self.out.at[_index(block)]
        if part is not None:
            rows = pl.ds(part * self.half, self.half)
            slot = slot.at[rows]
            src = None if src is None else src.at[rows]
        return _remote(slot if src is None else src, slot, self.send_sems.at[k], self.recv_sems.at[k], to)

    def _mine(self):
        me = _me()
        src = self.out.at[_index(me)] if self.src is None else self.src
        x, y = _peer(ACROSS_X), _peer(ACROSS_Y)
        return [self._copy(1, me, 0, x, src), self._copy(3, me, 1, y, src), self._copy(2, me, 1, x, src),
                self._copy(4, me, 0, y, src), self._copy(0, me, None, _peer(SIBLING), src)]

    def _relayed(self):
        return [self._copy(5, _peer(ACROSS_X), 0, _peer(ACROSS_Y)), self._copy(6, _peer(ACROSS_Y), 1, _peer(ACROSS_X))]

    def _passed(self):
        sib, far = _peer(SIBLING), _peer(ACROSS_BOTH)
        return [self._copy(7, _peer(ACROSS_X), None, sib), self._copy(8, _peer(ACROSS_Y), None, sib),
                self._copy(9, far, 0, sib), self._copy(10, far, 1, sib)]

    def _arrival(self, k, r, part):
        return self._copy(k, _peer(r), part, _me())

    def send_mine(self):
        for cp in self._mine():
            cp.start()

    def relay(self):
        relayed = self._relayed()
        self._arrival(1, ACROSS_X, 0).wait_recv()
        relayed[0].start()
        self._arrival(3, ACROSS_Y, 1).wait_recv()
        relayed[1].start()

    def pass_near(self):
        passed = self._passed()
        self._arrival(2, ACROSS_X, 1).wait_recv()
        passed[0].start()
        self._arrival(4, ACROSS_Y, 0).wait_recv()
        passed[1].start()

    def pass_far(self):
        passed = self._passed()
        self._arrival(5, ACROSS_BOTH, 0).wait_recv()
        passed[2].start()
        self._arrival(6, ACROSS_BOTH, 1).wait_recv()
        passed[3].start()

    def pass_on(self):
        self.pass_near()
        self.pass_far()

    def wait_rest(self):
        self._arrival(0, SIBLING, None).wait_recv()
        self._arrival(7, ACROSS_X ^ SIBLING, None).wait_recv()
        self._arrival(8, ACROSS_Y ^ SIBLING, None).wait_recv()
        self._arrival(9, ACROSS_BOTH ^ SIBLING, 0).wait_recv()
        self._arrival(10, ACROSS_BOTH ^ SIBLING, 1).wait_recv()

    def wait_sends(self):
        for cp in self._mine() + self._relayed() + self._passed():
            cp.wait_send()


class _ChipReduceScatter:
    SLOTS = 6

    def __init__(self, arrays, l_sem, d_send, d_recv, i_send, i_recv):
        self.arrays = arrays
        self.l_sem, self.d_send, self.d_recv, self.i_send, self.i_recv = l_sem, d_send, d_recv, i_send, i_recv

    @staticmethod
    def buffers(rows, cols, dtype, staged=True):
        stage = [pltpu.VMEM((4, rows, cols), dtype)] if staged else []
        return stage + [pltpu.VMEM((4, rows, cols), dtype), pltpu.VMEM((3, rows, cols), dtype),
                        pltpu.VMEM((2, rows, cols), dtype), pltpu.VMEM((2, rows // 2, cols), dtype)]

    @classmethod
    def semaphores(cls, n):
        return [pltpu.SemaphoreType.DMA((n,)), pltpu.SemaphoreType.DMA((n,)), pltpu.SemaphoreType.DMA((n,)),
                pltpu.SemaphoreType.DMA((n, cls.SLOTS)), pltpu.SemaphoreType.DMA((n, cls.SLOTS))]

    def _pick(self, which):
        return list(enumerate(self.arrays)) if which is None else [(n, self.arrays[n]) for n in which]

    def _staging(self, which):
        c = _me()[2]
        return [pltpu.make_async_copy(a["part"].at[pl.ds(0, 4), c], a["stage"], self.l_sem.at[n])
                for n, a in self._pick(which) if a["staged"]]

    def _first(self, which):
        other = 1 - _me()[2]
        return [_remote(a["part"].at[pl.ds(0, 4), other], a["sib"], self.d_send.at[n], self.d_recv.at[n],
                        _peer(SIBLING)) for n, a in self._pick(which)]

    @staticmethod
    def _halves(a):
        half = a["rcv"].shape[1] // 2
        return pl.ds(0, half), pl.ds(half, half)

    def _hops(self, n, a):
        h0, h1 = self._halves(a)
        x, y = _peer(ACROSS_X), _peer(ACROSS_Y)
        snd, rcv, relay = a["snd"], a["rcv"], a["relay"]
        pairs = [(snd.at[2, h0], relay.at[0], x), (snd.at[2, h1], relay.at[1], y),
                 (snd.at[0, h0], rcv.at[0, h0], x), (snd.at[0, h1], rcv.at[0, h1], x),
                 (snd.at[1, h1], rcv.at[1, h1], y), (snd.at[1, h0], rcv.at[1, h0], y)]
        return [_remote(s, d, self.i_send.at[n, k], self.i_recv.at[n, k], to) for k, (s, d, to) in enumerate(pairs)]

    def _mine(self, a, chip, rows=None):
        src = a["stage"].at[chip] if a["staged"] else a["part"].at[chip, _me()[2]]
        mine, sib = (src[...], a["sib"][chip]) if rows is None else (src[rows, :], a["sib"][chip, rows, :])
        return mine.astype(F32) + sib.astype(F32)

    def start(self, which=None):
        for cp in self._staging(which) + self._first(which):
            cp.start()

    def exchange(self, which=None):
        for cp in self._staging(which):
            cp.wait()
        for cp in self._first(which):
            cp.wait_recv()
        chip = lambda dev: 2 * dev[0] + dev[1]
        across_x, across_y, far = chip(_peer(ACROSS_X)), chip(_peer(ACROSS_Y)), chip(_peer(ACROSS_BOTH))
        picked = self._pick(which)
        for n, a in picked:
            hops = self._hops(n, a)
            a["snd"][2] = self._mine(a, far).astype(a["snd"].dtype)
            hops[0].start()
            hops[1].start()
        for n, a in picked:
            h0, h1 = self._halves(a)
            hops = self._hops(n, a)
            dtype = a["snd"].dtype
            a["snd"][0, h0, :] = self._mine(a, across_x, h0).astype(dtype)
            hops[2].start()
            a["snd"][1, h1, :] = self._mine(a, across_y, h1).astype(dtype)
            hops[4].start()

    def fold(self, which=None):
        chip = lambda dev: 2 * dev[0] + dev[1]
        across_x, across_y = chip(_peer(ACROSS_X)), chip(_peer(ACROSS_Y))
        for n, a in self._pick(which):
            h0, h1 = self._halves(a)
            hops = self._hops(n, a)
            dtype = a["snd"].dtype
            hops[1].wait_recv()
            a["snd"][0, h1, :] = (self._mine(a, across_x, h1) + a["relay"][1].astype(F32)).astype(dtype)
            hops[3].start()
            hops[0].wait_recv()
            a["snd"][1, h0, :] = (self._mine(a, across_y, h0) + a["relay"][0].astype(F32)).astype(dtype)
            hops[5].start()

    def finish(self, which=None):
        x, y, _ = _me()
        home = 2 * x + y
        for n, a in self._pick(which):
            hops = self._hops(n, a)
            a["out"][...] = self._mine(a, home)
            hops[2].wait_recv()
            hops[3].wait_recv()
            a["out"][...] += a["rcv"][0].astype(F32)
            hops[4].wait_recv()
            hops[5].wait_recv()
            a["out"][...] += a["rcv"][1].astype(F32)

    def wait_sends(self, which=None):
        for cp in self._first(which):
            cp.wait_send()
        for n, a in self._pick(which):
            for cp in self._hops(n, a):
                cp.wait_send()


def _direct_exchange(src_of, dst_of, send_sems, recv_sems):
    me = _me()
    copies = [_remote(src_of(_peer(r)), dst_of(me), send_sems.at[r - 1], recv_sems.at[r - 1], _peer(r))
              for r in range(1, N_DEV)]
    for cp in copies:
        cp.start()
    return copies


def _wait_direct(copies):
    for cp in copies:
        cp.wait_recv()
    for cp in copies:
        cp.wait_send()


def _prep(c_row, w_ada, b_ada, blocks):
    cols = w_ada.shape[2]
    n_blocks = len(blocks)

    def body(c_ref, wada_ref, bada_ref, *rest):
        loc_refs, rest = rest[:n_blocks], rest[n_blocks:]
        act_all, mod_ref = rest[:2]
        full_refs, rest = rest[2:2 + n_blocks], rest[2 + n_blocks:]
        act_src, part, mod_recv, w_send, w_recv, w_local, a_send, a_recv, m_send, m_recv = rest
        me = _me()
        mine = _index(me)

        cval = c_ref[...]
        act_src[...] = jnp.zeros_like(act_src)
        act_src[0:1, :] = cval * jax.nn.sigmoid(cval)
        act_all[mine] = act_src[...]
        act_copies = _direct_exchange(lambda p: act_src, lambda m: act_all.at[_index(m)], a_send, a_recv)

        gathers, locals_ = [], []
        for n in range(n_blocks):
            own = pltpu.make_async_copy(loc_refs[n], full_refs[n].at[mine], w_local.at[n])
            own.start()
            locals_.append(own)
            g = _TwoLevelGather(full_refs[n], w_send.at[n], w_recv.at[n], src=loc_refs[n])
            g.send_mine()
            gathers.append(g)

        _wait_direct(act_copies)
        acts = jnp.concatenate([act_all[j, 0:1, :] for j in range(N_DEV)], axis=0)
        part[...] = jnp.zeros_like(part)
        for layer in range(DEPTH):
            res = lax.dot_general(acts, wada_ref[layer], (((1,), (0,)), ((), ())), preferred_element_type=F32,
                                  precision=lax.Precision.HIGHEST)
            for b in range(N_DEV):
                part[b, layer:layer + 1, :] = res[b:b + 1, :]
        mod_recv[mine] = part[mine]
        mod_copies = _direct_exchange(lambda p: part.at[_index(p)], lambda m: mod_recv.at[_index(m)], m_send, m_recv)

        for g in gathers:
            g.relay()
        for g in gathers:
            g.pass_on()
        for g in gathers:
            g.wait_rest()
        _wait_direct(mod_copies)
        for layer in range(DEPTH):
            for j in range(N_DEV):
                sl = slice(j * cols, (j + 1) * cols)
                mod_ref[layer:layer + 1, sl] = mod_recv[j, layer:layer + 1, :] + bada_ref[layer:layer + 1, sl]
        for g in gathers:
            g.wait_sends()
        for own in locals_:
            own.wait()

    return pl.pallas_call(
        body,
        name="prep_gather",
        in_specs=[VMEM, VMEM, VMEM] + [ANY] * n_blocks,
        out_specs=[VMEM, VMEM] + [ANY] * n_blocks,
        out_shape=[jax.ShapeDtypeStruct((N_DEV, 8, D_MODEL), F32), jax.ShapeDtypeStruct((DEPTH, 3 * D_MODEL), F32)]
                  + [jax.ShapeDtypeStruct((N_DEV,) + blk.shape, blk.dtype) for blk in blocks],
        scratch_shapes=[pltpu.VMEM((8, D_MODEL), F32), pltpu.VMEM((N_DEV, 8, cols), F32),
                        pltpu.VMEM((N_DEV, 8, cols), F32),
                        pltpu.SemaphoreType.DMA((n_blocks, GATHER_SEMS)),
                        pltpu.SemaphoreType.DMA((n_blocks, GATHER_SEMS)),
                        pltpu.SemaphoreType.DMA((n_blocks,)),
                        pltpu.SemaphoreType.DMA((7,)), pltpu.SemaphoreType.DMA((7,)),
                        pltpu.SemaphoreType.DMA((7,)), pltpu.SemaphoreType.DMA((7,))],
        compiler_params=pltpu.CompilerParams(vmem_limit_bytes=VMEM_LIMIT),
    )(c_row, w_ada, b_ada, *blocks)


def _grad_tail(dproj, h, cat, dy, small, dmod8, loss_lanes):
    shard_in, shard_out, shard_small = D_IN // N_DEV, D_MODEL // N_DEV, small.shape[2]
    W_IN, W_OUT, SMALL = 0, 1, 2

    def body(dproj_hbm, h_hbm, cat_hbm, dy_hbm, small_hbm, dmod_ref, lanes_ref,
             gwin_ref, gwout_ref, stot_ref, dmod_all, loss_ref,
             dproj_v, h_v, cat_v, dy_v, part_in, part_out, own_small, loss_src, loss_all, *rest):
        bufs, rest = rest[:13], rest[13:]
        load_sems, rs_sems = rest[0], rest[1:6]
        m_send, m_recv, g_send, g_recv, s_send, s_recv = rest[6:]
        mine = _index(_me())

        loads = [pltpu.make_async_copy(s, d, load_sems.at[n]) for n, (s, d) in enumerate(
            ((cat_hbm, cat_v), (dy_hbm, dy_v), (dproj_hbm, dproj_v), (h_hbm, h_v)))]
        for cp in loads:
            cp.start()
        arrays = [dict(part=part_in, out=gwin_ref, staged=False, sib=bufs[0], snd=bufs[1], rcv=bufs[2], relay=bufs[3]),
                  dict(part=part_out, out=gwout_ref, staged=False, sib=bufs[4], snd=bufs[5], rcv=bufs[6],
                       relay=bufs[7]),
                  dict(part=small_hbm, out=own_small, staged=True, stage=bufs[8], sib=bufs[9], snd=bufs[10],
                       rcv=bufs[11], relay=bufs[12])]
        scatter = _ChipReduceScatter(arrays, *rs_sems)
        scatter.start([SMALL])
        dmod_all[mine] = dmod_ref[...]
        dmod_copies = _direct_exchange(lambda p: dmod_ref, lambda m: dmod_all.at[_index(m)], m_send, m_recv)
        loss_src[...] = jnp.full(loss_src.shape, (0.5 / D_MODEL) * jnp.sum(lanes_ref[...]), F32)
        loss_all[mine] = loss_src[...]
        loss_copies = _direct_exchange(lambda p: loss_src, lambda m: loss_all.at[_index(m)], s_send, s_recv)

        loads[0].wait()
        loads[1].wait()
        for blk in range(2):
            res = _dot_tn(cat_v[:, blk * 512:(blk + 1) * 512], dy_v[...]).astype(BF16)
            for s in range(4):
                part_out[2 * blk + s // 2, s % 2] = res[s * shard_out:(s + 1) * shard_out]
        scatter.start([W_OUT])
        scatter.exchange([SMALL])

        gather = _TwoLevelGather(stot_ref, g_send, g_recv)
        loads[2].wait()
        loads[3].wait()
        for chip in range(4):
            res = _dot_tn(dproj_v[:, chip * 2 * shard_in:(chip + 1) * 2 * shard_in], h_v[...]).astype(BF16)
            part_in[chip, 0] = res[:shard_in]
            part_in[chip, 1] = res[shard_in:]
            if chip == 0:
                scatter.exchange([W_OUT])
                scatter.fold([SMALL])
            if chip == 1:
                scatter.fold([W_OUT])
                scatter.finish([SMALL])
                stot_ref[mine] = own_small[...]
                gather.send_mine()
            if chip == 2:
                gather.relay()
        scatter.start([W_IN])
        scatter.finish([W_OUT])
        scatter.exchange([W_IN])
        gather.pass_on()
        scatter.fold([W_IN])
        gather.wait_rest()
        _wait_direct(dmod_copies)
        _wait_direct(loss_copies)
        total = loss_all[0]
        for j in range(1, N_DEV):
            total = total + loss_all[j]
        loss_ref[...] = total
        scatter.finish([W_IN])
        gather.wait_sends()
        scatter.wait_sends()

    buffers = _ChipReduceScatter.buffers
    comm_scratch = (buffers(shard_in, D_MODEL, BF16, staged=False) + buffers(shard_out, D_MODEL, BF16, staged=False)
                    + buffers(shard_small, 128, F32))
    comm_scratch += [pltpu.SemaphoreType.DMA((4,))] + _ChipReduceScatter.semaphores(3)
    comm_scratch += [pltpu.SemaphoreType.DMA((n,)) for n in (7, 7, GATHER_SEMS, GATHER_SEMS, 7, 7)]
    return pl.pallas_call(
        body,
        name="grad_tail",
        in_specs=[ANY] * 5 + [VMEM, VMEM],
        out_specs=[VMEM] * 5,
        out_shape=[jax.ShapeDtypeStruct((shard_in, D_MODEL), F32), jax.ShapeDtypeStruct((shard_out, D_MODEL), F32),
                   jax.ShapeDtypeStruct((N_DEV, shard_small, 128), F32),
                   jax.ShapeDtypeStruct((N_DEV,) + dmod8.shape, F32), jax.ShapeDtypeStruct((8, 128), F32)],
        scratch_shapes=[pltpu.VMEM(dproj.shape, BF16), pltpu.VMEM(h.shape, BF16), pltpu.VMEM(cat.shape, BF16),
                        pltpu.VMEM(dy.shape, BF16), pltpu.VMEM((4, 2, shard_in, D_MODEL), BF16),
                        pltpu.VMEM((4, 2, shard_out, D_MODEL), BF16), pltpu.VMEM((shard_small, 128), F32),
                        pltpu.VMEM((8, 128), F32), pltpu.VMEM((N_DEV, 8, 128), F32)] + comm_scratch,
        compiler_params=pltpu.CompilerParams(vmem_limit_bytes=VMEM_LIMIT),
    )(dproj, h, cat, dy, small, dmod8, loss_lanes)


SMALL_NAMES = ("w_pool", "w_sgu", "pool_scale", "sgu_ln_g", "sgu_ln_b", "b_sgu", "ln_g", "ln_b")
SMALL_ROWS = (512, 512, 4, 4, 4, 4, 8, 8)


def _adamw_small(g_packed, ws, ms, vs, name):
    n = len(SMALL_NAMES)

    def body(g_ref, *refs):
        w_refs, m_refs, v_refs = refs[:n], refs[n:2 * n], refs[2 * n:3 * n]
        outs = refs[3 * n:]

        def update(p, at, g):
            delta, new_m, new_v = _adamw_math(w_refs[p][at], g, m_refs[p][at], v_refs[p][at])
            outs[p][at] = g
            outs[n + p][at] = delta
            outs[2 * n + p][at] = new_m
            outs[3 * n + p][at] = new_v

        row = 0
        for p, r in enumerate(SMALL_ROWS):
            shape = ws[p].shape
            for layer in range(DEPTH):
                first = layer * PACK_ROWS + row
                if len(shape) == 4:
                    for k in range(shape[1]):
                        update(p, (layer, k), g_ref[first + k * shape[2]:first + (k + 1) * shape[2], :])
                elif len(shape) == 3:
                    update(p, (layer,), g_ref[first:first + r, :])
                else:
                    g = jnp.concatenate([g_ref[first + k:first + k + 1, :] for k in range(r)], axis=1)
                    update(p, (slice(layer, layer + 1), slice(None)), g)
            row += r

    res = pl.pallas_call(
        body,
        name=name,
        out_shape=[jax.ShapeDtypeStruct(w.shape, F32) for w in ws] * 4,
        compiler_params=pltpu.CompilerParams(vmem_limit_bytes=VMEM_LIMIT),
    )(g_packed, *ws, *ms, *vs)
    return res[:n], res[n:2 * n], res[2 * n:3 * n], res[3 * n:]


def kernel(x, c, w_ada, b_ada, w_in, w_pool, pool_scale, sgu_ln_g, sgu_ln_b, w_sgu, b_sgu, w_out, ln_g, ln_b, loss_target, m_w_ada, m_b_ada, m_w_in, m_w_pool, m_pool_scale, m_sgu_ln_g, m_sgu_ln_b, m_w_sgu, m_b_sgu, m_w_out, m_ln_g, m_ln_b, v_w_ada, v_b_ada, v_w_in, v_w_pool, v_pool_scale, v_sgu_ln_g, v_sgu_ln_b, v_w_sgu, v_b_sgu, v_w_out, v_ln_g, v_ln_b):
    mine = _index(_me())
    small_w = dict(w_pool=w_pool, w_sgu=w_sgu, pool_scale=pool_scale, sgu_ln_g=sgu_ln_g, sgu_ln_b=sgu_ln_b,
                   b_sgu=b_sgu, ln_g=ln_g, ln_b=ln_b)
    small_m = dict(w_pool=m_w_pool, w_sgu=m_w_sgu, pool_scale=m_pool_scale, sgu_ln_g=m_sgu_ln_g,
                   sgu_ln_b=m_sgu_ln_b, b_sgu=m_b_sgu, ln_g=m_ln_g, ln_b=m_ln_b)
    small_v = dict(w_pool=v_w_pool, w_sgu=v_w_sgu, pool_scale=v_pool_scale, sgu_ln_g=v_sgu_ln_g,
                   sgu_ln_b=v_sgu_ln_b, b_sgu=v_b_sgu, ln_g=v_ln_g, ln_b=v_ln_b)

    wint_loc = jnp.transpose(w_in, (0, 2, 1)).astype(BF16)
    wout_loc = w_out.astype(BF16)
    act_slots, mod, wint0, wout0 = _prep(c, w_ada, b_ada, [wint_loc[0], wout_loc[0]])
    act_all = act_slots[:, 0, :]
    w_int, w_outf = [wint0.reshape(D_IN, D_MODEL)], [wout0.reshape(D_MODEL, D_MODEL)]
    small = (w_pool, pool_scale, sgu_ln_g, sgu_ln_b, w_sgu, jnp.transpose(b_sgu, (0, 2, 1)))

    acts, cur = [], x[0]
    for l in range(DEPTH):
        nxt = [wint_loc[l + 1], wout_loc[l + 1]] if l + 1 < DEPTH else []
        out, proj, y, cdf, *gathered = _layer_forward(l, cur, mod, w_int[l], w_outf[l], small, ln_g, ln_b,
                                                      f"layer_fwd_{l}", gather=nxt)
        if gathered:
            w_int.append(gathered[0].reshape(D_IN, D_MODEL))
            w_outf.append(gathered[1].reshape(D_MODEL, D_MODEL))
        acts.append((cur, proj, y, cdf))
        cur = out

    shard_in, shard_out = D_IN // N_DEV, D_MODEL // N_DEV
    a, b = cur, loss_target[0]
    loss_lanes, carry, pending = None, (), []
    g_w_in_t, g_w_out = [None] * DEPTH, [None] * DEPTH
    for l in reversed(range(DEPTH)):
        dx, dproj, h, cat, dy, small_grads, dmod8, lanes, *shards = _layer_backward(
            l, a, b, *acts[l], mod, w_int[l], w_outf[l], small, ln_g, l == DEPTH - 1, f"layer_bwd_{l}",
            carry=carry, reduce=pending)
        if shards:
            g_w_in_t[l + 1], g_w_out[l + 1] = shards
        if l == DEPTH - 1:
            loss_lanes = lanes
        if l > 0:
            pending = [_grad_matmul(dproj, h, 640, f"grad_w_in_{l}").reshape(4, 2, shard_in, D_MODEL),
                       _grad_matmul(cat, dy, 512, f"grad_w_out_{l}").reshape(4, 2, shard_out, D_MODEL)]
        carry = (small_grads, dmod8)
        a = b = dx
    grad_x = a[None]

    g_w_in_t[0], g_w_out[0], small_tot, dmod_slots, loss_tile = _grad_tail(
        dproj, h, cat, dy, small_grads.reshape(4, 2, DEPTH * PACK_ROWS // N_DEV, 128), dmod8, loss_lanes)
    loss = loss_tile[0, 0]
    dmod_all = dmod_slots.reshape(N_DEV, DEPTH, 3 * D_MODEL)

    cols_ada = w_ada.shape[2]
    dmod_cols = jnp.transpose(lax.dynamic_slice_in_dim(dmod_all, mine * cols_ada, cols_ada, axis=2), (1, 0, 2))
    g_w_ada, d_w_ada, nm_w_ada, nv_w_ada = _adamw_ada(w_ada, m_w_ada, v_w_ada, jnp.transpose(act_all), dmod_cols,
                                                      "adamw_w_ada")
    g_b_ada, d_b_ada, nm_b_ada, nv_b_ada = _adamw_bias(b_ada, m_b_ada, v_b_ada, dmod_all, "adamw_b_ada")
    flat = lambda t: t.reshape(-1, t.shape[-1])
    to_t = lambda t: flat(jnp.transpose(t, (0, 2, 1)))
    from_t = lambda t: jnp.transpose(t.reshape(DEPTH, shard_in, D_MODEL), (0, 2, 1))
    g_w_in, d_w_in, nm_w_in, nv_w_in = [from_t(t) for t in _adamw(to_t(w_in), g_w_in_t, to_t(m_w_in), to_t(v_w_in),
                                                                  shard_in // 2, "adamw_w_in")]
    gwout, d_w_out, nm_w_out, nv_w_out = [t.reshape(w_out.shape) for t in _adamw(
        flat(w_out), g_w_out, flat(m_w_out), flat(v_w_out), shard_out, "adamw_w_out")]
    small_out = _adamw_small(small_tot.reshape(DEPTH * PACK_ROWS, 128), [small_w[n] for n in SMALL_NAMES],
                             [small_m[n] for n in SMALL_NAMES], [small_v[n] for n in SMALL_NAMES], "adamw_small")
    gs, ds, ms, vs = [dict(zip(SMALL_NAMES, group)) for group in small_out]

    def ordered(w_ada_, b_ada_, w_in_, small, w_out_):
        return (w_ada_, b_ada_, w_in_, small["w_pool"], small["pool_scale"], small["sgu_ln_g"], small["sgu_ln_b"],
                small["w_sgu"], small["b_sgu"], w_out_, small["ln_g"], small["ln_b"])

    return (loss, grad_x,
            *ordered(g_w_ada, g_b_ada, g_w_in, gs, gwout),
            *ordered(d_w_ada, d_b_ada, d_w_in, ds, d_w_out),
            *ordered(nm_w_ada, nm_b_ada, nm_w_in, ms, nm_w_out),
            *ordered(nv_w_ada, nv_b_ada, nv_w_in, vs, nv_w_out))
```

```python
import jax
import jax.numpy as jnp
from jax import lax
from jax.experimental import pallas as pl
from jax.experimental.pallas import tpu as pltpu

F32 = jnp.float32
BF16 = jnp.bfloat16

D_MODEL = 1024
SEQ = 2048
DEPTH = 2
D_POOL = 512
D_SGU = 512
D_IN = 2560
N_GROUPS = 4
GROUP = 128
N_HEADS = 4
HEAD = 128
CHUNK = 128
WINDOWS = (2, 4, 8, 16)
ALPHA = (2.0 * DEPTH) ** 0.25
LN_EPS = 1e-5
N_DEV = 8

ADAM_LR = 0.001
ADAM_B1 = 0.9
ADAM_B2 = 0.999
ADAM_EPS = 1e-08
ADAM_WD = 0.01
ADAM_STEP = 10

TM = 256
HALO = 16
N_TILES = SEQ // TM
VMEM_LIMIT = 60 * 1024 * 1024

ROW_WPOOL = 0
ROW_WSGU = 512
ROW_PSCALE = 1024
ROW_SLNG = 1028
ROW_SLNB = 1032
ROW_BSGU = 1036
ROW_LNG = 1040
ROW_LNB = 1048
PACK_ROWS = 1088
DMOD_COLS = DEPTH * 3 * D_MODEL // 8

SQRT_HALF = 0.7071067811865476
INV_SQRT_2PI = 0.3989422804014327


def _ln(x):
    mu = jnp.mean(x, axis=-1, keepdims=True)
    xc = x - mu
    var = jnp.mean(xc * xc, axis=-1, keepdims=True)
    rstd = lax.rsqrt(var + LN_EPS)
    return xc * rstd, rstd


def _ln_bwd(dxn, xn, rstd):
    m1 = jnp.mean(dxn, axis=-1, keepdims=True)
    m2 = jnp.mean(dxn * xn, axis=-1, keepdims=True)
    return rstd * (dxn - m1 - xn * m2)


def _normal_cdf(x):
    return 0.5 * (1.0 + lax.erf(x * SQRT_HALF))


def _gelu_parts(x, cdf, with_grad):
    if not with_grad:
        return x * cdf, None
    return x * cdf, cdf + x * (INV_SQRT_2PI * jnp.exp(-0.5 * x * x))


def _silu_parts(x):
    s = jax.nn.sigmoid(x)
    return x * s, s * (1.0 + x * (1.0 - s))


def _dot(a, b):
    return lax.dot_general(a, b, (((1,), (0,)), ((), ())), preferred_element_type=F32)


def _dot_nt(a, b):
    return lax.dot_general(a, b, (((1,), (1,)), ((), ())), preferred_element_type=F32)


def _dot_tn(a, b):
    return lax.dot_general(a, b, (((0,), (0,)), ((), ())), preferred_element_type=F32)


def _row_index(tile):
    return tile * TM + lax.broadcasted_iota(jnp.int32, (TM, 1), 0)


def _window_sums(ext, forward):
    n = TM + HALO
    cur = ext
    outs = []
    for g in range(N_GROUPS):
        step = 1 << g
        cur = cur + pltpu.roll(cur, step if forward else n - step, 0)
        rows = cur[HALO:, :GROUP] if forward else cur[:TM, :GROUP]
        outs.append(rows)
        cur = cur[:, GROUP:] if g + 1 < N_GROUPS else None
    return outs


def _inverse_counts(rows):
    return [1.0 / jnp.minimum(rows + 1, w).astype(F32) for w in WINDOWS]


def _tril_bf16(w):
    t = lax.broadcasted_iota(jnp.int32, (CHUNK, CHUNK), 0)
    s = lax.broadcasted_iota(jnp.int32, (CHUNK, CHUNK), 1)
    return jnp.where(t >= s, w, 0.0).astype(BF16)


class _MixWeights:
    def __init__(self, layer, wpool_ref, pscale_ref, slng_ref, slnb_ref, wsgu_ref, bsgut_ref):
        self.layer = layer
        self.wpool_ref, self.pscale_ref, self.slng_ref, self.slnb_ref = wpool_ref, pscale_ref, slng_ref, slnb_ref
        self.wsgu_ref, self.bsgut_ref = wsgu_ref, bsgut_ref

    def pool(self, g):
        return self.wpool_ref[self.layer, g].astype(BF16)

    def pool_scale(self, g):
        return self.pscale_ref[self.layer:self.layer + 1, g * GROUP:(g + 1) * GROUP]

    def ln_gain(self, h):
        return self.slng_ref[self.layer, h:h + 1, :]

    def ln_bias(self, h):
        return self.slnb_ref[self.layer, h:h + 1, :]

    def mix(self, h):
        return _tril_bf16(self.wsgu_ref[self.layer, h])

    def mix_bias(self, h):
        return self.bsgut_ref[self.layer, :, h:h + 1]


SMALL_SPECS = ((DEPTH, N_GROUPS, GROUP, GROUP), (DEPTH, D_POOL), (DEPTH, N_HEADS, HEAD), (DEPTH, N_HEADS, HEAD),
               (DEPTH, N_HEADS, CHUNK, CHUNK), (DEPTH, CHUNK, N_HEADS))


def _mix_forward(proj, halo, tile, w, cdf=None):
    keep = cdf is not None
    rows = _row_index(tile)
    inv_counts = _inverse_counts(rows)
    xa = proj[:, 0:D_POOL]
    ga = proj[:, D_POOL:2 * D_POOL]
    sums = _window_sums(jnp.concatenate([halo, xa], axis=0), True)
    ga_act, ga_grad = _silu_parts(ga)
    pooled, pw, ya = [], [], []
    for g in range(N_GROUPS):
        sl = slice(g * GROUP, (g + 1) * GROUP)
        p = (sums[g] * inv_counts[g] - xa[:, sl]).astype(BF16)
        q = _dot(p, w.pool(g))
        pooled.append(p)
        pw.append(q)
        ya.append(q * w.pool_scale(g) * ga_act[:, sl])

    u = proj[:, 2 * D_POOL:2 * D_POOL + D_SGU]
    v = proj[:, 2 * D_POOL + D_SGU:2 * D_POOL + 2 * D_SGU]
    gb = proj[:, 2 * D_POOL + 2 * D_SGU:]
    gb_act, gb_grad = _silu_parts(gb)
    if cdf is None:
        cdf = jnp.concatenate([_normal_cdf(u), _normal_cdf(v)], axis=1)
    u_act, u_grad = _gelu_parts(u, cdf[:, :D_SGU], keep)
    v_act, v_grad = _gelu_parts(v, cdf[:, D_SGU:], keep)
    vn, vrstd, vln, mixed, yb = [], [], [], [], []
    for h in range(N_HEADS):
        sl = slice(h * HEAD, (h + 1) * HEAD)
        n_h, r_h = _ln(v_act[:, sl])
        l_h = (n_h * w.ln_gain(h) + w.ln_bias(h)).astype(BF16)
        w_h = w.mix(h)
        bias = w.mix_bias(h)
        m_h = jnp.concatenate(
            [_dot(w_h, l_h[k * CHUNK:(k + 1) * CHUNK]) + bias for k in range(TM // CHUNK)], axis=0)
        vn.append(n_h)
        vrstd.append(r_h)
        vln.append(l_h)
        mixed.append(m_h)
        yb.append(u_act[:, sl] * m_h * gb_act[:, sl])
    cat = jnp.concatenate(ya + yb, axis=1)
    if not keep:
        return cat, cdf
    return cat, dict(inv_counts=inv_counts, ga_act=ga_act, ga_grad=ga_grad, pooled=pooled, pw=pw, u_grad=u_grad,
                     v_grad=v_grad, u_act=u_act, gb_act=gb_act, gb_grad=gb_grad, vn=vn, vrstd=vrstd, vln=vln,
                     mixed=mixed)


def _const_spec(shape):
    nd = len(shape)
    return pl.BlockSpec(shape, lambda i: (0,) * nd)


def _layer_forward(layer, x, mod, w_int, w_outf, small, ln_g, ln_b, name, gather=(), proj=None):
    n_gather = len(gather)
    projected = proj is not None
    n_out = 3 if projected else 4

    def body(x_ref, mod_ref, first_ref, wout_ref, wpool_ref, pscale_ref, slng_ref, slnb_ref, wsgu_ref, bsgut_ref,
             lng_ref, lnb_ref, *rest):
        weights = _MixWeights(layer, wpool_ref, pscale_ref, slng_ref, slnb_ref, wsgu_ref, bsgut_ref)
        loc_refs, rest = rest[:n_gather], rest[n_gather:]
        out_ref, y_ref, cdf_ref = rest[0], rest[n_out - 2], rest[n_out - 1]
        proj_ref = None if projected else rest[1]
        full_refs, rest = rest[n_out:n_out + n_gather], rest[n_out + n_gather:]
        halo_ref = rest[0]
        tile = pl.program_id(0)

        def gathers():
            g_send, g_recv, _ = rest[1:]
            return [_TwoLevelGather(full_refs[n], g_send.at[n], g_recv.at[n], src=loc_refs[n])
                    for n in range(n_gather)]

        def own_copies():
            mine = _index(_me())
            return [pltpu.make_async_copy(loc_refs[n], full_refs[n].at[mine], rest[3].at[n]) for n in range(n_gather)]

        @pl.when(tile == 0)
        def _():
            halo_ref[...] = jnp.zeros_like(halo_ref)
            if n_gather:
                for cp in own_copies():
                    cp.start()
                for g in gathers():
                    g.send_mine()

        if n_gather:
            @pl.when(tile == N_TILES // 2 - 1)
            def _():
                for g in gathers():
                    g.relay()

            @pl.when(tile == N_TILES - 2)
            def _():
                for g in gathers():
                    g.pass_near()

        xt = x_ref[...]
        shift = mod_ref[layer:layer + 1, 0:D_MODEL]
        scale = mod_ref[layer:layer + 1, D_MODEL:2 * D_MODEL]
        gate = mod_ref[layer:layer + 1, 2 * D_MODEL:]
        if projected:
            proj = first_ref[...]
        else:
            xn, _ = _ln(xt)
            h = (xn * (1.0 + scale) + shift).astype(BF16)
            proj = _dot_nt(h, first_ref[...])
            proj_ref[...] = proj
        cat, cdf_ref[...] = _mix_forward(proj, halo_ref[...], tile, weights)
        halo_ref[...] = proj[TM - HALO:, 0:D_POOL]
        y = _dot(cat.astype(BF16), wout_ref[...])
        y_ref[...] = y
        zn, _ = _ln(ALPHA * xt + gate * y)
        out_ref[...] = zn * lng_ref[layer:layer + 1, :] + lnb_ref[layer:layer + 1, :]

        if n_gather:
            @pl.when(tile == N_TILES - 1)
            def _():
                for g in gathers():
                    g.pass_far()
                for g in gathers():
                    g.wait_rest()
                for g in gathers():
                    g.wait_sends()
                for cp in own_copies():
                    cp.wait()

    row = lambda w: pl.BlockSpec((TM, w), lambda i: (i, 0))
    comm_scratch = [pltpu.SemaphoreType.DMA((n_gather, GATHER_SEMS)), pltpu.SemaphoreType.DMA((n_gather, GATHER_SEMS)),
                    pltpu.SemaphoreType.DMA((n_gather,))] if n_gather else []
    return pl.pallas_call(
        body,
        name=name,
        grid=(N_TILES,),
        in_specs=[row(D_MODEL), _const_spec((DEPTH, 3 * D_MODEL)),
                  row(D_IN) if projected else _const_spec((D_IN, D_MODEL)),
                  _const_spec((D_MODEL, D_MODEL))] + [_const_spec(s) for s in SMALL_SPECS]
                 + [_const_spec((DEPTH, D_MODEL)), _const_spec((DEPTH, D_MODEL))] + [ANY] * n_gather,
        out_specs=[row(D_MODEL)] + ([] if projected else [row(D_IN)]) + [row(D_MODEL), row(2 * D_SGU)]
                  + [ANY] * n_gather,
        out_shape=[jax.ShapeDtypeStruct((SEQ, D_MODEL), F32)]
                  + ([] if projected else [jax.ShapeDtypeStruct((SEQ, D_IN), F32)])
                  + [jax.ShapeDtypeStruct((SEQ, D_MODEL), F32), jax.ShapeDtypeStruct((SEQ, 2 * D_SGU), F32)]
                  + [jax.ShapeDtypeStruct((N_DEV,) + g.shape, g.dtype) for g in gather],
        scratch_shapes=[pltpu.VMEM((HALO, D_POOL), F32)] + comm_scratch,
        compiler_params=pltpu.CompilerParams(dimension_semantics=("arbitrary",), vmem_limit_bytes=VMEM_LIMIT),
    )(x, mod, proj if projected else w_int, w_outf, *small, ln_g, ln_b, *gather)


VEC_LNG, VEC_LNB, VEC_POOL, VEC_SGU, VEC_SHIFT, VEC_SCALE, VEC_GATE, VEC_LOSS = range(8)


def _layer_backward(layer, a, b, x, proj, y, cdf, mod, w_int, w_outf, small, ln_g, is_last, name, carry=(),
                    reduce=()):
    n_red, n_carry = len(reduce), len(carry)
    base = layer * PACK_ROWS

    def body(a_ref, b_ref, x_ref, proj_ref, prev_ref, y_ref, cdf_ref, mod_ref, wint_ref, wout_ref, wpool_ref,
             pscale_ref, slng_ref, slnb_ref, wsgu_ref, bsgut_ref, lng_ref, *rest):
        weights = _MixWeights(layer, wpool_ref, pscale_ref, slng_ref, slnb_ref, wsgu_ref, bsgut_ref)
        carry_refs, rest = rest[:n_carry], rest[n_carry:]
        part_refs, rest = rest[:n_red], rest[n_red:]
        dx_ref, dproj_ref, h_ref, cat_ref, dy_ref, small_ref, dmod_ref, loss_ref = rest[:8]
        shard_refs, rest = rest[8:8 + n_red], rest[8 + n_red:]
        vec_ref, dmix_ref, halo_ref = rest[:3]
        step = pl.program_id(0)
        tile = N_TILES - 1 - step

        def scatter():
            bufs, sems = rest[3:3 + 5 * n_red], rest[3 + 5 * n_red:]
            arrays = [dict(part=part_refs[n], out=shard_refs[n], staged=True, stage=bufs[5 * n], sib=bufs[5 * n + 1],
                           snd=bufs[5 * n + 2], rcv=bufs[5 * n + 3], relay=bufs[5 * n + 4]) for n in range(n_red)]
            return _ChipReduceScatter(arrays, *sems)

        @pl.when(step == 0)
        def _():
            small_ref[...] = jnp.zeros_like(small_ref)
            dmod_ref[...] = jnp.zeros_like(dmod_ref)
            vec_ref[...] = jnp.zeros_like(vec_ref)
            dmix_ref[...] = jnp.zeros_like(dmix_ref)
            halo_ref[...] = jnp.zeros_like(halo_ref)
            if n_red:
                scatter().start()

        if n_red:
            @pl.when(step == 1)
            def _():
                scatter().exchange()

            @pl.when(step == N_TILES // 2)
            def _():
                scatter().fold()

        def acc(row, lo, val):
            hi = lo + val.shape[1]
            vec_ref[row:row + 1, lo:hi] += jnp.sum(val, axis=0, keepdims=True)

        xt = x_ref[...]
        yt = y_ref[...]
        shift = mod_ref[layer:layer + 1, 0:D_MODEL]
        scale = mod_ref[layer:layer + 1, D_MODEL:2 * D_MODEL]
        gate = mod_ref[layer:layer + 1, 2 * D_MODEL:]
        ln_gain = lng_ref[layer:layer + 1, :]

        zn, zrstd = _ln(ALPHA * xt + gate * yt)
        if is_last:
            diff = a_ref[...] - b_ref[...]
            acc(VEC_LOSS, 0, diff * diff)
            dout = diff * (1.0 / D_MODEL)
        else:
            dout = a_ref[...]
        acc(VEC_LNG, 0, dout * zn)
        acc(VEC_LNB, 0, dout)
        dz = _ln_bwd(dout * ln_gain, zn, zrstd)
        acc(VEC_GATE, 0, dz * yt)
        dy = (dz * gate).astype(BF16)
        dy_ref[...] = dy
        dcat = _dot_nt(dy, wout_ref[...])

        proj = proj_ref[...]
        prev = jnp.where(tile > 0, prev_ref[...], 0.0)
        cat, k = _mix_forward(proj, prev, tile, weights, cdf_ref[...])
        cat_ref[...] = cat.astype(BF16)

        dga, dq = [], []
        for g in range(N_GROUPS):
            sl = slice(g * GROUP, (g + 1) * GROUP)
            pscale = weights.pool_scale(g)
            dya = dcat[:, sl]
            dyp = dya * k["ga_act"][:, sl]
            dga.append(dya * k["pw"][g] * pscale * k["ga_grad"][:, sl])
            acc(VEC_POOL, g * GROUP, dyp * k["pw"][g])
            dpw = (dyp * pscale).astype(BF16)
            rows = pl.ds(base + ROW_WPOOL + g * GROUP, GROUP)
            small_ref[rows, :] += _dot_tn(k["pooled"][g], dpw)
            dq.append(_dot_nt(dpw, weights.pool(g)))
        dpooled = jnp.concatenate(dq, axis=1)
        scaled = jnp.concatenate([dq[g] * k["inv_counts"][g] for g in range(N_GROUPS)], axis=1)
        sums = _window_sums(jnp.concatenate([scaled, halo_ref[...]], axis=0), False)
        halo_ref[...] = scaled[0:HALO]
        dxa = jnp.concatenate(sums, axis=1) - dpooled

        du, dv, dgb = [], [], []
        for h in range(N_HEADS):
            sl = slice(h * HEAD, (h + 1) * HEAD)
            dyb = dcat[:, D_POOL + h * HEAD:D_POOL + (h + 1) * HEAD]
            m_h = k["mixed"][h]
            ug = k["u_act"][:, sl] * dyb
            du.append(dyb * m_h * k["gb_act"][:, sl] * k["u_grad"][:, sl])
            dgb.append(ug * m_h * k["gb_grad"][:, sl])
            dmixed = ug * k["gb_act"][:, sl]
            dmixed_bf = dmixed.astype(BF16)
            w_h = weights.mix(h)
            dvln_parts = []
            dmix_sum = dmix_ref[h]
            wsgu_rows = pl.ds(base + ROW_WSGU + h * CHUNK, CHUNK)
            dws = small_ref[wsgu_rows, :]
            for c in range(TM // CHUNK):
                cs = slice(c * CHUNK, (c + 1) * CHUNK)
                dmix_sum = dmix_sum + dmixed[cs]
                dws = dws + _dot_nt(dmixed_bf[cs], k["vln"][h][cs])
                dvln_parts.append(_dot_tn(w_h, dmixed_bf[cs]))
            dmix_ref[h] = dmix_sum
            small_ref[wsgu_rows, :] = dws
            dvln = jnp.concatenate(dvln_parts, axis=0)
            acc(VEC_SGU, h * HEAD, dvln * k["vn"][h])
            acc(VEC_SGU, D_SGU + h * HEAD, dvln)
            dvv = _ln_bwd(dvln * weights.ln_gain(h), k["vn"][h], k["vrstd"][h])
            dv.append(dvv * k["v_grad"][:, sl])

        dproj = jnp.concatenate([dxa] + dga + du + dv + dgb, axis=1).astype(BF16)
        dproj_ref[...] = dproj
        dh = _dot(dproj, wint_ref[...])

        xn, xrstd = _ln(xt)
        h_ref[...] = (xn * (1.0 + scale) + shift).astype(BF16)
        acc(VEC_SCALE, 0, dh * xn)
        acc(VEC_SHIFT, 0, dh)
        dx_ref[...] = _ln_bwd(dh * (1.0 + scale), xn, xrstd) + ALPHA * dz

        @pl.when(step == N_TILES - 1)
        def _():
            def put(row0, vec_row, lo, n):
                for r in range(n):
                    small_ref[base + row0 + r:base + row0 + r + 1, :] = (
                        vec_ref[vec_row:vec_row + 1, lo + r * 128:lo + (r + 1) * 128])

            put(ROW_PSCALE, VEC_POOL, 0, 4)
            put(ROW_SLNG, VEC_SGU, 0, 4)
            put(ROW_SLNB, VEC_SGU, D_SGU, 4)
            put(ROW_LNG, VEC_LNG, 0, 8)
            put(ROW_LNB, VEC_LNB, 0, 8)
            ones = jnp.ones((8, HEAD), F32)
            t = lax.broadcasted_iota(jnp.int32, (CHUNK, CHUNK), 0)
            s = lax.broadcasted_iota(jnp.int32, (CHUNK, CHUNK), 1)
            for h in range(N_HEADS):
                bias_rows = lax.dot_general(ones, dmix_ref[h], (((1,), (1,)), ((), ())),
                                            preferred_element_type=F32, precision=lax.Precision.HIGHEST)
                small_ref[base + ROW_BSGU + h:base + ROW_BSGU + h + 1, :] = bias_rows[0:1]
                rows = pl.ds(base + ROW_WSGU + h * CHUNK, CHUNK)
                small_ref[rows, :] = jnp.where(t >= s, small_ref[rows, :], 0.0)
            pieces = ((0, VEC_SHIFT, 0, 768),
                      (1, VEC_SHIFT, 768, 256), (1, VEC_SCALE, 0, 512),
                      (2, VEC_SCALE, 512, 512), (2, VEC_GATE, 0, 256),
                      (3, VEC_GATE, 256, 768))
            filled = [0] * 4
            for q, vec_row, lo, n in pieces:
                row = 4 * layer + q
                dmod_ref[row:row + 1, filled[q]:filled[q] + n] = vec_ref[vec_row:vec_row + 1, lo:lo + n]
                filled[q] += n
            if n_carry:
                for other in range(layer + 1, DEPTH):
                    rows = pl.ds(other * PACK_ROWS, PACK_ROWS)
                    small_ref[rows, :] = carry_refs[0][rows, :]
                    dmod_ref[4 * other:4 * other + 4, :] = carry_refs[1][4 * other:4 * other + 4, :]
            loss_ref[...] = vec_ref[VEC_LOSS:VEC_LOSS + 1, :]
            if n_red:
                scatter().finish()
                scatter().wait_sends()

    rev = lambda w: pl.BlockSpec((TM, w), lambda i: (N_TILES - 1 - i, 0))
    prev_spec = pl.BlockSpec(
        (HALO, D_POOL), lambda i: (jnp.maximum((N_TILES - 1 - i) * (TM // HALO) - 1, 0), 0))
    comm_scratch = []
    for p in reduce:
        comm_scratch += _ChipReduceScatter.buffers(p.shape[2], p.shape[3], p.dtype)
    if n_red:
        comm_scratch += _ChipReduceScatter.semaphores(n_red)
    return pl.pallas_call(
        body,
        name=name,
        grid=(N_TILES,),
        in_specs=[rev(D_MODEL), rev(D_MODEL) if is_last else pl.BlockSpec((TM, D_MODEL), lambda i: (0, 0)),
                  rev(D_MODEL), rev(D_IN), prev_spec, rev(D_MODEL), rev(2 * D_SGU),
                  _const_spec((DEPTH, 3 * D_MODEL)), _const_spec((D_IN, D_MODEL)), _const_spec((D_MODEL, D_MODEL))]
                 + [_const_spec(s) for s in SMALL_SPECS] + [_const_spec((DEPTH, D_MODEL))]
                 + [_const_spec(c.shape) for c in carry] + [ANY] * n_red,
        out_specs=[rev(D_MODEL), rev(D_IN), rev(D_MODEL), rev(D_MODEL), rev(D_MODEL),
                   _const_spec((DEPTH * PACK_ROWS, 128)), _const_spec((8, DMOD_COLS)), _const_spec((1, D_MODEL))]
                  + [_const_spec(p.shape[2:]) for p in reduce],
        out_shape=[jax.ShapeDtypeStruct((SEQ, D_MODEL), F32), jax.ShapeDtypeStruct((SEQ, D_IN), BF16),
                   jax.ShapeDtypeStruct((SEQ, D_MODEL), BF16), jax.ShapeDtypeStruct((SEQ, D_MODEL), BF16),
                   jax.ShapeDtypeStruct((SEQ, D_MODEL), BF16), jax.ShapeDtypeStruct((DEPTH * PACK_ROWS, 128), F32),
                   jax.ShapeDtypeStruct((8, DMOD_COLS), F32), jax.ShapeDtypeStruct((1, D_MODEL), F32)]
                  + [jax.ShapeDtypeStruct(p.shape[2:], F32) for p in reduce],
        scratch_shapes=[pltpu.VMEM((8, D_MODEL), F32), pltpu.VMEM((N_HEADS, CHUNK, HEAD), F32),
                        pltpu.VMEM((HALO, D_POOL), F32)] + comm_scratch,
        compiler_params=pltpu.CompilerParams(dimension_semantics=("arbitrary",), vmem_limit_bytes=VMEM_LIMIT),
    )(a, b, x, proj, proj, y, cdf, mod, w_int, w_outf, *small, ln_g, *carry, *reduce)


def _grad_matmul(lhs, rhs, block_cols, name):
    m, n = lhs.shape[1], rhs.shape[1]

    def body(lhs_ref, rhs_ref, out_ref):
        out_ref[...] = _dot_tn(lhs_ref[...], rhs_ref[...]).astype(BF16)

    return pl.pallas_call(
        body,
        name=name,
        grid=(m // block_cols,),
        in_specs=[pl.BlockSpec((SEQ, block_cols), lambda j: (0, j)), pl.BlockSpec((SEQ, n), lambda j: (0, 0))],
        out_specs=pl.BlockSpec((block_cols, n), lambda j: (j, 0)),
        out_shape=jax.ShapeDtypeStruct((m, n), BF16),
        compiler_params=pltpu.CompilerParams(dimension_semantics=("arbitrary",), vmem_limit_bytes=VMEM_LIMIT),
    )(lhs, rhs)


def _adamw_math(w, g, m, v):
    m = ADAM_B1 * m + (1.0 - ADAM_B1) * g
    v = ADAM_B2 * v + (1.0 - ADAM_B2) * (g * g)
    m_hat = m / (1.0 - ADAM_B1 ** ADAM_STEP)
    v_hat = v / (1.0 - ADAM_B2 ** ADAM_STEP)
    delta = -ADAM_LR * (m_hat / (jnp.sqrt(v_hat) + ADAM_EPS) + ADAM_WD * w)
    return delta, m, v


def _adamw(w, grads, m, v, block_rows, name):
    rows, cols = grads[0].shape
    blocks = rows // block_rows

    def body(w_ref, m_ref, v_ref, *rest):
        g_refs, (g_ref, d_ref, nm_ref, nv_ref) = rest[:DEPTH], rest[DEPTH:]
        for layer in range(DEPTH):
            @pl.when(pl.program_id(0) == layer)
            def _():
                g = g_refs[layer][...]
                g_ref[...] = g
                d_ref[...], nm_ref[...], nv_ref[...] = _adamw_math(w_ref[...], g, m_ref[...], v_ref[...])

    def grad_spec(layer):
        return pl.BlockSpec((block_rows, cols),
                            lambda l, i: (jnp.where(l == layer, i, jnp.where(l < layer, 0, blocks - 1)), 0))

    spec = pl.BlockSpec((block_rows, cols), lambda l, i: (l * blocks + i, 0))
    return pl.pallas_call(
        body,
        name=name,
        grid=(DEPTH, blocks),
        in_specs=[spec] * 3 + [grad_spec(layer) for layer in range(DEPTH)],
        out_specs=[spec] * 4,
        out_shape=[jax.ShapeDtypeStruct(w.shape, F32)] * 4,
        compiler_params=pltpu.CompilerParams(dimension_semantics=("arbitrary", "arbitrary"),
                                             vmem_limit_bytes=VMEM_LIMIT),
    )(w, m, v, *grads)


def _adamw_ada(w, m, v, act_t, dmod_cols, name):
    cols = w.shape[2]

    rows = 256

    def body(w_ref, m_ref, v_ref, act_ref, dmod_ref, g_ref, d_ref, nm_ref, nv_ref):
        act = act_ref[...]
        dm = dmod_ref[0]
        g = act[:, 0:1] * dm[0:1, :]
        for b in range(1, N_DEV):
            g = g + act[:, b:b + 1] * dm[b:b + 1, :]
        g_ref[0] = g
        d_ref[0], nm_ref[0], nv_ref[0] = _adamw_math(w_ref[0], g, m_ref[0], v_ref[0])

    spec = pl.BlockSpec((1, rows, cols), lambda l, i: (l, i, 0))
    return pl.pallas_call(
        body,
        name=name,
        grid=(DEPTH, D_MODEL // rows),
        in_specs=[spec, spec, spec, pl.BlockSpec((rows, N_DEV), lambda l, i: (i, 0)),
                  pl.BlockSpec((1, N_DEV, cols), lambda l, i: (l, 0, 0))],
        out_specs=[spec] * 4,
        out_shape=[jax.ShapeDtypeStruct(w.shape, F32)] * 4,
        compiler_params=pltpu.CompilerParams(dimension_semantics=("arbitrary", "arbitrary"),
                                             vmem_limit_bytes=VMEM_LIMIT),
    )(w, m, v, act_t, dmod_cols)


def _adamw_bias(w, m, v, dmod_all, name):
    def body(w_ref, m_ref, v_ref, dmod_ref, g_ref, d_ref, nm_ref, nv_ref):
        g = dmod_ref[0]
        for b in range(1, N_DEV):
            g = g + dmod_ref[b]
        g_ref[...] = g
        d_ref[...], nm_ref[...], nv_ref[...] = _adamw_math(w_ref[...], g, m_ref[...], v_ref[...])

    return pl.pallas_call(
        body,
        name=name,
        out_shape=[jax.ShapeDtypeStruct(w.shape, F32)] * 4,
        compiler_params=pltpu.CompilerParams(vmem_limit_bytes=VMEM_LIMIT),
    )(w, m, v, dmod_all)


MESH = pl.DeviceIdType.MESH
SIBLING = 1
ANY = pl.BlockSpec(memory_space=pl.ANY)
VMEM = pl.BlockSpec(memory_space=pltpu.VMEM)


def _me():
    return lax.axis_index("x"), lax.axis_index("y"), lax.axis_index("c")


def _peer(r):
    x, y, c = _me()
    return (1 - x if r & 4 else x, 1 - y if r & 2 else y, 1 - c if r & 1 else c)


def _index(dev):
    return 4 * dev[0] + 2 * dev[1] + dev[2]


def _remote(src, dst, send_sem, recv_sem, dev):
    return pltpu.make_async_remote_copy(src_ref=src, dst_ref=dst, send_sem=send_sem, recv_sem=recv_sem,
                                        device_id=dev, device_id_type=MESH)


ACROSS_X, ACROSS_Y, ACROSS_BOTH = 4, 2, 6
GATHER_SEMS = 11


class _TwoLevelGather:
    def __init__(self, out, send_sems, recv_sems, src=None):
        self.out, self.send_sems, self.recv_sems, self.src = out, send_sems, recv_sems, src
        self.rows = (out.shape[0] // N_DEV) if len(out.shape) == 2 else out.shape[1]
        self.half = self.rows // 2

    def _slot(self, block):
        if len(self.out.shape) == 2:
            return self.out.at[pl.ds(pl.multiple_of(_index(block) * self.rows, self.rows), self.rows)]
        return self.out.at[_index(block)]

    def _copy(self, k, block, part, to, src=None):
        slot = self._slot(block)
        if part is not None:
            rows = pl.ds(part * self.half, self.half)
            slot = slot.at[rows]
            src = None if src is None else src.at[rows]
        return _remote(slot if src is None else src, slot, self.send_sems.at[k], self.recv_sems.at[k], to)

    def _mine(self):
        me = _me()
        src = self._slot(me) if self.src is None else self.src
        x, y = _peer(ACROSS_X), _peer(ACROSS_Y)
        return [self._copy(1, me, 0, x, src), self._copy(3, me, 1, y, src), self._copy(0, me, None, _peer(SIBLING), src),
                self._copy(2, me, 1, x, src), self._copy(4, me, 0, y, src)]

    def _relayed(self):
        return [self._copy(5, _peer(ACROSS_X), 0, _peer(ACROSS_Y)), self._copy(6, _peer(ACROSS_Y), 1, _peer(ACROSS_X))]

    def _passed(self):
        sib, far = _peer(SIBLING), _peer(ACROSS_BOTH)
        return [self._copy(7, _peer(ACROSS_X), None, sib), self._copy(8, _peer(ACROSS_Y), None, sib),
                self._copy(9, far, 0, sib), self._copy(10, far, 1, sib)]

    def _arrival(self, k, r, part):
        return self._copy(k, _peer(r), part, _me())

    def send_first(self):
        for cp in self._mine()[:3]:
            cp.start()

    def send_second(self):
        for cp in self._mine()[3:]:
            cp.start()

    def send_mine(self):
        self.send_first()
        self.send_second()

    def relay(self):
        relayed = self._relayed()
        self._arrival(1, ACROSS_X, 0).wait_recv()
        relayed[0].start()
        self._arrival(3, ACROSS_Y, 1).wait_recv()
        relayed[1].start()

    def pass_near(self):
        passed = self._passed()
        self._arrival(2, ACROSS_X, 1).wait_recv()
        passed[0].start()
        self._arrival(4, ACROSS_Y, 0).wait_recv()
        passed[1].start()

    def pass_far(self):
        passed = self._passed()
        self._arrival(5, ACROSS_BOTH, 0).wait_recv()
        passed[2].start()
        self._arrival(6, ACROSS_BOTH, 1).wait_recv()
        passed[3].start()

    def pass_on(self):
        self.pass_near()
        self.pass_far()

    def wait_sibling(self):
        self._arrival(0, SIBLING, None).wait_recv()

    def wait_passed(self, r):
        if r == ACROSS_BOTH:
            self._arrival(9, r ^ SIBLING, 0).wait_recv()
            self._arrival(10, r ^ SIBLING, 1).wait_recv()
        else:
            self._arrival(7 if r == ACROSS_X else 8, r ^ SIBLING, None).wait_recv()

    def wait_rest(self):
        self.wait_sibling()
        for r in (ACROSS_X, ACROSS_Y, ACROSS_BOTH):
            self.wait_passed(r)

    def wait_sends(self):
        for cp in self._mine() + self._relayed() + self._passed():
            cp.wait_send()


class _ChipReduceScatter:
    SLOTS = 6

    def __init__(self, arrays, l_sem, d_send, d_recv, i_send, i_recv):
        self.arrays = arrays
        self.l_sem, self.d_send, self.d_recv, self.i_send, self.i_recv = l_sem, d_send, d_recv, i_send, i_recv

    @staticmethod
    def buffers(rows, cols, dtype, staged=True):
        stage = [pltpu.VMEM((4, rows, cols), dtype)] if staged else []
        return stage + [pltpu.VMEM((4, rows, cols), dtype), pltpu.VMEM((3, rows, cols), dtype),
                        pltpu.VMEM((2, rows, cols), dtype), pltpu.VMEM((2, rows // 2, cols), dtype)]

    @classmethod
    def semaphores(cls, n):
        return [pltpu.SemaphoreType.DMA((n,)), pltpu.SemaphoreType.DMA((n,)), pltpu.SemaphoreType.DMA((n,)),
                pltpu.SemaphoreType.DMA((n, cls.SLOTS)), pltpu.SemaphoreType.DMA((n, cls.SLOTS))]

    def _pick(self, which):
        return list(enumerate(self.arrays)) if which is None else [(n, self.arrays[n]) for n in which]

    def _staging(self, which):
        c = _me()[2]
        return [pltpu.make_async_copy(a["part"].at[pl.ds(0, 4), c], a["stage"], self.l_sem.at[n])
                for n, a in self._pick(which) if a["staged"]]

    def _first(self, which):
        other = 1 - _me()[2]
        return [_remote(a["part"].at[pl.ds(0, 4), other], a["sib"], self.d_send.at[n], self.d_recv.at[n],
                        _peer(SIBLING)) for n, a in self._pick(which)]

    @staticmethod
    def _halves(a):
        half = a["rcv"].shape[1] // 2
        return pl.ds(0, half), pl.ds(half, half)

    def _hops(self, n, a):
        h0, h1 = self._halves(a)
        x, y = _peer(ACROSS_X), _peer(ACROSS_Y)
        snd, rcv, relay = a["snd"], a["rcv"], a["relay"]
        pairs = [(snd.at[2, h0], relay.at[0], x), (snd.at[2, h1], relay.at[1], y),
                 (snd.at[0, h0], rcv.at[0, h0], x), (snd.at[0, h1], rcv.at[0, h1], x),
                 (snd.at[1, h1], rcv.at[1, h1], y), (snd.at[1, h0], rcv.at[1, h0], y)]
        return [_remote(s, d, self.i_send.at[n, k], self.i_recv.at[n, k], to) for k, (s, d, to) in enumerate(pairs)]

    def _mine(self, a, chip, rows=None):
        src = a["stage"].at[chip] if a["staged"] else a["part"].at[chip, _me()[2]]
        mine, sib = (src[...], a["sib"][chip]) if rows is None else (src[rows, :], a["sib"][chip, rows, :])
        return mine.astype(F32) + sib.astype(F32)

    def start(self, which=None):
        for cp in self._staging(which) + self._first(which):
            cp.start()

    def exchange(self, which=None):
        for cp in self._staging(which):
            cp.wait()
        for cp in self._first(which):
            cp.wait_recv()
        chip = lambda dev: 2 * dev[0] + dev[1]
        across_x, across_y, far = chip(_peer(ACROSS_X)), chip(_peer(ACROSS_Y)), chip(_peer(ACROSS_BOTH))
        picked = self._pick(which)
        for n, a in picked:
            hops = self._hops(n, a)
            a["snd"][2] = self._mine(a, far).astype(a["snd"].dtype)
            hops[0].start()
            hops[1].start()
        for n, a in picked:
            h0, h1 = self._halves(a)
            hops = self._hops(n, a)
            dtype = a["snd"].dtype
            a["snd"][0, h0, :] = self._mine(a, across_x, h0).astype(dtype)
            hops[2].start()
            a["snd"][1, h1, :] = self._mine(a, across_y, h1).astype(dtype)
            hops[4].start()

    def fold(self, which=None):
        chip = lambda dev: 2 * dev[0] + dev[1]
        across_x, across_y = chip(_peer(ACROSS_X)), chip(_peer(ACROSS_Y))
        for n, a in self._pick(which):
            h0, h1 = self._halves(a)
            hops = self._hops(n, a)
            dtype = a["snd"].dtype
            hops[1].wait_recv()
            a["snd"][0, h1, :] = (self._mine(a, across_x, h1) + a["relay"][1].astype(F32)).astype(dtype)
            hops[3].start()
            hops[0].wait_recv()
            a["snd"][1, h0, :] = (self._mine(a, across_y, h0) + a["relay"][0].astype(F32)).astype(dtype)
            hops[5].start()

    def finish(self, which=None):
        x, y, _ = _me()
        home = 2 * x + y
        for n, a in self._pick(which):
            hops = self._hops(n, a)
            a["out"][...] = self._mine(a, home)
            hops[2].wait_recv()
            hops[3].wait_recv()
            a["out"][...] += a["rcv"][0].astype(F32)
            hops[4].wait_recv()
            hops[5].wait_recv()
            a["out"][...] += a["rcv"][1].astype(F32)

    def wait_sends(self, which=None):
        for cp in self._first(which):
            cp.wait_send()
        for n, a in self._pick(which):
            for cp in self._hops(n, a):
                cp.wait_send()


def _direct_exchange(src_of, dst_of, send_sems, recv_sems):
    me = _me()
    copies = [_remote(src_of(_peer(r)), dst_of(me), send_sems.at[r - 1], recv_sems.at[r - 1], _peer(r))
              for r in range(1, N_DEV)]
    for cp in copies:
        cp.start()
    return copies


def _wait_direct(copies):
    for cp in copies:
        cp.wait_recv()
    for cp in copies:
        cp.wait_send()


def _head(x, c_row, w_ada, b_ada, wint_loc, wout_loc):
    cols = w_ada.shape[2]
    shard = wint_loc.shape[0]
    pair = 2 * shard

    def body(x_ref, c_ref, wada_ref, bada_ref, wint_hbm, wout_hbm, act_all, mod_ref, proj_hbm, wint_out, wout_full,
             wint_full, h_buf, proj_buf, act_src, part, mod_recv, w_send, w_recv, w_local, p_sems,
             a_send, a_recv, m_send, m_recv):
        me = _me()
        mine = _index(me)

        cval = c_ref[...]
        act_src[...] = jnp.zeros_like(act_src)
        act_src[0:1, :] = cval * jax.nn.sigmoid(cval)
        act_all[mine] = act_src[...]
        act_copies = _direct_exchange(lambda p: act_src, lambda m: act_all.at[_index(m)], a_send, a_recv)

        gather_in = _TwoLevelGather(wint_full, w_send.at[0], w_recv.at[0], src=wint_hbm)
        gather_out = _TwoLevelGather(wout_full, w_send.at[1], w_recv.at[1], src=wout_hbm)
        own_in = pltpu.make_async_copy(wint_hbm, gather_in._slot(me), w_local.at[0])
        own_out = pltpu.make_async_copy(wout_hbm, wout_full.at[mine], w_local.at[1])
        own_in.start()
        own_out.start()
        gather_in.send_first()

        _wait_direct(act_copies)
        acts = jnp.concatenate([act_all[j, 0:1, :] for j in range(N_DEV)], axis=0)
        part[...] = jnp.zeros_like(part)
        for layer in range(DEPTH):
            res = lax.dot_general(acts, wada_ref[layer], (((1,), (0,)), ((), ())), preferred_element_type=F32,
                                  precision=lax.Precision.HIGHEST)
            for b in range(N_DEV):
                part[b, layer:layer + 1, :] = res[b:b + 1, :]
        mod_recv[mine] = part[mine]
        mod_copies = _direct_exchange(lambda p: part.at[_index(p)], lambda m: mod_recv.at[_index(m)], m_send, m_recv)
        gather_in.send_second()
        gather_out.send_mine()

        _wait_direct(mod_copies)
        for layer in range(DEPTH):
            for j in range(N_DEV):
                sl = slice(j * cols, (j + 1) * cols)
                mod_ref[layer:layer + 1, sl] = mod_recv[j, layer:layer + 1, :] + bada_ref[layer:layer + 1, sl]
        shift = mod_ref[0:1, 0:D_MODEL]
        scale = mod_ref[0:1, D_MODEL:2 * D_MODEL]
        for t in range(N_TILES):
            rows = pl.ds(t * TM, TM)
            xn, _ = _ln(x_ref[rows, :])
            h_buf[rows, :] = (xn * (1.0 + scale) + shift).astype(BF16)

        chip_of = lambda dev: 2 * dev[0] + dev[1]
        writes = []

        def project(n, dev):
            first = pl.multiple_of(chip_of(dev) * pair, pair)
            if n >= 2:
                writes[n - 2].wait()
            proj_buf[n % 2] = _dot_nt(h_buf[...], wint_full[pl.ds(first, pair), :])
            cp = pltpu.make_async_copy(proj_buf.at[n % 2], proj_hbm.at[:, pl.ds(first, pair)], p_sems.at[n % 2])
            cp.start()
            writes.append(cp)

        gather_in.relay()
        own_in.wait()
        gather_in.wait_sibling()
        project(0, me)
        gather_in.pass_near()
        gather_in.wait_passed(ACROSS_X)
        project(1, _peer(ACROSS_X))
        gather_in.wait_passed(ACROSS_Y)
        project(2, _peer(ACROSS_Y))
        gather_in.pass_far()
        gather_in.wait_passed(ACROSS_BOTH)
        project(3, _peer(ACROSS_BOTH))
        keep = pltpu.make_async_copy(wint_full, wint_out, w_local.at[2])
        keep.start()

        gather_out.relay()
        gather_out.pass_on()
        gather_out.wait_rest()
        gather_in.wait_sends()
        gather_out.wait_sends()
        own_out.wait()
        writes[2].wait()
        writes[3].wait()
        keep.wait()

    return pl.pallas_call(
        body,
        name="head_gather_project",
        in_specs=[VMEM, VMEM, VMEM, VMEM, ANY, ANY],
        out_specs=[VMEM, VMEM, ANY, ANY, ANY],
        out_shape=[jax.ShapeDtypeStruct((N_DEV, 8, D_MODEL), F32), jax.ShapeDtypeStruct((DEPTH, 3 * D_MODEL), F32),
                   jax.ShapeDtypeStruct((SEQ, D_IN), F32), jax.ShapeDtypeStruct((D_IN, D_MODEL), BF16),
                   jax.ShapeDtypeStruct((N_DEV,) + wout_loc.shape, BF16)],
        scratch_shapes=[pltpu.VMEM((D_IN, D_MODEL), BF16), pltpu.VMEM((SEQ, D_MODEL), BF16),
                        pltpu.VMEM((2, SEQ, pair), F32),
                        pltpu.VMEM((8, D_MODEL), F32), pltpu.VMEM((N_DEV, 8, cols), F32),
                        pltpu.VMEM((N_DEV, 8, cols), F32),
                        pltpu.SemaphoreType.DMA((2, GATHER_SEMS)), pltpu.SemaphoreType.DMA((2, GATHER_SEMS)),
                        pltpu.SemaphoreType.DMA((3,)), pltpu.SemaphoreType.DMA((2,)),
                        pltpu.SemaphoreType.DMA((7,)), pltpu.SemaphoreType.DMA((7,)),
                        pltpu.SemaphoreType.DMA((7,)), pltpu.SemaphoreType.DMA((7,))],
        compiler_params=pltpu.CompilerParams(vmem_limit_bytes=VMEM_LIMIT),
    )(x, c_row, w_ada, b_ada, wint_loc, wout_loc)


def _grad_tail(dproj, h, cat, dy, small, dmod8, loss_lanes):
    shard_in, shard_out, shard_small = D_IN // N_DEV, D_MODEL // N_DEV, small.shape[2]
    W_IN, W_OUT, SMALL = 0, 1, 2

    def body(dproj_hbm, h_hbm, cat_hbm, dy_hbm, small_hbm, dmod_ref, lanes_ref,
             gwin_ref, gwout_ref, stot_ref, dmod_all, loss_ref,
             dproj_v, h_v, cat_v, dy_v, part_in, part_out, own_small, loss_src, loss_all, *rest):
        bufs, rest = rest[:13], rest[13:]
        load_sems, rs_sems = rest[0], rest[1:6]
        m_send, m_recv, g_send, g_recv, s_send, s_recv = rest[6:]
        mine = _index(_me())

        loads = [pltpu.make_async_copy(s, d, load_sems.at[n]) for n, (s, d) in enumerate(
            ((cat_hbm, cat_v), (dy_hbm, dy_v), (dproj_hbm, dproj_v), (h_hbm, h_v)))]
        for cp in loads:
            cp.start()
        arrays = [dict(part=part_in, out=gwin_ref, staged=False, sib=bufs[0], snd=bufs[1], rcv=bufs[2], relay=bufs[3]),
                  dict(part=part_out, out=gwout_ref, staged=False, sib=bufs[4], snd=bufs[5], rcv=bufs[6],
                       relay=bufs[7]),
                  dict(part=small_hbm, out=own_small, staged=True, stage=bufs[8], sib=bufs[9], snd=bufs[10],
                       rcv=bufs[11], relay=bufs[12])]
        scatter = _ChipReduceScatter(arrays, *rs_sems)
        scatter.start([SMALL])
        dmod_all[mine] = dmod_ref[...]
        dmod_copies = _direct_exchange(lambda p: dmod_ref, lambda m: dmod_all.at[_index(m)], m_send, m_recv)
        loss_src[...] = jnp.full(loss_src.shape, (0.5 / D_MODEL) * jnp.sum(lanes_ref[...]), F32)
        loss_all[mine] = loss_src[...]
        loss_copies = _direct_exchange(lambda p: loss_src, lambda m: loss_all.at[_index(m)], s_send, s_recv)

        loads[0].wait()
        loads[1].wait()
        for blk in range(2):
            res = _dot_tn(cat_v[:, blk * 512:(blk + 1) * 512], dy_v[...]).astype(BF16)
            for s in range(4):
                part_out[2 * blk + s // 2, s % 2] = res[s * shard_out:(s + 1) * shard_out]
        scatter.start([W_OUT])
        scatter.exchange([SMALL])

        gather = _TwoLevelGather(stot_ref, g_send, g_recv)
        loads[2].wait()
        loads[3].wait()
        for chip in range(4):
            res = _dot_tn(dproj_v[:, chip * 2 * shard_in:(chip + 1) * 2 * shard_in], h_v[...]).astype(BF16)
            part_in[chip, 0] = res[:shard_in]
            part_in[chip, 1] = res[shard_in:]
            if chip == 0:
                scatter.exchange([W_OUT])
                scatter.fold([SMALL])
            if chip == 1:
                scatter.fold([W_OUT])
                scatter.finish([SMALL])
                stot_ref[mine] = own_small[...]
                gather.send_mine()
            if chip == 2:
                gather.relay()
        scatter.start([W_IN])
        scatter.finish([W_OUT])
        scatter.exchange([W_IN])
        gather.pass_on()
        scatter.fold([W_IN])
        gather.wait_rest()
        _wait_direct(dmod_copies)
        _wait_direct(loss_copies)
        total = loss_all[0]
        for j in range(1, N_DEV):
            total = total + loss_all[j]
        loss_ref[...] = total
        scatter.finish([W_IN])
        gather.wait_sends()
        scatter.wait_sends()

    buffers = _ChipReduceScatter.buffers
    comm_scratch = (buffers(shard_in, D_MODEL, BF16, staged=False) + buffers(shard_out, D_MODEL, BF16, staged=False)
                    + buffers(shard_small, 128, F32))
    comm_scratch += [pltpu.SemaphoreType.DMA((4,))] + _ChipReduceScatter.semaphores(3)
    comm_scratch += [pltpu.SemaphoreType.DMA((n,)) for n in (7, 7, GATHER_SEMS, GATHER_SEMS, 7, 7)]
    return pl.pallas_call(
        body,
        name="grad_tail",
        in_specs=[ANY] * 5 + [VMEM, VMEM],
        out_specs=[VMEM] * 5,
        out_shape=[jax.ShapeDtypeStruct((shard_in, D_MODEL), F32), jax.ShapeDtypeStruct((shard_out, D_MODEL), F32),
                   jax.ShapeDtypeStruct((N_DEV, shard_small, 128), F32),
                   jax.ShapeDtypeStruct((N_DEV,) + dmod8.shape, F32), jax.ShapeDtypeStruct((8, 128), F32)],
        scratch_shapes=[pltpu.VMEM(dproj.shape, BF16), pltpu.VMEM(h.shape, BF16), pltpu.VMEM(cat.shape, BF16),
                        pltpu.VMEM(dy.shape, BF16), pltpu.VMEM((4, 2, shard_in, D_MODEL), BF16),
                        pltpu.VMEM((4, 2, shard_out, D_MODEL), BF16), pltpu.VMEM((shard_small, 128), F32),
                        pltpu.VMEM((8, 128), F32), pltpu.VMEM((N_DEV, 8, 128), F32)] + comm_scratch,
        compiler_params=pltpu.CompilerParams(vmem_limit_bytes=VMEM_LIMIT),
    )(dproj, h, cat, dy, small, dmod8, loss_lanes)


SMALL_NAMES = ("w_pool", "w_sgu", "pool_scale", "sgu_ln_g", "sgu_ln_b", "b_sgu", "ln_g", "ln_b")
SMALL_ROWS = (512, 512, 4, 4, 4, 4, 8, 8)


def _adamw_small(g_packed, ws, ms, vs, name):
    n = len(SMALL_NAMES)

    def body(g_ref, *refs):
        w_refs, m_refs, v_refs = refs[:n], refs[n:2 * n], refs[2 * n:3 * n]
        outs = refs[3 * n:]

        def update(p, at, g):
            delta, new_m, new_v = _adamw_math(w_refs[p][at], g, m_refs[p][at], v_refs[p][at])
            outs[p][at] = g
            outs[n + p][at] = delta
            outs[2 * n + p][at] = new_m
            outs[3 * n + p][at] = new_v

        row = 0
        for p, r in enumerate(SMALL_ROWS):
            shape = ws[p].shape
            for layer in range(DEPTH):
                first = layer * PACK_ROWS + row
                if len(shape) == 4:
                    for k in range(shape[1]):
                        update(p, (layer, k), g_ref[first + k * shape[2]:first + (k + 1) * shape[2], :])
                elif len(shape) == 3:
                    update(p, (layer,), g_ref[first:first + r, :])
                else:
                    g = jnp.concatenate([g_ref[first + k:first + k + 1, :] for k in range(r)], axis=1)
                    update(p, (slice(layer, layer + 1), slice(None)), g)
            row += r

    res = pl.pallas_call(
        body,
        name=name,
        out_shape=[jax.ShapeDtypeStruct(w.shape, F32) for w in ws] * 4,
        compiler_params=pltpu.CompilerParams(vmem_limit_bytes=VMEM_LIMIT),
    )(g_packed, *ws, *ms, *vs)
    return res[:n], res[n:2 * n], res[2 * n:3 * n], res[3 * n:]


def kernel(x, c, w_ada, b_ada, w_in, w_pool, pool_scale, sgu_ln_g, sgu_ln_b, w_sgu, b_sgu, w_out, ln_g, ln_b, loss_target, m_w_ada, m_b_ada, m_w_in, m_w_pool, m_pool_scale, m_sgu_ln_g, m_sgu_ln_b, m_w_sgu, m_b_sgu, m_w_out, m_ln_g, m_ln_b, v_w_ada, v_b_ada, v_w_in, v_w_pool, v_pool_scale, v_sgu_ln_g, v_sgu_ln_b, v_w_sgu, v_b_sgu, v_w_out, v_ln_g, v_ln_b):
    mine = _index(_me())
    small_w = dict(w_pool=w_pool, w_sgu=w_sgu, pool_scale=pool_scale, sgu_ln_g=sgu_ln_g, sgu_ln_b=sgu_ln_b,
                   b_sgu=b_sgu, ln_g=ln_g, ln_b=ln_b)
    small_m = dict(w_pool=m_w_pool, w_sgu=m_w_sgu, pool_scale=m_pool_scale, sgu_ln_g=m_sgu_ln_g,
                   sgu_ln_b=m_sgu_ln_b, b_sgu=m_b_sgu, ln_g=m_ln_g, ln_b=m_ln_b)
    small_v = dict(w_pool=v_w_pool, w_sgu=v_w_sgu, pool_scale=v_pool_scale, sgu_ln_g=v_sgu_ln_g,
                   sgu_ln_b=v_sgu_ln_b, b_sgu=v_b_sgu, ln_g=v_ln_g, ln_b=v_ln_b)

    wint_loc = jnp.transpose(w_in, (0, 2, 1)).astype(BF16)
    wout_loc = w_out.astype(BF16)
    act_slots, mod, proj0, wint0, wout0 = _head(x[0], c, w_ada, b_ada, wint_loc[0], wout_loc[0])
    act_all = act_slots[:, 0, :]
    w_int, w_outf = [wint0], [wout0.reshape(D_MODEL, D_MODEL)]
    small = (w_pool, pool_scale, sgu_ln_g, sgu_ln_b, w_sgu, jnp.transpose(b_sgu, (0, 2, 1)))

    acts, cur = [], x[0]
    for l in range(DEPTH):
        nxt = [wint_loc[l + 1], wout_loc[l + 1]] if l + 1 < DEPTH else []
        if l == 0:
            proj = proj0
            out, y, cdf, *gathered = _layer_forward(l, cur, mod, None, w_outf[l], small, ln_g, ln_b,
                                                    f"layer_fwd_{l}", gather=nxt, proj=proj0)
        else:
            out, proj, y, cdf, *gathered = _layer_forward(l, cur, mod, w_int[l], w_outf[l], small, ln_g, ln_b,
                                                          f"layer_fwd_{l}", gather=nxt)
        if gathered:
            w_int.append(gathered[0].reshape(D_IN, D_MODEL))
            w_outf.append(gathered[1].reshape(D_MODEL, D_MODEL))
        acts.append((cur, proj, y, cdf))
        cur = out

    shard_in, shard_out = D_IN // N_DEV, D_MODEL // N_DEV
    a, b = cur, loss_target[0]
    loss_lanes, carry, pending = None, (), []
    g_w_in_t, g_w_out = [None] * DEPTH, [None] * DEPTH
    for l in reversed(range(DEPTH)):
        dx, dproj, h, cat, dy, small_grads, dmod8, lanes, *shards = _layer_backward(
            l, a, b, *acts[l], mod, w_int[l], w_outf[l], small, ln_g, l == DEPTH - 1, f"layer_bwd_{l}",
            carry=carry, reduce=pending)
        if shards:
            g_w_in_t[l + 1], g_w_out[l + 1] = shards
        if l == DEPTH - 1:
            loss_lanes = lanes
        if l > 0:
            pending = [_grad_matmul(dproj, h, 640, f"grad_w_in_{l}").reshape(4, 2, shard_in, D_MODEL),
                       _grad_matmul(cat, dy, 512, f"grad_w_out_{l}").reshape(4, 2, shard_out, D_MODEL)]
        carry = (small_grads, dmod8)
        a = b = dx
    grad_x = a[None]

    g_w_in_t[0], g_w_out[0], small_tot, dmod_slots, loss_tile = _grad_tail(
        dproj, h, cat, dy, small_grads.reshape(4, 2, DEPTH * PACK_ROWS // N_DEV, 128), dmod8, loss_lanes)
    loss = loss_tile[0, 0]
    dmod_all = dmod_slots.reshape(N_DEV, DEPTH, 3 * D_MODEL)

    cols_ada = w_ada.shape[2]
    dmod_cols = jnp.transpose(lax.dynamic_slice_in_dim(dmod_all, mine * cols_ada, cols_ada, axis=2), (1, 0, 2))
    g_w_ada, d_w_ada, nm_w_ada, nv_w_ada = _adamw_ada(w_ada, m_w_ada, v_w_ada, jnp.transpose(act_all), dmod_cols,
                                                      "adamw_w_ada")
    g_b_ada, d_b_ada, nm_b_ada, nv_b_ada = _adamw_bias(b_ada, m_b_ada, v_b_ada, dmod_all, "adamw_b_ada")
    flat = lambda t: t.reshape(-1, t.shape[-1])
    to_t = lambda t: flat(jnp.transpose(t, (0, 2, 1)))
    from_t = lambda t: jnp.transpose(t.reshape(DEPTH, shard_in, D_MODEL), (0, 2, 1))
    g_w_in, d_w_in, nm_w_in, nv_w_in = [from_t(t) for t in _adamw(to_t(w_in), g_w_in_t, to_t(m_w_in), to_t(v_w_in),
                                                                  shard_in // 2, "adamw_w_in")]
    gwout, d_w_out, nm_w_out, nv_w_out = [t.reshape(w_out.shape) for t in _adamw(
        flat(w_out), g_w_out, flat(m_w_out), flat(v_w_out), shard_out, "adamw_w_out")]
    small_out = _adamw_small(small_tot.reshape(DEPTH * PACK_ROWS, 128), [small_w[n] for n in SMALL_NAMES],
                             [small_m[n] for n in SMALL_NAMES], [small_v[n] for n in SMALL_NAMES], "adamw_small")
    gs, ds, ms, vs = [dict(zip(SMALL_NAMES, group)) for group in small_out]

    def ordered(w_ada_, b_ada_, w_in_, small, w_out_):
        return (w_ada_, b_ada_, w_in_, small["w_pool"], small["pool_scale"], small["sgu_ln_g"], small["sgu_ln_b"],
                small["w_sgu"], small["b_sgu"], w_out_, small["ln_g"], small["ln_b"])

    return (loss, grad_x,
            *ordered(g_w_ada, g_b_ada, g_w_in, gs, gwout),
            *ordered(d_w_ada, d_b_ada, d_w_in, ds, d_w_out),
            *ordered(nm_w_ada, nm_b_ada, nm_w_in, ms, nm_w_out),
            *ordered(nv_w_ada, nv_b_ada, nv_w_in, vs, nv_w_out))
```

```python
import jax
import jax.numpy as jnp
from jax import lax
from jax.experimental import pallas as pl
from jax.experimental.pallas import tpu as pltpu

F32 = jnp.float32
BF16 = jnp.bfloat16

D_MODEL = 1024
SEQ = 2048
DEPTH = 2
D_POOL = 512
D_SGU = 512
D_IN = 2560
N_GROUPS = 4
GROUP = 128
N_HEADS = 4
HEAD = 128
CHUNK = 128
WINDOWS = (2, 4, 8, 16)
ALPHA = (2.0 * DEPTH) ** 0.25
LN_EPS = 1e-5
N_DEV = 8

ADAM_LR = 0.001
ADAM_B1 = 0.9
ADAM_B2 = 0.999
ADAM_EPS = 1e-08
ADAM_WD = 0.01
ADAM_STEP = 10

TM = 256
HALO = 16
N_TILES = SEQ // TM
VMEM_LIMIT = 60 * 1024 * 1024

ROW_WPOOL = 0
ROW_WSGU = 512
ROW_PSCALE = 1024
ROW_SLNG = 1028
ROW_SLNB = 1032
ROW_BSGU = 1036
ROW_LNG = 1040
ROW_LNB = 1048
PACK_ROWS = 1088
DMOD_COLS = DEPTH * 3 * D_MODEL // 8

SQRT_HALF = 0.7071067811865476
INV_SQRT_2PI = 0.3989422804014327


def _ln(x):
    mu = jnp.mean(x, axis=-1, keepdims=True)
    xc = x - mu
    var = jnp.mean(xc * xc, axis=-1, keepdims=True)
    rstd = lax.rsqrt(var + LN_EPS)
    return xc * rstd, rstd


def _ln_bwd(dxn, xn, rstd):
    m1 = jnp.mean(dxn, axis=-1, keepdims=True)
    m2 = jnp.mean(dxn * xn, axis=-1, keepdims=True)
    return rstd * (dxn - m1 - xn * m2)


def _normal_cdf(x):
    return 0.5 * (1.0 + lax.erf(x * SQRT_HALF))


def _gelu_parts(x, cdf, with_grad):
    if not with_grad:
        return x * cdf, None
    return x * cdf, cdf + x * (INV_SQRT_2PI * jnp.exp(-0.5 * x * x))


def _silu_parts(x):
    s = jax.nn.sigmoid(x)
    return x * s, s * (1.0 + x * (1.0 - s))


def _dot(a, b):
    return lax.dot_general(a, b, (((1,), (0,)), ((), ())), preferred_element_type=F32)


def _dot_nt(a, b):
    return lax.dot_general(a, b, (((1,), (1,)), ((), ())), preferred_element_type=F32)


def _dot_tn(a, b):
    return lax.dot_general(a, b, (((0,), (0,)), ((), ())), preferred_element_type=F32)


def _row_index(tile):
    return tile * TM + lax.broadcasted_iota(jnp.int32, (TM, 1), 0)


def _window_sums(ext, forward):
    n = TM + HALO
    cur = ext
    outs = []
    for g in range(N_GROUPS):
        step = 1 << g
        cur = cur + pltpu.roll(cur, step if forward else n - step, 0)
        rows = cur[HALO:, :GROUP] if forward else cur[:TM, :GROUP]
        outs.append(rows)
        cur = cur[:, GROUP:] if g + 1 < N_GROUPS else None
    return outs


def _inverse_counts(rows):
    return [1.0 / jnp.minimum(rows + 1, w).astype(F32) for w in WINDOWS]


def _tril_bf16(w):
    t = lax.broadcasted_iota(jnp.int32, (CHUNK, CHUNK), 0)
    s = lax.broadcasted_iota(jnp.int32, (CHUNK, CHUNK), 1)
    return jnp.where(t >= s, w, 0.0).astype(BF16)


class _MixWeights:
    def __init__(self, layer, wpool_ref, pscale_ref, slng_ref, slnb_ref, wsgu_ref, bsgut_ref):
        self.layer = layer
        self.wpool_ref, self.pscale_ref, self.slng_ref, self.slnb_ref = wpool_ref, pscale_ref, slng_ref, slnb_ref
        self.wsgu_ref, self.bsgut_ref = wsgu_ref, bsgut_ref

    def pool(self, g):
        return self.wpool_ref[self.layer, g].astype(BF16)

    def pool_scale(self, g):
        return self.pscale_ref[self.layer:self.layer + 1, g * GROUP:(g + 1) * GROUP]

    def ln_gain(self, h):
        return self.slng_ref[self.layer, h:h + 1, :]

    def ln_bias(self, h):
        return self.slnb_ref[self.layer, h:h + 1, :]

    def mix(self, h):
        return _tril_bf16(self.wsgu_ref[self.layer, h])

    def mix_bias(self, h):
        return self.bsgut_ref[self.layer, :, h:h + 1]


SMALL_SPECS = ((DEPTH, N_GROUPS, GROUP, GROUP), (DEPTH, D_POOL), (DEPTH, N_HEADS, HEAD), (DEPTH, N_HEADS, HEAD),
               (DEPTH, N_HEADS, CHUNK, CHUNK), (DEPTH, CHUNK, N_HEADS))


def _mix_forward(proj, halo, tile, w, cdf=None):
    keep = cdf is not None
    rows = _row_index(tile)
    inv_counts = _inverse_counts(rows)
    xa = proj[:, 0:D_POOL]
    ga = proj[:, D_POOL:2 * D_POOL]
    sums = _window_sums(jnp.concatenate([halo, xa], axis=0), True)
    ga_act, ga_grad = _silu_parts(ga)
    pooled, pw, ya = [], [], []
    for g in range(N_GROUPS):
        sl = slice(g * GROUP, (g + 1) * GROUP)
        p = (sums[g] * inv_counts[g] - xa[:, sl]).astype(BF16)
        q = _dot(p, w.pool(g))
        pooled.append(p)
        pw.append(q)
        ya.append(q * w.pool_scale(g) * ga_act[:, sl])

    u = proj[:, 2 * D_POOL:2 * D_POOL + D_SGU]
    v = proj[:, 2 * D_POOL + D_SGU:2 * D_POOL + 2 * D_SGU]
    gb = proj[:, 2 * D_POOL + 2 * D_SGU:]
    gb_act, gb_grad = _silu_parts(gb)
    if cdf is None:
        cdf = jnp.concatenate([_normal_cdf(u), _normal_cdf(v)], axis=1)
    u_act, u_grad = _gelu_parts(u, cdf[:, :D_SGU], keep)
    v_act, v_grad = _gelu_parts(v, cdf[:, D_SGU:], keep)
    vn, vrstd, vln, mixed, yb = [], [], [], [], []
    for h in range(N_HEADS):
        sl = slice(h * HEAD, (h + 1) * HEAD)
        n_h, r_h = _ln(v_act[:, sl])
        l_h = (n_h * w.ln_gain(h) + w.ln_bias(h)).astype(BF16)
        w_h = w.mix(h)
        bias = w.mix_bias(h)
        m_h = jnp.concatenate(
            [_dot(w_h, l_h[k * CHUNK:(k + 1) * CHUNK]) + bias for k in range(TM // CHUNK)], axis=0)
        vn.append(n_h)
        vrstd.append(r_h)
        vln.append(l_h)
        mixed.append(m_h)
        yb.append(u_act[:, sl] * m_h * gb_act[:, sl])
    cat = jnp.concatenate(ya + yb, axis=1)
    if not keep:
        return cat, cdf
    return cat, dict(inv_counts=inv_counts, ga_act=ga_act, ga_grad=ga_grad, pooled=pooled, pw=pw, u_grad=u_grad,
                     v_grad=v_grad, u_act=u_act, gb_act=gb_act, gb_grad=gb_grad, vn=vn, vrstd=vrstd, vln=vln,
                     mixed=mixed)


def _const_spec(shape):
    nd = len(shape)
    return pl.BlockSpec(shape, lambda i: (0,) * nd)


def _layer_forward(layer, x, mod, w_int, w_outf, small, ln_g, ln_b, name):
    def body(x_ref, mod_ref, wint_ref, wout_ref, wpool_ref, pscale_ref, slng_ref, slnb_ref, wsgu_ref, bsgut_ref,
             lng_ref, lnb_ref, out_ref, proj_ref, y_ref, cdf_ref, halo_ref):
        weights = _MixWeights(layer, wpool_ref, pscale_ref, slng_ref, slnb_ref, wsgu_ref, bsgut_ref)
        tile = pl.program_id(0)

        @pl.when(tile == 0)
        def _():
            halo_ref[...] = jnp.zeros_like(halo_ref)

        xt = x_ref[...]
        shift = mod_ref[layer:layer + 1, 0:D_MODEL]
        scale = mod_ref[layer:layer + 1, D_MODEL:2 * D_MODEL]
        gate = mod_ref[layer:layer + 1, 2 * D_MODEL:]
        xn, _ = _ln(xt)
        h = (xn * (1.0 + scale) + shift).astype(BF16)
        proj = _dot_nt(h, wint_ref[...])
        proj_ref[...] = proj
        cat, cdf_ref[...] = _mix_forward(proj, halo_ref[...], tile, weights)
        halo_ref[...] = proj[TM - HALO:, 0:D_POOL]
        y = _dot(cat.astype(BF16), wout_ref[...])
        y_ref[...] = y
        zn, _ = _ln(ALPHA * xt + gate * y)
        out_ref[...] = zn * lng_ref[layer:layer + 1, :] + lnb_ref[layer:layer + 1, :]

    row = lambda w: pl.BlockSpec((TM, w), lambda i: (i, 0))
    return pl.pallas_call(
        body,
        name=name,
        grid=(N_TILES,),
        in_specs=[row(D_MODEL), _const_spec((DEPTH, 3 * D_MODEL)), _const_spec((D_IN, D_MODEL)),
                  _const_spec((D_MODEL, D_MODEL))] + [_const_spec(s) for s in SMALL_SPECS]
                 + [_const_spec((DEPTH, D_MODEL)), _const_spec((DEPTH, D_MODEL))],
        out_specs=[row(D_MODEL), row(D_IN), row(D_MODEL), row(2 * D_SGU)],
        out_shape=[jax.ShapeDtypeStruct((SEQ, D_MODEL), F32), jax.ShapeDtypeStruct((SEQ, D_IN), F32),
                   jax.ShapeDtypeStruct((SEQ, D_MODEL), F32), jax.ShapeDtypeStruct((SEQ, 2 * D_SGU), F32)],
        scratch_shapes=[pltpu.VMEM((HALO, D_POOL), F32)],
        compiler_params=pltpu.CompilerParams(dimension_semantics=("arbitrary",), vmem_limit_bytes=VMEM_LIMIT),
    )(x, mod, w_int, w_outf, *small, ln_g, ln_b)


VEC_LNG, VEC_LNB, VEC_POOL, VEC_SGU, VEC_SHIFT, VEC_SCALE, VEC_GATE, VEC_LOSS = range(8)


def _layer_backward(layer, a, b, x, proj, y, cdf, mod, w_int, w_outf, small, ln_g, is_last, name, carry=(),
                    reduce=()):
    n_red, n_carry = len(reduce), len(carry)
    base = layer * PACK_ROWS

    def body(a_ref, b_ref, x_ref, proj_ref, prev_ref, y_ref, cdf_ref, mod_ref, wint_ref, wout_ref, wpool_ref,
             pscale_ref, slng_ref, slnb_ref, wsgu_ref, bsgut_ref, lng_ref, *rest):
        weights = _MixWeights(layer, wpool_ref, pscale_ref, slng_ref, slnb_ref, wsgu_ref, bsgut_ref)
        carry_refs, rest = rest[:n_carry], rest[n_carry:]
        part_refs, rest = rest[:n_red], rest[n_red:]
        dx_ref, dproj_ref, h_ref, cat_ref, dy_ref, small_ref, dmod_ref, loss_ref = rest[:8]
        shard_refs, rest = rest[8:8 + n_red], rest[8 + n_red:]
        vec_ref, dmix_ref, halo_ref = rest[:3]
        step = pl.program_id(0)
        tile = N_TILES - 1 - step

        def scatter():
            bufs, sems = rest[3:3 + 5 * n_red], rest[3 + 5 * n_red:]
            arrays = [dict(part=part_refs[n], out=shard_refs[n], staged=True, stage=bufs[5 * n], sib=bufs[5 * n + 1],
                           snd=bufs[5 * n + 2], rcv=bufs[5 * n + 3], relay=bufs[5 * n + 4]) for n in range(n_red)]
            return _ChipReduceScatter(arrays, *sems)

        @pl.when(step == 0)
        def _():
            small_ref[...] = jnp.zeros_like(small_ref)
            dmod_ref[...] = jnp.zeros_like(dmod_ref)
            vec_ref[...] = jnp.zeros_like(vec_ref)
            dmix_ref[...] = jnp.zeros_like(dmix_ref)
            halo_ref[...] = jnp.zeros_like(halo_ref)
            if n_red:
                scatter().start()

        if n_red:
            @pl.when(step == 1)
            def _():
                scatter().exchange()

            @pl.when(step == N_TILES // 2)
            def _():
                scatter().fold()

        def acc(row, lo, val):
            hi = lo + val.shape[1]
            vec_ref[row:row + 1, lo:hi] += jnp.sum(val, axis=0, keepdims=True)

        xt = x_ref[...]
        yt = y_ref[...]
        shift = mod_ref[layer:layer + 1, 0:D_MODEL]
        scale = mod_ref[layer:layer + 1, D_MODEL:2 * D_MODEL]
        gate = mod_ref[layer:layer + 1, 2 * D_MODEL:]
        ln_gain = lng_ref[layer:layer + 1, :]

        zn, zrstd = _ln(ALPHA * xt + gate * yt)
        if is_last:
            diff = a_ref[...] - b_ref[...]
            acc(VEC_LOSS, 0, diff * diff)
            dout = diff * (1.0 / D_MODEL)
        else:
            dout = a_ref[...]
        acc(VEC_LNG, 0, dout * zn)
        acc(VEC_LNB, 0, dout)
        dz = _ln_bwd(dout * ln_gain, zn, zrstd)
        acc(VEC_GATE, 0, dz * yt)
        dy = (dz * gate).astype(BF16)
        dy_ref[...] = dy
        dcat = _dot_nt(dy, wout_ref[...])

        proj = proj_ref[...]
        prev = jnp.where(tile > 0, prev_ref[...], 0.0)
        cat, k = _mix_forward(proj, prev, tile, weights, cdf_ref[...])
        cat_ref[...] = cat.astype(BF16)

        dga, dq = [], []
        for g in range(N_GROUPS):
            sl = slice(g * GROUP, (g + 1) * GROUP)
            pscale = weights.pool_scale(g)
            dya = dcat[:, sl]
            dyp = dya * k["ga_act"][:, sl]
            dga.append(dya * k["pw"][g] * pscale * k["ga_grad"][:, sl])
            acc(VEC_POOL, g * GROUP, dyp * k["pw"][g])
            dpw = (dyp * pscale).astype(BF16)
            rows = pl.ds(base + ROW_WPOOL + g * GROUP, GROUP)
            small_ref[rows, :] += _dot_tn(k["pooled"][g], dpw)
            dq.append(_dot_nt(dpw, weights.pool(g)))
        dpooled = jnp.concatenate(dq, axis=1)
        scaled = jnp.concatenate([dq[g] * k["inv_counts"][g] for g in range(N_GROUPS)], axis=1)
        sums = _window_sums(jnp.concatenate([scaled, halo_ref[...]], axis=0), False)
        halo_ref[...] = scaled[0:HALO]
        dxa = jnp.concatenate(sums, axis=1) - dpooled

        du, dv, dgb = [], [], []
        for h in range(N_HEADS):
            sl = slice(h * HEAD, (h + 1) * HEAD)
            dyb = dcat[:, D_POOL + h * HEAD:D_POOL + (h + 1) * HEAD]
            m_h = k["mixed"][h]
            ug = k["u_act"][:, sl] * dyb
            du.append(dyb * m_h * k["gb_act"][:, sl] * k["u_grad"][:, sl])
            dgb.append(ug * m_h * k["gb_grad"][:, sl])
            dmixed = ug * k["gb_act"][:, sl]
            dmixed_bf = dmixed.astype(BF16)
            w_h = weights.mix(h)
            dvln_parts = []
            dmix_sum = dmix_ref[h]
            wsgu_rows = pl.ds(base + ROW_WSGU + h * CHUNK, CHUNK)
            dws = small_ref[wsgu_rows, :]
            for c in range(TM // CHUNK):
                cs = slice(c * CHUNK, (c + 1) * CHUNK)
                dmix_sum = dmix_sum + dmixed[cs]
                dws = dws + _dot_nt(dmixed_bf[cs], k["vln"][h][cs])
                dvln_parts.append(_dot_tn(w_h, dmixed_bf[cs]))
            dmix_ref[h] = dmix_sum
            small_ref[wsgu_rows, :] = dws
            dvln = jnp.concatenate(dvln_parts, axis=0)
            acc(VEC_SGU, h * HEAD, dvln * k["vn"][h])
            acc(VEC_SGU, D_SGU + h * HEAD, dvln)
            dvv = _ln_bwd(dvln * weights.ln_gain(h), k["vn"][h], k["vrstd"][h])
            dv.append(dvv * k["v_grad"][:, sl])

        dproj = jnp.concatenate([dxa] + dga + du + dv + dgb, axis=1).astype(BF16)
        dproj_ref[...] = dproj
        dh = _dot(dproj, wint_ref[...])

        xn, xrstd = _ln(xt)
        h_ref[...] = (xn * (1.0 + scale) + shift).astype(BF16)
        acc(VEC_SCALE, 0, dh * xn)
        acc(VEC_SHIFT, 0, dh)
        dx_ref[...] = _ln_bwd(dh * (1.0 + scale), xn, xrstd) + ALPHA * dz

        @pl.when(step == N_TILES - 1)
        def _():
            def put(row0, vec_row, lo, n):
                for r in range(n):
                    small_ref[base + row0 + r:base + row0 + r + 1, :] = (
                        vec_ref[vec_row:vec_row + 1, lo + r * 128:lo + (r + 1) * 128])

            put(ROW_PSCALE, VEC_POOL, 0, 4)
            put(ROW_SLNG, VEC_SGU, 0, 4)
            put(ROW_SLNB, VEC_SGU, D_SGU, 4)
            put(ROW_LNG, VEC_LNG, 0, 8)
            put(ROW_LNB, VEC_LNB, 0, 8)
            ones = jnp.ones((8, HEAD), F32)
            t = lax.broadcasted_iota(jnp.int32, (CHUNK, CHUNK), 0)
            s = lax.broadcasted_iota(jnp.int32, (CHUNK, CHUNK), 1)
            for h in range(N_HEADS):
                bias_rows = lax.dot_general(ones, dmix_ref[h], (((1,), (1,)), ((), ())),
                                            preferred_element_type=F32, precision=lax.Precision.HIGHEST)
                small_ref[base + ROW_BSGU + h:base + ROW_BSGU + h + 1, :] = bias_rows[0:1]
                rows = pl.ds(base + ROW_WSGU + h * CHUNK, CHUNK)
                small_ref[rows, :] = jnp.where(t >= s, small_ref[rows, :], 0.0)
            pieces = ((0, VEC_SHIFT, 0, 768),
                      (1, VEC_SHIFT, 768, 256), (1, VEC_SCALE, 0, 512),
                      (2, VEC_SCALE, 512, 512), (2, VEC_GATE, 0, 256),
                      (3, VEC_GATE, 256, 768))
            filled = [0] * 4
            for q, vec_row, lo, n in pieces:
                row = 4 * layer + q
                dmod_ref[row:row + 1, filled[q]:filled[q] + n] = vec_ref[vec_row:vec_row + 1, lo:lo + n]
                filled[q] += n
            if n_carry:
                for other in range(layer + 1, DEPTH):
                    rows = pl.ds(other * PACK_ROWS, PACK_ROWS)
                    small_ref[rows, :] = carry_refs[0][rows, :]
                    dmod_ref[4 * other:4 * other + 4, :] = carry_refs[1][4 * other:4 * other + 4, :]
            loss_ref[...] = vec_ref[VEC_LOSS:VEC_LOSS + 1, :]
            if n_red:
                scatter().finish()
                scatter().wait_sends()

    rev = lambda w: pl.BlockSpec((TM, w), lambda i: (N_TILES - 1 - i, 0))
    prev_spec = pl.BlockSpec(
        (HALO, D_POOL), lambda i: (jnp.maximum((N_TILES - 1 - i) * (TM // HALO) - 1, 0), 0))
    comm_scratch = []
    for p in reduce:
        comm_scratch += _ChipReduceScatter.buffers(p.shape[2], p.shape[3], p.dtype)
    if n_red:
        comm_scratch += _ChipReduceScatter.semaphores(n_red)
    return pl.pallas_call(
        body,
        name=name,
        grid=(N_TILES,),
        in_specs=[rev(D_MODEL), rev(D_MODEL) if is_last else pl.BlockSpec((TM, D_MODEL), lambda i: (0, 0)),
                  rev(D_MODEL), rev(D_IN), prev_spec, rev(D_MODEL), rev(2 * D_SGU),
                  _const_spec((DEPTH, 3 * D_MODEL)), _const_spec((D_IN, D_MODEL)), _const_spec((D_MODEL, D_MODEL))]
                 + [_const_spec(s) for s in SMALL_SPECS] + [_const_spec((DEPTH, D_MODEL))]
                 + [_const_spec(c.shape) for c in carry] + [ANY] * n_red,
        out_specs=[rev(D_MODEL), rev(D_IN), rev(D_MODEL), rev(D_MODEL), rev(D_MODEL),
                   _const_spec((DEPTH * PACK_ROWS, 128)), _const_spec((8, DMOD_COLS)), _const_spec((1, D_MODEL))]
                  + [_const_spec(p.shape[2:]) for p in reduce],
        out_shape=[jax.ShapeDtypeStruct((SEQ, D_MODEL), F32), jax.ShapeDtypeStruct((SEQ, D_IN), BF16),
                   jax.ShapeDtypeStruct((SEQ, D_MODEL), BF16), jax.ShapeDtypeStruct((SEQ, D_MODEL), BF16),
                   jax.ShapeDtypeStruct((SEQ, D_MODEL), BF16), jax.ShapeDtypeStruct((DEPTH * PACK_ROWS, 128), F32),
                   jax.ShapeDtypeStruct((8, DMOD_COLS), F32), jax.ShapeDtypeStruct((1, D_MODEL), F32)]
                  + [jax.ShapeDtypeStruct(p.shape[2:], F32) for p in reduce],
        scratch_shapes=[pltpu.VMEM((8, D_MODEL), F32), pltpu.VMEM((N_HEADS, CHUNK, HEAD), F32),
                        pltpu.VMEM((HALO, D_POOL), F32)] + comm_scratch,
        compiler_params=pltpu.CompilerParams(dimension_semantics=("arbitrary",), vmem_limit_bytes=VMEM_LIMIT),
    )(a, b, x, proj, proj, y, cdf, mod, w_int, w_outf, *small, ln_g, *carry, *reduce)


def _grad_matmul(lhs, rhs, block_cols, name):
    m, n = lhs.shape[1], rhs.shape[1]

    def body(lhs_ref, rhs_ref, out_ref):
        out_ref[...] = _dot_tn(lhs_ref[...], rhs_ref[...]).astype(BF16)

    return pl.pallas_call(
        body,
        name=name,
        grid=(m // block_cols,),
        in_specs=[pl.BlockSpec((SEQ, block_cols), lambda j: (0, j)), pl.BlockSpec((SEQ, n), lambda j: (0, 0))],
        out_specs=pl.BlockSpec((block_cols, n), lambda j: (j, 0)),
        out_shape=jax.ShapeDtypeStruct((m, n), BF16),
        compiler_params=pltpu.CompilerParams(dimension_semantics=("arbitrary",), vmem_limit_bytes=VMEM_LIMIT),
    )(lhs, rhs)


def _adamw_math(w, g, m, v):
    m = ADAM_B1 * m + (1.0 - ADAM_B1) * g
    v = ADAM_B2 * v + (1.0 - ADAM_B2) * (g * g)
    m_hat = m / (1.0 - ADAM_B1 ** ADAM_STEP)
    v_hat = v / (1.0 - ADAM_B2 ** ADAM_STEP)
    delta = -ADAM_LR * (m_hat / (jnp.sqrt(v_hat) + ADAM_EPS) + ADAM_WD * w)
    return delta, m, v


def _adamw(w, grads, m, v, block_rows, name):
    rows, cols = grads[0].shape
    blocks = rows // block_rows

    def body(w_ref, m_ref, v_ref, *rest):
        g_refs, (g_ref, d_ref, nm_ref, nv_ref) = rest[:DEPTH], rest[DEPTH:]
        for layer in range(DEPTH):
            @pl.when(pl.program_id(0) == layer)
            def _():
                g = g_refs[layer][...]
                g_ref[...] = g
                d_ref[...], nm_ref[...], nv_ref[...] = _adamw_math(w_ref[...], g, m_ref[...], v_ref[...])

    def grad_spec(layer):
        return pl.BlockSpec((block_rows, cols),
                            lambda l, i: (jnp.where(l == layer, i, jnp.where(l < layer, 0, blocks - 1)), 0))

    spec = pl.BlockSpec((block_rows, cols), lambda l, i: (l * blocks + i, 0))
    return pl.pallas_call(
        body,
        name=name,
        grid=(DEPTH, blocks),
        in_specs=[spec] * 3 + [grad_spec(layer) for layer in range(DEPTH)],
        out_specs=[spec] * 4,
        out_shape=[jax.ShapeDtypeStruct(w.shape, F32)] * 4,
        compiler_params=pltpu.CompilerParams(dimension_semantics=("arbitrary", "arbitrary"),
                                             vmem_limit_bytes=VMEM_LIMIT),
    )(w, m, v, *grads)


def _adamw_ada(w, m, v, act_t, dmod_cols, name):
    cols = w.shape[2]

    rows = 256

    def body(w_ref, m_ref, v_ref, act_ref, dmod_ref, g_ref, d_ref, nm_ref, nv_ref):
        act = act_ref[...]
        dm = dmod_ref[0]
        g = act[:, 0:1] * dm[0:1, :]
        for b in range(1, N_DEV):
            g = g + act[:, b:b + 1] * dm[b:b + 1, :]
        g_ref[0] = g
        d_ref[0], nm_ref[0], nv_ref[0] = _adamw_math(w_ref[0], g, m_ref[0], v_ref[0])

    spec = pl.BlockSpec((1, rows, cols), lambda l, i: (l, i, 0))
    return pl.pallas_call(
        body,
        name=name,
        grid=(DEPTH, D_MODEL // rows),
        in_specs=[spec, spec, spec, pl.BlockSpec((rows, N_DEV), lambda l, i: (i, 0)),
                  pl.BlockSpec((1, N_DEV, cols), lambda l, i: (l, 0, 0))],
        out_specs=[spec] * 4,
        out_shape=[jax.ShapeDtypeStruct(w.shape, F32)] * 4,
        compiler_params=pltpu.CompilerParams(dimension_semantics=("arbitrary", "arbitrary"),
                                             vmem_limit_bytes=VMEM_LIMIT),
    )(w, m, v, act_t, dmod_cols)


def _adamw_bias(w, m, v, dmod_all, name):
    def body(w_ref, m_ref, v_ref, dmod_ref, g_ref, d_ref, nm_ref, nv_ref):
        g = dmod_ref[0]
        for b in range(1, N_DEV):
            g = g + dmod_ref[b]
        g_ref[...] = g
        d_ref[...], nm_ref[...], nv_ref[...] = _adamw_math(w_ref[...], g, m_ref[...], v_ref[...])

    return pl.pallas_call(
        body,
        name=name,
        out_shape=[jax.ShapeDtypeStruct(w.shape, F32)] * 4,
        compiler_params=pltpu.CompilerParams(vmem_limit_bytes=VMEM_LIMIT),
    )(w, m, v, dmod_all)


MESH = pl.DeviceIdType.MESH
SIBLING = 1
ANY = pl.BlockSpec(memory_space=pl.ANY)
VMEM = pl.BlockSpec(memory_space=pltpu.VMEM)


def _me():
    return lax.axis_index("x"), lax.axis_index("y"), lax.axis_index("c")


def _peer(r):
    x, y, c = _me()
    return (1 - x if r & 4 else x, 1 - y if r & 2 else y, 1 - c if r & 1 else c)


def _index(dev):
    return 4 * dev[0] + 2 * dev[1] + dev[2]


def _remote(src, dst, send_sem, recv_sem, dev):
    return pltpu.make_async_remote_copy(src_ref=src, dst_ref=dst, send_sem=send_sem, recv_sem=recv_sem,
                                        device_id=dev, device_id_type=MESH)


ACROSS_X, ACROSS_Y, ACROSS_BOTH = 4, 2, 6
GATHER_SEMS = 11


class _TwoLevelGather:
    def __init__(self, out, send_sems, recv_sems, src=None):
        self.out, self.send_sems, self.recv_sems, self.src = out, send_sems, recv_sems, src
        self.rows = (out.shape[0] // N_DEV) if len(out.shape) == 2 else out.shape[1]
        self.half = self.rows // 2

    def _slot(self, block):
        if len(self.out.shape) == 2:
            return self.out.at[pl.ds(pl.multiple_of(_index(block) * self.rows, self.rows), self.rows)]
        return self.out.at[_index(block)]

    def _copy(self, k, block, part, to, src=None):
        slot = self._slot(block)
        if part is not None:
            rows = pl.ds(part * self.half, self.half)
            slot = slot.at[rows]
            src = None if src is None else src.at[rows]
        return _remote(slot if src is None else src, slot, self.send_sems.at[k], self.recv_sems.at[k], to)

    def _mine(self):
        me = _me()
        src = self._slot(me) if self.src is None else self.src
        x, y = _peer(ACROSS_X), _peer(ACROSS_Y)
        return [self._copy(1, me, 0, x, src), self._copy(3, me, 1, y, src), self._copy(0, me, None, _peer(SIBLING), src),
                self._copy(2, me, 1, x, src), self._copy(4, me, 0, y, src)]

    def _relayed(self):
        return [self._copy(5, _peer(ACROSS_X), 0, _peer(ACROSS_Y)), self._copy(6, _peer(ACROSS_Y), 1, _peer(ACROSS_X))]

    def _passed(self):
        sib, far = _peer(SIBLING), _peer(ACROSS_BOTH)
        return [self._copy(7, _peer(ACROSS_X), None, sib), self._copy(8, _peer(ACROSS_Y), None, sib),
                self._copy(9, far, 0, sib), self._copy(10, far, 1, sib)]

    def _arrival(self, k, r, part):
        return self._copy(k, _peer(r), part, _me())

    def send_first(self):
        for cp in self._mine()[:3]:
            cp.start()

    def send_second(self):
        for cp in self._mine()[3:]:
            cp.start()

    def send_mine(self):
        self.send_first()
        self.send_second()

    def relay(self):
        relayed = self._relayed()
        self._arrival(1, ACROSS_X, 0).wait_recv()
        relayed[0].start()
        self._arrival(3, ACROSS_Y, 1).wait_recv()
        relayed[1].start()

    def pass_near(self):
        passed = self._passed()
        self._arrival(2, ACROSS_X, 1).wait_recv()
        passed[0].start()
        self._arrival(4, ACROSS_Y, 0).wait_recv()
        passed[1].start()

    def pass_far(self):
        passed = self._passed()
        self._arrival(5, ACROSS_BOTH, 0).wait_recv()
        passed[2].start()
        self._arrival(6, ACROSS_BOTH, 1).wait_recv()
        passed[3].start()

    def pass_on(self):
        self.pass_near()
        self.pass_far()

    def wait_sibling(self):
        self._arrival(0, SIBLING, None).wait_recv()

    def wait_passed(self, r):
        if r == ACROSS_BOTH:
            self._arrival(9, r ^ SIBLING, 0).wait_recv()
            self._arrival(10, r ^ SIBLING, 1).wait_recv()
        else:
            self._arrival(7 if r == ACROSS_X else 8, r ^ SIBLING, None).wait_recv()

    def wait_rest(self):
        self.wait_sibling()
        for r in (ACROSS_X, ACROSS_Y, ACROSS_BOTH):
            self.wait_passed(r)

    def wait_sends(self):
        for cp in self._mine() + self._relayed() + self._passed():
            cp.wait_send()


class _ChipReduceScatter:
    SLOTS = 6

    def __init__(self, arrays, l_sem, d_send, d_recv, i_send, i_recv):
        self.arrays = arrays
        self.l_sem, self.d_send, self.d_recv, self.i_send, self.i_recv = l_sem, d_send, d_recv, i_send, i_recv

    @staticmethod
    def buffers(rows, cols, dtype, staged=True):
        stage = [pltpu.VMEM((4, rows, cols), dtype)] if staged else []
        return stage + [pltpu.VMEM((4, rows, cols), dtype), pltpu.VMEM((3, rows, cols), dtype),
                        pltpu.VMEM((2, rows, cols), dtype), pltpu.VMEM((2, rows // 2, cols), dtype)]

    @classmethod
    def semaphores(cls, n):
        return [pltpu.SemaphoreType.DMA((n,)), pltpu.SemaphoreType.DMA((n,)), pltpu.SemaphoreType.DMA((n,)),
                pltpu.SemaphoreType.DMA((n, cls.SLOTS)), pltpu.SemaphoreType.DMA((n, cls.SLOTS))]

    def _pick(self, which):
        return list(enumerate(self.arrays)) if which is None else [(n, self.arrays[n]) for n in which]

    def _staging(self, which):
        c = _me()[2]
        return [pltpu.make_async_copy(a["part"].at[pl.ds(0, 4), c], a["stage"], self.l_sem.at[n])
                for n, a in self._pick(which) if a["staged"]]

    def _first(self, which):
        other = 1 - _me()[2]
        return [_remote(a["part"].at[pl.ds(0, 4), other], a["sib"], self.d_send.at[n], self.d_recv.at[n],
                        _peer(SIBLING)) for n, a in self._pick(which)]

    @staticmethod
    def _halves(a):
        half = a["rcv"].shape[1] // 2
        return pl.ds(0, half), pl.ds(half, half)

    def _hops(self, n, a):
        h0, h1 = self._halves(a)
        x, y = _peer(ACROSS_X), _peer(ACROSS_Y)
        snd, rcv, relay = a["snd"], a["rcv"], a["relay"]
        pairs = [(snd.at[2, h0], relay.at[0], x), (snd.at[2, h1], relay.at[1], y),
                 (snd.at[0, h0], rcv.at[0, h0], x), (snd.at[0, h1], rcv.at[0, h1], x),
                 (snd.at[1, h1], rcv.at[1, h1], y), (snd.at[1, h0], rcv.at[1, h0], y)]
        return [_remote(s, d, self.i_send.at[n, k], self.i_recv.at[n, k], to) for k, (s, d, to) in enumerate(pairs)]

    def _mine(self, a, chip, rows=None):
        src = a["stage"].at[chip] if a["staged"] else a["part"].at[chip, _me()[2]]
        mine, sib = (src[...], a["sib"][chip]) if rows is None else (src[rows, :], a["sib"][chip, rows, :])
        return mine.astype(F32) + sib.astype(F32)

    def start(self, which=None):
        for cp in self._staging(which) + self._first(which):
            cp.start()

    def exchange(self, which=None):
        for cp in self._staging(which):
            cp.wait()
        for cp in self._first(which):
            cp.wait_recv()
        chip = lambda dev: 2 * dev[0] + dev[1]
        across_x, across_y, far = chip(_peer(ACROSS_X)), chip(_peer(ACROSS_Y)), chip(_peer(ACROSS_BOTH))
        picked = self._pick(which)
        for n, a in picked:
            hops = self._hops(n, a)
            a["snd"][2] = self._mine(a, far).astype(a["snd"].dtype)
            hops[0].start()
            hops[1].start()
        for n, a in picked:
            h0, h1 = self._halves(a)
            hops = self._hops(n, a)
            dtype = a["snd"].dtype
            a["snd"][0, h0, :] = self._mine(a, across_x, h0).astype(dtype)
            hops[2].start()
            a["snd"][1, h1, :] = self._mine(a, across_y, h1).astype(dtype)
            hops[4].start()

    def fold(self, which=None):
        chip = lambda dev: 2 * dev[0] + dev[1]
        across_x, across_y = chip(_peer(ACROSS_X)), chip(_peer(ACROSS_Y))
        for n, a in self._pick(which):
            h0, h1 = self._halves(a)
            hops = self._hops(n, a)
            dtype = a["snd"].dtype
            hops[1].wait_recv()
            a["snd"][0, h1, :] = (self._mine(a, across_x, h1) + a["relay"][1].astype(F32)).astype(dtype)
            hops[3].start()
            hops[0].wait_recv()
            a["snd"][1, h0, :] = (self._mine(a, across_y, h0) + a["relay"][0].astype(F32)).astype(dtype)
            hops[5].start()

    def finish(self, which=None):
        x, y, _ = _me()
        home = 2 * x + y
        for n, a in self._pick(which):
            hops = self._hops(n, a)
            a["out"][...] = self._mine(a, home)
            hops[2].wait_recv()
            hops[3].wait_recv()
            a["out"][...] += a["rcv"][0].astype(F32)
            hops[4].wait_recv()
            hops[5].wait_recv()
            a["out"][...] += a["rcv"][1].astype(F32)

    def wait_sends(self, which=None):
        for cp in self._first(which):
            cp.wait_send()
        for n, a in self._pick(which):
            for cp in self._hops(n, a):
                cp.wait_send()


def _direct_exchange(src_of, dst_of, send_sems, recv_sems):
    me = _me()
    copies = [_remote(src_of(_peer(r)), dst_of(me), send_sems.at[r - 1], recv_sems.at[r - 1], _peer(r))
              for r in range(1, N_DEV)]
    for cp in copies:
        cp.start()
    return copies


def _wait_direct(copies):
    for cp in copies:
        cp.wait_recv()
    for cp in copies:
        cp.wait_send()


def _first_layer(x, c_row, w_ada, b_ada, mine_now, mine_next, small, ln_g, ln_b):
    layer = 0
    cols = w_ada.shape[2]
    shard = mine_now[0].shape[0]
    pair = 2 * shard

    def body(x_ref, c_ref, wada_ref, bada_ref, wpool_ref, pscale_ref, slng_ref, slnb_ref, wsgu_ref, bsgut_ref,
             lng_ref, lnb_ref, wint_hbm, wout_hbm, nxt_in_hbm, nxt_out_hbm,
             out_ref, y_ref, cdf_ref, acts_ref, mod_ref, proj_hbm, wint_keep, wout_keep, nxt_in_full, nxt_out_full,
             wint_v, wout_v, h_buf, proj_blk, proj_tile, halo_ref, act_all, act_src, part, mod_recv,
             w_send, w_recv, w_local, p_sems, t_sems, a_send, a_recv, m_send, m_recv, n_send, n_recv, n_local):
        weights = _MixWeights(layer, wpool_ref, pscale_ref, slng_ref, slnb_ref, wsgu_ref, bsgut_ref)
        tile = pl.program_id(0)

        def now_gathers():
            return (_TwoLevelGather(wint_v, w_send.at[0], w_recv.at[0], src=wint_hbm),
                    _TwoLevelGather(wout_v, w_send.at[1], w_recv.at[1], src=wout_hbm))

        def next_gathers():
            return (_TwoLevelGather(nxt_in_full, n_send.at[0], n_recv.at[0], src=nxt_in_hbm),
                    _TwoLevelGather(nxt_out_full, n_send.at[1], n_recv.at[1], src=nxt_out_hbm))

        def next_own():
            mine = _index(_me())
            return [pltpu.make_async_copy(nxt_in_hbm, nxt_in_full.at[mine], n_local.at[0]),
                    pltpu.make_async_copy(nxt_out_hbm, nxt_out_full.at[mine], n_local.at[1])]

        def keeps():
            return [pltpu.make_async_copy(wint_v, wint_keep, w_local.at[2]),
                    pltpu.make_async_copy(wout_v, wout_keep, w_local.at[3])]

        def tile_read(t):
            slot = t % 2
            return pltpu.make_async_copy(proj_hbm.at[pl.ds(pl.multiple_of(t * TM, TM), TM)], proj_tile.at[slot],
                                         t_sems.at[slot])

        @pl.when(tile == 0)
        def _():
            me = _me()
            mine = _index(me)
            halo_ref[...] = jnp.zeros_like(halo_ref)

            cval = c_ref[...]
            act_src[...] = jnp.zeros_like(act_src)
            act_src[0:1, :] = cval * jax.nn.sigmoid(cval)
            act_all[mine] = act_src[...]
            act_copies = _direct_exchange(lambda p: act_src, lambda m: act_all.at[_index(m)], a_send, a_recv)

            gather_in, gather_out = now_gathers()
            own_in = pltpu.make_async_copy(wint_hbm, gather_in._slot(me), w_local.at[0])
            own_out = pltpu.make_async_copy(wout_hbm, gather_out._slot(me), w_local.at[1])
            own_in.start()
            own_out.start()
            gather_in.send_first()

            _wait_direct(act_copies)
            acts = jnp.concatenate([act_all[j, 0:1, :] for j in range(N_DEV)], axis=0)
            acts_ref[...] = acts
            part[...] = jnp.zeros_like(part)
            for l in range(DEPTH):
                res = lax.dot_general(acts, wada_ref[l], (((1,), (0,)), ((), ())), preferred_element_type=F32,
                                      precision=lax.Precision.HIGHEST)
                for b in range(N_DEV):
                    part[b, l:l + 1, :] = res[b:b + 1, :]
            mod_recv[mine] = part[mine]
            mod_copies = _direct_exchange(lambda p: part.at[_index(p)], lambda m: mod_recv.at[_index(m)],
                                          m_send, m_recv)
            gather_in.send_second()
            gather_in.relay()
            gather_out.send_mine()

            _wait_direct(mod_copies)
            for l in range(DEPTH):
                for j in range(N_DEV):
                    sl = slice(j * cols, (j + 1) * cols)
                    mod_ref[l:l + 1, sl] = mod_recv[j, l:l + 1, :] + bada_ref[l:l + 1, sl]
            shift = mod_ref[layer:layer + 1, 0:D_MODEL]
            scale = mod_ref[layer:layer + 1, D_MODEL:2 * D_MODEL]
            for t in range(N_TILES):
                rows = pl.ds(t * TM, TM)
                xn, _ = _ln(x_ref[rows, :])
                h_buf[rows, :] = (xn * (1.0 + scale) + shift).astype(BF16)

            chip_of = lambda dev: 2 * dev[0] + dev[1]
            writes = []

            def project(n, dev):
                first = pl.multiple_of(chip_of(dev) * pair, pair)
                if n >= 2:
                    writes[n - 2].wait()
                proj_blk[n % 2] = _dot_nt(h_buf[...], wint_v[pl.ds(first, pair), :])
                cp = pltpu.make_async_copy(proj_blk.at[n % 2], proj_hbm.at[:, pl.ds(first, pair)], p_sems.at[n % 2])
                cp.start()
                writes.append(cp)

            own_in.wait()
            gather_in.wait_sibling()
            project(0, me)
            gather_in.pass_near()
            gather_in.wait_passed(ACROSS_X)
            project(1, _peer(ACROSS_X))
            gather_in.wait_passed(ACROSS_Y)
            project(2, _peer(ACROSS_Y))
            gather_in.pass_far()
            gather_in.wait_passed(ACROSS_BOTH)
            project(3, _peer(ACROSS_BOTH))

            gather_out.relay()
            gather_out.pass_on()
            gather_out.wait_rest()
            own_out.wait()
            for cp in keeps():
                cp.start()
            for cp in next_own():
                cp.start()
            for g in next_gathers():
                g.send_mine()
            writes[2].wait()
            writes[3].wait()
            tile_read(0).start()

        @pl.when(tile + 1 < N_TILES)
        def _():
            tile_read(tile + 1).start()

        @pl.when(tile == N_TILES // 2 - 1)
        def _():
            for g in next_gathers():
                g.relay()

        @pl.when(tile == N_TILES - 2)
        def _():
            for g in next_gathers():
                g.pass_near()

        tile_read(tile).wait()
        xt = x_ref[pl.ds(pl.multiple_of(tile * TM, TM), TM), :]
        gate = mod_ref[layer:layer + 1, 2 * D_MODEL:]
        proj = proj_tile[tile % 2]
        cat, cdf_ref[...] = _mix_forward(proj, halo_ref[...], tile, weights)
        halo_ref[...] = proj[TM - HALO:, 0:D_POOL]
        y = _dot(cat.astype(BF16), wout_v[...])
        y_ref[...] = y
        zn, _ = _ln(ALPHA * xt + gate * y)
        out_ref[...] = zn * lng_ref[layer:layer + 1, :] + lnb_ref[layer:layer + 1, :]

        @pl.when(tile == N_TILES - 1)
        def _():
            for g in next_gathers():
                g.pass_far()
            for g in next_gathers():
                g.wait_rest()
            for g in now_gathers() + next_gathers():
                g.wait_sends()
            for cp in keeps() + next_own():
                cp.wait()

    row = lambda w: pl.BlockSpec((TM, w), lambda i: (i, 0))
    gather_sems = pltpu.SemaphoreType.DMA((2, GATHER_SEMS))
    seven = pltpu.SemaphoreType.DMA((7,))
    return pl.pallas_call(
        body,
        name="layer_fwd_0",
        grid=(N_TILES,),
        in_specs=[_const_spec((SEQ, D_MODEL)), _const_spec((1, D_MODEL)), _const_spec(w_ada.shape),
                  _const_spec(b_ada.shape)] + [_const_spec(s) for s in SMALL_SPECS]
                 + [_const_spec((DEPTH, D_MODEL)), _const_spec((DEPTH, D_MODEL))] + [ANY] * 4,
        out_specs=[row(D_MODEL), row(D_MODEL), row(2 * D_SGU), _const_spec((N_DEV, D_MODEL)),
                   _const_spec((DEPTH, 3 * D_MODEL))] + [ANY] * 5,
        out_shape=[jax.ShapeDtypeStruct((SEQ, D_MODEL), F32), jax.ShapeDtypeStruct((SEQ, D_MODEL), F32),
                   jax.ShapeDtypeStruct((SEQ, 2 * D_SGU), F32), jax.ShapeDtypeStruct((N_DEV, D_MODEL), F32),
                   jax.ShapeDtypeStruct((DEPTH, 3 * D_MODEL), F32), jax.ShapeDtypeStruct((SEQ, D_IN), F32),
                   jax.ShapeDtypeStruct((D_IN, D_MODEL), BF16), jax.ShapeDtypeStruct((D_MODEL, D_MODEL), BF16)]
                  + [jax.ShapeDtypeStruct((N_DEV,) + blk.shape, blk.dtype) for blk in mine_next],
        scratch_shapes=[pltpu.VMEM((D_IN, D_MODEL), BF16), pltpu.VMEM((D_MODEL, D_MODEL), BF16),
                        pltpu.VMEM((SEQ, D_MODEL), BF16), pltpu.VMEM((2, SEQ, pair), F32),
                        pltpu.VMEM((2, TM, D_IN), F32), pltpu.VMEM((HALO, D_POOL), F32),
                        pltpu.VMEM((N_DEV, 8, D_MODEL), F32),
                        pltpu.VMEM((8, D_MODEL), F32), pltpu.VMEM((N_DEV, 8, cols), F32),
                        pltpu.VMEM((N_DEV, 8, cols), F32),
                        gather_sems, gather_sems, pltpu.SemaphoreType.DMA((4,)), pltpu.SemaphoreType.DMA((2,)),
                        pltpu.SemaphoreType.DMA((2,)), seven, seven, seven, seven,
                        gather_sems, gather_sems, pltpu.SemaphoreType.DMA((2,))],
        compiler_params=pltpu.CompilerParams(dimension_semantics=("arbitrary",), vmem_limit_bytes=VMEM_LIMIT),
    )(x, c_row, w_ada, b_ada, *small, ln_g, ln_b, *mine_now, *mine_next)


def _grad_tail(dproj, h, cat, dy, small, dmod8, loss_lanes):
    shard_in, shard_out, shard_small = D_IN // N_DEV, D_MODEL // N_DEV, small.shape[2]
    W_IN, W_OUT, SMALL = 0, 1, 2

    def body(dproj_hbm, h_hbm, cat_hbm, dy_hbm, small_hbm, dmod_ref, lanes_ref,
             gwin_ref, gwout_ref, stot_ref, dmod_all, loss_ref,
             dproj_v, h_v, cat_v, dy_v, part_in, part_out, own_small, loss_src, loss_all, *rest):
        bufs, rest = rest[:13], rest[13:]
        load_sems, rs_sems = rest[0], rest[1:6]
        m_send, m_recv, g_send, g_recv, s_send, s_recv = rest[6:]
        mine = _index(_me())

        loads = [pltpu.make_async_copy(s, d, load_sems.at[n]) for n, (s, d) in enumerate(
            ((cat_hbm, cat_v), (dy_hbm, dy_v), (dproj_hbm, dproj_v), (h_hbm, h_v)))]
        for cp in loads:
            cp.start()
        arrays = [dict(part=part_in, out=gwin_ref, staged=False, sib=bufs[0], snd=bufs[1], rcv=bufs[2], relay=bufs[3]),
                  dict(part=part_out, out=gwout_ref, staged=False, sib=bufs[4], snd=bufs[5], rcv=bufs[6],
                       relay=bufs[7]),
                  dict(part=small_hbm, out=own_small, staged=True, stage=bufs[8], sib=bufs[9], snd=bufs[10],
                       rcv=bufs[11], relay=bufs[12])]
        scatter = _ChipReduceScatter(arrays, *rs_sems)
        scatter.start([SMALL])
        dmod_all[mine] = dmod_ref[...]
        dmod_copies = _direct_exchange(lambda p: dmod_ref, lambda m: dmod_all.at[_index(m)], m_send, m_recv)
        loss_src[...] = jnp.full(loss_src.shape, (0.5 / D_MODEL) * jnp.sum(lanes_ref[...]), F32)
        loss_all[mine] = loss_src[...]
        loss_copies = _direct_exchange(lambda p: loss_src, lambda m: loss_all.at[_index(m)], s_send, s_recv)

        loads[0].wait()
        loads[1].wait()
        for blk in range(2):
            res = _dot_tn(cat_v[:, blk * 512:(blk + 1) * 512], dy_v[...]).astype(BF16)
            for s in range(4):
                part_out[2 * blk + s // 2, s % 2] = res[s * shard_out:(s + 1) * shard_out]
        scatter.start([W_OUT])
        scatter.exchange([SMALL])

        gather = _TwoLevelGather(stot_ref, g_send, g_recv)
        loads[2].wait()
        loads[3].wait()
        for chip in range(4):
            res = _dot_tn(dproj_v[:, chip * 2 * shard_in:(chip + 1) * 2 * shard_in], h_v[...]).astype(BF16)
            part_in[chip, 0] = res[:shard_in]
            part_in[chip, 1] = res[shard_in:]
            if chip == 0:
                scatter.exchange([W_OUT])
                scatter.fold([SMALL])
            if chip == 1:
                scatter.fold([W_OUT])
                scatter.finish([SMALL])
                stot_ref[mine] = own_small[...]
                gather.send_mine()
            if chip == 2:
                gather.relay()
        scatter.start([W_IN])
        scatter.finish([W_OUT])
        scatter.exchange([W_IN])
        gather.pass_on()
        scatter.fold([W_IN])
        gather.wait_rest()
        _wait_direct(dmod_copies)
        _wait_direct(loss_copies)
        total = loss_all[0]
        for j in range(1, N_DEV):
            total = total + loss_all[j]
        loss_ref[...] = total
        scatter.finish([W_IN])
        gather.wait_sends()
        scatter.wait_sends()

    buffers = _ChipReduceScatter.buffers
    comm_scratch = (buffers(shard_in, D_MODEL, BF16, staged=False) + buffers(shard_out, D_MODEL, BF16, staged=False)
                    + buffers(shard_small, 128, F32))
    comm_scratch += [pltpu.SemaphoreType.DMA((4,))] + _ChipReduceScatter.semaphores(3)
    comm_scratch += [pltpu.SemaphoreType.DMA((n,)) for n in (7, 7, GATHER_SEMS, GATHER_SEMS, 7, 7)]
    return pl.pallas_call(
        body,
        name="grad_tail",
        in_specs=[ANY] * 5 + [VMEM, VMEM],
        out_specs=[VMEM] * 5,
        out_shape=[jax.ShapeDtypeStruct((shard_in, D_MODEL), F32), jax.ShapeDtypeStruct((shard_out, D_MODEL), F32),
                   jax.ShapeDtypeStruct((N_DEV, shard_small, 128), F32),
                   jax.ShapeDtypeStruct((N_DEV,) + dmod8.shape, F32), jax.ShapeDtypeStruct((8, 128), F32)],
        scratch_shapes=[pltpu.VMEM(dproj.shape, BF16), pltpu.VMEM(h.shape, BF16), pltpu.VMEM(cat.shape, BF16),
                        pltpu.VMEM(dy.shape, BF16), pltpu.VMEM((4, 2, shard_in, D_MODEL), BF16),
                        pltpu.VMEM((4, 2, shard_out, D_MODEL), BF16), pltpu.VMEM((shard_small, 128), F32),
                        pltpu.VMEM((8, 128), F32), pltpu.VMEM((N_DEV, 8, 128), F32)] + comm_scratch,
        compiler_params=pltpu.CompilerParams(vmem_limit_bytes=VMEM_LIMIT),
    )(dproj, h, cat, dy, small, dmod8, loss_lanes)


SMALL_NAMES = ("w_pool", "w_sgu", "pool_scale", "sgu_ln_g", "sgu_ln_b", "b_sgu", "ln_g", "ln_b")
SMALL_ROWS = (512, 512, 4, 4, 4, 4, 8, 8)


def _adamw_small(g_packed, ws, ms, vs, name):
    n = len(SMALL_NAMES)

    def body(g_ref, *refs):
        w_refs, m_refs, v_refs = refs[:n], refs[n:2 * n], refs[2 * n:3 * n]
        outs = refs[3 * n:]

        def update(p, at, g):
            delta, new_m, new_v = _adamw_math(w_refs[p][at], g, m_refs[p][at], v_refs[p][at])
            outs[p][at] = g
            outs[n + p][at] = delta
            outs[2 * n + p][at] = new_m
            outs[3 * n + p][at] = new_v

        row = 0
        for p, r in enumerate(SMALL_ROWS):
            shape = ws[p].shape
            for layer in range(DEPTH):
                first = layer * PACK_ROWS + row
                if len(shape) == 4:
                    for k in range(shape[1]):
                        update(p, (layer, k), g_ref[first + k * shape[2]:first + (k + 1) * shape[2], :])
                elif len(shape) == 3:
                    update(p, (layer,), g_ref[first:first + r, :])
                else:
                    g = jnp.concatenate([g_ref[first + k:first + k + 1, :] for k in range(r)], axis=1)
                    update(p, (slice(layer, layer + 1), slice(None)), g)
            row += r

    res = pl.pallas_call(
        body,
        name=name,
        out_shape=[jax.ShapeDtypeStruct(w.shape, F32) for w in ws] * 4,
        compiler_params=pltpu.CompilerParams(vmem_limit_bytes=VMEM_LIMIT),
    )(g_packed, *ws, *ms, *vs)
    return res[:n], res[n:2 * n], res[2 * n:3 * n], res[3 * n:]


def kernel(x, c, w_ada, b_ada, w_in, w_pool, pool_scale, sgu_ln_g, sgu_ln_b, w_sgu, b_sgu, w_out, ln_g, ln_b, loss_target, m_w_ada, m_b_ada, m_w_in, m_w_pool, m_pool_scale, m_sgu_ln_g, m_sgu_ln_b, m_w_sgu, m_b_sgu, m_w_out, m_ln_g, m_ln_b, v_w_ada, v_b_ada, v_w_in, v_w_pool, v_pool_scale, v_sgu_ln_g, v_sgu_ln_b, v_w_sgu, v_b_sgu, v_w_out, v_ln_g, v_ln_b):
    mine = _index(_me())
    small_w = dict(w_pool=w_pool, w_sgu=w_sgu, pool_scale=pool_scale, sgu_ln_g=sgu_ln_g, sgu_ln_b=sgu_ln_b,
                   b_sgu=b_sgu, ln_g=ln_g, ln_b=ln_b)
    small_m = dict(w_pool=m_w_pool, w_sgu=m_w_sgu, pool_scale=m_pool_scale, sgu_ln_g=m_sgu_ln_g,
                   sgu_ln_b=m_sgu_ln_b, b_sgu=m_b_sgu, ln_g=m_ln_g, ln_b=m_ln_b)
    small_v = dict(w_pool=v_w_pool, w_sgu=v_w_sgu, pool_scale=v_pool_scale, sgu_ln_g=v_sgu_ln_g,
                   sgu_ln_b=v_sgu_ln_b, b_sgu=v_b_sgu, ln_g=v_ln_g, ln_b=v_ln_b)

    wint_loc = jnp.transpose(w_in, (0, 2, 1)).astype(BF16)
    wout_loc = w_out.astype(BF16)
    small = (w_pool, pool_scale, sgu_ln_g, sgu_ln_b, w_sgu, jnp.transpose(b_sgu, (0, 2, 1)))
    out, y, cdf, act_all, mod, proj, wint0, wout0, wint1, wout1 = _first_layer(
        x[0], c, w_ada, b_ada, [wint_loc[0], wout_loc[0]], [wint_loc[1], wout_loc[1]], small, ln_g, ln_b)
    w_int, w_outf = [wint0, wint1.reshape(D_IN, D_MODEL)], [wout0, wout1.reshape(D_MODEL, D_MODEL)]
    acts, cur = [(x[0], proj, y, cdf)], out

    for l in range(1, DEPTH):
        out, proj, y, cdf = _layer_forward(l, cur, mod, w_int[l], w_outf[l], small, ln_g, ln_b, f"layer_fwd_{l}")
        acts.append((cur, proj, y, cdf))
        cur = out

    shard_in, shard_out = D_IN // N_DEV, D_MODEL // N_DEV
    a, b = cur, loss_target[0]
    loss_lanes, carry, pending = None, (), []
    g_w_in_t, g_w_out = [None] * DEPTH, [None] * DEPTH
    for l in reversed(range(DEPTH)):
        dx, dproj, h, cat, dy, small_grads, dmod8, lanes, *shards = _layer_backward(
            l, a, b, *acts[l], mod, w_int[l], w_outf[l], small, ln_g, l == DEPTH - 1, f"layer_bwd_{l}",
            carry=carry, reduce=pending)
        if shards:
            g_w_in_t[l + 1], g_w_out[l + 1] = shards
        if l == DEPTH - 1:
            loss_lanes = lanes
        if l > 0:
            pending = [_grad_matmul(dproj, h, 640, f"grad_w_in_{l}").reshape(4, 2, shard_in, D_MODEL),
                       _grad_matmul(cat, dy, 512, f"grad_w_out_{l}").reshape(4, 2, shard_out, D_MODEL)]
        carry = (small_grads, dmod8)
        a = b = dx
    grad_x = a[None]

    g_w_in_t[0], g_w_out[0], small_tot, dmod_slots, loss_tile = _grad_tail(
        dproj, h, cat, dy, small_grads.reshape(4, 2, DEPTH * PACK_ROWS // N_DEV, 128), dmod8, loss_lanes)
    loss = loss_tile[0, 0]
    dmod_all = dmod_slots.reshape(N_DEV, DEPTH, 3 * D_MODEL)

    cols_ada = w_ada.shape[2]
    dmod_cols = jnp.transpose(lax.dynamic_slice_in_dim(dmod_all, mine * cols_ada, cols_ada, axis=2), (1, 0, 2))
    g_w_ada, d_w_ada, nm_w_ada, nv_w_ada = _adamw_ada(w_ada, m_w_ada, v_w_ada, jnp.transpose(act_all), dmod_cols,
                                                      "adamw_w_ada")
    g_b_ada, d_b_ada, nm_b_ada, nv_b_ada = _adamw_bias(b_ada, m_b_ada, v_b_ada, dmod_all, "adamw_b_ada")
    flat = lambda t: t.reshape(-1, t.shape[-1])
    to_t = lambda t: flat(jnp.transpose(t, (0, 2, 1)))
    from_t = lambda t: jnp.transpose(t.reshape(DEPTH, shard_in, D_MODEL), (0, 2, 1))
    g_w_in, d_w_in, nm_w_in, nv_w_in = [from_t(t) for t in _adamw(to_t(w_in), g_w_in_t, to_t(m_w_in), to_t(v_w_in),
                                                                  shard_in // 2, "adamw_w_in")]
    gwout, d_w_out, nm_w_out, nv_w_out = [t.reshape(w_out.shape) for t in _adamw(
        flat(w_out), g_w_out, flat(m_w_out), flat(v_w_out), shard_out, "adamw_w_out")]
    small_out = _adamw_small(small_tot.reshape(DEPTH * PACK_ROWS, 128), [small_w[n] for n in SMALL_NAMES],
                             [small_m[n] for n in SMALL_NAMES], [small_v[n] for n in SMALL_NAMES], "adamw_small")
    gs, ds, ms, vs = [dict(zip(SMALL_NAMES, group)) for group in small_out]

    def ordered(w_ada_, b_ada_, w_in_, small, w_out_):
        return (w_ada_, b_ada_, w_in_, small["w_pool"], small["pool_scale"], small["sgu_ln_g"], small["sgu_ln_b"],
                small["w_sgu"], small["b_sgu"], w_out_, small["ln_g"], small["ln_b"])

    return (loss, grad_x,
            *ordered(g_w_ada, g_b_ada, g_w_in, gs, gwout),
            *ordered(d_w_ada, d_b_ada, d_w_in, ds, d_w_out),
            *ordered(nm_w_ada, nm_b_ada, nm_w_in, ms, nm_w_out),
            *ordered(nv_w_ada, nv_b_ada, nv_w_in, vs, nv_w_out))
```

```python
import jax
import jax.numpy as jnp
from jax import lax
from jax.experimental import pallas as pl
from jax.experimental.pallas import tpu as pltpu

F32 = jnp.float32
BF16 = jnp.bfloat16

D_MODEL = 1024
SEQ = 2048
DEPTH = 2
D_POOL = 512
D_SGU = 512
D_IN = 2560
N_GROUPS = 4
GROUP = 128
N_HEADS = 4
HEAD = 128
CHUNK = 128
WINDOWS = (2, 4, 8, 16)
ALPHA = (2.0 * DEPTH) ** 0.25
LN_EPS = 1e-5
N_DEV = 8

ADAM_LR = 0.001
ADAM_B1 = 0.9
ADAM_B2 = 0.999
ADAM_EPS = 1e-08
ADAM_WD = 0.01
ADAM_STEP = 10

TM = 256
HALO = 16
N_TILES = SEQ // TM
VMEM_LIMIT = 60 * 1024 * 1024

ROW_WPOOL = 0
ROW_WSGU = 512
ROW_PSCALE = 1024
ROW_SLNG = 1028
ROW_SLNB = 1032
ROW_BSGU = 1036
ROW_LNG = 1040
ROW_LNB = 1048
PACK_ROWS = 1088
DMOD_COLS = DEPTH * 3 * D_MODEL // 8

SQRT_HALF = 0.7071067811865476
INV_SQRT_2PI = 0.3989422804014327


def _ln(x):
    mu = jnp.mean(x, axis=-1, keepdims=True)
    xc = x - mu
    var = jnp.mean(xc * xc, axis=-1, keepdims=True)
    rstd = lax.rsqrt(var + LN_EPS)
    return xc * rstd, rstd


def _ln_bwd(dxn, xn, rstd):
    m1 = jnp.mean(dxn, axis=-1, keepdims=True)
    m2 = jnp.mean(dxn * xn, axis=-1, keepdims=True)
    return rstd * (dxn - m1 - xn * m2)


def _normal_cdf(x):
    return 0.5 * (1.0 + lax.erf(x * SQRT_HALF))


def _gelu_parts(x, cdf, with_grad):
    if not with_grad:
        return x * cdf, None
    return x * cdf, cdf + x * (INV_SQRT_2PI * jnp.exp(-0.5 * x * x))


def _silu_parts(x):
    s = jax.nn.sigmoid(x)
    return x * s, s * (1.0 + x * (1.0 - s))


def _dot(a, b):
    return lax.dot_general(a, b, (((1,), (0,)), ((), ())), preferred_element_type=F32)


def _dot_nt(a, b):
    return lax.dot_general(a, b, (((1,), (1,)), ((), ())), preferred_element_type=F32)


def _dot_tn(a, b):
    return lax.dot_general(a, b, (((0,), (0,)), ((), ())), preferred_element_type=F32)


def _row_index(tile):
    return tile * TM + lax.broadcasted_iota(jnp.int32, (TM, 1), 0)


def _window_sums(ext, forward):
    n = TM + HALO
    cur = ext
    outs = []
    for g in range(N_GROUPS):
        step = 1 << g
        cur = cur + pltpu.roll(cur, step if forward else n - step, 0)
        rows = cur[HALO:, :GROUP] if forward else cur[:TM, :GROUP]
        outs.append(rows)
        cur = cur[:, GROUP:] if g + 1 < N_GROUPS else None
    return outs


def _inverse_counts(rows):
    return [1.0 / jnp.minimum(rows + 1, w).astype(F32) for w in WINDOWS]


def _tril_bf16(w):
    t = lax.broadcasted_iota(jnp.int32, (CHUNK, CHUNK), 0)
    s = lax.broadcasted_iota(jnp.int32, (CHUNK, CHUNK), 1)
    return jnp.where(t >= s, w, 0.0).astype(BF16)


class _MixWeights:
    def __init__(self, layer, wpool_ref, pscale_ref, slng_ref, slnb_ref, wsgu_ref, bsgut_ref):
        self.layer = layer
        self.wpool_ref, self.pscale_ref, self.slng_ref, self.slnb_ref = wpool_ref, pscale_ref, slng_ref, slnb_ref
        self.wsgu_ref, self.bsgut_ref = wsgu_ref, bsgut_ref

    def pool(self, g):
        return self.wpool_ref[self.layer, g].astype(BF16)

    def pool_scale(self, g):
        return self.pscale_ref[self.layer:self.layer + 1, g * GROUP:(g + 1) * GROUP]

    def ln_gain(self, h):
        return self.slng_ref[self.layer, h:h + 1, :]

    def ln_bias(self, h):
        return self.slnb_ref[self.layer, h:h + 1, :]

    def mix(self, h):
        return _tril_bf16(self.wsgu_ref[self.layer, h])

    def mix_bias(self, h):
        return self.bsgut_ref[self.layer, :, h:h + 1]


SMALL_SPECS = ((DEPTH, N_GROUPS, GROUP, GROUP), (DEPTH, D_POOL), (DEPTH, N_HEADS, HEAD), (DEPTH, N_HEADS, HEAD),
               (DEPTH, N_HEADS, CHUNK, CHUNK), (DEPTH, CHUNK, N_HEADS))


def _mix_forward(proj, halo, tile, w, cdf=None):
    keep = cdf is not None
    rows = _row_index(tile)
    inv_counts = _inverse_counts(rows)
    xa = proj[:, 0:D_POOL]
    ga = proj[:, D_POOL:2 * D_POOL]
    sums = _window_sums(jnp.concatenate([halo, xa], axis=0), True)
    ga_act, ga_grad = _silu_parts(ga)
    pooled, pw, ya = [], [], []
    for g in range(N_GROUPS):
        sl = slice(g * GROUP, (g + 1) * GROUP)
        p = (sums[g] * inv_counts[g] - xa[:, sl]).astype(BF16)
        q = _dot(p, w.pool(g))
        pooled.append(p)
        pw.append(q)
        ya.append(q * w.pool_scale(g) * ga_act[:, sl])

    u = proj[:, 2 * D_POOL:2 * D_POOL + D_SGU]
    v = proj[:, 2 * D_POOL + D_SGU:2 * D_POOL + 2 * D_SGU]
    gb = proj[:, 2 * D_POOL + 2 * D_SGU:]
    gb_act, gb_grad = _silu_parts(gb)
    if cdf is None:
        cdf = jnp.concatenate([_normal_cdf(u), _normal_cdf(v)], axis=1)
    u_act, u_grad = _gelu_parts(u, cdf[:, :D_SGU], keep)
    v_act, v_grad = _gelu_parts(v, cdf[:, D_SGU:], keep)
    vn, vrstd, vln, mixed, yb = [], [], [], [], []
    for h in range(N_HEADS):
        sl = slice(h * HEAD, (h + 1) * HEAD)
        n_h, r_h = _ln(v_act[:, sl])
        l_h = (n_h * w.ln_gain(h) + w.ln_bias(h)).astype(BF16)
        w_h = w.mix(h)
        bias = w.mix_bias(h)
        m_h = jnp.concatenate(
            [_dot(w_h, l_h[k * CHUNK:(k + 1) * CHUNK]) + bias for k in range(TM // CHUNK)], axis=0)
        vn.append(n_h)
        vrstd.append(r_h)
        vln.append(l_h)
        mixed.append(m_h)
        yb.append(u_act[:, sl] * m_h * gb_act[:, sl])
    cat = jnp.concatenate(ya + yb, axis=1)
    if not keep:
        return cat, cdf
    return cat, dict(inv_counts=inv_counts, ga_act=ga_act, ga_grad=ga_grad, pooled=pooled, pw=pw, u_grad=u_grad,
                     v_grad=v_grad, u_act=u_act, gb_act=gb_act, gb_grad=gb_grad, vn=vn, vrstd=vrstd, vln=vln,
                     mixed=mixed)


def _const_spec(shape):
    nd = len(shape)
    return pl.BlockSpec(shape, lambda i: (0,) * nd)


def _layer_forward(layer, x, mod, w_int, w_outf, small, ln_g, ln_b, name):
    def body(x_ref, mod_ref, wint_ref, wout_ref, wpool_ref, pscale_ref, slng_ref, slnb_ref, wsgu_ref, bsgut_ref,
             lng_ref, lnb_ref, out_ref, proj_ref, y_ref, cdf_ref, halo_ref):
        weights = _MixWeights(layer, wpool_ref, pscale_ref, slng_ref, slnb_ref, wsgu_ref, bsgut_ref)
        tile = pl.program_id(0)

        @pl.when(tile == 0)
        def _():
            halo_ref[...] = jnp.zeros_like(halo_ref)

        xt = x_ref[...]
        shift = mod_ref[layer:layer + 1, 0:D_MODEL]
        scale = mod_ref[layer:layer + 1, D_MODEL:2 * D_MODEL]
        gate = mod_ref[layer:layer + 1, 2 * D_MODEL:]
        xn, _ = _ln(xt)
        h = (xn * (1.0 + scale) + shift).astype(BF16)
        proj = _dot_nt(h, wint_ref[...])
        proj_ref[...] = proj
        cat, cdf_ref[...] = _mix_forward(proj, halo_ref[...], tile, weights)
        halo_ref[...] = proj[TM - HALO:, 0:D_POOL]
        y = _dot(cat.astype(BF16), wout_ref[...])
        y_ref[...] = y
        zn, _ = _ln(ALPHA * xt + gate * y)
        out_ref[...] = zn * lng_ref[layer:layer + 1, :] + lnb_ref[layer:layer + 1, :]

    row = lambda w: pl.BlockSpec((TM, w), lambda i: (i, 0))
    return pl.pallas_call(
        body,
        name=name,
        grid=(N_TILES,),
        in_specs=[row(D_MODEL), _const_spec((DEPTH, 3 * D_MODEL)), _const_spec((D_IN, D_MODEL)),
                  _const_spec((D_MODEL, D_MODEL))] + [_const_spec(s) for s in SMALL_SPECS]
                 + [_const_spec((DEPTH, D_MODEL)), _const_spec((DEPTH, D_MODEL))],
        out_specs=[row(D_MODEL), row(D_IN), row(D_MODEL), row(2 * D_SGU)],
        out_shape=[jax.ShapeDtypeStruct((SEQ, D_MODEL), F32), jax.ShapeDtypeStruct((SEQ, D_IN), F32),
                   jax.ShapeDtypeStruct((SEQ, D_MODEL), F32), jax.ShapeDtypeStruct((SEQ, 2 * D_SGU), F32)],
        scratch_shapes=[pltpu.VMEM((HALO, D_POOL), F32)],
        compiler_params=pltpu.CompilerParams(dimension_semantics=("arbitrary",), vmem_limit_bytes=VMEM_LIMIT),
    )(x, mod, w_int, w_outf, *small, ln_g, ln_b)


VEC_LNG, VEC_LNB, VEC_POOL, VEC_SGU, VEC_SHIFT, VEC_SCALE, VEC_GATE, VEC_LOSS = range(8)


def _layer_backward(layer, a, b, x, proj, y, cdf, mod, w_int, w_outf, small, ln_g, is_last, name, carry=(),
                    reduce=()):
    n_red, n_carry = len(reduce), len(carry)
    base = layer * PACK_ROWS

    def body(a_ref, b_ref, x_ref, proj_ref, prev_ref, y_ref, cdf_ref, mod_ref, wint_ref, wout_ref, wpool_ref,
             pscale_ref, slng_ref, slnb_ref, wsgu_ref, bsgut_ref, lng_ref, *rest):
        weights = _MixWeights(layer, wpool_ref, pscale_ref, slng_ref, slnb_ref, wsgu_ref, bsgut_ref)
        carry_refs, rest = rest[:n_carry], rest[n_carry:]
        part_refs, rest = rest[:n_red], rest[n_red:]
        dx_ref, dproj_ref, h_ref, cat_ref, dy_ref, small_ref, dmod_ref, loss_ref = rest[:8]
        shard_refs, rest = rest[8:8 + n_red], rest[8 + n_red:]
        vec_ref, dmix_ref, halo_ref = rest[:3]
        step = pl.program_id(0)
        tile = N_TILES - 1 - step

        def scatter():
            bufs, sems = rest[3:3 + 5 * n_red], rest[3 + 5 * n_red:]
            arrays = [dict(part=part_refs[n], out=shard_refs[n], staged=True, stage=bufs[5 * n], sib=bufs[5 * n + 1],
                           snd=bufs[5 * n + 2], rcv=bufs[5 * n + 3], relay=bufs[5 * n + 4]) for n in range(n_red)]
            return _ChipReduceScatter(arrays, *sems)

        @pl.when(step == 0)
        def _():
            small_ref[...] = jnp.zeros_like(small_ref)
            dmod_ref[...] = jnp.zeros_like(dmod_ref)
            vec_ref[...] = jnp.zeros_like(vec_ref)
            dmix_ref[...] = jnp.zeros_like(dmix_ref)
            halo_ref[...] = jnp.zeros_like(halo_ref)
            if n_red:
                scatter().start()

        if n_red:
            @pl.when(step == 1)
            def _():
                scatter().exchange()

            @pl.when(step == N_TILES // 2)
            def _():
                scatter().fold()

        def acc(row, lo, val):
            hi = lo + val.shape[1]
            vec_ref[row:row + 1, lo:hi] += jnp.sum(val, axis=0, keepdims=True)

        xt = x_ref[...]
        yt = y_ref[...]
        shift = mod_ref[layer:layer + 1, 0:D_MODEL]
        scale = mod_ref[layer:layer + 1, D_MODEL:2 * D_MODEL]
        gate = mod_ref[layer:layer + 1, 2 * D_MODEL:]
        ln_gain = lng_ref[layer:layer + 1, :]

        zn, zrstd = _ln(ALPHA * xt + gate * yt)
        if is_last:
            diff = a_ref[...] - b_ref[...]
            acc(VEC_LOSS, 0, diff * diff)
            dout = diff * (1.0 / D_MODEL)
        else:
            dout = a_ref[...]
        acc(VEC_LNG, 0, dout * zn)
        acc(VEC_LNB, 0, dout)
        dz = _ln_bwd(dout * ln_gain, zn, zrstd)
        acc(VEC_GATE, 0, dz * yt)
        dy = (dz * gate).astype(BF16)
        dy_ref[...] = dy
        dcat = _dot_nt(dy, wout_ref[...])

        proj = proj_ref[...]
        prev = jnp.where(tile > 0, prev_ref[...], 0.0)
        cat, k = _mix_forward(proj, prev, tile, weights, cdf_ref[...])
        cat_ref[...] = cat.astype(BF16)

        dga, dq = [], []
        for g in range(N_GROUPS):
            sl = slice(g * GROUP, (g + 1) * GROUP)
            pscale = weights.pool_scale(g)
            dya = dcat[:, sl]
            dyp = dya * k["ga_act"][:, sl]
            dga.append(dya * k["pw"][g] * pscale * k["ga_grad"][:, sl])
            acc(VEC_POOL, g * GROUP, dyp * k["pw"][g])
            dpw = (dyp * pscale).astype(BF16)
            rows = pl.ds(base + ROW_WPOOL + g * GROUP, GROUP)
            small_ref[rows, :] += _dot_tn(k["pooled"][g], dpw)
            dq.append(_dot_nt(dpw, weights.pool(g)))
        dpooled = jnp.concatenate(dq, axis=1)
        scaled = jnp.concatenate([dq[g] * k["inv_counts"][g] for g in range(N_GROUPS)], axis=1)
        sums = _window_sums(jnp.concatenate([scaled, halo_ref[...]], axis=0), False)
        halo_ref[...] = scaled[0:HALO]
        dxa = jnp.concatenate(sums, axis=1) - dpooled

        du, dv, dgb = [], [], []
        for h in range(N_HEADS):
            sl = slice(h * HEAD, (h + 1) * HEAD)
            dyb = dcat[:, D_POOL + h * HEAD:D_POOL + (h + 1) * HEAD]
            m_h = k["mixed"][h]
            ug = k["u_act"][:, sl] * dyb
            du.append(dyb * m_h * k["gb_act"][:, sl] * k["u_grad"][:, sl])
            dgb.append(ug * m_h * k["gb_grad"][:, sl])
            dmixed = ug * k["gb_act"][:, sl]
            dmixed_bf = dmixed.astype(BF16)
            w_h = weights.mix(h)
            dvln_parts = []
            dmix_sum = dmix_ref[h]
            wsgu_rows = pl.ds(base + ROW_WSGU + h * CHUNK, CHUNK)
            dws = small_ref[wsgu_rows, :]
            for c in range(TM // CHUNK):
                cs = slice(c * CHUNK, (c + 1) * CHUNK)
                dmix_sum = dmix_sum + dmixed[cs]
                dws = dws + _dot_nt(dmixed_bf[cs], k["vln"][h][cs])
                dvln_parts.append(_dot_tn(w_h, dmixed_bf[cs]))
            dmix_ref[h] = dmix_sum
            small_ref[wsgu_rows, :] = dws
            dvln = jnp.concatenate(dvln_parts, axis=0)
            acc(VEC_SGU, h * HEAD, dvln * k["vn"][h])
            acc(VEC_SGU, D_SGU + h * HEAD, dvln)
            dvv = _ln_bwd(dvln * weights.ln_gain(h), k["vn"][h], k["vrstd"][h])
            dv.append(dvv * k["v_grad"][:, sl])

        dproj = jnp.concatenate([dxa] + dga + du + dv + dgb, axis=1).astype(BF16)
        dproj_ref[...] = dproj
        dh = _dot(dproj, wint_ref[...])

        xn, xrstd = _ln(xt)
        h_ref[...] = (xn * (1.0 + scale) + shift).astype(BF16)
        acc(VEC_SCALE, 0, dh * xn)
        acc(VEC_SHIFT, 0, dh)
        dx_ref[...] = _ln_bwd(dh * (1.0 + scale), xn, xrstd) + ALPHA * dz

        @pl.when(step == N_TILES - 1)
        def _():
            def put(row0, vec_row, lo, n):
                for r in range(n):
                    small_ref[base + row0 + r:base + row0 + r + 1, :] = (
                        vec_ref[vec_row:vec_row + 1, lo + r * 128:lo + (r + 1) * 128])

            put(ROW_PSCALE, VEC_POOL, 0, 4)
            put(ROW_SLNG, VEC_SGU, 0, 4)
            put(ROW_SLNB, VEC_SGU, D_SGU, 4)
            put(ROW_LNG, VEC_LNG, 0, 8)
            put(ROW_LNB, VEC_LNB, 0, 8)
            ones = jnp.ones((8, HEAD), F32)
            t = lax.broadcasted_iota(jnp.int32, (CHUNK, CHUNK), 0)
            s = lax.broadcasted_iota(jnp.int32, (CHUNK, CHUNK), 1)
            for h in range(N_HEADS):
                bias_rows = lax.dot_general(ones, dmix_ref[h], (((1,), (1,)), ((), ())),
                                            preferred_element_type=F32, precision=lax.Precision.HIGHEST)
                small_ref[base + ROW_BSGU + h:base + ROW_BSGU + h + 1, :] = bias_rows[0:1]
                rows = pl.ds(base + ROW_WSGU + h * CHUNK, CHUNK)
                small_ref[rows, :] = jnp.where(t >= s, small_ref[rows, :], 0.0)
            pieces = ((0, VEC_SHIFT, 0, 768),
                      (1, VEC_SHIFT, 768, 256), (1, VEC_SCALE, 0, 512),
                      (2, VEC_SCALE, 512, 512), (2, VEC_GATE, 0, 256),
                      (3, VEC_GATE, 256, 768))
            filled = [0] * 4
            for q, vec_row, lo, n in pieces:
                row = 4 * layer + q
                dmod_ref[row:row + 1, filled[q]:filled[q] + n] = vec_ref[vec_row:vec_row + 1, lo:lo + n]
                filled[q] += n
            if n_carry:
                for other in range(layer + 1, DEPTH):
                    rows = pl.ds(other * PACK_ROWS, PACK_ROWS)
                    small_ref[rows, :] = carry_refs[0][rows, :]
                    dmod_ref[4 * other:4 * other + 4, :] = carry_refs[1][4 * other:4 * other + 4, :]
            loss_ref[...] = vec_ref[VEC_LOSS:VEC_LOSS + 1, :]
            if n_red:
                scatter().finish()
                scatter().wait_sends()

    rev = lambda w: pl.BlockSpec((TM, w), lambda i: (N_TILES - 1 - i, 0))
    prev_spec = pl.BlockSpec(
        (HALO, D_POOL), lambda i: (jnp.maximum((N_TILES - 1 - i) * (TM // HALO) - 1, 0), 0))
    comm_scratch = []
    for p in reduce:
        comm_scratch += _ChipReduceScatter.buffers(p.shape[2], p.shape[3], p.dtype)
    if n_red:
        comm_scratch += _ChipReduceScatter.semaphores(n_red)
    return pl.pallas_call(
        body,
        name=name,
        grid=(N_TILES,),
        in_specs=[rev(D_MODEL), rev(D_MODEL) if is_last else pl.BlockSpec((TM, D_MODEL), lambda i: (0, 0)),
                  rev(D_MODEL), rev(D_IN), prev_spec, rev(D_MODEL), rev(2 * D_SGU),
                  _const_spec((DEPTH, 3 * D_MODEL)), _const_spec((D_IN, D_MODEL)), _const_spec((D_MODEL, D_MODEL))]
                 + [_const_spec(s) for s in SMALL_SPECS] + [_const_spec((DEPTH, D_MODEL))]
                 + [_const_spec(c.shape) for c in carry] + [ANY] * n_red,
        out_specs=[rev(D_MODEL), rev(D_IN), rev(D_MODEL), rev(D_MODEL), rev(D_MODEL),
                   _const_spec((DEPTH * PACK_ROWS, 128)), _const_spec((8, DMOD_COLS)), _const_spec((1, D_MODEL))]
                  + [_const_spec(p.shape[2:]) for p in reduce],
        out_shape=[jax.ShapeDtypeStruct((SEQ, D_MODEL), F32), jax.ShapeDtypeStruct((SEQ, D_IN), BF16),
                   jax.ShapeDtypeStruct((SEQ, D_MODEL), BF16), jax.ShapeDtypeStruct((SEQ, D_MODEL), BF16),
                   jax.ShapeDtypeStruct((SEQ, D_MODEL), BF16), jax.ShapeDtypeStruct((DEPTH * PACK_ROWS, 128), F32),
                   jax.ShapeDtypeStruct((8, DMOD_COLS), F32), jax.ShapeDtypeStruct((1, D_MODEL), F32)]
                  + [jax.ShapeDtypeStruct(p.shape[2:], F32) for p in reduce],
        scratch_shapes=[pltpu.VMEM((8, D_MODEL), F32), pltpu.VMEM((N_HEADS, CHUNK, HEAD), F32),
                        pltpu.VMEM((HALO, D_POOL), F32)] + comm_scratch,
        compiler_params=pltpu.CompilerParams(dimension_semantics=("arbitrary",), vmem_limit_bytes=VMEM_LIMIT),
    )(a, b, x, proj, proj, y, cdf, mod, w_int, w_outf, *small, ln_g, *carry, *reduce)


def _grad_matmul(lhs, rhs, block_cols, name):
    m, n = lhs.shape[1], rhs.shape[1]

    def body(lhs_ref, rhs_ref, out_ref):
        out_ref[...] = _dot_tn(lhs_ref[...], rhs_ref[...]).astype(BF16)

    return pl.pallas_call(
        body,
        name=name,
        grid=(m // block_cols,),
        in_specs=[pl.BlockSpec((SEQ, block_cols), lambda j: (0, j)), pl.BlockSpec((SEQ, n), lambda j: (0, 0))],
        out_specs=pl.BlockSpec((block_cols, n), lambda j: (j, 0)),
        out_shape=jax.ShapeDtypeStruct((m, n), BF16),
        compiler_params=pltpu.CompilerParams(dimension_semantics=("arbitrary",), vmem_limit_bytes=VMEM_LIMIT),
    )(lhs, rhs)


def _adamw_math(w, g, m, v):
    m = ADAM_B1 * m + (1.0 - ADAM_B1) * g
    v = ADAM_B2 * v + (1.0 - ADAM_B2) * (g * g)
    m_hat = m / (1.0 - ADAM_B1 ** ADAM_STEP)
    v_hat = v / (1.0 - ADAM_B2 ** ADAM_STEP)
    delta = -ADAM_LR * (m_hat / (jnp.sqrt(v_hat) + ADAM_EPS) + ADAM_WD * w)
    return delta, m, v


def _adamw(w, grads, m, v, block_rows, name):
    rows, cols = grads[0].shape
    blocks = rows // block_rows

    def body(w_ref, m_ref, v_ref, *rest):
        g_refs, (g_ref, d_ref, nm_ref, nv_ref) = rest[:DEPTH], rest[DEPTH:]
        for layer in range(DEPTH):
            @pl.when(pl.program_id(0) == layer)
            def _():
                g = g_refs[layer][...]
                g_ref[...] = g
                d_ref[...], nm_ref[...], nv_ref[...] = _adamw_math(w_ref[...], g, m_ref[...], v_ref[...])

    def grad_spec(layer):
        return pl.BlockSpec((block_rows, cols),
                            lambda l, i: (jnp.where(l == layer, i, jnp.where(l < layer, 0, blocks - 1)), 0))

    spec = pl.BlockSpec((block_rows, cols), lambda l, i: (l * blocks + i, 0))
    return pl.pallas_call(
        body,
        name=name,
        grid=(DEPTH, blocks),
        in_specs=[spec] * 3 + [grad_spec(layer) for layer in range(DEPTH)],
        out_specs=[spec] * 4,
        out_shape=[jax.ShapeDtypeStruct(w.shape, F32)] * 4,
        compiler_params=pltpu.CompilerParams(dimension_semantics=("arbitrary", "arbitrary"),
                                             vmem_limit_bytes=VMEM_LIMIT),
    )(w, m, v, *grads)


def _adamw_ada(w, m, v, act_t, dmod_cols, name):
    cols = w.shape[2]

    rows = 256

    def body(w_ref, m_ref, v_ref, act_ref, dmod_ref, g_ref, d_ref, nm_ref, nv_ref):
        act = act_ref[...]
        dm = dmod_ref[0]
        g = act[:, 0:1] * dm[0:1, :]
        for b in range(1, N_DEV):
            g = g + act[:, b:b + 1] * dm[b:b + 1, :]
        g_ref[0] = g
        d_ref[0], nm_ref[0], nv_ref[0] = _adamw_math(w_ref[0], g, m_ref[0], v_ref[0])

    spec = pl.BlockSpec((1, rows, cols), lambda l, i: (l, i, 0))
    return pl.pallas_call(
        body,
        name=name,
        grid=(DEPTH, D_MODEL // rows),
        in_specs=[spec, spec, spec, pl.BlockSpec((rows, N_DEV), lambda l, i: (i, 0)),
                  pl.BlockSpec((1, N_DEV, cols), lambda l, i: (l, 0, 0))],
        out_specs=[spec] * 4,
        out_shape=[jax.ShapeDtypeStruct(w.shape, F32)] * 4,
        compiler_params=pltpu.CompilerParams(dimension_semantics=("arbitrary", "arbitrary"),
                                             vmem_limit_bytes=VMEM_LIMIT),
    )(w, m, v, act_t, dmod_cols)


def _adamw_bias(w, m, v, dmod_all, name):
    def body(w_ref, m_ref, v_ref, dmod_ref, g_ref, d_ref, nm_ref, nv_ref):
        g = dmod_ref[0]
        for b in range(1, N_DEV):
            g = g + dmod_ref[b]
        g_ref[...] = g
        d_ref[...], nm_ref[...], nv_ref[...] = _adamw_math(w_ref[...], g, m_ref[...], v_ref[...])

    return pl.pallas_call(
        body,
        name=name,
        out_shape=[jax.ShapeDtypeStruct(w.shape, F32)] * 4,
        compiler_params=pltpu.CompilerParams(vmem_limit_bytes=VMEM_LIMIT),
    )(w, m, v, dmod_all)


MESH = pl.DeviceIdType.MESH
SIBLING = 1
ANY = pl.BlockSpec(memory_space=pl.ANY)
VMEM = pl.BlockSpec(memory_space=pltpu.VMEM)


def _me():
    return lax.axis_index("x"), lax.axis_index("y"), lax.axis_index("c")


def _peer(r):
    x, y, c = _me()
    return (1 - x if r & 4 else x, 1 - y if r & 2 else y, 1 - c if r & 1 else c)


def _index(dev):
    return 4 * dev[0] + 2 * dev[1] + dev[2]


def _remote(src, dst, send_sem, recv_sem, dev):
    return pltpu.make_async_remote_copy(src_ref=src, dst_ref=dst, send_sem=send_sem, recv_sem=recv_sem,
                                        device_id=dev, device_id_type=MESH)


ACROSS_X, ACROSS_Y, ACROSS_BOTH = 4, 2, 6
GATHER_SEMS = 11


class _TwoLevelGather:
    def __init__(self, out, send_sems, recv_sems, src=None):
        self.out, self.send_sems, self.recv_sems, self.src = out, send_sems, recv_sems, src
        self.rows = (out.shape[0] // N_DEV) if len(out.shape) == 2 else out.shape[1]
        self.half = self.rows // 2

    def _slot(self, block):
        if len(self.out.shape) == 2:
            return self.out.at[pl.ds(pl.multiple_of(_index(block) * self.rows, self.rows), self.rows)]
        return self.out.at[_index(block)]

    def _copy(self, k, block, part, to, src=None):
        slot = self._slot(block)
        if part is not None:
            rows = pl.ds(part * self.half, self.half)
            slot = slot.at[rows]
            src = None if src is None else src.at[rows]
        return _remote(slot if src is None else src, slot, self.send_sems.at[k], self.recv_sems.at[k], to)

    def _mine(self):
        me = _me()
        src = self._slot(me) if self.src is None else self.src
        x, y = _peer(ACROSS_X), _peer(ACROSS_Y)
        return [self._copy(1, me, 0, x, src), self._copy(3, me, 1, y, src), self._copy(0, me, None, _peer(SIBLING), src),
                self._copy(2, me, 1, x, src), self._copy(4, me, 0, y, src)]

    def _relayed(self):
        return [self._copy(5, _peer(ACROSS_X), 0, _peer(ACROSS_Y)), self._copy(6, _peer(ACROSS_Y), 1, _peer(ACROSS_X))]

    def _passed(self):
        sib, far = _peer(SIBLING), _peer(ACROSS_BOTH)
        return [self._copy(7, _peer(ACROSS_X), None, sib), self._copy(8, _peer(ACROSS_Y), None, sib),
                self._copy(9, far, 0, sib), self._copy(10, far, 1, sib)]

    def _arrival(self, k, r, part):
        return self._copy(k, _peer(r), part, _me())

    def send_first(self):
        for cp in self._mine()[:3]:
            cp.start()

    def send_second(self):
        for cp in self._mine()[3:]:
            cp.start()

    def send_mine(self):
        self.send_first()
        self.send_second()

    def relay(self):
        relayed = self._relayed()
        self._arrival(1, ACROSS_X, 0).wait_recv()
        relayed[0].start()
        self._arrival(3, ACROSS_Y, 1).wait_recv()
        relayed[1].start()

    def pass_near(self):
        passed = self._passed()
        self._arrival(2, ACROSS_X, 1).wait_recv()
        passed[0].start()
        self._arrival(4, ACROSS_Y, 0).wait_recv()
        passed[1].start()

    def pass_far(self):
        passed = self._passed()
        self._arrival(5, ACROSS_BOTH, 0).wait_recv()
        passed[2].start()
        self._arrival(6, ACROSS_BOTH, 1).wait_recv()
        passed[3].start()

    def pass_on(self):
        self.pass_near()
        self.pass_far()

    def wait_sibling(self):
        self._arrival(0, SIBLING, None).wait_recv()

    def wait_passed(self, r):
        if r == ACROSS_BOTH:
            self._arrival(9, r ^ SIBLING, 0).wait_recv()
            self._arrival(10, r ^ SIBLING, 1).wait_recv()
        else:
            self._arrival(7 if r == ACROSS_X else 8, r ^ SIBLING, None).wait_recv()

    def wait_rest(self):
        self.wait_sibling()
        for r in (ACROSS_X, ACROSS_Y, ACROSS_BOTH):
            self.wait_passed(r)

    def wait_sends(self):
        for cp in self._mine() + self._relayed() + self._passed():
            cp.wait_send()


class _ChipReduceScatter:
    SLOTS = 6

    def __init__(self, arrays, l_sem, d_send, d_recv, i_send, i_recv):
        self.arrays = arrays
        self.l_sem, self.d_send, self.d_recv, self.i_send, self.i_recv = l_sem, d_send, d_recv, i_send, i_recv

    @staticmethod
    def buffers(rows, cols, dtype, staged=True):
        stage = [pltpu.VMEM((4, rows, cols), dtype)] if staged else []
        return stage + [pltpu.VMEM((4, rows, cols), dtype), pltpu.VMEM((3, rows, cols), dtype),
                        pltpu.VMEM((2, rows, cols), dtype), pltpu.VMEM((2, rows // 2, cols), dtype)]

    @classmethod
    def semaphores(cls, n):
        return [pltpu.SemaphoreType.DMA((n,)), pltpu.SemaphoreType.DMA((n,)), pltpu.SemaphoreType.DMA((n,)),
                pltpu.SemaphoreType.DMA((n, cls.SLOTS)), pltpu.SemaphoreType.DMA((n, cls.SLOTS))]

    def _pick(self, which):
        return list(enumerate(self.arrays)) if which is None else [(n, self.arrays[n]) for n in which]

    def _staging(self, which):
        c = _me()[2]
        return [pltpu.make_async_copy(a["part"].at[pl.ds(0, 4), c], a["stage"], self.l_sem.at[n])
                for n, a in self._pick(which) if a["staged"]]

    def _first(self, which):
        other = 1 - _me()[2]
        return [_remote(a["part"].at[pl.ds(0, 4), other], a["sib"], self.d_send.at[n], self.d_recv.at[n],
                        _peer(SIBLING)) for n, a in self._pick(which)]

    @staticmethod
    def _halves(a):
        half = a["rcv"].shape[1] // 2
        return pl.ds(0, half), pl.ds(half, half)

    def _hops(self, n, a):
        h0, h1 = self._halves(a)
        x, y = _peer(ACROSS_X), _peer(ACROSS_Y)
        snd, rcv, relay = a["snd"], a["rcv"], a["relay"]
        pairs = [(snd.at[2, h0], relay.at[0], x), (snd.at[2, h1], relay.at[1], y),
                 (snd.at[0, h0], rcv.at[0, h0], x), (snd.at[0, h1], rcv.at[0, h1], x),
                 (snd.at[1, h1], rcv.at[1, h1], y), (snd.at[1, h0], rcv.at[1, h0], y)]
        return [_remote(s, d, self.i_send.at[n, k], self.i_recv.at[n, k], to) for k, (s, d, to) in enumerate(pairs)]

    def _mine(self, a, chip, rows=None):
        src = a["stage"].at[chip] if a["staged"] else a["part"].at[chip, _me()[2]]
        mine, sib = (src[...], a["sib"][chip]) if rows is None else (src[rows, :], a["sib"][chip, rows, :])
        return mine.astype(F32) + sib.astype(F32)

    def start(self, which=None):
        for cp in self._staging(which) + self._first(which):
            cp.start()

    def exchange(self, which=None):
        for cp in self._staging(which):
            cp.wait()
        for cp in self._first(which):
            cp.wait_recv()
        chip = lambda dev: 2 * dev[0] + dev[1]
        across_x, across_y, far = chip(_peer(ACROSS_X)), chip(_peer(ACROSS_Y)), chip(_peer(ACROSS_BOTH))
        picked = self._pick(which)
        for n, a in picked:
            hops = self._hops(n, a)
            a["snd"][2] = self._mine(a, far).astype(a["snd"].dtype)
            hops[0].start()
            hops[1].start()
        for n, a in picked:
            h0, h1 = self._halves(a)
            hops = self._hops(n, a)
            dtype = a["snd"].dtype
            a["snd"][0, h0, :] = self._mine(a, across_x, h0).astype(dtype)
            hops[2].start()
            a["snd"][1, h1, :] = self._mine(a, across_y, h1).astype(dtype)
            hops[4].start()

    def fold(self, which=None):
        chip = lambda dev: 2 * dev[0] + dev[1]
        across_x, across_y = chip(_peer(ACROSS_X)), chip(_peer(ACROSS_Y))
        for n, a in self._pick(which):
            h0, h1 = self._halves(a)
            hops = self._hops(n, a)
            dtype = a["snd"].dtype
            hops[1].wait_recv()
            a["snd"][0, h1, :] = (self._mine(a, across_x, h1) + a["relay"][1].astype(F32)).astype(dtype)
            hops[3].start()
            hops[0].wait_recv()
            a["snd"][1, h0, :] = (self._mine(a, across_y, h0) + a["relay"][0].astype(F32)).astype(dtype)
            hops[5].start()

    def finish(self, which=None):
        x, y, _ = _me()
        home = 2 * x + y
        for n, a in self._pick(which):
            hops = self._hops(n, a)
            a["out"][...] = self._mine(a, home)
            hops[2].wait_recv()
            hops[3].wait_recv()
            a["out"][...] += a["rcv"][0].astype(F32)
            hops[4].wait_recv()
            hops[5].wait_recv()
            a["out"][...] += a["rcv"][1].astype(F32)

    def wait_sends(self, which=None):
        for cp in self._first(which):
            cp.wait_send()
        for n, a in self._pick(which):
            for cp in self._hops(n, a):
                cp.wait_send()


def _direct_exchange(src_of, dst_of, send_sems, recv_sems):
    me = _me()
    copies = [_remote(src_of(_peer(r)), dst_of(me), send_sems.at[r - 1], recv_sems.at[r - 1], _peer(r))
              for r in range(1, N_DEV)]
    for cp in copies:
        cp.start()
    return copies


def _wait_direct(copies):
    for cp in copies:
        cp.wait_recv()
    for cp in copies:
        cp.wait_send()


def _first_layer(x, c_row, w_ada, b_ada, mine_now, mine_next, small, ln_g, ln_b):
    layer = 0
    cols = w_ada.shape[2]
    shard = mine_now[0].shape[0]
    pair = 2 * shard

    def body(x_ref, c_ref, wada_ref, bada_ref, wpool_ref, pscale_ref, slng_ref, slnb_ref, wsgu_ref, bsgut_ref,
             lng_ref, lnb_ref, wint_hbm, wout_hbm, nxt_in_hbm, nxt_out_hbm,
             out_ref, y_ref, cdf_ref, acts_ref, mod_ref, proj_hbm, wint_keep, wout_keep, nxt_in_full, nxt_out_full,
             wint_v, wout_v, h_buf, proj_blk, proj_tile, halo_ref, act_all, act_src, part, mod_recv,
             w_send, w_recv, w_local, p_sems, t_sems, a_send, a_recv, m_send, m_recv, n_send, n_recv, n_local):
        weights = _MixWeights(layer, wpool_ref, pscale_ref, slng_ref, slnb_ref, wsgu_ref, bsgut_ref)
        tile = pl.program_id(0)

        def now_gathers():
            return (_TwoLevelGather(wint_v, w_send.at[0], w_recv.at[0], src=wint_hbm),
                    _TwoLevelGather(wout_v, w_send.at[1], w_recv.at[1], src=wout_hbm))

        def next_gathers():
            return (_TwoLevelGather(nxt_in_full, n_send.at[0], n_recv.at[0], src=nxt_in_hbm),
                    _TwoLevelGather(nxt_out_full, n_send.at[1], n_recv.at[1], src=nxt_out_hbm))

        def next_own():
            mine = _index(_me())
            return [pltpu.make_async_copy(nxt_in_hbm, nxt_in_full.at[mine], n_local.at[0]),
                    pltpu.make_async_copy(nxt_out_hbm, nxt_out_full.at[mine], n_local.at[1])]

        def keeps():
            return [pltpu.make_async_copy(wint_v, wint_keep, w_local.at[2]),
                    pltpu.make_async_copy(wout_v, wout_keep, w_local.at[3])]

        def tile_read(t):
            slot = t % 2
            return pltpu.make_async_copy(proj_hbm.at[pl.ds(pl.multiple_of(t * TM, TM), TM)], proj_tile.at[slot],
                                         t_sems.at[slot])

        @pl.when(tile == 0)
        def _():
            me = _me()
            mine = _index(me)
            halo_ref[...] = jnp.zeros_like(halo_ref)

            cval = c_ref[...]
            act_src[...] = jnp.zeros_like(act_src)
            act_src[0:1, :] = cval * jax.nn.sigmoid(cval)
            act_all[mine] = act_src[...]
            act_copies = _direct_exchange(lambda p: act_src, lambda m: act_all.at[_index(m)], a_send, a_recv)

            gather_in, gather_out = now_gathers()
            own_in = pltpu.make_async_copy(wint_hbm, gather_in._slot(me), w_local.at[0])
            own_out = pltpu.make_async_copy(wout_hbm, gather_out._slot(me), w_local.at[1])
            own_in.start()
            own_out.start()
            gather_in.send_first()

            _wait_direct(act_copies)
            acts = jnp.concatenate([act_all[j, 0:1, :] for j in range(N_DEV)], axis=0)
            acts_ref[...] = acts
            part[...] = jnp.zeros_like(part)
            for l in range(DEPTH):
                res = lax.dot_general(acts, wada_ref[l], (((1,), (0,)), ((), ())), preferred_element_type=F32,
                                      precision=lax.Precision.HIGHEST)
                for b in range(N_DEV):
                    part[b, l:l + 1, :] = res[b:b + 1, :]
            mod_recv[mine] = part[mine]
            mod_copies = _direct_exchange(lambda p: part.at[_index(p)], lambda m: mod_recv.at[_index(m)],
                                          m_send, m_recv)
            gather_in.send_second()
            gather_out.send_mine()

            _wait_direct(mod_copies)
            for l in range(DEPTH):
                for j in range(N_DEV):
                    sl = slice(j * cols, (j + 1) * cols)
                    mod_ref[l:l + 1, sl] = mod_recv[j, l:l + 1, :] + bada_ref[l:l + 1, sl]
            shift = mod_ref[layer:layer + 1, 0:D_MODEL]
            scale = mod_ref[layer:layer + 1, D_MODEL:2 * D_MODEL]
            for t in range(N_TILES):
                rows = pl.ds(t * TM, TM)
                xn, _ = _ln(x_ref[rows, :])
                h_buf[rows, :] = (xn * (1.0 + scale) + shift).astype(BF16)

            chip_of = lambda dev: 2 * dev[0] + dev[1]
            writes = []

            def project(n, dev):
                first = pl.multiple_of(chip_of(dev) * pair, pair)
                if n >= 2:
                    writes[n - 2].wait()
                proj_blk[n % 2] = _dot_nt(h_buf[...], wint_v[pl.ds(first, pair), :])
                cp = pltpu.make_async_copy(proj_blk.at[n % 2], proj_hbm.at[:, pl.ds(first, pair)], p_sems.at[n % 2])
                cp.start()
                writes.append(cp)

            gather_in.relay()
            own_in.wait()
            gather_in.wait_sibling()
            project(0, me)
            gather_in.pass_near()
            gather_in.wait_passed(ACROSS_X)
            project(1, _peer(ACROSS_X))
            gather_out.relay()
            for cp in next_own():
                cp.start()
            for g in next_gathers():
                g.send_mine()
            gather_in.wait_passed(ACROSS_Y)
            project(2, _peer(ACROSS_Y))
            gather_in.pass_far()
            gather_in.wait_passed(ACROSS_BOTH)
            project(3, _peer(ACROSS_BOTH))

            gather_out.pass_on()
            gather_out.wait_rest()
            own_out.wait()
            for cp in keeps():
                cp.start()
            writes[2].wait()
            writes[3].wait()
            tile_read(0).start()

        @pl.when(tile + 1 < N_TILES)
        def _():
            tile_read(tile + 1).start()

        @pl.when(tile == 1)
        def _():
            for g in next_gathers():
                g.relay()

        @pl.when(tile == N_TILES // 2)
        def _():
            for g in next_gathers():
                g.pass_near()

        tile_read(tile).wait()
        xt = x_ref[pl.ds(pl.multiple_of(tile * TM, TM), TM), :]
        gate = mod_ref[layer:layer + 1, 2 * D_MODEL:]
        proj = proj_tile[tile % 2]
        cat, cdf_ref[...] = _mix_forward(proj, halo_ref[...], tile, weights)
        halo_ref[...] = proj[TM - HALO:, 0:D_POOL]
        y = _dot(cat.astype(BF16), wout_v[...])
        y_ref[...] = y
        zn, _ = _ln(ALPHA * xt + gate * y)
        out_ref[...] = zn * lng_ref[layer:layer + 1, :] + lnb_ref[layer:layer + 1, :]

        @pl.when(tile == N_TILES - 1)
        def _():
            for g in next_gathers():
                g.pass_far()
            for g in next_gathers():
                g.wait_rest()
            for g in now_gathers() + next_gathers():
                g.wait_sends()
            for cp in keeps() + next_own():
                cp.wait()

    row = lambda w: pl.BlockSpec((TM, w), lambda i: (i, 0))
    gather_sems = pltpu.SemaphoreType.DMA((2, GATHER_SEMS))
    seven = pltpu.SemaphoreType.DMA((7,))
    return pl.pallas_call(
        body,
        name="layer_fwd_0",
        grid=(N_TILES,),
        in_specs=[_const_spec((SEQ, D_MODEL)), _const_spec((1, D_MODEL)), _const_spec(w_ada.shape),
                  _const_spec(b_ada.shape)] + [_const_spec(s) for s in SMALL_SPECS]
                 + [_const_spec((DEPTH, D_MODEL)), _const_spec((DEPTH, D_MODEL))] + [ANY] * 4,
        out_specs=[row(D_MODEL), row(D_MODEL), row(2 * D_SGU), _const_spec((N_DEV, D_MODEL)),
                   _const_spec((DEPTH, 3 * D_MODEL))] + [ANY] * 5,
        out_shape=[jax.ShapeDtypeStruct((SEQ, D_MODEL), F32), jax.ShapeDtypeStruct((SEQ, D_MODEL), F32),
                   jax.ShapeDtypeStruct((SEQ, 2 * D_SGU), F32), jax.ShapeDtypeStruct((N_DEV, D_MODEL), F32),
                   jax.ShapeDtypeStruct((DEPTH, 3 * D_MODEL), F32), jax.ShapeDtypeStruct((SEQ, D_IN), F32),
                   jax.ShapeDtypeStruct((D_IN, D_MODEL), BF16), jax.ShapeDtypeStruct((D_MODEL, D_MODEL), BF16)]
                  + [jax.ShapeDtypeStruct((N_DEV,) + blk.shape, blk.dtype) for blk in mine_next],
        scratch_shapes=[pltpu.VMEM((D_IN, D_MODEL), BF16), pltpu.VMEM((D_MODEL, D_MODEL), BF16),
                        pltpu.VMEM((SEQ, D_MODEL), BF16), pltpu.VMEM((2, SEQ, pair), F32),
                        pltpu.VMEM((2, TM, D_IN), F32), pltpu.VMEM((HALO, D_POOL), F32),
                        pltpu.VMEM((N_DEV, 8, D_MODEL), F32),
                        pltpu.VMEM((8, D_MODEL), F32), pltpu.VMEM((N_DEV, 8, cols), F32),
                        pltpu.VMEM((N_DEV, 8, cols), F32),
                        gather_sems, gather_sems, pltpu.SemaphoreType.DMA((4,)), pltpu.SemaphoreType.DMA((2,)),
                        pltpu.SemaphoreType.DMA((2,)), seven, seven, seven, seven,
                        gather_sems, gather_sems, pltpu.SemaphoreType.DMA((2,))],
        compiler_params=pltpu.CompilerParams(dimension_semantics=("arbitrary",), vmem_limit_bytes=VMEM_LIMIT),
    )(x, c_row, w_ada, b_ada, *small, ln_g, ln_b, *mine_now, *mine_next)


def _grad_tail(dproj, h, cat, dy, small, dmod8, loss_lanes):
    shard_in, shard_out, shard_small = D_IN // N_DEV, D_MODEL // N_DEV, small.shape[2]
    W_IN, W_OUT, SMALL = 0, 1, 2

    def body(dproj_hbm, h_hbm, cat_hbm, dy_hbm, small_hbm, dmod_ref, lanes_ref,
             gwin_ref, gwout_ref, stot_ref, dmod_all, loss_ref,
             dproj_v, h_v, cat_v, dy_v, part_in, part_out, own_small, loss_src, loss_all, *rest):
        bufs, rest = rest[:13], rest[13:]
        load_sems, rs_sems = rest[0], rest[1:6]
        m_send, m_recv, g_send, g_recv, s_send, s_recv = rest[6:]
        mine = _index(_me())

        loads = [pltpu.make_async_copy(s, d, load_sems.at[n]) for n, (s, d) in enumerate(
            ((cat_hbm, cat_v), (dy_hbm, dy_v), (dproj_hbm, dproj_v), (h_hbm, h_v)))]
        for cp in loads:
            cp.start()
        arrays = [dict(part=part_in, out=gwin_ref, staged=False, sib=bufs[0], snd=bufs[1], rcv=bufs[2], relay=bufs[3]),
                  dict(part=part_out, out=gwout_ref, staged=False, sib=bufs[4], snd=bufs[5], rcv=bufs[6],
                       relay=bufs[7]),
                  dict(part=small_hbm, out=own_small, staged=True, stage=bufs[8], sib=bufs[9], snd=bufs[10],
                       rcv=bufs[11], relay=bufs[12])]
        scatter = _ChipReduceScatter(arrays, *rs_sems)
        scatter.start([SMALL])
        dmod_all[mine] = dmod_ref[...]
        dmod_copies = _direct_exchange(lambda p: dmod_ref, lambda m: dmod_all.at[_index(m)], m_send, m_recv)
        loss_src[...] = jnp.full(loss_src.shape, (0.5 / D_MODEL) * jnp.sum(lanes_ref[...]), F32)
        loss_all[mine] = loss_src[...]
        loss_copies = _direct_exchange(lambda p: loss_src, lambda m: loss_all.at[_index(m)], s_send, s_recv)

        loads[0].wait()
        loads[1].wait()
        for blk in range(2):
            res = _dot_tn(cat_v[:, blk * 512:(blk + 1) * 512], dy_v[...]).astype(BF16)
            for s in range(4):
                part_out[2 * blk + s // 2, s % 2] = res[s * shard_out:(s + 1) * shard_out]
        scatter.start([W_OUT])
        scatter.exchange([SMALL])

        gather = _TwoLevelGather(stot_ref, g_send, g_recv)
        loads[2].wait()
        loads[3].wait()
        for chip in range(4):
            res = _dot_tn(dproj_v[:, chip * 2 * shard_in:(chip + 1) * 2 * shard_in], h_v[...]).astype(BF16)
            part_in[chip, 0] = res[:shard_in]
            part_in[chip, 1] = res[shard_in:]
            if chip == 0:
                scatter.exchange([W_OUT])
                scatter.fold([SMALL])
            if chip == 1:
                scatter.fold([W_OUT])
                scatter.finish([SMALL])
                stot_ref[mine] = own_small[...]
                gather.send_mine()
            if chip == 2:
                gather.relay()
        scatter.start([W_IN])
        scatter.finish([W_OUT])
        scatter.exchange([W_IN])
        gather.pass_on()
        scatter.fold([W_IN])
        gather.wait_rest()
        _wait_direct(dmod_copies)
        _wait_direct(loss_copies)
        total = loss_all[0]
        for j in range(1, N_DEV):
            total = total + loss_all[j]
        loss_ref[...] = total
        scatter.finish([W_IN])
        gather.wait_sends()
        scatter.wait_sends()

    buffers = _ChipReduceScatter.buffers
    comm_scratch = (buffers(shard_in, D_MODEL, BF16, staged=False) + buffers(shard_out, D_MODEL, BF16, staged=False)
                    + buffers(shard_small, 128, F32))
    comm_scratch += [pltpu.SemaphoreType.DMA((4,))] + _ChipReduceScatter.semaphores(3)
    comm_scratch += [pltpu.SemaphoreType.DMA((n,)) for n in (7, 7, GATHER_SEMS, GATHER_SEMS, 7, 7)]
    return pl.pallas_call(
        body,
        name="grad_tail",
        in_specs=[ANY] * 5 + [VMEM, VMEM],
        out_specs=[VMEM] * 5,
        out_shape=[jax.ShapeDtypeStruct((shard_in, D_MODEL), F32), jax.ShapeDtypeStruct((shard_out, D_MODEL), F32),
                   jax.ShapeDtypeStruct((N_DEV, shard_small, 128), F32),
                   jax.ShapeDtypeStruct((N_DEV,) + dmod8.shape, F32), jax.ShapeDtypeStruct((8, 128), F32)],
        scratch_shapes=[pltpu.VMEM(dproj.shape, BF16), pltpu.VMEM(h.shape, BF16), pltpu.VMEM(cat.shape, BF16),
                        pltpu.VMEM(dy.shape, BF16), pltpu.VMEM((4, 2, shard_in, D_MODEL), BF16),
                        pltpu.VMEM((4, 2, shard_out, D_MODEL), BF16), pltpu.VMEM((shard_small, 128), F32),
                        pltpu.VMEM((8, 128), F32), pltpu.VMEM((N_DEV, 8, 128), F32)] + comm_scratch,
        compiler_params=pltpu.CompilerParams(vmem_limit_bytes=VMEM_LIMIT),
    )(dproj, h, cat, dy, small, dmod8, loss_lanes)


SMALL_NAMES = ("w_pool", "w_sgu", "pool_scale", "sgu_ln_g", "sgu_ln_b", "b_sgu", "ln_g", "ln_b")
SMALL_ROWS = (512, 512, 4, 4, 4, 4, 8, 8)


def _adamw_small(g_packed, ws, ms, vs, name):
    n = len(SMALL_NAMES)

    def body(g_ref, *refs):
        w_refs, m_refs, v_refs = refs[:n], refs[n:2 * n], refs[2 * n:3 * n]
        outs = refs[3 * n:]

        def update(p, at, g):
            delta, new_m, new_v = _adamw_math(w_refs[p][at], g, m_refs[p][at], v_refs[p][at])
            outs[p][at] = g
            outs[n + p][at] = delta
            outs[2 * n + p][at] = new_m
            outs[3 * n + p][at] = new_v

        row = 0
        for p, r in enumerate(SMALL_ROWS):
            shape = ws[p].shape
            for layer in range(DEPTH):
                first = layer * PACK_ROWS + row
                if len(shape) == 4:
                    for k in range(shape[1]):
                        update(p, (layer, k), g_ref[first + k * shape[2]:first + (k + 1) * shape[2], :])
                elif len(shape) == 3:
                    update(p, (layer,), g_ref[first:first + r, :])
                else:
                    g = jnp.concatenate([g_ref[first + k:first + k + 1, :] for k in range(r)], axis=1)
                    update(p, (slice(layer, layer + 1), slice(None)), g)
            row += r

    res = pl.pallas_call(
        body,
        name=name,
        out_shape=[jax.ShapeDtypeStruct(w.shape, F32) for w in ws] * 4,
        compiler_params=pltpu.CompilerParams(vmem_limit_bytes=VMEM_LIMIT),
    )(g_packed, *ws, *ms, *vs)
    return res[:n], res[n:2 * n], res[2 * n:3 * n], res[3 * n:]


def kernel(x, c, w_ada, b_ada, w_in, w_pool, pool_scale, sgu_ln_g, sgu_ln_b, w_sgu, b_sgu, w_out, ln_g, ln_b, loss_target, m_w_ada, m_b_ada, m_w_in, m_w_pool, m_pool_scale, m_sgu_ln_g, m_sgu_ln_b, m_w_sgu, m_b_sgu, m_w_out, m_ln_g, m_ln_b, v_w_ada, v_b_ada, v_w_in, v_w_pool, v_pool_scale, v_sgu_ln_g, v_sgu_ln_b, v_w_sgu, v_b_sgu, v_w_out, v_ln_g, v_ln_b):
    mine = _index(_me())
    small_w = dict(w_pool=w_pool, w_sgu=w_sgu, pool_scale=pool_scale, sgu_ln_g=sgu_ln_g, sgu_ln_b=sgu_ln_b,
                   b_sgu=b_sgu, ln_g=ln_g, ln_b=ln_b)
    small_m = dict(w_pool=m_w_pool, w_sgu=m_w_sgu, pool_scale=m_pool_scale, sgu_ln_g=m_sgu_ln_g,
                   sgu_ln_b=m_sgu_ln_b, b_sgu=m_b_sgu, ln_g=m_ln_g, ln_b=m_ln_b)
    small_v = dict(w_pool=v_w_pool, w_sgu=v_w_sgu, pool_scale=v_pool_scale, sgu_ln_g=v_sgu_ln_g,
                   sgu_ln_b=v_sgu_ln_b, b_sgu=v_b_sgu, ln_g=v_ln_g, ln_b=v_ln_b)

    wint_loc = jnp.transpose(w_in, (0, 2, 1)).astype(BF16)
    wout_loc = w_out.astype(BF16)
    small = (w_pool, pool_scale, sgu_ln_g, sgu_ln_b, w_sgu, jnp.transpose(b_sgu, (0, 2, 1)))
    out, y, cdf, act_all, mod, proj, wint0, wout0, wint1, wout1 = _first_layer(
        x[0], c, w_ada, b_ada, [wint_loc[0], wout_loc[0]], [wint_loc[1], wout_loc[1]], small, ln_g, ln_b)
    w_int, w_outf = [wint0, wint1.reshape(D_IN, D_MODEL)], [wout0, wout1.reshape(D_MODEL, D_MODEL)]
    acts, cur = [(x[0], proj, y, cdf)], out

    for l in range(1, DEPTH):
        out, proj, y, cdf = _layer_forward(l, cur, mod, w_int[l], w_outf[l], small, ln_g, ln_b, f"layer_fwd_{l}")
        acts.append((cur, proj, y, cdf))
        cur = out

    shard_in, shard_out = D_IN // N_DEV, D_MODEL // N_DEV
    a, b = cur, loss_target[0]
    loss_lanes, carry, pending = None, (), []
    g_w_in_t, g_w_out = [None] * DEPTH, [None] * DEPTH
    for l in reversed(range(DEPTH)):
        dx, dproj, h, cat, dy, small_grads, dmod8, lanes, *shards = _layer_backward(
            l, a, b, *acts[l], mod, w_int[l], w_outf[l], small, ln_g, l == DEPTH - 1, f"layer_bwd_{l}",
            carry=carry, reduce=pending)
        if shards:
            g_w_in_t[l + 1], g_w_out[l + 1] = shards
        if l == DEPTH - 1:
            loss_lanes = lanes
        if l > 0:
            pending = [_grad_matmul(dproj, h, 640, f"grad_w_in_{l}").reshape(4, 2, shard_in, D_MODEL),
                       _grad_matmul(cat, dy, 512, f"grad_w_out_{l}").reshape(4, 2, shard_out, D_MODEL)]
        carry = (small_grads, dmod8)
        a = b = dx
    grad_x = a[None]

    g_w_in_t[0], g_w_out[0], small_tot, dmod_slots, loss_tile = _grad_tail(
        dproj, h, cat, dy, small_grads.reshape(4, 2, DEPTH * PACK_ROWS // N_DEV, 128), dmod8, loss_lanes)
    loss = loss_tile[0, 0]
    dmod_all = dmod_slots.reshape(N_DEV, DEPTH, 3 * D_MODEL)

    cols_ada = w_ada.shape[2]
    dmod_cols = jnp.transpose(lax.dynamic_slice_in_dim(dmod_all, mine * cols_ada, cols_ada, axis=2), (1, 0, 2))
    g_w_ada, d_w_ada, nm_w_ada, nv_w_ada = _adamw_ada(w_ada, m_w_ada, v_w_ada, jnp.transpose(act_all), dmod_cols,
                                                      "adamw_w_ada")
    g_b_ada, d_b_ada, nm_b_ada, nv_b_ada = _adamw_bias(b_ada, m_b_ada, v_b_ada, dmod_all, "adamw_b_ada")
    flat = lambda t: t.reshape(-1, t.shape[-1])
    to_t = lambda t: flat(jnp.transpose(t, (0, 2, 1)))
    from_t = lambda t: jnp.transpose(t.reshape(DEPTH, shard_in, D_MODEL), (0, 2, 1))
    g_w_in, d_w_in, nm_w_in, nv_w_in = [from_t(t) for t in _adamw(to_t(w_in), g_w_in_t, to_t(m_w_in), to_t(v_w_in),
                                                                  shard_in // 2, "adamw_w_in")]
    gwout, d_w_out, nm_w_out, nv_w_out = [t.reshape(w_out.shape) for t in _adamw(
        flat(w_out), g_w_out, flat(m_w_out), flat(v_w_out), shard_out, "adamw_w_out")]
    small_out = _adamw_small(small_tot.reshape(DEPTH * PACK_ROWS, 128), [small_w[n] for n in SMALL_NAMES],
                             [small_m[n] for n in SMALL_NAMES], [small_v[n] for n in SMALL_NAMES], "adamw_small")
    gs, ds, ms, vs = [dict(zip(SMALL_NAMES, group)) for group in small_out]

    def ordered(w_ada_, b_ada_, w_in_, small, w_out_):
        return (w_ada_, b_ada_, w_in_, small["w_pool"], small["pool_scale"], small["sgu_ln_g"], small["sgu_ln_b"],
                small["w_sgu"], small["b_sgu"], w_out_, small["ln_g"], small["ln_b"])

    return (loss, grad_x,
            *ordered(g_w_ada, g_b_ada, g_w_in, gs, gwout),
            *ordered(d_w_ada, d_b_ada, d_w_in, ds, d_w_out),
            *ordered(nm_w_ada, nm_b_ada, nm_w_in, ms, nm_w_out),
            *ordered(nv_w_ada, nv_b_ada, nv_w_in, vs, nv_w_out))
```

```python
import jax
import jax.numpy as jnp
from jax import lax
from jax.experimental import pallas as pl
from jax.experimental.pallas import tpu as pltpu

F32 = jnp.float32
BF16 = jnp.bfloat16

D_MODEL = 1024
SEQ = 2048
DEPTH = 2
D_POOL = 512
D_SGU = 512
D_IN = 2560
N_GROUPS = 4
GROUP = 128
N_HEADS = 4
HEAD = 128
CHUNK = 128
WINDOWS = (2, 4, 8, 16)
ALPHA = (2.0 * DEPTH) ** 0.25
LN_EPS = 1e-5
N_DEV = 8

ADAM_LR = 0.001
ADAM_B1 = 0.9
ADAM_B2 = 0.999
ADAM_EPS = 1e-08
ADAM_WD = 0.01
ADAM_STEP = 10

TM = 256
HALO = 16
N_TILES = SEQ // TM
VMEM_LIMIT = 60 * 1024 * 1024

ROW_WPOOL = 0
ROW_WSGU = 512
ROW_PSCALE = 1024
ROW_SLNG = 1028
ROW_SLNB = 1032
ROW_BSGU = 1036
ROW_LNG = 1040
ROW_LNB = 1048
PACK_ROWS = 1088
DMOD_COLS = DEPTH * 3 * D_MODEL // 8

SQRT_HALF = 0.7071067811865476
INV_SQRT_2PI = 0.3989422804014327


def _ln(x):
    mu = jnp.mean(x, axis=-1, keepdims=True)
    xc = x - mu
    var = jnp.mean(xc * xc, axis=-1, keepdims=True)
    rstd = lax.rsqrt(var + LN_EPS)
    return xc * rstd, rstd


def _ln_bwd(dxn, xn, rstd):
    m1 = jnp.mean(dxn, axis=-1, keepdims=True)
    m2 = jnp.mean(dxn * xn, axis=-1, keepdims=True)
    return rstd * (dxn - m1 - xn * m2)


def _normal_cdf(x):
    return 0.5 * (1.0 + lax.erf(x * SQRT_HALF))


def _gelu_parts(x, cdf, with_grad):
    if not with_grad:
        return x * cdf, None
    return x * cdf, cdf + x * (INV_SQRT_2PI * jnp.exp(-0.5 * x * x))


def _silu_parts(x):
    s = jax.nn.sigmoid(x)
    return x * s, s * (1.0 + x * (1.0 - s))


def _dot(a, b):
    return lax.dot_general(a, b, (((1,), (0,)), ((), ())), preferred_element_type=F32)


def _dot_nt(a, b):
    return lax.dot_general(a, b, (((1,), (1,)), ((), ())), preferred_element_type=F32)


def _dot_tn(a, b):
    return lax.dot_general(a, b, (((0,), (0,)), ((), ())), preferred_element_type=F32)


def _row_index(tile):
    return tile * TM + lax.broadcasted_iota(jnp.int32, (TM, 1), 0)


def _window_sums(ext, forward):
    n = TM + HALO
    cur = ext
    outs = []
    for g in range(N_GROUPS):
        step = 1 << g
        cur = cur + pltpu.roll(cur, step if forward else n - step, 0)
        rows = cur[HALO:, :GROUP] if forward else cur[:TM, :GROUP]
        outs.append(rows)
        cur = cur[:, GROUP:] if g + 1 < N_GROUPS else None
    return outs


def _inverse_counts(rows):
    return [1.0 / jnp.minimum(rows + 1, w).astype(F32) for w in WINDOWS]


def _tril_bf16(w):
    t = lax.broadcasted_iota(jnp.int32, (CHUNK, CHUNK), 0)
    s = lax.broadcasted_iota(jnp.int32, (CHUNK, CHUNK), 1)
    return jnp.where(t >= s, w, 0.0).astype(BF16)


class _MixWeights:
    def __init__(self, layer, wpool_ref, pscale_ref, slng_ref, slnb_ref, wsgu_ref, bsgut_ref):
        self.layer = layer
        self.wpool_ref, self.pscale_ref, self.slng_ref, self.slnb_ref = wpool_ref, pscale_ref, slng_ref, slnb_ref
        self.wsgu_ref, self.bsgut_ref = wsgu_ref, bsgut_ref

    def pool(self, g):
        return self.wpool_ref[self.layer, g].astype(BF16)

    def pool_scale(self, g):
        return self.pscale_ref[self.layer:self.layer + 1, g * GROUP:(g + 1) * GROUP]

    def ln_gain(self, h):
        return self.slng_ref[self.layer, h:h + 1, :]

    def ln_bias(self, h):
        return self.slnb_ref[self.layer, h:h + 1, :]

    def mix(self, h):
        return _tril_bf16(self.wsgu_ref[self.layer, h])

    def mix_bias(self, h):
        return self.bsgut_ref[self.layer, :, h:h + 1]


SMALL_SPECS = ((DEPTH, N_GROUPS, GROUP, GROUP), (DEPTH, D_POOL), (DEPTH, N_HEADS, HEAD), (DEPTH, N_HEADS, HEAD),
               (DEPTH, N_HEADS, CHUNK, CHUNK), (DEPTH, CHUNK, N_HEADS))


def _mix_forward(proj, halo, tile, w, cdf=None):
    keep = cdf is not None
    rows = _row_index(tile)
    inv_counts = _inverse_counts(rows)
    xa = proj[:, 0:D_POOL]
    ga = proj[:, D_POOL:2 * D_POOL]
    sums = _window_sums(jnp.concatenate([halo, xa], axis=0), True)
    ga_act, ga_grad = _silu_parts(ga)
    pooled, pw, ya = [], [], []
    for g in range(N_GROUPS):
        sl = slice(g * GROUP, (g + 1) * GROUP)
        p = (sums[g] * inv_counts[g] - xa[:, sl]).astype(BF16)
        q = _dot(p, w.pool(g))
        pooled.append(p)
        pw.append(q)
        ya.append(q * w.pool_scale(g) * ga_act[:, sl])

    u = proj[:, 2 * D_POOL:2 * D_POOL + D_SGU]
    v = proj[:, 2 * D_POOL + D_SGU:2 * D_POOL + 2 * D_SGU]
    gb = proj[:, 2 * D_POOL + 2 * D_SGU:]
    gb_act, gb_grad = _silu_parts(gb)
    if cdf is None:
        cdf = jnp.concatenate([_normal_cdf(u), _normal_cdf(v)], axis=1)
    u_act, u_grad = _gelu_parts(u, cdf[:, :D_SGU], keep)
    v_act, v_grad = _gelu_parts(v, cdf[:, D_SGU:], keep)
    vn, vrstd, vln, mixed, yb = [], [], [], [], []
    for h in range(N_HEADS):
        sl = slice(h * HEAD, (h + 1) * HEAD)
        n_h, r_h = _ln(v_act[:, sl])
        l_h = (n_h * w.ln_gain(h) + w.ln_bias(h)).astype(BF16)
        w_h = w.mix(h)
        bias = w.mix_bias(h)
        m_h = jnp.concatenate(
            [_dot(w_h, l_h[k * CHUNK:(k + 1) * CHUNK]) + bias for k in range(TM // CHUNK)], axis=0)
        vn.append(n_h)
        vrstd.append(r_h)
        vln.append(l_h)
        mixed.append(m_h)
        yb.append(u_act[:, sl] * m_h * gb_act[:, sl])
    cat = jnp.concatenate(ya + yb, axis=1)
    if not keep:
        return cat, cdf
    return cat, dict(inv_counts=inv_counts, ga_act=ga_act, ga_grad=ga_grad, pooled=pooled, pw=pw, u_grad=u_grad,
                     v_grad=v_grad, u_act=u_act, gb_act=gb_act, gb_grad=gb_grad, vn=vn, vrstd=vrstd, vln=vln,
                     mixed=mixed)


def _const_spec(shape):
    nd = len(shape)
    return pl.BlockSpec(shape, lambda i: (0,) * nd)


def _layer_forward(layer, x, mod, w_int, w_outf, small, ln_g, ln_b, name):
    def body(x_ref, mod_ref, wint_ref, wout_ref, wpool_ref, pscale_ref, slng_ref, slnb_ref, wsgu_ref, bsgut_ref,
             lng_ref, lnb_ref, out_ref, proj_ref, y_ref, cdf_ref, halo_ref):
        weights = _MixWeights(layer, wpool_ref, pscale_ref, slng_ref, slnb_ref, wsgu_ref, bsgut_ref)
        tile = pl.program_id(0)

        @pl.when(tile == 0)
        def _():
            halo_ref[...] = jnp.zeros_like(halo_ref)

        xt = x_ref[...]
        shift = mod_ref[layer:layer + 1, 0:D_MODEL]
        scale = mod_ref[layer:layer + 1, D_MODEL:2 * D_MODEL]
        gate = mod_ref[layer:layer + 1, 2 * D_MODEL:]
        xn, _ = _ln(xt)
        h = (xn * (1.0 + scale) + shift).astype(BF16)
        proj = _dot_nt(h, wint_ref[...])
        proj_ref[...] = proj
        cat, cdf_ref[...] = _mix_forward(proj, halo_ref[...], tile, weights)
        halo_ref[...] = proj[TM - HALO:, 0:D_POOL]
        y = _dot(cat.astype(BF16), wout_ref[...])
        y_ref[...] = y
        zn, _ = _ln(ALPHA * xt + gate * y)
        out_ref[...] = zn * lng_ref[layer:layer + 1, :] + lnb_ref[layer:layer + 1, :]

    row = lambda w: pl.BlockSpec((TM, w), lambda i: (i, 0))
    return pl.pallas_call(
        body,
        name=name,
        grid=(N_TILES,),
        in_specs=[row(D_MODEL), _const_spec((DEPTH, 3 * D_MODEL)), _const_spec((D_IN, D_MODEL)),
                  _const_spec((D_MODEL, D_MODEL))] + [_const_spec(s) for s in SMALL_SPECS]
                 + [_const_spec((DEPTH, D_MODEL)), _const_spec((DEPTH, D_MODEL))],
        out_specs=[row(D_MODEL), row(D_IN), row(D_MODEL), row(2 * D_SGU)],
        out_shape=[jax.ShapeDtypeStruct((SEQ, D_MODEL), F32), jax.ShapeDtypeStruct((SEQ, D_IN), F32),
                   jax.ShapeDtypeStruct((SEQ, D_MODEL), F32), jax.ShapeDtypeStruct((SEQ, 2 * D_SGU), F32)],
        scratch_shapes=[pltpu.VMEM((HALO, D_POOL), F32)],
        compiler_params=pltpu.CompilerParams(dimension_semantics=("arbitrary",), vmem_limit_bytes=VMEM_LIMIT),
    )(x, mod, w_int, w_outf, *small, ln_g, ln_b)


VEC_LNG, VEC_LNB, VEC_POOL, VEC_SGU, VEC_SHIFT, VEC_SCALE, VEC_GATE, VEC_LOSS = range(8)


def _layer_backward(layer, a, b, x, proj, y, cdf, mod, w_int, w_outf, small, ln_g, is_last, name, carry=(),
                    reduce=()):
    n_red, n_carry = len(reduce), len(carry)
    base = layer * PACK_ROWS

    def body(a_ref, b_ref, x_ref, proj_ref, prev_ref, y_ref, cdf_ref, mod_ref, wint_ref, wout_ref, wpool_ref,
             pscale_ref, slng_ref, slnb_ref, wsgu_ref, bsgut_ref, lng_ref, *rest):
        weights = _MixWeights(layer, wpool_ref, pscale_ref, slng_ref, slnb_ref, wsgu_ref, bsgut_ref)
        carry_refs, rest = rest[:n_carry], rest[n_carry:]
        part_refs, rest = rest[:n_red], rest[n_red:]
        dx_ref, dproj_ref, h_ref, cat_ref, dy_ref, small_ref, dmod_ref, loss_ref = rest[:8]
        shard_refs, rest = rest[8:8 + n_red], rest[8 + n_red:]
        vec_ref, dmix_ref, halo_ref = rest[:3]
        step = pl.program_id(0)
        tile = N_TILES - 1 - step

        def scatter():
            bufs, sems = rest[3:3 + 5 * n_red], rest[3 + 5 * n_red:]
            arrays = [dict(part=part_refs[n], out=shard_refs[n], staged=True, stage=bufs[5 * n], sib=bufs[5 * n + 1],
                           snd=bufs[5 * n + 2], rcv=bufs[5 * n + 3], relay=bufs[5 * n + 4]) for n in range(n_red)]
            return _ChipReduceScatter(arrays, *sems)

        @pl.when(step == 0)
        def _():
            small_ref[...] = jnp.zeros_like(small_ref)
            dmod_ref[...] = jnp.zeros_like(dmod_ref)
            vec_ref[...] = jnp.zeros_like(vec_ref)
            dmix_ref[...] = jnp.zeros_like(dmix_ref)
            halo_ref[...] = jnp.zeros_like(halo_ref)
            if n_red:
                scatter().start()

        if n_red:
            @pl.when(step == 1)
            def _():
                scatter().exchange()

            @pl.when(step == N_TILES // 2)
            def _():
                scatter().fold()

        def acc(row, lo, val):
            hi = lo + val.shape[1]
            vec_ref[row:row + 1, lo:hi] += jnp.sum(val, axis=0, keepdims=True)

        xt = x_ref[...]
        yt = y_ref[...]
        shift = mod_ref[layer:layer + 1, 0:D_MODEL]
        scale = mod_ref[layer:layer + 1, D_MODEL:2 * D_MODEL]
        gate = mod_ref[layer:layer + 1, 2 * D_MODEL:]
        ln_gain = lng_ref[layer:layer + 1, :]

        zn, zrstd = _ln(ALPHA * xt + gate * yt)
        if is_last:
            diff = a_ref[...] - b_ref[...]
            acc(VEC_LOSS, 0, diff * diff)
            dout = diff * (1.0 / D_MODEL)
        else:
            dout = a_ref[...]
        acc(VEC_LNG, 0, dout * zn)
        acc(VEC_LNB, 0, dout)
        dz = _ln_bwd(dout * ln_gain, zn, zrstd)
        acc(VEC_GATE, 0, dz * yt)
        dy = (dz * gate).astype(BF16)
        dy_ref[...] = dy
        dcat = _dot_nt(dy, wout_ref[...])

        proj = proj_ref[...]
        prev = jnp.where(tile > 0, prev_ref[...], 0.0)
        cat, k = _mix_forward(proj, prev, tile, weights, cdf_ref[...])
        cat_ref[...] = cat.astype(BF16)

        dga, dq = [], []
        for g in range(N_GROUPS):
            sl = slice(g * GROUP, (g + 1) * GROUP)
            pscale = weights.pool_scale(g)
            dya = dcat[:, sl]
            dyp = dya * k["ga_act"][:, sl]
            dga.append(dya * k["pw"][g] * pscale * k["ga_grad"][:, sl])
            acc(VEC_POOL, g * GROUP, dyp * k["pw"][g])
            dpw = (dyp * pscale).astype(BF16)
            rows = pl.ds(base + ROW_WPOOL + g * GROUP, GROUP)
            small_ref[rows, :] += _dot_tn(k["pooled"][g], dpw)
            dq.append(_dot_nt(dpw, weights.pool(g)))
        dpooled = jnp.concatenate(dq, axis=1)
        scaled = jnp.concatenate([dq[g] * k["inv_counts"][g] for g in range(N_GROUPS)], axis=1)
        sums = _window_sums(jnp.concatenate([scaled, halo_ref[...]], axis=0), False)
        halo_ref[...] = scaled[0:HALO]
        dxa = jnp.concatenate(sums, axis=1) - dpooled

        du, dv, dgb = [], [], []
        for h in range(N_HEADS):
            sl = slice(h * HEAD, (h + 1) * HEAD)
            dyb = dcat[:, D_POOL + h * HEAD:D_POOL + (h + 1) * HEAD]
            m_h = k["mixed"][h]
            ug = k["u_act"][:, sl] * dyb
            du.append(dyb * m_h * k["gb_act"][:, sl] * k["u_grad"][:, sl])
            dgb.append(ug * m_h * k["gb_grad"][:, sl])
            dmixed = ug * k["gb_act"][:, sl]
            dmixed_bf = dmixed.astype(BF16)
            w_h = weights.mix(h)
            dvln_parts = []
            dmix_sum = dmix_ref[h]
            wsgu_rows = pl.ds(base + ROW_WSGU + h * CHUNK, CHUNK)
            dws = small_ref[wsgu_rows, :]
            for c in range(TM // CHUNK):
                cs = slice(c * CHUNK, (c + 1) * CHUNK)
                dmix_sum = dmix_sum + dmixed[cs]
                dws = dws + _dot_nt(dmixed_bf[cs], k["vln"][h][cs])
                dvln_parts.append(_dot_tn(w_h, dmixed_bf[cs]))
            dmix_ref[h] = dmix_sum
            small_ref[wsgu_rows, :] = dws
            dvln = jnp.concatenate(dvln_parts, axis=0)
            acc(VEC_SGU, h * HEAD, dvln * k["vn"][h])
            acc(VEC_SGU, D_SGU + h * HEAD, dvln)
            dvv = _ln_bwd(dvln * weights.ln_gain(h), k["vn"][h], k["vrstd"][h])
            dv.append(dvv * k["v_grad"][:, sl])

        dproj = jnp.concatenate([dxa] + dga + du + dv + dgb, axis=1).astype(BF16)
        dproj_ref[...] = dproj
        dh = _dot(dproj, wint_ref[...])

        xn, xrstd = _ln(xt)
        h_ref[...] = (xn * (1.0 + scale) + shift).astype(BF16)
        acc(VEC_SCALE, 0, dh * xn)
        acc(VEC_SHIFT, 0, dh)
        dx_ref[...] = _ln_bwd(dh * (1.0 + scale), xn, xrstd) + ALPHA * dz

        @pl.when(step == N_TILES - 1)
        def _():
            def put(row0, vec_row, lo, n):
                for r in range(n):
                    small_ref[base + row0 + r:base + row0 + r + 1, :] = (
                        vec_ref[vec_row:vec_row + 1, lo + r * 128:lo + (r + 1) * 128])

            put(ROW_PSCALE, VEC_POOL, 0, 4)
            put(ROW_SLNG, VEC_SGU, 0, 4)
            put(ROW_SLNB, VEC_SGU, D_SGU, 4)
            put(ROW_LNG, VEC_LNG, 0, 8)
            put(ROW_LNB, VEC_LNB, 0, 8)
            ones = jnp.ones((8, HEAD), F32)
            t = lax.broadcasted_iota(jnp.int32, (CHUNK, CHUNK), 0)
            s = lax.broadcasted_iota(jnp.int32, (CHUNK, CHUNK), 1)
            for h in range(N_HEADS):
                bias_rows = lax.dot_general(ones, dmix_ref[h], (((1,), (1,)), ((), ())),
                                            preferred_element_type=F32, precision=lax.Precision.HIGHEST)
                small_ref[base + ROW_BSGU + h:base + ROW_BSGU + h + 1, :] = bias_rows[0:1]
                rows = pl.ds(base + ROW_WSGU + h * CHUNK, CHUNK)
                small_ref[rows, :] = jnp.where(t >= s, small_ref[rows, :], 0.0)
            pieces = ((0, VEC_SHIFT, 0, 768),
                      (1, VEC_SHIFT, 768, 256), (1, VEC_SCALE, 0, 512),
                      (2, VEC_SCALE, 512, 512), (2, VEC_GATE, 0, 256),
                      (3, VEC_GATE, 256, 768))
            filled = [0] * 4
            for q, vec_row, lo, n in pieces:
                row = 4 * layer + q
                dmod_ref[row:row + 1, filled[q]:filled[q] + n] = vec_ref[vec_row:vec_row + 1, lo:lo + n]
                filled[q] += n
            if n_carry:
                for other in range(layer + 1, DEPTH):
                    rows = pl.ds(other * PACK_ROWS, PACK_ROWS)
                    small_ref[rows, :] = carry_refs[0][rows, :]
                    dmod_ref[4 * other:4 * other + 4, :] = carry_refs[1][4 * other:4 * other + 4, :]
            loss_ref[...] = vec_ref[VEC_LOSS:VEC_LOSS + 1, :]
            if n_red:
                scatter().finish()
                scatter().wait_sends()

    rev = lambda w: pl.BlockSpec((TM, w), lambda i: (N_TILES - 1 - i, 0))
    prev_spec = pl.BlockSpec(
        (HALO, D_POOL), lambda i: (jnp.maximum((N_TILES - 1 - i) * (TM // HALO) - 1, 0), 0))
    comm_scratch = []
    for p in reduce:
        comm_scratch += _ChipReduceScatter.buffers(p.shape[2], p.shape[3], p.dtype)
    if n_red:
        comm_scratch += _ChipReduceScatter.semaphores(n_red)
    return pl.pallas_call(
        body,
        name=name,
        grid=(N_TILES,),
        in_specs=[rev(D_MODEL), rev(D_MODEL) if is_last else pl.BlockSpec((TM, D_MODEL), lambda i: (0, 0)),
                  rev(D_MODEL), rev(D_IN), prev_spec, rev(D_MODEL), rev(2 * D_SGU),
                  _const_spec((DEPTH, 3 * D_MODEL)), _const_spec((D_IN, D_MODEL)), _const_spec((D_MODEL, D_MODEL))]
                 + [_const_spec(s) for s in SMALL_SPECS] + [_const_spec((DEPTH, D_MODEL))]
                 + [_const_spec(c.shape) for c in carry] + [ANY] * n_red,
        out_specs=[rev(D_MODEL), rev(D_IN), rev(D_MODEL), rev(D_MODEL), rev(D_MODEL),
                   _const_spec((DEPTH * PACK_ROWS, 128)), _const_spec((8, DMOD_COLS)), _const_spec((1, D_MODEL))]
                  + [_const_spec(p.shape[2:]) for p in reduce],
        out_shape=[jax.ShapeDtypeStruct((SEQ, D_MODEL), F32), jax.ShapeDtypeStruct((SEQ, D_IN), BF16),
                   jax.ShapeDtypeStruct((SEQ, D_MODEL), BF16), jax.ShapeDtypeStruct((SEQ, D_MODEL), BF16),
                   jax.ShapeDtypeStruct((SEQ, D_MODEL), BF16), jax.ShapeDtypeStruct((DEPTH * PACK_ROWS, 128), F32),
                   jax.ShapeDtypeStruct((8, DMOD_COLS), F32), jax.ShapeDtypeStruct((1, D_MODEL), F32)]
                  + [jax.ShapeDtypeStruct(p.shape[2:], F32) for p in reduce],
        scratch_shapes=[pltpu.VMEM((8, D_MODEL), F32), pltpu.VMEM((N_HEADS, CHUNK, HEAD), F32),
                        pltpu.VMEM((HALO, D_POOL), F32)] + comm_scratch,
        compiler_params=pltpu.CompilerParams(dimension_semantics=("arbitrary",), vmem_limit_bytes=VMEM_LIMIT),
    )(a, b, x, proj, proj, y, cdf, mod, w_int, w_outf, *small, ln_g, *carry, *reduce)


def _grad_matmul(lhs, rhs, block_cols, name):
    m, n = lhs.shape[1], rhs.shape[1]

    def body(lhs_ref, rhs_ref, out_ref):
        out_ref[...] = _dot_tn(lhs_ref[...], rhs_ref[...]).astype(BF16)

    return pl.pallas_call(
        body,
        name=name,
        grid=(m // block_cols,),
        in_specs=[pl.BlockSpec((SEQ, block_cols), lambda j: (0, j)), pl.BlockSpec((SEQ, n), lambda j: (0, 0))],
        out_specs=pl.BlockSpec((block_cols, n), lambda j: (j, 0)),
        out_shape=jax.ShapeDtypeStruct((m, n), BF16),
        compiler_params=pltpu.CompilerParams(dimension_semantics=("arbitrary",), vmem_limit_bytes=VMEM_LIMIT),
    )(lhs, rhs)


def _adamw_math(w, g, m, v):
    m = ADAM_B1 * m + (1.0 - ADAM_B1) * g
    v = ADAM_B2 * v + (1.0 - ADAM_B2) * (g * g)
    m_hat = m / (1.0 - ADAM_B1 ** ADAM_STEP)
    v_hat = v / (1.0 - ADAM_B2 ** ADAM_STEP)
    delta = -ADAM_LR * (m_hat / (jnp.sqrt(v_hat) + ADAM_EPS) + ADAM_WD * w)
    return delta, m, v


def _adamw(w, grads, m, v, block_rows, name):
    rows, cols = grads[0].shape
    blocks = rows // block_rows

    def body(w_ref, m_ref, v_ref, *rest):
        g_refs, (g_ref, d_ref, nm_ref, nv_ref) = rest[:DEPTH], rest[DEPTH:]
        for layer in range(DEPTH):
            @pl.when(pl.program_id(0) == layer)
            def _():
                g = g_refs[layer][...]
                g_ref[...] = g
                d_ref[...], nm_ref[...], nv_ref[...] = _adamw_math(w_ref[...], g, m_ref[...], v_ref[...])

    def grad_spec(layer):
        return pl.BlockSpec((block_rows, cols),
                            lambda l, i: (jnp.where(l == layer, i, jnp.where(l < layer, 0, blocks - 1)), 0))

    spec = pl.BlockSpec((block_rows, cols), lambda l, i: (l * blocks + i, 0))
    return pl.pallas_call(
        body,
        name=name,
        grid=(DEPTH, blocks),
        in_specs=[spec] * 3 + [grad_spec(layer) for layer in range(DEPTH)],
        out_specs=[spec] * 4,
        out_shape=[jax.ShapeDtypeStruct(w.shape, F32)] * 4,
        compiler_params=pltpu.CompilerParams(dimension_semantics=("arbitrary", "arbitrary"),
                                             vmem_limit_bytes=VMEM_LIMIT),
    )(w, m, v, *grads)


def _adamw_ada(w, m, v, act_t, dmod_cols, name):
    cols = w.shape[2]

    rows = 256

    def body(w_ref, m_ref, v_ref, act_ref, dmod_ref, g_ref, d_ref, nm_ref, nv_ref):
        act = act_ref[...]
        dm = dmod_ref[0]
        g = act[:, 0:1] * dm[0:1, :]
        for b in range(1, N_DEV):
            g = g + act[:, b:b + 1] * dm[b:b + 1, :]
        g_ref[0] = g
        d_ref[0], nm_ref[0], nv_ref[0] = _adamw_math(w_ref[0], g, m_ref[0], v_ref[0])

    spec = pl.BlockSpec((1, rows, cols), lambda l, i: (l, i, 0))
    return pl.pallas_call(
        body,
        name=name,
        grid=(DEPTH, D_MODEL // rows),
        in_specs=[spec, spec, spec, pl.BlockSpec((rows, N_DEV), lambda l, i: (i, 0)),
                  pl.BlockSpec((1, N_DEV, cols), lambda l, i: (l, 0, 0))],
        out_specs=[spec] * 4,
        out_shape=[jax.ShapeDtypeStruct(w.shape, F32)] * 4,
        compiler_params=pltpu.CompilerParams(dimension_semantics=("arbitrary", "arbitrary"),
                                             vmem_limit_bytes=VMEM_LIMIT),
    )(w, m, v, act_t, dmod_cols)


def _adamw_bias(w, m, v, dmod_all, name):
    def body(w_ref, m_ref, v_ref, dmod_ref, g_ref, d_ref, nm_ref, nv_ref):
        g = dmod_ref[0]
        for b in range(1, N_DEV):
            g = g + dmod_ref[b]
        g_ref[...] = g
        d_ref[...], nm_ref[...], nv_ref[...] = _adamw_math(w_ref[...], g, m_ref[...], v_ref[...])

    return pl.pallas_call(
        body,
        name=name,
        out_shape=[jax.ShapeDtypeStruct(w.shape, F32)] * 4,
        compiler_params=pltpu.CompilerParams(vmem_limit_bytes=VMEM_LIMIT),
    )(w, m, v, dmod_all)


MESH = pl.DeviceIdType.MESH
SIBLING = 1
ANY = pl.BlockSpec(memory_space=pl.ANY)
VMEM = pl.BlockSpec(memory_space=pltpu.VMEM)


def _me():
    return lax.axis_index("x"), lax.axis_index("y"), lax.axis_index("c")


def _peer(r):
    x, y, c = _me()
    return (1 - x if r & 4 else x, 1 - y if r & 2 else y, 1 - c if r & 1 else c)


def _index(dev):
    return 4 * dev[0] + 2 * dev[1] + dev[2]


def _remote(src, dst, send_sem, recv_sem, dev):
    return pltpu.make_async_remote_copy(src_ref=src, dst_ref=dst, send_sem=send_sem, recv_sem=recv_sem,
                                        device_id=dev, device_id_type=MESH)


ACROSS_X, ACROSS_Y, ACROSS_BOTH = 4, 2, 6
GATHER_SEMS = 11


class _TwoLevelGather:
    def __init__(self, out, send_sems, recv_sems, src=None):
        self.out, self.send_sems, self.recv_sems, self.src = out, send_sems, recv_sems, src
        self.rows = (out.shape[0] // N_DEV) if len(out.shape) == 2 else out.shape[1]
        self.half = self.rows // 2

    def _slot(self, block):
        if len(self.out.shape) == 2:
            return self.out.at[pl.ds(pl.multiple_of(_index(block) * self.rows, self.rows), self.rows)]
        return self.out.at[_index(block)]

    def _copy(self, k, block, part, to, src=None):
        slot = self._slot(block)
        if part is not None:
            rows = pl.ds(part * self.half, self.half)
            slot = slot.at[rows]
            src = None if src is None else src.at[rows]
        return _remote(slot if src is None else src, slot, self.send_sems.at[k], self.recv_sems.at[k], to)

    def _mine(self):
        me = _me()
        src = self._slot(me) if self.src is None else self.src
        x, y = _peer(ACROSS_X), _peer(ACROSS_Y)
        return [self._copy(1, me, 0, x, src), self._copy(3, me, 1, y, src), self._copy(0, me, None, _peer(SIBLING), src),
                self._copy(2, me, 1, x, src), self._copy(4, me, 0, y, src)]

    def _relayed(self):
        return [self._copy(5, _peer(ACROSS_X), 0, _peer(ACROSS_Y)), self._copy(6, _peer(ACROSS_Y), 1, _peer(ACROSS_X))]

    def _passed(self):
        sib, far = _peer(SIBLING), _peer(ACROSS_BOTH)
        return [self._copy(7, _peer(ACROSS_X), None, sib), self._copy(8, _peer(ACROSS_Y), None, sib),
                self._copy(9, far, 0, sib), self._copy(10, far, 1, sib)]

    def _arrival(self, k, r, part):
        return self._copy(k, _peer(r), part, _me())

    def send_first(self):
        for cp in self._mine()[:3]:
            cp.start()

    def send_second(self):
        for cp in self._mine()[3:]:
            cp.start()

    def send_mine(self):
        self.send_first()
        self.send_second()

    def relay(self):
        relayed = self._relayed()
        self._arrival(1, ACROSS_X, 0).wait_recv()
        relayed[0].start()
        self._arrival(3, ACROSS_Y, 1).wait_recv()
        relayed[1].start()

    def pass_near(self):
        passed = self._passed()
        self._arrival(2, ACROSS_X, 1).wait_recv()
        passed[0].start()
        self._arrival(4, ACROSS_Y, 0).wait_recv()
        passed[1].start()

    def pass_far(self):
        passed = self._passed()
        self._arrival(5, ACROSS_BOTH, 0).wait_recv()
        passed[2].start()
        self._arrival(6, ACROSS_BOTH, 1).wait_recv()
        passed[3].start()

    def pass_on(self):
        self.pass_near()
        self.pass_far()

    def wait_sibling(self):
        self._arrival(0, SIBLING, None).wait_recv()

    def wait_passed(self, r):
        if r == ACROSS_BOTH:
            self._arrival(9, r ^ SIBLING, 0).wait_recv()
            self._arrival(10, r ^ SIBLING, 1).wait_recv()
        else:
            self._arrival(7 if r == ACROSS_X else 8, r ^ SIBLING, None).wait_recv()

    def wait_rest(self):
        self.wait_sibling()
        for r in (ACROSS_X, ACROSS_Y, ACROSS_BOTH):
            self.wait_passed(r)

    def wait_sends(self):
        for cp in self._mine() + self._relayed() + self._passed():
            cp.wait_send()


class _ChipReduceScatter:
    SLOTS = 6

    def __init__(self, arrays, l_sem, d_send, d_recv, i_send, i_recv):
        self.arrays = arrays
        self.l_sem, self.d_send, self.d_recv, self.i_send, self.i_recv = l_sem, d_send, d_recv, i_send, i_recv

    @staticmethod
    def buffers(rows, cols, dtype, staged=True):
        stage = [pltpu.VMEM((4, rows, cols), dtype)] if staged else []
        return stage + [pltpu.VMEM((4, rows, cols), dtype), pltpu.VMEM((3, rows, cols), dtype),
                        pltpu.VMEM((2, rows, cols), dtype), pltpu.VMEM((2, rows // 2, cols), dtype)]

    @classmethod
    def semaphores(cls, n):
        return [pltpu.SemaphoreType.DMA((n,)), pltpu.SemaphoreType.DMA((n, 4)), pltpu.SemaphoreType.DMA((n, 4)),
                pltpu.SemaphoreType.DMA((n, cls.SLOTS)), pltpu.SemaphoreType.DMA((n, cls.SLOTS))]

    def _pick(self, which):
        return list(enumerate(self.arrays)) if which is None else [(n, self.arrays[n]) for n in which]

    @staticmethod
    def _chip(r):
        dev = _me() if r is None else _peer(r)
        return 2 * dev[0] + dev[1]

    def _staging(self, which):
        c = _me()[2]
        return [pltpu.make_async_copy(a["part"].at[pl.ds(0, 4), c], a["stage"], self.l_sem.at[n])
                for n, a in self._pick(which) if a["staged"]]

    def _first(self, which, chip):
        other = 1 - _me()[2]
        return [_remote(a["part"].at[chip, other], a["sib"].at[chip], self.d_send.at[n, chip], self.d_recv.at[n, chip],
                        _peer(SIBLING)) for n, a in self._pick(which)]

    @staticmethod
    def _halves(a):
        half = a["rcv"].shape[1] // 2
        return pl.ds(0, half), pl.ds(half, half)

    def _hops(self, n, a):
        h0, h1 = self._halves(a)
        x, y = _peer(ACROSS_X), _peer(ACROSS_Y)
        snd, rcv, relay = a["snd"], a["rcv"], a["relay"]
        pairs = [(snd.at[2, h0], relay.at[0], x), (snd.at[2, h1], relay.at[1], y),
                 (snd.at[0, h0], rcv.at[0, h0], x), (snd.at[0, h1], rcv.at[0, h1], x),
                 (snd.at[1, h1], rcv.at[1, h1], y), (snd.at[1, h0], rcv.at[1, h0], y)]
        return [_remote(s, d, self.i_send.at[n, k], self.i_recv.at[n, k], to) for k, (s, d, to) in enumerate(pairs)]

    def _mine(self, a, chip, rows=None):
        src = a["stage"].at[chip] if a["staged"] else a["part"].at[chip, _me()[2]]
        mine, sib = (src[...], a["sib"][chip]) if rows is None else (src[rows, :], a["sib"][chip, rows, :])
        return mine.astype(F32) + sib.astype(F32)

    def start(self, which=None, chips=None):
        if chips is None:
            for cp in self._staging(which):
                cp.start()
        for chip in range(4) if chips is None else chips:
            for cp in self._first(which, chip):
                cp.start()

    def send_far(self, which=None):
        far = self._chip(ACROSS_BOTH)
        for cp in self._staging(which):
            cp.wait()
        for cp in self._first(which, far):
            cp.wait_recv()
        for n, a in self._pick(which):
            hops = self._hops(n, a)
            a["snd"][2] = self._mine(a, far).astype(a["snd"].dtype)
            hops[0].start()
            hops[1].start()

    def send_near(self, r, which=None):
        chip = self._chip(r)
        for cp in self._first(which, chip):
            cp.wait_recv()
        for n, a in self._pick(which):
            h0, h1 = self._halves(a)
            hops = self._hops(n, a)
            if r == ACROSS_X:
                a["snd"][0, h0, :] = self._mine(a, chip, h0).astype(a["snd"].dtype)
                hops[2].start()
            else:
                a["snd"][1, h1, :] = self._mine(a, chip, h1).astype(a["snd"].dtype)
                hops[4].start()

    def exchange(self, which=None):
        self.send_far(which)
        self.send_near(ACROSS_X, which)
        self.send_near(ACROSS_Y, which)

    def fold(self, which=None):
        across_x, across_y = self._chip(ACROSS_X), self._chip(ACROSS_Y)
        for n, a in self._pick(which):
            h0, h1 = self._halves(a)
            hops = self._hops(n, a)
            dtype = a["snd"].dtype
            hops[1].wait_recv()
            a["snd"][0, h1, :] = (self._mine(a, across_x, h1) + a["relay"][1].astype(F32)).astype(dtype)
            hops[3].start()
            hops[0].wait_recv()
            a["snd"][1, h0, :] = (self._mine(a, across_y, h0) + a["relay"][0].astype(F32)).astype(dtype)
            hops[5].start()

    def finish(self, which=None):
        home = self._chip(None)
        for cp in self._first(which, home):
            cp.wait_recv()
        for n, a in self._pick(which):
            hops = self._hops(n, a)
            a["out"][...] = self._mine(a, home)
            hops[2].wait_recv()
            hops[3].wait_recv()
            a["out"][...] += a["rcv"][0].astype(F32)
            hops[4].wait_recv()
            hops[5].wait_recv()
            a["out"][...] += a["rcv"][1].astype(F32)

    def wait_sends(self, which=None):
        for chip in range(4):
            for cp in self._first(which, chip):
                cp.wait_send()
        for n, a in self._pick(which):
            for cp in self._hops(n, a):
                cp.wait_send()


def _direct_exchange(src_of, dst_of, send_sems, recv_sems):
    me = _me()
    copies = [_remote(src_of(_peer(r)), dst_of(me), send_sems.at[r - 1], recv_sems.at[r - 1], _peer(r))
              for r in range(1, N_DEV)]
    for cp in copies:
        cp.start()
    return copies


def _wait_direct(copies):
    for cp in copies:
        cp.wait_recv()
    for cp in copies:
        cp.wait_send()


def _first_layer(x, c_row, w_ada, b_ada, mine_now, mine_next, small, ln_g, ln_b):
    layer = 0
    cols = w_ada.shape[2]
    shard = mine_now[0].shape[0]
    pair = 2 * shard

    def body(x_ref, c_ref, wada_ref, bada_ref, wpool_ref, pscale_ref, slng_ref, slnb_ref, wsgu_ref, bsgut_ref,
             lng_ref, lnb_ref, wint_hbm, wout_hbm, nxt_in_hbm, nxt_out_hbm,
             out_ref, y_ref, cdf_ref, acts_ref, mod_ref, proj_hbm, wint_keep, wout_keep, nxt_in_full, nxt_out_full,
             wint_v, wout_v, h_buf, proj_blk, proj_tile, halo_ref, act_all, act_src, part, mod_recv,
             w_send, w_recv, w_local, p_sems, t_sems, a_send, a_recv, m_send, m_recv, n_send, n_recv, n_local):
        weights = _MixWeights(layer, wpool_ref, pscale_ref, slng_ref, slnb_ref, wsgu_ref, bsgut_ref)
        tile = pl.program_id(0)

        def now_gathers():
            return (_TwoLevelGather(wint_v, w_send.at[0], w_recv.at[0], src=wint_hbm),
                    _TwoLevelGather(wout_v, w_send.at[1], w_recv.at[1], src=wout_hbm))

        def next_gathers():
            return (_TwoLevelGather(nxt_in_full, n_send.at[0], n_recv.at[0], src=nxt_in_hbm),
                    _TwoLevelGather(nxt_out_full, n_send.at[1], n_recv.at[1], src=nxt_out_hbm))

        def next_own():
            mine = _index(_me())
            return [pltpu.make_async_copy(nxt_in_hbm, nxt_in_full.at[mine], n_local.at[0]),
                    pltpu.make_async_copy(nxt_out_hbm, nxt_out_full.at[mine], n_local.at[1])]

        def keeps():
            return [pltpu.make_async_copy(wint_v, wint_keep, w_local.at[2]),
                    pltpu.make_async_copy(wout_v, wout_keep, w_local.at[3])]

        def tile_read(t):
            slot = t % 2
            return pltpu.make_async_copy(proj_hbm.at[pl.ds(pl.multiple_of(t * TM, TM), TM)], proj_tile.at[slot],
                                         t_sems.at[slot])

        @pl.when(tile == 0)
        def _():
            me = _me()
            mine = _index(me)
            halo_ref[...] = jnp.zeros_like(halo_ref)

            cval = c_ref[...]
            act_src[...] = jnp.zeros_like(act_src)
            act_src[0:1, :] = cval * jax.nn.sigmoid(cval)
            act_all[mine] = act_src[...]
            act_copies = _direct_exchange(lambda p: act_src, lambda m: act_all.at[_index(m)], a_send, a_recv)

            gather_in, gather_out = now_gathers()
            own_in = pltpu.make_async_copy(wint_hbm, gather_in._slot(me), w_local.at[0])
            own_out = pltpu.make_async_copy(wout_hbm, gather_out._slot(me), w_local.at[1])
            own_in.start()
            own_out.start()
            gather_in.send_first()

            _wait_direct(act_copies)
            acts = jnp.concatenate([act_all[j, 0:1, :] for j in range(N_DEV)], axis=0)
            acts_ref[...] = acts
            part[...] = jnp.zeros_like(part)
            for l in range(DEPTH):
                res = lax.dot_general(acts, wada_ref[l], (((1,), (0,)), ((), ())), preferred_element_type=F32,
                                      precision=lax.Precision.HIGHEST)
                for b in range(N_DEV):
                    part[b, l:l + 1, :] = res[b:b + 1, :]
            mod_recv[mine] = part[mine]
            mod_copies = _direct_exchange(lambda p: part.at[_index(p)], lambda m: mod_recv.at[_index(m)],
                                          m_send, m_recv)
            gather_in.send_second()
            gather_out.send_mine()

            _wait_direct(mod_copies)
            for l in range(DEPTH):
                for j in range(N_DEV):
                    sl = slice(j * cols, (j + 1) * cols)
                    mod_ref[l:l + 1, sl] = mod_recv[j, l:l + 1, :] + bada_ref[l:l + 1, sl]
            shift = mod_ref[layer:layer + 1, 0:D_MODEL]
            scale = mod_ref[layer:layer + 1, D_MODEL:2 * D_MODEL]
            for t in range(N_TILES):
                rows = pl.ds(t * TM, TM)
                xn, _ = _ln(x_ref[rows, :])
                h_buf[rows, :] = (xn * (1.0 + scale) + shift).astype(BF16)

            chip_of = lambda dev: 2 * dev[0] + dev[1]
            writes = []

            def project(n, dev):
                first = pl.multiple_of(chip_of(dev) * pair, pair)
                if n >= 2:
                    writes[n - 2].wait()
                proj_blk[n % 2] = _dot_nt(h_buf[...], wint_v[pl.ds(first, pair), :])
                cp = pltpu.make_async_copy(proj_blk.at[n % 2], proj_hbm.at[:, pl.ds(first, pair)], p_sems.at[n % 2])
                cp.start()
                writes.append(cp)

            gather_in.relay()
            own_in.wait()
            gather_in.wait_sibling()
            project(0, me)
            gather_in.pass_near()
            gather_in.wait_passed(ACROSS_X)
            project(1, _peer(ACROSS_X))
            gather_out.relay()
            for cp in next_own():
                cp.start()
            for g in next_gathers():
                g.send_mine()
            gather_in.wait_passed(ACROSS_Y)
            project(2, _peer(ACROSS_Y))
            gather_in.pass_far()
            gather_in.wait_passed(ACROSS_BOTH)
            project(3, _peer(ACROSS_BOTH))

            gather_out.pass_on()
            gather_out.wait_rest()
            own_out.wait()
            for cp in keeps():
                cp.start()
            writes[2].wait()
            writes[3].wait()
            tile_read(0).start()

        @pl.when(tile + 1 < N_TILES)
        def _():
            tile_read(tile + 1).start()

        @pl.when(tile == 1)
        def _():
            for g in next_gathers():
                g.relay()

        @pl.when(tile == N_TILES // 2)
        def _():
            for g in next_gathers():
                g.pass_near()

        tile_read(tile).wait()
        xt = x_ref[pl.ds(pl.multiple_of(tile * TM, TM), TM), :]
        gate = mod_ref[layer:layer + 1, 2 * D_MODEL:]
        proj = proj_tile[tile % 2]
        cat, cdf_ref[...] = _mix_forward(proj, halo_ref[...], tile, weights)
        halo_ref[...] = proj[TM - HALO:, 0:D_POOL]
        y = _dot(cat.astype(BF16), wout_v[...])
        y_ref[...] = y
        zn, _ = _ln(ALPHA * xt + gate * y)
        out_ref[...] = zn * lng_ref[layer:layer + 1, :] + lnb_ref[layer:layer + 1, :]

        @pl.when(tile == N_TILES - 1)
        def _():
            for g in next_gathers():
                g.pass_far()
            for g in next_gathers():
                g.wait_rest()
            for g in now_gathers() + next_gathers():
                g.wait_sends()
            for cp in keeps() + next_own():
                cp.wait()

    row = lambda w: pl.BlockSpec((TM, w), lambda i: (i, 0))
    gather_sems = pltpu.SemaphoreType.DMA((2, GATHER_SEMS))
    seven = pltpu.SemaphoreType.DMA((7,))
    return pl.pallas_call(
        body,
        name="layer_fwd_0",
        grid=(N_TILES,),
        in_specs=[_const_spec((SEQ, D_MODEL)), _const_spec((1, D_MODEL)), _const_spec(w_ada.shape),
                  _const_spec(b_ada.shape)] + [_const_spec(s) for s in SMALL_SPECS]
                 + [_const_spec((DEPTH, D_MODEL)), _const_spec((DEPTH, D_MODEL))] + [ANY] * 4,
        out_specs=[row(D_MODEL), row(D_MODEL), row(2 * D_SGU), _const_spec((N_DEV, D_MODEL)),
                   _const_spec((DEPTH, 3 * D_MODEL))] + [ANY] * 5,
        out_shape=[jax.ShapeDtypeStruct((SEQ, D_MODEL), F32), jax.ShapeDtypeStruct((SEQ, D_MODEL), F32),
                   jax.ShapeDtypeStruct((SEQ, 2 * D_SGU), F32), jax.ShapeDtypeStruct((N_DEV, D_MODEL), F32),
                   jax.ShapeDtypeStruct((DEPTH, 3 * D_MODEL), F32), jax.ShapeDtypeStruct((SEQ, D_IN), F32),
                   jax.ShapeDtypeStruct((D_IN, D_MODEL), BF16), jax.ShapeDtypeStruct((D_MODEL, D_MODEL), BF16)]
                  + [jax.ShapeDtypeStruct((N_DEV,) + blk.shape, blk.dtype) for blk in mine_next],
        scratch_shapes=[pltpu.VMEM((D_IN, D_MODEL), BF16), pltpu.VMEM((D_MODEL, D_MODEL), BF16),
                        pltpu.VMEM((SEQ, D_MODEL), BF16), pltpu.VMEM((2, SEQ, pair), F32),
                        pltpu.VMEM((2, TM, D_IN), F32), pltpu.VMEM((HALO, D_POOL), F32),
                        pltpu.VMEM((N_DEV, 8, D_MODEL), F32),
                        pltpu.VMEM((8, D_MODEL), F32), pltpu.VMEM((N_DEV, 8, cols), F32),
                        pltpu.VMEM((N_DEV, 8, cols), F32),
                        gather_sems, gather_sems, pltpu.SemaphoreType.DMA((4,)), pltpu.SemaphoreType.DMA((2,)),
                        pltpu.SemaphoreType.DMA((2,)), seven, seven, seven, seven,
                        gather_sems, gather_sems, pltpu.SemaphoreType.DMA((2,))],
        compiler_params=pltpu.CompilerParams(dimension_semantics=("arbitrary",), vmem_limit_bytes=VMEM_LIMIT),
    )(x, c_row, w_ada, b_ada, *small, ln_g, ln_b, *mine_now, *mine_next)


def _grad_tail(dproj, h, cat, dy, small, dmod8, loss_lanes):
    shard_in, shard_out, shard_small = D_IN // N_DEV, D_MODEL // N_DEV, small.shape[2]
    W_IN, W_OUT, SMALL = 0, 1, 2

    def body(dproj_hbm, h_hbm, cat_hbm, dy_hbm, small_hbm, dmod_ref, lanes_ref,
             gwin_ref, gwout_ref, stot_ref, dmod_all, loss_ref,
             dproj_v, h_v, cat_v, dy_v, part_in, part_out, own_small, loss_src, loss_all, *rest):
        bufs, rest = rest[:13], rest[13:]
        load_sems, rs_sems = rest[0], rest[1:6]
        m_send, m_recv, g_send, g_recv, s_send, s_recv = rest[6:]
        mine = _index(_me())

        order = (ACROSS_BOTH, ACROSS_X, ACROSS_Y, None)
        chips = [_ChipReduceScatter._chip(r) for r in order]
        loads = [pltpu.make_async_copy(s, d, load_sems.at[n]) for n, (s, d) in enumerate(
            ((cat_hbm, cat_v), (dy_hbm, dy_v), (h_hbm, h_v)))]
        loads += [pltpu.make_async_copy(dproj_hbm.at[:, pl.ds(pl.multiple_of(chip * 2 * shard_in, 2 * shard_in),
                                                             2 * shard_in)], dproj_v.at[n], load_sems.at[3 + n])
                  for n, chip in enumerate(chips)]
        for cp in loads:
            cp.start()
        arrays = [dict(part=part_in, out=gwin_ref, staged=False, sib=bufs[0], snd=bufs[1], rcv=bufs[2], relay=bufs[3]),
                  dict(part=part_out, out=gwout_ref, staged=False, sib=bufs[4], snd=bufs[5], rcv=bufs[6],
                       relay=bufs[7]),
                  dict(part=small_hbm, out=own_small, staged=True, stage=bufs[8], sib=bufs[9], snd=bufs[10],
                       rcv=bufs[11], relay=bufs[12])]
        scatter = _ChipReduceScatter(arrays, *rs_sems)
        scatter.start([SMALL])
        dmod_all[mine] = dmod_ref[...]
        dmod_copies = _direct_exchange(lambda p: dmod_ref, lambda m: dmod_all.at[_index(m)], m_send, m_recv)
        loss_src[...] = jnp.full(loss_src.shape, (0.5 / D_MODEL) * jnp.sum(lanes_ref[...]), F32)
        loss_all[mine] = loss_src[...]
        loss_copies = _direct_exchange(lambda p: loss_src, lambda m: loss_all.at[_index(m)], s_send, s_recv)

        loads[0].wait()
        loads[1].wait()
        for blk in range(2):
            res = _dot_tn(cat_v[:, blk * 512:(blk + 1) * 512], dy_v[...]).astype(BF16)
            for s in range(4):
                part_out[2 * blk + s // 2, s % 2] = res[s * shard_out:(s + 1) * shard_out]
        scatter.start([W_OUT])
        scatter.exchange([SMALL])

        gather = _TwoLevelGather(stot_ref, g_send, g_recv)
        loads[2].wait()
        for n, chip in enumerate(chips):
            loads[3 + n].wait()
            res = _dot_tn(dproj_v[n], h_v[...]).astype(BF16)
            part_in[chip, 0] = res[:shard_in]
            part_in[chip, 1] = res[shard_in:]
            scatter.start([W_IN], chips=[chip])
            if n == 0:
                scatter.exchange([W_OUT])
                scatter.fold([SMALL])
            if n == 1:
                scatter.send_far([W_IN])
                scatter.fold([W_OUT])
                scatter.finish([SMALL])
                stot_ref[mine] = own_small[...]
                gather.send_mine()
            if n == 2:
                scatter.send_near(ACROSS_X, [W_IN])
                gather.relay()
            if n == 3:
                scatter.send_near(ACROSS_Y, [W_IN])
        scatter.fold([W_IN])
        scatter.finish([W_OUT])
        gather.pass_on()
        gather.wait_rest()
        _wait_direct(dmod_copies)
        _wait_direct(loss_copies)
        total = loss_all[0]
        for j in range(1, N_DEV):
            total = total + loss_all[j]
        loss_ref[...] = total
        scatter.finish([W_IN])
        gather.wait_sends()
        scatter.wait_sends()

    buffers = _ChipReduceScatter.buffers
    comm_scratch = (buffers(shard_in, D_MODEL, BF16, staged=False) + buffers(shard_out, D_MODEL, BF16, staged=False)
                    + buffers(shard_small, 128, F32))
    comm_scratch += [pltpu.SemaphoreType.DMA((7,))] + _ChipReduceScatter.semaphores(3)
    comm_scratch += [pltpu.SemaphoreType.DMA((n,)) for n in (7, 7, GATHER_SEMS, GATHER_SEMS, 7, 7)]
    return pl.pallas_call(
        body,
        name="grad_tail",
        in_specs=[ANY] * 5 + [VMEM, VMEM],
        out_specs=[VMEM] * 5,
        out_shape=[jax.ShapeDtypeStruct((shard_in, D_MODEL), F32), jax.ShapeDtypeStruct((shard_out, D_MODEL), F32),
                   jax.ShapeDtypeStruct((N_DEV, shard_small, 128), F32),
                   jax.ShapeDtypeStruct((N_DEV,) + dmod8.shape, F32), jax.ShapeDtypeStruct((8, 128), F32)],
        scratch_shapes=[pltpu.VMEM((4, SEQ, 2 * shard_in), BF16), pltpu.VMEM(h.shape, BF16), pltpu.VMEM(cat.shape, BF16),
                        pltpu.VMEM(dy.shape, BF16), pltpu.VMEM((4, 2, shard_in, D_MODEL), BF16),
                        pltpu.VMEM((4, 2, shard_out, D_MODEL), BF16), pltpu.VMEM((shard_small, 128), F32),
                        pltpu.VMEM((8, 128), F32), pltpu.VMEM((N_DEV, 8, 128), F32)] + comm_scratch,
        compiler_params=pltpu.CompilerParams(vmem_limit_bytes=VMEM_LIMIT),
    )(dproj, h, cat, dy, small, dmod8, loss_lanes)


SMALL_NAMES = ("w_pool", "w_sgu", "pool_scale", "sgu_ln_g", "sgu_ln_b", "b_sgu", "ln_g", "ln_b")
SMALL_ROWS = (512, 512, 4, 4, 4, 4, 8, 8)


def _adamw_small(g_packed, ws, ms, vs, name):
    n = len(SMALL_NAMES)

    def body(g_ref, *refs):
        w_refs, m_refs, v_refs = refs[:n], refs[n:2 * n], refs[2 * n:3 * n]
        outs = refs[3 * n:]

        def update(p, at, g):
            delta, new_m, new_v = _adamw_math(w_refs[p][at], g, m_refs[p][at], v_refs[p][at])
            outs[p][at] = g
            outs[n + p][at] = delta
            outs[2 * n + p][at] = new_m
            outs[3 * n + p][at] = new_v

        row = 0
        for p, r in enumerate(SMALL_ROWS):
            shape = ws[p].shape
            for layer in range(DEPTH):
                first = layer * PACK_ROWS + row
                if len(shape) == 4:
                    for k in range(shape[1]):
                        update(p, (layer, k), g_ref[first + k * shape[2]:first + (k + 1) * shape[2], :])
                elif len(shape) == 3:
                    update(p, (layer,), g_ref[first:first + r, :])
                else:
                    g = jnp.concatenate([g_ref[first + k:first + k + 1, :] for k in range(r)], axis=1)
                    update(p, (slice(layer, layer + 1), slice(None)), g)
            row += r

    res = pl.pallas_call(
        body,
        name=name,
        out_shape=[jax.ShapeDtypeStruct(w.shape, F32) for w in ws] * 4,
        compiler_params=pltpu.CompilerParams(vmem_limit_bytes=VMEM_LIMIT),
    )(g_packed, *ws, *ms, *vs)
    return res[:n], res[n:2 * n], res[2 * n:3 * n], res[3 * n:]


def kernel(x, c, w_ada, b_ada, w_in, w_pool, pool_scale, sgu_ln_g, sgu_ln_b, w_sgu, b_sgu, w_out, ln_g, ln_b, loss_target, m_w_ada, m_b_ada, m_w_in, m_w_pool, m_pool_scale, m_sgu_ln_g, m_sgu_ln_b, m_w_sgu, m_b_sgu, m_w_out, m_ln_g, m_ln_b, v_w_ada, v_b_ada, v_w_in, v_w_pool, v_pool_scale, v_sgu_ln_g, v_sgu_ln_b, v_w_sgu, v_b_sgu, v_w_out, v_ln_g, v_ln_b):
    mine = _index(_me())
    small_w = dict(w_pool=w_pool, w_sgu=w_sgu, pool_scale=pool_scale, sgu_ln_g=sgu_ln_g, sgu_ln_b=sgu_ln_b,
                   b_sgu=b_sgu, ln_g=ln_g, ln_b=ln_b)
    small_m = dict(w_pool=m_w_pool, w_sgu=m_w_sgu, pool_scale=m_pool_scale, sgu_ln_g=m_sgu_ln_g,
                   sgu_ln_b=m_sgu_ln_b, b_sgu=m_b_sgu, ln_g=m_ln_g, ln_b=m_ln_b)
    small_v = dict(w_pool=v_w_pool, w_sgu=v_w_sgu, pool_scale=v_pool_scale, sgu_ln_g=v_sgu_ln_g,
                   sgu_ln_b=v_sgu_ln_b, b_sgu=v_b_sgu, ln_g=v_ln_g, ln_b=v_ln_b)

    wint_loc = jnp.transpose(w_in, (0, 2, 1)).astype(BF16)
    wout_loc = w_out.astype(BF16)
    small = (w_pool, pool_scale, sgu_ln_g, sgu_ln_b, w_sgu, jnp.transpose(b_sgu, (0, 2, 1)))
    out, y, cdf, act_all, mod, proj, wint0, wout0, wint1, wout1 = _first_layer(
        x[0], c, w_ada, b_ada, [wint_loc[0], wout_loc[0]], [wint_loc[1], wout_loc[1]], small, ln_g, ln_b)
    w_int, w_outf = [wint0, wint1.reshape(D_IN, D_MODEL)], [wout0, wout1.reshape(D_MODEL, D_MODEL)]
    acts, cur = [(x[0], proj, y, cdf)], out

    for l in range(1, DEPTH):
        out, proj, y, cdf = _layer_forward(l, cur, mod, w_int[l], w_outf[l], small, ln_g, ln_b, f"layer_fwd_{l}")
        acts.append((cur, proj, y, cdf))
        cur = out

    shard_in, shard_out = D_IN // N_DEV, D_MODEL // N_DEV
    a, b = cur, loss_target[0]
    loss_lanes, carry, pending = None, (), []
    g_w_in_t, g_w_out = [None] * DEPTH, [None] * DEPTH
    for l in reversed(range(DEPTH)):
        dx, dproj, h, cat, dy, small_grads, dmod8, lanes, *shards = _layer_backward(
            l, a, b, *acts[l], mod, w_int[l], w_outf[l], small, ln_g, l == DEPTH - 1, f"layer_bwd_{l}",
            carry=carry, reduce=pending)
        if shards:
            g_w_in_t[l + 1], g_w_out[l + 1] = shards
        if l == DEPTH - 1:
            loss_lanes = lanes
        if l > 0:
            pending = [_grad_matmul(dproj, h, 640, f"grad_w_in_{l}").reshape(4, 2, shard_in, D_MODEL),
                       _grad_matmul(cat, dy, 512, f"grad_w_out_{l}").reshape(4, 2, shard_out, D_MODEL)]
        carry = (small_grads, dmod8)
        a = b = dx
    grad_x = a[None]

    g_w_in_t[0], g_w_out[0], small_tot, dmod_slots, loss_tile = _grad_tail(
        dproj, h, cat, dy, small_grads.reshape(4, 2, DEPTH * PACK_ROWS // N_DEV, 128), dmod8, loss_lanes)
    loss = loss_tile[0, 0]
    dmod_all = dmod_slots.reshape(N_DEV, DEPTH, 3 * D_MODEL)

    cols_ada = w_ada.shape[2]
    dmod_cols = jnp.transpose(lax.dynamic_slice_in_dim(dmod_all, mine * cols_ada, cols_ada, axis=2), (1, 0, 2))
    g_w_ada, d_w_ada, nm_w_ada, nv_w_ada = _adamw_ada(w_ada, m_w_ada, v_w_ada, jnp.transpose(act_all), dmod_cols,
                                                      "adamw_w_ada")
    g_b_ada, d_b_ada, nm_b_ada, nv_b_ada = _adamw_bias(b_ada, m_b_ada, v_b_ada, dmod_all, "adamw_b_ada")
    flat = lambda t: t.reshape(-1, t.shape[-1])
    to_t = lambda t: flat(jnp.transpose(t, (0, 2, 1)))
    from_t = lambda t: jnp.transpose(t.reshape(DEPTH, shard_in, D_MODEL), (0, 2, 1))
    g_w_in, d_w_in, nm_w_in, nv_w_in = [from_t(t) for t in _adamw(to_t(w_in), g_w_in_t, to_t(m_w_in), to_t(v_w_in),
                                                                  shard_in // 2, "adamw_w_in")]
    gwout, d_w_out, nm_w_out, nv_w_out = [t.reshape(w_out.shape) for t in _adamw(
        flat(w_out), g_w_out, flat(m_w_out), flat(v_w_out), shard_out, "adamw_w_out")]
    small_out = _adamw_small(small_tot.reshape(DEPTH * PACK_ROWS, 128), [small_w[n] for n in SMALL_NAMES],
                             [small_m[n] for n in SMALL_NAMES], [small_v[n] for n in SMALL_NAMES], "adamw_small")
    gs, ds, ms, vs = [dict(zip(SMALL_NAMES, group)) for group in small_out]

    def ordered(w_ada_, b_ada_, w_in_, small, w_out_):
        return (w_ada_, b_ada_, w_in_, small["w_pool"], small["pool_scale"], small["sgu_ln_g"], small["sgu_ln_b"],
                small["w_sgu"], small["b_sgu"], w_out_, small["ln_g"], small["ln_b"])

    return (loss, grad_x,
            *ordered(g_w_ada, g_b_ada, g_w_in, gs, gwout),
            *ordered(d_w_ada, d_b_ada, d_w_in, ds, d_w_out),
            *ordered(nm_w_ada, nm_b_ada, nm_w_in, ms, nm_w_out),
            *ordered(nv_w_ada, nv_b_ada, nv_w_in, vs, nv_w_out))
```

```python
import jax
import jax.numpy as jnp
from jax import lax
from jax.experimental import pallas as pl
from jax.experimental.pallas import tpu as pltpu

F32 = jnp.float32
BF16 = jnp.bfloat16

D_MODEL = 1024
SEQ = 2048
DEPTH = 2
D_POOL = 512
D_SGU = 512
D_IN = 2560
N_GROUPS = 4
GROUP = 128
N_HEADS = 4
HEAD = 128
CHUNK = 128
WINDOWS = (2, 4, 8, 16)
ALPHA = (2.0 * DEPTH) ** 0.25
LN_EPS = 1e-5
N_DEV = 8

ADAM_LR = 0.001
ADAM_B1 = 0.9
ADAM_B2 = 0.999
ADAM_EPS = 1e-08
ADAM_WD = 0.01
ADAM_STEP = 10

TM = 256
HALO = 16
N_TILES = SEQ // TM
VMEM_LIMIT = 60 * 1024 * 1024

ROW_WPOOL = 0
ROW_WSGU = 512
ROW_PSCALE = 1024
ROW_SLNG = 1028
ROW_SLNB = 1032
ROW_BSGU = 1036
ROW_LNG = 1040
ROW_LNB = 1048
PACK_ROWS = 1088
DMOD_COLS = DEPTH * 3 * D_MODEL // 8

SQRT_HALF = 0.7071067811865476
INV_SQRT_2PI = 0.3989422804014327


def _ln(x):
    mu = jnp.mean(x, axis=-1, keepdims=True)
    xc = x - mu
    var = jnp.mean(xc * xc, axis=-1, keepdims=True)
    rstd = lax.rsqrt(var + LN_EPS)
    return xc * rstd, rstd


def _ln_bwd(dxn, xn, rstd):
    m1 = jnp.mean(dxn, axis=-1, keepdims=True)
    m2 = jnp.mean(dxn * xn, axis=-1, keepdims=True)
    return rstd * (dxn - m1 - xn * m2)


def _normal_cdf(x):
    return 0.5 * (1.0 + lax.erf(x * SQRT_HALF))


def _gelu_parts(x, cdf, with_grad):
    if not with_grad:
        return x * cdf, None
    return x * cdf, cdf + x * (INV_SQRT_2PI * jnp.exp(-0.5 * x * x))


def _silu_parts(x):
    s = jax.nn.sigmoid(x)
    return x * s, s * (1.0 + x * (1.0 - s))


def _dot(a, b):
    return lax.dot_general(a, b, (((1,), (0,)), ((), ())), preferred_element_type=F32)


def _dot_nt(a, b):
    return lax.dot_general(a, b, (((1,), (1,)), ((), ())), preferred_element_type=F32)


def _dot_tn(a, b):
    return lax.dot_general(a, b, (((0,), (0,)), ((), ())), preferred_element_type=F32)


def _row_index(tile):
    return tile * TM + lax.broadcasted_iota(jnp.int32, (TM, 1), 0)


def _window_sums(ext, forward):
    n = TM + HALO
    cur = ext
    outs = []
    for g in range(N_GROUPS):
        step = 1 << g
        cur = cur + pltpu.roll(cur, step if forward else n - step, 0)
        rows = cur[HALO:, :GROUP] if forward else cur[:TM, :GROUP]
        outs.append(rows)
        cur = cur[:, GROUP:] if g + 1 < N_GROUPS else None
    return outs


def _inverse_counts(rows):
    return [1.0 / jnp.minimum(rows + 1, w).astype(F32) for w in WINDOWS]


def _tril_bf16(w):
    t = lax.broadcasted_iota(jnp.int32, (CHUNK, CHUNK), 0)
    s = lax.broadcasted_iota(jnp.int32, (CHUNK, CHUNK), 1)
    return jnp.where(t >= s, w, 0.0).astype(BF16)


class _MixWeights:
    def __init__(self, layer, wpool_ref, pscale_ref, slng_ref, slnb_ref, wsgu_ref, bsgut_ref):
        self.layer = layer
        self.wpool_ref, self.pscale_ref, self.slng_ref, self.slnb_ref = wpool_ref, pscale_ref, slng_ref, slnb_ref
        self.wsgu_ref, self.bsgut_ref = wsgu_ref, bsgut_ref

    def pool(self, g):
        return self.wpool_ref[self.layer, g].astype(BF16)

    def pool_scale(self, g):
        return self.pscale_ref[self.layer:self.layer + 1, g * GROUP:(g + 1) * GROUP]

    def ln_gain(self, h):
        return self.slng_ref[self.layer, h:h + 1, :]

    def ln_bias(self, h):
        return self.slnb_ref[self.layer, h:h + 1, :]

    def mix(self, h):
        return _tril_bf16(self.wsgu_ref[self.layer, h])

    def mix_bias(self, h):
        return self.bsgut_ref[self.layer, :, h:h + 1]


SMALL_SPECS = ((DEPTH, N_GROUPS, GROUP, GROUP), (DEPTH, D_POOL), (DEPTH, N_HEADS, HEAD), (DEPTH, N_HEADS, HEAD),
               (DEPTH, N_HEADS, CHUNK, CHUNK), (DEPTH, CHUNK, N_HEADS))


def _mix_forward(proj, halo, tile, w, cdf=None):
    keep = cdf is not None
    rows = _row_index(tile)
    inv_counts = _inverse_counts(rows)
    xa = proj[:, 0:D_POOL]
    ga = proj[:, D_POOL:2 * D_POOL]
    sums = _window_sums(jnp.concatenate([halo, xa], axis=0), True)
    ga_act, ga_grad = _silu_parts(ga)
    pooled, pw, ya = [], [], []
    for g in range(N_GROUPS):
        sl = slice(g * GROUP, (g + 1) * GROUP)
        p = (sums[g] * inv_counts[g] - xa[:, sl]).astype(BF16)
        q = _dot(p, w.pool(g))
        pooled.append(p)
        pw.append(q)
        ya.append(q * w.pool_scale(g) * ga_act[:, sl])

    u = proj[:, 2 * D_POOL:2 * D_POOL + D_SGU]
    v = proj[:, 2 * D_POOL + D_SGU:2 * D_POOL + 2 * D_SGU]
    gb = proj[:, 2 * D_POOL + 2 * D_SGU:]
    gb_act, gb_grad = _silu_parts(gb)
    if cdf is None:
        cdf = jnp.concatenate([_normal_cdf(u), _normal_cdf(v)], axis=1)
    u_act, u_grad = _gelu_parts(u, cdf[:, :D_SGU], keep)
    v_act, v_grad = _gelu_parts(v, cdf[:, D_SGU:], keep)
    vn, vrstd, vln, mixed, yb = [], [], [], [], []
    for h in range(N_HEADS):
        sl = slice(h * HEAD, (h + 1) * HEAD)
        n_h, r_h = _ln(v_act[:, sl])
        l_h = (n_h * w.ln_gain(h) + w.ln_bias(h)).astype(BF16)
        w_h = w.mix(h)
        bias = w.mix_bias(h)
        m_h = jnp.concatenate(
            [_dot(w_h, l_h[k * CHUNK:(k + 1) * CHUNK]) + bias for k in range(TM // CHUNK)], axis=0)
        vn.append(n_h)
        vrstd.append(r_h)
        vln.append(l_h)
        mixed.append(m_h)
        yb.append(u_act[:, sl] * m_h * gb_act[:, sl])
    cat = jnp.concatenate(ya + yb, axis=1)
    if not keep:
        return cat, cdf
    return cat, dict(inv_counts=inv_counts, ga_act=ga_act, ga_grad=ga_grad, pooled=pooled, pw=pw, u_grad=u_grad,
                     v_grad=v_grad, u_act=u_act, gb_act=gb_act, gb_grad=gb_grad, vn=vn, vrstd=vrstd, vln=vln,
                     mixed=mixed)


def _const_spec(shape):
    nd = len(shape)
    return pl.BlockSpec(shape, lambda i: (0,) * nd)


def _layer_forward(layer, x, mod, w_int, w_outf, small, ln_g, ln_b, name):
    def body(x_ref, mod_ref, wint_ref, wout_ref, wpool_ref, pscale_ref, slng_ref, slnb_ref, wsgu_ref, bsgut_ref,
             lng_ref, lnb_ref, out_ref, proj_ref, y_ref, cdf_ref, halo_ref):
        weights = _MixWeights(layer, wpool_ref, pscale_ref, slng_ref, slnb_ref, wsgu_ref, bsgut_ref)
        tile = pl.program_id(0)

        @pl.when(tile == 0)
        def _():
            halo_ref[...] = jnp.zeros_like(halo_ref)

        xt = x_ref[...]
        shift = mod_ref[layer:layer + 1, 0:D_MODEL]
        scale = mod_ref[layer:layer + 1, D_MODEL:2 * D_MODEL]
        gate = mod_ref[layer:layer + 1, 2 * D_MODEL:]
        xn, _ = _ln(xt)
        h = (xn * (1.0 + scale) + shift).astype(BF16)
        proj = _dot_nt(h, wint_ref[...])
        proj_ref[...] = proj
        cat, cdf_ref[...] = _mix_forward(proj, halo_ref[...], tile, weights)
        halo_ref[...] = proj[TM - HALO:, 0:D_POOL]
        y = _dot(cat.astype(BF16), wout_ref[...])
        y_ref[...] = y
        zn, _ = _ln(ALPHA * xt + gate * y)
        out_ref[...] = zn * lng_ref[layer:layer + 1, :] + lnb_ref[layer:layer + 1, :]

    row = lambda w: pl.BlockSpec((TM, w), lambda i: (i, 0))
    return pl.pallas_call(
        body,
        name=name,
        grid=(N_TILES,),
        in_specs=[row(D_MODEL), _const_spec((DEPTH, 3 * D_MODEL)), _const_spec((D_IN, D_MODEL)),
                  _const_spec((D_MODEL, D_MODEL))] + [_const_spec(s) for s in SMALL_SPECS]
                 + [_const_spec((DEPTH, D_MODEL)), _const_spec((DEPTH, D_MODEL))],
        out_specs=[row(D_MODEL), row(D_IN), row(D_MODEL), row(2 * D_SGU)],
        out_shape=[jax.ShapeDtypeStruct((SEQ, D_MODEL), F32), jax.ShapeDtypeStruct((SEQ, D_IN), F32),
                   jax.ShapeDtypeStruct((SEQ, D_MODEL), F32), jax.ShapeDtypeStruct((SEQ, 2 * D_SGU), F32)],
        scratch_shapes=[pltpu.VMEM((HALO, D_POOL), F32)],
        compiler_params=pltpu.CompilerParams(dimension_semantics=("arbitrary",), vmem_limit_bytes=VMEM_LIMIT),
    )(x, mod, w_int, w_outf, *small, ln_g, ln_b)


VEC_LNG, VEC_LNB, VEC_POOL, VEC_SGU, VEC_SHIFT, VEC_SCALE, VEC_GATE, VEC_LOSS = range(8)


def _layer_backward(layer, a, b, x, proj, y, cdf, mod, w_int, w_outf, small, ln_g, is_last, name, carry=(),
                    reduce=()):
    n_red, n_carry = len(reduce), len(carry)
    base = layer * PACK_ROWS

    def body(a_ref, b_ref, x_ref, proj_ref, prev_ref, y_ref, cdf_ref, mod_ref, wint_ref, wout_ref, wpool_ref,
             pscale_ref, slng_ref, slnb_ref, wsgu_ref, bsgut_ref, lng_ref, *rest):
        weights = _MixWeights(layer, wpool_ref, pscale_ref, slng_ref, slnb_ref, wsgu_ref, bsgut_ref)
        carry_refs, rest = rest[:n_carry], rest[n_carry:]
        part_refs, rest = rest[:n_red], rest[n_red:]
        dx_ref, dproj_ref, h_ref, cat_ref, dy_ref, small_ref, dmod_ref, loss_ref = rest[:8]
        shard_refs, rest = rest[8:8 + n_red], rest[8 + n_red:]
        vec_ref, dmix_ref, halo_ref = rest[:3]
        step = pl.program_id(0)
        tile = N_TILES - 1 - step

        def scatter():
            bufs, sems = rest[3:3 + 5 * n_red], rest[3 + 5 * n_red:]
            arrays = [dict(part=part_refs[n], out=shard_refs[n], staged=True, stage=bufs[5 * n], sib=bufs[5 * n + 1],
                           snd=bufs[5 * n + 2], rcv=bufs[5 * n + 3], relay=bufs[5 * n + 4]) for n in range(n_red)]
            return _ChipReduceScatter(arrays, *sems)

        @pl.when(step == 0)
        def _():
            small_ref[...] = jnp.zeros_like(small_ref)
            dmod_ref[...] = jnp.zeros_like(dmod_ref)
            vec_ref[...] = jnp.zeros_like(vec_ref)
            dmix_ref[...] = jnp.zeros_like(dmix_ref)
            halo_ref[...] = jnp.zeros_like(halo_ref)
            if n_red:
                scatter().start()

        if n_red:
            @pl.when(step == 1)
            def _():
                scatter().exchange()

            @pl.when(step == N_TILES // 2)
            def _():
                scatter().fold()

        def acc(row, lo, val):
            hi = lo + val.shape[1]
            vec_ref[row:row + 1, lo:hi] += jnp.sum(val, axis=0, keepdims=True)

        xt = x_ref[...]
        yt = y_ref[...]
        shift = mod_ref[layer:layer + 1, 0:D_MODEL]
        scale = mod_ref[layer:layer + 1, D_MODEL:2 * D_MODEL]
        gate = mod_ref[layer:layer + 1, 2 * D_MODEL:]
        ln_gain = lng_ref[layer:layer + 1, :]

        zn, zrstd = _ln(ALPHA * xt + gate * yt)
        if is_last:
            diff = a_ref[...] - b_ref[...]
            acc(VEC_LOSS, 0, diff * diff)
            dout = diff * (1.0 / D_MODEL)
        else:
            dout = a_ref[...]
        acc(VEC_LNG, 0, dout * zn)
        acc(VEC_LNB, 0, dout)
        dz = _ln_bwd(dout * ln_gain, zn, zrstd)
        acc(VEC_GATE, 0, dz * yt)
        dy = (dz * gate).astype(BF16)
        dy_ref[...] = dy
        dcat = _dot_nt(dy, wout_ref[...])

        proj = proj_ref[...]
        prev = jnp.where(tile > 0, prev_ref[...], 0.0)
        cat, k = _mix_forward(proj, prev, tile, weights, cdf_ref[...])
        cat_ref[...] = cat.astype(BF16)

        dga, dq = [], []
        for g in range(N_GROUPS):
            sl = slice(g * GROUP, (g + 1) * GROUP)
            pscale = weights.pool_scale(g)
            dya = dcat[:, sl]
            dyp = dya * k["ga_act"][:, sl]
            dga.append(dya * k["pw"][g] * pscale * k["ga_grad"][:, sl])
            acc(VEC_POOL, g * GROUP, dyp * k["pw"][g])
            dpw = (dyp * pscale).astype(BF16)
            rows = pl.ds(base + ROW_WPOOL + g * GROUP, GROUP)
            small_ref[rows, :] += _dot_tn(k["pooled"][g], dpw)
            dq.append(_dot_nt(dpw, weights.pool(g)))
        dpooled = jnp.concatenate(dq, axis=1)
        scaled = jnp.concatenate([dq[g] * k["inv_counts"][g] for g in range(N_GROUPS)], axis=1)
        sums = _window_sums(jnp.concatenate([scaled, halo_ref[...]], axis=0), False)
        halo_ref[...] = scaled[0:HALO]
        dxa = jnp.concatenate(sums, axis=1) - dpooled

        du, dv, dgb = [], [], []
        for h in range(N_HEADS):
            sl = slice(h * HEAD, (h + 1) * HEAD)
            dyb = dcat[:, D_POOL + h * HEAD:D_POOL + (h + 1) * HEAD]
            m_h = k["mixed"][h]
            ug = k["u_act"][:, sl] * dyb
            du.append(dyb * m_h * k["gb_act"][:, sl] * k["u_grad"][:, sl])
            dgb.append(ug * m_h * k["gb_grad"][:, sl])
            dmixed = ug * k["gb_act"][:, sl]
            dmixed_bf = dmixed.astype(BF16)
            w_h = weights.mix(h)
            dvln_parts = []
            dmix_sum = dmix_ref[h]
            wsgu_rows = pl.ds(base + ROW_WSGU + h * CHUNK, CHUNK)
            dws = small_ref[wsgu_rows, :]
            for c in range(TM // CHUNK):
                cs = slice(c * CHUNK, (c + 1) * CHUNK)
                dmix_sum = dmix_sum + dmixed[cs]
                dws = dws + _dot_nt(dmixed_bf[cs], k["vln"][h][cs])
                dvln_parts.append(_dot_tn(w_h, dmixed_bf[cs]))
            dmix_ref[h] = dmix_sum
            small_ref[wsgu_rows, :] = dws
            dvln = jnp.concatenate(dvln_parts, axis=0)
            acc(VEC_SGU, h * HEAD, dvln * k["vn"][h])
            acc(VEC_SGU, D_SGU + h * HEAD, dvln)
            dvv = _ln_bwd(dvln * weights.ln_gain(h), k["vn"][h], k["vrstd"][h])
            dv.append(dvv * k["v_grad"][:, sl])

        dproj = jnp.concatenate([dxa] + dga + du + dv + dgb, axis=1).astype(BF16)
        dproj_ref[...] = dproj
        dh = _dot(dproj, wint_ref[...])

        xn, xrstd = _ln(xt)
        h_ref[...] = (xn * (1.0 + scale) + shift).astype(BF16)
        acc(VEC_SCALE, 0, dh * xn)
        acc(VEC_SHIFT, 0, dh)
        dx_ref[...] = _ln_bwd(dh * (1.0 + scale), xn, xrstd) + ALPHA * dz

        @pl.when(step == N_TILES - 1)
        def _():
            def put(row0, vec_row, lo, n):
                for r in range(n):
                    small_ref[base + row0 + r:base + row0 + r + 1, :] = (
                        vec_ref[vec_row:vec_row + 1, lo + r * 128:lo + (r + 1) * 128])

            put(ROW_PSCALE, VEC_POOL, 0, 4)
            put(ROW_SLNG, VEC_SGU, 0, 4)
            put(ROW_SLNB, VEC_SGU, D_SGU, 4)
            put(ROW_LNG, VEC_LNG, 0, 8)
            put(ROW_LNB, VEC_LNB, 0, 8)
            ones = jnp.ones((8, HEAD), F32)
            t = lax.broadcasted_iota(jnp.int32, (CHUNK, CHUNK), 0)
            s = lax.broadcasted_iota(jnp.int32, (CHUNK, CHUNK), 1)
            for h in range(N_HEADS):
                bias_rows = lax.dot_general(ones, dmix_ref[h], (((1,), (1,)), ((), ())),
                                            preferred_element_type=F32, precision=lax.Precision.HIGHEST)
                small_ref[base + ROW_BSGU + h:base + ROW_BSGU + h + 1, :] = bias_rows[0:1]
                rows = pl.ds(base + ROW_WSGU + h * CHUNK, CHUNK)
                small_ref[rows, :] = jnp.where(t >= s, small_ref[rows, :], 0.0)
            pieces = ((0, VEC_SHIFT, 0, 768),
                      (1, VEC_SHIFT, 768, 256), (1, VEC_SCALE, 0, 512),
                      (2, VEC_SCALE, 512, 512), (2, VEC_GATE, 0, 256),
                      (3, VEC_GATE, 256, 768))
            filled = [0] * 4
            for q, vec_row, lo, n in pieces:
                row = 4 * layer + q
                dmod_ref[row:row + 1, filled[q]:filled[q] + n] = vec_ref[vec_row:vec_row + 1, lo:lo + n]
                filled[q] += n
            if n_carry:
                for other in range(layer + 1, DEPTH):
                    rows = pl.ds(other * PACK_ROWS, PACK_ROWS)
                    small_ref[rows, :] = carry_refs[0][rows, :]
                    dmod_ref[4 * other:4 * other + 4, :] = carry_refs[1][4 * other:4 * other + 4, :]
            loss_ref[...] = vec_ref[VEC_LOSS:VEC_LOSS + 1, :]
            if n_red:
                scatter().finish()
                scatter().wait_sends()

    rev = lambda w: pl.BlockSpec((TM, w), lambda i: (N_TILES - 1 - i, 0))
    prev_spec = pl.BlockSpec(
        (HALO, D_POOL), lambda i: (jnp.maximum((N_TILES - 1 - i) * (TM // HALO) - 1, 0), 0))
    comm_scratch = []
    for p in reduce:
        comm_scratch += _ChipReduceScatter.buffers(p.shape[2], p.shape[3], p.dtype)
    if n_red:
        comm_scratch += _ChipReduceScatter.semaphores(n_red)
    return pl.pallas_call(
        body,
        name=name,
        grid=(N_TILES,),
        in_specs=[rev(D_MODEL), rev(D_MODEL) if is_last else pl.BlockSpec((TM, D_MODEL), lambda i: (0, 0)),
                  rev(D_MODEL), rev(D_IN), prev_spec, rev(D_MODEL), rev(2 * D_SGU),
                  _const_spec((DEPTH, 3 * D_MODEL)), _const_spec((D_IN, D_MODEL)), _const_spec((D_MODEL, D_MODEL))]
                 + [_const_spec(s) for s in SMALL_SPECS] + [_const_spec((DEPTH, D_MODEL))]
                 + [_const_spec(c.shape) for c in carry] + [ANY] * n_red,
        out_specs=[rev(D_MODEL), rev(D_IN), rev(D_MODEL), rev(D_MODEL), rev(D_MODEL),
                   _const_spec((DEPTH * PACK_ROWS, 128)), _const_spec((8, DMOD_COLS)), _const_spec((1, D_MODEL))]
                  + [_const_spec(p.shape[2:]) for p in reduce],
        out_shape=[jax.ShapeDtypeStruct((SEQ, D_MODEL), F32), jax.ShapeDtypeStruct((SEQ, D_IN), BF16),
                   jax.ShapeDtypeStruct((SEQ, D_MODEL), BF16), jax.ShapeDtypeStruct((SEQ, D_MODEL), BF16),
                   jax.ShapeDtypeStruct((SEQ, D_MODEL), BF16), jax.ShapeDtypeStruct((DEPTH * PACK_ROWS, 128), F32),
                   jax.ShapeDtypeStruct((8, DMOD_COLS), F32), jax.ShapeDtypeStruct((1, D_MODEL), F32)]
                  + [jax.ShapeDtypeStruct(p.shape[2:], F32) for p in reduce],
        scratch_shapes=[pltpu.VMEM((8, D_MODEL), F32), pltpu.VMEM((N_HEADS, CHUNK, HEAD), F32),
                        pltpu.VMEM((HALO, D_POOL), F32)] + comm_scratch,
        compiler_params=pltpu.CompilerParams(dimension_semantics=("arbitrary",), vmem_limit_bytes=VMEM_LIMIT),
    )(a, b, x, proj, proj, y, cdf, mod, w_int, w_outf, *small, ln_g, *carry, *reduce)


def _grad_matmul(lhs, rhs, block_cols, name):
    m, n = lhs.shape[1], rhs.shape[1]

    def body(lhs_ref, rhs_ref, out_ref):
        out_ref[...] = _dot_tn(lhs_ref[...], rhs_ref[...]).astype(BF16)

    return pl.pallas_call(
        body,
        name=name,
        grid=(m // block_cols,),
        in_specs=[pl.BlockSpec((SEQ, block_cols), lambda j: (0, j)), pl.BlockSpec((SEQ, n), lambda j: (0, 0))],
        out_specs=pl.BlockSpec((block_cols, n), lambda j: (j, 0)),
        out_shape=jax.ShapeDtypeStruct((m, n), BF16),
        compiler_params=pltpu.CompilerParams(dimension_semantics=("arbitrary",), vmem_limit_bytes=VMEM_LIMIT),
    )(lhs, rhs)


def _adamw_math(w, g, m, v):
    m = ADAM_B1 * m + (1.0 - ADAM_B1) * g
    v = ADAM_B2 * v + (1.0 - ADAM_B2) * (g * g)
    m_hat = m / (1.0 - ADAM_B1 ** ADAM_STEP)
    v_hat = v / (1.0 - ADAM_B2 ** ADAM_STEP)
    delta = -ADAM_LR * (m_hat / (jnp.sqrt(v_hat) + ADAM_EPS) + ADAM_WD * w)
    return delta, m, v


def _adamw(w, grads, m, v, block_rows, name):
    rows, cols = grads[0].shape
    blocks = rows // block_rows

    def body(w_ref, m_ref, v_ref, *rest):
        g_refs, (g_ref, d_ref, nm_ref, nv_ref) = rest[:DEPTH], rest[DEPTH:]
        for layer in range(DEPTH):
            @pl.when(pl.program_id(0) == layer)
            def _():
                g = g_refs[layer][...]
                g_ref[...] = g
                d_ref[...], nm_ref[...], nv_ref[...] = _adamw_math(w_ref[...], g, m_ref[...], v_ref[...])

    def grad_spec(layer):
        return pl.BlockSpec((block_rows, cols),
                            lambda l, i: (jnp.where(l == layer, i, jnp.where(l < layer, 0, blocks - 1)), 0))

    spec = pl.BlockSpec((block_rows, cols), lambda l, i: (l * blocks + i, 0))
    return pl.pallas_call(
        body,
        name=name,
        grid=(DEPTH, blocks),
        in_specs=[spec] * 3 + [grad_spec(layer) for layer in range(DEPTH)],
        out_specs=[spec] * 4,
        out_shape=[jax.ShapeDtypeStruct(w.shape, F32)] * 4,
        compiler_params=pltpu.CompilerParams(dimension_semantics=("arbitrary", "arbitrary"),
                                             vmem_limit_bytes=VMEM_LIMIT),
    )(w, m, v, *grads)


MESH = pl.DeviceIdType.MESH
SIBLING = 1
ANY = pl.BlockSpec(memory_space=pl.ANY)
VMEM = pl.BlockSpec(memory_space=pltpu.VMEM)


def _me():
    return lax.axis_index("x"), lax.axis_index("y"), lax.axis_index("c")


def _peer(r):
    x, y, c = _me()
    return (1 - x if r & 4 else x, 1 - y if r & 2 else y, 1 - c if r & 1 else c)


def _index(dev):
    return 4 * dev[0] + 2 * dev[1] + dev[2]


def _remote(src, dst, send_sem, recv_sem, dev):
    return pltpu.make_async_remote_copy(src_ref=src, dst_ref=dst, send_sem=send_sem, recv_sem=recv_sem,
                                        device_id=dev, device_id_type=MESH)


ACROSS_X, ACROSS_Y, ACROSS_BOTH = 4, 2, 6
GATHER_SEMS = 11


class _TwoLevelGather:
    def __init__(self, out, send_sems, recv_sems, src=None):
        self.out, self.send_sems, self.recv_sems, self.src = out, send_sems, recv_sems, src
        self.rows = (out.shape[0] // N_DEV) if len(out.shape) == 2 else out.shape[1]
        self.half = self.rows // 2

    def _slot(self, block):
        if len(self.out.shape) == 2:
            return self.out.at[pl.ds(pl.multiple_of(_index(block) * self.rows, self.rows), self.rows)]
        return self.out.at[_index(block)]

    def _copy(self, k, block, part, to, src=None):
        slot = self._slot(block)
        if part is not None:
            rows = pl.ds(part * self.half, self.half)
            slot = slot.at[rows]
            src = None if src is None else src.at[rows]
        return _remote(slot if src is None else src, slot, self.send_sems.at[k], self.recv_sems.at[k], to)

    def _mine(self):
        me = _me()
        src = self._slot(me) if self.src is None else self.src
        x, y = _peer(ACROSS_X), _peer(ACROSS_Y)
        return [self._copy(1, me, 0, x, src), self._copy(3, me, 1, y, src), self._copy(0, me, None, _peer(SIBLING), src),
                self._copy(2, me, 1, x, src), self._copy(4, me, 0, y, src)]

    def _relayed(self):
        return [self._copy(5, _peer(ACROSS_X), 0, _peer(ACROSS_Y)), self._copy(6, _peer(ACROSS_Y), 1, _peer(ACROSS_X))]

    def _passed(self):
        sib, far = _peer(SIBLING), _peer(ACROSS_BOTH)
        return [self._copy(7, _peer(ACROSS_X), None, sib), self._copy(8, _peer(ACROSS_Y), None, sib),
                self._copy(9, far, 0, sib), self._copy(10, far, 1, sib)]

    def _arrival(self, k, r, part):
        return self._copy(k, _peer(r), part, _me())

    def send_first(self):
        for cp in self._mine()[:3]:
            cp.start()

    def send_second(self):
        for cp in self._mine()[3:]:
            cp.start()

    def send_mine(self):
        self.send_first()
        self.send_second()

    def relay(self):
        relayed = self._relayed()
        self._arrival(1, ACROSS_X, 0).wait_recv()
        relayed[0].start()
        self._arrival(3, ACROSS_Y, 1).wait_recv()
        relayed[1].start()

    def pass_near(self):
        passed = self._passed()
        self._arrival(2, ACROSS_X, 1).wait_recv()
        passed[0].start()
        self._arrival(4, ACROSS_Y, 0).wait_recv()
        passed[1].start()

    def pass_far(self):
        passed = self._passed()
        self._arrival(5, ACROSS_BOTH, 0).wait_recv()
        passed[2].start()
        self._arrival(6, ACROSS_BOTH, 1).wait_recv()
        passed[3].start()

    def pass_on(self):
        self.pass_near()
        self.pass_far()

    def wait_sibling(self):
        self._arrival(0, SIBLING, None).wait_recv()

    def wait_passed(self, r):
        if r == ACROSS_BOTH:
            self._arrival(9, r ^ SIBLING, 0).wait_recv()
            self._arrival(10, r ^ SIBLING, 1).wait_recv()
        else:
            self._arrival(7 if r == ACROSS_X else 8, r ^ SIBLING, None).wait_recv()

    def wait_rest(self):
        self.wait_sibling()
        for r in (ACROSS_X, ACROSS_Y, ACROSS_BOTH):
            self.wait_passed(r)

    def wait_sends(self):
        for cp in self._mine() + self._relayed() + self._passed():
            cp.wait_send()


class _ChipReduceScatter:
    SLOTS = 6

    def __init__(self, arrays, l_sem, d_send, d_recv, i_send, i_recv):
        self.arrays = arrays
        self.l_sem, self.d_send, self.d_recv, self.i_send, self.i_recv = l_sem, d_send, d_recv, i_send, i_recv

    @staticmethod
    def buffers(rows, cols, dtype, staged=True):
        stage = [pltpu.VMEM((4, rows, cols), dtype)] if staged else []
        return stage + [pltpu.VMEM((4, rows, cols), dtype), pltpu.VMEM((3, rows, cols), dtype),
                        pltpu.VMEM((2, rows, cols), dtype), pltpu.VMEM((2, rows // 2, cols), dtype)]

    @classmethod
    def semaphores(cls, n):
        return [pltpu.SemaphoreType.DMA((n,)), pltpu.SemaphoreType.DMA((n, 4)), pltpu.SemaphoreType.DMA((n, 4)),
                pltpu.SemaphoreType.DMA((n, cls.SLOTS)), pltpu.SemaphoreType.DMA((n, cls.SLOTS))]

    def _pick(self, which):
        return list(enumerate(self.arrays)) if which is None else [(n, self.arrays[n]) for n in which]

    @staticmethod
    def _chip(r):
        dev = _me() if r is None else _peer(r)
        return 2 * dev[0] + dev[1]

    def _staging(self, which):
        c = _me()[2]
        return [pltpu.make_async_copy(a["part"].at[pl.ds(0, 4), c], a["stage"], self.l_sem.at[n])
                for n, a in self._pick(which) if a["staged"]]

    def _first(self, which, chip):
        other = 1 - _me()[2]
        return [_remote(a["part"].at[chip, other], a["sib"].at[chip], self.d_send.at[n, chip], self.d_recv.at[n, chip],
                        _peer(SIBLING)) for n, a in self._pick(which)]

    @staticmethod
    def _halves(a):
        half = a["rcv"].shape[1] // 2
        return pl.ds(0, half), pl.ds(half, half)

    def _hops(self, n, a):
        h0, h1 = self._halves(a)
        x, y = _peer(ACROSS_X), _peer(ACROSS_Y)
        snd, rcv, relay = a["snd"], a["rcv"], a["relay"]
        pairs = [(snd.at[2, h0], relay.at[0], x), (snd.at[2, h1], relay.at[1], y),
                 (snd.at[0, h0], rcv.at[0, h0], x), (snd.at[0, h1], rcv.at[0, h1], x),
                 (snd.at[1, h1], rcv.at[1, h1], y), (snd.at[1, h0], rcv.at[1, h0], y)]
        return [_remote(s, d, self.i_send.at[n, k], self.i_recv.at[n, k], to) for k, (s, d, to) in enumerate(pairs)]

    def _mine(self, a, chip, rows=None):
        src = a["stage"].at[chip] if a["staged"] else a["part"].at[chip, _me()[2]]
        mine, sib = (src[...], a["sib"][chip]) if rows is None else (src[rows, :], a["sib"][chip, rows, :])
        return mine.astype(F32) + sib.astype(F32)

    def start(self, which=None, chips=None):
        if chips is None:
            for cp in self._staging(which):
                cp.start()
        for chip in range(4) if chips is None else chips:
            for cp in self._first(which, chip):
                cp.start()

    def send_far(self, which=None):
        far = self._chip(ACROSS_BOTH)
        for cp in self._staging(which):
            cp.wait()
        for cp in self._first(which, far):
            cp.wait_recv()
        for n, a in self._pick(which):
            hops = self._hops(n, a)
            a["snd"][2] = self._mine(a, far).astype(a["snd"].dtype)
            hops[0].start()
            hops[1].start()

    def send_near(self, r, which=None):
        chip = self._chip(r)
        for cp in self._first(which, chip):
            cp.wait_recv()
        for n, a in self._pick(which):
            h0, h1 = self._halves(a)
            hops = self._hops(n, a)
            if r == ACROSS_X:
                a["snd"][0, h0, :] = self._mine(a, chip, h0).astype(a["snd"].dtype)
                hops[2].start()
            else:
                a["snd"][1, h1, :] = self._mine(a, chip, h1).astype(a["snd"].dtype)
                hops[4].start()

    def exchange(self, which=None):
        self.send_far(which)
        self.send_near(ACROSS_X, which)
        self.send_near(ACROSS_Y, which)

    def fold(self, which=None):
        across_x, across_y = self._chip(ACROSS_X), self._chip(ACROSS_Y)
        for n, a in self._pick(which):
            h0, h1 = self._halves(a)
            hops = self._hops(n, a)
            dtype = a["snd"].dtype
            hops[1].wait_recv()
            a["snd"][0, h1, :] = (self._mine(a, across_x, h1) + a["relay"][1].astype(F32)).astype(dtype)
            hops[3].start()
            hops[0].wait_recv()
            a["snd"][1, h0, :] = (self._mine(a, across_y, h0) + a["relay"][0].astype(F32)).astype(dtype)
            hops[5].start()

    def finish(self, which=None):
        home = self._chip(None)
        for cp in self._first(which, home):
            cp.wait_recv()
        for n, a in self._pick(which):
            hops = self._hops(n, a)
            a["out"][...] = self._mine(a, home)
            hops[2].wait_recv()
            hops[3].wait_recv()
            a["out"][...] += a["rcv"][0].astype(F32)
            hops[4].wait_recv()
            hops[5].wait_recv()
            a["out"][...] += a["rcv"][1].astype(F32)

    def wait_sends(self, which=None):
        for chip in range(4):
            for cp in self._first(which, chip):
                cp.wait_send()
        for n, a in self._pick(which):
            for cp in self._hops(n, a):
                cp.wait_send()


def _direct_exchange(src_of, dst_of, send_sems, recv_sems):
    me = _me()
    copies = [_remote(src_of(_peer(r)), dst_of(me), send_sems.at[r - 1], recv_sems.at[r - 1], _peer(r))
              for r in range(1, N_DEV)]
    for cp in copies:
        cp.start()
    return copies


def _wait_direct(copies):
    for cp in copies:
        cp.wait_recv()
    for cp in copies:
        cp.wait_send()


def _first_layer(x, c_row, w_ada, b_ada, mine_now, mine_next, small, ln_g, ln_b):
    layer = 0
    cols = w_ada.shape[2]
    shard = mine_now[0].shape[0]
    pair = 2 * shard

    def body(x_ref, c_ref, wada_ref, bada_ref, wpool_ref, pscale_ref, slng_ref, slnb_ref, wsgu_ref, bsgut_ref,
             lng_ref, lnb_ref, wint_hbm, wout_hbm, nxt_in_hbm, nxt_out_hbm,
             out_ref, y_ref, cdf_ref, acts_ref, mod_ref, proj_hbm, wint_keep, wout_keep, nxt_in_full, nxt_out_full,
             wint_v, wout_v, h_buf, proj_blk, proj_tile, halo_ref, act_all, act_src, part, mod_recv,
             w_send, w_recv, w_local, p_sems, t_sems, a_send, a_recv, m_send, m_recv, n_send, n_recv, n_local):
        weights = _MixWeights(layer, wpool_ref, pscale_ref, slng_ref, slnb_ref, wsgu_ref, bsgut_ref)
        tile = pl.program_id(0)

        def now_gathers():
            return (_TwoLevelGather(wint_v, w_send.at[0], w_recv.at[0], src=wint_hbm),
                    _TwoLevelGather(wout_v, w_send.at[1], w_recv.at[1], src=wout_hbm))

        def next_gathers():
            return (_TwoLevelGather(nxt_in_full, n_send.at[0], n_recv.at[0], src=nxt_in_hbm),
                    _TwoLevelGather(nxt_out_full, n_send.at[1], n_recv.at[1], src=nxt_out_hbm))

        def next_own():
            mine = _index(_me())
            return [pltpu.make_async_copy(nxt_in_hbm, nxt_in_full.at[mine], n_local.at[0]),
                    pltpu.make_async_copy(nxt_out_hbm, nxt_out_full.at[mine], n_local.at[1])]

        def keeps():
            return [pltpu.make_async_copy(wint_v, wint_keep, w_local.at[2]),
                    pltpu.make_async_copy(wout_v, wout_keep, w_local.at[3])]

        def tile_read(t):
            slot = t % 2
            return pltpu.make_async_copy(proj_hbm.at[pl.ds(pl.multiple_of(t * TM, TM), TM)], proj_tile.at[slot],
                                         t_sems.at[slot])

        @pl.when(tile == 0)
        def _():
            me = _me()
            mine = _index(me)
            halo_ref[...] = jnp.zeros_like(halo_ref)

            cval = c_ref[...]
            act_src[...] = jnp.zeros_like(act_src)
            act_src[0:1, :] = cval * jax.nn.sigmoid(cval)
            act_all[mine] = act_src[...]
            act_copies = _direct_exchange(lambda p: act_src, lambda m: act_all.at[_index(m)], a_send, a_recv)

            gather_in, gather_out = now_gathers()
            own_in = pltpu.make_async_copy(wint_hbm, gather_in._slot(me), w_local.at[0])
            own_out = pltpu.make_async_copy(wout_hbm, gather_out._slot(me), w_local.at[1])
            own_in.start()
            own_out.start()
            gather_in.send_first()

            _wait_direct(act_copies)
            acts = jnp.concatenate([act_all[j, 0:1, :] for j in range(N_DEV)], axis=0)
            acts_ref[...] = acts
            part[...] = jnp.zeros_like(part)
            for l in range(DEPTH):
                res = lax.dot_general(acts, wada_ref[l], (((1,), (0,)), ((), ())), preferred_element_type=F32,
                                      precision=lax.Precision.HIGHEST)
                for b in range(N_DEV):
                    part[b, l:l + 1, :] = res[b:b + 1, :]
            mod_recv[mine] = part[mine]
            mod_copies = _direct_exchange(lambda p: part.at[_index(p)], lambda m: mod_recv.at[_index(m)],
                                          m_send, m_recv)
            gather_in.send_second()
            gather_out.send_mine()

            _wait_direct(mod_copies)
            for l in range(DEPTH):
                for j in range(N_DEV):
                    sl = slice(j * cols, (j + 1) * cols)
                    mod_ref[l:l + 1, sl] = mod_recv[j, l:l + 1, :] + bada_ref[l:l + 1, sl]
            shift = mod_ref[layer:layer + 1, 0:D_MODEL]
            scale = mod_ref[layer:layer + 1, D_MODEL:2 * D_MODEL]
            for t in range(N_TILES):
                rows = pl.ds(t * TM, TM)
                xn, _ = _ln(x_ref[rows, :])
                h_buf[rows, :] = (xn * (1.0 + scale) + shift).astype(BF16)

            chip_of = lambda dev: 2 * dev[0] + dev[1]
            writes = []

            def project(n, dev):
                first = pl.multiple_of(chip_of(dev) * pair, pair)
                if n >= 2:
                    writes[n - 2].wait()
                proj_blk[n % 2] = _dot_nt(h_buf[...], wint_v[pl.ds(first, pair), :])
                cp = pltpu.make_async_copy(proj_blk.at[n % 2], proj_hbm.at[:, pl.ds(first, pair)], p_sems.at[n % 2])
                cp.start()
                writes.append(cp)

            gather_in.relay()
            own_in.wait()
            gather_in.wait_sibling()
            project(0, me)
            gather_in.pass_near()
            gather_in.wait_passed(ACROSS_X)
            project(1, _peer(ACROSS_X))
            gather_out.relay()
            for cp in next_own():
                cp.start()
            for g in next_gathers():
                g.send_mine()
            gather_in.wait_passed(ACROSS_Y)
            project(2, _peer(ACROSS_Y))
            gather_in.pass_far()
            gather_in.wait_passed(ACROSS_BOTH)
            project(3, _peer(ACROSS_BOTH))

            gather_out.pass_on()
            gather_out.wait_rest()
            own_out.wait()
            for cp in keeps():
                cp.start()
            writes[2].wait()
            writes[3].wait()
            tile_read(0).start()

        @pl.when(tile + 1 < N_TILES)
        def _():
            tile_read(tile + 1).start()

        @pl.when(tile == 1)
        def _():
            for g in next_gathers():
                g.relay()

        @pl.when(tile == N_TILES // 2)
        def _():
            for g in next_gathers():
                g.pass_near()

        tile_read(tile).wait()
        xt = x_ref[pl.ds(pl.multiple_of(tile * TM, TM), TM), :]
        gate = mod_ref[layer:layer + 1, 2 * D_MODEL:]
        proj = proj_tile[tile % 2]
        cat, cdf_ref[...] = _mix_forward(proj, halo_ref[...], tile, weights)
        halo_ref[...] = proj[TM - HALO:, 0:D_POOL]
        y = _dot(cat.astype(BF16), wout_v[...])
        y_ref[...] = y
        zn, _ = _ln(ALPHA * xt + gate * y)
        out_ref[...] = zn * lng_ref[layer:layer + 1, :] + lnb_ref[layer:layer + 1, :]

        @pl.when(tile == N_TILES - 1)
        def _():
            for g in next_gathers():
                g.pass_far()
            for g in next_gathers():
                g.wait_rest()
            for g in now_gathers() + next_gathers():
                g.wait_sends()
            for cp in keeps() + next_own():
                cp.wait()

    row = lambda w: pl.BlockSpec((TM, w), lambda i: (i, 0))
    gather_sems = pltpu.SemaphoreType.DMA((2, GATHER_SEMS))
    seven = pltpu.SemaphoreType.DMA((7,))
    return pl.pallas_call(
        body,
        name="layer_fwd_0",
        grid=(N_TILES,),
        in_specs=[_const_spec((SEQ, D_MODEL)), _const_spec((1, D_MODEL)), _const_spec(w_ada.shape),
                  _const_spec(b_ada.shape)] + [_const_spec(s) for s in SMALL_SPECS]
                 + [_const_spec((DEPTH, D_MODEL)), _const_spec((DEPTH, D_MODEL))] + [ANY] * 4,
        out_specs=[row(D_MODEL), row(D_MODEL), row(2 * D_SGU), _const_spec((N_DEV, D_MODEL)),
                   _const_spec((DEPTH, 3 * D_MODEL))] + [ANY] * 5,
        out_shape=[jax.ShapeDtypeStruct((SEQ, D_MODEL), F32), jax.ShapeDtypeStruct((SEQ, D_MODEL), F32),
                   jax.ShapeDtypeStruct((SEQ, 2 * D_SGU), F32), jax.ShapeDtypeStruct((N_DEV, D_MODEL), F32),
                   jax.ShapeDtypeStruct((DEPTH, 3 * D_MODEL), F32), jax.ShapeDtypeStruct((SEQ, D_IN), F32),
                   jax.ShapeDtypeStruct((D_IN, D_MODEL), BF16), jax.ShapeDtypeStruct((D_MODEL, D_MODEL), BF16)]
                  + [jax.ShapeDtypeStruct((N_DEV,) + blk.shape, blk.dtype) for blk in mine_next],
        scratch_shapes=[pltpu.VMEM((D_IN, D_MODEL), BF16), pltpu.VMEM((D_MODEL, D_MODEL), BF16),
                        pltpu.VMEM((SEQ, D_MODEL), BF16), pltpu.VMEM((2, SEQ, pair), F32),
                        pltpu.VMEM((2, TM, D_IN), F32), pltpu.VMEM((HALO, D_POOL), F32),
                        pltpu.VMEM((N_DEV, 8, D_MODEL), F32),
                        pltpu.VMEM((8, D_MODEL), F32), pltpu.VMEM((N_DEV, 8, cols), F32),
                        pltpu.VMEM((N_DEV, 8, cols), F32),
                        gather_sems, gather_sems, pltpu.SemaphoreType.DMA((4,)), pltpu.SemaphoreType.DMA((2,)),
                        pltpu.SemaphoreType.DMA((2,)), seven, seven, seven, seven,
                        gather_sems, gather_sems, pltpu.SemaphoreType.DMA((2,))],
        compiler_params=pltpu.CompilerParams(dimension_semantics=("arbitrary",), vmem_limit_bytes=VMEM_LIMIT),
    )(x, c_row, w_ada, b_ada, *small, ln_g, ln_b, *mine_now, *mine_next)


ADA_CHUNK = 256


def _grad_tail(dproj, h, cat, dy, small, dmod8, loss_lanes, w_ada, m_ada, v_ada, act_t, b_ada, m_bada, v_bada):
    shard_in, shard_out, shard_small = D_IN // N_DEV, D_MODEL // N_DEV, small.shape[2]
    cols = w_ada.shape[2]
    W_IN, W_OUT, SMALL = 0, 1, 2

    def body(dproj_hbm, h_hbm, cat_hbm, dy_hbm, small_hbm, dmod_ref, lanes_ref, wada_hbm, mada_hbm, vada_hbm,
             act_ref, bada_ref, mbada_ref, vbada_ref,
             gwin_ref, gwout_ref, stot_ref, loss_ref, gada_hbm, dada_hbm, nmada_hbm, nvada_hbm,
             gb_ref, db_ref, nmb_ref, nvb_ref,
             dproj_v, h_v, cat_v, dy_v, part_in, part_out, own_small, loss_src, loss_all, dmod_all, ada_in, ada_out,
             *rest):
        bufs, rest = rest[:13], rest[13:]
        load_sems, rs_sems = rest[0], rest[1:6]
        m_send, m_recv, g_send, g_recv, s_send, s_recv, ada_lsem, ada_ssem = rest[6:]
        mine = _index(_me())

        def update_ada():
            upper = (mine % 2) == 1

            def dmod_of(layer):
                rows = []
                for b in range(N_DEV):
                    r = dmod_all[b, pl.ds(4 * layer + mine // 2, 1), :]
                    rows.append(jnp.where(upper, r[:, cols:], r[:, :cols]))
                return jnp.concatenate(rows, axis=0)

            chunks = [(layer, c) for layer in range(DEPTH) for c in range(D_MODEL // ADA_CHUNK)]

            def loads(i):
                layer, c = chunks[i]
                rows = pl.ds(c * ADA_CHUNK, ADA_CHUNK)
                return [pltpu.make_async_copy(src.at[layer, rows], ada_in.at[i % 2, k], ada_lsem.at[i % 2, k])
                        for k, src in enumerate((wada_hbm, mada_hbm, vada_hbm))]

            def stores(i):
                layer, c = chunks[i]
                rows = pl.ds(c * ADA_CHUNK, ADA_CHUNK)
                return [pltpu.make_async_copy(ada_out.at[i % 2, k], dst.at[layer, rows], ada_ssem.at[i % 2, k])
                        for k, dst in enumerate((gada_hbm, dada_hbm, nmada_hbm, nvada_hbm))]

            for cp in loads(0):
                cp.start()
            dmods = {}
            for i, (layer, c) in enumerate(chunks):
                if i + 1 < len(chunks):
                    for cp in loads(i + 1):
                        cp.start()
                for cp in loads(i):
                    cp.wait()
                if i >= 2:
                    for cp in stores(i - 2):
                        cp.wait()
                if layer not in dmods:
                    dmods[layer] = dmod_of(layer)
                act = act_ref[pl.ds(c * ADA_CHUNK, ADA_CHUNK), :]
                g = act[:, 0:1] * dmods[layer][0:1, :]
                for b in range(1, N_DEV):
                    g = g + act[:, b:b + 1] * dmods[layer][b:b + 1, :]
                slot = i % 2
                delta, new_m, new_v = _adamw_math(ada_in[slot, 0], g, ada_in[slot, 1], ada_in[slot, 2])
                ada_out[slot, 0] = g
                ada_out[slot, 1] = delta
                ada_out[slot, 2] = new_m
                ada_out[slot, 3] = new_v
                for cp in stores(i):
                    cp.start()
            for i in (len(chunks) - 2, len(chunks) - 1):
                for cp in stores(i):
                    cp.wait()

            total = dmod_all[0]
            for b in range(1, N_DEV):
                total = total + dmod_all[b]
            width = total.shape[1]
            for layer in range(DEPTH):
                for q in range(4):
                    gb_ref[layer:layer + 1, q * width:(q + 1) * width] = total[4 * layer + q:4 * layer + q + 1, :]
            db_ref[...], nmb_ref[...], nvb_ref[...] = _adamw_math(bada_ref[...], gb_ref[...], mbada_ref[...],
                                                                  vbada_ref[...])

        order = (ACROSS_BOTH, ACROSS_X, ACROSS_Y, None)
        chips = [_ChipReduceScatter._chip(r) for r in order]
        loads = [pltpu.make_async_copy(s, d, load_sems.at[n]) for n, (s, d) in enumerate(
            ((cat_hbm, cat_v), (dy_hbm, dy_v), (h_hbm, h_v)))]
        loads += [pltpu.make_async_copy(dproj_hbm.at[:, pl.ds(pl.multiple_of(chip * 2 * shard_in, 2 * shard_in),
                                                             2 * shard_in)], dproj_v.at[n], load_sems.at[3 + n])
                  for n, chip in enumerate(chips)]
        for cp in loads:
            cp.start()
        arrays = [dict(part=part_in, out=gwin_ref, staged=False, sib=bufs[0], snd=bufs[1], rcv=bufs[2], relay=bufs[3]),
                  dict(part=part_out, out=gwout_ref, staged=False, sib=bufs[4], snd=bufs[5], rcv=bufs[6],
                       relay=bufs[7]),
                  dict(part=small_hbm, out=own_small, staged=True, stage=bufs[8], sib=bufs[9], snd=bufs[10],
                       rcv=bufs[11], relay=bufs[12])]
        scatter = _ChipReduceScatter(arrays, *rs_sems)
        scatter.start([SMALL])
        dmod_all[mine] = dmod_ref[...]
        dmod_copies = _direct_exchange(lambda p: dmod_ref, lambda m: dmod_all.at[_index(m)], m_send, m_recv)
        loss_src[...] = jnp.full(loss_src.shape, (0.5 / D_MODEL) * jnp.sum(lanes_ref[...]), F32)
        loss_all[mine] = loss_src[...]
        loss_copies = _direct_exchange(lambda p: loss_src, lambda m: loss_all.at[_index(m)], s_send, s_recv)

        loads[0].wait()
        loads[1].wait()
        for blk in range(2):
            res = _dot_tn(cat_v[:, blk * 512:(blk + 1) * 512], dy_v[...]).astype(BF16)
            for s in range(4):
                part_out[2 * blk + s // 2, s % 2] = res[s * shard_out:(s + 1) * shard_out]
        scatter.start([W_OUT])
        scatter.exchange([SMALL])

        gather = _TwoLevelGather(stot_ref, g_send, g_recv)
        loads[2].wait()
        for n, chip in enumerate(chips):
            loads[3 + n].wait()
            res = _dot_tn(dproj_v[n], h_v[...]).astype(BF16)
            part_in[chip, 0] = res[:shard_in]
            part_in[chip, 1] = res[shard_in:]
            scatter.start([W_IN], chips=[chip])
            if n == 0:
                scatter.exchange([W_OUT])
                scatter.fold([SMALL])
            if n == 1:
                scatter.send_far([W_IN])
                scatter.fold([W_OUT])
                scatter.finish([SMALL])
                stot_ref[mine] = own_small[...]
                gather.send_mine()
            if n == 2:
                scatter.send_near(ACROSS_X, [W_IN])
                gather.relay()
            if n == 3:
                scatter.send_near(ACROSS_Y, [W_IN])
        scatter.fold([W_IN])
        scatter.finish([W_OUT])
        _wait_direct(dmod_copies)
        update_ada()
        gather.pass_on()
        gather.wait_rest()
        _wait_direct(loss_copies)
        total = loss_all[0]
        for j in range(1, N_DEV):
            total = total + loss_all[j]
        loss_ref[...] = total
        scatter.finish([W_IN])
        gather.wait_sends()
        scatter.wait_sends()

    buffers = _ChipReduceScatter.buffers
    comm_scratch = (buffers(shard_in, D_MODEL, BF16, staged=False) + buffers(shard_out, D_MODEL, BF16, staged=False)
                    + buffers(shard_small, 128, F32))
    comm_scratch += [pltpu.SemaphoreType.DMA((7,))] + _ChipReduceScatter.semaphores(3)
    comm_scratch += [pltpu.SemaphoreType.DMA((n,)) for n in (7, 7, GATHER_SEMS, GATHER_SEMS, 7, 7)]
    comm_scratch += [pltpu.SemaphoreType.DMA((2, 3)), pltpu.SemaphoreType.DMA((2, 4))]
    return pl.pallas_call(
        body,
        name="grad_tail",
        in_specs=[ANY] * 5 + [VMEM, VMEM] + [ANY] * 3 + [VMEM] * 4,
        out_specs=[VMEM] * 4 + [ANY] * 4 + [VMEM] * 4,
        out_shape=[jax.ShapeDtypeStruct((shard_in, D_MODEL), F32), jax.ShapeDtypeStruct((shard_out, D_MODEL), F32),
                   jax.ShapeDtypeStruct((N_DEV, shard_small, 128), F32), jax.ShapeDtypeStruct((8, 128), F32)]
                  + [jax.ShapeDtypeStruct(w_ada.shape, F32)] * 4 + [jax.ShapeDtypeStruct(b_ada.shape, F32)] * 4,
        scratch_shapes=[pltpu.VMEM((4, SEQ, 2 * shard_in), BF16), pltpu.VMEM(h.shape, BF16), pltpu.VMEM(cat.shape, BF16),
                        pltpu.VMEM(dy.shape, BF16), pltpu.VMEM((4, 2, shard_in, D_MODEL), BF16),
                        pltpu.VMEM((4, 2, shard_out, D_MODEL), BF16), pltpu.VMEM((shard_small, 128), F32),
                        pltpu.VMEM((8, 128), F32), pltpu.VMEM((N_DEV, 8, 128), F32),
                        pltpu.VMEM((N_DEV,) + dmod8.shape, F32), pltpu.VMEM((2, 3, ADA_CHUNK, cols), F32),
                        pltpu.VMEM((2, 4, ADA_CHUNK, cols), F32)] + comm_scratch,
        compiler_params=pltpu.CompilerParams(vmem_limit_bytes=VMEM_LIMIT),
    )(dproj, h, cat, dy, small, dmod8, loss_lanes, w_ada, m_ada, v_ada, act_t, b_ada, m_bada, v_bada)


SMALL_NAMES = ("w_pool", "w_sgu", "pool_scale", "sgu_ln_g", "sgu_ln_b", "b_sgu", "ln_g", "ln_b")
SMALL_ROWS = (512, 512, 4, 4, 4, 4, 8, 8)


def _adamw_small(g_packed, ws, ms, vs, name):
    n = len(SMALL_NAMES)

    def body(g_ref, *refs):
        w_refs, m_refs, v_refs = refs[:n], refs[n:2 * n], refs[2 * n:3 * n]
        outs = refs[3 * n:]

        def update(p, at, g):
            delta, new_m, new_v = _adamw_math(w_refs[p][at], g, m_refs[p][at], v_refs[p][at])
            outs[p][at] = g
            outs[n + p][at] = delta
            outs[2 * n + p][at] = new_m
            outs[3 * n + p][at] = new_v

        row = 0
        for p, r in enumerate(SMALL_ROWS):
            shape = ws[p].shape
            for layer in range(DEPTH):
                first = layer * PACK_ROWS + row
                if len(shape) == 4:
                    for k in range(shape[1]):
                        update(p, (layer, k), g_ref[first + k * shape[2]:first + (k + 1) * shape[2], :])
                elif len(shape) == 3:
                    update(p, (layer,), g_ref[first:first + r, :])
                else:
                    g = jnp.concatenate([g_ref[first + k:first + k + 1, :] for k in range(r)], axis=1)
                    update(p, (slice(layer, layer + 1), slice(None)), g)
            row += r

    res = pl.pallas_call(
        body,
        name=name,
        out_shape=[jax.ShapeDtypeStruct(w.shape, F32) for w in ws] * 4,
        compiler_params=pltpu.CompilerParams(vmem_limit_bytes=VMEM_LIMIT),
    )(g_packed, *ws, *ms, *vs)
    return res[:n], res[n:2 * n], res[2 * n:3 * n], res[3 * n:]


def kernel(x, c, w_ada, b_ada, w_in, w_pool, pool_scale, sgu_ln_g, sgu_ln_b, w_sgu, b_sgu, w_out, ln_g, ln_b, loss_target, m_w_ada, m_b_ada, m_w_in, m_w_pool, m_pool_scale, m_sgu_ln_g, m_sgu_ln_b, m_w_sgu, m_b_sgu, m_w_out, m_ln_g, m_ln_b, v_w_ada, v_b_ada, v_w_in, v_w_pool, v_pool_scale, v_sgu_ln_g, v_sgu_ln_b, v_w_sgu, v_b_sgu, v_w_out, v_ln_g, v_ln_b):
    small_w = dict(w_pool=w_pool, w_sgu=w_sgu, pool_scale=pool_scale, sgu_ln_g=sgu_ln_g, sgu_ln_b=sgu_ln_b,
                   b_sgu=b_sgu, ln_g=ln_g, ln_b=ln_b)
    small_m = dict(w_pool=m_w_pool, w_sgu=m_w_sgu, pool_scale=m_pool_scale, sgu_ln_g=m_sgu_ln_g,
                   sgu_ln_b=m_sgu_ln_b, b_sgu=m_b_sgu, ln_g=m_ln_g, ln_b=m_ln_b)
    small_v = dict(w_pool=v_w_pool, w_sgu=v_w_sgu, pool_scale=v_pool_scale, sgu_ln_g=v_sgu_ln_g,
                   sgu_ln_b=v_sgu_ln_b, b_sgu=v_b_sgu, ln_g=v_ln_g, ln_b=v_ln_b)

    wint_loc = jnp.transpose(w_in, (0, 2, 1)).astype(BF16)
    wout_loc = w_out.astype(BF16)
    small = (w_pool, pool_scale, sgu_ln_g, sgu_ln_b, w_sgu, jnp.transpose(b_sgu, (0, 2, 1)))
    out, y, cdf, act_all, mod, proj, wint0, wout0, wint1, wout1 = _first_layer(
        x[0], c, w_ada, b_ada, [wint_loc[0], wout_loc[0]], [wint_loc[1], wout_loc[1]], small, ln_g, ln_b)
    w_int, w_outf = [wint0, wint1.reshape(D_IN, D_MODEL)], [wout0, wout1.reshape(D_MODEL, D_MODEL)]
    acts, cur = [(x[0], proj, y, cdf)], out

    for l in range(1, DEPTH):
        out, proj, y, cdf = _layer_forward(l, cur, mod, w_int[l], w_outf[l], small, ln_g, ln_b, f"layer_fwd_{l}")
        acts.append((cur, proj, y, cdf))
        cur = out

    shard_in, shard_out = D_IN // N_DEV, D_MODEL // N_DEV
    a, b = cur, loss_target[0]
    loss_lanes, carry, pending = None, (), []
    g_w_in_t, g_w_out = [None] * DEPTH, [None] * DEPTH
    for l in reversed(range(DEPTH)):
        dx, dproj, h, cat, dy, small_grads, dmod8, lanes, *shards = _layer_backward(
            l, a, b, *acts[l], mod, w_int[l], w_outf[l], small, ln_g, l == DEPTH - 1, f"layer_bwd_{l}",
            carry=carry, reduce=pending)
        if shards:
            g_w_in_t[l + 1], g_w_out[l + 1] = shards
        if l == DEPTH - 1:
            loss_lanes = lanes
        if l > 0:
            pending = [_grad_matmul(dproj, h, 640, f"grad_w_in_{l}").reshape(4, 2, shard_in, D_MODEL),
                       _grad_matmul(cat, dy, 512, f"grad_w_out_{l}").reshape(4, 2, shard_out, D_MODEL)]
        carry = (small_grads, dmod8)
        a = b = dx
    grad_x = a[None]

    (g_w_in_t[0], g_w_out[0], small_tot, loss_tile, g_w_ada, d_w_ada, nm_w_ada, nv_w_ada,
     g_b_ada, d_b_ada, nm_b_ada, nv_b_ada) = _grad_tail(
        dproj, h, cat, dy, small_grads.reshape(4, 2, DEPTH * PACK_ROWS // N_DEV, 128), dmod8, loss_lanes,
        w_ada, m_w_ada, v_w_ada, jnp.transpose(act_all), b_ada, m_b_ada, v_b_ada)
    loss = loss_tile[0, 0]

    flat = lambda t: t.reshape(-1, t.shape[-1])
    to_t = lambda t: flat(jnp.transpose(t, (0, 2, 1)))
    from_t = lambda t: jnp.transpose(t.reshape(DEPTH, shard_in, D_MODEL), (0, 2, 1))
    g_w_in, d_w_in, nm_w_in, nv_w_in = [from_t(t) for t in _adamw(to_t(w_in), g_w_in_t, to_t(m_w_in), to_t(v_w_in),
                                                                  shard_in // 2, "adamw_w_in")]
    gwout, d_w_out, nm_w_out, nv_w_out = [t.reshape(w_out.shape) for t in _adamw(
        flat(w_out), g_w_out, flat(m_w_out), flat(v_w_out), shard_out, "adamw_w_out")]
    small_out = _adamw_small(small_tot.reshape(DEPTH * PACK_ROWS, 128), [small_w[n] for n in SMALL_NAMES],
                             [small_m[n] for n in SMALL_NAMES], [small_v[n] for n in SMALL_NAMES], "adamw_small")
    gs, ds, ms, vs = [dict(zip(SMALL_NAMES, group)) for group in small_out]

    def ordered(w_ada_, b_ada_, w_in_, small, w_out_):
        return (w_ada_, b_ada_, w_in_, small["w_pool"], small["pool_scale"], small["sgu_ln_g"], small["sgu_ln_b"],
                small["w_sgu"], small["b_sgu"], w_out_, small["ln_g"], small["ln_b"])

    return (loss, grad_x,
            *ordered(g_w_ada, g_b_ada, g_w_in, gs, gwout),
            *ordered(d_w_ada, d_b_ada, d_w_in, ds, d_w_out),
            *ordered(nm_w_ada, nm_b_ada, nm_w_in, ms, nm_w_out),
            *ordered(nv_w_ada, nv_b_ada, nv_w_in, vs, nv_w_out))
```

```python
import jax
import jax.numpy as jnp
from jax import lax
from jax.experimental import pallas as pl
from jax.experimental.pallas import tpu as pltpu

F32 = jnp.float32
BF16 = jnp.bfloat16

D_MODEL = 1024
SEQ = 2048
DEPTH = 2
D_POOL = 512
D_SGU = 512
D_IN = 2560
N_GROUPS = 4
GROUP = 128
N_HEADS = 4
HEAD = 128
CHUNK = 128
WINDOWS = (2, 4, 8, 16)
ALPHA = (2.0 * DEPTH) ** 0.25
LN_EPS = 1e-5
N_DEV = 8

ADAM_LR = 0.001
ADAM_B1 = 0.9
ADAM_B2 = 0.999
ADAM_EPS = 1e-08
ADAM_WD = 0.01
ADAM_STEP = 10

TM = 256
HALO = 16
N_TILES = SEQ // TM
VMEM_LIMIT = 60 * 1024 * 1024

ROW_WPOOL = 0
ROW_WSGU = 512
ROW_PSCALE = 1024
ROW_SLNG = 1028
ROW_SLNB = 1032
ROW_BSGU = 1036
ROW_LNG = 1040
ROW_LNB = 1048
PACK_ROWS = 1088
DMOD_COLS = DEPTH * 3 * D_MODEL // 8

SQRT_HALF = 0.7071067811865476
INV_SQRT_2PI = 0.3989422804014327


def _ln(x):
    mu = jnp.mean(x, axis=-1, keepdims=True)
    xc = x - mu
    var = jnp.mean(xc * xc, axis=-1, keepdims=True)
    rstd = lax.rsqrt(var + LN_EPS)
    return xc * rstd, rstd


def _ln_bwd(dxn, xn, rstd):
    m1 = jnp.mean(dxn, axis=-1, keepdims=True)
    m2 = jnp.mean(dxn * xn, axis=-1, keepdims=True)
    return rstd * (dxn - m1 - xn * m2)


def _normal_cdf(x):
    return 0.5 * (1.0 + lax.erf(x * SQRT_HALF))


def _gelu_parts(x, cdf, with_grad):
    if not with_grad:
        return x * cdf, None
    return x * cdf, cdf + x * (INV_SQRT_2PI * jnp.exp(-0.5 * x * x))


def _silu_parts(x):
    s = jax.nn.sigmoid(x)
    return x * s, s * (1.0 + x * (1.0 - s))


def _dot(a, b):
    return lax.dot_general(a, b, (((1,), (0,)), ((), ())), preferred_element_type=F32)


def _dot_nt(a, b):
    return lax.dot_general(a, b, (((1,), (1,)), ((), ())), preferred_element_type=F32)


def _dot_tn(a, b):
    return lax.dot_general(a, b, (((0,), (0,)), ((), ())), preferred_element_type=F32)


def _row_index(tile):
    return tile * TM + lax.broadcasted_iota(jnp.int32, (TM, 1), 0)


def _window_sums(ext, forward):
    n = TM + HALO
    cur = ext
    outs = []
    for g in range(N_GROUPS):
        step = 1 << g
        cur = cur + pltpu.roll(cur, step if forward else n - step, 0)
        rows = cur[HALO:, :GROUP] if forward else cur[:TM, :GROUP]
        outs.append(rows)
        cur = cur[:, GROUP:] if g + 1 < N_GROUPS else None
    return outs


def _inverse_counts(rows):
    return [1.0 / jnp.minimum(rows + 1, w).astype(F32) for w in WINDOWS]


def _tril_bf16(w):
    t = lax.broadcasted_iota(jnp.int32, (CHUNK, CHUNK), 0)
    s = lax.broadcasted_iota(jnp.int32, (CHUNK, CHUNK), 1)
    return jnp.where(t >= s, w, 0.0).astype(BF16)


class _MixWeights:
    def __init__(self, layer, wpool_ref, pscale_ref, slng_ref, slnb_ref, wsgu_ref, bsgut_ref):
        self.layer = layer
        self.wpool_ref, self.pscale_ref, self.slng_ref, self.slnb_ref = wpool_ref, pscale_ref, slng_ref, slnb_ref
        self.wsgu_ref, self.bsgut_ref = wsgu_ref, bsgut_ref

    def pool(self, g):
        return self.wpool_ref[self.layer, g].astype(BF16)

    def pool_scale(self, g):
        return self.pscale_ref[self.layer:self.layer + 1, g * GROUP:(g + 1) * GROUP]

    def ln_gain(self, h):
        return self.slng_ref[self.layer, h:h + 1, :]

    def ln_bias(self, h):
        return self.slnb_ref[self.layer, h:h + 1, :]

    def mix(self, h):
        return _tril_bf16(self.wsgu_ref[self.layer, h])

    def mix_bias(self, h):
        return self.bsgut_ref[self.layer, :, h:h + 1]


SMALL_SPECS = ((DEPTH, N_GROUPS, GROUP, GROUP), (DEPTH, D_POOL), (DEPTH, N_HEADS, HEAD), (DEPTH, N_HEADS, HEAD),
               (DEPTH, N_HEADS, CHUNK, CHUNK), (DEPTH, CHUNK, N_HEADS))


def _mix_forward(proj, halo, tile, w, cdf=None):
    keep = cdf is not None
    rows = _row_index(tile)
    inv_counts = _inverse_counts(rows)
    xa = proj[:, 0:D_POOL]
    ga = proj[:, D_POOL:2 * D_POOL]
    sums = _window_sums(jnp.concatenate([halo, xa], axis=0), True)
    ga_act, ga_grad = _silu_parts(ga)
    pooled, pw, ya = [], [], []
    for g in range(N_GROUPS):
        sl = slice(g * GROUP, (g + 1) * GROUP)
        p = (sums[g] * inv_counts[g] - xa[:, sl]).astype(BF16)
        q = _dot(p, w.pool(g))
        pooled.append(p)
        pw.append(q)
        ya.append(q * w.pool_scale(g) * ga_act[:, sl])

    u = proj[:, 2 * D_POOL:2 * D_POOL + D_SGU]
    v = proj[:, 2 * D_POOL + D_SGU:2 * D_POOL + 2 * D_SGU]
    gb = proj[:, 2 * D_POOL + 2 * D_SGU:]
    gb_act, gb_grad = _silu_parts(gb)
    if cdf is None:
        cdf = jnp.concatenate([_normal_cdf(u), _normal_cdf(v)], axis=1)
    u_act, u_grad = _gelu_parts(u, cdf[:, :D_SGU], keep)
    v_act, v_grad = _gelu_parts(v, cdf[:, D_SGU:], keep)
    vn, vrstd, vln, mixed, yb = [], [], [], [], []
    for h in range(N_HEADS):
        sl = slice(h * HEAD, (h + 1) * HEAD)
        n_h, r_h = _ln(v_act[:, sl])
        l_h = (n_h * w.ln_gain(h) + w.ln_bias(h)).astype(BF16)
        w_h = w.mix(h)
        bias = w.mix_bias(h)
        m_h = jnp.concatenate(
            [_dot(w_h, l_h[k * CHUNK:(k + 1) * CHUNK]) + bias for k in range(TM // CHUNK)], axis=0)
        vn.append(n_h)
        vrstd.append(r_h)
        vln.append(l_h)
        mixed.append(m_h)
        yb.append(u_act[:, sl] * m_h * gb_act[:, sl])
    cat = jnp.concatenate(ya + yb, axis=1)
    if not keep:
        return cat, cdf
    return cat, dict(inv_counts=inv_counts, ga_act=ga_act, ga_grad=ga_grad, pooled=pooled, pw=pw, u_grad=u_grad,
                     v_grad=v_grad, u_act=u_act, gb_act=gb_act, gb_grad=gb_grad, vn=vn, vrstd=vrstd, vln=vln,
                     mixed=mixed)


def _const_spec(shape):
    nd = len(shape)
    return pl.BlockSpec(shape, lambda i: (0,) * nd)


VEC_LNG, VEC_LNB, VEC_POOL, VEC_SGU, VEC_SHIFT, VEC_SCALE, VEC_GATE, VEC_LOSS = range(8)


def _layer_backward(layer, a, b, x, proj, y, cdf, mod, w_int, w_outf, small, ln_g, is_last, name, carry=(),
                    reduce=()):
    n_red, n_carry = len(reduce), len(carry)
    base = layer * PACK_ROWS

    def body(a_ref, b_ref, x_ref, proj_ref, prev_ref, y_ref, cdf_ref, mod_ref, wint_ref, wout_ref, wpool_ref,
             pscale_ref, slng_ref, slnb_ref, wsgu_ref, bsgut_ref, lng_ref, *rest):
        weights = _MixWeights(layer, wpool_ref, pscale_ref, slng_ref, slnb_ref, wsgu_ref, bsgut_ref)
        carry_refs, rest = rest[:n_carry], rest[n_carry:]
        part_refs, rest = rest[:n_red], rest[n_red:]
        dx_ref, dproj_ref, h_ref, cat_ref, dy_ref, small_ref, dmod_ref, loss_ref = rest[:8]
        shard_refs, rest = rest[8:8 + n_red], rest[8 + n_red:]
        vec_ref, dmix_ref, halo_ref = rest[:3]
        step = pl.program_id(0)
        tile = N_TILES - 1 - step

        def scatter():
            bufs, sems = rest[3:3 + 5 * n_red], rest[3 + 5 * n_red:]
            arrays = [dict(part=part_refs[n], out=shard_refs[n], staged=True, stage=bufs[5 * n], sib=bufs[5 * n + 1],
                           snd=bufs[5 * n + 2], rcv=bufs[5 * n + 3], relay=bufs[5 * n + 4]) for n in range(n_red)]
            return _ChipReduceScatter(arrays, *sems)

        @pl.when(step == 0)
        def _():
            small_ref[...] = jnp.zeros_like(small_ref)
            dmod_ref[...] = jnp.zeros_like(dmod_ref)
            vec_ref[...] = jnp.zeros_like(vec_ref)
            dmix_ref[...] = jnp.zeros_like(dmix_ref)
            halo_ref[...] = jnp.zeros_like(halo_ref)
            if n_red:
                scatter().start()

        if n_red:
            @pl.when(step == 1)
            def _():
                scatter().exchange()

            @pl.when(step == N_TILES // 2)
            def _():
                scatter().fold()

        def acc(row, lo, val):
            hi = lo + val.shape[1]
            vec_ref[row:row + 1, lo:hi] += jnp.sum(val, axis=0, keepdims=True)

        xt = x_ref[...]
        yt = y_ref[...]
        shift = mod_ref[layer:layer + 1, 0:D_MODEL]
        scale = mod_ref[layer:layer + 1, D_MODEL:2 * D_MODEL]
        gate = mod_ref[layer:layer + 1, 2 * D_MODEL:]
        ln_gain = lng_ref[layer:layer + 1, :]

        zn, zrstd = _ln(ALPHA * xt + gate * yt)
        if is_last:
            diff = a_ref[...] - b_ref[...]
            acc(VEC_LOSS, 0, diff * diff)
            dout = diff * (1.0 / D_MODEL)
        else:
            dout = a_ref[...]
        acc(VEC_LNG, 0, dout * zn)
        acc(VEC_LNB, 0, dout)
        dz = _ln_bwd(dout * ln_gain, zn, zrstd)
        acc(VEC_GATE, 0, dz * yt)
        dy = (dz * gate).astype(BF16)
        dy_ref[...] = dy
        dcat = _dot_nt(dy, wout_ref[...])

        proj = proj_ref[...]
        prev = jnp.where(tile > 0, prev_ref[...], 0.0)
        cat, k = _mix_forward(proj, prev, tile, weights, cdf_ref[...])
        cat_ref[...] = cat.astype(BF16)

        dga, dq = [], []
        for g in range(N_GROUPS):
            sl = slice(g * GROUP, (g + 1) * GROUP)
            pscale = weights.pool_scale(g)
            dya = dcat[:, sl]
            dyp = dya * k["ga_act"][:, sl]
            dga.append(dya * k["pw"][g] * pscale * k["ga_grad"][:, sl])
            acc(VEC_POOL, g * GROUP, dyp * k["pw"][g])
            dpw = (dyp * pscale).astype(BF16)
            rows = pl.ds(base + ROW_WPOOL + g * GROUP, GROUP)
            small_ref[rows, :] += _dot_tn(k["pooled"][g], dpw)
            dq.append(_dot_nt(dpw, weights.pool(g)))
        dpooled = jnp.concatenate(dq, axis=1)
        scaled = jnp.concatenate([dq[g] * k["inv_counts"][g] for g in range(N_GROUPS)], axis=1)
        sums = _window_sums(jnp.concatenate([scaled, halo_ref[...]], axis=0), False)
        halo_ref[...] = scaled[0:HALO]
        dxa = jnp.concatenate(sums, axis=1) - dpooled

        du, dv, dgb = [], [], []
        for h in range(N_HEADS):
            sl = slice(h * HEAD, (h + 1) * HEAD)
            dyb = dcat[:, D_POOL + h * HEAD:D_POOL + (h + 1) * HEAD]
            m_h = k["mixed"][h]
            ug = k["u_act"][:, sl] * dyb
            du.append(dyb * m_h * k["gb_act"][:, sl] * k["u_grad"][:, sl])
            dgb.append(ug * m_h * k["gb_grad"][:, sl])
            dmixed = ug * k["gb_act"][:, sl]
            dmixed_bf = dmixed.astype(BF16)
            w_h = weights.mix(h)
            dvln_parts = []
            dmix_sum = dmix_ref[h]
            wsgu_rows = pl.ds(base + ROW_WSGU + h * CHUNK, CHUNK)
            dws = small_ref[wsgu_rows, :]
            for c in range(TM // CHUNK):
                cs = slice(c * CHUNK, (c + 1) * CHUNK)
                dmix_sum = dmix_sum + dmixed[cs]
                dws = dws + _dot_nt(dmixed_bf[cs], k["vln"][h][cs])
                dvln_parts.append(_dot_tn(w_h, dmixed_bf[cs]))
            dmix_ref[h] = dmix_sum
            small_ref[wsgu_rows, :] = dws
            dvln = jnp.concatenate(dvln_parts, axis=0)
            acc(VEC_SGU, h * HEAD, dvln * k["vn"][h])
            acc(VEC_SGU, D_SGU + h * HEAD, dvln)
            dvv = _ln_bwd(dvln * weights.ln_gain(h), k["vn"][h], k["vrstd"][h])
            dv.append(dvv * k["v_grad"][:, sl])

        dproj = jnp.concatenate([dxa] + dga + du + dv + dgb, axis=1).astype(BF16)
        dproj_ref[...] = dproj
        dh = _dot(dproj, wint_ref[...])

        xn, xrstd = _ln(xt)
        h_ref[...] = (xn * (1.0 + scale) + shift).astype(BF16)
        acc(VEC_SCALE, 0, dh * xn)
        acc(VEC_SHIFT, 0, dh)
        dx_ref[...] = _ln_bwd(dh * (1.0 + scale), xn, xrstd) + ALPHA * dz

        @pl.when(step == N_TILES - 1)
        def _():
            def put(row0, vec_row, lo, n):
                for r in range(n):
                    small_ref[base + row0 + r:base + row0 + r + 1, :] = (
                        vec_ref[vec_row:vec_row + 1, lo + r * 128:lo + (r + 1) * 128])

            put(ROW_PSCALE, VEC_POOL, 0, 4)
            put(ROW_SLNG, VEC_SGU, 0, 4)
            put(ROW_SLNB, VEC_SGU, D_SGU, 4)
            put(ROW_LNG, VEC_LNG, 0, 8)
            put(ROW_LNB, VEC_LNB, 0, 8)
            ones = jnp.ones((8, HEAD), F32)
            t = lax.broadcasted_iota(jnp.int32, (CHUNK, CHUNK), 0)
            s = lax.broadcasted_iota(jnp.int32, (CHUNK, CHUNK), 1)
            for h in range(N_HEADS):
                bias_rows = lax.dot_general(ones, dmix_ref[h], (((1,), (1,)), ((), ())),
                                            preferred_element_type=F32, precision=lax.Precision.HIGHEST)
                small_ref[base + ROW_BSGU + h:base + ROW_BSGU + h + 1, :] = bias_rows[0:1]
                rows = pl.ds(base + ROW_WSGU + h * CHUNK, CHUNK)
                small_ref[rows, :] = jnp.where(t >= s, small_ref[rows, :], 0.0)
            pieces = ((0, VEC_SHIFT, 0, 768),
                      (1, VEC_SHIFT, 768, 256), (1, VEC_SCALE, 0, 512),
                      (2, VEC_SCALE, 512, 512), (2, VEC_GATE, 0, 256),
                      (3, VEC_GATE, 256, 768))
            filled = [0] * 4
            for q, vec_row, lo, n in pieces:
                row = 4 * layer + q
                dmod_ref[row:row + 1, filled[q]:filled[q] + n] = vec_ref[vec_row:vec_row + 1, lo:lo + n]
                filled[q] += n
            if n_carry:
                for other in range(layer + 1, DEPTH):
                    rows = pl.ds(other * PACK_ROWS, PACK_ROWS)
                    small_ref[rows, :] = carry_refs[0][rows, :]
                    dmod_ref[4 * other:4 * other + 4, :] = carry_refs[1][4 * other:4 * other + 4, :]
            loss_ref[...] = vec_ref[VEC_LOSS:VEC_LOSS + 1, :]
            if n_red:
                scatter().finish()
                scatter().wait_sends()

    rev = lambda w: pl.BlockSpec((TM, w), lambda i: (N_TILES - 1 - i, 0))
    prev_spec = pl.BlockSpec(
        (HALO, D_POOL), lambda i: (jnp.maximum((N_TILES - 1 - i) * (TM // HALO) - 1, 0), 0))
    comm_scratch = []
    for p in reduce:
        comm_scratch += _ChipReduceScatter.buffers(p.shape[2], p.shape[3], p.dtype)
    if n_red:
        comm_scratch += _ChipReduceScatter.semaphores(n_red)
    return pl.pallas_call(
        body,
        name=name,
        grid=(N_TILES,),
        in_specs=[rev(D_MODEL), rev(D_MODEL) if is_last else pl.BlockSpec((TM, D_MODEL), lambda i: (0, 0)),
                  rev(D_MODEL), rev(D_IN), prev_spec, rev(D_MODEL), rev(2 * D_SGU),
                  _const_spec((DEPTH, 3 * D_MODEL)), _const_spec((D_IN, D_MODEL)), _const_spec((D_MODEL, D_MODEL))]
                 + [_const_spec(s) for s in SMALL_SPECS] + [_const_spec((DEPTH, D_MODEL))]
                 + [_const_spec(c.shape) for c in carry] + [ANY] * n_red,
        out_specs=[rev(D_MODEL), rev(D_IN), rev(D_MODEL), rev(D_MODEL), rev(D_MODEL),
                   _const_spec((DEPTH * PACK_ROWS, 128)), _const_spec((8, DMOD_COLS)), _const_spec((1, D_MODEL))]
                  + [_const_spec(p.shape[2:]) for p in reduce],
        out_shape=[jax.ShapeDtypeStruct((SEQ, D_MODEL), F32), jax.ShapeDtypeStruct((SEQ, D_IN), BF16),
                   jax.ShapeDtypeStruct((SEQ, D_MODEL), BF16), jax.ShapeDtypeStruct((SEQ, D_MODEL), BF16),
                   jax.ShapeDtypeStruct((SEQ, D_MODEL), BF16), jax.ShapeDtypeStruct((DEPTH * PACK_ROWS, 128), F32),
                   jax.ShapeDtypeStruct((8, DMOD_COLS), F32), jax.ShapeDtypeStruct((1, D_MODEL), F32)]
                  + [jax.ShapeDtypeStruct(p.shape[2:], F32) for p in reduce],
        scratch_shapes=[pltpu.VMEM((8, D_MODEL), F32), pltpu.VMEM((N_HEADS, CHUNK, HEAD), F32),
                        pltpu.VMEM((HALO, D_POOL), F32)] + comm_scratch,
        compiler_params=pltpu.CompilerParams(dimension_semantics=("arbitrary",), vmem_limit_bytes=VMEM_LIMIT),
    )(a, b, x, proj, proj, y, cdf, mod, w_int, w_outf, *small, ln_g, *carry, *reduce)


def _grad_matmul(lhs, rhs, block_cols, name):
    m, n = lhs.shape[1], rhs.shape[1]

    def body(lhs_ref, rhs_ref, out_ref):
        out_ref[...] = _dot_tn(lhs_ref[...], rhs_ref[...]).astype(BF16)

    return pl.pallas_call(
        body,
        name=name,
        grid=(m // block_cols,),
        in_specs=[pl.BlockSpec((SEQ, block_cols), lambda j: (0, j)), pl.BlockSpec((SEQ, n), lambda j: (0, 0))],
        out_specs=pl.BlockSpec((block_cols, n), lambda j: (j, 0)),
        out_shape=jax.ShapeDtypeStruct((m, n), BF16),
        compiler_params=pltpu.CompilerParams(dimension_semantics=("arbitrary",), vmem_limit_bytes=VMEM_LIMIT),
    )(lhs, rhs)


def _adamw_math(w, g, m, v):
    m = ADAM_B1 * m + (1.0 - ADAM_B1) * g
    v = ADAM_B2 * v + (1.0 - ADAM_B2) * (g * g)
    m_hat = m / (1.0 - ADAM_B1 ** ADAM_STEP)
    v_hat = v / (1.0 - ADAM_B2 ** ADAM_STEP)
    delta = -ADAM_LR * (m_hat / (jnp.sqrt(v_hat) + ADAM_EPS) + ADAM_WD * w)
    return delta, m, v


def _adamw(w, grads, m, v, block_rows, name):
    rows, cols = grads[0].shape
    blocks = rows // block_rows

    def body(w_ref, m_ref, v_ref, *rest):
        g_refs, (g_ref, d_ref, nm_ref, nv_ref) = rest[:DEPTH], rest[DEPTH:]
        for layer in range(DEPTH):
            @pl.when(pl.program_id(0) == layer)
            def _():
                g = g_refs[layer][...]
                g_ref[...] = g
                d_ref[...], nm_ref[...], nv_ref[...] = _adamw_math(w_ref[...], g, m_ref[...], v_ref[...])

    def grad_spec(layer):
        return pl.BlockSpec((block_rows, cols),
                            lambda l, i: (jnp.where(l == layer, i, jnp.where(l < layer, 0, blocks - 1)), 0))

    spec = pl.BlockSpec((block_rows, cols), lambda l, i: (l * blocks + i, 0))
    return pl.pallas_call(
        body,
        name=name,
        grid=(DEPTH, blocks),
        in_specs=[spec] * 3 + [grad_spec(layer) for layer in range(DEPTH)],
        out_specs=[spec] * 4,
        out_shape=[jax.ShapeDtypeStruct(w.shape, F32)] * 4,
        compiler_params=pltpu.CompilerParams(dimension_semantics=("arbitrary", "arbitrary"),
                                             vmem_limit_bytes=VMEM_LIMIT),
    )(w, m, v, *grads)


MESH = pl.DeviceIdType.MESH
SIBLING = 1
ANY = pl.BlockSpec(memory_space=pl.ANY)
VMEM = pl.BlockSpec(memory_space=pltpu.VMEM)


def _me():
    return lax.axis_index("x"), lax.axis_index("y"), lax.axis_index("c")


def _peer(r):
    x, y, c = _me()
    return (1 - x if r & 4 else x, 1 - y if r & 2 else y, 1 - c if r & 1 else c)


def _index(dev):
    return 4 * dev[0] + 2 * dev[1] + dev[2]


def _remote(src, dst, send_sem, recv_sem, dev):
    return pltpu.make_async_remote_copy(src_ref=src, dst_ref=dst, send_sem=send_sem, recv_sem=recv_sem,
                                        device_id=dev, device_id_type=MESH)


ACROSS_X, ACROSS_Y, ACROSS_BOTH = 4, 2, 6
GATHER_SEMS = 11


class _TwoLevelGather:
    def __init__(self, out, send_sems, recv_sems, src=None):
        self.out, self.send_sems, self.recv_sems, self.src = out, send_sems, recv_sems, src
        self.rows = (out.shape[0] // N_DEV) if len(out.shape) == 2 else out.shape[1]
        self.half = self.rows // 2

    def _slot(self, block):
        if len(self.out.shape) == 2:
            return self.out.at[pl.ds(pl.multiple_of(_index(block) * self.rows, self.rows), self.rows)]
        return self.out.at[_index(block)]

    def _copy(self, k, block, part, to, src=None):
        slot = self._slot(block)
        if part is not None:
            rows = pl.ds(part * self.half, self.half)
            slot = slot.at[rows]
            src = None if src is None else src.at[rows]
        return _remote(slot if src is None else src, slot, self.send_sems.at[k], self.recv_sems.at[k], to)

    def _mine(self):
        me = _me()
        src = self._slot(me) if self.src is None else self.src
        x, y = _peer(ACROSS_X), _peer(ACROSS_Y)
        return [self._copy(1, me, 0, x, src), self._copy(3, me, 1, y, src), self._copy(0, me, None, _peer(SIBLING), src),
                self._copy(2, me, 1, x, src), self._copy(4, me, 0, y, src)]

    def _relayed(self):
        return [self._copy(5, _peer(ACROSS_X), 0, _peer(ACROSS_Y)), self._copy(6, _peer(ACROSS_Y), 1, _peer(ACROSS_X))]

    def _passed(self):
        sib, far = _peer(SIBLING), _peer(ACROSS_BOTH)
        return [self._copy(7, _peer(ACROSS_X), None, sib), self._copy(8, _peer(ACROSS_Y), None, sib),
                self._copy(9, far, 0, sib), self._copy(10, far, 1, sib)]

    def _arrival(self, k, r, part):
        return self._copy(k, _peer(r), part, _me())

    def send_first(self):
        for cp in self._mine()[:3]:
            cp.start()

    def send_second(self):
        for cp in self._mine()[3:]:
            cp.start()

    def send_mine(self):
        self.send_first()
        self.send_second()

    def relay(self):
        relayed = self._relayed()
        self._arrival(1, ACROSS_X, 0).wait_recv()
        relayed[0].start()
        self._arrival(3, ACROSS_Y, 1).wait_recv()
        relayed[1].start()

    def pass_near(self):
        passed = self._passed()
        self._arrival(2, ACROSS_X, 1).wait_recv()
        passed[0].start()
        self._arrival(4, ACROSS_Y, 0).wait_recv()
        passed[1].start()

    def pass_far(self):
        passed = self._passed()
        self._arrival(5, ACROSS_BOTH, 0).wait_recv()
        passed[2].start()
        self._arrival(6, ACROSS_BOTH, 1).wait_recv()
        passed[3].start()

    def pass_on(self):
        self.pass_near()
        self.pass_far()

    def wait_sibling(self):
        self._arrival(0, SIBLING, None).wait_recv()

    def wait_passed(self, r):
        if r == ACROSS_BOTH:
            self._arrival(9, r ^ SIBLING, 0).wait_recv()
            self._arrival(10, r ^ SIBLING, 1).wait_recv()
        else:
            self._arrival(7 if r == ACROSS_X else 8, r ^ SIBLING, None).wait_recv()

    def wait_rest(self):
        self.wait_sibling()
        for r in (ACROSS_X, ACROSS_Y, ACROSS_BOTH):
            self.wait_passed(r)

    def wait_sends(self):
        for cp in self._mine() + self._relayed() + self._passed():
            cp.wait_send()


class _ChipReduceScatter:
    SLOTS = 6

    def __init__(self, arrays, l_sem, d_send, d_recv, i_send, i_recv):
        self.arrays = arrays
        self.l_sem, self.d_send, self.d_recv, self.i_send, self.i_recv = l_sem, d_send, d_recv, i_send, i_recv

    @staticmethod
    def buffers(rows, cols, dtype, staged=True):
        stage = [pltpu.VMEM((4, rows, cols), dtype)] if staged else []
        return stage + [pltpu.VMEM((4, rows, cols), dtype), pltpu.VMEM((3, rows, cols), dtype),
                        pltpu.VMEM((2, rows, cols), dtype), pltpu.VMEM((2, rows // 2, cols), dtype)]

    @classmethod
    def semaphores(cls, n):
        return [pltpu.SemaphoreType.DMA((n,)), pltpu.SemaphoreType.DMA((n, 4)), pltpu.SemaphoreType.DMA((n, 4)),
                pltpu.SemaphoreType.DMA((n, cls.SLOTS)), pltpu.SemaphoreType.DMA((n, cls.SLOTS))]

    def _pick(self, which):
        return list(enumerate(self.arrays)) if which is None else [(n, self.arrays[n]) for n in which]

    @staticmethod
    def _chip(r):
        dev = _me() if r is None else _peer(r)
        return 2 * dev[0] + dev[1]

    def _staging(self, which):
        c = _me()[2]
        return [pltpu.make_async_copy(a["part"].at[pl.ds(0, 4), c], a["stage"], self.l_sem.at[n])
                for n, a in self._pick(which) if a["staged"]]

    def _first(self, which, chip):
        other = 1 - _me()[2]
        return [_remote(a["part"].at[chip, other], a["sib"].at[chip], self.d_send.at[n, chip], self.d_recv.at[n, chip],
                        _peer(SIBLING)) for n, a in self._pick(which)]

    @staticmethod
    def _halves(a):
        half = a["rcv"].shape[1] // 2
        return pl.ds(0, half), pl.ds(half, half)

    def _hops(self, n, a):
        h0, h1 = self._halves(a)
        x, y = _peer(ACROSS_X), _peer(ACROSS_Y)
        snd, rcv, relay = a["snd"], a["rcv"], a["relay"]
        pairs = [(snd.at[2, h0], relay.at[0], x), (snd.at[2, h1], relay.at[1], y),
                 (snd.at[0, h0], rcv.at[0, h0], x), (snd.at[0, h1], rcv.at[0, h1], x),
                 (snd.at[1, h1], rcv.at[1, h1], y), (snd.at[1, h0], rcv.at[1, h0], y)]
        return [_remote(s, d, self.i_send.at[n, k], self.i_recv.at[n, k], to) for k, (s, d, to) in enumerate(pairs)]

    def _mine(self, a, chip, rows=None):
        src = a["stage"].at[chip] if a["staged"] else a["part"].at[chip, _me()[2]]
        mine, sib = (src[...], a["sib"][chip]) if rows is None else (src[rows, :], a["sib"][chip, rows, :])
        return mine.astype(F32) + sib.astype(F32)

    def start(self, which=None, chips=None):
        if chips is None:
            for cp in self._staging(which):
                cp.start()
        for chip in range(4) if chips is None else chips:
            for cp in self._first(which, chip):
                cp.start()

    def send_far(self, which=None):
        far = self._chip(ACROSS_BOTH)
        for cp in self._staging(which):
            cp.wait()
        for cp in self._first(which, far):
            cp.wait_recv()
        for n, a in self._pick(which):
            hops = self._hops(n, a)
            a["snd"][2] = self._mine(a, far).astype(a["snd"].dtype)
            hops[0].start()
            hops[1].start()

    def send_near(self, r, which=None):
        chip = self._chip(r)
        for cp in self._first(which, chip):
            cp.wait_recv()
        for n, a in self._pick(which):
            h0, h1 = self._halves(a)
            hops = self._hops(n, a)
            if r == ACROSS_X:
                a["snd"][0, h0, :] = self._mine(a, chip, h0).astype(a["snd"].dtype)
                hops[2].start()
            else:
                a["snd"][1, h1, :] = self._mine(a, chip, h1).astype(a["snd"].dtype)
                hops[4].start()

    def exchange(self, which=None):
        self.send_far(which)
        self.send_near(ACROSS_X, which)
        self.send_near(ACROSS_Y, which)

    def fold(self, which=None):
        across_x, across_y = self._chip(ACROSS_X), self._chip(ACROSS_Y)
        for n, a in self._pick(which):
            h0, h1 = self._halves(a)
            hops = self._hops(n, a)
            dtype = a["snd"].dtype
            hops[1].wait_recv()
            a["snd"][0, h1, :] = (self._mine(a, across_x, h1) + a["relay"][1].astype(F32)).astype(dtype)
            hops[3].start()
            hops[0].wait_recv()
            a["snd"][1, h0, :] = (self._mine(a, across_y, h0) + a["relay"][0].astype(F32)).astype(dtype)
            hops[5].start()

    def finish(self, which=None):
        home = self._chip(None)
        for cp in self._first(which, home):
            cp.wait_recv()
        for n, a in self._pick(which):
            hops = self._hops(n, a)
            a["out"][...] = self._mine(a, home)
            hops[2].wait_recv()
            hops[3].wait_recv()
            a["out"][...] += a["rcv"][0].astype(F32)
            hops[4].wait_recv()
            hops[5].wait_recv()
            a["out"][...] += a["rcv"][1].astype(F32)

    def wait_sends(self, which=None):
        for chip in range(4):
            for cp in self._first(which, chip):
                cp.wait_send()
        for n, a in self._pick(which):
            for cp in self._hops(n, a):
                cp.wait_send()


def _direct_exchange(src_of, dst_of, send_sems, recv_sems):
    me = _me()
    copies = [_remote(src_of(_peer(r)), dst_of(me), send_sems.at[r - 1], recv_sems.at[r - 1], _peer(r))
              for r in range(1, N_DEV)]
    for cp in copies:
        cp.start()
    return copies


def _wait_direct(copies):
    for cp in copies:
        cp.wait_recv()
    for cp in copies:
        cp.wait_send()


def _gathered_layer(layer, x, small, ln_g, ln_b, mine, name, mod=None, exchange=None):
    exchanging = exchange is not None
    cols = exchange[1].shape[2] if exchanging else 0
    shard = mine[0].shape[0]
    pair = 2 * shard
    n_lead = 4 if exchanging else 2

    def body(*refs):
        x_ref = refs[0]
        lead, refs = refs[1:n_lead], refs[n_lead:]
        (wpool_ref, pscale_ref, slng_ref, slnb_ref, wsgu_ref, bsgut_ref, lng_ref, lnb_ref, wint_hbm, wout_hbm,
         out_ref, y_ref, cdf_ref, proj_hbm, wint_keep, wout_keep), refs = refs[:16], refs[16:]
        if exchanging:
            c_ref, wada_ref, bada_ref = lead
            (acts_ref, mod_ref), refs = refs[:2], refs[2:]
        else:
            mod_ref, = lead
        (wint_v, wout_v, h_buf, proj_blk, proj_tile, halo_ref, w_send, w_recv, w_local, p_sems,
         t_sems), refs = refs[:11], refs[11:]
        weights = _MixWeights(layer, wpool_ref, pscale_ref, slng_ref, slnb_ref, wsgu_ref, bsgut_ref)
        tile = pl.program_id(0)

        def gathers():
            return (_TwoLevelGather(wint_v, w_send.at[0], w_recv.at[0], src=wint_hbm),
                    _TwoLevelGather(wout_v, w_send.at[1], w_recv.at[1], src=wout_hbm))

        def keeps():
            return [pltpu.make_async_copy(wint_v, wint_keep, w_local.at[2]),
                    pltpu.make_async_copy(wout_v, wout_keep, w_local.at[3])]

        def tile_read(t):
            slot = t % 2
            return pltpu.make_async_copy(proj_hbm.at[pl.ds(pl.multiple_of(t * TM, TM), TM)], proj_tile.at[slot],
                                         t_sems.at[slot])

        @pl.when(tile == 0)
        def _():
            me = _me()
            halo_ref[...] = jnp.zeros_like(halo_ref)
            gather_in, gather_out = gathers()
            own_in = pltpu.make_async_copy(wint_hbm, gather_in._slot(me), w_local.at[0])
            own_out = pltpu.make_async_copy(wout_hbm, gather_out._slot(me), w_local.at[1])

            if exchanging:
                act_all, act_src, part, mod_recv, a_send, a_recv, m_send, m_recv = refs
                mine_index = _index(me)
                cval = c_ref[...]
                act_src[...] = jnp.zeros_like(act_src)
                act_src[0:1, :] = cval * jax.nn.sigmoid(cval)
                act_all[mine_index] = act_src[...]
                act_copies = _direct_exchange(lambda p: act_src, lambda m: act_all.at[_index(m)], a_send, a_recv)

            own_in.start()
            own_out.start()
            gather_in.send_first()

            if exchanging:
                _wait_direct(act_copies)
                acts = jnp.concatenate([act_all[j, 0:1, :] for j in range(N_DEV)], axis=0)
                acts_ref[...] = acts
                part[...] = jnp.zeros_like(part)
                for l in range(DEPTH):
                    res = lax.dot_general(acts, wada_ref[l], (((1,), (0,)), ((), ())), preferred_element_type=F32,
                                          precision=lax.Precision.HIGHEST)
                    for b in range(N_DEV):
                        part[b, l:l + 1, :] = res[b:b + 1, :]
                mod_recv[mine_index] = part[mine_index]
                mod_copies = _direct_exchange(lambda p: part.at[_index(p)], lambda m: mod_recv.at[_index(m)],
                                              m_send, m_recv)
            gather_in.send_second()
            gather_out.send_mine()

            if exchanging:
                _wait_direct(mod_copies)
                for l in range(DEPTH):
                    for j in range(N_DEV):
                        sl = slice(j * cols, (j + 1) * cols)
                        mod_ref[l:l + 1, sl] = mod_recv[j, l:l + 1, :] + bada_ref[l:l + 1, sl]
            shift = mod_ref[layer:layer + 1, 0:D_MODEL]
            scale = mod_ref[layer:layer + 1, D_MODEL:2 * D_MODEL]
            for t in range(N_TILES):
                rows = pl.ds(t * TM, TM)
                xn, _ = _ln(x_ref[rows, :])
                h_buf[rows, :] = (xn * (1.0 + scale) + shift).astype(BF16)

            chip_of = lambda dev: 2 * dev[0] + dev[1]
            writes = []

            def project(n, dev):
                first = pl.multiple_of(chip_of(dev) * pair, pair)
                if n >= 2:
                    writes[n - 2].wait()
                proj_blk[n % 2] = _dot_nt(h_buf[...], wint_v[pl.ds(first, pair), :])
                cp = pltpu.make_async_copy(proj_blk.at[n % 2], proj_hbm.at[:, pl.ds(first, pair)], p_sems.at[n % 2])
                cp.start()
                writes.append(cp)

            gather_in.relay()
            own_in.wait()
            gather_in.wait_sibling()
            project(0, me)
            gather_in.pass_near()
            gather_in.wait_passed(ACROSS_X)
            project(1, _peer(ACROSS_X))
            gather_out.relay()
            gather_in.wait_passed(ACROSS_Y)
            project(2, _peer(ACROSS_Y))
            gather_in.pass_far()
            gather_in.wait_passed(ACROSS_BOTH)
            project(3, _peer(ACROSS_BOTH))

            gather_out.pass_on()
            gather_out.wait_rest()
            own_out.wait()
            for cp in keeps():
                cp.start()
            writes[2].wait()
            writes[3].wait()
            tile_read(0).start()

        @pl.when(tile + 1 < N_TILES)
        def _():
            tile_read(tile + 1).start()

        tile_read(tile).wait()
        xt = x_ref[pl.ds(pl.multiple_of(tile * TM, TM), TM), :]
        gate = mod_ref[layer:layer + 1, 2 * D_MODEL:]
        proj = proj_tile[tile % 2]
        cat, cdf_ref[...] = _mix_forward(proj, halo_ref[...], tile, weights)
        halo_ref[...] = proj[TM - HALO:, 0:D_POOL]
        y = _dot(cat.astype(BF16), wout_v[...])
        y_ref[...] = y
        zn, _ = _ln(ALPHA * xt + gate * y)
        out_ref[...] = zn * lng_ref[layer:layer + 1, :] + lnb_ref[layer:layer + 1, :]

        @pl.when(tile == N_TILES - 1)
        def _():
            for g in gathers():
                g.wait_sends()
            for cp in keeps():
                cp.wait()

    row = lambda w: pl.BlockSpec((TM, w), lambda i: (i, 0))
    gather_sems = pltpu.SemaphoreType.DMA((2, GATHER_SEMS))
    seven = pltpu.SemaphoreType.DMA((7,))
    lead_in = list(exchange) if exchanging else [mod]
    lead_specs = [_const_spec(a.shape) for a in lead_in]
    extra_out_specs = [_const_spec((N_DEV, D_MODEL)), _const_spec((DEPTH, 3 * D_MODEL))] if exchanging else []
    extra_out_shape = [jax.ShapeDtypeStruct((N_DEV, D_MODEL), F32),
                       jax.ShapeDtypeStruct((DEPTH, 3 * D_MODEL), F32)] if exchanging else []
    extra_scratch = [pltpu.VMEM((N_DEV, 8, D_MODEL), F32), pltpu.VMEM((8, D_MODEL), F32),
                     pltpu.VMEM((N_DEV, 8, cols), F32), pltpu.VMEM((N_DEV, 8, cols), F32),
                     seven, seven, seven, seven] if exchanging else []
    return pl.pallas_call(
        body,
        name=name,
        grid=(N_TILES,),
        in_specs=[_const_spec((SEQ, D_MODEL))] + lead_specs + [_const_spec(s) for s in SMALL_SPECS]
                 + [_const_spec((DEPTH, D_MODEL)), _const_spec((DEPTH, D_MODEL)), ANY, ANY],
        out_specs=[row(D_MODEL), row(D_MODEL), row(2 * D_SGU), ANY, ANY, ANY] + extra_out_specs,
        out_shape=[jax.ShapeDtypeStruct((SEQ, D_MODEL), F32), jax.ShapeDtypeStruct((SEQ, D_MODEL), F32),
                   jax.ShapeDtypeStruct((SEQ, 2 * D_SGU), F32), jax.ShapeDtypeStruct((SEQ, D_IN), F32),
                   jax.ShapeDtypeStruct((D_IN, D_MODEL), BF16), jax.ShapeDtypeStruct((D_MODEL, D_MODEL), BF16)]
                  + extra_out_shape,
        scratch_shapes=[pltpu.VMEM((D_IN, D_MODEL), BF16), pltpu.VMEM((D_MODEL, D_MODEL), BF16),
                        pltpu.VMEM((SEQ, D_MODEL), BF16), pltpu.VMEM((2, SEQ, pair), F32),
                        pltpu.VMEM((2, TM, D_IN), F32), pltpu.VMEM((HALO, D_POOL), F32),
                        gather_sems, gather_sems, pltpu.SemaphoreType.DMA((4,)), pltpu.SemaphoreType.DMA((2,)),
                        pltpu.SemaphoreType.DMA((2,))] + extra_scratch,
        compiler_params=pltpu.CompilerParams(dimension_semantics=("arbitrary",), vmem_limit_bytes=VMEM_LIMIT),
    )(x, *lead_in, *small, ln_g, ln_b, *mine)


ADA_CHUNK = 256


def _grad_tail(dproj, h, cat, dy, small, dmod8, loss_lanes, w_ada, m_ada, v_ada, act_t, b_ada, m_bada, v_bada):
    shard_in, shard_out, shard_small = D_IN // N_DEV, D_MODEL // N_DEV, small.shape[2]
    cols = w_ada.shape[2]
    W_IN, W_OUT, SMALL = 0, 1, 2

    def body(dproj_hbm, h_hbm, cat_hbm, dy_hbm, small_hbm, dmod_ref, lanes_ref, wada_hbm, mada_hbm, vada_hbm,
             act_ref, bada_ref, mbada_ref, vbada_ref,
             gwin_ref, gwout_ref, stot_ref, loss_ref, gada_hbm, dada_hbm, nmada_hbm, nvada_hbm,
             gb_ref, db_ref, nmb_ref, nvb_ref,
             dproj_v, h_v, cat_v, dy_v, part_in, part_out, own_small, loss_src, loss_all, dmod_all, ada_in, ada_out,
             *rest):
        bufs, rest = rest[:13], rest[13:]
        load_sems, rs_sems = rest[0], rest[1:6]
        m_send, m_recv, g_send, g_recv, s_send, s_recv, ada_lsem, ada_ssem = rest[6:]
        mine = _index(_me())

        def update_ada():
            upper = (mine % 2) == 1

            def dmod_of(layer):
                rows = []
                for b in range(N_DEV):
                    r = dmod_all[b, pl.ds(4 * layer + mine // 2, 1), :]
                    rows.append(jnp.where(upper, r[:, cols:], r[:, :cols]))
                return jnp.concatenate(rows, axis=0)

            chunks = [(layer, c) for layer in range(DEPTH) for c in range(D_MODEL // ADA_CHUNK)]

            def loads(i):
                layer, c = chunks[i]
                rows = pl.ds(c * ADA_CHUNK, ADA_CHUNK)
                return [pltpu.make_async_copy(src.at[layer, rows], ada_in.at[i % 2, k], ada_lsem.at[i % 2, k])
                        for k, src in enumerate((wada_hbm, mada_hbm, vada_hbm))]

            def stores(i):
                layer, c = chunks[i]
                rows = pl.ds(c * ADA_CHUNK, ADA_CHUNK)
                return [pltpu.make_async_copy(ada_out.at[i % 2, k], dst.at[layer, rows], ada_ssem.at[i % 2, k])
                        for k, dst in enumerate((gada_hbm, dada_hbm, nmada_hbm, nvada_hbm))]

            for cp in loads(0):
                cp.start()
            dmods = {}
            for i, (layer, c) in enumerate(chunks):
                if i + 1 < len(chunks):
                    for cp in loads(i + 1):
                        cp.start()
                for cp in loads(i):
                    cp.wait()
                if i >= 2:
                    for cp in stores(i - 2):
                        cp.wait()
                if layer not in dmods:
                    dmods[layer] = dmod_of(layer)
                act = act_ref[pl.ds(c * ADA_CHUNK, ADA_CHUNK), :]
                g = act[:, 0:1] * dmods[layer][0:1, :]
                for b in range(1, N_DEV):
                    g = g + act[:, b:b + 1] * dmods[layer][b:b + 1, :]
                slot = i % 2
                delta, new_m, new_v = _adamw_math(ada_in[slot, 0], g, ada_in[slot, 1], ada_in[slot, 2])
                ada_out[slot, 0] = g
                ada_out[slot, 1] = delta
                ada_out[slot, 2] = new_m
                ada_out[slot, 3] = new_v
                for cp in stores(i):
                    cp.start()
            for i in (len(chunks) - 2, len(chunks) - 1):
                for cp in stores(i):
                    cp.wait()

            total = dmod_all[0]
            for b in range(1, N_DEV):
                total = total + dmod_all[b]
            width = total.shape[1]
            for layer in range(DEPTH):
                for q in range(4):
                    gb_ref[layer:layer + 1, q * width:(q + 1) * width] = total[4 * layer + q:4 * layer + q + 1, :]
            db_ref[...], nmb_ref[...], nvb_ref[...] = _adamw_math(bada_ref[...], gb_ref[...], mbada_ref[...],
                                                                  vbada_ref[...])

        order = (ACROSS_BOTH, ACROSS_X, ACROSS_Y, None)
        chips = [_ChipReduceScatter._chip(r) for r in order]
        loads = [pltpu.make_async_copy(s, d, load_sems.at[n]) for n, (s, d) in enumerate(
            ((cat_hbm, cat_v), (dy_hbm, dy_v), (h_hbm, h_v)))]
        loads += [pltpu.make_async_copy(dproj_hbm.at[:, pl.ds(pl.multiple_of(chip * 2 * shard_in, 2 * shard_in),
                                                             2 * shard_in)], dproj_v.at[n], load_sems.at[3 + n])
                  for n, chip in enumerate(chips)]
        for cp in loads:
            cp.start()
        arrays = [dict(part=part_in, out=gwin_ref, staged=False, sib=bufs[0], snd=bufs[1], rcv=bufs[2], relay=bufs[3]),
                  dict(part=part_out, out=gwout_ref, staged=False, sib=bufs[4], snd=bufs[5], rcv=bufs[6],
                       relay=bufs[7]),
                  dict(part=small_hbm, out=own_small, staged=True, stage=bufs[8], sib=bufs[9], snd=bufs[10],
                       rcv=bufs[11], relay=bufs[12])]
        scatter = _ChipReduceScatter(arrays, *rs_sems)
        scatter.start([SMALL])
        dmod_all[mine] = dmod_ref[...]
        dmod_copies = _direct_exchange(lambda p: dmod_ref, lambda m: dmod_all.at[_index(m)], m_send, m_recv)
        loss_src[...] = jnp.full(loss_src.shape, (0.5 / D_MODEL) * jnp.sum(lanes_ref[...]), F32)
        loss_all[mine] = loss_src[...]
        loss_copies = _direct_exchange(lambda p: loss_src, lambda m: loss_all.at[_index(m)], s_send, s_recv)

        loads[0].wait()
        loads[1].wait()
        for blk in range(2):
            res = _dot_tn(cat_v[:, blk * 512:(blk + 1) * 512], dy_v[...]).astype(BF16)
            for s in range(4):
                part_out[2 * blk + s // 2, s % 2] = res[s * shard_out:(s + 1) * shard_out]
        scatter.start([W_OUT])
        scatter.exchange([SMALL])

        gather = _TwoLevelGather(stot_ref, g_send, g_recv)
        loads[2].wait()
        for n, chip in enumerate(chips):
            loads[3 + n].wait()
            res = _dot_tn(dproj_v[n], h_v[...]).astype(BF16)
            part_in[chip, 0] = res[:shard_in]
            part_in[chip, 1] = res[shard_in:]
            scatter.start([W_IN], chips=[chip])
            if n == 0:
                scatter.exchange([W_OUT])
                scatter.fold([SMALL])
            if n == 1:
                scatter.send_far([W_IN])
                scatter.fold([W_OUT])
                scatter.finish([SMALL])
                stot_ref[mine] = own_small[...]
                gather.send_mine()
            if n == 2:
                scatter.send_near(ACROSS_X, [W_IN])
                gather.relay()
            if n == 3:
                scatter.send_near(ACROSS_Y, [W_IN])
        scatter.fold([W_IN])
        scatter.finish([W_OUT])
        _wait_direct(dmod_copies)
        update_ada()
        gather.pass_on()
        gather.wait_rest()
        _wait_direct(loss_copies)
        total = loss_all[0]
        for j in range(1, N_DEV):
            total = total + loss_all[j]
        loss_ref[...] = total
        scatter.finish([W_IN])
        gather.wait_sends()
        scatter.wait_sends()

    buffers = _ChipReduceScatter.buffers
    comm_scratch = (buffers(shard_in, D_MODEL, BF16, staged=False) + buffers(shard_out, D_MODEL, BF16, staged=False)
                    + buffers(shard_small, 128, F32))
    comm_scratch += [pltpu.SemaphoreType.DMA((7,))] + _ChipReduceScatter.semaphores(3)
    comm_scratch += [pltpu.SemaphoreType.DMA((n,)) for n in (7, 7, GATHER_SEMS, GATHER_SEMS, 7, 7)]
    comm_scratch += [pltpu.SemaphoreType.DMA((2, 3)), pltpu.SemaphoreType.DMA((2, 4))]
    return pl.pallas_call(
        body,
        name="grad_tail",
        in_specs=[ANY] * 5 + [VMEM, VMEM] + [ANY] * 3 + [VMEM] * 4,
        out_specs=[VMEM] * 4 + [ANY] * 4 + [VMEM] * 4,
        out_shape=[jax.ShapeDtypeStruct((shard_in, D_MODEL), F32), jax.ShapeDtypeStruct((shard_out, D_MODEL), F32),
                   jax.ShapeDtypeStruct((N_DEV, shard_small, 128), F32), jax.ShapeDtypeStruct((8, 128), F32)]
                  + [jax.ShapeDtypeStruct(w_ada.shape, F32)] * 4 + [jax.ShapeDtypeStruct(b_ada.shape, F32)] * 4,
        scratch_shapes=[pltpu.VMEM((4, SEQ, 2 * shard_in), BF16), pltpu.VMEM(h.shape, BF16), pltpu.VMEM(cat.shape, BF16),
                        pltpu.VMEM(dy.shape, BF16), pltpu.VMEM((4, 2, shard_in, D_MODEL), BF16),
                        pltpu.VMEM((4, 2, shard_out, D_MODEL), BF16), pltpu.VMEM((shard_small, 128), F32),
                        pltpu.VMEM((8, 128), F32), pltpu.VMEM((N_DEV, 8, 128), F32),
                        pltpu.VMEM((N_DEV,) + dmod8.shape, F32), pltpu.VMEM((2, 3, ADA_CHUNK, cols), F32),
                        pltpu.VMEM((2, 4, ADA_CHUNK, cols), F32)] + comm_scratch,
        compiler_params=pltpu.CompilerParams(vmem_limit_bytes=VMEM_LIMIT),
    )(dproj, h, cat, dy, small, dmod8, loss_lanes, w_ada, m_ada, v_ada, act_t, b_ada, m_bada, v_bada)


SMALL_NAMES = ("w_pool", "w_sgu", "pool_scale", "sgu_ln_g", "sgu_ln_b", "b_sgu", "ln_g", "ln_b")
SMALL_ROWS = (512, 512, 4, 4, 4, 4, 8, 8)


def _adamw_small(g_packed, ws, ms, vs, name):
    n = len(SMALL_NAMES)

    def body(g_ref, *refs):
        w_refs, m_refs, v_refs = refs[:n], refs[n:2 * n], refs[2 * n:3 * n]
        outs = refs[3 * n:]

        def update(p, at, g):
            delta, new_m, new_v = _adamw_math(w_refs[p][at], g, m_refs[p][at], v_refs[p][at])
            outs[p][at] = g
            outs[n + p][at] = delta
            outs[2 * n + p][at] = new_m
            outs[3 * n + p][at] = new_v

        row = 0
        for p, r in enumerate(SMALL_ROWS):
            shape = ws[p].shape
            for layer in range(DEPTH):
                first = layer * PACK_ROWS + row
                if len(shape) == 4:
                    for k in range(shape[1]):
                        update(p, (layer, k), g_ref[first + k * shape[2]:first + (k + 1) * shape[2], :])
                elif len(shape) == 3:
                    update(p, (layer,), g_ref[first:first + r, :])
                else:
                    g = jnp.concatenate([g_ref[first + k:first + k + 1, :] for k in range(r)], axis=1)
                    update(p, (slice(layer, layer + 1), slice(None)), g)
            row += r

    res = pl.pallas_call(
        body,
        name=name,
        out_shape=[jax.ShapeDtypeStruct(w.shape, F32) for w in ws] * 4,
        compiler_params=pltpu.CompilerParams(vmem_limit_bytes=VMEM_LIMIT),
    )(g_packed, *ws, *ms, *vs)
    return res[:n], res[n:2 * n], res[2 * n:3 * n], res[3 * n:]


def kernel(x, c, w_ada, b_ada, w_in, w_pool, pool_scale, sgu_ln_g, sgu_ln_b, w_sgu, b_sgu, w_out, ln_g, ln_b, loss_target, m_w_ada, m_b_ada, m_w_in, m_w_pool, m_pool_scale, m_sgu_ln_g, m_sgu_ln_b, m_w_sgu, m_b_sgu, m_w_out, m_ln_g, m_ln_b, v_w_ada, v_b_ada, v_w_in, v_w_pool, v_pool_scale, v_sgu_ln_g, v_sgu_ln_b, v_w_sgu, v_b_sgu, v_w_out, v_ln_g, v_ln_b):
    small_w = dict(w_pool=w_pool, w_sgu=w_sgu, pool_scale=pool_scale, sgu_ln_g=sgu_ln_g, sgu_ln_b=sgu_ln_b,
                   b_sgu=b_sgu, ln_g=ln_g, ln_b=ln_b)
    small_m = dict(w_pool=m_w_pool, w_sgu=m_w_sgu, pool_scale=m_pool_scale, sgu_ln_g=m_sgu_ln_g,
                   sgu_ln_b=m_sgu_ln_b, b_sgu=m_b_sgu, ln_g=m_ln_g, ln_b=m_ln_b)
    small_v = dict(w_pool=v_w_pool, w_sgu=v_w_sgu, pool_scale=v_pool_scale, sgu_ln_g=v_sgu_ln_g,
                   sgu_ln_b=v_sgu_ln_b, b_sgu=v_b_sgu, ln_g=v_ln_g, ln_b=v_ln_b)

    wint_loc = jnp.transpose(w_in, (0, 2, 1)).astype(BF16)
    wout_loc = w_out.astype(BF16)
    small = (w_pool, pool_scale, sgu_ln_g, sgu_ln_b, w_sgu, jnp.transpose(b_sgu, (0, 2, 1)))
    acts, cur, w_int, w_outf, mod, act_all = [], x[0], [], [], None, None
    for l in range(DEPTH):
        blocks = [wint_loc[l], wout_loc[l]]
        if l == 0:
            out, y, cdf, proj, wi, wo, act_all, mod = _gathered_layer(
                l, cur, small, ln_g, ln_b, blocks, f"layer_fwd_{l}", exchange=(c, w_ada, b_ada))
        else:
            out, y, cdf, proj, wi, wo = _gathered_layer(l, cur, small, ln_g, ln_b, blocks, f"layer_fwd_{l}", mod=mod)
        w_int.append(wi)
        w_outf.append(wo)
        acts.append((cur, proj, y, cdf))
        cur = out

    shard_in, shard_out = D_IN // N_DEV, D_MODEL // N_DEV
    a, b = cur, loss_target[0]
    loss_lanes, carry, pending = None, (), []
    g_w_in_t, g_w_out = [None] * DEPTH, [None] * DEPTH
    for l in reversed(range(DEPTH)):
        dx, dproj, h, cat, dy, small_grads, dmod8, lanes, *shards = _layer_backward(
            l, a, b, *acts[l], mod, w_int[l], w_outf[l], small, ln_g, l == DEPTH - 1, f"layer_bwd_{l}",
            carry=carry, reduce=pending)
        if shards:
            g_w_in_t[l + 1], g_w_out[l + 1] = shards
        if l == DEPTH - 1:
            loss_lanes = lanes
        if l > 0:
            pending = [_grad_matmul(dproj, h, 640, f"grad_w_in_{l}").reshape(4, 2, shard_in, D_MODEL),
                       _grad_matmul(cat, dy, 512, f"grad_w_out_{l}").reshape(4, 2, shard_out, D_MODEL)]
        carry = (small_grads, dmod8)
        a = b = dx
    grad_x = a[None]

    (g_w_in_t[0], g_w_out[0], small_tot, loss_tile, g_w_ada, d_w_ada, nm_w_ada, nv_w_ada,
     g_b_ada, d_b_ada, nm_b_ada, nv_b_ada) = _grad_tail(
        dproj, h, cat, dy, small_grads.reshape(4, 2, DEPTH * PACK_ROWS // N_DEV, 128), dmod8, loss_lanes,
        w_ada, m_w_ada, v_w_ada, jnp.transpose(act_all), b_ada, m_b_ada, v_b_ada)
    loss = loss_tile[0, 0]

    flat = lambda t: t.reshape(-1, t.shape[-1])
    to_t = lambda t: flat(jnp.transpose(t, (0, 2, 1)))
    from_t = lambda t: jnp.transpose(t.reshape(DEPTH, shard_in, D_MODEL), (0, 2, 1))
    g_w_in, d_w_in, nm_w_in, nv_w_in = [from_t(t) for t in _adamw(to_t(w_in), g_w_in_t, to_t(m_w_in), to_t(v_w_in),
                                                                  shard_in // 2, "adamw_w_in")]
    gwout, d_w_out, nm_w_out, nv_w_out = [t.reshape(w_out.shape) for t in _adamw(
        flat(w_out), g_w_out, flat(m_w_out), flat(v_w_out), shard_out, "adamw_w_out")]
    small_out = _adamw_small(small_tot.reshape(DEPTH * PACK_ROWS, 128), [small_w[n] for n in SMALL_NAMES],
                             [small_m[n] for n in SMALL_NAMES], [small_v[n] for n in SMALL_NAMES], "adamw_small")
    gs, ds, ms, vs = [dict(zip(SMALL_NAMES, group)) for group in small_out]

    def ordered(w_ada_, b_ada_, w_in_, small, w_out_):
        return (w_ada_, b_ada_, w_in_, small["w_pool"], small["pool_scale"], small["sgu_ln_g"], small["sgu_ln_b"],
                small["w_sgu"], small["b_sgu"], w_out_, small["ln_g"], small["ln_b"])

    return (loss, grad_x,
            *ordered(g_w_ada, g_b_ada, g_w_in, gs, gwout),
            *ordered(d_w_ada, d_b_ada, d_w_in, ds, d_w_out),
            *ordered(nm_w_ada, nm_b_ada, nm_w_in, ms, nm_w_out),
            *ordered(nv_w_ada, nv_b_ada, nv_w_in, vs, nv_w_out))
```

```python
import jax
import jax.numpy as jnp
from jax import lax
from jax.experimental import pallas as pl
from jax.experimental.pallas import tpu as pltpu

F32 = jnp.float32
BF16 = jnp.bfloat16

D_MODEL = 1024
SEQ = 2048
DEPTH = 2
D_POOL = 512
D_SGU = 512
D_IN = 2560
N_GROUPS = 4
GROUP = 128
N_HEADS = 4
HEAD = 128
CHUNK = 128
WINDOWS = (2, 4, 8, 16)
ALPHA = (2.0 * DEPTH) ** 0.25
LN_EPS = 1e-5
N_DEV = 8

ADAM_LR = 0.001
ADAM_B1 = 0.9
ADAM_B2 = 0.999
ADAM_EPS = 1e-08
ADAM_WD = 0.01
ADAM_STEP = 10

TM = 256
HALO = 16
N_TILES = SEQ // TM
VMEM_LIMIT = 60 * 1024 * 1024

ROW_WPOOL = 0
ROW_WSGU = 512
ROW_PSCALE = 1024
ROW_SLNG = 1028
ROW_SLNB = 1032
ROW_BSGU = 1036
ROW_LNG = 1040
ROW_LNB = 1048
PACK_ROWS = 1088
DMOD_COLS = DEPTH * 3 * D_MODEL // 8

SQRT_HALF = 0.7071067811865476
INV_SQRT_2PI = 0.3989422804014327


def _ln(x):
    mu = jnp.mean(x, axis=-1, keepdims=True)
    xc = x - mu
    var = jnp.mean(xc * xc, axis=-1, keepdims=True)
    rstd = lax.rsqrt(var + LN_EPS)
    return xc * rstd, rstd


def _ln_bwd(dxn, xn, rstd):
    m1 = jnp.mean(dxn, axis=-1, keepdims=True)
    m2 = jnp.mean(dxn * xn, axis=-1, keepdims=True)
    return rstd * (dxn - m1 - xn * m2)


def _normal_cdf(x):
    return 0.5 * (1.0 + lax.erf(x * SQRT_HALF))


def _gelu_parts(x, cdf, with_grad):
    if not with_grad:
        return x * cdf, None
    return x * cdf, cdf + x * (INV_SQRT_2PI * jnp.exp(-0.5 * x * x))


def _silu_parts(x):
    s = jax.nn.sigmoid(x)
    return x * s, s * (1.0 + x * (1.0 - s))


def _dot(a, b):
    return lax.dot_general(a, b, (((1,), (0,)), ((), ())), preferred_element_type=F32)


def _dot_nt(a, b):
    return lax.dot_general(a, b, (((1,), (1,)), ((), ())), preferred_element_type=F32)


def _dot_tn(a, b):
    return lax.dot_general(a, b, (((0,), (0,)), ((), ())), preferred_element_type=F32)


def _row_index(tile):
    return tile * TM + lax.broadcasted_iota(jnp.int32, (TM, 1), 0)


def _window_sums(ext, forward):
    n = TM + HALO
    cur = ext
    outs = []
    for g in range(N_GROUPS):
        step = 1 << g
        cur = cur + pltpu.roll(cur, step if forward else n - step, 0)
        rows = cur[HALO:, :GROUP] if forward else cur[:TM, :GROUP]
        outs.append(rows)
        cur = cur[:, GROUP:] if g + 1 < N_GROUPS else None
    return outs


def _inverse_counts(rows):
    return [1.0 / jnp.minimum(rows + 1, w).astype(F32) for w in WINDOWS]


def _tril_bf16(w):
    t = lax.broadcasted_iota(jnp.int32, (CHUNK, CHUNK), 0)
    s = lax.broadcasted_iota(jnp.int32, (CHUNK, CHUNK), 1)
    return jnp.where(t >= s, w, 0.0).astype(BF16)


class _MixWeights:
    def __init__(self, layer, wpool_ref, pscale_ref, slng_ref, slnb_ref, wsgu_ref, bsgut_ref):
        self.layer = layer
        self.wpool_ref, self.pscale_ref, self.slng_ref, self.slnb_ref = wpool_ref, pscale_ref, slng_ref, slnb_ref
        self.wsgu_ref, self.bsgut_ref = wsgu_ref, bsgut_ref

    def pool(self, g):
        return self.wpool_ref[self.layer, g].astype(BF16)

    def pool_scale(self, g):
        return self.pscale_ref[self.layer:self.layer + 1, g * GROUP:(g + 1) * GROUP]

    def ln_gain(self, h):
        return self.slng_ref[self.layer, h:h + 1, :]

    def ln_bias(self, h):
        return self.slnb_ref[self.layer, h:h + 1, :]

    def mix(self, h):
        return _tril_bf16(self.wsgu_ref[self.layer, h])

    def mix_bias(self, h):
        return self.bsgut_ref[self.layer, :, h:h + 1]


SMALL_SPECS = ((DEPTH, N_GROUPS, GROUP, GROUP), (DEPTH, D_POOL), (DEPTH, N_HEADS, HEAD), (DEPTH, N_HEADS, HEAD),
               (DEPTH, N_HEADS, CHUNK, CHUNK), (DEPTH, CHUNK, N_HEADS))


def _mix_forward(proj, halo, tile, w, cdf=None):
    keep = cdf is not None
    rows = _row_index(tile)
    inv_counts = _inverse_counts(rows)
    xa = proj[:, 0:D_POOL]
    ga = proj[:, D_POOL:2 * D_POOL]
    sums = _window_sums(jnp.concatenate([halo, xa], axis=0), True)
    ga_act, ga_grad = _silu_parts(ga)
    pooled, pw, ya = [], [], []
    for g in range(N_GROUPS):
        sl = slice(g * GROUP, (g + 1) * GROUP)
        p = (sums[g] * inv_counts[g] - xa[:, sl]).astype(BF16)
        q = _dot(p, w.pool(g))
        pooled.append(p)
        pw.append(q)
        ya.append(q * w.pool_scale(g) * ga_act[:, sl])

    u = proj[:, 2 * D_POOL:2 * D_POOL + D_SGU]
    v = proj[:, 2 * D_POOL + D_SGU:2 * D_POOL + 2 * D_SGU]
    gb = proj[:, 2 * D_POOL + 2 * D_SGU:]
    gb_act, gb_grad = _silu_parts(gb)
    if cdf is None:
        cdf = jnp.concatenate([_normal_cdf(u), _normal_cdf(v)], axis=1)
    u_act, u_grad = _gelu_parts(u, cdf[:, :D_SGU], keep)
    v_act, v_grad = _gelu_parts(v, cdf[:, D_SGU:], keep)
    vn, vrstd, vln, mixed, yb = [], [], [], [], []
    for h in range(N_HEADS):
        sl = slice(h * HEAD, (h + 1) * HEAD)
        n_h, r_h = _ln(v_act[:, sl])
        l_h = (n_h * w.ln_gain(h) + w.ln_bias(h)).astype(BF16)
        w_h = w.mix(h)
        bias = w.mix_bias(h)
        m_h = jnp.concatenate(
            [_dot(w_h, l_h[k * CHUNK:(k + 1) * CHUNK]) + bias for k in range(TM // CHUNK)], axis=0)
        vn.append(n_h)
        vrstd.append(r_h)
        vln.append(l_h)
        mixed.append(m_h)
        yb.append(u_act[:, sl] * m_h * gb_act[:, sl])
    cat = jnp.concatenate(ya + yb, axis=1)
    if not keep:
        return cat, cdf
    return cat, dict(inv_counts=inv_counts, ga_act=ga_act, ga_grad=ga_grad, pooled=pooled, pw=pw, u_grad=u_grad,
                     v_grad=v_grad, u_act=u_act, gb_act=gb_act, gb_grad=gb_grad, vn=vn, vrstd=vrstd, vln=vln,
                     mixed=mixed)


def _const_spec(shape):
    nd = len(shape)
    return pl.BlockSpec(shape, lambda i: (0,) * nd)


def _layer_forward(layer, x, mod, w_int, w_outf, small, ln_g, ln_b, name):
    def body(x_ref, mod_ref, wint_ref, wout_ref, wpool_ref, pscale_ref, slng_ref, slnb_ref, wsgu_ref, bsgut_ref,
             lng_ref, lnb_ref, out_ref, proj_ref, y_ref, cdf_ref, halo_ref):
        weights = _MixWeights(layer, wpool_ref, pscale_ref, slng_ref, slnb_ref, wsgu_ref, bsgut_ref)
        tile = pl.program_id(0)

        @pl.when(tile == 0)
        def _():
            halo_ref[...] = jnp.zeros_like(halo_ref)

        xt = x_ref[...]
        shift = mod_ref[layer:layer + 1, 0:D_MODEL]
        scale = mod_ref[layer:layer + 1, D_MODEL:2 * D_MODEL]
        gate = mod_ref[layer:layer + 1, 2 * D_MODEL:]
        xn, _ = _ln(xt)
        h = (xn * (1.0 + scale) + shift).astype(BF16)
        proj = _dot_nt(h, wint_ref[...])
        proj_ref[...] = proj
        cat, cdf_ref[...] = _mix_forward(proj, halo_ref[...], tile, weights)
        halo_ref[...] = proj[TM - HALO:, 0:D_POOL]
        y = _dot(cat.astype(BF16), wout_ref[...])
        y_ref[...] = y
        zn, _ = _ln(ALPHA * xt + gate * y)
        out_ref[...] = zn * lng_ref[layer:layer + 1, :] + lnb_ref[layer:layer + 1, :]

    row = lambda w: pl.BlockSpec((TM, w), lambda i: (i, 0))
    return pl.pallas_call(
        body,
        name=name,
        grid=(N_TILES,),
        in_specs=[row(D_MODEL), _const_spec((DEPTH, 3 * D_MODEL)), _const_spec((D_IN, D_MODEL)),
                  _const_spec((D_MODEL, D_MODEL))] + [_const_spec(s) for s in SMALL_SPECS]
                 + [_const_spec((DEPTH, D_MODEL)), _const_spec((DEPTH, D_MODEL))],
        out_specs=[row(D_MODEL), row(D_IN), row(D_MODEL), row(2 * D_SGU)],
        out_shape=[jax.ShapeDtypeStruct((SEQ, D_MODEL), F32), jax.ShapeDtypeStruct((SEQ, D_IN), F32),
                   jax.ShapeDtypeStruct((SEQ, D_MODEL), F32), jax.ShapeDtypeStruct((SEQ, 2 * D_SGU), F32)],
        scratch_shapes=[pltpu.VMEM((HALO, D_POOL), F32)],
        compiler_params=pltpu.CompilerParams(dimension_semantics=("arbitrary",), vmem_limit_bytes=VMEM_LIMIT),
    )(x, mod, w_int, w_outf, *small, ln_g, ln_b)


VEC_LNG, VEC_LNB, VEC_POOL, VEC_SGU, VEC_SHIFT, VEC_SCALE, VEC_GATE, VEC_LOSS = range(8)


def _layer_backward(layer, a, b, x, proj, y, cdf, mod, w_int, w_outf, small, ln_g, is_last, name, carry=(),
                    reduce=()):
    n_red, n_carry = len(reduce), len(carry)
    base = layer * PACK_ROWS

    def body(a_ref, b_ref, x_ref, proj_ref, prev_ref, y_ref, cdf_ref, mod_ref, wint_ref, wout_ref, wpool_ref,
             pscale_ref, slng_ref, slnb_ref, wsgu_ref, bsgut_ref, lng_ref, *rest):
        weights = _MixWeights(layer, wpool_ref, pscale_ref, slng_ref, slnb_ref, wsgu_ref, bsgut_ref)
        carry_refs, rest = rest[:n_carry], rest[n_carry:]
        part_refs, rest = rest[:n_red], rest[n_red:]
        dx_ref, dproj_ref, h_ref, cat_ref, dy_ref, small_ref, dmod_ref, loss_ref = rest[:8]
        shard_refs, rest = rest[8:8 + n_red], rest[8 + n_red:]
        vec_ref, dmix_ref, halo_ref = rest[:3]
        step = pl.program_id(0)
        tile = N_TILES - 1 - step

        def scatter():
            bufs, sems = rest[3:3 + 5 * n_red], rest[3 + 5 * n_red:]
            arrays = [dict(part=part_refs[n], out=shard_refs[n], staged=True, stage=bufs[5 * n], sib=bufs[5 * n + 1],
                           snd=bufs[5 * n + 2], rcv=bufs[5 * n + 3], relay=bufs[5 * n + 4]) for n in range(n_red)]
            return _ChipReduceScatter(arrays, *sems)

        @pl.when(step == 0)
        def _():
            small_ref[...] = jnp.zeros_like(small_ref)
            dmod_ref[...] = jnp.zeros_like(dmod_ref)
            vec_ref[...] = jnp.zeros_like(vec_ref)
            dmix_ref[...] = jnp.zeros_like(dmix_ref)
            halo_ref[...] = jnp.zeros_like(halo_ref)
            if n_red:
                scatter().start()

        if n_red:
            @pl.when(step == 1)
            def _():
                scatter().exchange()

            @pl.when(step == N_TILES // 2)
            def _():
                scatter().fold()

        def acc(row, lo, val):
            hi = lo + val.shape[1]
            vec_ref[row:row + 1, lo:hi] += jnp.sum(val, axis=0, keepdims=True)

        xt = x_ref[...]
        yt = y_ref[...]
        shift = mod_ref[layer:layer + 1, 0:D_MODEL]
        scale = mod_ref[layer:layer + 1, D_MODEL:2 * D_MODEL]
        gate = mod_ref[layer:layer + 1, 2 * D_MODEL:]
        ln_gain = lng_ref[layer:layer + 1, :]

        zn, zrstd = _ln(ALPHA * xt + gate * yt)
        if is_last:
            diff = a_ref[...] - b_ref[...]
            acc(VEC_LOSS, 0, diff * diff)
            dout = diff * (1.0 / D_MODEL)
        else:
            dout = a_ref[...]
        acc(VEC_LNG, 0, dout * zn)
        acc(VEC_LNB, 0, dout)
        dz = _ln_bwd(dout * ln_gain, zn, zrstd)
        acc(VEC_GATE, 0, dz * yt)
        dy = (dz * gate).astype(BF16)
        dy_ref[...] = dy
        dcat = _dot_nt(dy, wout_ref[...])

        proj = proj_ref[...]
        prev = jnp.where(tile > 0, prev_ref[...], 0.0)
        cat, k = _mix_forward(proj, prev, tile, weights, cdf_ref[...])
        cat_ref[...] = cat.astype(BF16)

        dga, dq = [], []
        for g in range(N_GROUPS):
            sl = slice(g * GROUP, (g + 1) * GROUP)
            pscale = weights.pool_scale(g)
            dya = dcat[:, sl]
            dyp = dya * k["ga_act"][:, sl]
            dga.append(dya * k["pw"][g] * pscale * k["ga_grad"][:, sl])
            acc(VEC_POOL, g * GROUP, dyp * k["pw"][g])
            dpw = (dyp * pscale).astype(BF16)
            rows = pl.ds(base + ROW_WPOOL + g * GROUP, GROUP)
            small_ref[rows, :] += _dot_tn(k["pooled"][g], dpw)
            dq.append(_dot_nt(dpw, weights.pool(g)))
        dpooled = jnp.concatenate(dq, axis=1)
        scaled = jnp.concatenate([dq[g] * k["inv_counts"][g] for g in range(N_GROUPS)], axis=1)
        sums = _window_sums(jnp.concatenate([scaled, halo_ref[...]], axis=0), False)
        halo_ref[...] = scaled[0:HALO]
        dxa = jnp.concatenate(sums, axis=1) - dpooled

        du, dv, dgb = [], [], []
        for h in range(N_HEADS):
            sl = slice(h * HEAD, (h + 1) * HEAD)
            dyb = dcat[:, D_POOL + h * HEAD:D_POOL + (h + 1) * HEAD]
            m_h = k["mixed"][h]
            ug = k["u_act"][:, sl] * dyb
            du.append(dyb * m_h * k["gb_act"][:, sl] * k["u_grad"][:, sl])
            dgb.append(ug * m_h * k["gb_grad"][:, sl])
            dmixed = ug * k["gb_act"][:, sl]
            dmixed_bf = dmixed.astype(BF16)
            w_h = weights.mix(h)
            dvln_parts = []
            dmix_sum = dmix_ref[h]
            wsgu_rows = pl.ds(base + ROW_WSGU + h * CHUNK, CHUNK)
            dws = small_ref[wsgu_rows, :]
            for c in range(TM // CHUNK):
                cs = slice(c * CHUNK, (c + 1) * CHUNK)
                dmix_sum = dmix_sum + dmixed[cs]
                dws = dws + _dot_nt(dmixed_bf[cs], k["vln"][h][cs])
                dvln_parts.append(_dot_tn(w_h, dmixed_bf[cs]))
            dmix_ref[h] = dmix_sum
            small_ref[wsgu_rows, :] = dws
            dvln = jnp.concatenate(dvln_parts, axis=0)
            acc(VEC_SGU, h * HEAD, dvln * k["vn"][h])
            acc(VEC_SGU, D_SGU + h * HEAD, dvln)
            dvv = _ln_bwd(dvln * weights.ln_gain(h), k["vn"][h], k["vrstd"][h])
            dv.append(dvv * k["v_grad"][:, sl])

        dproj = jnp.concatenate([dxa] + dga + du + dv + dgb, axis=1).astype(BF16)
        dproj_ref[...] = dproj
        dh = _dot(dproj, wint_ref[...])

        xn, xrstd = _ln(xt)
        h_ref[...] = (xn * (1.0 + scale) + shift).astype(BF16)
        acc(VEC_SCALE, 0, dh * xn)
        acc(VEC_SHIFT, 0, dh)
        dx_ref[...] = _ln_bwd(dh * (1.0 + scale), xn, xrstd) + ALPHA * dz

        @pl.when(step == N_TILES - 1)
        def _():
            def put(row0, vec_row, lo, n):
                for r in range(n):
                    small_ref[base + row0 + r:base + row0 + r + 1, :] = (
                        vec_ref[vec_row:vec_row + 1, lo + r * 128:lo + (r + 1) * 128])

            put(ROW_PSCALE, VEC_POOL, 0, 4)
            put(ROW_SLNG, VEC_SGU, 0, 4)
            put(ROW_SLNB, VEC_SGU, D_SGU, 4)
            put(ROW_LNG, VEC_LNG, 0, 8)
            put(ROW_LNB, VEC_LNB, 0, 8)
            ones = jnp.ones((8, HEAD), F32)
            t = lax.broadcasted_iota(jnp.int32, (CHUNK, CHUNK), 0)
            s = lax.broadcasted_iota(jnp.int32, (CHUNK, CHUNK), 1)
            for h in range(N_HEADS):
                bias_rows = lax.dot_general(ones, dmix_ref[h], (((1,), (1,)), ((), ())),
                                            preferred_element_type=F32, precision=lax.Precision.HIGHEST)
                small_ref[base + ROW_BSGU + h:base + ROW_BSGU + h + 1, :] = bias_rows[0:1]
                rows = pl.ds(base + ROW_WSGU + h * CHUNK, CHUNK)
                small_ref[rows, :] = jnp.where(t >= s, small_ref[rows, :], 0.0)
            pieces = ((0, VEC_SHIFT, 0, 768),
                      (1, VEC_SHIFT, 768, 256), (1, VEC_SCALE, 0, 512),
                      (2, VEC_SCALE, 512, 512), (2, VEC_GATE, 0, 256),
                      (3, VEC_GATE, 256, 768))
            filled = [0] * 4
            for q, vec_row, lo, n in pieces:
                row = 4 * layer + q
                dmod_ref[row:row + 1, filled[q]:filled[q] + n] = vec_ref[vec_row:vec_row + 1, lo:lo + n]
                filled[q] += n
            if n_carry:
                for other in range(layer + 1, DEPTH):
                    rows = pl.ds(other * PACK_ROWS, PACK_ROWS)
                    small_ref[rows, :] = carry_refs[0][rows, :]
                    dmod_ref[4 * other:4 * other + 4, :] = carry_refs[1][4 * other:4 * other + 4, :]
            loss_ref[...] = vec_ref[VEC_LOSS:VEC_LOSS + 1, :]
            if n_red:
                scatter().finish()
                scatter().wait_sends()

    rev = lambda w: pl.BlockSpec((TM, w), lambda i: (N_TILES - 1 - i, 0))
    prev_spec = pl.BlockSpec(
        (HALO, D_POOL), lambda i: (jnp.maximum((N_TILES - 1 - i) * (TM // HALO) - 1, 0), 0))
    comm_scratch = []
    for p in reduce:
        comm_scratch += _ChipReduceScatter.buffers(p.shape[2], p.shape[3], p.dtype)
    if n_red:
        comm_scratch += _ChipReduceScatter.semaphores(n_red)
    return pl.pallas_call(
        body,
        name=name,
        grid=(N_TILES,),
        in_specs=[rev(D_MODEL), rev(D_MODEL) if is_last else pl.BlockSpec((TM, D_MODEL), lambda i: (0, 0)),
                  rev(D_MODEL), rev(D_IN), prev_spec, rev(D_MODEL), rev(2 * D_SGU),
                  _const_spec((DEPTH, 3 * D_MODEL)), _const_spec((D_IN, D_MODEL)), _const_spec((D_MODEL, D_MODEL))]
                 + [_const_spec(s) for s in SMALL_SPECS] + [_const_spec((DEPTH, D_MODEL))]
                 + [_const_spec(c.shape) for c in carry] + [ANY] * n_red,
        out_specs=[rev(D_MODEL), rev(D_IN), rev(D_MODEL), rev(D_MODEL), rev(D_MODEL),
                   _const_spec((DEPTH * PACK_ROWS, 128)), _const_spec((8, DMOD_COLS)), _const_spec((1, D_MODEL))]
                  + [_const_spec(p.shape[2:]) for p in reduce],
        out_shape=[jax.ShapeDtypeStruct((SEQ, D_MODEL), F32), jax.ShapeDtypeStruct((SEQ, D_IN), BF16),
                   jax.ShapeDtypeStruct((SEQ, D_MODEL), BF16), jax.ShapeDtypeStruct((SEQ, D_MODEL), BF16),
                   jax.ShapeDtypeStruct((SEQ, D_MODEL), BF16), jax.ShapeDtypeStruct((DEPTH * PACK_ROWS, 128), F32),
                   jax.ShapeDtypeStruct((8, DMOD_COLS), F32), jax.ShapeDtypeStruct((1, D_MODEL), F32)]
                  + [jax.ShapeDtypeStruct(p.shape[2:], F32) for p in reduce],
        scratch_shapes=[pltpu.VMEM((8, D_MODEL), F32), pltpu.VMEM((N_HEADS, CHUNK, HEAD), F32),
                        pltpu.VMEM((HALO, D_POOL), F32)] + comm_scratch,
        compiler_params=pltpu.CompilerParams(dimension_semantics=("arbitrary",), vmem_limit_bytes=VMEM_LIMIT),
    )(a, b, x, proj, proj, y, cdf, mod, w_int, w_outf, *small, ln_g, *carry, *reduce)


def _grad_matmul(lhs, rhs, block_cols, name):
    m, n = lhs.shape[1], rhs.shape[1]

    def body(lhs_ref, rhs_ref, out_ref):
        out_ref[...] = _dot_tn(lhs_ref[...], rhs_ref[...]).astype(BF16)

    return pl.pallas_call(
        body,
        name=name,
        grid=(m // block_cols,),
        in_specs=[pl.BlockSpec((SEQ, block_cols), lambda j: (0, j)), pl.BlockSpec((SEQ, n), lambda j: (0, 0))],
        out_specs=pl.BlockSpec((block_cols, n), lambda j: (j, 0)),
        out_shape=jax.ShapeDtypeStruct((m, n), BF16),
        compiler_params=pltpu.CompilerParams(dimension_semantics=("arbitrary",), vmem_limit_bytes=VMEM_LIMIT),
    )(lhs, rhs)


def _adamw_math(w, g, m, v):
    m = ADAM_B1 * m + (1.0 - ADAM_B1) * g
    v = ADAM_B2 * v + (1.0 - ADAM_B2) * (g * g)
    m_hat = m / (1.0 - ADAM_B1 ** ADAM_STEP)
    v_hat = v / (1.0 - ADAM_B2 ** ADAM_STEP)
    delta = -ADAM_LR * (m_hat / (jnp.sqrt(v_hat) + ADAM_EPS) + ADAM_WD * w)
    return delta, m, v


def _adamw(w, grads, m, v, block_rows, name):
    rows, cols = grads[0].shape
    blocks = rows // block_rows

    def body(w_ref, m_ref, v_ref, *rest):
        g_refs, (g_ref, d_ref, nm_ref, nv_ref) = rest[:DEPTH], rest[DEPTH:]
        for layer in range(DEPTH):
            @pl.when(pl.program_id(0) == layer)
            def _():
                g = g_refs[layer][...]
                g_ref[...] = g
                d_ref[...], nm_ref[...], nv_ref[...] = _adamw_math(w_ref[...], g, m_ref[...], v_ref[...])

    def grad_spec(layer):
        return pl.BlockSpec((block_rows, cols),
                            lambda l, i: (jnp.where(l == layer, i, jnp.where(l < layer, 0, blocks - 1)), 0))

    spec = pl.BlockSpec((block_rows, cols), lambda l, i: (l * blocks + i, 0))
    return pl.pallas_call(
        body,
        name=name,
        grid=(DEPTH, blocks),
        in_specs=[spec] * 3 + [grad_spec(layer) for layer in range(DEPTH)],
        out_specs=[spec] * 4,
        out_shape=[jax.ShapeDtypeStruct(w.shape, F32)] * 4,
        compiler_params=pltpu.CompilerParams(dimension_semantics=("arbitrary", "arbitrary"),
                                             vmem_limit_bytes=VMEM_LIMIT),
    )(w, m, v, *grads)


MESH = pl.DeviceIdType.MESH
SIBLING = 1
ANY = pl.BlockSpec(memory_space=pl.ANY)
VMEM = pl.BlockSpec(memory_space=pltpu.VMEM)


def _me():
    return lax.axis_index("x"), lax.axis_index("y"), lax.axis_index("c")


def _peer(r):
    x, y, c = _me()
    return (1 - x if r & 4 else x, 1 - y if r & 2 else y, 1 - c if r & 1 else c)


def _index(dev):
    return 4 * dev[0] + 2 * dev[1] + dev[2]


def _remote(src, dst, send_sem, recv_sem, dev):
    return pltpu.make_async_remote_copy(src_ref=src, dst_ref=dst, send_sem=send_sem, recv_sem=recv_sem,
                                        device_id=dev, device_id_type=MESH)


ACROSS_X, ACROSS_Y, ACROSS_BOTH = 4, 2, 6
GATHER_SEMS = 11


class _TwoLevelGather:
    def __init__(self, out, send_sems, recv_sems, src=None):
        self.out, self.send_sems, self.recv_sems, self.src = out, send_sems, recv_sems, src
        self.rows = (out.shape[0] // N_DEV) if len(out.shape) == 2 else out.shape[1]
        self.half = self.rows // 2

    def _slot(self, block):
        if len(self.out.shape) == 2:
            return self.out.at[pl.ds(pl.multiple_of(_index(block) * self.rows, self.rows), self.rows)]
        return self.out.at[_index(block)]

    def _copy(self, k, block, part, to, src=None):
        slot = self._slot(block)
        if part is not None:
            rows = pl.ds(part * self.half, self.half)
            slot = slot.at[rows]
            src = None if src is None else src.at[rows]
        return _remote(slot if src is None else src, slot, self.send_sems.at[k], self.recv_sems.at[k], to)

    def _mine(self):
        me = _me()
        src = self._slot(me) if self.src is None else self.src
        x, y = _peer(ACROSS_X), _peer(ACROSS_Y)
        return [self._copy(1, me, 0, x, src), self._copy(3, me, 1, y, src), self._copy(0, me, None, _peer(SIBLING), src),
                self._copy(2, me, 1, x, src), self._copy(4, me, 0, y, src)]

    def _relayed(self):
        return [self._copy(5, _peer(ACROSS_X), 0, _peer(ACROSS_Y)), self._copy(6, _peer(ACROSS_Y), 1, _peer(ACROSS_X))]

    def _passed(self):
        sib, far = _peer(SIBLING), _peer(ACROSS_BOTH)
        return [self._copy(7, _peer(ACROSS_X), None, sib), self._copy(8, _peer(ACROSS_Y), None, sib),
                self._copy(9, far, 0, sib), self._copy(10, far, 1, sib)]

    def _arrival(self, k, r, part):
        return self._copy(k, _peer(r), part, _me())

    def send_first(self):
        for cp in self._mine()[:3]:
            cp.start()

    def send_second(self):
        for cp in self._mine()[3:]:
            cp.start()

    def send_mine(self):
        self.send_first()
        self.send_second()

    def relay(self):
        relayed = self._relayed()
        self._arrival(1, ACROSS_X, 0).wait_recv()
        relayed[0].start()
        self._arrival(3, ACROSS_Y, 1).wait_recv()
        relayed[1].start()

    def pass_near(self):
        passed = self._passed()
        self._arrival(2, ACROSS_X, 1).wait_recv()
        passed[0].start()
        self._arrival(4, ACROSS_Y, 0).wait_recv()
        passed[1].start()

    def pass_far(self):
        passed = self._passed()
        self._arrival(5, ACROSS_BOTH, 0).wait_recv()
        passed[2].start()
        self._arrival(6, ACROSS_BOTH, 1).wait_recv()
        passed[3].start()

    def pass_on(self):
        self.pass_near()
        self.pass_far()

    def wait_sibling(self):
        self._arrival(0, SIBLING, None).wait_recv()

    def wait_passed(self, r):
        if r == ACROSS_BOTH:
            self._arrival(9, r ^ SIBLING, 0).wait_recv()
            self._arrival(10, r ^ SIBLING, 1).wait_recv()
        else:
            self._arrival(7 if r == ACROSS_X else 8, r ^ SIBLING, None).wait_recv()

    def wait_rest(self):
        self.wait_sibling()
        for r in (ACROSS_X, ACROSS_Y, ACROSS_BOTH):
            self.wait_passed(r)

    def wait_sends(self):
        for cp in self._mine() + self._relayed() + self._passed():
            cp.wait_send()


class _ChipReduceScatter:
    SLOTS = 6

    def __init__(self, arrays, l_sem, d_send, d_recv, i_send, i_recv):
        self.arrays = arrays
        self.l_sem, self.d_send, self.d_recv, self.i_send, self.i_recv = l_sem, d_send, d_recv, i_send, i_recv

    @staticmethod
    def buffers(rows, cols, dtype, staged=True):
        stage = [pltpu.VMEM((4, rows, cols), dtype)] if staged else []
        return stage + [pltpu.VMEM((4, rows, cols), dtype), pltpu.VMEM((3, rows, cols), dtype),
                        pltpu.VMEM((2, rows, cols), dtype), pltpu.VMEM((2, rows // 2, cols), dtype)]

    @classmethod
    def semaphores(cls, n):
        return [pltpu.SemaphoreType.DMA((n,)), pltpu.SemaphoreType.DMA((n, 4)), pltpu.SemaphoreType.DMA((n, 4)),
                pltpu.SemaphoreType.DMA((n, cls.SLOTS)), pltpu.SemaphoreType.DMA((n, cls.SLOTS))]

    def _pick(self, which):
        return list(enumerate(self.arrays)) if which is None else [(n, self.arrays[n]) for n in which]

    @staticmethod
    def _chip(r):
        dev = _me() if r is None else _peer(r)
        return 2 * dev[0] + dev[1]

    def _staging(self, which):
        c = _me()[2]
        return [pltpu.make_async_copy(a["part"].at[pl.ds(0, 4), c], a["stage"], self.l_sem.at[n])
                for n, a in self._pick(which) if a["staged"]]

    def _first(self, which, chip):
        other = 1 - _me()[2]
        return [_remote(a["part"].at[chip, other], a["sib"].at[chip], self.d_send.at[n, chip], self.d_recv.at[n, chip],
                        _peer(SIBLING)) for n, a in self._pick(which)]

    @staticmethod
    def _halves(a):
        half = a["rcv"].shape[1] // 2
        return pl.ds(0, half), pl.ds(half, half)

    def _hops(self, n, a):
        h0, h1 = self._halves(a)
        x, y = _peer(ACROSS_X), _peer(ACROSS_Y)
        snd, rcv, relay = a["snd"], a["rcv"], a["relay"]
        pairs = [(snd.at[2, h0], relay.at[0], x), (snd.at[2, h1], relay.at[1], y),
                 (snd.at[0, h0], rcv.at[0, h0], x), (snd.at[0, h1], rcv.at[0, h1], x),
                 (snd.at[1, h1], rcv.at[1, h1], y), (snd.at[1, h0], rcv.at[1, h0], y)]
        return [_remote(s, d, self.i_send.at[n, k], self.i_recv.at[n, k], to) for k, (s, d, to) in enumerate(pairs)]

    def _mine(self, a, chip, rows=None):
        src = a["stage"].at[chip] if a["staged"] else a["part"].at[chip, _me()[2]]
        mine, sib = (src[...], a["sib"][chip]) if rows is None else (src[rows, :], a["sib"][chip, rows, :])
        return mine.astype(F32) + sib.astype(F32)

    def start(self, which=None, chips=None):
        if chips is None:
            for cp in self._staging(which):
                cp.start()
        for chip in range(4) if chips is None else chips:
            for cp in self._first(which, chip):
                cp.start()

    def send_far(self, which=None):
        far = self._chip(ACROSS_BOTH)
        for cp in self._staging(which):
            cp.wait()
        for cp in self._first(which, far):
            cp.wait_recv()
        for n, a in self._pick(which):
            hops = self._hops(n, a)
            a["snd"][2] = self._mine(a, far).astype(a["snd"].dtype)
            hops[0].start()
            hops[1].start()

    def send_near(self, r, which=None):
        chip = self._chip(r)
        for cp in self._first(which, chip):
            cp.wait_recv()
        for n, a in self._pick(which):
            h0, h1 = self._halves(a)
            hops = self._hops(n, a)
            if r == ACROSS_X:
                a["snd"][0, h0, :] = self._mine(a, chip, h0).astype(a["snd"].dtype)
                hops[2].start()
            else:
                a["snd"][1, h1, :] = self._mine(a, chip, h1).astype(a["snd"].dtype)
                hops[4].start()

    def exchange(self, which=None):
        self.send_far(which)
        self.send_near(ACROSS_X, which)
        self.send_near(ACROSS_Y, which)

    def fold(self, which=None):
        across_x, across_y = self._chip(ACROSS_X), self._chip(ACROSS_Y)
        for n, a in self._pick(which):
            h0, h1 = self._halves(a)
            hops = self._hops(n, a)
            dtype = a["snd"].dtype
            hops[1].wait_recv()
            a["snd"][0, h1, :] = (self._mine(a, across_x, h1) + a["relay"][1].astype(F32)).astype(dtype)
            hops[3].start()
            hops[0].wait_recv()
            a["snd"][1, h0, :] = (self._mine(a, across_y, h0) + a["relay"][0].astype(F32)).astype(dtype)
            hops[5].start()

    def finish(self, which=None):
        home = self._chip(None)
        for cp in self._first(which, home):
            cp.wait_recv()
        for n, a in self._pick(which):
            hops = self._hops(n, a)
            a["out"][...] = self._mine(a, home)
            hops[2].wait_recv()
            hops[3].wait_recv()
            a["out"][...] += a["rcv"][0].astype(F32)
            hops[4].wait_recv()
            hops[5].wait_recv()
            a["out"][...] += a["rcv"][1].astype(F32)

    def wait_sends(self, which=None):
        for chip in range(4):
            for cp in self._first(which, chip):
                cp.wait_send()
        for n, a in self._pick(which):
            for cp in self._hops(n, a):
                cp.wait_send()


def _direct_exchange(src_of, dst_of, send_sems, recv_sems):
    me = _me()
    copies = [_remote(src_of(_peer(r)), dst_of(me), send_sems.at[r - 1], recv_sems.at[r - 1], _peer(r))
              for r in range(1, N_DEV)]
    for cp in copies:
        cp.start()
    return copies


def _wait_direct(copies):
    for cp in copies:
        cp.wait_recv()
    for cp in copies:
        cp.wait_send()


def _gathered_layer(layer, x, small, ln_g, ln_b, mine, name, mod=None, exchange=None, host=()):
    exchanging = exchange is not None
    cols = exchange[1].shape[2] if exchanging else 0
    shard = mine[0].shape[0]
    pair = 2 * shard
    n_lead = 4 if exchanging else 2
    n_host = len(host)

    def body(*refs):
        x_ref = refs[0]
        lead, refs = refs[1:n_lead], refs[n_lead:]
        (wpool_ref, pscale_ref, slng_ref, slnb_ref, wsgu_ref, bsgut_ref, lng_ref, lnb_ref, wint_hbm,
         wout_hbm), refs = refs[:10], refs[10:]
        host_in, refs = refs[:n_host], refs[n_host:]
        (out_ref, y_ref, cdf_ref, proj_hbm, wint_keep, wout_keep), refs = refs[:6], refs[6:]
        if exchanging:
            c_ref, wada_ref, bada_ref = lead
            (acts_ref, mod_ref), refs = refs[:2], refs[2:]
        else:
            mod_ref, = lead
        host_out, refs = refs[:n_host], refs[n_host:]
        (wint_v, wout_v, h_buf, proj_blk, proj_tile, halo_ref, w_send, w_recv, w_local, p_sems,
         t_sems), refs = refs[:11], refs[11:]
        if n_host:
            refs, (n_send, n_recv, n_local) = refs[:-3], refs[-3:]
        weights = _MixWeights(layer, wpool_ref, pscale_ref, slng_ref, slnb_ref, wsgu_ref, bsgut_ref)
        tile = pl.program_id(0)

        def hosted():
            return [_TwoLevelGather(host_out[n], n_send.at[n], n_recv.at[n], src=host_in[n]) for n in range(n_host)]

        def hosted_own():
            index = _index(_me())
            return [pltpu.make_async_copy(host_in[n], host_out[n].at[index], n_local.at[n]) for n in range(n_host)]

        def gathers():
            return (_TwoLevelGather(wint_v, w_send.at[0], w_recv.at[0], src=wint_hbm),
                    _TwoLevelGather(wout_v, w_send.at[1], w_recv.at[1], src=wout_hbm))

        def keeps():
            return [pltpu.make_async_copy(wint_v, wint_keep, w_local.at[2]),
                    pltpu.make_async_copy(wout_v, wout_keep, w_local.at[3])]

        def tile_read(t):
            slot = t % 2
            return pltpu.make_async_copy(proj_hbm.at[pl.ds(pl.multiple_of(t * TM, TM), TM)], proj_tile.at[slot],
                                         t_sems.at[slot])

        @pl.when(tile == 0)
        def _():
            me = _me()
            halo_ref[...] = jnp.zeros_like(halo_ref)
            gather_in, gather_out = gathers()
            own_in = pltpu.make_async_copy(wint_hbm, gather_in._slot(me), w_local.at[0])
            own_out = pltpu.make_async_copy(wout_hbm, gather_out._slot(me), w_local.at[1])

            if exchanging:
                act_all, act_src, part, mod_recv, a_send, a_recv, m_send, m_recv = refs
                mine_index = _index(me)
                cval = c_ref[...]
                act_src[...] = jnp.zeros_like(act_src)
                act_src[0:1, :] = cval * jax.nn.sigmoid(cval)
                act_all[mine_index] = act_src[...]
                act_copies = _direct_exchange(lambda p: act_src, lambda m: act_all.at[_index(m)], a_send, a_recv)

            own_in.start()
            own_out.start()
            gather_in.send_first()

            if exchanging:
                _wait_direct(act_copies)
                acts = jnp.concatenate([act_all[j, 0:1, :] for j in range(N_DEV)], axis=0)
                acts_ref[...] = acts
                part[...] = jnp.zeros_like(part)
                for l in range(DEPTH):
                    res = lax.dot_general(acts, wada_ref[l], (((1,), (0,)), ((), ())), preferred_element_type=F32,
                                          precision=lax.Precision.HIGHEST)
                    for b in range(N_DEV):
                        part[b, l:l + 1, :] = res[b:b + 1, :]
                mod_recv[mine_index] = part[mine_index]
                mod_copies = _direct_exchange(lambda p: part.at[_index(p)], lambda m: mod_recv.at[_index(m)],
                                              m_send, m_recv)
            gather_in.send_second()
            gather_out.send_mine()

            if exchanging:
                _wait_direct(mod_copies)
                for l in range(DEPTH):
                    for j in range(N_DEV):
                        sl = slice(j * cols, (j + 1) * cols)
                        mod_ref[l:l + 1, sl] = mod_recv[j, l:l + 1, :] + bada_ref[l:l + 1, sl]
            shift = mod_ref[layer:layer + 1, 0:D_MODEL]
            scale = mod_ref[layer:layer + 1, D_MODEL:2 * D_MODEL]
            for t in range(N_TILES):
                rows = pl.ds(t * TM, TM)
                xn, _ = _ln(x_ref[rows, :])
                h_buf[rows, :] = (xn * (1.0 + scale) + shift).astype(BF16)

            chip_of = lambda dev: 2 * dev[0] + dev[1]
            writes = []

            def project(n, dev):
                first = pl.multiple_of(chip_of(dev) * pair, pair)
                if n >= 2:
                    writes[n - 2].wait()
                proj_blk[n % 2] = _dot_nt(h_buf[...], wint_v[pl.ds(first, pair), :])
                cp = pltpu.make_async_copy(proj_blk.at[n % 2], proj_hbm.at[:, pl.ds(first, pair)], p_sems.at[n % 2])
                cp.start()
                writes.append(cp)

            gather_in.relay()
            for cp in hosted_own():
                cp.start()
            for g in hosted():
                g.send_mine()
            own_in.wait()
            gather_in.wait_sibling()
            project(0, me)
            gather_in.pass_near()
            gather_in.wait_passed(ACROSS_X)
            project(1, _peer(ACROSS_X))
            gather_out.relay()
            gather_in.wait_passed(ACROSS_Y)
            project(2, _peer(ACROSS_Y))
            gather_in.pass_far()
            gather_in.wait_passed(ACROSS_BOTH)
            project(3, _peer(ACROSS_BOTH))

            gather_out.pass_on()
            gather_out.wait_rest()
            own_out.wait()
            for cp in keeps():
                cp.start()
            writes[2].wait()
            writes[3].wait()
            tile_read(0).start()

        @pl.when(tile + 1 < N_TILES)
        def _():
            tile_read(tile + 1).start()

        if n_host:
            @pl.when(tile == 1)
            def _():
                for g in hosted():
                    g.relay()

            @pl.when(tile == N_TILES // 2)
            def _():
                for g in hosted():
                    g.pass_near()

        tile_read(tile).wait()
        xt = x_ref[pl.ds(pl.multiple_of(tile * TM, TM), TM), :]
        gate = mod_ref[layer:layer + 1, 2 * D_MODEL:]
        proj = proj_tile[tile % 2]
        cat, cdf_ref[...] = _mix_forward(proj, halo_ref[...], tile, weights)
        halo_ref[...] = proj[TM - HALO:, 0:D_POOL]
        y = _dot(cat.astype(BF16), wout_v[...])
        y_ref[...] = y
        zn, _ = _ln(ALPHA * xt + gate * y)
        out_ref[...] = zn * lng_ref[layer:layer + 1, :] + lnb_ref[layer:layer + 1, :]

        @pl.when(tile == N_TILES - 1)
        def _():
            for g in hosted():
                g.pass_far()
            for g in hosted():
                g.wait_rest()
            for g in list(gathers()) + hosted():
                g.wait_sends()
            for cp in keeps() + hosted_own():
                cp.wait()

    row = lambda w: pl.BlockSpec((TM, w), lambda i: (i, 0))
    gather_sems = pltpu.SemaphoreType.DMA((2, GATHER_SEMS))
    seven = pltpu.SemaphoreType.DMA((7,))
    lead_in = list(exchange) if exchanging else [mod]
    lead_specs = [_const_spec(a.shape) for a in lead_in]
    extra_out_specs = [_const_spec((N_DEV, D_MODEL)), _const_spec((DEPTH, 3 * D_MODEL))] if exchanging else []
    extra_out_shape = [jax.ShapeDtypeStruct((N_DEV, D_MODEL), F32),
                       jax.ShapeDtypeStruct((DEPTH, 3 * D_MODEL), F32)] if exchanging else []
    extra_scratch = [pltpu.VMEM((N_DEV, 8, D_MODEL), F32), pltpu.VMEM((8, D_MODEL), F32),
                     pltpu.VMEM((N_DEV, 8, cols), F32), pltpu.VMEM((N_DEV, 8, cols), F32),
                     seven, seven, seven, seven] if exchanging else []
    host_scratch = [pltpu.SemaphoreType.DMA((n_host, GATHER_SEMS)), pltpu.SemaphoreType.DMA((n_host, GATHER_SEMS)),
                    pltpu.SemaphoreType.DMA((n_host,))] if n_host else []
    return pl.pallas_call(
        body,
        name=name,
        grid=(N_TILES,),
        in_specs=[_const_spec((SEQ, D_MODEL))] + lead_specs + [_const_spec(s) for s in SMALL_SPECS]
                 + [_const_spec((DEPTH, D_MODEL)), _const_spec((DEPTH, D_MODEL)), ANY, ANY] + [ANY] * n_host,
        out_specs=[row(D_MODEL), row(D_MODEL), row(2 * D_SGU), ANY, ANY, ANY] + extra_out_specs + [ANY] * n_host,
        out_shape=[jax.ShapeDtypeStruct((SEQ, D_MODEL), F32), jax.ShapeDtypeStruct((SEQ, D_MODEL), F32),
                   jax.ShapeDtypeStruct((SEQ, 2 * D_SGU), F32), jax.ShapeDtypeStruct((SEQ, D_IN), F32),
                   jax.ShapeDtypeStruct((D_IN, D_MODEL), BF16), jax.ShapeDtypeStruct((D_MODEL, D_MODEL), BF16)]
                  + extra_out_shape + [jax.ShapeDtypeStruct((N_DEV,) + blk.shape, blk.dtype) for blk in host],
        scratch_shapes=[pltpu.VMEM((D_IN, D_MODEL), BF16), pltpu.VMEM((D_MODEL, D_MODEL), BF16),
                        pltpu.VMEM((SEQ, D_MODEL), BF16), pltpu.VMEM((2, SEQ, pair), F32),
                        pltpu.VMEM((2, TM, D_IN), F32), pltpu.VMEM((HALO, D_POOL), F32),
                        gather_sems, gather_sems, pltpu.SemaphoreType.DMA((4,)), pltpu.SemaphoreType.DMA((2,)),
                        pltpu.SemaphoreType.DMA((2,))] + extra_scratch + host_scratch,
        compiler_params=pltpu.CompilerParams(dimension_semantics=("arbitrary",), vmem_limit_bytes=VMEM_LIMIT),
    )(x, *lead_in, *small, ln_g, ln_b, *mine, *host)


ADA_CHUNK = 256


def _grad_tail(dproj, h, cat, dy, small, dmod8, loss_lanes, w_ada, m_ada, v_ada, act_t, b_ada, m_bada, v_bada):
    shard_in, shard_out, shard_small = D_IN // N_DEV, D_MODEL // N_DEV, small.shape[2]
    cols = w_ada.shape[2]
    W_IN, W_OUT, SMALL = 0, 1, 2

    def body(dproj_hbm, h_hbm, cat_hbm, dy_hbm, small_hbm, dmod_ref, lanes_ref, wada_hbm, mada_hbm, vada_hbm,
             act_ref, bada_ref, mbada_ref, vbada_ref,
             gwin_ref, gwout_ref, stot_ref, loss_ref, gada_hbm, dada_hbm, nmada_hbm, nvada_hbm,
             gb_ref, db_ref, nmb_ref, nvb_ref,
             dproj_v, h_v, cat_v, dy_v, part_in, part_out, own_small, loss_src, loss_all, dmod_all, ada_in, ada_out,
             *rest):
        bufs, rest = rest[:13], rest[13:]
        load_sems, rs_sems = rest[0], rest[1:6]
        m_send, m_recv, g_send, g_recv, s_send, s_recv, ada_lsem, ada_ssem = rest[6:]
        mine = _index(_me())

        def update_ada():
            upper = (mine % 2) == 1

            def dmod_of(layer):
                rows = []
                for b in range(N_DEV):
                    r = dmod_all[b, pl.ds(4 * layer + mine // 2, 1), :]
                    rows.append(jnp.where(upper, r[:, cols:], r[:, :cols]))
                return jnp.concatenate(rows, axis=0)

            chunks = [(layer, c) for layer in range(DEPTH) for c in range(D_MODEL // ADA_CHUNK)]

            def loads(i):
                layer, c = chunks[i]
                rows = pl.ds(c * ADA_CHUNK, ADA_CHUNK)
                return [pltpu.make_async_copy(src.at[layer, rows], ada_in.at[i % 2, k], ada_lsem.at[i % 2, k])
                        for k, src in enumerate((wada_hbm, mada_hbm, vada_hbm))]

            def stores(i):
                layer, c = chunks[i]
                rows = pl.ds(c * ADA_CHUNK, ADA_CHUNK)
                return [pltpu.make_async_copy(ada_out.at[i % 2, k], dst.at[layer, rows], ada_ssem.at[i % 2, k])
                        for k, dst in enumerate((gada_hbm, dada_hbm, nmada_hbm, nvada_hbm))]

            for cp in loads(0):
                cp.start()
            dmods = {}
            for i, (layer, c) in enumerate(chunks):
                if i + 1 < len(chunks):
                    for cp in loads(i + 1):
                        cp.start()
                for cp in loads(i):
                    cp.wait()
                if i >= 2:
                    for cp in stores(i - 2):
                        cp.wait()
                if layer not in dmods:
                    dmods[layer] = dmod_of(layer)
                act = act_ref[pl.ds(c * ADA_CHUNK, ADA_CHUNK), :]
                g = act[:, 0:1] * dmods[layer][0:1, :]
                for b in range(1, N_DEV):
                    g = g + act[:, b:b + 1] * dmods[layer][b:b + 1, :]
                slot = i % 2
                delta, new_m, new_v = _adamw_math(ada_in[slot, 0], g, ada_in[slot, 1], ada_in[slot, 2])
                ada_out[slot, 0] = g
                ada_out[slot, 1] = delta
                ada_out[slot, 2] = new_m
                ada_out[slot, 3] = new_v
                for cp in stores(i):
                    cp.start()
            for i in (len(chunks) - 2, len(chunks) - 1):
                for cp in stores(i):
                    cp.wait()

            total = dmod_all[0]
            for b in range(1, N_DEV):
                total = total + dmod_all[b]
            width = total.shape[1]
            for layer in range(DEPTH):
                for q in range(4):
                    gb_ref[layer:layer + 1, q * width:(q + 1) * width] = total[4 * layer + q:4 * layer + q + 1, :]
            db_ref[...], nmb_ref[...], nvb_ref[...] = _adamw_math(bada_ref[...], gb_ref[...], mbada_ref[...],
                                                                  vbada_ref[...])

        order = (ACROSS_BOTH, ACROSS_X, ACROSS_Y, None)
        chips = [_ChipReduceScatter._chip(r) for r in order]
        loads = [pltpu.make_async_copy(s, d, load_sems.at[n]) for n, (s, d) in enumerate(
            ((cat_hbm, cat_v), (dy_hbm, dy_v), (h_hbm, h_v)))]
        loads += [pltpu.make_async_copy(dproj_hbm.at[:, pl.ds(pl.multiple_of(chip * 2 * shard_in, 2 * shard_in),
                                                             2 * shard_in)], dproj_v.at[n], load_sems.at[3 + n])
                  for n, chip in enumerate(chips)]
        for cp in loads:
            cp.start()
        arrays = [dict(part=part_in, out=gwin_ref, staged=False, sib=bufs[0], snd=bufs[1], rcv=bufs[2], relay=bufs[3]),
                  dict(part=part_out, out=gwout_ref, staged=False, sib=bufs[4], snd=bufs[5], rcv=bufs[6],
                       relay=bufs[7]),
                  dict(part=small_hbm, out=own_small, staged=True, stage=bufs[8], sib=bufs[9], snd=bufs[10],
                       rcv=bufs[11], relay=bufs[12])]
        scatter = _ChipReduceScatter(arrays, *rs_sems)
        scatter.start([SMALL])
        dmod_all[mine] = dmod_ref[...]
        dmod_copies = _direct_exchange(lambda p: dmod_ref, lambda m: dmod_all.at[_index(m)], m_send, m_recv)
        loss_src[...] = jnp.full(loss_src.shape, (0.5 / D_MODEL) * jnp.sum(lanes_ref[...]), F32)
        loss_all[mine] = loss_src[...]
        loss_copies = _direct_exchange(lambda p: loss_src, lambda m: loss_all.at[_index(m)], s_send, s_recv)

        loads[0].wait()
        loads[1].wait()
        for blk in range(2):
            res = _dot_tn(cat_v[:, blk * 512:(blk + 1) * 512], dy_v[...]).astype(BF16)
            for s in range(4):
                part_out[2 * blk + s // 2, s % 2] = res[s * shard_out:(s + 1) * shard_out]
        scatter.start([W_OUT])
        scatter.exchange([SMALL])

        gather = _TwoLevelGather(stot_ref, g_send, g_recv)
        loads[2].wait()
        for n, chip in enumerate(chips):
            loads[3 + n].wait()
            res = _dot_tn(dproj_v[n], h_v[...]).astype(BF16)
            part_in[chip, 0] = res[:shard_in]
            part_in[chip, 1] = res[shard_in:]
            scatter.start([W_IN], chips=[chip])
            if n == 0:
                scatter.exchange([W_OUT])
                scatter.fold([SMALL])
            if n == 1:
                scatter.send_far([W_IN])
                scatter.fold([W_OUT])
                scatter.finish([SMALL])
                stot_ref[mine] = own_small[...]
                gather.send_mine()
            if n == 2:
                scatter.send_near(ACROSS_X, [W_IN])
                gather.relay()
            if n == 3:
                scatter.send_near(ACROSS_Y, [W_IN])
        scatter.fold([W_IN])
        scatter.finish([W_OUT])
        _wait_direct(dmod_copies)
        update_ada()
        gather.pass_on()
        gather.wait_rest()
        _wait_direct(loss_copies)
        total = loss_all[0]
        for j in range(1, N_DEV):
            total = total + loss_all[j]
        loss_ref[...] = total
        scatter.finish([W_IN])
        gather.wait_sends()
        scatter.wait_sends()

    buffers = _ChipReduceScatter.buffers
    comm_scratch = (buffers(shard_in, D_MODEL, BF16, staged=False) + buffers(shard_out, D_MODEL, BF16, staged=False)
                    + buffers(shard_small, 128, F32))
    comm_scratch += [pltpu.SemaphoreType.DMA((7,))] + _ChipReduceScatter.semaphores(3)
    comm_scratch += [pltpu.SemaphoreType.DMA((n,)) for n in (7, 7, GATHER_SEMS, GATHER_SEMS, 7, 7)]
    comm_scratch += [pltpu.SemaphoreType.DMA((2, 3)), pltpu.SemaphoreType.DMA((2, 4))]
    return pl.pallas_call(
        body,
        name="grad_tail",
        in_specs=[ANY] * 5 + [VMEM, VMEM] + [ANY] * 3 + [VMEM] * 4,
        out_specs=[VMEM] * 4 + [ANY] * 4 + [VMEM] * 4,
        out_shape=[jax.ShapeDtypeStruct((shard_in, D_MODEL), F32), jax.ShapeDtypeStruct((shard_out, D_MODEL), F32),
                   jax.ShapeDtypeStruct((N_DEV, shard_small, 128), F32), jax.ShapeDtypeStruct((8, 128), F32)]
                  + [jax.ShapeDtypeStruct(w_ada.shape, F32)] * 4 + [jax.ShapeDtypeStruct(b_ada.shape, F32)] * 4,
        scratch_shapes=[pltpu.VMEM((4, SEQ, 2 * shard_in), BF16), pltpu.VMEM(h.shape, BF16), pltpu.VMEM(cat.shape, BF16),
                        pltpu.VMEM(dy.shape, BF16), pltpu.VMEM((4, 2, shard_in, D_MODEL), BF16),
                        pltpu.VMEM((4, 2, shard_out, D_MODEL), BF16), pltpu.VMEM((shard_small, 128), F32),
                        pltpu.VMEM((8, 128), F32), pltpu.VMEM((N_DEV, 8, 128), F32),
                        pltpu.VMEM((N_DEV,) + dmod8.shape, F32), pltpu.VMEM((2, 3, ADA_CHUNK, cols), F32),
                        pltpu.VMEM((2, 4, ADA_CHUNK, cols), F32)] + comm_scratch,
        compiler_params=pltpu.CompilerParams(vmem_limit_bytes=VMEM_LIMIT),
    )(dproj, h, cat, dy, small, dmod8, loss_lanes, w_ada, m_ada, v_ada, act_t, b_ada, m_bada, v_bada)


SMALL_NAMES = ("w_pool", "w_sgu", "pool_scale", "sgu_ln_g", "sgu_ln_b", "b_sgu", "ln_g", "ln_b")
SMALL_ROWS = (512, 512, 4, 4, 4, 4, 8, 8)


def _adamw_small(g_packed, ws, ms, vs, name):
    n = len(SMALL_NAMES)

    def body(g_ref, *refs):
        w_refs, m_refs, v_refs = refs[:n], refs[n:2 * n], refs[2 * n:3 * n]
        outs = refs[3 * n:]

        def update(p, at, g):
            delta, new_m, new_v = _adamw_math(w_refs[p][at], g, m_refs[p][at], v_refs[p][at])
            outs[p][at] = g
            outs[n + p][at] = delta
            outs[2 * n + p][at] = new_m
            outs[3 * n + p][at] = new_v

        row = 0
        for p, r in enumerate(SMALL_ROWS):
            shape = ws[p].shape
            for layer in range(DEPTH):
                first = layer * PACK_ROWS + row
                if len(shape) == 4:
                    for k in range(shape[1]):
                        update(p, (layer, k), g_ref[first + k * shape[2]:first + (k + 1) * shape[2], :])
                elif len(shape) == 3:
                    update(p, (layer,), g_ref[first:first + r, :])
                else:
                    g = jnp.concatenate([g_ref[first + k:first + k + 1, :] for k in range(r)], axis=1)
                    update(p, (slice(layer, layer + 1), slice(None)), g)
            row += r

    res = pl.pallas_call(
        body,
        name=name,
        out_shape=[jax.ShapeDtypeStruct(w.shape, F32) for w in ws] * 4,
        compiler_params=pltpu.CompilerParams(vmem_limit_bytes=VMEM_LIMIT),
    )(g_packed, *ws, *ms, *vs)
    return res[:n], res[n:2 * n], res[2 * n:3 * n], res[3 * n:]


def kernel(x, c, w_ada, b_ada, w_in, w_pool, pool_scale, sgu_ln_g, sgu_ln_b, w_sgu, b_sgu, w_out, ln_g, ln_b, loss_target, m_w_ada, m_b_ada, m_w_in, m_w_pool, m_pool_scale, m_sgu_ln_g, m_sgu_ln_b, m_w_sgu, m_b_sgu, m_w_out, m_ln_g, m_ln_b, v_w_ada, v_b_ada, v_w_in, v_w_pool, v_pool_scale, v_sgu_ln_g, v_sgu_ln_b, v_w_sgu, v_b_sgu, v_w_out, v_ln_g, v_ln_b):
    small_w = dict(w_pool=w_pool, w_sgu=w_sgu, pool_scale=pool_scale, sgu_ln_g=sgu_ln_g, sgu_ln_b=sgu_ln_b,
                   b_sgu=b_sgu, ln_g=ln_g, ln_b=ln_b)
    small_m = dict(w_pool=m_w_pool, w_sgu=m_w_sgu, pool_scale=m_pool_scale, sgu_ln_g=m_sgu_ln_g,
                   sgu_ln_b=m_sgu_ln_b, b_sgu=m_b_sgu, ln_g=m_ln_g, ln_b=m_ln_b)
    small_v = dict(w_pool=v_w_pool, w_sgu=v_w_sgu, pool_scale=v_pool_scale, sgu_ln_g=v_sgu_ln_g,
                   sgu_ln_b=v_sgu_ln_b, b_sgu=v_b_sgu, ln_g=v_ln_g, ln_b=v_ln_b)

    wint_loc = jnp.transpose(w_in, (0, 2, 1)).astype(BF16)
    wout_loc = w_out.astype(BF16)
    small = (w_pool, pool_scale, sgu_ln_g, sgu_ln_b, w_sgu, jnp.transpose(b_sgu, (0, 2, 1)))
    out, y, cdf, proj, wint0, wout0, act_all, mod, wint1, wout1 = _gathered_layer(
        0, x[0], small, ln_g, ln_b, [wint_loc[0], wout_loc[0]], "layer_fwd_0", exchange=(c, w_ada, b_ada),
        host=[wint_loc[1], wout_loc[1]])
    w_int, w_outf = [wint0, wint1.reshape(D_IN, D_MODEL)], [wout0, wout1.reshape(D_MODEL, D_MODEL)]
    acts, cur = [(x[0], proj, y, cdf)], out
    for l in range(1, DEPTH):
        out, proj, y, cdf = _layer_forward(l, cur, mod, w_int[l], w_outf[l], small, ln_g, ln_b, f"layer_fwd_{l}")
        acts.append((cur, proj, y, cdf))
        cur = out

    shard_in, shard_out = D_IN // N_DEV, D_MODEL // N_DEV
    a, b = cur, loss_target[0]
    loss_lanes, carry, pending = None, (), []
    g_w_in_t, g_w_out = [None] * DEPTH, [None] * DEPTH
    for l in reversed(range(DEPTH)):
        dx, dproj, h, cat, dy, small_grads, dmod8, lanes, *shards = _layer_backward(
            l, a, b, *acts[l], mod, w_int[l], w_outf[l], small, ln_g, l == DEPTH - 1, f"layer_bwd_{l}",
            carry=carry, reduce=pending)
        if shards:
            g_w_in_t[l + 1], g_w_out[l + 1] = shards
        if l == DEPTH - 1:
            loss_lanes = lanes
        if l > 0:
            pending = [_grad_matmul(dproj, h, 640, f"grad_w_in_{l}").reshape(4, 2, shard_in, D_MODEL),
                       _grad_matmul(cat, dy, 512, f"grad_w_out_{l}").reshape(4, 2, shard_out, D_MODEL)]
        carry = (small_grads, dmod8)
        a = b = dx
    grad_x = a[None]

    (g_w_in_t[0], g_w_out[0], small_tot, loss_tile, g_w_ada, d_w_ada, nm_w_ada, nv_w_ada,
     g_b_ada, d_b_ada, nm_b_ada, nv_b_ada) = _grad_tail(
        dproj, h, cat, dy, small_grads.reshape(4, 2, DEPTH * PACK_ROWS // N_DEV, 128), dmod8, loss_lanes,
        w_ada, m_w_ada, v_w_ada, jnp.transpose(act_all), b_ada, m_b_ada, v_b_ada)
    loss = loss_tile[0, 0]

    flat = lambda t: t.reshape(-1, t.shape[-1])
    to_t = lambda t: flat(jnp.transpose(t, (0, 2, 1)))
    from_t = lambda t: jnp.transpose(t.reshape(DEPTH, shard_in, D_MODEL), (0, 2, 1))
    g_w_in, d_w_in, nm_w_in, nv_w_in = [from_t(t) for t in _adamw(to_t(w_in), g_w_in_t, to_t(m_w_in), to_t(v_w_in),
                                                                  shard_in // 2, "adamw_w_in")]
    gwout, d_w_out, nm_w_out, nv_w_out = [t.reshape(w_out.shape) for t in _adamw(
        flat(w_out), g_w_out, flat(m_w_out), flat(v_w_out), shard_out, "adamw_w_out")]
    small_out = _adamw_small(small_tot.reshape(DEPTH * PACK_ROWS, 128), [small_w[n] for n in SMALL_NAMES],
                             [small_m[n] for n in SMALL_NAMES], [small_v[n] for n in SMALL_NAMES], "adamw_small")
    gs, ds, ms, vs = [dict(zip(SMALL_NAMES, group)) for group in small_out]

    def ordered(w_ada_, b_ada_, w_in_, small, w_out_):
        return (w_ada_, b_ada_, w_in_, small["w_pool"], small["pool_scale"], small["sgu_ln_g"], small["sgu_ln_b"],
                small["w_sgu"], small["b_sgu"], w_out_, small["ln_g"], small["ln_b"])

    return (loss, grad_x,
            *ordered(g_w_ada, g_b_ada, g_w_in, gs, gwout),
            *ordered(d_w_ada, d_b_ada, d_w_in, ds, d_w_out),
            *ordered(nm_w_ada, nm_b_ada, nm_w_in, ms, nm_w_out),
            *ordered(nv_w_ada, nv_b_ada, nv_w_in, vs, nv_w_out))
```

```python
import jax
import jax.numpy as jnp
from jax import lax
from jax.experimental import pallas as pl
from jax.experimental.pallas import tpu as pltpu

F32 = jnp.float32
BF16 = jnp.bfloat16

D_MODEL = 1024
SEQ = 2048
DEPTH = 2
D_POOL = 512
D_SGU = 512
D_IN = 2560
N_GROUPS = 4
GROUP = 128
N_HEADS = 4
HEAD = 128
CHUNK = 128
WINDOWS = (2, 4, 8, 16)
ALPHA = (2.0 * DEPTH) ** 0.25
LN_EPS = 1e-5
N_DEV = 8

ADAM_LR = 0.001
ADAM_B1 = 0.9
ADAM_B2 = 0.999
ADAM_EPS = 1e-08
ADAM_WD = 0.01
ADAM_STEP = 10

TM = 256
HALO = 16
N_TILES = SEQ // TM
VMEM_LIMIT = 60 * 1024 * 1024

ROW_WPOOL = 0
ROW_WSGU = 512
ROW_PSCALE = 1024
ROW_SLNG = 1028
ROW_SLNB = 1032
ROW_BSGU = 1036
ROW_LNG = 1040
ROW_LNB = 1048
PACK_ROWS = 1088
DMOD_COLS = DEPTH * 3 * D_MODEL // 8

SQRT_HALF = 0.7071067811865476
INV_SQRT_2PI = 0.3989422804014327


def _ln(x):
    mu = jnp.mean(x, axis=-1, keepdims=True)
    xc = x - mu
    var = jnp.mean(xc * xc, axis=-1, keepdims=True)
    rstd = lax.rsqrt(var + LN_EPS)
    return xc * rstd, rstd


def _ln_bwd(dxn, xn, rstd):
    m1 = jnp.mean(dxn, axis=-1, keepdims=True)
    m2 = jnp.mean(dxn * xn, axis=-1, keepdims=True)
    return rstd * (dxn - m1 - xn * m2)


def _normal_cdf(x):
    return 0.5 * (1.0 + lax.erf(x * SQRT_HALF))


def _gelu_parts(x, cdf, with_grad):
    if not with_grad:
        return x * cdf, None
    return x * cdf, cdf + x * (INV_SQRT_2PI * jnp.exp(-0.5 * x * x))


def _silu_parts(x):
    s = jax.nn.sigmoid(x)
    return x * s, s * (1.0 + x * (1.0 - s))


def _dot(a, b):
    return lax.dot_general(a, b, (((1,), (0,)), ((), ())), preferred_element_type=F32)


def _dot_nt(a, b):
    return lax.dot_general(a, b, (((1,), (1,)), ((), ())), preferred_element_type=F32)


def _dot_tn(a, b):
    return lax.dot_general(a, b, (((0,), (0,)), ((), ())), preferred_element_type=F32)


def _row_index(tile):
    return tile * TM + lax.broadcasted_iota(jnp.int32, (TM, 1), 0)


def _window_sums(ext, forward):
    n = TM + HALO
    cur = ext
    outs = []
    for g in range(N_GROUPS):
        step = 1 << g
        cur = cur + pltpu.roll(cur, step if forward else n - step, 0)
        rows = cur[HALO:, :GROUP] if forward else cur[:TM, :GROUP]
        outs.append(rows)
        cur = cur[:, GROUP:] if g + 1 < N_GROUPS else None
    return outs


def _inverse_counts(rows):
    return [1.0 / jnp.minimum(rows + 1, w).astype(F32) for w in WINDOWS]


def _tril_bf16(w):
    t = lax.broadcasted_iota(jnp.int32, (CHUNK, CHUNK), 0)
    s = lax.broadcasted_iota(jnp.int32, (CHUNK, CHUNK), 1)
    return jnp.where(t >= s, w, 0.0).astype(BF16)


class _MixWeights:
    def __init__(self, layer, wpool_ref, pscale_ref, slng_ref, slnb_ref, wsgu_ref, bsgut_ref):
        self.layer = layer
        self.wpool_ref, self.pscale_ref, self.slng_ref, self.slnb_ref = wpool_ref, pscale_ref, slng_ref, slnb_ref
        self.wsgu_ref, self.bsgut_ref = wsgu_ref, bsgut_ref

    def pool(self, g):
        return self.wpool_ref[self.layer, g].astype(BF16)

    def pool_scale(self, g):
        return self.pscale_ref[self.layer:self.layer + 1, g * GROUP:(g + 1) * GROUP]

    def ln_gain(self, h):
        return self.slng_ref[self.layer, h:h + 1, :]

    def ln_bias(self, h):
        return self.slnb_ref[self.layer, h:h + 1, :]

    def mix(self, h):
        return _tril_bf16(self.wsgu_ref[self.layer, h])

    def mix_bias(self, h):
        return self.bsgut_ref[self.layer, :, h:h + 1]


SMALL_SPECS = ((DEPTH, N_GROUPS, GROUP, GROUP), (DEPTH, D_POOL), (DEPTH, N_HEADS, HEAD), (DEPTH, N_HEADS, HEAD),
               (DEPTH, N_HEADS, CHUNK, CHUNK), (DEPTH, CHUNK, N_HEADS))


def _mix_forward(proj, halo, tile, w, cdf=None):
    keep = cdf is not None
    rows = _row_index(tile)
    inv_counts = _inverse_counts(rows)
    xa = proj[:, 0:D_POOL]
    ga = proj[:, D_POOL:2 * D_POOL]
    sums = _window_sums(jnp.concatenate([halo, xa], axis=0), True)
    ga_act, ga_grad = _silu_parts(ga)
    pooled, pw, ya = [], [], []
    for g in range(N_GROUPS):
        sl = slice(g * GROUP, (g + 1) * GROUP)
        p = (sums[g] * inv_counts[g] - xa[:, sl]).astype(BF16)
        q = _dot(p, w.pool(g))
        pooled.append(p)
        pw.append(q)
        ya.append(q * w.pool_scale(g) * ga_act[:, sl])

    u = proj[:, 2 * D_POOL:2 * D_POOL + D_SGU]
    v = proj[:, 2 * D_POOL + D_SGU:2 * D_POOL + 2 * D_SGU]
    gb = proj[:, 2 * D_POOL + 2 * D_SGU:]
    gb_act, gb_grad = _silu_parts(gb)
    if cdf is None:
        cdf = jnp.concatenate([_normal_cdf(u), _normal_cdf(v)], axis=1)
    u_act, u_grad = _gelu_parts(u, cdf[:, :D_SGU], keep)
    v_act, v_grad = _gelu_parts(v, cdf[:, D_SGU:], keep)
    vn, vrstd, vln, mixed, yb = [], [], [], [], []
    for h in range(N_HEADS):
        sl = slice(h * HEAD, (h + 1) * HEAD)
        n_h, r_h = _ln(v_act[:, sl])
        l_h = (n_h * w.ln_gain(h) + w.ln_bias(h)).astype(BF16)
        w_h = w.mix(h)
        bias = w.mix_bias(h)
        m_h = jnp.concatenate(
            [_dot(w_h, l_h[k * CHUNK:(k + 1) * CHUNK]) + bias for k in range(TM // CHUNK)], axis=0)
        vn.append(n_h)
        vrstd.append(r_h)
        vln.append(l_h)
        mixed.append(m_h)
        yb.append(u_act[:, sl] * m_h * gb_act[:, sl])
    cat = jnp.concatenate(ya + yb, axis=1)
    if not keep:
        return cat, cdf
    return cat, dict(inv_counts=inv_counts, ga_act=ga_act, ga_grad=ga_grad, pooled=pooled, pw=pw, u_grad=u_grad,
                     v_grad=v_grad, u_act=u_act, gb_act=gb_act, gb_grad=gb_grad, vn=vn, vrstd=vrstd, vln=vln,
                     mixed=mixed)


def _const_spec(shape):
    nd = len(shape)
    return pl.BlockSpec(shape, lambda i: (0,) * nd)


def _layer_forward(layer, x, mod, w_int, w_outf, small, ln_g, ln_b, name):
    def body(x_ref, mod_ref, wint_ref, wout_ref, wpool_ref, pscale_ref, slng_ref, slnb_ref, wsgu_ref, bsgut_ref,
             lng_ref, lnb_ref, out_ref, proj_ref, y_ref, cdf_ref, halo_ref):
        weights = _MixWeights(layer, wpool_ref, pscale_ref, slng_ref, slnb_ref, wsgu_ref, bsgut_ref)
        tile = pl.program_id(0)

        @pl.when(tile == 0)
        def _():
            halo_ref[...] = jnp.zeros_like(halo_ref)

        xt = x_ref[...]
        shift = mod_ref[layer:layer + 1, 0:D_MODEL]
        scale = mod_ref[layer:layer + 1, D_MODEL:2 * D_MODEL]
        gate = mod_ref[layer:layer + 1, 2 * D_MODEL:]
        xn, _ = _ln(xt)
        h = (xn * (1.0 + scale) + shift).astype(BF16)
        proj = _dot_nt(h, wint_ref[...])
        proj_ref[...] = proj
        cat, cdf_ref[...] = _mix_forward(proj, halo_ref[...], tile, weights)
        halo_ref[...] = proj[TM - HALO:, 0:D_POOL]
        y = _dot(cat.astype(BF16), wout_ref[...])
        y_ref[...] = y
        zn, _ = _ln(ALPHA * xt + gate * y)
        out_ref[...] = zn * lng_ref[layer:layer + 1, :] + lnb_ref[layer:layer + 1, :]

    row = lambda w: pl.BlockSpec((TM, w), lambda i: (i, 0))
    return pl.pallas_call(
        body,
        name=name,
        grid=(N_TILES,),
        in_specs=[row(D_MODEL), _const_spec((DEPTH, 3 * D_MODEL)), _const_spec((D_IN, D_MODEL)),
                  _const_spec((D_MODEL, D_MODEL))] + [_const_spec(s) for s in SMALL_SPECS]
                 + [_const_spec((DEPTH, D_MODEL)), _const_spec((DEPTH, D_MODEL))],
        out_specs=[row(D_MODEL), row(D_IN), row(D_MODEL), row(2 * D_SGU)],
        out_shape=[jax.ShapeDtypeStruct((SEQ, D_MODEL), F32), jax.ShapeDtypeStruct((SEQ, D_IN), F32),
                   jax.ShapeDtypeStruct((SEQ, D_MODEL), F32), jax.ShapeDtypeStruct((SEQ, 2 * D_SGU), F32)],
        scratch_shapes=[pltpu.VMEM((HALO, D_POOL), F32)],
        compiler_params=pltpu.CompilerParams(dimension_semantics=("arbitrary",), vmem_limit_bytes=VMEM_LIMIT),
    )(x, mod, w_int, w_outf, *small, ln_g, ln_b)


VEC_LNG, VEC_LNB, VEC_POOL, VEC_SGU, VEC_SHIFT, VEC_SCALE, VEC_GATE, VEC_LOSS = range(8)


def _layer_backward(layer, a, b, x, proj, y, cdf, mod, w_int, w_outf, small, ln_g, is_last, name, carry=(),
                    reduce=()):
    n_red, n_carry = len(reduce), len(carry)
    base = layer * PACK_ROWS

    def body(a_ref, b_ref, x_ref, proj_ref, prev_ref, y_ref, cdf_ref, mod_ref, wint_ref, wout_ref, wpool_ref,
             pscale_ref, slng_ref, slnb_ref, wsgu_ref, bsgut_ref, lng_ref, *rest):
        weights = _MixWeights(layer, wpool_ref, pscale_ref, slng_ref, slnb_ref, wsgu_ref, bsgut_ref)
        carry_refs, rest = rest[:n_carry], rest[n_carry:]
        part_refs, rest = rest[:n_red], rest[n_red:]
        dx_ref, dproj_ref, h_ref, cat_ref, dy_ref, small_ref, dmod_ref, loss_ref = rest[:8]
        shard_refs, rest = rest[8:8 + n_red], rest[8 + n_red:]
        vec_ref, dmix_ref, halo_ref = rest[:3]
        step = pl.program_id(0)
        tile = N_TILES - 1 - step

        def scatter():
            bufs, sems = rest[3:3 + 5 * n_red], rest[3 + 5 * n_red:]
            arrays = [dict(part=part_refs[n], out=shard_refs[n], staged=True, stage=bufs[5 * n], sib=bufs[5 * n + 1],
                           snd=bufs[5 * n + 2], rcv=bufs[5 * n + 3], relay=bufs[5 * n + 4]) for n in range(n_red)]
            return _ChipReduceScatter(arrays, *sems)

        @pl.when(step == 0)
        def _():
            small_ref[...] = jnp.zeros_like(small_ref)
            dmod_ref[...] = jnp.zeros_like(dmod_ref)
            vec_ref[...] = jnp.zeros_like(vec_ref)
            dmix_ref[...] = jnp.zeros_like(dmix_ref)
            halo_ref[...] = jnp.zeros_like(halo_ref)
            if n_red:
                scatter().start()

        if n_red:
            @pl.when(step == 1)
            def _():
                scatter().exchange()

            @pl.when(step == N_TILES // 2)
            def _():
                scatter().fold()

        def acc(row, lo, val):
            hi = lo + val.shape[1]
            vec_ref[row:row + 1, lo:hi] += jnp.sum(val, axis=0, keepdims=True)

        xt = x_ref[...]
        yt = y_ref[...]
        shift = mod_ref[layer:layer + 1, 0:D_MODEL]
        scale = mod_ref[layer:layer + 1, D_MODEL:2 * D_MODEL]
        gate = mod_ref[layer:layer + 1, 2 * D_MODEL:]
        ln_gain = lng_ref[layer:layer + 1, :]

        zn, zrstd = _ln(ALPHA * xt + gate * yt)
        if is_last:
            diff = a_ref[...] - b_ref[...]
            acc(VEC_LOSS, 0, diff * diff)
            dout = diff * (1.0 / D_MODEL)
        else:
            dout = a_ref[...]
        acc(VEC_LNG, 0, dout * zn)
        acc(VEC_LNB, 0, dout)
        dz = _ln_bwd(dout * ln_gain, zn, zrstd)
        acc(VEC_GATE, 0, dz * yt)
        dy = (dz * gate).astype(BF16)
        dy_ref[...] = dy
        dcat = _dot_nt(dy, wout_ref[...])

        proj = proj_ref[...]
        prev = jnp.where(tile > 0, prev_ref[...], 0.0)
        cat, k = _mix_forward(proj, prev, tile, weights, cdf_ref[...])
        cat_ref[...] = cat.astype(BF16)

        dga, dq = [], []
        for g in range(N_GROUPS):
            sl = slice(g * GROUP, (g + 1) * GROUP)
            pscale = weights.pool_scale(g)
            dya = dcat[:, sl]
            dyp = dya * k["ga_act"][:, sl]
            dga.append(dya * k["pw"][g] * pscale * k["ga_grad"][:, sl])
            acc(VEC_POOL, g * GROUP, dyp * k["pw"][g])
            dpw = (dyp * pscale).astype(BF16)
            rows = pl.ds(base + ROW_WPOOL + g * GROUP, GROUP)
            small_ref[rows, :] += _dot_tn(k["pooled"][g], dpw)
            dq.append(_dot_nt(dpw, weights.pool(g)))
        dpooled = jnp.concatenate(dq, axis=1)
        scaled = jnp.concatenate([dq[g] * k["inv_counts"][g] for g in range(N_GROUPS)], axis=1)
        sums = _window_sums(jnp.concatenate([scaled, halo_ref[...]], axis=0), False)
        halo_ref[...] = scaled[0:HALO]
        dxa = jnp.concatenate(sums, axis=1) - dpooled

        du, dv, dgb = [], [], []
        for h in range(N_HEADS):
            sl = slice(h * HEAD, (h + 1) * HEAD)
            dyb = dcat[:, D_POOL + h * HEAD:D_POOL + (h + 1) * HEAD]
            m_h = k["mixed"][h]
            ug = k["u_act"][:, sl] * dyb
            du.append(dyb * m_h * k["gb_act"][:, sl] * k["u_grad"][:, sl])
            dgb.append(ug * m_h * k["gb_grad"][:, sl])
            dmixed = ug * k["gb_act"][:, sl]
            dmixed_bf = dmixed.astype(BF16)
            w_h = weights.mix(h)
            dvln_parts = []
            dmix_sum = dmix_ref[h]
            wsgu_rows = pl.ds(base + ROW_WSGU + h * CHUNK, CHUNK)
            dws = small_ref[wsgu_rows, :]
            for c in range(TM // CHUNK):
                cs = slice(c * CHUNK, (c + 1) * CHUNK)
                dmix_sum = dmix_sum + dmixed[cs]
                dws = dws + _dot_nt(dmixed_bf[cs], k["vln"][h][cs])
                dvln_parts.append(_dot_tn(w_h, dmixed_bf[cs]))
            dmix_ref[h] = dmix_sum
            small_ref[wsgu_rows, :] = dws
            dvln = jnp.concatenate(dvln_parts, axis=0)
            acc(VEC_SGU, h * HEAD, dvln * k["vn"][h])
            acc(VEC_SGU, D_SGU + h * HEAD, dvln)
            dvv = _ln_bwd(dvln * weights.ln_gain(h), k["vn"][h], k["vrstd"][h])
            dv.append(dvv * k["v_grad"][:, sl])

        dproj = jnp.concatenate([dxa] + dga + du + dv + dgb, axis=1).astype(BF16)
        dproj_ref[...] = dproj
        dh = _dot(dproj, wint_ref[...])

        xn, xrstd = _ln(xt)
        h_ref[...] = (xn * (1.0 + scale) + shift).astype(BF16)
        acc(VEC_SCALE, 0, dh * xn)
        acc(VEC_SHIFT, 0, dh)
        dx_ref[...] = _ln_bwd(dh * (1.0 + scale), xn, xrstd) + ALPHA * dz

        @pl.when(step == N_TILES - 1)
        def _():
            def put(row0, vec_row, lo, n):
                for r in range(n):
                    small_ref[base + row0 + r:base + row0 + r + 1, :] = (
                        vec_ref[vec_row:vec_row + 1, lo + r * 128:lo + (r + 1) * 128])

            put(ROW_PSCALE, VEC_POOL, 0, 4)
            put(ROW_SLNG, VEC_SGU, 0, 4)
            put(ROW_SLNB, VEC_SGU, D_SGU, 4)
            put(ROW_LNG, VEC_LNG, 0, 8)
            put(ROW_LNB, VEC_LNB, 0, 8)
            ones = jnp.ones((8, HEAD), F32)
            t = lax.broadcasted_iota(jnp.int32, (CHUNK, CHUNK), 0)
            s = lax.broadcasted_iota(jnp.int32, (CHUNK, CHUNK), 1)
            for h in range(N_HEADS):
                bias_rows = lax.dot_general(ones, dmix_ref[h], (((1,), (1,)), ((), ())),
                                            preferred_element_type=F32, precision=lax.Precision.HIGHEST)
                small_ref[base + ROW_BSGU + h:base + ROW_BSGU + h + 1, :] = bias_rows[0:1]
                rows = pl.ds(base + ROW_WSGU + h * CHUNK, CHUNK)
                small_ref[rows, :] = jnp.where(t >= s, small_ref[rows, :], 0.0)
            pieces = ((0, VEC_SHIFT, 0, 768),
                      (1, VEC_SHIFT, 768, 256), (1, VEC_SCALE, 0, 512),
                      (2, VEC_SCALE, 512, 512), (2, VEC_GATE, 0, 256),
                      (3, VEC_GATE, 256, 768))
            filled = [0] * 4
            for q, vec_row, lo, n in pieces:
                row = 4 * layer + q
                dmod_ref[row:row + 1, filled[q]:filled[q] + n] = vec_ref[vec_row:vec_row + 1, lo:lo + n]
                filled[q] += n
            if n_carry:
                for other in range(layer + 1, DEPTH):
                    rows = pl.ds(other * PACK_ROWS, PACK_ROWS)
                    small_ref[rows, :] = carry_refs[0][rows, :]
                    dmod_ref[4 * other:4 * other + 4, :] = carry_refs[1][4 * other:4 * other + 4, :]
            loss_ref[...] = vec_ref[VEC_LOSS:VEC_LOSS + 1, :]
            if n_red:
                scatter().finish()
                scatter().wait_sends()

    rev = lambda w: pl.BlockSpec((TM, w), lambda i: (N_TILES - 1 - i, 0))
    prev_spec = pl.BlockSpec(
        (HALO, D_POOL), lambda i: (jnp.maximum((N_TILES - 1 - i) * (TM // HALO) - 1, 0), 0))
    comm_scratch = []
    for p in reduce:
        comm_scratch += _ChipReduceScatter.buffers(p.shape[2], p.shape[3], p.dtype)
    if n_red:
        comm_scratch += _ChipReduceScatter.semaphores(n_red)
    return pl.pallas_call(
        body,
        name=name,
        grid=(N_TILES,),
        in_specs=[rev(D_MODEL), rev(D_MODEL) if is_last else pl.BlockSpec((TM, D_MODEL), lambda i: (0, 0)),
                  rev(D_MODEL), rev(D_IN), prev_spec, rev(D_MODEL), rev(2 * D_SGU),
                  _const_spec((DEPTH, 3 * D_MODEL)), _const_spec((D_IN, D_MODEL)), _const_spec((D_MODEL, D_MODEL))]
                 + [_const_spec(s) for s in SMALL_SPECS] + [_const_spec((DEPTH, D_MODEL))]
                 + [_const_spec(c.shape) for c in carry] + [ANY] * n_red,
        out_specs=[rev(D_MODEL), rev(D_IN), rev(D_MODEL), rev(D_MODEL), rev(D_MODEL),
                   _const_spec((DEPTH * PACK_ROWS, 128)), _const_spec((8, DMOD_COLS)), _const_spec((1, D_MODEL))]
                  + [_const_spec(p.shape[2:]) for p in reduce],
        out_shape=[jax.ShapeDtypeStruct((SEQ, D_MODEL), F32), jax.ShapeDtypeStruct((SEQ, D_IN), BF16),
                   jax.ShapeDtypeStruct((SEQ, D_MODEL), BF16), jax.ShapeDtypeStruct((SEQ, D_MODEL), BF16),
                   jax.ShapeDtypeStruct((SEQ, D_MODEL), BF16), jax.ShapeDtypeStruct((DEPTH * PACK_ROWS, 128), F32),
                   jax.ShapeDtypeStruct((8, DMOD_COLS), F32), jax.ShapeDtypeStruct((1, D_MODEL), F32)]
                  + [jax.ShapeDtypeStruct(p.shape[2:], F32) for p in reduce],
        scratch_shapes=[pltpu.VMEM((8, D_MODEL), F32), pltpu.VMEM((N_HEADS, CHUNK, HEAD), F32),
                        pltpu.VMEM((HALO, D_POOL), F32)] + comm_scratch,
        compiler_params=pltpu.CompilerParams(dimension_semantics=("arbitrary",), vmem_limit_bytes=VMEM_LIMIT),
    )(a, b, x, proj, proj, y, cdf, mod, w_int, w_outf, *small, ln_g, *carry, *reduce)


def _grad_matmuls(dproj, h, cat, dy, name):
    in_cols, out_cols = D_IN // 4, D_MODEL // 2
    in_steps = D_IN // in_cols

    def body(dproj_ref, h_ref, cat_ref, dy_ref, gin_ref, gout_ref):
        step = pl.program_id(0)

        @pl.when(step < in_steps)
        def _():
            gin_ref[...] = _dot_tn(dproj_ref[...], h_ref[...]).astype(BF16)

        @pl.when(step >= in_steps)
        def _():
            gout_ref[...] = _dot_tn(cat_ref[...], dy_ref[...]).astype(BF16)

    in_block = lambda j: jnp.minimum(j, in_steps - 1)
    out_block = lambda j: jnp.maximum(j - in_steps, 0)
    return pl.pallas_call(
        body,
        name=name,
        grid=(in_steps + D_MODEL // out_cols,),
        in_specs=[pl.BlockSpec((SEQ, in_cols), lambda j: (0, in_block(j))), _const_spec((SEQ, D_MODEL)),
                  pl.BlockSpec((SEQ, out_cols), lambda j: (0, out_block(j))), _const_spec((SEQ, D_MODEL))],
        out_specs=[pl.BlockSpec((in_cols, D_MODEL), lambda j: (in_block(j), 0)),
                   pl.BlockSpec((out_cols, D_MODEL), lambda j: (out_block(j), 0))],
        out_shape=[jax.ShapeDtypeStruct((D_IN, D_MODEL), BF16), jax.ShapeDtypeStruct((D_MODEL, D_MODEL), BF16)],
        compiler_params=pltpu.CompilerParams(dimension_semantics=("arbitrary",), vmem_limit_bytes=VMEM_LIMIT),
    )(dproj, h, cat, dy)


def _adamw_math(w, g, m, v):
    m = ADAM_B1 * m + (1.0 - ADAM_B1) * g
    v = ADAM_B2 * v + (1.0 - ADAM_B2) * (g * g)
    m_hat = m / (1.0 - ADAM_B1 ** ADAM_STEP)
    v_hat = v / (1.0 - ADAM_B2 ** ADAM_STEP)
    delta = -ADAM_LR * (m_hat / (jnp.sqrt(v_hat) + ADAM_EPS) + ADAM_WD * w)
    return delta, m, v


def _adamw(w, grads, m, v, block_rows, name):
    rows, cols = grads[0].shape
    blocks = rows // block_rows

    def body(w_ref, m_ref, v_ref, *rest):
        g_refs, (g_ref, d_ref, nm_ref, nv_ref) = rest[:DEPTH], rest[DEPTH:]
        for layer in range(DEPTH):
            @pl.when(pl.program_id(0) == layer)
            def _():
                g = g_refs[layer][...]
                g_ref[...] = g
                d_ref[...], nm_ref[...], nv_ref[...] = _adamw_math(w_ref[...], g, m_ref[...], v_ref[...])

    def grad_spec(layer):
        return pl.BlockSpec((block_rows, cols),
                            lambda l, i: (jnp.where(l == layer, i, jnp.where(l < layer, 0, blocks - 1)), 0))

    spec = pl.BlockSpec((block_rows, cols), lambda l, i: (l * blocks + i, 0))
    return pl.pallas_call(
        body,
        name=name,
        grid=(DEPTH, blocks),
        in_specs=[spec] * 3 + [grad_spec(layer) for layer in range(DEPTH)],
        out_specs=[spec] * 4,
        out_shape=[jax.ShapeDtypeStruct(w.shape, F32)] * 4,
        compiler_params=pltpu.CompilerParams(dimension_semantics=("arbitrary", "arbitrary"),
                                             vmem_limit_bytes=VMEM_LIMIT),
    )(w, m, v, *grads)


MESH = pl.DeviceIdType.MESH
SIBLING = 1
ANY = pl.BlockSpec(memory_space=pl.ANY)
VMEM = pl.BlockSpec(memory_space=pltpu.VMEM)


def _me():
    return lax.axis_index("x"), lax.axis_index("y"), lax.axis_index("c")


def _peer(r):
    x, y, c = _me()
    return (1 - x if r & 4 else x, 1 - y if r & 2 else y, 1 - c if r & 1 else c)


def _index(dev):
    return 4 * dev[0] + 2 * dev[1] + dev[2]


def _remote(src, dst, send_sem, recv_sem, dev):
    return pltpu.make_async_remote_copy(src_ref=src, dst_ref=dst, send_sem=send_sem, recv_sem=recv_sem,
                                        device_id=dev, device_id_type=MESH)


ACROSS_X, ACROSS_Y, ACROSS_BOTH = 4, 2, 6
GATHER_SEMS = 11


class _TwoLevelGather:
    def __init__(self, out, send_sems, recv_sems, src=None):
        self.out, self.send_sems, self.recv_sems, self.src = out, send_sems, recv_sems, src
        self.rows = (out.shape[0] // N_DEV) if len(out.shape) == 2 else out.shape[1]
        self.half = self.rows // 2

    def _slot(self, block):
        if len(self.out.shape) == 2:
            return self.out.at[pl.ds(pl.multiple_of(_index(block) * self.rows, self.rows), self.rows)]
        return self.out.at[_index(block)]

    def _copy(self, k, block, part, to, src=None):
        slot = self._slot(block)
        if part is not None:
            rows = pl.ds(part * self.half, self.half)
            slot = slot.at[rows]
            src = None if src is None else src.at[rows]
        return _remote(slot if src is None else src, slot, self.send_sems.at[k], self.recv_sems.at[k], to)

    def _mine(self):
        me = _me()
        src = self._slot(me) if self.src is None else self.src
        x, y = _peer(ACROSS_X), _peer(ACROSS_Y)
        return [self._copy(1, me, 0, x, src), self._copy(3, me, 1, y, src), self._copy(0, me, None, _peer(SIBLING), src),
                self._copy(2, me, 1, x, src), self._copy(4, me, 0, y, src)]

    def _relayed(self):
        return [self._copy(5, _peer(ACROSS_X), 0, _peer(ACROSS_Y)), self._copy(6, _peer(ACROSS_Y), 1, _peer(ACROSS_X))]

    def _passed(self):
        sib, far = _peer(SIBLING), _peer(ACROSS_BOTH)
        return [self._copy(7, _peer(ACROSS_X), None, sib), self._copy(8, _peer(ACROSS_Y), None, sib),
                self._copy(9, far, 0, sib), self._copy(10, far, 1, sib)]

    def _arrival(self, k, r, part):
        return self._copy(k, _peer(r), part, _me())

    def send_first(self):
        for cp in self._mine()[:3]:
            cp.start()

    def send_second(self):
        for cp in self._mine()[3:]:
            cp.start()

    def send_mine(self):
        self.send_first()
        self.send_second()

    def relay(self):
        relayed = self._relayed()
        self._arrival(1, ACROSS_X, 0).wait_recv()
        relayed[0].start()
        self._arrival(3, ACROSS_Y, 1).wait_recv()
        relayed[1].start()

    def pass_near(self):
        passed = self._passed()
        self._arrival(2, ACROSS_X, 1).wait_recv()
        passed[0].start()
        self._arrival(4, ACROSS_Y, 0).wait_recv()
        passed[1].start()

    def pass_far(self):
        passed = self._passed()
        self._arrival(5, ACROSS_BOTH, 0).wait_recv()
        passed[2].start()
        self._arrival(6, ACROSS_BOTH, 1).wait_recv()
        passed[3].start()

    def pass_on(self):
        self.pass_near()
        self.pass_far()

    def wait_sibling(self):
        self._arrival(0, SIBLING, None).wait_recv()

    def wait_passed(self, r):
        if r == ACROSS_BOTH:
            self._arrival(9, r ^ SIBLING, 0).wait_recv()
            self._arrival(10, r ^ SIBLING, 1).wait_recv()
        else:
            self._arrival(7 if r == ACROSS_X else 8, r ^ SIBLING, None).wait_recv()

    def wait_rest(self):
        self.wait_sibling()
        for r in (ACROSS_X, ACROSS_Y, ACROSS_BOTH):
            self.wait_passed(r)

    def wait_sends(self):
        for cp in self._mine() + self._relayed() + self._passed():
            cp.wait_send()


class _ChipReduceScatter:
    SLOTS = 6

    def __init__(self, arrays, l_sem, d_send, d_recv, i_send, i_recv):
        self.arrays = arrays
        self.l_sem, self.d_send, self.d_recv, self.i_send, self.i_recv = l_sem, d_send, d_recv, i_send, i_recv

    @staticmethod
    def buffers(rows, cols, dtype, staged=True):
        stage = [pltpu.VMEM((4, rows, cols), dtype)] if staged else []
        return stage + [pltpu.VMEM((4, rows, cols), dtype), pltpu.VMEM((3, rows, cols), dtype),
                        pltpu.VMEM((2, rows, cols), dtype), pltpu.VMEM((2, rows // 2, cols), dtype)]

    @classmethod
    def semaphores(cls, n):
        return [pltpu.SemaphoreType.DMA((n,)), pltpu.SemaphoreType.DMA((n, 4)), pltpu.SemaphoreType.DMA((n, 4)),
                pltpu.SemaphoreType.DMA((n, cls.SLOTS)), pltpu.SemaphoreType.DMA((n, cls.SLOTS))]

    def _pick(self, which):
        return list(enumerate(self.arrays)) if which is None else [(n, self.arrays[n]) for n in which]

    @staticmethod
    def _chip(r):
        dev = _me() if r is None else _peer(r)
        return 2 * dev[0] + dev[1]

    def _staging(self, which):
        c = _me()[2]
        return [pltpu.make_async_copy(a["part"].at[pl.ds(0, 4), c], a["stage"], self.l_sem.at[n])
                for n, a in self._pick(which) if a["staged"]]

    def _first(self, which, chip):
        other = 1 - _me()[2]
        return [_remote(a["part"].at[chip, other], a["sib"].at[chip], self.d_send.at[n, chip], self.d_recv.at[n, chip],
                        _peer(SIBLING)) for n, a in self._pick(which)]

    @staticmethod
    def _halves(a):
        half = a["rcv"].shape[1] // 2
        return pl.ds(0, half), pl.ds(half, half)

    def _hops(self, n, a):
        h0, h1 = self._halves(a)
        x, y = _peer(ACROSS_X), _peer(ACROSS_Y)
        snd, rcv, relay = a["snd"], a["rcv"], a["relay"]
        pairs = [(snd.at[2, h0], relay.at[0], x), (snd.at[2, h1], relay.at[1], y),
                 (snd.at[0, h0], rcv.at[0, h0], x), (snd.at[0, h1], rcv.at[0, h1], x),
                 (snd.at[1, h1], rcv.at[1, h1], y), (snd.at[1, h0], rcv.at[1, h0], y)]
        return [_remote(s, d, self.i_send.at[n, k], self.i_recv.at[n, k], to) for k, (s, d, to) in enumerate(pairs)]

    def _mine(self, a, chip, rows=None):
        src = a["stage"].at[chip] if a["staged"] else a["part"].at[chip, _me()[2]]
        mine, sib = (src[...], a["sib"][chip]) if rows is None else (src[rows, :], a["sib"][chip, rows, :])
        return mine.astype(F32) + sib.astype(F32)

    def start(self, which=None, chips=None):
        if chips is None:
            for cp in self._staging(which):
                cp.start()
        for chip in range(4) if chips is None else chips:
            for cp in self._first(which, chip):
                cp.start()

    def send_far(self, which=None):
        far = self._chip(ACROSS_BOTH)
        for cp in self._staging(which):
            cp.wait()
        for cp in self._first(which, far):
            cp.wait_recv()
        for n, a in self._pick(which):
            hops = self._hops(n, a)
            a["snd"][2] = self._mine(a, far).astype(a["snd"].dtype)
            hops[0].start()
            hops[1].start()

    def send_near(self, r, which=None):
        chip = self._chip(r)
        for cp in self._first(which, chip):
            cp.wait_recv()
        for n, a in self._pick(which):
            h0, h1 = self._halves(a)
            hops = self._hops(n, a)
            if r == ACROSS_X:
                a["snd"][0, h0, :] = self._mine(a, chip, h0).astype(a["snd"].dtype)
                hops[2].start()
            else:
                a["snd"][1, h1, :] = self._mine(a, chip, h1).astype(a["snd"].dtype)
                hops[4].start()

    def exchange(self, which=None):
        self.send_far(which)
        self.send_near(ACROSS_X, which)
        self.send_near(ACROSS_Y, which)

    def fold(self, which=None):
        across_x, across_y = self._chip(ACROSS_X), self._chip(ACROSS_Y)
        for n, a in self._pick(which):
            h0, h1 = self._halves(a)
            hops = self._hops(n, a)
            dtype = a["snd"].dtype
            hops[1].wait_recv()
            a["snd"][0, h1, :] = (self._mine(a, across_x, h1) + a["relay"][1].astype(F32)).astype(dtype)
            hops[3].start()
            hops[0].wait_recv()
            a["snd"][1, h0, :] = (self._mine(a, across_y, h0) + a["relay"][0].astype(F32)).astype(dtype)
            hops[5].start()

    def finish(self, which=None):
        home = self._chip(None)
        for cp in self._first(which, home):
            cp.wait_recv()
        for n, a in self._pick(which):
            hops = self._hops(n, a)
            a["out"][...] = self._mine(a, home)
            hops[2].wait_recv()
            hops[3].wait_recv()
            a["out"][...] += a["rcv"][0].astype(F32)
            hops[4].wait_recv()
            hops[5].wait_recv()
            a["out"][...] += a["rcv"][1].astype(F32)

    def wait_sends(self, which=None):
        for chip in range(4):
            for cp in self._first(which, chip):
                cp.wait_send()
        for n, a in self._pick(which):
            for cp in self._hops(n, a):
                cp.wait_send()


def _direct_exchange(src_of, dst_of, send_sems, recv_sems):
    me = _me()
    copies = [_remote(src_of(_peer(r)), dst_of(me), send_sems.at[r - 1], recv_sems.at[r - 1], _peer(r))
              for r in range(1, N_DEV)]
    for cp in copies:
        cp.start()
    return copies


def _wait_direct(copies):
    for cp in copies:
        cp.wait_recv()
    for cp in copies:
        cp.wait_send()


def _gathered_layer(layer, x, small, ln_g, ln_b, mine, name, mod=None, exchange=None, host=()):
    exchanging = exchange is not None
    cols = exchange[1].shape[2] if exchanging else 0
    shard = mine[0].shape[0]
    pair = 2 * shard
    n_lead = 4 if exchanging else 2
    n_host = len(host)

    def body(*refs):
        x_ref = refs[0]
        lead, refs = refs[1:n_lead], refs[n_lead:]
        (wpool_ref, pscale_ref, slng_ref, slnb_ref, wsgu_ref, bsgut_ref, lng_ref, lnb_ref, wint_hbm,
         wout_hbm), refs = refs[:10], refs[10:]
        host_in, refs = refs[:n_host], refs[n_host:]
        (out_ref, y_ref, cdf_ref, proj_hbm, wint_keep, wout_keep), refs = refs[:6], refs[6:]
        if exchanging:
            c_ref, wada_ref, bada_ref = lead
            (acts_ref, mod_ref), refs = refs[:2], refs[2:]
        else:
            mod_ref, = lead
        host_out, refs = refs[:n_host], refs[n_host:]
        (wint_v, wout_v, h_buf, proj_blk, proj_tile, halo_ref, w_send, w_recv, w_local, p_sems,
         t_sems), refs = refs[:11], refs[11:]
        if n_host:
            refs, (n_send, n_recv, n_local) = refs[:-3], refs[-3:]
        weights = _MixWeights(layer, wpool_ref, pscale_ref, slng_ref, slnb_ref, wsgu_ref, bsgut_ref)
        tile = pl.program_id(0)

        def hosted():
            return [_TwoLevelGather(host_out[n], n_send.at[n], n_recv.at[n], src=host_in[n]) for n in range(n_host)]

        def hosted_own():
            index = _index(_me())
            return [pltpu.make_async_copy(host_in[n], host_out[n].at[index], n_local.at[n]) for n in range(n_host)]

        def gathers():
            return (_TwoLevelGather(wint_v, w_send.at[0], w_recv.at[0], src=wint_hbm),
                    _TwoLevelGather(wout_v, w_send.at[1], w_recv.at[1], src=wout_hbm))

        def keeps():
            return [pltpu.make_async_copy(wint_v, wint_keep, w_local.at[2]),
                    pltpu.make_async_copy(wout_v, wout_keep, w_local.at[3])]

        def tile_read(t):
            slot = t % 2
            return pltpu.make_async_copy(proj_hbm.at[pl.ds(pl.multiple_of(t * TM, TM), TM)], proj_tile.at[slot],
                                         t_sems.at[slot])

        @pl.when(tile == 0)
        def _():
            me = _me()
            halo_ref[...] = jnp.zeros_like(halo_ref)
            gather_in, gather_out = gathers()
            own_in = pltpu.make_async_copy(wint_hbm, gather_in._slot(me), w_local.at[0])
            own_out = pltpu.make_async_copy(wout_hbm, gather_out._slot(me), w_local.at[1])

            if exchanging:
                act_all, act_src, part, mod_recv, a_send, a_recv, m_send, m_recv = refs
                mine_index = _index(me)
                cval = c_ref[...]
                act_src[...] = jnp.zeros_like(act_src)
                act_src[0:1, :] = cval * jax.nn.sigmoid(cval)
                act_all[mine_index] = act_src[...]
                act_copies = _direct_exchange(lambda p: act_src, lambda m: act_all.at[_index(m)], a_send, a_recv)

            own_in.start()
            own_out.start()
            gather_in.send_first()

            if exchanging:
                _wait_direct(act_copies)
                acts = jnp.concatenate([act_all[j, 0:1, :] for j in range(N_DEV)], axis=0)
                acts_ref[...] = acts
                part[...] = jnp.zeros_like(part)
                for l in range(DEPTH):
                    res = lax.dot_general(acts, wada_ref[l], (((1,), (0,)), ((), ())), preferred_element_type=F32,
                                          precision=lax.Precision.HIGHEST)
                    for b in range(N_DEV):
                        part[b, l:l + 1, :] = res[b:b + 1, :]
                mod_recv[mine_index] = part[mine_index]
                mod_copies = _direct_exchange(lambda p: part.at[_index(p)], lambda m: mod_recv.at[_index(m)],
                                              m_send, m_recv)
            gather_in.send_second()
            gather_out.send_mine()

            if exchanging:
                _wait_direct(mod_copies)
                for l in range(DEPTH):
                    for j in range(N_DEV):
                        sl = slice(j * cols, (j + 1) * cols)
                        mod_ref[l:l + 1, sl] = mod_recv[j, l:l + 1, :] + bada_ref[l:l + 1, sl]
            shift = mod_ref[layer:layer + 1, 0:D_MODEL]
            scale = mod_ref[layer:layer + 1, D_MODEL:2 * D_MODEL]
            for t in range(N_TILES):
                rows = pl.ds(t * TM, TM)
                xn, _ = _ln(x_ref[rows, :])
                h_buf[rows, :] = (xn * (1.0 + scale) + shift).astype(BF16)

            chip_of = lambda dev: 2 * dev[0] + dev[1]
            writes = []

            def project(n, dev):
                first = pl.multiple_of(chip_of(dev) * pair, pair)
                if n >= 2:
                    writes[n - 2].wait()
                proj_blk[n % 2] = _dot_nt(h_buf[...], wint_v[pl.ds(first, pair), :])
                cp = pltpu.make_async_copy(proj_blk.at[n % 2], proj_hbm.at[:, pl.ds(first, pair)], p_sems.at[n % 2])
                cp.start()
                writes.append(cp)

            gather_in.relay()
            own_in.wait()
            gather_in.wait_sibling()
            project(0, me)
            gather_in.pass_near()
            gather_in.wait_passed(ACROSS_X)
            project(1, _peer(ACROSS_X))
            gather_out.relay()
            for cp in hosted_own():
                cp.start()
            for g in hosted():
                g.send_mine()
            gather_in.wait_passed(ACROSS_Y)
            project(2, _peer(ACROSS_Y))
            gather_in.pass_far()
            gather_in.wait_passed(ACROSS_BOTH)
            project(3, _peer(ACROSS_BOTH))

            gather_out.pass_on()
            gather_out.wait_rest()
            own_out.wait()
            for cp in keeps():
                cp.start()
            writes[2].wait()
            writes[3].wait()
            tile_read(0).start()

        @pl.when(tile + 1 < N_TILES)
        def _():
            tile_read(tile + 1).start()

        if n_host:
            @pl.when(tile == 1)
            def _():
                for g in hosted():
                    g.relay()

            @pl.when(tile == N_TILES // 2)
            def _():
                for g in hosted():
                    g.pass_near()

        tile_read(tile).wait()
        xt = x_ref[pl.ds(pl.multiple_of(tile * TM, TM), TM), :]
        gate = mod_ref[layer:layer + 1, 2 * D_MODEL:]
        proj = proj_tile[tile % 2]
        cat, cdf_ref[...] = _mix_forward(proj, halo_ref[...], tile, weights)
        halo_ref[...] = proj[TM - HALO:, 0:D_POOL]
        y = _dot(cat.astype(BF16), wout_v[...])
        y_ref[...] = y
        zn, _ = _ln(ALPHA * xt + gate * y)
        out_ref[...] = zn * lng_ref[layer:layer + 1, :] + lnb_ref[layer:layer + 1, :]

        @pl.when(tile == N_TILES - 1)
        def _():
            for g in hosted():
                g.pass_far()
            for g in hosted():
                g.wait_rest()
            for g in list(gathers()) + hosted():
                g.wait_sends()
            for cp in keeps() + hosted_own():
                cp.wait()

    row = lambda w: pl.BlockSpec((TM, w), lambda i: (i, 0))
    gather_sems = pltpu.SemaphoreType.DMA((2, GATHER_SEMS))
    seven = pltpu.SemaphoreType.DMA((7,))
    lead_in = list(exchange) if exchanging else [mod]
    lead_specs = [_const_spec(a.shape) for a in lead_in]
    extra_out_specs = [_const_spec((N_DEV, D_MODEL)), _const_spec((DEPTH, 3 * D_MODEL))] if exchanging else []
    extra_out_shape = [jax.ShapeDtypeStruct((N_DEV, D_MODEL), F32),
                       jax.ShapeDtypeStruct((DEPTH, 3 * D_MODEL), F32)] if exchanging else []
    extra_scratch = [pltpu.VMEM((N_DEV, 8, D_MODEL), F32), pltpu.VMEM((8, D_MODEL), F32),
                     pltpu.VMEM((N_DEV, 8, cols), F32), pltpu.VMEM((N_DEV, 8, cols), F32),
                     seven, seven, seven, seven] if exchanging else []
    host_scratch = [pltpu.SemaphoreType.DMA((n_host, GATHER_SEMS)), pltpu.SemaphoreType.DMA((n_host, GATHER_SEMS)),
                    pltpu.SemaphoreType.DMA((n_host,))] if n_host else []
    return pl.pallas_call(
        body,
        name=name,
        grid=(N_TILES,),
        in_specs=[_const_spec((SEQ, D_MODEL))] + lead_specs + [_const_spec(s) for s in SMALL_SPECS]
                 + [_const_spec((DEPTH, D_MODEL)), _const_spec((DEPTH, D_MODEL)), ANY, ANY] + [ANY] * n_host,
        out_specs=[row(D_MODEL), row(D_MODEL), row(2 * D_SGU), ANY, ANY, ANY] + extra_out_specs + [ANY] * n_host,
        out_shape=[jax.ShapeDtypeStruct((SEQ, D_MODEL), F32), jax.ShapeDtypeStruct((SEQ, D_MODEL), F32),
                   jax.ShapeDtypeStruct((SEQ, 2 * D_SGU), F32), jax.ShapeDtypeStruct((SEQ, D_IN), F32),
                   jax.ShapeDtypeStruct((D_IN, D_MODEL), BF16), jax.ShapeDtypeStruct((D_MODEL, D_MODEL), BF16)]
                  + extra_out_shape + [jax.ShapeDtypeStruct((N_DEV,) + blk.shape, blk.dtype) for blk in host],
        scratch_shapes=[pltpu.VMEM((D_IN, D_MODEL), BF16), pltpu.VMEM((D_MODEL, D_MODEL), BF16),
                        pltpu.VMEM((SEQ, D_MODEL), BF16), pltpu.VMEM((2, SEQ, pair), F32),
                        pltpu.VMEM((2, TM, D_IN), F32), pltpu.VMEM((HALO, D_POOL), F32),
                        gather_sems, gather_sems, pltpu.SemaphoreType.DMA((4,)), pltpu.SemaphoreType.DMA((2,)),
                        pltpu.SemaphoreType.DMA((2,))] + extra_scratch + host_scratch,
        compiler_params=pltpu.CompilerParams(dimension_semantics=("arbitrary",), vmem_limit_bytes=VMEM_LIMIT),
    )(x, *lead_in, *small, ln_g, ln_b, *mine, *host)


ADA_CHUNK = 256


def _grad_tail(dproj, h, cat, dy, small, dmod8, loss_lanes, w_ada, m_ada, v_ada, act_t, b_ada, m_bada, v_bada):
    shard_in, shard_out, shard_small = D_IN // N_DEV, D_MODEL // N_DEV, small.shape[2]
    cols = w_ada.shape[2]
    W_IN, W_OUT, SMALL = 0, 1, 2

    def body(dproj_hbm, h_hbm, cat_hbm, dy_hbm, small_hbm, dmod_ref, lanes_ref, wada_hbm, mada_hbm, vada_hbm,
             act_ref, bada_ref, mbada_ref, vbada_ref,
             gwin_ref, gwout_ref, stot_ref, loss_ref, gada_hbm, dada_hbm, nmada_hbm, nvada_hbm,
             gb_ref, db_ref, nmb_ref, nvb_ref,
             dproj_v, h_v, cat_v, dy_v, part_in, part_out, own_small, loss_src, loss_all, dmod_all, ada_in, ada_out,
             *rest):
        bufs, rest = rest[:13], rest[13:]
        load_sems, rs_sems = rest[0], rest[1:6]
        m_send, m_recv, g_send, g_recv, s_send, s_recv, ada_lsem, ada_ssem = rest[6:]
        mine = _index(_me())

        def update_ada():
            upper = (mine % 2) == 1

            def dmod_of(layer):
                rows = []
                for b in range(N_DEV):
                    r = dmod_all[b, pl.ds(4 * layer + mine // 2, 1), :]
                    rows.append(jnp.where(upper, r[:, cols:], r[:, :cols]))
                return jnp.concatenate(rows, axis=0)

            chunks = [(layer, c) for layer in range(DEPTH) for c in range(D_MODEL // ADA_CHUNK)]

            def loads(i):
                layer, c = chunks[i]
                rows = pl.ds(c * ADA_CHUNK, ADA_CHUNK)
                return [pltpu.make_async_copy(src.at[layer, rows], ada_in.at[i % 2, k], ada_lsem.at[i % 2, k])
                        for k, src in enumerate((wada_hbm, mada_hbm, vada_hbm))]

            def stores(i):
                layer, c = chunks[i]
                rows = pl.ds(c * ADA_CHUNK, ADA_CHUNK)
                return [pltpu.make_async_copy(ada_out.at[i % 2, k], dst.at[layer, rows], ada_ssem.at[i % 2, k])
                        for k, dst in enumerate((gada_hbm, dada_hbm, nmada_hbm, nvada_hbm))]

            for cp in loads(0):
                cp.start()
            dmods = {}
            for i, (layer, c) in enumerate(chunks):
                if i + 1 < len(chunks):
                    for cp in loads(i + 1):
                        cp.start()
                for cp in loads(i):
                    cp.wait()
                if i >= 2:
                    for cp in stores(i - 2):
                        cp.wait()
                if layer not in dmods:
                    dmods[layer] = dmod_of(layer)
                act = act_ref[pl.ds(c * ADA_CHUNK, ADA_CHUNK), :]
                g = act[:, 0:1] * dmods[layer][0:1, :]
                for b in range(1, N_DEV):
                    g = g + act[:, b:b + 1] * dmods[layer][b:b + 1, :]
                slot = i % 2
                delta, new_m, new_v = _adamw_math(ada_in[slot, 0], g, ada_in[slot, 1], ada_in[slot, 2])
                ada_out[slot, 0] = g
                ada_out[slot, 1] = delta
                ada_out[slot, 2] = new_m
                ada_out[slot, 3] = new_v
                for cp in stores(i):
                    cp.start()
            for i in (len(chunks) - 2, len(chunks) - 1):
                for cp in stores(i):
                    cp.wait()

            total = dmod_all[0]
            for b in range(1, N_DEV):
                total = total + dmod_all[b]
            width = total.shape[1]
            for layer in range(DEPTH):
                for q in range(4):
                    gb_ref[layer:layer + 1, q * width:(q + 1) * width] = total[4 * layer + q:4 * layer + q + 1, :]
            db_ref[...], nmb_ref[...], nvb_ref[...] = _adamw_math(bada_ref[...], gb_ref[...], mbada_ref[...],
                                                                  vbada_ref[...])

        order = (ACROSS_BOTH, ACROSS_X, ACROSS_Y, None)
        chips = [_ChipReduceScatter._chip(r) for r in order]
        loads = [pltpu.make_async_copy(s, d, load_sems.at[n]) for n, (s, d) in enumerate(
            ((cat_hbm, cat_v), (dy_hbm, dy_v), (h_hbm, h_v)))]
        loads += [pltpu.make_async_copy(dproj_hbm.at[:, pl.ds(pl.multiple_of(chip * 2 * shard_in, 2 * shard_in),
                                                             2 * shard_in)], dproj_v.at[n], load_sems.at[3 + n])
                  for n, chip in enumerate(chips)]
        for cp in loads:
            cp.start()
        arrays = [dict(part=part_in, out=gwin_ref, staged=False, sib=bufs[0], snd=bufs[1], rcv=bufs[2], relay=bufs[3]),
                  dict(part=part_out, out=gwout_ref, staged=False, sib=bufs[4], snd=bufs[5], rcv=bufs[6],
                       relay=bufs[7]),
                  dict(part=small_hbm, out=own_small, staged=True, stage=bufs[8], sib=bufs[9], snd=bufs[10],
                       rcv=bufs[11], relay=bufs[12])]
        scatter = _ChipReduceScatter(arrays, *rs_sems)
        scatter.start([SMALL])
        dmod_all[mine] = dmod_ref[...]
        dmod_copies = _direct_exchange(lambda p: dmod_ref, lambda m: dmod_all.at[_index(m)], m_send, m_recv)
        loss_src[...] = jnp.full(loss_src.shape, (0.5 / D_MODEL) * jnp.sum(lanes_ref[...]), F32)
        loss_all[mine] = loss_src[...]
        loss_copies = _direct_exchange(lambda p: loss_src, lambda m: loss_all.at[_index(m)], s_send, s_recv)

        loads[0].wait()
        loads[1].wait()
        for blk in range(2):
            res = _dot_tn(cat_v[:, blk * 512:(blk + 1) * 512], dy_v[...]).astype(BF16)
            for s in range(4):
                part_out[2 * blk + s // 2, s % 2] = res[s * shard_out:(s + 1) * shard_out]
        scatter.start([W_OUT])
        scatter.exchange([SMALL])

        gather = _TwoLevelGather(stot_ref, g_send, g_recv)
        loads[2].wait()
        for n, chip in enumerate(chips):
            loads[3 + n].wait()
            res = _dot_tn(dproj_v[n], h_v[...]).astype(BF16)
            part_in[chip, 0] = res[:shard_in]
            part_in[chip, 1] = res[shard_in:]
            scatter.start([W_IN], chips=[chip])
            if n == 0:
                scatter.exchange([W_OUT])
                scatter.fold([SMALL])
            if n == 1:
                scatter.send_far([W_IN])
                scatter.fold([W_OUT])
                scatter.finish([SMALL])
                stot_ref[mine] = own_small[...]
                gather.send_mine()
            if n == 2:
                scatter.send_near(ACROSS_X, [W_IN])
                gather.relay()
            if n == 3:
                scatter.send_near(ACROSS_Y, [W_IN])
        scatter.fold([W_IN])
        scatter.finish([W_OUT])
        _wait_direct(dmod_copies)
        update_ada()
        gather.pass_on()
        gather.wait_rest()
        _wait_direct(loss_copies)
        total = loss_all[0]
        for j in range(1, N_DEV):
            total = total + loss_all[j]
        loss_ref[...] = total
        scatter.finish([W_IN])
        gather.wait_sends()
        scatter.wait_sends()

    buffers = _ChipReduceScatter.buffers
    comm_scratch = (buffers(shard_in, D_MODEL, BF16, staged=False) + buffers(shard_out, D_MODEL, BF16, staged=False)
                    + buffers(shard_small, 128, F32))
    comm_scratch += [pltpu.SemaphoreType.DMA((7,))] + _ChipReduceScatter.semaphores(3)
    comm_scratch += [pltpu.SemaphoreType.DMA((n,)) for n in (7, 7, GATHER_SEMS, GATHER_SEMS, 7, 7)]
    comm_scratch += [pltpu.SemaphoreType.DMA((2, 3)), pltpu.SemaphoreType.DMA((2, 4))]
    return pl.pallas_call(
        body,
        name="grad_tail",
        in_specs=[ANY] * 5 + [VMEM, VMEM] + [ANY] * 3 + [VMEM] * 4,
        out_specs=[VMEM] * 4 + [ANY] * 4 + [VMEM] * 4,
        out_shape=[jax.ShapeDtypeStruct((shard_in, D_MODEL), F32), jax.ShapeDtypeStruct((shard_out, D_MODEL), F32),
                   jax.ShapeDtypeStruct((N_DEV, shard_small, 128), F32), jax.ShapeDtypeStruct((8, 128), F32)]
                  + [jax.ShapeDtypeStruct(w_ada.shape, F32)] * 4 + [jax.ShapeDtypeStruct(b_ada.shape, F32)] * 4,
        scratch_shapes=[pltpu.VMEM((4, SEQ, 2 * shard_in), BF16), pltpu.VMEM(h.shape, BF16), pltpu.VMEM(cat.shape, BF16),
                        pltpu.VMEM(dy.shape, BF16), pltpu.VMEM((4, 2, shard_in, D_MODEL), BF16),
                        pltpu.VMEM((4, 2, shard_out, D_MODEL), BF16), pltpu.VMEM((shard_small, 128), F32),
                        pltpu.VMEM((8, 128), F32), pltpu.VMEM((N_DEV, 8, 128), F32),
                        pltpu.VMEM((N_DEV,) + dmod8.shape, F32), pltpu.VMEM((2, 3, ADA_CHUNK, cols), F32),
                        pltpu.VMEM((2, 4, ADA_CHUNK, cols), F32)] + comm_scratch,
        compiler_params=pltpu.CompilerParams(vmem_limit_bytes=VMEM_LIMIT),
    )(dproj, h, cat, dy, small, dmod8, loss_lanes, w_ada, m_ada, v_ada, act_t, b_ada, m_bada, v_bada)


SMALL_NAMES = ("w_pool", "w_sgu", "pool_scale", "sgu_ln_g", "sgu_ln_b", "b_sgu", "ln_g", "ln_b")
SMALL_ROWS = (512, 512, 4, 4, 4, 4, 8, 8)


def _adamw_small(g_packed, ws, ms, vs, name):
    n = len(SMALL_NAMES)

    def body(g_ref, *refs):
        w_refs, m_refs, v_refs = refs[:n], refs[n:2 * n], refs[2 * n:3 * n]
        outs = refs[3 * n:]

        def update(p, at, g):
            delta, new_m, new_v = _adamw_math(w_refs[p][at], g, m_refs[p][at], v_refs[p][at])
            outs[p][at] = g
            outs[n + p][at] = delta
            outs[2 * n + p][at] = new_m
            outs[3 * n + p][at] = new_v

        row = 0
        for p, r in enumerate(SMALL_ROWS):
            shape = ws[p].shape
            for layer in range(DEPTH):
                first = layer * PACK_ROWS + row
                if len(shape) == 4:
                    for k in range(shape[1]):
                        update(p, (layer, k), g_ref[first + k * shape[2]:first + (k + 1) * shape[2], :])
                elif len(shape) == 3:
                    update(p, (layer,), g_ref[first:first + r, :])
                else:
                    g = jnp.concatenate([g_ref[first + k:first + k + 1, :] for k in range(r)], axis=1)
                    update(p, (slice(layer, layer + 1), slice(None)), g)
            row += r

    res = pl.pallas_call(
        body,
        name=name,
        out_shape=[jax.ShapeDtypeStruct(w.shape, F32) for w in ws] * 4,
        compiler_params=pltpu.CompilerParams(vmem_limit_bytes=VMEM_LIMIT),
    )(g_packed, *ws, *ms, *vs)
    return res[:n], res[n:2 * n], res[2 * n:3 * n], res[3 * n:]


def kernel(x, c, w_ada, b_ada, w_in, w_pool, pool_scale, sgu_ln_g, sgu_ln_b, w_sgu, b_sgu, w_out, ln_g, ln_b, loss_target, m_w_ada, m_b_ada, m_w_in, m_w_pool, m_pool_scale, m_sgu_ln_g, m_sgu_ln_b, m_w_sgu, m_b_sgu, m_w_out, m_ln_g, m_ln_b, v_w_ada, v_b_ada, v_w_in, v_w_pool, v_pool_scale, v_sgu_ln_g, v_sgu_ln_b, v_w_sgu, v_b_sgu, v_w_out, v_ln_g, v_ln_b):
    small_w = dict(w_pool=w_pool, w_sgu=w_sgu, pool_scale=pool_scale, sgu_ln_g=sgu_ln_g, sgu_ln_b=sgu_ln_b,
                   b_sgu=b_sgu, ln_g=ln_g, ln_b=ln_b)
    small_m = dict(w_pool=m_w_pool, w_sgu=m_w_sgu, pool_scale=m_pool_scale, sgu_ln_g=m_sgu_ln_g,
                   sgu_ln_b=m_sgu_ln_b, b_sgu=m_b_sgu, ln_g=m_ln_g, ln_b=m_ln_b)
    small_v = dict(w_pool=v_w_pool, w_sgu=v_w_sgu, pool_scale=v_pool_scale, sgu_ln_g=v_sgu_ln_g,
                   sgu_ln_b=v_sgu_ln_b, b_sgu=v_b_sgu, ln_g=v_ln_g, ln_b=v_ln_b)

    wint_loc = jnp.transpose(w_in, (0, 2, 1)).astype(BF16)
    wout_loc = w_out.astype(BF16)
    small = (w_pool, pool_scale, sgu_ln_g, sgu_ln_b, w_sgu, jnp.transpose(b_sgu, (0, 2, 1)))
    out, y, cdf, proj, wint0, wout0, act_all, mod, wint1, wout1 = _gathered_layer(
        0, x[0], small, ln_g, ln_b, [wint_loc[0], wout_loc[0]], "layer_fwd_0", exchange=(c, w_ada, b_ada),
        host=[wint_loc[1], wout_loc[1]])
    w_int, w_outf = [wint0, wint1.reshape(D_IN, D_MODEL)], [wout0, wout1.reshape(D_MODEL, D_MODEL)]
    acts, cur = [(x[0], proj, y, cdf)], out
    for l in range(1, DEPTH):
        out, proj, y, cdf = _layer_forward(l, cur, mod, w_int[l], w_outf[l], small, ln_g, ln_b, f"layer_fwd_{l}")
        acts.append((cur, proj, y, cdf))
        cur = out

    shard_in, shard_out = D_IN // N_DEV, D_MODEL // N_DEV
    a, b = cur, loss_target[0]
    loss_lanes, carry, pending = None, (), []
    g_w_in_t, g_w_out = [None] * DEPTH, [None] * DEPTH
    for l in reversed(range(DEPTH)):
        dx, dproj, h, cat, dy, small_grads, dmod8, lanes, *shards = _layer_backward(
            l, a, b, *acts[l], mod, w_int[l], w_outf[l], small, ln_g, l == DEPTH - 1, f"layer_bwd_{l}",
            carry=carry, reduce=pending)
        if shards:
            g_w_in_t[l + 1], g_w_out[l + 1] = shards
        if l == DEPTH - 1:
            loss_lanes = lanes
        if l > 0:
            g_in, g_out = _grad_matmuls(dproj, h, cat, dy, f"grad_w_{l}")
            pending = [g_in.reshape(4, 2, shard_in, D_MODEL), g_out.reshape(4, 2, shard_out, D_MODEL)]
        carry = (small_grads, dmod8)
        a = b = dx
    grad_x = a[None]

    (g_w_in_t[0], g_w_out[0], small_tot, loss_tile, g_w_ada, d_w_ada, nm_w_ada, nv_w_ada,
     g_b_ada, d_b_ada, nm_b_ada, nv_b_ada) = _grad_tail(
        dproj, h, cat, dy, small_grads.reshape(4, 2, DEPTH * PACK_ROWS // N_DEV, 128), dmod8, loss_lanes,
        w_ada, m_w_ada, v_w_ada, jnp.transpose(act_all), b_ada, m_b_ada, v_b_ada)
    loss = loss_tile[0, 0]

    flat = lambda t: t.reshape(-1, t.shape[-1])
    to_t = lambda t: flat(jnp.transpose(t, (0, 2, 1)))
    from_t = lambda t: jnp.transpose(t.reshape(DEPTH, shard_in, D_MODEL), (0, 2, 1))
    g_w_in, d_w_in, nm_w_in, nv_w_in = [from_t(t) for t in _adamw(to_t(w_in), g_w_in_t, to_t(m_w_in), to_t(v_w_in),
                                                                  shard_in // 2, "adamw_w_in")]
    gwout, d_w_out, nm_w_out, nv_w_out = [t.reshape(w_out.shape) for t in _adamw(
        flat(w_out), g_w_out, flat(m_w_out), flat(v_w_out), shard_out, "adamw_w_out")]
    small_out = _adamw_small(small_tot.reshape(DEPTH * PACK_ROWS, 128), [small_w[n] for n in SMALL_NAMES],
                             [small_m[n] for n in SMALL_NAMES], [small_v[n] for n in SMALL_NAMES], "adamw_small")
    gs, ds, ms, vs = [dict(zip(SMALL_NAMES, group)) for group in small_out]

    def ordered(w_ada_, b_ada_, w_in_, small, w_out_):
        return (w_ada_, b_ada_, w_in_, small["w_pool"], small["pool_scale"], small["sgu_ln_g"], small["sgu_ln_b"],
                small["w_sgu"], small["b_sgu"], w_out_, small["ln_g"], small["ln_b"])

    return (loss, grad_x,
            *ordered(g_w_ada, g_b_ada, g_w_in, gs, gwout),
            *ordered(d_w_ada, d_b_ada, d_w_in, ds, d_w_out),
            *ordered(nm_w_ada, nm_b_ada, nm_w_in, ms, nm_w_out),
            *ordered(nv_w_ada, nv_b_ada, nv_w_in, vs, nv_w_out))
```

```python
import jax
import jax.numpy as jnp
from jax import lax
from jax.experimental import pallas as pl
from jax.experimental.pallas import tpu as pltpu

F32 = jnp.float32
BF16 = jnp.bfloat16

D_MODEL = 1024
SEQ = 2048
DEPTH = 2
D_POOL = 512
D_SGU = 512
D_IN = 2560
N_GROUPS = 4
GROUP = 128
N_HEADS = 4
HEAD = 128
CHUNK = 128
WINDOWS = (2, 4, 8, 16)
ALPHA = (2.0 * DEPTH) ** 0.25
LN_EPS = 1e-5
N_DEV = 8

ADAM_LR = 0.001
ADAM_B1 = 0.9
ADAM_B2 = 0.999
ADAM_EPS = 1e-08
ADAM_WD = 0.01
ADAM_STEP = 10

TM = 256
HALO = 16
N_TILES = SEQ // TM
VMEM_LIMIT = 60 * 1024 * 1024

ROW_WPOOL = 0
ROW_WSGU = 512
ROW_PSCALE = 1024
ROW_SLNG = 1028
ROW_SLNB = 1032
ROW_BSGU = 1036
ROW_LNG = 1040
ROW_LNB = 1048
PACK_ROWS = 1088
DMOD_COLS = DEPTH * 3 * D_MODEL // 8

SQRT_HALF = 0.7071067811865476
INV_SQRT_2PI = 0.3989422804014327


def _ln(x):
    mu = jnp.mean(x, axis=-1, keepdims=True)
    xc = x - mu
    var = jnp.mean(xc * xc, axis=-1, keepdims=True)
    rstd = lax.rsqrt(var + LN_EPS)
    return xc * rstd, rstd


def _ln_bwd(dxn, xn, rstd):
    m1 = jnp.mean(dxn, axis=-1, keepdims=True)
    m2 = jnp.mean(dxn * xn, axis=-1, keepdims=True)
    return rstd * (dxn - m1 - xn * m2)


def _normal_cdf(x):
    return 0.5 * (1.0 + lax.erf(x * SQRT_HALF))


def _gelu_parts(x, cdf, with_grad):
    if not with_grad:
        return x * cdf, None
    return x * cdf, cdf + x * (INV_SQRT_2PI * jnp.exp(-0.5 * x * x))


def _silu_parts(x):
    s = jax.nn.sigmoid(x)
    return x * s, s * (1.0 + x * (1.0 - s))


def _dot(a, b):
    return lax.dot_general(a, b, (((1,), (0,)), ((), ())), preferred_element_type=F32)


def _dot_nt(a, b):
    return lax.dot_general(a, b, (((1,), (1,)), ((), ())), preferred_element_type=F32)


def _dot_tn(a, b):
    return lax.dot_general(a, b, (((0,), (0,)), ((), ())), preferred_element_type=F32)


def _row_index(tile):
    return tile * TM + lax.broadcasted_iota(jnp.int32, (TM, 1), 0)


def _window_sums(ext, forward):
    n = TM + HALO
    cur = ext
    outs = []
    for g in range(N_GROUPS):
        step = 1 << g
        cur = cur + pltpu.roll(cur, step if forward else n - step, 0)
        rows = cur[HALO:, :GROUP] if forward else cur[:TM, :GROUP]
        outs.append(rows)
        cur = cur[:, GROUP:] if g + 1 < N_GROUPS else None
    return outs


def _inverse_counts(rows):
    return [1.0 / jnp.minimum(rows + 1, w).astype(F32) for w in WINDOWS]


def _tril_bf16(w):
    t = lax.broadcasted_iota(jnp.int32, (CHUNK, CHUNK), 0)
    s = lax.broadcasted_iota(jnp.int32, (CHUNK, CHUNK), 1)
    return jnp.where(t >= s, w, 0.0).astype(BF16)


class _MixWeights:
    def __init__(self, layer, wpool_ref, pscale_ref, slng_ref, slnb_ref, wsgu_ref, bsgut_ref):
        self.layer = layer
        self.wpool_ref, self.pscale_ref, self.slng_ref, self.slnb_ref = wpool_ref, pscale_ref, slng_ref, slnb_ref
        self.wsgu_ref, self.bsgut_ref = wsgu_ref, bsgut_ref

    def pool(self, g):
        return self.wpool_ref[self.layer, g].astype(BF16)

    def pool_scale(self, g):
        return self.pscale_ref[self.layer:self.layer + 1, g * GROUP:(g + 1) * GROUP]

    def ln_gain(self, h):
        return self.slng_ref[self.layer, h:h + 1, :]

    def ln_bias(self, h):
        return self.slnb_ref[self.layer, h:h + 1, :]

    def mix(self, h):
        return _tril_bf16(self.wsgu_ref[self.layer, h])

    def mix_bias(self, h):
        return self.bsgut_ref[self.layer, :, h:h + 1]


SMALL_SPECS = ((DEPTH, N_GROUPS, GROUP, GROUP), (DEPTH, D_POOL), (DEPTH, N_HEADS, HEAD), (DEPTH, N_HEADS, HEAD),
               (DEPTH, N_HEADS, CHUNK, CHUNK), (DEPTH, CHUNK, N_HEADS))


def _mix_forward(proj, halo, tile, w, cdf=None):
    keep = cdf is not None
    rows = _row_index(tile)
    inv_counts = _inverse_counts(rows)
    xa = proj[:, 0:D_POOL]
    ga = proj[:, D_POOL:2 * D_POOL]
    sums = _window_sums(jnp.concatenate([halo, xa], axis=0), True)
    ga_act, ga_grad = _silu_parts(ga)
    pooled, pw, ya = [], [], []
    for g in range(N_GROUPS):
        sl = slice(g * GROUP, (g + 1) * GROUP)
        p = (sums[g] * inv_counts[g] - xa[:, sl]).astype(BF16)
        q = _dot(p, w.pool(g))
        pooled.append(p)
        pw.append(q)
        ya.append(q * w.pool_scale(g) * ga_act[:, sl])

    u = proj[:, 2 * D_POOL:2 * D_POOL + D_SGU]
    v = proj[:, 2 * D_POOL + D_SGU:2 * D_POOL + 2 * D_SGU]
    gb = proj[:, 2 * D_POOL + 2 * D_SGU:]
    gb_act, gb_grad = _silu_parts(gb)
    if cdf is None:
        cdf = jnp.concatenate([_normal_cdf(u), _normal_cdf(v)], axis=1)
    u_act, u_grad = _gelu_parts(u, cdf[:, :D_SGU], keep)
    v_act, v_grad = _gelu_parts(v, cdf[:, D_SGU:], keep)
    vn, vrstd, vln, mixed, yb = [], [], [], [], []
    for h in range(N_HEADS):
        sl = slice(h * HEAD, (h + 1) * HEAD)
        n_h, r_h = _ln(v_act[:, sl])
        l_h = (n_h * w.ln_gain(h) + w.ln_bias(h)).astype(BF16)
        w_h = w.mix(h)
        bias = w.mix_bias(h)
        m_h = jnp.concatenate(
            [_dot(w_h, l_h[k * CHUNK:(k + 1) * CHUNK]) + bias for k in range(TM // CHUNK)], axis=0)
        vn.append(n_h)
        vrstd.append(r_h)
        vln.append(l_h)
        mixed.append(m_h)
        yb.append(u_act[:, sl] * m_h * gb_act[:, sl])
    cat = jnp.concatenate(ya + yb, axis=1)
    if not keep:
        return cat, cdf
    return cat, dict(inv_counts=inv_counts, ga_act=ga_act, ga_grad=ga_grad, pooled=pooled, pw=pw, u_grad=u_grad,
                     v_grad=v_grad, u_act=u_act, gb_act=gb_act, gb_grad=gb_grad, vn=vn, vrstd=vrstd, vln=vln,
                     mixed=mixed)


def _const_spec(shape):
    nd = len(shape)
    return pl.BlockSpec(shape, lambda i: (0,) * nd)


def _layer_forward(layer, x, mod, w_int, w_outf, small, ln_g, ln_b, name):
    def body(x_ref, mod_ref, wint_ref, wout_ref, wpool_ref, pscale_ref, slng_ref, slnb_ref, wsgu_ref, bsgut_ref,
             lng_ref, lnb_ref, out_ref, proj_ref, y_ref, cdf_ref, halo_ref):
        weights = _MixWeights(layer, wpool_ref, pscale_ref, slng_ref, slnb_ref, wsgu_ref, bsgut_ref)
        tile = pl.program_id(0)

        @pl.when(tile == 0)
        def _():
            halo_ref[...] = jnp.zeros_like(halo_ref)

        xt = x_ref[...]
        shift = mod_ref[layer:layer + 1, 0:D_MODEL]
        scale = mod_ref[layer:layer + 1, D_MODEL:2 * D_MODEL]
        gate = mod_ref[layer:layer + 1, 2 * D_MODEL:]
        xn, _ = _ln(xt)
        h = (xn * (1.0 + scale) + shift).astype(BF16)
        proj = _dot_nt(h, wint_ref[...])
        proj_ref[...] = proj
        cat, cdf_ref[...] = _mix_forward(proj, halo_ref[...], tile, weights)
        halo_ref[...] = proj[TM - HALO:, 0:D_POOL]
        y = _dot(cat.astype(BF16), wout_ref[...])
        y_ref[...] = y
        zn, _ = _ln(ALPHA * xt + gate * y)
        out_ref[...] = zn * lng_ref[layer:layer + 1, :] + lnb_ref[layer:layer + 1, :]

    row = lambda w: pl.BlockSpec((TM, w), lambda i: (i, 0))
    return pl.pallas_call(
        body,
        name=name,
        grid=(N_TILES,),
        in_specs=[row(D_MODEL), _const_spec((DEPTH, 3 * D_MODEL)), _const_spec((D_IN, D_MODEL)),
                  _const_spec((D_MODEL, D_MODEL))] + [_const_spec(s) for s in SMALL_SPECS]
                 + [_const_spec((DEPTH, D_MODEL)), _const_spec((DEPTH, D_MODEL))],
        out_specs=[row(D_MODEL), row(D_IN), row(D_MODEL), row(2 * D_SGU)],
        out_shape=[jax.ShapeDtypeStruct((SEQ, D_MODEL), F32), jax.ShapeDtypeStruct((SEQ, D_IN), F32),
                   jax.ShapeDtypeStruct((SEQ, D_MODEL), F32), jax.ShapeDtypeStruct((SEQ, 2 * D_SGU), F32)],
        scratch_shapes=[pltpu.VMEM((HALO, D_POOL), F32)],
        compiler_params=pltpu.CompilerParams(dimension_semantics=("arbitrary",), vmem_limit_bytes=VMEM_LIMIT),
    )(x, mod, w_int, w_outf, *small, ln_g, ln_b)


VEC_LNG, VEC_LNB, VEC_POOL, VEC_SGU, VEC_SHIFT, VEC_SCALE, VEC_GATE, VEC_LOSS = range(8)


def _layer_backward(layer, a, b, x, proj, y, cdf, mod, w_int, w_outf, small, ln_g, is_last, name, carry=(),
                    reduce=()):
    n_red, n_carry = len(reduce), len(carry)
    base = layer * PACK_ROWS

    def body(a_ref, b_ref, x_ref, proj_ref, prev_ref, y_ref, cdf_ref, mod_ref, wint_ref, wout_ref, wpool_ref,
             pscale_ref, slng_ref, slnb_ref, wsgu_ref, bsgut_ref, lng_ref, *rest):
        weights = _MixWeights(layer, wpool_ref, pscale_ref, slng_ref, slnb_ref, wsgu_ref, bsgut_ref)
        carry_refs, rest = rest[:n_carry], rest[n_carry:]
        part_refs, rest = rest[:n_red], rest[n_red:]
        dx_ref, dproj_ref, h_ref, cat_ref, dy_ref, small_ref, dmod_ref, loss_ref = rest[:8]
        shard_refs, rest = rest[8:8 + n_red], rest[8 + n_red:]
        vec_ref, dmix_ref, halo_ref = rest[:3]
        step = pl.program_id(0)
        tile = N_TILES - 1 - step

        def scatter():
            bufs, sems = rest[3:3 + 5 * n_red], rest[3 + 5 * n_red:]
            arrays = [dict(part=part_refs[n], out=shard_refs[n], staged=True, stage=bufs[5 * n], sib=bufs[5 * n + 1],
                           snd=bufs[5 * n + 2], rcv=bufs[5 * n + 3], relay=bufs[5 * n + 4]) for n in range(n_red)]
            return _ChipReduceScatter(arrays, *sems)

        @pl.when(step == 0)
        def _():
            small_ref[...] = jnp.zeros_like(small_ref)
            dmod_ref[...] = jnp.zeros_like(dmod_ref)
            vec_ref[...] = jnp.zeros_like(vec_ref)
            dmix_ref[...] = jnp.zeros_like(dmix_ref)
            halo_ref[...] = jnp.zeros_like(halo_ref)
            if n_red:
                scatter().start()

        if n_red:
            @pl.when(step == 1)
            def _():
                scatter().exchange()

            @pl.when(step == N_TILES // 2)
            def _():
                scatter().fold()

        def acc(row, lo, val):
            hi = lo + val.shape[1]
            vec_ref[row:row + 1, lo:hi] += jnp.sum(val, axis=0, keepdims=True)

        xt = x_ref[...]
        yt = y_ref[...]
        shift = mod_ref[layer:layer + 1, 0:D_MODEL]
        scale = mod_ref[layer:layer + 1, D_MODEL:2 * D_MODEL]
        gate = mod_ref[layer:layer + 1, 2 * D_MODEL:]
        ln_gain = lng_ref[layer:layer + 1, :]

        zn, zrstd = _ln(ALPHA * xt + gate * yt)
        if is_last:
            diff = a_ref[...] - b_ref[...]
            acc(VEC_LOSS, 0, diff * diff)
            dout = diff * (1.0 / D_MODEL)
        else:
            dout = a_ref[...]
        acc(VEC_LNG, 0, dout * zn)
        acc(VEC_LNB, 0, dout)
        dz = _ln_bwd(dout * ln_gain, zn, zrstd)
        acc(VEC_GATE, 0, dz * yt)
        dy = (dz * gate).astype(BF16)
        dy_ref[...] = dy
        dcat = _dot_nt(dy, wout_ref[...])

        proj = proj_ref[...]
        prev = jnp.where(tile > 0, prev_ref[...], 0.0)
        cat, k = _mix_forward(proj, prev, tile, weights, cdf_ref[...])
        cat_ref[...] = cat.astype(BF16)

        dga, dq = [], []
        for g in range(N_GROUPS):
            sl = slice(g * GROUP, (g + 1) * GROUP)
            pscale = weights.pool_scale(g)
            dya = dcat[:, sl]
            dyp = dya * k["ga_act"][:, sl]
            dga.append(dya * k["pw"][g] * pscale * k["ga_grad"][:, sl])
            acc(VEC_POOL, g * GROUP, dyp * k["pw"][g])
            dpw = (dyp * pscale).astype(BF16)
            rows = pl.ds(base + ROW_WPOOL + g * GROUP, GROUP)
            small_ref[rows, :] += _dot_tn(k["pooled"][g], dpw)
            dq.append(_dot_nt(dpw, weights.pool(g)))
        dpooled = jnp.concatenate(dq, axis=1)
        scaled = jnp.concatenate([dq[g] * k["inv_counts"][g] for g in range(N_GROUPS)], axis=1)
        sums = _window_sums(jnp.concatenate([scaled, halo_ref[...]], axis=0), False)
        halo_ref[...] = scaled[0:HALO]
        dxa = jnp.concatenate(sums, axis=1) - dpooled

        du, dv, dgb = [], [], []
        for h in range(N_HEADS):
            sl = slice(h * HEAD, (h + 1) * HEAD)
            dyb = dcat[:, D_POOL + h * HEAD:D_POOL + (h + 1) * HEAD]
            m_h = k["mixed"][h]
            ug = k["u_act"][:, sl] * dyb
            du.append(dyb * m_h * k["gb_act"][:, sl] * k["u_grad"][:, sl])
            dgb.append(ug * m_h * k["gb_grad"][:, sl])
            dmixed = ug * k["gb_act"][:, sl]
            dmixed_bf = dmixed.astype(BF16)
            w_h = weights.mix(h)
            dvln_parts = []
            dmix_sum = dmix_ref[h]
            wsgu_rows = pl.ds(base + ROW_WSGU + h * CHUNK, CHUNK)
            dws = small_ref[wsgu_rows, :]
            for c in range(TM // CHUNK):
                cs = slice(c * CHUNK, (c + 1) * CHUNK)
                dmix_sum = dmix_sum + dmixed[cs]
                dws = dws + _dot_nt(dmixed_bf[cs], k["vln"][h][cs])
                dvln_parts.append(_dot_tn(w_h, dmixed_bf[cs]))
            dmix_ref[h] = dmix_sum
            small_ref[wsgu_rows, :] = dws
            dvln = jnp.concatenate(dvln_parts, axis=0)
            acc(VEC_SGU, h * HEAD, dvln * k["vn"][h])
            acc(VEC_SGU, D_SGU + h * HEAD, dvln)
            dvv = _ln_bwd(dvln * weights.ln_gain(h), k["vn"][h], k["vrstd"][h])
            dv.append(dvv * k["v_grad"][:, sl])

        dproj = jnp.concatenate([dxa] + dga + du + dv + dgb, axis=1).astype(BF16)
        dproj_ref[...] = dproj
        dh = _dot(dproj, wint_ref[...])

        xn, xrstd = _ln(xt)
        h_ref[...] = (xn * (1.0 + scale) + shift).astype(BF16)
        acc(VEC_SCALE, 0, dh * xn)
        acc(VEC_SHIFT, 0, dh)
        dx_ref[...] = _ln_bwd(dh * (1.0 + scale), xn, xrstd) + ALPHA * dz

        @pl.when(step == N_TILES - 1)
        def _():
            def put(row0, vec_row, lo, n):
                for r in range(n):
                    small_ref[base + row0 + r:base + row0 + r + 1, :] = (
                        vec_ref[vec_row:vec_row + 1, lo + r * 128:lo + (r + 1) * 128])

            put(ROW_PSCALE, VEC_POOL, 0, 4)
            put(ROW_SLNG, VEC_SGU, 0, 4)
            put(ROW_SLNB, VEC_SGU, D_SGU, 4)
            put(ROW_LNG, VEC_LNG, 0, 8)
            put(ROW_LNB, VEC_LNB, 0, 8)
            ones = jnp.ones((8, HEAD), F32)
            t = lax.broadcasted_iota(jnp.int32, (CHUNK, CHUNK), 0)
            s = lax.broadcasted_iota(jnp.int32, (CHUNK, CHUNK), 1)
            for h in range(N_HEADS):
                bias_rows = lax.dot_general(ones, dmix_ref[h], (((1,), (1,)), ((), ())),
                                            preferred_element_type=F32, precision=lax.Precision.HIGHEST)
                small_ref[base + ROW_BSGU + h:base + ROW_BSGU + h + 1, :] = bias_rows[0:1]
                rows = pl.ds(base + ROW_WSGU + h * CHUNK, CHUNK)
                small_ref[rows, :] = jnp.where(t >= s, small_ref[rows, :], 0.0)
            pieces = ((0, VEC_SHIFT, 0, 768),
                      (1, VEC_SHIFT, 768, 256), (1, VEC_SCALE, 0, 512),
                      (2, VEC_SCALE, 512, 512), (2, VEC_GATE, 0, 256),
                      (3, VEC_GATE, 256, 768))
            filled = [0] * 4
            for q, vec_row, lo, n in pieces:
                row = 4 * layer + q
                dmod_ref[row:row + 1, filled[q]:filled[q] + n] = vec_ref[vec_row:vec_row + 1, lo:lo + n]
                filled[q] += n
            if n_carry:
                for other in range(layer + 1, DEPTH):
                    rows = pl.ds(other * PACK_ROWS, PACK_ROWS)
                    small_ref[rows, :] = carry_refs[0][rows, :]
                    dmod_ref[4 * other:4 * other + 4, :] = carry_refs[1][4 * other:4 * other + 4, :]
            loss_ref[...] = vec_ref[VEC_LOSS:VEC_LOSS + 1, :]
            if n_red:
                scatter().finish()
                scatter().wait_sends()

    rev = lambda w: pl.BlockSpec((TM, w), lambda i: (N_TILES - 1 - i, 0))
    prev_spec = pl.BlockSpec(
        (HALO, D_POOL), lambda i: (jnp.maximum((N_TILES - 1 - i) * (TM // HALO) - 1, 0), 0))
    comm_scratch = []
    for p in reduce:
        comm_scratch += _ChipReduceScatter.buffers(p.shape[2], p.shape[3], p.dtype)
    if n_red:
        comm_scratch += _ChipReduceScatter.semaphores(n_red)
    return pl.pallas_call(
        body,
        name=name,
        grid=(N_TILES,),
        in_specs=[rev(D_MODEL), rev(D_MODEL) if is_last else pl.BlockSpec((TM, D_MODEL), lambda i: (0, 0)),
                  rev(D_MODEL), rev(D_IN), prev_spec, rev(D_MODEL), rev(2 * D_SGU),
                  _const_spec((DEPTH, 3 * D_MODEL)), _const_spec((D_IN, D_MODEL)), _const_spec((D_MODEL, D_MODEL))]
                 + [_const_spec(s) for s in SMALL_SPECS] + [_const_spec((DEPTH, D_MODEL))]
                 + [_const_spec(c.shape) for c in carry] + [ANY] * n_red,
        out_specs=[rev(D_MODEL), rev(D_IN), rev(D_MODEL), rev(D_MODEL), rev(D_MODEL),
                   _const_spec((DEPTH * PACK_ROWS, 128)), _const_spec((8, DMOD_COLS)), _const_spec((1, D_MODEL))]
                  + [_const_spec(p.shape[2:]) for p in reduce],
        out_shape=[jax.ShapeDtypeStruct((SEQ, D_MODEL), F32), jax.ShapeDtypeStruct((SEQ, D_IN), BF16),
                   jax.ShapeDtypeStruct((SEQ, D_MODEL), BF16), jax.ShapeDtypeStruct((SEQ, D_MODEL), BF16),
                   jax.ShapeDtypeStruct((SEQ, D_MODEL), BF16), jax.ShapeDtypeStruct((DEPTH * PACK_ROWS, 128), F32),
                   jax.ShapeDtypeStruct((8, DMOD_COLS), F32), jax.ShapeDtypeStruct((1, D_MODEL), F32)]
                  + [jax.ShapeDtypeStruct(p.shape[2:], F32) for p in reduce],
        scratch_shapes=[pltpu.VMEM((8, D_MODEL), F32), pltpu.VMEM((N_HEADS, CHUNK, HEAD), F32),
                        pltpu.VMEM((HALO, D_POOL), F32)] + comm_scratch,
        compiler_params=pltpu.CompilerParams(dimension_semantics=("arbitrary",), vmem_limit_bytes=VMEM_LIMIT),
    )(a, b, x, proj, proj, y, cdf, mod, w_int, w_outf, *small, ln_g, *carry, *reduce)


def _grad_matmuls(dproj, h, cat, dy, name):
    in_cols, out_cols = D_IN // 4, D_MODEL // 2
    in_steps = D_IN // in_cols

    def body(dproj_ref, h_ref, cat_ref, dy_ref, gin_ref, gout_ref):
        step = pl.program_id(0)

        @pl.when(step < in_steps)
        def _():
            gin_ref[...] = _dot_tn(dproj_ref[...], h_ref[...]).astype(BF16)

        @pl.when(step >= in_steps)
        def _():
            gout_ref[...] = _dot_tn(cat_ref[...], dy_ref[...]).astype(BF16)

    in_block = lambda j: jnp.minimum(j, in_steps - 1)
    out_block = lambda j: jnp.maximum(j - in_steps, 0)
    return pl.pallas_call(
        body,
        name=name,
        grid=(in_steps + D_MODEL // out_cols,),
        in_specs=[pl.BlockSpec((SEQ, in_cols), lambda j: (0, in_block(j))), _const_spec((SEQ, D_MODEL)),
                  pl.BlockSpec((SEQ, out_cols), lambda j: (0, out_block(j))), _const_spec((SEQ, D_MODEL))],
        out_specs=[pl.BlockSpec((in_cols, D_MODEL), lambda j: (in_block(j), 0)),
                   pl.BlockSpec((out_cols, D_MODEL), lambda j: (out_block(j), 0))],
        out_shape=[jax.ShapeDtypeStruct((D_IN, D_MODEL), BF16), jax.ShapeDtypeStruct((D_MODEL, D_MODEL), BF16)],
        compiler_params=pltpu.CompilerParams(dimension_semantics=("arbitrary",), vmem_limit_bytes=VMEM_LIMIT),
    )(dproj, h, cat, dy)


def _adamw_math(w, g, m, v):
    m = ADAM_B1 * m + (1.0 - ADAM_B1) * g
    v = ADAM_B2 * v + (1.0 - ADAM_B2) * (g * g)
    m_hat = m / (1.0 - ADAM_B1 ** ADAM_STEP)
    v_hat = v / (1.0 - ADAM_B2 ** ADAM_STEP)
    delta = -ADAM_LR * (m_hat / (jnp.sqrt(v_hat) + ADAM_EPS) + ADAM_WD * w)
    return delta, m, v


def _adamw(w, grads, m, v, block_rows, name):
    rows, cols = grads[0].shape
    blocks = rows // block_rows

    def body(w_ref, m_ref, v_ref, *rest):
        g_refs, (g_ref, d_ref, nm_ref, nv_ref) = rest[:DEPTH], rest[DEPTH:]
        for layer in range(DEPTH):
            @pl.when(pl.program_id(0) == layer)
            def _():
                g = g_refs[layer][...]
                g_ref[...] = g
                d_ref[...], nm_ref[...], nv_ref[...] = _adamw_math(w_ref[...], g, m_ref[...], v_ref[...])

    def grad_spec(layer):
        return pl.BlockSpec((block_rows, cols),
                            lambda l, i: (jnp.where(l == layer, i, jnp.where(l < layer, 0, blocks - 1)), 0))

    spec = pl.BlockSpec((block_rows, cols), lambda l, i: (l * blocks + i, 0))
    return pl.pallas_call(
        body,
        name=name,
        grid=(DEPTH, blocks),
        in_specs=[spec] * 3 + [grad_spec(layer) for layer in range(DEPTH)],
        out_specs=[spec] * 4,
        out_shape=[jax.ShapeDtypeStruct(w.shape, F32)] * 4,
        compiler_params=pltpu.CompilerParams(dimension_semantics=("arbitrary", "arbitrary"),
                                             vmem_limit_bytes=VMEM_LIMIT),
    )(w, m, v, *grads)


MESH = pl.DeviceIdType.MESH
SIBLING = 1
ANY = pl.BlockSpec(memory_space=pl.ANY)
VMEM = pl.BlockSpec(memory_space=pltpu.VMEM)


def _me():
    return lax.axis_index("x"), lax.axis_index("y"), lax.axis_index("c")


def _peer(r):
    x, y, c = _me()
    return (1 - x if r & 4 else x, 1 - y if r & 2 else y, 1 - c if r & 1 else c)


def _index(dev):
    return 4 * dev[0] + 2 * dev[1] + dev[2]


def _remote(src, dst, send_sem, recv_sem, dev):
    return pltpu.make_async_remote_copy(src_ref=src, dst_ref=dst, send_sem=send_sem, recv_sem=recv_sem,
                                        device_id=dev, device_id_type=MESH)


ACROSS_X, ACROSS_Y, ACROSS_BOTH = 4, 2, 6
GATHER_SEMS = 11


class _TwoLevelGather:
    def __init__(self, out, send_sems, recv_sems, src=None):
        self.out, self.send_sems, self.recv_sems, self.src = out, send_sems, recv_sems, src
        self.rows = (out.shape[0] // N_DEV) if len(out.shape) == 2 else out.shape[1]
        self.half = self.rows // 2

    def _slot(self, block):
        if len(self.out.shape) == 2:
            return self.out.at[pl.ds(pl.multiple_of(_index(block) * self.rows, self.rows), self.rows)]
        return self.out.at[_index(block)]

    def _copy(self, k, block, part, to, src=None):
        slot = self._slot(block)
        if part is not None:
            rows = pl.ds(part * self.half, self.half)
            slot = slot.at[rows]
            src = None if src is None else src.at[rows]
        return _remote(slot if src is None else src, slot, self.send_sems.at[k], self.recv_sems.at[k], to)

    def _mine(self):
        me = _me()
        src = self._slot(me) if self.src is None else self.src
        x, y = _peer(ACROSS_X), _peer(ACROSS_Y)
        return [self._copy(1, me, 0, x, src), self._copy(3, me, 1, y, src), self._copy(0, me, None, _peer(SIBLING), src),
                self._copy(2, me, 1, x, src), self._copy(4, me, 0, y, src)]

    def _relayed(self):
        return [self._copy(5, _peer(ACROSS_X), 0, _peer(ACROSS_Y)), self._copy(6, _peer(ACROSS_Y), 1, _peer(ACROSS_X))]

    def _passed(self):
        sib, far = _peer(SIBLING), _peer(ACROSS_BOTH)
        return [self._copy(7, _peer(ACROSS_X), None, sib), self._copy(8, _peer(ACROSS_Y), None, sib),
                self._copy(9, far, 0, sib), self._copy(10, far, 1, sib)]

    def _arrival(self, k, r, part):
        return self._copy(k, _peer(r), part, _me())

    def send_first(self):
        for cp in self._mine()[:3]:
            cp.start()

    def send_second(self):
        for cp in self._mine()[3:]:
            cp.start()

    def send_mine(self):
        self.send_first()
        self.send_second()

    def relay(self):
        relayed = self._relayed()
        self._arrival(1, ACROSS_X, 0).wait_recv()
        relayed[0].start()
        self._arrival(3, ACROSS_Y, 1).wait_recv()
        relayed[1].start()

    def pass_near(self):
        passed = self._passed()
        self._arrival(2, ACROSS_X, 1).wait_recv()
        passed[0].start()
        self._arrival(4, ACROSS_Y, 0).wait_recv()
        passed[1].start()

    def pass_far(self):
        passed = self._passed()
        self._arrival(5, ACROSS_BOTH, 0).wait_recv()
        passed[2].start()
        self._arrival(6, ACROSS_BOTH, 1).wait_recv()
        passed[3].start()

    def pass_on(self):
        self.pass_near()
        self.pass_far()

    def wait_sibling(self):
        self._arrival(0, SIBLING, None).wait_recv()

    def wait_passed(self, r):
        if r == ACROSS_BOTH:
            self._arrival(9, r ^ SIBLING, 0).wait_recv()
            self._arrival(10, r ^ SIBLING, 1).wait_recv()
        else:
            self._arrival(7 if r == ACROSS_X else 8, r ^ SIBLING, None).wait_recv()

    def wait_rest(self):
        self.wait_sibling()
        for r in (ACROSS_X, ACROSS_Y, ACROSS_BOTH):
            self.wait_passed(r)

    def wait_sends(self):
        for cp in self._mine() + self._relayed() + self._passed():
            cp.wait_send()


class _ChipReduceScatter:
    SLOTS = 6

    def __init__(self, arrays, l_sem, d_send, d_recv, i_send, i_recv):
        self.arrays = arrays
        self.l_sem, self.d_send, self.d_recv, self.i_send, self.i_recv = l_sem, d_send, d_recv, i_send, i_recv

    @staticmethod
    def buffers(rows, cols, dtype, staged=True):
        stage = [pltpu.VMEM((4, rows, cols), dtype)] if staged else []
        return stage + [pltpu.VMEM((4, rows, cols), dtype), pltpu.VMEM((3, rows, cols), dtype),
                        pltpu.VMEM((2, rows, cols), dtype), pltpu.VMEM((2, rows // 2, cols), dtype)]

    @classmethod
    def semaphores(cls, n):
        return [pltpu.SemaphoreType.DMA((n,)), pltpu.SemaphoreType.DMA((n, 4)), pltpu.SemaphoreType.DMA((n, 4)),
                pltpu.SemaphoreType.DMA((n, cls.SLOTS)), pltpu.SemaphoreType.DMA((n, cls.SLOTS))]

    def _pick(self, which):
        return list(enumerate(self.arrays)) if which is None else [(n, self.arrays[n]) for n in which]

    @staticmethod
    def _chip(r):
        dev = _me() if r is None else _peer(r)
        return 2 * dev[0] + dev[1]

    def _staging(self, which):
        c = _me()[2]
        return [pltpu.make_async_copy(a["part"].at[pl.ds(0, 4), c], a["stage"], self.l_sem.at[n])
                for n, a in self._pick(which) if a["staged"]]

    def _first(self, which, chip):
        other = 1 - _me()[2]
        return [_remote(a["part"].at[chip, other], a["sib"].at[chip], self.d_send.at[n, chip], self.d_recv.at[n, chip],
                        _peer(SIBLING)) for n, a in self._pick(which)]

    @staticmethod
    def _halves(a):
        half = a["rcv"].shape[1] // 2
        return pl.ds(0, half), pl.ds(half, half)

    def _hops(self, n, a):
        h0, h1 = self._halves(a)
        x, y = _peer(ACROSS_X), _peer(ACROSS_Y)
        snd, rcv, relay = a["snd"], a["rcv"], a["relay"]
        pairs = [(snd.at[2, h0], relay.at[0], x), (snd.at[2, h1], relay.at[1], y),
                 (snd.at[0, h0], rcv.at[0, h0], x), (snd.at[0, h1], rcv.at[0, h1], x),
                 (snd.at[1, h1], rcv.at[1, h1], y), (snd.at[1, h0], rcv.at[1, h0], y)]
        return [_remote(s, d, self.i_send.at[n, k], self.i_recv.at[n, k], to) for k, (s, d, to) in enumerate(pairs)]

    def _mine(self, a, chip, rows=None):
        src = a["stage"].at[chip] if a["staged"] else a["part"].at[chip, _me()[2]]
        mine, sib = (src[...], a["sib"][chip]) if rows is None else (src[rows, :], a["sib"][chip, rows, :])
        return mine.astype(F32) + sib.astype(F32)

    def start(self, which=None, chips=None):
        if chips is None:
            for cp in self._staging(which):
                cp.start()
        for chip in range(4) if chips is None else chips:
            for cp in self._first(which, chip):
                cp.start()

    def send_far(self, which=None):
        far = self._chip(ACROSS_BOTH)
        for cp in self._staging(which):
            cp.wait()
        for cp in self._first(which, far):
            cp.wait_recv()
        for n, a in self._pick(which):
            hops = self._hops(n, a)
            a["snd"][2] = self._mine(a, far).astype(a["snd"].dtype)
            hops[0].start()
            hops[1].start()

    def send_near(self, r, which=None):
        chip = self._chip(r)
        for cp in self._first(which, chip):
            cp.wait_recv()
        for n, a in self._pick(which):
            h0, h1 = self._halves(a)
            hops = self._hops(n, a)
            if r == ACROSS_X:
                a["snd"][0, h0, :] = self._mine(a, chip, h0).astype(a["snd"].dtype)
                hops[2].start()
            else:
                a["snd"][1, h1, :] = self._mine(a, chip, h1).astype(a["snd"].dtype)
                hops[4].start()

    def exchange(self, which=None):
        self.send_far(which)
        self.send_near(ACROSS_X, which)
        self.send_near(ACROSS_Y, which)

    def fold(self, which=None):
        across_x, across_y = self._chip(ACROSS_X), self._chip(ACROSS_Y)
        for n, a in self._pick(which):
            h0, h1 = self._halves(a)
            hops = self._hops(n, a)
            dtype = a["snd"].dtype
            hops[1].wait_recv()
            a["snd"][0, h1, :] = (self._mine(a, across_x, h1) + a["relay"][1].astype(F32)).astype(dtype)
            hops[3].start()
            hops[0].wait_recv()
            a["snd"][1, h0, :] = (self._mine(a, across_y, h0) + a["relay"][0].astype(F32)).astype(dtype)
            hops[5].start()

    def finish(self, which=None):
        home = self._chip(None)
        for cp in self._first(which, home):
            cp.wait_recv()
        for n, a in self._pick(which):
            hops = self._hops(n, a)
            a["out"][...] = self._mine(a, home)
            hops[2].wait_recv()
            hops[3].wait_recv()
            a["out"][...] += a["rcv"][0].astype(F32)
            hops[4].wait_recv()
            hops[5].wait_recv()
            a["out"][...] += a["rcv"][1].astype(F32)

    def wait_sends(self, which=None):
        for chip in range(4):
            for cp in self._first(which, chip):
                cp.wait_send()
        for n, a in self._pick(which):
            for cp in self._hops(n, a):
                cp.wait_send()


def _direct_exchange(src_of, dst_of, send_sems, recv_sems):
    me = _me()
    copies = [_remote(src_of(_peer(r)), dst_of(me), send_sems.at[r - 1], recv_sems.at[r - 1], _peer(r))
              for r in range(1, N_DEV)]
    for cp in copies:
        cp.start()
    return copies


def _wait_direct(copies):
    for cp in copies:
        cp.wait_recv()
    for cp in copies:
        cp.wait_send()


def _gathered_layer(layer, x, small, ln_g, ln_b, mine, name, mod=None, exchange=None, host=()):
    exchanging = exchange is not None
    cols = exchange[1].shape[2] if exchanging else 0
    shard = mine[0].shape[0]
    pair = 2 * shard
    n_lead = 4 if exchanging else 2
    n_host = len(host)

    def body(*refs):
        x_hbm = refs[0]
        lead, refs = refs[1:n_lead], refs[n_lead:]
        (wpool_ref, pscale_ref, slng_ref, slnb_ref, wsgu_ref, bsgut_ref, lng_ref, lnb_ref, wint_hbm,
         wout_hbm), refs = refs[:10], refs[10:]
        host_in, refs = refs[:n_host], refs[n_host:]
        (out_ref, y_ref, cdf_ref, proj_hbm, wint_keep, wout_keep), refs = refs[:6], refs[6:]
        if exchanging:
            c_ref, wada_hbm, bada_ref = lead
            (acts_ref, mod_ref), refs = refs[:2], refs[2:]
        else:
            mod_ref, = lead
        host_out, refs = refs[:n_host], refs[n_host:]
        (wint_v, wout_v, h_buf, proj_blk, proj_tile, halo_ref, x_ref, w_send, w_recv, w_local, p_sems,
         t_sems, in_sems), refs = refs[:13], refs[13:]
        if n_host:
            refs, (n_send, n_recv, n_local) = refs[:-3], refs[-3:]
        weights = _MixWeights(layer, wpool_ref, pscale_ref, slng_ref, slnb_ref, wsgu_ref, bsgut_ref)
        tile = pl.program_id(0)

        def hosted():
            return [_TwoLevelGather(host_out[n], n_send.at[n], n_recv.at[n], src=host_in[n]) for n in range(n_host)]

        def hosted_own():
            index = _index(_me())
            return [pltpu.make_async_copy(host_in[n], host_out[n].at[index], n_local.at[n]) for n in range(n_host)]

        def gathers():
            return (_TwoLevelGather(wint_v, w_send.at[0], w_recv.at[0], src=wint_hbm),
                    _TwoLevelGather(wout_v, w_send.at[1], w_recv.at[1], src=wout_hbm))

        def keeps():
            return [pltpu.make_async_copy(wint_v, wint_keep, w_local.at[2]),
                    pltpu.make_async_copy(wout_v, wout_keep, w_local.at[3])]

        def tile_read(t):
            slot = t % 2
            return pltpu.make_async_copy(proj_hbm.at[pl.ds(pl.multiple_of(t * TM, TM), TM)], proj_tile.at[slot],
                                         t_sems.at[slot])

        @pl.when(tile == 0)
        def _():
            me = _me()
            halo_ref[...] = jnp.zeros_like(halo_ref)
            gather_in, gather_out = gathers()
            own_in = pltpu.make_async_copy(wint_hbm, gather_in._slot(me), w_local.at[0])
            own_out = pltpu.make_async_copy(wout_hbm, gather_out._slot(me), w_local.at[1])
            x_load = pltpu.make_async_copy(x_hbm, x_ref, in_sems.at[0])
            x_load.start()

            if exchanging:
                act_all, act_src, part, mod_recv, wada_ref, a_send, a_recv, m_send, m_recv = refs
                wada_load = pltpu.make_async_copy(wada_hbm, wada_ref, in_sems.at[1])
                wada_load.start()
                mine_index = _index(me)
                cval = c_ref[...]
                act_src[...] = jnp.zeros_like(act_src)
                act_src[0:1, :] = cval * jax.nn.sigmoid(cval)
                act_all[mine_index] = act_src[...]
                act_copies = _direct_exchange(lambda p: act_src, lambda m: act_all.at[_index(m)], a_send, a_recv)

            own_in.start()
            own_out.start()
            gather_in.send_first()

            if exchanging:
                _wait_direct(act_copies)
                acts = jnp.concatenate([act_all[j, 0:1, :] for j in range(N_DEV)], axis=0)
                acts_ref[...] = acts
                part[...] = jnp.zeros_like(part)
                wada_load.wait()
                for l in range(DEPTH):
                    res = lax.dot_general(acts, wada_ref[l], (((1,), (0,)), ((), ())), preferred_element_type=F32,
                                          precision=lax.Precision.HIGHEST)
                    for b in range(N_DEV):
                        part[b, l:l + 1, :] = res[b:b + 1, :]
                mod_recv[mine_index] = part[mine_index]
                mod_copies = _direct_exchange(lambda p: part.at[_index(p)], lambda m: mod_recv.at[_index(m)],
                                              m_send, m_recv)
            gather_in.send_second()
            gather_out.send_mine()

            if exchanging:
                _wait_direct(mod_copies)
                for l in range(DEPTH):
                    for j in range(N_DEV):
                        sl = slice(j * cols, (j + 1) * cols)
                        mod_ref[l:l + 1, sl] = mod_recv[j, l:l + 1, :] + bada_ref[l:l + 1, sl]
            shift = mod_ref[layer:layer + 1, 0:D_MODEL]
            scale = mod_ref[layer:layer + 1, D_MODEL:2 * D_MODEL]
            x_load.wait()
            for t in range(N_TILES):
                rows = pl.ds(t * TM, TM)
                xn, _ = _ln(x_ref[rows, :])
                h_buf[rows, :] = (xn * (1.0 + scale) + shift).astype(BF16)

            chip_of = lambda dev: 2 * dev[0] + dev[1]
            writes = []

            def project(n, dev):
                first = pl.multiple_of(chip_of(dev) * pair, pair)
                if n >= 2:
                    writes[n - 2].wait()
                proj_blk[n % 2] = _dot_nt(h_buf[...], wint_v[pl.ds(first, pair), :])
                cp = pltpu.make_async_copy(proj_blk.at[n % 2], proj_hbm.at[:, pl.ds(first, pair)], p_sems.at[n % 2])
                cp.start()
                writes.append(cp)

            gather_in.relay()
            own_in.wait()
            gather_in.wait_sibling()
            project(0, me)
            gather_in.pass_near()
            gather_in.wait_passed(ACROSS_X)
            project(1, _peer(ACROSS_X))
            gather_out.relay()
            for cp in hosted_own():
                cp.start()
            for g in hosted():
                g.send_mine()
            gather_in.wait_passed(ACROSS_Y)
            project(2, _peer(ACROSS_Y))
            gather_in.pass_far()
            gather_in.wait_passed(ACROSS_BOTH)
            project(3, _peer(ACROSS_BOTH))

            gather_out.pass_on()
            gather_out.wait_rest()
            own_out.wait()
            for cp in keeps():
                cp.start()
            writes[2].wait()
            writes[3].wait()
            tile_read(0).start()

        @pl.when(tile + 1 < N_TILES)
        def _():
            tile_read(tile + 1).start()

        if n_host:
            @pl.when(tile == 1)
            def _():
                for g in hosted():
                    g.relay()

            @pl.when(tile == N_TILES // 2)
            def _():
                for g in hosted():
                    g.pass_near()

        tile_read(tile).wait()
        xt = x_ref[pl.ds(pl.multiple_of(tile * TM, TM), TM), :]
        gate = mod_ref[layer:layer + 1, 2 * D_MODEL:]
        proj = proj_tile[tile % 2]
        cat, cdf_ref[...] = _mix_forward(proj, halo_ref[...], tile, weights)
        halo_ref[...] = proj[TM - HALO:, 0:D_POOL]
        y = _dot(cat.astype(BF16), wout_v[...])
        y_ref[...] = y
        zn, _ = _ln(ALPHA * xt + gate * y)
        out_ref[...] = zn * lng_ref[layer:layer + 1, :] + lnb_ref[layer:layer + 1, :]

        @pl.when(tile == N_TILES - 1)
        def _():
            for g in hosted():
                g.pass_far()
            for g in hosted():
                g.wait_rest()
            for g in list(gathers()) + hosted():
                g.wait_sends()
            for cp in keeps() + hosted_own():
                cp.wait()

    row = lambda w: pl.BlockSpec((TM, w), lambda i: (i, 0))
    gather_sems = pltpu.SemaphoreType.DMA((2, GATHER_SEMS))
    seven = pltpu.SemaphoreType.DMA((7,))
    lead_in = list(exchange) if exchanging else [mod]
    lead_specs = [_const_spec(a.shape) for a in lead_in]
    if exchanging:
        lead_specs[1] = ANY
    extra_out_specs = [_const_spec((N_DEV, D_MODEL)), _const_spec((DEPTH, 3 * D_MODEL))] if exchanging else []
    extra_out_shape = [jax.ShapeDtypeStruct((N_DEV, D_MODEL), F32),
                       jax.ShapeDtypeStruct((DEPTH, 3 * D_MODEL), F32)] if exchanging else []
    extra_scratch = [pltpu.VMEM((N_DEV, 8, D_MODEL), F32), pltpu.VMEM((8, D_MODEL), F32),
                     pltpu.VMEM((N_DEV, 8, cols), F32), pltpu.VMEM((N_DEV, 8, cols), F32),
                     pltpu.VMEM(exchange[1].shape, F32), seven, seven, seven, seven] if exchanging else []
    host_scratch = [pltpu.SemaphoreType.DMA((n_host, GATHER_SEMS)), pltpu.SemaphoreType.DMA((n_host, GATHER_SEMS)),
                    pltpu.SemaphoreType.DMA((n_host,))] if n_host else []
    return pl.pallas_call(
        body,
        name=name,
        grid=(N_TILES,),
        in_specs=[ANY] + lead_specs + [_const_spec(s) for s in SMALL_SPECS]
                 + [_const_spec((DEPTH, D_MODEL)), _const_spec((DEPTH, D_MODEL)), ANY, ANY] + [ANY] * n_host,
        out_specs=[row(D_MODEL), row(D_MODEL), row(2 * D_SGU), ANY, ANY, ANY] + extra_out_specs + [ANY] * n_host,
        out_shape=[jax.ShapeDtypeStruct((SEQ, D_MODEL), F32), jax.ShapeDtypeStruct((SEQ, D_MODEL), F32),
                   jax.ShapeDtypeStruct((SEQ, 2 * D_SGU), F32), jax.ShapeDtypeStruct((SEQ, D_IN), F32),
                   jax.ShapeDtypeStruct((D_IN, D_MODEL), BF16), jax.ShapeDtypeStruct((D_MODEL, D_MODEL), BF16)]
                  + extra_out_shape + [jax.ShapeDtypeStruct((N_DEV,) + blk.shape, blk.dtype) for blk in host],
        scratch_shapes=[pltpu.VMEM((D_IN, D_MODEL), BF16), pltpu.VMEM((D_MODEL, D_MODEL), BF16),
                        pltpu.VMEM((SEQ, D_MODEL), BF16), pltpu.VMEM((2, SEQ, pair), F32),
                        pltpu.VMEM((2, TM, D_IN), F32), pltpu.VMEM((HALO, D_POOL), F32),
                        pltpu.VMEM((SEQ, D_MODEL), F32),
                        gather_sems, gather_sems, pltpu.SemaphoreType.DMA((4,)), pltpu.SemaphoreType.DMA((2,)),
                        pltpu.SemaphoreType.DMA((2,)), pltpu.SemaphoreType.DMA((2,))] + extra_scratch + host_scratch,
        compiler_params=pltpu.CompilerParams(dimension_semantics=("arbitrary",), vmem_limit_bytes=VMEM_LIMIT),
    )(x, *lead_in, *small, ln_g, ln_b, *mine, *host)


ADA_CHUNK = 256


def _grad_tail(dproj, h, cat, dy, small, dmod8, loss_lanes, w_ada, m_ada, v_ada, act_t, b_ada, m_bada, v_bada):
    shard_in, shard_out, shard_small = D_IN // N_DEV, D_MODEL // N_DEV, small.shape[2]
    cols = w_ada.shape[2]
    W_IN, W_OUT, SMALL = 0, 1, 2

    def body(dproj_hbm, h_hbm, cat_hbm, dy_hbm, small_hbm, dmod_ref, lanes_ref, wada_hbm, mada_hbm, vada_hbm,
             act_ref, bada_ref, mbada_ref, vbada_ref,
             gwin_ref, gwout_ref, stot_ref, loss_ref, gada_hbm, dada_hbm, nmada_hbm, nvada_hbm,
             gb_ref, db_ref, nmb_ref, nvb_ref,
             dproj_v, h_v, cat_v, dy_v, part_in, part_out, own_small, loss_src, loss_all, dmod_all, ada_in, ada_out,
             *rest):
        bufs, rest = rest[:13], rest[13:]
        load_sems, rs_sems = rest[0], rest[1:6]
        m_send, m_recv, g_send, g_recv, s_send, s_recv, ada_lsem, ada_ssem = rest[6:]
        mine = _index(_me())

        def update_ada():
            upper = (mine % 2) == 1

            def dmod_of(layer):
                rows = []
                for b in range(N_DEV):
                    r = dmod_all[b, pl.ds(4 * layer + mine // 2, 1), :]
                    rows.append(jnp.where(upper, r[:, cols:], r[:, :cols]))
                return jnp.concatenate(rows, axis=0)

            chunks = [(layer, c) for layer in range(DEPTH) for c in range(D_MODEL // ADA_CHUNK)]

            def loads(i):
                layer, c = chunks[i]
                rows = pl.ds(c * ADA_CHUNK, ADA_CHUNK)
                return [pltpu.make_async_copy(src.at[layer, rows], ada_in.at[i % 2, k], ada_lsem.at[i % 2, k])
                        for k, src in enumerate((wada_hbm, mada_hbm, vada_hbm))]

            def stores(i):
                layer, c = chunks[i]
                rows = pl.ds(c * ADA_CHUNK, ADA_CHUNK)
                return [pltpu.make_async_copy(ada_out.at[i % 2, k], dst.at[layer, rows], ada_ssem.at[i % 2, k])
                        for k, dst in enumerate((gada_hbm, dada_hbm, nmada_hbm, nvada_hbm))]

            for cp in loads(0):
                cp.start()
            dmods = {}
            for i, (layer, c) in enumerate(chunks):
                if i + 1 < len(chunks):
                    for cp in loads(i + 1):
                        cp.start()
                for cp in loads(i):
                    cp.wait()
                if i >= 2:
                    for cp in stores(i - 2):
                        cp.wait()
                if layer not in dmods:
                    dmods[layer] = dmod_of(layer)
                act = act_ref[pl.ds(c * ADA_CHUNK, ADA_CHUNK), :]
                g = act[:, 0:1] * dmods[layer][0:1, :]
                for b in range(1, N_DEV):
                    g = g + act[:, b:b + 1] * dmods[layer][b:b + 1, :]
                slot = i % 2
                delta, new_m, new_v = _adamw_math(ada_in[slot, 0], g, ada_in[slot, 1], ada_in[slot, 2])
                ada_out[slot, 0] = g
                ada_out[slot, 1] = delta
                ada_out[slot, 2] = new_m
                ada_out[slot, 3] = new_v
                for cp in stores(i):
                    cp.start()
            for i in (len(chunks) - 2, len(chunks) - 1):
                for cp in stores(i):
                    cp.wait()

            total = dmod_all[0]
            for b in range(1, N_DEV):
                total = total + dmod_all[b]
            width = total.shape[1]
            for layer in range(DEPTH):
                for q in range(4):
                    gb_ref[layer:layer + 1, q * width:(q + 1) * width] = total[4 * layer + q:4 * layer + q + 1, :]
            db_ref[...], nmb_ref[...], nvb_ref[...] = _adamw_math(bada_ref[...], gb_ref[...], mbada_ref[...],
                                                                  vbada_ref[...])

        order = (ACROSS_BOTH, ACROSS_X, ACROSS_Y, None)
        chips = [_ChipReduceScatter._chip(r) for r in order]
        loads = [pltpu.make_async_copy(s, d, load_sems.at[n]) for n, (s, d) in enumerate(
            ((cat_hbm, cat_v), (dy_hbm, dy_v), (h_hbm, h_v)))]
        loads += [pltpu.make_async_copy(dproj_hbm.at[:, pl.ds(pl.multiple_of(chip * 2 * shard_in, 2 * shard_in),
                                                             2 * shard_in)], dproj_v.at[n], load_sems.at[3 + n])
                  for n, chip in enumerate(chips)]
        for cp in loads:
            cp.start()
        arrays = [dict(part=part_in, out=gwin_ref, staged=False, sib=bufs[0], snd=bufs[1], rcv=bufs[2], relay=bufs[3]),
                  dict(part=part_out, out=gwout_ref, staged=False, sib=bufs[4], snd=bufs[5], rcv=bufs[6],
                       relay=bufs[7]),
                  dict(part=small_hbm, out=own_small, staged=True, stage=bufs[8], sib=bufs[9], snd=bufs[10],
                       rcv=bufs[11], relay=bufs[12])]
        scatter = _ChipReduceScatter(arrays, *rs_sems)
        scatter.start([SMALL])
        dmod_all[mine] = dmod_ref[...]
        dmod_copies = _direct_exchange(lambda p: dmod_ref, lambda m: dmod_all.at[_index(m)], m_send, m_recv)
        loss_src[...] = jnp.full(loss_src.shape, (0.5 / D_MODEL) * jnp.sum(lanes_ref[...]), F32)
        loss_all[mine] = loss_src[...]
        loss_copies = _direct_exchange(lambda p: loss_src, lambda m: loss_all.at[_index(m)], s_send, s_recv)

        loads[0].wait()
        loads[1].wait()
        for blk in range(2):
            res = _dot_tn(cat_v[:, blk * 512:(blk + 1) * 512], dy_v[...]).astype(BF16)
            for s in range(4):
                part_out[2 * blk + s // 2, s % 2] = res[s * shard_out:(s + 1) * shard_out]
        scatter.start([W_OUT])
        scatter.exchange([SMALL])

        gather = _TwoLevelGather(stot_ref, g_send, g_recv)
        loads[2].wait()
        for n, chip in enumerate(chips):
            loads[3 + n].wait()
            res = _dot_tn(dproj_v[n], h_v[...]).astype(BF16)
            part_in[chip, 0] = res[:shard_in]
            part_in[chip, 1] = res[shard_in:]
            scatter.start([W_IN], chips=[chip])
            if n == 0:
                scatter.exchange([W_OUT])
                scatter.fold([SMALL])
            if n == 1:
                scatter.send_far([W_IN])
                scatter.fold([W_OUT])
                scatter.finish([SMALL])
                stot_ref[mine] = own_small[...]
                gather.send_mine()
            if n == 2:
                scatter.send_near(ACROSS_X, [W_IN])
                gather.relay()
            if n == 3:
                scatter.send_near(ACROSS_Y, [W_IN])
        scatter.fold([W_IN])
        scatter.finish([W_OUT])
        _wait_direct(dmod_copies)
        update_ada()
        gather.pass_on()
        gather.wait_rest()
        _wait_direct(loss_copies)
        total = loss_all[0]
        for j in range(1, N_DEV):
            total = total + loss_all[j]
        loss_ref[...] = total
        scatter.finish([W_IN])
        gather.wait_sends()
        scatter.wait_sends()

    buffers = _ChipReduceScatter.buffers
    comm_scratch = (buffers(shard_in, D_MODEL, BF16, staged=False) + buffers(shard_out, D_MODEL, BF16, staged=False)
                    + buffers(shard_small, 128, F32))
    comm_scratch += [pltpu.SemaphoreType.DMA((7,))] + _ChipReduceScatter.semaphores(3)
    comm_scratch += [pltpu.SemaphoreType.DMA((n,)) for n in (7, 7, GATHER_SEMS, GATHER_SEMS, 7, 7)]
    comm_scratch += [pltpu.SemaphoreType.DMA((2, 3)), pltpu.SemaphoreType.DMA((2, 4))]
    return pl.pallas_call(
        body,
        name="grad_tail",
        in_specs=[ANY] * 5 + [VMEM, VMEM] + [ANY] * 3 + [VMEM] * 4,
        out_specs=[VMEM] * 4 + [ANY] * 4 + [VMEM] * 4,
        out_shape=[jax.ShapeDtypeStruct((shard_in, D_MODEL), F32), jax.ShapeDtypeStruct((shard_out, D_MODEL), F32),
                   jax.ShapeDtypeStruct((N_DEV, shard_small, 128), F32), jax.ShapeDtypeStruct((8, 128), F32)]
                  + [jax.ShapeDtypeStruct(w_ada.shape, F32)] * 4 + [jax.ShapeDtypeStruct(b_ada.shape, F32)] * 4,
        scratch_shapes=[pltpu.VMEM((4, SEQ, 2 * shard_in), BF16), pltpu.VMEM(h.shape, BF16), pltpu.VMEM(cat.shape, BF16),
                        pltpu.VMEM(dy.shape, BF16), pltpu.VMEM((4, 2, shard_in, D_MODEL), BF16),
                        pltpu.VMEM((4, 2, shard_out, D_MODEL), BF16), pltpu.VMEM((shard_small, 128), F32),
                        pltpu.VMEM((8, 128), F32), pltpu.VMEM((N_DEV, 8, 128), F32),
                        pltpu.VMEM((N_DEV,) + dmod8.shape, F32), pltpu.VMEM((2, 3, ADA_CHUNK, cols), F32),
                        pltpu.VMEM((2, 4, ADA_CHUNK, cols), F32)] + comm_scratch,
        compiler_params=pltpu.CompilerParams(vmem_limit_bytes=VMEM_LIMIT),
    )(dproj, h, cat, dy, small, dmod8, loss_lanes, w_ada, m_ada, v_ada, act_t, b_ada, m_bada, v_bada)


SMALL_NAMES = ("w_pool", "w_sgu", "pool_scale", "sgu_ln_g", "sgu_ln_b", "b_sgu", "ln_g", "ln_b")
SMALL_ROWS = (512, 512, 4, 4, 4, 4, 8, 8)


def _adamw_small(g_packed, ws, ms, vs, name):
    n = len(SMALL_NAMES)

    def body(g_ref, *refs):
        w_refs, m_refs, v_refs = refs[:n], refs[n:2 * n], refs[2 * n:3 * n]
        outs = refs[3 * n:]

        def update(p, at, g):
            delta, new_m, new_v = _adamw_math(w_refs[p][at], g, m_refs[p][at], v_refs[p][at])
            outs[p][at] = g
            outs[n + p][at] = delta
            outs[2 * n + p][at] = new_m
            outs[3 * n + p][at] = new_v

        row = 0
        for p, r in enumerate(SMALL_ROWS):
            shape = ws[p].shape
            for layer in range(DEPTH):
                first = layer * PACK_ROWS + row
                if len(shape) == 4:
                    for k in range(shape[1]):
                        update(p, (layer, k), g_ref[first + k * shape[2]:first + (k + 1) * shape[2], :])
                elif len(shape) == 3:
                    update(p, (layer,), g_ref[first:first + r, :])
                else:
                    g = jnp.concatenate([g_ref[first + k:first + k + 1, :] for k in range(r)], axis=1)
                    update(p, (slice(layer, layer + 1), slice(None)), g)
            row += r

    res = pl.pallas_call(
        body,
        name=name,
        out_shape=[jax.ShapeDtypeStruct(w.shape, F32) for w in ws] * 4,
        compiler_params=pltpu.CompilerParams(vmem_limit_bytes=VMEM_LIMIT),
    )(g_packed, *ws, *ms, *vs)
    return res[:n], res[n:2 * n], res[2 * n:3 * n], res[3 * n:]


def kernel(x, c, w_ada, b_ada, w_in, w_pool, pool_scale, sgu_ln_g, sgu_ln_b, w_sgu, b_sgu, w_out, ln_g, ln_b, loss_target, m_w_ada, m_b_ada, m_w_in, m_w_pool, m_pool_scale, m_sgu_ln_g, m_sgu_ln_b, m_w_sgu, m_b_sgu, m_w_out, m_ln_g, m_ln_b, v_w_ada, v_b_ada, v_w_in, v_w_pool, v_pool_scale, v_sgu_ln_g, v_sgu_ln_b, v_w_sgu, v_b_sgu, v_w_out, v_ln_g, v_ln_b):
    small_w = dict(w_pool=w_pool, w_sgu=w_sgu, pool_scale=pool_scale, sgu_ln_g=sgu_ln_g, sgu_ln_b=sgu_ln_b,
                   b_sgu=b_sgu, ln_g=ln_g, ln_b=ln_b)
    small_m = dict(w_pool=m_w_pool, w_sgu=m_w_sgu, pool_scale=m_pool_scale, sgu_ln_g=m_sgu_ln_g,
                   sgu_ln_b=m_sgu_ln_b, b_sgu=m_b_sgu, ln_g=m_ln_g, ln_b=m_ln_b)
    small_v = dict(w_pool=v_w_pool, w_sgu=v_w_sgu, pool_scale=v_pool_scale, sgu_ln_g=v_sgu_ln_g,
                   sgu_ln_b=v_sgu_ln_b, b_sgu=v_b_sgu, ln_g=v_ln_g, ln_b=v_ln_b)

    wint_loc = jnp.transpose(w_in, (0, 2, 1)).astype(BF16)
    wout_loc = w_out.astype(BF16)
    small = (w_pool, pool_scale, sgu_ln_g, sgu_ln_b, w_sgu, jnp.transpose(b_sgu, (0, 2, 1)))
    out, y, cdf, proj, wint0, wout0, act_all, mod, wint1, wout1 = _gathered_layer(
        0, x[0], small, ln_g, ln_b, [wint_loc[0], wout_loc[0]], "layer_fwd_0", exchange=(c, w_ada, b_ada),
        host=[wint_loc[1], wout_loc[1]])
    w_int, w_outf = [wint0, wint1.reshape(D_IN, D_MODEL)], [wout0, wout1.reshape(D_MODEL, D_MODEL)]
    acts, cur = [(x[0], proj, y, cdf)], out
    for l in range(1, DEPTH):
        out, proj, y, cdf = _layer_forward(l, cur, mod, w_int[l], w_outf[l], small, ln_g, ln_b, f"layer_fwd_{l}")
        acts.append((cur, proj, y, cdf))
        cur = out

    shard_in, shard_out = D_IN // N_DEV, D_MODEL // N_DEV
    a, b = cur, loss_target[0]
    loss_lanes, carry, pending = None, (), []
    g_w_in_t, g_w_out = [None] * DEPTH, [None] * DEPTH
    for l in reversed(range(DEPTH)):
        dx, dproj, h, cat, dy, small_grads, dmod8, lanes, *shards = _layer_backward(
            l, a, b, *acts[l], mod, w_int[l], w_outf[l], small, ln_g, l == DEPTH - 1, f"layer_bwd_{l}",
            carry=carry, reduce=pending)
        if shards:
            g_w_in_t[l + 1], g_w_out[l + 1] = shards
        if l == DEPTH - 1:
            loss_lanes = lanes
        if l > 0:
            g_in, g_out = _grad_matmuls(dproj, h, cat, dy, f"grad_w_{l}")
            pending = [g_in.reshape(4, 2, shard_in, D_MODEL), g_out.reshape(4, 2, shard_out, D_MODEL)]
        carry = (small_grads, dmod8)
        a = b = dx
    grad_x = a[None]

    (g_w_in_t[0], g_w_out[0], small_tot, loss_tile, g_w_ada, d_w_ada, nm_w_ada, nv_w_ada,
     g_b_ada, d_b_ada, nm_b_ada, nv_b_ada) = _grad_tail(
        dproj, h, cat, dy, small_grads.reshape(4, 2, DEPTH * PACK_ROWS // N_DEV, 128), dmod8, loss_lanes,
        w_ada, m_w_ada, v_w_ada, jnp.transpose(act_all), b_ada, m_b_ada, v_b_ada)
    loss = loss_tile[0, 0]

    flat = lambda t: t.reshape(-1, t.shape[-1])
    to_t = lambda t: flat(jnp.transpose(t, (0, 2, 1)))
    from_t = lambda t: jnp.transpose(t.reshape(DEPTH, shard_in, D_MODEL), (0, 2, 1))
    g_w_in, d_w_in, nm_w_in, nv_w_in = [from_t(t) for t in _adamw(to_t(w_in), g_w_in_t, to_t(m_w_in), to_t(v_w_in),
                                                                  shard_in // 2, "adamw_w_in")]
    gwout, d_w_out, nm_w_out, nv_w_out = [t.reshape(w_out.shape) for t in _adamw(
        flat(w_out), g_w_out, flat(m_w_out), flat(v_w_out), shard_out, "adamw_w_out")]
    small_out = _adamw_small(small_tot.reshape(DEPTH * PACK_ROWS, 128), [small_w[n] for n in SMALL_NAMES],
                             [small_m[n] for n in SMALL_NAMES], [small_v[n] for n in SMALL_NAMES], "adamw_small")
    gs, ds, ms, vs = [dict(zip(SMALL_NAMES, group)) for group in small_out]

    def ordered(w_ada_, b_ada_, w_in_, small, w_out_):
        return (w_ada_, b_ada_, w_in_, small["w_pool"], small["pool_scale"], small["sgu_ln_g"], small["sgu_ln_b"],
                small["w_sgu"], small["b_sgu"], w_out_, small["ln_g"], small["ln_b"])

    return (loss, grad_x,
            *ordered(g_w_ada, g_b_ada, g_w_in, gs, gwout),
            *ordered(d_w_ada, d_b_ada, d_w_in, ds, d_w_out),
            *ordered(nm_w_ada, nm_b_ada, nm_w_in, ms, nm_w_out),
            *ordered(nv_w_ada, nv_b_ada, nv_w_in, vs, nv_w_out))
```

```python
import jax
import jax.numpy as jnp
from jax import lax
from jax.experimental import pallas as pl
from jax.experimental.pallas import tpu as pltpu

F32 = jnp.float32
BF16 = jnp.bfloat16

D_MODEL = 1024
SEQ = 2048
DEPTH = 2
D_POOL = 512
D_SGU = 512
D_IN = 2560
N_GROUPS = 4
GROUP = 128
N_HEADS = 4
HEAD = 128
CHUNK = 128
WINDOWS = (2, 4, 8, 16)
ALPHA = (2.0 * DEPTH) ** 0.25
LN_EPS = 1e-5
N_DEV = 8

ADAM_LR = 0.001
ADAM_B1 = 0.9
ADAM_B2 = 0.999
ADAM_EPS = 1e-08
ADAM_WD = 0.01
ADAM_STEP = 10

TM = 256
HALO = 16
N_TILES = SEQ // TM
VMEM_LIMIT = 60 * 1024 * 1024

ROW_WPOOL = 0
ROW_WSGU = 512
ROW_PSCALE = 1024
ROW_SLNG = 1028
ROW_SLNB = 1032
ROW_BSGU = 1036
ROW_LNG = 1040
ROW_LNB = 1048
PACK_ROWS = 1088
DMOD_COLS = DEPTH * 3 * D_MODEL // 8

SQRT_HALF = 0.7071067811865476
INV_SQRT_2PI = 0.3989422804014327


def _ln(x):
    mu = jnp.mean(x, axis=-1, keepdims=True)
    xc = x - mu
    var = jnp.mean(xc * xc, axis=-1, keepdims=True)
    rstd = lax.rsqrt(var + LN_EPS)
    return xc * rstd, rstd


def _ln_bwd(dxn, xn, rstd):
    m1 = jnp.mean(dxn, axis=-1, keepdims=True)
    m2 = jnp.mean(dxn * xn, axis=-1, keepdims=True)
    return rstd * (dxn - m1 - xn * m2)


def _normal_cdf(x):
    return 0.5 * (1.0 + lax.erf(x * SQRT_HALF))


def _gelu_parts(x, cdf, with_grad):
    if not with_grad:
        return x * cdf, None
    return x * cdf, cdf + x * (INV_SQRT_2PI * jnp.exp(-0.5 * x * x))


def _silu_parts(x):
    s = jax.nn.sigmoid(x)
    return x * s, s * (1.0 + x * (1.0 - s))


def _dot(a, b):
    return lax.dot_general(a, b, (((1,), (0,)), ((), ())), preferred_element_type=F32)


def _dot_nt(a, b):
    return lax.dot_general(a, b, (((1,), (1,)), ((), ())), preferred_element_type=F32)


def _dot_tn(a, b):
    return lax.dot_general(a, b, (((0,), (0,)), ((), ())), preferred_element_type=F32)


def _row_index(tile):
    return tile * TM + lax.broadcasted_iota(jnp.int32, (TM, 1), 0)


def _window_sums(ext, forward):
    n = TM + HALO
    cur = ext
    outs = []
    for g in range(N_GROUPS):
        step = 1 << g
        cur = cur + pltpu.roll(cur, step if forward else n - step, 0)
        rows = cur[HALO:, :GROUP] if forward else cur[:TM, :GROUP]
        outs.append(rows)
        cur = cur[:, GROUP:] if g + 1 < N_GROUPS else None
    return outs


def _inverse_counts(rows):
    return [1.0 / jnp.minimum(rows + 1, w).astype(F32) for w in WINDOWS]


def _tril_bf16(w):
    t = lax.broadcasted_iota(jnp.int32, (CHUNK, CHUNK), 0)
    s = lax.broadcasted_iota(jnp.int32, (CHUNK, CHUNK), 1)
    return jnp.where(t >= s, w, 0.0).astype(BF16)


class _MixWeights:
    def __init__(self, layer, wpool_ref, pscale_ref, slng_ref, slnb_ref, wsgu_ref, bsgut_ref):
        self.layer = layer
        self.wpool_ref, self.pscale_ref, self.slng_ref, self.slnb_ref = wpool_ref, pscale_ref, slng_ref, slnb_ref
        self.wsgu_ref, self.bsgut_ref = wsgu_ref, bsgut_ref

    def pool(self, g):
        return self.wpool_ref[self.layer, g].astype(BF16)

    def pool_scale(self, g):
        return self.pscale_ref[self.layer:self.layer + 1, g * GROUP:(g + 1) * GROUP]

    def ln_gain(self, h):
        return self.slng_ref[self.layer, h:h + 1, :]

    def ln_bias(self, h):
        return self.slnb_ref[self.layer, h:h + 1, :]

    def mix(self, h):
        return _tril_bf16(self.wsgu_ref[self.layer, h])

    def mix_bias(self, h):
        return self.bsgut_ref[self.layer, :, h:h + 1]


SMALL_SPECS = ((DEPTH, N_GROUPS, GROUP, GROUP), (DEPTH, D_POOL), (DEPTH, N_HEADS, HEAD), (DEPTH, N_HEADS, HEAD),
               (DEPTH, N_HEADS, CHUNK, CHUNK), (DEPTH, CHUNK, N_HEADS))


def _mix_forward(proj, halo, tile, w, cdf=None):
    keep = cdf is not None
    rows = _row_index(tile)
    inv_counts = _inverse_counts(rows)
    xa = proj[:, 0:D_POOL]
    ga = proj[:, D_POOL:2 * D_POOL]
    sums = _window_sums(jnp.concatenate([halo, xa], axis=0), True)
    ga_act, ga_grad = _silu_parts(ga)
    pooled, pw, ya = [], [], []
    for g in range(N_GROUPS):
        sl = slice(g * GROUP, (g + 1) * GROUP)
        p = (sums[g] * inv_counts[g] - xa[:, sl]).astype(BF16)
        q = _dot(p, w.pool(g))
        pooled.append(p)
        pw.append(q)
        ya.append(q * w.pool_scale(g) * ga_act[:, sl])

    u = proj[:, 2 * D_POOL:2 * D_POOL + D_SGU]
    v = proj[:, 2 * D_POOL + D_SGU:2 * D_POOL + 2 * D_SGU]
    gb = proj[:, 2 * D_POOL + 2 * D_SGU:]
    gb_act, gb_grad = _silu_parts(gb)
    if cdf is None:
        cdf = jnp.concatenate([_normal_cdf(u), _normal_cdf(v)], axis=1)
    u_act, u_grad = _gelu_parts(u, cdf[:, :D_SGU], keep)
    v_act, v_grad = _gelu_parts(v, cdf[:, D_SGU:], keep)
    vn, vrstd, vln, mixed, yb = [], [], [], [], []
    for h in range(N_HEADS):
        sl = slice(h * HEAD, (h + 1) * HEAD)
        n_h, r_h = _ln(v_act[:, sl])
        l_h = (n_h * w.ln_gain(h) + w.ln_bias(h)).astype(BF16)
        w_h = w.mix(h)
        bias = w.mix_bias(h)
        m_h = jnp.concatenate(
            [_dot(w_h, l_h[k * CHUNK:(k + 1) * CHUNK]) + bias for k in range(TM // CHUNK)], axis=0)
        vn.append(n_h)
        vrstd.append(r_h)
        vln.append(l_h)
        mixed.append(m_h)
        yb.append(u_act[:, sl] * m_h * gb_act[:, sl])
    cat = jnp.concatenate(ya + yb, axis=1)
    if not keep:
        return cat, cdf
    return cat, dict(inv_counts=inv_counts, ga_act=ga_act, ga_grad=ga_grad, pooled=pooled, pw=pw, u_grad=u_grad,
                     v_grad=v_grad, u_act=u_act, gb_act=gb_act, gb_grad=gb_grad, vn=vn, vrstd=vrstd, vln=vln,
                     mixed=mixed)


def _const_spec(shape):
    nd = len(shape)
    return pl.BlockSpec(shape, lambda i: (0,) * nd)


def _layer_forward(layer, x, mod, w_int, w_outf, small, ln_g, ln_b, name):
    def body(x_ref, mod_ref, wint_ref, wout_ref, wpool_ref, pscale_ref, slng_ref, slnb_ref, wsgu_ref, bsgut_ref,
             lng_ref, lnb_ref, out_ref, proj_ref, y_ref, cdf_ref, halo_ref):
        weights = _MixWeights(layer, wpool_ref, pscale_ref, slng_ref, slnb_ref, wsgu_ref, bsgut_ref)
        tile = pl.program_id(0)

        @pl.when(tile == 0)
        def _():
            halo_ref[...] = jnp.zeros_like(halo_ref)

        xt = x_ref[...]
        shift = mod_ref[layer:layer + 1, 0:D_MODEL]
        scale = mod_ref[layer:layer + 1, D_MODEL:2 * D_MODEL]
        gate = mod_ref[layer:layer + 1, 2 * D_MODEL:]
        xn, _ = _ln(xt)
        h = (xn * (1.0 + scale) + shift).astype(BF16)
        proj = _dot_nt(h, wint_ref[...])
        proj_ref[...] = proj
        cat, cdf_ref[...] = _mix_forward(proj, halo_ref[...], tile, weights)
        halo_ref[...] = proj[TM - HALO:, 0:D_POOL]
        y = _dot(cat.astype(BF16), wout_ref[...])
        y_ref[...] = y
        zn, _ = _ln(ALPHA * xt + gate * y)
        out_ref[...] = zn * lng_ref[layer:layer + 1, :] + lnb_ref[layer:layer + 1, :]

    row = lambda w: pl.BlockSpec((TM, w), lambda i: (i, 0))
    return pl.pallas_call(
        body,
        name=name,
        grid=(N_TILES,),
        in_specs=[row(D_MODEL), _const_spec((DEPTH, 3 * D_MODEL)), _const_spec((D_IN, D_MODEL)),
                  _const_spec((D_MODEL, D_MODEL))] + [_const_spec(s) for s in SMALL_SPECS]
                 + [_const_spec((DEPTH, D_MODEL)), _const_spec((DEPTH, D_MODEL))],
        out_specs=[row(D_MODEL), row(D_IN), row(D_MODEL), row(2 * D_SGU)],
        out_shape=[jax.ShapeDtypeStruct((SEQ, D_MODEL), F32), jax.ShapeDtypeStruct((SEQ, D_IN), F32),
                   jax.ShapeDtypeStruct((SEQ, D_MODEL), F32), jax.ShapeDtypeStruct((SEQ, 2 * D_SGU), F32)],
        scratch_shapes=[pltpu.VMEM((HALO, D_POOL), F32)],
        compiler_params=pltpu.CompilerParams(dimension_semantics=("arbitrary",), vmem_limit_bytes=VMEM_LIMIT),
    )(x, mod, w_int, w_outf, *small, ln_g, ln_b)


VEC_LNG, VEC_LNB, VEC_POOL, VEC_SGU, VEC_SHIFT, VEC_SCALE, VEC_GATE, VEC_LOSS = range(8)


def _layer_backward(layer, a, b, x, proj, y, cdf, mod, w_int, w_outf, small, ln_g, is_last, name, carry=(),
                    reduce=()):
    n_red, n_carry = len(reduce), len(carry)
    base = layer * PACK_ROWS

    def body(a_ref, b_ref, x_ref, proj_ref, prev_ref, y_ref, cdf_ref, mod_ref, wint_hbm, wout_hbm, wpool_ref,
             pscale_ref, slng_ref, slnb_ref, wsgu_ref, bsgut_ref, lng_ref, *rest):
        weights = _MixWeights(layer, wpool_ref, pscale_ref, slng_ref, slnb_ref, wsgu_ref, bsgut_ref)
        carry_refs, rest = rest[:n_carry], rest[n_carry:]
        part_refs, rest = rest[:n_red], rest[n_red:]
        dx_ref, dproj_ref, h_ref, cat_ref, dy_ref, small_ref, dmod_ref, loss_ref = rest[:8]
        shard_refs, rest = rest[8:8 + n_red], rest[8 + n_red:]
        vec_ref, dmix_ref, halo_ref, wint_ref, wout_ref, w_sems = rest[:6]
        step = pl.program_id(0)
        tile = N_TILES - 1 - step

        def weight_loads():
            return (pltpu.make_async_copy(wout_hbm, wout_ref, w_sems.at[0]),
                    pltpu.make_async_copy(wint_hbm, wint_ref, w_sems.at[1]))

        def scatter():
            bufs, sems = rest[6:6 + 5 * n_red], rest[6 + 5 * n_red:]
            arrays = [dict(part=part_refs[n], out=shard_refs[n], staged=True, stage=bufs[5 * n], sib=bufs[5 * n + 1],
                           snd=bufs[5 * n + 2], rcv=bufs[5 * n + 3], relay=bufs[5 * n + 4]) for n in range(n_red)]
            return _ChipReduceScatter(arrays, *sems)

        @pl.when(step == 0)
        def _():
            for cp in weight_loads():
                cp.start()
            small_ref[...] = jnp.zeros_like(small_ref)
            dmod_ref[...] = jnp.zeros_like(dmod_ref)
            vec_ref[...] = jnp.zeros_like(vec_ref)
            dmix_ref[...] = jnp.zeros_like(dmix_ref)
            halo_ref[...] = jnp.zeros_like(halo_ref)
            if n_red:
                scatter().start()

        if n_red:
            @pl.when(step == 1)
            def _():
                scatter().exchange()

            @pl.when(step == N_TILES // 2)
            def _():
                scatter().fold()

        def acc(row, lo, val):
            hi = lo + val.shape[1]
            vec_ref[row:row + 1, lo:hi] += jnp.sum(val, axis=0, keepdims=True)

        xt = x_ref[...]
        yt = y_ref[...]
        shift = mod_ref[layer:layer + 1, 0:D_MODEL]
        scale = mod_ref[layer:layer + 1, D_MODEL:2 * D_MODEL]
        gate = mod_ref[layer:layer + 1, 2 * D_MODEL:]
        ln_gain = lng_ref[layer:layer + 1, :]

        zn, zrstd = _ln(ALPHA * xt + gate * yt)
        if is_last:
            diff = a_ref[...] - b_ref[...]
            acc(VEC_LOSS, 0, diff * diff)
            dout = diff * (1.0 / D_MODEL)
        else:
            dout = a_ref[...]
        acc(VEC_LNG, 0, dout * zn)
        acc(VEC_LNB, 0, dout)
        dz = _ln_bwd(dout * ln_gain, zn, zrstd)
        acc(VEC_GATE, 0, dz * yt)
        dy = (dz * gate).astype(BF16)
        dy_ref[...] = dy

        @pl.when(step == 0)
        def _():
            weight_loads()[0].wait()

        dcat = _dot_nt(dy, wout_ref[...])

        proj = proj_ref[...]
        prev = jnp.where(tile > 0, prev_ref[...], 0.0)
        cat, k = _mix_forward(proj, prev, tile, weights, cdf_ref[...])
        cat_ref[...] = cat.astype(BF16)

        dga, dq = [], []
        for g in range(N_GROUPS):
            sl = slice(g * GROUP, (g + 1) * GROUP)
            pscale = weights.pool_scale(g)
            dya = dcat[:, sl]
            dyp = dya * k["ga_act"][:, sl]
            dga.append(dya * k["pw"][g] * pscale * k["ga_grad"][:, sl])
            acc(VEC_POOL, g * GROUP, dyp * k["pw"][g])
            dpw = (dyp * pscale).astype(BF16)
            rows = pl.ds(base + ROW_WPOOL + g * GROUP, GROUP)
            small_ref[rows, :] += _dot_tn(k["pooled"][g], dpw)
            dq.append(_dot_nt(dpw, weights.pool(g)))
        dpooled = jnp.concatenate(dq, axis=1)
        scaled = jnp.concatenate([dq[g] * k["inv_counts"][g] for g in range(N_GROUPS)], axis=1)
        sums = _window_sums(jnp.concatenate([scaled, halo_ref[...]], axis=0), False)
        halo_ref[...] = scaled[0:HALO]
        dxa = jnp.concatenate(sums, axis=1) - dpooled

        du, dv, dgb = [], [], []
        for h in range(N_HEADS):
            sl = slice(h * HEAD, (h + 1) * HEAD)
            dyb = dcat[:, D_POOL + h * HEAD:D_POOL + (h + 1) * HEAD]
            m_h = k["mixed"][h]
            ug = k["u_act"][:, sl] * dyb
            du.append(dyb * m_h * k["gb_act"][:, sl] * k["u_grad"][:, sl])
            dgb.append(ug * m_h * k["gb_grad"][:, sl])
            dmixed = ug * k["gb_act"][:, sl]
            dmixed_bf = dmixed.astype(BF16)
            w_h = weights.mix(h)
            dvln_parts = []
            dmix_sum = dmix_ref[h]
            wsgu_rows = pl.ds(base + ROW_WSGU + h * CHUNK, CHUNK)
            dws = small_ref[wsgu_rows, :]
            for c in range(TM // CHUNK):
                cs = slice(c * CHUNK, (c + 1) * CHUNK)
                dmix_sum = dmix_sum + dmixed[cs]
                dws = dws + _dot_nt(dmixed_bf[cs], k["vln"][h][cs])
                dvln_parts.append(_dot_tn(w_h, dmixed_bf[cs]))
            dmix_ref[h] = dmix_sum
            small_ref[wsgu_rows, :] = dws
            dvln = jnp.concatenate(dvln_parts, axis=0)
            acc(VEC_SGU, h * HEAD, dvln * k["vn"][h])
            acc(VEC_SGU, D_SGU + h * HEAD, dvln)
            dvv = _ln_bwd(dvln * weights.ln_gain(h), k["vn"][h], k["vrstd"][h])
            dv.append(dvv * k["v_grad"][:, sl])

        dproj = jnp.concatenate([dxa] + dga + du + dv + dgb, axis=1).astype(BF16)
        dproj_ref[...] = dproj

        @pl.when(step == 0)
        def _():
            weight_loads()[1].wait()

        dh = _dot(dproj, wint_ref[...])

        xn, xrstd = _ln(xt)
        h_ref[...] = (xn * (1.0 + scale) + shift).astype(BF16)
        acc(VEC_SCALE, 0, dh * xn)
        acc(VEC_SHIFT, 0, dh)
        dx_ref[...] = _ln_bwd(dh * (1.0 + scale), xn, xrstd) + ALPHA * dz

        @pl.when(step == N_TILES - 1)
        def _():
            def put(row0, vec_row, lo, n):
                for r in range(n):
                    small_ref[base + row0 + r:base + row0 + r + 1, :] = (
                        vec_ref[vec_row:vec_row + 1, lo + r * 128:lo + (r + 1) * 128])

            put(ROW_PSCALE, VEC_POOL, 0, 4)
            put(ROW_SLNG, VEC_SGU, 0, 4)
            put(ROW_SLNB, VEC_SGU, D_SGU, 4)
            put(ROW_LNG, VEC_LNG, 0, 8)
            put(ROW_LNB, VEC_LNB, 0, 8)
            ones = jnp.ones((8, HEAD), F32)
            t = lax.broadcasted_iota(jnp.int32, (CHUNK, CHUNK), 0)
            s = lax.broadcasted_iota(jnp.int32, (CHUNK, CHUNK), 1)
            for h in range(N_HEADS):
                bias_rows = lax.dot_general(ones, dmix_ref[h], (((1,), (1,)), ((), ())),
                                            preferred_element_type=F32, precision=lax.Precision.HIGHEST)
                small_ref[base + ROW_BSGU + h:base + ROW_BSGU + h + 1, :] = bias_rows[0:1]
                rows = pl.ds(base + ROW_WSGU + h * CHUNK, CHUNK)
                small_ref[rows, :] = jnp.where(t >= s, small_ref[rows, :], 0.0)
            pieces = ((0, VEC_SHIFT, 0, 768),
                      (1, VEC_SHIFT, 768, 256), (1, VEC_SCALE, 0, 512),
                      (2, VEC_SCALE, 512, 512), (2, VEC_GATE, 0, 256),
                      (3, VEC_GATE, 256, 768))
            filled = [0] * 4
            for q, vec_row, lo, n in pieces:
                row = 4 * layer + q
                dmod_ref[row:row + 1, filled[q]:filled[q] + n] = vec_ref[vec_row:vec_row + 1, lo:lo + n]
                filled[q] += n
            if n_carry:
                for other in range(layer + 1, DEPTH):
                    rows = pl.ds(other * PACK_ROWS, PACK_ROWS)
                    small_ref[rows, :] = carry_refs[0][rows, :]
                    dmod_ref[4 * other:4 * other + 4, :] = carry_refs[1][4 * other:4 * other + 4, :]
            loss_ref[...] = vec_ref[VEC_LOSS:VEC_LOSS + 1, :]
            if n_red:
                scatter().finish()
                scatter().wait_sends()

    rev = lambda w: pl.BlockSpec((TM, w), lambda i: (N_TILES - 1 - i, 0))
    prev_spec = pl.BlockSpec(
        (HALO, D_POOL), lambda i: (jnp.maximum((N_TILES - 1 - i) * (TM // HALO) - 1, 0), 0))
    comm_scratch = []
    for p in reduce:
        comm_scratch += _ChipReduceScatter.buffers(p.shape[2], p.shape[3], p.dtype)
    if n_red:
        comm_scratch += _ChipReduceScatter.semaphores(n_red)
    return pl.pallas_call(
        body,
        name=name,
        grid=(N_TILES,),
        in_specs=[rev(D_MODEL), rev(D_MODEL) if is_last else pl.BlockSpec((TM, D_MODEL), lambda i: (0, 0)),
                  rev(D_MODEL), rev(D_IN), prev_spec, rev(D_MODEL), rev(2 * D_SGU),
                  _const_spec((DEPTH, 3 * D_MODEL)), ANY, ANY]
                 + [_const_spec(s) for s in SMALL_SPECS] + [_const_spec((DEPTH, D_MODEL))]
                 + [_const_spec(c.shape) for c in carry] + [ANY] * n_red,
        out_specs=[rev(D_MODEL), rev(D_IN), rev(D_MODEL), rev(D_MODEL), rev(D_MODEL),
                   _const_spec((DEPTH * PACK_ROWS, 128)), _const_spec((8, DMOD_COLS)), _const_spec((1, D_MODEL))]
                  + [_const_spec(p.shape[2:]) for p in reduce],
        out_shape=[jax.ShapeDtypeStruct((SEQ, D_MODEL), F32), jax.ShapeDtypeStruct((SEQ, D_IN), BF16),
                   jax.ShapeDtypeStruct((SEQ, D_MODEL), BF16), jax.ShapeDtypeStruct((SEQ, D_MODEL), BF16),
                   jax.ShapeDtypeStruct((SEQ, D_MODEL), BF16), jax.ShapeDtypeStruct((DEPTH * PACK_ROWS, 128), F32),
                   jax.ShapeDtypeStruct((8, DMOD_COLS), F32), jax.ShapeDtypeStruct((1, D_MODEL), F32)]
                  + [jax.ShapeDtypeStruct(p.shape[2:], F32) for p in reduce],
        scratch_shapes=[pltpu.VMEM((8, D_MODEL), F32), pltpu.VMEM((N_HEADS, CHUNK, HEAD), F32),
                        pltpu.VMEM((HALO, D_POOL), F32), pltpu.VMEM((D_IN, D_MODEL), BF16),
                        pltpu.VMEM((D_MODEL, D_MODEL), BF16), pltpu.SemaphoreType.DMA((2,))] + comm_scratch,
        compiler_params=pltpu.CompilerParams(dimension_semantics=("arbitrary",), vmem_limit_bytes=VMEM_LIMIT),
    )(a, b, x, proj, proj, y, cdf, mod, w_int, w_outf, *small, ln_g, *carry, *reduce)


def _grad_matmuls(dproj, h, cat, dy, name):
    in_cols, out_cols = D_IN // 4, D_MODEL // 2
    in_steps = D_IN // in_cols

    def body(dproj_ref, h_ref, cat_ref, dy_ref, gin_ref, gout_ref):
        step = pl.program_id(0)

        @pl.when(step < in_steps)
        def _():
            gin_ref[...] = _dot_tn(dproj_ref[...], h_ref[...]).astype(BF16)

        @pl.when(step >= in_steps)
        def _():
            gout_ref[...] = _dot_tn(cat_ref[...], dy_ref[...]).astype(BF16)

    in_block = lambda j: jnp.minimum(j, in_steps - 1)
    out_block = lambda j: jnp.maximum(j - in_steps, 0)
    return pl.pallas_call(
        body,
        name=name,
        grid=(in_steps + D_MODEL // out_cols,),
        in_specs=[pl.BlockSpec((SEQ, in_cols), lambda j: (0, in_block(j))), _const_spec((SEQ, D_MODEL)),
                  pl.BlockSpec((SEQ, out_cols), lambda j: (0, out_block(j))), _const_spec((SEQ, D_MODEL))],
        out_specs=[pl.BlockSpec((in_cols, D_MODEL), lambda j: (in_block(j), 0)),
                   pl.BlockSpec((out_cols, D_MODEL), lambda j: (out_block(j), 0))],
        out_shape=[jax.ShapeDtypeStruct((D_IN, D_MODEL), BF16), jax.ShapeDtypeStruct((D_MODEL, D_MODEL), BF16)],
        compiler_params=pltpu.CompilerParams(dimension_semantics=("arbitrary",), vmem_limit_bytes=VMEM_LIMIT),
    )(dproj, h, cat, dy)


def _adamw_math(w, g, m, v):
    m = ADAM_B1 * m + (1.0 - ADAM_B1) * g
    v = ADAM_B2 * v + (1.0 - ADAM_B2) * (g * g)
    m_hat = m / (1.0 - ADAM_B1 ** ADAM_STEP)
    v_hat = v / (1.0 - ADAM_B2 ** ADAM_STEP)
    delta = -ADAM_LR * (m_hat / (jnp.sqrt(v_hat) + ADAM_EPS) + ADAM_WD * w)
    return delta, m, v


def _adamw(w, grads, m, v, block_rows, name):
    rows, cols = grads[0].shape
    blocks = rows // block_rows

    def body(w_ref, m_ref, v_ref, *rest):
        g_refs, (g_ref, d_ref, nm_ref, nv_ref) = rest[:DEPTH], rest[DEPTH:]
        for layer in range(DEPTH):
            @pl.when(pl.program_id(0) == layer)
            def _():
                g = g_refs[layer][...]
                g_ref[...] = g
                d_ref[...], nm_ref[...], nv_ref[...] = _adamw_math(w_ref[...], g, m_ref[...], v_ref[...])

    def grad_spec(layer):
        return pl.BlockSpec((block_rows, cols),
                            lambda l, i: (jnp.where(l == layer, i, jnp.where(l < layer, 0, blocks - 1)), 0))

    spec = pl.BlockSpec((block_rows, cols), lambda l, i: (l * blocks + i, 0))
    return pl.pallas_call(
        body,
        name=name,
        grid=(DEPTH, blocks),
        in_specs=[spec] * 3 + [grad_spec(layer) for layer in range(DEPTH)],
        out_specs=[spec] * 4,
        out_shape=[jax.ShapeDtypeStruct(w.shape, F32)] * 4,
        compiler_params=pltpu.CompilerParams(dimension_semantics=("arbitrary", "arbitrary"),
                                             vmem_limit_bytes=VMEM_LIMIT),
    )(w, m, v, *grads)


MESH = pl.DeviceIdType.MESH
SIBLING = 1
ANY = pl.BlockSpec(memory_space=pl.ANY)
VMEM = pl.BlockSpec(memory_space=pltpu.VMEM)


def _me():
    return lax.axis_index("x"), lax.axis_index("y"), lax.axis_index("c")


def _peer(r):
    x, y, c = _me()
    return (1 - x if r & 4 else x, 1 - y if r & 2 else y, 1 - c if r & 1 else c)


def _index(dev):
    return 4 * dev[0] + 2 * dev[1] + dev[2]


def _remote(src, dst, send_sem, recv_sem, dev):
    return pltpu.make_async_remote_copy(src_ref=src, dst_ref=dst, send_sem=send_sem, recv_sem=recv_sem,
                                        device_id=dev, device_id_type=MESH)


ACROSS_X, ACROSS_Y, ACROSS_BOTH = 4, 2, 6
GATHER_SEMS = 11


class _TwoLevelGather:
    def __init__(self, out, send_sems, recv_sems, src=None):
        self.out, self.send_sems, self.recv_sems, self.src = out, send_sems, recv_sems, src
        self.rows = (out.shape[0] // N_DEV) if len(out.shape) == 2 else out.shape[1]
        self.half = self.rows // 2

    def _slot(self, block):
        if len(self.out.shape) == 2:
            return self.out.at[pl.ds(pl.multiple_of(_index(block) * self.rows, self.rows), self.rows)]
        return self.out.at[_index(block)]

    def _copy(self, k, block, part, to, src=None):
        slot = self._slot(block)
        if part is not None:
            rows = pl.ds(part * self.half, self.half)
            slot = slot.at[rows]
            src = None if src is None else src.at[rows]
        return _remote(slot if src is None else src, slot, self.send_sems.at[k], self.recv_sems.at[k], to)

    def _mine(self):
        me = _me()
        src = self._slot(me) if self.src is None else self.src
        x, y = _peer(ACROSS_X), _peer(ACROSS_Y)
        return [self._copy(1, me, 0, x, src), self._copy(3, me, 1, y, src), self._copy(0, me, None, _peer(SIBLING), src),
                self._copy(2, me, 1, x, src), self._copy(4, me, 0, y, src)]

    def _relayed(self):
        return [self._copy(5, _peer(ACROSS_X), 0, _peer(ACROSS_Y)), self._copy(6, _peer(ACROSS_Y), 1, _peer(ACROSS_X))]

    def _passed(self):
        sib, far = _peer(SIBLING), _peer(ACROSS_BOTH)
        return [self._copy(7, _peer(ACROSS_X), None, sib), self._copy(8, _peer(ACROSS_Y), None, sib),
                self._copy(9, far, 0, sib), self._copy(10, far, 1, sib)]

    def _arrival(self, k, r, part):
        return self._copy(k, _peer(r), part, _me())

    def send_first(self):
        for cp in self._mine()[:3]:
            cp.start()

    def send_second(self):
        for cp in self._mine()[3:]:
            cp.start()

    def send_mine(self):
        self.send_first()
        self.send_second()

    def relay(self):
        relayed = self._relayed()
        self._arrival(1, ACROSS_X, 0).wait_recv()
        relayed[0].start()
        self._arrival(3, ACROSS_Y, 1).wait_recv()
        relayed[1].start()

    def pass_near(self):
        passed = self._passed()
        self._arrival(2, ACROSS_X, 1).wait_recv()
        passed[0].start()
        self._arrival(4, ACROSS_Y, 0).wait_recv()
        passed[1].start()

    def pass_far(self):
        passed = self._passed()
        self._arrival(5, ACROSS_BOTH, 0).wait_recv()
        passed[2].start()
        self._arrival(6, ACROSS_BOTH, 1).wait_recv()
        passed[3].start()

    def pass_on(self):
        self.pass_near()
        self.pass_far()

    def wait_sibling(self):
        self._arrival(0, SIBLING, None).wait_recv()

    def wait_passed(self, r):
        if r == ACROSS_BOTH:
            self._arrival(9, r ^ SIBLING, 0).wait_recv()
            self._arrival(10, r ^ SIBLING, 1).wait_recv()
        else:
            self._arrival(7 if r == ACROSS_X else 8, r ^ SIBLING, None).wait_recv()

    def wait_rest(self):
        self.wait_sibling()
        for r in (ACROSS_X, ACROSS_Y, ACROSS_BOTH):
            self.wait_passed(r)

    def wait_sends(self):
        for cp in self._mine() + self._relayed() + self._passed():
            cp.wait_send()


class _ChipReduceScatter:
    SLOTS = 6

    def __init__(self, arrays, l_sem, d_send, d_recv, i_send, i_recv):
        self.arrays = arrays
        self.l_sem, self.d_send, self.d_recv, self.i_send, self.i_recv = l_sem, d_send, d_recv, i_send, i_recv

    @staticmethod
    def buffers(rows, cols, dtype, staged=True):
        stage = [pltpu.VMEM((4, rows, cols), dtype)] if staged else []
        return stage + [pltpu.VMEM((4, rows, cols), dtype), pltpu.VMEM((3, rows, cols), dtype),
                        pltpu.VMEM((2, rows, cols), dtype), pltpu.VMEM((2, rows // 2, cols), dtype)]

    @classmethod
    def semaphores(cls, n):
        return [pltpu.SemaphoreType.DMA((n,)), pltpu.SemaphoreType.DMA((n, 4)), pltpu.SemaphoreType.DMA((n, 4)),
                pltpu.SemaphoreType.DMA((n, cls.SLOTS)), pltpu.SemaphoreType.DMA((n, cls.SLOTS))]

    def _pick(self, which):
        return list(enumerate(self.arrays)) if which is None else [(n, self.arrays[n]) for n in which]

    @staticmethod
    def _chip(r):
        dev = _me() if r is None else _peer(r)
        return 2 * dev[0] + dev[1]

    def _staging(self, which):
        c = _me()[2]
        return [pltpu.make_async_copy(a["part"].at[pl.ds(0, 4), c], a["stage"], self.l_sem.at[n])
                for n, a in self._pick(which) if a["staged"]]

    def _first(self, which, chip):
        other = 1 - _me()[2]
        return [_remote(a["part"].at[chip, other], a["sib"].at[chip], self.d_send.at[n, chip], self.d_recv.at[n, chip],
                        _peer(SIBLING)) for n, a in self._pick(which)]

    @staticmethod
    def _halves(a):
        half = a["rcv"].shape[1] // 2
        return pl.ds(0, half), pl.ds(half, half)

    def _hops(self, n, a):
        h0, h1 = self._halves(a)
        x, y = _peer(ACROSS_X), _peer(ACROSS_Y)
        snd, rcv, relay = a["snd"], a["rcv"], a["relay"]
        pairs = [(snd.at[2, h0], relay.at[0], x), (snd.at[2, h1], relay.at[1], y),
                 (snd.at[0, h0], rcv.at[0, h0], x), (snd.at[0, h1], rcv.at[0, h1], x),
                 (snd.at[1, h1], rcv.at[1, h1], y), (snd.at[1, h0], rcv.at[1, h0], y)]
        return [_remote(s, d, self.i_send.at[n, k], self.i_recv.at[n, k], to) for k, (s, d, to) in enumerate(pairs)]

    def _mine(self, a, chip, rows=None):
        src = a["stage"].at[chip] if a["staged"] else a["part"].at[chip, _me()[2]]
        mine, sib = (src[...], a["sib"][chip]) if rows is None else (src[rows, :], a["sib"][chip, rows, :])
        return mine.astype(F32) + sib.astype(F32)

    def start(self, which=None, chips=None):
        if chips is None:
            for cp in self._staging(which):
                cp.start()
        for chip in range(4) if chips is None else chips:
            for cp in self._first(which, chip):
                cp.start()

    def send_far(self, which=None):
        far = self._chip(ACROSS_BOTH)
        for cp in self._staging(which):
            cp.wait()
        for cp in self._first(which, far):
            cp.wait_recv()
        for n, a in self._pick(which):
            hops = self._hops(n, a)
            a["snd"][2] = self._mine(a, far).astype(a["snd"].dtype)
            hops[0].start()
            hops[1].start()

    def send_near(self, r, which=None):
        chip = self._chip(r)
        for cp in self._first(which, chip):
            cp.wait_recv()
        for n, a in self._pick(which):
            h0, h1 = self._halves(a)
            hops = self._hops(n, a)
            if r == ACROSS_X:
                a["snd"][0, h0, :] = self._mine(a, chip, h0).astype(a["snd"].dtype)
                hops[2].start()
            else:
                a["snd"][1, h1, :] = self._mine(a, chip, h1).astype(a["snd"].dtype)
                hops[4].start()

    def exchange(self, which=None):
        self.send_far(which)
        self.send_near(ACROSS_X, which)
        self.send_near(ACROSS_Y, which)

    def fold(self, which=None):
        across_x, across_y = self._chip(ACROSS_X), self._chip(ACROSS_Y)
        for n, a in self._pick(which):
            h0, h1 = self._halves(a)
            hops = self._hops(n, a)
            dtype = a["snd"].dtype
            hops[1].wait_recv()
            a["snd"][0, h1, :] = (self._mine(a, across_x, h1) + a["relay"][1].astype(F32)).astype(dtype)
            hops[3].start()
            hops[0].wait_recv()
            a["snd"][1, h0, :] = (self._mine(a, across_y, h0) + a["relay"][0].astype(F32)).astype(dtype)
            hops[5].start()

    def finish(self, which=None):
        home = self._chip(None)
        for cp in self._first(which, home):
            cp.wait_recv()
        for n, a in self._pick(which):
            hops = self._hops(n, a)
            a["out"][...] = self._mine(a, home)
            hops[2].wait_recv()
            hops[3].wait_recv()
            a["out"][...] += a["rcv"][0].astype(F32)
            hops[4].wait_recv()
            hops[5].wait_recv()
            a["out"][...] += a["rcv"][1].astype(F32)

    def wait_sends(self, which=None):
        for chip in range(4):
            for cp in self._first(which, chip):
                cp.wait_send()
        for n, a in self._pick(which):
            for cp in self._hops(n, a):
                cp.wait_send()


def _direct_exchange(src_of, dst_of, send_sems, recv_sems):
    me = _me()
    copies = [_remote(src_of(_peer(r)), dst_of(me), send_sems.at[r - 1], recv_sems.at[r - 1], _peer(r))
              for r in range(1, N_DEV)]
    for cp in copies:
        cp.start()
    return copies


def _wait_direct(copies):
    for cp in copies:
        cp.wait_recv()
    for cp in copies:
        cp.wait_send()


def _gathered_layer(layer, x, small, ln_g, ln_b, mine, name, mod=None, exchange=None, host=()):
    exchanging = exchange is not None
    cols = exchange[1].shape[2] if exchanging else 0
    shard = mine[0].shape[0]
    pair = 2 * shard
    n_lead = 4 if exchanging else 2
    n_host = len(host)

    def body(*refs):
        x_hbm = refs[0]
        lead, refs = refs[1:n_lead], refs[n_lead:]
        (wpool_ref, pscale_ref, slng_ref, slnb_ref, wsgu_ref, bsgut_ref, lng_ref, lnb_ref, wint_hbm,
         wout_hbm), refs = refs[:10], refs[10:]
        host_in, refs = refs[:n_host], refs[n_host:]
        (out_ref, y_ref, cdf_ref, proj_hbm, wint_keep, wout_keep), refs = refs[:6], refs[6:]
        if exchanging:
            c_ref, wada_hbm, bada_ref = lead
            (acts_ref, mod_ref), refs = refs[:2], refs[2:]
        else:
            mod_ref, = lead
        host_out, refs = refs[:n_host], refs[n_host:]
        (wint_v, wout_v, h_buf, proj_blk, proj_tile, halo_ref, x_ref, w_send, w_recv, w_local, p_sems,
         t_sems, in_sems), refs = refs[:13], refs[13:]
        if n_host:
            refs, (n_send, n_recv, n_local) = refs[:-3], refs[-3:]
        weights = _MixWeights(layer, wpool_ref, pscale_ref, slng_ref, slnb_ref, wsgu_ref, bsgut_ref)
        tile = pl.program_id(0)

        def hosted():
            return [_TwoLevelGather(host_out[n], n_send.at[n], n_recv.at[n], src=host_in[n]) for n in range(n_host)]

        def hosted_own():
            index = _index(_me())
            return [pltpu.make_async_copy(host_in[n], host_out[n].at[index], n_local.at[n]) for n in range(n_host)]

        def gathers():
            return (_TwoLevelGather(wint_v, w_send.at[0], w_recv.at[0], src=wint_hbm),
                    _TwoLevelGather(wout_v, w_send.at[1], w_recv.at[1], src=wout_hbm))

        def keeps():
            return [pltpu.make_async_copy(wint_v, wint_keep, w_local.at[2]),
                    pltpu.make_async_copy(wout_v, wout_keep, w_local.at[3])]

        def tile_read(t):
            slot = t % 2
            return pltpu.make_async_copy(proj_hbm.at[pl.ds(pl.multiple_of(t * TM, TM), TM)], proj_tile.at[slot],
                                         t_sems.at[slot])

        @pl.when(tile == 0)
        def _():
            me = _me()
            halo_ref[...] = jnp.zeros_like(halo_ref)
            gather_in, gather_out = gathers()
            own_in = pltpu.make_async_copy(wint_hbm, gather_in._slot(me), w_local.at[0])
            own_out = pltpu.make_async_copy(wout_hbm, gather_out._slot(me), w_local.at[1])
            x_load = pltpu.make_async_copy(x_hbm, x_ref, in_sems.at[0])
            x_load.start()

            if exchanging:
                act_all, act_src, part, mod_recv, wada_ref, a_send, a_recv, m_send, m_recv = refs
                wada_load = pltpu.make_async_copy(wada_hbm, wada_ref, in_sems.at[1])
                wada_load.start()
                mine_index = _index(me)
                cval = c_ref[...]
                act_src[...] = jnp.zeros_like(act_src)
                act_src[0:1, :] = cval * jax.nn.sigmoid(cval)
                act_all[mine_index] = act_src[...]
                act_copies = _direct_exchange(lambda p: act_src, lambda m: act_all.at[_index(m)], a_send, a_recv)

            own_in.start()
            own_out.start()
            gather_in.send_first()

            if exchanging:
                _wait_direct(act_copies)
                acts = jnp.concatenate([act_all[j, 0:1, :] for j in range(N_DEV)], axis=0)
                acts_ref[...] = acts
                part[...] = jnp.zeros_like(part)
                wada_load.wait()
                for l in range(DEPTH):
                    res = lax.dot_general(acts, wada_ref[l], (((1,), (0,)), ((), ())), preferred_element_type=F32,
                                          precision=lax.Precision.HIGHEST)
                    for b in range(N_DEV):
                        part[b, l:l + 1, :] = res[b:b + 1, :]
                mod_recv[mine_index] = part[mine_index]
                mod_copies = _direct_exchange(lambda p: part.at[_index(p)], lambda m: mod_recv.at[_index(m)],
                                              m_send, m_recv)
            gather_in.send_second()
            gather_out.send_mine()

            if exchanging:
                _wait_direct(mod_copies)
                for l in range(DEPTH):
                    for j in range(N_DEV):
                        sl = slice(j * cols, (j + 1) * cols)
                        mod_ref[l:l + 1, sl] = mod_recv[j, l:l + 1, :] + bada_ref[l:l + 1, sl]
            shift = mod_ref[layer:layer + 1, 0:D_MODEL]
            scale = mod_ref[layer:layer + 1, D_MODEL:2 * D_MODEL]
            x_load.wait()
            for t in range(N_TILES):
                rows = pl.ds(t * TM, TM)
                xn, _ = _ln(x_ref[rows, :])
                h_buf[rows, :] = (xn * (1.0 + scale) + shift).astype(BF16)

            chip_of = lambda dev: 2 * dev[0] + dev[1]
            writes = []

            def project(n, dev):
                first = pl.multiple_of(chip_of(dev) * pair, pair)
                if n >= 2:
                    writes[n - 2].wait()
                proj_blk[n % 2] = _dot_nt(h_buf[...], wint_v[pl.ds(first, pair), :])
                cp = pltpu.make_async_copy(proj_blk.at[n % 2], proj_hbm.at[:, pl.ds(first, pair)], p_sems.at[n % 2])
                cp.start()
                writes.append(cp)

            gather_in.relay()
            own_in.wait()
            gather_in.wait_sibling()
            project(0, me)
            gather_in.pass_near()
            gather_in.wait_passed(ACROSS_X)
            project(1, _peer(ACROSS_X))
            gather_out.relay()
            for cp in hosted_own():
                cp.start()
            for g in hosted():
                g.send_mine()
            gather_in.wait_passed(ACROSS_Y)
            project(2, _peer(ACROSS_Y))
            gather_in.pass_far()
            gather_in.wait_passed(ACROSS_BOTH)
            project(3, _peer(ACROSS_BOTH))

            gather_out.pass_on()
            gather_out.wait_rest()
            own_out.wait()
            for cp in keeps():
                cp.start()
            writes[2].wait()
            writes[3].wait()
            tile_read(0).start()

        @pl.when(tile + 1 < N_TILES)
        def _():
            tile_read(tile + 1).start()

        if n_host:
            @pl.when(tile == 1)
            def _():
                for g in hosted():
                    g.relay()

            @pl.when(tile == N_TILES // 2)
            def _():
                for g in hosted():
                    g.pass_near()

        tile_read(tile).wait()
        xt = x_ref[pl.ds(pl.multiple_of(tile * TM, TM), TM), :]
        gate = mod_ref[layer:layer + 1, 2 * D_MODEL:]
        proj = proj_tile[tile % 2]
        cat, cdf_ref[...] = _mix_forward(proj, halo_ref[...], tile, weights)
        halo_ref[...] = proj[TM - HALO:, 0:D_POOL]
        y = _dot(cat.astype(BF16), wout_v[...])
        y_ref[...] = y
        zn, _ = _ln(ALPHA * xt + gate * y)
        out_ref[...] = zn * lng_ref[layer:layer + 1, :] + lnb_ref[layer:layer + 1, :]

        @pl.when(tile == N_TILES - 1)
        def _():
            for g in hosted():
                g.pass_far()
            for g in hosted():
                g.wait_rest()
            for g in list(gathers()) + hosted():
                g.wait_sends()
            for cp in keeps() + hosted_own():
                cp.wait()

    row = lambda w: pl.BlockSpec((TM, w), lambda i: (i, 0))
    gather_sems = pltpu.SemaphoreType.DMA((2, GATHER_SEMS))
    seven = pltpu.SemaphoreType.DMA((7,))
    lead_in = list(exchange) if exchanging else [mod]
    lead_specs = [_const_spec(a.shape) for a in lead_in]
    if exchanging:
        lead_specs[1] = ANY
    extra_out_specs = [_const_spec((N_DEV, D_MODEL)), _const_spec((DEPTH, 3 * D_MODEL))] if exchanging else []
    extra_out_shape = [jax.ShapeDtypeStruct((N_DEV, D_MODEL), F32),
                       jax.ShapeDtypeStruct((DEPTH, 3 * D_MODEL), F32)] if exchanging else []
    extra_scratch = [pltpu.VMEM((N_DEV, 8, D_MODEL), F32), pltpu.VMEM((8, D_MODEL), F32),
                     pltpu.VMEM((N_DEV, 8, cols), F32), pltpu.VMEM((N_DEV, 8, cols), F32),
                     pltpu.VMEM(exchange[1].shape, F32), seven, seven, seven, seven] if exchanging else []
    host_scratch = [pltpu.SemaphoreType.DMA((n_host, GATHER_SEMS)), pltpu.SemaphoreType.DMA((n_host, GATHER_SEMS)),
                    pltpu.SemaphoreType.DMA((n_host,))] if n_host else []
    return pl.pallas_call(
        body,
        name=name,
        grid=(N_TILES,),
        in_specs=[ANY] + lead_specs + [_const_spec(s) for s in SMALL_SPECS]
                 + [_const_spec((DEPTH, D_MODEL)), _const_spec((DEPTH, D_MODEL)), ANY, ANY] + [ANY] * n_host,
        out_specs=[row(D_MODEL), row(D_MODEL), row(2 * D_SGU), ANY, ANY, ANY] + extra_out_specs + [ANY] * n_host,
        out_shape=[jax.ShapeDtypeStruct((SEQ, D_MODEL), F32), jax.ShapeDtypeStruct((SEQ, D_MODEL), F32),
                   jax.ShapeDtypeStruct((SEQ, 2 * D_SGU), F32), jax.ShapeDtypeStruct((SEQ, D_IN), F32),
                   jax.ShapeDtypeStruct((D_IN, D_MODEL), BF16), jax.ShapeDtypeStruct((D_MODEL, D_MODEL), BF16)]
                  + extra_out_shape + [jax.ShapeDtypeStruct((N_DEV,) + blk.shape, blk.dtype) for blk in host],
        scratch_shapes=[pltpu.VMEM((D_IN, D_MODEL), BF16), pltpu.VMEM((D_MODEL, D_MODEL), BF16),
                        pltpu.VMEM((SEQ, D_MODEL), BF16), pltpu.VMEM((2, SEQ, pair), F32),
                        pltpu.VMEM((2, TM, D_IN), F32), pltpu.VMEM((HALO, D_POOL), F32),
                        pltpu.VMEM((SEQ, D_MODEL), F32),
                        gather_sems, gather_sems, pltpu.SemaphoreType.DMA((4,)), pltpu.SemaphoreType.DMA((2,)),
                        pltpu.SemaphoreType.DMA((2,)), pltpu.SemaphoreType.DMA((2,))] + extra_scratch + host_scratch,
        compiler_params=pltpu.CompilerParams(dimension_semantics=("arbitrary",), vmem_limit_bytes=VMEM_LIMIT),
    )(x, *lead_in, *small, ln_g, ln_b, *mine, *host)


ADA_CHUNK = 256


def _grad_tail(dproj, h, cat, dy, small, dmod8, loss_lanes, w_ada, m_ada, v_ada, act_t, b_ada, m_bada, v_bada):
    shard_in, shard_out, shard_small = D_IN // N_DEV, D_MODEL // N_DEV, small.shape[2]
    cols = w_ada.shape[2]
    W_IN, W_OUT, SMALL = 0, 1, 2

    def body(dproj_hbm, h_hbm, cat_hbm, dy_hbm, small_hbm, dmod_ref, lanes_ref, wada_hbm, mada_hbm, vada_hbm,
             act_ref, bada_ref, mbada_ref, vbada_ref,
             gwin_ref, gwout_ref, stot_ref, loss_ref, gada_hbm, dada_hbm, nmada_hbm, nvada_hbm,
             gb_ref, db_ref, nmb_ref, nvb_ref,
             dproj_v, h_v, cat_v, dy_v, part_in, part_out, own_small, loss_src, loss_all, dmod_all, ada_in, ada_out,
             *rest):
        bufs, rest = rest[:13], rest[13:]
        load_sems, rs_sems = rest[0], rest[1:6]
        m_send, m_recv, g_send, g_recv, s_send, s_recv, ada_lsem, ada_ssem = rest[6:]
        mine = _index(_me())

        def update_ada():
            upper = (mine % 2) == 1

            def dmod_of(layer):
                rows = []
                for b in range(N_DEV):
                    r = dmod_all[b, pl.ds(4 * layer + mine // 2, 1), :]
                    rows.append(jnp.where(upper, r[:, cols:], r[:, :cols]))
                return jnp.concatenate(rows, axis=0)

            chunks = [(layer, c) for layer in range(DEPTH) for c in range(D_MODEL // ADA_CHUNK)]

            def loads(i):
                layer, c = chunks[i]
                rows = pl.ds(c * ADA_CHUNK, ADA_CHUNK)
                return [pltpu.make_async_copy(src.at[layer, rows], ada_in.at[i % 2, k], ada_lsem.at[i % 2, k])
                        for k, src in enumerate((wada_hbm, mada_hbm, vada_hbm))]

            def stores(i):
                layer, c = chunks[i]
                rows = pl.ds(c * ADA_CHUNK, ADA_CHUNK)
                return [pltpu.make_async_copy(ada_out.at[i % 2, k], dst.at[layer, rows], ada_ssem.at[i % 2, k])
                        for k, dst in enumerate((gada_hbm, dada_hbm, nmada_hbm, nvada_hbm))]

            for cp in loads(0):
                cp.start()
            dmods = {}
            for i, (layer, c) in enumerate(chunks):
                if i + 1 < len(chunks):
                    for cp in loads(i + 1):
                        cp.start()
                for cp in loads(i):
                    cp.wait()
                if i >= 2:
                    for cp in stores(i - 2):
                        cp.wait()
                if layer not in dmods:
                    dmods[layer] = dmod_of(layer)
                act = act_ref[pl.ds(c * ADA_CHUNK, ADA_CHUNK), :]
                g = act[:, 0:1] * dmods[layer][0:1, :]
                for b in range(1, N_DEV):
                    g = g + act[:, b:b + 1] * dmods[layer][b:b + 1, :]
                slot = i % 2
                delta, new_m, new_v = _adamw_math(ada_in[slot, 0], g, ada_in[slot, 1], ada_in[slot, 2])
                ada_out[slot, 0] = g
                ada_out[slot, 1] = delta
                ada_out[slot, 2] = new_m
                ada_out[slot, 3] = new_v
                for cp in stores(i):
                    cp.start()
            for i in (len(chunks) - 2, len(chunks) - 1):
                for cp in stores(i):
                    cp.wait()

            total = dmod_all[0]
            for b in range(1, N_DEV):
                total = total + dmod_all[b]
            width = total.shape[1]
            for layer in range(DEPTH):
                for q in range(4):
                    gb_ref[layer:layer + 1, q * width:(q + 1) * width] = total[4 * layer + q:4 * layer + q + 1, :]
            db_ref[...], nmb_ref[...], nvb_ref[...] = _adamw_math(bada_ref[...], gb_ref[...], mbada_ref[...],
                                                                  vbada_ref[...])

        order = (ACROSS_BOTH, ACROSS_X, ACROSS_Y, None)
        chips = [_ChipReduceScatter._chip(r) for r in order]
        loads = [pltpu.make_async_copy(s, d, load_sems.at[n]) for n, (s, d) in enumerate(
            ((cat_hbm, cat_v), (dy_hbm, dy_v), (h_hbm, h_v)))]
        loads += [pltpu.make_async_copy(dproj_hbm.at[:, pl.ds(pl.multiple_of(chip * 2 * shard_in, 2 * shard_in),
                                                             2 * shard_in)], dproj_v.at[n], load_sems.at[3 + n])
                  for n, chip in enumerate(chips)]
        for cp in loads:
            cp.start()
        arrays = [dict(part=part_in, out=gwin_ref, staged=False, sib=bufs[0], snd=bufs[1], rcv=bufs[2], relay=bufs[3]),
                  dict(part=part_out, out=gwout_ref, staged=False, sib=bufs[4], snd=bufs[5], rcv=bufs[6],
                       relay=bufs[7]),
                  dict(part=small_hbm, out=own_small, staged=True, stage=bufs[8], sib=bufs[9], snd=bufs[10],
                       rcv=bufs[11], relay=bufs[12])]
        scatter = _ChipReduceScatter(arrays, *rs_sems)
        scatter.start([SMALL])
        dmod_all[mine] = dmod_ref[...]
        dmod_copies = _direct_exchange(lambda p: dmod_ref, lambda m: dmod_all.at[_index(m)], m_send, m_recv)
        loss_src[...] = jnp.full(loss_src.shape, (0.5 / D_MODEL) * jnp.sum(lanes_ref[...]), F32)
        loss_all[mine] = loss_src[...]
        loss_copies = _direct_exchange(lambda p: loss_src, lambda m: loss_all.at[_index(m)], s_send, s_recv)

        loads[0].wait()
        loads[1].wait()
        for blk in range(2):
            res = _dot_tn(cat_v[:, blk * 512:(blk + 1) * 512], dy_v[...]).astype(BF16)
            for s in range(4):
                part_out[2 * blk + s // 2, s % 2] = res[s * shard_out:(s + 1) * shard_out]
        scatter.start([W_OUT])
        scatter.exchange([SMALL])

        gather = _TwoLevelGather(stot_ref, g_send, g_recv)
        loads[2].wait()
        for n, chip in enumerate(chips):
            loads[3 + n].wait()
            res = _dot_tn(dproj_v[n], h_v[...]).astype(BF16)
            part_in[chip, 0] = res[:shard_in]
            part_in[chip, 1] = res[shard_in:]
            scatter.start([W_IN], chips=[chip])
            if n == 0:
                scatter.exchange([W_OUT])
                scatter.fold([SMALL])
            if n == 1:
                scatter.send_far([W_IN])
                scatter.fold([W_OUT])
                scatter.finish([SMALL])
                stot_ref[mine] = own_small[...]
                gather.send_mine()
            if n == 2:
                scatter.send_near(ACROSS_X, [W_IN])
                gather.relay()
            if n == 3:
                scatter.send_near(ACROSS_Y, [W_IN])
        scatter.fold([W_IN])
        scatter.finish([W_OUT])
        _wait_direct(dmod_copies)
        update_ada()
        gather.pass_on()
        gather.wait_rest()
        _wait_direct(loss_copies)
        total = loss_all[0]
        for j in range(1, N_DEV):
            total = total + loss_all[j]
        loss_ref[...] = total
        scatter.finish([W_IN])
        gather.wait_sends()
        scatter.wait_sends()

    buffers = _ChipReduceScatter.buffers
    comm_scratch = (buffers(shard_in, D_MODEL, BF16, staged=False) + buffers(shard_out, D_MODEL, BF16, staged=False)
                    + buffers(shard_small, 128, F32))
    comm_scratch += [pltpu.SemaphoreType.DMA((7,))] + _ChipReduceScatter.semaphores(3)
    comm_scratch += [pltpu.SemaphoreType.DMA((n,)) for n in (7, 7, GATHER_SEMS, GATHER_SEMS, 7, 7)]
    comm_scratch += [pltpu.SemaphoreType.DMA((2, 3)), pltpu.SemaphoreType.DMA((2, 4))]
    return pl.pallas_call(
        body,
        name="grad_tail",
        in_specs=[ANY] * 5 + [VMEM, VMEM] + [ANY] * 3 + [VMEM] * 4,
        out_specs=[VMEM] * 4 + [ANY] * 4 + [VMEM] * 4,
        out_shape=[jax.ShapeDtypeStruct((shard_in, D_MODEL), F32), jax.ShapeDtypeStruct((shard_out, D_MODEL), F32),
                   jax.ShapeDtypeStruct((N_DEV, shard_small, 128), F32), jax.ShapeDtypeStruct((8, 128), F32)]
                  + [jax.ShapeDtypeStruct(w_ada.shape, F32)] * 4 + [jax.ShapeDtypeStruct(b_ada.shape, F32)] * 4,
        scratch_shapes=[pltpu.VMEM((4, SEQ, 2 * shard_in), BF16), pltpu.VMEM(h.shape, BF16), pltpu.VMEM(cat.shape, BF16),
                        pltpu.VMEM(dy.shape, BF16), pltpu.VMEM((4, 2, shard_in, D_MODEL), BF16),
                        pltpu.VMEM((4, 2, shard_out, D_MODEL), BF16), pltpu.VMEM((shard_small, 128), F32),
                        pltpu.VMEM((8, 128), F32), pltpu.VMEM((N_DEV, 8, 128), F32),
                        pltpu.VMEM((N_DEV,) + dmod8.shape, F32), pltpu.VMEM((2, 3, ADA_CHUNK, cols), F32),
                        pltpu.VMEM((2, 4, ADA_CHUNK, cols), F32)] + comm_scratch,
        compiler_params=pltpu.CompilerParams(vmem_limit_bytes=VMEM_LIMIT),
    )(dproj, h, cat, dy, small, dmod8, loss_lanes, w_ada, m_ada, v_ada, act_t, b_ada, m_bada, v_bada)


SMALL_NAMES = ("w_pool", "w_sgu", "pool_scale", "sgu_ln_g", "sgu_ln_b", "b_sgu", "ln_g", "ln_b")
SMALL_ROWS = (512, 512, 4, 4, 4, 4, 8, 8)


def _adamw_small(g_packed, ws, ms, vs, name):
    n = len(SMALL_NAMES)

    def body(g_ref, *refs):
        w_refs, m_refs, v_refs = refs[:n], refs[n:2 * n], refs[2 * n:3 * n]
        outs = refs[3 * n:]

        def update(p, at, g):
            delta, new_m, new_v = _adamw_math(w_refs[p][at], g, m_refs[p][at], v_refs[p][at])
            outs[p][at] = g
            outs[n + p][at] = delta
            outs[2 * n + p][at] = new_m
            outs[3 * n + p][at] = new_v

        row = 0
        for p, r in enumerate(SMALL_ROWS):
            shape = ws[p].shape
            for layer in range(DEPTH):
                first = layer * PACK_ROWS + row
                if len(shape) == 4:
                    for k in range(shape[1]):
                        update(p, (layer, k), g_ref[first + k * shape[2]:first + (k + 1) * shape[2], :])
                elif len(shape) == 3:
                    update(p, (layer,), g_ref[first:first + r, :])
                else:
                    g = jnp.concatenate([g_ref[first + k:first + k + 1, :] for k in range(r)], axis=1)
                    update(p, (slice(layer, layer + 1), slice(None)), g)
            row += r

    res = pl.pallas_call(
        body,
        name=name,
        out_shape=[jax.ShapeDtypeStruct(w.shape, F32) for w in ws] * 4,
        compiler_params=pltpu.CompilerParams(vmem_limit_bytes=VMEM_LIMIT),
    )(g_packed, *ws, *ms, *vs)
    return res[:n], res[n:2 * n], res[2 * n:3 * n], res[3 * n:]


def kernel(x, c, w_ada, b_ada, w_in, w_pool, pool_scale, sgu_ln_g, sgu_ln_b, w_sgu, b_sgu, w_out, ln_g, ln_b, loss_target, m_w_ada, m_b_ada, m_w_in, m_w_pool, m_pool_scale, m_sgu_ln_g, m_sgu_ln_b, m_w_sgu, m_b_sgu, m_w_out, m_ln_g, m_ln_b, v_w_ada, v_b_ada, v_w_in, v_w_pool, v_pool_scale, v_sgu_ln_g, v_sgu_ln_b, v_w_sgu, v_b_sgu, v_w_out, v_ln_g, v_ln_b):
    small_w = dict(w_pool=w_pool, w_sgu=w_sgu, pool_scale=pool_scale, sgu_ln_g=sgu_ln_g, sgu_ln_b=sgu_ln_b,
                   b_sgu=b_sgu, ln_g=ln_g, ln_b=ln_b)
    small_m = dict(w_pool=m_w_pool, w_sgu=m_w_sgu, pool_scale=m_pool_scale, sgu_ln_g=m_sgu_ln_g,
                   sgu_ln_b=m_sgu_ln_b, b_sgu=m_b_sgu, ln_g=m_ln_g, ln_b=m_ln_b)
    small_v = dict(w_pool=v_w_pool, w_sgu=v_w_sgu, pool_scale=v_pool_scale, sgu_ln_g=v_sgu_ln_g,
                   sgu_ln_b=v_sgu_ln_b, b_sgu=v_b_sgu, ln_g=v_ln_g, ln_b=v_ln_b)

    wint_loc = jnp.transpose(w_in, (0, 2, 1)).astype(BF16)
    wout_loc = w_out.astype(BF16)
    small = (w_pool, pool_scale, sgu_ln_g, sgu_ln_b, w_sgu, jnp.transpose(b_sgu, (0, 2, 1)))
    out, y, cdf, proj, wint0, wout0, act_all, mod, wint1, wout1 = _gathered_layer(
        0, x[0], small, ln_g, ln_b, [wint_loc[0], wout_loc[0]], "layer_fwd_0", exchange=(c, w_ada, b_ada),
        host=[wint_loc[1], wout_loc[1]])
    w_int, w_outf = [wint0, wint1.reshape(D_IN, D_MODEL)], [wout0, wout1.reshape(D_MODEL, D_MODEL)]
    acts, cur = [(x[0], proj, y, cdf)], out
    for l in range(1, DEPTH):
        out, proj, y, cdf = _layer_forward(l, cur, mod, w_int[l], w_outf[l], small, ln_g, ln_b, f"layer_fwd_{l}")
        acts.append((cur, proj, y, cdf))
        cur = out

    shard_in, shard_out = D_IN // N_DEV, D_MODEL // N_DEV
    a, b = cur, loss_target[0]
    loss_lanes, carry, pending = None, (), []
    g_w_in_t, g_w_out = [None] * DEPTH, [None] * DEPTH
    for l in reversed(range(DEPTH)):
        dx, dproj, h, cat, dy, small_grads, dmod8, lanes, *shards = _layer_backward(
            l, a, b, *acts[l], mod, w_int[l], w_outf[l], small, ln_g, l == DEPTH - 1, f"layer_bwd_{l}",
            carry=carry, reduce=pending)
        if shards:
            g_w_in_t[l + 1], g_w_out[l + 1] = shards
        if l == DEPTH - 1:
            loss_lanes = lanes
        if l > 0:
            g_in, g_out = _grad_matmuls(dproj, h, cat, dy, f"grad_w_{l}")
            pending = [g_in.reshape(4, 2, shard_in, D_MODEL), g_out.reshape(4, 2, shard_out, D_MODEL)]
        carry = (small_grads, dmod8)
        a = b = dx
    grad_x = a[None]

    (g_w_in_t[0], g_w_out[0], small_tot, loss_tile, g_w_ada, d_w_ada, nm_w_ada, nv_w_ada,
     g_b_ada, d_b_ada, nm_b_ada, nv_b_ada) = _grad_tail(
        dproj, h, cat, dy, small_grads.reshape(4, 2, DEPTH * PACK_ROWS // N_DEV, 128), dmod8, loss_lanes,
        w_ada, m_w_ada, v_w_ada, jnp.transpose(act_all), b_ada, m_b_ada, v_b_ada)
    loss = loss_tile[0, 0]

    flat = lambda t: t.reshape(-1, t.shape[-1])
    to_t = lambda t: flat(jnp.transpose(t, (0, 2, 1)))
    from_t = lambda t: jnp.transpose(t.reshape(DEPTH, shard_in, D_MODEL), (0, 2, 1))
    g_w_in, d_w_in, nm_w_in, nv_w_in = [from_t(t) for t in _adamw(to_t(w_in), g_w_in_t, to_t(m_w_in), to_t(v_w_in),
                                                                  shard_in // 2, "adamw_w_in")]
    gwout, d_w_out, nm_w_out, nv_w_out = [t.reshape(w_out.shape) for t in _adamw(
        flat(w_out), g_w_out, flat(m_w_out), flat(v_w_out), shard_out, "adamw_w_out")]
    small_out = _adamw_small(small_tot.reshape(DEPTH * PACK_ROWS, 128), [small_w[n] for n in SMALL_NAMES],
                             [small_m[n] for n in SMALL_NAMES], [small_v[n] for n in SMALL_NAMES], "adamw_small")
    gs, ds, ms, vs = [dict(zip(SMALL_NAMES, group)) for group in small_out]

    def ordered(w_ada_, b_ada_, w_in_, small, w_out_):
        return (w_ada_, b_ada_, w_in_, small["w_pool"], small["pool_scale"], small["sgu_ln_g"], small["sgu_ln_b"],
                small["w_sgu"], small["b_sgu"], w_out_, small["ln_g"], small["ln_b"])

    return (loss, grad_x,
            *ordered(g_w_ada, g_b_ada, g_w_in, gs, gwout),
            *ordered(d_w_ada, d_b_ada, d_w_in, ds, d_w_out),
            *ordered(nm_w_ada, nm_b_ada, nm_w_in, ms, nm_w_out),
            *ordered(nv_w_ada, nv_b_ada, nv_w_in, vs, nv_w_out))
```

```python
import jax
import jax.numpy as jnp
from jax import lax
from jax.experimental import pallas as pl
from jax.experimental.pallas import tpu as pltpu

F32 = jnp.float32
BF16 = jnp.bfloat16

D_MODEL = 1024
SEQ = 2048
DEPTH = 2
D_POOL = 512
D_SGU = 512
D_IN = 2560
N_GROUPS = 4
GROUP = 128
N_HEADS = 4
HEAD = 128
CHUNK = 128
WINDOWS = (2, 4, 8, 16)
ALPHA = (2.0 * DEPTH) ** 0.25
LN_EPS = 1e-5
N_DEV = 8

ADAM_LR = 0.001
ADAM_B1 = 0.9
ADAM_B2 = 0.999
ADAM_EPS = 1e-08
ADAM_WD = 0.01
ADAM_STEP = 10

TM = 256
HALO = 16
N_TILES = SEQ // TM
VMEM_LIMIT = 60 * 1024 * 1024

ROW_WPOOL = 0
ROW_WSGU = 512
ROW_PSCALE = 1024
ROW_SLNG = 1028
ROW_SLNB = 1032
ROW_BSGU = 1036
ROW_LNG = 1040
ROW_LNB = 1048
PACK_ROWS = 1088
DMOD_COLS = DEPTH * 3 * D_MODEL // 8

SQRT_HALF = 0.7071067811865476
INV_SQRT_2PI = 0.3989422804014327


def _ln(x):
    mu = jnp.mean(x, axis=-1, keepdims=True)
    xc = x - mu
    var = jnp.mean(xc * xc, axis=-1, keepdims=True)
    rstd = lax.rsqrt(var + LN_EPS)
    return xc * rstd, rstd


def _ln_bwd(dxn, xn, rstd):
    m1 = jnp.mean(dxn, axis=-1, keepdims=True)
    m2 = jnp.mean(dxn * xn, axis=-1, keepdims=True)
    return rstd * (dxn - m1 - xn * m2)


def _normal_cdf(x):
    return 0.5 * (1.0 + lax.erf(x * SQRT_HALF))


def _gelu_parts(x, cdf, with_grad):
    if not with_grad:
        return x * cdf, None
    return x * cdf, cdf + x * (INV_SQRT_2PI * jnp.exp(-0.5 * x * x))


def _silu_parts(x):
    s = jax.nn.sigmoid(x)
    return x * s, s * (1.0 + x * (1.0 - s))


def _dot(a, b):
    return lax.dot_general(a, b, (((1,), (0,)), ((), ())), preferred_element_type=F32)


def _dot_nt(a, b):
    return lax.dot_general(a, b, (((1,), (1,)), ((), ())), preferred_element_type=F32)


def _dot_tn(a, b):
    return lax.dot_general(a, b, (((0,), (0,)), ((), ())), preferred_element_type=F32)


def _row_index(tile):
    return tile * TM + lax.broadcasted_iota(jnp.int32, (TM, 1), 0)


def _window_sums(ext, forward):
    n = TM + HALO
    cur = ext
    outs = []
    for g in range(N_GROUPS):
        step = 1 << g
        cur = cur + pltpu.roll(cur, step if forward else n - step, 0)
        rows = cur[HALO:, :GROUP] if forward else cur[:TM, :GROUP]
        outs.append(rows)
        cur = cur[:, GROUP:] if g + 1 < N_GROUPS else None
    return outs


def _inverse_counts(rows):
    return [1.0 / jnp.minimum(rows + 1, w).astype(F32) for w in WINDOWS]


def _tril_bf16(w):
    t = lax.broadcasted_iota(jnp.int32, (CHUNK, CHUNK), 0)
    s = lax.broadcasted_iota(jnp.int32, (CHUNK, CHUNK), 1)
    return jnp.where(t >= s, w, 0.0).astype(BF16)


class _MixWeights:
    def __init__(self, layer, wpool_ref, pscale_ref, slng_ref, slnb_ref, wsgu_ref, bsgut_ref):
        self.layer = layer
        self.wpool_ref, self.pscale_ref, self.slng_ref, self.slnb_ref = wpool_ref, pscale_ref, slng_ref, slnb_ref
        self.wsgu_ref, self.bsgut_ref = wsgu_ref, bsgut_ref

    def pool(self, g):
        return self.wpool_ref[self.layer, g].astype(BF16)

    def pool_scale(self, g):
        return self.pscale_ref[self.layer:self.layer + 1, g * GROUP:(g + 1) * GROUP]

    def ln_gain(self, h):
        return self.slng_ref[self.layer, h:h + 1, :]

    def ln_bias(self, h):
        return self.slnb_ref[self.layer, h:h + 1, :]

    def mix(self, h):
        return _tril_bf16(self.wsgu_ref[self.layer, h])

    def mix_bias(self, h):
        return self.bsgut_ref[self.layer, :, h:h + 1]


SMALL_SPECS = ((DEPTH, N_GROUPS, GROUP, GROUP), (DEPTH, D_POOL), (DEPTH, N_HEADS, HEAD), (DEPTH, N_HEADS, HEAD),
               (DEPTH, N_HEADS, CHUNK, CHUNK), (DEPTH, CHUNK, N_HEADS))


def _mix_forward(proj, halo, tile, w, cdf=None):
    keep = cdf is not None
    rows = _row_index(tile)
    inv_counts = _inverse_counts(rows)
    xa = proj[:, 0:D_POOL]
    ga = proj[:, D_POOL:2 * D_POOL]
    sums = _window_sums(jnp.concatenate([halo, xa], axis=0), True)
    ga_act, ga_grad = _silu_parts(ga)
    pooled, pw, ya = [], [], []
    for g in range(N_GROUPS):
        sl = slice(g * GROUP, (g + 1) * GROUP)
        p = (sums[g] * inv_counts[g] - xa[:, sl]).astype(BF16)
        q = _dot(p, w.pool(g))
        pooled.append(p)
        pw.append(q)
        ya.append(q * w.pool_scale(g) * ga_act[:, sl])

    u = proj[:, 2 * D_POOL:2 * D_POOL + D_SGU]
    v = proj[:, 2 * D_POOL + D_SGU:2 * D_POOL + 2 * D_SGU]
    gb = proj[:, 2 * D_POOL + 2 * D_SGU:]
    gb_act, gb_grad = _silu_parts(gb)
    if cdf is None:
        cdf = jnp.concatenate([_normal_cdf(u), _normal_cdf(v)], axis=1)
    u_act, u_grad = _gelu_parts(u, cdf[:, :D_SGU], keep)
    v_act, v_grad = _gelu_parts(v, cdf[:, D_SGU:], keep)
    vn, vrstd, vln, mixed, yb = [], [], [], [], []
    for h in range(N_HEADS):
        sl = slice(h * HEAD, (h + 1) * HEAD)
        n_h, r_h = _ln(v_act[:, sl])
        l_h = (n_h * w.ln_gain(h) + w.ln_bias(h)).astype(BF16)
        w_h = w.mix(h)
        bias = w.mix_bias(h)
        m_h = jnp.concatenate(
            [_dot(w_h, l_h[k * CHUNK:(k + 1) * CHUNK]) + bias for k in range(TM // CHUNK)], axis=0)
        vn.append(n_h)
        vrstd.append(r_h)
        vln.append(l_h)
        mixed.append(m_h)
        yb.append(u_act[:, sl] * m_h * gb_act[:, sl])
    cat = jnp.concatenate(ya + yb, axis=1)
    if not keep:
        return cat, cdf
    return cat, dict(inv_counts=inv_counts, ga_act=ga_act, ga_grad=ga_grad, pooled=pooled, pw=pw, u_grad=u_grad,
                     v_grad=v_grad, u_act=u_act, gb_act=gb_act, gb_grad=gb_grad, vn=vn, vrstd=vrstd, vln=vln,
                     mixed=mixed)


def _const_spec(shape):
    nd = len(shape)
    return pl.BlockSpec(shape, lambda i: (0,) * nd)


def _layer_forward(layer, x, mod, w_int, w_outf, small, ln_g, ln_b, name):
    def body(x_ref, mod_ref, wint_ref, wout_ref, wpool_ref, pscale_ref, slng_ref, slnb_ref, wsgu_ref, bsgut_ref,
             lng_ref, lnb_ref, out_ref, proj_ref, y_ref, cdf_ref, halo_ref):
        weights = _MixWeights(layer, wpool_ref, pscale_ref, slng_ref, slnb_ref, wsgu_ref, bsgut_ref)
        tile = pl.program_id(0)

        @pl.when(tile == 0)
        def _():
            halo_ref[...] = jnp.zeros_like(halo_ref)

        xt = x_ref[...]
        shift = mod_ref[layer:layer + 1, 0:D_MODEL]
        scale = mod_ref[layer:layer + 1, D_MODEL:2 * D_MODEL]
        gate = mod_ref[layer:layer + 1, 2 * D_MODEL:]
        xn, _ = _ln(xt)
        h = (xn * (1.0 + scale) + shift).astype(BF16)
        proj = _dot_nt(h, wint_ref[...])
        proj_ref[...] = proj
        cat, cdf_ref[...] = _mix_forward(proj, halo_ref[...], tile, weights)
        halo_ref[...] = proj[TM - HALO:, 0:D_POOL]
        y = _dot(cat.astype(BF16), wout_ref[...])
        y_ref[...] = y
        zn, _ = _ln(ALPHA * xt + gate * y)
        out_ref[...] = zn * lng_ref[layer:layer + 1, :] + lnb_ref[layer:layer + 1, :]

    row = lambda w: pl.BlockSpec((TM, w), lambda i: (i, 0))
    return pl.pallas_call(
        body,
        name=name,
        grid=(N_TILES,),
        in_specs=[row(D_MODEL), _const_spec((DEPTH, 3 * D_MODEL)), _const_spec((D_IN, D_MODEL)),
                  _const_spec((D_MODEL, D_MODEL))] + [_const_spec(s) for s in SMALL_SPECS]
                 + [_const_spec((DEPTH, D_MODEL)), _const_spec((DEPTH, D_MODEL))],
        out_specs=[row(D_MODEL), row(D_IN), row(D_MODEL), row(2 * D_SGU)],
        out_shape=[jax.ShapeDtypeStruct((SEQ, D_MODEL), F32), jax.ShapeDtypeStruct((SEQ, D_IN), F32),
                   jax.ShapeDtypeStruct((SEQ, D_MODEL), F32), jax.ShapeDtypeStruct((SEQ, 2 * D_SGU), F32)],
        scratch_shapes=[pltpu.VMEM((HALO, D_POOL), F32)],
        compiler_params=pltpu.CompilerParams(dimension_semantics=("arbitrary",), vmem_limit_bytes=VMEM_LIMIT),
    )(x, mod, w_int, w_outf, *small, ln_g, ln_b)


VEC_LNG, VEC_LNB, VEC_POOL, VEC_SGU, VEC_SHIFT, VEC_SCALE, VEC_GATE, VEC_LOSS = range(8)


def _layer_backward(layer, a, b, x, proj, y, cdf, mod, w_int, w_outf, small, ln_g, is_last, name, carry=(),
                    reduce=(), allreduce=None):
    n_red, n_carry = len(reduce), len(carry)
    n_all = 0 if allreduce is None else 1
    n_rs = n_red + n_all
    base = 0

    def body(a_ref, b_ref, x_ref, proj_ref, prev_ref, y_ref, cdf_ref, mod_ref, wint_ref, wout_ref, wpool_ref,
             pscale_ref, slng_ref, slnb_ref, wsgu_ref, bsgut_ref, lng_ref, *rest):
        weights = _MixWeights(layer, wpool_ref, pscale_ref, slng_ref, slnb_ref, wsgu_ref, bsgut_ref)
        carry_refs, rest = rest[:n_carry], rest[n_carry:]
        part_refs, rest = rest[:n_rs], rest[n_rs:]
        dx_ref, dproj_ref, h_ref, cat_ref, dy_ref, small_ref, dmod_ref, loss_ref = rest[:8]
        shard_refs, rest = rest[8:8 + n_red], rest[8 + n_red:]
        total_refs, rest = rest[:n_all], rest[n_all:]
        vec_ref, dmix_ref, halo_ref = rest[:3]
        step = pl.program_id(0)
        tile = N_TILES - 1 - step
        bufs = rest[3:3 + 5 * n_rs]
        rs_sems = rest[3 + 5 * n_rs:3 + 5 * n_rs + 5]
        if n_all:
            own_total, totals, g_send, g_recv = rest[3 + 5 * n_rs + 5:]
        big, tiny = list(range(n_red)), list(range(n_red, n_rs))

        def scatter():
            outs = list(shard_refs) + ([own_total] if n_all else [])
            arrays = [dict(part=part_refs[n], out=outs[n], staged=True, stage=bufs[5 * n], sib=bufs[5 * n + 1],
                           snd=bufs[5 * n + 2], rcv=bufs[5 * n + 3], relay=bufs[5 * n + 4]) for n in range(n_rs)]
            return _ChipReduceScatter(arrays, *rs_sems)

        @pl.when(step == 0)
        def _():
            small_ref[...] = jnp.zeros_like(small_ref)
            dmod_ref[...] = jnp.zeros_like(dmod_ref)
            vec_ref[...] = jnp.zeros_like(vec_ref)
            dmix_ref[...] = jnp.zeros_like(dmix_ref)
            halo_ref[...] = jnp.zeros_like(halo_ref)
            if n_rs:
                scatter().start()

        if n_rs:
            @pl.when(step == 1)
            def _():
                scatter().exchange()

            @pl.when(step == N_TILES // 2)
            def _():
                scatter().fold()

        if n_all:
            @pl.when(step == N_TILES // 2 + 1)
            def _():
                scatter().finish(tiny)
                totals[_index(_me())] = own_total[...]
                _TwoLevelGather(totals, g_send, g_recv).send_mine()

            @pl.when(step == N_TILES // 2 + 2)
            def _():
                _TwoLevelGather(totals, g_send, g_recv).relay()

        def acc(row, lo, val):
            hi = lo + val.shape[1]
            vec_ref[row:row + 1, lo:hi] += jnp.sum(val, axis=0, keepdims=True)

        xt = x_ref[...]
        yt = y_ref[...]
        shift = mod_ref[layer:layer + 1, 0:D_MODEL]
        scale = mod_ref[layer:layer + 1, D_MODEL:2 * D_MODEL]
        gate = mod_ref[layer:layer + 1, 2 * D_MODEL:]
        ln_gain = lng_ref[layer:layer + 1, :]

        zn, zrstd = _ln(ALPHA * xt + gate * yt)
        if is_last:
            diff = a_ref[...] - b_ref[...]
            acc(VEC_LOSS, 0, diff * diff)
            dout = diff * (1.0 / D_MODEL)
        else:
            dout = a_ref[...]
        acc(VEC_LNG, 0, dout * zn)
        acc(VEC_LNB, 0, dout)
        dz = _ln_bwd(dout * ln_gain, zn, zrstd)
        acc(VEC_GATE, 0, dz * yt)
        dy = (dz * gate).astype(BF16)
        dy_ref[...] = dy
        dcat = _dot_nt(dy, wout_ref[...])

        proj = proj_ref[...]
        prev = jnp.where(tile > 0, prev_ref[...], 0.0)
        cat, k = _mix_forward(proj, prev, tile, weights, cdf_ref[...])
        cat_ref[...] = cat.astype(BF16)

        dga, dq = [], []
        for g in range(N_GROUPS):
            sl = slice(g * GROUP, (g + 1) * GROUP)
            pscale = weights.pool_scale(g)
            dya = dcat[:, sl]
            dyp = dya * k["ga_act"][:, sl]
            dga.append(dya * k["pw"][g] * pscale * k["ga_grad"][:, sl])
            acc(VEC_POOL, g * GROUP, dyp * k["pw"][g])
            dpw = (dyp * pscale).astype(BF16)
            rows = pl.ds(base + ROW_WPOOL + g * GROUP, GROUP)
            small_ref[rows, :] += _dot_tn(k["pooled"][g], dpw)
            dq.append(_dot_nt(dpw, weights.pool(g)))
        dpooled = jnp.concatenate(dq, axis=1)
        scaled = jnp.concatenate([dq[g] * k["inv_counts"][g] for g in range(N_GROUPS)], axis=1)
        sums = _window_sums(jnp.concatenate([scaled, halo_ref[...]], axis=0), False)
        halo_ref[...] = scaled[0:HALO]
        dxa = jnp.concatenate(sums, axis=1) - dpooled

        du, dv, dgb = [], [], []
        for h in range(N_HEADS):
            sl = slice(h * HEAD, (h + 1) * HEAD)
            dyb = dcat[:, D_POOL + h * HEAD:D_POOL + (h + 1) * HEAD]
            m_h = k["mixed"][h]
            ug = k["u_act"][:, sl] * dyb
            du.append(dyb * m_h * k["gb_act"][:, sl] * k["u_grad"][:, sl])
            dgb.append(ug * m_h * k["gb_grad"][:, sl])
            dmixed = ug * k["gb_act"][:, sl]
            dmixed_bf = dmixed.astype(BF16)
            w_h = weights.mix(h)
            dvln_parts = []
            dmix_sum = dmix_ref[h]
            wsgu_rows = pl.ds(base + ROW_WSGU + h * CHUNK, CHUNK)
            dws = small_ref[wsgu_rows, :]
            for c in range(TM // CHUNK):
                cs = slice(c * CHUNK, (c + 1) * CHUNK)
                dmix_sum = dmix_sum + dmixed[cs]
                dws = dws + _dot_nt(dmixed_bf[cs], k["vln"][h][cs])
                dvln_parts.append(_dot_tn(w_h, dmixed_bf[cs]))
            dmix_ref[h] = dmix_sum
            small_ref[wsgu_rows, :] = dws
            dvln = jnp.concatenate(dvln_parts, axis=0)
            acc(VEC_SGU, h * HEAD, dvln * k["vn"][h])
            acc(VEC_SGU, D_SGU + h * HEAD, dvln)
            dvv = _ln_bwd(dvln * weights.ln_gain(h), k["vn"][h], k["vrstd"][h])
            dv.append(dvv * k["v_grad"][:, sl])

        dproj = jnp.concatenate([dxa] + dga + du + dv + dgb, axis=1).astype(BF16)
        dproj_ref[...] = dproj
        dh = _dot(dproj, wint_ref[...])

        xn, xrstd = _ln(xt)
        h_ref[...] = (xn * (1.0 + scale) + shift).astype(BF16)
        acc(VEC_SCALE, 0, dh * xn)
        acc(VEC_SHIFT, 0, dh)
        dx_ref[...] = _ln_bwd(dh * (1.0 + scale), xn, xrstd) + ALPHA * dz

        @pl.when(step == N_TILES - 1)
        def _():
            def put(row0, vec_row, lo, n):
                for r in range(n):
                    small_ref[base + row0 + r:base + row0 + r + 1, :] = (
                        vec_ref[vec_row:vec_row + 1, lo + r * 128:lo + (r + 1) * 128])

            put(ROW_PSCALE, VEC_POOL, 0, 4)
            put(ROW_SLNG, VEC_SGU, 0, 4)
            put(ROW_SLNB, VEC_SGU, D_SGU, 4)
            put(ROW_LNG, VEC_LNG, 0, 8)
            put(ROW_LNB, VEC_LNB, 0, 8)
            ones = jnp.ones((8, HEAD), F32)
            t = lax.broadcasted_iota(jnp.int32, (CHUNK, CHUNK), 0)
            s = lax.broadcasted_iota(jnp.int32, (CHUNK, CHUNK), 1)
            for h in range(N_HEADS):
                bias_rows = lax.dot_general(ones, dmix_ref[h], (((1,), (1,)), ((), ())),
                                            preferred_element_type=F32, precision=lax.Precision.HIGHEST)
                small_ref[base + ROW_BSGU + h:base + ROW_BSGU + h + 1, :] = bias_rows[0:1]
                rows = pl.ds(base + ROW_WSGU + h * CHUNK, CHUNK)
                small_ref[rows, :] = jnp.where(t >= s, small_ref[rows, :], 0.0)
            pieces = ((0, VEC_SHIFT, 0, 768),
                      (1, VEC_SHIFT, 768, 256), (1, VEC_SCALE, 0, 512),
                      (2, VEC_SCALE, 512, 512), (2, VEC_GATE, 0, 256),
                      (3, VEC_GATE, 256, 768))
            filled = [0] * 4
            for q, vec_row, lo, n in pieces:
                row = 4 * layer + q
                dmod_ref[row:row + 1, filled[q]:filled[q] + n] = vec_ref[vec_row:vec_row + 1, lo:lo + n]
                filled[q] += n
            if n_carry:
                for other in range(layer + 1, DEPTH):
                    dmod_ref[4 * other:4 * other + 4, :] = carry_refs[0][4 * other:4 * other + 4, :]
            loss_ref[...] = vec_ref[VEC_LOSS:VEC_LOSS + 1, :]
            if n_red:
                scatter().finish(big)
            if n_all:
                gather = _TwoLevelGather(totals, g_send, g_recv)
                gather.pass_on()
                gather.wait_rest()
                gather.wait_sends()
                total_refs[0][...] = totals[...]
            if n_rs:
                scatter().wait_sends()

    rev = lambda w: pl.BlockSpec((TM, w), lambda i: (N_TILES - 1 - i, 0))
    prev_spec = pl.BlockSpec(
        (HALO, D_POOL), lambda i: (jnp.maximum((N_TILES - 1 - i) * (TM // HALO) - 1, 0), 0))
    scattered = list(reduce) + ([allreduce] if n_all else [])
    comm_scratch = []
    for p in scattered:
        comm_scratch += _ChipReduceScatter.buffers(p.shape[2], p.shape[3], p.dtype)
    if n_rs:
        comm_scratch += _ChipReduceScatter.semaphores(n_rs)
    gathered = (N_DEV,) + allreduce.shape[2:] if n_all else None
    if n_all:
        comm_scratch += [pltpu.VMEM(allreduce.shape[2:], F32), pltpu.VMEM(gathered, F32),
                         pltpu.SemaphoreType.DMA((GATHER_SEMS,)), pltpu.SemaphoreType.DMA((GATHER_SEMS,))]
    return pl.pallas_call(
        body,
        name=name,
        grid=(N_TILES,),
        in_specs=[rev(D_MODEL), rev(D_MODEL) if is_last else pl.BlockSpec((TM, D_MODEL), lambda i: (0, 0)),
                  rev(D_MODEL), rev(D_IN), prev_spec, rev(D_MODEL), rev(2 * D_SGU),
                  _const_spec((DEPTH, 3 * D_MODEL)), _const_spec((D_IN, D_MODEL)), _const_spec((D_MODEL, D_MODEL))]
                 + [_const_spec(s) for s in SMALL_SPECS] + [_const_spec((DEPTH, D_MODEL))]
                 + [_const_spec(c.shape) for c in carry] + [ANY] * n_rs,
        out_specs=[rev(D_MODEL), rev(D_IN), rev(D_MODEL), rev(D_MODEL), rev(D_MODEL),
                   _const_spec((PACK_ROWS, 128)), _const_spec((8, DMOD_COLS)), _const_spec((1, D_MODEL))]
                  + [_const_spec(p.shape[2:]) for p in reduce] + ([_const_spec(gathered)] if n_all else []),
        out_shape=[jax.ShapeDtypeStruct((SEQ, D_MODEL), F32), jax.ShapeDtypeStruct((SEQ, D_IN), BF16),
                   jax.ShapeDtypeStruct((SEQ, D_MODEL), BF16), jax.ShapeDtypeStruct((SEQ, D_MODEL), BF16),
                   jax.ShapeDtypeStruct((SEQ, D_MODEL), BF16), jax.ShapeDtypeStruct((PACK_ROWS, 128), F32),
                   jax.ShapeDtypeStruct((8, DMOD_COLS), F32), jax.ShapeDtypeStruct((1, D_MODEL), F32)]
                  + [jax.ShapeDtypeStruct(p.shape[2:], F32) for p in reduce]
                  + ([jax.ShapeDtypeStruct(gathered, F32)] if n_all else []),
        scratch_shapes=[pltpu.VMEM((8, D_MODEL), F32), pltpu.VMEM((N_HEADS, CHUNK, HEAD), F32),
                        pltpu.VMEM((HALO, D_POOL), F32)] + comm_scratch,
        compiler_params=pltpu.CompilerParams(dimension_semantics=("arbitrary",), vmem_limit_bytes=VMEM_LIMIT),
    )(a, b, x, proj, proj, y, cdf, mod, w_int, w_outf, *small, ln_g, *carry, *scattered)


def _grad_matmuls(dproj, h, cat, dy, name):
    in_cols, out_cols = D_IN // 4, D_MODEL // 2
    in_steps = D_IN // in_cols

    def body(dproj_ref, h_ref, cat_ref, dy_ref, gin_ref, gout_ref):
        step = pl.program_id(0)

        @pl.when(step < in_steps)
        def _():
            gin_ref[...] = _dot_tn(dproj_ref[...], h_ref[...]).astype(BF16)

        @pl.when(step >= in_steps)
        def _():
            gout_ref[...] = _dot_tn(cat_ref[...], dy_ref[...]).astype(BF16)

    in_block = lambda j: jnp.minimum(j, in_steps - 1)
    out_block = lambda j: jnp.maximum(j - in_steps, 0)
    return pl.pallas_call(
        body,
        name=name,
        grid=(in_steps + D_MODEL // out_cols,),
        in_specs=[pl.BlockSpec((SEQ, in_cols), lambda j: (0, in_block(j))), _const_spec((SEQ, D_MODEL)),
                  pl.BlockSpec((SEQ, out_cols), lambda j: (0, out_block(j))), _const_spec((SEQ, D_MODEL))],
        out_specs=[pl.BlockSpec((in_cols, D_MODEL), lambda j: (in_block(j), 0)),
                   pl.BlockSpec((out_cols, D_MODEL), lambda j: (out_block(j), 0))],
        out_shape=[jax.ShapeDtypeStruct((D_IN, D_MODEL), BF16), jax.ShapeDtypeStruct((D_MODEL, D_MODEL), BF16)],
        compiler_params=pltpu.CompilerParams(dimension_semantics=("arbitrary",), vmem_limit_bytes=VMEM_LIMIT),
    )(dproj, h, cat, dy)


def _adamw_math(w, g, m, v):
    m = ADAM_B1 * m + (1.0 - ADAM_B1) * g
    v = ADAM_B2 * v + (1.0 - ADAM_B2) * (g * g)
    m_hat = m / (1.0 - ADAM_B1 ** ADAM_STEP)
    v_hat = v / (1.0 - ADAM_B2 ** ADAM_STEP)
    delta = -ADAM_LR * (m_hat / (jnp.sqrt(v_hat) + ADAM_EPS) + ADAM_WD * w)
    return delta, m, v


def _adamw(w, grads, m, v, block_rows, name):
    rows, cols = grads[0].shape
    blocks = rows // block_rows

    def body(w_ref, m_ref, v_ref, *rest):
        g_refs, (g_ref, d_ref, nm_ref, nv_ref) = rest[:DEPTH], rest[DEPTH:]
        for layer in range(DEPTH):
            @pl.when(pl.program_id(0) == layer)
            def _():
                g = g_refs[layer][...]
                g_ref[...] = g
                d_ref[...], nm_ref[...], nv_ref[...] = _adamw_math(w_ref[...], g, m_ref[...], v_ref[...])

    def grad_spec(layer):
        return pl.BlockSpec((block_rows, cols),
                            lambda l, i: (jnp.where(l == layer, i, jnp.where(l < layer, 0, blocks - 1)), 0))

    spec = pl.BlockSpec((block_rows, cols), lambda l, i: (l * blocks + i, 0))
    return pl.pallas_call(
        body,
        name=name,
        grid=(DEPTH, blocks),
        in_specs=[spec] * 3 + [grad_spec(layer) for layer in range(DEPTH)],
        out_specs=[spec] * 4,
        out_shape=[jax.ShapeDtypeStruct(w.shape, F32)] * 4,
        compiler_params=pltpu.CompilerParams(dimension_semantics=("arbitrary", "arbitrary"),
                                             vmem_limit_bytes=VMEM_LIMIT),
    )(w, m, v, *grads)


MESH = pl.DeviceIdType.MESH
SIBLING = 1
ANY = pl.BlockSpec(memory_space=pl.ANY)
VMEM = pl.BlockSpec(memory_space=pltpu.VMEM)


def _me():
    return lax.axis_index("x"), lax.axis_index("y"), lax.axis_index("c")


def _peer(r):
    x, y, c = _me()
    return (1 - x if r & 4 else x, 1 - y if r & 2 else y, 1 - c if r & 1 else c)


def _index(dev):
    return 4 * dev[0] + 2 * dev[1] + dev[2]


def _remote(src, dst, send_sem, recv_sem, dev):
    return pltpu.make_async_remote_copy(src_ref=src, dst_ref=dst, send_sem=send_sem, recv_sem=recv_sem,
                                        device_id=dev, device_id_type=MESH)


ACROSS_X, ACROSS_Y, ACROSS_BOTH = 4, 2, 6
GATHER_SEMS = 11


class _TwoLevelGather:
    def __init__(self, out, send_sems, recv_sems, src=None):
        self.out, self.send_sems, self.recv_sems, self.src = out, send_sems, recv_sems, src
        self.rows = (out.shape[0] // N_DEV) if len(out.shape) == 2 else out.shape[1]
        self.half = self.rows // 2

    def _slot(self, block):
        if len(self.out.shape) == 2:
            return self.out.at[pl.ds(pl.multiple_of(_index(block) * self.rows, self.rows), self.rows)]
        return self.out.at[_index(block)]

    def _copy(self, k, block, part, to, src=None):
        slot = self._slot(block)
        if part is not None:
            rows = pl.ds(part * self.half, self.half)
            slot = slot.at[rows]
            src = None if src is None else src.at[rows]
        return _remote(slot if src is None else src, slot, self.send_sems.at[k], self.recv_sems.at[k], to)

    def _mine(self):
        me = _me()
        src = self._slot(me) if self.src is None else self.src
        x, y = _peer(ACROSS_X), _peer(ACROSS_Y)
        return [self._copy(1, me, 0, x, src), self._copy(3, me, 1, y, src), self._copy(0, me, None, _peer(SIBLING), src),
                self._copy(2, me, 1, x, src), self._copy(4, me, 0, y, src)]

    def _relayed(self):
        return [self._copy(5, _peer(ACROSS_X), 0, _peer(ACROSS_Y)), self._copy(6, _peer(ACROSS_Y), 1, _peer(ACROSS_X))]

    def _passed(self):
        sib, far = _peer(SIBLING), _peer(ACROSS_BOTH)
        return [self._copy(7, _peer(ACROSS_X), None, sib), self._copy(8, _peer(ACROSS_Y), None, sib),
                self._copy(9, far, 0, sib), self._copy(10, far, 1, sib)]

    def _arrival(self, k, r, part):
        return self._copy(k, _peer(r), part, _me())

    def send_first(self):
        for cp in self._mine()[:3]:
            cp.start()

    def send_second(self):
        for cp in self._mine()[3:]:
            cp.start()

    def send_mine(self):
        self.send_first()
        self.send_second()

    def relay(self):
        relayed = self._relayed()
        self._arrival(1, ACROSS_X, 0).wait_recv()
        relayed[0].start()
        self._arrival(3, ACROSS_Y, 1).wait_recv()
        relayed[1].start()

    def pass_near(self):
        passed = self._passed()
        self._arrival(2, ACROSS_X, 1).wait_recv()
        passed[0].start()
        self._arrival(4, ACROSS_Y, 0).wait_recv()
        passed[1].start()

    def pass_far(self):
        passed = self._passed()
        self._arrival(5, ACROSS_BOTH, 0).wait_recv()
        passed[2].start()
        self._arrival(6, ACROSS_BOTH, 1).wait_recv()
        passed[3].start()

    def pass_on(self):
        self.pass_near()
        self.pass_far()

    def wait_sibling(self):
        self._arrival(0, SIBLING, None).wait_recv()

    def wait_passed(self, r):
        if r == ACROSS_BOTH:
            self._arrival(9, r ^ SIBLING, 0).wait_recv()
            self._arrival(10, r ^ SIBLING, 1).wait_recv()
        else:
            self._arrival(7 if r == ACROSS_X else 8, r ^ SIBLING, None).wait_recv()

    def wait_rest(self):
        self.wait_sibling()
        for r in (ACROSS_X, ACROSS_Y, ACROSS_BOTH):
            self.wait_passed(r)

    def wait_sends(self):
        for cp in self._mine() + self._relayed() + self._passed():
            cp.wait_send()


class _ChipReduceScatter:
    SLOTS = 6

    def __init__(self, arrays, l_sem, d_send, d_recv, i_send, i_recv):
        self.arrays = arrays
        self.l_sem, self.d_send, self.d_recv, self.i_send, self.i_recv = l_sem, d_send, d_recv, i_send, i_recv

    @staticmethod
    def buffers(rows, cols, dtype, staged=True):
        stage = [pltpu.VMEM((4, rows, cols), dtype)] if staged else []
        return stage + [pltpu.VMEM((4, rows, cols), dtype), pltpu.VMEM((3, rows, cols), dtype),
                        pltpu.VMEM((2, rows, cols), dtype), pltpu.VMEM((2, rows // 2, cols), dtype)]

    @classmethod
    def semaphores(cls, n):
        return [pltpu.SemaphoreType.DMA((n,)), pltpu.SemaphoreType.DMA((n, 4)), pltpu.SemaphoreType.DMA((n, 4)),
                pltpu.SemaphoreType.DMA((n, cls.SLOTS)), pltpu.SemaphoreType.DMA((n, cls.SLOTS))]

    def _pick(self, which):
        return list(enumerate(self.arrays)) if which is None else [(n, self.arrays[n]) for n in which]

    @staticmethod
    def _chip(r):
        dev = _me() if r is None else _peer(r)
        return 2 * dev[0] + dev[1]

    def _staging(self, which):
        c = _me()[2]
        return [pltpu.make_async_copy(a["part"].at[pl.ds(0, 4), c], a["stage"], self.l_sem.at[n])
                for n, a in self._pick(which) if a["staged"]]

    def _first(self, which, chip):
        other = 1 - _me()[2]
        return [_remote(a["part"].at[chip, other], a["sib"].at[chip], self.d_send.at[n, chip], self.d_recv.at[n, chip],
                        _peer(SIBLING)) for n, a in self._pick(which)]

    @staticmethod
    def _halves(a):
        half = a["rcv"].shape[1] // 2
        return pl.ds(0, half), pl.ds(half, half)

    def _hops(self, n, a):
        h0, h1 = self._halves(a)
        x, y = _peer(ACROSS_X), _peer(ACROSS_Y)
        snd, rcv, relay = a["snd"], a["rcv"], a["relay"]
        pairs = [(snd.at[2, h0], relay.at[0], x), (snd.at[2, h1], relay.at[1], y),
                 (snd.at[0, h0], rcv.at[0, h0], x), (snd.at[0, h1], rcv.at[0, h1], x),
                 (snd.at[1, h1], rcv.at[1, h1], y), (snd.at[1, h0], rcv.at[1, h0], y)]
        return [_remote(s, d, self.i_send.at[n, k], self.i_recv.at[n, k], to) for k, (s, d, to) in enumerate(pairs)]

    def _mine(self, a, chip, rows=None):
        src = a["stage"].at[chip] if a["staged"] else a["part"].at[chip, _me()[2]]
        mine, sib = (src[...], a["sib"][chip]) if rows is None else (src[rows, :], a["sib"][chip, rows, :])
        return mine.astype(F32) + sib.astype(F32)

    def start(self, which=None, chips=None):
        if chips is None:
            for cp in self._staging(which):
                cp.start()
        for chip in range(4) if chips is None else chips:
            for cp in self._first(which, chip):
                cp.start()

    def send_far(self, which=None):
        far = self._chip(ACROSS_BOTH)
        for cp in self._staging(which):
            cp.wait()
        for cp in self._first(which, far):
            cp.wait_recv()
        for n, a in self._pick(which):
            hops = self._hops(n, a)
            a["snd"][2] = self._mine(a, far).astype(a["snd"].dtype)
            hops[0].start()
            hops[1].start()

    def send_near(self, r, which=None):
        chip = self._chip(r)
        for cp in self._first(which, chip):
            cp.wait_recv()
        for n, a in self._pick(which):
            h0, h1 = self._halves(a)
            hops = self._hops(n, a)
            if r == ACROSS_X:
                a["snd"][0, h0, :] = self._mine(a, chip, h0).astype(a["snd"].dtype)
                hops[2].start()
            else:
                a["snd"][1, h1, :] = self._mine(a, chip, h1).astype(a["snd"].dtype)
                hops[4].start()

    def exchange(self, which=None):
        self.send_far(which)
        self.send_near(ACROSS_X, which)
        self.send_near(ACROSS_Y, which)

    def fold(self, which=None):
        across_x, across_y = self._chip(ACROSS_X), self._chip(ACROSS_Y)
        for n, a in self._pick(which):
            h0, h1 = self._halves(a)
            hops = self._hops(n, a)
            dtype = a["snd"].dtype
            hops[1].wait_recv()
            a["snd"][0, h1, :] = (self._mine(a, across_x, h1) + a["relay"][1].astype(F32)).astype(dtype)
            hops[3].start()
            hops[0].wait_recv()
            a["snd"][1, h0, :] = (self._mine(a, across_y, h0) + a["relay"][0].astype(F32)).astype(dtype)
            hops[5].start()

    def finish(self, which=None):
        home = self._chip(None)
        for cp in self._first(which, home):
            cp.wait_recv()
        for n, a in self._pick(which):
            hops = self._hops(n, a)
            a["out"][...] = self._mine(a, home)
            hops[2].wait_recv()
            hops[3].wait_recv()
            a["out"][...] += a["rcv"][0].astype(F32)
            hops[4].wait_recv()
            hops[5].wait_recv()
            a["out"][...] += a["rcv"][1].astype(F32)

    def wait_sends(self, which=None):
        for chip in range(4):
            for cp in self._first(which, chip):
                cp.wait_send()
        for n, a in self._pick(which):
            for cp in self._hops(n, a):
                cp.wait_send()


def _direct_exchange(src_of, dst_of, send_sems, recv_sems):
    me = _me()
    copies = [_remote(src_of(_peer(r)), dst_of(me), send_sems.at[r - 1], recv_sems.at[r - 1], _peer(r))
              for r in range(1, N_DEV)]
    for cp in copies:
        cp.start()
    return copies


def _wait_direct(copies):
    for cp in copies:
        cp.wait_recv()
    for cp in copies:
        cp.wait_send()


def _gathered_layer(layer, x, small, ln_g, ln_b, mine, name, mod=None, exchange=None, host=()):
    exchanging = exchange is not None
    cols = exchange[1].shape[2] if exchanging else 0
    shard = mine[0].shape[0]
    pair = 2 * shard
    n_lead = 4 if exchanging else 2
    n_host = len(host)

    def body(*refs):
        x_hbm = refs[0]
        lead, refs = refs[1:n_lead], refs[n_lead:]
        (wpool_ref, pscale_ref, slng_ref, slnb_ref, wsgu_ref, bsgut_ref, lng_ref, lnb_ref, wint_hbm,
         wout_hbm), refs = refs[:10], refs[10:]
        host_in, refs = refs[:n_host], refs[n_host:]
        (out_ref, y_ref, cdf_ref, proj_hbm, wint_keep, wout_keep), refs = refs[:6], refs[6:]
        if exchanging:
            c_ref, wada_hbm, bada_ref = lead
            (acts_ref, mod_ref), refs = refs[:2], refs[2:]
        else:
            mod_ref, = lead
        host_out, refs = refs[:n_host], refs[n_host:]
        (wint_v, wout_v, h_buf, proj_blk, proj_tile, halo_ref, x_ref, w_send, w_recv, w_local, p_sems,
         t_sems, in_sems), refs = refs[:13], refs[13:]
        if n_host:
            refs, (n_send, n_recv, n_local) = refs[:-3], refs[-3:]
        weights = _MixWeights(layer, wpool_ref, pscale_ref, slng_ref, slnb_ref, wsgu_ref, bsgut_ref)
        tile = pl.program_id(0)

        def hosted():
            return [_TwoLevelGather(host_out[n], n_send.at[n], n_recv.at[n], src=host_in[n]) for n in range(n_host)]

        def hosted_own():
            index = _index(_me())
            return [pltpu.make_async_copy(host_in[n], host_out[n].at[index], n_local.at[n]) for n in range(n_host)]

        def gathers():
            return (_TwoLevelGather(wint_v, w_send.at[0], w_recv.at[0], src=wint_hbm),
                    _TwoLevelGather(wout_v, w_send.at[1], w_recv.at[1], src=wout_hbm))

        def keeps():
            return [pltpu.make_async_copy(wint_v, wint_keep, w_local.at[2]),
                    pltpu.make_async_copy(wout_v, wout_keep, w_local.at[3])]

        def tile_read(t):
            slot = t % 2
            return pltpu.make_async_copy(proj_hbm.at[pl.ds(pl.multiple_of(t * TM, TM), TM)], proj_tile.at[slot],
                                         t_sems.at[slot])

        @pl.when(tile == 0)
        def _():
            me = _me()
            halo_ref[...] = jnp.zeros_like(halo_ref)
            gather_in, gather_out = gathers()
            own_in = pltpu.make_async_copy(wint_hbm, gather_in._slot(me), w_local.at[0])
            own_out = pltpu.make_async_copy(wout_hbm, gather_out._slot(me), w_local.at[1])
            x_load = pltpu.make_async_copy(x_hbm, x_ref, in_sems.at[0])
            x_load.start()

            if exchanging:
                act_all, act_src, part, mod_recv, wada_ref, a_send, a_recv, m_send, m_recv = refs
                wada_load = pltpu.make_async_copy(wada_hbm, wada_ref, in_sems.at[1])
                wada_load.start()
                mine_index = _index(me)
                cval = c_ref[...]
                act_src[...] = jnp.zeros_like(act_src)
                act_src[0:1, :] = cval * jax.nn.sigmoid(cval)
                act_all[mine_index] = act_src[...]
                act_copies = _direct_exchange(lambda p: act_src, lambda m: act_all.at[_index(m)], a_send, a_recv)

            own_in.start()
            own_out.start()
            gather_in.send_first()

            if exchanging:
                _wait_direct(act_copies)
                acts = jnp.concatenate([act_all[j, 0:1, :] for j in range(N_DEV)], axis=0)
                acts_ref[...] = acts
                part[...] = jnp.zeros_like(part)
                wada_load.wait()
                for l in range(DEPTH):
                    res = lax.dot_general(acts, wada_ref[l], (((1,), (0,)), ((), ())), preferred_element_type=F32,
                                          precision=lax.Precision.HIGHEST)
                    for b in range(N_DEV):
                        part[b, l:l + 1, :] = res[b:b + 1, :]
                mod_recv[mine_index] = part[mine_index]
                mod_copies = _direct_exchange(lambda p: part.at[_index(p)], lambda m: mod_recv.at[_index(m)],
                                              m_send, m_recv)
            gather_in.send_second()
            gather_out.send_mine()

            if exchanging:
                _wait_direct(mod_copies)
                for l in range(DEPTH):
                    for j in range(N_DEV):
                        sl = slice(j * cols, (j + 1) * cols)
                        mod_ref[l:l + 1, sl] = mod_recv[j, l:l + 1, :] + bada_ref[l:l + 1, sl]
            shift = mod_ref[layer:layer + 1, 0:D_MODEL]
            scale = mod_ref[layer:layer + 1, D_MODEL:2 * D_MODEL]
            x_load.wait()
            for t in range(N_TILES):
                rows = pl.ds(t * TM, TM)
                xn, _ = _ln(x_ref[rows, :])
                h_buf[rows, :] = (xn * (1.0 + scale) + shift).astype(BF16)

            chip_of = lambda dev: 2 * dev[0] + dev[1]
            writes = []

            def project(n, dev):
                first = pl.multiple_of(chip_of(dev) * pair, pair)
                if n >= 2:
                    writes[n - 2].wait()
                proj_blk[n % 2] = _dot_nt(h_buf[...], wint_v[pl.ds(first, pair), :])
                cp = pltpu.make_async_copy(proj_blk.at[n % 2], proj_hbm.at[:, pl.ds(first, pair)], p_sems.at[n % 2])
                cp.start()
                writes.append(cp)

            gather_in.relay()
            own_in.wait()
            gather_in.wait_sibling()
            project(0, me)
            gather_in.pass_near()
            gather_in.wait_passed(ACROSS_X)
            project(1, _peer(ACROSS_X))
            gather_out.relay()
            for cp in hosted_own():
                cp.start()
            for g in hosted():
                g.send_mine()
            gather_in.wait_passed(ACROSS_Y)
            project(2, _peer(ACROSS_Y))
            gather_in.pass_far()
            gather_in.wait_passed(ACROSS_BOTH)
            project(3, _peer(ACROSS_BOTH))

            gather_out.pass_on()
            gather_out.wait_rest()
            own_out.wait()
            for cp in keeps():
                cp.start()
            writes[2].wait()
            writes[3].wait()
            tile_read(0).start()

        @pl.when(tile + 1 < N_TILES)
        def _():
            tile_read(tile + 1).start()

        if n_host:
            @pl.when(tile == 1)
            def _():
                for g in hosted():
                    g.relay()

            @pl.when(tile == N_TILES // 2)
            def _():
                for g in hosted():
                    g.pass_near()

        tile_read(tile).wait()
        xt = x_ref[pl.ds(pl.multiple_of(tile * TM, TM), TM), :]
        gate = mod_ref[layer:layer + 1, 2 * D_MODEL:]
        proj = proj_tile[tile % 2]
        cat, cdf_ref[...] = _mix_forward(proj, halo_ref[...], tile, weights)
        halo_ref[...] = proj[TM - HALO:, 0:D_POOL]
        y = _dot(cat.astype(BF16), wout_v[...])
        y_ref[...] = y
        zn, _ = _ln(ALPHA * xt + gate * y)
        out_ref[...] = zn * lng_ref[layer:layer + 1, :] + lnb_ref[layer:layer + 1, :]

        @pl.when(tile == N_TILES - 1)
        def _():
            for g in hosted():
                g.pass_far()
            for g in hosted():
                g.wait_rest()
            for g in list(gathers()) + hosted():
                g.wait_sends()
            for cp in keeps() + hosted_own():
                cp.wait()

    row = lambda w: pl.BlockSpec((TM, w), lambda i: (i, 0))
    gather_sems = pltpu.SemaphoreType.DMA((2, GATHER_SEMS))
    seven = pltpu.SemaphoreType.DMA((7,))
    lead_in = list(exchange) if exchanging else [mod]
    lead_specs = [_const_spec(a.shape) for a in lead_in]
    if exchanging:
        lead_specs[1] = ANY
    extra_out_specs = [_const_spec((N_DEV, D_MODEL)), _const_spec((DEPTH, 3 * D_MODEL))] if exchanging else []
    extra_out_shape = [jax.ShapeDtypeStruct((N_DEV, D_MODEL), F32),
                       jax.ShapeDtypeStruct((DEPTH, 3 * D_MODEL), F32)] if exchanging else []
    extra_scratch = [pltpu.VMEM((N_DEV, 8, D_MODEL), F32), pltpu.VMEM((8, D_MODEL), F32),
                     pltpu.VMEM((N_DEV, 8, cols), F32), pltpu.VMEM((N_DEV, 8, cols), F32),
                     pltpu.VMEM(exchange[1].shape, F32), seven, seven, seven, seven] if exchanging else []
    host_scratch = [pltpu.SemaphoreType.DMA((n_host, GATHER_SEMS)), pltpu.SemaphoreType.DMA((n_host, GATHER_SEMS)),
                    pltpu.SemaphoreType.DMA((n_host,))] if n_host else []
    return pl.pallas_call(
        body,
        name=name,
        grid=(N_TILES,),
        in_specs=[ANY] + lead_specs + [_const_spec(s) for s in SMALL_SPECS]
                 + [_const_spec((DEPTH, D_MODEL)), _const_spec((DEPTH, D_MODEL)), ANY, ANY] + [ANY] * n_host,
        out_specs=[row(D_MODEL), row(D_MODEL), row(2 * D_SGU), ANY, ANY, ANY] + extra_out_specs + [ANY] * n_host,
        out_shape=[jax.ShapeDtypeStruct((SEQ, D_MODEL), F32), jax.ShapeDtypeStruct((SEQ, D_MODEL), F32),
                   jax.ShapeDtypeStruct((SEQ, 2 * D_SGU), F32), jax.ShapeDtypeStruct((SEQ, D_IN), F32),
                   jax.ShapeDtypeStruct((D_IN, D_MODEL), BF16), jax.ShapeDtypeStruct((D_MODEL, D_MODEL), BF16)]
                  + extra_out_shape + [jax.ShapeDtypeStruct((N_DEV,) + blk.shape, blk.dtype) for blk in host],
        scratch_shapes=[pltpu.VMEM((D_IN, D_MODEL), BF16), pltpu.VMEM((D_MODEL, D_MODEL), BF16),
                        pltpu.VMEM((SEQ, D_MODEL), BF16), pltpu.VMEM((2, SEQ, pair), F32),
                        pltpu.VMEM((2, TM, D_IN), F32), pltpu.VMEM((HALO, D_POOL), F32),
                        pltpu.VMEM((SEQ, D_MODEL), F32),
                        gather_sems, gather_sems, pltpu.SemaphoreType.DMA((4,)), pltpu.SemaphoreType.DMA((2,)),
                        pltpu.SemaphoreType.DMA((2,)), pltpu.SemaphoreType.DMA((2,))] + extra_scratch + host_scratch,
        compiler_params=pltpu.CompilerParams(dimension_semantics=("arbitrary",), vmem_limit_bytes=VMEM_LIMIT),
    )(x, *lead_in, *small, ln_g, ln_b, *mine, *host)


ADA_CHUNK = 256


def _grad_tail(dproj, h, cat, dy, small, dmod8, loss_lanes, w_ada, m_ada, v_ada, act_t, b_ada, m_bada, v_bada):
    shard_in, shard_out, shard_small = D_IN // N_DEV, D_MODEL // N_DEV, small.shape[2]
    cols = w_ada.shape[2]
    W_IN, W_OUT, SMALL = 0, 1, 2

    def body(dproj_hbm, h_hbm, cat_hbm, dy_hbm, small_hbm, dmod_ref, lanes_ref, wada_hbm, mada_hbm, vada_hbm,
             act_ref, bada_ref, mbada_ref, vbada_ref,
             gwin_ref, gwout_ref, stot_ref, loss_ref, gada_hbm, dada_hbm, nmada_hbm, nvada_hbm,
             gb_ref, db_ref, nmb_ref, nvb_ref,
             dproj_v, h_v, cat_v, dy_v, part_in, part_out, own_small, loss_src, loss_all, dmod_all, ada_in, ada_out,
             *rest):
        bufs, rest = rest[:13], rest[13:]
        load_sems, rs_sems = rest[0], rest[1:6]
        m_send, m_recv, g_send, g_recv, s_send, s_recv, ada_lsem, ada_ssem = rest[6:]
        mine = _index(_me())

        def update_ada():
            upper = (mine % 2) == 1

            def dmod_of(layer):
                rows = []
                for b in range(N_DEV):
                    r = dmod_all[b, pl.ds(4 * layer + mine // 2, 1), :]
                    rows.append(jnp.where(upper, r[:, cols:], r[:, :cols]))
                return jnp.concatenate(rows, axis=0)

            chunks = [(layer, c) for layer in range(DEPTH) for c in range(D_MODEL // ADA_CHUNK)]

            def loads(i):
                layer, c = chunks[i]
                rows = pl.ds(c * ADA_CHUNK, ADA_CHUNK)
                return [pltpu.make_async_copy(src.at[layer, rows], ada_in.at[i % 2, k], ada_lsem.at[i % 2, k])
                        for k, src in enumerate((wada_hbm, mada_hbm, vada_hbm))]

            def stores(i):
                layer, c = chunks[i]
                rows = pl.ds(c * ADA_CHUNK, ADA_CHUNK)
                return [pltpu.make_async_copy(ada_out.at[i % 2, k], dst.at[layer, rows], ada_ssem.at[i % 2, k])
                        for k, dst in enumerate((gada_hbm, dada_hbm, nmada_hbm, nvada_hbm))]

            for cp in loads(0):
                cp.start()
            dmods = {}
            for i, (layer, c) in enumerate(chunks):
                if i + 1 < len(chunks):
                    for cp in loads(i + 1):
                        cp.start()
                for cp in loads(i):
                    cp.wait()
                if i >= 2:
                    for cp in stores(i - 2):
                        cp.wait()
                if layer not in dmods:
                    dmods[layer] = dmod_of(layer)
                act = act_ref[pl.ds(c * ADA_CHUNK, ADA_CHUNK), :]
                g = act[:, 0:1] * dmods[layer][0:1, :]
                for b in range(1, N_DEV):
                    g = g + act[:, b:b + 1] * dmods[layer][b:b + 1, :]
                slot = i % 2
                delta, new_m, new_v = _adamw_math(ada_in[slot, 0], g, ada_in[slot, 1], ada_in[slot, 2])
                ada_out[slot, 0] = g
                ada_out[slot, 1] = delta
                ada_out[slot, 2] = new_m
                ada_out[slot, 3] = new_v
                for cp in stores(i):
                    cp.start()
            for i in (len(chunks) - 2, len(chunks) - 1):
                for cp in stores(i):
                    cp.wait()

            total = dmod_all[0]
            for b in range(1, N_DEV):
                total = total + dmod_all[b]
            width = total.shape[1]
            for layer in range(DEPTH):
                for q in range(4):
                    gb_ref[layer:layer + 1, q * width:(q + 1) * width] = total[4 * layer + q:4 * layer + q + 1, :]
            db_ref[...], nmb_ref[...], nvb_ref[...] = _adamw_math(bada_ref[...], gb_ref[...], mbada_ref[...],
                                                                  vbada_ref[...])

        order = (ACROSS_BOTH, ACROSS_X, ACROSS_Y, None)
        chips = [_ChipReduceScatter._chip(r) for r in order]
        loads = [pltpu.make_async_copy(s, d, load_sems.at[n]) for n, (s, d) in enumerate(
            ((cat_hbm, cat_v), (dy_hbm, dy_v), (h_hbm, h_v)))]
        loads += [pltpu.make_async_copy(dproj_hbm.at[:, pl.ds(pl.multiple_of(chip * 2 * shard_in, 2 * shard_in),
                                                             2 * shard_in)], dproj_v.at[n], load_sems.at[3 + n])
                  for n, chip in enumerate(chips)]
        for cp in loads:
            cp.start()
        arrays = [dict(part=part_in, out=gwin_ref, staged=False, sib=bufs[0], snd=bufs[1], rcv=bufs[2], relay=bufs[3]),
                  dict(part=part_out, out=gwout_ref, staged=False, sib=bufs[4], snd=bufs[5], rcv=bufs[6],
                       relay=bufs[7]),
                  dict(part=small_hbm, out=own_small, staged=True, stage=bufs[8], sib=bufs[9], snd=bufs[10],
                       rcv=bufs[11], relay=bufs[12])]
        scatter = _ChipReduceScatter(arrays, *rs_sems)
        scatter.start([SMALL])
        dmod_all[mine] = dmod_ref[...]
        dmod_copies = _direct_exchange(lambda p: dmod_ref, lambda m: dmod_all.at[_index(m)], m_send, m_recv)
        loss_src[...] = jnp.full(loss_src.shape, (0.5 / D_MODEL) * jnp.sum(lanes_ref[...]), F32)
        loss_all[mine] = loss_src[...]
        loss_copies = _direct_exchange(lambda p: loss_src, lambda m: loss_all.at[_index(m)], s_send, s_recv)

        loads[0].wait()
        loads[1].wait()
        for blk in range(2):
            res = _dot_tn(cat_v[:, blk * 512:(blk + 1) * 512], dy_v[...]).astype(BF16)
            for s in range(4):
                part_out[2 * blk + s // 2, s % 2] = res[s * shard_out:(s + 1) * shard_out]
        scatter.start([W_OUT])
        scatter.exchange([SMALL])

        gather = _TwoLevelGather(stot_ref, g_send, g_recv)
        loads[2].wait()
        for n, chip in enumerate(chips):
            loads[3 + n].wait()
            res = _dot_tn(dproj_v[n], h_v[...]).astype(BF16)
            part_in[chip, 0] = res[:shard_in]
            part_in[chip, 1] = res[shard_in:]
            scatter.start([W_IN], chips=[chip])
            if n == 0:
                scatter.exchange([W_OUT])
                scatter.fold([SMALL])
            if n == 1:
                scatter.send_far([W_IN])
                scatter.fold([W_OUT])
                scatter.finish([SMALL])
                stot_ref[mine] = own_small[...]
                gather.send_mine()
            if n == 2:
                scatter.send_near(ACROSS_X, [W_IN])
                gather.relay()
            if n == 3:
                scatter.send_near(ACROSS_Y, [W_IN])
        scatter.fold([W_IN])
        scatter.finish([W_OUT])
        _wait_direct(dmod_copies)
        update_ada()
        gather.pass_on()
        gather.wait_rest()
        _wait_direct(loss_copies)
        total = loss_all[0]
        for j in range(1, N_DEV):
            total = total + loss_all[j]
        loss_ref[...] = total
        scatter.finish([W_IN])
        gather.wait_sends()
        scatter.wait_sends()

    buffers = _ChipReduceScatter.buffers
    comm_scratch = (buffers(shard_in, D_MODEL, BF16, staged=False) + buffers(shard_out, D_MODEL, BF16, staged=False)
                    + buffers(shard_small, 128, F32))
    comm_scratch += [pltpu.SemaphoreType.DMA((7,))] + _ChipReduceScatter.semaphores(3)
    comm_scratch += [pltpu.SemaphoreType.DMA((n,)) for n in (7, 7, GATHER_SEMS, GATHER_SEMS, 7, 7)]
    comm_scratch += [pltpu.SemaphoreType.DMA((2, 3)), pltpu.SemaphoreType.DMA((2, 4))]
    return pl.pallas_call(
        body,
        name="grad_tail",
        in_specs=[ANY] * 5 + [VMEM, VMEM] + [ANY] * 3 + [VMEM] * 4,
        out_specs=[VMEM] * 4 + [ANY] * 4 + [VMEM] * 4,
        out_shape=[jax.ShapeDtypeStruct((shard_in, D_MODEL), F32), jax.ShapeDtypeStruct((shard_out, D_MODEL), F32),
                   jax.ShapeDtypeStruct((N_DEV, shard_small, 128), F32), jax.ShapeDtypeStruct((8, 128), F32)]
                  + [jax.ShapeDtypeStruct(w_ada.shape, F32)] * 4 + [jax.ShapeDtypeStruct(b_ada.shape, F32)] * 4,
        scratch_shapes=[pltpu.VMEM((4, SEQ, 2 * shard_in), BF16), pltpu.VMEM(h.shape, BF16), pltpu.VMEM(cat.shape, BF16),
                        pltpu.VMEM(dy.shape, BF16), pltpu.VMEM((4, 2, shard_in, D_MODEL), BF16),
                        pltpu.VMEM((4, 2, shard_out, D_MODEL), BF16), pltpu.VMEM((shard_small, 128), F32),
                        pltpu.VMEM((8, 128), F32), pltpu.VMEM((N_DEV, 8, 128), F32),
                        pltpu.VMEM((N_DEV,) + dmod8.shape, F32), pltpu.VMEM((2, 3, ADA_CHUNK, cols), F32),
                        pltpu.VMEM((2, 4, ADA_CHUNK, cols), F32)] + comm_scratch,
        compiler_params=pltpu.CompilerParams(vmem_limit_bytes=VMEM_LIMIT),
    )(dproj, h, cat, dy, small, dmod8, loss_lanes, w_ada, m_ada, v_ada, act_t, b_ada, m_bada, v_bada)


SMALL_NAMES = ("w_pool", "w_sgu", "pool_scale", "sgu_ln_g", "sgu_ln_b", "b_sgu", "ln_g", "ln_b")
SMALL_ROWS = (512, 512, 4, 4, 4, 4, 8, 8)


def _adamw_small(g_packed, ws, ms, vs, name):
    n = len(SMALL_NAMES)

    def body(*refs):
        g_refs, refs = refs[:DEPTH], refs[DEPTH:]
        w_refs, m_refs, v_refs = refs[:n], refs[n:2 * n], refs[2 * n:3 * n]
        outs = refs[3 * n:]

        def update(p, at, g):
            delta, new_m, new_v = _adamw_math(w_refs[p][at], g, m_refs[p][at], v_refs[p][at])
            outs[p][at] = g
            outs[n + p][at] = delta
            outs[2 * n + p][at] = new_m
            outs[3 * n + p][at] = new_v

        row = 0
        for p, r in enumerate(SMALL_ROWS):
            shape = ws[p].shape
            for layer in range(DEPTH):
                first, g_ref = row, g_refs[layer]
                if len(shape) == 4:
                    for k in range(shape[1]):
                        update(p, (layer, k), g_ref[first + k * shape[2]:first + (k + 1) * shape[2], :])
                elif len(shape) == 3:
                    update(p, (layer,), g_ref[first:first + r, :])
                else:
                    g = jnp.concatenate([g_ref[first + k:first + k + 1, :] for k in range(r)], axis=1)
                    update(p, (slice(layer, layer + 1), slice(None)), g)
            row += r

    res = pl.pallas_call(
        body,
        name=name,
        out_shape=[jax.ShapeDtypeStruct(w.shape, F32) for w in ws] * 4,
        compiler_params=pltpu.CompilerParams(vmem_limit_bytes=VMEM_LIMIT),
    )(*g_packed, *ws, *ms, *vs)
    return res[:n], res[n:2 * n], res[2 * n:3 * n], res[3 * n:]


def kernel(x, c, w_ada, b_ada, w_in, w_pool, pool_scale, sgu_ln_g, sgu_ln_b, w_sgu, b_sgu, w_out, ln_g, ln_b, loss_target, m_w_ada, m_b_ada, m_w_in, m_w_pool, m_pool_scale, m_sgu_ln_g, m_sgu_ln_b, m_w_sgu, m_b_sgu, m_w_out, m_ln_g, m_ln_b, v_w_ada, v_b_ada, v_w_in, v_w_pool, v_pool_scale, v_sgu_ln_g, v_sgu_ln_b, v_w_sgu, v_b_sgu, v_w_out, v_ln_g, v_ln_b):
    small_w = dict(w_pool=w_pool, w_sgu=w_sgu, pool_scale=pool_scale, sgu_ln_g=sgu_ln_g, sgu_ln_b=sgu_ln_b,
                   b_sgu=b_sgu, ln_g=ln_g, ln_b=ln_b)
    small_m = dict(w_pool=m_w_pool, w_sgu=m_w_sgu, pool_scale=m_pool_scale, sgu_ln_g=m_sgu_ln_g,
                   sgu_ln_b=m_sgu_ln_b, b_sgu=m_b_sgu, ln_g=m_ln_g, ln_b=m_ln_b)
    small_v = dict(w_pool=v_w_pool, w_sgu=v_w_sgu, pool_scale=v_pool_scale, sgu_ln_g=v_sgu_ln_g,
                   sgu_ln_b=v_sgu_ln_b, b_sgu=v_b_sgu, ln_g=v_ln_g, ln_b=v_ln_b)

    wint_loc = jnp.transpose(w_in, (0, 2, 1)).astype(BF16)
    wout_loc = w_out.astype(BF16)
    small = (w_pool, pool_scale, sgu_ln_g, sgu_ln_b, w_sgu, jnp.transpose(b_sgu, (0, 2, 1)))
    out, y, cdf, proj, wint0, wout0, act_all, mod, wint1, wout1 = _gathered_layer(
        0, x[0], small, ln_g, ln_b, [wint_loc[0], wout_loc[0]], "layer_fwd_0", exchange=(c, w_ada, b_ada),
        host=[wint_loc[1], wout_loc[1]])
    w_int, w_outf = [wint0, wint1.reshape(D_IN, D_MODEL)], [wout0, wout1.reshape(D_MODEL, D_MODEL)]
    acts, cur = [(x[0], proj, y, cdf)], out
    for l in range(1, DEPTH):
        out, proj, y, cdf = _layer_forward(l, cur, mod, w_int[l], w_outf[l], small, ln_g, ln_b, f"layer_fwd_{l}")
        acts.append((cur, proj, y, cdf))
        cur = out

    shard_in, shard_out = D_IN // N_DEV, D_MODEL // N_DEV
    a, b = cur, loss_target[0]
    loss_lanes, carry, pending, small_above = None, (), [], None
    g_w_in_t, g_w_out, small_tot = [None] * DEPTH, [None] * DEPTH, [None] * DEPTH
    shards_of = lambda pack: pack.reshape(4, 2, PACK_ROWS // N_DEV, 128)
    for l in reversed(range(DEPTH)):
        dx, dproj, h, cat, dy, small_grads, dmod8, lanes, *reduced = _layer_backward(
            l, a, b, *acts[l], mod, w_int[l], w_outf[l], small, ln_g, l == DEPTH - 1, f"layer_bwd_{l}",
            carry=carry, reduce=pending, allreduce=small_above)
        if reduced:
            g_w_in_t[l + 1], g_w_out[l + 1], small_tot[l + 1] = reduced
        if l == DEPTH - 1:
            loss_lanes = lanes
        if l > 0:
            g_in, g_out = _grad_matmuls(dproj, h, cat, dy, f"grad_w_{l}")
            pending = [g_in.reshape(4, 2, shard_in, D_MODEL), g_out.reshape(4, 2, shard_out, D_MODEL)]
        carry, small_above = (dmod8,), shards_of(small_grads)
        a = b = dx
    grad_x = a[None]

    (g_w_in_t[0], g_w_out[0], small_tot[0], loss_tile, g_w_ada, d_w_ada, nm_w_ada, nv_w_ada,
     g_b_ada, d_b_ada, nm_b_ada, nv_b_ada) = _grad_tail(
        dproj, h, cat, dy, small_above, dmod8, loss_lanes,
        w_ada, m_w_ada, v_w_ada, jnp.transpose(act_all), b_ada, m_b_ada, v_b_ada)
    loss = loss_tile[0, 0]

    flat = lambda t: t.reshape(-1, t.shape[-1])
    to_t = lambda t: flat(jnp.transpose(t, (0, 2, 1)))
    from_t = lambda t: jnp.transpose(t.reshape(DEPTH, shard_in, D_MODEL), (0, 2, 1))
    g_w_in, d_w_in, nm_w_in, nv_w_in = [from_t(t) for t in _adamw(to_t(w_in), g_w_in_t, to_t(m_w_in), to_t(v_w_in),
                                                                  shard_in // 2, "adamw_w_in")]
    gwout, d_w_out, nm_w_out, nv_w_out = [t.reshape(w_out.shape) for t in _adamw(
        flat(w_out), g_w_out, flat(m_w_out), flat(v_w_out), shard_out, "adamw_w_out")]
    small_out = _adamw_small([t.reshape(PACK_ROWS, 128) for t in small_tot], [small_w[n] for n in SMALL_NAMES],
                             [small_m[n] for n in SMALL_NAMES], [small_v[n] for n in SMALL_NAMES], "adamw_small")
    gs, ds, ms, vs = [dict(zip(SMALL_NAMES, group)) for group in small_out]

    def ordered(w_ada_, b_ada_, w_in_, small, w_out_):
        return (w_ada_, b_ada_, w_in_, small["w_pool"], small["pool_scale"], small["sgu_ln_g"], small["sgu_ln_b"],
                small["w_sgu"], small["b_sgu"], w_out_, small["ln_g"], small["ln_b"])

    return (loss, grad_x,
            *ordered(g_w_ada, g_b_ada, g_w_in, gs, gwout),
            *ordered(d_w_ada, d_b_ada, d_w_in, ds, d_w_out),
            *ordered(nm_w_ada, nm_b_ada, nm_w_in, ms, nm_w_out),
            *ordered(nv_w_ada, nv_b_ada, nv_w_in, vs, nv_w_out))
```

```python
import jax
import jax.numpy as jnp
from jax import lax
from jax.experimental import pallas as pl
from jax.experimental.pallas import tpu as pltpu

F32 = jnp.float32
BF16 = jnp.bfloat16

D_MODEL = 1024
SEQ = 2048
DEPTH = 2
D_POOL = 512
D_SGU = 512
D_IN = 2560
N_GROUPS = 4
GROUP = 128
N_HEADS = 4
HEAD = 128
CHUNK = 128
WINDOWS = (2, 4, 8, 16)
ALPHA = (2.0 * DEPTH) ** 0.25
LN_EPS = 1e-5
N_DEV = 8

ADAM_LR = 0.001
ADAM_B1 = 0.9
ADAM_B2 = 0.999
ADAM_EPS = 1e-08
ADAM_WD = 0.01
ADAM_STEP = 10

TM = 256
HALO = 16
N_TILES = SEQ // TM
VMEM_LIMIT = 60 * 1024 * 1024

ROW_WPOOL = 0
ROW_WSGU = 512
ROW_PSCALE = 1024
ROW_SLNG = 1028
ROW_SLNB = 1032
ROW_BSGU = 1036
ROW_LNG = 1040
ROW_LNB = 1048
PACK_ROWS = 1088
DMOD_COLS = DEPTH * 3 * D_MODEL // 8

SQRT_HALF = 0.7071067811865476
INV_SQRT_2PI = 0.3989422804014327


def _ln(x):
    mu = jnp.mean(x, axis=-1, keepdims=True)
    xc = x - mu
    var = jnp.mean(xc * xc, axis=-1, keepdims=True)
    rstd = lax.rsqrt(var + LN_EPS)
    return xc * rstd, rstd


def _ln_bwd(dxn, xn, rstd):
    m1 = jnp.mean(dxn, axis=-1, keepdims=True)
    m2 = jnp.mean(dxn * xn, axis=-1, keepdims=True)
    return rstd * (dxn - m1 - xn * m2)


def _normal_cdf(x):
    return 0.5 * (1.0 + lax.erf(x * SQRT_HALF))


def _gelu_parts(x, cdf, with_grad):
    if not with_grad:
        return x * cdf, None
    return x * cdf, cdf + x * (INV_SQRT_2PI * jnp.exp(-0.5 * x * x))


def _silu_parts(x):
    s = jax.nn.sigmoid(x)
    return x * s, s * (1.0 + x * (1.0 - s))


def _dot(a, b):
    return lax.dot_general(a, b, (((1,), (0,)), ((), ())), preferred_element_type=F32)


def _dot_nt(a, b):
    return lax.dot_general(a, b, (((1,), (1,)), ((), ())), preferred_element_type=F32)


def _dot_tn(a, b):
    return lax.dot_general(a, b, (((0,), (0,)), ((), ())), preferred_element_type=F32)


def _row_index(tile):
    return tile * TM + lax.broadcasted_iota(jnp.int32, (TM, 1), 0)


def _window_sums(ext, forward):
    n = TM + HALO
    cur = ext
    outs = []
    for g in range(N_GROUPS):
        step = 1 << g
        cur = cur + pltpu.roll(cur, step if forward else n - step, 0)
        rows = cur[HALO:, :GROUP] if forward else cur[:TM, :GROUP]
        outs.append(rows)
        cur = cur[:, GROUP:] if g + 1 < N_GROUPS else None
    return outs


def _inverse_counts(rows):
    return [1.0 / jnp.minimum(rows + 1, w).astype(F32) for w in WINDOWS]


def _tril_bf16(w):
    t = lax.broadcasted_iota(jnp.int32, (CHUNK, CHUNK), 0)
    s = lax.broadcasted_iota(jnp.int32, (CHUNK, CHUNK), 1)
    return jnp.where(t >= s, w, 0.0).astype(BF16)


class _MixWeights:
    def __init__(self, layer, wpool_ref, pscale_ref, slng_ref, slnb_ref, wsgu_ref, bsgut_ref):
        self.layer = layer
        self.wpool_ref, self.pscale_ref, self.slng_ref, self.slnb_ref = wpool_ref, pscale_ref, slng_ref, slnb_ref
        self.wsgu_ref, self.bsgut_ref = wsgu_ref, bsgut_ref

    def pool(self, g):
        return self.wpool_ref[self.layer, g].astype(BF16)

    def pool_scale(self, g):
        return self.pscale_ref[self.layer:self.layer + 1, g * GROUP:(g + 1) * GROUP]

    def ln_gain(self, h):
        return self.slng_ref[self.layer, h:h + 1, :]

    def ln_bias(self, h):
        return self.slnb_ref[self.layer, h:h + 1, :]

    def mix(self, h):
        return _tril_bf16(self.wsgu_ref[self.layer, h])

    def mix_bias(self, h):
        return self.bsgut_ref[self.layer, :, h:h + 1]


SMALL_SPECS = ((DEPTH, N_GROUPS, GROUP, GROUP), (DEPTH, D_POOL), (DEPTH, N_HEADS, HEAD), (DEPTH, N_HEADS, HEAD),
               (DEPTH, N_HEADS, CHUNK, CHUNK), (DEPTH, CHUNK, N_HEADS))


def _mix_forward(proj, halo, tile, w, cdf=None):
    keep = cdf is not None
    rows = _row_index(tile)
    inv_counts = _inverse_counts(rows)
    xa = proj[:, 0:D_POOL]
    ga = proj[:, D_POOL:2 * D_POOL]
    sums = _window_sums(jnp.concatenate([halo, xa], axis=0), True)
    ga_act, ga_grad = _silu_parts(ga)
    pooled, pw, ya = [], [], []
    for g in range(N_GROUPS):
        sl = slice(g * GROUP, (g + 1) * GROUP)
        p = (sums[g] * inv_counts[g] - xa[:, sl]).astype(BF16)
        q = _dot(p, w.pool(g))
        pooled.append(p)
        pw.append(q)
        ya.append(q * w.pool_scale(g) * ga_act[:, sl])

    u = proj[:, 2 * D_POOL:2 * D_POOL + D_SGU]
    v = proj[:, 2 * D_POOL + D_SGU:2 * D_POOL + 2 * D_SGU]
    gb = proj[:, 2 * D_POOL + 2 * D_SGU:]
    gb_act, gb_grad = _silu_parts(gb)
    if cdf is None:
        cdf = jnp.concatenate([_normal_cdf(u), _normal_cdf(v)], axis=1)
    u_act, u_grad = _gelu_parts(u, cdf[:, :D_SGU], keep)
    v_act, v_grad = _gelu_parts(v, cdf[:, D_SGU:], keep)
    vn, vrstd, vln, mixed, yb = [], [], [], [], []
    for h in range(N_HEADS):
        sl = slice(h * HEAD, (h + 1) * HEAD)
        n_h, r_h = _ln(v_act[:, sl])
        l_h = (n_h * w.ln_gain(h) + w.ln_bias(h)).astype(BF16)
        w_h = w.mix(h)
        bias = w.mix_bias(h)
        m_h = jnp.concatenate(
            [_dot(w_h, l_h[k * CHUNK:(k + 1) * CHUNK]) + bias for k in range(TM // CHUNK)], axis=0)
        vn.append(n_h)
        vrstd.append(r_h)
        vln.append(l_h)
        mixed.append(m_h)
        yb.append(u_act[:, sl] * m_h * gb_act[:, sl])
    cat = jnp.concatenate(ya + yb, axis=1)
    if not keep:
        return cat, cdf
    return cat, dict(inv_counts=inv_counts, ga_act=ga_act, ga_grad=ga_grad, pooled=pooled, pw=pw, u_grad=u_grad,
                     v_grad=v_grad, u_act=u_act, gb_act=gb_act, gb_grad=gb_grad, vn=vn, vrstd=vrstd, vln=vln,
                     mixed=mixed)


def _const_spec(shape):
    nd = len(shape)
    return pl.BlockSpec(shape, lambda i: (0,) * nd)


VEC_LNG, VEC_LNB, VEC_POOL, VEC_SGU, VEC_SHIFT, VEC_SCALE, VEC_GATE, VEC_LOSS = range(8)


def _layer_backward(layer, a, b, x, proj, y, cdf, mod, w_int, w_outf, small, ln_g, is_last, name, carry=(),
                    reduce=()):
    n_red, n_carry = len(reduce), len(carry)
    base = layer * PACK_ROWS

    def body(a_ref, b_ref, x_ref, proj_ref, prev_ref, y_ref, cdf_ref, mod_ref, wint_ref, wout_ref, wpool_ref,
             pscale_ref, slng_ref, slnb_ref, wsgu_ref, bsgut_ref, lng_ref, *rest):
        weights = _MixWeights(layer, wpool_ref, pscale_ref, slng_ref, slnb_ref, wsgu_ref, bsgut_ref)
        carry_refs, rest = rest[:n_carry], rest[n_carry:]
        part_refs, rest = rest[:n_red], rest[n_red:]
        dx_ref, dproj_ref, h_ref, cat_ref, dy_ref, small_ref, dmod_ref, loss_ref = rest[:8]
        shard_refs, rest = rest[8:8 + n_red], rest[8 + n_red:]
        vec_ref, dmix_ref, halo_ref = rest[:3]
        step = pl.program_id(0)
        tile = N_TILES - 1 - step

        def scatter():
            bufs, sems = rest[3:3 + 5 * n_red], rest[3 + 5 * n_red:]
            arrays = [dict(part=part_refs[n], out=shard_refs[n], staged=True, stage=bufs[5 * n], sib=bufs[5 * n + 1],
                           snd=bufs[5 * n + 2], rcv=bufs[5 * n + 3], relay=bufs[5 * n + 4]) for n in range(n_red)]
            return _ChipReduceScatter(arrays, *sems)

        @pl.when(step == 0)
        def _():
            small_ref[...] = jnp.zeros_like(small_ref)
            dmod_ref[...] = jnp.zeros_like(dmod_ref)
            vec_ref[...] = jnp.zeros_like(vec_ref)
            dmix_ref[...] = jnp.zeros_like(dmix_ref)
            halo_ref[...] = jnp.zeros_like(halo_ref)
            if n_red:
                scatter().start()

        if n_red:
            @pl.when(step == 1)
            def _():
                scatter().exchange()

            @pl.when(step == N_TILES // 2)
            def _():
                scatter().fold()

        def acc(row, lo, val):
            hi = lo + val.shape[1]
            vec_ref[row:row + 1, lo:hi] += jnp.sum(val, axis=0, keepdims=True)

        xt = x_ref[...]
        yt = y_ref[...]
        shift = mod_ref[layer:layer + 1, 0:D_MODEL]
        scale = mod_ref[layer:layer + 1, D_MODEL:2 * D_MODEL]
        gate = mod_ref[layer:layer + 1, 2 * D_MODEL:]
        ln_gain = lng_ref[layer:layer + 1, :]

        zn, zrstd = _ln(ALPHA * xt + gate * yt)
        if is_last:
            diff = a_ref[...] - b_ref[...]
            acc(VEC_LOSS, 0, diff * diff)
            dout = diff * (1.0 / D_MODEL)
        else:
            dout = a_ref[...]
        acc(VEC_LNG, 0, dout * zn)
        acc(VEC_LNB, 0, dout)
        dz = _ln_bwd(dout * ln_gain, zn, zrstd)
        acc(VEC_GATE, 0, dz * yt)
        dy = (dz * gate).astype(BF16)
        dy_ref[...] = dy
        dcat = _dot_nt(dy, wout_ref[...])

        proj = proj_ref[...]
        prev = jnp.where(tile > 0, prev_ref[...], 0.0)
        cat, k = _mix_forward(proj, prev, tile, weights, cdf_ref[...])
        cat_ref[...] = cat.astype(BF16)

        dga, dq = [], []
        for g in range(N_GROUPS):
            sl = slice(g * GROUP, (g + 1) * GROUP)
            pscale = weights.pool_scale(g)
            dya = dcat[:, sl]
            dyp = dya * k["ga_act"][:, sl]
            dga.append(dya * k["pw"][g] * pscale * k["ga_grad"][:, sl])
            acc(VEC_POOL, g * GROUP, dyp * k["pw"][g])
            dpw = (dyp * pscale).astype(BF16)
            rows = pl.ds(base + ROW_WPOOL + g * GROUP, GROUP)
            small_ref[rows, :] += _dot_tn(k["pooled"][g], dpw)
            dq.append(_dot_nt(dpw, weights.pool(g)))
        dpooled = jnp.concatenate(dq, axis=1)
        scaled = jnp.concatenate([dq[g] * k["inv_counts"][g] for g in range(N_GROUPS)], axis=1)
        sums = _window_sums(jnp.concatenate([scaled, halo_ref[...]], axis=0), False)
        halo_ref[...] = scaled[0:HALO]
        dxa = jnp.concatenate(sums, axis=1) - dpooled

        du, dv, dgb = [], [], []
        for h in range(N_HEADS):
            sl = slice(h * HEAD, (h + 1) * HEAD)
            dyb = dcat[:, D_POOL + h * HEAD:D_POOL + (h + 1) * HEAD]
            m_h = k["mixed"][h]
            ug = k["u_act"][:, sl] * dyb
            du.append(dyb * m_h * k["gb_act"][:, sl] * k["u_grad"][:, sl])
            dgb.append(ug * m_h * k["gb_grad"][:, sl])
            dmixed = ug * k["gb_act"][:, sl]
            dmixed_bf = dmixed.astype(BF16)
            w_h = weights.mix(h)
            dvln_parts = []
            dmix_sum = dmix_ref[h]
            wsgu_rows = pl.ds(base + ROW_WSGU + h * CHUNK, CHUNK)
            dws = small_ref[wsgu_rows, :]
            for c in range(TM // CHUNK):
                cs = slice(c * CHUNK, (c + 1) * CHUNK)
                dmix_sum = dmix_sum + dmixed[cs]
                dws = dws + _dot_nt(dmixed_bf[cs], k["vln"][h][cs])
                dvln_parts.append(_dot_tn(w_h, dmixed_bf[cs]))
            dmix_ref[h] = dmix_sum
            small_ref[wsgu_rows, :] = dws
            dvln = jnp.concatenate(dvln_parts, axis=0)
            acc(VEC_SGU, h * HEAD, dvln * k["vn"][h])
            acc(VEC_SGU, D_SGU + h * HEAD, dvln)
            dvv = _ln_bwd(dvln * weights.ln_gain(h), k["vn"][h], k["vrstd"][h])
            dv.append(dvv * k["v_grad"][:, sl])

        dproj = jnp.concatenate([dxa] + dga + du + dv + dgb, axis=1).astype(BF16)
        dproj_ref[...] = dproj
        dh = _dot(dproj, wint_ref[...])

        xn, xrstd = _ln(xt)
        h_ref[...] = (xn * (1.0 + scale) + shift).astype(BF16)
        acc(VEC_SCALE, 0, dh * xn)
        acc(VEC_SHIFT, 0, dh)
        dx_ref[...] = _ln_bwd(dh * (1.0 + scale), xn, xrstd) + ALPHA * dz

        @pl.when(step == N_TILES - 1)
        def _():
            def put(row0, vec_row, lo, n):
                for r in range(n):
                    small_ref[base + row0 + r:base + row0 + r + 1, :] = (
                        vec_ref[vec_row:vec_row + 1, lo + r * 128:lo + (r + 1) * 128])

            put(ROW_PSCALE, VEC_POOL, 0, 4)
            put(ROW_SLNG, VEC_SGU, 0, 4)
            put(ROW_SLNB, VEC_SGU, D_SGU, 4)
            put(ROW_LNG, VEC_LNG, 0, 8)
            put(ROW_LNB, VEC_LNB, 0, 8)
            ones = jnp.ones((8, HEAD), F32)
            t = lax.broadcasted_iota(jnp.int32, (CHUNK, CHUNK), 0)
            s = lax.broadcasted_iota(jnp.int32, (CHUNK, CHUNK), 1)
            for h in range(N_HEADS):
                bias_rows = lax.dot_general(ones, dmix_ref[h], (((1,), (1,)), ((), ())),
                                            preferred_element_type=F32, precision=lax.Precision.HIGHEST)
                small_ref[base + ROW_BSGU + h:base + ROW_BSGU + h + 1, :] = bias_rows[0:1]
                rows = pl.ds(base + ROW_WSGU + h * CHUNK, CHUNK)
                small_ref[rows, :] = jnp.where(t >= s, small_ref[rows, :], 0.0)
            pieces = ((0, VEC_SHIFT, 0, 768),
                      (1, VEC_SHIFT, 768, 256), (1, VEC_SCALE, 0, 512),
                      (2, VEC_SCALE, 512, 512), (2, VEC_GATE, 0, 256),
                      (3, VEC_GATE, 256, 768))
            filled = [0] * 4
            for q, vec_row, lo, n in pieces:
                row = 4 * layer + q
                dmod_ref[row:row + 1, filled[q]:filled[q] + n] = vec_ref[vec_row:vec_row + 1, lo:lo + n]
                filled[q] += n
            if n_carry:
                for other in range(layer + 1, DEPTH):
                    rows = pl.ds(other * PACK_ROWS, PACK_ROWS)
                    small_ref[rows, :] = carry_refs[0][rows, :]
                    dmod_ref[4 * other:4 * other + 4, :] = carry_refs[1][4 * other:4 * other + 4, :]
            loss_ref[...] = vec_ref[VEC_LOSS:VEC_LOSS + 1, :]
            if n_red:
                scatter().finish()
                scatter().wait_sends()

    rev = lambda w: pl.BlockSpec((TM, w), lambda i: (N_TILES - 1 - i, 0))
    prev_spec = pl.BlockSpec(
        (HALO, D_POOL), lambda i: (jnp.maximum((N_TILES - 1 - i) * (TM // HALO) - 1, 0), 0))
    comm_scratch = []
    for p in reduce:
        comm_scratch += _ChipReduceScatter.buffers(p.shape[2], p.shape[3], p.dtype)
    if n_red:
        comm_scratch += _ChipReduceScatter.semaphores(n_red)
    return pl.pallas_call(
        body,
        name=name,
        grid=(N_TILES,),
        in_specs=[rev(D_MODEL), rev(D_MODEL) if is_last else pl.BlockSpec((TM, D_MODEL), lambda i: (0, 0)),
                  rev(D_MODEL), rev(D_IN), prev_spec, rev(D_MODEL), rev(2 * D_SGU),
                  _const_spec((DEPTH, 3 * D_MODEL)), _const_spec((D_IN, D_MODEL)), _const_spec((D_MODEL, D_MODEL))]
                 + [_const_spec(s) for s in SMALL_SPECS] + [_const_spec((DEPTH, D_MODEL))]
                 + [_const_spec(c.shape) for c in carry] + [ANY] * n_red,
        out_specs=[rev(D_MODEL), rev(D_IN), rev(D_MODEL), rev(D_MODEL), rev(D_MODEL),
                   _const_spec((DEPTH * PACK_ROWS, 128)), _const_spec((8, DMOD_COLS)), _const_spec((1, D_MODEL))]
                  + [_const_spec(p.shape[2:]) for p in reduce],
        out_shape=[jax.ShapeDtypeStruct((SEQ, D_MODEL), F32), jax.ShapeDtypeStruct((SEQ, D_IN), BF16),
                   jax.ShapeDtypeStruct((SEQ, D_MODEL), BF16), jax.ShapeDtypeStruct((SEQ, D_MODEL), BF16),
                   jax.ShapeDtypeStruct((SEQ, D_MODEL), BF16), jax.ShapeDtypeStruct((DEPTH * PACK_ROWS, 128), F32),
                   jax.ShapeDtypeStruct((8, DMOD_COLS), F32), jax.ShapeDtypeStruct((1, D_MODEL), F32)]
                  + [jax.ShapeDtypeStruct(p.shape[2:], F32) for p in reduce],
        scratch_shapes=[pltpu.VMEM((8, D_MODEL), F32), pltpu.VMEM((N_HEADS, CHUNK, HEAD), F32),
                        pltpu.VMEM((HALO, D_POOL), F32)] + comm_scratch,
        compiler_params=pltpu.CompilerParams(dimension_semantics=("arbitrary",), vmem_limit_bytes=VMEM_LIMIT),
    )(a, b, x, proj, proj, y, cdf, mod, w_int, w_outf, *small, ln_g, *carry, *reduce)


def _grad_matmuls(dproj, h, cat, dy, name):
    in_cols, out_cols = D_IN // 4, D_MODEL // 2
    in_steps = D_IN // in_cols

    def body(dproj_ref, h_ref, cat_ref, dy_ref, gin_ref, gout_ref):
        step = pl.program_id(0)

        @pl.when(step < in_steps)
        def _():
            gin_ref[...] = _dot_tn(dproj_ref[...], h_ref[...]).astype(BF16)

        @pl.when(step >= in_steps)
        def _():
            gout_ref[...] = _dot_tn(cat_ref[...], dy_ref[...]).astype(BF16)

    in_block = lambda j: jnp.minimum(j, in_steps - 1)
    out_block = lambda j: jnp.maximum(j - in_steps, 0)
    return pl.pallas_call(
        body,
        name=name,
        grid=(in_steps + D_MODEL // out_cols,),
        in_specs=[pl.BlockSpec((SEQ, in_cols), lambda j: (0, in_block(j))), _const_spec((SEQ, D_MODEL)),
                  pl.BlockSpec((SEQ, out_cols), lambda j: (0, out_block(j))), _const_spec((SEQ, D_MODEL))],
        out_specs=[pl.BlockSpec((in_cols, D_MODEL), lambda j: (in_block(j), 0)),
                   pl.BlockSpec((out_cols, D_MODEL), lambda j: (out_block(j), 0))],
        out_shape=[jax.ShapeDtypeStruct((D_IN, D_MODEL), BF16), jax.ShapeDtypeStruct((D_MODEL, D_MODEL), BF16)],
        compiler_params=pltpu.CompilerParams(dimension_semantics=("arbitrary",), vmem_limit_bytes=VMEM_LIMIT),
    )(dproj, h, cat, dy)


def _adamw_math(w, g, m, v):
    m = ADAM_B1 * m + (1.0 - ADAM_B1) * g
    v = ADAM_B2 * v + (1.0 - ADAM_B2) * (g * g)
    m_hat = m / (1.0 - ADAM_B1 ** ADAM_STEP)
    v_hat = v / (1.0 - ADAM_B2 ** ADAM_STEP)
    delta = -ADAM_LR * (m_hat / (jnp.sqrt(v_hat) + ADAM_EPS) + ADAM_WD * w)
    return delta, m, v


def _adamw(w, grads, m, v, block_rows, name):
    rows, cols = grads[0].shape
    blocks = rows // block_rows

    def body(w_ref, m_ref, v_ref, *rest):
        g_refs, (g_ref, d_ref, nm_ref, nv_ref) = rest[:DEPTH], rest[DEPTH:]
        for layer in range(DEPTH):
            @pl.when(pl.program_id(0) == layer)
            def _():
                g = g_refs[layer][...]
                g_ref[...] = g
                d_ref[...], nm_ref[...], nv_ref[...] = _adamw_math(w_ref[...], g, m_ref[...], v_ref[...])

    def grad_spec(layer):
        return pl.BlockSpec((block_rows, cols),
                            lambda l, i: (jnp.where(l == layer, i, jnp.where(l < layer, 0, blocks - 1)), 0))

    spec = pl.BlockSpec((block_rows, cols), lambda l, i: (l * blocks + i, 0))
    return pl.pallas_call(
        body,
        name=name,
        grid=(DEPTH, blocks),
        in_specs=[spec] * 3 + [grad_spec(layer) for layer in range(DEPTH)],
        out_specs=[spec] * 4,
        out_shape=[jax.ShapeDtypeStruct(w.shape, F32)] * 4,
        compiler_params=pltpu.CompilerParams(dimension_semantics=("arbitrary", "arbitrary"),
                                             vmem_limit_bytes=VMEM_LIMIT),
    )(w, m, v, *grads)


MESH = pl.DeviceIdType.MESH
SIBLING = 1
ANY = pl.BlockSpec(memory_space=pl.ANY)
VMEM = pl.BlockSpec(memory_space=pltpu.VMEM)


def _me():
    return lax.axis_index("x"), lax.axis_index("y"), lax.axis_index("c")


def _peer(r):
    x, y, c = _me()
    return (1 - x if r & 4 else x, 1 - y if r & 2 else y, 1 - c if r & 1 else c)


def _index(dev):
    return 4 * dev[0] + 2 * dev[1] + dev[2]


def _remote(src, dst, send_sem, recv_sem, dev):
    return pltpu.make_async_remote_copy(src_ref=src, dst_ref=dst, send_sem=send_sem, recv_sem=recv_sem,
                                        device_id=dev, device_id_type=MESH)


ACROSS_X, ACROSS_Y, ACROSS_BOTH = 4, 2, 6
GATHER_SEMS = 11


class _TwoLevelGather:
    def __init__(self, out, send_sems, recv_sems, src=None):
        self.out, self.send_sems, self.recv_sems, self.src = out, send_sems, recv_sems, src
        self.rows = (out.shape[0] // N_DEV) if len(out.shape) == 2 else out.shape[1]
        self.half = self.rows // 2

    def _slot(self, block):
        if len(self.out.shape) == 2:
            return self.out.at[pl.ds(pl.multiple_of(_index(block) * self.rows, self.rows), self.rows)]
        return self.out.at[_index(block)]

    def _copy(self, k, block, part, to, src=None):
        slot = self._slot(block)
        if part is not None:
            rows = pl.ds(part * self.half, self.half)
            slot = slot.at[rows]
            src = None if src is None else src.at[rows]
        return _remote(slot if src is None else src, slot, self.send_sems.at[k], self.recv_sems.at[k], to)

    def _mine(self):
        me = _me()
        src = self._slot(me) if self.src is None else self.src
        x, y = _peer(ACROSS_X), _peer(ACROSS_Y)
        return [self._copy(1, me, 0, x, src), self._copy(3, me, 1, y, src), self._copy(0, me, None, _peer(SIBLING), src),
                self._copy(2, me, 1, x, src), self._copy(4, me, 0, y, src)]

    def _relayed(self):
        return [self._copy(5, _peer(ACROSS_X), 0, _peer(ACROSS_Y)), self._copy(6, _peer(ACROSS_Y), 1, _peer(ACROSS_X))]

    def _passed(self):
        sib, far = _peer(SIBLING), _peer(ACROSS_BOTH)
        return [self._copy(7, _peer(ACROSS_X), None, sib), self._copy(8, _peer(ACROSS_Y), None, sib),
                self._copy(9, far, 0, sib), self._copy(10, far, 1, sib)]

    def _arrival(self, k, r, part):
        return self._copy(k, _peer(r), part, _me())

    def send_first(self):
        for cp in self._mine()[:3]:
            cp.start()

    def send_second(self):
        for cp in self._mine()[3:]:
            cp.start()

    def send_mine(self):
        self.send_first()
        self.send_second()

    def relay(self):
        relayed = self._relayed()
        self._arrival(1, ACROSS_X, 0).wait_recv()
        relayed[0].start()
        self._arrival(3, ACROSS_Y, 1).wait_recv()
        relayed[1].start()

    def pass_near(self):
        passed = self._passed()
        self._arrival(2, ACROSS_X, 1).wait_recv()
        passed[0].start()
        self._arrival(4, ACROSS_Y, 0).wait_recv()
        passed[1].start()

    def pass_far(self):
        passed = self._passed()
        self._arrival(5, ACROSS_BOTH, 0).wait_recv()
        passed[2].start()
        self._arrival(6, ACROSS_BOTH, 1).wait_recv()
        passed[3].start()

    def pass_on(self):
        self.pass_near()
        self.pass_far()

    def wait_sibling(self):
        self._arrival(0, SIBLING, None).wait_recv()

    def wait_passed(self, r):
        if r == ACROSS_BOTH:
            self._arrival(9, r ^ SIBLING, 0).wait_recv()
            self._arrival(10, r ^ SIBLING, 1).wait_recv()
        else:
            self._arrival(7 if r == ACROSS_X else 8, r ^ SIBLING, None).wait_recv()

    def wait_rest(self):
        self.wait_sibling()
        for r in (ACROSS_X, ACROSS_Y, ACROSS_BOTH):
            self.wait_passed(r)

    def wait_sends(self):
        for cp in self._mine() + self._relayed() + self._passed():
            cp.wait_send()


class _ChipReduceScatter:
    SLOTS = 6

    def __init__(self, arrays, l_sem, d_send, d_recv, i_send, i_recv):
        self.arrays = arrays
        self.l_sem, self.d_send, self.d_recv, self.i_send, self.i_recv = l_sem, d_send, d_recv, i_send, i_recv

    @staticmethod
    def buffers(rows, cols, dtype, staged=True):
        stage = [pltpu.VMEM((4, rows, cols), dtype)] if staged else []
        return stage + [pltpu.VMEM((4, rows, cols), dtype), pltpu.VMEM((3, rows, cols), dtype),
                        pltpu.VMEM((2, rows, cols), dtype), pltpu.VMEM((2, rows // 2, cols), dtype)]

    @classmethod
    def semaphores(cls, n):
        return [pltpu.SemaphoreType.DMA((n,)), pltpu.SemaphoreType.DMA((n, 4)), pltpu.SemaphoreType.DMA((n, 4)),
                pltpu.SemaphoreType.DMA((n, cls.SLOTS)), pltpu.SemaphoreType.DMA((n, cls.SLOTS))]

    def _pick(self, which):
        return list(enumerate(self.arrays)) if which is None else [(n, self.arrays[n]) for n in which]

    @staticmethod
    def _chip(r):
        dev = _me() if r is None else _peer(r)
        return 2 * dev[0] + dev[1]

    def _staging(self, which):
        c = _me()[2]
        return [pltpu.make_async_copy(a["part"].at[pl.ds(0, 4), c], a["stage"], self.l_sem.at[n])
                for n, a in self._pick(which) if a["staged"]]

    def _first(self, which, chip):
        other = 1 - _me()[2]
        return [_remote(a["part"].at[chip, other], a["sib"].at[chip], self.d_send.at[n, chip], self.d_recv.at[n, chip],
                        _peer(SIBLING)) for n, a in self._pick(which)]

    @staticmethod
    def _halves(a):
        half = a["rcv"].shape[1] // 2
        return pl.ds(0, half), pl.ds(half, half)

    def _hops(self, n, a):
        h0, h1 = self._halves(a)
        x, y = _peer(ACROSS_X), _peer(ACROSS_Y)
        snd, rcv, relay = a["snd"], a["rcv"], a["relay"]
        pairs = [(snd.at[2, h0], relay.at[0], x), (snd.at[2, h1], relay.at[1], y),
                 (snd.at[0, h0], rcv.at[0, h0], x), (snd.at[0, h1], rcv.at[0, h1], x),
                 (snd.at[1, h1], rcv.at[1, h1], y), (snd.at[1, h0], rcv.at[1, h0], y)]
        return [_remote(s, d, self.i_send.at[n, k], self.i_recv.at[n, k], to) for k, (s, d, to) in enumerate(pairs)]

    def _mine(self, a, chip, rows=None):
        src = a["stage"].at[chip] if a["staged"] else a["part"].at[chip, _me()[2]]
        mine, sib = (src[...], a["sib"][chip]) if rows is None else (src[rows, :], a["sib"][chip, rows, :])
        return mine.astype(F32) + sib.astype(F32)

    def start(self, which=None, chips=None):
        if chips is None:
            for cp in self._staging(which):
                cp.start()
        for chip in range(4) if chips is None else chips:
            for cp in self._first(which, chip):
                cp.start()

    def send_far(self, which=None):
        far = self._chip(ACROSS_BOTH)
        for cp in self._staging(which):
            cp.wait()
        for cp in self._first(which, far):
            cp.wait_recv()
        for n, a in self._pick(which):
            hops = self._hops(n, a)
            a["snd"][2] = self._mine(a, far).astype(a["snd"].dtype)
            hops[0].start()
            hops[1].start()

    def send_near(self, r, which=None):
        chip = self._chip(r)
        for cp in self._first(which, chip):
            cp.wait_recv()
        for n, a in self._pick(which):
            h0, h1 = self._halves(a)
            hops = self._hops(n, a)
            if r == ACROSS_X:
                a["snd"][0, h0, :] = self._mine(a, chip, h0).astype(a["snd"].dtype)
                hops[2].start()
            else:
                a["snd"][1, h1, :] = self._mine(a, chip, h1).astype(a["snd"].dtype)
                hops[4].start()

    def exchange(self, which=None):
        self.send_far(which)
        self.send_near(ACROSS_X, which)
        self.send_near(ACROSS_Y, which)

    def fold(self, which=None):
        across_x, across_y = self._chip(ACROSS_X), self._chip(ACROSS_Y)
        for n, a in self._pick(which):
            h0, h1 = self._halves(a)
            hops = self._hops(n, a)
            dtype = a["snd"].dtype
            hops[1].wait_recv()
            a["snd"][0, h1, :] = (self._mine(a, across_x, h1) + a["relay"][1].astype(F32)).astype(dtype)
            hops[3].start()
            hops[0].wait_recv()
            a["snd"][1, h0, :] = (self._mine(a, across_y, h0) + a["relay"][0].astype(F32)).astype(dtype)
            hops[5].start()

    def finish(self, which=None):
        home = self._chip(None)
        for cp in self._first(which, home):
            cp.wait_recv()
        for n, a in self._pick(which):
            hops = self._hops(n, a)
            a["out"][...] = self._mine(a, home)
            hops[2].wait_recv()
            hops[3].wait_recv()
            a["out"][...] += a["rcv"][0].astype(F32)
            hops[4].wait_recv()
            hops[5].wait_recv()
            a["out"][...] += a["rcv"][1].astype(F32)

    def wait_sends(self, which=None):
        for chip in range(4):
            for cp in self._first(which, chip):
                cp.wait_send()
        for n, a in self._pick(which):
            for cp in self._hops(n, a):
                cp.wait_send()


def _direct_exchange(src_of, dst_of, send_sems, recv_sems):
    me = _me()
    copies = [_remote(src_of(_peer(r)), dst_of(me), send_sems.at[r - 1], recv_sems.at[r - 1], _peer(r))
              for r in range(1, N_DEV)]
    for cp in copies:
        cp.start()
    return copies


def _wait_direct(copies):
    for cp in copies:
        cp.wait_recv()
    for cp in copies:
        cp.wait_send()


def _forward(x, c, w_ada, b_ada, small, ln_g, ln_b, mine, following):
    assert DEPTH == 2
    cols = w_ada.shape[2]
    shard = mine[0].shape[0]
    pair = 2 * shard

    def body(x_hbm, c_ref, wada_hbm, bada_ref, wpool_ref, pscale_ref, slng_ref, slnb_ref, wsgu_ref, bsgut_ref,
             lng_ref, lnb_ref, wint_hbm, wout_hbm, next_in_hbm, next_out_hbm,
             out0_ref, y0_ref, cdf0_ref, proj0_hbm, out1_ref, y1_ref, cdf1_ref, proj1_hbm,
             wint_keep, wout_keep, wint_next, wout_next, acts_ref, mod_ref,
             wint_v, wout_v, h_buf, proj_blk, proj_tile, halo_ref, x_ref, w_send, w_recv, w_local, p_sems,
             t_sems, in_sems, act_all, act_src, part, mod_recv, wada_ref, a_send, a_recv, m_send, m_recv,
             n_send, n_recv, n_local, f_sems):
        step = pl.program_id(0)
        chip_of = lambda dev: 2 * dev[0] + dev[1]

        def hosted():
            return [_TwoLevelGather(out, n_send.at[n], n_recv.at[n], src=src)
                    for n, (out, src) in enumerate(((wint_next, next_in_hbm), (wout_next, next_out_hbm)))]

        def hosted_own():
            me = _me()
            return [pltpu.make_async_copy(g.src, g._slot(me), n_local.at[n]) for n, g in enumerate(hosted())]

        def gathers():
            return (_TwoLevelGather(wint_v, w_send.at[0], w_recv.at[0], src=wint_hbm),
                    _TwoLevelGather(wout_v, w_send.at[1], w_recv.at[1], src=wout_hbm))

        def keeps():
            return [pltpu.make_async_copy(wint_v, wint_keep, w_local.at[2]),
                    pltpu.make_async_copy(wout_v, wout_keep, w_local.at[3])]

        def tile_read(proj_hbm, t):
            slot = t % 2
            return pltpu.make_async_copy(proj_hbm.at[pl.ds(pl.multiple_of(t * TM, TM), TM)], proj_tile.at[slot],
                                         t_sems.at[slot])

        def modulated(layer):
            shift = mod_ref[layer:layer + 1, 0:D_MODEL]
            scale = mod_ref[layer:layer + 1, D_MODEL:2 * D_MODEL]
            for t in range(N_TILES):
                rows = pl.ds(t * TM, TM)
                xn, _ = _ln(x_ref[rows, :])
                h_buf[rows, :] = (xn * (1.0 + scale) + shift).astype(BF16)

        def projector(proj_hbm):
            writes = []

            def project(n, dev):
                first = pl.multiple_of(chip_of(dev) * pair, pair)
                if n >= 2:
                    writes[n - 2].wait()
                proj_blk[n % 2] = _dot_nt(h_buf[...], wint_v[pl.ds(first, pair), :])
                cp = pltpu.make_async_copy(proj_blk.at[n % 2], proj_hbm.at[:, pl.ds(first, pair)], p_sems.at[n % 2])
                cp.start()
                writes.append(cp)

            return project, writes

        def first_layer_start():
            me = _me()
            halo_ref[...] = jnp.zeros_like(halo_ref)
            gather_in, gather_out = gathers()
            own_in = pltpu.make_async_copy(wint_hbm, gather_in._slot(me), w_local.at[0])
            own_out = pltpu.make_async_copy(wout_hbm, gather_out._slot(me), w_local.at[1])
            x_load = pltpu.make_async_copy(x_hbm, x_ref, in_sems.at[0])
            x_load.start()
            wada_load = pltpu.make_async_copy(wada_hbm, wada_ref, in_sems.at[1])
            wada_load.start()
            mine_index = _index(me)
            cval = c_ref[...]
            act_src[...] = jnp.zeros_like(act_src)
            act_src[0:1, :] = cval * jax.nn.sigmoid(cval)
            act_all[mine_index] = act_src[...]
            act_copies = _direct_exchange(lambda p: act_src, lambda m: act_all.at[_index(m)], a_send, a_recv)

            own_in.start()
            own_out.start()
            gather_in.send_first()

            _wait_direct(act_copies)
            acts = jnp.concatenate([act_all[j, 0:1, :] for j in range(N_DEV)], axis=0)
            acts_ref[...] = acts
            part[...] = jnp.zeros_like(part)
            wada_load.wait()
            for l in range(DEPTH):
                res = lax.dot_general(acts, wada_ref[l], (((1,), (0,)), ((), ())), preferred_element_type=F32,
                                      precision=lax.Precision.HIGHEST)
                for b in range(N_DEV):
                    part[b, l:l + 1, :] = res[b:b + 1, :]
            mod_recv[mine_index] = part[mine_index]
            mod_copies = _direct_exchange(lambda p: part.at[_index(p)], lambda m: mod_recv.at[_index(m)],
                                          m_send, m_recv)
            gather_in.send_second()
            gather_out.send_mine()

            _wait_direct(mod_copies)
            for l in range(DEPTH):
                for j in range(N_DEV):
                    sl = slice(j * cols, (j + 1) * cols)
                    mod_ref[l:l + 1, sl] = mod_recv[j, l:l + 1, :] + bada_ref[l:l + 1, sl]
            x_load.wait()
            modulated(0)

            project, writes = projector(proj0_hbm)
            gather_in.relay()
            own_in.wait()
            gather_in.wait_sibling()
            project(0, me)
            gather_in.pass_near()
            gather_in.wait_passed(ACROSS_X)
            project(1, _peer(ACROSS_X))
            gather_out.relay()
            for cp in hosted_own():
                cp.start()
            for g in hosted():
                g.send_mine()
            gather_in.wait_passed(ACROSS_Y)
            project(2, _peer(ACROSS_Y))
            gather_in.pass_far()
            gather_in.wait_passed(ACROSS_BOTH)
            project(3, _peer(ACROSS_BOTH))

            gather_out.pass_on()
            gather_out.wait_rest()
            own_out.wait()
            for cp in keeps():
                cp.start()
            writes[2].wait()
            writes[3].wait()
            tile_read(proj0_hbm, 0).start()

        def second_layer_start():
            me = _me()
            halo_ref[...] = jnp.zeros_like(halo_ref)
            next_in, next_out = hosted()
            modulated(1)

            def fetch(n, dev):
                rows = pl.ds(pl.multiple_of(chip_of(dev) * pair, pair), pair)
                cp = pltpu.make_async_copy(wint_next.at[rows], wint_v.at[rows], f_sems.at[n])
                cp.start()
                return cp

            project, writes = projector(proj1_hbm)
            for cp in hosted_own():
                cp.wait()
            devs = [me, _peer(ACROSS_X), _peer(ACROSS_Y), _peer(ACROSS_BOTH)]
            next_in.wait_sibling()
            fetched = [fetch(0, devs[0])]
            for n, r in enumerate((ACROSS_X, ACROSS_Y, ACROSS_BOTH)):
                next_in.wait_passed(r)
                fetched.append(fetch(n + 1, devs[n + 1]))
                fetched[n].wait()
                project(n, devs[n])
            next_out.wait_rest()
            fetch_out = pltpu.make_async_copy(wout_next, wout_v, f_sems.at[4])
            fetch_out.start()
            fetched[3].wait()
            project(3, devs[3])
            fetch_out.wait()
            writes[2].wait()
            writes[3].wait()
            tile_read(proj1_hbm, 0).start()

        def run_tile(layer, tile, proj_hbm, out_ref, y_ref, cdf_ref):
            weights = _MixWeights(layer, wpool_ref, pscale_ref, slng_ref, slnb_ref, wsgu_ref, bsgut_ref)

            @pl.when(tile + 1 < N_TILES)
            def _():
                tile_read(proj_hbm, tile + 1).start()

            tile_read(proj_hbm, tile).wait()
            rows = pl.ds(pl.multiple_of(tile * TM, TM), TM)
            xt = x_ref[rows, :]
            gate = mod_ref[layer:layer + 1, 2 * D_MODEL:]
            proj = proj_tile[tile % 2]
            cat, cdf_ref[...] = _mix_forward(proj, halo_ref[...], tile, weights)
            halo_ref[...] = proj[TM - HALO:, 0:D_POOL]
            y = _dot(cat.astype(BF16), wout_v[...])
            y_ref[...] = y
            zn, _ = _ln(ALPHA * xt + gate * y)
            out = zn * lng_ref[layer:layer + 1, :] + lnb_ref[layer:layer + 1, :]
            out_ref[...] = out
            if layer + 1 < DEPTH:
                x_ref[rows, :] = out

        @pl.when(step < N_TILES)
        def _():
            @pl.when(step == 0)
            def _():
                first_layer_start()

            @pl.when(step == 1)
            def _():
                for g in hosted():
                    g.relay()

            @pl.when(step == N_TILES // 2)
            def _():
                for g in hosted():
                    g.pass_near()

            run_tile(0, step, proj0_hbm, out0_ref, y0_ref, cdf0_ref)

            @pl.when(step == N_TILES - 1)
            def _():
                for g in hosted():
                    g.pass_far()
                for g in gathers():
                    g.wait_sends()
                for cp in keeps():
                    cp.wait()

        @pl.when(step >= N_TILES)
        def _():
            @pl.when(step == N_TILES)
            def _():
                second_layer_start()

            run_tile(1, step - N_TILES, proj1_hbm, out1_ref, y1_ref, cdf1_ref)

            @pl.when(step == 2 * N_TILES - 1)
            def _():
                for g in hosted():
                    g.wait_sends()

    first = lambda w: pl.BlockSpec((TM, w), lambda i: (jnp.minimum(i, N_TILES - 1), 0))
    second = lambda w: pl.BlockSpec((TM, w), lambda i: (jnp.maximum(i - N_TILES, 0), 0))
    gather_sems = pltpu.SemaphoreType.DMA((2, GATHER_SEMS))
    seven = pltpu.SemaphoreType.DMA((7,))
    per_layer = [jax.ShapeDtypeStruct((SEQ, D_MODEL), F32), jax.ShapeDtypeStruct((SEQ, D_MODEL), F32),
                 jax.ShapeDtypeStruct((SEQ, 2 * D_SGU), F32), jax.ShapeDtypeStruct((SEQ, D_IN), F32)]
    gathered = [jax.ShapeDtypeStruct((D_IN, D_MODEL), BF16), jax.ShapeDtypeStruct((D_MODEL, D_MODEL), BF16)]
    res = pl.pallas_call(
        body,
        name="layers_fwd",
        grid=(DEPTH * N_TILES,),
        in_specs=[ANY, _const_spec(c.shape), ANY, _const_spec(b_ada.shape)] + [_const_spec(s) for s in SMALL_SPECS]
                 + [_const_spec((DEPTH, D_MODEL)), _const_spec((DEPTH, D_MODEL))] + [ANY] * 4,
        out_specs=[first(D_MODEL), first(D_MODEL), first(2 * D_SGU), ANY,
                   second(D_MODEL), second(D_MODEL), second(2 * D_SGU), ANY] + [ANY] * 4
                  + [_const_spec((N_DEV, D_MODEL)), _const_spec((DEPTH, 3 * D_MODEL))],
        out_shape=per_layer * 2 + gathered * 2 + [jax.ShapeDtypeStruct((N_DEV, D_MODEL), F32),
                                                  jax.ShapeDtypeStruct((DEPTH, 3 * D_MODEL), F32)],
        scratch_shapes=[pltpu.VMEM((D_IN, D_MODEL), BF16), pltpu.VMEM((D_MODEL, D_MODEL), BF16),
                        pltpu.VMEM((SEQ, D_MODEL), BF16), pltpu.VMEM((2, SEQ, pair), F32),
                        pltpu.VMEM((2, TM, D_IN), F32), pltpu.VMEM((HALO, D_POOL), F32),
                        pltpu.VMEM((SEQ, D_MODEL), F32),
                        gather_sems, gather_sems, pltpu.SemaphoreType.DMA((4,)), pltpu.SemaphoreType.DMA((2,)),
                        pltpu.SemaphoreType.DMA((2,)), pltpu.SemaphoreType.DMA((2,)),
                        pltpu.VMEM((N_DEV, 8, D_MODEL), F32), pltpu.VMEM((8, D_MODEL), F32),
                        pltpu.VMEM((N_DEV, 8, cols), F32), pltpu.VMEM((N_DEV, 8, cols), F32),
                        pltpu.VMEM(w_ada.shape, F32), seven, seven, seven, seven,
                        gather_sems, gather_sems, pltpu.SemaphoreType.DMA((2,)), pltpu.SemaphoreType.DMA((5,))],
        compiler_params=pltpu.CompilerParams(dimension_semantics=("arbitrary",), vmem_limit_bytes=VMEM_LIMIT),
    )(x, c, w_ada, b_ada, *small, ln_g, ln_b, *mine, *following)
    return res[0:4], res[4:8], res[8:10], res[10:12], res[12], res[13]


ADA_CHUNK = 256


def _grad_tail(dproj, h, cat, dy, small, dmod8, loss_lanes, w_ada, m_ada, v_ada, act_t, b_ada, m_bada, v_bada):
    shard_in, shard_out, shard_small = D_IN // N_DEV, D_MODEL // N_DEV, small.shape[2]
    cols = w_ada.shape[2]
    W_IN, W_OUT, SMALL = 0, 1, 2

    def body(dproj_hbm, h_hbm, cat_hbm, dy_hbm, small_hbm, dmod_ref, lanes_ref, wada_hbm, mada_hbm, vada_hbm,
             act_ref, bada_ref, mbada_ref, vbada_ref,
             gwin_ref, gwout_ref, stot_ref, loss_ref, gada_hbm, dada_hbm, nmada_hbm, nvada_hbm,
             gb_ref, db_ref, nmb_ref, nvb_ref,
             dproj_v, h_v, cat_v, dy_v, part_in, part_out, own_small, loss_src, loss_all, dmod_all, ada_in, ada_out,
             *rest):
        bufs, rest = rest[:13], rest[13:]
        load_sems, rs_sems = rest[0], rest[1:6]
        m_send, m_recv, g_send, g_recv, s_send, s_recv, ada_lsem, ada_ssem = rest[6:]
        mine = _index(_me())

        def update_ada():
            upper = (mine % 2) == 1

            def dmod_of(layer):
                rows = []
                for b in range(N_DEV):
                    r = dmod_all[b, pl.ds(4 * layer + mine // 2, 1), :]
                    rows.append(jnp.where(upper, r[:, cols:], r[:, :cols]))
                return jnp.concatenate(rows, axis=0)

            chunks = [(layer, c) for layer in range(DEPTH) for c in range(D_MODEL // ADA_CHUNK)]

            def loads(i):
                layer, c = chunks[i]
                rows = pl.ds(c * ADA_CHUNK, ADA_CHUNK)
                return [pltpu.make_async_copy(src.at[layer, rows], ada_in.at[i % 2, k], ada_lsem.at[i % 2, k])
                        for k, src in enumerate((wada_hbm, mada_hbm, vada_hbm))]

            def stores(i):
                layer, c = chunks[i]
                rows = pl.ds(c * ADA_CHUNK, ADA_CHUNK)
                return [pltpu.make_async_copy(ada_out.at[i % 2, k], dst.at[layer, rows], ada_ssem.at[i % 2, k])
                        for k, dst in enumerate((gada_hbm, dada_hbm, nmada_hbm, nvada_hbm))]

            for cp in loads(0):
                cp.start()
            dmods = {}
            for i, (layer, c) in enumerate(chunks):
                if i + 1 < len(chunks):
                    for cp in loads(i + 1):
                        cp.start()
                for cp in loads(i):
                    cp.wait()
                if i >= 2:
                    for cp in stores(i - 2):
                        cp.wait()
                if layer not in dmods:
                    dmods[layer] = dmod_of(layer)
                act = act_ref[pl.ds(c * ADA_CHUNK, ADA_CHUNK), :]
                g = act[:, 0:1] * dmods[layer][0:1, :]
                for b in range(1, N_DEV):
                    g = g + act[:, b:b + 1] * dmods[layer][b:b + 1, :]
                slot = i % 2
                delta, new_m, new_v = _adamw_math(ada_in[slot, 0], g, ada_in[slot, 1], ada_in[slot, 2])
                ada_out[slot, 0] = g
                ada_out[slot, 1] = delta
                ada_out[slot, 2] = new_m
                ada_out[slot, 3] = new_v
                for cp in stores(i):
                    cp.start()
            for i in (len(chunks) - 2, len(chunks) - 1):
                for cp in stores(i):
                    cp.wait()

            total = dmod_all[0]
            for b in range(1, N_DEV):
                total = total + dmod_all[b]
            width = total.shape[1]
            for layer in range(DEPTH):
                for q in range(4):
                    gb_ref[layer:layer + 1, q * width:(q + 1) * width] = total[4 * layer + q:4 * layer + q + 1, :]
            db_ref[...], nmb_ref[...], nvb_ref[...] = _adamw_math(bada_ref[...], gb_ref[...], mbada_ref[...],
                                                                  vbada_ref[...])

        order = (ACROSS_BOTH, ACROSS_X, ACROSS_Y, None)
        chips = [_ChipReduceScatter._chip(r) for r in order]
        loads = [pltpu.make_async_copy(s, d, load_sems.at[n]) for n, (s, d) in enumerate(
            ((cat_hbm, cat_v), (dy_hbm, dy_v), (h_hbm, h_v)))]
        loads += [pltpu.make_async_copy(dproj_hbm.at[:, pl.ds(pl.multiple_of(chip * 2 * shard_in, 2 * shard_in),
                                                             2 * shard_in)], dproj_v.at[n], load_sems.at[3 + n])
                  for n, chip in enumerate(chips)]
        for cp in loads:
            cp.start()
        arrays = [dict(part=part_in, out=gwin_ref, staged=False, sib=bufs[0], snd=bufs[1], rcv=bufs[2], relay=bufs[3]),
                  dict(part=part_out, out=gwout_ref, staged=False, sib=bufs[4], snd=bufs[5], rcv=bufs[6],
                       relay=bufs[7]),
                  dict(part=small_hbm, out=own_small, staged=True, stage=bufs[8], sib=bufs[9], snd=bufs[10],
                       rcv=bufs[11], relay=bufs[12])]
        scatter = _ChipReduceScatter(arrays, *rs_sems)
        scatter.start([SMALL])
        dmod_all[mine] = dmod_ref[...]
        dmod_copies = _direct_exchange(lambda p: dmod_ref, lambda m: dmod_all.at[_index(m)], m_send, m_recv)
        loss_src[...] = jnp.full(loss_src.shape, (0.5 / D_MODEL) * jnp.sum(lanes_ref[...]), F32)
        loss_all[mine] = loss_src[...]
        loss_copies = _direct_exchange(lambda p: loss_src, lambda m: loss_all.at[_index(m)], s_send, s_recv)

        loads[0].wait()
        loads[1].wait()
        for blk in range(2):
            res = _dot_tn(cat_v[:, blk * 512:(blk + 1) * 512], dy_v[...]).astype(BF16)
            for s in range(4):
                part_out[2 * blk + s // 2, s % 2] = res[s * shard_out:(s + 1) * shard_out]
        scatter.start([W_OUT])
        scatter.exchange([SMALL])

        gather = _TwoLevelGather(stot_ref, g_send, g_recv)
        loads[2].wait()
        for n, chip in enumerate(chips):
            loads[3 + n].wait()
            res = _dot_tn(dproj_v[n], h_v[...]).astype(BF16)
            part_in[chip, 0] = res[:shard_in]
            part_in[chip, 1] = res[shard_in:]
            scatter.start([W_IN], chips=[chip])
            if n == 0:
                scatter.exchange([W_OUT])
                scatter.fold([SMALL])
            if n == 1:
                scatter.send_far([W_IN])
                scatter.fold([W_OUT])
                scatter.finish([SMALL])
                stot_ref[mine] = own_small[...]
                gather.send_mine()
            if n == 2:
                scatter.send_near(ACROSS_X, [W_IN])
                gather.relay()
            if n == 3:
                scatter.send_near(ACROSS_Y, [W_IN])
        scatter.fold([W_IN])
        scatter.finish([W_OUT])
        _wait_direct(dmod_copies)
        update_ada()
        gather.pass_on()
        gather.wait_rest()
        _wait_direct(loss_copies)
        total = loss_all[0]
        for j in range(1, N_DEV):
            total = total + loss_all[j]
        loss_ref[...] = total
        scatter.finish([W_IN])
        gather.wait_sends()
        scatter.wait_sends()

    buffers = _ChipReduceScatter.buffers
    comm_scratch = (buffers(shard_in, D_MODEL, BF16, staged=False) + buffers(shard_out, D_MODEL, BF16, staged=False)
                    + buffers(shard_small, 128, F32))
    comm_scratch += [pltpu.SemaphoreType.DMA((7,))] + _ChipReduceScatter.semaphores(3)
    comm_scratch += [pltpu.SemaphoreType.DMA((n,)) for n in (7, 7, GATHER_SEMS, GATHER_SEMS, 7, 7)]
    comm_scratch += [pltpu.SemaphoreType.DMA((2, 3)), pltpu.SemaphoreType.DMA((2, 4))]
    return pl.pallas_call(
        body,
        name="grad_tail",
        in_specs=[ANY] * 5 + [VMEM, VMEM] + [ANY] * 3 + [VMEM] * 4,
        out_specs=[VMEM] * 4 + [ANY] * 4 + [VMEM] * 4,
        out_shape=[jax.ShapeDtypeStruct((shard_in, D_MODEL), F32), jax.ShapeDtypeStruct((shard_out, D_MODEL), F32),
                   jax.ShapeDtypeStruct((N_DEV, shard_small, 128), F32), jax.ShapeDtypeStruct((8, 128), F32)]
                  + [jax.ShapeDtypeStruct(w_ada.shape, F32)] * 4 + [jax.ShapeDtypeStruct(b_ada.shape, F32)] * 4,
        scratch_shapes=[pltpu.VMEM((4, SEQ, 2 * shard_in), BF16), pltpu.VMEM(h.shape, BF16), pltpu.VMEM(cat.shape, BF16),
                        pltpu.VMEM(dy.shape, BF16), pltpu.VMEM((4, 2, shard_in, D_MODEL), BF16),
                        pltpu.VMEM((4, 2, shard_out, D_MODEL), BF16), pltpu.VMEM((shard_small, 128), F32),
                        pltpu.VMEM((8, 128), F32), pltpu.VMEM((N_DEV, 8, 128), F32),
                        pltpu.VMEM((N_DEV,) + dmod8.shape, F32), pltpu.VMEM((2, 3, ADA_CHUNK, cols), F32),
                        pltpu.VMEM((2, 4, ADA_CHUNK, cols), F32)] + comm_scratch,
        compiler_params=pltpu.CompilerParams(vmem_limit_bytes=VMEM_LIMIT),
    )(dproj, h, cat, dy, small, dmod8, loss_lanes, w_ada, m_ada, v_ada, act_t, b_ada, m_bada, v_bada)


SMALL_NAMES = ("w_pool", "w_sgu", "pool_scale", "sgu_ln_g", "sgu_ln_b", "b_sgu", "ln_g", "ln_b")
SMALL_ROWS = (512, 512, 4, 4, 4, 4, 8, 8)


def _adamw_small(g_packed, ws, ms, vs, name):
    n = len(SMALL_NAMES)

    def body(g_ref, *refs):
        w_refs, m_refs, v_refs = refs[:n], refs[n:2 * n], refs[2 * n:3 * n]
        outs = refs[3 * n:]

        def update(p, at, g):
            delta, new_m, new_v = _adamw_math(w_refs[p][at], g, m_refs[p][at], v_refs[p][at])
            outs[p][at] = g
            outs[n + p][at] = delta
            outs[2 * n + p][at] = new_m
            outs[3 * n + p][at] = new_v

        row = 0
        for p, r in enumerate(SMALL_ROWS):
            shape = ws[p].shape
            for layer in range(DEPTH):
                first = layer * PACK_ROWS + row
                if len(shape) == 4:
                    for k in range(shape[1]):
                        update(p, (layer, k), g_ref[first + k * shape[2]:first + (k + 1) * shape[2], :])
                elif len(shape) == 3:
                    update(p, (layer,), g_ref[first:first + r, :])
                else:
                    g = jnp.concatenate([g_ref[first + k:first + k + 1, :] for k in range(r)], axis=1)
                    update(p, (slice(layer, layer + 1), slice(None)), g)
            row += r

    res = pl.pallas_call(
        body,
        name=name,
        out_shape=[jax.ShapeDtypeStruct(w.shape, F32) for w in ws] * 4,
        compiler_params=pltpu.CompilerParams(vmem_limit_bytes=VMEM_LIMIT),
    )(g_packed, *ws, *ms, *vs)
    return res[:n], res[n:2 * n], res[2 * n:3 * n], res[3 * n:]


def kernel(x, c, w_ada, b_ada, w_in, w_pool, pool_scale, sgu_ln_g, sgu_ln_b, w_sgu, b_sgu, w_out, ln_g, ln_b, loss_target, m_w_ada, m_b_ada, m_w_in, m_w_pool, m_pool_scale, m_sgu_ln_g, m_sgu_ln_b, m_w_sgu, m_b_sgu, m_w_out, m_ln_g, m_ln_b, v_w_ada, v_b_ada, v_w_in, v_w_pool, v_pool_scale, v_sgu_ln_g, v_sgu_ln_b, v_w_sgu, v_b_sgu, v_w_out, v_ln_g, v_ln_b):
    small_w = dict(w_pool=w_pool, w_sgu=w_sgu, pool_scale=pool_scale, sgu_ln_g=sgu_ln_g, sgu_ln_b=sgu_ln_b,
                   b_sgu=b_sgu, ln_g=ln_g, ln_b=ln_b)
    small_m = dict(w_pool=m_w_pool, w_sgu=m_w_sgu, pool_scale=m_pool_scale, sgu_ln_g=m_sgu_ln_g,
                   sgu_ln_b=m_sgu_ln_b, b_sgu=m_b_sgu, ln_g=m_ln_g, ln_b=m_ln_b)
    small_v = dict(w_pool=v_w_pool, w_sgu=v_w_sgu, pool_scale=v_pool_scale, sgu_ln_g=v_sgu_ln_g,
                   sgu_ln_b=v_sgu_ln_b, b_sgu=v_b_sgu, ln_g=v_ln_g, ln_b=v_ln_b)

    wint_loc = jnp.transpose(w_in, (0, 2, 1)).astype(BF16)
    wout_loc = w_out.astype(BF16)
    small = (w_pool, pool_scale, sgu_ln_g, sgu_ln_b, w_sgu, jnp.transpose(b_sgu, (0, 2, 1)))
    (out0, y0, cdf0, proj0), (cur, y1, cdf1, proj1), gathered0, gathered1, act_all, mod = _forward(
        x[0], c, w_ada, b_ada, small, ln_g, ln_b, [wint_loc[0], wout_loc[0]], [wint_loc[1], wout_loc[1]])
    w_int, w_outf = [gathered0[0], gathered1[0]], [gathered0[1], gathered1[1]]
    acts = [(x[0], proj0, y0, cdf0), (out0, proj1, y1, cdf1)]

    shard_in, shard_out = D_IN // N_DEV, D_MODEL // N_DEV
    a, b = cur, loss_target[0]
    loss_lanes, carry, pending = None, (), []
    g_w_in_t, g_w_out = [None] * DEPTH, [None] * DEPTH
    for l in reversed(range(DEPTH)):
        dx, dproj, h, cat, dy, small_grads, dmod8, lanes, *shards = _layer_backward(
            l, a, b, *acts[l], mod, w_int[l], w_outf[l], small, ln_g, l == DEPTH - 1, f"layer_bwd_{l}",
            carry=carry, reduce=pending)
        if shards:
            g_w_in_t[l + 1], g_w_out[l + 1] = shards
        if l == DEPTH - 1:
            loss_lanes = lanes
        if l > 0:
            g_in, g_out = _grad_matmuls(dproj, h, cat, dy, f"grad_w_{l}")
            pending = [g_in.reshape(4, 2, shard_in, D_MODEL), g_out.reshape(4, 2, shard_out, D_MODEL)]
        carry = (small_grads, dmod8)
        a = b = dx
    grad_x = a[None]

    (g_w_in_t[0], g_w_out[0], small_tot, loss_tile, g_w_ada, d_w_ada, nm_w_ada, nv_w_ada,
     g_b_ada, d_b_ada, nm_b_ada, nv_b_ada) = _grad_tail(
        dproj, h, cat, dy, small_grads.reshape(4, 2, DEPTH * PACK_ROWS // N_DEV, 128), dmod8, loss_lanes,
        w_ada, m_w_ada, v_w_ada, jnp.transpose(act_all), b_ada, m_b_ada, v_b_ada)
    loss = loss_tile[0, 0]

    flat = lambda t: t.reshape(-1, t.shape[-1])
    to_t = lambda t: flat(jnp.transpose(t, (0, 2, 1)))
    from_t = lambda t: jnp.transpose(t.reshape(DEPTH, shard_in, D_MODEL), (0, 2, 1))
    g_w_in, d_w_in, nm_w_in, nv_w_in = [from_t(t) for t in _adamw(to_t(w_in), g_w_in_t, to_t(m_w_in), to_t(v_w_in),
                                                                  shard_in // 2, "adamw_w_in")]
    gwout, d_w_out, nm_w_out, nv_w_out = [t.reshape(w_out.shape) for t in _adamw(
        flat(w_out), g_w_out, flat(m_w_out), flat(v_w_out), shard_out, "adamw_w_out")]
    small_out = _adamw_small(small_tot.reshape(DEPTH * PACK_ROWS, 128), [small_w[n] for n in SMALL_NAMES],
                             [small_m[n] for n in SMALL_NAMES], [small_v[n] for n in SMALL_NAMES], "adamw_small")
    gs, ds, ms, vs = [dict(zip(SMALL_NAMES, group)) for group in small_out]

    def ordered(w_ada_, b_ada_, w_in_, small, w_out_):
        return (w_ada_, b_ada_, w_in_, small["w_pool"], small["pool_scale"], small["sgu_ln_g"], small["sgu_ln_b"],
                small["w_sgu"], small["b_sgu"], w_out_, small["ln_g"], small["ln_b"])

    return (loss, grad_x,
            *ordered(g_w_ada, g_b_ada, g_w_in, gs, gwout),
            *ordered(d_w_ada, d_b_ada, d_w_in, ds, d_w_out),
            *ordered(nm_w_ada, nm_b_ada, nm_w_in, ms, nm_w_out),
            *ordered(nv_w_ada, nv_b_ada, nv_w_in, vs, nv_w_out))
```

```python
import jax
import jax.numpy as jnp
from jax import lax
from jax.experimental import pallas as pl
from jax.experimental.pallas import tpu as pltpu

F32 = jnp.float32
BF16 = jnp.bfloat16

D_MODEL = 1024
SEQ = 2048
DEPTH = 2
D_POOL = 512
D_SGU = 512
D_IN = 2560
N_GROUPS = 4
GROUP = 128
N_HEADS = 4
HEAD = 128
CHUNK = 128
WINDOWS = (2, 4, 8, 16)
ALPHA = (2.0 * DEPTH) ** 0.25
LN_EPS = 1e-5
N_DEV = 8

ADAM_LR = 0.001
ADAM_B1 = 0.9
ADAM_B2 = 0.999
ADAM_EPS = 1e-08
ADAM_WD = 0.01
ADAM_STEP = 10

TM = 256
HALO = 16
N_TILES = SEQ // TM
VMEM_LIMIT = 60 * 1024 * 1024

ROW_WPOOL = 0
ROW_WSGU = 512
ROW_PSCALE = 1024
ROW_SLNG = 1028
ROW_SLNB = 1032
ROW_BSGU = 1036
ROW_LNG = 1040
ROW_LNB = 1048
PACK_ROWS = 1088
DMOD_COLS = DEPTH * 3 * D_MODEL // 8

SQRT_HALF = 0.7071067811865476
INV_SQRT_2PI = 0.3989422804014327


def _ln(x):
    mu = jnp.mean(x, axis=-1, keepdims=True)
    xc = x - mu
    var = jnp.mean(xc * xc, axis=-1, keepdims=True)
    rstd = lax.rsqrt(var + LN_EPS)
    return xc * rstd, rstd


def _ln_bwd(dxn, xn, rstd):
    m1 = jnp.mean(dxn, axis=-1, keepdims=True)
    m2 = jnp.mean(dxn * xn, axis=-1, keepdims=True)
    return rstd * (dxn - m1 - xn * m2)


def _normal_cdf(x):
    return 0.5 * (1.0 + lax.erf(x * SQRT_HALF))


def _gelu_parts(x, cdf, with_grad):
    if not with_grad:
        return x * cdf, None
    return x * cdf, cdf + x * (INV_SQRT_2PI * jnp.exp(-0.5 * x * x))


def _silu_parts(x):
    s = jax.nn.sigmoid(x)
    return x * s, s * (1.0 + x * (1.0 - s))


def _dot(a, b):
    return lax.dot_general(a, b, (((1,), (0,)), ((), ())), preferred_element_type=F32)


def _dot_nt(a, b):
    return lax.dot_general(a, b, (((1,), (1,)), ((), ())), preferred_element_type=F32)


def _dot_tn(a, b):
    return lax.dot_general(a, b, (((0,), (0,)), ((), ())), preferred_element_type=F32)


def _row_index(tile):
    return tile * TM + lax.broadcasted_iota(jnp.int32, (TM, 1), 0)


def _window_sums(ext, forward):
    n = TM + HALO
    cur = ext
    outs = []
    for g in range(N_GROUPS):
        step = 1 << g
        cur = cur + pltpu.roll(cur, step if forward else n - step, 0)
        rows = cur[HALO:, :GROUP] if forward else cur[:TM, :GROUP]
        outs.append(rows)
        cur = cur[:, GROUP:] if g + 1 < N_GROUPS else None
    return outs


def _inverse_counts(rows):
    return [1.0 / jnp.minimum(rows + 1, w).astype(F32) for w in WINDOWS]


def _tril_bf16(w):
    t = lax.broadcasted_iota(jnp.int32, (CHUNK, CHUNK), 0)
    s = lax.broadcasted_iota(jnp.int32, (CHUNK, CHUNK), 1)
    return jnp.where(t >= s, w, 0.0).astype(BF16)


class _MixWeights:
    def __init__(self, layer, wpool_ref, pscale_ref, slng_ref, slnb_ref, wsgu_ref, bsgut_ref):
        self.layer = layer
        self.wpool_ref, self.pscale_ref, self.slng_ref, self.slnb_ref = wpool_ref, pscale_ref, slng_ref, slnb_ref
        self.wsgu_ref, self.bsgut_ref = wsgu_ref, bsgut_ref

    def pool(self, g):
        return self.wpool_ref[self.layer, g].astype(BF16)

    def pool_scale(self, g):
        return self.pscale_ref[self.layer:self.layer + 1, g * GROUP:(g + 1) * GROUP]

    def ln_gain(self, h):
        return self.slng_ref[self.layer, h:h + 1, :]

    def ln_bias(self, h):
        return self.slnb_ref[self.layer, h:h + 1, :]

    def mix(self, h):
        return _tril_bf16(self.wsgu_ref[self.layer, h])

    def mix_bias(self, h):
        return self.bsgut_ref[self.layer, :, h:h + 1]


SMALL_SPECS = ((DEPTH, N_GROUPS, GROUP, GROUP), (DEPTH, D_POOL), (DEPTH, N_HEADS, HEAD), (DEPTH, N_HEADS, HEAD),
               (DEPTH, N_HEADS, CHUNK, CHUNK), (DEPTH, CHUNK, N_HEADS))


def _mix_forward(proj, halo, tile, w, cdf=None):
    keep = cdf is not None
    rows = _row_index(tile)
    inv_counts = _inverse_counts(rows)
    xa = proj[:, 0:D_POOL]
    ga = proj[:, D_POOL:2 * D_POOL]
    sums = _window_sums(jnp.concatenate([halo, xa], axis=0), True)
    ga_act, ga_grad = _silu_parts(ga)
    pooled, pw, ya = [], [], []
    for g in range(N_GROUPS):
        sl = slice(g * GROUP, (g + 1) * GROUP)
        p = (sums[g] * inv_counts[g] - xa[:, sl]).astype(BF16)
        q = _dot(p, w.pool(g))
        pooled.append(p)
        pw.append(q)
        ya.append(q * w.pool_scale(g) * ga_act[:, sl])

    u = proj[:, 2 * D_POOL:2 * D_POOL + D_SGU]
    v = proj[:, 2 * D_POOL + D_SGU:2 * D_POOL + 2 * D_SGU]
    gb = proj[:, 2 * D_POOL + 2 * D_SGU:]
    gb_act, gb_grad = _silu_parts(gb)
    if cdf is None:
        cdf = jnp.concatenate([_normal_cdf(u), _normal_cdf(v)], axis=1)
    u_act, u_grad = _gelu_parts(u, cdf[:, :D_SGU], keep)
    v_act, v_grad = _gelu_parts(v, cdf[:, D_SGU:], keep)
    vn, vrstd, vln, mixed, yb = [], [], [], [], []
    for h in range(N_HEADS):
        sl = slice(h * HEAD, (h + 1) * HEAD)
        n_h, r_h = _ln(v_act[:, sl])
        l_h = (n_h * w.ln_gain(h) + w.ln_bias(h)).astype(BF16)
        w_h = w.mix(h)
        bias = w.mix_bias(h)
        m_h = jnp.concatenate(
            [_dot(w_h, l_h[k * CHUNK:(k + 1) * CHUNK]) + bias for k in range(TM // CHUNK)], axis=0)
        vn.append(n_h)
        vrstd.append(r_h)
        vln.append(l_h)
        mixed.append(m_h)
        yb.append(u_act[:, sl] * m_h * gb_act[:, sl])
    cat = jnp.concatenate(ya + yb, axis=1)
    if not keep:
        return cat, cdf
    return cat, dict(inv_counts=inv_counts, ga_act=ga_act, ga_grad=ga_grad, pooled=pooled, pw=pw, u_grad=u_grad,
                     v_grad=v_grad, u_act=u_act, gb_act=gb_act, gb_grad=gb_grad, vn=vn, vrstd=vrstd, vln=vln,
                     mixed=mixed)


def _const_spec(shape):
    nd = len(shape)
    return pl.BlockSpec(shape, lambda i: (0,) * nd)


VEC_LNG, VEC_LNB, VEC_POOL, VEC_SGU, VEC_SHIFT, VEC_SCALE, VEC_GATE, VEC_LOSS = range(8)


def _layer_backward(layer, a, b, x, proj, y, cdf, mod, w_int, w_outf, small, ln_g, is_last, name, carry=(),
                    reduce=()):
    n_red, n_carry = len(reduce), len(carry)
    base = layer * PACK_ROWS

    def body(a_ref, b_ref, x_ref, proj_ref, prev_ref, y_ref, cdf_ref, mod_ref, wint_ref, wout_ref, wpool_ref,
             pscale_ref, slng_ref, slnb_ref, wsgu_ref, bsgut_ref, lng_ref, *rest):
        weights = _MixWeights(layer, wpool_ref, pscale_ref, slng_ref, slnb_ref, wsgu_ref, bsgut_ref)
        carry_refs, rest = rest[:n_carry], rest[n_carry:]
        part_refs, rest = rest[:n_red], rest[n_red:]
        dx_ref, dproj_ref, h_ref, cat_ref, dy_ref, small_ref, dmod_ref, loss_ref = rest[:8]
        shard_refs, rest = rest[8:8 + n_red], rest[8 + n_red:]
        vec_ref, dmix_ref, halo_ref = rest[:3]
        step = pl.program_id(0)
        tile = N_TILES - 1 - step

        def scatter():
            bufs, sems = rest[3:3 + 5 * n_red], rest[3 + 5 * n_red:]
            arrays = [dict(part=part_refs[n], out=shard_refs[n], staged=True, stage=bufs[5 * n], sib=bufs[5 * n + 1],
                           snd=bufs[5 * n + 2], rcv=bufs[5 * n + 3], relay=bufs[5 * n + 4]) for n in range(n_red)]
            return _ChipReduceScatter(arrays, *sems)

        @pl.when(step == 0)
        def _():
            small_ref[...] = jnp.zeros_like(small_ref)
            dmod_ref[...] = jnp.zeros_like(dmod_ref)
            vec_ref[...] = jnp.zeros_like(vec_ref)
            dmix_ref[...] = jnp.zeros_like(dmix_ref)
            halo_ref[...] = jnp.zeros_like(halo_ref)
            if n_red:
                scatter().start()

        if n_red:
            @pl.when(step == 1)
            def _():
                scatter().exchange()

            @pl.when(step == N_TILES // 2)
            def _():
                scatter().fold()

        def acc(row, lo, val):
            hi = lo + val.shape[1]
            vec_ref[row:row + 1, lo:hi] += jnp.sum(val, axis=0, keepdims=True)

        xt = x_ref[...]
        yt = y_ref[...]
        shift = mod_ref[layer:layer + 1, 0:D_MODEL]
        scale = mod_ref[layer:layer + 1, D_MODEL:2 * D_MODEL]
        gate = mod_ref[layer:layer + 1, 2 * D_MODEL:]
        ln_gain = lng_ref[layer:layer + 1, :]

        zn, zrstd = _ln(ALPHA * xt + gate * yt)
        if is_last:
            diff = a_ref[...] - b_ref[...]
            acc(VEC_LOSS, 0, diff * diff)
            dout = diff * (1.0 / D_MODEL)
        else:
            dout = a_ref[...]
        acc(VEC_LNG, 0, dout * zn)
        acc(VEC_LNB, 0, dout)
        dz = _ln_bwd(dout * ln_gain, zn, zrstd)
        acc(VEC_GATE, 0, dz * yt)
        dy = (dz * gate).astype(BF16)
        dy_ref[...] = dy
        dcat = _dot_nt(dy, wout_ref[...])

        proj = proj_ref[...]
        prev = jnp.where(tile > 0, prev_ref[...], 0.0)
        cat, k = _mix_forward(proj, prev, tile, weights, cdf_ref[...])
        cat_ref[...] = cat.astype(BF16)

        dga, dq = [], []
        for g in range(N_GROUPS):
            sl = slice(g * GROUP, (g + 1) * GROUP)
            pscale = weights.pool_scale(g)
            dya = dcat[:, sl]
            dyp = dya * k["ga_act"][:, sl]
            dga.append(dya * k["pw"][g] * pscale * k["ga_grad"][:, sl])
            acc(VEC_POOL, g * GROUP, dyp * k["pw"][g])
            dpw = (dyp * pscale).astype(BF16)
            rows = pl.ds(base + ROW_WPOOL + g * GROUP, GROUP)
            small_ref[rows, :] += _dot_tn(k["pooled"][g], dpw)
            dq.append(_dot_nt(dpw, weights.pool(g)))
        dpooled = jnp.concatenate(dq, axis=1)
        scaled = jnp.concatenate([dq[g] * k["inv_counts"][g] for g in range(N_GROUPS)], axis=1)
        sums = _window_sums(jnp.concatenate([scaled, halo_ref[...]], axis=0), False)
        halo_ref[...] = scaled[0:HALO]
        dxa = jnp.concatenate(sums, axis=1) - dpooled

        du, dv, dgb = [], [], []
        for h in range(N_HEADS):
            sl = slice(h * HEAD, (h + 1) * HEAD)
            dyb = dcat[:, D_POOL + h * HEAD:D_POOL + (h + 1) * HEAD]
            m_h = k["mixed"][h]
            ug = k["u_act"][:, sl] * dyb
            du.append(dyb * m_h * k["gb_act"][:, sl] * k["u_grad"][:, sl])
            dgb.append(ug * m_h * k["gb_grad"][:, sl])
            dmixed = ug * k["gb_act"][:, sl]
            dmixed_bf = dmixed.astype(BF16)
            w_h = weights.mix(h)
            dvln_parts = []
            dmix_sum = dmix_ref[h]
            wsgu_rows = pl.ds(base + ROW_WSGU + h * CHUNK, CHUNK)
            dws = small_ref[wsgu_rows, :]
            for c in range(TM // CHUNK):
                cs = slice(c * CHUNK, (c + 1) * CHUNK)
                dmix_sum = dmix_sum + dmixed[cs]
                dws = dws + _dot_nt(dmixed_bf[cs], k["vln"][h][cs])
                dvln_parts.append(_dot_tn(w_h, dmixed_bf[cs]))
            dmix_ref[h] = dmix_sum
            small_ref[wsgu_rows, :] = dws
            dvln = jnp.concatenate(dvln_parts, axis=0)
            acc(VEC_SGU, h * HEAD, dvln * k["vn"][h])
            acc(VEC_SGU, D_SGU + h * HEAD, dvln)
            dvv = _ln_bwd(dvln * weights.ln_gain(h), k["vn"][h], k["vrstd"][h])
            dv.append(dvv * k["v_grad"][:, sl])

        dproj = jnp.concatenate([dxa] + dga + du + dv + dgb, axis=1).astype(BF16)
        dproj_ref[...] = dproj
        dh = _dot(dproj, wint_ref[...])

        xn, xrstd = _ln(xt)
        h_ref[...] = (xn * (1.0 + scale) + shift).astype(BF16)
        acc(VEC_SCALE, 0, dh * xn)
        acc(VEC_SHIFT, 0, dh)
        dx_ref[...] = _ln_bwd(dh * (1.0 + scale), xn, xrstd) + ALPHA * dz

        @pl.when(step == N_TILES - 1)
        def _():
            def put(row0, vec_row, lo, n):
                for r in range(n):
                    small_ref[base + row0 + r:base + row0 + r + 1, :] = (
                        vec_ref[vec_row:vec_row + 1, lo + r * 128:lo + (r + 1) * 128])

            put(ROW_PSCALE, VEC_POOL, 0, 4)
            put(ROW_SLNG, VEC_SGU, 0, 4)
            put(ROW_SLNB, VEC_SGU, D_SGU, 4)
            put(ROW_LNG, VEC_LNG, 0, 8)
            put(ROW_LNB, VEC_LNB, 0, 8)
            ones = jnp.ones((8, HEAD), F32)
            t = lax.broadcasted_iota(jnp.int32, (CHUNK, CHUNK), 0)
            s = lax.broadcasted_iota(jnp.int32, (CHUNK, CHUNK), 1)
            for h in range(N_HEADS):
                bias_rows = lax.dot_general(ones, dmix_ref[h], (((1,), (1,)), ((), ())),
                                            preferred_element_type=F32, precision=lax.Precision.HIGHEST)
                small_ref[base + ROW_BSGU + h:base + ROW_BSGU + h + 1, :] = bias_rows[0:1]
                rows = pl.ds(base + ROW_WSGU + h * CHUNK, CHUNK)
                small_ref[rows, :] = jnp.where(t >= s, small_ref[rows, :], 0.0)
            pieces = ((0, VEC_SHIFT, 0, 768),
                      (1, VEC_SHIFT, 768, 256), (1, VEC_SCALE, 0, 512),
                      (2, VEC_SCALE, 512, 512), (2, VEC_GATE, 0, 256),
                      (3, VEC_GATE, 256, 768))
            filled = [0] * 4
            for q, vec_row, lo, n in pieces:
                row = 4 * layer + q
                dmod_ref[row:row + 1, filled[q]:filled[q] + n] = vec_ref[vec_row:vec_row + 1, lo:lo + n]
                filled[q] += n
            if n_carry:
                for other in range(layer + 1, DEPTH):
                    rows = pl.ds(other * PACK_ROWS, PACK_ROWS)
                    small_ref[rows, :] = carry_refs[0][rows, :]
                    dmod_ref[4 * other:4 * other + 4, :] = carry_refs[1][4 * other:4 * other + 4, :]
            loss_ref[...] = vec_ref[VEC_LOSS:VEC_LOSS + 1, :]
            if n_red:
                scatter().finish()
                scatter().wait_sends()

    rev = lambda w: pl.BlockSpec((TM, w), lambda i: (N_TILES - 1 - i, 0))
    prev_spec = pl.BlockSpec(
        (HALO, D_POOL), lambda i: (jnp.maximum((N_TILES - 1 - i) * (TM // HALO) - 1, 0), 0))
    comm_scratch = []
    for p in reduce:
        comm_scratch += _ChipReduceScatter.buffers(p.shape[2], p.shape[3], p.dtype)
    if n_red:
        comm_scratch += _ChipReduceScatter.semaphores(n_red)
    return pl.pallas_call(
        body,
        name=name,
        grid=(N_TILES,),
        in_specs=[rev(D_MODEL), rev(D_MODEL) if is_last else pl.BlockSpec((TM, D_MODEL), lambda i: (0, 0)),
                  rev(D_MODEL), rev(D_IN), prev_spec, rev(D_MODEL), rev(2 * D_SGU),
                  _const_spec((DEPTH, 3 * D_MODEL)), _const_spec((D_IN, D_MODEL)), _const_spec((D_MODEL, D_MODEL))]
                 + [_const_spec(s) for s in SMALL_SPECS] + [_const_spec((DEPTH, D_MODEL))]
                 + [_const_spec(c.shape) for c in carry] + [ANY] * n_red,
        out_specs=[rev(D_MODEL), rev(D_IN), rev(D_MODEL), rev(D_MODEL), rev(D_MODEL),
                   _const_spec((DEPTH * PACK_ROWS, 128)), _const_spec((8, DMOD_COLS)), _const_spec((1, D_MODEL))]
                  + [_const_spec(p.shape[2:]) for p in reduce],
        out_shape=[jax.ShapeDtypeStruct((SEQ, D_MODEL), F32), jax.ShapeDtypeStruct((SEQ, D_IN), BF16),
                   jax.ShapeDtypeStruct((SEQ, D_MODEL), BF16), jax.ShapeDtypeStruct((SEQ, D_MODEL), BF16),
                   jax.ShapeDtypeStruct((SEQ, D_MODEL), BF16), jax.ShapeDtypeStruct((DEPTH * PACK_ROWS, 128), F32),
                   jax.ShapeDtypeStruct((8, DMOD_COLS), F32), jax.ShapeDtypeStruct((1, D_MODEL), F32)]
                  + [jax.ShapeDtypeStruct(p.shape[2:], F32) for p in reduce],
        scratch_shapes=[pltpu.VMEM((8, D_MODEL), F32), pltpu.VMEM((N_HEADS, CHUNK, HEAD), F32),
                        pltpu.VMEM((HALO, D_POOL), F32)] + comm_scratch,
        compiler_params=pltpu.CompilerParams(dimension_semantics=("arbitrary",), vmem_limit_bytes=VMEM_LIMIT),
    )(a, b, x, proj, proj, y, cdf, mod, w_int, w_outf, *small, ln_g, *carry, *reduce)


def _grad_matmuls(dproj, h, cat, dy, name):
    in_cols, out_cols = D_IN // 4, D_MODEL // 2
    in_steps = D_IN // in_cols

    def body(dproj_ref, h_ref, cat_ref, dy_ref, gin_ref, gout_ref):
        step = pl.program_id(0)

        @pl.when(step < in_steps)
        def _():
            gin_ref[...] = _dot_tn(dproj_ref[...], h_ref[...]).astype(BF16)

        @pl.when(step >= in_steps)
        def _():
            gout_ref[...] = _dot_tn(cat_ref[...], dy_ref[...]).astype(BF16)

    in_block = lambda j: jnp.minimum(j, in_steps - 1)
    out_block = lambda j: jnp.maximum(j - in_steps, 0)
    return pl.pallas_call(
        body,
        name=name,
        grid=(in_steps + D_MODEL // out_cols,),
        in_specs=[pl.BlockSpec((SEQ, in_cols), lambda j: (0, in_block(j))), _const_spec((SEQ, D_MODEL)),
                  pl.BlockSpec((SEQ, out_cols), lambda j: (0, out_block(j))), _const_spec((SEQ, D_MODEL))],
        out_specs=[pl.BlockSpec((in_cols, D_MODEL), lambda j: (in_block(j), 0)),
                   pl.BlockSpec((out_cols, D_MODEL), lambda j: (out_block(j), 0))],
        out_shape=[jax.ShapeDtypeStruct((D_IN, D_MODEL), BF16), jax.ShapeDtypeStruct((D_MODEL, D_MODEL), BF16)],
        compiler_params=pltpu.CompilerParams(dimension_semantics=("arbitrary",), vmem_limit_bytes=VMEM_LIMIT),
    )(dproj, h, cat, dy)


def _adamw_math(w, g, m, v):
    m = ADAM_B1 * m + (1.0 - ADAM_B1) * g
    v = ADAM_B2 * v + (1.0 - ADAM_B2) * (g * g)
    m_hat = m / (1.0 - ADAM_B1 ** ADAM_STEP)
    v_hat = v / (1.0 - ADAM_B2 ** ADAM_STEP)
    delta = -ADAM_LR * (m_hat / (jnp.sqrt(v_hat) + ADAM_EPS) + ADAM_WD * w)
    return delta, m, v


def _adamw(w, grads, m, v, block_rows, name):
    rows, cols = grads[0].shape
    blocks = rows // block_rows

    def body(w_ref, m_ref, v_ref, *rest):
        g_refs, (g_ref, d_ref, nm_ref, nv_ref) = rest[:DEPTH], rest[DEPTH:]
        for layer in range(DEPTH):
            @pl.when(pl.program_id(0) == layer)
            def _():
                g = g_refs[layer][...]
                g_ref[...] = g
                d_ref[...], nm_ref[...], nv_ref[...] = _adamw_math(w_ref[...], g, m_ref[...], v_ref[...])

    def grad_spec(layer):
        return pl.BlockSpec((block_rows, cols),
                            lambda l, i: (jnp.where(l == layer, i, jnp.where(l < layer, 0, blocks - 1)), 0))

    spec = pl.BlockSpec((block_rows, cols), lambda l, i: (l * blocks + i, 0))
    return pl.pallas_call(
        body,
        name=name,
        grid=(DEPTH, blocks),
        in_specs=[spec] * 3 + [grad_spec(layer) for layer in range(DEPTH)],
        out_specs=[spec] * 4,
        out_shape=[jax.ShapeDtypeStruct(w.shape, F32)] * 4,
        compiler_params=pltpu.CompilerParams(dimension_semantics=("arbitrary", "arbitrary"),
                                             vmem_limit_bytes=VMEM_LIMIT),
    )(w, m, v, *grads)


MESH = pl.DeviceIdType.MESH
SIBLING = 1
ANY = pl.BlockSpec(memory_space=pl.ANY)
VMEM = pl.BlockSpec(memory_space=pltpu.VMEM)


def _me():
    return lax.axis_index("x"), lax.axis_index("y"), lax.axis_index("c")


def _peer(r):
    x, y, c = _me()
    return (1 - x if r & 4 else x, 1 - y if r & 2 else y, 1 - c if r & 1 else c)


def _index(dev):
    return 4 * dev[0] + 2 * dev[1] + dev[2]


def _remote(src, dst, send_sem, recv_sem, dev):
    return pltpu.make_async_remote_copy(src_ref=src, dst_ref=dst, send_sem=send_sem, recv_sem=recv_sem,
                                        device_id=dev, device_id_type=MESH)


ACROSS_X, ACROSS_Y, ACROSS_BOTH = 4, 2, 6
GATHER_SEMS = 11


class _TwoLevelGather:
    def __init__(self, out, send_sems, recv_sems, src=None):
        self.out, self.send_sems, self.recv_sems, self.src = out, send_sems, recv_sems, src
        self.rows = (out.shape[0] // N_DEV) if len(out.shape) == 2 else out.shape[1]
        self.half = self.rows // 2

    def _slot(self, block):
        if len(self.out.shape) == 2:
            return self.out.at[pl.ds(pl.multiple_of(_index(block) * self.rows, self.rows), self.rows)]
        return self.out.at[_index(block)]

    def _copy(self, k, block, part, to, src=None):
        slot = self._slot(block)
        if part is not None:
            rows = pl.ds(part * self.half, self.half)
            slot = slot.at[rows]
            src = None if src is None else src.at[rows]
        return _remote(slot if src is None else src, slot, self.send_sems.at[k], self.recv_sems.at[k], to)

    def _mine(self):
        me = _me()
        src = self._slot(me) if self.src is None else self.src
        x, y = _peer(ACROSS_X), _peer(ACROSS_Y)
        return [self._copy(1, me, 0, x, src), self._copy(3, me, 1, y, src), self._copy(0, me, None, _peer(SIBLING), src),
                self._copy(2, me, 1, x, src), self._copy(4, me, 0, y, src)]

    def _relayed(self):
        return [self._copy(5, _peer(ACROSS_X), 0, _peer(ACROSS_Y)), self._copy(6, _peer(ACROSS_Y), 1, _peer(ACROSS_X))]

    def _passed(self):
        sib, far = _peer(SIBLING), _peer(ACROSS_BOTH)
        return [self._copy(7, _peer(ACROSS_X), None, sib), self._copy(8, _peer(ACROSS_Y), None, sib),
                self._copy(9, far, 0, sib), self._copy(10, far, 1, sib)]

    def _arrival(self, k, r, part):
        return self._copy(k, _peer(r), part, _me())

    def send_first(self):
        for cp in self._mine()[:3]:
            cp.start()

    def send_second(self):
        for cp in self._mine()[3:]:
            cp.start()

    def send_mine(self):
        self.send_first()
        self.send_second()

    def relay(self):
        relayed = self._relayed()
        self._arrival(1, ACROSS_X, 0).wait_recv()
        relayed[0].start()
        self._arrival(3, ACROSS_Y, 1).wait_recv()
        relayed[1].start()

    def pass_near(self):
        passed = self._passed()
        self._arrival(2, ACROSS_X, 1).wait_recv()
        passed[0].start()
        self._arrival(4, ACROSS_Y, 0).wait_recv()
        passed[1].start()

    def pass_far(self):
        passed = self._passed()
        self._arrival(5, ACROSS_BOTH, 0).wait_recv()
        passed[2].start()
        self._arrival(6, ACROSS_BOTH, 1).wait_recv()
        passed[3].start()

    def pass_on(self):
        self.pass_near()
        self.pass_far()

    def wait_sibling(self):
        self._arrival(0, SIBLING, None).wait_recv()

    def wait_passed(self, r):
        if r == ACROSS_BOTH:
            self._arrival(9, r ^ SIBLING, 0).wait_recv()
            self._arrival(10, r ^ SIBLING, 1).wait_recv()
        else:
            self._arrival(7 if r == ACROSS_X else 8, r ^ SIBLING, None).wait_recv()

    def wait_rest(self):
        self.wait_sibling()
        for r in (ACROSS_X, ACROSS_Y, ACROSS_BOTH):
            self.wait_passed(r)

    def wait_sends(self):
        for cp in self._mine() + self._relayed() + self._passed():
            cp.wait_send()


class _ChipReduceScatter:
    SLOTS = 6

    def __init__(self, arrays, l_sem, d_send, d_recv, i_send, i_recv):
        self.arrays = arrays
        self.l_sem, self.d_send, self.d_recv, self.i_send, self.i_recv = l_sem, d_send, d_recv, i_send, i_recv

    @staticmethod
    def buffers(rows, cols, dtype, staged=True):
        stage = [pltpu.VMEM((4, rows, cols), dtype)] if staged else []
        return stage + [pltpu.VMEM((4, rows, cols), dtype), pltpu.VMEM((3, rows, cols), dtype),
                        pltpu.VMEM((2, rows, cols), dtype), pltpu.VMEM((2, rows // 2, cols), dtype)]

    @classmethod
    def semaphores(cls, n):
        return [pltpu.SemaphoreType.DMA((n,)), pltpu.SemaphoreType.DMA((n, 4)), pltpu.SemaphoreType.DMA((n, 4)),
                pltpu.SemaphoreType.DMA((n, cls.SLOTS)), pltpu.SemaphoreType.DMA((n, cls.SLOTS))]

    def _pick(self, which):
        return list(enumerate(self.arrays)) if which is None else [(n, self.arrays[n]) for n in which]

    @staticmethod
    def _chip(r):
        dev = _me() if r is None else _peer(r)
        return 2 * dev[0] + dev[1]

    def _staging(self, which):
        c = _me()[2]
        return [pltpu.make_async_copy(a["part"].at[pl.ds(0, 4), c], a["stage"], self.l_sem.at[n])
                for n, a in self._pick(which) if a["staged"]]

    def _first(self, which, chip):
        other = 1 - _me()[2]
        return [_remote(a["part"].at[chip, other], a["sib"].at[chip], self.d_send.at[n, chip], self.d_recv.at[n, chip],
                        _peer(SIBLING)) for n, a in self._pick(which)]

    @staticmethod
    def _halves(a):
        half = a["rcv"].shape[1] // 2
        return pl.ds(0, half), pl.ds(half, half)

    def _hops(self, n, a):
        h0, h1 = self._halves(a)
        x, y = _peer(ACROSS_X), _peer(ACROSS_Y)
        snd, rcv, relay = a["snd"], a["rcv"], a["relay"]
        pairs = [(snd.at[2, h0], relay.at[0], x), (snd.at[2, h1], relay.at[1], y),
                 (snd.at[0, h0], rcv.at[0, h0], x), (snd.at[0, h1], rcv.at[0, h1], x),
                 (snd.at[1, h1], rcv.at[1, h1], y), (snd.at[1, h0], rcv.at[1, h0], y)]
        return [_remote(s, d, self.i_send.at[n, k], self.i_recv.at[n, k], to) for k, (s, d, to) in enumerate(pairs)]

    def _mine(self, a, chip, rows=None):
        src = a["stage"].at[chip] if a["staged"] else a["part"].at[chip, _me()[2]]
        mine, sib = (src[...], a["sib"][chip]) if rows is None else (src[rows, :], a["sib"][chip, rows, :])
        return mine.astype(F32) + sib.astype(F32)

    def start(self, which=None, chips=None):
        if chips is None:
            for cp in self._staging(which):
                cp.start()
        for chip in range(4) if chips is None else chips:
            for cp in self._first(which, chip):
                cp.start()

    def send_far(self, which=None):
        far = self._chip(ACROSS_BOTH)
        for cp in self._staging(which):
            cp.wait()
        for cp in self._first(which, far):
            cp.wait_recv()
        for n, a in self._pick(which):
            hops = self._hops(n, a)
            a["snd"][2] = self._mine(a, far).astype(a["snd"].dtype)
            hops[0].start()
            hops[1].start()

    def send_near(self, r, which=None):
        chip = self._chip(r)
        for cp in self._first(which, chip):
            cp.wait_recv()
        for n, a in self._pick(which):
            h0, h1 = self._halves(a)
            hops = self._hops(n, a)
            if r == ACROSS_X:
                a["snd"][0, h0, :] = self._mine(a, chip, h0).astype(a["snd"].dtype)
                hops[2].start()
            else:
                a["snd"][1, h1, :] = self._mine(a, chip, h1).astype(a["snd"].dtype)
                hops[4].start()

    def exchange(self, which=None):
        self.send_far(which)
        self.send_near(ACROSS_X, which)
        self.send_near(ACROSS_Y, which)

    def fold(self, which=None):
        across_x, across_y = self._chip(ACROSS_X), self._chip(ACROSS_Y)
        for n, a in self._pick(which):
            h0, h1 = self._halves(a)
            hops = self._hops(n, a)
            dtype = a["snd"].dtype
            hops[1].wait_recv()
            a["snd"][0, h1, :] = (self._mine(a, across_x, h1) + a["relay"][1].astype(F32)).astype(dtype)
            hops[3].start()
            hops[0].wait_recv()
            a["snd"][1, h0, :] = (self._mine(a, across_y, h0) + a["relay"][0].astype(F32)).astype(dtype)
            hops[5].start()

    def finish(self, which=None):
        home = self._chip(None)
        for cp in self._first(which, home):
            cp.wait_recv()
        for n, a in self._pick(which):
            hops = self._hops(n, a)
            a["out"][...] = self._mine(a, home)
            hops[2].wait_recv()
            hops[3].wait_recv()
            a["out"][...] += a["rcv"][0].astype(F32)
            hops[4].wait_recv()
            hops[5].wait_recv()
            a["out"][...] += a["rcv"][1].astype(F32)

    def wait_sends(self, which=None):
        for chip in range(4):
            for cp in self._first(which, chip):
                cp.wait_send()
        for n, a in self._pick(which):
            for cp in self._hops(n, a):
                cp.wait_send()


def _direct_exchange(src_of, dst_of, send_sems, recv_sems):
    me = _me()
    copies = [_remote(src_of(_peer(r)), dst_of(me), send_sems.at[r - 1], recv_sems.at[r - 1], _peer(r))
              for r in range(1, N_DEV)]
    for cp in copies:
        cp.start()
    return copies


def _wait_direct(copies):
    for cp in copies:
        cp.wait_recv()
    for cp in copies:
        cp.wait_send()


def _forward(x, c, w_ada, b_ada, small, ln_g, ln_b, mine, following):
    assert DEPTH == 2
    cols = w_ada.shape[2]
    shard = mine[0].shape[0]
    pair = 2 * shard

    def body(x_hbm, c_ref, wada_hbm, bada_ref, wpool_ref, pscale_ref, slng_ref, slnb_ref, wsgu_ref, bsgut_ref,
             lng_ref, lnb_ref, wint_hbm, wout_hbm, next_in_hbm, next_out_hbm,
             out0_ref, y0_ref, cdf0_ref, proj0_hbm, out1_ref, y1_ref, cdf1_ref, proj1_hbm,
             wint_keep, wout_keep, wint_next, wout_next, acts_ref, mod_ref,
             wint_v, wout_v, h_buf, proj_blk, proj_tile, halo_ref, x_ref, w_send, w_recv, w_local, p_sems,
             t_sems, in_sems, act_all, act_src, part, mod_recv, wada_ref, a_send, a_recv, m_send, m_recv,
             n_send, n_recv, n_local, f_sems):
        step = pl.program_id(0)
        chip_of = lambda dev: 2 * dev[0] + dev[1]

        def hosted():
            return [_TwoLevelGather(out, n_send.at[n], n_recv.at[n], src=src)
                    for n, (out, src) in enumerate(((wint_next, next_in_hbm), (wout_next, next_out_hbm)))]

        def hosted_own():
            me = _me()
            return [pltpu.make_async_copy(g.src, g._slot(me), n_local.at[n]) for n, g in enumerate(hosted())]

        def gathers():
            return (_TwoLevelGather(wint_v, w_send.at[0], w_recv.at[0], src=wint_hbm),
                    _TwoLevelGather(wout_v, w_send.at[1], w_recv.at[1], src=wout_hbm))

        def keeps():
            return [pltpu.make_async_copy(wint_v, wint_keep, w_local.at[2]),
                    pltpu.make_async_copy(wout_v, wout_keep, w_local.at[3])]

        def tile_read(proj_hbm, t):
            slot = t % 2
            return pltpu.make_async_copy(proj_hbm.at[pl.ds(pl.multiple_of(t * TM, TM), TM)], proj_tile.at[slot],
                                         t_sems.at[slot])

        def modulated(layer):
            shift = mod_ref[layer:layer + 1, 0:D_MODEL]
            scale = mod_ref[layer:layer + 1, D_MODEL:2 * D_MODEL]

            @pl.loop(0, N_TILES)
            def _(t):
                rows = pl.ds(pl.multiple_of(t * TM, TM), TM)
                xn, _ = _ln(x_ref[rows, :])
                h_buf[rows, :] = (xn * (1.0 + scale) + shift).astype(BF16)

        def projector(proj_hbm):
            writes = []

            def project(n, dev):
                first = pl.multiple_of(chip_of(dev) * pair, pair)
                if n >= 2:
                    writes[n - 2].wait()

                @pl.loop(0, N_TILES)
                def _(t):
                    rows = pl.ds(pl.multiple_of(t * TM, TM), TM)
                    proj_blk[n % 2, rows, :] = _dot_nt(h_buf[rows, :], wint_v[pl.ds(first, pair), :])

                cp = pltpu.make_async_copy(proj_blk.at[n % 2], proj_hbm.at[:, pl.ds(first, pair)], p_sems.at[n % 2])
                cp.start()
                writes.append(cp)

            return project, writes

        def first_layer_start():
            me = _me()
            halo_ref[...] = jnp.zeros_like(halo_ref)
            gather_in, gather_out = gathers()
            own_in = pltpu.make_async_copy(wint_hbm, gather_in._slot(me), w_local.at[0])
            own_out = pltpu.make_async_copy(wout_hbm, gather_out._slot(me), w_local.at[1])
            x_load = pltpu.make_async_copy(x_hbm, x_ref, in_sems.at[0])
            x_load.start()
            wada_load = pltpu.make_async_copy(wada_hbm, wada_ref, in_sems.at[1])
            wada_load.start()
            mine_index = _index(me)
            cval = c_ref[...]
            act_src[...] = jnp.zeros_like(act_src)
            act_src[0:1, :] = cval * jax.nn.sigmoid(cval)
            act_all[mine_index] = act_src[...]
            act_copies = _direct_exchange(lambda p: act_src, lambda m: act_all.at[_index(m)], a_send, a_recv)

            own_in.start()
            own_out.start()
            gather_in.send_first()

            _wait_direct(act_copies)
            acts = jnp.concatenate([act_all[j, 0:1, :] for j in range(N_DEV)], axis=0)
            acts_ref[...] = acts
            part[...] = jnp.zeros_like(part)
            wada_load.wait()
            for l in range(DEPTH):
                res = lax.dot_general(acts, wada_ref[l], (((1,), (0,)), ((), ())), preferred_element_type=F32,
                                      precision=lax.Precision.HIGHEST)
                for b in range(N_DEV):
                    part[b, l:l + 1, :] = res[b:b + 1, :]
            mod_recv[mine_index] = part[mine_index]
            mod_copies = _direct_exchange(lambda p: part.at[_index(p)], lambda m: mod_recv.at[_index(m)],
                                          m_send, m_recv)
            gather_in.send_second()
            gather_out.send_mine()

            _wait_direct(mod_copies)
            for l in range(DEPTH):
                for j in range(N_DEV):
                    sl = slice(j * cols, (j + 1) * cols)
                    mod_ref[l:l + 1, sl] = mod_recv[j, l:l + 1, :] + bada_ref[l:l + 1, sl]
            x_load.wait()
            modulated(0)

            project, writes = projector(proj0_hbm)
            gather_in.relay()
            own_in.wait()
            gather_in.wait_sibling()
            project(0, me)
            gather_in.pass_near()
            gather_in.wait_passed(ACROSS_X)
            project(1, _peer(ACROSS_X))
            gather_out.relay()
            for cp in hosted_own():
                cp.start()
            for g in hosted():
                g.send_mine()
            gather_in.wait_passed(ACROSS_Y)
            project(2, _peer(ACROSS_Y))
            gather_in.pass_far()
            gather_in.wait_passed(ACROSS_BOTH)
            project(3, _peer(ACROSS_BOTH))

            gather_out.pass_on()
            gather_out.wait_rest()
            own_out.wait()
            for cp in keeps():
                cp.start()
            writes[2].wait()
            writes[3].wait()
            tile_read(proj0_hbm, 0).start()

        def second_layer_start():
            me = _me()
            halo_ref[...] = jnp.zeros_like(halo_ref)
            next_in, next_out = hosted()
            modulated(1)

            def fetch(n, dev):
                rows = pl.ds(pl.multiple_of(chip_of(dev) * pair, pair), pair)
                cp = pltpu.make_async_copy(wint_next.at[rows], wint_v.at[rows], f_sems.at[n])
                cp.start()
                return cp

            project, writes = projector(proj1_hbm)
            for cp in hosted_own():
                cp.wait()
            devs = [me, _peer(ACROSS_X), _peer(ACROSS_Y), _peer(ACROSS_BOTH)]
            next_in.wait_sibling()
            fetched = [fetch(0, devs[0])]
            for n, r in enumerate((ACROSS_X, ACROSS_Y, ACROSS_BOTH)):
                next_in.wait_passed(r)
                fetched.append(fetch(n + 1, devs[n + 1]))
                fetched[n].wait()
                project(n, devs[n])
            next_out.wait_rest()
            fetch_out = pltpu.make_async_copy(wout_next, wout_v, f_sems.at[4])
            fetch_out.start()
            fetched[3].wait()
            project(3, devs[3])
            fetch_out.wait()
            writes[2].wait()
            writes[3].wait()
            tile_read(proj1_hbm, 0).start()

        def run_tile(layer, tile, proj_hbm, out_ref, y_ref, cdf_ref):
            weights = _MixWeights(layer, wpool_ref, pscale_ref, slng_ref, slnb_ref, wsgu_ref, bsgut_ref)

            @pl.when(tile + 1 < N_TILES)
            def _():
                tile_read(proj_hbm, tile + 1).start()

            tile_read(proj_hbm, tile).wait()
            rows = pl.ds(pl.multiple_of(tile * TM, TM), TM)
            xt = x_ref[rows, :]
            gate = mod_ref[layer:layer + 1, 2 * D_MODEL:]
            proj = proj_tile[tile % 2]
            cat, cdf_ref[...] = _mix_forward(proj, halo_ref[...], tile, weights)
            halo_ref[...] = proj[TM - HALO:, 0:D_POOL]
            y = _dot(cat.astype(BF16), wout_v[...])
            y_ref[...] = y
            zn, _ = _ln(ALPHA * xt + gate * y)
            out = zn * lng_ref[layer:layer + 1, :] + lnb_ref[layer:layer + 1, :]
            out_ref[...] = out
            if layer + 1 < DEPTH:
                x_ref[rows, :] = out

        @pl.when(step < N_TILES)
        def _():
            @pl.when(step == 0)
            def _():
                first_layer_start()

            @pl.when(step == 1)
            def _():
                for g in hosted():
                    g.relay()

            @pl.when(step == N_TILES // 2)
            def _():
                for g in hosted():
                    g.pass_near()

            run_tile(0, step, proj0_hbm, out0_ref, y0_ref, cdf0_ref)

            @pl.when(step == N_TILES - 1)
            def _():
                for g in hosted():
                    g.pass_far()
                for g in gathers():
                    g.wait_sends()
                for cp in keeps():
                    cp.wait()

        @pl.when(step >= N_TILES)
        def _():
            @pl.when(step == N_TILES)
            def _():
                second_layer_start()

            run_tile(1, step - N_TILES, proj1_hbm, out1_ref, y1_ref, cdf1_ref)

            @pl.when(step == 2 * N_TILES - 1)
            def _():
                for g in hosted():
                    g.wait_sends()

    first = lambda w: pl.BlockSpec((TM, w), lambda i: (jnp.minimum(i, N_TILES - 1), 0))
    second = lambda w: pl.BlockSpec((TM, w), lambda i: (jnp.maximum(i - N_TILES, 0), 0))
    gather_sems = pltpu.SemaphoreType.DMA((2, GATHER_SEMS))
    seven = pltpu.SemaphoreType.DMA((7,))
    per_layer = [jax.ShapeDtypeStruct((SEQ, D_MODEL), F32), jax.ShapeDtypeStruct((SEQ, D_MODEL), F32),
                 jax.ShapeDtypeStruct((SEQ, 2 * D_SGU), F32), jax.ShapeDtypeStruct((SEQ, D_IN), F32)]
    gathered = [jax.ShapeDtypeStruct((D_IN, D_MODEL), BF16), jax.ShapeDtypeStruct((D_MODEL, D_MODEL), BF16)]
    res = pl.pallas_call(
        body,
        name="layers_fwd",
        grid=(DEPTH * N_TILES,),
        in_specs=[ANY, _const_spec(c.shape), ANY, _const_spec(b_ada.shape)] + [_const_spec(s) for s in SMALL_SPECS]
                 + [_const_spec((DEPTH, D_MODEL)), _const_spec((DEPTH, D_MODEL))] + [ANY] * 4,
        out_specs=[first(D_MODEL), first(D_MODEL), first(2 * D_SGU), ANY,
                   second(D_MODEL), second(D_MODEL), second(2 * D_SGU), ANY] + [ANY] * 4
                  + [_const_spec((N_DEV, D_MODEL)), _const_spec((DEPTH, 3 * D_MODEL))],
        out_shape=per_layer * 2 + gathered * 2 + [jax.ShapeDtypeStruct((N_DEV, D_MODEL), F32),
                                                  jax.ShapeDtypeStruct((DEPTH, 3 * D_MODEL), F32)],
        scratch_shapes=[pltpu.VMEM((D_IN, D_MODEL), BF16), pltpu.VMEM((D_MODEL, D_MODEL), BF16),
                        pltpu.VMEM((SEQ, D_MODEL), BF16), pltpu.VMEM((2, SEQ, pair), F32),
                        pltpu.VMEM((2, TM, D_IN), F32), pltpu.VMEM((HALO, D_POOL), F32),
                        pltpu.VMEM((SEQ, D_MODEL), F32),
                        gather_sems, gather_sems, pltpu.SemaphoreType.DMA((4,)), pltpu.SemaphoreType.DMA((2,)),
                        pltpu.SemaphoreType.DMA((2,)), pltpu.SemaphoreType.DMA((2,)),
                        pltpu.VMEM((N_DEV, 8, D_MODEL), F32), pltpu.VMEM((8, D_MODEL), F32),
                        pltpu.VMEM((N_DEV, 8, cols), F32), pltpu.VMEM((N_DEV, 8, cols), F32),
                        pltpu.VMEM(w_ada.shape, F32), seven, seven, seven, seven,
                        gather_sems, gather_sems, pltpu.SemaphoreType.DMA((2,)), pltpu.SemaphoreType.DMA((5,))],
        compiler_params=pltpu.CompilerParams(dimension_semantics=("arbitrary",), vmem_limit_bytes=VMEM_LIMIT),
    )(x, c, w_ada, b_ada, *small, ln_g, ln_b, *mine, *following)
    return res[0:4], res[4:8], res[8:10], res[10:12], res[12], res[13]


ADA_CHUNK = 256


def _grad_tail(dproj, h, cat, dy, small, dmod8, loss_lanes, w_ada, m_ada, v_ada, act_t, b_ada, m_bada, v_bada):
    shard_in, shard_out, shard_small = D_IN // N_DEV, D_MODEL // N_DEV, small.shape[2]
    cols = w_ada.shape[2]
    W_IN, W_OUT, SMALL = 0, 1, 2

    def body(dproj_hbm, h_hbm, cat_hbm, dy_hbm, small_hbm, dmod_ref, lanes_ref, wada_hbm, mada_hbm, vada_hbm,
             act_ref, bada_ref, mbada_ref, vbada_ref,
             gwin_ref, gwout_ref, stot_ref, loss_ref, gada_hbm, dada_hbm, nmada_hbm, nvada_hbm,
             gb_ref, db_ref, nmb_ref, nvb_ref,
             dproj_v, h_v, cat_v, dy_v, part_in, part_out, own_small, loss_src, loss_all, dmod_all, ada_in, ada_out,
             *rest):
        bufs, rest = rest[:13], rest[13:]
        load_sems, rs_sems = rest[0], rest[1:6]
        m_send, m_recv, g_send, g_recv, s_send, s_recv, ada_lsem, ada_ssem = rest[6:]
        mine = _index(_me())

        def update_ada():
            upper = (mine % 2) == 1

            def dmod_of(layer):
                rows = []
                for b in range(N_DEV):
                    r = dmod_all[b, pl.ds(4 * layer + mine // 2, 1), :]
                    rows.append(jnp.where(upper, r[:, cols:], r[:, :cols]))
                return jnp.concatenate(rows, axis=0)

            chunks = [(layer, c) for layer in range(DEPTH) for c in range(D_MODEL // ADA_CHUNK)]

            def loads(i):
                layer, c = chunks[i]
                rows = pl.ds(c * ADA_CHUNK, ADA_CHUNK)
                return [pltpu.make_async_copy(src.at[layer, rows], ada_in.at[i % 2, k], ada_lsem.at[i % 2, k])
                        for k, src in enumerate((wada_hbm, mada_hbm, vada_hbm))]

            def stores(i):
                layer, c = chunks[i]
                rows = pl.ds(c * ADA_CHUNK, ADA_CHUNK)
                return [pltpu.make_async_copy(ada_out.at[i % 2, k], dst.at[layer, rows], ada_ssem.at[i % 2, k])
                        for k, dst in enumerate((gada_hbm, dada_hbm, nmada_hbm, nvada_hbm))]

            for cp in loads(0):
                cp.start()
            dmods = {}
            for i, (layer, c) in enumerate(chunks):
                if i + 1 < len(chunks):
                    for cp in loads(i + 1):
                        cp.start()
                for cp in loads(i):
                    cp.wait()
                if i >= 2:
                    for cp in stores(i - 2):
                        cp.wait()
                if layer not in dmods:
                    dmods[layer] = dmod_of(layer)
                act = act_ref[pl.ds(c * ADA_CHUNK, ADA_CHUNK), :]
                g = act[:, 0:1] * dmods[layer][0:1, :]
                for b in range(1, N_DEV):
                    g = g + act[:, b:b + 1] * dmods[layer][b:b + 1, :]
                slot = i % 2
                delta, new_m, new_v = _adamw_math(ada_in[slot, 0], g, ada_in[slot, 1], ada_in[slot, 2])
                ada_out[slot, 0] = g
                ada_out[slot, 1] = delta
                ada_out[slot, 2] = new_m
                ada_out[slot, 3] = new_v
                for cp in stores(i):
                    cp.start()
            for i in (len(chunks) - 2, len(chunks) - 1):
                for cp in stores(i):
                    cp.wait()

            total = dmod_all[0]
            for b in range(1, N_DEV):
                total = total + dmod_all[b]
            width = total.shape[1]
            for layer in range(DEPTH):
                for q in range(4):
                    gb_ref[layer:layer + 1, q * width:(q + 1) * width] = total[4 * layer + q:4 * layer + q + 1, :]
            db_ref[...], nmb_ref[...], nvb_ref[...] = _adamw_math(bada_ref[...], gb_ref[...], mbada_ref[...],
                                                                  vbada_ref[...])

        order = (ACROSS_BOTH, ACROSS_X, ACROSS_Y, None)
        chips = [_ChipReduceScatter._chip(r) for r in order]
        loads = [pltpu.make_async_copy(s, d, load_sems.at[n]) for n, (s, d) in enumerate(
            ((cat_hbm, cat_v), (dy_hbm, dy_v), (h_hbm, h_v)))]
        loads += [pltpu.make_async_copy(dproj_hbm.at[:, pl.ds(pl.multiple_of(chip * 2 * shard_in, 2 * shard_in),
                                                             2 * shard_in)], dproj_v.at[n], load_sems.at[3 + n])
                  for n, chip in enumerate(chips)]
        for cp in loads:
            cp.start()
        arrays = [dict(part=part_in, out=gwin_ref, staged=False, sib=bufs[0], snd=bufs[1], rcv=bufs[2], relay=bufs[3]),
                  dict(part=part_out, out=gwout_ref, staged=False, sib=bufs[4], snd=bufs[5], rcv=bufs[6],
                       relay=bufs[7]),
                  dict(part=small_hbm, out=own_small, staged=True, stage=bufs[8], sib=bufs[9], snd=bufs[10],
                       rcv=bufs[11], relay=bufs[12])]
        scatter = _ChipReduceScatter(arrays, *rs_sems)
        scatter.start([SMALL])
        dmod_all[mine] = dmod_ref[...]
        dmod_copies = _direct_exchange(lambda p: dmod_ref, lambda m: dmod_all.at[_index(m)], m_send, m_recv)
        loss_src[...] = jnp.full(loss_src.shape, (0.5 / D_MODEL) * jnp.sum(lanes_ref[...]), F32)
        loss_all[mine] = loss_src[...]
        loss_copies = _direct_exchange(lambda p: loss_src, lambda m: loss_all.at[_index(m)], s_send, s_recv)

        loads[0].wait()
        loads[1].wait()
        for blk in range(2):
            res = _dot_tn(cat_v[:, blk * 512:(blk + 1) * 512], dy_v[...]).astype(BF16)
            for s in range(4):
                part_out[2 * blk + s // 2, s % 2] = res[s * shard_out:(s + 1) * shard_out]
        scatter.start([W_OUT])
        scatter.exchange([SMALL])

        gather = _TwoLevelGather(stot_ref, g_send, g_recv)
        loads[2].wait()
        for n, chip in enumerate(chips):
            loads[3 + n].wait()
            res = _dot_tn(dproj_v[n], h_v[...]).astype(BF16)
            part_in[chip, 0] = res[:shard_in]
            part_in[chip, 1] = res[shard_in:]
            scatter.start([W_IN], chips=[chip])
            if n == 0:
                scatter.exchange([W_OUT])
                scatter.fold([SMALL])
            if n == 1:
                scatter.send_far([W_IN])
                scatter.fold([W_OUT])
                scatter.finish([SMALL])
                stot_ref[mine] = own_small[...]
                gather.send_mine()
            if n == 2:
                scatter.send_near(ACROSS_X, [W_IN])
                gather.relay()
            if n == 3:
                scatter.send_near(ACROSS_Y, [W_IN])
        scatter.fold([W_IN])
        scatter.finish([W_OUT])
        _wait_direct(dmod_copies)
        update_ada()
        gather.pass_on()
        gather.wait_rest()
        _wait_direct(loss_copies)
        total = loss_all[0]
        for j in range(1, N_DEV):
            total = total + loss_all[j]
        loss_ref[...] = total
        scatter.finish([W_IN])
        gather.wait_sends()
        scatter.wait_sends()

    buffers = _ChipReduceScatter.buffers
    comm_scratch = (buffers(shard_in, D_MODEL, BF16, staged=False) + buffers(shard_out, D_MODEL, BF16, staged=False)
                    + buffers(shard_small, 128, F32))
    comm_scratch += [pltpu.SemaphoreType.DMA((7,))] + _ChipReduceScatter.semaphores(3)
    comm_scratch += [pltpu.SemaphoreType.DMA((n,)) for n in (7, 7, GATHER_SEMS, GATHER_SEMS, 7, 7)]
    comm_scratch += [pltpu.SemaphoreType.DMA((2, 3)), pltpu.SemaphoreType.DMA((2, 4))]
    return pl.pallas_call(
        body,
        name="grad_tail",
        in_specs=[ANY] * 5 + [VMEM, VMEM] + [ANY] * 3 + [VMEM] * 4,
        out_specs=[VMEM] * 4 + [ANY] * 4 + [VMEM] * 4,
        out_shape=[jax.ShapeDtypeStruct((shard_in, D_MODEL), F32), jax.ShapeDtypeStruct((shard_out, D_MODEL), F32),
                   jax.ShapeDtypeStruct((N_DEV, shard_small, 128), F32), jax.ShapeDtypeStruct((8, 128), F32)]
                  + [jax.ShapeDtypeStruct(w_ada.shape, F32)] * 4 + [jax.ShapeDtypeStruct(b_ada.shape, F32)] * 4,
        scratch_shapes=[pltpu.VMEM((4, SEQ, 2 * shard_in), BF16), pltpu.VMEM(h.shape, BF16), pltpu.VMEM(cat.shape, BF16),
                        pltpu.VMEM(dy.shape, BF16), pltpu.VMEM((4, 2, shard_in, D_MODEL), BF16),
                        pltpu.VMEM((4, 2, shard_out, D_MODEL), BF16), pltpu.VMEM((shard_small, 128), F32),
                        pltpu.VMEM((8, 128), F32), pltpu.VMEM((N_DEV, 8, 128), F32),
                        pltpu.VMEM((N_DEV,) + dmod8.shape, F32), pltpu.VMEM((2, 3, ADA_CHUNK, cols), F32),
                        pltpu.VMEM((2, 4, ADA_CHUNK, cols), F32)] + comm_scratch,
        compiler_params=pltpu.CompilerParams(vmem_limit_bytes=VMEM_LIMIT),
    )(dproj, h, cat, dy, small, dmod8, loss_lanes, w_ada, m_ada, v_ada, act_t, b_ada, m_bada, v_bada)


SMALL_NAMES = ("w_pool", "w_sgu", "pool_scale", "sgu_ln_g", "sgu_ln_b", "b_sgu", "ln_g", "ln_b")
SMALL_ROWS = (512, 512, 4, 4, 4, 4, 8, 8)


def _adamw_small(g_packed, ws, ms, vs, name):
    n = len(SMALL_NAMES)

    def body(g_ref, *refs):
        w_refs, m_refs, v_refs = refs[:n], refs[n:2 * n], refs[2 * n:3 * n]
        outs = refs[3 * n:]

        def update(p, at, g):
            delta, new_m, new_v = _adamw_math(w_refs[p][at], g, m_refs[p][at], v_refs[p][at])
            outs[p][at] = g
            outs[n + p][at] = delta
            outs[2 * n + p][at] = new_m
            outs[3 * n + p][at] = new_v

        row = 0
        for p, r in enumerate(SMALL_ROWS):
            shape = ws[p].shape
            for layer in range(DEPTH):
                first = layer * PACK_ROWS + row
                if len(shape) == 4:
                    for k in range(shape[1]):
                        update(p, (layer, k), g_ref[first + k * shape[2]:first + (k + 1) * shape[2], :])
                elif len(shape) == 3:
                    update(p, (layer,), g_ref[first:first + r, :])
                else:
                    g = jnp.concatenate([g_ref[first + k:first + k + 1, :] for k in range(r)], axis=1)
                    update(p, (slice(layer, layer + 1), slice(None)), g)
            row += r

    res = pl.pallas_call(
        body,
        name=name,
        out_shape=[jax.ShapeDtypeStruct(w.shape, F32) for w in ws] * 4,
        compiler_params=pltpu.CompilerParams(vmem_limit_bytes=VMEM_LIMIT),
    )(g_packed, *ws, *ms, *vs)
    return res[:n], res[n:2 * n], res[2 * n:3 * n], res[3 * n:]


def kernel(x, c, w_ada, b_ada, w_in, w_pool, pool_scale, sgu_ln_g, sgu_ln_b, w_sgu, b_sgu, w_out, ln_g, ln_b, loss_target, m_w_ada, m_b_ada, m_w_in, m_w_pool, m_pool_scale, m_sgu_ln_g, m_sgu_ln_b, m_w_sgu, m_b_sgu, m_w_out, m_ln_g, m_ln_b, v_w_ada, v_b_ada, v_w_in, v_w_pool, v_pool_scale, v_sgu_ln_g, v_sgu_ln_b, v_w_sgu, v_b_sgu, v_w_out, v_ln_g, v_ln_b):
    small_w = dict(w_pool=w_pool, w_sgu=w_sgu, pool_scale=pool_scale, sgu_ln_g=sgu_ln_g, sgu_ln_b=sgu_ln_b,
                   b_sgu=b_sgu, ln_g=ln_g, ln_b=ln_b)
    small_m = dict(w_pool=m_w_pool, w_sgu=m_w_sgu, pool_scale=m_pool_scale, sgu_ln_g=m_sgu_ln_g,
                   sgu_ln_b=m_sgu_ln_b, b_sgu=m_b_sgu, ln_g=m_ln_g, ln_b=m_ln_b)
    small_v = dict(w_pool=v_w_pool, w_sgu=v_w_sgu, pool_scale=v_pool_scale, sgu_ln_g=v_sgu_ln_g,
                   sgu_ln_b=v_sgu_ln_b, b_sgu=v_b_sgu, ln_g=v_ln_g, ln_b=v_ln_b)

    wint_loc = jnp.transpose(w_in, (0, 2, 1)).astype(BF16)
    wout_loc = w_out.astype(BF16)
    small = (w_pool, pool_scale, sgu_ln_g, sgu_ln_b, w_sgu, jnp.transpose(b_sgu, (0, 2, 1)))
    (out0, y0, cdf0, proj0), (cur, y1, cdf1, proj1), gathered0, gathered1, act_all, mod = _forward(
        x[0], c, w_ada, b_ada, small, ln_g, ln_b, [wint_loc[0], wout_loc[0]], [wint_loc[1], wout_loc[1]])
    w_int, w_outf = [gathered0[0], gathered1[0]], [gathered0[1], gathered1[1]]
    acts = [(x[0], proj0, y0, cdf0), (out0, proj1, y1, cdf1)]

    shard_in, shard_out = D_IN // N_DEV, D_MODEL // N_DEV
    a, b = cur, loss_target[0]
    loss_lanes, carry, pending = None, (), []
    g_w_in_t, g_w_out = [None] * DEPTH, [None] * DEPTH
    for l in reversed(range(DEPTH)):
        dx, dproj, h, cat, dy, small_grads, dmod8, lanes, *shards = _layer_backward(
            l, a, b, *acts[l], mod, w_int[l], w_outf[l], small, ln_g, l == DEPTH - 1, f"layer_bwd_{l}",
            carry=carry, reduce=pending)
        if shards:
            g_w_in_t[l + 1], g_w_out[l + 1] = shards
        if l == DEPTH - 1:
            loss_lanes = lanes
        if l > 0:
            g_in, g_out = _grad_matmuls(dproj, h, cat, dy, f"grad_w_{l}")
            pending = [g_in.reshape(4, 2, shard_in, D_MODEL), g_out.reshape(4, 2, shard_out, D_MODEL)]
        carry = (small_grads, dmod8)
        a = b = dx
    grad_x = a[None]

    (g_w_in_t[0], g_w_out[0], small_tot, loss_tile, g_w_ada, d_w_ada, nm_w_ada, nv_w_ada,
     g_b_ada, d_b_ada, nm_b_ada, nv_b_ada) = _grad_tail(
        dproj, h, cat, dy, small_grads.reshape(4, 2, DEPTH * PACK_ROWS // N_DEV, 128), dmod8, loss_lanes,
        w_ada, m_w_ada, v_w_ada, jnp.transpose(act_all), b_ada, m_b_ada, v_b_ada)
    loss = loss_tile[0, 0]

    flat = lambda t: t.reshape(-1, t.shape[-1])
    to_t = lambda t: flat(jnp.transpose(t, (0, 2, 1)))
    from_t = lambda t: jnp.transpose(t.reshape(DEPTH, shard_in, D_MODEL), (0, 2, 1))
    g_w_in, d_w_in, nm_w_in, nv_w_in = [from_t(t) for t in _adamw(to_t(w_in), g_w_in_t, to_t(m_w_in), to_t(v_w_in),
                                                                  shard_in // 2, "adamw_w_in")]
    gwout, d_w_out, nm_w_out, nv_w_out = [t.reshape(w_out.shape) for t in _adamw(
        flat(w_out), g_w_out, flat(m_w_out), flat(v_w_out), shard_out, "adamw_w_out")]
    small_out = _adamw_small(small_tot.reshape(DEPTH * PACK_ROWS, 128), [small_w[n] for n in SMALL_NAMES],
                             [small_m[n] for n in SMALL_NAMES], [small_v[n] for n in SMALL_NAMES], "adamw_small")
    gs, ds, ms, vs = [dict(zip(SMALL_NAMES, group)) for group in small_out]

    def ordered(w_ada_, b_ada_, w_in_, small, w_out_):
        return (w_ada_, b_ada_, w_in_, small["w_pool"], small["pool_scale"], small["sgu_ln_g"], small["sgu_ln_b"],
                small["w_sgu"], small["b_sgu"], w_out_, small["ln_g"], small["ln_b"])

    return (loss, grad_x,
            *ordered(g_w_ada, g_b_ada, g_w_in, gs, gwout),
            *ordered(d_w_ada, d_b_ada, d_w_in, ds, d_w_out),
            *ordered(nm_w_ada, nm_b_ada, nm_w_in, ms, nm_w_out),
            *ordered(nv_w_ada, nv_b_ada, nv_w_in, vs, nv_w_out))
```

```python
import jax
import jax.numpy as jnp
from jax import lax
from jax.experimental import pallas as pl
from jax.experimental.pallas import tpu as pltpu

F32 = jnp.float32
BF16 = jnp.bfloat16

D_MODEL = 1024
SEQ = 2048
DEPTH = 2
D_POOL = 512
D_SGU = 512
D_IN = 2560
N_GROUPS = 4
GROUP = 128
N_HEADS = 4
HEAD = 128
CHUNK = 128
WINDOWS = (2, 4, 8, 16)
ALPHA = (2.0 * DEPTH) ** 0.25
LN_EPS = 1e-5
N_DEV = 8

ADAM_LR = 0.001
ADAM_B1 = 0.9
ADAM_B2 = 0.999
ADAM_EPS = 1e-08
ADAM_WD = 0.01
ADAM_STEP = 10

TM = 256
HALO = 16
N_TILES = SEQ // TM
VMEM_LIMIT = 60 * 1024 * 1024

ROW_WPOOL = 0
ROW_WSGU = 512
ROW_PSCALE = 1024
ROW_SLNG = 1028
ROW_SLNB = 1032
ROW_BSGU = 1036
ROW_LNG = 1040
ROW_LNB = 1048
PACK_ROWS = 1088
DMOD_COLS = DEPTH * 3 * D_MODEL // 8

SQRT_HALF = 0.7071067811865476
INV_SQRT_2PI = 0.3989422804014327


def _ln(x):
    mu = jnp.mean(x, axis=-1, keepdims=True)
    xc = x - mu
    var = jnp.mean(xc * xc, axis=-1, keepdims=True)
    rstd = lax.rsqrt(var + LN_EPS)
    return xc * rstd, rstd


def _ln_bwd(dxn, xn, rstd):
    m1 = jnp.mean(dxn, axis=-1, keepdims=True)
    m2 = jnp.mean(dxn * xn, axis=-1, keepdims=True)
    return rstd * (dxn - m1 - xn * m2)


def _normal_cdf(x):
    return 0.5 * (1.0 + lax.erf(x * SQRT_HALF))


def _gelu_parts(x, cdf, with_grad):
    if not with_grad:
        return x * cdf, None
    return x * cdf, cdf + x * (INV_SQRT_2PI * jnp.exp(-0.5 * x * x))


def _silu_parts(x):
    s = jax.nn.sigmoid(x)
    return x * s, s * (1.0 + x * (1.0 - s))


def _dot(a, b):
    return lax.dot_general(a, b, (((1,), (0,)), ((), ())), preferred_element_type=F32)


def _dot_nt(a, b):
    return lax.dot_general(a, b, (((1,), (1,)), ((), ())), preferred_element_type=F32)


def _dot_tn(a, b):
    return lax.dot_general(a, b, (((0,), (0,)), ((), ())), preferred_element_type=F32)


def _row_index(tile):
    return tile * TM + lax.broadcasted_iota(jnp.int32, (TM, 1), 0)


def _window_sums(ext, forward):
    n = TM + HALO
    cur = ext
    outs = []
    for g in range(N_GROUPS):
        step = 1 << g
        cur = cur + pltpu.roll(cur, step if forward else n - step, 0)
        rows = cur[HALO:, :GROUP] if forward else cur[:TM, :GROUP]
        outs.append(rows)
        cur = cur[:, GROUP:] if g + 1 < N_GROUPS else None
    return outs


def _inverse_counts(rows):
    return [1.0 / jnp.minimum(rows + 1, w).astype(F32) for w in WINDOWS]


def _tril_bf16(w):
    t = lax.broadcasted_iota(jnp.int32, (CHUNK, CHUNK), 0)
    s = lax.broadcasted_iota(jnp.int32, (CHUNK, CHUNK), 1)
    return jnp.where(t >= s, w, 0.0).astype(BF16)


class _MixWeights:
    def __init__(self, layer, wpool_ref, pscale_ref, slng_ref, slnb_ref, wsgu_ref, bsgut_ref):
        self.layer = layer
        self.wpool_ref, self.pscale_ref, self.slng_ref, self.slnb_ref = wpool_ref, pscale_ref, slng_ref, slnb_ref
        self.wsgu_ref, self.bsgut_ref = wsgu_ref, bsgut_ref

    def pool(self, g):
        return self.wpool_ref[self.layer, g].astype(BF16)

    def pool_scale(self, g):
        return self.pscale_ref[self.layer:self.layer + 1, g * GROUP:(g + 1) * GROUP]

    def ln_gain(self, h):
        return self.slng_ref[self.layer, h:h + 1, :]

    def ln_bias(self, h):
        return self.slnb_ref[self.layer, h:h + 1, :]

    def mix(self, h):
        return _tril_bf16(self.wsgu_ref[self.layer, h])

    def mix_bias(self, h):
        return self.bsgut_ref[self.layer, :, h:h + 1]


SMALL_SPECS = ((DEPTH, N_GROUPS, GROUP, GROUP), (DEPTH, D_POOL), (DEPTH, N_HEADS, HEAD), (DEPTH, N_HEADS, HEAD),
               (DEPTH, N_HEADS, CHUNK, CHUNK), (DEPTH, CHUNK, N_HEADS))


def _mix_forward(proj, halo, tile, w, cdf=None):
    keep = cdf is not None
    rows = _row_index(tile)
    inv_counts = _inverse_counts(rows)
    xa = proj[:, 0:D_POOL]
    ga = proj[:, D_POOL:2 * D_POOL]
    sums = _window_sums(jnp.concatenate([halo, xa], axis=0), True)
    ga_act, ga_grad = _silu_parts(ga)
    pooled, pw, ya = [], [], []
    for g in range(N_GROUPS):
        sl = slice(g * GROUP, (g + 1) * GROUP)
        p = (sums[g] * inv_counts[g] - xa[:, sl]).astype(BF16)
        q = _dot(p, w.pool(g))
        pooled.append(p)
        pw.append(q)
        ya.append(q * w.pool_scale(g) * ga_act[:, sl])

    u = proj[:, 2 * D_POOL:2 * D_POOL + D_SGU]
    v = proj[:, 2 * D_POOL + D_SGU:2 * D_POOL + 2 * D_SGU]
    gb = proj[:, 2 * D_POOL + 2 * D_SGU:]
    gb_act, gb_grad = _silu_parts(gb)
    if cdf is None:
        cdf = jnp.concatenate([_normal_cdf(u), _normal_cdf(v)], axis=1)
    u_act, u_grad = _gelu_parts(u, cdf[:, :D_SGU], keep)
    v_act, v_grad = _gelu_parts(v, cdf[:, D_SGU:], keep)
    vn, vrstd, vln, mixed, yb = [], [], [], [], []
    for h in range(N_HEADS):
        sl = slice(h * HEAD, (h + 1) * HEAD)
        n_h, r_h = _ln(v_act[:, sl])
        l_h = (n_h * w.ln_gain(h) + w.ln_bias(h)).astype(BF16)
        w_h = w.mix(h)
        bias = w.mix_bias(h)
        m_h = jnp.concatenate(
            [_dot(w_h, l_h[k * CHUNK:(k + 1) * CHUNK]) + bias for k in range(TM // CHUNK)], axis=0)
        vn.append(n_h)
        vrstd.append(r_h)
        vln.append(l_h)
        mixed.append(m_h)
        yb.append(u_act[:, sl] * m_h * gb_act[:, sl])
    cat = jnp.concatenate(ya + yb, axis=1)
    if not keep:
        return cat, cdf
    return cat, dict(inv_counts=inv_counts, ga_act=ga_act, ga_grad=ga_grad, pooled=pooled, pw=pw, u_grad=u_grad,
                     v_grad=v_grad, u_act=u_act, gb_act=gb_act, gb_grad=gb_grad, vn=vn, vrstd=vrstd, vln=vln,
                     mixed=mixed)


def _const_spec(shape):
    nd = len(shape)
    return pl.BlockSpec(shape, lambda i: (0,) * nd)


VEC_LNG, VEC_LNB, VEC_POOL, VEC_SGU, VEC_SHIFT, VEC_SCALE, VEC_GATE, VEC_LOSS = range(8)


def _layer_backward(layer, a, b, x, proj, y, cdf, mod, w_int, w_outf, small, ln_g, is_last, name, carry=(),
                    reduce=()):
    n_red, n_carry = len(reduce), len(carry)
    base = layer * PACK_ROWS

    def body(a_ref, b_ref, x_ref, proj_ref, prev_ref, y_ref, cdf_ref, mod_ref, wint_ref, wout_ref, wpool_ref,
             pscale_ref, slng_ref, slnb_ref, wsgu_ref, bsgut_ref, lng_ref, *rest):
        weights = _MixWeights(layer, wpool_ref, pscale_ref, slng_ref, slnb_ref, wsgu_ref, bsgut_ref)
        carry_refs, rest = rest[:n_carry], rest[n_carry:]
        part_refs, rest = rest[:n_red], rest[n_red:]
        dx_ref, dproj_ref, h_ref, cat_ref, dy_ref, small_ref, dmod_ref, loss_ref = rest[:8]
        shard_refs, rest = rest[8:8 + n_red], rest[8 + n_red:]
        vec_ref, dmix_ref, halo_ref = rest[:3]
        step = pl.program_id(0)
        tile = N_TILES - 1 - step

        def scatter():
            bufs, sems = rest[3:3 + 5 * n_red], rest[3 + 5 * n_red:]
            arrays = [dict(part=part_refs[n], out=shard_refs[n], staged=True, stage=bufs[5 * n], sib=bufs[5 * n + 1],
                           snd=bufs[5 * n + 2], rcv=bufs[5 * n + 3], relay=bufs[5 * n + 4]) for n in range(n_red)]
            return _ChipReduceScatter(arrays, *sems)

        @pl.when(step == 0)
        def _():
            small_ref[...] = jnp.zeros_like(small_ref)
            dmod_ref[...] = jnp.zeros_like(dmod_ref)
            vec_ref[...] = jnp.zeros_like(vec_ref)
            dmix_ref[...] = jnp.zeros_like(dmix_ref)
            halo_ref[...] = jnp.zeros_like(halo_ref)
            if n_red:
                scatter().start()

        if n_red:
            @pl.when(step == 1)
            def _():
                scatter().exchange()

            @pl.when(step == N_TILES // 2)
            def _():
                scatter().fold()

        def acc(row, lo, val):
            hi = lo + val.shape[1]
            vec_ref[row:row + 1, lo:hi] += jnp.sum(val, axis=0, keepdims=True)

        xt = x_ref[...]
        yt = y_ref[...]
        shift = mod_ref[layer:layer + 1, 0:D_MODEL]
        scale = mod_ref[layer:layer + 1, D_MODEL:2 * D_MODEL]
        gate = mod_ref[layer:layer + 1, 2 * D_MODEL:]
        ln_gain = lng_ref[layer:layer + 1, :]

        zn, zrstd = _ln(ALPHA * xt + gate * yt)
        if is_last:
            diff = a_ref[...] - b_ref[...]
            acc(VEC_LOSS, 0, diff * diff)
            dout = diff * (1.0 / D_MODEL)
        else:
            dout = a_ref[...]
        acc(VEC_LNG, 0, dout * zn)
        acc(VEC_LNB, 0, dout)
        dz = _ln_bwd(dout * ln_gain, zn, zrstd)
        acc(VEC_GATE, 0, dz * yt)
        dy = (dz * gate).astype(BF16)
        dy_ref[...] = dy
        dcat = _dot_nt(dy, wout_ref[...])

        proj = proj_ref[...]
        prev = jnp.where(tile > 0, prev_ref[...], 0.0)
        cat, k = _mix_forward(proj, prev, tile, weights, cdf_ref[...])
        cat_ref[...] = cat.astype(BF16)

        dga, dq = [], []
        for g in range(N_GROUPS):
            sl = slice(g * GROUP, (g + 1) * GROUP)
            pscale = weights.pool_scale(g)
            dya = dcat[:, sl]
            dyp = dya * k["ga_act"][:, sl]
            dga.append(dya * k["pw"][g] * pscale * k["ga_grad"][:, sl])
            acc(VEC_POOL, g * GROUP, dyp * k["pw"][g])
            dpw = (dyp * pscale).astype(BF16)
            rows = pl.ds(base + ROW_WPOOL + g * GROUP, GROUP)
            small_ref[rows, :] += _dot_tn(k["pooled"][g], dpw)
            dq.append(_dot_nt(dpw, weights.pool(g)))
        dpooled = jnp.concatenate(dq, axis=1)
        scaled = jnp.concatenate([dq[g] * k["inv_counts"][g] for g in range(N_GROUPS)], axis=1)
        sums = _window_sums(jnp.concatenate([scaled, halo_ref[...]], axis=0), False)
        halo_ref[...] = scaled[0:HALO]
        dxa = jnp.concatenate(sums, axis=1) - dpooled

        du, dv, dgb = [], [], []
        for h in range(N_HEADS):
            sl = slice(h * HEAD, (h + 1) * HEAD)
            dyb = dcat[:, D_POOL + h * HEAD:D_POOL + (h + 1) * HEAD]
            m_h = k["mixed"][h]
            ug = k["u_act"][:, sl] * dyb
            du.append(dyb * m_h * k["gb_act"][:, sl] * k["u_grad"][:, sl])
            dgb.append(ug * m_h * k["gb_grad"][:, sl])
            dmixed = ug * k["gb_act"][:, sl]
            dmixed_bf = dmixed.astype(BF16)
            w_h = weights.mix(h)
            dvln_parts = []
            dmix_sum = dmix_ref[h]
            wsgu_rows = pl.ds(base + ROW_WSGU + h * CHUNK, CHUNK)
            dws = small_ref[wsgu_rows, :]
            for c in range(TM // CHUNK):
                cs = slice(c * CHUNK, (c + 1) * CHUNK)
                dmix_sum = dmix_sum + dmixed[cs]
                dws = dws + _dot_nt(dmixed_bf[cs], k["vln"][h][cs])
                dvln_parts.append(_dot_tn(w_h, dmixed_bf[cs]))
            dmix_ref[h] = dmix_sum
            small_ref[wsgu_rows, :] = dws
            dvln = jnp.concatenate(dvln_parts, axis=0)
            acc(VEC_SGU, h * HEAD, dvln * k["vn"][h])
            acc(VEC_SGU, D_SGU + h * HEAD, dvln)
            dvv = _ln_bwd(dvln * weights.ln_gain(h), k["vn"][h], k["vrstd"][h])
            dv.append(dvv * k["v_grad"][:, sl])

        dproj = jnp.concatenate([dxa] + dga + du + dv + dgb, axis=1).astype(BF16)
        dproj_ref[...] = dproj
        dh = _dot(dproj, wint_ref[...])

        xn, xrstd = _ln(xt)
        h_ref[...] = (xn * (1.0 + scale) + shift).astype(BF16)
        acc(VEC_SCALE, 0, dh * xn)
        acc(VEC_SHIFT, 0, dh)
        dx_ref[...] = _ln_bwd(dh * (1.0 + scale), xn, xrstd) + ALPHA * dz

        @pl.when(step == N_TILES - 1)
        def _():
            def put(row0, vec_row, lo, n):
                for r in range(n):
                    small_ref[base + row0 + r:base + row0 + r + 1, :] = (
                        vec_ref[vec_row:vec_row + 1, lo + r * 128:lo + (r + 1) * 128])

            put(ROW_PSCALE, VEC_POOL, 0, 4)
            put(ROW_SLNG, VEC_SGU, 0, 4)
            put(ROW_SLNB, VEC_SGU, D_SGU, 4)
            put(ROW_LNG, VEC_LNG, 0, 8)
            put(ROW_LNB, VEC_LNB, 0, 8)
            ones = jnp.ones((8, HEAD), F32)
            t = lax.broadcasted_iota(jnp.int32, (CHUNK, CHUNK), 0)
            s = lax.broadcasted_iota(jnp.int32, (CHUNK, CHUNK), 1)
            for h in range(N_HEADS):
                bias_rows = lax.dot_general(ones, dmix_ref[h], (((1,), (1,)), ((), ())),
                                            preferred_element_type=F32, precision=lax.Precision.HIGHEST)
                small_ref[base + ROW_BSGU + h:base + ROW_BSGU + h + 1, :] = bias_rows[0:1]
                rows = pl.ds(base + ROW_WSGU + h * CHUNK, CHUNK)
                small_ref[rows, :] = jnp.where(t >= s, small_ref[rows, :], 0.0)
            pieces = ((0, VEC_SHIFT, 0, 768),
                      (1, VEC_SHIFT, 768, 256), (1, VEC_SCALE, 0, 512),
                      (2, VEC_SCALE, 512, 512), (2, VEC_GATE, 0, 256),
                      (3, VEC_GATE, 256, 768))
            filled = [0] * 4
            for q, vec_row, lo, n in pieces:
                row = 4 * layer + q
                dmod_ref[row:row + 1, filled[q]:filled[q] + n] = vec_ref[vec_row:vec_row + 1, lo:lo + n]
                filled[q] += n
            if n_carry:
                for other in range(layer + 1, DEPTH):
                    rows = pl.ds(other * PACK_ROWS, PACK_ROWS)
                    small_ref[rows, :] = carry_refs[0][rows, :]
                    dmod_ref[4 * other:4 * other + 4, :] = carry_refs[1][4 * other:4 * other + 4, :]
            loss_ref[...] = vec_ref[VEC_LOSS:VEC_LOSS + 1, :]
            if n_red:
                scatter().finish()
                scatter().wait_sends()

    rev = lambda w: pl.BlockSpec((TM, w), lambda i: (N_TILES - 1 - i, 0))
    prev_spec = pl.BlockSpec(
        (HALO, D_POOL), lambda i: (jnp.maximum((N_TILES - 1 - i) * (TM // HALO) - 1, 0), 0))
    comm_scratch = []
    for p in reduce:
        comm_scratch += _ChipReduceScatter.buffers(p.shape[2], p.shape[3], p.dtype)
    if n_red:
        comm_scratch += _ChipReduceScatter.semaphores(n_red)
    return pl.pallas_call(
        body,
        name=name,
        grid=(N_TILES,),
        in_specs=[rev(D_MODEL), rev(D_MODEL) if is_last else pl.BlockSpec((TM, D_MODEL), lambda i: (0, 0)),
                  rev(D_MODEL), rev(D_IN), prev_spec, rev(D_MODEL), rev(2 * D_SGU),
                  _const_spec((DEPTH, 3 * D_MODEL)), _const_spec((D_IN, D_MODEL)), _const_spec((D_MODEL, D_MODEL))]
                 + [_const_spec(s) for s in SMALL_SPECS] + [_const_spec((DEPTH, D_MODEL))]
                 + [_const_spec(c.shape) for c in carry] + [ANY] * n_red,
        out_specs=[rev(D_MODEL), rev(D_IN), rev(D_MODEL), rev(D_MODEL), rev(D_MODEL),
                   _const_spec((DEPTH * PACK_ROWS, 128)), _const_spec((8, DMOD_COLS)), _const_spec((1, D_MODEL))]
                  + [_const_spec(p.shape[2:]) for p in reduce],
        out_shape=[jax.ShapeDtypeStruct((SEQ, D_MODEL), F32), jax.ShapeDtypeStruct((SEQ, D_IN), BF16),
                   jax.ShapeDtypeStruct((SEQ, D_MODEL), BF16), jax.ShapeDtypeStruct((SEQ, D_MODEL), BF16),
                   jax.ShapeDtypeStruct((SEQ, D_MODEL), BF16), jax.ShapeDtypeStruct((DEPTH * PACK_ROWS, 128), F32),
                   jax.ShapeDtypeStruct((8, DMOD_COLS), F32), jax.ShapeDtypeStruct((1, D_MODEL), F32)]
                  + [jax.ShapeDtypeStruct(p.shape[2:], F32) for p in reduce],
        scratch_shapes=[pltpu.VMEM((8, D_MODEL), F32), pltpu.VMEM((N_HEADS, CHUNK, HEAD), F32),
                        pltpu.VMEM((HALO, D_POOL), F32)] + comm_scratch,
        compiler_params=pltpu.CompilerParams(dimension_semantics=("arbitrary",), vmem_limit_bytes=VMEM_LIMIT),
    )(a, b, x, proj, proj, y, cdf, mod, w_int, w_outf, *small, ln_g, *carry, *reduce)


def _grad_matmuls(dproj, h, cat, dy, name):
    in_cols, out_cols = D_IN // 4, D_MODEL // 2
    in_steps = D_IN // in_cols

    def body(dproj_ref, h_ref, cat_ref, dy_ref, gin_ref, gout_ref):
        step = pl.program_id(0)

        @pl.when(step < in_steps)
        def _():
            gin_ref[...] = _dot_tn(dproj_ref[...], h_ref[...]).astype(BF16)

        @pl.when(step >= in_steps)
        def _():
            gout_ref[...] = _dot_tn(cat_ref[...], dy_ref[...]).astype(BF16)

    in_block = lambda j: jnp.minimum(j, in_steps - 1)
    out_block = lambda j: jnp.maximum(j - in_steps, 0)
    return pl.pallas_call(
        body,
        name=name,
        grid=(in_steps + D_MODEL // out_cols,),
        in_specs=[pl.BlockSpec((SEQ, in_cols), lambda j: (0, in_block(j))), _const_spec((SEQ, D_MODEL)),
                  pl.BlockSpec((SEQ, out_cols), lambda j: (0, out_block(j))), _const_spec((SEQ, D_MODEL))],
        out_specs=[pl.BlockSpec((in_cols, D_MODEL), lambda j: (in_block(j), 0)),
                   pl.BlockSpec((out_cols, D_MODEL), lambda j: (out_block(j), 0))],
        out_shape=[jax.ShapeDtypeStruct((D_IN, D_MODEL), BF16), jax.ShapeDtypeStruct((D_MODEL, D_MODEL), BF16)],
        compiler_params=pltpu.CompilerParams(dimension_semantics=("arbitrary",), vmem_limit_bytes=VMEM_LIMIT),
    )(dproj, h, cat, dy)


def _adamw_math(w, g, m, v):
    m = ADAM_B1 * m + (1.0 - ADAM_B1) * g
    v = ADAM_B2 * v + (1.0 - ADAM_B2) * (g * g)
    m_hat = m / (1.0 - ADAM_B1 ** ADAM_STEP)
    v_hat = v / (1.0 - ADAM_B2 ** ADAM_STEP)
    delta = -ADAM_LR * (m_hat / (jnp.sqrt(v_hat) + ADAM_EPS) + ADAM_WD * w)
    return delta, m, v


def _adamw(w, grads, m, v, block_rows, name):
    rows, cols = grads[0].shape
    blocks = rows // block_rows

    def body(w_ref, m_ref, v_ref, *rest):
        g_refs, (g_ref, d_ref, nm_ref, nv_ref) = rest[:DEPTH], rest[DEPTH:]
        for layer in range(DEPTH):
            @pl.when(pl.program_id(0) == layer)
            def _():
                g = g_refs[layer][...]
                g_ref[...] = g
                d_ref[...], nm_ref[...], nv_ref[...] = _adamw_math(w_ref[...], g, m_ref[...], v_ref[...])

    def grad_spec(layer):
        return pl.BlockSpec((block_rows, cols),
                            lambda l, i: (jnp.where(l == layer, i, jnp.where(l < layer, 0, blocks - 1)), 0))

    spec = pl.BlockSpec((block_rows, cols), lambda l, i: (l * blocks + i, 0))
    return pl.pallas_call(
        body,
        name=name,
        grid=(DEPTH, blocks),
        in_specs=[spec] * 3 + [grad_spec(layer) for layer in range(DEPTH)],
        out_specs=[spec] * 4,
        out_shape=[jax.ShapeDtypeStruct(w.shape, F32)] * 4,
        compiler_params=pltpu.CompilerParams(dimension_semantics=("arbitrary", "arbitrary"),
                                             vmem_limit_bytes=VMEM_LIMIT),
    )(w, m, v, *grads)


MESH = pl.DeviceIdType.MESH
SIBLING = 1
ANY = pl.BlockSpec(memory_space=pl.ANY)
VMEM = pl.BlockSpec(memory_space=pltpu.VMEM)


def _me():
    return lax.axis_index("x"), lax.axis_index("y"), lax.axis_index("c")


def _peer(r):
    x, y, c = _me()
    return (1 - x if r & 4 else x, 1 - y if r & 2 else y, 1 - c if r & 1 else c)


def _index(dev):
    return 4 * dev[0] + 2 * dev[1] + dev[2]


def _remote(src, dst, send_sem, recv_sem, dev):
    return pltpu.make_async_remote_copy(src_ref=src, dst_ref=dst, send_sem=send_sem, recv_sem=recv_sem,
                                        device_id=dev, device_id_type=MESH)


ACROSS_X, ACROSS_Y, ACROSS_BOTH = 4, 2, 6
GATHER_SEMS = 11


class _TwoLevelGather:
    def __init__(self, out, send_sems, recv_sems, src=None):
        self.out, self.send_sems, self.recv_sems, self.src = out, send_sems, recv_sems, src
        self.rows = (out.shape[0] // N_DEV) if len(out.shape) == 2 else out.shape[1]
        self.half = self.rows // 2

    def _slot(self, block):
        if len(self.out.shape) == 2:
            return self.out.at[pl.ds(pl.multiple_of(_index(block) * self.rows, self.rows), self.rows)]
        return self.out.at[_index(block)]

    def _copy(self, k, block, part, to, src=None):
        slot = self._slot(block)
        if part is not None:
            rows = pl.ds(part * self.half, self.half)
            slot = slot.at[rows]
            src = None if src is None else src.at[rows]
        return _remote(slot if src is None else src, slot, self.send_sems.at[k], self.recv_sems.at[k], to)

    def _mine(self):
        me = _me()
        src = self._slot(me) if self.src is None else self.src
        x, y = _peer(ACROSS_X), _peer(ACROSS_Y)
        return [self._copy(1, me, 0, x, src), self._copy(3, me, 1, y, src), self._copy(0, me, None, _peer(SIBLING), src),
                self._copy(2, me, 1, x, src), self._copy(4, me, 0, y, src)]

    def _relayed(self):
        return [self._copy(5, _peer(ACROSS_X), 0, _peer(ACROSS_Y)), self._copy(6, _peer(ACROSS_Y), 1, _peer(ACROSS_X))]

    def _passed(self):
        sib, far = _peer(SIBLING), _peer(ACROSS_BOTH)
        return [self._copy(7, _peer(ACROSS_X), None, sib), self._copy(8, _peer(ACROSS_Y), None, sib),
                self._copy(9, far, 0, sib), self._copy(10, far, 1, sib)]

    def _arrival(self, k, r, part):
        return self._copy(k, _peer(r), part, _me())

    def send_first(self):
        for cp in self._mine()[:3]:
            cp.start()

    def send_second(self):
        for cp in self._mine()[3:]:
            cp.start()

    def send_mine(self):
        self.send_first()
        self.send_second()

    def relay(self):
        relayed = self._relayed()
        self._arrival(1, ACROSS_X, 0).wait_recv()
        relayed[0].start()
        self._arrival(3, ACROSS_Y, 1).wait_recv()
        relayed[1].start()

    def pass_near(self):
        passed = self._passed()
        self._arrival(2, ACROSS_X, 1).wait_recv()
        passed[0].start()
        self._arrival(4, ACROSS_Y, 0).wait_recv()
        passed[1].start()

    def pass_far(self):
        passed = self._passed()
        self._arrival(5, ACROSS_BOTH, 0).wait_recv()
        passed[2].start()
        self._arrival(6, ACROSS_BOTH, 1).wait_recv()
        passed[3].start()

    def pass_on(self):
        self.pass_near()
        self.pass_far()

    def wait_sibling(self):
        self._arrival(0, SIBLING, None).wait_recv()

    def wait_passed(self, r):
        if r == ACROSS_BOTH:
            self._arrival(9, r ^ SIBLING, 0).wait_recv()
            self._arrival(10, r ^ SIBLING, 1).wait_recv()
        else:
            self._arrival(7 if r == ACROSS_X else 8, r ^ SIBLING, None).wait_recv()

    def wait_rest(self):
        self.wait_sibling()
        for r in (ACROSS_X, ACROSS_Y, ACROSS_BOTH):
            self.wait_passed(r)

    def wait_sends(self):
        for cp in self._mine() + self._relayed() + self._passed():
            cp.wait_send()


class _ChipReduceScatter:
    SLOTS = 6

    def __init__(self, arrays, l_sem, d_send, d_recv, i_send, i_recv):
        self.arrays = arrays
        self.l_sem, self.d_send, self.d_recv, self.i_send, self.i_recv = l_sem, d_send, d_recv, i_send, i_recv

    @staticmethod
    def buffers(rows, cols, dtype, staged=True):
        stage = [pltpu.VMEM((4, rows, cols), dtype)] if staged else []
        return stage + [pltpu.VMEM((4, rows, cols), dtype), pltpu.VMEM((3, rows, cols), dtype),
                        pltpu.VMEM((2, rows, cols), dtype), pltpu.VMEM((2, rows // 2, cols), dtype)]

    @classmethod
    def semaphores(cls, n):
        return [pltpu.SemaphoreType.DMA((n,)), pltpu.SemaphoreType.DMA((n, 4)), pltpu.SemaphoreType.DMA((n, 4)),
                pltpu.SemaphoreType.DMA((n, cls.SLOTS)), pltpu.SemaphoreType.DMA((n, cls.SLOTS))]

    def _pick(self, which):
        return list(enumerate(self.arrays)) if which is None else [(n, self.arrays[n]) for n in which]

    @staticmethod
    def _chip(r):
        dev = _me() if r is None else _peer(r)
        return 2 * dev[0] + dev[1]

    def _staging(self, which):
        c = _me()[2]
        return [pltpu.make_async_copy(a["part"].at[pl.ds(0, 4), c], a["stage"], self.l_sem.at[n])
                for n, a in self._pick(which) if a["staged"]]

    def _first(self, which, chip):
        other = 1 - _me()[2]
        return [_remote(a["part"].at[chip, other], a["sib"].at[chip], self.d_send.at[n, chip], self.d_recv.at[n, chip],
                        _peer(SIBLING)) for n, a in self._pick(which)]

    @staticmethod
    def _halves(a):
        half = a["rcv"].shape[1] // 2
        return pl.ds(0, half), pl.ds(half, half)

    def _hops(self, n, a):
        h0, h1 = self._halves(a)
        x, y = _peer(ACROSS_X), _peer(ACROSS_Y)
        snd, rcv, relay = a["snd"], a["rcv"], a["relay"]
        pairs = [(snd.at[2, h0], relay.at[0], x), (snd.at[2, h1], relay.at[1], y),
                 (snd.at[0, h0], rcv.at[0, h0], x), (snd.at[0, h1], rcv.at[0, h1], x),
                 (snd.at[1, h1], rcv.at[1, h1], y), (snd.at[1, h0], rcv.at[1, h0], y)]
        return [_remote(s, d, self.i_send.at[n, k], self.i_recv.at[n, k], to) for k, (s, d, to) in enumerate(pairs)]

    def _mine(self, a, chip, rows=None):
        src = a["stage"].at[chip] if a["staged"] else a["part"].at[chip, _me()[2]]
        mine, sib = (src[...], a["sib"][chip]) if rows is None else (src[rows, :], a["sib"][chip, rows, :])
        return mine.astype(F32) + sib.astype(F32)

    def start(self, which=None, chips=None):
        if chips is None:
            for cp in self._staging(which):
                cp.start()
        for chip in range(4) if chips is None else chips:
            for cp in self._first(which, chip):
                cp.start()

    def send_far(self, which=None):
        far = self._chip(ACROSS_BOTH)
        for cp in self._staging(which):
            cp.wait()
        for cp in self._first(which, far):
            cp.wait_recv()
        for n, a in self._pick(which):
            hops = self._hops(n, a)
            a["snd"][2] = self._mine(a, far).astype(a["snd"].dtype)
            hops[0].start()
            hops[1].start()

    def send_near(self, r, which=None):
        chip = self._chip(r)
        for cp in self._first(which, chip):
            cp.wait_recv()
        for n, a in self._pick(which):
            h0, h1 = self._halves(a)
            hops = self._hops(n, a)
            if r == ACROSS_X:
                a["snd"][0, h0, :] = self._mine(a, chip, h0).astype(a["snd"].dtype)
                hops[2].start()
            else:
                a["snd"][1, h1, :] = self._mine(a, chip, h1).astype(a["snd"].dtype)
                hops[4].start()

    def exchange(self, which=None):
        self.send_far(which)
        self.send_near(ACROSS_X, which)
        self.send_near(ACROSS_Y, which)

    def fold(self, which=None):
        across_x, across_y = self._chip(ACROSS_X), self._chip(ACROSS_Y)
        for n, a in self._pick(which):
            h0, h1 = self._halves(a)
            hops = self._hops(n, a)
            dtype = a["snd"].dtype
            hops[1].wait_recv()
            a["snd"][0, h1, :] = (self._mine(a, across_x, h1) + a["relay"][1].astype(F32)).astype(dtype)
            hops[3].start()
            hops[0].wait_recv()
            a["snd"][1, h0, :] = (self._mine(a, across_y, h0) + a["relay"][0].astype(F32)).astype(dtype)
            hops[5].start()

    def finish(self, which=None):
        home = self._chip(None)
        for cp in self._first(which, home):
            cp.wait_recv()
        for n, a in self._pick(which):
            hops = self._hops(n, a)
            a["out"][...] = self._mine(a, home)
            hops[2].wait_recv()
            hops[3].wait_recv()
            a["out"][...] += a["rcv"][0].astype(F32)
            hops[4].wait_recv()
            hops[5].wait_recv()
            a["out"][...] += a["rcv"][1].astype(F32)

    def wait_sends(self, which=None):
        for chip in range(4):
            for cp in self._first(which, chip):
                cp.wait_send()
        for n, a in self._pick(which):
            for cp in self._hops(n, a):
                cp.wait_send()


def _direct_exchange(src_of, dst_of, send_sems, recv_sems):
    me = _me()
    copies = [_remote(src_of(_peer(r)), dst_of(me), send_sems.at[r - 1], recv_sems.at[r - 1], _peer(r))
              for r in range(1, N_DEV)]
    for cp in copies:
        cp.start()
    return copies


def _wait_direct(copies):
    for cp in copies:
        cp.wait_recv()
    for cp in copies:
        cp.wait_send()


def _forward(x, c, w_ada, b_ada, small, ln_g, ln_b, mine, following):
    assert DEPTH == 2
    cols = w_ada.shape[2]
    shard = mine[0].shape[0]
    pair = 2 * shard

    def body(x_hbm, c_ref, wada_hbm, bada_ref, wpool_ref, pscale_ref, slng_ref, slnb_ref, wsgu_ref, bsgut_ref,
             lng_ref, lnb_ref, wint_hbm, wout_hbm, next_in_hbm, next_out_hbm,
             out0_ref, y0_ref, cdf0_ref, proj0_hbm, out1_ref, y1_ref, cdf1_ref, proj1_hbm,
             wint_keep, wout_keep, wint_next, wout_next, acts_ref, mod_ref,
             wint_v, wout_v, h_buf, proj_blk, proj_tile, halo_ref, x_ref, w_send, w_recv, w_local, p_sems,
             t_sems, in_sems, act_all, act_src, part, mod_recv, wada_ref, a_send, a_recv, m_send, m_recv,
             n_send, n_recv, n_local, f_sems):
        step = pl.program_id(0)
        chip_of = lambda dev: 2 * dev[0] + dev[1]

        def hosted():
            return [_TwoLevelGather(out, n_send.at[n], n_recv.at[n], src=src)
                    for n, (out, src) in enumerate(((wint_next, next_in_hbm), (wout_next, next_out_hbm)))]

        def hosted_own():
            me = _me()
            return [pltpu.make_async_copy(g.src, g._slot(me), n_local.at[n]) for n, g in enumerate(hosted())]

        def gathers():
            return (_TwoLevelGather(wint_v, w_send.at[0], w_recv.at[0], src=wint_hbm),
                    _TwoLevelGather(wout_v, w_send.at[1], w_recv.at[1], src=wout_hbm))

        def keeps():
            return [pltpu.make_async_copy(wint_v, wint_keep, w_local.at[2]),
                    pltpu.make_async_copy(wout_v, wout_keep, w_local.at[3])]

        def tile_read(t):
            slot = t % 2
            return pltpu.make_async_copy(proj0_hbm.at[pl.ds(pl.multiple_of(t * TM, TM), TM)], proj_tile.at[slot],
                                         t_sems.at[slot])

        def first_layer_start():
            writes = []

            def project(n, dev):
                first = pl.multiple_of(chip_of(dev) * pair, pair)
                if n >= 2:
                    writes[n - 2].wait()

                @pl.loop(0, N_TILES)
                def _(t):
                    rows = pl.ds(pl.multiple_of(t * TM, TM), TM)
                    proj_blk[n % 2, rows, :] = _dot_nt(h_buf[rows, :], wint_v[pl.ds(first, pair), :])

                cp = pltpu.make_async_copy(proj_blk.at[n % 2], proj0_hbm.at[:, pl.ds(first, pair)], p_sems.at[n % 2])
                cp.start()
                writes.append(cp)

            me = _me()
            halo_ref[...] = jnp.zeros_like(halo_ref)
            gather_in, gather_out = gathers()
            own_in = pltpu.make_async_copy(wint_hbm, gather_in._slot(me), w_local.at[0])
            own_out = pltpu.make_async_copy(wout_hbm, gather_out._slot(me), w_local.at[1])
            x_load = pltpu.make_async_copy(x_hbm, x_ref, in_sems.at[0])
            x_load.start()
            wada_load = pltpu.make_async_copy(wada_hbm, wada_ref, in_sems.at[1])
            wada_load.start()
            mine_index = _index(me)
            cval = c_ref[...]
            act_src[...] = jnp.zeros_like(act_src)
            act_src[0:1, :] = cval * jax.nn.sigmoid(cval)
            act_all[mine_index] = act_src[...]
            act_copies = _direct_exchange(lambda p: act_src, lambda m: act_all.at[_index(m)], a_send, a_recv)

            own_in.start()
            own_out.start()
            gather_in.send_first()

            _wait_direct(act_copies)
            acts = jnp.concatenate([act_all[j, 0:1, :] for j in range(N_DEV)], axis=0)
            acts_ref[...] = acts
            part[...] = jnp.zeros_like(part)
            wada_load.wait()
            for l in range(DEPTH):
                res = lax.dot_general(acts, wada_ref[l], (((1,), (0,)), ((), ())), preferred_element_type=F32,
                                      precision=lax.Precision.HIGHEST)
                for b in range(N_DEV):
                    part[b, l:l + 1, :] = res[b:b + 1, :]
            mod_recv[mine_index] = part[mine_index]
            mod_copies = _direct_exchange(lambda p: part.at[_index(p)], lambda m: mod_recv.at[_index(m)],
                                          m_send, m_recv)
            gather_in.send_second()
            gather_out.send_mine()

            _wait_direct(mod_copies)
            for l in range(DEPTH):
                for j in range(N_DEV):
                    sl = slice(j * cols, (j + 1) * cols)
                    mod_ref[l:l + 1, sl] = mod_recv[j, l:l + 1, :] + bada_ref[l:l + 1, sl]
            x_load.wait()
            shift = mod_ref[0:1, 0:D_MODEL]
            scale = mod_ref[0:1, D_MODEL:2 * D_MODEL]

            @pl.loop(0, N_TILES)
            def _(t):
                rows = pl.ds(pl.multiple_of(t * TM, TM), TM)
                xn, _ = _ln(x_ref[rows, :])
                h_buf[rows, :] = (xn * (1.0 + scale) + shift).astype(BF16)

            gather_in.relay()
            own_in.wait()
            gather_in.wait_sibling()
            project(0, me)
            gather_in.pass_near()
            gather_in.wait_passed(ACROSS_X)
            project(1, _peer(ACROSS_X))
            gather_out.relay()
            for cp in hosted_own():
                cp.start()
            for g in hosted():
                g.send_mine()
            gather_in.wait_passed(ACROSS_Y)
            project(2, _peer(ACROSS_Y))
            gather_in.pass_far()
            gather_in.wait_passed(ACROSS_BOTH)
            project(3, _peer(ACROSS_BOTH))

            gather_out.pass_on()
            gather_out.wait_rest()
            own_out.wait()
            for cp in keeps():
                cp.start()
            writes[2].wait()
            writes[3].wait()
            tile_read(0).start()

        def fetches():
            return [pltpu.make_async_copy(wint_next, wint_v, f_sems.at[0]),
                    pltpu.make_async_copy(wout_next, wout_v, f_sems.at[1])]

        def tile_write(t):
            slot = t % 2
            return pltpu.make_async_copy(proj_tile.at[slot], proj1_hbm.at[pl.ds(pl.multiple_of(t * TM, TM), TM)],
                                         t_sems.at[slot])

        def mix_and_close(layer, tile, rows, proj, out_ref, y_ref, cdf_ref):
            weights = _MixWeights(layer, wpool_ref, pscale_ref, slng_ref, slnb_ref, wsgu_ref, bsgut_ref)
            xt = x_ref[rows, :]
            gate = mod_ref[layer:layer + 1, 2 * D_MODEL:]
            cat, cdf_ref[...] = _mix_forward(proj, halo_ref[...], tile, weights)
            halo_ref[...] = proj[TM - HALO:, 0:D_POOL]
            if layer == 1:
                @pl.when(tile == 0)
                def _():
                    fetches()[1].wait()
            y = _dot(cat.astype(BF16), wout_v[...])
            y_ref[...] = y
            zn, _ = _ln(ALPHA * xt + gate * y)
            out = zn * lng_ref[layer:layer + 1, :] + lnb_ref[layer:layer + 1, :]
            out_ref[...] = out
            if layer == 0:
                x_ref[rows, :] = out

        def first_layer_tile(tile):
            @pl.when(tile + 1 < N_TILES)
            def _():
                tile_read(tile + 1).start()

            tile_read(tile).wait()
            rows = pl.ds(pl.multiple_of(tile * TM, TM), TM)
            mix_and_close(0, tile, rows, proj_tile[tile % 2], out0_ref, y0_ref, cdf0_ref)

        def second_layer_tile(tile):
            rows = pl.ds(pl.multiple_of(tile * TM, TM), TM)
            shift = mod_ref[1:2, 0:D_MODEL]
            scale = mod_ref[1:2, D_MODEL:2 * D_MODEL]
            xn, _ = _ln(x_ref[rows, :])
            h = (xn * (1.0 + scale) + shift).astype(BF16)

            @pl.when(tile >= 2)
            def _():
                tile_write(tile - 2).wait()

            proj_tile[tile % 2] = _dot_nt(h, wint_v[...])
            tile_write(tile).start()
            mix_and_close(1, tile, rows, proj_tile[tile % 2], out1_ref, y1_ref, cdf1_ref)

        @pl.when(step < N_TILES)
        def _():
            @pl.when(step == 0)
            def _():
                first_layer_start()

            @pl.when(step == 1)
            def _():
                for g in hosted():
                    g.relay()

            @pl.when(step == N_TILES // 2)
            def _():
                for g in hosted():
                    g.pass_near()

            @pl.when(step == N_TILES - 1)
            def _():
                for g in hosted():
                    g.pass_far()
                for g in gathers():
                    g.wait_sends()
                for cp in keeps() + hosted_own():
                    cp.wait()
                hosted()[0].wait_rest()
                fetches()[0].start()

            first_layer_tile(step)

        @pl.when(step >= N_TILES)
        def _():
            @pl.when(step == N_TILES)
            def _():
                halo_ref[...] = jnp.zeros_like(halo_ref)
                hosted()[1].wait_rest()
                fetches()[1].start()
                fetches()[0].wait()

            second_layer_tile(step - N_TILES)

            @pl.when(step == 2 * N_TILES - 1)
            def _():
                for g in hosted():
                    g.wait_sends()
                tile_write(N_TILES - 2).wait()
                tile_write(N_TILES - 1).wait()

    first = lambda w: pl.BlockSpec((TM, w), lambda i: (jnp.minimum(i, N_TILES - 1), 0))
    second = lambda w: pl.BlockSpec((TM, w), lambda i: (jnp.maximum(i - N_TILES, 0), 0))
    gather_sems = pltpu.SemaphoreType.DMA((2, GATHER_SEMS))
    seven = pltpu.SemaphoreType.DMA((7,))
    per_layer = [jax.ShapeDtypeStruct((SEQ, D_MODEL), F32), jax.ShapeDtypeStruct((SEQ, D_MODEL), F32),
                 jax.ShapeDtypeStruct((SEQ, 2 * D_SGU), F32), jax.ShapeDtypeStruct((SEQ, D_IN), F32)]
    gathered = [jax.ShapeDtypeStruct((D_IN, D_MODEL), BF16), jax.ShapeDtypeStruct((D_MODEL, D_MODEL), BF16)]
    res = pl.pallas_call(
        body,
        name="layers_fwd",
        grid=(DEPTH * N_TILES,),
        in_specs=[ANY, _const_spec(c.shape), ANY, _const_spec(b_ada.shape)] + [_const_spec(s) for s in SMALL_SPECS]
                 + [_const_spec((DEPTH, D_MODEL)), _const_spec((DEPTH, D_MODEL))] + [ANY] * 4,
        out_specs=[first(D_MODEL), first(D_MODEL), first(2 * D_SGU), ANY,
                   second(D_MODEL), second(D_MODEL), second(2 * D_SGU), ANY] + [ANY] * 4
                  + [_const_spec((N_DEV, D_MODEL)), _const_spec((DEPTH, 3 * D_MODEL))],
        out_shape=per_layer * 2 + gathered * 2 + [jax.ShapeDtypeStruct((N_DEV, D_MODEL), F32),
                                                  jax.ShapeDtypeStruct((DEPTH, 3 * D_MODEL), F32)],
        scratch_shapes=[pltpu.VMEM((D_IN, D_MODEL), BF16), pltpu.VMEM((D_MODEL, D_MODEL), BF16),
                        pltpu.VMEM((SEQ, D_MODEL), BF16), pltpu.VMEM((2, SEQ, pair), F32),
                        pltpu.VMEM((2, TM, D_IN), F32), pltpu.VMEM((HALO, D_POOL), F32),
                        pltpu.VMEM((SEQ, D_MODEL), F32),
                        gather_sems, gather_sems, pltpu.SemaphoreType.DMA((4,)), pltpu.SemaphoreType.DMA((2,)),
                        pltpu.SemaphoreType.DMA((2,)), pltpu.SemaphoreType.DMA((2,)),
                        pltpu.VMEM((N_DEV, 8, D_MODEL), F32), pltpu.VMEM((8, D_MODEL), F32),
                        pltpu.VMEM((N_DEV, 8, cols), F32), pltpu.VMEM((N_DEV, 8, cols), F32),
                        pltpu.VMEM(w_ada.shape, F32), seven, seven, seven, seven,
                        gather_sems, gather_sems, pltpu.SemaphoreType.DMA((2,)), pltpu.SemaphoreType.DMA((2,))],
        compiler_params=pltpu.CompilerParams(dimension_semantics=("arbitrary",), vmem_limit_bytes=VMEM_LIMIT),
    )(x, c, w_ada, b_ada, *small, ln_g, ln_b, *mine, *following)
    return res[0:4], res[4:8], res[8:10], res[10:12], res[12], res[13]


ADA_CHUNK = 256


def _grad_tail(dproj, h, cat, dy, small, dmod8, loss_lanes, w_ada, m_ada, v_ada, act_t, b_ada, m_bada, v_bada):
    shard_in, shard_out, shard_small = D_IN // N_DEV, D_MODEL // N_DEV, small.shape[2]
    cols = w_ada.shape[2]
    W_IN, W_OUT, SMALL = 0, 1, 2

    def body(dproj_hbm, h_hbm, cat_hbm, dy_hbm, small_hbm, dmod_ref, lanes_ref, wada_hbm, mada_hbm, vada_hbm,
             act_ref, bada_ref, mbada_ref, vbada_ref,
             gwin_ref, gwout_ref, stot_ref, loss_ref, gada_hbm, dada_hbm, nmada_hbm, nvada_hbm,
             gb_ref, db_ref, nmb_ref, nvb_ref,
             dproj_v, h_v, cat_v, dy_v, part_in, part_out, own_small, loss_src, loss_all, dmod_all, ada_in, ada_out,
             *rest):
        bufs, rest = rest[:13], rest[13:]
        load_sems, rs_sems = rest[0], rest[1:6]
        m_send, m_recv, g_send, g_recv, s_send, s_recv, ada_lsem, ada_ssem = rest[6:]
        mine = _index(_me())

        def update_ada():
            upper = (mine % 2) == 1

            def dmod_of(layer):
                rows = []
                for b in range(N_DEV):
                    r = dmod_all[b, pl.ds(4 * layer + mine // 2, 1), :]
                    rows.append(jnp.where(upper, r[:, cols:], r[:, :cols]))
                return jnp.concatenate(rows, axis=0)

            chunks = [(layer, c) for layer in range(DEPTH) for c in range(D_MODEL // ADA_CHUNK)]

            def loads(i):
                layer, c = chunks[i]
                rows = pl.ds(c * ADA_CHUNK, ADA_CHUNK)
                return [pltpu.make_async_copy(src.at[layer, rows], ada_in.at[i % 2, k], ada_lsem.at[i % 2, k])
                        for k, src in enumerate((wada_hbm, mada_hbm, vada_hbm))]

            def stores(i):
                layer, c = chunks[i]
                rows = pl.ds(c * ADA_CHUNK, ADA_CHUNK)
                return [pltpu.make_async_copy(ada_out.at[i % 2, k], dst.at[layer, rows], ada_ssem.at[i % 2, k])
                        for k, dst in enumerate((gada_hbm, dada_hbm, nmada_hbm, nvada_hbm))]

            for cp in loads(0):
                cp.start()
            dmods = {}
            for i, (layer, c) in enumerate(chunks):
                if i + 1 < len(chunks):
                    for cp in loads(i + 1):
                        cp.start()
                for cp in loads(i):
                    cp.wait()
                if i >= 2:
                    for cp in stores(i - 2):
                        cp.wait()
                if layer not in dmods:
                    dmods[layer] = dmod_of(layer)
                act = act_ref[pl.ds(c * ADA_CHUNK, ADA_CHUNK), :]
                g = act[:, 0:1] * dmods[layer][0:1, :]
                for b in range(1, N_DEV):
                    g = g + act[:, b:b + 1] * dmods[layer][b:b + 1, :]
                slot = i % 2
                delta, new_m, new_v = _adamw_math(ada_in[slot, 0], g, ada_in[slot, 1], ada_in[slot, 2])
                ada_out[slot, 0] = g
                ada_out[slot, 1] = delta
                ada_out[slot, 2] = new_m
                ada_out[slot, 3] = new_v
                for cp in stores(i):
                    cp.start()
            for i in (len(chunks) - 2, len(chunks) - 1):
                for cp in stores(i):
                    cp.wait()

            total = dmod_all[0]
            for b in range(1, N_DEV):
                total = total + dmod_all[b]
            width = total.shape[1]
            for layer in range(DEPTH):
                for q in range(4):
                    gb_ref[layer:layer + 1, q * width:(q + 1) * width] = total[4 * layer + q:4 * layer + q + 1, :]
            db_ref[...], nmb_ref[...], nvb_ref[...] = _adamw_math(bada_ref[...], gb_ref[...], mbada_ref[...],
                                                                  vbada_ref[...])

        order = (ACROSS_BOTH, ACROSS_X, ACROSS_Y, None)
        chips = [_ChipReduceScatter._chip(r) for r in order]
        loads = [pltpu.make_async_copy(s, d, load_sems.at[n]) for n, (s, d) in enumerate(
            ((cat_hbm, cat_v), (dy_hbm, dy_v), (h_hbm, h_v)))]
        loads += [pltpu.make_async_copy(dproj_hbm.at[:, pl.ds(pl.multiple_of(chip * 2 * shard_in, 2 * shard_in),
                                                             2 * shard_in)], dproj_v.at[n], load_sems.at[3 + n])
                  for n, chip in enumerate(chips)]
        for cp in loads:
            cp.start()
        arrays = [dict(part=part_in, out=gwin_ref, staged=False, sib=bufs[0], snd=bufs[1], rcv=bufs[2], relay=bufs[3]),
                  dict(part=part_out, out=gwout_ref, staged=False, sib=bufs[4], snd=bufs[5], rcv=bufs[6],
                       relay=bufs[7]),
                  dict(part=small_hbm, out=own_small, staged=True, stage=bufs[8], sib=bufs[9], snd=bufs[10],
                       rcv=bufs[11], relay=bufs[12])]
        scatter = _ChipReduceScatter(arrays, *rs_sems)
        scatter.start([SMALL])
        dmod_all[mine] = dmod_ref[...]
        dmod_copies = _direct_exchange(lambda p: dmod_ref, lambda m: dmod_all.at[_index(m)], m_send, m_recv)
        loss_src[...] = jnp.full(loss_src.shape, (0.5 / D_MODEL) * jnp.sum(lanes_ref[...]), F32)
        loss_all[mine] = loss_src[...]
        loss_copies = _direct_exchange(lambda p: loss_src, lambda m: loss_all.at[_index(m)], s_send, s_recv)

        loads[0].wait()
        loads[1].wait()
        for blk in range(2):
            res = _dot_tn(cat_v[:, blk * 512:(blk + 1) * 512], dy_v[...]).astype(BF16)
            for s in range(4):
                part_out[2 * blk + s // 2, s % 2] = res[s * shard_out:(s + 1) * shard_out]
        scatter.start([W_OUT])
        scatter.exchange([SMALL])

        gather = _TwoLevelGather(stot_ref, g_send, g_recv)
        loads[2].wait()
        for n, chip in enumerate(chips):
            loads[3 + n].wait()
            res = _dot_tn(dproj_v[n], h_v[...]).astype(BF16)
            part_in[chip, 0] = res[:shard_in]
            part_in[chip, 1] = res[shard_in:]
            scatter.start([W_IN], chips=[chip])
            if n == 0:
                scatter.exchange([W_OUT])
                scatter.fold([SMALL])
            if n == 1:
                scatter.send_far([W_IN])
                scatter.fold([W_OUT])
                scatter.finish([SMALL])
                stot_ref[mine] = own_small[...]
                gather.send_mine()
            if n == 2:
                scatter.send_near(ACROSS_X, [W_IN])
                gather.relay()
            if n == 3:
                scatter.send_near(ACROSS_Y, [W_IN])
        scatter.fold([W_IN])
        scatter.finish([W_OUT])
        _wait_direct(dmod_copies)
        update_ada()
        gather.pass_on()
        gather.wait_rest()
        _wait_direct(loss_copies)
        total = loss_all[0]
        for j in range(1, N_DEV):
            total = total + loss_all[j]
        loss_ref[...] = total
        scatter.finish([W_IN])
        gather.wait_sends()
        scatter.wait_sends()

    buffers = _ChipReduceScatter.buffers
    comm_scratch = (buffers(shard_in, D_MODEL, BF16, staged=False) + buffers(shard_out, D_MODEL, BF16, staged=False)
                    + buffers(shard_small, 128, F32))
    comm_scratch += [pltpu.SemaphoreType.DMA((7,))] + _ChipReduceScatter.semaphores(3)
    comm_scratch += [pltpu.SemaphoreType.DMA((n,)) for n in (7, 7, GATHER_SEMS, GATHER_SEMS, 7, 7)]
    comm_scratch += [pltpu.SemaphoreType.DMA((2, 3)), pltpu.SemaphoreType.DMA((2, 4))]
    return pl.pallas_call(
        body,
        name="grad_tail",
        in_specs=[ANY] * 5 + [VMEM, VMEM] + [ANY] * 3 + [VMEM] * 4,
        out_specs=[VMEM] * 4 + [ANY] * 4 + [VMEM] * 4,
        out_shape=[jax.ShapeDtypeStruct((shard_in, D_MODEL), F32), jax.ShapeDtypeStruct((shard_out, D_MODEL), F32),
                   jax.ShapeDtypeStruct((N_DEV, shard_small, 128), F32), jax.ShapeDtypeStruct((8, 128), F32)]
                  + [jax.ShapeDtypeStruct(w_ada.shape, F32)] * 4 + [jax.ShapeDtypeStruct(b_ada.shape, F32)] * 4,
        scratch_shapes=[pltpu.VMEM((4, SEQ, 2 * shard_in), BF16), pltpu.VMEM(h.shape, BF16), pltpu.VMEM(cat.shape, BF16),
                        pltpu.VMEM(dy.shape, BF16), pltpu.VMEM((4, 2, shard_in, D_MODEL), BF16),
                        pltpu.VMEM((4, 2, shard_out, D_MODEL), BF16), pltpu.VMEM((shard_small, 128), F32),
                        pltpu.VMEM((8, 128), F32), pltpu.VMEM((N_DEV, 8, 128), F32),
                        pltpu.VMEM((N_DEV,) + dmod8.shape, F32), pltpu.VMEM((2, 3, ADA_CHUNK, cols), F32),
                        pltpu.VMEM((2, 4, ADA_CHUNK, cols), F32)] + comm_scratch,
        compiler_params=pltpu.CompilerParams(vmem_limit_bytes=VMEM_LIMIT),
    )(dproj, h, cat, dy, small, dmod8, loss_lanes, w_ada, m_ada, v_ada, act_t, b_ada, m_bada, v_bada)


SMALL_NAMES = ("w_pool", "w_sgu", "pool_scale", "sgu_ln_g", "sgu_ln_b", "b_sgu", "ln_g", "ln_b")
SMALL_ROWS = (512, 512, 4, 4, 4, 4, 8, 8)


def _adamw_small(g_packed, ws, ms, vs, name):
    n = len(SMALL_NAMES)

    def body(g_ref, *refs):
        w_refs, m_refs, v_refs = refs[:n], refs[n:2 * n], refs[2 * n:3 * n]
        outs = refs[3 * n:]

        def update(p, at, g):
            delta, new_m, new_v = _adamw_math(w_refs[p][at], g, m_refs[p][at], v_refs[p][at])
            outs[p][at] = g
            outs[n + p][at] = delta
            outs[2 * n + p][at] = new_m
            outs[3 * n + p][at] = new_v

        row = 0
        for p, r in enumerate(SMALL_ROWS):
            shape = ws[p].shape
            for layer in range(DEPTH):
                first = layer * PACK_ROWS + row
                if len(shape) == 4:
                    for k in range(shape[1]):
                        update(p, (layer, k), g_ref[first + k * shape[2]:first + (k + 1) * shape[2], :])
                elif len(shape) == 3:
                    update(p, (layer,), g_ref[first:first + r, :])
                else:
                    g = jnp.concatenate([g_ref[first + k:first + k + 1, :] for k in range(r)], axis=1)
                    update(p, (slice(layer, layer + 1), slice(None)), g)
            row += r

    res = pl.pallas_call(
        body,
        name=name,
        out_shape=[jax.ShapeDtypeStruct(w.shape, F32) for w in ws] * 4,
        compiler_params=pltpu.CompilerParams(vmem_limit_bytes=VMEM_LIMIT),
    )(g_packed, *ws, *ms, *vs)
    return res[:n], res[n:2 * n], res[2 * n:3 * n], res[3 * n:]


def kernel(x, c, w_ada, b_ada, w_in, w_pool, pool_scale, sgu_ln_g, sgu_ln_b, w_sgu, b_sgu, w_out, ln_g, ln_b, loss_target, m_w_ada, m_b_ada, m_w_in, m_w_pool, m_pool_scale, m_sgu_ln_g, m_sgu_ln_b, m_w_sgu, m_b_sgu, m_w_out, m_ln_g, m_ln_b, v_w_ada, v_b_ada, v_w_in, v_w_pool, v_pool_scale, v_sgu_ln_g, v_sgu_ln_b, v_w_sgu, v_b_sgu, v_w_out, v_ln_g, v_ln_b):
    small_w = dict(w_pool=w_pool, w_sgu=w_sgu, pool_scale=pool_scale, sgu_ln_g=sgu_ln_g, sgu_ln_b=sgu_ln_b,
                   b_sgu=b_sgu, ln_g=ln_g, ln_b=ln_b)
    small_m = dict(w_pool=m_w_pool, w_sgu=m_w_sgu, pool_scale=m_pool_scale, sgu_ln_g=m_sgu_ln_g,
                   sgu_ln_b=m_sgu_ln_b, b_sgu=m_b_sgu, ln_g=m_ln_g, ln_b=m_ln_b)
    small_v = dict(w_pool=v_w_pool, w_sgu=v_w_sgu, pool_scale=v_pool_scale, sgu_ln_g=v_sgu_ln_g,
                   sgu_ln_b=v_sgu_ln_b, b_sgu=v_b_sgu, ln_g=v_ln_g, ln_b=v_ln_b)

    wint_loc = jnp.transpose(w_in, (0, 2, 1)).astype(BF16)
    wout_loc = w_out.astype(BF16)
    small = (w_pool, pool_scale, sgu_ln_g, sgu_ln_b, w_sgu, jnp.transpose(b_sgu, (0, 2, 1)))
    (out0, y0, cdf0, proj0), (cur, y1, cdf1, proj1), gathered0, gathered1, act_all, mod = _forward(
        x[0], c, w_ada, b_ada, small, ln_g, ln_b, [wint_loc[0], wout_loc[0]], [wint_loc[1], wout_loc[1]])
    w_int, w_outf = [gathered0[0], gathered1[0]], [gathered0[1], gathered1[1]]
    acts = [(x[0], proj0, y0, cdf0), (out0, proj1, y1, cdf1)]

    shard_in, shard_out = D_IN // N_DEV, D_MODEL // N_DEV
    a, b = cur, loss_target[0]
    loss_lanes, carry, pending = None, (), []
    g_w_in_t, g_w_out = [None] * DEPTH, [None] * DEPTH
    for l in reversed(range(DEPTH)):
        dx, dproj, h, cat, dy, small_grads, dmod8, lanes, *shards = _layer_backward(
            l, a, b, *acts[l], mod, w_int[l], w_outf[l], small, ln_g, l == DEPTH - 1, f"layer_bwd_{l}",
            carry=carry, reduce=pending)
        if shards:
            g_w_in_t[l + 1], g_w_out[l + 1] = shards
        if l == DEPTH - 1:
            loss_lanes = lanes
        if l > 0:
            g_in, g_out = _grad_matmuls(dproj, h, cat, dy, f"grad_w_{l}")
            pending = [g_in.reshape(4, 2, shard_in, D_MODEL), g_out.reshape(4, 2, shard_out, D_MODEL)]
        carry = (small_grads, dmod8)
        a = b = dx
    grad_x = a[None]

    (g_w_in_t[0], g_w_out[0], small_tot, loss_tile, g_w_ada, d_w_ada, nm_w_ada, nv_w_ada,
     g_b_ada, d_b_ada, nm_b_ada, nv_b_ada) = _grad_tail(
        dproj, h, cat, dy, small_grads.reshape(4, 2, DEPTH * PACK_ROWS // N_DEV, 128), dmod8, loss_lanes,
        w_ada, m_w_ada, v_w_ada, jnp.transpose(act_all), b_ada, m_b_ada, v_b_ada)
    loss = loss_tile[0, 0]

    flat = lambda t: t.reshape(-1, t.shape[-1])
    to_t = lambda t: flat(jnp.transpose(t, (0, 2, 1)))
    from_t = lambda t: jnp.transpose(t.reshape(DEPTH, shard_in, D_MODEL), (0, 2, 1))
    g_w_in, d_w_in, nm_w_in, nv_w_in = [from_t(t) for t in _adamw(to_t(w_in), g_w_in_t, to_t(m_w_in), to_t(v_w_in),
                                                                  shard_in // 2, "adamw_w_in")]
    gwout, d_w_out, nm_w_out, nv_w_out = [t.reshape(w_out.shape) for t in _adamw(
        flat(w_out), g_w_out, flat(m_w_out), flat(v_w_out), shard_out, "adamw_w_out")]
    small_out = _adamw_small(small_tot.reshape(DEPTH * PACK_ROWS, 128), [small_w[n] for n in SMALL_NAMES],
                             [small_m[n] for n in SMALL_NAMES], [small_v[n] for n in SMALL_NAMES], "adamw_small")
    gs, ds, ms, vs = [dict(zip(SMALL_NAMES, group)) for group in small_out]

    def ordered(w_ada_, b_ada_, w_in_, small, w_out_):
        return (w_ada_, b_ada_, w_in_, small["w_pool"], small["pool_scale"], small["sgu_ln_g"], small["sgu_ln_b"],
                small["w_sgu"], small["b_sgu"], w_out_, small["ln_g"], small["ln_b"])

    return (loss, grad_x,
            *ordered(g_w_ada, g_b_ada, g_w_in, gs, gwout),
            *ordered(d_w_ada, d_b_ada, d_w_in, ds, d_w_out),
            *ordered(nm_w_ada, nm_b_ada, nm_w_in, ms, nm_w_out),
            *ordered(nv_w_ada, nv_b_ada, nv_w_in, vs, nv_w_out))
```

```python
import jax
import jax.numpy as jnp
from jax import lax
from jax.experimental import pallas as pl
from jax.experimental.pallas import tpu as pltpu

F32 = jnp.float32
BF16 = jnp.bfloat16

D_MODEL = 1024
SEQ = 2048
DEPTH = 2
D_POOL = 512
D_SGU = 512
D_IN = 2560
N_GROUPS = 4
GROUP = 128
N_HEADS = 4
HEAD = 128
CHUNK = 128
WINDOWS = (2, 4, 8, 16)
ALPHA = (2.0 * DEPTH) ** 0.25
LN_EPS = 1e-5
N_DEV = 8

ADAM_LR = 0.001
ADAM_B1 = 0.9
ADAM_B2 = 0.999
ADAM_EPS = 1e-08
ADAM_WD = 0.01
ADAM_STEP = 10

TM = 256
HALO = 16
N_TILES = SEQ // TM
VMEM_LIMIT = 60 * 1024 * 1024

ROW_WPOOL = 0
ROW_WSGU = 512
ROW_PSCALE = 1024
ROW_SLNG = 1028
ROW_SLNB = 1032
ROW_BSGU = 1036
ROW_LNG = 1040
ROW_LNB = 1048
PACK_ROWS = 1088
DMOD_COLS = DEPTH * 3 * D_MODEL // 8

SQRT_HALF = 0.7071067811865476
INV_SQRT_2PI = 0.3989422804014327


def _ln(x):
    mu = jnp.mean(x, axis=-1, keepdims=True)
    xc = x - mu
    var = jnp.mean(xc * xc, axis=-1, keepdims=True)
    rstd = lax.rsqrt(var + LN_EPS)
    return xc * rstd, rstd


def _ln_bwd(dxn, xn, rstd):
    m1 = jnp.mean(dxn, axis=-1, keepdims=True)
    m2 = jnp.mean(dxn * xn, axis=-1, keepdims=True)
    return rstd * (dxn - m1 - xn * m2)


def _normal_cdf(x):
    return 0.5 * (1.0 + lax.erf(x * SQRT_HALF))


def _gelu_parts(x, cdf, with_grad):
    if not with_grad:
        return x * cdf, None
    return x * cdf, cdf + x * (INV_SQRT_2PI * jnp.exp(-0.5 * x * x))


def _silu_parts(x):
    s = jax.nn.sigmoid(x)
    return x * s, s * (1.0 + x * (1.0 - s))


def _dot(a, b):
    return lax.dot_general(a, b, (((1,), (0,)), ((), ())), preferred_element_type=F32)


def _dot_nt(a, b):
    return lax.dot_general(a, b, (((1,), (1,)), ((), ())), preferred_element_type=F32)


def _dot_tn(a, b):
    return lax.dot_general(a, b, (((0,), (0,)), ((), ())), preferred_element_type=F32)


def _row_index(tile):
    return tile * TM + lax.broadcasted_iota(jnp.int32, (TM, 1), 0)


def _window_sums(ext, forward):
    n = TM + HALO
    cur = ext
    outs = []
    for g in range(N_GROUPS):
        step = 1 << g
        cur = cur + pltpu.roll(cur, step if forward else n - step, 0)
        rows = cur[HALO:, :GROUP] if forward else cur[:TM, :GROUP]
        outs.append(rows)
        cur = cur[:, GROUP:] if g + 1 < N_GROUPS else None
    return outs


def _inverse_counts(rows):
    return [1.0 / jnp.minimum(rows + 1, w).astype(F32) for w in WINDOWS]


def _tril_bf16(w):
    t = lax.broadcasted_iota(jnp.int32, (CHUNK, CHUNK), 0)
    s = lax.broadcasted_iota(jnp.int32, (CHUNK, CHUNK), 1)
    return jnp.where(t >= s, w, 0.0).astype(BF16)


class _MixWeights:
    def __init__(self, layer, wpool_ref, pscale_ref, slng_ref, slnb_ref, wsgu_ref, bsgut_ref):
        self.layer = layer
        self.wpool_ref, self.pscale_ref, self.slng_ref, self.slnb_ref = wpool_ref, pscale_ref, slng_ref, slnb_ref
        self.wsgu_ref, self.bsgut_ref = wsgu_ref, bsgut_ref

    def pool(self, g):
        return self.wpool_ref[self.layer, g].astype(BF16)

    def pool_scale(self, g):
        return self.pscale_ref[self.layer:self.layer + 1, g * GROUP:(g + 1) * GROUP]

    def ln_gain(self, h):
        return self.slng_ref[self.layer, h:h + 1, :]

    def ln_bias(self, h):
        return self.slnb_ref[self.layer, h:h + 1, :]

    def mix(self, h):
        return _tril_bf16(self.wsgu_ref[self.layer, h])

    def mix_bias(self, h):
        return self.bsgut_ref[self.layer, :, h:h + 1]


SMALL_SPECS = ((DEPTH, N_GROUPS, GROUP, GROUP), (DEPTH, D_POOL), (DEPTH, N_HEADS, HEAD), (DEPTH, N_HEADS, HEAD),
               (DEPTH, N_HEADS, CHUNK, CHUNK), (DEPTH, CHUNK, N_HEADS))


def _mix_forward(proj, halo, tile, w, cdf=None):
    keep = cdf is not None
    rows = _row_index(tile)
    inv_counts = _inverse_counts(rows)
    xa = proj[:, 0:D_POOL]
    ga = proj[:, D_POOL:2 * D_POOL]
    sums = _window_sums(jnp.concatenate([halo, xa], axis=0), True)
    ga_act, ga_grad = _silu_parts(ga)
    pooled, pw, ya = [], [], []
    for g in range(N_GROUPS):
        sl = slice(g * GROUP, (g + 1) * GROUP)
        p = (sums[g] * inv_counts[g] - xa[:, sl]).astype(BF16)
        q = _dot(p, w.pool(g))
        pooled.append(p)
        pw.append(q)
        ya.append(q * w.pool_scale(g) * ga_act[:, sl])

    u = proj[:, 2 * D_POOL:2 * D_POOL + D_SGU]
    v = proj[:, 2 * D_POOL + D_SGU:2 * D_POOL + 2 * D_SGU]
    gb = proj[:, 2 * D_POOL + 2 * D_SGU:]
    gb_act, gb_grad = _silu_parts(gb)
    if cdf is None:
        cdf = jnp.concatenate([_normal_cdf(u), _normal_cdf(v)], axis=1)
    u_act, u_grad = _gelu_parts(u, cdf[:, :D_SGU], keep)
    v_act, v_grad = _gelu_parts(v, cdf[:, D_SGU:], keep)
    vn, vrstd, vln, mixed, yb = [], [], [], [], []
    for h in range(N_HEADS):
        sl = slice(h * HEAD, (h + 1) * HEAD)
        n_h, r_h = _ln(v_act[:, sl])
        l_h = (n_h * w.ln_gain(h) + w.ln_bias(h)).astype(BF16)
        w_h = w.mix(h)
        bias = w.mix_bias(h)
        m_h = jnp.concatenate(
            [_dot(w_h, l_h[k * CHUNK:(k + 1) * CHUNK]) + bias for k in range(TM // CHUNK)], axis=0)
        vn.append(n_h)
        vrstd.append(r_h)
        vln.append(l_h)
        mixed.append(m_h)
        yb.append(u_act[:, sl] * m_h * gb_act[:, sl])
    cat = jnp.concatenate(ya + yb, axis=1)
    if not keep:
        return cat, cdf
    return cat, dict(inv_counts=inv_counts, ga_act=ga_act, ga_grad=ga_grad, pooled=pooled, pw=pw, u_grad=u_grad,
                     v_grad=v_grad, u_act=u_act, gb_act=gb_act, gb_grad=gb_grad, vn=vn, vrstd=vrstd, vln=vln,
                     mixed=mixed)


def _const_spec(shape):
    nd = len(shape)
    return pl.BlockSpec(shape, lambda i: (0,) * nd)


VEC_LNG, VEC_LNB, VEC_POOL, VEC_SGU, VEC_SHIFT, VEC_SCALE, VEC_GATE, VEC_LOSS = range(8)


def _layer_backward(layer, a, b, x, proj, y, cdf, mod, w_int, w_outf, small, ln_g, is_last, name, carry=(),
                    reduce=()):
    n_red, n_carry = len(reduce), len(carry)
    base = layer * PACK_ROWS

    def body(a_ref, b_ref, x_ref, proj_ref, prev_ref, y_ref, cdf_ref, mod_ref, wint_ref, wout_ref, wpool_ref,
             pscale_ref, slng_ref, slnb_ref, wsgu_ref, bsgut_ref, lng_ref, *rest):
        weights = _MixWeights(layer, wpool_ref, pscale_ref, slng_ref, slnb_ref, wsgu_ref, bsgut_ref)
        carry_refs, rest = rest[:n_carry], rest[n_carry:]
        part_refs, rest = rest[:n_red], rest[n_red:]
        dx_ref, dproj_ref, h_ref, cat_ref, dy_ref, small_ref, dmod_ref, loss_ref = rest[:8]
        shard_refs, rest = rest[8:8 + n_red], rest[8 + n_red:]
        vec_ref, dmix_ref, halo_ref = rest[:3]
        step = pl.program_id(0)
        tile = N_TILES - 1 - step

        def scatter():
            bufs, sems = rest[3:3 + 5 * n_red], rest[3 + 5 * n_red:]
            arrays = [dict(part=part_refs[n], out=shard_refs[n], staged=True, stage=bufs[5 * n], sib=bufs[5 * n + 1],
                           snd=bufs[5 * n + 2], rcv=bufs[5 * n + 3], relay=bufs[5 * n + 4]) for n in range(n_red)]
            return _ChipReduceScatter(arrays, *sems)

        @pl.when(step == 0)
        def _():
            small_ref[...] = jnp.zeros_like(small_ref)
            dmod_ref[...] = jnp.zeros_like(dmod_ref)
            vec_ref[...] = jnp.zeros_like(vec_ref)
            dmix_ref[...] = jnp.zeros_like(dmix_ref)
            halo_ref[...] = jnp.zeros_like(halo_ref)
            if n_red:
                scatter().start()

        if n_red:
            @pl.when(step == 1)
            def _():
                scatter().exchange()

            @pl.when(step == N_TILES // 2)
            def _():
                scatter().fold()

        def acc(row, lo, val):
            hi = lo + val.shape[1]
            vec_ref[row:row + 1, lo:hi] += jnp.sum(val, axis=0, keepdims=True)

        xt = x_ref[...]
        yt = y_ref[...]
        shift = mod_ref[layer:layer + 1, 0:D_MODEL]
        scale = mod_ref[layer:layer + 1, D_MODEL:2 * D_MODEL]
        gate = mod_ref[layer:layer + 1, 2 * D_MODEL:]
        ln_gain = lng_ref[layer:layer + 1, :]

        zn, zrstd = _ln(ALPHA * xt + gate * yt)
        if is_last:
            diff = a_ref[...] - b_ref[...]
            acc(VEC_LOSS, 0, diff * diff)
            dout = diff * (1.0 / D_MODEL)
        else:
            dout = a_ref[...]
        acc(VEC_LNG, 0, dout * zn)
        acc(VEC_LNB, 0, dout)
        dz = _ln_bwd(dout * ln_gain, zn, zrstd)
        acc(VEC_GATE, 0, dz * yt)
        dy = (dz * gate).astype(BF16)
        dy_ref[...] = dy
        dcat = _dot_nt(dy, wout_ref[...])

        proj = proj_ref[...]
        prev = jnp.where(tile > 0, prev_ref[...], 0.0)
        cat, k = _mix_forward(proj, prev, tile, weights, cdf_ref[...])
        cat_ref[...] = cat.astype(BF16)

        dga, dq = [], []
        for g in range(N_GROUPS):
            sl = slice(g * GROUP, (g + 1) * GROUP)
            pscale = weights.pool_scale(g)
            dya = dcat[:, sl]
            dyp = dya * k["ga_act"][:, sl]
            dga.append(dya * k["pw"][g] * pscale * k["ga_grad"][:, sl])
            acc(VEC_POOL, g * GROUP, dyp * k["pw"][g])
            dpw = (dyp * pscale).astype(BF16)
            rows = pl.ds(base + ROW_WPOOL + g * GROUP, GROUP)
            small_ref[rows, :] += _dot_tn(k["pooled"][g], dpw)
            dq.append(_dot_nt(dpw, weights.pool(g)))
        dpooled = jnp.concatenate(dq, axis=1)
        scaled = jnp.concatenate([dq[g] * k["inv_counts"][g] for g in range(N_GROUPS)], axis=1)
        sums = _window_sums(jnp.concatenate([scaled, halo_ref[...]], axis=0), False)
        halo_ref[...] = scaled[0:HALO]
        dxa = jnp.concatenate(sums, axis=1) - dpooled

        du, dv, dgb = [], [], []
        for h in range(N_HEADS):
            sl = slice(h * HEAD, (h + 1) * HEAD)
            dyb = dcat[:, D_POOL + h * HEAD:D_POOL + (h + 1) * HEAD]
            m_h = k["mixed"][h]
            ug = k["u_act"][:, sl] * dyb
            du.append(dyb * m_h * k["gb_act"][:, sl] * k["u_grad"][:, sl])
            dgb.append(ug * m_h * k["gb_grad"][:, sl])
            dmixed = ug * k["gb_act"][:, sl]
            dmixed_bf = dmixed.astype(BF16)
            w_h = weights.mix(h)
            dvln_parts = []
            dmix_sum = dmix_ref[h]
            wsgu_rows = pl.ds(base + ROW_WSGU + h * CHUNK, CHUNK)
            dws = small_ref[wsgu_rows, :]
            for c in range(TM // CHUNK):
                cs = slice(c * CHUNK, (c + 1) * CHUNK)
                dmix_sum = dmix_sum + dmixed[cs]
                dws = dws + _dot_nt(dmixed_bf[cs], k["vln"][h][cs])
                dvln_parts.append(_dot_tn(w_h, dmixed_bf[cs]))
            dmix_ref[h] = dmix_sum
            small_ref[wsgu_rows, :] = dws
            dvln = jnp.concatenate(dvln_parts, axis=0)
            acc(VEC_SGU, h * HEAD, dvln * k["vn"][h])
            acc(VEC_SGU, D_SGU + h * HEAD, dvln)
            dvv = _ln_bwd(dvln * weights.ln_gain(h), k["vn"][h], k["vrstd"][h])
            dv.append(dvv * k["v_grad"][:, sl])

        dproj = jnp.concatenate([dxa] + dga + du + dv + dgb, axis=1).astype(BF16)
        dproj_ref[...] = dproj
        dh = _dot(dproj, wint_ref[...])

        xn, xrstd = _ln(xt)
        h_ref[...] = (xn * (1.0 + scale) + shift).astype(BF16)
        acc(VEC_SCALE, 0, dh * xn)
        acc(VEC_SHIFT, 0, dh)
        dx_ref[...] = _ln_bwd(dh * (1.0 + scale), xn, xrstd) + ALPHA * dz

        @pl.when(step == N_TILES - 1)
        def _():
            def put(row0, vec_row, lo, n):
                for r in range(n):
                    small_ref[base + row0 + r:base + row0 + r + 1, :] = (
                        vec_ref[vec_row:vec_row + 1, lo + r * 128:lo + (r + 1) * 128])

            put(ROW_PSCALE, VEC_POOL, 0, 4)
            put(ROW_SLNG, VEC_SGU, 0, 4)
            put(ROW_SLNB, VEC_SGU, D_SGU, 4)
            put(ROW_LNG, VEC_LNG, 0, 8)
            put(ROW_LNB, VEC_LNB, 0, 8)
            ones = jnp.ones((8, HEAD), F32)
            t = lax.broadcasted_iota(jnp.int32, (CHUNK, CHUNK), 0)
            s = lax.broadcasted_iota(jnp.int32, (CHUNK, CHUNK), 1)
            for h in range(N_HEADS):
                bias_rows = lax.dot_general(ones, dmix_ref[h], (((1,), (1,)), ((), ())),
                                            preferred_element_type=F32, precision=lax.Precision.HIGHEST)
                small_ref[base + ROW_BSGU + h:base + ROW_BSGU + h + 1, :] = bias_rows[0:1]
                rows = pl.ds(base + ROW_WSGU + h * CHUNK, CHUNK)
                small_ref[rows, :] = jnp.where(t >= s, small_ref[rows, :], 0.0)
            pieces = ((0, VEC_SHIFT, 0, 768),
                      (1, VEC_SHIFT, 768, 256), (1, VEC_SCALE, 0, 512),
                      (2, VEC_SCALE, 512, 512), (2, VEC_GATE, 0, 256),
                      (3, VEC_GATE, 256, 768))
            filled = [0] * 4
            for q, vec_row, lo, n in pieces:
                row = 4 * layer + q
                dmod_ref[row:row + 1, filled[q]:filled[q] + n] = vec_ref[vec_row:vec_row + 1, lo:lo + n]
                filled[q] += n
            if n_carry:
                for other in range(layer + 1, DEPTH):
                    rows = pl.ds(other * PACK_ROWS, PACK_ROWS)
                    small_ref[rows, :] = carry_refs[0][rows, :]
                    dmod_ref[4 * other:4 * other + 4, :] = carry_refs[1][4 * other:4 * other + 4, :]
            loss_ref[...] = vec_ref[VEC_LOSS:VEC_LOSS + 1, :]
            if n_red:
                scatter().finish()
                scatter().wait_sends()

    rev = lambda w: pl.BlockSpec((TM, w), lambda i: (N_TILES - 1 - i, 0))
    prev_spec = pl.BlockSpec(
        (HALO, D_POOL), lambda i: (jnp.maximum((N_TILES - 1 - i) * (TM // HALO) - 1, 0), 0))
    comm_scratch = []
    for p in reduce:
        comm_scratch += _ChipReduceScatter.buffers(p.shape[2], p.shape[3], p.dtype)
    if n_red:
        comm_scratch += _ChipReduceScatter.semaphores(n_red)
    return pl.pallas_call(
        body,
        name=name,
        grid=(N_TILES,),
        in_specs=[rev(D_MODEL), rev(D_MODEL) if is_last else pl.BlockSpec((TM, D_MODEL), lambda i: (0, 0)),
                  rev(D_MODEL), rev(D_IN), prev_spec, rev(D_MODEL), rev(2 * D_SGU),
                  _const_spec((DEPTH, 3 * D_MODEL)), _const_spec((D_IN, D_MODEL)), _const_spec((D_MODEL, D_MODEL))]
                 + [_const_spec(s) for s in SMALL_SPECS] + [_const_spec((DEPTH, D_MODEL))]
                 + [_const_spec(c.shape) for c in carry] + [ANY] * n_red,
        out_specs=[rev(D_MODEL), rev(D_IN), rev(D_MODEL), rev(D_MODEL), rev(D_MODEL),
                   _const_spec((DEPTH * PACK_ROWS, 128)), _const_spec((8, DMOD_COLS)), _const_spec((1, D_MODEL))]
                  + [_const_spec(p.shape[2:]) for p in reduce],
        out_shape=[jax.ShapeDtypeStruct((SEQ, D_MODEL), F32), jax.ShapeDtypeStruct((SEQ, D_IN), BF16),
                   jax.ShapeDtypeStruct((SEQ, D_MODEL), BF16), jax.ShapeDtypeStruct((SEQ, D_MODEL), BF16),
                   jax.ShapeDtypeStruct((SEQ, D_MODEL), BF16), jax.ShapeDtypeStruct((DEPTH * PACK_ROWS, 128), F32),
                   jax.ShapeDtypeStruct((8, DMOD_COLS), F32), jax.ShapeDtypeStruct((1, D_MODEL), F32)]
                  + [jax.ShapeDtypeStruct(p.shape[2:], F32) for p in reduce],
        scratch_shapes=[pltpu.VMEM((8, D_MODEL), F32), pltpu.VMEM((N_HEADS, CHUNK, HEAD), F32),
                        pltpu.VMEM((HALO, D_POOL), F32)] + comm_scratch,
        compiler_params=pltpu.CompilerParams(dimension_semantics=("arbitrary",), vmem_limit_bytes=VMEM_LIMIT),
    )(a, b, x, proj, proj, y, cdf, mod, w_int, w_outf, *small, ln_g, *carry, *reduce)


def _grad_matmuls(dproj, h, cat, dy, name):
    in_cols, out_cols = D_IN // 4, D_MODEL // 2
    in_steps = D_IN // in_cols

    def body(dproj_ref, h_ref, cat_ref, dy_ref, gin_ref, gout_ref):
        step = pl.program_id(0)

        @pl.when(step < in_steps)
        def _():
            gin_ref[...] = _dot_tn(dproj_ref[...], h_ref[...]).astype(BF16)

        @pl.when(step >= in_steps)
        def _():
            gout_ref[...] = _dot_tn(cat_ref[...], dy_ref[...]).astype(BF16)

    in_block = lambda j: jnp.minimum(j, in_steps - 1)
    out_block = lambda j: jnp.maximum(j - in_steps, 0)
    return pl.pallas_call(
        body,
        name=name,
        grid=(in_steps + D_MODEL // out_cols,),
        in_specs=[pl.BlockSpec((SEQ, in_cols), lambda j: (0, in_block(j))), _const_spec((SEQ, D_MODEL)),
                  pl.BlockSpec((SEQ, out_cols), lambda j: (0, out_block(j))), _const_spec((SEQ, D_MODEL))],
        out_specs=[pl.BlockSpec((in_cols, D_MODEL), lambda j: (in_block(j), 0)),
                   pl.BlockSpec((out_cols, D_MODEL), lambda j: (out_block(j), 0))],
        out_shape=[jax.ShapeDtypeStruct((D_IN, D_MODEL), BF16), jax.ShapeDtypeStruct((D_MODEL, D_MODEL), BF16)],
        compiler_params=pltpu.CompilerParams(dimension_semantics=("arbitrary",), vmem_limit_bytes=VMEM_LIMIT),
    )(dproj, h, cat, dy)


def _adamw_math(w, g, m, v):
    m = ADAM_B1 * m + (1.0 - ADAM_B1) * g
    v = ADAM_B2 * v + (1.0 - ADAM_B2) * (g * g)
    m_hat = m / (1.0 - ADAM_B1 ** ADAM_STEP)
    v_hat = v / (1.0 - ADAM_B2 ** ADAM_STEP)
    delta = -ADAM_LR * (m_hat / (jnp.sqrt(v_hat) + ADAM_EPS) + ADAM_WD * w)
    return delta, m, v


def _adamw(w, grads, m, v, block_rows, name):
    rows, cols = grads[0].shape
    blocks = rows // block_rows

    def body(w_ref, m_ref, v_ref, *rest):
        g_refs, (g_ref, d_ref, nm_ref, nv_ref) = rest[:DEPTH], rest[DEPTH:]
        for layer in range(DEPTH):
            @pl.when(pl.program_id(0) == layer)
            def _():
                g = g_refs[layer][...]
                g_ref[...] = g
                d_ref[...], nm_ref[...], nv_ref[...] = _adamw_math(w_ref[...], g, m_ref[...], v_ref[...])

    def grad_spec(layer):
        return pl.BlockSpec((block_rows, cols),
                            lambda l, i: (jnp.where(l == layer, i, jnp.where(l < layer, 0, blocks - 1)), 0))

    spec = pl.BlockSpec((block_rows, cols), lambda l, i: (l * blocks + i, 0))
    return pl.pallas_call(
        body,
        name=name,
        grid=(DEPTH, blocks),
        in_specs=[spec] * 3 + [grad_spec(layer) for layer in range(DEPTH)],
        out_specs=[spec] * 4,
        out_shape=[jax.ShapeDtypeStruct(w.shape, F32)] * 4,
        compiler_params=pltpu.CompilerParams(dimension_semantics=("arbitrary", "arbitrary"),
                                             vmem_limit_bytes=VMEM_LIMIT),
    )(w, m, v, *grads)


MESH = pl.DeviceIdType.MESH
SIBLING = 1
ANY = pl.BlockSpec(memory_space=pl.ANY)
VMEM = pl.BlockSpec(memory_space=pltpu.VMEM)


def _me():
    return lax.axis_index("x"), lax.axis_index("y"), lax.axis_index("c")


def _peer(r):
    x, y, c = _me()
    return (1 - x if r & 4 else x, 1 - y if r & 2 else y, 1 - c if r & 1 else c)


def _index(dev):
    return 4 * dev[0] + 2 * dev[1] + dev[2]


def _remote(src, dst, send_sem, recv_sem, dev):
    return pltpu.make_async_remote_copy(src_ref=src, dst_ref=dst, send_sem=send_sem, recv_sem=recv_sem,
                                        device_id=dev, device_id_type=MESH)


ACROSS_X, ACROSS_Y, ACROSS_BOTH = 4, 2, 6
GATHER_SEMS = 11


class _TwoLevelGather:
    def __init__(self, out, send_sems, recv_sems, src=None):
        self.out, self.send_sems, self.recv_sems, self.src = out, send_sems, recv_sems, src
        self.rows = (out.shape[0] // N_DEV) if len(out.shape) == 2 else out.shape[1]
        self.half = self.rows // 2

    def _slot(self, block):
        if len(self.out.shape) == 2:
            return self.out.at[pl.ds(pl.multiple_of(_index(block) * self.rows, self.rows), self.rows)]
        return self.out.at[_index(block)]

    def _copy(self, k, block, part, to, src=None):
        slot = self._slot(block)
        if part is not None:
            rows = pl.ds(part * self.half, self.half)
            slot = slot.at[rows]
            src = None if src is None else src.at[rows]
        return _remote(slot if src is None else src, slot, self.send_sems.at[k], self.recv_sems.at[k], to)

    def _mine(self):
        me = _me()
        src = self._slot(me) if self.src is None else self.src
        x, y = _peer(ACROSS_X), _peer(ACROSS_Y)
        return [self._copy(1, me, 0, x, src), self._copy(3, me, 1, y, src), self._copy(0, me, None, _peer(SIBLING), src),
                self._copy(2, me, 1, x, src), self._copy(4, me, 0, y, src)]

    def _relayed(self):
        return [self._copy(5, _peer(ACROSS_X), 0, _peer(ACROSS_Y)), self._copy(6, _peer(ACROSS_Y), 1, _peer(ACROSS_X))]

    def _passed(self):
        sib, far = _peer(SIBLING), _peer(ACROSS_BOTH)
        return [self._copy(7, _peer(ACROSS_X), None, sib), self._copy(8, _peer(ACROSS_Y), None, sib),
                self._copy(9, far, 0, sib), self._copy(10, far, 1, sib)]

    def _arrival(self, k, r, part):
        return self._copy(k, _peer(r), part, _me())

    def send_first(self):
        for cp in self._mine()[:3]:
            cp.start()

    def send_second(self):
        for cp in self._mine()[3:]:
            cp.start()

    def send_mine(self):
        self.send_first()
        self.send_second()

    def relay(self):
        relayed = self._relayed()
        self._arrival(1, ACROSS_X, 0).wait_recv()
        relayed[0].start()
        self._arrival(3, ACROSS_Y, 1).wait_recv()
        relayed[1].start()

    def pass_near(self):
        passed = self._passed()
        self._arrival(2, ACROSS_X, 1).wait_recv()
        passed[0].start()
        self._arrival(4, ACROSS_Y, 0).wait_recv()
        passed[1].start()

    def pass_far(self):
        passed = self._passed()
        self._arrival(5, ACROSS_BOTH, 0).wait_recv()
        passed[2].start()
        self._arrival(6, ACROSS_BOTH, 1).wait_recv()
        passed[3].start()

    def pass_on(self):
        self.pass_near()
        self.pass_far()

    def wait_sibling(self):
        self._arrival(0, SIBLING, None).wait_recv()

    def wait_passed(self, r):
        if r == ACROSS_BOTH:
            self._arrival(9, r ^ SIBLING, 0).wait_recv()
            self._arrival(10, r ^ SIBLING, 1).wait_recv()
        else:
            self._arrival(7 if r == ACROSS_X else 8, r ^ SIBLING, None).wait_recv()

    def wait_rest(self):
        self.wait_sibling()
        for r in (ACROSS_X, ACROSS_Y, ACROSS_BOTH):
            self.wait_passed(r)

    def wait_sends(self):
        for cp in self._mine() + self._relayed() + self._passed():
            cp.wait_send()


class _ChipReduceScatter:
    SLOTS = 6

    def __init__(self, arrays, l_sem, d_send, d_recv, i_send, i_recv):
        self.arrays = arrays
        self.l_sem, self.d_send, self.d_recv, self.i_send, self.i_recv = l_sem, d_send, d_recv, i_send, i_recv

    @staticmethod
    def buffers(rows, cols, dtype, staged=True):
        stage = [pltpu.VMEM((4, rows, cols), dtype)] if staged else []
        return stage + [pltpu.VMEM((4, rows, cols), dtype), pltpu.VMEM((3, rows, cols), dtype),
                        pltpu.VMEM((2, rows, cols), dtype), pltpu.VMEM((2, rows // 2, cols), dtype)]

    @classmethod
    def semaphores(cls, n):
        return [pltpu.SemaphoreType.DMA((n,)), pltpu.SemaphoreType.DMA((n, 4)), pltpu.SemaphoreType.DMA((n, 4)),
                pltpu.SemaphoreType.DMA((n, cls.SLOTS)), pltpu.SemaphoreType.DMA((n, cls.SLOTS))]

    def _pick(self, which):
        return list(enumerate(self.arrays)) if which is None else [(n, self.arrays[n]) for n in which]

    @staticmethod
    def _chip(r):
        dev = _me() if r is None else _peer(r)
        return 2 * dev[0] + dev[1]

    def _staging(self, which):
        c = _me()[2]
        return [pltpu.make_async_copy(a["part"].at[pl.ds(0, 4), c], a["stage"], self.l_sem.at[n])
                for n, a in self._pick(which) if a["staged"]]

    def _first(self, which, chip):
        other = 1 - _me()[2]
        return [_remote(a["part"].at[chip, other], a["sib"].at[chip], self.d_send.at[n, chip], self.d_recv.at[n, chip],
                        _peer(SIBLING)) for n, a in self._pick(which)]

    @staticmethod
    def _halves(a):
        half = a["rcv"].shape[1] // 2
        return pl.ds(0, half), pl.ds(half, half)

    def _hops(self, n, a):
        h0, h1 = self._halves(a)
        x, y = _peer(ACROSS_X), _peer(ACROSS_Y)
        snd, rcv, relay = a["snd"], a["rcv"], a["relay"]
        pairs = [(snd.at[2, h0], relay.at[0], x), (snd.at[2, h1], relay.at[1], y),
                 (snd.at[0, h0], rcv.at[0, h0], x), (snd.at[0, h1], rcv.at[0, h1], x),
                 (snd.at[1, h1], rcv.at[1, h1], y), (snd.at[1, h0], rcv.at[1, h0], y)]
        return [_remote(s, d, self.i_send.at[n, k], self.i_recv.at[n, k], to) for k, (s, d, to) in enumerate(pairs)]

    def _mine(self, a, chip, rows=None):
        src = a["stage"].at[chip] if a["staged"] else a["part"].at[chip, _me()[2]]
        mine, sib = (src[...], a["sib"][chip]) if rows is None else (src[rows, :], a["sib"][chip, rows, :])
        return mine.astype(F32) + sib.astype(F32)

    def start(self, which=None, chips=None):
        if chips is None:
            for cp in self._staging(which):
                cp.start()
        for chip in range(4) if chips is None else chips:
            for cp in self._first(which, chip):
                cp.start()

    def send_far(self, which=None):
        far = self._chip(ACROSS_BOTH)
        for cp in self._staging(which):
            cp.wait()
        for cp in self._first(which, far):
            cp.wait_recv()
        for n, a in self._pick(which):
            hops = self._hops(n, a)
            a["snd"][2] = self._mine(a, far).astype(a["snd"].dtype)
            hops[0].start()
            hops[1].start()

    def send_near(self, r, which=None):
        chip = self._chip(r)
        for cp in self._first(which, chip):
            cp.wait_recv()
        for n, a in self._pick(which):
            h0, h1 = self._halves(a)
            hops = self._hops(n, a)
            if r == ACROSS_X:
                a["snd"][0, h0, :] = self._mine(a, chip, h0).astype(a["snd"].dtype)
                hops[2].start()
            else:
                a["snd"][1, h1, :] = self._mine(a, chip, h1).astype(a["snd"].dtype)
                hops[4].start()

    def exchange(self, which=None):
        self.send_far(which)
        self.send_near(ACROSS_X, which)
        self.send_near(ACROSS_Y, which)

    def fold(self, which=None):
        across_x, across_y = self._chip(ACROSS_X), self._chip(ACROSS_Y)
        for n, a in self._pick(which):
            h0, h1 = self._halves(a)
            hops = self._hops(n, a)
            dtype = a["snd"].dtype
            hops[1].wait_recv()
            a["snd"][0, h1, :] = (self._mine(a, across_x, h1) + a["relay"][1].astype(F32)).astype(dtype)
            hops[3].start()
            hops[0].wait_recv()
            a["snd"][1, h0, :] = (self._mine(a, across_y, h0) + a["relay"][0].astype(F32)).astype(dtype)
            hops[5].start()

    def finish(self, which=None):
        home = self._chip(None)
        for cp in self._first(which, home):
            cp.wait_recv()
        for n, a in self._pick(which):
            hops = self._hops(n, a)
            a["out"][...] = self._mine(a, home)
            hops[2].wait_recv()
            hops[3].wait_recv()
            a["out"][...] += a["rcv"][0].astype(F32)
            hops[4].wait_recv()
            hops[5].wait_recv()
            a["out"][...] += a["rcv"][1].astype(F32)

    def wait_sends(self, which=None):
        for chip in range(4):
            for cp in self._first(which, chip):
                cp.wait_send()
        for n, a in self._pick(which):
            for cp in self._hops(n, a):
                cp.wait_send()


def _direct_exchange(src_of, dst_of, send_sems, recv_sems):
    me = _me()
    copies = [_remote(src_of(_peer(r)), dst_of(me), send_sems.at[r - 1], recv_sems.at[r - 1], _peer(r))
              for r in range(1, N_DEV)]
    for cp in copies:
        cp.start()
    return copies


def _wait_direct(copies):
    for cp in copies:
        cp.wait_recv()
    for cp in copies:
        cp.wait_send()


def _forward(x, c, w_ada, b_ada, small, ln_g, ln_b, mine, following):
    assert DEPTH == 2
    cols = w_ada.shape[2]
    shard = mine[0].shape[0]
    pair = 2 * shard

    def body(x_hbm, c_ref, wada_hbm, bada_ref, wpool_ref, pscale_ref, slng_ref, slnb_ref, wsgu_ref, bsgut_ref,
             lng_ref, lnb_ref, wint_hbm, wout_hbm, next_in_hbm, next_out_hbm,
             out0_ref, y0_ref, cdf0_ref, proj0_hbm, out1_ref, y1_ref, cdf1_ref, proj1_hbm,
             wint_keep, wout_keep, wint_next, wout_next, acts_ref, mod_ref,
             wint_v, wout_v, h_buf, proj_blk, proj_tile, halo_ref, x_ref, w_send, w_recv, w_local, p_sems,
             t_sems, in_sems, act_all, act_src, part, mod_recv, wada_ref, a_send, a_recv, m_send, m_recv,
             n_send, n_recv, n_local, f_sems):
        step = pl.program_id(0)
        chip_of = lambda dev: 2 * dev[0] + dev[1]

        def hosted():
            return [_TwoLevelGather(out, n_send.at[n], n_recv.at[n], src=src)
                    for n, (out, src) in enumerate(((wint_next, next_in_hbm), (wout_next, next_out_hbm)))]

        def hosted_own():
            me = _me()
            return [pltpu.make_async_copy(g.src, g._slot(me), n_local.at[n]) for n, g in enumerate(hosted())]

        def gathers():
            return (_TwoLevelGather(wint_v, w_send.at[0], w_recv.at[0], src=wint_hbm),
                    _TwoLevelGather(wout_v, w_send.at[1], w_recv.at[1], src=wout_hbm))

        def keeps():
            return [pltpu.make_async_copy(wint_v, wint_keep, w_local.at[2]),
                    pltpu.make_async_copy(wout_v, wout_keep, w_local.at[3])]

        def tile_read(t):
            slot = t % 2
            return pltpu.make_async_copy(proj0_hbm.at[pl.ds(pl.multiple_of(t * TM, TM), TM)], proj_tile.at[slot],
                                         t_sems.at[slot])

        def first_layer_start():
            writes = []

            def project(n, dev):
                first = pl.multiple_of(chip_of(dev) * pair, pair)
                if n >= 2:
                    writes[n - 2].wait()

                @pl.loop(0, N_TILES)
                def _(t):
                    rows = pl.ds(pl.multiple_of(t * TM, TM), TM)
                    proj_blk[n % 2, rows, :] = _dot_nt(h_buf[rows, :], wint_v[pl.ds(first, pair), :])

                cp = pltpu.make_async_copy(proj_blk.at[n % 2], proj0_hbm.at[:, pl.ds(first, pair)], p_sems.at[n % 2])
                cp.start()
                writes.append(cp)

            me = _me()
            halo_ref[...] = jnp.zeros_like(halo_ref)
            gather_in, gather_out = gathers()
            own_in = pltpu.make_async_copy(wint_hbm, gather_in._slot(me), w_local.at[0])
            own_out = pltpu.make_async_copy(wout_hbm, gather_out._slot(me), w_local.at[1])
            x_load = pltpu.make_async_copy(x_hbm, x_ref, in_sems.at[0])
            x_load.start()
            wada_load = pltpu.make_async_copy(wada_hbm, wada_ref, in_sems.at[1])
            wada_load.start()
            mine_index = _index(me)
            cval = c_ref[...]
            act_src[...] = jnp.zeros_like(act_src)
            act_src[0:1, :] = cval * jax.nn.sigmoid(cval)
            act_all[mine_index] = act_src[...]
            act_copies = _direct_exchange(lambda p: act_src, lambda m: act_all.at[_index(m)], a_send, a_recv)

            own_in.start()
            own_out.start()
            gather_in.send_first()

            _wait_direct(act_copies)
            acts = jnp.concatenate([act_all[j, 0:1, :] for j in range(N_DEV)], axis=0)
            acts_ref[...] = acts
            part[...] = jnp.zeros_like(part)
            wada_load.wait()
            for l in range(DEPTH):
                res = lax.dot_general(acts, wada_ref[l], (((1,), (0,)), ((), ())), preferred_element_type=F32,
                                      precision=lax.Precision.HIGHEST)
                for b in range(N_DEV):
                    part[b, l:l + 1, :] = res[b:b + 1, :]
            mod_recv[mine_index] = part[mine_index]
            mod_copies = _direct_exchange(lambda p: part.at[_index(p)], lambda m: mod_recv.at[_index(m)],
                                          m_send, m_recv)
            gather_in.send_second()
            gather_out.send_mine()

            _wait_direct(mod_copies)
            for l in range(DEPTH):
                for j in range(N_DEV):
                    sl = slice(j * cols, (j + 1) * cols)
                    mod_ref[l:l + 1, sl] = mod_recv[j, l:l + 1, :] + bada_ref[l:l + 1, sl]
            x_load.wait()
            shift = mod_ref[0:1, 0:D_MODEL]
            scale = mod_ref[0:1, D_MODEL:2 * D_MODEL]

            @pl.loop(0, N_TILES)
            def _(t):
                rows = pl.ds(pl.multiple_of(t * TM, TM), TM)
                xn, _ = _ln(x_ref[rows, :])
                h_buf[rows, :] = (xn * (1.0 + scale) + shift).astype(BF16)

            gather_in.relay()
            own_in.wait()
            gather_in.wait_sibling()
            project(0, me)
            gather_in.pass_near()
            gather_in.wait_passed(ACROSS_X)
            project(1, _peer(ACROSS_X))
            gather_out.relay()
            hosted_own()[0].start()
            hosted()[0].send_mine()
            gather_in.wait_passed(ACROSS_Y)
            project(2, _peer(ACROSS_Y))
            gather_in.pass_far()
            gather_in.wait_passed(ACROSS_BOTH)
            project(3, _peer(ACROSS_BOTH))

            gather_out.pass_on()
            gather_out.wait_rest()
            own_out.wait()
            for cp in keeps():
                cp.start()
            writes[2].wait()
            writes[3].wait()
            tile_read(0).start()

        def fetches():
            return [pltpu.make_async_copy(wint_next, wint_v, f_sems.at[0]),
                    pltpu.make_async_copy(wout_next, wout_v, f_sems.at[1])]

        def tile_write(t):
            slot = t % 2
            return pltpu.make_async_copy(proj_tile.at[slot], proj1_hbm.at[pl.ds(pl.multiple_of(t * TM, TM), TM)],
                                         t_sems.at[slot])

        def mix(layer, tile, proj, cdf_ref):
            weights = _MixWeights(layer, wpool_ref, pscale_ref, slng_ref, slnb_ref, wsgu_ref, bsgut_ref)
            cat, cdf_ref[...] = _mix_forward(proj, halo_ref[...], tile, weights)
            halo_ref[...] = proj[TM - HALO:, 0:D_POOL]
            return cat.astype(BF16)

        def close(layer, rows, cat, out_ref, y_ref):
            y = _dot(cat, wout_v[...])
            y_ref[...] = y
            zn, _ = _ln(ALPHA * x_ref[rows, :] + mod_ref[layer:layer + 1, 2 * D_MODEL:] * y)
            out = zn * lng_ref[layer:layer + 1, :] + lnb_ref[layer:layer + 1, :]
            out_ref[...] = out
            return out

        def first_layer_tile(tile):
            @pl.when(tile + 1 < N_TILES)
            def _():
                tile_read(tile + 1).start()

            tile_read(tile).wait()
            rows = pl.ds(pl.multiple_of(tile * TM, TM), TM)
            cat = mix(0, tile, proj_tile[tile % 2], cdf0_ref)
            x_ref[rows, :] = close(0, rows, cat, out0_ref, y0_ref)

        def second_layer_mix(tile):
            rows = pl.ds(pl.multiple_of(tile * TM, TM), TM)
            shift = mod_ref[1:2, 0:D_MODEL]
            scale = mod_ref[1:2, D_MODEL:2 * D_MODEL]
            xn, _ = _ln(x_ref[rows, :])
            h = (xn * (1.0 + scale) + shift).astype(BF16)

            @pl.when(tile >= 2)
            def _():
                tile_write(tile - 2).wait()

            proj_tile[tile % 2] = _dot_nt(h, wint_v[...])
            tile_write(tile).start()
            h_buf[rows, :] = mix(1, tile, proj_tile[tile % 2], cdf1_ref)

        def second_layer_close(tile):
            rows = pl.ds(pl.multiple_of(tile * TM, TM), TM)
            close(1, rows, h_buf[rows, :], out1_ref, y1_ref)

        next_in, next_out = 0, 1

        @pl.when(step < N_TILES)
        def _():
            @pl.when(step == 0)
            def _():
                first_layer_start()

            @pl.when(step == 1)
            def _():
                hosted()[next_in].relay()

            @pl.when(step == N_TILES // 2)
            def _():
                hosted()[next_in].pass_near()
                hosted_own()[next_out].start()
                hosted()[next_out].send_mine()

            @pl.when(step == N_TILES - 1)
            def _():
                hosted()[next_in].pass_far()
                for g in gathers():
                    g.wait_sends()
                for cp in keeps() + [hosted_own()[next_in]]:
                    cp.wait()
                hosted()[next_in].wait_rest()
                fetches()[next_in].start()

            first_layer_tile(step)

        @pl.when((step >= N_TILES) & (step < 2 * N_TILES))
        def _():
            @pl.when(step == N_TILES)
            def _():
                halo_ref[...] = jnp.zeros_like(halo_ref)
                fetches()[next_in].wait()
                hosted()[next_out].relay()

            @pl.when(step == N_TILES + 3)
            def _():
                hosted()[next_out].pass_near()

            @pl.when(step == N_TILES + 6)
            def _():
                hosted()[next_out].pass_far()

            second_layer_mix(step - N_TILES)

            @pl.when(step == 2 * N_TILES - 1)
            def _():
                tile_write(N_TILES - 2).wait()
                tile_write(N_TILES - 1).wait()

        @pl.when(step >= 2 * N_TILES)
        def _():
            @pl.when(step == 2 * N_TILES)
            def _():
                hosted_own()[next_out].wait()
                hosted()[next_out].wait_rest()
                fetches()[next_out].start()
                fetches()[next_out].wait()

            second_layer_close(step - 2 * N_TILES)

            @pl.when(step == 3 * N_TILES - 1)
            def _():
                for g in hosted():
                    g.wait_sends()

    first = lambda w: pl.BlockSpec((TM, w), lambda i: (jnp.minimum(i, N_TILES - 1), 0))
    second = lambda w: pl.BlockSpec((TM, w), lambda i: (jnp.clip(i - N_TILES, 0, N_TILES - 1), 0))
    third = lambda w: pl.BlockSpec((TM, w), lambda i: (jnp.maximum(i - 2 * N_TILES, 0), 0))
    gather_sems = pltpu.SemaphoreType.DMA((2, GATHER_SEMS))
    seven = pltpu.SemaphoreType.DMA((7,))
    per_layer = [jax.ShapeDtypeStruct((SEQ, D_MODEL), F32), jax.ShapeDtypeStruct((SEQ, D_MODEL), F32),
                 jax.ShapeDtypeStruct((SEQ, 2 * D_SGU), F32), jax.ShapeDtypeStruct((SEQ, D_IN), F32)]
    gathered = [jax.ShapeDtypeStruct((D_IN, D_MODEL), BF16), jax.ShapeDtypeStruct((D_MODEL, D_MODEL), BF16)]
    res = pl.pallas_call(
        body,
        name="layers_fwd",
        grid=(3 * N_TILES,),
        in_specs=[ANY, _const_spec(c.shape), ANY, _const_spec(b_ada.shape)] + [_const_spec(s) for s in SMALL_SPECS]
                 + [_const_spec((DEPTH, D_MODEL)), _const_spec((DEPTH, D_MODEL))] + [ANY] * 4,
        out_specs=[first(D_MODEL), first(D_MODEL), first(2 * D_SGU), ANY,
                   third(D_MODEL), third(D_MODEL), second(2 * D_SGU), ANY] + [ANY] * 4
                  + [_const_spec((N_DEV, D_MODEL)), _const_spec((DEPTH, 3 * D_MODEL))],
        out_shape=per_layer * 2 + gathered * 2 + [jax.ShapeDtypeStruct((N_DEV, D_MODEL), F32),
                                                  jax.ShapeDtypeStruct((DEPTH, 3 * D_MODEL), F32)],
        scratch_shapes=[pltpu.VMEM((D_IN, D_MODEL), BF16), pltpu.VMEM((D_MODEL, D_MODEL), BF16),
                        pltpu.VMEM((SEQ, D_MODEL), BF16), pltpu.VMEM((2, SEQ, pair), F32),
                        pltpu.VMEM((2, TM, D_IN), F32), pltpu.VMEM((HALO, D_POOL), F32),
                        pltpu.VMEM((SEQ, D_MODEL), F32),
                        gather_sems, gather_sems, pltpu.SemaphoreType.DMA((4,)), pltpu.SemaphoreType.DMA((2,)),
                        pltpu.SemaphoreType.DMA((2,)), pltpu.SemaphoreType.DMA((2,)),
                        pltpu.VMEM((N_DEV, 8, D_MODEL), F32), pltpu.VMEM((8, D_MODEL), F32),
                        pltpu.VMEM((N_DEV, 8, cols), F32), pltpu.VMEM((N_DEV, 8, cols), F32),
                        pltpu.VMEM(w_ada.shape, F32), seven, seven, seven, seven,
                        gather_sems, gather_sems, pltpu.SemaphoreType.DMA((2,)), pltpu.SemaphoreType.DMA((2,))],
        compiler_params=pltpu.CompilerParams(dimension_semantics=("arbitrary",), vmem_limit_bytes=VMEM_LIMIT),
    )(x, c, w_ada, b_ada, *small, ln_g, ln_b, *mine, *following)
    return res[0:4], res[4:8], res[8:10], res[10:12], res[12], res[13]


ADA_CHUNK = 256


def _grad_tail(dproj, h, cat, dy, small, dmod8, loss_lanes, w_ada, m_ada, v_ada, act_t, b_ada, m_bada, v_bada):
    shard_in, shard_out, shard_small = D_IN // N_DEV, D_MODEL // N_DEV, small.shape[2]
    cols = w_ada.shape[2]
    W_IN, W_OUT, SMALL = 0, 1, 2

    def body(dproj_hbm, h_hbm, cat_hbm, dy_hbm, small_hbm, dmod_ref, lanes_ref, wada_hbm, mada_hbm, vada_hbm,
             act_ref, bada_ref, mbada_ref, vbada_ref,
             gwin_ref, gwout_ref, stot_ref, loss_ref, gada_hbm, dada_hbm, nmada_hbm, nvada_hbm,
             gb_ref, db_ref, nmb_ref, nvb_ref,
             dproj_v, h_v, cat_v, dy_v, part_in, part_out, own_small, loss_src, loss_all, dmod_all, ada_in, ada_out,
             *rest):
        bufs, rest = rest[:13], rest[13:]
        load_sems, rs_sems = rest[0], rest[1:6]
        m_send, m_recv, g_send, g_recv, s_send, s_recv, ada_lsem, ada_ssem = rest[6:]
        mine = _index(_me())

        def update_ada():
            upper = (mine % 2) == 1

            def dmod_of(layer):
                rows = []
                for b in range(N_DEV):
                    r = dmod_all[b, pl.ds(4 * layer + mine // 2, 1), :]
                    rows.append(jnp.where(upper, r[:, cols:], r[:, :cols]))
                return jnp.concatenate(rows, axis=0)

            chunks = [(layer, c) for layer in range(DEPTH) for c in range(D_MODEL // ADA_CHUNK)]

            def loads(i):
                layer, c = chunks[i]
                rows = pl.ds(c * ADA_CHUNK, ADA_CHUNK)
                return [pltpu.make_async_copy(src.at[layer, rows], ada_in.at[i % 2, k], ada_lsem.at[i % 2, k])
                        for k, src in enumerate((wada_hbm, mada_hbm, vada_hbm))]

            def stores(i):
                layer, c = chunks[i]
                rows = pl.ds(c * ADA_CHUNK, ADA_CHUNK)
                return [pltpu.make_async_copy(ada_out.at[i % 2, k], dst.at[layer, rows], ada_ssem.at[i % 2, k])
                        for k, dst in enumerate((gada_hbm, dada_hbm, nmada_hbm, nvada_hbm))]

            for cp in loads(0):
                cp.start()
            dmods = {}
            for i, (layer, c) in enumerate(chunks):
                if i + 1 < len(chunks):
                    for cp in loads(i + 1):
                        cp.start()
                for cp in loads(i):
                    cp.wait()
                if i >= 2:
                    for cp in stores(i - 2):
                        cp.wait()
                if layer not in dmods:
                    dmods[layer] = dmod_of(layer)
                act = act_ref[pl.ds(c * ADA_CHUNK, ADA_CHUNK), :]
                g = act[:, 0:1] * dmods[layer][0:1, :]
                for b in range(1, N_DEV):
                    g = g + act[:, b:b + 1] * dmods[layer][b:b + 1, :]
                slot = i % 2
                delta, new_m, new_v = _adamw_math(ada_in[slot, 0], g, ada_in[slot, 1], ada_in[slot, 2])
                ada_out[slot, 0] = g
                ada_out[slot, 1] = delta
                ada_out[slot, 2] = new_m
                ada_out[slot, 3] = new_v
                for cp in stores(i):
                    cp.start()
            for i in (len(chunks) - 2, len(chunks) - 1):
                for cp in stores(i):
                    cp.wait()

            total = dmod_all[0]
            for b in range(1, N_DEV):
                total = total + dmod_all[b]
            width = total.shape[1]
            for layer in range(DEPTH):
                for q in range(4):
                    gb_ref[layer:layer + 1, q * width:(q + 1) * width] = total[4 * layer + q:4 * layer + q + 1, :]
            db_ref[...], nmb_ref[...], nvb_ref[...] = _adamw_math(bada_ref[...], gb_ref[...], mbada_ref[...],
                                                                  vbada_ref[...])

        order = (ACROSS_BOTH, ACROSS_X, ACROSS_Y, None)
        chips = [_ChipReduceScatter._chip(r) for r in order]
        loads = [pltpu.make_async_copy(s, d, load_sems.at[n]) for n, (s, d) in enumerate(
            ((cat_hbm, cat_v), (dy_hbm, dy_v), (h_hbm, h_v)))]
        loads += [pltpu.make_async_copy(dproj_hbm.at[:, pl.ds(pl.multiple_of(chip * 2 * shard_in, 2 * shard_in),
                                                             2 * shard_in)], dproj_v.at[n], load_sems.at[3 + n])
                  for n, chip in enumerate(chips)]
        for cp in loads:
            cp.start()
        arrays = [dict(part=part_in, out=gwin_ref, staged=False, sib=bufs[0], snd=bufs[1], rcv=bufs[2], relay=bufs[3]),
                  dict(part=part_out, out=gwout_ref, staged=False, sib=bufs[4], snd=bufs[5], rcv=bufs[6],
                       relay=bufs[7]),
                  dict(part=small_hbm, out=own_small, staged=True, stage=bufs[8], sib=bufs[9], snd=bufs[10],
                       rcv=bufs[11], relay=bufs[12])]
        scatter = _ChipReduceScatter(arrays, *rs_sems)
        scatter.start([SMALL])
        dmod_all[mine] = dmod_ref[...]
        dmod_copies = _direct_exchange(lambda p: dmod_ref, lambda m: dmod_all.at[_index(m)], m_send, m_recv)
        loss_src[...] = jnp.full(loss_src.shape, (0.5 / D_MODEL) * jnp.sum(lanes_ref[...]), F32)
        loss_all[mine] = loss_src[...]
        loss_copies = _direct_exchange(lambda p: loss_src, lambda m: loss_all.at[_index(m)], s_send, s_recv)

        loads[0].wait()
        loads[1].wait()
        for blk in range(2):
            res = _dot_tn(cat_v[:, blk * 512:(blk + 1) * 512], dy_v[...]).astype(BF16)
            for s in range(4):
                part_out[2 * blk + s // 2, s % 2] = res[s * shard_out:(s + 1) * shard_out]
        scatter.start([W_OUT])
        scatter.exchange([SMALL])

        gather = _TwoLevelGather(stot_ref, g_send, g_recv)
        loads[2].wait()
        for n, chip in enumerate(chips):
            loads[3 + n].wait()
            res = _dot_tn(dproj_v[n], h_v[...]).astype(BF16)
            part_in[chip, 0] = res[:shard_in]
            part_in[chip, 1] = res[shard_in:]
            scatter.start([W_IN], chips=[chip])
            if n == 0:
                scatter.exchange([W_OUT])
                scatter.fold([SMALL])
            if n == 1:
                scatter.send_far([W_IN])
                scatter.fold([W_OUT])
                scatter.finish([SMALL])
                stot_ref[mine] = own_small[...]
                gather.send_mine()
            if n == 2:
                scatter.send_near(ACROSS_X, [W_IN])
                gather.relay()
            if n == 3:
                scatter.send_near(ACROSS_Y, [W_IN])
        scatter.fold([W_IN])
        scatter.finish([W_OUT])
        _wait_direct(dmod_copies)
        update_ada()
        gather.pass_on()
        gather.wait_rest()
        _wait_direct(loss_copies)
        total = loss_all[0]
        for j in range(1, N_DEV):
            total = total + loss_all[j]
        loss_ref[...] = total
        scatter.finish([W_IN])
        gather.wait_sends()
        scatter.wait_sends()

    buffers = _ChipReduceScatter.buffers
    comm_scratch = (buffers(shard_in, D_MODEL, BF16, staged=False) + buffers(shard_out, D_MODEL, BF16, staged=False)
                    + buffers(shard_small, 128, F32))
    comm_scratch += [pltpu.SemaphoreType.DMA((7,))] + _ChipReduceScatter.semaphores(3)
    comm_scratch += [pltpu.SemaphoreType.DMA((n,)) for n in (7, 7, GATHER_SEMS, GATHER_SEMS, 7, 7)]
    comm_scratch += [pltpu.SemaphoreType.DMA((2, 3)), pltpu.SemaphoreType.DMA((2, 4))]
    return pl.pallas_call(
        body,
        name="grad_tail",
        in_specs=[ANY] * 5 + [VMEM, VMEM] + [ANY] * 3 + [VMEM] * 4,
        out_specs=[VMEM] * 4 + [ANY] * 4 + [VMEM] * 4,
        out_shape=[jax.ShapeDtypeStruct((shard_in, D_MODEL), F32), jax.ShapeDtypeStruct((shard_out, D_MODEL), F32),
                   jax.ShapeDtypeStruct((N_DEV, shard_small, 128), F32), jax.ShapeDtypeStruct((8, 128), F32)]
                  + [jax.ShapeDtypeStruct(w_ada.shape, F32)] * 4 + [jax.ShapeDtypeStruct(b_ada.shape, F32)] * 4,
        scratch_shapes=[pltpu.VMEM((4, SEQ, 2 * shard_in), BF16), pltpu.VMEM(h.shape, BF16), pltpu.VMEM(cat.shape, BF16),
                        pltpu.VMEM(dy.shape, BF16), pltpu.VMEM((4, 2, shard_in, D_MODEL), BF16),
                        pltpu.VMEM((4, 2, shard_out, D_MODEL), BF16), pltpu.VMEM((shard_small, 128), F32),
                        pltpu.VMEM((8, 128), F32), pltpu.VMEM((N_DEV, 8, 128), F32),
                        pltpu.VMEM((N_DEV,) + dmod8.shape, F32), pltpu.VMEM((2, 3, ADA_CHUNK, cols), F32),
                        pltpu.VMEM((2, 4, ADA_CHUNK, cols), F32)] + comm_scratch,
        compiler_params=pltpu.CompilerParams(vmem_limit_bytes=VMEM_LIMIT),
    )(dproj, h, cat, dy, small, dmod8, loss_lanes, w_ada, m_ada, v_ada, act_t, b_ada, m_bada, v_bada)


SMALL_NAMES = ("w_pool", "w_sgu", "pool_scale", "sgu_ln_g", "sgu_ln_b", "b_sgu", "ln_g", "ln_b")
SMALL_ROWS = (512, 512, 4, 4, 4, 4, 8, 8)


def _adamw_small(g_packed, ws, ms, vs, name):
    n = len(SMALL_NAMES)

    def body(g_ref, *refs):
        w_refs, m_refs, v_refs = refs[:n], refs[n:2 * n], refs[2 * n:3 * n]
        outs = refs[3 * n:]

        def update(p, at, g):
            delta, new_m, new_v = _adamw_math(w_refs[p][at], g, m_refs[p][at], v_refs[p][at])
            outs[p][at] = g
            outs[n + p][at] = delta
            outs[2 * n + p][at] = new_m
            outs[3 * n + p][at] = new_v

        row = 0
        for p, r in enumerate(SMALL_ROWS):
            shape = ws[p].shape
            for layer in range(DEPTH):
                first = layer * PACK_ROWS + row
                if len(shape) == 4:
                    for k in range(shape[1]):
                        update(p, (layer, k), g_ref[first + k * shape[2]:first + (k + 1) * shape[2], :])
                elif len(shape) == 3:
                    update(p, (layer,), g_ref[first:first + r, :])
                else:
                    g = jnp.concatenate([g_ref[first + k:first + k + 1, :] for k in range(r)], axis=1)
                    update(p, (slice(layer, layer + 1), slice(None)), g)
            row += r

    res = pl.pallas_call(
        body,
        name=name,
        out_shape=[jax.ShapeDtypeStruct(w.shape, F32) for w in ws] * 4,
        compiler_params=pltpu.CompilerParams(vmem_limit_bytes=VMEM_LIMIT),
    )(g_packed, *ws, *ms, *vs)
    return res[:n], res[n:2 * n], res[2 * n:3 * n], res[3 * n:]


def kernel(x, c, w_ada, b_ada, w_in, w_pool, pool_scale, sgu_ln_g, sgu_ln_b, w_sgu, b_sgu, w_out, ln_g, ln_b, loss_target, m_w_ada, m_b_ada, m_w_in, m_w_pool, m_pool_scale, m_sgu_ln_g, m_sgu_ln_b, m_w_sgu, m_b_sgu, m_w_out, m_ln_g, m_ln_b, v_w_ada, v_b_ada, v_w_in, v_w_pool, v_pool_scale, v_sgu_ln_g, v_sgu_ln_b, v_w_sgu, v_b_sgu, v_w_out, v_ln_g, v_ln_b):
    small_w = dict(w_pool=w_pool, w_sgu=w_sgu, pool_scale=pool_scale, sgu_ln_g=sgu_ln_g, sgu_ln_b=sgu_ln_b,
                   b_sgu=b_sgu, ln_g=ln_g, ln_b=ln_b)
    small_m = dict(w_pool=m_w_pool, w_sgu=m_w_sgu, pool_scale=m_pool_scale, sgu_ln_g=m_sgu_ln_g,
                   sgu_ln_b=m_sgu_ln_b, b_sgu=m_b_sgu, ln_g=m_ln_g, ln_b=m_ln_b)
    small_v = dict(w_pool=v_w_pool, w_sgu=v_w_sgu, pool_scale=v_pool_scale, sgu_ln_g=v_sgu_ln_g,
                   sgu_ln_b=v_sgu_ln_b, b_sgu=v_b_sgu, ln_g=v_ln_g, ln_b=v_ln_b)

    wint_loc = jnp.transpose(w_in, (0, 2, 1)).astype(BF16)
    wout_loc = w_out.astype(BF16)
    small = (w_pool, pool_scale, sgu_ln_g, sgu_ln_b, w_sgu, jnp.transpose(b_sgu, (0, 2, 1)))
    (out0, y0, cdf0, proj0), (cur, y1, cdf1, proj1), gathered0, gathered1, act_all, mod = _forward(
        x[0], c, w_ada, b_ada, small, ln_g, ln_b, [wint_loc[0], wout_loc[0]], [wint_loc[1], wout_loc[1]])
    w_int, w_outf = [gathered0[0], gathered1[0]], [gathered0[1], gathered1[1]]
    acts = [(x[0], proj0, y0, cdf0), (out0, proj1, y1, cdf1)]

    shard_in, shard_out = D_IN // N_DEV, D_MODEL // N_DEV
    a, b = cur, loss_target[0]
    loss_lanes, carry, pending = None, (), []
    g_w_in_t, g_w_out = [None] * DEPTH, [None] * DEPTH
    for l in reversed(range(DEPTH)):
        dx, dproj, h, cat, dy, small_grads, dmod8, lanes, *shards = _layer_backward(
            l, a, b, *acts[l], mod, w_int[l], w_outf[l], small, ln_g, l == DEPTH - 1, f"layer_bwd_{l}",
            carry=carry, reduce=pending)
        if shards:
            g_w_in_t[l + 1], g_w_out[l + 1] = shards
        if l == DEPTH - 1:
            loss_lanes = lanes
        if l > 0:
            g_in, g_out = _grad_matmuls(dproj, h, cat, dy, f"grad_w_{l}")
            pending = [g_in.reshape(4, 2, shard_in, D_MODEL), g_out.reshape(4, 2, shard_out, D_MODEL)]
        carry = (small_grads, dmod8)
        a = b = dx
    grad_x = a[None]

    (g_w_in_t[0], g_w_out[0], small_tot, loss_tile, g_w_ada, d_w_ada, nm_w_ada, nv_w_ada,
     g_b_ada, d_b_ada, nm_b_ada, nv_b_ada) = _grad_tail(
        dproj, h, cat, dy, small_grads.reshape(4, 2, DEPTH * PACK_ROWS // N_DEV, 128), dmod8, loss_lanes,
        w_ada, m_w_ada, v_w_ada, jnp.transpose(act_all), b_ada, m_b_ada, v_b_ada)
    loss = loss_tile[0, 0]

    flat = lambda t: t.reshape(-1, t.shape[-1])
    to_t = lambda t: flat(jnp.transpose(t, (0, 2, 1)))
    from_t = lambda t: jnp.transpose(t.reshape(DEPTH, shard_in, D_MODEL), (0, 2, 1))
    g_w_in, d_w_in, nm_w_in, nv_w_in = [from_t(t) for t in _adamw(to_t(w_in), g_w_in_t, to_t(m_w_in), to_t(v_w_in),
                                                                  shard_in // 2, "adamw_w_in")]
    gwout, d_w_out, nm_w_out, nv_w_out = [t.reshape(w_out.shape) for t in _adamw(
        flat(w_out), g_w_out, flat(m_w_out), flat(v_w_out), shard_out, "adamw_w_out")]
    small_out = _adamw_small(small_tot.reshape(DEPTH * PACK_ROWS, 128), [small_w[n] for n in SMALL_NAMES],
                             [small_m[n] for n in SMALL_NAMES], [small_v[n] for n in SMALL_NAMES], "adamw_small")
    gs, ds, ms, vs = [dict(zip(SMALL_NAMES, group)) for group in small_out]

    def ordered(w_ada_, b_ada_, w_in_, small, w_out_):
        return (w_ada_, b_ada_, w_in_, small["w_pool"], small["pool_scale"], small["sgu_ln_g"], small["sgu_ln_b"],
                small["w_sgu"], small["b_sgu"], w_out_, small["ln_g"], small["ln_b"])

    return (loss, grad_x,
            *ordered(g_w_ada, g_b_ada, g_w_in, gs, gwout),
            *ordered(d_w_ada, d_b_ada, d_w_in, ds, d_w_out),
            *ordered(nm_w_ada, nm_b_ada, nm_w_in, ms, nm_w_out),
            *ordered(nv_w_ada, nv_b_ada, nv_w_in, vs, nv_w_out))
```

```python
import jax
import jax.numpy as jnp
from jax import lax
from jax.experimental import pallas as pl
from jax.experimental.pallas import tpu as pltpu

F32 = jnp.float32
BF16 = jnp.bfloat16

D_MODEL = 1024
SEQ = 2048
DEPTH = 2
D_POOL = 512
D_SGU = 512
D_IN = 2560
N_GROUPS = 4
GROUP = 128
N_HEADS = 4
HEAD = 128
CHUNK = 128
WINDOWS = (2, 4, 8, 16)
ALPHA = (2.0 * DEPTH) ** 0.25
LN_EPS = 1e-5
N_DEV = 8

ADAM_LR = 0.001
ADAM_B1 = 0.9
ADAM_B2 = 0.999
ADAM_EPS = 1e-08
ADAM_WD = 0.01
ADAM_STEP = 10

TM = 256
HALO = 16
N_TILES = SEQ // TM
VMEM_LIMIT = 60 * 1024 * 1024

ROW_WPOOL = 0
ROW_WSGU = 512
ROW_PSCALE = 1024
ROW_SLNG = 1028
ROW_SLNB = 1032
ROW_BSGU = 1036
ROW_LNG = 1040
ROW_LNB = 1048
PACK_ROWS = 1088
DMOD_COLS = DEPTH * 3 * D_MODEL // 8

SQRT_HALF = 0.7071067811865476
INV_SQRT_2PI = 0.3989422804014327


def _ln(x):
    mu = jnp.mean(x, axis=-1, keepdims=True)
    xc = x - mu
    var = jnp.mean(xc * xc, axis=-1, keepdims=True)
    rstd = lax.rsqrt(var + LN_EPS)
    return xc * rstd, rstd


def _ln_bwd(dxn, xn, rstd):
    m1 = jnp.mean(dxn, axis=-1, keepdims=True)
    m2 = jnp.mean(dxn * xn, axis=-1, keepdims=True)
    return rstd * (dxn - m1 - xn * m2)


def _normal_cdf(x):
    return 0.5 * (1.0 + lax.erf(x * SQRT_HALF))


def _gelu_parts(x, cdf, with_grad):
    if not with_grad:
        return x * cdf, None
    return x * cdf, cdf + x * (INV_SQRT_2PI * jnp.exp(-0.5 * x * x))


def _silu_parts(x):
    s = jax.nn.sigmoid(x)
    return x * s, s * (1.0 + x * (1.0 - s))


def _dot(a, b):
    return lax.dot_general(a, b, (((1,), (0,)), ((), ())), preferred_element_type=F32)


def _dot_nt(a, b):
    return lax.dot_general(a, b, (((1,), (1,)), ((), ())), preferred_element_type=F32)


def _dot_tn(a, b):
    return lax.dot_general(a, b, (((0,), (0,)), ((), ())), preferred_element_type=F32)


def _row_index(tile):
    return tile * TM + lax.broadcasted_iota(jnp.int32, (TM, 1), 0)


def _window_sums(ext, forward):
    n = TM + HALO
    cur = ext
    outs = []
    for g in range(N_GROUPS):
        step = 1 << g
        cur = cur + pltpu.roll(cur, step if forward else n - step, 0)
        rows = cur[HALO:, :GROUP] if forward else cur[:TM, :GROUP]
        outs.append(rows)
        cur = cur[:, GROUP:] if g + 1 < N_GROUPS else None
    return outs


def _inverse_counts(rows):
    return [1.0 / jnp.minimum(rows + 1, w).astype(F32) for w in WINDOWS]


def _tril_bf16(w):
    t = lax.broadcasted_iota(jnp.int32, (CHUNK, CHUNK), 0)
    s = lax.broadcasted_iota(jnp.int32, (CHUNK, CHUNK), 1)
    return jnp.where(t >= s, w, 0.0).astype(BF16)


class _MixWeights:
    def __init__(self, layer, wpool_ref, pscale_ref, slng_ref, slnb_ref, wsgu_ref, bsgut_ref):
        self.layer = layer
        self.wpool_ref, self.pscale_ref, self.slng_ref, self.slnb_ref = wpool_ref, pscale_ref, slng_ref, slnb_ref
        self.wsgu_ref, self.bsgut_ref = wsgu_ref, bsgut_ref

    def pool(self, g):
        return self.wpool_ref[self.layer, g].astype(BF16)

    def pool_scale(self, g):
        return self.pscale_ref[self.layer:self.layer + 1, g * GROUP:(g + 1) * GROUP]

    def ln_gain(self, h):
        return self.slng_ref[self.layer, h:h + 1, :]

    def ln_bias(self, h):
        return self.slnb_ref[self.layer, h:h + 1, :]

    def mix(self, h):
        return _tril_bf16(self.wsgu_ref[self.layer, h])

    def mix_bias(self, h):
        return self.bsgut_ref[self.layer, :, h:h + 1]


SMALL_SPECS = ((DEPTH, N_GROUPS, GROUP, GROUP), (DEPTH, D_POOL), (DEPTH, N_HEADS, HEAD), (DEPTH, N_HEADS, HEAD),
               (DEPTH, N_HEADS, CHUNK, CHUNK), (DEPTH, CHUNK, N_HEADS))


def _mix_forward(proj, halo, tile, w, cdf=None):
    keep = cdf is not None
    rows = _row_index(tile)
    inv_counts = _inverse_counts(rows)
    xa = proj[:, 0:D_POOL]
    ga = proj[:, D_POOL:2 * D_POOL]
    sums = _window_sums(jnp.concatenate([halo, xa], axis=0), True)
    ga_act, ga_grad = _silu_parts(ga)
    pooled, pw, ya = [], [], []
    for g in range(N_GROUPS):
        sl = slice(g * GROUP, (g + 1) * GROUP)
        p = (sums[g] * inv_counts[g] - xa[:, sl]).astype(BF16)
        q = _dot(p, w.pool(g))
        pooled.append(p)
        pw.append(q)
        ya.append(q * w.pool_scale(g) * ga_act[:, sl])

    u = proj[:, 2 * D_POOL:2 * D_POOL + D_SGU]
    v = proj[:, 2 * D_POOL + D_SGU:2 * D_POOL + 2 * D_SGU]
    gb = proj[:, 2 * D_POOL + 2 * D_SGU:]
    gb_act, gb_grad = _silu_parts(gb)
    if cdf is None:
        cdf = jnp.concatenate([_normal_cdf(u), _normal_cdf(v)], axis=1)
    u_act, u_grad = _gelu_parts(u, cdf[:, :D_SGU], keep)
    v_act, v_grad = _gelu_parts(v, cdf[:, D_SGU:], keep)
    vn, vrstd, vln, mixed, yb = [], [], [], [], []
    for h in range(N_HEADS):
        sl = slice(h * HEAD, (h + 1) * HEAD)
        n_h, r_h = _ln(v_act[:, sl])
        l_h = (n_h * w.ln_gain(h) + w.ln_bias(h)).astype(BF16)
        w_h = w.mix(h)
        bias = w.mix_bias(h)
        m_h = jnp.concatenate(
            [_dot(w_h, l_h[k * CHUNK:(k + 1) * CHUNK]) + bias for k in range(TM // CHUNK)], axis=0)
        vn.append(n_h)
        vrstd.append(r_h)
        vln.append(l_h)
        mixed.append(m_h)
        yb.append(u_act[:, sl] * m_h * gb_act[:, sl])
    cat = jnp.concatenate(ya + yb, axis=1)
    if not keep:
        return cat, cdf
    return cat, dict(inv_counts=inv_counts, ga_act=ga_act, ga_grad=ga_grad, pooled=pooled, pw=pw, u_grad=u_grad,
                     v_grad=v_grad, u_act=u_act, gb_act=gb_act, gb_grad=gb_grad, vn=vn, vrstd=vrstd, vln=vln,
                     mixed=mixed)


def _const_spec(shape):
    nd = len(shape)
    return pl.BlockSpec(shape, lambda i: (0,) * nd)


VEC_LNG, VEC_LNB, VEC_POOL, VEC_SGU, VEC_SHIFT, VEC_SCALE, VEC_GATE, VEC_LOSS = range(8)


def _layer_backward(layer, a, b, x, proj, y, cdf, mod, w_int, w_outf, small, ln_g, is_last, name, carry=(),
                    reduce=()):
    n_red, n_carry = len(reduce), len(carry)
    base = layer * PACK_ROWS

    def body(a_ref, b_ref, x_ref, proj_ref, prev_ref, y_ref, cdf_ref, mod_ref, wint_ref, wout_ref, wpool_ref,
             pscale_ref, slng_ref, slnb_ref, wsgu_ref, bsgut_ref, lng_ref, *rest):
        weights = _MixWeights(layer, wpool_ref, pscale_ref, slng_ref, slnb_ref, wsgu_ref, bsgut_ref)
        carry_refs, rest = rest[:n_carry], rest[n_carry:]
        part_refs, rest = rest[:n_red], rest[n_red:]
        dx_ref, dproj_ref, h_ref, cat_ref, dy_ref, small_ref, dmod_ref, loss_ref = rest[:8]
        shard_refs, rest = rest[8:8 + n_red], rest[8 + n_red:]
        vec_ref, dmix_ref, halo_ref = rest[:3]
        step = pl.program_id(0)
        tile = N_TILES - 1 - step

        def scatter():
            bufs, sems = rest[3:3 + 5 * n_red], rest[3 + 5 * n_red:]
            arrays = [dict(part=part_refs[n], out=shard_refs[n], staged=True, stage=bufs[5 * n], sib=bufs[5 * n + 1],
                           snd=bufs[5 * n + 2], rcv=bufs[5 * n + 3], relay=bufs[5 * n + 4]) for n in range(n_red)]
            return _ChipReduceScatter(arrays, *sems)

        @pl.when(step == 0)
        def _():
            small_ref[...] = jnp.zeros_like(small_ref)
            dmod_ref[...] = jnp.zeros_like(dmod_ref)
            vec_ref[...] = jnp.zeros_like(vec_ref)
            dmix_ref[...] = jnp.zeros_like(dmix_ref)
            halo_ref[...] = jnp.zeros_like(halo_ref)
            if n_red:
                scatter().start()

        if n_red:
            @pl.when(step == 1)
            def _():
                scatter().exchange()

            @pl.when(step == N_TILES // 2)
            def _():
                scatter().fold()

        def acc(row, lo, val):
            hi = lo + val.shape[1]
            vec_ref[row:row + 1, lo:hi] += jnp.sum(val, axis=0, keepdims=True)

        xt = x_ref[...]
        yt = y_ref[...]
        shift = mod_ref[layer:layer + 1, 0:D_MODEL]
        scale = mod_ref[layer:layer + 1, D_MODEL:2 * D_MODEL]
        gate = mod_ref[layer:layer + 1, 2 * D_MODEL:]
        ln_gain = lng_ref[layer:layer + 1, :]

        zn, zrstd = _ln(ALPHA * xt + gate * yt)
        if is_last:
            diff = a_ref[...] - b_ref[...]
            acc(VEC_LOSS, 0, diff * diff)
            dout = diff * (1.0 / D_MODEL)
        else:
            dout = a_ref[...]
        acc(VEC_LNG, 0, dout * zn)
        acc(VEC_LNB, 0, dout)
        dz = _ln_bwd(dout * ln_gain, zn, zrstd)
        acc(VEC_GATE, 0, dz * yt)
        dy = (dz * gate).astype(BF16)
        dy_ref[...] = dy
        dcat = _dot_nt(dy, wout_ref[...])

        proj = proj_ref[...]
        prev = jnp.where(tile > 0, prev_ref[...], 0.0)
        cat, k = _mix_forward(proj, prev, tile, weights, cdf_ref[...])
        cat_ref[...] = cat.astype(BF16)

        dga, dq = [], []
        for g in range(N_GROUPS):
            sl = slice(g * GROUP, (g + 1) * GROUP)
            pscale = weights.pool_scale(g)
            dya = dcat[:, sl]
            dyp = dya * k["ga_act"][:, sl]
            dga.append(dya * k["pw"][g] * pscale * k["ga_grad"][:, sl])
            acc(VEC_POOL, g * GROUP, dyp * k["pw"][g])
            dpw = (dyp * pscale).astype(BF16)
            rows = pl.ds(base + ROW_WPOOL + g * GROUP, GROUP)
            small_ref[rows, :] += _dot_tn(k["pooled"][g], dpw)
            dq.append(_dot_nt(dpw, weights.pool(g)))
        dpooled = jnp.concatenate(dq, axis=1)
        scaled = jnp.concatenate([dq[g] * k["inv_counts"][g] for g in range(N_GROUPS)], axis=1)
        sums = _window_sums(jnp.concatenate([scaled, halo_ref[...]], axis=0), False)
        halo_ref[...] = scaled[0:HALO]
        dxa = jnp.concatenate(sums, axis=1) - dpooled

        du, dv, dgb = [], [], []
        for h in range(N_HEADS):
            sl = slice(h * HEAD, (h + 1) * HEAD)
            dyb = dcat[:, D_POOL + h * HEAD:D_POOL + (h + 1) * HEAD]
            m_h = k["mixed"][h]
            ug = k["u_act"][:, sl] * dyb
            du.append(dyb * m_h * k["gb_act"][:, sl] * k["u_grad"][:, sl])
            dgb.append(ug * m_h * k["gb_grad"][:, sl])
            dmixed = ug * k["gb_act"][:, sl]
            dmixed_bf = dmixed.astype(BF16)
            w_h = weights.mix(h)
            dvln_parts = []
            dmix_sum = dmix_ref[h]
            wsgu_rows = pl.ds(base + ROW_WSGU + h * CHUNK, CHUNK)
            dws = small_ref[wsgu_rows, :]
            for c in range(TM // CHUNK):
                cs = slice(c * CHUNK, (c + 1) * CHUNK)
                dmix_sum = dmix_sum + dmixed[cs]
                dws = dws + _dot_nt(dmixed_bf[cs], k["vln"][h][cs])
                dvln_parts.append(_dot_tn(w_h, dmixed_bf[cs]))
            dmix_ref[h] = dmix_sum
            small_ref[wsgu_rows, :] = dws
            dvln = jnp.concatenate(dvln_parts, axis=0)
            acc(VEC_SGU, h * HEAD, dvln * k["vn"][h])
            acc(VEC_SGU, D_SGU + h * HEAD, dvln)
            dvv = _ln_bwd(dvln * weights.ln_gain(h), k["vn"][h], k["vrstd"][h])
            dv.append(dvv * k["v_grad"][:, sl])

        dproj = jnp.concatenate([dxa] + dga + du + dv + dgb, axis=1).astype(BF16)
        dproj_ref[...] = dproj
        dh = _dot(dproj, wint_ref[...])

        xn, xrstd = _ln(xt)
        h_ref[...] = (xn * (1.0 + scale) + shift).astype(BF16)
        acc(VEC_SCALE, 0, dh * xn)
        acc(VEC_SHIFT, 0, dh)
        dx_ref[...] = _ln_bwd(dh * (1.0 + scale), xn, xrstd) + ALPHA * dz

        @pl.when(step == N_TILES - 1)
        def _():
            def put(row0, vec_row, lo, n):
                for r in range(n):
                    small_ref[base + row0 + r:base + row0 + r + 1, :] = (
                        vec_ref[vec_row:vec_row + 1, lo + r * 128:lo + (r + 1) * 128])

            put(ROW_PSCALE, VEC_POOL, 0, 4)
            put(ROW_SLNG, VEC_SGU, 0, 4)
            put(ROW_SLNB, VEC_SGU, D_SGU, 4)
            put(ROW_LNG, VEC_LNG, 0, 8)
            put(ROW_LNB, VEC_LNB, 0, 8)
            ones = jnp.ones((8, HEAD), F32)
            t = lax.broadcasted_iota(jnp.int32, (CHUNK, CHUNK), 0)
            s = lax.broadcasted_iota(jnp.int32, (CHUNK, CHUNK), 1)
            for h in range(N_HEADS):
                bias_rows = lax.dot_general(ones, dmix_ref[h], (((1,), (1,)), ((), ())),
                                            preferred_element_type=F32, precision=lax.Precision.HIGHEST)
                small_ref[base + ROW_BSGU + h:base + ROW_BSGU + h + 1, :] = bias_rows[0:1]
                rows = pl.ds(base + ROW_WSGU + h * CHUNK, CHUNK)
                small_ref[rows, :] = jnp.where(t >= s, small_ref[rows, :], 0.0)
            pieces = ((0, VEC_SHIFT, 0, 768),
                      (1, VEC_SHIFT, 768, 256), (1, VEC_SCALE, 0, 512),
                      (2, VEC_SCALE, 512, 512), (2, VEC_GATE, 0, 256),
                      (3, VEC_GATE, 256, 768))
            filled = [0] * 4
            for q, vec_row, lo, n in pieces:
                row = 4 * layer + q
                dmod_ref[row:row + 1, filled[q]:filled[q] + n] = vec_ref[vec_row:vec_row + 1, lo:lo + n]
                filled[q] += n
            if n_carry:
                for other in range(layer + 1, DEPTH):
                    rows = pl.ds(other * PACK_ROWS, PACK_ROWS)
                    small_ref[rows, :] = carry_refs[0][rows, :]
                    dmod_ref[4 * other:4 * other + 4, :] = carry_refs[1][4 * other:4 * other + 4, :]
            loss_ref[...] = vec_ref[VEC_LOSS:VEC_LOSS + 1, :]
            if n_red:
                scatter().finish()
                scatter().wait_sends()

    rev = lambda w: pl.BlockSpec((TM, w), lambda i: (N_TILES - 1 - i, 0))
    prev_spec = pl.BlockSpec(
        (HALO, D_POOL), lambda i: (jnp.maximum((N_TILES - 1 - i) * (TM // HALO) - 1, 0), 0))
    comm_scratch = []
    for p in reduce:
        comm_scratch += _ChipReduceScatter.buffers(p.shape[2], p.shape[3], p.dtype)
    if n_red:
        comm_scratch += _ChipReduceScatter.semaphores(n_red)
    return pl.pallas_call(
        body,
        name=name,
        grid=(N_TILES,),
        in_specs=[rev(D_MODEL), rev(D_MODEL) if is_last else pl.BlockSpec((TM, D_MODEL), lambda i: (0, 0)),
                  rev(D_MODEL), rev(D_IN), prev_spec, rev(D_MODEL), rev(2 * D_SGU),
                  _const_spec((DEPTH, 3 * D_MODEL)), _const_spec((D_IN, D_MODEL)), _const_spec((D_MODEL, D_MODEL))]
                 + [_const_spec(s) for s in SMALL_SPECS] + [_const_spec((DEPTH, D_MODEL))]
                 + [_const_spec(c.shape) for c in carry] + [ANY] * n_red,
        out_specs=[rev(D_MODEL), rev(D_IN), rev(D_MODEL), rev(D_MODEL), rev(D_MODEL),
                   _const_spec((DEPTH * PACK_ROWS, 128)), _const_spec((8, DMOD_COLS)), _const_spec((1, D_MODEL))]
                  + [_const_spec(p.shape[2:]) for p in reduce],
        out_shape=[jax.ShapeDtypeStruct((SEQ, D_MODEL), F32), jax.ShapeDtypeStruct((SEQ, D_IN), BF16),
                   jax.ShapeDtypeStruct((SEQ, D_MODEL), BF16), jax.ShapeDtypeStruct((SEQ, D_MODEL), BF16),
                   jax.ShapeDtypeStruct((SEQ, D_MODEL), BF16), jax.ShapeDtypeStruct((DEPTH * PACK_ROWS, 128), F32),
                   jax.ShapeDtypeStruct((8, DMOD_COLS), F32), jax.ShapeDtypeStruct((1, D_MODEL), F32)]
                  + [jax.ShapeDtypeStruct(p.shape[2:], F32) for p in reduce],
        scratch_shapes=[pltpu.VMEM((8, D_MODEL), F32), pltpu.VMEM((N_HEADS, CHUNK, HEAD), F32),
                        pltpu.VMEM((HALO, D_POOL), F32)] + comm_scratch,
        compiler_params=pltpu.CompilerParams(dimension_semantics=("arbitrary",), vmem_limit_bytes=VMEM_LIMIT),
    )(a, b, x, proj, proj, y, cdf, mod, w_int, w_outf, *small, ln_g, *carry, *reduce)


def _grad_matmuls(dproj, h, cat, dy, name):
    in_cols, out_cols = D_IN // 4, D_MODEL // 2
    in_steps = D_IN // in_cols

    def body(dproj_ref, h_ref, cat_ref, dy_ref, gin_ref, gout_ref):
        step = pl.program_id(0)

        @pl.when(step < in_steps)
        def _():
            gin_ref[...] = _dot_tn(dproj_ref[...], h_ref[...]).astype(BF16)

        @pl.when(step >= in_steps)
        def _():
            gout_ref[...] = _dot_tn(cat_ref[...], dy_ref[...]).astype(BF16)

    in_block = lambda j: jnp.minimum(j, in_steps - 1)
    out_block = lambda j: jnp.maximum(j - in_steps, 0)
    return pl.pallas_call(
        body,
        name=name,
        grid=(in_steps + D_MODEL // out_cols,),
        in_specs=[pl.BlockSpec((SEQ, in_cols), lambda j: (0, in_block(j))), _const_spec((SEQ, D_MODEL)),
                  pl.BlockSpec((SEQ, out_cols), lambda j: (0, out_block(j))), _const_spec((SEQ, D_MODEL))],
        out_specs=[pl.BlockSpec((in_cols, D_MODEL), lambda j: (in_block(j), 0)),
                   pl.BlockSpec((out_cols, D_MODEL), lambda j: (out_block(j), 0))],
        out_shape=[jax.ShapeDtypeStruct((D_IN, D_MODEL), BF16), jax.ShapeDtypeStruct((D_MODEL, D_MODEL), BF16)],
        compiler_params=pltpu.CompilerParams(dimension_semantics=("arbitrary",), vmem_limit_bytes=VMEM_LIMIT),
    )(dproj, h, cat, dy)


def _adamw_math(w, g, m, v):
    m = ADAM_B1 * m + (1.0 - ADAM_B1) * g
    v = ADAM_B2 * v + (1.0 - ADAM_B2) * (g * g)
    m_hat = m / (1.0 - ADAM_B1 ** ADAM_STEP)
    v_hat = v / (1.0 - ADAM_B2 ** ADAM_STEP)
    delta = -ADAM_LR * (m_hat / (jnp.sqrt(v_hat) + ADAM_EPS) + ADAM_WD * w)
    return delta, m, v


def _adamw(w, grads, m, v, block_rows, name):
    rows, cols = grads[0].shape
    blocks = rows // block_rows

    def body(w_ref, m_ref, v_ref, *rest):
        g_refs, (g_ref, d_ref, nm_ref, nv_ref) = rest[:DEPTH], rest[DEPTH:]
        for layer in range(DEPTH):
            @pl.when(pl.program_id(0) == layer)
            def _():
                g = g_refs[layer][...]
                g_ref[...] = g
                d_ref[...], nm_ref[...], nv_ref[...] = _adamw_math(w_ref[...], g, m_ref[...], v_ref[...])

    def grad_spec(layer):
        return pl.BlockSpec((block_rows, cols),
                            lambda l, i: (jnp.where(l == layer, i, jnp.where(l < layer, 0, blocks - 1)), 0))

    spec = pl.BlockSpec((block_rows, cols), lambda l, i: (l * blocks + i, 0))
    return pl.pallas_call(
        body,
        name=name,
        grid=(DEPTH, blocks),
        in_specs=[spec] * 3 + [grad_spec(layer) for layer in range(DEPTH)],
        out_specs=[spec] * 4,
        out_shape=[jax.ShapeDtypeStruct(w.shape, F32)] * 4,
        compiler_params=pltpu.CompilerParams(dimension_semantics=("arbitrary", "arbitrary"),
                                             vmem_limit_bytes=VMEM_LIMIT),
    )(w, m, v, *grads)


MESH = pl.DeviceIdType.MESH
SIBLING = 1
ANY = pl.BlockSpec(memory_space=pl.ANY)
VMEM = pl.BlockSpec(memory_space=pltpu.VMEM)


def _me():
    return lax.axis_index("x"), lax.axis_index("y"), lax.axis_index("c")


def _peer(r):
    x, y, c = _me()
    return (1 - x if r & 4 else x, 1 - y if r & 2 else y, 1 - c if r & 1 else c)


def _index(dev):
    return 4 * dev[0] + 2 * dev[1] + dev[2]


def _remote(src, dst, send_sem, recv_sem, dev):
    return pltpu.make_async_remote_copy(src_ref=src, dst_ref=dst, send_sem=send_sem, recv_sem=recv_sem,
                                        device_id=dev, device_id_type=MESH)


ACROSS_X, ACROSS_Y, ACROSS_BOTH = 4, 2, 6
GATHER_SEMS = 11


class _TwoLevelGather:
    def __init__(self, out, send_sems, recv_sems, src=None):
        self.out, self.send_sems, self.recv_sems, self.src = out, send_sems, recv_sems, src
        self.rows = (out.shape[0] // N_DEV) if len(out.shape) == 2 else out.shape[1]
        self.half = self.rows // 2

    def _slot(self, block):
        if len(self.out.shape) == 2:
            return self.out.at[pl.ds(pl.multiple_of(_index(block) * self.rows, self.rows), self.rows)]
        return self.out.at[_index(block)]

    def _copy(self, k, block, part, to, src=None):
        slot = self._slot(block)
        if part is not None:
            rows = pl.ds(part * self.half, self.half)
            slot = slot.at[rows]
            src = None if src is None else src.at[rows]
        return _remote(slot if src is None else src, slot, self.send_sems.at[k], self.recv_sems.at[k], to)

    def _mine(self):
        me = _me()
        src = self._slot(me) if self.src is None else self.src
        x, y = _peer(ACROSS_X), _peer(ACROSS_Y)
        return [self._copy(1, me, 0, x, src), self._copy(3, me, 1, y, src), self._copy(0, me, None, _peer(SIBLING), src),
                self._copy(2, me, 1, x, src), self._copy(4, me, 0, y, src)]

    def _relayed(self):
        return [self._copy(5, _peer(ACROSS_X), 0, _peer(ACROSS_Y)), self._copy(6, _peer(ACROSS_Y), 1, _peer(ACROSS_X))]

    def _passed(self):
        sib, far = _peer(SIBLING), _peer(ACROSS_BOTH)
        return [self._copy(7, _peer(ACROSS_X), None, sib), self._copy(8, _peer(ACROSS_Y), None, sib),
                self._copy(9, far, 0, sib), self._copy(10, far, 1, sib)]

    def _arrival(self, k, r, part):
        return self._copy(k, _peer(r), part, _me())

    def send_first(self):
        for cp in self._mine()[:3]:
            cp.start()

    def send_second(self):
        for cp in self._mine()[3:]:
            cp.start()

    def send_mine(self):
        self.send_first()
        self.send_second()

    def relay(self):
        relayed = self._relayed()
        self._arrival(1, ACROSS_X, 0).wait_recv()
        relayed[0].start()
        self._arrival(3, ACROSS_Y, 1).wait_recv()
        relayed[1].start()

    def pass_near(self):
        passed = self._passed()
        self._arrival(2, ACROSS_X, 1).wait_recv()
        passed[0].start()
        self._arrival(4, ACROSS_Y, 0).wait_recv()
        passed[1].start()

    def pass_far(self):
        passed = self._passed()
        self._arrival(5, ACROSS_BOTH, 0).wait_recv()
        passed[2].start()
        self._arrival(6, ACROSS_BOTH, 1).wait_recv()
        passed[3].start()

    def pass_on(self):
        self.pass_near()
        self.pass_far()

    def wait_sibling(self):
        self._arrival(0, SIBLING, None).wait_recv()

    def wait_passed(self, r):
        if r == ACROSS_BOTH:
            self._arrival(9, r ^ SIBLING, 0).wait_recv()
            self._arrival(10, r ^ SIBLING, 1).wait_recv()
        else:
            self._arrival(7 if r == ACROSS_X else 8, r ^ SIBLING, None).wait_recv()

    def wait_rest(self):
        self.wait_sibling()
        for r in (ACROSS_X, ACROSS_Y, ACROSS_BOTH):
            self.wait_passed(r)

    def wait_sends(self):
        for cp in self._mine() + self._relayed() + self._passed():
            cp.wait_send()


class _ChipReduceScatter:
    SLOTS = 6

    def __init__(self, arrays, l_sem, d_send, d_recv, i_send, i_recv):
        self.arrays = arrays
        self.l_sem, self.d_send, self.d_recv, self.i_send, self.i_recv = l_sem, d_send, d_recv, i_send, i_recv

    @staticmethod
    def buffers(rows, cols, dtype, staged=True):
        stage = [pltpu.VMEM((4, rows, cols), dtype)] if staged else []
        return stage + [pltpu.VMEM((4, rows, cols), dtype), pltpu.VMEM((3, rows, cols), dtype),
                        pltpu.VMEM((2, rows, cols), dtype), pltpu.VMEM((2, rows // 2, cols), dtype)]

    @classmethod
    def semaphores(cls, n):
        return [pltpu.SemaphoreType.DMA((n,)), pltpu.SemaphoreType.DMA((n, 4)), pltpu.SemaphoreType.DMA((n, 4)),
                pltpu.SemaphoreType.DMA((n, cls.SLOTS)), pltpu.SemaphoreType.DMA((n, cls.SLOTS))]

    def _pick(self, which):
        return list(enumerate(self.arrays)) if which is None else [(n, self.arrays[n]) for n in which]

    @staticmethod
    def _chip(r):
        dev = _me() if r is None else _peer(r)
        return 2 * dev[0] + dev[1]

    def _staging(self, which):
        c = _me()[2]
        return [pltpu.make_async_copy(a["part"].at[pl.ds(0, 4), c], a["stage"], self.l_sem.at[n])
                for n, a in self._pick(which) if a["staged"]]

    def _first(self, which, chip):
        other = 1 - _me()[2]
        return [_remote(a["part"].at[chip, other], a["sib"].at[chip], self.d_send.at[n, chip], self.d_recv.at[n, chip],
                        _peer(SIBLING)) for n, a in self._pick(which)]

    @staticmethod
    def _halves(a):
        half = a["rcv"].shape[1] // 2
        return pl.ds(0, half), pl.ds(half, half)

    def _hops(self, n, a):
        h0, h1 = self._halves(a)
        x, y = _peer(ACROSS_X), _peer(ACROSS_Y)
        snd, rcv, relay = a["snd"], a["rcv"], a["relay"]
        pairs = [(snd.at[2, h0], relay.at[0], x), (snd.at[2, h1], relay.at[1], y),
                 (snd.at[0, h0], rcv.at[0, h0], x), (snd.at[0, h1], rcv.at[0, h1], x),
                 (snd.at[1, h1], rcv.at[1, h1], y), (snd.at[1, h0], rcv.at[1, h0], y)]
        return [_remote(s, d, self.i_send.at[n, k], self.i_recv.at[n, k], to) for k, (s, d, to) in enumerate(pairs)]

    def _mine(self, a, chip, rows=None):
        src = a["stage"].at[chip] if a["staged"] else a["part"].at[chip, _me()[2]]
        mine, sib = (src[...], a["sib"][chip]) if rows is None else (src[rows, :], a["sib"][chip, rows, :])
        return mine.astype(F32) + sib.astype(F32)

    def start(self, which=None, chips=None):
        if chips is None:
            for cp in self._staging(which):
                cp.start()
        for chip in range(4) if chips is None else chips:
            for cp in self._first(which, chip):
                cp.start()

    def send_far(self, which=None):
        far = self._chip(ACROSS_BOTH)
        for cp in self._staging(which):
            cp.wait()
        for cp in self._first(which, far):
            cp.wait_recv()
        for n, a in self._pick(which):
            hops = self._hops(n, a)
            a["snd"][2] = self._mine(a, far).astype(a["snd"].dtype)
            hops[0].start()
            hops[1].start()

    def send_near(self, r, which=None):
        chip = self._chip(r)
        for cp in self._first(which, chip):
            cp.wait_recv()
        for n, a in self._pick(which):
            h0, h1 = self._halves(a)
            hops = self._hops(n, a)
            if r == ACROSS_X:
                a["snd"][0, h0, :] = self._mine(a, chip, h0).astype(a["snd"].dtype)
                hops[2].start()
            else:
                a["snd"][1, h1, :] = self._mine(a, chip, h1).astype(a["snd"].dtype)
                hops[4].start()

    def exchange(self, which=None):
        self.send_far(which)
        self.send_near(ACROSS_X, which)
        self.send_near(ACROSS_Y, which)

    def fold(self, which=None):
        across_x, across_y = self._chip(ACROSS_X), self._chip(ACROSS_Y)
        for n, a in self._pick(which):
            h0, h1 = self._halves(a)
            hops = self._hops(n, a)
            dtype = a["snd"].dtype
            hops[1].wait_recv()
            a["snd"][0, h1, :] = (self._mine(a, across_x, h1) + a["relay"][1].astype(F32)).astype(dtype)
            hops[3].start()
            hops[0].wait_recv()
            a["snd"][1, h0, :] = (self._mine(a, across_y, h0) + a["relay"][0].astype(F32)).astype(dtype)
            hops[5].start()

    def finish(self, which=None):
        home = self._chip(None)
        for cp in self._first(which, home):
            cp.wait_recv()
        for n, a in self._pick(which):
            hops = self._hops(n, a)
            a["out"][...] = self._mine(a, home)
            hops[2].wait_recv()
            hops[3].wait_recv()
            a["out"][...] += a["rcv"][0].astype(F32)
            hops[4].wait_recv()
            hops[5].wait_recv()
            a["out"][...] += a["rcv"][1].astype(F32)

    def wait_sends(self, which=None):
        for chip in range(4):
            for cp in self._first(which, chip):
                cp.wait_send()
        for n, a in self._pick(which):
            for cp in self._hops(n, a):
                cp.wait_send()


def _direct_exchange(src_of, dst_of, send_sems, recv_sems):
    me = _me()
    copies = [_remote(src_of(_peer(r)), dst_of(me), send_sems.at[r - 1], recv_sems.at[r - 1], _peer(r))
              for r in range(1, N_DEV)]
    for cp in copies:
        cp.start()
    return copies


def _wait_direct(copies):
    for cp in copies:
        cp.wait_recv()
    for cp in copies:
        cp.wait_send()


def _forward(x, c, w_ada, b_ada, small, ln_g, ln_b, mine, following):
    assert DEPTH == 2
    cols = w_ada.shape[2]
    shard = mine[0].shape[0]
    pair = 2 * shard

    def body(x_hbm, c_ref, wada_hbm, bada_ref, wpool_ref, pscale_ref, slng_ref, slnb_ref, wsgu_ref, bsgut_ref,
             lng_ref, lnb_ref, wint_hbm, wout_hbm, next_in_hbm, next_out_hbm,
             out0_ref, y0_ref, cdf0_ref, proj0_hbm, out1_ref, y1_ref, cdf1_ref, proj1_hbm,
             wint_keep, wout_keep, wint_next, wout_next, acts_ref, mod_ref,
             wint_v, wout_v, h_buf, proj_blk, proj_tile, halo_ref, x_ref, w_send, w_recv, w_local, p_sems,
             t_sems, in_sems, act_all, act_src, part, mod_recv, wada_ref, a_send, a_recv, m_send, m_recv,
             n_send, n_recv, n_local, f_sems):
        step = pl.program_id(0)
        chip_of = lambda dev: 2 * dev[0] + dev[1]

        def hosted():
            return [_TwoLevelGather(out, n_send.at[n], n_recv.at[n], src=src)
                    for n, (out, src) in enumerate(((wint_next, next_in_hbm), (wout_next, next_out_hbm)))]

        def hosted_own():
            me = _me()
            return [pltpu.make_async_copy(g.src, g._slot(me), n_local.at[n]) for n, g in enumerate(hosted())]

        def gathers():
            return (_TwoLevelGather(wint_v, w_send.at[0], w_recv.at[0], src=wint_hbm),
                    _TwoLevelGather(wout_v, w_send.at[1], w_recv.at[1], src=wout_hbm))

        def keeps():
            return [pltpu.make_async_copy(wint_v, wint_keep, w_local.at[2]),
                    pltpu.make_async_copy(wout_v, wout_keep, w_local.at[3])]

        def tile_read(t):
            slot = t % 2
            return pltpu.make_async_copy(proj0_hbm.at[pl.ds(pl.multiple_of(t * TM, TM), TM)], proj_tile.at[slot],
                                         t_sems.at[slot])

        def first_layer_start():
            writes = []

            def project(n, dev):
                first = pl.multiple_of(chip_of(dev) * pair, pair)
                if n >= 2:
                    writes[n - 2].wait()

                @pl.loop(0, N_TILES)
                def _(t):
                    rows = pl.ds(pl.multiple_of(t * TM, TM), TM)
                    proj_blk[n % 2, rows, :] = _dot_nt(h_buf[rows, :], wint_v[pl.ds(first, pair), :])

                cp = pltpu.make_async_copy(proj_blk.at[n % 2], proj0_hbm.at[:, pl.ds(first, pair)], p_sems.at[n % 2])
                cp.start()
                writes.append(cp)

            me = _me()
            halo_ref[...] = jnp.zeros_like(halo_ref)
            gather_in, gather_out = gathers()
            own_in = pltpu.make_async_copy(wint_hbm, gather_in._slot(me), w_local.at[0])
            own_out = pltpu.make_async_copy(wout_hbm, gather_out._slot(me), w_local.at[1])
            x_load = pltpu.make_async_copy(x_hbm, x_ref, in_sems.at[0])
            x_load.start()
            wada_load = pltpu.make_async_copy(wada_hbm, wada_ref, in_sems.at[1])
            wada_load.start()
            mine_index = _index(me)
            cval = c_ref[...]
            act_src[...] = jnp.zeros_like(act_src)
            act_src[0:1, :] = cval * jax.nn.sigmoid(cval)
            act_all[mine_index] = act_src[...]
            act_copies = _direct_exchange(lambda p: act_src, lambda m: act_all.at[_index(m)], a_send, a_recv)

            own_in.start()
            own_out.start()
            gather_in.send_first()

            _wait_direct(act_copies)
            acts = jnp.concatenate([act_all[j, 0:1, :] for j in range(N_DEV)], axis=0)
            acts_ref[...] = acts
            part[...] = jnp.zeros_like(part)
            wada_load.wait()
            for l in range(DEPTH):
                res = lax.dot_general(acts, wada_ref[l], (((1,), (0,)), ((), ())), preferred_element_type=F32,
                                      precision=lax.Precision.HIGHEST)
                for b in range(N_DEV):
                    part[b, l:l + 1, :] = res[b:b + 1, :]
            mod_recv[mine_index] = part[mine_index]
            mod_copies = _direct_exchange(lambda p: part.at[_index(p)], lambda m: mod_recv.at[_index(m)],
                                          m_send, m_recv)
            gather_in.send_second()
            gather_out.send_mine()

            _wait_direct(mod_copies)
            for l in range(DEPTH):
                for j in range(N_DEV):
                    sl = slice(j * cols, (j + 1) * cols)
                    mod_ref[l:l + 1, sl] = mod_recv[j, l:l + 1, :] + bada_ref[l:l + 1, sl]
            x_load.wait()
            shift = mod_ref[0:1, 0:D_MODEL]
            scale = mod_ref[0:1, D_MODEL:2 * D_MODEL]

            @pl.loop(0, N_TILES)
            def _(t):
                rows = pl.ds(pl.multiple_of(t * TM, TM), TM)
                xn, _ = _ln(x_ref[rows, :])
                h_buf[rows, :] = (xn * (1.0 + scale) + shift).astype(BF16)

            gather_in.relay()
            own_in.wait()
            gather_in.wait_sibling()
            project(0, me)
            gather_in.pass_near()
            gather_in.wait_passed(ACROSS_X)
            project(1, _peer(ACROSS_X))
            gather_out.relay()
            for cp in hosted_own():
                cp.start()
            for g in hosted():
                g.send_mine()
            gather_in.wait_passed(ACROSS_Y)
            project(2, _peer(ACROSS_Y))
            gather_in.pass_far()
            gather_in.wait_passed(ACROSS_BOTH)
            project(3, _peer(ACROSS_BOTH))

            gather_out.pass_on()
            gather_out.wait_rest()
            own_out.wait()
            for cp in keeps():
                cp.start()
            writes[2].wait()
            writes[3].wait()
            tile_read(0).start()

        def fetch_out():
            return pltpu.make_async_copy(wout_next, wout_v, f_sems.at[N_DEV])

        def fetch_in(relations):
            source = hosted()[0]
            return [pltpu.make_async_copy(source._slot(_peer(r)), gathers()[0]._slot(_peer(r)), f_sems.at[r])
                    for r in relations]

        def tile_write(t):
            slot = t % 2
            return pltpu.make_async_copy(proj_tile.at[slot], proj1_hbm.at[pl.ds(pl.multiple_of(t * TM, TM), TM)],
                                         t_sems.at[slot])

        def mix_and_close(layer, tile, rows, proj, out_ref, y_ref, cdf_ref):
            weights = _MixWeights(layer, wpool_ref, pscale_ref, slng_ref, slnb_ref, wsgu_ref, bsgut_ref)
            xt = x_ref[rows, :]
            gate = mod_ref[layer:layer + 1, 2 * D_MODEL:]
            cat, cdf_ref[...] = _mix_forward(proj, halo_ref[...], tile, weights)
            halo_ref[...] = proj[TM - HALO:, 0:D_POOL]
            if layer == 1:
                @pl.when(tile == 0)
                def _():
                    fetch_out().wait()
            y = _dot(cat.astype(BF16), wout_v[...])
            y_ref[...] = y
            zn, _ = _ln(ALPHA * xt + gate * y)
            out = zn * lng_ref[layer:layer + 1, :] + lnb_ref[layer:layer + 1, :]
            out_ref[...] = out
            if layer == 0:
                x_ref[rows, :] = out

        def first_layer_tile(tile):
            @pl.when(tile + 1 < N_TILES)
            def _():
                tile_read(tile + 1).start()

            tile_read(tile).wait()
            rows = pl.ds(pl.multiple_of(tile * TM, TM), TM)
            mix_and_close(0, tile, rows, proj_tile[tile % 2], out0_ref, y0_ref, cdf0_ref)

        def second_layer_tile(tile):
            rows = pl.ds(pl.multiple_of(tile * TM, TM), TM)
            shift = mod_ref[1:2, 0:D_MODEL]
            scale = mod_ref[1:2, D_MODEL:2 * D_MODEL]
            xn, _ = _ln(x_ref[rows, :])
            h = (xn * (1.0 + scale) + shift).astype(BF16)

            @pl.when(tile >= 2)
            def _():
                tile_write(tile - 2).wait()

            proj_tile[tile % 2] = _dot_nt(h, wint_v[...])
            tile_write(tile).start()
            mix_and_close(1, tile, rows, proj_tile[tile % 2], out1_ref, y1_ref, cdf1_ref)

        early = [0, SIBLING, ACROSS_X, ACROSS_Y]
        late = [r ^ SIBLING for r in (ACROSS_X, ACROSS_Y, ACROSS_BOTH)]

        @pl.when(step < N_TILES)
        def _():
            @pl.when(step == 0)
            def _():
                first_layer_start()

            @pl.when(step == 1)
            def _():
                for g in hosted():
                    g.relay()

            @pl.when(step == N_TILES // 2)
            def _():
                for g in hosted():
                    g.pass_near()
                for g in gathers():
                    g.wait_sends()
                for cp in keeps() + hosted_own():
                    cp.wait()
                hosted()[0].wait_sibling()
                for cp in fetch_in(early):
                    cp.start()

            @pl.when(step == N_TILES - 2)
            def _():
                for g in hosted():
                    g.pass_far()
                for cp in fetch_in([ACROSS_BOTH]):
                    cp.start()

            @pl.when(step == N_TILES - 1)
            def _():
                for r in (ACROSS_X, ACROSS_Y, ACROSS_BOTH):
                    hosted()[0].wait_passed(r)
                for cp in fetch_in(late):
                    cp.start()

            first_layer_tile(step)

        @pl.when(step >= N_TILES)
        def _():
            @pl.when(step == N_TILES)
            def _():
                halo_ref[...] = jnp.zeros_like(halo_ref)
                hosted()[1].wait_rest()
                fetch_out().start()
                for cp in fetch_in(early + [ACROSS_BOTH] + late):
                    cp.wait()

            second_layer_tile(step - N_TILES)

            @pl.when(step == 2 * N_TILES - 1)
            def _():
                for g in hosted():
                    g.wait_sends()
                tile_write(N_TILES - 2).wait()
                tile_write(N_TILES - 1).wait()

    first = lambda w: pl.BlockSpec((TM, w), lambda i: (jnp.minimum(i, N_TILES - 1), 0))
    second = lambda w: pl.BlockSpec((TM, w), lambda i: (jnp.maximum(i - N_TILES, 0), 0))
    gather_sems = pltpu.SemaphoreType.DMA((2, GATHER_SEMS))
    seven = pltpu.SemaphoreType.DMA((7,))
    per_layer = [jax.ShapeDtypeStruct((SEQ, D_MODEL), F32), jax.ShapeDtypeStruct((SEQ, D_MODEL), F32),
                 jax.ShapeDtypeStruct((SEQ, 2 * D_SGU), F32), jax.ShapeDtypeStruct((SEQ, D_IN), F32)]
    gathered = [jax.ShapeDtypeStruct((D_IN, D_MODEL), BF16), jax.ShapeDtypeStruct((D_MODEL, D_MODEL), BF16)]
    res = pl.pallas_call(
        body,
        name="layers_fwd",
        grid=(DEPTH * N_TILES,),
        in_specs=[ANY, _const_spec(c.shape), ANY, _const_spec(b_ada.shape)] + [_const_spec(s) for s in SMALL_SPECS]
                 + [_const_spec((DEPTH, D_MODEL)), _const_spec((DEPTH, D_MODEL))] + [ANY] * 4,
        out_specs=[first(D_MODEL), first(D_MODEL), first(2 * D_SGU), ANY,
                   second(D_MODEL), second(D_MODEL), second(2 * D_SGU), ANY] + [ANY] * 4
                  + [_const_spec((N_DEV, D_MODEL)), _const_spec((DEPTH, 3 * D_MODEL))],
        out_shape=per_layer * 2 + gathered * 2 + [jax.ShapeDtypeStruct((N_DEV, D_MODEL), F32),
                                                  jax.ShapeDtypeStruct((DEPTH, 3 * D_MODEL), F32)],
        scratch_shapes=[pltpu.VMEM((D_IN, D_MODEL), BF16), pltpu.VMEM((D_MODEL, D_MODEL), BF16),
                        pltpu.VMEM((SEQ, D_MODEL), BF16), pltpu.VMEM((2, SEQ, pair), F32),
                        pltpu.VMEM((2, TM, D_IN), F32), pltpu.VMEM((HALO, D_POOL), F32),
                        pltpu.VMEM((SEQ, D_MODEL), F32),
                        gather_sems, gather_sems, pltpu.SemaphoreType.DMA((4,)), pltpu.SemaphoreType.DMA((2,)),
                        pltpu.SemaphoreType.DMA((2,)), pltpu.SemaphoreType.DMA((2,)),
                        pltpu.VMEM((N_DEV, 8, D_MODEL), F32), pltpu.VMEM((8, D_MODEL), F32),
                        pltpu.VMEM((N_DEV, 8, cols), F32), pltpu.VMEM((N_DEV, 8, cols), F32),
                        pltpu.VMEM(w_ada.shape, F32), seven, seven, seven, seven,
                        gather_sems, gather_sems, pltpu.SemaphoreType.DMA((2,)),
                        pltpu.SemaphoreType.DMA((N_DEV + 1,))],
        compiler_params=pltpu.CompilerParams(dimension_semantics=("arbitrary",), vmem_limit_bytes=VMEM_LIMIT),
    )(x, c, w_ada, b_ada, *small, ln_g, ln_b, *mine, *following)
    return res[0:4], res[4:8], res[8:10], res[10:12], res[12], res[13]


ADA_CHUNK = 256


def _grad_tail(dproj, h, cat, dy, small, dmod8, loss_lanes, w_ada, m_ada, v_ada, act_t, b_ada, m_bada, v_bada):
    shard_in, shard_out, shard_small = D_IN // N_DEV, D_MODEL // N_DEV, small.shape[2]
    cols = w_ada.shape[2]
    W_IN, W_OUT, SMALL = 0, 1, 2

    def body(dproj_hbm, h_hbm, cat_hbm, dy_hbm, small_hbm, dmod_ref, lanes_ref, wada_hbm, mada_hbm, vada_hbm,
             act_ref, bada_ref, mbada_ref, vbada_ref,
             gwin_ref, gwout_ref, stot_ref, loss_ref, gada_hbm, dada_hbm, nmada_hbm, nvada_hbm,
             gb_ref, db_ref, nmb_ref, nvb_ref,
             dproj_v, h_v, cat_v, dy_v, part_in, part_out, own_small, loss_src, loss_all, dmod_all, ada_in, ada_out,
             *rest):
        bufs, rest = rest[:13], rest[13:]
        load_sems, rs_sems = rest[0], rest[1:6]
        m_send, m_recv, g_send, g_recv, s_send, s_recv, ada_lsem, ada_ssem = rest[6:]
        mine = _index(_me())

        def update_ada():
            upper = (mine % 2) == 1

            def dmod_of(layer):
                rows = []
                for b in range(N_DEV):
                    r = dmod_all[b, pl.ds(4 * layer + mine // 2, 1), :]
                    rows.append(jnp.where(upper, r[:, cols:], r[:, :cols]))
                return jnp.concatenate(rows, axis=0)

            chunks = [(layer, c) for layer in range(DEPTH) for c in range(D_MODEL // ADA_CHUNK)]

            def loads(i):
                layer, c = chunks[i]
                rows = pl.ds(c * ADA_CHUNK, ADA_CHUNK)
                return [pltpu.make_async_copy(src.at[layer, rows], ada_in.at[i % 2, k], ada_lsem.at[i % 2, k])
                        for k, src in enumerate((wada_hbm, mada_hbm, vada_hbm))]

            def stores(i):
                layer, c = chunks[i]
                rows = pl.ds(c * ADA_CHUNK, ADA_CHUNK)
                return [pltpu.make_async_copy(ada_out.at[i % 2, k], dst.at[layer, rows], ada_ssem.at[i % 2, k])
                        for k, dst in enumerate((gada_hbm, dada_hbm, nmada_hbm, nvada_hbm))]

            for cp in loads(0):
                cp.start()
            dmods = {}
            for i, (layer, c) in enumerate(chunks):
                if i + 1 < len(chunks):
                    for cp in loads(i + 1):
                        cp.start()
                for cp in loads(i):
                    cp.wait()
                if i >= 2:
                    for cp in stores(i - 2):
                        cp.wait()
                if layer not in dmods:
                    dmods[layer] = dmod_of(layer)
                act = act_ref[pl.ds(c * ADA_CHUNK, ADA_CHUNK), :]
                g = act[:, 0:1] * dmods[layer][0:1, :]
                for b in range(1, N_DEV):
                    g = g + act[:, b:b + 1] * dmods[layer][b:b + 1, :]
                slot = i % 2
                delta, new_m, new_v = _adamw_math(ada_in[slot, 0], g, ada_in[slot, 1], ada_in[slot, 2])
                ada_out[slot, 0] = g
                ada_out[slot, 1] = delta
                ada_out[slot, 2] = new_m
                ada_out[slot, 3] = new_v
                for cp in stores(i):
                    cp.start()
            for i in (len(chunks) - 2, len(chunks) - 1):
                for cp in stores(i):
                    cp.wait()

            total = dmod_all[0]
            for b in range(1, N_DEV):
                total = total + dmod_all[b]
            width = total.shape[1]
            for layer in range(DEPTH):
                for q in range(4):
                    gb_ref[layer:layer + 1, q * width:(q + 1) * width] = total[4 * layer + q:4 * layer + q + 1, :]
            db_ref[...], nmb_ref[...], nvb_ref[...] = _adamw_math(bada_ref[...], gb_ref[...], mbada_ref[...],
                                                                  vbada_ref[...])

        order = (ACROSS_BOTH, ACROSS_X, ACROSS_Y, None)
        chips = [_ChipReduceScatter._chip(r) for r in order]
        loads = [pltpu.make_async_copy(s, d, load_sems.at[n]) for n, (s, d) in enumerate(
            ((cat_hbm, cat_v), (dy_hbm, dy_v), (h_hbm, h_v)))]
        loads += [pltpu.make_async_copy(dproj_hbm.at[:, pl.ds(pl.multiple_of(chip * 2 * shard_in, 2 * shard_in),
                                                             2 * shard_in)], dproj_v.at[n], load_sems.at[3 + n])
                  for n, chip in enumerate(chips)]
        for cp in loads:
            cp.start()
        arrays = [dict(part=part_in, out=gwin_ref, staged=False, sib=bufs[0], snd=bufs[1], rcv=bufs[2], relay=bufs[3]),
                  dict(part=part_out, out=gwout_ref, staged=False, sib=bufs[4], snd=bufs[5], rcv=bufs[6],
                       relay=bufs[7]),
                  dict(part=small_hbm, out=own_small, staged=True, stage=bufs[8], sib=bufs[9], snd=bufs[10],
                       rcv=bufs[11], relay=bufs[12])]
        scatter = _ChipReduceScatter(arrays, *rs_sems)
        scatter.start([SMALL])
        dmod_all[mine] = dmod_ref[...]
        dmod_copies = _direct_exchange(lambda p: dmod_ref, lambda m: dmod_all.at[_index(m)], m_send, m_recv)
        loss_src[...] = jnp.full(loss_src.shape, (0.5 / D_MODEL) * jnp.sum(lanes_ref[...]), F32)
        loss_all[mine] = loss_src[...]
        loss_copies = _direct_exchange(lambda p: loss_src, lambda m: loss_all.at[_index(m)], s_send, s_recv)

        loads[0].wait()
        loads[1].wait()
        for blk in range(2):
            res = _dot_tn(cat_v[:, blk * 512:(blk + 1) * 512], dy_v[...]).astype(BF16)
            for s in range(4):
                part_out[2 * blk + s // 2, s % 2] = res[s * shard_out:(s + 1) * shard_out]
        scatter.start([W_OUT])
        scatter.exchange([SMALL])

        gather = _TwoLevelGather(stot_ref, g_send, g_recv)
        loads[2].wait()
        for n, chip in enumerate(chips):
            loads[3 + n].wait()
            res = _dot_tn(dproj_v[n], h_v[...]).astype(BF16)
            part_in[chip, 0] = res[:shard_in]
            part_in[chip, 1] = res[shard_in:]
            scatter.start([W_IN], chips=[chip])
            if n == 0:
                scatter.exchange([W_OUT])
                scatter.fold([SMALL])
            if n == 1:
                scatter.send_far([W_IN])
                scatter.fold([W_OUT])
                scatter.finish([SMALL])
                stot_ref[mine] = own_small[...]
                gather.send_mine()
            if n == 2:
                scatter.send_near(ACROSS_X, [W_IN])
                gather.relay()
            if n == 3:
                scatter.send_near(ACROSS_Y, [W_IN])
        scatter.fold([W_IN])
        scatter.finish([W_OUT])
        _wait_direct(dmod_copies)
        update_ada()
        gather.pass_on()
        gather.wait_rest()
        _wait_direct(loss_copies)
        total = loss_all[0]
        for j in range(1, N_DEV):
            total = total + loss_all[j]
        loss_ref[...] = total
        scatter.finish([W_IN])
        gather.wait_sends()
        scatter.wait_sends()

    buffers = _ChipReduceScatter.buffers
    comm_scratch = (buffers(shard_in, D_MODEL, BF16, staged=False) + buffers(shard_out, D_MODEL, BF16, staged=False)
                    + buffers(shard_small, 128, F32))
    comm_scratch += [pltpu.SemaphoreType.DMA((7,))] + _ChipReduceScatter.semaphores(3)
    comm_scratch += [pltpu.SemaphoreType.DMA((n,)) for n in (7, 7, GATHER_SEMS, GATHER_SEMS, 7, 7)]
    comm_scratch += [pltpu.SemaphoreType.DMA((2, 3)), pltpu.SemaphoreType.DMA((2, 4))]
    return pl.pallas_call(
        body,
        name="grad_tail",
        in_specs=[ANY] * 5 + [VMEM, VMEM] + [ANY] * 3 + [VMEM] * 4,
        out_specs=[VMEM] * 4 + [ANY] * 4 + [VMEM] * 4,
        out_shape=[jax.ShapeDtypeStruct((shard_in, D_MODEL), F32), jax.ShapeDtypeStruct((shard_out, D_MODEL), F32),
                   jax.ShapeDtypeStruct((N_DEV, shard_small, 128), F32), jax.ShapeDtypeStruct((8, 128), F32)]
                  + [jax.ShapeDtypeStruct(w_ada.shape, F32)] * 4 + [jax.ShapeDtypeStruct(b_ada.shape, F32)] * 4,
        scratch_shapes=[pltpu.VMEM((4, SEQ, 2 * shard_in), BF16), pltpu.VMEM(h.shape, BF16), pltpu.VMEM(cat.shape, BF16),
                        pltpu.VMEM(dy.shape, BF16), pltpu.VMEM((4, 2, shard_in, D_MODEL), BF16),
                        pltpu.VMEM((4, 2, shard_out, D_MODEL), BF16), pltpu.VMEM((shard_small, 128), F32),
                        pltpu.VMEM((8, 128), F32), pltpu.VMEM((N_DEV, 8, 128), F32),
                        pltpu.VMEM((N_DEV,) + dmod8.shape, F32), pltpu.VMEM((2, 3, ADA_CHUNK, cols), F32),
                        pltpu.VMEM((2, 4, ADA_CHUNK, cols), F32)] + comm_scratch,
        compiler_params=pltpu.CompilerParams(vmem_limit_bytes=VMEM_LIMIT),
    )(dproj, h, cat, dy, small, dmod8, loss_lanes, w_ada, m_ada, v_ada, act_t, b_ada, m_bada, v_bada)


SMALL_NAMES = ("w_pool", "w_sgu", "pool_scale", "sgu_ln_g", "sgu_ln_b", "b_sgu", "ln_g", "ln_b")
SMALL_ROWS = (512, 512, 4, 4, 4, 4, 8, 8)


def _adamw_small(g_packed, ws, ms, vs, name):
    n = len(SMALL_NAMES)

    def body(g_ref, *refs):
        w_refs, m_refs, v_refs = refs[:n], refs[n:2 * n], refs[2 * n:3 * n]
        outs = refs[3 * n:]

        def update(p, at, g):
            delta, new_m, new_v = _adamw_math(w_refs[p][at], g, m_refs[p][at], v_refs[p][at])
            outs[p][at] = g
            outs[n + p][at] = delta
            outs[2 * n + p][at] = new_m
            outs[3 * n + p][at] = new_v

        row = 0
        for p, r in enumerate(SMALL_ROWS):
            shape = ws[p].shape
            for layer in range(DEPTH):
                first = layer * PACK_ROWS + row
                if len(shape) == 4:
                    for k in range(shape[1]):
                        update(p, (layer, k), g_ref[first + k * shape[2]:first + (k + 1) * shape[2], :])
                elif len(shape) == 3:
                    update(p, (layer,), g_ref[first:first + r, :])
                else:
                    g = jnp.concatenate([g_ref[first + k:first + k + 1, :] for k in range(r)], axis=1)
                    update(p, (slice(layer, layer + 1), slice(None)), g)
            row += r

    res = pl.pallas_call(
        body,
        name=name,
        out_shape=[jax.ShapeDtypeStruct(w.shape, F32) for w in ws] * 4,
        compiler_params=pltpu.CompilerParams(vmem_limit_bytes=VMEM_LIMIT),
    )(g_packed, *ws, *ms, *vs)
    return res[:n], res[n:2 * n], res[2 * n:3 * n], res[3 * n:]


def kernel(x, c, w_ada, b_ada, w_in, w_pool, pool_scale, sgu_ln_g, sgu_ln_b, w_sgu, b_sgu, w_out, ln_g, ln_b, loss_target, m_w_ada, m_b_ada, m_w_in, m_w_pool, m_pool_scale, m_sgu_ln_g, m_sgu_ln_b, m_w_sgu, m_b_sgu, m_w_out, m_ln_g, m_ln_b, v_w_ada, v_b_ada, v_w_in, v_w_pool, v_pool_scale, v_sgu_ln_g, v_sgu_ln_b, v_w_sgu, v_b_sgu, v_w_out, v_ln_g, v_ln_b):
    small_w = dict(w_pool=w_pool, w_sgu=w_sgu, pool_scale=pool_scale, sgu_ln_g=sgu_ln_g, sgu_ln_b=sgu_ln_b,
                   b_sgu=b_sgu, ln_g=ln_g, ln_b=ln_b)
    small_m = dict(w_pool=m_w_pool, w_sgu=m_w_sgu, pool_scale=m_pool_scale, sgu_ln_g=m_sgu_ln_g,
                   sgu_ln_b=m_sgu_ln_b, b_sgu=m_b_sgu, ln_g=m_ln_g, ln_b=m_ln_b)
    small_v = dict(w_pool=v_w_pool, w_sgu=v_w_sgu, pool_scale=v_pool_scale, sgu_ln_g=v_sgu_ln_g,
                   sgu_ln_b=v_sgu_ln_b, b_sgu=v_b_sgu, ln_g=v_ln_g, ln_b=v_ln_b)

    wint_loc = jnp.transpose(w_in, (0, 2, 1)).astype(BF16)
    wout_loc = w_out.astype(BF16)
    small = (w_pool, pool_scale, sgu_ln_g, sgu_ln_b, w_sgu, jnp.transpose(b_sgu, (0, 2, 1)))
    (out0, y0, cdf0, proj0), (cur, y1, cdf1, proj1), gathered0, gathered1, act_all, mod = _forward(
        x[0], c, w_ada, b_ada, small, ln_g, ln_b, [wint_loc[0], wout_loc[0]], [wint_loc[1], wout_loc[1]])
    w_int, w_outf = [gathered0[0], gathered1[0]], [gathered0[1], gathered1[1]]
    acts = [(x[0], proj0, y0, cdf0), (out0, proj1, y1, cdf1)]

    shard_in, shard_out = D_IN // N_DEV, D_MODEL // N_DEV
    a, b = cur, loss_target[0]
    loss_lanes, carry, pending = None, (), []
    g_w_in_t, g_w_out = [None] * DEPTH, [None] * DEPTH
    for l in reversed(range(DEPTH)):
        dx, dproj, h, cat, dy, small_grads, dmod8, lanes, *shards = _layer_backward(
            l, a, b, *acts[l], mod, w_int[l], w_outf[l], small, ln_g, l == DEPTH - 1, f"layer_bwd_{l}",
            carry=carry, reduce=pending)
        if shards:
            g_w_in_t[l + 1], g_w_out[l + 1] = shards
        if l == DEPTH - 1:
            loss_lanes = lanes
        if l > 0:
            g_in, g_out = _grad_matmuls(dproj, h, cat, dy, f"grad_w_{l}")
            pending = [g_in.reshape(4, 2, shard_in, D_MODEL), g_out.reshape(4, 2, shard_out, D_MODEL)]
        carry = (small_grads, dmod8)
        a = b = dx
    grad_x = a[None]

    (g_w_in_t[0], g_w_out[0], small_tot, loss_tile, g_w_ada, d_w_ada, nm_w_ada, nv_w_ada,
     g_b_ada, d_b_ada, nm_b_ada, nv_b_ada) = _grad_tail(
        dproj, h, cat, dy, small_grads.reshape(4, 2, DEPTH * PACK_ROWS // N_DEV, 128), dmod8, loss_lanes,
        w_ada, m_w_ada, v_w_ada, jnp.transpose(act_all), b_ada, m_b_ada, v_b_ada)
    loss = loss_tile[0, 0]

    flat = lambda t: t.reshape(-1, t.shape[-1])
    to_t = lambda t: flat(jnp.transpose(t, (0, 2, 1)))
    from_t = lambda t: jnp.transpose(t.reshape(DEPTH, shard_in, D_MODEL), (0, 2, 1))
    g_w_in, d_w_in, nm_w_in, nv_w_in = [from_t(t) for t in _adamw(to_t(w_in), g_w_in_t, to_t(m_w_in), to_t(v_w_in),
                                                                  shard_in // 2, "adamw_w_in")]
    gwout, d_w_out, nm_w_out, nv_w_out = [t.reshape(w_out.shape) for t in _adamw(
        flat(w_out), g_w_out, flat(m_w_out), flat(v_w_out), shard_out, "adamw_w_out")]
    small_out = _adamw_small(small_tot.reshape(DEPTH * PACK_ROWS, 128), [small_w[n] for n in SMALL_NAMES],
                             [small_m[n] for n in SMALL_NAMES], [small_v[n] for n in SMALL_NAMES], "adamw_small")
    gs, ds, ms, vs = [dict(zip(SMALL_NAMES, group)) for group in small_out]

    def ordered(w_ada_, b_ada_, w_in_, small, w_out_):
        return (w_ada_, b_ada_, w_in_, small["w_pool"], small["pool_scale"], small["sgu_ln_g"], small["sgu_ln_b"],
                small["w_sgu"], small["b_sgu"], w_out_, small["ln_g"], small["ln_b"])

    return (loss, grad_x,
            *ordered(g_w_ada, g_b_ada, g_w_in, gs, gwout),
            *ordered(d_w_ada, d_b_ada, d_w_in, ds, d_w_out),
            *ordered(nm_w_ada, nm_b_ada, nm_w_in, ms, nm_w_out),
            *ordered(nv_w_ada, nv_b_ada, nv_w_in, vs, nv_w_out))
```

```python
import jax
import jax.numpy as jnp
from jax import lax
from jax.experimental import pallas as pl
from jax.experimental.pallas import tpu as pltpu

F32 = jnp.float32
BF16 = jnp.bfloat16

D_MODEL = 1024
SEQ = 2048
DEPTH = 2
D_POOL = 512
D_SGU = 512
D_IN = 2560
N_GROUPS = 4
GROUP = 128
N_HEADS = 4
HEAD = 128
CHUNK = 128
WINDOWS = (2, 4, 8, 16)
ALPHA = (2.0 * DEPTH) ** 0.25
LN_EPS = 1e-5
N_DEV = 8

ADAM_LR = 0.001
ADAM_B1 = 0.9
ADAM_B2 = 0.999
ADAM_EPS = 1e-08
ADAM_WD = 0.01
ADAM_STEP = 10

TM = 256
HALO = 16
N_TILES = SEQ // TM
VMEM_LIMIT = 60 * 1024 * 1024

ROW_WPOOL = 0
ROW_WSGU = 512
ROW_PSCALE = 1024
ROW_SLNG = 1028
ROW_SLNB = 1032
ROW_BSGU = 1036
ROW_LNG = 1040
ROW_LNB = 1048
PACK_ROWS = 1088
DMOD_COLS = DEPTH * 3 * D_MODEL // 8

SQRT_HALF = 0.7071067811865476
INV_SQRT_2PI = 0.3989422804014327


def _ln(x):
    mu = jnp.mean(x, axis=-1, keepdims=True)
    xc = x - mu
    var = jnp.mean(xc * xc, axis=-1, keepdims=True)
    rstd = lax.rsqrt(var + LN_EPS)
    return xc * rstd, rstd


def _ln_bwd(dxn, xn, rstd):
    m1 = jnp.mean(dxn, axis=-1, keepdims=True)
    m2 = jnp.mean(dxn * xn, axis=-1, keepdims=True)
    return rstd * (dxn - m1 - xn * m2)


def _normal_cdf(x):
    return 0.5 * (1.0 + lax.erf(x * SQRT_HALF))


def _gelu_parts(x, cdf, with_grad):
    if not with_grad:
        return x * cdf, None
    return x * cdf, cdf + x * (INV_SQRT_2PI * jnp.exp(-0.5 * x * x))


def _silu_parts(x):
    s = jax.nn.sigmoid(x)
    return x * s, s * (1.0 + x * (1.0 - s))


def _dot(a, b):
    return lax.dot_general(a, b, (((1,), (0,)), ((), ())), preferred_element_type=F32)


def _dot_nt(a, b):
    return lax.dot_general(a, b, (((1,), (1,)), ((), ())), preferred_element_type=F32)


def _dot_tn(a, b):
    return lax.dot_general(a, b, (((0,), (0,)), ((), ())), preferred_element_type=F32)


def _row_index(tile):
    return tile * TM + lax.broadcasted_iota(jnp.int32, (TM, 1), 0)


def _window_sums(ext, forward):
    n = TM + HALO
    cur = ext
    outs = []
    for g in range(N_GROUPS):
        step = 1 << g
        cur = cur + pltpu.roll(cur, step if forward else n - step, 0)
        rows = cur[HALO:, :GROUP] if forward else cur[:TM, :GROUP]
        outs.append(rows)
        cur = cur[:, GROUP:] if g + 1 < N_GROUPS else None
    return outs


def _inverse_counts(rows):
    return [1.0 / jnp.minimum(rows + 1, w).astype(F32) for w in WINDOWS]


def _tril_bf16(w):
    t = lax.broadcasted_iota(jnp.int32, (CHUNK, CHUNK), 0)
    s = lax.broadcasted_iota(jnp.int32, (CHUNK, CHUNK), 1)
    return jnp.where(t >= s, w, 0.0).astype(BF16)


class _MixWeights:
    def __init__(self, layer, wpool_ref, pscale_ref, slng_ref, slnb_ref, wsgu_ref, bsgut_ref):
        self.layer = layer
        self.wpool_ref, self.pscale_ref, self.slng_ref, self.slnb_ref = wpool_ref, pscale_ref, slng_ref, slnb_ref
        self.wsgu_ref, self.bsgut_ref = wsgu_ref, bsgut_ref

    def pool(self, g):
        return self.wpool_ref[self.layer, g].astype(BF16)

    def pool_scale(self, g):
        return self.pscale_ref[self.layer:self.layer + 1, g * GROUP:(g + 1) * GROUP]

    def ln_gain(self, h):
        return self.slng_ref[self.layer, h:h + 1, :]

    def ln_bias(self, h):
        return self.slnb_ref[self.layer, h:h + 1, :]

    def mix(self, h):
        return _tril_bf16(self.wsgu_ref[self.layer, h])

    def mix_bias(self, h):
        return self.bsgut_ref[self.layer, :, h:h + 1]


SMALL_SPECS = ((DEPTH, N_GROUPS, GROUP, GROUP), (DEPTH, D_POOL), (DEPTH, N_HEADS, HEAD), (DEPTH, N_HEADS, HEAD),
               (DEPTH, N_HEADS, CHUNK, CHUNK), (DEPTH, CHUNK, N_HEADS))


def _mix_forward(proj, halo, tile, w, cdf=None):
    keep = cdf is not None
    rows = _row_index(tile)
    inv_counts = _inverse_counts(rows)
    xa = proj[:, 0:D_POOL]
    ga = proj[:, D_POOL:2 * D_POOL]
    sums = _window_sums(jnp.concatenate([halo, xa], axis=0), True)
    ga_act, ga_grad = _silu_parts(ga)
    pooled, pw, ya = [], [], []
    for g in range(N_GROUPS):
        sl = slice(g * GROUP, (g + 1) * GROUP)
        p = (sums[g] * inv_counts[g] - xa[:, sl]).astype(BF16)
        q = _dot(p, w.pool(g))
        pooled.append(p)
        pw.append(q)
        ya.append(q * w.pool_scale(g) * ga_act[:, sl])

    u = proj[:, 2 * D_POOL:2 * D_POOL + D_SGU]
    v = proj[:, 2 * D_POOL + D_SGU:2 * D_POOL + 2 * D_SGU]
    gb = proj[:, 2 * D_POOL + 2 * D_SGU:]
    gb_act, gb_grad = _silu_parts(gb)
    if cdf is None:
        cdf = jnp.concatenate([_normal_cdf(u), _normal_cdf(v)], axis=1)
    u_act, u_grad = _gelu_parts(u, cdf[:, :D_SGU], keep)
    v_act, v_grad = _gelu_parts(v, cdf[:, D_SGU:], keep)
    vn, vrstd, vln, mixed, yb = [], [], [], [], []
    for h in range(N_HEADS):
        sl = slice(h * HEAD, (h + 1) * HEAD)
        n_h, r_h = _ln(v_act[:, sl])
        l_h = (n_h * w.ln_gain(h) + w.ln_bias(h)).astype(BF16)
        w_h = w.mix(h)
        bias = w.mix_bias(h)
        m_h = jnp.concatenate(
            [_dot(w_h, l_h[k * CHUNK:(k + 1) * CHUNK]) + bias for k in range(TM // CHUNK)], axis=0)
        vn.append(n_h)
        vrstd.append(r_h)
        vln.append(l_h)
        mixed.append(m_h)
        yb.append(u_act[:, sl] * m_h * gb_act[:, sl])
    cat = jnp.concatenate(ya + yb, axis=1)
    if not keep:
        return cat, cdf
    return cat, dict(inv_counts=inv_counts, ga_act=ga_act, ga_grad=ga_grad, pooled=pooled, pw=pw, u_grad=u_grad,
                     v_grad=v_grad, u_act=u_act, gb_act=gb_act, gb_grad=gb_grad, vn=vn, vrstd=vrstd, vln=vln,
                     mixed=mixed)


def _const_spec(shape):
    nd = len(shape)
    return pl.BlockSpec(shape, lambda i: (0,) * nd)


VEC_LNG, VEC_LNB, VEC_POOL, VEC_SGU, VEC_SHIFT, VEC_SCALE, VEC_GATE, VEC_LOSS = range(8)


def _layer_backward(layer, a, b, x, proj, y, cdf, mod, w_int, w_outf, small, ln_g, is_last, name, carry=(),
                    reduce=()):
    n_red, n_carry = len(reduce), len(carry)
    base = layer * PACK_ROWS

    def body(a_ref, b_ref, x_ref, proj_ref, prev_ref, y_ref, cdf_ref, mod_ref, wint_ref, wout_ref, wpool_ref,
             pscale_ref, slng_ref, slnb_ref, wsgu_ref, bsgut_ref, lng_ref, *rest):
        weights = _MixWeights(layer, wpool_ref, pscale_ref, slng_ref, slnb_ref, wsgu_ref, bsgut_ref)
        carry_refs, rest = rest[:n_carry], rest[n_carry:]
        part_refs, rest = rest[:n_red], rest[n_red:]
        dx_ref, dproj_ref, h_ref, cat_ref, dy_ref, small_ref, dmod_ref, loss_ref = rest[:8]
        shard_refs, rest = rest[8:8 + n_red], rest[8 + n_red:]
        vec_ref, dmix_ref, halo_ref = rest[:3]
        step = pl.program_id(0)
        tile = N_TILES - 1 - step

        def scatter():
            bufs, sems = rest[3:3 + 5 * n_red], rest[3 + 5 * n_red:]
            arrays = [dict(part=part_refs[n], out=shard_refs[n], staged=True, stage=bufs[5 * n], sib=bufs[5 * n + 1],
                           snd=bufs[5 * n + 2], rcv=bufs[5 * n + 3], relay=bufs[5 * n + 4]) for n in range(n_red)]
            return _ChipReduceScatter(arrays, *sems)

        @pl.when(step == 0)
        def _():
            small_ref[...] = jnp.zeros_like(small_ref)
            dmod_ref[...] = jnp.zeros_like(dmod_ref)
            vec_ref[...] = jnp.zeros_like(vec_ref)
            dmix_ref[...] = jnp.zeros_like(dmix_ref)
            halo_ref[...] = jnp.zeros_like(halo_ref)
            if n_red:
                scatter().start()

        if n_red:
            @pl.when(step == 1)
            def _():
                scatter().exchange()

            @pl.when(step == N_TILES // 2)
            def _():
                scatter().fold()

        def acc(row, lo, val):
            hi = lo + val.shape[1]
            vec_ref[row:row + 1, lo:hi] += jnp.sum(val, axis=0, keepdims=True)

        xt = x_ref[...]
        yt = y_ref[...]
        shift = mod_ref[layer:layer + 1, 0:D_MODEL]
        scale = mod_ref[layer:layer + 1, D_MODEL:2 * D_MODEL]
        gate = mod_ref[layer:layer + 1, 2 * D_MODEL:]
        ln_gain = lng_ref[layer:layer + 1, :]

        zn, zrstd = _ln(ALPHA * xt + gate * yt)
        if is_last:
            diff = a_ref[...] - b_ref[...]
            acc(VEC_LOSS, 0, diff * diff)
            dout = diff * (1.0 / D_MODEL)
        else:
            dout = a_ref[...]
        acc(VEC_LNG, 0, dout * zn)
        acc(VEC_LNB, 0, dout)
        dz = _ln_bwd(dout * ln_gain, zn, zrstd)
        acc(VEC_GATE, 0, dz * yt)
        dy = (dz * gate).astype(BF16)
        dy_ref[...] = dy
        dcat = _dot_nt(dy, wout_ref[...])

        proj = proj_ref[...]
        prev = jnp.where(tile > 0, prev_ref[...], 0.0)
        cat, k = _mix_forward(proj, prev, tile, weights, cdf_ref[...])
        cat_ref[...] = cat.astype(BF16)

        dga, dq = [], []
        for g in range(N_GROUPS):
            sl = slice(g * GROUP, (g + 1) * GROUP)
            pscale = weights.pool_scale(g)
            dya = dcat[:, sl]
            dyp = dya * k["ga_act"][:, sl]
            dga.append(dya * k["pw"][g] * pscale * k["ga_grad"][:, sl])
            acc(VEC_POOL, g * GROUP, dyp * k["pw"][g])
            dpw = (dyp * pscale).astype(BF16)
            rows = pl.ds(base + ROW_WPOOL + g * GROUP, GROUP)
            small_ref[rows, :] += _dot_tn(k["pooled"][g], dpw)
            dq.append(_dot_nt(dpw, weights.pool(g)))
        dpooled = jnp.concatenate(dq, axis=1)
        scaled = jnp.concatenate([dq[g] * k["inv_counts"][g] for g in range(N_GROUPS)], axis=1)
        sums = _window_sums(jnp.concatenate([scaled, halo_ref[...]], axis=0), False)
        halo_ref[...] = scaled[0:HALO]
        dxa = jnp.concatenate(sums, axis=1) - dpooled

        du, dv, dgb = [], [], []
        for h in range(N_HEADS):
            sl = slice(h * HEAD, (h + 1) * HEAD)
            dyb = dcat[:, D_POOL + h * HEAD:D_POOL + (h + 1) * HEAD]
            m_h = k["mixed"][h]
            ug = k["u_act"][:, sl] * dyb
            du.append(dyb * m_h * k["gb_act"][:, sl] * k["u_grad"][:, sl])
            dgb.append(ug * m_h * k["gb_grad"][:, sl])
            dmixed = ug * k["gb_act"][:, sl]
            dmixed_bf = dmixed.astype(BF16)
            w_h = weights.mix(h)
            dvln_parts = []
            dmix_sum = dmix_ref[h]
            wsgu_rows = pl.ds(base + ROW_WSGU + h * CHUNK, CHUNK)
            dws = small_ref[wsgu_rows, :]
            for c in range(TM // CHUNK):
                cs = slice(c * CHUNK, (c + 1) * CHUNK)
                dmix_sum = dmix_sum + dmixed[cs]
                dws = dws + _dot_nt(dmixed_bf[cs], k["vln"][h][cs])
                dvln_parts.append(_dot_tn(w_h, dmixed_bf[cs]))
            dmix_ref[h] = dmix_sum
            small_ref[wsgu_rows, :] = dws
            dvln = jnp.concatenate(dvln_parts, axis=0)
            acc(VEC_SGU, h * HEAD, dvln * k["vn"][h])
            acc(VEC_SGU, D_SGU + h * HEAD, dvln)
            dvv = _ln_bwd(dvln * weights.ln_gain(h), k["vn"][h], k["vrstd"][h])
            dv.append(dvv * k["v_grad"][:, sl])

        dproj = jnp.concatenate([dxa] + dga + du + dv + dgb, axis=1).astype(BF16)
        dproj_ref[...] = dproj
        dh = _dot(dproj, wint_ref[...])

        xn, xrstd = _ln(xt)
        h_ref[...] = (xn * (1.0 + scale) + shift).astype(BF16)
        acc(VEC_SCALE, 0, dh * xn)
        acc(VEC_SHIFT, 0, dh)
        dx_ref[...] = _ln_bwd(dh * (1.0 + scale), xn, xrstd) + ALPHA * dz

        @pl.when(step == N_TILES - 1)
        def _():
            def put(row0, vec_row, lo, n):
                for r in range(n):
                    small_ref[base + row0 + r:base + row0 + r + 1, :] = (
                        vec_ref[vec_row:vec_row + 1, lo + r * 128:lo + (r + 1) * 128])

            put(ROW_PSCALE, VEC_POOL, 0, 4)
            put(ROW_SLNG, VEC_SGU, 0, 4)
            put(ROW_SLNB, VEC_SGU, D_SGU, 4)
            put(ROW_LNG, VEC_LNG, 0, 8)
            put(ROW_LNB, VEC_LNB, 0, 8)
            ones = jnp.ones((8, HEAD), F32)
            t = lax.broadcasted_iota(jnp.int32, (CHUNK, CHUNK), 0)
            s = lax.broadcasted_iota(jnp.int32, (CHUNK, CHUNK), 1)
            for h in range(N_HEADS):
                bias_rows = lax.dot_general(ones, dmix_ref[h], (((1,), (1,)), ((), ())),
                                            preferred_element_type=F32, precision=lax.Precision.HIGHEST)
                small_ref[base + ROW_BSGU + h:base + ROW_BSGU + h + 1, :] = bias_rows[0:1]
                rows = pl.ds(base + ROW_WSGU + h * CHUNK, CHUNK)
                small_ref[rows, :] = jnp.where(t >= s, small_ref[rows, :], 0.0)
            pieces = ((0, VEC_SHIFT, 0, 768),
                      (1, VEC_SHIFT, 768, 256), (1, VEC_SCALE, 0, 512),
                      (2, VEC_SCALE, 512, 512), (2, VEC_GATE, 0, 256),
                      (3, VEC_GATE, 256, 768))
            filled = [0] * 4
            for q, vec_row, lo, n in pieces:
                row = 4 * layer + q
                dmod_ref[row:row + 1, filled[q]:filled[q] + n] = vec_ref[vec_row:vec_row + 1, lo:lo + n]
                filled[q] += n
            if n_carry:
                for other in range(layer + 1, DEPTH):
                    rows = pl.ds(other * PACK_ROWS, PACK_ROWS)
                    small_ref[rows, :] = carry_refs[0][rows, :]
                    dmod_ref[4 * other:4 * other + 4, :] = carry_refs[1][4 * other:4 * other + 4, :]
            loss_ref[...] = vec_ref[VEC_LOSS:VEC_LOSS + 1, :]
            if n_red:
                scatter().finish()
                scatter().wait_sends()

    rev = lambda w: pl.BlockSpec((TM, w), lambda i: (N_TILES - 1 - i, 0))
    prev_spec = pl.BlockSpec(
        (HALO, D_POOL), lambda i: (jnp.maximum((N_TILES - 1 - i) * (TM // HALO) - 1, 0), 0))
    comm_scratch = []
    for p in reduce:
        comm_scratch += _ChipReduceScatter.buffers(p.shape[2], p.shape[3], p.dtype)
    if n_red:
        comm_scratch += _ChipReduceScatter.semaphores(n_red)
    return pl.pallas_call(
        body,
        name=name,
        grid=(N_TILES,),
        in_specs=[rev(D_MODEL), rev(D_MODEL) if is_last else pl.BlockSpec((TM, D_MODEL), lambda i: (0, 0)),
                  rev(D_MODEL), rev(D_IN), prev_spec, rev(D_MODEL), rev(2 * D_SGU),
                  _const_spec((DEPTH, 3 * D_MODEL)), _const_spec((D_IN, D_MODEL)), _const_spec((D_MODEL, D_MODEL))]
                 + [_const_spec(s) for s in SMALL_SPECS] + [_const_spec((DEPTH, D_MODEL))]
                 + [_const_spec(c.shape) for c in carry] + [ANY] * n_red,
        out_specs=[rev(D_MODEL), rev(D_IN), rev(D_MODEL), rev(D_MODEL), rev(D_MODEL),
                   _const_spec((DEPTH * PACK_ROWS, 128)), _const_spec((8, DMOD_COLS)), _const_spec((1, D_MODEL))]
                  + [_const_spec(p.shape[2:]) for p in reduce],
        out_shape=[jax.ShapeDtypeStruct((SEQ, D_MODEL), F32), jax.ShapeDtypeStruct((SEQ, D_IN), BF16),
                   jax.ShapeDtypeStruct((SEQ, D_MODEL), BF16), jax.ShapeDtypeStruct((SEQ, D_MODEL), BF16),
                   jax.ShapeDtypeStruct((SEQ, D_MODEL), BF16), jax.ShapeDtypeStruct((DEPTH * PACK_ROWS, 128), F32),
                   jax.ShapeDtypeStruct((8, DMOD_COLS), F32), jax.ShapeDtypeStruct((1, D_MODEL), F32)]
                  + [jax.ShapeDtypeStruct(p.shape[2:], F32) for p in reduce],
        scratch_shapes=[pltpu.VMEM((8, D_MODEL), F32), pltpu.VMEM((N_HEADS, CHUNK, HEAD), F32),
                        pltpu.VMEM((HALO, D_POOL), F32)] + comm_scratch,
        compiler_params=pltpu.CompilerParams(dimension_semantics=("arbitrary",), vmem_limit_bytes=VMEM_LIMIT),
    )(a, b, x, proj, proj, y, cdf, mod, w_int, w_outf, *small, ln_g, *carry, *reduce)


def _grad_matmuls(dproj, h, cat, dy, name):
    in_cols, out_cols = D_IN // 4, D_MODEL // 2
    in_steps = D_IN // in_cols

    def body(dproj_ref, h_ref, cat_ref, dy_ref, gin_ref, gout_ref):
        step = pl.program_id(0)

        @pl.when(step < in_steps)
        def _():
            gin_ref[...] = _dot_tn(dproj_ref[...], h_ref[...]).astype(BF16)

        @pl.when(step >= in_steps)
        def _():
            gout_ref[...] = _dot_tn(cat_ref[...], dy_ref[...]).astype(BF16)

    in_block = lambda j: jnp.minimum(j, in_steps - 1)
    out_block = lambda j: jnp.maximum(j - in_steps, 0)
    return pl.pallas_call(
        body,
        name=name,
        grid=(in_steps + D_MODEL // out_cols,),
        in_specs=[pl.BlockSpec((SEQ, in_cols), lambda j: (0, in_block(j))), _const_spec((SEQ, D_MODEL)),
                  pl.BlockSpec((SEQ, out_cols), lambda j: (0, out_block(j))), _const_spec((SEQ, D_MODEL))],
        out_specs=[pl.BlockSpec((in_cols, D_MODEL), lambda j: (in_block(j), 0)),
                   pl.BlockSpec((out_cols, D_MODEL), lambda j: (out_block(j), 0))],
        out_shape=[jax.ShapeDtypeStruct((D_IN, D_MODEL), BF16), jax.ShapeDtypeStruct((D_MODEL, D_MODEL), BF16)],
        compiler_params=pltpu.CompilerParams(dimension_semantics=("arbitrary",), vmem_limit_bytes=VMEM_LIMIT),
    )(dproj, h, cat, dy)


def _adamw_math(w, g, m, v):
    m = ADAM_B1 * m + (1.0 - ADAM_B1) * g
    v = ADAM_B2 * v + (1.0 - ADAM_B2) * (g * g)
    m_hat = m / (1.0 - ADAM_B1 ** ADAM_STEP)
    v_hat = v / (1.0 - ADAM_B2 ** ADAM_STEP)
    delta = -ADAM_LR * (m_hat / (jnp.sqrt(v_hat) + ADAM_EPS) + ADAM_WD * w)
    return delta, m, v


def _adamw(w, grads, m, v, block_rows, name):
    rows, cols = grads[0].shape
    blocks = rows // block_rows

    def body(w_ref, m_ref, v_ref, *rest):
        g_refs, (g_ref, d_ref, nm_ref, nv_ref) = rest[:DEPTH], rest[DEPTH:]
        for layer in range(DEPTH):
            @pl.when(pl.program_id(0) == layer)
            def _():
                g = g_refs[layer][...]
                g_ref[...] = g
                d_ref[...], nm_ref[...], nv_ref[...] = _adamw_math(w_ref[...], g, m_ref[...], v_ref[...])

    def grad_spec(layer):
        return pl.BlockSpec((block_rows, cols),
                            lambda l, i: (jnp.where(l == layer, i, jnp.where(l < layer, 0, blocks - 1)), 0))

    spec = pl.BlockSpec((block_rows, cols), lambda l, i: (l * blocks + i, 0))
    return pl.pallas_call(
        body,
        name=name,
        grid=(DEPTH, blocks),
        in_specs=[spec] * 3 + [grad_spec(layer) for layer in range(DEPTH)],
        out_specs=[spec] * 4,
        out_shape=[jax.ShapeDtypeStruct(w.shape, F32)] * 4,
        compiler_params=pltpu.CompilerParams(dimension_semantics=("arbitrary", "arbitrary"),
                                             vmem_limit_bytes=VMEM_LIMIT),
    )(w, m, v, *grads)


MESH = pl.DeviceIdType.MESH
SIBLING = 1
ANY = pl.BlockSpec(memory_space=pl.ANY)
VMEM = pl.BlockSpec(memory_space=pltpu.VMEM)


def _me():
    return lax.axis_index("x"), lax.axis_index("y"), lax.axis_index("c")


def _peer(r):
    x, y, c = _me()
    return (1 - x if r & 4 else x, 1 - y if r & 2 else y, 1 - c if r & 1 else c)


def _index(dev):
    return 4 * dev[0] + 2 * dev[1] + dev[2]


def _remote(src, dst, send_sem, recv_sem, dev):
    return pltpu.make_async_remote_copy(src_ref=src, dst_ref=dst, send_sem=send_sem, recv_sem=recv_sem,
                                        device_id=dev, device_id_type=MESH)


ACROSS_X, ACROSS_Y, ACROSS_BOTH = 4, 2, 6
GATHER_SEMS = 11


class _TwoLevelGather:
    def __init__(self, out, send_sems, recv_sems, src=None):
        self.out, self.send_sems, self.recv_sems, self.src = out, send_sems, recv_sems, src
        self.rows = (out.shape[0] // N_DEV) if len(out.shape) == 2 else out.shape[1]
        self.half = self.rows // 2

    def _slot(self, block):
        if len(self.out.shape) == 2:
            return self.out.at[pl.ds(pl.multiple_of(_index(block) * self.rows, self.rows), self.rows)]
        return self.out.at[_index(block)]

    def _copy(self, k, block, part, to, src=None):
        slot = self._slot(block)
        if part is not None:
            rows = pl.ds(part * self.half, self.half)
            slot = slot.at[rows]
            src = None if src is None else src.at[rows]
        return _remote(slot if src is None else src, slot, self.send_sems.at[k], self.recv_sems.at[k], to)

    def _mine(self):
        me = _me()
        src = self._slot(me) if self.src is None else self.src
        x, y = _peer(ACROSS_X), _peer(ACROSS_Y)
        return [self._copy(1, me, 0, x, src), self._copy(3, me, 1, y, src), self._copy(0, me, None, _peer(SIBLING), src),
                self._copy(2, me, 1, x, src), self._copy(4, me, 0, y, src)]

    def _relayed(self):
        return [self._copy(5, _peer(ACROSS_X), 0, _peer(ACROSS_Y)), self._copy(6, _peer(ACROSS_Y), 1, _peer(ACROSS_X))]

    def _passed(self):
        sib, far = _peer(SIBLING), _peer(ACROSS_BOTH)
        return [self._copy(7, _peer(ACROSS_X), None, sib), self._copy(8, _peer(ACROSS_Y), None, sib),
                self._copy(9, far, 0, sib), self._copy(10, far, 1, sib)]

    def _arrival(self, k, r, part):
        return self._copy(k, _peer(r), part, _me())

    def send_first(self):
        for cp in self._mine()[:3]:
            cp.start()

    def send_second(self):
        for cp in self._mine()[3:]:
            cp.start()

    def send_mine(self):
        self.send_first()
        self.send_second()

    def relay(self):
        relayed = self._relayed()
        self._arrival(1, ACROSS_X, 0).wait_recv()
        relayed[0].start()
        self._arrival(3, ACROSS_Y, 1).wait_recv()
        relayed[1].start()

    def pass_near(self):
        passed = self._passed()
        self._arrival(2, ACROSS_X, 1).wait_recv()
        passed[0].start()
        self._arrival(4, ACROSS_Y, 0).wait_recv()
        passed[1].start()

    def pass_far(self):
        passed = self._passed()
        self._arrival(5, ACROSS_BOTH, 0).wait_recv()
        passed[2].start()
        self._arrival(6, ACROSS_BOTH, 1).wait_recv()
        passed[3].start()

    def pass_on(self):
        self.pass_near()
        self.pass_far()

    def wait_sibling(self):
        self._arrival(0, SIBLING, None).wait_recv()

    def wait_passed(self, r):
        if r == ACROSS_BOTH:
            self._arrival(9, r ^ SIBLING, 0).wait_recv()
            self._arrival(10, r ^ SIBLING, 1).wait_recv()
        else:
            self._arrival(7 if r == ACROSS_X else 8, r ^ SIBLING, None).wait_recv()

    def wait_rest(self):
        self.wait_sibling()
        for r in (ACROSS_X, ACROSS_Y, ACROSS_BOTH):
            self.wait_passed(r)

    def wait_sends(self):
        for cp in self._mine() + self._relayed() + self._passed():
            cp.wait_send()


class _ChipReduceScatter:
    SLOTS = 6

    def __init__(self, arrays, l_sem, d_send, d_recv, i_send, i_recv):
        self.arrays = arrays
        self.l_sem, self.d_send, self.d_recv, self.i_send, self.i_recv = l_sem, d_send, d_recv, i_send, i_recv

    @staticmethod
    def buffers(rows, cols, dtype, staged=True):
        stage = [pltpu.VMEM((4, rows, cols), dtype)] if staged else []
        return stage + [pltpu.VMEM((4, rows, cols), dtype), pltpu.VMEM((3, rows, cols), dtype),
                        pltpu.VMEM((2, rows, cols), dtype), pltpu.VMEM((2, rows // 2, cols), dtype)]

    @classmethod
    def semaphores(cls, n):
        return [pltpu.SemaphoreType.DMA((n,)), pltpu.SemaphoreType.DMA((n, 4)), pltpu.SemaphoreType.DMA((n, 4)),
                pltpu.SemaphoreType.DMA((n, cls.SLOTS)), pltpu.SemaphoreType.DMA((n, cls.SLOTS))]

    def _pick(self, which):
        return list(enumerate(self.arrays)) if which is None else [(n, self.arrays[n]) for n in which]

    @staticmethod
    def _chip(r):
        dev = _me() if r is None else _peer(r)
        return 2 * dev[0] + dev[1]

    def _staging(self, which):
        c = _me()[2]
        return [pltpu.make_async_copy(a["part"].at[pl.ds(0, 4), c], a["stage"], self.l_sem.at[n])
                for n, a in self._pick(which) if a["staged"]]

    def _first(self, which, chip):
        other = 1 - _me()[2]
        return [_remote(a["part"].at[chip, other], a["sib"].at[chip], self.d_send.at[n, chip], self.d_recv.at[n, chip],
                        _peer(SIBLING)) for n, a in self._pick(which)]

    @staticmethod
    def _halves(a):
        half = a["rcv"].shape[1] // 2
        return pl.ds(0, half), pl.ds(half, half)

    def _hops(self, n, a):
        h0, h1 = self._halves(a)
        x, y = _peer(ACROSS_X), _peer(ACROSS_Y)
        snd, rcv, relay = a["snd"], a["rcv"], a["relay"]
        pairs = [(snd.at[2, h0], relay.at[0], x), (snd.at[2, h1], relay.at[1], y),
                 (snd.at[0, h0], rcv.at[0, h0], x), (snd.at[0, h1], rcv.at[0, h1], x),
                 (snd.at[1, h1], rcv.at[1, h1], y), (snd.at[1, h0], rcv.at[1, h0], y)]
        return [_remote(s, d, self.i_send.at[n, k], self.i_recv.at[n, k], to) for k, (s, d, to) in enumerate(pairs)]

    def _mine(self, a, chip, rows=None):
        src = a["stage"].at[chip] if a["staged"] else a["part"].at[chip, _me()[2]]
        mine, sib = (src[...], a["sib"][chip]) if rows is None else (src[rows, :], a["sib"][chip, rows, :])
        return mine.astype(F32) + sib.astype(F32)

    def start(self, which=None, chips=None):
        if chips is None:
            for cp in self._staging(which):
                cp.start()
        for chip in range(4) if chips is None else chips:
            for cp in self._first(which, chip):
                cp.start()

    def send_far(self, which=None):
        far = self._chip(ACROSS_BOTH)
        for cp in self._staging(which):
            cp.wait()
        for cp in self._first(which, far):
            cp.wait_recv()
        for n, a in self._pick(which):
            hops = self._hops(n, a)
            a["snd"][2] = self._mine(a, far).astype(a["snd"].dtype)
            hops[0].start()
            hops[1].start()

    def send_near(self, r, which=None):
        chip = self._chip(r)
        for cp in self._first(which, chip):
            cp.wait_recv()
        for n, a in self._pick(which):
            h0, h1 = self._halves(a)
            hops = self._hops(n, a)
            if r == ACROSS_X:
                a["snd"][0, h0, :] = self._mine(a, chip, h0).astype(a["snd"].dtype)
                hops[2].start()
            else:
                a["snd"][1, h1, :] = self._mine(a, chip, h1).astype(a["snd"].dtype)
                hops[4].start()

    def exchange(self, which=None):
        self.send_far(which)
        self.send_near(ACROSS_X, which)
        self.send_near(ACROSS_Y, which)

    def fold(self, which=None):
        across_x, across_y = self._chip(ACROSS_X), self._chip(ACROSS_Y)
        for n, a in self._pick(which):
            h0, h1 = self._halves(a)
            hops = self._hops(n, a)
            dtype = a["snd"].dtype
            hops[1].wait_recv()
            a["snd"][0, h1, :] = (self._mine(a, across_x, h1) + a["relay"][1].astype(F32)).astype(dtype)
            hops[3].start()
            hops[0].wait_recv()
            a["snd"][1, h0, :] = (self._mine(a, across_y, h0) + a["relay"][0].astype(F32)).astype(dtype)
            hops[5].start()

    def finish(self, which=None):
        home = self._chip(None)
        for cp in self._first(which, home):
            cp.wait_recv()
        for n, a in self._pick(which):
            hops = self._hops(n, a)
            a["out"][...] = self._mine(a, home)
            hops[2].wait_recv()
            hops[3].wait_recv()
            a["out"][...] += a["rcv"][0].astype(F32)
            hops[4].wait_recv()
            hops[5].wait_recv()
            a["out"][...] += a["rcv"][1].astype(F32)

    def wait_sends(self, which=None):
        for chip in range(4):
            for cp in self._first(which, chip):
                cp.wait_send()
        for n, a in self._pick(which):
            for cp in self._hops(n, a):
                cp.wait_send()


def _direct_exchange(src_of, dst_of, send_sems, recv_sems):
    me = _me()
    copies = [_remote(src_of(_peer(r)), dst_of(me), send_sems.at[r - 1], recv_sems.at[r - 1], _peer(r))
              for r in range(1, N_DEV)]
    for cp in copies:
        cp.start()
    return copies


def _wait_direct(copies):
    for cp in copies:
        cp.wait_recv()
    for cp in copies:
        cp.wait_send()


def _forward(x, c, w_ada, b_ada, small, ln_g, ln_b, mine, following):
    assert DEPTH == 2
    cols = w_ada.shape[2]
    shard = mine[0].shape[0]
    pair = 2 * shard

    def body(x_hbm, c_ref, wada_hbm, bada_ref, wpool_ref, pscale_ref, slng_ref, slnb_ref, wsgu_ref, bsgut_ref,
             lng_ref, lnb_ref, wint_hbm, wout_hbm, next_in_hbm, next_out_hbm,
             out0_ref, y0_ref, cdf0_ref, proj0_hbm, out1_ref, y1_ref, cdf1_ref, proj1_hbm,
             wint_keep, wout_keep, wint_next, wout_next, acts_ref, mod_ref,
             wint_v, wout_v, h_buf, proj_blk, proj_tile, halo_ref, x_ref, w_send, w_recv, w_local, p_sems,
             t_sems, in_sems, act_all, act_src, part, mod_recv, wada_ref, a_send, a_recv, m_send, m_recv,
             n_send, n_recv, n_local, f_sems):
        step = pl.program_id(0)
        chip_of = lambda dev: 2 * dev[0] + dev[1]

        def hosted():
            return [_TwoLevelGather(out, n_send.at[n], n_recv.at[n], src=src)
                    for n, (out, src) in enumerate(((wint_next, next_in_hbm), (wout_next, next_out_hbm)))]

        def hosted_own():
            me = _me()
            return [pltpu.make_async_copy(g.src, g._slot(me), n_local.at[n]) for n, g in enumerate(hosted())]

        def gathers():
            return (_TwoLevelGather(wint_v, w_send.at[0], w_recv.at[0], src=wint_hbm),
                    _TwoLevelGather(wout_v, w_send.at[1], w_recv.at[1], src=wout_hbm))

        def keeps():
            return [pltpu.make_async_copy(wint_v, wint_keep, w_local.at[2]),
                    pltpu.make_async_copy(wout_v, wout_keep, w_local.at[3])]

        def tile_read(t):
            slot = t % 2
            return pltpu.make_async_copy(proj0_hbm.at[pl.ds(pl.multiple_of(t * TM, TM), TM)], proj_tile.at[slot],
                                         t_sems.at[slot])

        def first_layer_start():
            writes = []

            def project(n, dev):
                first = pl.multiple_of(chip_of(dev) * pair, pair)
                if n >= 2:
                    writes[n - 2].wait()

                @pl.loop(0, N_TILES)
                def _(t):
                    rows = pl.ds(pl.multiple_of(t * TM, TM), TM)
                    proj_blk[n % 2, rows, :] = _dot_nt(h_buf[rows, :], wint_v[pl.ds(first, pair), :])

                cp = pltpu.make_async_copy(proj_blk.at[n % 2], proj0_hbm.at[:, pl.ds(first, pair)], p_sems.at[n % 2])
                cp.start()
                writes.append(cp)

            me = _me()
            halo_ref[...] = jnp.zeros_like(halo_ref)
            gather_in, gather_out = gathers()
            own_in = pltpu.make_async_copy(wint_hbm, gather_in._slot(me), w_local.at[0])
            own_out = pltpu.make_async_copy(wout_hbm, gather_out._slot(me), w_local.at[1])
            x_load = pltpu.make_async_copy(x_hbm, x_ref, in_sems.at[0])
            x_load.start()
            wada_load = pltpu.make_async_copy(wada_hbm, wada_ref, in_sems.at[1])
            wada_load.start()
            mine_index = _index(me)
            cval = c_ref[...]
            act_src[...] = jnp.zeros_like(act_src)
            act_src[0:1, :] = cval * jax.nn.sigmoid(cval)
            act_all[mine_index] = act_src[...]
            act_copies = _direct_exchange(lambda p: act_src, lambda m: act_all.at[_index(m)], a_send, a_recv)

            own_in.start()
            own_out.start()
            gather_in.send_first()

            _wait_direct(act_copies)
            acts = jnp.concatenate([act_all[j, 0:1, :] for j in range(N_DEV)], axis=0)
            acts_ref[...] = acts
            part[...] = jnp.zeros_like(part)
            wada_load.wait()
            for l in range(DEPTH):
                res = lax.dot_general(acts, wada_ref[l], (((1,), (0,)), ((), ())), preferred_element_type=F32,
                                      precision=lax.Precision.HIGHEST)
                for b in range(N_DEV):
                    part[b, l:l + 1, :] = res[b:b + 1, :]
            mod_recv[mine_index] = part[mine_index]
            mod_copies = _direct_exchange(lambda p: part.at[_index(p)], lambda m: mod_recv.at[_index(m)],
                                          m_send, m_recv)
            gather_in.send_second()
            gather_out.send_mine()

            _wait_direct(mod_copies)
            for l in range(DEPTH):
                for j in range(N_DEV):
                    sl = slice(j * cols, (j + 1) * cols)
                    mod_ref[l:l + 1, sl] = mod_recv[j, l:l + 1, :] + bada_ref[l:l + 1, sl]
            x_load.wait()
            shift = mod_ref[0:1, 0:D_MODEL]
            scale = mod_ref[0:1, D_MODEL:2 * D_MODEL]

            @pl.loop(0, N_TILES)
            def _(t):
                rows = pl.ds(pl.multiple_of(t * TM, TM), TM)
                xn, _ = _ln(x_ref[rows, :])
                h_buf[rows, :] = (xn * (1.0 + scale) + shift).astype(BF16)

            gather_in.relay()
            own_in.wait()
            gather_in.wait_sibling()
            project(0, me)
            gather_in.pass_near()
            gather_in.wait_passed(ACROSS_X)
            project(1, _peer(ACROSS_X))
            gather_out.relay()
            hosted_own()[0].start()
            hosted()[0].send_mine()
            gather_in.wait_passed(ACROSS_Y)
            project(2, _peer(ACROSS_Y))
            gather_in.pass_far()
            gather_in.wait_passed(ACROSS_BOTH)
            project(3, _peer(ACROSS_BOTH))

            gather_out.pass_on()
            gather_out.wait_rest()
            own_out.wait()
            for cp in keeps():
                cp.start()
            writes[2].wait()
            writes[3].wait()
            tile_read(0).start()

        def fetches():
            return [pltpu.make_async_copy(wint_next, wint_v, f_sems.at[0]),
                    pltpu.make_async_copy(wout_next, wout_v, f_sems.at[1])]

        def tile_write(t):
            slot = t % 2
            return pltpu.make_async_copy(proj_tile.at[slot], proj1_hbm.at[pl.ds(pl.multiple_of(t * TM, TM), TM)],
                                         t_sems.at[slot])

        def mix(layer, tile, proj, cdf_ref):
            weights = _MixWeights(layer, wpool_ref, pscale_ref, slng_ref, slnb_ref, wsgu_ref, bsgut_ref)
            cat, cdf_ref[...] = _mix_forward(proj, halo_ref[...], tile, weights)
            halo_ref[...] = proj[TM - HALO:, 0:D_POOL]
            return cat.astype(BF16)

        def close(layer, rows, cat, out_ref, y_ref):
            y = _dot(cat, wout_v[...])
            y_ref[...] = y
            zn, _ = _ln(ALPHA * x_ref[rows, :] + mod_ref[layer:layer + 1, 2 * D_MODEL:] * y)
            out = zn * lng_ref[layer:layer + 1, :] + lnb_ref[layer:layer + 1, :]
            out_ref[...] = out
            return out

        def first_layer_tile(tile):
            @pl.when(tile + 1 < N_TILES)
            def _():
                tile_read(tile + 1).start()

            tile_read(tile).wait()
            rows = pl.ds(pl.multiple_of(tile * TM, TM), TM)
            cat = mix(0, tile, proj_tile[tile % 2], cdf0_ref)
            x_ref[rows, :] = close(0, rows, cat, out0_ref, y0_ref)

        def second_layer_mix(tile):
            rows = pl.ds(pl.multiple_of(tile * TM, TM), TM)
            shift = mod_ref[1:2, 0:D_MODEL]
            scale = mod_ref[1:2, D_MODEL:2 * D_MODEL]
            xn, _ = _ln(x_ref[rows, :])
            h = (xn * (1.0 + scale) + shift).astype(BF16)

            @pl.when(tile >= 2)
            def _():
                tile_write(tile - 2).wait()

            proj_tile[tile % 2] = _dot_nt(h, wint_v[...])
            tile_write(tile).start()
            h_buf[rows, :] = mix(1, tile, proj_tile[tile % 2], cdf1_ref)

        def second_layer_close(tile):
            rows = pl.ds(pl.multiple_of(tile * TM, TM), TM)
            close(1, rows, h_buf[rows, :], out1_ref, y1_ref)

        next_in, next_out = 0, 1

        @pl.when(step < N_TILES)
        def _():
            @pl.when(step == 0)
            def _():
                first_layer_start()

            @pl.when(step == 1)
            def _():
                hosted()[next_in].relay()

            @pl.when(step == N_TILES // 2)
            def _():
                hosted()[next_in].pass_near()

            @pl.when(step == N_TILES - 1)
            def _():
                hosted()[next_in].pass_far()
                for g in gathers():
                    g.wait_sends()
                for cp in keeps() + [hosted_own()[next_in]]:
                    cp.wait()
                hosted()[next_in].wait_rest()
                fetches()[next_in].start()

            first_layer_tile(step)

        @pl.when((step >= N_TILES) & (step < 2 * N_TILES))
        def _():
            @pl.when(step == N_TILES)
            def _():
                halo_ref[...] = jnp.zeros_like(halo_ref)
                hosted_own()[next_out].start()
                hosted()[next_out].send_mine()
                fetches()[next_in].wait()

            @pl.when(step == N_TILES + 2)
            def _():
                hosted()[next_out].relay()

            @pl.when(step == N_TILES + 4)
            def _():
                hosted()[next_out].pass_near()

            @pl.when(step == N_TILES + 6)
            def _():
                hosted()[next_out].pass_far()

            second_layer_mix(step - N_TILES)

            @pl.when(step == 2 * N_TILES - 1)
            def _():
                tile_write(N_TILES - 2).wait()
                tile_write(N_TILES - 1).wait()

        @pl.when(step >= 2 * N_TILES)
        def _():
            @pl.when(step == 2 * N_TILES)
            def _():
                hosted_own()[next_out].wait()
                hosted()[next_out].wait_rest()
                fetches()[next_out].start()
                fetches()[next_out].wait()

            second_layer_close(step - 2 * N_TILES)

            @pl.when(step == 3 * N_TILES - 1)
            def _():
                for g in hosted():
                    g.wait_sends()

    first = lambda w: pl.BlockSpec((TM, w), lambda i: (jnp.minimum(i, N_TILES - 1), 0))
    second = lambda w: pl.BlockSpec((TM, w), lambda i: (jnp.clip(i - N_TILES, 0, N_TILES - 1), 0))
    third = lambda w: pl.BlockSpec((TM, w), lambda i: (jnp.maximum(i - 2 * N_TILES, 0), 0))
    gather_sems = pltpu.SemaphoreType.DMA((2, GATHER_SEMS))
    seven = pltpu.SemaphoreType.DMA((7,))
    per_layer = [jax.ShapeDtypeStruct((SEQ, D_MODEL), F32), jax.ShapeDtypeStruct((SEQ, D_MODEL), F32),
                 jax.ShapeDtypeStruct((SEQ, 2 * D_SGU), F32), jax.ShapeDtypeStruct((SEQ, D_IN), F32)]
    gathered = [jax.ShapeDtypeStruct((D_IN, D_MODEL), BF16), jax.ShapeDtypeStruct((D_MODEL, D_MODEL), BF16)]
    res = pl.pallas_call(
        body,
        name="layers_fwd",
        grid=(3 * N_TILES,),
        in_specs=[ANY, _const_spec(c.shape), ANY, _const_spec(b_ada.shape)] + [_const_spec(s) for s in SMALL_SPECS]
                 + [_const_spec((DEPTH, D_MODEL)), _const_spec((DEPTH, D_MODEL))] + [ANY] * 4,
        out_specs=[first(D_MODEL), first(D_MODEL), first(2 * D_SGU), ANY,
                   third(D_MODEL), third(D_MODEL), second(2 * D_SGU), ANY] + [ANY] * 4
                  + [_const_spec((N_DEV, D_MODEL)), _const_spec((DEPTH, 3 * D_MODEL))],
        out_shape=per_layer * 2 + gathered * 2 + [jax.ShapeDtypeStruct((N_DEV, D_MODEL), F32),
                                                  jax.ShapeDtypeStruct((DEPTH, 3 * D_MODEL), F32)],
        scratch_shapes=[pltpu.VMEM((D_IN, D_MODEL), BF16), pltpu.VMEM((D_MODEL, D_MODEL), BF16),
                        pltpu.VMEM((SEQ, D_MODEL), BF16), pltpu.VMEM((2, SEQ, pair), F32),
                        pltpu.VMEM((2, TM, D_IN), F32), pltpu.VMEM((HALO, D_POOL), F32),
                        pltpu.VMEM((SEQ, D_MODEL), F32),
                        gather_sems, gather_sems, pltpu.SemaphoreType.DMA((4,)), pltpu.SemaphoreType.DMA((2,)),
                        pltpu.SemaphoreType.DMA((2,)), pltpu.SemaphoreType.DMA((2,)),
                        pltpu.VMEM((N_DEV, 8, D_MODEL), F32), pltpu.VMEM((8, D_MODEL), F32),
                        pltpu.VMEM((N_DEV, 8, cols), F32), pltpu.VMEM((N_DEV, 8, cols), F32),
                        pltpu.VMEM(w_ada.shape, F32), seven, seven, seven, seven,
                        gather_sems, gather_sems, pltpu.SemaphoreType.DMA((2,)), pltpu.SemaphoreType.DMA((2,))],
        compiler_params=pltpu.CompilerParams(dimension_semantics=("arbitrary",), vmem_limit_bytes=VMEM_LIMIT),
    )(x, c, w_ada, b_ada, *small, ln_g, ln_b, *mine, *following)
    return res[0:4], res[4:8], res[8:10], res[10:12], res[12], res[13]


ADA_CHUNK = 256


def _grad_tail(dproj, h, cat, dy, small, dmod8, loss_lanes, w_ada, m_ada, v_ada, act_t, b_ada, m_bada, v_bada):
    shard_in, shard_out, shard_small = D_IN // N_DEV, D_MODEL // N_DEV, small.shape[2]
    cols = w_ada.shape[2]
    W_IN, W_OUT, SMALL = 0, 1, 2

    def body(dproj_hbm, h_hbm, cat_hbm, dy_hbm, small_hbm, dmod_ref, lanes_ref, wada_hbm, mada_hbm, vada_hbm,
             act_ref, bada_ref, mbada_ref, vbada_ref,
             gwin_ref, gwout_ref, stot_ref, loss_ref, gada_hbm, dada_hbm, nmada_hbm, nvada_hbm,
             gb_ref, db_ref, nmb_ref, nvb_ref,
             dproj_v, h_v, cat_v, dy_v, part_in, part_out, own_small, loss_src, loss_all, dmod_all, ada_in, ada_out,
             *rest):
        bufs, rest = rest[:13], rest[13:]
        load_sems, rs_sems = rest[0], rest[1:6]
        m_send, m_recv, g_send, g_recv, s_send, s_recv, ada_lsem, ada_ssem = rest[6:]
        mine = _index(_me())

        def update_ada():
            upper = (mine % 2) == 1

            def dmod_of(layer):
                rows = []
                for b in range(N_DEV):
                    r = dmod_all[b, pl.ds(4 * layer + mine // 2, 1), :]
                    rows.append(jnp.where(upper, r[:, cols:], r[:, :cols]))
                return jnp.concatenate(rows, axis=0)

            chunks = [(layer, c) for layer in range(DEPTH) for c in range(D_MODEL // ADA_CHUNK)]

            def loads(i):
                layer, c = chunks[i]
                rows = pl.ds(c * ADA_CHUNK, ADA_CHUNK)
                return [pltpu.make_async_copy(src.at[layer, rows], ada_in.at[i % 2, k], ada_lsem.at[i % 2, k])
                        for k, src in enumerate((wada_hbm, mada_hbm, vada_hbm))]

            def stores(i):
                layer, c = chunks[i]
                rows = pl.ds(c * ADA_CHUNK, ADA_CHUNK)
                return [pltpu.make_async_copy(ada_out.at[i % 2, k], dst.at[layer, rows], ada_ssem.at[i % 2, k])
                        for k, dst in enumerate((gada_hbm, dada_hbm, nmada_hbm, nvada_hbm))]

            for cp in loads(0):
                cp.start()
            dmods = {}
            for i, (layer, c) in enumerate(chunks):
                if i + 1 < len(chunks):
                    for cp in loads(i + 1):
                        cp.start()
                for cp in loads(i):
                    cp.wait()
                if i >= 2:
                    for cp in stores(i - 2):
                        cp.wait()
                if layer not in dmods:
                    dmods[layer] = dmod_of(layer)
                act = act_ref[pl.ds(c * ADA_CHUNK, ADA_CHUNK), :]
                g = act[:, 0:1] * dmods[layer][0:1, :]
                for b in range(1, N_DEV):
                    g = g + act[:, b:b + 1] * dmods[layer][b:b + 1, :]
                slot = i % 2
                delta, new_m, new_v = _adamw_math(ada_in[slot, 0], g, ada_in[slot, 1], ada_in[slot, 2])
                ada_out[slot, 0] = g
                ada_out[slot, 1] = delta
                ada_out[slot, 2] = new_m
                ada_out[slot, 3] = new_v
                for cp in stores(i):
                    cp.start()
            for i in (len(chunks) - 2, len(chunks) - 1):
                for cp in stores(i):
                    cp.wait()

            total = dmod_all[0]
            for b in range(1, N_DEV):
                total = total + dmod_all[b]
            width = total.shape[1]
            for layer in range(DEPTH):
                for q in range(4):
                    gb_ref[layer:layer + 1, q * width:(q + 1) * width] = total[4 * layer + q:4 * layer + q + 1, :]
            db_ref[...], nmb_ref[...], nvb_ref[...] = _adamw_math(bada_ref[...], gb_ref[...], mbada_ref[...],
                                                                  vbada_ref[...])

        order = (ACROSS_BOTH, ACROSS_X, ACROSS_Y, None)
        chips = [_ChipReduceScatter._chip(r) for r in order]
        loads = [pltpu.make_async_copy(s, d, load_sems.at[n]) for n, (s, d) in enumerate(
            ((cat_hbm, cat_v), (dy_hbm, dy_v), (h_hbm, h_v)))]
        loads += [pltpu.make_async_copy(dproj_hbm.at[:, pl.ds(pl.multiple_of(chip * 2 * shard_in, 2 * shard_in),
                                                             2 * shard_in)], dproj_v.at[n], load_sems.at[3 + n])
                  for n, chip in enumerate(chips)]
        for cp in loads:
            cp.start()
        arrays = [dict(part=part_in, out=gwin_ref, staged=False, sib=bufs[0], snd=bufs[1], rcv=bufs[2], relay=bufs[3]),
                  dict(part=part_out, out=gwout_ref, staged=False, sib=bufs[4], snd=bufs[5], rcv=bufs[6],
                       relay=bufs[7]),
                  dict(part=small_hbm, out=own_small, staged=True, stage=bufs[8], sib=bufs[9], snd=bufs[10],
                       rcv=bufs[11], relay=bufs[12])]
        scatter = _ChipReduceScatter(arrays, *rs_sems)
        scatter.start([SMALL])
        dmod_all[mine] = dmod_ref[...]
        dmod_copies = _direct_exchange(lambda p: dmod_ref, lambda m: dmod_all.at[_index(m)], m_send, m_recv)
        loss_src[...] = jnp.full(loss_src.shape, (0.5 / D_MODEL) * jnp.sum(lanes_ref[...]), F32)
        loss_all[mine] = loss_src[...]
        loss_copies = _direct_exchange(lambda p: loss_src, lambda m: loss_all.at[_index(m)], s_send, s_recv)

        loads[0].wait()
        loads[1].wait()
        for blk in range(2):
            res = _dot_tn(cat_v[:, blk * 512:(blk + 1) * 512], dy_v[...]).astype(BF16)
            for s in range(4):
                part_out[2 * blk + s // 2, s % 2] = res[s * shard_out:(s + 1) * shard_out]
        scatter.start([W_OUT])
        scatter.exchange([SMALL])

        gather = _TwoLevelGather(stot_ref, g_send, g_recv)
        loads[2].wait()
        for n, chip in enumerate(chips):
            loads[3 + n].wait()
            res = _dot_tn(dproj_v[n], h_v[...]).astype(BF16)
            part_in[chip, 0] = res[:shard_in]
            part_in[chip, 1] = res[shard_in:]
            scatter.start([W_IN], chips=[chip])
            if n == 0:
                scatter.exchange([W_OUT])
                scatter.fold([SMALL])
            if n == 1:
                scatter.send_far([W_IN])
                scatter.fold([W_OUT])
                scatter.finish([SMALL])
                stot_ref[mine] = own_small[...]
                gather.send_mine()
            if n == 2:
                scatter.send_near(ACROSS_X, [W_IN])
                gather.relay()
            if n == 3:
                scatter.send_near(ACROSS_Y, [W_IN])
        scatter.fold([W_IN])
        scatter.finish([W_OUT])
        _wait_direct(dmod_copies)
        update_ada()
        gather.pass_on()
        gather.wait_rest()
        _wait_direct(loss_copies)
        total = loss_all[0]
        for j in range(1, N_DEV):
            total = total + loss_all[j]
        loss_ref[...] = total
        scatter.finish([W_IN])
        gather.wait_sends()
        scatter.wait_sends()

    buffers = _ChipReduceScatter.buffers
    comm_scratch = (buffers(shard_in, D_MODEL, BF16, staged=False) + buffers(shard_out, D_MODEL, BF16, staged=False)
                    + buffers(shard_small, 128, F32))
    comm_scratch += [pltpu.SemaphoreType.DMA((7,))] + _ChipReduceScatter.semaphores(3)
    comm_scratch += [pltpu.SemaphoreType.DMA((n,)) for n in (7, 7, GATHER_SEMS, GATHER_SEMS, 7, 7)]
    comm_scratch += [pltpu.SemaphoreType.DMA((2, 3)), pltpu.SemaphoreType.DMA((2, 4))]
    return pl.pallas_call(
        body,
        name="grad_tail",
        in_specs=[ANY] * 5 + [VMEM, VMEM] + [ANY] * 3 + [VMEM] * 4,
        out_specs=[VMEM] * 4 + [ANY] * 4 + [VMEM] * 4,
        out_shape=[jax.ShapeDtypeStruct((shard_in, D_MODEL), F32), jax.ShapeDtypeStruct((shard_out, D_MODEL), F32),
                   jax.ShapeDtypeStruct((N_DEV, shard_small, 128), F32), jax.ShapeDtypeStruct((8, 128), F32)]
                  + [jax.ShapeDtypeStruct(w_ada.shape, F32)] * 4 + [jax.ShapeDtypeStruct(b_ada.shape, F32)] * 4,
        scratch_shapes=[pltpu.VMEM((4, SEQ, 2 * shard_in), BF16), pltpu.VMEM(h.shape, BF16), pltpu.VMEM(cat.shape, BF16),
                        pltpu.VMEM(dy.shape, BF16), pltpu.VMEM((4, 2, shard_in, D_MODEL), BF16),
                        pltpu.VMEM((4, 2, shard_out, D_MODEL), BF16), pltpu.VMEM((shard_small, 128), F32),
                        pltpu.VMEM((8, 128), F32), pltpu.VMEM((N_DEV, 8, 128), F32),
                        pltpu.VMEM((N_DEV,) + dmod8.shape, F32), pltpu.VMEM((2, 3, ADA_CHUNK, cols), F32),
                        pltpu.VMEM((2, 4, ADA_CHUNK, cols), F32)] + comm_scratch,
        compiler_params=pltpu.CompilerParams(vmem_limit_bytes=VMEM_LIMIT),
    )(dproj, h, cat, dy, small, dmod8, loss_lanes, w_ada, m_ada, v_ada, act_t, b_ada, m_bada, v_bada)


SMALL_NAMES = ("w_pool", "w_sgu", "pool_scale", "sgu_ln_g", "sgu_ln_b", "b_sgu", "ln_g", "ln_b")
SMALL_ROWS = (512, 512, 4, 4, 4, 4, 8, 8)


def _adamw_small(g_packed, ws, ms, vs, name):
    n = len(SMALL_NAMES)

    def body(g_ref, *refs):
        w_refs, m_refs, v_refs = refs[:n], refs[n:2 * n], refs[2 * n:3 * n]
        outs = refs[3 * n:]

        def update(p, at, g):
            delta, new_m, new_v = _adamw_math(w_refs[p][at], g, m_refs[p][at], v_refs[p][at])
            outs[p][at] = g
            outs[n + p][at] = delta
            outs[2 * n + p][at] = new_m
            outs[3 * n + p][at] = new_v

        row = 0
        for p, r in enumerate(SMALL_ROWS):
            shape = ws[p].shape
            for layer in range(DEPTH):
                first = layer * PACK_ROWS + row
                if len(shape) == 4:
                    for k in range(shape[1]):
                        update(p, (layer, k), g_ref[first + k * shape[2]:first + (k + 1) * shape[2], :])
                elif len(shape) == 3:
                    update(p, (layer,), g_ref[first:first + r, :])
                else:
                    g = jnp.concatenate([g_ref[first + k:first + k + 1, :] for k in range(r)], axis=1)
                    update(p, (slice(layer, layer + 1), slice(None)), g)
            row += r

    res = pl.pallas_call(
        body,
        name=name,
        out_shape=[jax.ShapeDtypeStruct(w.shape, F32) for w in ws] * 4,
        compiler_params=pltpu.CompilerParams(vmem_limit_bytes=VMEM_LIMIT),
    )(g_packed, *ws, *ms, *vs)
    return res[:n], res[n:2 * n], res[2 * n:3 * n], res[3 * n:]


def kernel(x, c, w_ada, b_ada, w_in, w_pool, pool_scale, sgu_ln_g, sgu_ln_b, w_sgu, b_sgu, w_out, ln_g, ln_b, loss_target, m_w_ada, m_b_ada, m_w_in, m_w_pool, m_pool_scale, m_sgu_ln_g, m_sgu_ln_b, m_w_sgu, m_b_sgu, m_w_out, m_ln_g, m_ln_b, v_w_ada, v_b_ada, v_w_in, v_w_pool, v_pool_scale, v_sgu_ln_g, v_sgu_ln_b, v_w_sgu, v_b_sgu, v_w_out, v_ln_g, v_ln_b):
    small_w = dict(w_pool=w_pool, w_sgu=w_sgu, pool_scale=pool_scale, sgu_ln_g=sgu_ln_g, sgu_ln_b=sgu_ln_b,
                   b_sgu=b_sgu, ln_g=ln_g, ln_b=ln_b)
    small_m = dict(w_pool=m_w_pool, w_sgu=m_w_sgu, pool_scale=m_pool_scale, sgu_ln_g=m_sgu_ln_g,
                   sgu_ln_b=m_sgu_ln_b, b_sgu=m_b_sgu, ln_g=m_ln_g, ln_b=m_ln_b)
    small_v = dict(w_pool=v_w_pool, w_sgu=v_w_sgu, pool_scale=v_pool_scale, sgu_ln_g=v_sgu_ln_g,
                   sgu_ln_b=v_sgu_ln_b, b_sgu=v_b_sgu, ln_g=v_ln_g, ln_b=v_ln_b)

    wint_loc = jnp.transpose(w_in, (0, 2, 1)).astype(BF16)
    wout_loc = w_out.astype(BF16)
    small = (w_pool, pool_scale, sgu_ln_g, sgu_ln_b, w_sgu, jnp.transpose(b_sgu, (0, 2, 1)))
    (out0, y0, cdf0, proj0), (cur, y1, cdf1, proj1), gathered0, gathered1, act_all, mod = _forward(
        x[0], c, w_ada, b_ada, small, ln_g, ln_b, [wint_loc[0], wout_loc[0]], [wint_loc[1], wout_loc[1]])
    w_int, w_outf = [gathered0[0], gathered1[0]], [gathered0[1], gathered1[1]]
    acts = [(x[0], proj0, y0, cdf0), (out0, proj1, y1, cdf1)]

    shard_in, shard_out = D_IN // N_DEV, D_MODEL // N_DEV
    a, b = cur, loss_target[0]
    loss_lanes, carry, pending = None, (), []
    g_w_in_t, g_w_out = [None] * DEPTH, [None] * DEPTH
    for l in reversed(range(DEPTH)):
        dx, dproj, h, cat, dy, small_grads, dmod8, lanes, *shards = _layer_backward(
            l, a, b, *acts[l], mod, w_int[l], w_outf[l], small, ln_g, l == DEPTH - 1, f"layer_bwd_{l}",
            carry=carry, reduce=pending)
        if shards:
            g_w_in_t[l + 1], g_w_out[l + 1] = shards
        if l == DEPTH - 1:
            loss_lanes = lanes
        if l > 0:
            g_in, g_out = _grad_matmuls(dproj, h, cat, dy, f"grad_w_{l}")
            pending = [g_in.reshape(4, 2, shard_in, D_MODEL), g_out.reshape(4, 2, shard_out, D_MODEL)]
        carry = (small_grads, dmod8)
        a = b = dx
    grad_x = a[None]

    (g_w_in_t[0], g_w_out[0], small_tot, loss_tile, g_w_ada, d_w_ada, nm_w_ada, nv_w_ada,
     g_b_ada, d_b_ada, nm_b_ada, nv_b_ada) = _grad_tail(
        dproj, h, cat, dy, small_grads.reshape(4, 2, DEPTH * PACK_ROWS // N_DEV, 128), dmod8, loss_lanes,
        w_ada, m_w_ada, v_w_ada, jnp.transpose(act_all), b_ada, m_b_ada, v_b_ada)
    loss = loss_tile[0, 0]

    flat = lambda t: t.reshape(-1, t.shape[-1])
    to_t = lambda t: flat(jnp.transpose(t, (0, 2, 1)))
    from_t = lambda t: jnp.transpose(t.reshape(DEPTH, shard_in, D_MODEL), (0, 2, 1))
    g_w_in, d_w_in, nm_w_in, nv_w_in = [from_t(t) for t in _adamw(to_t(w_in), g_w_in_t, to_t(m_w_in), to_t(v_w_in),
                                                                  shard_in // 2, "adamw_w_in")]
    gwout, d_w_out, nm_w_out, nv_w_out = [t.reshape(w_out.shape) for t in _adamw(
        flat(w_out), g_w_out, flat(m_w_out), flat(v_w_out), shard_out, "adamw_w_out")]
    small_out = _adamw_small(small_tot.reshape(DEPTH * PACK_ROWS, 128), [small_w[n] for n in SMALL_NAMES],
                             [small_m[n] for n in SMALL_NAMES], [small_v[n] for n in SMALL_NAMES], "adamw_small")
    gs, ds, ms, vs = [dict(zip(SMALL_NAMES, group)) for group in small_out]

    def ordered(w_ada_, b_ada_, w_in_, small, w_out_):
        return (w_ada_, b_ada_, w_in_, small["w_pool"], small["pool_scale"], small["sgu_ln_g"], small["sgu_ln_b"],
                small["w_sgu"], small["b_sgu"], w_out_, small["ln_g"], small["ln_b"])

    return (loss, grad_x,
            *ordered(g_w_ada, g_b_ada, g_w_in, gs, gwout),
            *ordered(d_w_ada, d_b_ada, d_w_in, ds, d_w_out),
            *ordered(nm_w_ada, nm_b_ada, nm_w_in, ms, nm_w_out),
            *ordered(nv_w_ada, nv_b_ada, nv_w_in, vs, nv_w_out))
```

```python
import jax
import jax.numpy as jnp
from jax import lax
from jax.experimental import pallas as pl
from jax.experimental.pallas import tpu as pltpu

F32 = jnp.float32
BF16 = jnp.bfloat16

D_MODEL = 1024
SEQ = 2048
DEPTH = 2
D_POOL = 512
D_SGU = 512
D_IN = 2560
N_GROUPS = 4
GROUP = 128
N_HEADS = 4
HEAD = 128
CHUNK = 128
WINDOWS = (2, 4, 8, 16)
ALPHA = (2.0 * DEPTH) ** 0.25
LN_EPS = 1e-5
N_DEV = 8

ADAM_LR = 0.001
ADAM_B1 = 0.9
ADAM_B2 = 0.999
ADAM_EPS = 1e-08
ADAM_WD = 0.01
ADAM_STEP = 10

TM = 256
HALO = 16
N_TILES = SEQ // TM
VMEM_LIMIT = 60 * 1024 * 1024

ROW_WPOOL = 0
ROW_WSGU = 512
ROW_PSCALE = 1024
ROW_SLNG = 1028
ROW_SLNB = 1032
ROW_BSGU = 1036
ROW_LNG = 1040
ROW_LNB = 1048
PACK_ROWS = 1088
DMOD_COLS = DEPTH * 3 * D_MODEL // 8

SQRT_HALF = 0.7071067811865476
INV_SQRT_2PI = 0.3989422804014327


def _ln(x):
    mu = jnp.mean(x, axis=-1, keepdims=True)
    xc = x - mu
    var = jnp.mean(xc * xc, axis=-1, keepdims=True)
    rstd = lax.rsqrt(var + LN_EPS)
    return xc * rstd, rstd


def _ln_bwd(dxn, xn, rstd):
    m1 = jnp.mean(dxn, axis=-1, keepdims=True)
    m2 = jnp.mean(dxn * xn, axis=-1, keepdims=True)
    return rstd * (dxn - m1 - xn * m2)


def _normal_cdf(x):
    return 0.5 * (1.0 + lax.erf(x * SQRT_HALF))


def _gelu_parts(x, cdf, with_grad):
    if not with_grad:
        return x * cdf, None
    return x * cdf, cdf + x * (INV_SQRT_2PI * jnp.exp(-0.5 * x * x))


def _silu_parts(x):
    s = jax.nn.sigmoid(x)
    return x * s, s * (1.0 + x * (1.0 - s))


def _dot(a, b):
    return lax.dot_general(a, b, (((1,), (0,)), ((), ())), preferred_element_type=F32)


def _dot_nt(a, b):
    return lax.dot_general(a, b, (((1,), (1,)), ((), ())), preferred_element_type=F32)


def _dot_tn(a, b):
    return lax.dot_general(a, b, (((0,), (0,)), ((), ())), preferred_element_type=F32)


def _row_index(tile):
    return tile * TM + lax.broadcasted_iota(jnp.int32, (TM, 1), 0)


def _window_sums(ext, forward):
    n = TM + HALO
    cur = ext
    outs = []
    for g in range(N_GROUPS):
        step = 1 << g
        cur = cur + pltpu.roll(cur, step if forward else n - step, 0)
        rows = cur[HALO:, :GROUP] if forward else cur[:TM, :GROUP]
        outs.append(rows)
        cur = cur[:, GROUP:] if g + 1 < N_GROUPS else None
    return outs


def _inverse_counts(rows):
    return [1.0 / jnp.minimum(rows + 1, w).astype(F32) for w in WINDOWS]


def _tril_bf16(w):
    t = lax.broadcasted_iota(jnp.int32, (CHUNK, CHUNK), 0)
    s = lax.broadcasted_iota(jnp.int32, (CHUNK, CHUNK), 1)
    return jnp.where(t >= s, w, 0.0).astype(BF16)


class _MixWeights:
    def __init__(self, layer, wpool_ref, pscale_ref, slng_ref, slnb_ref, wsgu_ref, bsgut_ref):
        self.layer = layer
        self.wpool_ref, self.pscale_ref, self.slng_ref, self.slnb_ref = wpool_ref, pscale_ref, slng_ref, slnb_ref
        self.wsgu_ref, self.bsgut_ref = wsgu_ref, bsgut_ref

    def pool(self, g):
        return self.wpool_ref[self.layer, g].astype(BF16)

    def pool_scale(self, g):
        return self.pscale_ref[self.layer:self.layer + 1, g * GROUP:(g + 1) * GROUP]

    def ln_gain(self, h):
        return self.slng_ref[self.layer, h:h + 1, :]

    def ln_bias(self, h):
        return self.slnb_ref[self.layer, h:h + 1, :]

    def mix(self, h):
        return _tril_bf16(self.wsgu_ref[self.layer, h])

    def mix_bias(self, h):
        return self.bsgut_ref[self.layer, :, h:h + 1]


SMALL_SPECS = ((DEPTH, N_GROUPS, GROUP, GROUP), (DEPTH, D_POOL), (DEPTH, N_HEADS, HEAD), (DEPTH, N_HEADS, HEAD),
               (DEPTH, N_HEADS, CHUNK, CHUNK), (DEPTH, CHUNK, N_HEADS))


def _mix_forward(proj, halo, tile, w, cdf=None):
    keep = cdf is not None
    rows = _row_index(tile)
    inv_counts = _inverse_counts(rows)
    xa = proj[:, 0:D_POOL]
    ga = proj[:, D_POOL:2 * D_POOL]
    sums = _window_sums(jnp.concatenate([halo, xa], axis=0), True)
    ga_act, ga_grad = _silu_parts(ga)
    pooled, pw, ya = [], [], []
    for g in range(N_GROUPS):
        sl = slice(g * GROUP, (g + 1) * GROUP)
        p = (sums[g] * inv_counts[g] - xa[:, sl]).astype(BF16)
        q = _dot(p, w.pool(g))
        pooled.append(p)
        pw.append(q)
        ya.append(q * w.pool_scale(g) * ga_act[:, sl])

    u = proj[:, 2 * D_POOL:2 * D_POOL + D_SGU]
    v = proj[:, 2 * D_POOL + D_SGU:2 * D_POOL + 2 * D_SGU]
    gb = proj[:, 2 * D_POOL + 2 * D_SGU:]
    gb_act, gb_grad = _silu_parts(gb)
    if cdf is None:
        cdf = jnp.concatenate([_normal_cdf(u), _normal_cdf(v)], axis=1)
    u_act, u_grad = _gelu_parts(u, cdf[:, :D_SGU], keep)
    v_act, v_grad = _gelu_parts(v, cdf[:, D_SGU:], keep)
    vn, vrstd, vln, mixed, yb = [], [], [], [], []
    for h in range(N_HEADS):
        sl = slice(h * HEAD, (h + 1) * HEAD)
        n_h, r_h = _ln(v_act[:, sl])
        l_h = (n_h * w.ln_gain(h) + w.ln_bias(h)).astype(BF16)
        w_h = w.mix(h)
        bias = w.mix_bias(h)
        m_h = jnp.concatenate(
            [_dot(w_h, l_h[k * CHUNK:(k + 1) * CHUNK]) + bias for k in range(TM // CHUNK)], axis=0)
        vn.append(n_h)
        vrstd.append(r_h)
        vln.append(l_h)
        mixed.append(m_h)
        yb.append(u_act[:, sl] * m_h * gb_act[:, sl])
    cat = jnp.concatenate(ya + yb, axis=1)
    if not keep:
        return cat, cdf
    return cat, dict(inv_counts=inv_counts, ga_act=ga_act, ga_grad=ga_grad, pooled=pooled, pw=pw, u_grad=u_grad,
                     v_grad=v_grad, u_act=u_act, gb_act=gb_act, gb_grad=gb_grad, vn=vn, vrstd=vrstd, vln=vln,
                     mixed=mixed)


def _const_spec(shape):
    nd = len(shape)
    return pl.BlockSpec(shape, lambda i: (0,) * nd)


VEC_LNG, VEC_LNB, VEC_POOL, VEC_SGU, VEC_SHIFT, VEC_SCALE, VEC_GATE, VEC_LOSS = range(8)


def _layer_backward(layer, a, b, x, proj, y, cdf, mod, w_int, w_outf, small, ln_g, is_last, name, carry=(),
                    reduce=()):
    n_red, n_carry = len(reduce), len(carry)
    base = layer * PACK_ROWS

    def body(a_ref, b_ref, x_ref, proj_ref, prev_ref, y_ref, cdf_ref, mod_ref, wint_ref, wout_ref, wpool_ref,
             pscale_ref, slng_ref, slnb_ref, wsgu_ref, bsgut_ref, lng_ref, *rest):
        weights = _MixWeights(layer, wpool_ref, pscale_ref, slng_ref, slnb_ref, wsgu_ref, bsgut_ref)
        carry_refs, rest = rest[:n_carry], rest[n_carry:]
        part_refs, rest = rest[:n_red], rest[n_red:]
        dx_ref, dproj_ref, h_ref, cat_ref, dy_ref, small_ref, dmod_ref, loss_ref = rest[:8]
        shard_refs, rest = rest[8:8 + n_red], rest[8 + n_red:]
        vec_ref, dmix_ref, halo_ref = rest[:3]
        step = pl.program_id(0)
        tile = N_TILES - 1 - step

        def scatter():
            bufs, sems = rest[3:3 + 5 * n_red], rest[3 + 5 * n_red:]
            arrays = [dict(part=part_refs[n], out=shard_refs[n], staged=True, stage=bufs[5 * n], sib=bufs[5 * n + 1],
                           snd=bufs[5 * n + 2], rcv=bufs[5 * n + 3], relay=bufs[5 * n + 4]) for n in range(n_red)]
            return _ChipReduceScatter(arrays, *sems)

        @pl.when(step == 0)
        def _():
            small_ref[...] = jnp.zeros_like(small_ref)
            dmod_ref[...] = jnp.zeros_like(dmod_ref)
            vec_ref[...] = jnp.zeros_like(vec_ref)
            dmix_ref[...] = jnp.zeros_like(dmix_ref)
            halo_ref[...] = jnp.zeros_like(halo_ref)
            if n_red:
                scatter().start()

        if n_red:
            @pl.when(step == 1)
            def _():
                scatter().exchange()

            @pl.when(step == N_TILES // 2)
            def _():
                scatter().fold()

        def acc(row, lo, val):
            hi = lo + val.shape[1]
            vec_ref[row:row + 1, lo:hi] += jnp.sum(val, axis=0, keepdims=True)

        xt = x_ref[...]
        yt = y_ref[...]
        shift = mod_ref[layer:layer + 1, 0:D_MODEL]
        scale = mod_ref[layer:layer + 1, D_MODEL:2 * D_MODEL]
        gate = mod_ref[layer:layer + 1, 2 * D_MODEL:]
        ln_gain = lng_ref[layer:layer + 1, :]

        zn, zrstd = _ln(ALPHA * xt + gate * yt)
        if is_last:
            diff = a_ref[...] - b_ref[...]
            acc(VEC_LOSS, 0, diff * diff)
            dout = diff * (1.0 / D_MODEL)
        else:
            dout = a_ref[...]
        acc(VEC_LNG, 0, dout * zn)
        acc(VEC_LNB, 0, dout)
        dz = _ln_bwd(dout * ln_gain, zn, zrstd)
        acc(VEC_GATE, 0, dz * yt)
        dy = (dz * gate).astype(BF16)
        dy_ref[...] = dy
        dcat = _dot_nt(dy, wout_ref[...])

        proj = proj_ref[...]
        prev = jnp.where(tile > 0, prev_ref[...], 0.0)
        cat, k = _mix_forward(proj, prev, tile, weights, cdf_ref[...])
        cat_ref[...] = cat.astype(BF16)

        dga, dq = [], []
        for g in range(N_GROUPS):
            sl = slice(g * GROUP, (g + 1) * GROUP)
            pscale = weights.pool_scale(g)
            dya = dcat[:, sl]
            dyp = dya * k["ga_act"][:, sl]
            dga.append(dya * k["pw"][g] * pscale * k["ga_grad"][:, sl])
            acc(VEC_POOL, g * GROUP, dyp * k["pw"][g])
            dpw = (dyp * pscale).astype(BF16)
            rows = pl.ds(base + ROW_WPOOL + g * GROUP, GROUP)
            small_ref[rows, :] += _dot_tn(k["pooled"][g], dpw)
            dq.append(_dot_nt(dpw, weights.pool(g)))
        dpooled = jnp.concatenate(dq, axis=1)
        scaled = jnp.concatenate([dq[g] * k["inv_counts"][g] for g in range(N_GROUPS)], axis=1)
        sums = _window_sums(jnp.concatenate([scaled, halo_ref[...]], axis=0), False)
        halo_ref[...] = scaled[0:HALO]
        dxa = jnp.concatenate(sums, axis=1) - dpooled

        du, dv, dgb = [], [], []
        for h in range(N_HEADS):
            sl = slice(h * HEAD, (h + 1) * HEAD)
            dyb = dcat[:, D_POOL + h * HEAD:D_POOL + (h + 1) * HEAD]
            m_h = k["mixed"][h]
            ug = k["u_act"][:, sl] * dyb
            du.append(dyb * m_h * k["gb_act"][:, sl] * k["u_grad"][:, sl])
            dgb.append(ug * m_h * k["gb_grad"][:, sl])
            dmixed = ug * k["gb_act"][:, sl]
            dmixed_bf = dmixed.astype(BF16)
            w_h = weights.mix(h)
            dvln_parts = []
            dmix_sum = dmix_ref[h]
            wsgu_rows = pl.ds(base + ROW_WSGU + h * CHUNK, CHUNK)
            dws = small_ref[wsgu_rows, :]
            for c in range(TM // CHUNK):
                cs = slice(c * CHUNK, (c + 1) * CHUNK)
                dmix_sum = dmix_sum + dmixed[cs]
                dws = dws + _dot_nt(dmixed_bf[cs], k["vln"][h][cs])
                dvln_parts.append(_dot_tn(w_h, dmixed_bf[cs]))
            dmix_ref[h] = dmix_sum
            small_ref[wsgu_rows, :] = dws
            dvln = jnp.concatenate(dvln_parts, axis=0)
            acc(VEC_SGU, h * HEAD, dvln * k["vn"][h])
            acc(VEC_SGU, D_SGU + h * HEAD, dvln)
            dvv = _ln_bwd(dvln * weights.ln_gain(h), k["vn"][h], k["vrstd"][h])
            dv.append(dvv * k["v_grad"][:, sl])

        dproj = jnp.concatenate([dxa] + dga + du + dv + dgb, axis=1).astype(BF16)
        dproj_ref[...] = dproj
        dh = _dot(dproj, wint_ref[...])

        xn, xrstd = _ln(xt)
        h_ref[...] = (xn * (1.0 + scale) + shift).astype(BF16)
        acc(VEC_SCALE, 0, dh * xn)
        acc(VEC_SHIFT, 0, dh)
        dx_ref[...] = _ln_bwd(dh * (1.0 + scale), xn, xrstd) + ALPHA * dz

        @pl.when(step == N_TILES - 1)
        def _():
            def put(row0, vec_row, lo, n):
                for r in range(n):
                    small_ref[base + row0 + r:base + row0 + r + 1, :] = (
                        vec_ref[vec_row:vec_row + 1, lo + r * 128:lo + (r + 1) * 128])

            put(ROW_PSCALE, VEC_POOL, 0, 4)
            put(ROW_SLNG, VEC_SGU, 0, 4)
            put(ROW_SLNB, VEC_SGU, D_SGU, 4)
            put(ROW_LNG, VEC_LNG, 0, 8)
            put(ROW_LNB, VEC_LNB, 0, 8)
            ones = jnp.ones((8, HEAD), F32)
            t = lax.broadcasted_iota(jnp.int32, (CHUNK, CHUNK), 0)
            s = lax.broadcasted_iota(jnp.int32, (CHUNK, CHUNK), 1)
            for h in range(N_HEADS):
                bias_rows = lax.dot_general(ones, dmix_ref[h], (((1,), (1,)), ((), ())),
                                            preferred_element_type=F32, precision=lax.Precision.HIGHEST)
                small_ref[base + ROW_BSGU + h:base + ROW_BSGU + h + 1, :] = bias_rows[0:1]
                rows = pl.ds(base + ROW_WSGU + h * CHUNK, CHUNK)
                small_ref[rows, :] = jnp.where(t >= s, small_ref[rows, :], 0.0)
            pieces = ((0, VEC_SHIFT, 0, 768),
                      (1, VEC_SHIFT, 768, 256), (1, VEC_SCALE, 0, 512),
                      (2, VEC_SCALE, 512, 512), (2, VEC_GATE, 0, 256),
                      (3, VEC_GATE, 256, 768))
            filled = [0] * 4
            for q, vec_row, lo, n in pieces:
                row = 4 * layer + q
                dmod_ref[row:row + 1, filled[q]:filled[q] + n] = vec_ref[vec_row:vec_row + 1, lo:lo + n]
                filled[q] += n
            if n_carry:
                for other in range(layer + 1, DEPTH):
                    rows = pl.ds(other * PACK_ROWS, PACK_ROWS)
                    small_ref[rows, :] = carry_refs[0][rows, :]
                    dmod_ref[4 * other:4 * other + 4, :] = carry_refs[1][4 * other:4 * other + 4, :]
            loss_ref[...] = vec_ref[VEC_LOSS:VEC_LOSS + 1, :]
            if n_red:
                scatter().finish()
                scatter().wait_sends()

    rev = lambda w: pl.BlockSpec((TM, w), lambda i: (N_TILES - 1 - i, 0))
    prev_spec = pl.BlockSpec(
        (HALO, D_POOL), lambda i: (jnp.maximum((N_TILES - 1 - i) * (TM // HALO) - 1, 0), 0))
    comm_scratch = []
    for p in reduce:
        comm_scratch += _ChipReduceScatter.buffers(p.shape[2], p.shape[3], p.dtype)
    if n_red:
        comm_scratch += _ChipReduceScatter.semaphores(n_red)
    return pl.pallas_call(
        body,
        name=name,
        grid=(N_TILES,),
        in_specs=[rev(D_MODEL), rev(D_MODEL) if is_last else pl.BlockSpec((TM, D_MODEL), lambda i: (0, 0)),
                  rev(D_MODEL), rev(D_IN), prev_spec, rev(D_MODEL), rev(2 * D_SGU),
                  _const_spec((DEPTH, 3 * D_MODEL)), _const_spec((D_IN, D_MODEL)), _const_spec((D_MODEL, D_MODEL))]
                 + [_const_spec(s) for s in SMALL_SPECS] + [_const_spec((DEPTH, D_MODEL))]
                 + [_const_spec(c.shape) for c in carry] + [ANY] * n_red,
        out_specs=[rev(D_MODEL), rev(D_IN), rev(D_MODEL), rev(D_MODEL), rev(D_MODEL),
                   _const_spec((DEPTH * PACK_ROWS, 128)), _const_spec((8, DMOD_COLS)), _const_spec((1, D_MODEL))]
                  + [_const_spec(p.shape[2:]) for p in reduce],
        out_shape=[jax.ShapeDtypeStruct((SEQ, D_MODEL), F32), jax.ShapeDtypeStruct((SEQ, D_IN), BF16),
                   jax.ShapeDtypeStruct((SEQ, D_MODEL), BF16), jax.ShapeDtypeStruct((SEQ, D_MODEL), BF16),
                   jax.ShapeDtypeStruct((SEQ, D_MODEL), BF16), jax.ShapeDtypeStruct((DEPTH * PACK_ROWS, 128), F32),
                   jax.ShapeDtypeStruct((8, DMOD_COLS), F32), jax.ShapeDtypeStruct((1, D_MODEL), F32)]
                  + [jax.ShapeDtypeStruct(p.shape[2:], F32) for p in reduce],
        scratch_shapes=[pltpu.VMEM((8, D_MODEL), F32), pltpu.VMEM((N_HEADS, CHUNK, HEAD), F32),
                        pltpu.VMEM((HALO, D_POOL), F32)] + comm_scratch,
        compiler_params=pltpu.CompilerParams(dimension_semantics=("arbitrary",), vmem_limit_bytes=VMEM_LIMIT),
    )(a, b, x, proj, proj, y, cdf, mod, w_int, w_outf, *small, ln_g, *carry, *reduce)


def _grad_matmuls(dproj, h, cat, dy, name):
    in_cols, out_cols = D_IN // 4, D_MODEL // 2
    in_steps = D_IN // in_cols

    def body(dproj_ref, h_ref, cat_ref, dy_ref, gin_ref, gout_ref):
        step = pl.program_id(0)

        @pl.when(step < in_steps)
        def _():
            gin_ref[...] = _dot_tn(dproj_ref[...], h_ref[...]).astype(BF16)

        @pl.when(step >= in_steps)
        def _():
            gout_ref[...] = _dot_tn(cat_ref[...], dy_ref[...]).astype(BF16)

    in_block = lambda j: jnp.minimum(j, in_steps - 1)
    out_block = lambda j: jnp.maximum(j - in_steps, 0)
    return pl.pallas_call(
        body,
        name=name,
        grid=(in_steps + D_MODEL // out_cols,),
        in_specs=[pl.BlockSpec((SEQ, in_cols), lambda j: (0, in_block(j))), _const_spec((SEQ, D_MODEL)),
                  pl.BlockSpec((SEQ, out_cols), lambda j: (0, out_block(j))), _const_spec((SEQ, D_MODEL))],
        out_specs=[pl.BlockSpec((in_cols, D_MODEL), lambda j: (in_block(j), 0)),
                   pl.BlockSpec((out_cols, D_MODEL), lambda j: (out_block(j), 0))],
        out_shape=[jax.ShapeDtypeStruct((D_IN, D_MODEL), BF16), jax.ShapeDtypeStruct((D_MODEL, D_MODEL), BF16)],
        compiler_params=pltpu.CompilerParams(dimension_semantics=("arbitrary",), vmem_limit_bytes=VMEM_LIMIT),
    )(dproj, h, cat, dy)


def _adamw_math(w, g, m, v):
    m = ADAM_B1 * m + (1.0 - ADAM_B1) * g
    v = ADAM_B2 * v + (1.0 - ADAM_B2) * (g * g)
    m_hat = m / (1.0 - ADAM_B1 ** ADAM_STEP)
    v_hat = v / (1.0 - ADAM_B2 ** ADAM_STEP)
    delta = -ADAM_LR * (m_hat / (jnp.sqrt(v_hat) + ADAM_EPS) + ADAM_WD * w)
    return delta, m, v


def _adamw(w, grads, m, v, block_rows, name):
    rows, cols = grads[0].shape
    blocks = rows // block_rows

    def body(w_ref, m_ref, v_ref, *rest):
        g_refs, (g_ref, d_ref, nm_ref, nv_ref) = rest[:DEPTH], rest[DEPTH:]
        for layer in range(DEPTH):
            @pl.when(pl.program_id(0) == layer)
            def _():
                g = g_refs[layer][...]
                g_ref[...] = g
                d_ref[...], nm_ref[...], nv_ref[...] = _adamw_math(w_ref[...], g, m_ref[...], v_ref[...])

    def grad_spec(layer):
        return pl.BlockSpec((block_rows, cols),
                            lambda l, i: (jnp.where(l == layer, i, jnp.where(l < layer, 0, blocks - 1)), 0))

    spec = pl.BlockSpec((block_rows, cols), lambda l, i: (l * blocks + i, 0))
    return pl.pallas_call(
        body,
        name=name,
        grid=(DEPTH, blocks),
        in_specs=[spec] * 3 + [grad_spec(layer) for layer in range(DEPTH)],
        out_specs=[spec] * 4,
        out_shape=[jax.ShapeDtypeStruct(w.shape, F32)] * 4,
        compiler_params=pltpu.CompilerParams(dimension_semantics=("arbitrary", "arbitrary"),
                                             vmem_limit_bytes=VMEM_LIMIT),
    )(w, m, v, *grads)


MESH = pl.DeviceIdType.MESH
SIBLING = 1
ANY = pl.BlockSpec(memory_space=pl.ANY)
VMEM = pl.BlockSpec(memory_space=pltpu.VMEM)


def _me():
    return lax.axis_index("x"), lax.axis_index("y"), lax.axis_index("c")


def _peer(r):
    x, y, c = _me()
    return (1 - x if r & 4 else x, 1 - y if r & 2 else y, 1 - c if r & 1 else c)


def _index(dev):
    return 4 * dev[0] + 2 * dev[1] + dev[2]


def _remote(src, dst, send_sem, recv_sem, dev):
    return pltpu.make_async_remote_copy(src_ref=src, dst_ref=dst, send_sem=send_sem, recv_sem=recv_sem,
                                        device_id=dev, device_id_type=MESH)


ACROSS_X, ACROSS_Y, ACROSS_BOTH = 4, 2, 6
GATHER_SEMS = 11


class _TwoLevelGather:
    def __init__(self, out, send_sems, recv_sems, src=None):
        self.out, self.send_sems, self.recv_sems, self.src = out, send_sems, recv_sems, src
        self.rows = (out.shape[0] // N_DEV) if len(out.shape) == 2 else out.shape[1]
        self.half = self.rows // 2

    def _slot(self, block):
        if len(self.out.shape) == 2:
            return self.out.at[pl.ds(pl.multiple_of(_index(block) * self.rows, self.rows), self.rows)]
        return self.out.at[_index(block)]

    def _copy(self, k, block, part, to, src=None):
        slot = self._slot(block)
        if part is not None:
            rows = pl.ds(part * self.half, self.half)
            slot = slot.at[rows]
            src = None if src is None else src.at[rows]
        return _remote(slot if src is None else src, slot, self.send_sems.at[k], self.recv_sems.at[k], to)

    def _mine(self):
        me = _me()
        src = self._slot(me) if self.src is None else self.src
        x, y = _peer(ACROSS_X), _peer(ACROSS_Y)
        return [self._copy(1, me, 0, x, src), self._copy(3, me, 1, y, src), self._copy(0, me, None, _peer(SIBLING), src),
                self._copy(2, me, 1, x, src), self._copy(4, me, 0, y, src)]

    def _relayed(self):
        return [self._copy(5, _peer(ACROSS_X), 0, _peer(ACROSS_Y)), self._copy(6, _peer(ACROSS_Y), 1, _peer(ACROSS_X))]

    def _passed(self):
        sib, far = _peer(SIBLING), _peer(ACROSS_BOTH)
        return [self._copy(7, _peer(ACROSS_X), None, sib), self._copy(8, _peer(ACROSS_Y), None, sib),
                self._copy(9, far, 0, sib), self._copy(10, far, 1, sib)]

    def _arrival(self, k, r, part):
        return self._copy(k, _peer(r), part, _me())

    def send_first(self):
        for cp in self._mine()[:3]:
            cp.start()

    def send_second(self):
        for cp in self._mine()[3:]:
            cp.start()

    def send_mine(self):
        self.send_first()
        self.send_second()

    def relay(self):
        relayed = self._relayed()
        self._arrival(1, ACROSS_X, 0).wait_recv()
        relayed[0].start()
        self._arrival(3, ACROSS_Y, 1).wait_recv()
        relayed[1].start()

    def pass_near(self):
        passed = self._passed()
        self._arrival(2, ACROSS_X, 1).wait_recv()
        passed[0].start()
        self._arrival(4, ACROSS_Y, 0).wait_recv()
        passed[1].start()

    def pass_far(self):
        passed = self._passed()
        self._arrival(5, ACROSS_BOTH, 0).wait_recv()
        passed[2].start()
        self._arrival(6, ACROSS_BOTH, 1).wait_recv()
        passed[3].start()

    def pass_on(self):
        self.pass_near()
        self.pass_far()

    def wait_sibling(self):
        self._arrival(0, SIBLING, None).wait_recv()

    def wait_passed(self, r):
        if r == ACROSS_BOTH:
            self._arrival(9, r ^ SIBLING, 0).wait_recv()
            self._arrival(10, r ^ SIBLING, 1).wait_recv()
        else:
            self._arrival(7 if r == ACROSS_X else 8, r ^ SIBLING, None).wait_recv()

    def wait_rest(self):
        self.wait_sibling()
        for r in (ACROSS_X, ACROSS_Y, ACROSS_BOTH):
            self.wait_passed(r)

    def wait_sends(self):
        for cp in self._mine() + self._relayed() + self._passed():
            cp.wait_send()


class _ChipReduceScatter:
    SLOTS = 6

    def __init__(self, arrays, l_sem, d_send, d_recv, i_send, i_recv):
        self.arrays = arrays
        self.l_sem, self.d_send, self.d_recv, self.i_send, self.i_recv = l_sem, d_send, d_recv, i_send, i_recv

    @staticmethod
    def buffers(rows, cols, dtype, staged=True):
        stage = [pltpu.VMEM((4, rows, cols), dtype)] if staged else []
        return stage + [pltpu.VMEM((4, rows, cols), dtype), pltpu.VMEM((3, rows, cols), dtype),
                        pltpu.VMEM((2, rows, cols), dtype), pltpu.VMEM((2, rows // 2, cols), dtype)]

    @classmethod
    def semaphores(cls, n):
        return [pltpu.SemaphoreType.DMA((n,)), pltpu.SemaphoreType.DMA((n, 4)), pltpu.SemaphoreType.DMA((n, 4)),
                pltpu.SemaphoreType.DMA((n, cls.SLOTS)), pltpu.SemaphoreType.DMA((n, cls.SLOTS))]

    def _pick(self, which):
        return list(enumerate(self.arrays)) if which is None else [(n, self.arrays[n]) for n in which]

    @staticmethod
    def _chip(r):
        dev = _me() if r is None else _peer(r)
        return 2 * dev[0] + dev[1]

    def _staging(self, which):
        c = _me()[2]
        return [pltpu.make_async_copy(a["part"].at[pl.ds(0, 4), c], a["stage"], self.l_sem.at[n])
                for n, a in self._pick(which) if a["staged"]]

    def _first(self, which, chip):
        other = 1 - _me()[2]
        return [_remote(a["part"].at[chip, other], a["sib"].at[chip], self.d_send.at[n, chip], self.d_recv.at[n, chip],
                        _peer(SIBLING)) for n, a in self._pick(which)]

    @staticmethod
    def _halves(a):
        half = a["rcv"].shape[1] // 2
        return pl.ds(0, half), pl.ds(half, half)

    def _hops(self, n, a):
        h0, h1 = self._halves(a)
        x, y = _peer(ACROSS_X), _peer(ACROSS_Y)
        snd, rcv, relay = a["snd"], a["rcv"], a["relay"]
        pairs = [(snd.at[2, h0], relay.at[0], x), (snd.at[2, h1], relay.at[1], y),
                 (snd.at[0, h0], rcv.at[0, h0], x), (snd.at[0, h1], rcv.at[0, h1], x),
                 (snd.at[1, h1], rcv.at[1, h1], y), (snd.at[1, h0], rcv.at[1, h0], y)]
        return [_remote(s, d, self.i_send.at[n, k], self.i_recv.at[n, k], to) for k, (s, d, to) in enumerate(pairs)]

    def _mine(self, a, chip, rows=None):
        src = a["stage"].at[chip] if a["staged"] else a["part"].at[chip, _me()[2]]
        mine, sib = (src[...], a["sib"][chip]) if rows is None else (src[rows, :], a["sib"][chip, rows, :])
        return mine.astype(F32) + sib.astype(F32)

    def start(self, which=None, chips=None):
        if chips is None:
            for cp in self._staging(which):
                cp.start()
        for chip in range(4) if chips is None else chips:
            for cp in self._first(which, chip):
                cp.start()

    def send_far(self, which=None):
        far = self._chip(ACROSS_BOTH)
        for cp in self._staging(which):
            cp.wait()
        for cp in self._first(which, far):
            cp.wait_recv()
        for n, a in self._pick(which):
            hops = self._hops(n, a)
            a["snd"][2] = self._mine(a, far).astype(a["snd"].dtype)
            hops[0].start()
            hops[1].start()

    def send_near(self, r, which=None):
        chip = self._chip(r)
        for cp in self._first(which, chip):
            cp.wait_recv()
        for n, a in self._pick(which):
            h0, h1 = self._halves(a)
            hops = self._hops(n, a)
            if r == ACROSS_X:
                a["snd"][0, h0, :] = self._mine(a, chip, h0).astype(a["snd"].dtype)
                hops[2].start()
            else:
                a["snd"][1, h1, :] = self._mine(a, chip, h1).astype(a["snd"].dtype)
                hops[4].start()

    def exchange(self, which=None):
        self.send_far(which)
        self.send_near(ACROSS_X, which)
        self.send_near(ACROSS_Y, which)

    def fold(self, which=None):
        across_x, across_y = self._chip(ACROSS_X), self._chip(ACROSS_Y)
        for n, a in self._pick(which):
            h0, h1 = self._halves(a)
            hops = self._hops(n, a)
            dtype = a["snd"].dtype
            hops[1].wait_recv()
            a["snd"][0, h1, :] = (self._mine(a, across_x, h1) + a["relay"][1].astype(F32)).astype(dtype)
            hops[3].start()
            hops[0].wait_recv()
            a["snd"][1, h0, :] = (self._mine(a, across_y, h0) + a["relay"][0].astype(F32)).astype(dtype)
            hops[5].start()

    def finish(self, which=None):
        home = self._chip(None)
        for cp in self._first(which, home):
            cp.wait_recv()
        for n, a in self._pick(which):
            hops = self._hops(n, a)
            a["out"][...] = self._mine(a, home)
            hops[2].wait_recv()
            hops[3].wait_recv()
            a["out"][...] += a["rcv"][0].astype(F32)
            hops[4].wait_recv()
            hops[5].wait_recv()
            a["out"][...] += a["rcv"][1].astype(F32)

    def wait_sends(self, which=None):
        for chip in range(4):
            for cp in self._first(which, chip):
                cp.wait_send()
        for n, a in self._pick(which):
            for cp in self._hops(n, a):
                cp.wait_send()


def _direct_exchange(src_of, dst_of, send_sems, recv_sems):
    me = _me()
    copies = [_remote(src_of(_peer(r)), dst_of(me), send_sems.at[r - 1], recv_sems.at[r - 1], _peer(r))
              for r in range(1, N_DEV)]
    for cp in copies:
        cp.start()
    return copies


def _wait_direct(copies):
    for cp in copies:
        cp.wait_recv()
    for cp in copies:
        cp.wait_send()


def _forward(x, c, w_ada, b_ada, small, ln_g, ln_b, mine, following):
    assert DEPTH == 2
    cols = w_ada.shape[2]
    shard = mine[0].shape[0]
    pair = 2 * shard

    def body(x_hbm, c_ref, wada_hbm, bada_ref, wpool_ref, pscale_ref, slng_ref, slnb_ref, wsgu_ref, bsgut_ref,
             lng_ref, lnb_ref, wint_hbm, wout_hbm, next_in_hbm, next_out_hbm,
             out0_ref, y0_ref, cdf0_ref, proj0_hbm, out1_ref, y1_ref, cdf1_ref, proj1_hbm,
             wint_keep, wout_keep, wint_next, wout_next, acts_ref, mod_ref,
             wint_v, wout_v, h_buf, proj_blk, proj_tile, halo_ref, x_ref, w_send, w_recv, w_local, p_sems,
             t_sems, in_sems, act_all, act_src, part, mod_recv, wada_ref, a_send, a_recv, m_send, m_recv,
             n_send, n_recv, n_local, f_sems):
        step = pl.program_id(0)
        chip_of = lambda dev: 2 * dev[0] + dev[1]

        def hosted():
            return [_TwoLevelGather(out, n_send.at[n], n_recv.at[n], src=src)
                    for n, (out, src) in enumerate(((wint_next, next_in_hbm), (wout_next, next_out_hbm)))]

        def hosted_own():
            me = _me()
            return [pltpu.make_async_copy(g.src, g._slot(me), n_local.at[n]) for n, g in enumerate(hosted())]

        def gathers():
            return (_TwoLevelGather(wint_v, w_send.at[0], w_recv.at[0], src=wint_hbm),
                    _TwoLevelGather(wout_v, w_send.at[1], w_recv.at[1], src=wout_hbm))

        def keeps():
            return [pltpu.make_async_copy(wint_v, wint_keep, w_local.at[2]),
                    pltpu.make_async_copy(wout_v, wout_keep, w_local.at[3])]

        def tile_read(t):
            slot = t % 2
            return pltpu.make_async_copy(proj0_hbm.at[pl.ds(pl.multiple_of(t * TM, TM), TM)], proj_tile.at[slot],
                                         t_sems.at[slot])

        def first_layer_start():
            writes = []

            def project(n, dev):
                first = pl.multiple_of(chip_of(dev) * pair, pair)
                if n >= 2:
                    writes[n - 2].wait()

                @pl.loop(0, N_TILES)
                def _(t):
                    rows = pl.ds(pl.multiple_of(t * TM, TM), TM)
                    proj_blk[n % 2, rows, :] = _dot_nt(h_buf[rows, :], wint_v[pl.ds(first, pair), :])

                cp = pltpu.make_async_copy(proj_blk.at[n % 2], proj0_hbm.at[:, pl.ds(first, pair)], p_sems.at[n % 2])
                cp.start()
                writes.append(cp)

            me = _me()
            halo_ref[...] = jnp.zeros_like(halo_ref)
            gather_in, gather_out = gathers()
            own_in = pltpu.make_async_copy(wint_hbm, gather_in._slot(me), w_local.at[0])
            own_out = pltpu.make_async_copy(wout_hbm, gather_out._slot(me), w_local.at[1])
            x_load = pltpu.make_async_copy(x_hbm, x_ref, in_sems.at[0])
            x_load.start()
            wada_load = pltpu.make_async_copy(wada_hbm, wada_ref, in_sems.at[1])
            wada_load.start()
            mine_index = _index(me)
            cval = c_ref[...]
            act_src[...] = jnp.zeros_like(act_src)
            act_src[0:1, :] = cval * jax.nn.sigmoid(cval)
            act_all[mine_index] = act_src[...]
            act_copies = _direct_exchange(lambda p: act_src, lambda m: act_all.at[_index(m)], a_send, a_recv)

            own_in.start()
            own_out.start()
            gather_in.send_first()

            _wait_direct(act_copies)
            acts = jnp.concatenate([act_all[j, 0:1, :] for j in range(N_DEV)], axis=0)
            acts_ref[...] = acts
            part[...] = jnp.zeros_like(part)
            wada_load.wait()
            for l in range(DEPTH):
                res = lax.dot_general(acts, wada_ref[l], (((1,), (0,)), ((), ())), preferred_element_type=F32,
                                      precision=lax.Precision.HIGHEST)
                for b in range(N_DEV):
                    part[b, l:l + 1, :] = res[b:b + 1, :]
            mod_recv[mine_index] = part[mine_index]
            mod_copies = _direct_exchange(lambda p: part.at[_index(p)], lambda m: mod_recv.at[_index(m)],
                                          m_send, m_recv)
            gather_in.send_second()
            gather_out.send_mine()

            _wait_direct(mod_copies)
            for l in range(DEPTH):
                for j in range(N_DEV):
                    sl = slice(j * cols, (j + 1) * cols)
                    mod_ref[l:l + 1, sl] = mod_recv[j, l:l + 1, :] + bada_ref[l:l + 1, sl]
            x_load.wait()
            shift = mod_ref[0:1, 0:D_MODEL]
            scale = mod_ref[0:1, D_MODEL:2 * D_MODEL]

            @pl.loop(0, N_TILES)
            def _(t):
                rows = pl.ds(pl.multiple_of(t * TM, TM), TM)
                xn, _ = _ln(x_ref[rows, :])
                h_buf[rows, :] = (xn * (1.0 + scale) + shift).astype(BF16)

            gather_in.relay()
            own_in.wait()
            gather_in.wait_sibling()
            gather_in.pass_near()
            project(0, me)
            gather_in.wait_passed(ACROSS_X)
            project(1, _peer(ACROSS_X))
            gather_out.relay()
            for cp in hosted_own():
                cp.start()
            for g in hosted():
                g.send_mine()
            gather_in.wait_passed(ACROSS_Y)
            project(2, _peer(ACROSS_Y))
            gather_in.pass_far()
            gather_in.wait_passed(ACROSS_BOTH)
            project(3, _peer(ACROSS_BOTH))

            gather_out.pass_on()
            gather_out.wait_rest()
            own_out.wait()
            for cp in keeps():
                cp.start()
            writes[2].wait()
            writes[3].wait()
            tile_read(0).start()

        def fetches():
            return [pltpu.make_async_copy(wint_next, wint_v, f_sems.at[0]),
                    pltpu.make_async_copy(wout_next, wout_v, f_sems.at[1])]

        def tile_write(t):
            slot = t % 2
            return pltpu.make_async_copy(proj_tile.at[slot], proj1_hbm.at[pl.ds(pl.multiple_of(t * TM, TM), TM)],
                                         t_sems.at[slot])

        def mix_and_close(layer, tile, rows, proj, out_ref, y_ref, cdf_ref):
            weights = _MixWeights(layer, wpool_ref, pscale_ref, slng_ref, slnb_ref, wsgu_ref, bsgut_ref)
            xt = x_ref[rows, :]
            gate = mod_ref[layer:layer + 1, 2 * D_MODEL:]
            cat, cdf_ref[...] = _mix_forward(proj, halo_ref[...], tile, weights)
            halo_ref[...] = proj[TM - HALO:, 0:D_POOL]
            if layer == 1:
                @pl.when(tile == 0)
                def _():
                    fetches()[1].wait()
            y = _dot(cat.astype(BF16), wout_v[...])
            y_ref[...] = y
            zn, _ = _ln(ALPHA * xt + gate * y)
            out = zn * lng_ref[layer:layer + 1, :] + lnb_ref[layer:layer + 1, :]
            out_ref[...] = out
            if layer == 0:
                x_ref[rows, :] = out

        def first_layer_tile(tile):
            @pl.when(tile + 1 < N_TILES)
            def _():
                tile_read(tile + 1).start()

            tile_read(tile).wait()
            rows = pl.ds(pl.multiple_of(tile * TM, TM), TM)
            mix_and_close(0, tile, rows, proj_tile[tile % 2], out0_ref, y0_ref, cdf0_ref)

        def second_layer_tile(tile):
            rows = pl.ds(pl.multiple_of(tile * TM, TM), TM)
            shift = mod_ref[1:2, 0:D_MODEL]
            scale = mod_ref[1:2, D_MODEL:2 * D_MODEL]
            xn, _ = _ln(x_ref[rows, :])
            h = (xn * (1.0 + scale) + shift).astype(BF16)

            @pl.when(tile >= 2)
            def _():
                tile_write(tile - 2).wait()

            proj_tile[tile % 2] = _dot_nt(h, wint_v[...])
            tile_write(tile).start()
            mix_and_close(1, tile, rows, proj_tile[tile % 2], out1_ref, y1_ref, cdf1_ref)

        @pl.when(step < N_TILES)
        def _():
            @pl.when(step == 0)
            def _():
                first_layer_start()

            @pl.when(step == 1)
            def _():
                for g in hosted():
                    g.relay()

            @pl.when(step == N_TILES // 2)
            def _():
                for g in hosted():
                    g.pass_near()

            @pl.when(step == N_TILES - 1)
            def _():
                for g in hosted():
                    g.pass_far()
                for g in gathers():
                    g.wait_sends()
                for cp in keeps() + hosted_own():
                    cp.wait()
                hosted()[0].wait_rest()
                fetches()[0].start()

            first_layer_tile(step)

        @pl.when(step >= N_TILES)
        def _():
            @pl.when(step == N_TILES)
            def _():
                halo_ref[...] = jnp.zeros_like(halo_ref)
                hosted()[1].wait_rest()
                fetches()[1].start()
                fetches()[0].wait()

            second_layer_tile(step - N_TILES)

            @pl.when(step == 2 * N_TILES - 1)
            def _():
                for g in hosted():
                    g.wait_sends()
                tile_write(N_TILES - 2).wait()
                tile_write(N_TILES - 1).wait()

    first = lambda w: pl.BlockSpec((TM, w), lambda i: (jnp.minimum(i, N_TILES - 1), 0))
    second = lambda w: pl.BlockSpec((TM, w), lambda i: (jnp.maximum(i - N_TILES, 0), 0))
    gather_sems = pltpu.SemaphoreType.DMA((2, GATHER_SEMS))
    seven = pltpu.SemaphoreType.DMA((7,))
    per_layer = [jax.ShapeDtypeStruct((SEQ, D_MODEL), F32), jax.ShapeDtypeStruct((SEQ, D_MODEL), F32),
                 jax.ShapeDtypeStruct((SEQ, 2 * D_SGU), F32), jax.ShapeDtypeStruct((SEQ, D_IN), F32)]
    gathered = [jax.ShapeDtypeStruct((D_IN, D_MODEL), BF16), jax.ShapeDtypeStruct((D_MODEL, D_MODEL), BF16)]
    res = pl.pallas_call(
        body,
        name="layers_fwd",
        grid=(DEPTH * N_TILES,),
        in_specs=[ANY, _const_spec(c.shape), ANY, _const_spec(b_ada.shape)] + [_const_spec(s) for s in SMALL_SPECS]
                 + [_const_spec((DEPTH, D_MODEL)), _const_spec((DEPTH, D_MODEL))] + [ANY] * 4,
        out_specs=[first(D_MODEL), first(D_MODEL), first(2 * D_SGU), ANY,
                   second(D_MODEL), second(D_MODEL), second(2 * D_SGU), ANY] + [ANY] * 4
                  + [_const_spec((N_DEV, D_MODEL)), _const_spec((DEPTH, 3 * D_MODEL))],
        out_shape=per_layer * 2 + gathered * 2 + [jax.ShapeDtypeStruct((N_DEV, D_MODEL), F32),
                                                  jax.ShapeDtypeStruct((DEPTH, 3 * D_MODEL), F32)],
        scratch_shapes=[pltpu.VMEM((D_IN, D_MODEL), BF16), pltpu.VMEM((D_MODEL, D_MODEL), BF16),
                        pltpu.VMEM((SEQ, D_MODEL), BF16), pltpu.VMEM((2, SEQ, pair), F32),
                        pltpu.VMEM((2, TM, D_IN), F32), pltpu.VMEM((HALO, D_POOL), F32),
                        pltpu.VMEM((SEQ, D_MODEL), F32),
                        gather_sems, gather_sems, pltpu.SemaphoreType.DMA((4,)), pltpu.SemaphoreType.DMA((2,)),
                        pltpu.SemaphoreType.DMA((2,)), pltpu.SemaphoreType.DMA((2,)),
                        pltpu.VMEM((N_DEV, 8, D_MODEL), F32), pltpu.VMEM((8, D_MODEL), F32),
                        pltpu.VMEM((N_DEV, 8, cols), F32), pltpu.VMEM((N_DEV, 8, cols), F32),
                        pltpu.VMEM(w_ada.shape, F32), seven, seven, seven, seven,
                        gather_sems, gather_sems, pltpu.SemaphoreType.DMA((2,)), pltpu.SemaphoreType.DMA((2,))],
        compiler_params=pltpu.CompilerParams(dimension_semantics=("arbitrary",), vmem_limit_bytes=VMEM_LIMIT),
    )(x, c, w_ada, b_ada, *small, ln_g, ln_b, *mine, *following)
    return res[0:4], res[4:8], res[8:10], res[10:12], res[12], res[13]


ADA_CHUNK = 256


def _grad_tail(dproj, h, cat, dy, small, dmod8, loss_lanes, w_ada, m_ada, v_ada, act_t, b_ada, m_bada, v_bada):
    shard_in, shard_out, shard_small = D_IN // N_DEV, D_MODEL // N_DEV, small.shape[2]
    cols = w_ada.shape[2]
    W_IN, W_OUT, SMALL = 0, 1, 2

    def body(dproj_hbm, h_hbm, cat_hbm, dy_hbm, small_hbm, dmod_ref, lanes_ref, wada_hbm, mada_hbm, vada_hbm,
             act_ref, bada_ref, mbada_ref, vbada_ref,
             gwin_ref, gwout_ref, stot_ref, loss_ref, gada_hbm, dada_hbm, nmada_hbm, nvada_hbm,
             gb_ref, db_ref, nmb_ref, nvb_ref,
             dproj_v, h_v, cat_v, dy_v, part_in, part_out, own_small, loss_src, loss_all, dmod_all, ada_in, ada_out,
             *rest):
        bufs, rest = rest[:13], rest[13:]
        load_sems, rs_sems = rest[0], rest[1:6]
        m_send, m_recv, g_send, g_recv, s_send, s_recv, ada_lsem, ada_ssem = rest[6:]
        mine = _index(_me())

        def update_ada():
            upper = (mine % 2) == 1

            def dmod_of(layer):
                rows = []
                for b in range(N_DEV):
                    r = dmod_all[b, pl.ds(4 * layer + mine // 2, 1), :]
                    rows.append(jnp.where(upper, r[:, cols:], r[:, :cols]))
                return jnp.concatenate(rows, axis=0)

            chunks = [(layer, c) for layer in range(DEPTH) for c in range(D_MODEL // ADA_CHUNK)]

            def loads(i):
                layer, c = chunks[i]
                rows = pl.ds(c * ADA_CHUNK, ADA_CHUNK)
                return [pltpu.make_async_copy(src.at[layer, rows], ada_in.at[i % 2, k], ada_lsem.at[i % 2, k])
                        for k, src in enumerate((wada_hbm, mada_hbm, vada_hbm))]

            def stores(i):
                layer, c = chunks[i]
                rows = pl.ds(c * ADA_CHUNK, ADA_CHUNK)
                return [pltpu.make_async_copy(ada_out.at[i % 2, k], dst.at[layer, rows], ada_ssem.at[i % 2, k])
                        for k, dst in enumerate((gada_hbm, dada_hbm, nmada_hbm, nvada_hbm))]

            for cp in loads(0):
                cp.start()
            dmods = {}
            for i, (layer, c) in enumerate(chunks):
                if i + 1 < len(chunks):
                    for cp in loads(i + 1):
                        cp.start()
                for cp in loads(i):
                    cp.wait()
                if i >= 2:
                    for cp in stores(i - 2):
                        cp.wait()
                if layer not in dmods:
                    dmods[layer] = dmod_of(layer)
                act = act_ref[pl.ds(c * ADA_CHUNK, ADA_CHUNK), :]
                g = act[:, 0:1] * dmods[layer][0:1, :]
                for b in range(1, N_DEV):
                    g = g + act[:, b:b + 1] * dmods[layer][b:b + 1, :]
                slot = i % 2
                delta, new_m, new_v = _adamw_math(ada_in[slot, 0], g, ada_in[slot, 1], ada_in[slot, 2])
                ada_out[slot, 0] = g
                ada_out[slot, 1] = delta
                ada_out[slot, 2] = new_m
                ada_out[slot, 3] = new_v
                for cp in stores(i):
                    cp.start()
            for i in (len(chunks) - 2, len(chunks) - 1):
                for cp in stores(i):
                    cp.wait()

            total = dmod_all[0]
            for b in range(1, N_DEV):
                total = total + dmod_all[b]
            width = total.shape[1]
            for layer in range(DEPTH):
                for q in range(4):
                    gb_ref[layer:layer + 1, q * width:(q + 1) * width] = total[4 * layer + q:4 * layer + q + 1, :]
            db_ref[...], nmb_ref[...], nvb_ref[...] = _adamw_math(bada_ref[...], gb_ref[...], mbada_ref[...],
                                                                  vbada_ref[...])

        order = (ACROSS_BOTH, ACROSS_X, ACROSS_Y, None)
        chips = [_ChipReduceScatter._chip(r) for r in order]
        loads = [pltpu.make_async_copy(s, d, load_sems.at[n]) for n, (s, d) in enumerate(
            ((cat_hbm, cat_v), (dy_hbm, dy_v), (h_hbm, h_v)))]
        loads += [pltpu.make_async_copy(dproj_hbm.at[:, pl.ds(pl.multiple_of(chip * 2 * shard_in, 2 * shard_in),
                                                             2 * shard_in)], dproj_v.at[n], load_sems.at[3 + n])
                  for n, chip in enumerate(chips)]
        for cp in loads:
            cp.start()
        arrays = [dict(part=part_in, out=gwin_ref, staged=False, sib=bufs[0], snd=bufs[1], rcv=bufs[2], relay=bufs[3]),
                  dict(part=part_out, out=gwout_ref, staged=False, sib=bufs[4], snd=bufs[5], rcv=bufs[6],
                       relay=bufs[7]),
                  dict(part=small_hbm, out=own_small, staged=True, stage=bufs[8], sib=bufs[9], snd=bufs[10],
                       rcv=bufs[11], relay=bufs[12])]
        scatter = _ChipReduceScatter(arrays, *rs_sems)
        scatter.start([SMALL])
        dmod_all[mine] = dmod_ref[...]
        dmod_copies = _direct_exchange(lambda p: dmod_ref, lambda m: dmod_all.at[_index(m)], m_send, m_recv)
        loss_src[...] = jnp.full(loss_src.shape, (0.5 / D_MODEL) * jnp.sum(lanes_ref[...]), F32)
        loss_all[mine] = loss_src[...]
        loss_copies = _direct_exchange(lambda p: loss_src, lambda m: loss_all.at[_index(m)], s_send, s_recv)

        loads[0].wait()
        loads[1].wait()
        for blk in range(2):
            res = _dot_tn(cat_v[:, blk * 512:(blk + 1) * 512], dy_v[...]).astype(BF16)
            for s in range(4):
                part_out[2 * blk + s // 2, s % 2] = res[s * shard_out:(s + 1) * shard_out]
        scatter.start([W_OUT])
        scatter.exchange([SMALL])

        gather = _TwoLevelGather(stot_ref, g_send, g_recv)
        loads[2].wait()
        for n, chip in enumerate(chips):
            loads[3 + n].wait()
            res = _dot_tn(dproj_v[n], h_v[...]).astype(BF16)
            part_in[chip, 0] = res[:shard_in]
            part_in[chip, 1] = res[shard_in:]
            scatter.start([W_IN], chips=[chip])
            if n == 0:
                scatter.exchange([W_OUT])
                scatter.fold([SMALL])
            if n == 1:
                scatter.send_far([W_IN])
                scatter.fold([W_OUT])
                scatter.finish([SMALL])
                stot_ref[mine] = own_small[...]
                gather.send_mine()
            if n == 2:
                scatter.send_near(ACROSS_X, [W_IN])
                gather.relay()
            if n == 3:
                scatter.send_near(ACROSS_Y, [W_IN])
        scatter.fold([W_IN])
        scatter.finish([W_OUT])
        _wait_direct(dmod_copies)
        update_ada()
        gather.pass_on()
        gather.wait_rest()
        _wait_direct(loss_copies)
        total = loss_all[0]
        for j in range(1, N_DEV):
            total = total + loss_all[j]
        loss_ref[...] = total
        scatter.finish([W_IN])
        gather.wait_sends()
        scatter.wait_sends()

    buffers = _ChipReduceScatter.buffers
    comm_scratch = (buffers(shard_in, D_MODEL, BF16, staged=False) + buffers(shard_out, D_MODEL, BF16, staged=False)
                    + buffers(shard_small, 128, F32))
    comm_scratch += [pltpu.SemaphoreType.DMA((7,))] + _ChipReduceScatter.semaphores(3)
    comm_scratch += [pltpu.SemaphoreType.DMA((n,)) for n in (7, 7, GATHER_SEMS, GATHER_SEMS, 7, 7)]
    comm_scratch += [pltpu.SemaphoreType.DMA((2, 3)), pltpu.SemaphoreType.DMA((2, 4))]
    return pl.pallas_call(
        body,
        name="grad_tail",
        in_specs=[ANY] * 5 + [VMEM, VMEM] + [ANY] * 3 + [VMEM] * 4,
        out_specs=[VMEM] * 4 + [ANY] * 4 + [VMEM] * 4,
        out_shape=[jax.ShapeDtypeStruct((shard_in, D_MODEL), F32), jax.ShapeDtypeStruct((shard_out, D_MODEL), F32),
                   jax.ShapeDtypeStruct((N_DEV, shard_small, 128), F32), jax.ShapeDtypeStruct((8, 128), F32)]
                  + [jax.ShapeDtypeStruct(w_ada.shape, F32)] * 4 + [jax.ShapeDtypeStruct(b_ada.shape, F32)] * 4,
        scratch_shapes=[pltpu.VMEM((4, SEQ, 2 * shard_in), BF16), pltpu.VMEM(h.shape, BF16), pltpu.VMEM(cat.shape, BF16),
                        pltpu.VMEM(dy.shape, BF16), pltpu.VMEM((4, 2, shard_in, D_MODEL), BF16),
                        pltpu.VMEM((4, 2, shard_out, D_MODEL), BF16), pltpu.VMEM((shard_small, 128), F32),
                        pltpu.VMEM((8, 128), F32), pltpu.VMEM((N_DEV, 8, 128), F32),
                        pltpu.VMEM((N_DEV,) + dmod8.shape, F32), pltpu.VMEM((2, 3, ADA_CHUNK, cols), F32),
                        pltpu.VMEM((2, 4, ADA_CHUNK, cols), F32)] + comm_scratch,
        compiler_params=pltpu.CompilerParams(vmem_limit_bytes=VMEM_LIMIT),
    )(dproj, h, cat, dy, small, dmod8, loss_lanes, w_ada, m_ada, v_ada, act_t, b_ada, m_bada, v_bada)


SMALL_NAMES = ("w_pool", "w_sgu", "pool_scale", "sgu_ln_g", "sgu_ln_b", "b_sgu", "ln_g", "ln_b")
SMALL_ROWS = (512, 512, 4, 4, 4, 4, 8, 8)


def _adamw_small(g_packed, ws, ms, vs, name):
    n = len(SMALL_NAMES)

    def body(g_ref, *refs):
        w_refs, m_refs, v_refs = refs[:n], refs[n:2 * n], refs[2 * n:3 * n]
        outs = refs[3 * n:]

        def update(p, at, g):
            delta, new_m, new_v = _adamw_math(w_refs[p][at], g, m_refs[p][at], v_refs[p][at])
            outs[p][at] = g
            outs[n + p][at] = delta
            outs[2 * n + p][at] = new_m
            outs[3 * n + p][at] = new_v

        row = 0
        for p, r in enumerate(SMALL_ROWS):
            shape = ws[p].shape
            for layer in range(DEPTH):
                first = layer * PACK_ROWS + row
                if len(shape) == 4:
                    for k in range(shape[1]):
                        update(p, (layer, k), g_ref[first + k * shape[2]:first + (k + 1) * shape[2], :])
                elif len(shape) == 3:
                    update(p, (layer,), g_ref[first:first + r, :])
                else:
                    g = jnp.concatenate([g_ref[first + k:first + k + 1, :] for k in range(r)], axis=1)
                    update(p, (slice(layer, layer + 1), slice(None)), g)
            row += r

    res = pl.pallas_call(
        body,
        name=name,
        out_shape=[jax.ShapeDtypeStruct(w.shape, F32) for w in ws] * 4,
        compiler_params=pltpu.CompilerParams(vmem_limit_bytes=VMEM_LIMIT),
    )(g_packed, *ws, *ms, *vs)
    return res[:n], res[n:2 * n], res[2 * n:3 * n], res[3 * n:]


def kernel(x, c, w_ada, b_ada, w_in, w_pool, pool_scale, sgu_ln_g, sgu_ln_b, w_sgu, b_sgu, w_out, ln_g, ln_b, loss_target, m_w_ada, m_b_ada, m_w_in, m_w_pool, m_pool_scale, m_sgu_ln_g, m_sgu_ln_b, m_w_sgu, m_b_sgu, m_w_out, m_ln_g, m_ln_b, v_w_ada, v_b_ada, v_w_in, v_w_pool, v_pool_scale, v_sgu_ln_g, v_sgu_ln_b, v_w_sgu, v_b_sgu, v_w_out, v_ln_g, v_ln_b):
    small_w = dict(w_pool=w_pool, w_sgu=w_sgu, pool_scale=pool_scale, sgu_ln_g=sgu_ln_g, sgu_ln_b=sgu_ln_b,
                   b_sgu=b_sgu, ln_g=ln_g, ln_b=ln_b)
    small_m = dict(w_pool=m_w_pool, w_sgu=m_w_sgu, pool_scale=m_pool_scale, sgu_ln_g=m_sgu_ln_g,
                   sgu_ln_b=m_sgu_ln_b, b_sgu=m_b_sgu, ln_g=m_ln_g, ln_b=m_ln_b)
    small_v = dict(w_pool=v_w_pool, w_sgu=v_w_sgu, pool_scale=v_pool_scale, sgu_ln_g=v_sgu_ln_g,
                   sgu_ln_b=v_sgu_ln_b, b_sgu=v_b_sgu, ln_g=v_ln_g, ln_b=v_ln_b)

    wint_loc = jnp.transpose(w_in, (0, 2, 1)).astype(BF16)
    wout_loc = w_out.astype(BF16)
    small = (w_pool, pool_scale, sgu_ln_g, sgu_ln_b, w_sgu, jnp.transpose(b_sgu, (0, 2, 1)))
    (out0, y0, cdf0, proj0), (cur, y1, cdf1, proj1), gathered0, gathered1, act_all, mod = _forward(
        x[0], c, w_ada, b_ada, small, ln_g, ln_b, [wint_loc[0], wout_loc[0]], [wint_loc[1], wout_loc[1]])
    w_int, w_outf = [gathered0[0], gathered1[0]], [gathered0[1], gathered1[1]]
    acts = [(x[0], proj0, y0, cdf0), (out0, proj1, y1, cdf1)]

    shard_in, shard_out = D_IN // N_DEV, D_MODEL // N_DEV
    a, b = cur, loss_target[0]
    loss_lanes, carry, pending = None, (), []
    g_w_in_t, g_w_out = [None] * DEPTH, [None] * DEPTH
    for l in reversed(range(DEPTH)):
        dx, dproj, h, cat, dy, small_grads, dmod8, lanes, *shards = _layer_backward(
            l, a, b, *acts[l], mod, w_int[l], w_outf[l], small, ln_g, l == DEPTH - 1, f"layer_bwd_{l}",
            carry=carry, reduce=pending)
        if shards:
            g_w_in_t[l + 1], g_w_out[l + 1] = shards
        if l == DEPTH - 1:
            loss_lanes = lanes
        if l > 0:
            g_in, g_out = _grad_matmuls(dproj, h, cat, dy, f"grad_w_{l}")
            pending = [g_in.reshape(4, 2, shard_in, D_MODEL), g_out.reshape(4, 2, shard_out, D_MODEL)]
        carry = (small_grads, dmod8)
        a = b = dx
    grad_x = a[None]

    (g_w_in_t[0], g_w_out[0], small_tot, loss_tile, g_w_ada, d_w_ada, nm_w_ada, nv_w_ada,
     g_b_ada, d_b_ada, nm_b_ada, nv_b_ada) = _grad_tail(
        dproj, h, cat, dy, small_grads.reshape(4, 2, DEPTH * PACK_ROWS // N_DEV, 128), dmod8, loss_lanes,
        w_ada, m_w_ada, v_w_ada, jnp.transpose(act_all), b_ada, m_b_ada, v_b_ada)
    loss = loss_tile[0, 0]

    flat = lambda t: t.reshape(-1, t.shape[-1])
    to_t = lambda t: flat(jnp.transpose(t, (0, 2, 1)))
    from_t = lambda t: jnp.transpose(t.reshape(DEPTH, shard_in, D_MODEL), (0, 2, 1))
    g_w_in, d_w_in, nm_w_in, nv_w_in = [from_t(t) for t in _adamw(to_t(w_in), g_w_in_t, to_t(m_w_in), to_t(v_w_in),
                                                                  shard_in // 2, "adamw_w_in")]
    gwout, d_w_out, nm_w_out, nv_w_out = [t.reshape(w_out.shape) for t in _adamw(
        flat(w_out), g_w_out, flat(m_w_out), flat(v_w_out), shard_out, "adamw_w_out")]
    small_out = _adamw_small(small_tot.reshape(DEPTH * PACK_ROWS, 128), [small_w[n] for n in SMALL_NAMES],
                             [small_m[n] for n in SMALL_NAMES], [small_v[n] for n in SMALL_NAMES], "adamw_small")
    gs, ds, ms, vs = [dict(zip(SMALL_NAMES, group)) for group in small_out]

    def ordered(w_ada_, b_ada_, w_in_, small, w_out_):
        return (w_ada_, b_ada_, w_in_, small["w_pool"], small["pool_scale"], small["sgu_ln_g"], small["sgu_ln_b"],
                small["w_sgu"], small["b_sgu"], w_out_, small["ln_g"], small["ln_b"])

    return (loss, grad_x,
            *ordered(g_w_ada, g_b_ada, g_w_in, gs, gwout),
            *ordered(d_w_ada, d_b_ada, d_w_in, ds, d_w_out),
            *ordered(nm_w_ada, nm_b_ada, nm_w_in, ms, nm_w_out),
            *ordered(nv_w_ada, nv_b_ada, nv_w_in, vs, nv_w_out))
```

```python
import jax
import jax.numpy as jnp
from jax import lax
from jax.experimental import pallas as pl
from jax.experimental.pallas import tpu as pltpu

F32 = jnp.float32
BF16 = jnp.bfloat16

D_MODEL = 1024
SEQ = 2048
DEPTH = 2
D_POOL = 512
D_SGU = 512
D_IN = 2560
N_GROUPS = 4
GROUP = 128
N_HEADS = 4
HEAD = 128
CHUNK = 128
WINDOWS = (2, 4, 8, 16)
ALPHA = (2.0 * DEPTH) ** 0.25
LN_EPS = 1e-5
N_DEV = 8

ADAM_LR = 0.001
ADAM_B1 = 0.9
ADAM_B2 = 0.999
ADAM_EPS = 1e-08
ADAM_WD = 0.01
ADAM_STEP = 10

TM = 256
HALO = 16
N_TILES = SEQ // TM
VMEM_LIMIT = 60 * 1024 * 1024

ROW_WPOOL = 0
ROW_WSGU = 512
ROW_PSCALE = 1024
ROW_SLNG = 1028
ROW_SLNB = 1032
ROW_BSGU = 1036
ROW_LNG = 1040
ROW_LNB = 1048
PACK_ROWS = 1088
DMOD_COLS = DEPTH * 3 * D_MODEL // 8

SQRT_HALF = 0.7071067811865476
INV_SQRT_2PI = 0.3989422804014327


def _ln(x):
    mu = jnp.mean(x, axis=-1, keepdims=True)
    xc = x - mu
    var = jnp.mean(xc * xc, axis=-1, keepdims=True)
    rstd = lax.rsqrt(var + LN_EPS)
    return xc * rstd, rstd


def _ln_bwd(dxn, xn, rstd):
    m1 = jnp.mean(dxn, axis=-1, keepdims=True)
    m2 = jnp.mean(dxn * xn, axis=-1, keepdims=True)
    return rstd * (dxn - m1 - xn * m2)


def _normal_cdf(x):
    return 0.5 * (1.0 + lax.erf(x * SQRT_HALF))


def _gelu_parts(x, cdf, with_grad):
    if not with_grad:
        return x * cdf, None
    return x * cdf, cdf + x * (INV_SQRT_2PI * jnp.exp(-0.5 * x * x))


def _silu_parts(x):
    s = jax.nn.sigmoid(x)
    return x * s, s * (1.0 + x * (1.0 - s))


def _dot(a, b):
    return lax.dot_general(a, b, (((1,), (0,)), ((), ())), preferred_element_type=F32)


def _dot_nt(a, b):
    return lax.dot_general(a, b, (((1,), (1,)), ((), ())), preferred_element_type=F32)


def _dot_tn(a, b):
    return lax.dot_general(a, b, (((0,), (0,)), ((), ())), preferred_element_type=F32)


def _row_index(tile):
    return tile * TM + lax.broadcasted_iota(jnp.int32, (TM, 1), 0)


def _window_sums(ext, forward):
    n = TM + HALO
    cur = ext
    outs = []
    for g in range(N_GROUPS):
        step = 1 << g
        cur = cur + pltpu.roll(cur, step if forward else n - step, 0)
        rows = cur[HALO:, :GROUP] if forward else cur[:TM, :GROUP]
        outs.append(rows)
        cur = cur[:, GROUP:] if g + 1 < N_GROUPS else None
    return outs


def _inverse_counts(rows):
    return [1.0 / jnp.minimum(rows + 1, w).astype(F32) for w in WINDOWS]


def _tril_bf16(w):
    t = lax.broadcasted_iota(jnp.int32, (CHUNK, CHUNK), 0)
    s = lax.broadcasted_iota(jnp.int32, (CHUNK, CHUNK), 1)
    return jnp.where(t >= s, w, 0.0).astype(BF16)


class _MixWeights:
    def __init__(self, layer, wpool_ref, pscale_ref, slng_ref, slnb_ref, wsgu_ref, bsgut_ref):
        self.layer = layer
        self.wpool_ref, self.pscale_ref, self.slng_ref, self.slnb_ref = wpool_ref, pscale_ref, slng_ref, slnb_ref
        self.wsgu_ref, self.bsgut_ref = wsgu_ref, bsgut_ref

    def pool(self, g):
        return self.wpool_ref[self.layer, g].astype(BF16)

    def pool_scale(self, g):
        return self.pscale_ref[self.layer:self.layer + 1, g * GROUP:(g + 1) * GROUP]

    def ln_gain(self, h):
        return self.slng_ref[self.layer, h:h + 1, :]

    def ln_bias(self, h):
        return self.slnb_ref[self.layer, h:h + 1, :]

    def mix(self, h):
        return _tril_bf16(self.wsgu_ref[self.layer, h])

    def mix_bias(self, h):
        return self.bsgut_ref[self.layer, :, h:h + 1]


SMALL_SPECS = ((DEPTH, N_GROUPS, GROUP, GROUP), (DEPTH, D_POOL), (DEPTH, N_HEADS, HEAD), (DEPTH, N_HEADS, HEAD),
               (DEPTH, N_HEADS, CHUNK, CHUNK), (DEPTH, CHUNK, N_HEADS))


def _mix_forward(proj, halo, tile, w, cdf=None):
    keep = cdf is not None
    rows = _row_index(tile)
    inv_counts = _inverse_counts(rows)
    xa = proj[:, 0:D_POOL]
    ga = proj[:, D_POOL:2 * D_POOL]
    sums = _window_sums(jnp.concatenate([halo, xa], axis=0), True)
    ga_act, ga_grad = _silu_parts(ga)
    pooled, pw, ya = [], [], []
    for g in range(N_GROUPS):
        sl = slice(g * GROUP, (g + 1) * GROUP)
        p = (sums[g] * inv_counts[g] - xa[:, sl]).astype(BF16)
        q = _dot(p, w.pool(g))
        pooled.append(p)
        pw.append(q)
        ya.append(q * w.pool_scale(g) * ga_act[:, sl])

    u = proj[:, 2 * D_POOL:2 * D_POOL + D_SGU]
    v = proj[:, 2 * D_POOL + D_SGU:2 * D_POOL + 2 * D_SGU]
    gb = proj[:, 2 * D_POOL + 2 * D_SGU:]
    gb_act, gb_grad = _silu_parts(gb)
    if cdf is None:
        cdf = jnp.concatenate([_normal_cdf(u), _normal_cdf(v)], axis=1)
    u_act, u_grad = _gelu_parts(u, cdf[:, :D_SGU], keep)
    v_act, v_grad = _gelu_parts(v, cdf[:, D_SGU:], keep)
    vn, vrstd, vln, mixed, yb = [], [], [], [], []
    for h in range(N_HEADS):
        sl = slice(h * HEAD, (h + 1) * HEAD)
        n_h, r_h = _ln(v_act[:, sl])
        l_h = (n_h * w.ln_gain(h) + w.ln_bias(h)).astype(BF16)
        w_h = w.mix(h)
        bias = w.mix_bias(h)
        m_h = jnp.concatenate(
            [_dot(w_h, l_h[k * CHUNK:(k + 1) * CHUNK]) + bias for k in range(TM // CHUNK)], axis=0)
        vn.append(n_h)
        vrstd.append(r_h)
        vln.append(l_h)
        mixed.append(m_h)
        yb.append(u_act[:, sl] * m_h * gb_act[:, sl])
    cat = jnp.concatenate(ya + yb, axis=1)
    if not keep:
        return cat, cdf
    return cat, dict(inv_counts=inv_counts, ga_act=ga_act, ga_grad=ga_grad, pooled=pooled, pw=pw, u_grad=u_grad,
                     v_grad=v_grad, u_act=u_act, gb_act=gb_act, gb_grad=gb_grad, vn=vn, vrstd=vrstd, vln=vln,
                     mixed=mixed)


def _const_spec(shape):
    nd = len(shape)
    return pl.BlockSpec(shape, lambda i: (0,) * nd)


VEC_LNG, VEC_LNB, VEC_POOL, VEC_SGU, VEC_SHIFT, VEC_SCALE, VEC_GATE, VEC_LOSS = range(8)


def _layer_backward(layer, a, b, x, proj, y, cdf, mod, w_int, w_outf, small, ln_g, is_last, name, carry=(),
                    reduce=()):
    n_red, n_carry = len(reduce), len(carry)
    base = layer * PACK_ROWS

    def body(a_ref, b_ref, x_ref, proj_ref, prev_ref, y_ref, cdf_ref, mod_ref, wint_ref, wout_ref, wpool_ref,
             pscale_ref, slng_ref, slnb_ref, wsgu_ref, bsgut_ref, lng_ref, *rest):
        weights = _MixWeights(layer, wpool_ref, pscale_ref, slng_ref, slnb_ref, wsgu_ref, bsgut_ref)
        carry_refs, rest = rest[:n_carry], rest[n_carry:]
        part_refs, rest = rest[:n_red], rest[n_red:]
        dx_ref, dproj_ref, h_ref, cat_ref, dy_ref, small_ref, dmod_ref, loss_ref = rest[:8]
        shard_refs, rest = rest[8:8 + n_red], rest[8 + n_red:]
        vec_ref, dmix_ref, halo_ref = rest[:3]
        step = pl.program_id(0)
        tile = N_TILES - 1 - step

        def scatter():
            bufs, sems = rest[3:3 + 5 * n_red], rest[3 + 5 * n_red:]
            arrays = [dict(part=part_refs[n], out=shard_refs[n], staged=True, stage=bufs[5 * n], sib=bufs[5 * n + 1],
                           snd=bufs[5 * n + 2], rcv=bufs[5 * n + 3], relay=bufs[5 * n + 4]) for n in range(n_red)]
            return _ChipReduceScatter(arrays, *sems)

        @pl.when(step == 0)
        def _():
            small_ref[...] = jnp.zeros_like(small_ref)
            dmod_ref[...] = jnp.zeros_like(dmod_ref)
            vec_ref[...] = jnp.zeros_like(vec_ref)
            dmix_ref[...] = jnp.zeros_like(dmix_ref)
            halo_ref[...] = jnp.zeros_like(halo_ref)
            if n_red:
                scatter().start()

        if n_red:
            @pl.when(step == 1)
            def _():
                scatter().exchange()

            @pl.when(step == N_TILES // 2)
            def _():
                scatter().fold()

        def acc(row, lo, val):
            hi = lo + val.shape[1]
            vec_ref[row:row + 1, lo:hi] += jnp.sum(val, axis=0, keepdims=True)

        xt = x_ref[...]
        yt = y_ref[...]
        shift = mod_ref[layer:layer + 1, 0:D_MODEL]
        scale = mod_ref[layer:layer + 1, D_MODEL:2 * D_MODEL]
        gate = mod_ref[layer:layer + 1, 2 * D_MODEL:]
        ln_gain = lng_ref[layer:layer + 1, :]

        zn, zrstd = _ln(ALPHA * xt + gate * yt)
        if is_last:
            diff = a_ref[...] - b_ref[...]
            acc(VEC_LOSS, 0, diff * diff)
            dout = diff * (1.0 / D_MODEL)
        else:
            dout = a_ref[...]
        acc(VEC_LNG, 0, dout * zn)
        acc(VEC_LNB, 0, dout)
        dz = _ln_bwd(dout * ln_gain, zn, zrstd)
        acc(VEC_GATE, 0, dz * yt)
        dy = (dz * gate).astype(BF16)
        dy_ref[...] = dy
        dcat = _dot_nt(dy, wout_ref[...])

        proj = proj_ref[...]
        prev = jnp.where(tile > 0, prev_ref[...], 0.0)
        cat, k = _mix_forward(proj, prev, tile, weights, cdf_ref[...])
        cat_ref[...] = cat.astype(BF16)

        dga, dq = [], []
        for g in range(N_GROUPS):
            sl = slice(g * GROUP, (g + 1) * GROUP)
            pscale = weights.pool_scale(g)
            dya = dcat[:, sl]
            dyp = dya * k["ga_act"][:, sl]
            dga.append(dya * k["pw"][g] * pscale * k["ga_grad"][:, sl])
            acc(VEC_POOL, g * GROUP, dyp * k["pw"][g])
            dpw = (dyp * pscale).astype(BF16)
            rows = pl.ds(base + ROW_WPOOL + g * GROUP, GROUP)
            small_ref[rows, :] += _dot_tn(k["pooled"][g], dpw)
            dq.append(_dot_nt(dpw, weights.pool(g)))
        dpooled = jnp.concatenate(dq, axis=1)
        scaled = jnp.concatenate([dq[g] * k["inv_counts"][g] for g in range(N_GROUPS)], axis=1)
        sums = _window_sums(jnp.concatenate([scaled, halo_ref[...]], axis=0), False)
        halo_ref[...] = scaled[0:HALO]
        dxa = jnp.concatenate(sums, axis=1) - dpooled

        du, dv, dgb = [], [], []
        for h in range(N_HEADS):
            sl = slice(h * HEAD, (h + 1) * HEAD)
            dyb = dcat[:, D_POOL + h * HEAD:D_POOL + (h + 1) * HEAD]
            m_h = k["mixed"][h]
            ug = k["u_act"][:, sl] * dyb
            du.append(dyb * m_h * k["gb_act"][:, sl] * k["u_grad"][:, sl])
            dgb.append(ug * m_h * k["gb_grad"][:, sl])
            dmixed = ug * k["gb_act"][:, sl]
            dmixed_bf = dmixed.astype(BF16)
            w_h = weights.mix(h)
            dvln_parts = []
            dmix_sum = dmix_ref[h]
            wsgu_rows = pl.ds(base + ROW_WSGU + h * CHUNK, CHUNK)
            dws = small_ref[wsgu_rows, :]
            for c in range(TM // CHUNK):
                cs = slice(c * CHUNK, (c + 1) * CHUNK)
                dmix_sum = dmix_sum + dmixed[cs]
                dws = dws + _dot_nt(dmixed_bf[cs], k["vln"][h][cs])
                dvln_parts.append(_dot_tn(w_h, dmixed_bf[cs]))
            dmix_ref[h] = dmix_sum
            small_ref[wsgu_rows, :] = dws
            dvln = jnp.concatenate(dvln_parts, axis=0)
            acc(VEC_SGU, h * HEAD, dvln * k["vn"][h])
            acc(VEC_SGU, D_SGU + h * HEAD, dvln)
            dvv = _ln_bwd(dvln * weights.ln_gain(h), k["vn"][h], k["vrstd"][h])
            dv.append(dvv * k["v_grad"][:, sl])

        dproj = jnp.concatenate([dxa] + dga + du + dv + dgb, axis=1).astype(BF16)
        dproj_ref[...] = dproj
        dh = _dot(dproj, wint_ref[...])

        xn, xrstd = _ln(xt)
        h_ref[...] = (xn * (1.0 + scale) + shift).astype(BF16)
        acc(VEC_SCALE, 0, dh * xn)
        acc(VEC_SHIFT, 0, dh)
        dx_ref[...] = _ln_bwd(dh * (1.0 + scale), xn, xrstd) + ALPHA * dz

        @pl.when(step == N_TILES - 1)
        def _():
            def put(row0, vec_row, lo, n):
                for r in range(n):
                    small_ref[base + row0 + r:base + row0 + r + 1, :] = (
                        vec_ref[vec_row:vec_row + 1, lo + r * 128:lo + (r + 1) * 128])

            put(ROW_PSCALE, VEC_POOL, 0, 4)
            put(ROW_SLNG, VEC_SGU, 0, 4)
            put(ROW_SLNB, VEC_SGU, D_SGU, 4)
            put(ROW_LNG, VEC_LNG, 0, 8)
            put(ROW_LNB, VEC_LNB, 0, 8)
            ones = jnp.ones((8, HEAD), F32)
            t = lax.broadcasted_iota(jnp.int32, (CHUNK, CHUNK), 0)
            s = lax.broadcasted_iota(jnp.int32, (CHUNK, CHUNK), 1)
            for h in range(N_HEADS):
                bias_rows = lax.dot_general(ones, dmix_ref[h], (((1,), (1,)), ((), ())),
                                            preferred_element_type=F32, precision=lax.Precision.HIGHEST)
                small_ref[base + ROW_BSGU + h:base + ROW_BSGU + h + 1, :] = bias_rows[0:1]
                rows = pl.ds(base + ROW_WSGU + h * CHUNK, CHUNK)
                small_ref[rows, :] = jnp.where(t >= s, small_ref[rows, :], 0.0)
            pieces = ((0, VEC_SHIFT, 0, 768),
                      (1, VEC_SHIFT, 768, 256), (1, VEC_SCALE, 0, 512),
                      (2, VEC_SCALE, 512, 512), (2, VEC_GATE, 0, 256),
                      (3, VEC_GATE, 256, 768))
            filled = [0] * 4
            for q, vec_row, lo, n in pieces:
                row = 4 * layer + q
                dmod_ref[row:row + 1, filled[q]:filled[q] + n] = vec_ref[vec_row:vec_row + 1, lo:lo + n]
                filled[q] += n
            if n_carry:
                for other in range(layer + 1, DEPTH):
                    rows = pl.ds(other * PACK_ROWS, PACK_ROWS)
                    small_ref[rows, :] = carry_refs[0][rows, :]
                    dmod_ref[4 * other:4 * other + 4, :] = carry_refs[1][4 * other:4 * other + 4, :]
            loss_ref[...] = vec_ref[VEC_LOSS:VEC_LOSS + 1, :]
            if n_red:
                scatter().finish()
                scatter().wait_sends()

    rev = lambda w: pl.BlockSpec((TM, w), lambda i: (N_TILES - 1 - i, 0))
    prev_spec = pl.BlockSpec(
        (HALO, D_POOL), lambda i: (jnp.maximum((N_TILES - 1 - i) * (TM // HALO) - 1, 0), 0))
    comm_scratch = []
    for p in reduce:
        comm_scratch += _ChipReduceScatter.buffers(p.shape[2], p.shape[3], p.dtype)
    if n_red:
        comm_scratch += _ChipReduceScatter.semaphores(n_red)
    return pl.pallas_call(
        body,
        name=name,
        grid=(N_TILES,),
        in_specs=[rev(D_MODEL), rev(D_MODEL) if is_last else pl.BlockSpec((TM, D_MODEL), lambda i: (0, 0)),
                  rev(D_MODEL), rev(D_IN), prev_spec, rev(D_MODEL), rev(2 * D_SGU),
                  _const_spec((DEPTH, 3 * D_MODEL)), _const_spec((D_IN, D_MODEL)), _const_spec((D_MODEL, D_MODEL))]
                 + [_const_spec(s) for s in SMALL_SPECS] + [_const_spec((DEPTH, D_MODEL))]
                 + [_const_spec(c.shape) for c in carry] + [ANY] * n_red,
        out_specs=[rev(D_MODEL), rev(D_IN), rev(D_MODEL), rev(D_MODEL), rev(D_MODEL),
                   _const_spec((DEPTH * PACK_ROWS, 128)), _const_spec((8, DMOD_COLS)), _const_spec((1, D_MODEL))]
                  + [_const_spec(p.shape[2:]) for p in reduce],
        out_shape=[jax.ShapeDtypeStruct((SEQ, D_MODEL), F32), jax.ShapeDtypeStruct((SEQ, D_IN), BF16),
                   jax.ShapeDtypeStruct((SEQ, D_MODEL), BF16), jax.ShapeDtypeStruct((SEQ, D_MODEL), BF16),
                   jax.ShapeDtypeStruct((SEQ, D_MODEL), BF16), jax.ShapeDtypeStruct((DEPTH * PACK_ROWS, 128), F32),
                   jax.ShapeDtypeStruct((8, DMOD_COLS), F32), jax.ShapeDtypeStruct((1, D_MODEL), F32)]
                  + [jax.ShapeDtypeStruct(p.shape[2:], F32) for p in reduce],
        scratch_shapes=[pltpu.VMEM((8, D_MODEL), F32), pltpu.VMEM((N_HEADS, CHUNK, HEAD), F32),
                        pltpu.VMEM((HALO, D_POOL), F32)] + comm_scratch,
        compiler_params=pltpu.CompilerParams(dimension_semantics=("arbitrary",), vmem_limit_bytes=VMEM_LIMIT),
    )(a, b, x, proj, proj, y, cdf, mod, w_int, w_outf, *small, ln_g, *carry, *reduce)


def _grad_matmuls(dproj, h, cat, dy, name):
    in_cols, out_cols = D_IN // 4, D_MODEL // 2
    in_steps = D_IN // in_cols

    def body(dproj_ref, h_ref, cat_ref, dy_ref, gin_ref, gout_ref):
        step = pl.program_id(0)

        @pl.when(step < in_steps)
        def _():
            gin_ref[...] = _dot_tn(dproj_ref[...], h_ref[...]).astype(BF16)

        @pl.when(step >= in_steps)
        def _():
            gout_ref[...] = _dot_tn(cat_ref[...], dy_ref[...]).astype(BF16)

    in_block = lambda j: jnp.minimum(j, in_steps - 1)
    out_block = lambda j: jnp.maximum(j - in_steps, 0)
    return pl.pallas_call(
        body,
        name=name,
        grid=(in_steps + D_MODEL // out_cols,),
        in_specs=[pl.BlockSpec((SEQ, in_cols), lambda j: (0, in_block(j))), _const_spec((SEQ, D_MODEL)),
                  pl.BlockSpec((SEQ, out_cols), lambda j: (0, out_block(j))), _const_spec((SEQ, D_MODEL))],
        out_specs=[pl.BlockSpec((in_cols, D_MODEL), lambda j: (in_block(j), 0)),
                   pl.BlockSpec((out_cols, D_MODEL), lambda j: (out_block(j), 0))],
        out_shape=[jax.ShapeDtypeStruct((D_IN, D_MODEL), BF16), jax.ShapeDtypeStruct((D_MODEL, D_MODEL), BF16)],
        compiler_params=pltpu.CompilerParams(dimension_semantics=("arbitrary",), vmem_limit_bytes=VMEM_LIMIT),
    )(dproj, h, cat, dy)


def _adamw_math(w, g, m, v):
    m = ADAM_B1 * m + (1.0 - ADAM_B1) * g
    v = ADAM_B2 * v + (1.0 - ADAM_B2) * (g * g)
    m_hat = m / (1.0 - ADAM_B1 ** ADAM_STEP)
    v_hat = v / (1.0 - ADAM_B2 ** ADAM_STEP)
    delta = -ADAM_LR * (m_hat / (jnp.sqrt(v_hat) + ADAM_EPS) + ADAM_WD * w)
    return delta, m, v


def _adamw(w, grads, m, v, block_rows, name):
    rows, cols = grads[0].shape
    blocks = rows // block_rows

    def body(w_ref, m_ref, v_ref, *rest):
        g_refs, (g_ref, d_ref, nm_ref, nv_ref) = rest[:DEPTH], rest[DEPTH:]
        for layer in range(DEPTH):
            @pl.when(pl.program_id(0) == layer)
            def _():
                g = g_refs[layer][...]
                g_ref[...] = g
                d_ref[...], nm_ref[...], nv_ref[...] = _adamw_math(w_ref[...], g, m_ref[...], v_ref[...])

    def grad_spec(layer):
        return pl.BlockSpec((block_rows, cols),
                            lambda l, i: (jnp.where(l == layer, i, jnp.where(l < layer, 0, blocks - 1)), 0))

    spec = pl.BlockSpec((block_rows, cols), lambda l, i: (l * blocks + i, 0))
    return pl.pallas_call(
        body,
        name=name,
        grid=(DEPTH, blocks),
        in_specs=[spec] * 3 + [grad_spec(layer) for layer in range(DEPTH)],
        out_specs=[spec] * 4,
        out_shape=[jax.ShapeDtypeStruct(w.shape, F32)] * 4,
        compiler_params=pltpu.CompilerParams(dimension_semantics=("arbitrary", "arbitrary"),
                                             vmem_limit_bytes=VMEM_LIMIT),
    )(w, m, v, *grads)


MESH = pl.DeviceIdType.MESH
SIBLING = 1
ANY = pl.BlockSpec(memory_space=pl.ANY)
VMEM = pl.BlockSpec(memory_space=pltpu.VMEM)


def _me():
    return lax.axis_index("x"), lax.axis_index("y"), lax.axis_index("c")


def _peer(r):
    x, y, c = _me()
    return (1 - x if r & 4 else x, 1 - y if r & 2 else y, 1 - c if r & 1 else c)


def _index(dev):
    return 4 * dev[0] + 2 * dev[1] + dev[2]


def _remote(src, dst, send_sem, recv_sem, dev):
    return pltpu.make_async_remote_copy(src_ref=src, dst_ref=dst, send_sem=send_sem, recv_sem=recv_sem,
                                        device_id=dev, device_id_type=MESH)


ACROSS_X, ACROSS_Y, ACROSS_BOTH = 4, 2, 6
GATHER_SEMS = 11


class _TwoLevelGather:
    def __init__(self, out, send_sems, recv_sems, src=None):
        self.out, self.send_sems, self.recv_sems, self.src = out, send_sems, recv_sems, src
        self.rows = (out.shape[0] // N_DEV) if len(out.shape) == 2 else out.shape[1]
        self.half = self.rows // 2

    def _slot(self, block):
        if len(self.out.shape) == 2:
            return self.out.at[pl.ds(pl.multiple_of(_index(block) * self.rows, self.rows), self.rows)]
        return self.out.at[_index(block)]

    def _copy(self, k, block, part, to, src=None):
        slot = self._slot(block)
        if part is not None:
            rows = pl.ds(part * self.half, self.half)
            slot = slot.at[rows]
            src = None if src is None else src.at[rows]
        return _remote(slot if src is None else src, slot, self.send_sems.at[k], self.recv_sems.at[k], to)

    def _mine(self):
        me = _me()
        src = self._slot(me) if self.src is None else self.src
        x, y = _peer(ACROSS_X), _peer(ACROSS_Y)
        return [self._copy(1, me, 0, x, src), self._copy(3, me, 1, y, src), self._copy(0, me, None, _peer(SIBLING), src),
                self._copy(2, me, 1, x, src), self._copy(4, me, 0, y, src)]

    def _relayed(self):
        return [self._copy(5, _peer(ACROSS_X), 0, _peer(ACROSS_Y)), self._copy(6, _peer(ACROSS_Y), 1, _peer(ACROSS_X))]

    def _passed(self):
        sib, far = _peer(SIBLING), _peer(ACROSS_BOTH)
        return [self._copy(7, _peer(ACROSS_X), None, sib), self._copy(8, _peer(ACROSS_Y), None, sib),
                self._copy(9, far, 0, sib), self._copy(10, far, 1, sib)]

    def _arrival(self, k, r, part):
        return self._copy(k, _peer(r), part, _me())

    def send_first(self):
        for cp in self._mine()[:3]:
            cp.start()

    def send_second(self):
        for cp in self._mine()[3:]:
            cp.start()

    def send_mine(self):
        self.send_first()
        self.send_second()

    def relay(self):
        relayed = self._relayed()
        self._arrival(1, ACROSS_X, 0).wait_recv()
        relayed[0].start()
        self._arrival(3, ACROSS_Y, 1).wait_recv()
        relayed[1].start()

    def pass_near(self):
        passed = self._passed()
        self._arrival(2, ACROSS_X, 1).wait_recv()
        passed[0].start()
        self._arrival(4, ACROSS_Y, 0).wait_recv()
        passed[1].start()

    def pass_far(self):
        passed = self._passed()
        self._arrival(5, ACROSS_BOTH, 0).wait_recv()
        passed[2].start()
        self._arrival(6, ACROSS_BOTH, 1).wait_recv()
        passed[3].start()

    def pass_on(self):
        self.pass_near()
        self.pass_far()

    def wait_sibling(self):
        self._arrival(0, SIBLING, None).wait_recv()

    def wait_passed(self, r):
        if r == ACROSS_BOTH:
            self._arrival(9, r ^ SIBLING, 0).wait_recv()
            self._arrival(10, r ^ SIBLING, 1).wait_recv()
        else:
            self._arrival(7 if r == ACROSS_X else 8, r ^ SIBLING, None).wait_recv()

    def wait_rest(self):
        self.wait_sibling()
        for r in (ACROSS_X, ACROSS_Y, ACROSS_BOTH):
            self.wait_passed(r)

    def wait_sends(self):
        for cp in self._mine() + self._relayed() + self._passed():
            cp.wait_send()


class _ChipReduceScatter:
    SLOTS = 6

    def __init__(self, arrays, l_sem, d_send, d_recv, i_send, i_recv):
        self.arrays = arrays
        self.l_sem, self.d_send, self.d_recv, self.i_send, self.i_recv = l_sem, d_send, d_recv, i_send, i_recv

    @staticmethod
    def buffers(rows, cols, dtype, staged=True):
        stage = [pltpu.VMEM((4, rows, cols), dtype)] if staged else []
        return stage + [pltpu.VMEM((4, rows, cols), dtype), pltpu.VMEM((3, rows, cols), dtype),
                        pltpu.VMEM((2, rows, cols), dtype), pltpu.VMEM((2, rows // 2, cols), dtype)]

    @classmethod
    def semaphores(cls, n):
        return [pltpu.SemaphoreType.DMA((n,)), pltpu.SemaphoreType.DMA((n, 4)), pltpu.SemaphoreType.DMA((n, 4)),
                pltpu.SemaphoreType.DMA((n, cls.SLOTS)), pltpu.SemaphoreType.DMA((n, cls.SLOTS))]

    def _pick(self, which):
        return list(enumerate(self.arrays)) if which is None else [(n, self.arrays[n]) for n in which]

    @staticmethod
    def _chip(r):
        dev = _me() if r is None else _peer(r)
        return 2 * dev[0] + dev[1]

    def _staging(self, which):
        c = _me()[2]
        return [pltpu.make_async_copy(a["part"].at[pl.ds(0, 4), c], a["stage"], self.l_sem.at[n])
                for n, a in self._pick(which) if a["staged"]]

    def _first(self, which, chip):
        other = 1 - _me()[2]
        return [_remote(a["part"].at[chip, other], a["sib"].at[chip], self.d_send.at[n, chip], self.d_recv.at[n, chip],
                        _peer(SIBLING)) for n, a in self._pick(which)]

    @staticmethod
    def _halves(a):
        half = a["rcv"].shape[1] // 2
        return pl.ds(0, half), pl.ds(half, half)

    def _hops(self, n, a):
        h0, h1 = self._halves(a)
        x, y = _peer(ACROSS_X), _peer(ACROSS_Y)
        snd, rcv, relay = a["snd"], a["rcv"], a["relay"]
        pairs = [(snd.at[2, h0], relay.at[0], x), (snd.at[2, h1], relay.at[1], y),
                 (snd.at[0, h0], rcv.at[0, h0], x), (snd.at[0, h1], rcv.at[0, h1], x),
                 (snd.at[1, h1], rcv.at[1, h1], y), (snd.at[1, h0], rcv.at[1, h0], y)]
        return [_remote(s, d, self.i_send.at[n, k], self.i_recv.at[n, k], to) for k, (s, d, to) in enumerate(pairs)]

    def _mine(self, a, chip, rows=None):
        src = a["stage"].at[chip] if a["staged"] else a["part"].at[chip, _me()[2]]
        mine, sib = (src[...], a["sib"][chip]) if rows is None else (src[rows, :], a["sib"][chip, rows, :])
        return mine.astype(F32) + sib.astype(F32)

    def start(self, which=None, chips=None):
        if chips is None:
            for cp in self._staging(which):
                cp.start()
        for chip in range(4) if chips is None else chips:
            for cp in self._first(which, chip):
                cp.start()

    def send_far(self, which=None):
        far = self._chip(ACROSS_BOTH)
        for cp in self._staging(which):
            cp.wait()
        for cp in self._first(which, far):
            cp.wait_recv()
        for n, a in self._pick(which):
            hops = self._hops(n, a)
            a["snd"][2] = self._mine(a, far).astype(a["snd"].dtype)
            hops[0].start()
            hops[1].start()

    def send_near(self, r, which=None):
        chip = self._chip(r)
        for cp in self._first(which, chip):
            cp.wait_recv()
        for n, a in self._pick(which):
            h0, h1 = self._halves(a)
            hops = self._hops(n, a)
            if r == ACROSS_X:
                a["snd"][0, h0, :] = self._mine(a, chip, h0).astype(a["snd"].dtype)
                hops[2].start()
            else:
                a["snd"][1, h1, :] = self._mine(a, chip, h1).astype(a["snd"].dtype)
                hops[4].start()

    def exchange(self, which=None):
        self.send_far(which)
        self.send_near(ACROSS_X, which)
        self.send_near(ACROSS_Y, which)

    def fold(self, which=None):
        across_x, across_y = self._chip(ACROSS_X), self._chip(ACROSS_Y)
        for n, a in self._pick(which):
            h0, h1 = self._halves(a)
            hops = self._hops(n, a)
            dtype = a["snd"].dtype
            hops[1].wait_recv()
            a["snd"][0, h1, :] = (self._mine(a, across_x, h1) + a["relay"][1].astype(F32)).astype(dtype)
            hops[3].start()
            hops[0].wait_recv()
            a["snd"][1, h0, :] = (self._mine(a, across_y, h0) + a["relay"][0].astype(F32)).astype(dtype)
            hops[5].start()

    def finish(self, which=None):
        home = self._chip(None)
        for cp in self._first(which, home):
            cp.wait_recv()
        for n, a in self._pick(which):
            hops = self._hops(n, a)
            a["out"][...] = self._mine(a, home)
            hops[2].wait_recv()
            hops[3].wait_recv()
            a["out"][...] += a["rcv"][0].astype(F32)
            hops[4].wait_recv()
            hops[5].wait_recv()
            a["out"][...] += a["rcv"][1].astype(F32)

    def wait_sends(self, which=None):
        for chip in range(4):
            for cp in self._first(which, chip):
                cp.wait_send()
        for n, a in self._pick(which):
            for cp in self._hops(n, a):
                cp.wait_send()


def _direct_exchange(src_of, dst_of, send_sems, recv_sems):
    me = _me()
    copies = [_remote(src_of(_peer(r)), dst_of(me), send_sems.at[r - 1], recv_sems.at[r - 1], _peer(r))
              for r in range(1, N_DEV)]
    for cp in copies:
        cp.start()
    return copies


def _wait_direct(copies):
    for cp in copies:
        cp.wait_recv()
    for cp in copies:
        cp.wait_send()


def _forward(x, c, w_ada, b_ada, small, ln_g, ln_b, mine, following):
    assert DEPTH == 2
    cols = w_ada.shape[2]
    shard = mine[0].shape[0]
    pair = 2 * shard

    def body(x_hbm, c_ref, wada_hbm, bada_ref, wpool_ref, pscale_ref, slng_ref, slnb_ref, wsgu_ref, bsgut_ref,
             lng_ref, lnb_ref, wint_hbm, wout_hbm, next_in_hbm, next_out_hbm,
             out0_ref, y0_ref, cdf0_ref, proj0_hbm, out1_ref, y1_ref, cdf1_ref, proj1_hbm,
             wint_keep, wout_keep, wint_next, wout_next, acts_ref, mod_ref,
             wint_v, wout_v, h_buf, proj_blk, proj_tile, halo_ref, x_ref, w_send, w_recv, w_local, p_sems,
             t_sems, in_sems, act_all, act_src, part, mod_recv, wada_ref, a_send, a_recv, m_send, m_recv,
             n_send, n_recv, n_local, f_sems):
        step = pl.program_id(0)
        chip_of = lambda dev: 2 * dev[0] + dev[1]

        def hosted():
            return [_TwoLevelGather(out, n_send.at[n], n_recv.at[n], src=src)
                    for n, (out, src) in enumerate(((wint_next, next_in_hbm), (wout_next, next_out_hbm)))]

        def hosted_own():
            me = _me()
            return [pltpu.make_async_copy(g.src, g._slot(me), n_local.at[n]) for n, g in enumerate(hosted())]

        def gathers():
            return (_TwoLevelGather(wint_v, w_send.at[0], w_recv.at[0], src=wint_hbm),
                    _TwoLevelGather(wout_v, w_send.at[1], w_recv.at[1], src=wout_hbm))

        def keeps():
            return [pltpu.make_async_copy(wint_v, wint_keep, w_local.at[2]),
                    pltpu.make_async_copy(wout_v, wout_keep, w_local.at[3])]

        def tile_read(t):
            slot = t % 2
            return pltpu.make_async_copy(proj0_hbm.at[pl.ds(pl.multiple_of(t * TM, TM), TM)], proj_tile.at[slot],
                                         t_sems.at[slot])

        def first_layer_start():
            writes = []

            def project(n, dev):
                first = pl.multiple_of(chip_of(dev) * pair, pair)
                if n >= 2:
                    writes[n - 2].wait()

                @pl.loop(0, N_TILES)
                def _(t):
                    rows = pl.ds(pl.multiple_of(t * TM, TM), TM)
                    proj_blk[n % 2, rows, :] = _dot_nt(h_buf[rows, :], wint_v[pl.ds(first, pair), :])

                cp = pltpu.make_async_copy(proj_blk.at[n % 2], proj0_hbm.at[:, pl.ds(first, pair)], p_sems.at[n % 2])
                cp.start()
                writes.append(cp)

            me = _me()
            halo_ref[...] = jnp.zeros_like(halo_ref)
            gather_in, gather_out = gathers()
            own_in = pltpu.make_async_copy(wint_hbm, gather_in._slot(me), w_local.at[0])
            own_out = pltpu.make_async_copy(wout_hbm, gather_out._slot(me), w_local.at[1])
            x_load = pltpu.make_async_copy(x_hbm, x_ref, in_sems.at[0])
            x_load.start()
            wada_load = pltpu.make_async_copy(wada_hbm, wada_ref, in_sems.at[1])
            wada_load.start()
            mine_index = _index(me)
            cval = c_ref[...]
            act_src[...] = jnp.zeros_like(act_src)
            act_src[0:1, :] = cval * jax.nn.sigmoid(cval)
            act_all[mine_index] = act_src[...]
            act_copies = _direct_exchange(lambda p: act_src, lambda m: act_all.at[_index(m)], a_send, a_recv)

            own_in.start()
            own_out.start()
            gather_in.send_first()

            _wait_direct(act_copies)
            acts = jnp.concatenate([act_all[j, 0:1, :] for j in range(N_DEV)], axis=0)
            acts_ref[...] = acts
            part[...] = jnp.zeros_like(part)
            wada_load.wait()
            for l in range(DEPTH):
                res = lax.dot_general(acts, wada_ref[l], (((1,), (0,)), ((), ())), preferred_element_type=F32,
                                      precision=lax.Precision.HIGHEST)
                for b in range(N_DEV):
                    part[b, l:l + 1, :] = res[b:b + 1, :]
            mod_recv[mine_index] = part[mine_index]
            mod_copies = _direct_exchange(lambda p: part.at[_index(p)], lambda m: mod_recv.at[_index(m)],
                                          m_send, m_recv)
            gather_in.send_second()
            gather_out.send_mine()

            _wait_direct(mod_copies)
            for l in range(DEPTH):
                for j in range(N_DEV):
                    sl = slice(j * cols, (j + 1) * cols)
                    mod_ref[l:l + 1, sl] = mod_recv[j, l:l + 1, :] + bada_ref[l:l + 1, sl]
            x_load.wait()
            shift = mod_ref[0:1, 0:D_MODEL]
            scale = mod_ref[0:1, D_MODEL:2 * D_MODEL]

            @pl.loop(0, N_TILES)
            def _(t):
                rows = pl.ds(pl.multiple_of(t * TM, TM), TM)
                xn, _ = _ln(x_ref[rows, :])
                h_buf[rows, :] = (xn * (1.0 + scale) + shift).astype(BF16)

            gather_in.relay()
            own_in.wait()
            gather_in.wait_sibling()
            project(0, me)
            gather_in.pass_near()
            gather_in.wait_passed(ACROSS_X)
            project(1, _peer(ACROSS_X))
            gather_out.relay()
            for cp in hosted_own():
                cp.start()
            for g in hosted():
                g.send_mine()
            gather_in.wait_passed(ACROSS_Y)
            project(2, _peer(ACROSS_Y))
            gather_in.pass_far()
            gather_in.wait_passed(ACROSS_BOTH)
            project(3, _peer(ACROSS_BOTH))

            gather_out.pass_on()
            gather_out.wait_rest()
            own_out.wait()
            for cp in keeps():
                cp.start()
            writes[2].wait()
            writes[3].wait()
            tile_read(0).start()

        def fetches():
            return [pltpu.make_async_copy(wint_next, wint_v, f_sems.at[0]),
                    pltpu.make_async_copy(wout_next, wout_v, f_sems.at[1])]

        def tile_write(t):
            slot = t % 2
            return pltpu.make_async_copy(proj_tile.at[slot], proj1_hbm.at[pl.ds(pl.multiple_of(t * TM, TM), TM)],
                                         t_sems.at[slot])

        def mix_and_close(layer, tile, rows, proj, out_ref, y_ref, cdf_ref):
            weights = _MixWeights(layer, wpool_ref, pscale_ref, slng_ref, slnb_ref, wsgu_ref, bsgut_ref)
            xt = x_ref[rows, :]
            gate = mod_ref[layer:layer + 1, 2 * D_MODEL:]
            cat, cdf_ref[...] = _mix_forward(proj, halo_ref[...], tile, weights)
            halo_ref[...] = proj[TM - HALO:, 0:D_POOL]
            if layer == 1:
                @pl.when(tile == 0)
                def _():
                    fetches()[1].wait()
            y = _dot(cat.astype(BF16), wout_v[...])
            y_ref[...] = y
            zn, _ = _ln(ALPHA * xt + gate * y)
            out = zn * lng_ref[layer:layer + 1, :] + lnb_ref[layer:layer + 1, :]
            out_ref[...] = out
            if layer == 0:
                x_ref[rows, :] = out

        def first_layer_tile(tile):
            @pl.when(tile + 1 < N_TILES)
            def _():
                tile_read(tile + 1).start()

            tile_read(tile).wait()
            rows = pl.ds(pl.multiple_of(tile * TM, TM), TM)
            mix_and_close(0, tile, rows, proj_tile[tile % 2], out0_ref, y0_ref, cdf0_ref)

        def second_layer_tile(tile):
            rows = pl.ds(pl.multiple_of(tile * TM, TM), TM)
            shift = mod_ref[1:2, 0:D_MODEL]
            scale = mod_ref[1:2, D_MODEL:2 * D_MODEL]
            xn, _ = _ln(x_ref[rows, :])
            h = (xn * (1.0 + scale) + shift).astype(BF16)

            @pl.when(tile >= 2)
            def _():
                tile_write(tile - 2).wait()

            proj_tile[tile % 2] = _dot_nt(h, wint_v[...])
            tile_write(tile).start()
            mix_and_close(1, tile, rows, proj_tile[tile % 2], out1_ref, y1_ref, cdf1_ref)

        @pl.when(step < N_TILES)
        def _():
            @pl.when(step == 0)
            def _():
                first_layer_start()

            @pl.when(step == 1)
            def _():
                for g in hosted():
                    g.relay()

            @pl.when(step == N_TILES // 2)
            def _():
                for g in hosted():
                    g.pass_near()

            @pl.when(step == N_TILES - 1)
            def _():
                for g in hosted():
                    g.pass_far()
                for g in gathers():
                    g.wait_sends()
                for cp in keeps() + hosted_own():
                    cp.wait()
                hosted()[0].wait_rest()
                fetches()[0].start()

            first_layer_tile(step)

        @pl.when(step >= N_TILES)
        def _():
            @pl.when(step == N_TILES)
            def _():
                halo_ref[...] = jnp.zeros_like(halo_ref)
                hosted()[1].wait_rest()
                fetches()[1].start()
                fetches()[0].wait()

            second_layer_tile(step - N_TILES)

            @pl.when(step == 2 * N_TILES - 1)
            def _():
                for g in hosted():
                    g.wait_sends()
                tile_write(N_TILES - 2).wait()
                tile_write(N_TILES - 1).wait()

    first = lambda w: pl.BlockSpec((TM, w), lambda i: (jnp.minimum(i, N_TILES - 1), 0))
    second = lambda w: pl.BlockSpec((TM, w), lambda i: (jnp.maximum(i - N_TILES, 0), 0))
    gather_sems = pltpu.SemaphoreType.DMA((2, GATHER_SEMS))
    seven = pltpu.SemaphoreType.DMA((7,))
    per_layer = [jax.ShapeDtypeStruct((SEQ, D_MODEL), F32), jax.ShapeDtypeStruct((SEQ, D_MODEL), F32),
                 jax.ShapeDtypeStruct((SEQ, 2 * D_SGU), F32), jax.ShapeDtypeStruct((SEQ, D_IN), F32)]
    gathered = [jax.ShapeDtypeStruct((D_IN, D_MODEL), BF16), jax.ShapeDtypeStruct((D_MODEL, D_MODEL), BF16)]
    res = pl.pallas_call(
        body,
        name="layers_fwd",
        grid=(DEPTH * N_TILES,),
        in_specs=[ANY, _const_spec(c.shape), ANY, _const_spec(b_ada.shape)] + [_const_spec(s) for s in SMALL_SPECS]
                 + [_const_spec((DEPTH, D_MODEL)), _const_spec((DEPTH, D_MODEL))] + [ANY] * 4,
        out_specs=[first(D_MODEL), first(D_MODEL), first(2 * D_SGU), ANY,
                   second(D_MODEL), second(D_MODEL), second(2 * D_SGU), ANY] + [ANY] * 4
                  + [_const_spec((N_DEV, D_MODEL)), _const_spec((DEPTH, 3 * D_MODEL))],
        out_shape=per_layer * 2 + gathered * 2 + [jax.ShapeDtypeStruct((N_DEV, D_MODEL), F32),
                                                  jax.ShapeDtypeStruct((DEPTH, 3 * D_MODEL), F32)],
        scratch_shapes=[pltpu.VMEM((D_IN, D_MODEL), BF16), pltpu.VMEM((D_MODEL, D_MODEL), BF16),
                        pltpu.VMEM((SEQ, D_MODEL), BF16), pltpu.VMEM((2, SEQ, pair), F32),
                        pltpu.VMEM((2, TM, D_IN), F32), pltpu.VMEM((HALO, D_POOL), F32),
                        pltpu.VMEM((SEQ, D_MODEL), F32),
                        gather_sems, gather_sems, pltpu.SemaphoreType.DMA((4,)), pltpu.SemaphoreType.DMA((2,)),
                        pltpu.SemaphoreType.DMA((2,)), pltpu.SemaphoreType.DMA((2,)),
                        pltpu.VMEM((N_DEV, 8, D_MODEL), F32), pltpu.VMEM((8, D_MODEL), F32),
                        pltpu.VMEM((N_DEV, 8, cols), F32), pltpu.VMEM((N_DEV, 8, cols), F32),
                        pltpu.VMEM(w_ada.shape, F32), seven, seven, seven, seven,
                        gather_sems, gather_sems, pltpu.SemaphoreType.DMA((2,)), pltpu.SemaphoreType.DMA((2,))],
        compiler_params=pltpu.CompilerParams(dimension_semantics=("arbitrary",), vmem_limit_bytes=VMEM_LIMIT),
    )(x, c, w_ada, b_ada, *small, ln_g, ln_b, *mine, *following)
    return res[0:4], res[4:8], res[8:10], res[10:12], res[12], res[13]


ADA_CHUNK = 256


def _grad_tail(dproj, h, cat, dy, small, dmod8, loss_lanes, w_ada, m_ada, v_ada, act_t, b_ada, m_bada, v_bada):
    shard_in, shard_out, shard_small = D_IN // N_DEV, D_MODEL // N_DEV, small.shape[2]
    cols = w_ada.shape[2]
    W_IN, W_OUT, SMALL = 0, 1, 2

    def body(dproj_hbm, h_hbm, cat_hbm, dy_hbm, small_hbm, dmod_ref, lanes_ref, wada_hbm, mada_hbm, vada_hbm,
             act_ref, bada_ref, mbada_ref, vbada_ref,
             gwin_ref, gwout_ref, stot_ref, loss_ref, gada_hbm, dada_hbm, nmada_hbm, nvada_hbm,
             gb_ref, db_ref, nmb_ref, nvb_ref,
             dproj_v, h_v, cat_v, dy_v, part_in, part_out, own_small, loss_src, loss_all, dmod_all, ada_in, ada_out,
             *rest):
        bufs, rest = rest[:13], rest[13:]
        load_sems, rs_sems = rest[0], rest[1:6]
        m_send, m_recv, g_send, g_recv, s_send, s_recv, ada_lsem, ada_ssem = rest[6:]
        mine = _index(_me())

        def update_ada():
            upper = (mine % 2) == 1

            def dmod_of(layer):
                rows = []
                for b in range(N_DEV):
                    r = dmod_all[b, pl.ds(4 * layer + mine // 2, 1), :]
                    rows.append(jnp.where(upper, r[:, cols:], r[:, :cols]))
                return jnp.concatenate(rows, axis=0)

            chunks = [(layer, c) for layer in range(DEPTH) for c in range(D_MODEL // ADA_CHUNK)]

            def loads(i):
                layer, c = chunks[i]
                rows = pl.ds(c * ADA_CHUNK, ADA_CHUNK)
                return [pltpu.make_async_copy(src.at[layer, rows], ada_in.at[i % 2, k], ada_lsem.at[i % 2, k])
                        for k, src in enumerate((wada_hbm, mada_hbm, vada_hbm))]

            def stores(i):
                layer, c = chunks[i]
                rows = pl.ds(c * ADA_CHUNK, ADA_CHUNK)
                return [pltpu.make_async_copy(ada_out.at[i % 2, k], dst.at[layer, rows], ada_ssem.at[i % 2, k])
                        for k, dst in enumerate((gada_hbm, dada_hbm, nmada_hbm, nvada_hbm))]

            for cp in loads(0):
                cp.start()
            dmods = {}
            for i, (layer, c) in enumerate(chunks):
                if i + 1 < len(chunks):
                    for cp in loads(i + 1):
                        cp.start()
                for cp in loads(i):
                    cp.wait()
                if i >= 2:
                    for cp in stores(i - 2):
                        cp.wait()
                if layer not in dmods:
                    dmods[layer] = dmod_of(layer)
                act = act_ref[pl.ds(c * ADA_CHUNK, ADA_CHUNK), :]
                g = act[:, 0:1] * dmods[layer][0:1, :]
                for b in range(1, N_DEV):
                    g = g + act[:, b:b + 1] * dmods[layer][b:b + 1, :]
                slot = i % 2
                delta, new_m, new_v = _adamw_math(ada_in[slot, 0], g, ada_in[slot, 1], ada_in[slot, 2])
                ada_out[slot, 0] = g
                ada_out[slot, 1] = delta
                ada_out[slot, 2] = new_m
                ada_out[slot, 3] = new_v
                for cp in stores(i):
                    cp.start()
            for i in (len(chunks) - 2, len(chunks) - 1):
                for cp in stores(i):
                    cp.wait()

            total = dmod_all[0]
            for b in range(1, N_DEV):
                total = total + dmod_all[b]
            width = total.shape[1]
            for layer in range(DEPTH):
                for q in range(4):
                    gb_ref[layer:layer + 1, q * width:(q + 1) * width] = total[4 * layer + q:4 * layer + q + 1, :]
            db_ref[...], nmb_ref[...], nvb_ref[...] = _adamw_math(bada_ref[...], gb_ref[...], mbada_ref[...],
                                                                  vbada_ref[...])

        order = (ACROSS_BOTH, ACROSS_X, ACROSS_Y, None)
        chips = [_ChipReduceScatter._chip(r) for r in order]
        loads = [pltpu.make_async_copy(s, d, load_sems.at[n]) for n, (s, d) in enumerate(
            ((cat_hbm, cat_v), (dy_hbm, dy_v), (h_hbm, h_v)))]
        loads += [pltpu.make_async_copy(dproj_hbm.at[:, pl.ds(pl.multiple_of(chip * 2 * shard_in, 2 * shard_in),
                                                             2 * shard_in)], dproj_v.at[n], load_sems.at[3 + n])
                  for n, chip in enumerate(chips)]
        for cp in loads:
            cp.start()
        arrays = [dict(part=part_in, out=gwin_ref, staged=False, sib=bufs[0], snd=bufs[1], rcv=bufs[2], relay=bufs[3]),
                  dict(part=part_out, out=gwout_ref, staged=False, sib=bufs[4], snd=bufs[5], rcv=bufs[6],
                       relay=bufs[7]),
                  dict(part=small_hbm, out=own_small, staged=True, stage=bufs[8], sib=bufs[9], snd=bufs[10],
                       rcv=bufs[11], relay=bufs[12])]
        scatter = _ChipReduceScatter(arrays, *rs_sems)
        scatter.start([SMALL])
        dmod_all[mine] = dmod_ref[...]
        dmod_copies = _direct_exchange(lambda p: dmod_ref, lambda m: dmod_all.at[_index(m)], m_send, m_recv)
        loss_src[...] = jnp.full(loss_src.shape, (0.5 / D_MODEL) * jnp.sum(lanes_ref[...]), F32)
        loss_all[mine] = loss_src[...]
        loss_copies = _direct_exchange(lambda p: loss_src, lambda m: loss_all.at[_index(m)], s_send, s_recv)

        scatter.exchange([SMALL])

        loads[0].wait()
        loads[1].wait()
        for blk in range(2):
            res = _dot_tn(cat_v[:, blk * 512:(blk + 1) * 512], dy_v[...]).astype(BF16)
            for s in range(4):
                part_out[2 * blk + s // 2, s % 2] = res[s * shard_out:(s + 1) * shard_out]
        scatter.start([W_OUT])
        scatter.fold([SMALL])

        gather = _TwoLevelGather(stot_ref, g_send, g_recv)
        loads[2].wait()
        for n, chip in enumerate(chips):
            loads[3 + n].wait()
            res = _dot_tn(dproj_v[n], h_v[...]).astype(BF16)
            part_in[chip, 0] = res[:shard_in]
            part_in[chip, 1] = res[shard_in:]
            scatter.start([W_IN], chips=[chip])
            if n == 0:
                scatter.exchange([W_OUT])
                scatter.finish([SMALL])
                stot_ref[mine] = own_small[...]
                gather.send_mine()
            if n == 1:
                scatter.send_far([W_IN])
                scatter.fold([W_OUT])
                gather.relay()
            if n == 2:
                scatter.send_near(ACROSS_X, [W_IN])
            if n == 3:
                scatter.send_near(ACROSS_Y, [W_IN])
        scatter.fold([W_IN])
        scatter.finish([W_OUT])
        _wait_direct(dmod_copies)
        update_ada()
        gather.pass_on()
        gather.wait_rest()
        _wait_direct(loss_copies)
        total = loss_all[0]
        for j in range(1, N_DEV):
            total = total + loss_all[j]
        loss_ref[...] = total
        scatter.finish([W_IN])
        gather.wait_sends()
        scatter.wait_sends()

    buffers = _ChipReduceScatter.buffers
    comm_scratch = (buffers(shard_in, D_MODEL, BF16, staged=False) + buffers(shard_out, D_MODEL, BF16, staged=False)
                    + buffers(shard_small, 128, F32))
    comm_scratch += [pltpu.SemaphoreType.DMA((7,))] + _ChipReduceScatter.semaphores(3)
    comm_scratch += [pltpu.SemaphoreType.DMA((n,)) for n in (7, 7, GATHER_SEMS, GATHER_SEMS, 7, 7)]
    comm_scratch += [pltpu.SemaphoreType.DMA((2, 3)), pltpu.SemaphoreType.DMA((2, 4))]
    return pl.pallas_call(
        body,
        name="grad_tail",
        in_specs=[ANY] * 5 + [VMEM, VMEM] + [ANY] * 3 + [VMEM] * 4,
        out_specs=[VMEM] * 4 + [ANY] * 4 + [VMEM] * 4,
        out_shape=[jax.ShapeDtypeStruct((shard_in, D_MODEL), F32), jax.ShapeDtypeStruct((shard_out, D_MODEL), F32),
                   jax.ShapeDtypeStruct((N_DEV, shard_small, 128), F32), jax.ShapeDtypeStruct((8, 128), F32)]
                  + [jax.ShapeDtypeStruct(w_ada.shape, F32)] * 4 + [jax.ShapeDtypeStruct(b_ada.shape, F32)] * 4,
        scratch_shapes=[pltpu.VMEM((4, SEQ, 2 * shard_in), BF16), pltpu.VMEM(h.shape, BF16), pltpu.VMEM(cat.shape, BF16),
                        pltpu.VMEM(dy.shape, BF16), pltpu.VMEM((4, 2, shard_in, D_MODEL), BF16),
                        pltpu.VMEM((4, 2, shard_out, D_MODEL), BF16), pltpu.VMEM((shard_small, 128), F32),
                        pltpu.VMEM((8, 128), F32), pltpu.VMEM((N_DEV, 8, 128), F32),
                        pltpu.VMEM((N_DEV,) + dmod8.shape, F32), pltpu.VMEM((2, 3, ADA_CHUNK, cols), F32),
                        pltpu.VMEM((2, 4, ADA_CHUNK, cols), F32)] + comm_scratch,
        compiler_params=pltpu.CompilerParams(vmem_limit_bytes=VMEM_LIMIT),
    )(dproj, h, cat, dy, small, dmod8, loss_lanes, w_ada, m_ada, v_ada, act_t, b_ada, m_bada, v_bada)


SMALL_NAMES = ("w_pool", "w_sgu", "pool_scale", "sgu_ln_g", "sgu_ln_b", "b_sgu", "ln_g", "ln_b")
SMALL_ROWS = (512, 512, 4, 4, 4, 4, 8, 8)


def _adamw_small(g_packed, ws, ms, vs, name):
    n = len(SMALL_NAMES)

    def body(g_ref, *refs):
        w_refs, m_refs, v_refs = refs[:n], refs[n:2 * n], refs[2 * n:3 * n]
        outs = refs[3 * n:]

        def update(p, at, g):
            delta, new_m, new_v = _adamw_math(w_refs[p][at], g, m_refs[p][at], v_refs[p][at])
            outs[p][at] = g
            outs[n + p][at] = delta
            outs[2 * n + p][at] = new_m
            outs[3 * n + p][at] = new_v

        row = 0
        for p, r in enumerate(SMALL_ROWS):
            shape = ws[p].shape
            for layer in range(DEPTH):
                first = layer * PACK_ROWS + row
                if len(shape) == 4:
                    for k in range(shape[1]):
                        update(p, (layer, k), g_ref[first + k * shape[2]:first + (k + 1) * shape[2], :])
                elif len(shape) == 3:
                    update(p, (layer,), g_ref[first:first + r, :])
                else:
                    g = jnp.concatenate([g_ref[first + k:first + k + 1, :] for k in range(r)], axis=1)
                    update(p, (slice(layer, layer + 1), slice(None)), g)
            row += r

    res = pl.pallas_call(
        body,
        name=name,
        out_shape=[jax.ShapeDtypeStruct(w.shape, F32) for w in ws] * 4,
        compiler_params=pltpu.CompilerParams(vmem_limit_bytes=VMEM_LIMIT),
    )(g_packed, *ws, *ms, *vs)
    return res[:n], res[n:2 * n], res[2 * n:3 * n], res[3 * n:]


def kernel(x, c, w_ada, b_ada, w_in, w_pool, pool_scale, sgu_ln_g, sgu_ln_b, w_sgu, b_sgu, w_out, ln_g, ln_b, loss_target, m_w_ada, m_b_ada, m_w_in, m_w_pool, m_pool_scale, m_sgu_ln_g, m_sgu_ln_b, m_w_sgu, m_b_sgu, m_w_out, m_ln_g, m_ln_b, v_w_ada, v_b_ada, v_w_in, v_w_pool, v_pool_scale, v_sgu_ln_g, v_sgu_ln_b, v_w_sgu, v_b_sgu, v_w_out, v_ln_g, v_ln_b):
    small_w = dict(w_pool=w_pool, w_sgu=w_sgu, pool_scale=pool_scale, sgu_ln_g=sgu_ln_g, sgu_ln_b=sgu_ln_b,
                   b_sgu=b_sgu, ln_g=ln_g, ln_b=ln_b)
    small_m = dict(w_pool=m_w_pool, w_sgu=m_w_sgu, pool_scale=m_pool_scale, sgu_ln_g=m_sgu_ln_g,
                   sgu_ln_b=m_sgu_ln_b, b_sgu=m_b_sgu, ln_g=m_ln_g, ln_b=m_ln_b)
    small_v = dict(w_pool=v_w_pool, w_sgu=v_w_sgu, pool_scale=v_pool_scale, sgu_ln_g=v_sgu_ln_g,
                   sgu_ln_b=v_sgu_ln_b, b_sgu=v_b_sgu, ln_g=v_ln_g, ln_b=v_ln_b)

    wint_loc = jnp.transpose(w_in, (0, 2, 1)).astype(BF16)
    wout_loc = w_out.astype(BF16)
    small = (w_pool, pool_scale, sgu_ln_g, sgu_ln_b, w_sgu, jnp.transpose(b_sgu, (0, 2, 1)))
    (out0, y0, cdf0, proj0), (cur, y1, cdf1, proj1), gathered0, gathered1, act_all, mod = _forward(
        x[0], c, w_ada, b_ada, small, ln_g, ln_b, [wint_loc[0], wout_loc[0]], [wint_loc[1], wout_loc[1]])
    w_int, w_outf = [gathered0[0], gathered1[0]], [gathered0[1], gathered1[1]]
    acts = [(x[0], proj0, y0, cdf0), (out0, proj1, y1, cdf1)]

    shard_in, shard_out = D_IN // N_DEV, D_MODEL // N_DEV
    a, b = cur, loss_target[0]
    loss_lanes, carry, pending = None, (), []
    g_w_in_t, g_w_out = [None] * DEPTH, [None] * DEPTH
    for l in reversed(range(DEPTH)):
        dx, dproj, h, cat, dy, small_grads, dmod8, lanes, *shards = _layer_backward(
            l, a, b, *acts[l], mod, w_int[l], w_outf[l], small, ln_g, l == DEPTH - 1, f"layer_bwd_{l}",
            carry=carry, reduce=pending)
        if shards:
            g_w_in_t[l + 1], g_w_out[l + 1] = shards
        if l == DEPTH - 1:
            loss_lanes = lanes
        if l > 0:
            g_in, g_out = _grad_matmuls(dproj, h, cat, dy, f"grad_w_{l}")
            pending = [g_in.reshape(4, 2, shard_in, D_MODEL), g_out.reshape(4, 2, shard_out, D_MODEL)]
        carry = (small_grads, dmod8)
        a = b = dx
    grad_x = a[None]

    (g_w_in_t[0], g_w_out[0], small_tot, loss_tile, g_w_ada, d_w_ada, nm_w_ada, nv_w_ada,
     g_b_ada, d_b_ada, nm_b_ada, nv_b_ada) = _grad_tail(
        dproj, h, cat, dy, small_grads.reshape(4, 2, DEPTH * PACK_ROWS // N_DEV, 128), dmod8, loss_lanes,
        w_ada, m_w_ada, v_w_ada, jnp.transpose(act_all), b_ada, m_b_ada, v_b_ada)
    loss = loss_tile[0, 0]

    flat = lambda t: t.reshape(-1, t.shape[-1])
    to_t = lambda t: flat(jnp.transpose(t, (0, 2, 1)))
    from_t = lambda t: jnp.transpose(t.reshape(DEPTH, shard_in, D_MODEL), (0, 2, 1))
    g_w_in, d_w_in, nm_w_in, nv_w_in = [from_t(t) for t in _adamw(to_t(w_in), g_w_in_t, to_t(m_w_in), to_t(v_w_in),
                                                                  shard_in // 2, "adamw_w_in")]
    gwout, d_w_out, nm_w_out, nv_w_out = [t.reshape(w_out.shape) for t in _adamw(
        flat(w_out), g_w_out, flat(m_w_out), flat(v_w_out), shard_out, "adamw_w_out")]
    small_out = _adamw_small(small_tot.reshape(DEPTH * PACK_ROWS, 128), [small_w[n] for n in SMALL_NAMES],
                             [small_m[n] for n in SMALL_NAMES], [small_v[n] for n in SMALL_NAMES], "adamw_small")
    gs, ds, ms, vs = [dict(zip(SMALL_NAMES, group)) for group in small_out]

    def ordered(w_ada_, b_ada_, w_in_, small, w_out_):
        return (w_ada_, b_ada_, w_in_, small["w_pool"], small["pool_scale"], small["sgu_ln_g"], small["sgu_ln_b"],
                small["w_sgu"], small["b_sgu"], w_out_, small["ln_g"], small["ln_b"])

    return (loss, grad_x,
            *ordered(g_w_ada, g_b_ada, g_w_in, gs, gwout),
            *ordered(d_w_ada, d_b_ada, d_w_in, ds, d_w_out),
            *ordered(nm_w_ada, nm_b_ada, nm_w_in, ms, nm_w_out),
            *ordered(nv_w_ada, nv_b_ada, nv_w_in, vs, nv_w_out))
```

```python
import jax
import jax.numpy as jnp
from jax import lax
from jax.experimental import pallas as pl
from jax.experimental.pallas import tpu as pltpu

F32 = jnp.float32
BF16 = jnp.bfloat16

D_MODEL = 1024
SEQ = 2048
DEPTH = 2
D_POOL = 512
D_SGU = 512
D_IN = 2560
N_GROUPS = 4
GROUP = 128
N_HEADS = 4
HEAD = 128
CHUNK = 128
WINDOWS = (2, 4, 8, 16)
ALPHA = (2.0 * DEPTH) ** 0.25
LN_EPS = 1e-5
N_DEV = 8

ADAM_LR = 0.001
ADAM_B1 = 0.9
ADAM_B2 = 0.999
ADAM_EPS = 1e-08
ADAM_WD = 0.01
ADAM_STEP = 10

TM = 256
HALO = 16
N_TILES = SEQ // TM
VMEM_LIMIT = 60 * 1024 * 1024

ROW_WPOOL = 0
ROW_WSGU = 512
ROW_PSCALE = 1024
ROW_SLNG = 1028
ROW_SLNB = 1032
ROW_BSGU = 1036
ROW_LNG = 1040
ROW_LNB = 1048
PACK_ROWS = 1088
DMOD_COLS = DEPTH * 3 * D_MODEL // 8

SQRT_HALF = 0.7071067811865476
INV_SQRT_2PI = 0.3989422804014327


def _ln(x):
    mu = jnp.mean(x, axis=-1, keepdims=True)
    xc = x - mu
    var = jnp.mean(xc * xc, axis=-1, keepdims=True)
    rstd = lax.rsqrt(var + LN_EPS)
    return xc * rstd, rstd


def _ln_bwd(dxn, xn, rstd):
    m1 = jnp.mean(dxn, axis=-1, keepdims=True)
    m2 = jnp.mean(dxn * xn, axis=-1, keepdims=True)
    return rstd * (dxn - m1 - xn * m2)


def _normal_cdf(x):
    return 0.5 * (1.0 + lax.erf(x * SQRT_HALF))


def _gelu_parts(x, cdf, with_grad):
    if not with_grad:
        return x * cdf, None
    return x * cdf, cdf + x * (INV_SQRT_2PI * jnp.exp(-0.5 * x * x))


def _silu_parts(x):
    s = jax.nn.sigmoid(x)
    return x * s, s * (1.0 + x * (1.0 - s))


def _dot(a, b):
    return lax.dot_general(a, b, (((1,), (0,)), ((), ())), preferred_element_type=F32)


def _dot_nt(a, b):
    return lax.dot_general(a, b, (((1,), (1,)), ((), ())), preferred_element_type=F32)


def _dot_tn(a, b):
    return lax.dot_general(a, b, (((0,), (0,)), ((), ())), preferred_element_type=F32)


def _row_index(tile):
    return tile * TM + lax.broadcasted_iota(jnp.int32, (TM, 1), 0)


def _window_sums(ext, forward):
    n = TM + HALO
    cur = ext
    outs = []
    for g in range(N_GROUPS):
        step = 1 << g
        cur = cur + pltpu.roll(cur, step if forward else n - step, 0)
        rows = cur[HALO:, :GROUP] if forward else cur[:TM, :GROUP]
        outs.append(rows)
        cur = cur[:, GROUP:] if g + 1 < N_GROUPS else None
    return outs


def _inverse_counts(rows):
    return [1.0 / jnp.minimum(rows + 1, w).astype(F32) for w in WINDOWS]


def _tril_bf16(w):
    t = lax.broadcasted_iota(jnp.int32, (CHUNK, CHUNK), 0)
    s = lax.broadcasted_iota(jnp.int32, (CHUNK, CHUNK), 1)
    return jnp.where(t >= s, w, 0.0).astype(BF16)


class _MixWeights:
    def __init__(self, layer, wpool_ref, pscale_ref, slng_ref, slnb_ref, wsgu_ref, bsgut_ref):
        self.layer = layer
        self.wpool_ref, self.pscale_ref, self.slng_ref, self.slnb_ref = wpool_ref, pscale_ref, slng_ref, slnb_ref
        self.wsgu_ref, self.bsgut_ref = wsgu_ref, bsgut_ref

    def pool(self, g):
        return self.wpool_ref[self.layer, g].astype(BF16)

    def pool_scale(self, g):
        return self.pscale_ref[self.layer:self.layer + 1, g * GROUP:(g + 1) * GROUP]

    def ln_gain(self, h):
        return self.slng_ref[self.layer, h:h + 1, :]

    def ln_bias(self, h):
        return self.slnb_ref[self.layer, h:h + 1, :]

    def mix(self, h):
        return _tril_bf16(self.wsgu_ref[self.layer, h])

    def mix_bias(self, h):
        return self.bsgut_ref[self.layer, :, h:h + 1]


SMALL_SPECS = ((DEPTH, N_GROUPS, GROUP, GROUP), (DEPTH, D_POOL), (DEPTH, N_HEADS, HEAD), (DEPTH, N_HEADS, HEAD),
               (DEPTH, N_HEADS, CHUNK, CHUNK), (DEPTH, CHUNK, N_HEADS))


def _mix_forward(proj, halo, tile, w, cdf=None):
    keep = cdf is not None
    rows = _row_index(tile)
    inv_counts = _inverse_counts(rows)
    xa = proj[:, 0:D_POOL]
    ga = proj[:, D_POOL:2 * D_POOL]
    sums = _window_sums(jnp.concatenate([halo, xa], axis=0), True)
    ga_act, ga_grad = _silu_parts(ga)
    pooled, pw, ya = [], [], []
    for g in range(N_GROUPS):
        sl = slice(g * GROUP, (g + 1) * GROUP)
        p = (sums[g] * inv_counts[g] - xa[:, sl]).astype(BF16)
        q = _dot(p, w.pool(g))
        pooled.append(p)
        pw.append(q)
        ya.append(q * w.pool_scale(g) * ga_act[:, sl])

    u = proj[:, 2 * D_POOL:2 * D_POOL + D_SGU]
    v = proj[:, 2 * D_POOL + D_SGU:2 * D_POOL + 2 * D_SGU]
    gb = proj[:, 2 * D_POOL + 2 * D_SGU:]
    gb_act, gb_grad = _silu_parts(gb)
    if cdf is None:
        cdf = jnp.concatenate([_normal_cdf(u), _normal_cdf(v)], axis=1)
    u_act, u_grad = _gelu_parts(u, cdf[:, :D_SGU], keep)
    v_act, v_grad = _gelu_parts(v, cdf[:, D_SGU:], keep)
    vn, vrstd, vln, mixed, yb = [], [], [], [], []
    for h in range(N_HEADS):
        sl = slice(h * HEAD, (h + 1) * HEAD)
        n_h, r_h = _ln(v_act[:, sl])
        l_h = (n_h * w.ln_gain(h) + w.ln_bias(h)).astype(BF16)
        w_h = w.mix(h)
        bias = w.mix_bias(h)
        m_h = jnp.concatenate(
            [_dot(w_h, l_h[k * CHUNK:(k + 1) * CHUNK]) + bias for k in range(TM // CHUNK)], axis=0)
        vn.append(n_h)
        vrstd.append(r_h)
        vln.append(l_h)
        mixed.append(m_h)
        yb.append(u_act[:, sl] * m_h * gb_act[:, sl])
    cat = jnp.concatenate(ya + yb, axis=1)
    if not keep:
        return cat, cdf
    return cat, dict(inv_counts=inv_counts, ga_act=ga_act, ga_grad=ga_grad, pooled=pooled, pw=pw, u_grad=u_grad,
                     v_grad=v_grad, u_act=u_act, gb_act=gb_act, gb_grad=gb_grad, vn=vn, vrstd=vrstd, vln=vln,
                     mixed=mixed)


def _const_spec(shape):
    nd = len(shape)
    return pl.BlockSpec(shape, lambda i: (0,) * nd)


VEC_LNG, VEC_LNB, VEC_POOL, VEC_SGU, VEC_SHIFT, VEC_SCALE, VEC_GATE, VEC_LOSS = range(8)


def _layer_backward(layer, a, b, x, proj, y, cdf, mod, w_int, w_outf, small, ln_g, is_last, name, carry=(),
                    reduce=()):
    n_red, n_carry = len(reduce), len(carry)
    base = layer * PACK_ROWS

    def body(a_ref, b_ref, x_ref, proj_ref, prev_ref, y_ref, cdf_ref, mod_ref, wint_ref, wout_ref, wpool_ref,
             pscale_ref, slng_ref, slnb_ref, wsgu_ref, bsgut_ref, lng_ref, *rest):
        weights = _MixWeights(layer, wpool_ref, pscale_ref, slng_ref, slnb_ref, wsgu_ref, bsgut_ref)
        carry_refs, rest = rest[:n_carry], rest[n_carry:]
        part_refs, rest = rest[:n_red], rest[n_red:]
        dx_ref, dproj_ref, h_ref, cat_ref, dy_ref, small_ref, dmod_ref, loss_ref = rest[:8]
        shard_refs, rest = rest[8:8 + n_red], rest[8 + n_red:]
        vec_ref, dmix_ref, halo_ref = rest[:3]
        step = pl.program_id(0)
        tile = N_TILES - 1 - step

        def scatter():
            bufs, sems = rest[3:3 + 5 * n_red], rest[3 + 5 * n_red:]
            arrays = [dict(part=part_refs[n], out=shard_refs[n], staged=True, stage=bufs[5 * n], sib=bufs[5 * n + 1],
                           snd=bufs[5 * n + 2], rcv=bufs[5 * n + 3], relay=bufs[5 * n + 4]) for n in range(n_red)]
            return _ChipReduceScatter(arrays, *sems)

        @pl.when(step == 0)
        def _():
            small_ref[...] = jnp.zeros_like(small_ref)
            dmod_ref[...] = jnp.zeros_like(dmod_ref)
            vec_ref[...] = jnp.zeros_like(vec_ref)
            dmix_ref[...] = jnp.zeros_like(dmix_ref)
            halo_ref[...] = jnp.zeros_like(halo_ref)
            if n_red:
                scatter().start()

        if n_red:
            @pl.when(step == 1)
            def _():
                scatter().exchange()

            @pl.when(step == N_TILES // 2)
            def _():
                scatter().fold()

        def acc(row, lo, val):
            hi = lo + val.shape[1]
            vec_ref[row:row + 1, lo:hi] += jnp.sum(val, axis=0, keepdims=True)

        xt = x_ref[...]
        yt = y_ref[...]
        shift = mod_ref[layer:layer + 1, 0:D_MODEL]
        scale = mod_ref[layer:layer + 1, D_MODEL:2 * D_MODEL]
        gate = mod_ref[layer:layer + 1, 2 * D_MODEL:]
        ln_gain = lng_ref[layer:layer + 1, :]

        zn, zrstd = _ln(ALPHA * xt + gate * yt)
        if is_last:
            diff = a_ref[...] - b_ref[...]
            acc(VEC_LOSS, 0, diff * diff)
            dout = diff * (1.0 / D_MODEL)
        else:
            dout = a_ref[...]
        acc(VEC_LNG, 0, dout * zn)
        acc(VEC_LNB, 0, dout)
        dz = _ln_bwd(dout * ln_gain, zn, zrstd)
        acc(VEC_GATE, 0, dz * yt)
        dy = (dz * gate).astype(BF16)
        dy_ref[...] = dy
        dcat = _dot_nt(dy, wout_ref[...])

        proj = proj_ref[...]
        prev = jnp.where(tile > 0, prev_ref[...], 0.0)
        cat, k = _mix_forward(proj, prev, tile, weights, cdf_ref[...])
        cat_ref[...] = cat.astype(BF16)

        dga, dq = [], []
        for g in range(N_GROUPS):
            sl = slice(g * GROUP, (g + 1) * GROUP)
            pscale = weights.pool_scale(g)
            dya = dcat[:, sl]
            dyp = dya * k["ga_act"][:, sl]
            dga.append(dya * k["pw"][g] * pscale * k["ga_grad"][:, sl])
            acc(VEC_POOL, g * GROUP, dyp * k["pw"][g])
            dpw = (dyp * pscale).astype(BF16)
            rows = pl.ds(base + ROW_WPOOL + g * GROUP, GROUP)
            small_ref[rows, :] += _dot_tn(k["pooled"][g], dpw)
            dq.append(_dot_nt(dpw, weights.pool(g)))
        dpooled = jnp.concatenate(dq, axis=1)
        scaled = jnp.concatenate([dq[g] * k["inv_counts"][g] for g in range(N_GROUPS)], axis=1)
        sums = _window_sums(jnp.concatenate([scaled, halo_ref[...]], axis=0), False)
        halo_ref[...] = scaled[0:HALO]
        dxa = jnp.concatenate(sums, axis=1) - dpooled

        du, dv, dgb = [], [], []
        for h in range(N_HEADS):
            sl = slice(h * HEAD, (h + 1) * HEAD)
            dyb = dcat[:, D_POOL + h * HEAD:D_POOL + (h + 1) * HEAD]
            m_h = k["mixed"][h]
            ug = k["u_act"][:, sl] * dyb
            du.append(dyb * m_h * k["gb_act"][:, sl] * k["u_grad"][:, sl])
            dgb.append(ug * m_h * k["gb_grad"][:, sl])
            dmixed = ug * k["gb_act"][:, sl]
            dmixed_bf = dmixed.astype(BF16)
            w_h = weights.mix(h)
            dvln_parts = []
            dmix_sum = dmix_ref[h]
            wsgu_rows = pl.ds(base + ROW_WSGU + h * CHUNK, CHUNK)
            dws = small_ref[wsgu_rows, :]
            for c in range(TM // CHUNK):
                cs = slice(c * CHUNK, (c + 1) * CHUNK)
                dmix_sum = dmix_sum + dmixed[cs]
                dws = dws + _dot_nt(dmixed_bf[cs], k["vln"][h][cs])
                dvln_parts.append(_dot_tn(w_h, dmixed_bf[cs]))
            dmix_ref[h] = dmix_sum
            small_ref[wsgu_rows, :] = dws
            dvln = jnp.concatenate(dvln_parts, axis=0)
            acc(VEC_SGU, h * HEAD, dvln * k["vn"][h])
            acc(VEC_SGU, D_SGU + h * HEAD, dvln)
            dvv = _ln_bwd(dvln * weights.ln_gain(h), k["vn"][h], k["vrstd"][h])
            dv.append(dvv * k["v_grad"][:, sl])

        dproj = jnp.concatenate([dxa] + dga + du + dv + dgb, axis=1).astype(BF16)
        dproj_ref[...] = dproj
        dh = _dot(dproj, wint_ref[...])

        xn, xrstd = _ln(xt)
        h_ref[...] = (xn * (1.0 + scale) + shift).astype(BF16)
        acc(VEC_SCALE, 0, dh * xn)
        acc(VEC_SHIFT, 0, dh)
        dx_ref[...] = _ln_bwd(dh * (1.0 + scale), xn, xrstd) + ALPHA * dz

        @pl.when(step == N_TILES - 1)
        def _():
            def put(row0, vec_row, lo, n):
                for r in range(n):
                    small_ref[base + row0 + r:base + row0 + r + 1, :] = (
                        vec_ref[vec_row:vec_row + 1, lo + r * 128:lo + (r + 1) * 128])

            put(ROW_PSCALE, VEC_POOL, 0, 4)
            put(ROW_SLNG, VEC_SGU, 0, 4)
            put(ROW_SLNB, VEC_SGU, D_SGU, 4)
            put(ROW_LNG, VEC_LNG, 0, 8)
            put(ROW_LNB, VEC_LNB, 0, 8)
            ones = jnp.ones((8, HEAD), F32)
            t = lax.broadcasted_iota(jnp.int32, (CHUNK, CHUNK), 0)
            s = lax.broadcasted_iota(jnp.int32, (CHUNK, CHUNK), 1)
            for h in range(N_HEADS):
                bias_rows = lax.dot_general(ones, dmix_ref[h], (((1,), (1,)), ((), ())),
                                            preferred_element_type=F32, precision=lax.Precision.HIGHEST)
                small_ref[base + ROW_BSGU + h:base + ROW_BSGU + h + 1, :] = bias_rows[0:1]
                rows = pl.ds(base + ROW_WSGU + h * CHUNK, CHUNK)
                small_ref[rows, :] = jnp.where(t >= s, small_ref[rows, :], 0.0)
            pieces = ((0, VEC_SHIFT, 0, 768),
                      (1, VEC_SHIFT, 768, 256), (1, VEC_SCALE, 0, 512),
                      (2, VEC_SCALE, 512, 512), (2, VEC_GATE, 0, 256),
                      (3, VEC_GATE, 256, 768))
            filled = [0] * 4
            for q, vec_row, lo, n in pieces:
                row = 4 * layer + q
                dmod_ref[row:row + 1, filled[q]:filled[q] + n] = vec_ref[vec_row:vec_row + 1, lo:lo + n]
                filled[q] += n
            if n_carry:
                for other in range(layer + 1, DEPTH):
                    rows = pl.ds(other * PACK_ROWS, PACK_ROWS)
                    small_ref[rows, :] = carry_refs[0][rows, :]
                    dmod_ref[4 * other:4 * other + 4, :] = carry_refs[1][4 * other:4 * other + 4, :]
            loss_ref[...] = vec_ref[VEC_LOSS:VEC_LOSS + 1, :]
            if n_red:
                scatter().finish()
                scatter().wait_sends()

    rev = lambda w: pl.BlockSpec((TM, w), lambda i: (N_TILES - 1 - i, 0))
    prev_spec = pl.BlockSpec(
        (HALO, D_POOL), lambda i: (jnp.maximum((N_TILES - 1 - i) * (TM // HALO) - 1, 0), 0))
    comm_scratch = []
    for p in reduce:
        comm_scratch += _ChipReduceScatter.buffers(p.shape[2], p.shape[3], p.dtype)
    if n_red:
        comm_scratch += _ChipReduceScatter.semaphores(n_red)
    return pl.pallas_call(
        body,
        name=name,
        grid=(N_TILES,),
        in_specs=[rev(D_MODEL), rev(D_MODEL) if is_last else pl.BlockSpec((TM, D_MODEL), lambda i: (0, 0)),
                  rev(D_MODEL), rev(D_IN), prev_spec, rev(D_MODEL), rev(2 * D_SGU),
                  _const_spec((DEPTH, 3 * D_MODEL)), _const_spec((D_IN, D_MODEL)), _const_spec((D_MODEL, D_MODEL))]
                 + [_const_spec(s) for s in SMALL_SPECS] + [_const_spec((DEPTH, D_MODEL))]
                 + [_const_spec(c.shape) for c in carry] + [ANY] * n_red,
        out_specs=[rev(D_MODEL), rev(D_IN), rev(D_MODEL), rev(D_MODEL), rev(D_MODEL),
                   _const_spec((DEPTH * PACK_ROWS, 128)), _const_spec((8, DMOD_COLS)), _const_spec((1, D_MODEL))]
                  + [_const_spec(p.shape[2:]) for p in reduce],
        out_shape=[jax.ShapeDtypeStruct((SEQ, D_MODEL), F32), jax.ShapeDtypeStruct((SEQ, D_IN), BF16),
                   jax.ShapeDtypeStruct((SEQ, D_MODEL), BF16), jax.ShapeDtypeStruct((SEQ, D_MODEL), BF16),
                   jax.ShapeDtypeStruct((SEQ, D_MODEL), BF16), jax.ShapeDtypeStruct((DEPTH * PACK_ROWS, 128), F32),
                   jax.ShapeDtypeStruct((8, DMOD_COLS), F32), jax.ShapeDtypeStruct((1, D_MODEL), F32)]
                  + [jax.ShapeDtypeStruct(p.shape[2:], F32) for p in reduce],
        scratch_shapes=[pltpu.VMEM((8, D_MODEL), F32), pltpu.VMEM((N_HEADS, CHUNK, HEAD), F32),
                        pltpu.VMEM((HALO, D_POOL), F32)] + comm_scratch,
        compiler_params=pltpu.CompilerParams(dimension_semantics=("arbitrary",), vmem_limit_bytes=VMEM_LIMIT),
    )(a, b, x, proj, proj, y, cdf, mod, w_int, w_outf, *small, ln_g, *carry, *reduce)


def _grad_matmuls(dproj, h, cat, dy, name):
    in_cols, out_cols = D_IN // 4, D_MODEL // 2
    in_steps = D_IN // in_cols

    def body(dproj_ref, h_ref, cat_ref, dy_ref, gin_ref, gout_ref):
        step = pl.program_id(0)

        @pl.when(step < in_steps)
        def _():
            gin_ref[...] = _dot_tn(dproj_ref[...], h_ref[...]).astype(BF16)

        @pl.when(step >= in_steps)
        def _():
            gout_ref[...] = _dot_tn(cat_ref[...], dy_ref[...]).astype(BF16)

    in_block = lambda j: jnp.minimum(j, in_steps - 1)
    out_block = lambda j: jnp.maximum(j - in_steps, 0)
    return pl.pallas_call(
        body,
        name=name,
        grid=(in_steps + D_MODEL // out_cols,),
        in_specs=[pl.BlockSpec((SEQ, in_cols), lambda j: (0, in_block(j))), _const_spec((SEQ, D_MODEL)),
                  pl.BlockSpec((SEQ, out_cols), lambda j: (0, out_block(j))), _const_spec((SEQ, D_MODEL))],
        out_specs=[pl.BlockSpec((in_cols, D_MODEL), lambda j: (in_block(j), 0)),
                   pl.BlockSpec((out_cols, D_MODEL), lambda j: (out_block(j), 0))],
        out_shape=[jax.ShapeDtypeStruct((D_IN, D_MODEL), BF16), jax.ShapeDtypeStruct((D_MODEL, D_MODEL), BF16)],
        compiler_params=pltpu.CompilerParams(dimension_semantics=("arbitrary",), vmem_limit_bytes=VMEM_LIMIT),
    )(dproj, h, cat, dy)


def _adamw_math(w, g, m, v):
    m = ADAM_B1 * m + (1.0 - ADAM_B1) * g
    v = ADAM_B2 * v + (1.0 - ADAM_B2) * (g * g)
    m_hat = m / (1.0 - ADAM_B1 ** ADAM_STEP)
    v_hat = v / (1.0 - ADAM_B2 ** ADAM_STEP)
    delta = -ADAM_LR * (m_hat / (jnp.sqrt(v_hat) + ADAM_EPS) + ADAM_WD * w)
    return delta, m, v


def _adamw(w, grads, m, v, block_rows, name):
    rows, cols = grads[0].shape
    blocks = rows // block_rows

    def body(w_ref, m_ref, v_ref, *rest):
        g_refs, (g_ref, d_ref, nm_ref, nv_ref) = rest[:DEPTH], rest[DEPTH:]
        for layer in range(DEPTH):
            @pl.when(pl.program_id(0) == layer)
            def _():
                g = g_refs[layer][...]
                g_ref[...] = g
                d_ref[...], nm_ref[...], nv_ref[...] = _adamw_math(w_ref[...], g, m_ref[...], v_ref[...])

    def grad_spec(layer):
        return pl.BlockSpec((block_rows, cols),
                            lambda l, i: (jnp.where(l == layer, i, jnp.where(l < layer, 0, blocks - 1)), 0))

    spec = pl.BlockSpec((block_rows, cols), lambda l, i: (l * blocks + i, 0))
    return pl.pallas_call(
        body,
        name=name,
        grid=(DEPTH, blocks),
        in_specs=[spec] * 3 + [grad_spec(layer) for layer in range(DEPTH)],
        out_specs=[spec] * 4,
        out_shape=[jax.ShapeDtypeStruct(w.shape, F32)] * 4,
        compiler_params=pltpu.CompilerParams(dimension_semantics=("arbitrary", "arbitrary"),
                                             vmem_limit_bytes=VMEM_LIMIT),
    )(w, m, v, *grads)


MESH = pl.DeviceIdType.MESH
SIBLING = 1
ANY = pl.BlockSpec(memory_space=pl.ANY)
VMEM = pl.BlockSpec(memory_space=pltpu.VMEM)


def _me():
    return lax.axis_index("x"), lax.axis_index("y"), lax.axis_index("c")


def _peer(r):
    x, y, c = _me()
    return (1 - x if r & 4 else x, 1 - y if r & 2 else y, 1 - c if r & 1 else c)


def _index(dev):
    return 4 * dev[0] + 2 * dev[1] + dev[2]


def _remote(src, dst, send_sem, recv_sem, dev):
    return pltpu.make_async_remote_copy(src_ref=src, dst_ref=dst, send_sem=send_sem, recv_sem=recv_sem,
                                        device_id=dev, device_id_type=MESH)


ACROSS_X, ACROSS_Y, ACROSS_BOTH = 4, 2, 6
GATHER_SEMS = 11


class _TwoLevelGather:
    def __init__(self, out, send_sems, recv_sems, src=None):
        self.out, self.send_sems, self.recv_sems, self.src = out, send_sems, recv_sems, src
        self.rows = (out.shape[0] // N_DEV) if len(out.shape) == 2 else out.shape[1]
        self.half = self.rows // 2

    def _slot(self, block):
        if len(self.out.shape) == 2:
            return self.out.at[pl.ds(pl.multiple_of(_index(block) * self.rows, self.rows), self.rows)]
        return self.out.at[_index(block)]

    def _copy(self, k, block, part, to, src=None):
        slot = self._slot(block)
        if part is not None:
            rows = pl.ds(part * self.half, self.half)
            slot = slot.at[rows]
            src = None if src is None else src.at[rows]
        return _remote(slot if src is None else src, slot, self.send_sems.at[k], self.recv_sems.at[k], to)

    def _mine(self):
        me = _me()
        src = self._slot(me) if self.src is None else self.src
        x, y = _peer(ACROSS_X), _peer(ACROSS_Y)
        return [self._copy(1, me, 0, x, src), self._copy(3, me, 1, y, src), self._copy(0, me, None, _peer(SIBLING), src),
                self._copy(2, me, 1, x, src), self._copy(4, me, 0, y, src)]

    def _relayed(self):
        return [self._copy(5, _peer(ACROSS_X), 0, _peer(ACROSS_Y)), self._copy(6, _peer(ACROSS_Y), 1, _peer(ACROSS_X))]

    def _passed(self):
        sib, far = _peer(SIBLING), _peer(ACROSS_BOTH)
        return [self._copy(7, _peer(ACROSS_X), None, sib), self._copy(8, _peer(ACROSS_Y), None, sib),
                self._copy(9, far, 0, sib), self._copy(10, far, 1, sib)]

    def _arrival(self, k, r, part):
        return self._copy(k, _peer(r), part, _me())

    def send_first(self):
        for cp in self._mine()[:3]:
            cp.start()

    def send_second(self):
        for cp in self._mine()[3:]:
            cp.start()

    def send_mine(self):
        self.send_first()
        self.send_second()

    def relay(self):
        relayed = self._relayed()
        self._arrival(1, ACROSS_X, 0).wait_recv()
        relayed[0].start()
        self._arrival(3, ACROSS_Y, 1).wait_recv()
        relayed[1].start()

    def pass_near(self):
        passed = self._passed()
        self._arrival(2, ACROSS_X, 1).wait_recv()
        passed[0].start()
        self._arrival(4, ACROSS_Y, 0).wait_recv()
        passed[1].start()

    def pass_far(self):
        passed = self._passed()
        self._arrival(5, ACROSS_BOTH, 0).wait_recv()
        passed[2].start()
        self._arrival(6, ACROSS_BOTH, 1).wait_recv()
        passed[3].start()

    def pass_on(self):
        self.pass_near()
        self.pass_far()

    def wait_sibling(self):
        self._arrival(0, SIBLING, None).wait_recv()

    def wait_passed(self, r):
        if r == ACROSS_BOTH:
            self._arrival(9, r ^ SIBLING, 0).wait_recv()
            self._arrival(10, r ^ SIBLING, 1).wait_recv()
        else:
            self._arrival(7 if r == ACROSS_X else 8, r ^ SIBLING, None).wait_recv()

    def wait_rest(self):
        self.wait_sibling()
        for r in (ACROSS_X, ACROSS_Y, ACROSS_BOTH):
            self.wait_passed(r)

    def wait_sends(self):
        for cp in self._mine() + self._relayed() + self._passed():
            cp.wait_send()


class _ChipReduceScatter:
    SLOTS = 6

    def __init__(self, arrays, l_sem, d_send, d_recv, i_send, i_recv):
        self.arrays = arrays
        self.l_sem, self.d_send, self.d_recv, self.i_send, self.i_recv = l_sem, d_send, d_recv, i_send, i_recv

    @staticmethod
    def buffers(rows, cols, dtype, staged=True):
        stage = [pltpu.VMEM((4, rows, cols), dtype)] if staged else []
        return stage + [pltpu.VMEM((4, rows, cols), dtype), pltpu.VMEM((3, rows, cols), dtype),
                        pltpu.VMEM((2, rows, cols), dtype), pltpu.VMEM((2, rows // 2, cols), dtype)]

    @classmethod
    def semaphores(cls, n):
        return [pltpu.SemaphoreType.DMA((n,)), pltpu.SemaphoreType.DMA((n, 4)), pltpu.SemaphoreType.DMA((n, 4)),
                pltpu.SemaphoreType.DMA((n, cls.SLOTS)), pltpu.SemaphoreType.DMA((n, cls.SLOTS))]

    def _pick(self, which):
        return list(enumerate(self.arrays)) if which is None else [(n, self.arrays[n]) for n in which]

    @staticmethod
    def _chip(r):
        dev = _me() if r is None else _peer(r)
        return 2 * dev[0] + dev[1]

    def _staging(self, which):
        c = _me()[2]
        return [pltpu.make_async_copy(a["part"].at[pl.ds(0, 4), c], a["stage"], self.l_sem.at[n])
                for n, a in self._pick(which) if a["staged"]]

    def _first(self, which, chip):
        other = 1 - _me()[2]
        return [_remote(a["part"].at[chip, other], a["sib"].at[chip], self.d_send.at[n, chip], self.d_recv.at[n, chip],
                        _peer(SIBLING)) for n, a in self._pick(which)]

    @staticmethod
    def _halves(a):
        half = a["rcv"].shape[1] // 2
        return pl.ds(0, half), pl.ds(half, half)

    def _hops(self, n, a):
        h0, h1 = self._halves(a)
        x, y = _peer(ACROSS_X), _peer(ACROSS_Y)
        snd, rcv, relay = a["snd"], a["rcv"], a["relay"]
        pairs = [(snd.at[2, h0], relay.at[0], x), (snd.at[2, h1], relay.at[1], y),
                 (snd.at[0, h0], rcv.at[0, h0], x), (snd.at[0, h1], rcv.at[0, h1], x),
                 (snd.at[1, h1], rcv.at[1, h1], y), (snd.at[1, h0], rcv.at[1, h0], y)]
        return [_remote(s, d, self.i_send.at[n, k], self.i_recv.at[n, k], to) for k, (s, d, to) in enumerate(pairs)]

    def _mine(self, a, chip, rows=None):
        src = a["stage"].at[chip] if a["staged"] else a["part"].at[chip, _me()[2]]
        mine, sib = (src[...], a["sib"][chip]) if rows is None else (src[rows, :], a["sib"][chip, rows, :])
        return mine.astype(F32) + sib.astype(F32)

    def start(self, which=None, chips=None):
        if chips is None:
            for cp in self._staging(which):
                cp.start()
        for chip in range(4) if chips is None else chips:
            for cp in self._first(which, chip):
                cp.start()

    def send_far(self, which=None):
        far = self._chip(ACROSS_BOTH)
        for cp in self._staging(which):
            cp.wait()
        for cp in self._first(which, far):
            cp.wait_recv()
        for n, a in self._pick(which):
            hops = self._hops(n, a)
            a["snd"][2] = self._mine(a, far).astype(a["snd"].dtype)
            hops[0].start()
            hops[1].start()

    def send_near(self, r, which=None):
        chip = self._chip(r)
        for cp in self._first(which, chip):
            cp.wait_recv()
        for n, a in self._pick(which):
            h0, h1 = self._halves(a)
            hops = self._hops(n, a)
            if r == ACROSS_X:
                a["snd"][0, h0, :] = self._mine(a, chip, h0).astype(a["snd"].dtype)
                hops[2].start()
            else:
                a["snd"][1, h1, :] = self._mine(a, chip, h1).astype(a["snd"].dtype)
                hops[4].start()

    def exchange(self, which=None):
        self.send_far(which)
        self.send_near(ACROSS_X, which)
        self.send_near(ACROSS_Y, which)

    def fold(self, which=None):
        across_x, across_y = self._chip(ACROSS_X), self._chip(ACROSS_Y)
        for n, a in self._pick(which):
            h0, h1 = self._halves(a)
            hops = self._hops(n, a)
            dtype = a["snd"].dtype
            hops[1].wait_recv()
            a["snd"][0, h1, :] = (self._mine(a, across_x, h1) + a["relay"][1].astype(F32)).astype(dtype)
            hops[3].start()
            hops[0].wait_recv()
            a["snd"][1, h0, :] = (self._mine(a, across_y, h0) + a["relay"][0].astype(F32)).astype(dtype)
            hops[5].start()

    def finish(self, which=None):
        home = self._chip(None)
        for cp in self._first(which, home):
            cp.wait_recv()
        for n, a in self._pick(which):
            hops = self._hops(n, a)
            a["out"][...] = self._mine(a, home)
            hops[2].wait_recv()
            hops[3].wait_recv()
            a["out"][...] += a["rcv"][0].astype(F32)
            hops[4].wait_recv()
            hops[5].wait_recv()
            a["out"][...] += a["rcv"][1].astype(F32)

    def wait_sends(self, which=None):
        for chip in range(4):
            for cp in self._first(which, chip):
                cp.wait_send()
        for n, a in self._pick(which):
            for cp in self._hops(n, a):
                cp.wait_send()


def _direct_exchange(src_of, dst_of, send_sems, recv_sems):
    me = _me()
    copies = [_remote(src_of(_peer(r)), dst_of(me), send_sems.at[r - 1], recv_sems.at[r - 1], _peer(r))
              for r in range(1, N_DEV)]
    for cp in copies:
        cp.start()
    return copies


def _wait_direct(copies):
    for cp in copies:
        cp.wait_recv()
    for cp in copies:
        cp.wait_send()


def _forward(x, c, w_ada, b_ada, small, ln_g, ln_b, mine, following):
    assert DEPTH == 2
    cols = w_ada.shape[2]
    shard = mine[0].shape[0]
    pair = 2 * shard

    def body(x_hbm, c_ref, wada_hbm, bada_ref, wpool_ref, pscale_ref, slng_ref, slnb_ref, wsgu_ref, bsgut_ref,
             lng_ref, lnb_ref, wint_hbm, wout_hbm, next_in_hbm, next_out_hbm,
             out0_ref, y0_ref, cdf0_ref, proj0_hbm, out1_ref, y1_ref, cdf1_ref, proj1_hbm,
             wint_keep, wout_keep, wint_next, wout_next, acts_ref, mod_ref,
             wint_v, wout_v, h_buf, proj_blk, proj_tile, halo_ref, x_ref, w_send, w_recv, w_local, p_sems,
             t_sems, in_sems, act_all, act_src, part, mod_recv, wada_ref, a_send, a_recv, m_send, m_recv,
             n_send, n_recv, n_local, f_sems):
        step = pl.program_id(0)
        chip_of = lambda dev: 2 * dev[0] + dev[1]

        def hosted():
            return [_TwoLevelGather(out, n_send.at[n], n_recv.at[n], src=src)
                    for n, (out, src) in enumerate(((wint_next, next_in_hbm), (wout_next, next_out_hbm)))]

        def hosted_own():
            me = _me()
            return [pltpu.make_async_copy(g.src, g._slot(me), n_local.at[n]) for n, g in enumerate(hosted())]

        def gathers():
            return (_TwoLevelGather(wint_v, w_send.at[0], w_recv.at[0], src=wint_hbm),
                    _TwoLevelGather(wout_v, w_send.at[1], w_recv.at[1], src=wout_hbm))

        def keeps():
            return [pltpu.make_async_copy(wint_v, wint_keep, w_local.at[2]),
                    pltpu.make_async_copy(wout_v, wout_keep, w_local.at[3])]

        def tile_read(t):
            slot = t % 2
            return pltpu.make_async_copy(proj0_hbm.at[pl.ds(pl.multiple_of(t * TM, TM), TM)], proj_tile.at[slot],
                                         t_sems.at[slot])

        def first_layer_start():
            writes = []

            def project(n, dev):
                first = pl.multiple_of(chip_of(dev) * pair, pair)
                if n >= 2:
                    writes[n - 2].wait()

                @pl.loop(0, N_TILES)
                def _(t):
                    rows = pl.ds(pl.multiple_of(t * TM, TM), TM)
                    proj_blk[n % 2, rows, :] = _dot_nt(h_buf[rows, :], wint_v[pl.ds(first, pair), :])

                cp = pltpu.make_async_copy(proj_blk.at[n % 2], proj0_hbm.at[:, pl.ds(first, pair)], p_sems.at[n % 2])
                cp.start()
                writes.append(cp)

            me = _me()
            halo_ref[...] = jnp.zeros_like(halo_ref)
            gather_in, gather_out = gathers()
            own_in = pltpu.make_async_copy(wint_hbm, gather_in._slot(me), w_local.at[0])
            own_out = pltpu.make_async_copy(wout_hbm, gather_out._slot(me), w_local.at[1])
            x_load = pltpu.make_async_copy(x_hbm, x_ref, in_sems.at[0])
            x_load.start()
            wada_load = pltpu.make_async_copy(wada_hbm, wada_ref, in_sems.at[1])
            wada_load.start()
            mine_index = _index(me)
            cval = c_ref[...]
            act_src[...] = jnp.zeros_like(act_src)
            act_src[0:1, :] = cval * jax.nn.sigmoid(cval)
            act_all[mine_index] = act_src[...]
            act_copies = _direct_exchange(lambda p: act_src, lambda m: act_all.at[_index(m)], a_send, a_recv)

            own_in.start()
            own_out.start()
            gather_in.send_first()

            _wait_direct(act_copies)
            acts = jnp.concatenate([act_all[j, 0:1, :] for j in range(N_DEV)], axis=0)
            acts_ref[...] = acts
            part[...] = jnp.zeros_like(part)
            wada_load.wait()
            for l in range(DEPTH):
                res = lax.dot_general(acts, wada_ref[l], (((1,), (0,)), ((), ())), preferred_element_type=F32,
                                      precision=lax.Precision.HIGHEST)
                for b in range(N_DEV):
                    part[b, l:l + 1, :] = res[b:b + 1, :]
            mod_recv[mine_index] = part[mine_index]
            mod_copies = _direct_exchange(lambda p: part.at[_index(p)], lambda m: mod_recv.at[_index(m)],
                                          m_send, m_recv)
            gather_in.send_second()
            gather_out.send_mine()

            _wait_direct(mod_copies)
            for l in range(DEPTH):
                for j in range(N_DEV):
                    sl = slice(j * cols, (j + 1) * cols)
                    mod_ref[l:l + 1, sl] = mod_recv[j, l:l + 1, :] + bada_ref[l:l + 1, sl]
            x_load.wait()
            shift = mod_ref[0:1, 0:D_MODEL]
            scale = mod_ref[0:1, D_MODEL:2 * D_MODEL]

            @pl.loop(0, N_TILES)
            def _(t):
                rows = pl.ds(pl.multiple_of(t * TM, TM), TM)
                xn, _ = _ln(x_ref[rows, :])
                h_buf[rows, :] = (xn * (1.0 + scale) + shift).astype(BF16)

            gather_in.relay()
            own_in.wait()
            gather_in.wait_sibling()
            project(0, me)
            gather_in.pass_near()
            gather_in.wait_passed(ACROSS_X)
            project(1, _peer(ACROSS_X))
            gather_out.relay()
            for cp in hosted_own():
                cp.start()
            for g in hosted():
                g.send_mine()
            gather_in.wait_passed(ACROSS_Y)
            project(2, _peer(ACROSS_Y))
            gather_in.pass_far()
            gather_in.wait_passed(ACROSS_BOTH)
            project(3, _peer(ACROSS_BOTH))

            gather_out.pass_on()
            gather_out.wait_rest()
            own_out.wait()
            for cp in keeps():
                cp.start()
            writes[2].wait()
            writes[3].wait()
            tile_read(0).start()

        def fetches():
            return [pltpu.make_async_copy(wint_next, wint_v, f_sems.at[0]),
                    pltpu.make_async_copy(wout_next, wout_v, f_sems.at[1])]

        def tile_write(t):
            slot = t % 2
            return pltpu.make_async_copy(proj_tile.at[slot], proj1_hbm.at[pl.ds(pl.multiple_of(t * TM, TM), TM)],
                                         t_sems.at[slot])

        def mix_and_close(layer, tile, rows, proj, out_ref, y_ref, cdf_ref):
            weights = _MixWeights(layer, wpool_ref, pscale_ref, slng_ref, slnb_ref, wsgu_ref, bsgut_ref)
            xt = x_ref[rows, :]
            gate = mod_ref[layer:layer + 1, 2 * D_MODEL:]
            cat, cdf_ref[...] = _mix_forward(proj, halo_ref[...], tile, weights)
            halo_ref[...] = proj[TM - HALO:, 0:D_POOL]
            if layer == 1:
                @pl.when(tile == 0)
                def _():
                    fetches()[1].wait()
            y = _dot(cat.astype(BF16), wout_v[...])
            y_ref[...] = y
            zn, _ = _ln(ALPHA * xt + gate * y)
            out = zn * lng_ref[layer:layer + 1, :] + lnb_ref[layer:layer + 1, :]
            out_ref[...] = out
            if layer == 0:
                x_ref[rows, :] = out

        def first_layer_tile(tile):
            @pl.when(tile + 1 < N_TILES)
            def _():
                tile_read(tile + 1).start()

            tile_read(tile).wait()
            rows = pl.ds(pl.multiple_of(tile * TM, TM), TM)
            mix_and_close(0, tile, rows, proj_tile[tile % 2], out0_ref, y0_ref, cdf0_ref)

        def second_layer_tile(tile):
            rows = pl.ds(pl.multiple_of(tile * TM, TM), TM)
            shift = mod_ref[1:2, 0:D_MODEL]
            scale = mod_ref[1:2, D_MODEL:2 * D_MODEL]
            xn, _ = _ln(x_ref[rows, :])
            h = (xn * (1.0 + scale) + shift).astype(BF16)

            @pl.when(tile >= 2)
            def _():
                tile_write(tile - 2).wait()

            proj_tile[tile % 2] = _dot_nt(h, wint_v[...])
            tile_write(tile).start()
            mix_and_close(1, tile, rows, proj_tile[tile % 2], out1_ref, y1_ref, cdf1_ref)

        @pl.when(step < N_TILES)
        def _():
            @pl.when(step == 0)
            def _():
                first_layer_start()

            @pl.when(step == 1)
            def _():
                for g in hosted():
                    g.relay()

            @pl.when(step == N_TILES // 2)
            def _():
                for g in hosted():
                    g.pass_near()

            @pl.when(step == N_TILES - 1)
            def _():
                for g in hosted():
                    g.pass_far()
                for g in gathers():
                    g.wait_sends()
                for cp in keeps() + hosted_own():
                    cp.wait()
                hosted()[0].wait_rest()
                fetches()[0].start()

            first_layer_tile(step)

        @pl.when(step >= N_TILES)
        def _():
            @pl.when(step == N_TILES)
            def _():
                halo_ref[...] = jnp.zeros_like(halo_ref)
                hosted()[1].wait_rest()
                fetches()[1].start()
                fetches()[0].wait()

            second_layer_tile(step - N_TILES)

            @pl.when(step == 2 * N_TILES - 1)
            def _():
                for g in hosted():
                    g.wait_sends()
                tile_write(N_TILES - 2).wait()
                tile_write(N_TILES - 1).wait()

    first = lambda w: pl.BlockSpec((TM, w), lambda i: (jnp.minimum(i, N_TILES - 1), 0))
    second = lambda w: pl.BlockSpec((TM, w), lambda i: (jnp.maximum(i - N_TILES, 0), 0))
    gather_sems = pltpu.SemaphoreType.DMA((2, GATHER_SEMS))
    seven = pltpu.SemaphoreType.DMA((7,))
    per_layer = [jax.ShapeDtypeStruct((SEQ, D_MODEL), F32), jax.ShapeDtypeStruct((SEQ, D_MODEL), F32),
                 jax.ShapeDtypeStruct((SEQ, 2 * D_SGU), F32), jax.ShapeDtypeStruct((SEQ, D_IN), F32)]
    gathered = [jax.ShapeDtypeStruct((D_IN, D_MODEL), BF16), jax.ShapeDtypeStruct((D_MODEL, D_MODEL), BF16)]
    res = pl.pallas_call(
        body,
        name="layers_fwd",
        grid=(DEPTH * N_TILES,),
        in_specs=[ANY, _const_spec(c.shape), ANY, _const_spec(b_ada.shape)] + [_const_spec(s) for s in SMALL_SPECS]
                 + [_const_spec((DEPTH, D_MODEL)), _const_spec((DEPTH, D_MODEL))] + [ANY] * 4,
        out_specs=[first(D_MODEL), first(D_MODEL), first(2 * D_SGU), ANY,
                   second(D_MODEL), second(D_MODEL), second(2 * D_SGU), ANY] + [ANY] * 4
                  + [_const_spec((N_DEV, D_MODEL)), _const_spec((DEPTH, 3 * D_MODEL))],
        out_shape=per_layer * 2 + gathered * 2 + [jax.ShapeDtypeStruct((N_DEV, D_MODEL), F32),
                                                  jax.ShapeDtypeStruct((DEPTH, 3 * D_MODEL), F32)],
        scratch_shapes=[pltpu.VMEM((D_IN, D_MODEL), BF16), pltpu.VMEM((D_MODEL, D_MODEL), BF16),
                        pltpu.VMEM((SEQ, D_MODEL), BF16), pltpu.VMEM((2, SEQ, pair), F32),
                        pltpu.VMEM((2, TM, D_IN), F32), pltpu.VMEM((HALO, D_POOL), F32),
                        pltpu.VMEM((SEQ, D_MODEL), F32),
                        gather_sems, gather_sems, pltpu.SemaphoreType.DMA((4,)), pltpu.SemaphoreType.DMA((2,)),
                        pltpu.SemaphoreType.DMA((2,)), pltpu.SemaphoreType.DMA((2,)),
                        pltpu.VMEM((N_DEV, 8, D_MODEL), F32), pltpu.VMEM((8, D_MODEL), F32),
                        pltpu.VMEM((N_DEV, 8, cols), F32), pltpu.VMEM((N_DEV, 8, cols), F32),
                        pltpu.VMEM(w_ada.shape, F32), seven, seven, seven, seven,
                        gather_sems, gather_sems, pltpu.SemaphoreType.DMA((2,)), pltpu.SemaphoreType.DMA((2,))],
        compiler_params=pltpu.CompilerParams(dimension_semantics=("arbitrary",), vmem_limit_bytes=VMEM_LIMIT),
    )(x, c, w_ada, b_ada, *small, ln_g, ln_b, *mine, *following)
    return res[0:4], res[4:8], res[8:10], res[10:12], res[12], res[13]


ADA_CHUNK = 256


def _grad_tail(dproj, h, cat, dy, small, dmod8, loss_lanes, w_ada, m_ada, v_ada, act_t, b_ada, m_bada, v_bada):
    shard_in, shard_out, shard_small = D_IN // N_DEV, D_MODEL // N_DEV, small.shape[2]
    cols = w_ada.shape[2]
    W_IN, W_OUT, SMALL = 0, 1, 2

    def body(dproj_hbm, h_hbm, cat_hbm, dy_hbm, small_hbm, dmod_ref, lanes_ref, wada_hbm, mada_hbm, vada_hbm,
             act_ref, bada_ref, mbada_ref, vbada_ref,
             gwin_ref, gwout_ref, stot_ref, loss_ref, gada_hbm, dada_hbm, nmada_hbm, nvada_hbm,
             gb_ref, db_ref, nmb_ref, nvb_ref,
             dproj_v, h_v, cat_v, dy_v, part_in, part_out, own_small, loss_src, loss_all, dmod_all, ada_in, ada_out,
             *rest):
        bufs, rest = rest[:13], rest[13:]
        load_sems, rs_sems = rest[0], rest[1:6]
        m_send, m_recv, g_send, g_recv, s_send, s_recv, ada_lsem, ada_ssem = rest[6:]
        mine = _index(_me())

        def update_ada():
            upper = (mine % 2) == 1

            def dmod_of(layer):
                rows = []
                for b in range(N_DEV):
                    r = dmod_all[b, pl.ds(4 * layer + mine // 2, 1), :]
                    rows.append(jnp.where(upper, r[:, cols:], r[:, :cols]))
                return jnp.concatenate(rows, axis=0)

            chunks = [(layer, c) for layer in range(DEPTH) for c in range(D_MODEL // ADA_CHUNK)]

            def loads(i):
                layer, c = chunks[i]
                rows = pl.ds(c * ADA_CHUNK, ADA_CHUNK)
                return [pltpu.make_async_copy(src.at[layer, rows], ada_in.at[i % 2, k], ada_lsem.at[i % 2, k])
                        for k, src in enumerate((wada_hbm, mada_hbm, vada_hbm))]

            def stores(i):
                layer, c = chunks[i]
                rows = pl.ds(c * ADA_CHUNK, ADA_CHUNK)
                return [pltpu.make_async_copy(ada_out.at[i % 2, k], dst.at[layer, rows], ada_ssem.at[i % 2, k])
                        for k, dst in enumerate((gada_hbm, dada_hbm, nmada_hbm, nvada_hbm))]

            for cp in loads(0):
                cp.start()
            dmods = {}
            for i, (layer, c) in enumerate(chunks):
                if i + 1 < len(chunks):
                    for cp in loads(i + 1):
                        cp.start()
                for cp in loads(i):
                    cp.wait()
                if i >= 2:
                    for cp in stores(i - 2):
                        cp.wait()
                if layer not in dmods:
                    dmods[layer] = dmod_of(layer)
                act = act_ref[pl.ds(c * ADA_CHUNK, ADA_CHUNK), :]
                g = act[:, 0:1] * dmods[layer][0:1, :]
                for b in range(1, N_DEV):
                    g = g + act[:, b:b + 1] * dmods[layer][b:b + 1, :]
                slot = i % 2
                delta, new_m, new_v = _adamw_math(ada_in[slot, 0], g, ada_in[slot, 1], ada_in[slot, 2])
                ada_out[slot, 0] = g
                ada_out[slot, 1] = delta
                ada_out[slot, 2] = new_m
                ada_out[slot, 3] = new_v
                for cp in stores(i):
                    cp.start()
            for i in (len(chunks) - 2, len(chunks) - 1):
                for cp in stores(i):
                    cp.wait()

            total = dmod_all[0]
            for b in range(1, N_DEV):
                total = total + dmod_all[b]
            width = total.shape[1]
            for layer in range(DEPTH):
                for q in range(4):
                    gb_ref[layer:layer + 1, q * width:(q + 1) * width] = total[4 * layer + q:4 * layer + q + 1, :]
            db_ref[...], nmb_ref[...], nvb_ref[...] = _adamw_math(bada_ref[...], gb_ref[...], mbada_ref[...],
                                                                  vbada_ref[...])

        order = (ACROSS_BOTH, ACROSS_X, ACROSS_Y, None)
        chips = [_ChipReduceScatter._chip(r) for r in order]
        loads = [pltpu.make_async_copy(s, d, load_sems.at[n]) for n, (s, d) in enumerate(
            ((cat_hbm, cat_v), (dy_hbm, dy_v), (h_hbm, h_v)))]
        loads += [pltpu.make_async_copy(dproj_hbm.at[:, pl.ds(pl.multiple_of(chip * 2 * shard_in, 2 * shard_in),
                                                             2 * shard_in)], dproj_v.at[n], load_sems.at[3 + n])
                  for n, chip in enumerate(chips)]
        for cp in loads[2:] + loads[:2]:
            cp.start()
        arrays = [dict(part=part_in, out=gwin_ref, staged=False, sib=bufs[0], snd=bufs[1], rcv=bufs[2], relay=bufs[3]),
                  dict(part=part_out, out=gwout_ref, staged=False, sib=bufs[4], snd=bufs[5], rcv=bufs[6],
                       relay=bufs[7]),
                  dict(part=small_hbm, out=own_small, staged=True, stage=bufs[8], sib=bufs[9], snd=bufs[10],
                       rcv=bufs[11], relay=bufs[12])]
        scatter = _ChipReduceScatter(arrays, *rs_sems)
        scatter.start([SMALL])
        dmod_all[mine] = dmod_ref[...]
        dmod_copies = _direct_exchange(lambda p: dmod_ref, lambda m: dmod_all.at[_index(m)], m_send, m_recv)
        loss_src[...] = jnp.full(loss_src.shape, (0.5 / D_MODEL) * jnp.sum(lanes_ref[...]), F32)
        loss_all[mine] = loss_src[...]
        loss_copies = _direct_exchange(lambda p: loss_src, lambda m: loss_all.at[_index(m)], s_send, s_recv)

        gather = _TwoLevelGather(stot_ref, g_send, g_recv)
        loads[2].wait()
        for n, chip in enumerate(chips):
            loads[3 + n].wait()
            res = _dot_tn(dproj_v[n], h_v[...]).astype(BF16)
            part_in[chip, 0] = res[:shard_in]
            part_in[chip, 1] = res[shard_in:]
            scatter.start([W_IN], chips=[chip])
            if n == 0:
                scatter.exchange([SMALL])
            if n == 1:
                scatter.send_far([W_IN])
                scatter.fold([SMALL])
            if n == 2:
                scatter.send_near(ACROSS_X, [W_IN])
                scatter.finish([SMALL])
                stot_ref[mine] = own_small[...]
                gather.send_mine()
            if n == 3:
                scatter.send_near(ACROSS_Y, [W_IN])

        loads[0].wait()
        loads[1].wait()
        for blk in range(2):
            res = _dot_tn(cat_v[:, blk * 512:(blk + 1) * 512], dy_v[...]).astype(BF16)
            for s in range(4):
                part_out[2 * blk + s // 2, s % 2] = res[s * shard_out:(s + 1) * shard_out]
        scatter.start([W_OUT])
        scatter.fold([W_IN])
        scatter.exchange([W_OUT])
        gather.relay()
        _wait_direct(dmod_copies)
        update_ada()
        scatter.fold([W_OUT])
        gather.pass_on()
        gather.wait_rest()
        _wait_direct(loss_copies)
        total = loss_all[0]
        for j in range(1, N_DEV):
            total = total + loss_all[j]
        loss_ref[...] = total
        scatter.finish([W_IN])
        scatter.finish([W_OUT])
        gather.wait_sends()
        scatter.wait_sends()

    buffers = _ChipReduceScatter.buffers
    comm_scratch = (buffers(shard_in, D_MODEL, BF16, staged=False) + buffers(shard_out, D_MODEL, BF16, staged=False)
                    + buffers(shard_small, 128, F32))
    comm_scratch += [pltpu.SemaphoreType.DMA((7,))] + _ChipReduceScatter.semaphores(3)
    comm_scratch += [pltpu.SemaphoreType.DMA((n,)) for n in (7, 7, GATHER_SEMS, GATHER_SEMS, 7, 7)]
    comm_scratch += [pltpu.SemaphoreType.DMA((2, 3)), pltpu.SemaphoreType.DMA((2, 4))]
    return pl.pallas_call(
        body,
        name="grad_tail",
        in_specs=[ANY] * 5 + [VMEM, VMEM] + [ANY] * 3 + [VMEM] * 4,
        out_specs=[VMEM] * 4 + [ANY] * 4 + [VMEM] * 4,
        out_shape=[jax.ShapeDtypeStruct((shard_in, D_MODEL), F32), jax.ShapeDtypeStruct((shard_out, D_MODEL), F32),
                   jax.ShapeDtypeStruct((N_DEV, shard_small, 128), F32), jax.ShapeDtypeStruct((8, 128), F32)]
                  + [jax.ShapeDtypeStruct(w_ada.shape, F32)] * 4 + [jax.ShapeDtypeStruct(b_ada.shape, F32)] * 4,
        scratch_shapes=[pltpu.VMEM((4, SEQ, 2 * shard_in), BF16), pltpu.VMEM(h.shape, BF16), pltpu.VMEM(cat.shape, BF16),
                        pltpu.VMEM(dy.shape, BF16), pltpu.VMEM((4, 2, shard_in, D_MODEL), BF16),
                        pltpu.VMEM((4, 2, shard_out, D_MODEL), BF16), pltpu.VMEM((shard_small, 128), F32),
                        pltpu.VMEM((8, 128), F32), pltpu.VMEM((N_DEV, 8, 128), F32),
                        pltpu.VMEM((N_DEV,) + dmod8.shape, F32), pltpu.VMEM((2, 3, ADA_CHUNK, cols), F32),
                        pltpu.VMEM((2, 4, ADA_CHUNK, cols), F32)] + comm_scratch,
        compiler_params=pltpu.CompilerParams(vmem_limit_bytes=VMEM_LIMIT),
    )(dproj, h, cat, dy, small, dmod8, loss_lanes, w_ada, m_ada, v_ada, act_t, b_ada, m_bada, v_bada)


SMALL_NAMES = ("w_pool", "w_sgu", "pool_scale", "sgu_ln_g", "sgu_ln_b", "b_sgu", "ln_g", "ln_b")
SMALL_ROWS = (512, 512, 4, 4, 4, 4, 8, 8)


def _adamw_small(g_packed, ws, ms, vs, name):
    n = len(SMALL_NAMES)

    def body(g_ref, *refs):
        w_refs, m_refs, v_refs = refs[:n], refs[n:2 * n], refs[2 * n:3 * n]
        outs = refs[3 * n:]

        def update(p, at, g):
            delta, new_m, new_v = _adamw_math(w_refs[p][at], g, m_refs[p][at], v_refs[p][at])
            outs[p][at] = g
            outs[n + p][at] = delta
            outs[2 * n + p][at] = new_m
            outs[3 * n + p][at] = new_v

        row = 0
        for p, r in enumerate(SMALL_ROWS):
            shape = ws[p].shape
            for layer in range(DEPTH):
                first = layer * PACK_ROWS + row
                if len(shape) == 4:
                    for k in range(shape[1]):
                        update(p, (layer, k), g_ref[first + k * shape[2]:first + (k + 1) * shape[2], :])
                elif len(shape) == 3:
                    update(p, (layer,), g_ref[first:first + r, :])
                else:
                    g = jnp.concatenate([g_ref[first + k:first + k + 1, :] for k in range(r)], axis=1)
                    update(p, (slice(layer, layer + 1), slice(None)), g)
            row += r

    res = pl.pallas_call(
        body,
        name=name,
        out_shape=[jax.ShapeDtypeStruct(w.shape, F32) for w in ws] * 4,
        compiler_params=pltpu.CompilerParams(vmem_limit_bytes=VMEM_LIMIT),
    )(g_packed, *ws, *ms, *vs)
    return res[:n], res[n:2 * n], res[2 * n:3 * n], res[3 * n:]


def kernel(x, c, w_ada, b_ada, w_in, w_pool, pool_scale, sgu_ln_g, sgu_ln_b, w_sgu, b_sgu, w_out, ln_g, ln_b, loss_target, m_w_ada, m_b_ada, m_w_in, m_w_pool, m_pool_scale, m_sgu_ln_g, m_sgu_ln_b, m_w_sgu, m_b_sgu, m_w_out, m_ln_g, m_ln_b, v_w_ada, v_b_ada, v_w_in, v_w_pool, v_pool_scale, v_sgu_ln_g, v_sgu_ln_b, v_w_sgu, v_b_sgu, v_w_out, v_ln_g, v_ln_b):
    small_w = dict(w_pool=w_pool, w_sgu=w_sgu, pool_scale=pool_scale, sgu_ln_g=sgu_ln_g, sgu_ln_b=sgu_ln_b,
                   b_sgu=b_sgu, ln_g=ln_g, ln_b=ln_b)
    small_m = dict(w_pool=m_w_pool, w_sgu=m_w_sgu, pool_scale=m_pool_scale, sgu_ln_g=m_sgu_ln_g,
                   sgu_ln_b=m_sgu_ln_b, b_sgu=m_b_sgu, ln_g=m_ln_g, ln_b=m_ln_b)
    small_v = dict(w_pool=v_w_pool, w_sgu=v_w_sgu, pool_scale=v_pool_scale, sgu_ln_g=v_sgu_ln_g,
                   sgu_ln_b=v_sgu_ln_b, b_sgu=v_b_sgu, ln_g=v_ln_g, ln_b=v_ln_b)

    wint_loc = jnp.transpose(w_in, (0, 2, 1)).astype(BF16)
    wout_loc = w_out.astype(BF16)
    small = (w_pool, pool_scale, sgu_ln_g, sgu_ln_b, w_sgu, jnp.transpose(b_sgu, (0, 2, 1)))
    (out0, y0, cdf0, proj0), (cur, y1, cdf1, proj1), gathered0, gathered1, act_all, mod = _forward(
        x[0], c, w_ada, b_ada, small, ln_g, ln_b, [wint_loc[0], wout_loc[0]], [wint_loc[1], wout_loc[1]])
    w_int, w_outf = [gathered0[0], gathered1[0]], [gathered0[1], gathered1[1]]
    acts = [(x[0], proj0, y0, cdf0), (out0, proj1, y1, cdf1)]

    shard_in, shard_out = D_IN // N_DEV, D_MODEL // N_DEV
    a, b = cur, loss_target[0]
    loss_lanes, carry, pending = None, (), []
    g_w_in_t, g_w_out = [None] * DEPTH, [None] * DEPTH
    for l in reversed(range(DEPTH)):
        dx, dproj, h, cat, dy, small_grads, dmod8, lanes, *shards = _layer_backward(
            l, a, b, *acts[l], mod, w_int[l], w_outf[l], small, ln_g, l == DEPTH - 1, f"layer_bwd_{l}",
            carry=carry, reduce=pending)
        if shards:
            g_w_in_t[l + 1], g_w_out[l + 1] = shards
        if l == DEPTH - 1:
            loss_lanes = lanes
        if l > 0:
            g_in, g_out = _grad_matmuls(dproj, h, cat, dy, f"grad_w_{l}")
            pending = [g_in.reshape(4, 2, shard_in, D_MODEL), g_out.reshape(4, 2, shard_out, D_MODEL)]
        carry = (small_grads, dmod8)
        a = b = dx
    grad_x = a[None]

    (g_w_in_t[0], g_w_out[0], small_tot, loss_tile, g_w_ada, d_w_ada, nm_w_ada, nv_w_ada,
     g_b_ada, d_b_ada, nm_b_ada, nv_b_ada) = _grad_tail(
        dproj, h, cat, dy, small_grads.reshape(4, 2, DEPTH * PACK_ROWS // N_DEV, 128), dmod8, loss_lanes,
        w_ada, m_w_ada, v_w_ada, jnp.transpose(act_all), b_ada, m_b_ada, v_b_ada)
    loss = loss_tile[0, 0]

    flat = lambda t: t.reshape(-1, t.shape[-1])
    to_t = lambda t: flat(jnp.transpose(t, (0, 2, 1)))
    from_t = lambda t: jnp.transpose(t.reshape(DEPTH, shard_in, D_MODEL), (0, 2, 1))
    g_w_in, d_w_in, nm_w_in, nv_w_in = [from_t(t) for t in _adamw(to_t(w_in), g_w_in_t, to_t(m_w_in), to_t(v_w_in),
                                                                  shard_in // 2, "adamw_w_in")]
    gwout, d_w_out, nm_w_out, nv_w_out = [t.reshape(w_out.shape) for t in _adamw(
        flat(w_out), g_w_out, flat(m_w_out), flat(v_w_out), shard_out, "adamw_w_out")]
    small_out = _adamw_small(small_tot.reshape(DEPTH * PACK_ROWS, 128), [small_w[n] for n in SMALL_NAMES],
                             [small_m[n] for n in SMALL_NAMES], [small_v[n] for n in SMALL_NAMES], "adamw_small")
    gs, ds, ms, vs = [dict(zip(SMALL_NAMES, group)) for group in small_out]

    def ordered(w_ada_, b_ada_, w_in_, small, w_out_):
        return (w_ada_, b_ada_, w_in_, small["w_pool"], small["pool_scale"], small["sgu_ln_g"], small["sgu_ln_b"],
                small["w_sgu"], small["b_sgu"], w_out_, small["ln_g"], small["ln_b"])

    return (loss, grad_x,
            *ordered(g_w_ada, g_b_ada, g_w_in, gs, gwout),
            *ordered(d_w_ada, d_b_ada, d_w_in, ds, d_w_out),
            *ordered(nm_w_ada, nm_b_ada, nm_w_in, ms, nm_w_out),
            *ordered(nv_w_ada, nv_b_ada, nv_w_in, vs, nv_w_out))
```

```python
import jax
import jax.numpy as jnp
from jax import lax
from jax.experimental import pallas as pl
from jax.experimental.pallas import tpu as pltpu

F32 = jnp.float32
BF16 = jnp.bfloat16

D_MODEL = 1024
SEQ = 2048
DEPTH = 2
D_POOL = 512
D_SGU = 512
D_IN = 2560
N_GROUPS = 4
GROUP = 128
N_HEADS = 4
HEAD = 128
CHUNK = 128
WINDOWS = (2, 4, 8, 16)
ALPHA = (2.0 * DEPTH) ** 0.25
LN_EPS = 1e-5
N_DEV = 8

ADAM_LR = 0.001
ADAM_B1 = 0.9
ADAM_B2 = 0.999
ADAM_EPS = 1e-08
ADAM_WD = 0.01
ADAM_STEP = 10

TM = 256
HALO = 16
N_TILES = SEQ // TM
VMEM_LIMIT = 60 * 1024 * 1024

ROW_WPOOL = 0
ROW_WSGU = 512
ROW_PSCALE = 1024
ROW_SLNG = 1028
ROW_SLNB = 1032
ROW_BSGU = 1036
ROW_LNG = 1040
ROW_LNB = 1048
PACK_ROWS = 1088
DMOD_COLS = DEPTH * 3 * D_MODEL // 8

SQRT_HALF = 0.7071067811865476
INV_SQRT_2PI = 0.3989422804014327


def _ln(x):
    mu = jnp.mean(x, axis=-1, keepdims=True)
    xc = x - mu
    var = jnp.mean(xc * xc, axis=-1, keepdims=True)
    rstd = lax.rsqrt(var + LN_EPS)
    return xc * rstd, rstd


def _ln_bwd(dxn, xn, rstd):
    m1 = jnp.mean(dxn, axis=-1, keepdims=True)
    m2 = jnp.mean(dxn * xn, axis=-1, keepdims=True)
    return rstd * (dxn - m1 - xn * m2)


def _normal_cdf(x):
    return 0.5 * (1.0 + lax.erf(x * SQRT_HALF))


def _gelu_parts(x, cdf, with_grad):
    if not with_grad:
        return x * cdf, None
    return x * cdf, cdf + x * (INV_SQRT_2PI * jnp.exp(-0.5 * x * x))


def _silu_parts(x):
    s = jax.nn.sigmoid(x)
    return x * s, s * (1.0 + x * (1.0 - s))


def _dot(a, b):
    return lax.dot_general(a, b, (((1,), (0,)), ((), ())), preferred_element_type=F32)


def _dot_nt(a, b):
    return lax.dot_general(a, b, (((1,), (1,)), ((), ())), preferred_element_type=F32)


def _dot_tn(a, b):
    return lax.dot_general(a, b, (((0,), (0,)), ((), ())), preferred_element_type=F32)


def _row_index(tile):
    return tile * TM + lax.broadcasted_iota(jnp.int32, (TM, 1), 0)


def _window_sums(ext, forward):
    n = TM + HALO
    cur = ext
    outs = []
    for g in range(N_GROUPS):
        step = 1 << g
        cur = cur + pltpu.roll(cur, step if forward else n - step, 0)
        rows = cur[HALO:, :GROUP] if forward else cur[:TM, :GROUP]
        outs.append(rows)
        cur = cur[:, GROUP:] if g + 1 < N_GROUPS else None
    return outs


def _inverse_counts(rows):
    return [1.0 / jnp.minimum(rows + 1, w).astype(F32) for w in WINDOWS]


def _tril_bf16(w):
    t = lax.broadcasted_iota(jnp.int32, (CHUNK, CHUNK), 0)
    s = lax.broadcasted_iota(jnp.int32, (CHUNK, CHUNK), 1)
    return jnp.where(t >= s, w, 0.0).astype(BF16)


class _MixWeights:
    def __init__(self, layer, wpool_ref, pscale_ref, slng_ref, slnb_ref, wsgu_ref, bsgut_ref):
        self.layer = layer
        self.wpool_ref, self.pscale_ref, self.slng_ref, self.slnb_ref = wpool_ref, pscale_ref, slng_ref, slnb_ref
        self.wsgu_ref, self.bsgut_ref = wsgu_ref, bsgut_ref

    def pool(self, g):
        return self.wpool_ref[self.layer, g].astype(BF16)

    def pool_scale(self, g):
        return self.pscale_ref[self.layer:self.layer + 1, g * GROUP:(g + 1) * GROUP]

    def ln_gain(self, h):
        return self.slng_ref[self.layer, h:h + 1, :]

    def ln_bias(self, h):
        return self.slnb_ref[self.layer, h:h + 1, :]

    def mix(self, h):
        return _tril_bf16(self.wsgu_ref[self.layer, h])

    def mix_bias(self, h):
        return self.bsgut_ref[self.layer, :, h:h + 1]


SMALL_SPECS = ((DEPTH, N_GROUPS, GROUP, GROUP), (DEPTH, D_POOL), (DEPTH, N_HEADS, HEAD), (DEPTH, N_HEADS, HEAD),
               (DEPTH, N_HEADS, CHUNK, CHUNK), (DEPTH, CHUNK, N_HEADS))


def _mix_forward(proj, halo, tile, w, cdf=None):
    keep = cdf is not None
    rows = _row_index(tile)
    inv_counts = _inverse_counts(rows)
    xa = proj[:, 0:D_POOL]
    ga = proj[:, D_POOL:2 * D_POOL]
    sums = _window_sums(jnp.concatenate([halo, xa], axis=0), True)
    ga_act, ga_grad = _silu_parts(ga)
    pooled, pw, ya = [], [], []
    for g in range(N_GROUPS):
        sl = slice(g * GROUP, (g + 1) * GROUP)
        p = (sums[g] * inv_counts[g] - xa[:, sl]).astype(BF16)
        q = _dot(p, w.pool(g))
        pooled.append(p)
        pw.append(q)
        ya.append(q * w.pool_scale(g) * ga_act[:, sl])

    u = proj[:, 2 * D_POOL:2 * D_POOL + D_SGU]
    v = proj[:, 2 * D_POOL + D_SGU:2 * D_POOL + 2 * D_SGU]
    gb = proj[:, 2 * D_POOL + 2 * D_SGU:]
    gb_act, gb_grad = _silu_parts(gb)
    if cdf is None:
        cdf = jnp.concatenate([_normal_cdf(u), _normal_cdf(v)], axis=1)
    u_act, u_grad = _gelu_parts(u, cdf[:, :D_SGU], keep)
    v_act, v_grad = _gelu_parts(v, cdf[:, D_SGU:], keep)
    vn, vrstd, vln, mixed, yb = [], [], [], [], []
    for h in range(N_HEADS):
        sl = slice(h * HEAD, (h + 1) * HEAD)
        n_h, r_h = _ln(v_act[:, sl])
        l_h = (n_h * w.ln_gain(h) + w.ln_bias(h)).astype(BF16)
        w_h = w.mix(h)
        bias = w.mix_bias(h)
        m_h = jnp.concatenate(
            [_dot(w_h, l_h[k * CHUNK:(k + 1) * CHUNK]) + bias for k in range(TM // CHUNK)], axis=0)
        vn.append(n_h)
        vrstd.append(r_h)
        vln.append(l_h)
        mixed.append(m_h)
        yb.append(u_act[:, sl] * m_h * gb_act[:, sl])
    cat = jnp.concatenate(ya + yb, axis=1)
    if not keep:
        return cat, cdf
    return cat, dict(inv_counts=inv_counts, ga_act=ga_act, ga_grad=ga_grad, pooled=pooled, pw=pw, u_grad=u_grad,
                     v_grad=v_grad, u_act=u_act, gb_act=gb_act, gb_grad=gb_grad, vn=vn, vrstd=vrstd, vln=vln,
                     mixed=mixed)


def _const_spec(shape):
    nd = len(shape)
    return pl.BlockSpec(shape, lambda i: (0,) * nd)


VEC_LNG, VEC_LNB, VEC_POOL, VEC_SGU, VEC_SHIFT, VEC_SCALE, VEC_GATE, VEC_LOSS = range(8)


def _layer_backward(layer, a, b, x, proj, y, cdf, mod, w_int, w_outf, small, ln_g, is_last, name, carry=(),
                    reduce=()):
    n_red, n_carry = len(reduce), len(carry)
    base = layer * PACK_ROWS

    def body(a_ref, b_ref, x_ref, proj_ref, prev_ref, y_ref, cdf_ref, mod_ref, wint_ref, wout_ref, wpool_ref,
             pscale_ref, slng_ref, slnb_ref, wsgu_ref, bsgut_ref, lng_ref, *rest):
        weights = _MixWeights(layer, wpool_ref, pscale_ref, slng_ref, slnb_ref, wsgu_ref, bsgut_ref)
        carry_refs, rest = rest[:n_carry], rest[n_carry:]
        part_refs, rest = rest[:n_red], rest[n_red:]
        dx_ref, dproj_ref, h_ref, cat_ref, dy_ref, small_ref, dmod_ref, loss_ref = rest[:8]
        shard_refs, rest = rest[8:8 + n_red], rest[8 + n_red:]
        vec_ref, dmix_ref, halo_ref = rest[:3]
        step = pl.program_id(0)
        tile = N_TILES - 1 - step

        def scatter():
            bufs, sems = rest[3:3 + 5 * n_red], rest[3 + 5 * n_red:]
            arrays = [dict(part=part_refs[n], out=shard_refs[n], staged=True, stage=bufs[5 * n], sib=bufs[5 * n + 1],
                           snd=bufs[5 * n + 2], rcv=bufs[5 * n + 3], relay=bufs[5 * n + 4]) for n in range(n_red)]
            return _ChipReduceScatter(arrays, *sems)

        @pl.when(step == 0)
        def _():
            small_ref[...] = jnp.zeros_like(small_ref)
            dmod_ref[...] = jnp.zeros_like(dmod_ref)
            vec_ref[...] = jnp.zeros_like(vec_ref)
            dmix_ref[...] = jnp.zeros_like(dmix_ref)
            halo_ref[...] = jnp.zeros_like(halo_ref)
            if n_red:
                scatter().start()

        if n_red:
            @pl.when(step == 1)
            def _():
                scatter().exchange()

            @pl.when(step == N_TILES // 2)
            def _():
                scatter().fold()

        def acc(row, lo, val):
            hi = lo + val.shape[1]
            vec_ref[row:row + 1, lo:hi] += jnp.sum(val, axis=0, keepdims=True)

        xt = x_ref[...]
        yt = y_ref[...]
        shift = mod_ref[layer:layer + 1, 0:D_MODEL]
        scale = mod_ref[layer:layer + 1, D_MODEL:2 * D_MODEL]
        gate = mod_ref[layer:layer + 1, 2 * D_MODEL:]
        ln_gain = lng_ref[layer:layer + 1, :]

        zn, zrstd = _ln(ALPHA * xt + gate * yt)
        if is_last:
            diff = a_ref[...] - b_ref[...]
            acc(VEC_LOSS, 0, diff * diff)
            dout = diff * (1.0 / D_MODEL)
        else:
            dout = a_ref[...]
        acc(VEC_LNG, 0, dout * zn)
        acc(VEC_LNB, 0, dout)
        dz = _ln_bwd(dout * ln_gain, zn, zrstd)
        acc(VEC_GATE, 0, dz * yt)
        dy = (dz * gate).astype(BF16)
        dy_ref[...] = dy
        dcat = _dot_nt(dy, wout_ref[...])

        proj = proj_ref[...]
        prev = jnp.where(tile > 0, prev_ref[...], 0.0)
        cat, k = _mix_forward(proj, prev, tile, weights, cdf_ref[...])
        cat_ref[...] = cat.astype(BF16)

        dga, dq = [], []
        for g in range(N_GROUPS):
            sl = slice(g * GROUP, (g + 1) * GROUP)
            pscale = weights.pool_scale(g)
            dya = dcat[:, sl]
            dyp = dya * k["ga_act"][:, sl]
            dga.append(dya * k["pw"][g] * pscale * k["ga_grad"][:, sl])
            acc(VEC_POOL, g * GROUP, dyp * k["pw"][g])
            dpw = (dyp * pscale).astype(BF16)
            rows = pl.ds(base + ROW_WPOOL + g * GROUP, GROUP)
            small_ref[rows, :] += _dot_tn(k["pooled"][g], dpw)
            dq.append(_dot_nt(dpw, weights.pool(g)))
        dpooled = jnp.concatenate(dq, axis=1)
        scaled = jnp.concatenate([dq[g] * k["inv_counts"][g] for g in range(N_GROUPS)], axis=1)
        sums = _window_sums(jnp.concatenate([scaled, halo_ref[...]], axis=0), False)
        halo_ref[...] = scaled[0:HALO]
        dxa = jnp.concatenate(sums, axis=1) - dpooled

        du, dv, dgb = [], [], []
        for h in range(N_HEADS):
            sl = slice(h * HEAD, (h + 1) * HEAD)
            dyb = dcat[:, D_POOL + h * HEAD:D_POOL + (h + 1) * HEAD]
            m_h = k["mixed"][h]
            ug = k["u_act"][:, sl] * dyb
            du.append(dyb * m_h * k["gb_act"][:, sl] * k["u_grad"][:, sl])
            dgb.append(ug * m_h * k["gb_grad"][:, sl])
            dmixed = ug * k["gb_act"][:, sl]
            dmixed_bf = dmixed.astype(BF16)
            w_h = weights.mix(h)
            dvln_parts = []
            dmix_sum = dmix_ref[h]
            wsgu_rows = pl.ds(base + ROW_WSGU + h * CHUNK, CHUNK)
            dws = small_ref[wsgu_rows, :]
            for c in range(TM // CHUNK):
                cs = slice(c * CHUNK, (c + 1) * CHUNK)
                dmix_sum = dmix_sum + dmixed[cs]
                dws = dws + _dot_nt(dmixed_bf[cs], k["vln"][h][cs])
                dvln_parts.append(_dot_tn(w_h, dmixed_bf[cs]))
            dmix_ref[h] = dmix_sum
            small_ref[wsgu_rows, :] = dws
            dvln = jnp.concatenate(dvln_parts, axis=0)
            acc(VEC_SGU, h * HEAD, dvln * k["vn"][h])
            acc(VEC_SGU, D_SGU + h * HEAD, dvln)
            dvv = _ln_bwd(dvln * weights.ln_gain(h), k["vn"][h], k["vrstd"][h])
            dv.append(dvv * k["v_grad"][:, sl])

        dproj = jnp.concatenate([dxa] + dga + du + dv + dgb, axis=1).astype(BF16)
        dproj_ref[...] = dproj
        dh = _dot(dproj, wint_ref[...])

        xn, xrstd = _ln(xt)
        h_ref[...] = (xn * (1.0 + scale) + shift).astype(BF16)
        acc(VEC_SCALE, 0, dh * xn)
        acc(VEC_SHIFT, 0, dh)
        dx_ref[...] = _ln_bwd(dh * (1.0 + scale), xn, xrstd) + ALPHA * dz

        @pl.when(step == N_TILES - 1)
        def _():
            def put(row0, vec_row, lo, n):
                for r in range(n):
                    small_ref[base + row0 + r:base + row0 + r + 1, :] = (
                        vec_ref[vec_row:vec_row + 1, lo + r * 128:lo + (r + 1) * 128])

            put(ROW_PSCALE, VEC_POOL, 0, 4)
            put(ROW_SLNG, VEC_SGU, 0, 4)
            put(ROW_SLNB, VEC_SGU, D_SGU, 4)
            put(ROW_LNG, VEC_LNG, 0, 8)
            put(ROW_LNB, VEC_LNB, 0, 8)
            ones = jnp.ones((8, HEAD), F32)
            t = lax.broadcasted_iota(jnp.int32, (CHUNK, CHUNK), 0)
            s = lax.broadcasted_iota(jnp.int32, (CHUNK, CHUNK), 1)
            for h in range(N_HEADS):
                bias_rows = lax.dot_general(ones, dmix_ref[h], (((1,), (1,)), ((), ())),
                                            preferred_element_type=F32, precision=lax.Precision.HIGHEST)
                small_ref[base + ROW_BSGU + h:base + ROW_BSGU + h + 1, :] = bias_rows[0:1]
                rows = pl.ds(base + ROW_WSGU + h * CHUNK, CHUNK)
                small_ref[rows, :] = jnp.where(t >= s, small_ref[rows, :], 0.0)
            pieces = ((0, VEC_SHIFT, 0, 768),
                      (1, VEC_SHIFT, 768, 256), (1, VEC_SCALE, 0, 512),
                      (2, VEC_SCALE, 512, 512), (2, VEC_GATE, 0, 256),
                      (3, VEC_GATE, 256, 768))
            filled = [0] * 4
            for q, vec_row, lo, n in pieces:
                row = 4 * layer + q
                dmod_ref[row:row + 1, filled[q]:filled[q] + n] = vec_ref[vec_row:vec_row + 1, lo:lo + n]
                filled[q] += n
            if n_carry:
                for other in range(layer + 1, DEPTH):
                    rows = pl.ds(other * PACK_ROWS, PACK_ROWS)
                    small_ref[rows, :] = carry_refs[0][rows, :]
                    dmod_ref[4 * other:4 * other + 4, :] = carry_refs[1][4 * other:4 * other + 4, :]
            loss_ref[...] = vec_ref[VEC_LOSS:VEC_LOSS + 1, :]
            if n_red:
                scatter().finish()
                scatter().wait_sends()

    rev = lambda w: pl.BlockSpec((TM, w), lambda i: (N_TILES - 1 - i, 0))
    prev_spec = pl.BlockSpec(
        (HALO, D_POOL), lambda i: (jnp.maximum((N_TILES - 1 - i) * (TM // HALO) - 1, 0), 0))
    comm_scratch = []
    for p in reduce:
        comm_scratch += _ChipReduceScatter.buffers(p.shape[2], p.shape[3], p.dtype)
    if n_red:
        comm_scratch += _ChipReduceScatter.semaphores(n_red)
    return pl.pallas_call(
        body,
        name=name,
        grid=(N_TILES,),
        in_specs=[rev(D_MODEL), rev(D_MODEL) if is_last else pl.BlockSpec((TM, D_MODEL), lambda i: (0, 0)),
                  rev(D_MODEL), rev(D_IN), prev_spec, rev(D_MODEL), rev(2 * D_SGU),
                  _const_spec((DEPTH, 3 * D_MODEL)), _const_spec((D_IN, D_MODEL)), _const_spec((D_MODEL, D_MODEL))]
                 + [_const_spec(s) for s in SMALL_SPECS] + [_const_spec((DEPTH, D_MODEL))]
                 + [_const_spec(c.shape) for c in carry] + [ANY] * n_red,
        out_specs=[rev(D_MODEL), rev(D_IN), rev(D_MODEL), rev(D_MODEL), rev(D_MODEL),
                   _const_spec((DEPTH * PACK_ROWS, 128)), _const_spec((8, DMOD_COLS)), _const_spec((1, D_MODEL))]
                  + [_const_spec(p.shape[2:]) for p in reduce],
        out_shape=[jax.ShapeDtypeStruct((SEQ, D_MODEL), F32), jax.ShapeDtypeStruct((SEQ, D_IN), BF16),
                   jax.ShapeDtypeStruct((SEQ, D_MODEL), BF16), jax.ShapeDtypeStruct((SEQ, D_MODEL), BF16),
                   jax.ShapeDtypeStruct((SEQ, D_MODEL), BF16), jax.ShapeDtypeStruct((DEPTH * PACK_ROWS, 128), F32),
                   jax.ShapeDtypeStruct((8, DMOD_COLS), F32), jax.ShapeDtypeStruct((1, D_MODEL), F32)]
                  + [jax.ShapeDtypeStruct(p.shape[2:], F32) for p in reduce],
        scratch_shapes=[pltpu.VMEM((8, D_MODEL), F32), pltpu.VMEM((N_HEADS, CHUNK, HEAD), F32),
                        pltpu.VMEM((HALO, D_POOL), F32)] + comm_scratch,
        compiler_params=pltpu.CompilerParams(dimension_semantics=("arbitrary",), vmem_limit_bytes=VMEM_LIMIT),
    )(a, b, x, proj, proj, y, cdf, mod, w_int, w_outf, *small, ln_g, *carry, *reduce)


def _grad_matmuls(dproj, h, cat, dy, name):
    in_cols, out_cols = D_IN // 4, D_MODEL // 2
    in_steps = D_IN // in_cols

    def body(dproj_ref, h_ref, cat_ref, dy_ref, gin_ref, gout_ref):
        step = pl.program_id(0)

        @pl.when(step < in_steps)
        def _():
            gin_ref[...] = _dot_tn(dproj_ref[...], h_ref[...]).astype(BF16)

        @pl.when(step >= in_steps)
        def _():
            gout_ref[...] = _dot_tn(cat_ref[...], dy_ref[...]).astype(BF16)

    in_block = lambda j: jnp.minimum(j, in_steps - 1)
    out_block = lambda j: jnp.maximum(j - in_steps, 0)
    return pl.pallas_call(
        body,
        name=name,
        grid=(in_steps + D_MODEL // out_cols,),
        in_specs=[pl.BlockSpec((SEQ, in_cols), lambda j: (0, in_block(j))), _const_spec((SEQ, D_MODEL)),
                  pl.BlockSpec((SEQ, out_cols), lambda j: (0, out_block(j))), _const_spec((SEQ, D_MODEL))],
        out_specs=[pl.BlockSpec((in_cols, D_MODEL), lambda j: (in_block(j), 0)),
                   pl.BlockSpec((out_cols, D_MODEL), lambda j: (out_block(j), 0))],
        out_shape=[jax.ShapeDtypeStruct((D_IN, D_MODEL), BF16), jax.ShapeDtypeStruct((D_MODEL, D_MODEL), BF16)],
        compiler_params=pltpu.CompilerParams(dimension_semantics=("arbitrary",), vmem_limit_bytes=VMEM_LIMIT),
    )(dproj, h, cat, dy)


def _adamw_math(w, g, m, v):
    m = ADAM_B1 * m + (1.0 - ADAM_B1) * g
    v = ADAM_B2 * v + (1.0 - ADAM_B2) * (g * g)
    m_hat = m / (1.0 - ADAM_B1 ** ADAM_STEP)
    v_hat = v / (1.0 - ADAM_B2 ** ADAM_STEP)
    delta = -ADAM_LR * (m_hat / (jnp.sqrt(v_hat) + ADAM_EPS) + ADAM_WD * w)
    return delta, m, v


def _adamw(w, grads, m, v, block_rows, name):
    rows, cols = grads[0].shape
    blocks = rows // block_rows

    def body(w_ref, m_ref, v_ref, *rest):
        g_refs, (g_ref, d_ref, nm_ref, nv_ref) = rest[:DEPTH], rest[DEPTH:]
        for layer in range(DEPTH):
            @pl.when(pl.program_id(0) == layer)
            def _():
                g = g_refs[layer][...]
                g_ref[...] = g
                d_ref[...], nm_ref[...], nv_ref[...] = _adamw_math(w_ref[...], g, m_ref[...], v_ref[...])

    def grad_spec(layer):
        return pl.BlockSpec((block_rows, cols),
                            lambda l, i: (jnp.where(l == layer, i, jnp.where(l < layer, 0, blocks - 1)), 0))

    spec = pl.BlockSpec((block_rows, cols), lambda l, i: (l * blocks + i, 0))
    return pl.pallas_call(
        body,
        name=name,
        grid=(DEPTH, blocks),
        in_specs=[spec] * 3 + [grad_spec(layer) for layer in range(DEPTH)],
        out_specs=[spec] * 4,
        out_shape=[jax.ShapeDtypeStruct(w.shape, F32)] * 4,
        compiler_params=pltpu.CompilerParams(dimension_semantics=("arbitrary", "arbitrary"),
                                             vmem_limit_bytes=VMEM_LIMIT),
    )(w, m, v, *grads)


MESH = pl.DeviceIdType.MESH
SIBLING = 1
ANY = pl.BlockSpec(memory_space=pl.ANY)
VMEM = pl.BlockSpec(memory_space=pltpu.VMEM)


def _me():
    return lax.axis_index("x"), lax.axis_index("y"), lax.axis_index("c")


def _peer(r):
    x, y, c = _me()
    return (1 - x if r & 4 else x, 1 - y if r & 2 else y, 1 - c if r & 1 else c)


def _index(dev):
    return 4 * dev[0] + 2 * dev[1] + dev[2]


def _remote(src, dst, send_sem, recv_sem, dev):
    return pltpu.make_async_remote_copy(src_ref=src, dst_ref=dst, send_sem=send_sem, recv_sem=recv_sem,
                                        device_id=dev, device_id_type=MESH)


ACROSS_X, ACROSS_Y, ACROSS_BOTH = 4, 2, 6
GATHER_SEMS = 11


class _TwoLevelGather:
    def __init__(self, out, send_sems, recv_sems, src=None):
        self.out, self.send_sems, self.recv_sems, self.src = out, send_sems, recv_sems, src
        self.rows = (out.shape[0] // N_DEV) if len(out.shape) == 2 else out.shape[1]
        self.half = self.rows // 2

    def _slot(self, block):
        if len(self.out.shape) == 2:
            return self.out.at[pl.ds(pl.multiple_of(_index(block) * self.rows, self.rows), self.rows)]
        return self.out.at[_index(block)]

    def _copy(self, k, block, part, to, src=None):
        slot = self._slot(block)
        if part is not None:
            rows = pl.ds(part * self.half, self.half)
            slot = slot.at[rows]
            src = None if src is None else src.at[rows]
        return _remote(slot if src is None else src, slot, self.send_sems.at[k], self.recv_sems.at[k], to)

    def _mine(self):
        me = _me()
        src = self._slot(me) if self.src is None else self.src
        x, y = _peer(ACROSS_X), _peer(ACROSS_Y)
        return [self._copy(1, me, 0, x, src), self._copy(3, me, 1, y, src), self._copy(0, me, None, _peer(SIBLING), src),
                self._copy(2, me, 1, x, src), self._copy(4, me, 0, y, src)]

    def _relayed(self):
        return [self._copy(5, _peer(ACROSS_X), 0, _peer(ACROSS_Y)), self._copy(6, _peer(ACROSS_Y), 1, _peer(ACROSS_X))]

    def _passed(self):
        sib, far = _peer(SIBLING), _peer(ACROSS_BOTH)
        return [self._copy(7, _peer(ACROSS_X), None, sib), self._copy(8, _peer(ACROSS_Y), None, sib),
                self._copy(9, far, 0, sib), self._copy(10, far, 1, sib)]

    def _arrival(self, k, r, part):
        return self._copy(k, _peer(r), part, _me())

    def send_first(self):
        for cp in self._mine()[:3]:
            cp.start()

    def send_second(self):
        for cp in self._mine()[3:]:
            cp.start()

    def send_mine(self):
        self.send_first()
        self.send_second()

    def relay(self):
        relayed = self._relayed()
        self._arrival(1, ACROSS_X, 0).wait_recv()
        relayed[0].start()
        self._arrival(3, ACROSS_Y, 1).wait_recv()
        relayed[1].start()

    def pass_near(self):
        passed = self._passed()
        self._arrival(2, ACROSS_X, 1).wait_recv()
        passed[0].start()
        self._arrival(4, ACROSS_Y, 0).wait_recv()
        passed[1].start()

    def pass_far(self):
        passed = self._passed()
        self._arrival(5, ACROSS_BOTH, 0).wait_recv()
        passed[2].start()
        self._arrival(6, ACROSS_BOTH, 1).wait_recv()
        passed[3].start()

    def pass_on(self):
        self.pass_near()
        self.pass_far()

    def wait_sibling(self):
        self._arrival(0, SIBLING, None).wait_recv()

    def wait_passed(self, r):
        if r == ACROSS_BOTH:
            self._arrival(9, r ^ SIBLING, 0).wait_recv()
            self._arrival(10, r ^ SIBLING, 1).wait_recv()
        else:
            self._arrival(7 if r == ACROSS_X else 8, r ^ SIBLING, None).wait_recv()

    def wait_rest(self):
        self.wait_sibling()
        for r in (ACROSS_X, ACROSS_Y, ACROSS_BOTH):
            self.wait_passed(r)

    def wait_sends(self):
        for cp in self._mine() + self._relayed() + self._passed():
            cp.wait_send()


class _ChipReduceScatter:
    SLOTS = 6

    def __init__(self, arrays, l_sem, d_send, d_recv, i_send, i_recv):
        self.arrays = arrays
        self.l_sem, self.d_send, self.d_recv, self.i_send, self.i_recv = l_sem, d_send, d_recv, i_send, i_recv

    @staticmethod
    def buffers(rows, cols, dtype, staged=True):
        stage = [pltpu.VMEM((4, rows, cols), dtype)] if staged else []
        return stage + [pltpu.VMEM((4, rows, cols), dtype), pltpu.VMEM((3, rows, cols), dtype),
                        pltpu.VMEM((2, rows, cols), dtype), pltpu.VMEM((2, rows // 2, cols), dtype)]

    @classmethod
    def semaphores(cls, n):
        return [pltpu.SemaphoreType.DMA((n,)), pltpu.SemaphoreType.DMA((n, 4)), pltpu.SemaphoreType.DMA((n, 4)),
                pltpu.SemaphoreType.DMA((n, cls.SLOTS)), pltpu.SemaphoreType.DMA((n, cls.SLOTS))]

    def _pick(self, which):
        return list(enumerate(self.arrays)) if which is None else [(n, self.arrays[n]) for n in which]

    @staticmethod
    def _chip(r):
        dev = _me() if r is None else _peer(r)
        return 2 * dev[0] + dev[1]

    def _staging(self, which):
        c = _me()[2]
        return [pltpu.make_async_copy(a["part"].at[pl.ds(0, 4), c], a["stage"], self.l_sem.at[n])
                for n, a in self._pick(which) if a["staged"]]

    def _first(self, which, chip):
        other = 1 - _me()[2]
        return [_remote(a["part"].at[chip, other], a["sib"].at[chip], self.d_send.at[n, chip], self.d_recv.at[n, chip],
                        _peer(SIBLING)) for n, a in self._pick(which)]

    @staticmethod
    def _halves(a):
        half = a["rcv"].shape[1] // 2
        return pl.ds(0, half), pl.ds(half, half)

    def _hops(self, n, a):
        h0, h1 = self._halves(a)
        x, y = _peer(ACROSS_X), _peer(ACROSS_Y)
        snd, rcv, relay = a["snd"], a["rcv"], a["relay"]
        pairs = [(snd.at[2, h0], relay.at[0], x), (snd.at[2, h1], relay.at[1], y),
                 (snd.at[0, h0], rcv.at[0, h0], x), (snd.at[0, h1], rcv.at[0, h1], x),
                 (snd.at[1, h1], rcv.at[1, h1], y), (snd.at[1, h0], rcv.at[1, h0], y)]
        return [_remote(s, d, self.i_send.at[n, k], self.i_recv.at[n, k], to) for k, (s, d, to) in enumerate(pairs)]

    def _mine(self, a, chip, rows=None):
        src = a["stage"].at[chip] if a["staged"] else a["part"].at[chip, _me()[2]]
        mine, sib = (src[...], a["sib"][chip]) if rows is None else (src[rows, :], a["sib"][chip, rows, :])
        return mine.astype(F32) + sib.astype(F32)

    def start(self, which=None, chips=None):
        if chips is None:
            for cp in self._staging(which):
                cp.start()
        for chip in range(4) if chips is None else chips:
            for cp in self._first(which, chip):
                cp.start()

    def send_far(self, which=None):
        far = self._chip(ACROSS_BOTH)
        for cp in self._staging(which):
            cp.wait()
        for cp in self._first(which, far):
            cp.wait_recv()
        for n, a in self._pick(which):
            hops = self._hops(n, a)
            a["snd"][2] = self._mine(a, far).astype(a["snd"].dtype)
            hops[0].start()
            hops[1].start()

    def send_near(self, r, which=None):
        chip = self._chip(r)
        for cp in self._first(which, chip):
            cp.wait_recv()
        for n, a in self._pick(which):
            h0, h1 = self._halves(a)
            hops = self._hops(n, a)
            if r == ACROSS_X:
                a["snd"][0, h0, :] = self._mine(a, chip, h0).astype(a["snd"].dtype)
                hops[2].start()
            else:
                a["snd"][1, h1, :] = self._mine(a, chip, h1).astype(a["snd"].dtype)
                hops[4].start()

    def exchange(self, which=None):
        self.send_far(which)
        self.send_near(ACROSS_X, which)
        self.send_near(ACROSS_Y, which)

    def fold(self, which=None):
        across_x, across_y = self._chip(ACROSS_X), self._chip(ACROSS_Y)
        for n, a in self._pick(which):
            h0, h1 = self._halves(a)
            hops = self._hops(n, a)
            dtype = a["snd"].dtype
            hops[1].wait_recv()
            a["snd"][0, h1, :] = (self._mine(a, across_x, h1) + a["relay"][1].astype(F32)).astype(dtype)
            hops[3].start()
            hops[0].wait_recv()
            a["snd"][1, h0, :] = (self._mine(a, across_y, h0) + a["relay"][0].astype(F32)).astype(dtype)
            hops[5].start()

    def finish(self, which=None):
        home = self._chip(None)
        for cp in self._first(which, home):
            cp.wait_recv()
        for n, a in self._pick(which):
            hops = self._hops(n, a)
            a["out"][...] = self._mine(a, home)
            hops[2].wait_recv()
            hops[3].wait_recv()
            a["out"][...] += a["rcv"][0].astype(F32)
            hops[4].wait_recv()
            hops[5].wait_recv()
            a["out"][...] += a["rcv"][1].astype(F32)

    def wait_sends(self, which=None):
        for chip in range(4):
            for cp in self._first(which, chip):
                cp.wait_send()
        for n, a in self._pick(which):
            for cp in self._hops(n, a):
                cp.wait_send()


def _direct_exchange(src_of, dst_of, send_sems, recv_sems):
    me = _me()
    copies = [_remote(src_of(_peer(r)), dst_of(me), send_sems.at[r - 1], recv_sems.at[r - 1], _peer(r))
              for r in range(1, N_DEV)]
    for cp in copies:
        cp.start()
    return copies


def _wait_direct(copies):
    for cp in copies:
        cp.wait_recv()
    for cp in copies:
        cp.wait_send()


def _forward(x, c, w_ada, b_ada, small, ln_g, ln_b, mine, following):
    assert DEPTH == 2
    cols = w_ada.shape[2]
    shard = mine[0].shape[0]
    pair = 2 * shard

    def body(x_hbm, c_ref, wada_hbm, bada_ref, wpool_ref, pscale_ref, slng_ref, slnb_ref, wsgu_ref, bsgut_ref,
             lng_ref, lnb_ref, wint_hbm, wout_hbm, next_in_hbm, next_out_hbm,
             out0_ref, y0_ref, cdf0_ref, proj0_hbm, out1_ref, y1_ref, cdf1_ref, proj1_hbm,
             wint_keep, wout_keep, wint_next, wout_next, acts_ref, mod_ref,
             wint_v, wout_v, h_buf, proj_blk, proj_tile, halo_ref, x_ref, w_send, w_recv, w_local, p_sems,
             t_sems, in_sems, act_all, act_src, part, mod_recv, wada_ref, a_send, a_recv, m_send, m_recv,
             n_send, n_recv, n_local, f_sems):
        step = pl.program_id(0)
        chip_of = lambda dev: 2 * dev[0] + dev[1]

        def hosted():
            return [_TwoLevelGather(out, n_send.at[n], n_recv.at[n], src=src)
                    for n, (out, src) in enumerate(((wint_next, next_in_hbm), (wout_next, next_out_hbm)))]

        def hosted_own():
            me = _me()
            return [pltpu.make_async_copy(g.src, g._slot(me), n_local.at[n]) for n, g in enumerate(hosted())]

        def gathers():
            return (_TwoLevelGather(wint_v, w_send.at[0], w_recv.at[0], src=wint_hbm),
                    _TwoLevelGather(wout_v, w_send.at[1], w_recv.at[1], src=wout_hbm))

        def keeps():
            return [pltpu.make_async_copy(wint_v, wint_keep, w_local.at[2]),
                    pltpu.make_async_copy(wout_v, wout_keep, w_local.at[3])]

        def tile_read(t):
            slot = t % 2
            return pltpu.make_async_copy(proj0_hbm.at[pl.ds(pl.multiple_of(t * TM, TM), TM)], proj_tile.at[slot],
                                         t_sems.at[slot])

        def first_layer_start():
            writes = []

            def project(n, dev):
                first = pl.multiple_of(chip_of(dev) * pair, pair)
                if n >= 2:
                    writes[n - 2].wait()

                @pl.loop(0, N_TILES)
                def _(t):
                    rows = pl.ds(pl.multiple_of(t * TM, TM), TM)
                    proj_blk[n % 2, rows, :] = _dot_nt(h_buf[rows, :], wint_v[pl.ds(first, pair), :])

                cp = pltpu.make_async_copy(proj_blk.at[n % 2], proj0_hbm.at[:, pl.ds(first, pair)], p_sems.at[n % 2])
                cp.start()
                writes.append(cp)

            me = _me()
            halo_ref[...] = jnp.zeros_like(halo_ref)
            gather_in, gather_out = gathers()
            own_in = pltpu.make_async_copy(wint_hbm, gather_in._slot(me), w_local.at[0])
            own_out = pltpu.make_async_copy(wout_hbm, gather_out._slot(me), w_local.at[1])
            x_load = pltpu.make_async_copy(x_hbm, x_ref, in_sems.at[0])
            x_load.start()
            wada_load = pltpu.make_async_copy(wada_hbm, wada_ref, in_sems.at[1])
            wada_load.start()
            mine_index = _index(me)
            cval = c_ref[...]
            act_src[...] = jnp.zeros_like(act_src)
            act_src[0:1, :] = cval * jax.nn.sigmoid(cval)
            act_all[mine_index] = act_src[...]
            act_copies = _direct_exchange(lambda p: act_src, lambda m: act_all.at[_index(m)], a_send, a_recv)

            own_in.start()
            own_out.start()
            gather_in.send_first()

            _wait_direct(act_copies)
            acts = jnp.concatenate([act_all[j, 0:1, :] for j in range(N_DEV)], axis=0)
            acts_ref[...] = acts
            part[...] = jnp.zeros_like(part)
            wada_load.wait()
            for l in range(DEPTH):
                res = lax.dot_general(acts, wada_ref[l], (((1,), (0,)), ((), ())), preferred_element_type=F32,
                                      precision=lax.Precision.HIGHEST)
                for b in range(N_DEV):
                    part[b, l:l + 1, :] = res[b:b + 1, :]
            mod_recv[mine_index] = part[mine_index]
            mod_copies = _direct_exchange(lambda p: part.at[_index(p)], lambda m: mod_recv.at[_index(m)],
                                          m_send, m_recv)
            gather_in.send_second()
            gather_out.send_mine()

            _wait_direct(mod_copies)
            for l in range(DEPTH):
                for j in range(N_DEV):
                    sl = slice(j * cols, (j + 1) * cols)
                    mod_ref[l:l + 1, sl] = mod_recv[j, l:l + 1, :] + bada_ref[l:l + 1, sl]
            x_load.wait()
            shift = mod_ref[0:1, 0:D_MODEL]
            scale = mod_ref[0:1, D_MODEL:2 * D_MODEL]

            @pl.loop(0, N_TILES)
            def _(t):
                rows = pl.ds(pl.multiple_of(t * TM, TM), TM)
                xn, _ = _ln(x_ref[rows, :])
                h_buf[rows, :] = (xn * (1.0 + scale) + shift).astype(BF16)

            gather_in.relay()
            own_in.wait()
            gather_in.wait_sibling()
            project(0, me)
            gather_in.pass_near()
            gather_in.wait_passed(ACROSS_X)
            project(1, _peer(ACROSS_X))
            gather_out.relay()
            for cp in hosted_own():
                cp.start()
            for g in hosted():
                g.send_mine()
            gather_in.wait_passed(ACROSS_Y)
            project(2, _peer(ACROSS_Y))
            gather_in.pass_far()
            gather_in.wait_passed(ACROSS_BOTH)
            project(3, _peer(ACROSS_BOTH))

            gather_out.pass_on()
            gather_out.wait_rest()
            own_out.wait()
            for cp in keeps():
                cp.start()
            writes[2].wait()
            writes[3].wait()
            tile_read(0).start()

        def fetches():
            return [pltpu.make_async_copy(wint_next, wint_v, f_sems.at[0]),
                    pltpu.make_async_copy(wout_next, wout_v, f_sems.at[1])]

        def tile_write(t):
            slot = t % 2
            return pltpu.make_async_copy(proj_tile.at[slot], proj1_hbm.at[pl.ds(pl.multiple_of(t * TM, TM), TM)],
                                         t_sems.at[slot])

        def mix_and_close(layer, tile, rows, proj, out_ref, y_ref, cdf_ref):
            weights = _MixWeights(layer, wpool_ref, pscale_ref, slng_ref, slnb_ref, wsgu_ref, bsgut_ref)
            xt = x_ref[rows, :]
            gate = mod_ref[layer:layer + 1, 2 * D_MODEL:]
            cat, cdf_ref[...] = _mix_forward(proj, halo_ref[...], tile, weights)
            halo_ref[...] = proj[TM - HALO:, 0:D_POOL]
            if layer == 1:
                @pl.when(tile == 0)
                def _():
                    fetches()[1].wait()
            y = _dot(cat.astype(BF16), wout_v[...])
            y_ref[...] = y
            zn, _ = _ln(ALPHA * xt + gate * y)
            out = zn * lng_ref[layer:layer + 1, :] + lnb_ref[layer:layer + 1, :]
            out_ref[...] = out
            if layer == 0:
                x_ref[rows, :] = out

        def first_layer_tile(tile):
            @pl.when(tile + 1 < N_TILES)
            def _():
                tile_read(tile + 1).start()

            tile_read(tile).wait()
            rows = pl.ds(pl.multiple_of(tile * TM, TM), TM)
            mix_and_close(0, tile, rows, proj_tile[tile % 2], out0_ref, y0_ref, cdf0_ref)

        def second_layer_tile(tile):
            rows = pl.ds(pl.multiple_of(tile * TM, TM), TM)
            shift = mod_ref[1:2, 0:D_MODEL]
            scale = mod_ref[1:2, D_MODEL:2 * D_MODEL]
            xn, _ = _ln(x_ref[rows, :])
            h = (xn * (1.0 + scale) + shift).astype(BF16)

            @pl.when(tile >= 2)
            def _():
                tile_write(tile - 2).wait()

            proj_tile[tile % 2] = _dot_nt(h, wint_v[...])
            tile_write(tile).start()
            mix_and_close(1, tile, rows, proj_tile[tile % 2], out1_ref, y1_ref, cdf1_ref)

        @pl.when(step < N_TILES)
        def _():
            @pl.when(step == 0)
            def _():
                first_layer_start()

            @pl.when(step == 1)
            def _():
                for g in hosted():
                    g.relay()

            @pl.when(step == N_TILES // 2)
            def _():
                for g in hosted():
                    g.pass_near()

            @pl.when(step == N_TILES - 1)
            def _():
                for g in hosted():
                    g.pass_far()
                for g in gathers():
                    g.wait_sends()
                for cp in keeps() + hosted_own():
                    cp.wait()
                hosted()[0].wait_rest()
                fetches()[0].start()

            first_layer_tile(step)

        @pl.when(step >= N_TILES)
        def _():
            @pl.when(step == N_TILES)
            def _():
                halo_ref[...] = jnp.zeros_like(halo_ref)
                hosted()[1].wait_rest()
                fetches()[1].start()
                fetches()[0].wait()

            second_layer_tile(step - N_TILES)

            @pl.when(step == 2 * N_TILES - 1)
            def _():
                for g in hosted():
                    g.wait_sends()
                tile_write(N_TILES - 2).wait()
                tile_write(N_TILES - 1).wait()

    first = lambda w: pl.BlockSpec((TM, w), lambda i: (jnp.minimum(i, N_TILES - 1), 0))
    second = lambda w: pl.BlockSpec((TM, w), lambda i: (jnp.maximum(i - N_TILES, 0), 0))
    gather_sems = pltpu.SemaphoreType.DMA((2, GATHER_SEMS))
    seven = pltpu.SemaphoreType.DMA((7,))
    per_layer = [jax.ShapeDtypeStruct((SEQ, D_MODEL), F32), jax.ShapeDtypeStruct((SEQ, D_MODEL), F32),
                 jax.ShapeDtypeStruct((SEQ, 2 * D_SGU), F32), jax.ShapeDtypeStruct((SEQ, D_IN), F32)]
    gathered = [jax.ShapeDtypeStruct((D_IN, D_MODEL), BF16), jax.ShapeDtypeStruct((D_MODEL, D_MODEL), BF16)]
    res = pl.pallas_call(
        body,
        name="layers_fwd",
        grid=(DEPTH * N_TILES,),
        in_specs=[ANY, _const_spec(c.shape), ANY, _const_spec(b_ada.shape)] + [_const_spec(s) for s in SMALL_SPECS]
                 + [_const_spec((DEPTH, D_MODEL)), _const_spec((DEPTH, D_MODEL))] + [ANY] * 4,
        out_specs=[first(D_MODEL), first(D_MODEL), first(2 * D_SGU), ANY,
                   second(D_MODEL), second(D_MODEL), second(2 * D_SGU), ANY] + [ANY] * 4
                  + [_const_spec((N_DEV, D_MODEL)), _const_spec((DEPTH, 3 * D_MODEL))],
        out_shape=per_layer * 2 + gathered * 2 + [jax.ShapeDtypeStruct((N_DEV, D_MODEL), F32),
                                                  jax.ShapeDtypeStruct((DEPTH, 3 * D_MODEL), F32)],
        scratch_shapes=[pltpu.VMEM((D_IN, D_MODEL), BF16), pltpu.VMEM((D_MODEL, D_MODEL), BF16),
                        pltpu.VMEM((SEQ, D_MODEL), BF16), pltpu.VMEM((2, SEQ, pair), F32),
                        pltpu.VMEM((2, TM, D_IN), F32), pltpu.VMEM((HALO, D_POOL), F32),
                        pltpu.VMEM((SEQ, D_MODEL), F32),
                        gather_sems, gather_sems, pltpu.SemaphoreType.DMA((4,)), pltpu.SemaphoreType.DMA((2,)),
                        pltpu.SemaphoreType.DMA((2,)), pltpu.SemaphoreType.DMA((2,)),
                        pltpu.VMEM((N_DEV, 8, D_MODEL), F32), pltpu.VMEM((8, D_MODEL), F32),
                        pltpu.VMEM((N_DEV, 8, cols), F32), pltpu.VMEM((N_DEV, 8, cols), F32),
                        pltpu.VMEM(w_ada.shape, F32), seven, seven, seven, seven,
                        gather_sems, gather_sems, pltpu.SemaphoreType.DMA((2,)), pltpu.SemaphoreType.DMA((2,))],
        compiler_params=pltpu.CompilerParams(dimension_semantics=("arbitrary",), vmem_limit_bytes=VMEM_LIMIT),
    )(x, c, w_ada, b_ada, *small, ln_g, ln_b, *mine, *following)
    return res[0:4], res[4:8], res[8:10], res[10:12], res[12], res[13]


ADA_CHUNK = 256


def _grad_tail(dproj, h, cat, dy, small, dmod8, loss_lanes, w_ada, m_ada, v_ada, act_t, b_ada, m_bada, v_bada):
    shard_in, shard_out, shard_small = D_IN // N_DEV, D_MODEL // N_DEV, small.shape[2]
    cols = w_ada.shape[2]
    W_IN, W_OUT, SMALL = 0, 1, 2

    def body(dproj_hbm, h_hbm, cat_hbm, dy_hbm, small_hbm, dmod_ref, lanes_ref, wada_hbm, mada_hbm, vada_hbm,
             act_ref, bada_ref, mbada_ref, vbada_ref,
             gwin_ref, gwout_ref, stot_ref, loss_ref, gada_hbm, dada_hbm, nmada_hbm, nvada_hbm,
             gb_ref, db_ref, nmb_ref, nvb_ref,
             dproj_v, h_v, cat_v, dy_v, part_in, part_out, own_small, loss_src, loss_all, dmod_all, ada_in, ada_out,
             *rest):
        bufs, rest = rest[:13], rest[13:]
        load_sems, rs_sems = rest[0], rest[1:6]
        m_send, m_recv, g_send, g_recv, s_send, s_recv, ada_lsem, ada_ssem = rest[6:]
        mine = _index(_me())

        def update_ada():
            upper = (mine % 2) == 1

            def dmod_of(layer):
                rows = []
                for b in range(N_DEV):
                    r = dmod_all[b, pl.ds(4 * layer + mine // 2, 1), :]
                    rows.append(jnp.where(upper, r[:, cols:], r[:, :cols]))
                return jnp.concatenate(rows, axis=0)

            chunks = [(layer, c) for layer in range(DEPTH) for c in range(D_MODEL // ADA_CHUNK)]

            def loads(i):
                layer, c = chunks[i]
                rows = pl.ds(c * ADA_CHUNK, ADA_CHUNK)
                return [pltpu.make_async_copy(src.at[layer, rows], ada_in.at[i % 2, k], ada_lsem.at[i % 2, k])
                        for k, src in enumerate((wada_hbm, mada_hbm, vada_hbm))]

            def stores(i):
                layer, c = chunks[i]
                rows = pl.ds(c * ADA_CHUNK, ADA_CHUNK)
                return [pltpu.make_async_copy(ada_out.at[i % 2, k], dst.at[layer, rows], ada_ssem.at[i % 2, k])
                        for k, dst in enumerate((gada_hbm, dada_hbm, nmada_hbm, nvada_hbm))]

            for cp in loads(0):
                cp.start()
            dmods = {}
            for i, (layer, c) in enumerate(chunks):
                if i + 1 < len(chunks):
                    for cp in loads(i + 1):
                        cp.start()
                for cp in loads(i):
                    cp.wait()
                if i >= 2:
                    for cp in stores(i - 2):
                        cp.wait()
                if layer not in dmods:
                    dmods[layer] = dmod_of(layer)
                act = act_ref[pl.ds(c * ADA_CHUNK, ADA_CHUNK), :]
                g = act[:, 0:1] * dmods[layer][0:1, :]
                for b in range(1, N_DEV):
                    g = g + act[:, b:b + 1] * dmods[layer][b:b + 1, :]
                slot = i % 2
                delta, new_m, new_v = _adamw_math(ada_in[slot, 0], g, ada_in[slot, 1], ada_in[slot, 2])
                ada_out[slot, 0] = g
                ada_out[slot, 1] = delta
                ada_out[slot, 2] = new_m
                ada_out[slot, 3] = new_v
                for cp in stores(i):
                    cp.start()
            for i in (len(chunks) - 2, len(chunks) - 1):
                for cp in stores(i):
                    cp.wait()

            total = dmod_all[0]
            for b in range(1, N_DEV):
                total = total + dmod_all[b]
            width = total.shape[1]
            for layer in range(DEPTH):
                for q in range(4):
                    gb_ref[layer:layer + 1, q * width:(q + 1) * width] = total[4 * layer + q:4 * layer + q + 1, :]
            db_ref[...], nmb_ref[...], nvb_ref[...] = _adamw_math(bada_ref[...], gb_ref[...], mbada_ref[...],
                                                                  vbada_ref[...])

        order = (ACROSS_BOTH, ACROSS_X, ACROSS_Y, None)
        chips = [_ChipReduceScatter._chip(r) for r in order]
        loads = [pltpu.make_async_copy(s, d, load_sems.at[n]) for n, (s, d) in enumerate(
            ((cat_hbm, cat_v), (dy_hbm, dy_v), (h_hbm, h_v)))]
        loads += [pltpu.make_async_copy(dproj_hbm.at[:, pl.ds(pl.multiple_of(chip * 2 * shard_in, 2 * shard_in),
                                                             2 * shard_in)], dproj_v.at[n], load_sems.at[3 + n])
                  for n, chip in enumerate(chips)]
        for cp in loads[2:] + loads[:2]:
            cp.start()
        arrays = [dict(part=part_in, out=gwin_ref, staged=False, sib=bufs[0], snd=bufs[1], rcv=bufs[2], relay=bufs[3]),
                  dict(part=part_out, out=gwout_ref, staged=False, sib=bufs[4], snd=bufs[5], rcv=bufs[6],
                       relay=bufs[7]),
                  dict(part=small_hbm, out=own_small, staged=True, stage=bufs[8], sib=bufs[9], snd=bufs[10],
                       rcv=bufs[11], relay=bufs[12])]
        scatter = _ChipReduceScatter(arrays, *rs_sems)
        scatter.start([SMALL])
        dmod_all[mine] = dmod_ref[...]
        dmod_copies = _direct_exchange(lambda p: dmod_ref, lambda m: dmod_all.at[_index(m)], m_send, m_recv)
        loss_src[...] = jnp.full(loss_src.shape, (0.5 / D_MODEL) * jnp.sum(lanes_ref[...]), F32)
        loss_all[mine] = loss_src[...]
        loss_copies = _direct_exchange(lambda p: loss_src, lambda m: loss_all.at[_index(m)], s_send, s_recv)

        gather = _TwoLevelGather(stot_ref, g_send, g_recv)
        loads[2].wait()
        for n, chip in enumerate(chips):
            loads[3 + n].wait()
            res = _dot_tn(dproj_v[n], h_v[...]).astype(BF16)
            part_in[chip, 0] = res[:shard_in]
            part_in[chip, 1] = res[shard_in:]
            scatter.start([W_IN], chips=[chip])
            if n == 0:
                scatter.exchange([SMALL])
            if n == 1:
                scatter.send_far([W_IN])
                scatter.fold([SMALL])
            if n == 2:
                scatter.send_near(ACROSS_X, [W_IN])
                scatter.finish([SMALL])
                stot_ref[mine] = own_small[...]
                gather.send_mine()
            if n == 3:
                scatter.send_near(ACROSS_Y, [W_IN])

        loads[0].wait()
        loads[1].wait()
        for blk in range(2):
            res = _dot_tn(cat_v[:, blk * 512:(blk + 1) * 512], dy_v[...]).astype(BF16)
            for s in range(4):
                part_out[2 * blk + s // 2, s % 2] = res[s * shard_out:(s + 1) * shard_out]
        scatter.start([W_OUT])
        scatter.fold([W_IN])
        scatter.exchange([W_OUT])
        gather.relay()
        _wait_direct(dmod_copies)
        update_ada()
        scatter.fold([W_OUT])
        gather.pass_on()
        gather.wait_rest()
        _wait_direct(loss_copies)
        total = loss_all[0]
        for j in range(1, N_DEV):
            total = total + loss_all[j]
        loss_ref[...] = total
        scatter.finish([W_IN])
        scatter.finish([W_OUT])
        gather.wait_sends()
        scatter.wait_sends()

    buffers = _ChipReduceScatter.buffers
    comm_scratch = (buffers(shard_in, D_MODEL, BF16, staged=False) + buffers(shard_out, D_MODEL, BF16, staged=False)
                    + buffers(shard_small, 128, F32))
    comm_scratch += [pltpu.SemaphoreType.DMA((7,))] + _ChipReduceScatter.semaphores(3)
    comm_scratch += [pltpu.SemaphoreType.DMA((n,)) for n in (7, 7, GATHER_SEMS, GATHER_SEMS, 7, 7)]
    comm_scratch += [pltpu.SemaphoreType.DMA((2, 3)), pltpu.SemaphoreType.DMA((2, 4))]
    return pl.pallas_call(
        body,
        name="grad_tail",
        in_specs=[ANY] * 5 + [VMEM, VMEM] + [ANY] * 3 + [VMEM] * 4,
        out_specs=[VMEM] * 4 + [ANY] * 4 + [VMEM] * 4,
        out_shape=[jax.ShapeDtypeStruct((shard_in, D_MODEL), F32), jax.ShapeDtypeStruct((shard_out, D_MODEL), F32),
                   jax.ShapeDtypeStruct((N_DEV, shard_small, 128), F32), jax.ShapeDtypeStruct((8, 128), F32)]
                  + [jax.ShapeDtypeStruct(w_ada.shape, F32)] * 4 + [jax.ShapeDtypeStruct(b_ada.shape, F32)] * 4,
        scratch_shapes=[pltpu.VMEM((4, SEQ, 2 * shard_in), BF16), pltpu.VMEM(h.shape, BF16), pltpu.VMEM(cat.shape, BF16),
                        pltpu.VMEM(dy.shape, BF16), pltpu.VMEM((4, 2, shard_in, D_MODEL), BF16),
                        pltpu.VMEM((4, 2, shard_out, D_MODEL), BF16), pltpu.VMEM((shard_small, 128), F32),
                        pltpu.VMEM((8, 128), F32), pltpu.VMEM((N_DEV, 8, 128), F32),
                        pltpu.VMEM((N_DEV,) + dmod8.shape, F32), pltpu.VMEM((2, 3, ADA_CHUNK, cols), F32),
                        pltpu.VMEM((2, 4, ADA_CHUNK, cols), F32)] + comm_scratch,
        compiler_params=pltpu.CompilerParams(vmem_limit_bytes=VMEM_LIMIT),
    )(dproj, h, cat, dy, small, dmod8, loss_lanes, w_ada, m_ada, v_ada, act_t, b_ada, m_bada, v_bada)


SMALL_NAMES = ("w_pool", "w_sgu", "pool_scale", "sgu_ln_g", "sgu_ln_b", "b_sgu", "ln_g", "ln_b")
SMALL_ROWS = (512, 512, 4, 4, 4, 4, 8, 8)


def _adamw_small(g_packed, ws, ms, vs, dx, name):
    n = len(SMALL_NAMES)

    def body(g_ref, *refs):
        w_refs, m_refs, v_refs = refs[:n], refs[n:2 * n], refs[2 * n:3 * n]
        dx_hbm, outs, dx_out, copy_sem = refs[3 * n], refs[3 * n + 1:7 * n + 1], refs[7 * n + 1], refs[7 * n + 2]
        dx_copy = pltpu.make_async_copy(dx_hbm, dx_out, copy_sem)
        dx_copy.start()

        def update(p, at, g):
            delta, new_m, new_v = _adamw_math(w_refs[p][at], g, m_refs[p][at], v_refs[p][at])
            outs[p][at] = g
            outs[n + p][at] = delta
            outs[2 * n + p][at] = new_m
            outs[3 * n + p][at] = new_v

        row = 0
        for p, r in enumerate(SMALL_ROWS):
            shape = ws[p].shape
            for layer in range(DEPTH):
                first = layer * PACK_ROWS + row
                if len(shape) == 4:
                    for k in range(shape[1]):
                        update(p, (layer, k), g_ref[first + k * shape[2]:first + (k + 1) * shape[2], :])
                elif len(shape) == 3:
                    update(p, (layer,), g_ref[first:first + r, :])
                else:
                    g = jnp.concatenate([g_ref[first + k:first + k + 1, :] for k in range(r)], axis=1)
                    update(p, (slice(layer, layer + 1), slice(None)), g)
            row += r
        dx_copy.wait()

    res = pl.pallas_call(
        body,
        name=name,
        in_specs=[VMEM] * (1 + 3 * n) + [ANY],
        out_specs=[VMEM] * (4 * n) + [ANY],
        out_shape=[jax.ShapeDtypeStruct(w.shape, F32) for w in ws] * 4 + [jax.ShapeDtypeStruct(dx.shape, dx.dtype)],
        scratch_shapes=[pltpu.SemaphoreType.DMA(())],
        compiler_params=pltpu.CompilerParams(vmem_limit_bytes=VMEM_LIMIT),
    )(g_packed, *ws, *ms, *vs, dx)
    return res[:n], res[n:2 * n], res[2 * n:3 * n], res[3 * n:4 * n], res[4 * n]


def kernel(x, c, w_ada, b_ada, w_in, w_pool, pool_scale, sgu_ln_g, sgu_ln_b, w_sgu, b_sgu, w_out, ln_g, ln_b, loss_target, m_w_ada, m_b_ada, m_w_in, m_w_pool, m_pool_scale, m_sgu_ln_g, m_sgu_ln_b, m_w_sgu, m_b_sgu, m_w_out, m_ln_g, m_ln_b, v_w_ada, v_b_ada, v_w_in, v_w_pool, v_pool_scale, v_sgu_ln_g, v_sgu_ln_b, v_w_sgu, v_b_sgu, v_w_out, v_ln_g, v_ln_b):
    small_w = dict(w_pool=w_pool, w_sgu=w_sgu, pool_scale=pool_scale, sgu_ln_g=sgu_ln_g, sgu_ln_b=sgu_ln_b,
                   b_sgu=b_sgu, ln_g=ln_g, ln_b=ln_b)
    small_m = dict(w_pool=m_w_pool, w_sgu=m_w_sgu, pool_scale=m_pool_scale, sgu_ln_g=m_sgu_ln_g,
                   sgu_ln_b=m_sgu_ln_b, b_sgu=m_b_sgu, ln_g=m_ln_g, ln_b=m_ln_b)
    small_v = dict(w_pool=v_w_pool, w_sgu=v_w_sgu, pool_scale=v_pool_scale, sgu_ln_g=v_sgu_ln_g,
                   sgu_ln_b=v_sgu_ln_b, b_sgu=v_b_sgu, ln_g=v_ln_g, ln_b=v_ln_b)

    wint_loc = jnp.transpose(w_in, (0, 2, 1)).astype(BF16)
    wout_loc = w_out.astype(BF16)
    small = (w_pool, pool_scale, sgu_ln_g, sgu_ln_b, w_sgu, jnp.transpose(b_sgu, (0, 2, 1)))
    (out0, y0, cdf0, proj0), (cur, y1, cdf1, proj1), gathered0, gathered1, act_all, mod = _forward(
        x[0], c, w_ada, b_ada, small, ln_g, ln_b, [wint_loc[0], wout_loc[0]], [wint_loc[1], wout_loc[1]])
    w_int, w_outf = [gathered0[0], gathered1[0]], [gathered0[1], gathered1[1]]
    acts = [(x[0], proj0, y0, cdf0), (out0, proj1, y1, cdf1)]

    shard_in, shard_out = D_IN // N_DEV, D_MODEL // N_DEV
    a, b = cur, loss_target[0]
    loss_lanes, carry, pending = None, (), []
    g_w_in_t, g_w_out = [None] * DEPTH, [None] * DEPTH
    for l in reversed(range(DEPTH)):
        dx, dproj, h, cat, dy, small_grads, dmod8, lanes, *shards = _layer_backward(
            l, a, b, *acts[l], mod, w_int[l], w_outf[l], small, ln_g, l == DEPTH - 1, f"layer_bwd_{l}",
            carry=carry, reduce=pending)
        if shards:
            g_w_in_t[l + 1], g_w_out[l + 1] = shards
        if l == DEPTH - 1:
            loss_lanes = lanes
        if l > 0:
            g_in, g_out = _grad_matmuls(dproj, h, cat, dy, f"grad_w_{l}")
            pending = [g_in.reshape(4, 2, shard_in, D_MODEL), g_out.reshape(4, 2, shard_out, D_MODEL)]
        carry = (small_grads, dmod8)
        a = b = dx

    (g_w_in_t[0], g_w_out[0], small_tot, loss_tile, g_w_ada, d_w_ada, nm_w_ada, nv_w_ada,
     g_b_ada, d_b_ada, nm_b_ada, nv_b_ada) = _grad_tail(
        dproj, h, cat, dy, small_grads.reshape(4, 2, DEPTH * PACK_ROWS // N_DEV, 128), dmod8, loss_lanes,
        w_ada, m_w_ada, v_w_ada, jnp.transpose(act_all), b_ada, m_b_ada, v_b_ada)
    loss = loss_tile[0, 0]

    flat = lambda t: t.reshape(-1, t.shape[-1])
    to_t = lambda t: flat(jnp.transpose(t, (0, 2, 1)))
    from_t = lambda t: jnp.transpose(t.reshape(DEPTH, shard_in, D_MODEL), (0, 2, 1))
    g_w_in, d_w_in, nm_w_in, nv_w_in = [from_t(t) for t in _adamw(to_t(w_in), g_w_in_t, to_t(m_w_in), to_t(v_w_in),
                                                                  shard_in // 2, "adamw_w_in")]
    gwout, d_w_out, nm_w_out, nv_w_out = [t.reshape(w_out.shape) for t in _adamw(
        flat(w_out), g_w_out, flat(m_w_out), flat(v_w_out), shard_out, "adamw_w_out")]
    *small_out, grad_x = _adamw_small(small_tot.reshape(DEPTH * PACK_ROWS, 128), [small_w[n] for n in SMALL_NAMES],
                                      [small_m[n] for n in SMALL_NAMES], [small_v[n] for n in SMALL_NAMES], a,
                                      "adamw_small")
    grad_x = grad_x[None]
    gs, ds, ms, vs = [dict(zip(SMALL_NAMES, group)) for group in small_out]

    def ordered(w_ada_, b_ada_, w_in_, small, w_out_):
        return (w_ada_, b_ada_, w_in_, small["w_pool"], small["pool_scale"], small["sgu_ln_g"], small["sgu_ln_b"],
                small["w_sgu"], small["b_sgu"], w_out_, small["ln_g"], small["ln_b"])

    return (loss, grad_x,
            *ordered(g_w_ada, g_b_ada, g_w_in, gs, gwout),
            *ordered(d_w_ada, d_b_ada, d_w_in, ds, d_w_out),
            *ordered(nm_w_ada, nm_b_ada, nm_w_in, ms, nm_w_out),
            *ordered(nv_w_ada, nv_b_ada, nv_w_in, vs, nv_w_out))
```

```python
import jax
import jax.numpy as jnp
from jax import lax
from jax.experimental import pallas as pl
from jax.experimental.pallas import tpu as pltpu

F32 = jnp.float32
BF16 = jnp.bfloat16

D_MODEL = 1024
SEQ = 2048
DEPTH = 2
D_POOL = 512
D_SGU = 512
D_IN = 2560
N_GROUPS = 4
GROUP = 128
N_HEADS = 4
HEAD = 128
CHUNK = 128
WINDOWS = (2, 4, 8, 16)
ALPHA = (2.0 * DEPTH) ** 0.25
LN_EPS = 1e-5
N_DEV = 8

ADAM_LR = 0.001
ADAM_B1 = 0.9
ADAM_B2 = 0.999
ADAM_EPS = 1e-08
ADAM_WD = 0.01
ADAM_STEP = 10

TM = 256
HALO = 16
N_TILES = SEQ // TM
VMEM_LIMIT = 60 * 1024 * 1024

ROW_WPOOL = 0
ROW_WSGU = 512
ROW_PSCALE = 1024
ROW_SLNG = 1028
ROW_SLNB = 1032
ROW_BSGU = 1036
ROW_LNG = 1040
ROW_LNB = 1048
PACK_ROWS = 1088
DMOD_COLS = DEPTH * 3 * D_MODEL // 8

SQRT_HALF = 0.7071067811865476
INV_SQRT_2PI = 0.3989422804014327


def _ln(x):
    mu = jnp.mean(x, axis=-1, keepdims=True)
    xc = x - mu
    var = jnp.mean(xc * xc, axis=-1, keepdims=True)
    rstd = lax.rsqrt(var + LN_EPS)
    return xc * rstd, rstd


def _ln_bwd(dxn, xn, rstd):
    m1 = jnp.mean(dxn, axis=-1, keepdims=True)
    m2 = jnp.mean(dxn * xn, axis=-1, keepdims=True)
    return rstd * (dxn - m1 - xn * m2)


def _normal_cdf(x):
    return 0.5 * (1.0 + lax.erf(x * SQRT_HALF))


def _gelu_parts(x, cdf, with_grad):
    if not with_grad:
        return x * cdf, None
    return x * cdf, cdf + x * (INV_SQRT_2PI * jnp.exp(-0.5 * x * x))


def _silu_parts(x):
    s = jax.nn.sigmoid(x)
    return x * s, s * (1.0 + x * (1.0 - s))


def _dot(a, b):
    return lax.dot_general(a, b, (((1,), (0,)), ((), ())), preferred_element_type=F32)


def _dot_nt(a, b):
    return lax.dot_general(a, b, (((1,), (1,)), ((), ())), preferred_element_type=F32)


def _dot_tn(a, b):
    return lax.dot_general(a, b, (((0,), (0,)), ((), ())), preferred_element_type=F32)


def _row_index(tile):
    return tile * TM + lax.broadcasted_iota(jnp.int32, (TM, 1), 0)


def _window_sums(ext, forward):
    n = TM + HALO
    cur = ext
    outs = []
    for g in range(N_GROUPS):
        step = 1 << g
        cur = cur + pltpu.roll(cur, step if forward else n - step, 0)
        rows = cur[HALO:, :GROUP] if forward else cur[:TM, :GROUP]
        outs.append(rows)
        cur = cur[:, GROUP:] if g + 1 < N_GROUPS else None
    return outs


def _inverse_counts(rows):
    return [1.0 / jnp.minimum(rows + 1, w).astype(F32) for w in WINDOWS]


def _tril_bf16(w):
    t = lax.broadcasted_iota(jnp.int32, (CHUNK, CHUNK), 0)
    s = lax.broadcasted_iota(jnp.int32, (CHUNK, CHUNK), 1)
    return jnp.where(t >= s, w, 0.0).astype(BF16)


class _MixWeights:
    def __init__(self, layer, wpool_ref, pscale_ref, slng_ref, slnb_ref, wsgu_ref, bsgut_ref):
        self.layer = layer
        self.wpool_ref, self.pscale_ref, self.slng_ref, self.slnb_ref = wpool_ref, pscale_ref, slng_ref, slnb_ref
        self.wsgu_ref, self.bsgut_ref = wsgu_ref, bsgut_ref

    def pool(self, g):
        return self.wpool_ref[self.layer, g].astype(BF16)

    def pool_scale(self, g):
        return self.pscale_ref[self.layer:self.layer + 1, g * GROUP:(g + 1) * GROUP]

    def ln_gain(self, h):
        return self.slng_ref[self.layer, h:h + 1, :]

    def ln_bias(self, h):
        return self.slnb_ref[self.layer, h:h + 1, :]

    def mix(self, h):
        return _tril_bf16(self.wsgu_ref[self.layer, h])

    def mix_bias(self, h):
        return self.bsgut_ref[self.layer, :, h:h + 1]


SMALL_SPECS = ((DEPTH, N_GROUPS, GROUP, GROUP), (DEPTH, D_POOL), (DEPTH, N_HEADS, HEAD), (DEPTH, N_HEADS, HEAD),
               (DEPTH, N_HEADS, CHUNK, CHUNK), (DEPTH, CHUNK, N_HEADS))


def _mix_forward(proj, halo, tile, w, cdf=None):
    keep = cdf is not None
    rows = _row_index(tile)
    inv_counts = _inverse_counts(rows)
    xa = proj[:, 0:D_POOL]
    ga = proj[:, D_POOL:2 * D_POOL]
    sums = _window_sums(jnp.concatenate([halo, xa], axis=0), True)
    ga_act, ga_grad = _silu_parts(ga)
    pooled, pw, ya = [], [], []
    for g in range(N_GROUPS):
        sl = slice(g * GROUP, (g + 1) * GROUP)
        p = (sums[g] * inv_counts[g] - xa[:, sl]).astype(BF16)
        q = _dot(p, w.pool(g))
        pooled.append(p)
        pw.append(q)
        ya.append(q * w.pool_scale(g) * ga_act[:, sl])

    u = proj[:, 2 * D_POOL:2 * D_POOL + D_SGU]
    v = proj[:, 2 * D_POOL + D_SGU:2 * D_POOL + 2 * D_SGU]
    gb = proj[:, 2 * D_POOL + 2 * D_SGU:]
    gb_act, gb_grad = _silu_parts(gb)
    if cdf is None:
        cdf = jnp.concatenate([_normal_cdf(u), _normal_cdf(v)], axis=1)
    u_act, u_grad = _gelu_parts(u, cdf[:, :D_SGU], keep)
    v_act, v_grad = _gelu_parts(v, cdf[:, D_SGU:], keep)
    vn, vrstd, vln, mixed, yb = [], [], [], [], []
    for h in range(N_HEADS):
        sl = slice(h * HEAD, (h + 1) * HEAD)
        n_h, r_h = _ln(v_act[:, sl])
        l_h = (n_h * w.ln_gain(h) + w.ln_bias(h)).astype(BF16)
        w_h = w.mix(h)
        bias = w.mix_bias(h)
        m_h = jnp.concatenate(
            [_dot(w_h, l_h[k * CHUNK:(k + 1) * CHUNK]) + bias for k in range(TM // CHUNK)], axis=0)
        vn.append(n_h)
        vrstd.append(r_h)
        vln.append(l_h)
        mixed.append(m_h)
        yb.append(u_act[:, sl] * m_h * gb_act[:, sl])
    cat = jnp.concatenate(ya + yb, axis=1)
    if not keep:
        return cat, cdf
    return cat, dict(inv_counts=inv_counts, ga_act=ga_act, ga_grad=ga_grad, pooled=pooled, pw=pw, u_grad=u_grad,
                     v_grad=v_grad, u_act=u_act, gb_act=gb_act, gb_grad=gb_grad, vn=vn, vrstd=vrstd, vln=vln,
                     mixed=mixed)


def _const_spec(shape):
    nd = len(shape)
    return pl.BlockSpec(shape, lambda i: (0,) * nd)


VEC_LNG, VEC_LNB, VEC_POOL, VEC_SGU, VEC_SHIFT, VEC_SCALE, VEC_GATE, VEC_LOSS = range(8)


def _layer_backward(layer, a, b, x, proj, y, cdf, mod, w_int, w_outf, small, ln_g, is_last, name, carry=(),
                    reduce=()):
    n_red, n_carry = len(reduce), len(carry)
    base = layer * PACK_ROWS

    def body(a_ref, b_ref, x_ref, proj_ref, prev_ref, y_ref, cdf_ref, mod_ref, wint_ref, wout_ref, wpool_ref,
             pscale_ref, slng_ref, slnb_ref, wsgu_ref, bsgut_ref, lng_ref, *rest):
        weights = _MixWeights(layer, wpool_ref, pscale_ref, slng_ref, slnb_ref, wsgu_ref, bsgut_ref)
        carry_refs, rest = rest[:n_carry], rest[n_carry:]
        part_refs, rest = rest[:n_red], rest[n_red:]
        dx_ref, dproj_ref, h_ref, cat_ref, dy_ref, small_ref, dmod_ref, loss_ref = rest[:8]
        shard_refs, rest = rest[8:8 + n_red], rest[8 + n_red:]
        vec_ref, dmix_ref, halo_ref = rest[:3]
        step = pl.program_id(0)
        tile = N_TILES - 1 - step

        def scatter():
            bufs, sems = rest[3:3 + 5 * n_red], rest[3 + 5 * n_red:]
            arrays = [dict(part=part_refs[n], out=shard_refs[n], staged=True, stage=bufs[5 * n], sib=bufs[5 * n + 1],
                           snd=bufs[5 * n + 2], rcv=bufs[5 * n + 3], relay=bufs[5 * n + 4]) for n in range(n_red)]
            return _ChipReduceScatter(arrays, *sems)

        @pl.when(step == 0)
        def _():
            small_ref[...] = jnp.zeros_like(small_ref)
            dmod_ref[...] = jnp.zeros_like(dmod_ref)
            vec_ref[...] = jnp.zeros_like(vec_ref)
            dmix_ref[...] = jnp.zeros_like(dmix_ref)
            halo_ref[...] = jnp.zeros_like(halo_ref)
            if n_red:
                scatter().start()

        if n_red:
            @pl.when(step == 1)
            def _():
                scatter().exchange()

            @pl.when(step == N_TILES // 2)
            def _():
                scatter().fold()

        def acc(row, lo, val):
            hi = lo + val.shape[1]
            vec_ref[row:row + 1, lo:hi] += jnp.sum(val, axis=0, keepdims=True)

        xt = x_ref[...]
        yt = y_ref[...]
        shift = mod_ref[layer:layer + 1, 0:D_MODEL]
        scale = mod_ref[layer:layer + 1, D_MODEL:2 * D_MODEL]
        gate = mod_ref[layer:layer + 1, 2 * D_MODEL:]
        ln_gain = lng_ref[layer:layer + 1, :]

        zn, zrstd = _ln(ALPHA * xt + gate * yt)
        if is_last:
            diff = a_ref[...] - b_ref[...]
            acc(VEC_LOSS, 0, diff * diff)
            dout = diff * (1.0 / D_MODEL)
        else:
            dout = a_ref[...]
        acc(VEC_LNG, 0, dout * zn)
        acc(VEC_LNB, 0, dout)
        dz = _ln_bwd(dout * ln_gain, zn, zrstd)
        acc(VEC_GATE, 0, dz * yt)
        dy = (dz * gate).astype(BF16)
        dy_ref[...] = dy
        dcat = _dot_nt(dy, wout_ref[...])

        proj = proj_ref[...]
        prev = jnp.where(tile > 0, prev_ref[...], 0.0)
        cat, k = _mix_forward(proj, prev, tile, weights, cdf_ref[...])
        cat_ref[...] = cat.astype(BF16)

        dga, dq = [], []
        for g in range(N_GROUPS):
            sl = slice(g * GROUP, (g + 1) * GROUP)
            pscale = weights.pool_scale(g)
            dya = dcat[:, sl]
            dyp = dya * k["ga_act"][:, sl]
            dga.append(dya * k["pw"][g] * pscale * k["ga_grad"][:, sl])
            acc(VEC_POOL, g * GROUP, dyp * k["pw"][g])
            dpw = (dyp * pscale).astype(BF16)
            rows = pl.ds(base + ROW_WPOOL + g * GROUP, GROUP)
            small_ref[rows, :] += _dot_tn(k["pooled"][g], dpw)
            dq.append(_dot_nt(dpw, weights.pool(g)))
        dpooled = jnp.concatenate(dq, axis=1)
        scaled = jnp.concatenate([dq[g] * k["inv_counts"][g] for g in range(N_GROUPS)], axis=1)
        sums = _window_sums(jnp.concatenate([scaled, halo_ref[...]], axis=0), False)
        halo_ref[...] = scaled[0:HALO]
        dxa = jnp.concatenate(sums, axis=1) - dpooled

        du, dv, dgb = [], [], []
        for h in range(N_HEADS):
            sl = slice(h * HEAD, (h + 1) * HEAD)
            dyb = dcat[:, D_POOL + h * HEAD:D_POOL + (h + 1) * HEAD]
            m_h = k["mixed"][h]
            ug = k["u_act"][:, sl] * dyb
            du.append(dyb * m_h * k["gb_act"][:, sl] * k["u_grad"][:, sl])
            dgb.append(ug * m_h * k["gb_grad"][:, sl])
            dmixed = ug * k["gb_act"][:, sl]
            dmixed_bf = dmixed.astype(BF16)
            w_h = weights.mix(h)
            dvln_parts = []
            dmix_sum = dmix_ref[h]
            wsgu_rows = pl.ds(base + ROW_WSGU + h * CHUNK, CHUNK)
            dws = small_ref[wsgu_rows, :]
            for c in range(TM // CHUNK):
                cs = slice(c * CHUNK, (c + 1) * CHUNK)
                dmix_sum = dmix_sum + dmixed[cs]
                dws = dws + _dot_nt(dmixed_bf[cs], k["vln"][h][cs])
                dvln_parts.append(_dot_tn(w_h, dmixed_bf[cs]))
            dmix_ref[h] = dmix_sum
            small_ref[wsgu_rows, :] = dws
            dvln = jnp.concatenate(dvln_parts, axis=0)
            acc(VEC_SGU, h * HEAD, dvln * k["vn"][h])
            acc(VEC_SGU, D_SGU + h * HEAD, dvln)
            dvv = _ln_bwd(dvln * weights.ln_gain(h), k["vn"][h], k["vrstd"][h])
            dv.append(dvv * k["v_grad"][:, sl])

        dproj = jnp.concatenate([dxa] + dga + du + dv + dgb, axis=1).astype(BF16)
        dproj_ref[...] = dproj
        dh = _dot(dproj, wint_ref[...])

        xn, xrstd = _ln(xt)
        h_ref[...] = (xn * (1.0 + scale) + shift).astype(BF16)
        acc(VEC_SCALE, 0, dh * xn)
        acc(VEC_SHIFT, 0, dh)
        dx_ref[...] = _ln_bwd(dh * (1.0 + scale), xn, xrstd) + ALPHA * dz

        @pl.when(step == N_TILES - 1)
        def _():
            def put(row0, vec_row, lo, n):
                for r in range(n):
                    small_ref[base + row0 + r:base + row0 + r + 1, :] = (
                        vec_ref[vec_row:vec_row + 1, lo + r * 128:lo + (r + 1) * 128])

            put(ROW_PSCALE, VEC_POOL, 0, 4)
            put(ROW_SLNG, VEC_SGU, 0, 4)
            put(ROW_SLNB, VEC_SGU, D_SGU, 4)
            put(ROW_LNG, VEC_LNG, 0, 8)
            put(ROW_LNB, VEC_LNB, 0, 8)
            ones = jnp.ones((8, HEAD), F32)
            t = lax.broadcasted_iota(jnp.int32, (CHUNK, CHUNK), 0)
            s = lax.broadcasted_iota(jnp.int32, (CHUNK, CHUNK), 1)
            for h in range(N_HEADS):
                bias_rows = lax.dot_general(ones, dmix_ref[h], (((1,), (1,)), ((), ())),
                                            preferred_element_type=F32, precision=lax.Precision.HIGHEST)
                small_ref[base + ROW_BSGU + h:base + ROW_BSGU + h + 1, :] = bias_rows[0:1]
                rows = pl.ds(base + ROW_WSGU + h * CHUNK, CHUNK)
                small_ref[rows, :] = jnp.where(t >= s, small_ref[rows, :], 0.0)
            pieces = ((0, VEC_SHIFT, 0, 768),
                      (1, VEC_SHIFT, 768, 256), (1, VEC_SCALE, 0, 512),
                      (2, VEC_SCALE, 512, 512), (2, VEC_GATE, 0, 256),
                      (3, VEC_GATE, 256, 768))
            filled = [0] * 4
            for q, vec_row, lo, n in pieces:
                row = 4 * layer + q
                dmod_ref[row:row + 1, filled[q]:filled[q] + n] = vec_ref[vec_row:vec_row + 1, lo:lo + n]
                filled[q] += n
            if n_carry:
                for other in range(layer + 1, DEPTH):
                    rows = pl.ds(other * PACK_ROWS, PACK_ROWS)
                    small_ref[rows, :] = carry_refs[0][rows, :]
                    dmod_ref[4 * other:4 * other + 4, :] = carry_refs[1][4 * other:4 * other + 4, :]
            loss_ref[...] = vec_ref[VEC_LOSS:VEC_LOSS + 1, :]
            if n_red:
                scatter().finish()
                scatter().wait_sends()

    rev = lambda w: pl.BlockSpec((TM, w), lambda i: (N_TILES - 1 - i, 0))
    prev_spec = pl.BlockSpec(
        (HALO, D_POOL), lambda i: (jnp.maximum((N_TILES - 1 - i) * (TM // HALO) - 1, 0), 0))
    comm_scratch = []
    for p in reduce:
        comm_scratch += _ChipReduceScatter.buffers(p.shape[2], p.shape[3], p.dtype)
    if n_red:
        comm_scratch += _ChipReduceScatter.semaphores(n_red)
    return pl.pallas_call(
        body,
        name=name,
        grid=(N_TILES,),
        in_specs=[rev(D_MODEL), rev(D_MODEL) if is_last else pl.BlockSpec((TM, D_MODEL), lambda i: (0, 0)),
                  rev(D_MODEL), rev(D_IN), prev_spec, rev(D_MODEL), rev(2 * D_SGU),
                  _const_spec((DEPTH, 3 * D_MODEL)), _const_spec((D_IN, D_MODEL)), _const_spec((D_MODEL, D_MODEL))]
                 + [_const_spec(s) for s in SMALL_SPECS] + [_const_spec((DEPTH, D_MODEL))]
                 + [_const_spec(c.shape) for c in carry] + [ANY] * n_red,
        out_specs=[rev(D_MODEL), rev(D_IN), rev(D_MODEL), rev(D_MODEL), rev(D_MODEL),
                   _const_spec((DEPTH * PACK_ROWS, 128)), _const_spec((8, DMOD_COLS)), _const_spec((1, D_MODEL))]
                  + [_const_spec(p.shape[2:]) for p in reduce],
        out_shape=[jax.ShapeDtypeStruct((SEQ, D_MODEL), F32), jax.ShapeDtypeStruct((SEQ, D_IN), BF16),
                   jax.ShapeDtypeStruct((SEQ, D_MODEL), BF16), jax.ShapeDtypeStruct((SEQ, D_MODEL), BF16),
                   jax.ShapeDtypeStruct((SEQ, D_MODEL), BF16), jax.ShapeDtypeStruct((DEPTH * PACK_ROWS, 128), F32),
                   jax.ShapeDtypeStruct((8, DMOD_COLS), F32), jax.ShapeDtypeStruct((1, D_MODEL), F32)]
                  + [jax.ShapeDtypeStruct(p.shape[2:], F32) for p in reduce],
        scratch_shapes=[pltpu.VMEM((8, D_MODEL), F32), pltpu.VMEM((N_HEADS, CHUNK, HEAD), F32),
                        pltpu.VMEM((HALO, D_POOL), F32)] + comm_scratch,
        compiler_params=pltpu.CompilerParams(dimension_semantics=("arbitrary",), vmem_limit_bytes=VMEM_LIMIT),
    )(a, b, x, proj, proj, y, cdf, mod, w_int, w_outf, *small, ln_g, *carry, *reduce)


def _grad_matmuls(dproj, h, cat, dy, name):
    in_cols, out_cols = D_IN // 4, D_MODEL // 2
    in_steps = D_IN // in_cols

    def body(dproj_ref, h_ref, cat_ref, dy_ref, gin_ref, gout_ref):
        step = pl.program_id(0)

        @pl.when(step < in_steps)
        def _():
            gin_ref[...] = _dot_tn(dproj_ref[...], h_ref[...]).astype(BF16)

        @pl.when(step >= in_steps)
        def _():
            gout_ref[...] = _dot_tn(cat_ref[...], dy_ref[...]).astype(BF16)

    in_block = lambda j: jnp.minimum(j, in_steps - 1)
    out_block = lambda j: jnp.maximum(j - in_steps, 0)
    return pl.pallas_call(
        body,
        name=name,
        grid=(in_steps + D_MODEL // out_cols,),
        in_specs=[pl.BlockSpec((SEQ, in_cols), lambda j: (0, in_block(j))), _const_spec((SEQ, D_MODEL)),
                  pl.BlockSpec((SEQ, out_cols), lambda j: (0, out_block(j))), _const_spec((SEQ, D_MODEL))],
        out_specs=[pl.BlockSpec((in_cols, D_MODEL), lambda j: (in_block(j), 0)),
                   pl.BlockSpec((out_cols, D_MODEL), lambda j: (out_block(j), 0))],
        out_shape=[jax.ShapeDtypeStruct((D_IN, D_MODEL), BF16), jax.ShapeDtypeStruct((D_MODEL, D_MODEL), BF16)],
        compiler_params=pltpu.CompilerParams(dimension_semantics=("arbitrary",), vmem_limit_bytes=VMEM_LIMIT),
    )(dproj, h, cat, dy)


def _adamw_math(w, g, m, v):
    m = ADAM_B1 * m + (1.0 - ADAM_B1) * g
    v = ADAM_B2 * v + (1.0 - ADAM_B2) * (g * g)
    m_hat = m / (1.0 - ADAM_B1 ** ADAM_STEP)
    v_hat = v / (1.0 - ADAM_B2 ** ADAM_STEP)
    delta = -ADAM_LR * (m_hat / (jnp.sqrt(v_hat) + ADAM_EPS) + ADAM_WD * w)
    return delta, m, v


def _adamw(w, grads, m, v, block_rows, name, ride=()):
    rows, cols = grads[0].shape
    blocks = rows // block_rows
    n_ride = len(ride)

    def body(w_ref, m_ref, v_ref, *rest):
        g_refs, rest = rest[:DEPTH], rest[DEPTH:]
        ride_in, (g_ref, d_ref, nm_ref, nv_ref), ride_out = rest[:n_ride], rest[n_ride:n_ride + 4], rest[n_ride + 4:]
        for src, dst in zip(ride_in, ride_out):
            dst[...] = src[...]
        for layer in range(DEPTH):
            @pl.when(pl.program_id(0) == layer)
            def _():
                g = g_refs[layer][...]
                g_ref[...] = g
                d_ref[...], nm_ref[...], nv_ref[...] = _adamw_math(w_ref[...], g, m_ref[...], v_ref[...])

    def grad_spec(layer):
        return pl.BlockSpec((block_rows, cols),
                            lambda l, i: (jnp.where(l == layer, i, jnp.where(l < layer, 0, blocks - 1)), 0))

    spec = pl.BlockSpec((block_rows, cols), lambda l, i: (l * blocks + i, 0))
    ride_specs = [pl.BlockSpec((r.shape[0] // (DEPTH * blocks), r.shape[1]), lambda l, i: (l * blocks + i, 0))
                  for r in ride]
    return pl.pallas_call(
        body,
        name=name,
        grid=(DEPTH, blocks),
        in_specs=[spec] * 3 + [grad_spec(layer) for layer in range(DEPTH)] + ride_specs,
        out_specs=[spec] * 4 + ride_specs,
        out_shape=[jax.ShapeDtypeStruct(w.shape, F32)] * 4 + [jax.ShapeDtypeStruct(r.shape, r.dtype) for r in ride],
        compiler_params=pltpu.CompilerParams(dimension_semantics=("arbitrary", "arbitrary"),
                                             vmem_limit_bytes=VMEM_LIMIT),
    )(w, m, v, *grads, *ride)


MESH = pl.DeviceIdType.MESH
SIBLING = 1
ANY = pl.BlockSpec(memory_space=pl.ANY)
VMEM = pl.BlockSpec(memory_space=pltpu.VMEM)


def _me():
    return lax.axis_index("x"), lax.axis_index("y"), lax.axis_index("c")


def _peer(r):
    x, y, c = _me()
    return (1 - x if r & 4 else x, 1 - y if r & 2 else y, 1 - c if r & 1 else c)


def _index(dev):
    return 4 * dev[0] + 2 * dev[1] + dev[2]


def _remote(src, dst, send_sem, recv_sem, dev):
    return pltpu.make_async_remote_copy(src_ref=src, dst_ref=dst, send_sem=send_sem, recv_sem=recv_sem,
                                        device_id=dev, device_id_type=MESH)


ACROSS_X, ACROSS_Y, ACROSS_BOTH = 4, 2, 6
GATHER_SEMS = 11


class _TwoLevelGather:
    def __init__(self, out, send_sems, recv_sems, src=None):
        self.out, self.send_sems, self.recv_sems, self.src = out, send_sems, recv_sems, src
        self.rows = (out.shape[0] // N_DEV) if len(out.shape) == 2 else out.shape[1]
        self.half = self.rows // 2

    def _slot(self, block):
        if len(self.out.shape) == 2:
            return self.out.at[pl.ds(pl.multiple_of(_index(block) * self.rows, self.rows), self.rows)]
        return self.out.at[_index(block)]

    def _copy(self, k, block, part, to, src=None):
        slot = self._slot(block)
        if part is not None:
            rows = pl.ds(part * self.half, self.half)
            slot = slot.at[rows]
            src = None if src is None else src.at[rows]
        return _remote(slot if src is None else src, slot, self.send_sems.at[k], self.recv_sems.at[k], to)

    def _mine(self):
        me = _me()
        src = self._slot(me) if self.src is None else self.src
        x, y = _peer(ACROSS_X), _peer(ACROSS_Y)
        return [self._copy(1, me, 0, x, src), self._copy(3, me, 1, y, src), self._copy(0, me, None, _peer(SIBLING), src),
                self._copy(2, me, 1, x, src), self._copy(4, me, 0, y, src)]

    def _relayed(self):
        return [self._copy(5, _peer(ACROSS_X), 0, _peer(ACROSS_Y)), self._copy(6, _peer(ACROSS_Y), 1, _peer(ACROSS_X))]

    def _passed(self):
        sib, far = _peer(SIBLING), _peer(ACROSS_BOTH)
        return [self._copy(7, _peer(ACROSS_X), None, sib), self._copy(8, _peer(ACROSS_Y), None, sib),
                self._copy(9, far, 0, sib), self._copy(10, far, 1, sib)]

    def _arrival(self, k, r, part):
        return self._copy(k, _peer(r), part, _me())

    def send_first(self):
        for cp in self._mine()[:3]:
            cp.start()

    def send_second(self):
        for cp in self._mine()[3:]:
            cp.start()

    def send_mine(self):
        self.send_first()
        self.send_second()

    def relay(self):
        relayed = self._relayed()
        self._arrival(1, ACROSS_X, 0).wait_recv()
        relayed[0].start()
        self._arrival(3, ACROSS_Y, 1).wait_recv()
        relayed[1].start()

    def pass_near(self):
        passed = self._passed()
        self._arrival(2, ACROSS_X, 1).wait_recv()
        passed[0].start()
        self._arrival(4, ACROSS_Y, 0).wait_recv()
        passed[1].start()

    def pass_far(self):
        passed = self._passed()
        self._arrival(5, ACROSS_BOTH, 0).wait_recv()
        passed[2].start()
        self._arrival(6, ACROSS_BOTH, 1).wait_recv()
        passed[3].start()

    def pass_on(self):
        self.pass_near()
        self.pass_far()

    def wait_sibling(self):
        self._arrival(0, SIBLING, None).wait_recv()

    def wait_passed(self, r):
        if r == ACROSS_BOTH:
            self._arrival(9, r ^ SIBLING, 0).wait_recv()
            self._arrival(10, r ^ SIBLING, 1).wait_recv()
        else:
            self._arrival(7 if r == ACROSS_X else 8, r ^ SIBLING, None).wait_recv()

    def wait_rest(self):
        self.wait_sibling()
        for r in (ACROSS_X, ACROSS_Y, ACROSS_BOTH):
            self.wait_passed(r)

    def wait_sends(self):
        for cp in self._mine() + self._relayed() + self._passed():
            cp.wait_send()


class _ChipReduceScatter:
    SLOTS = 6

    def __init__(self, arrays, l_sem, d_send, d_recv, i_send, i_recv):
        self.arrays = arrays
        self.l_sem, self.d_send, self.d_recv, self.i_send, self.i_recv = l_sem, d_send, d_recv, i_send, i_recv

    @staticmethod
    def buffers(rows, cols, dtype, staged=True):
        stage = [pltpu.VMEM((4, rows, cols), dtype)] if staged else []
        return stage + [pltpu.VMEM((4, rows, cols), dtype), pltpu.VMEM((3, rows, cols), dtype),
                        pltpu.VMEM((2, rows, cols), dtype), pltpu.VMEM((2, rows // 2, cols), dtype)]

    @classmethod
    def semaphores(cls, n):
        return [pltpu.SemaphoreType.DMA((n,)), pltpu.SemaphoreType.DMA((n, 4)), pltpu.SemaphoreType.DMA((n, 4)),
                pltpu.SemaphoreType.DMA((n, cls.SLOTS)), pltpu.SemaphoreType.DMA((n, cls.SLOTS))]

    def _pick(self, which):
        return list(enumerate(self.arrays)) if which is None else [(n, self.arrays[n]) for n in which]

    @staticmethod
    def _chip(r):
        dev = _me() if r is None else _peer(r)
        return 2 * dev[0] + dev[1]

    def _staging(self, which):
        c = _me()[2]
        return [pltpu.make_async_copy(a["part"].at[pl.ds(0, 4), c], a["stage"], self.l_sem.at[n])
                for n, a in self._pick(which) if a["staged"]]

    def _first(self, which, chip):
        other = 1 - _me()[2]
        return [_remote(a["part"].at[chip, other], a["sib"].at[chip], self.d_send.at[n, chip], self.d_recv.at[n, chip],
                        _peer(SIBLING)) for n, a in self._pick(which)]

    @staticmethod
    def _halves(a):
        half = a["rcv"].shape[1] // 2
        return pl.ds(0, half), pl.ds(half, half)

    def _hops(self, n, a):
        h0, h1 = self._halves(a)
        x, y = _peer(ACROSS_X), _peer(ACROSS_Y)
        snd, rcv, relay = a["snd"], a["rcv"], a["relay"]
        pairs = [(snd.at[2, h0], relay.at[0], x), (snd.at[2, h1], relay.at[1], y),
                 (snd.at[0, h0], rcv.at[0, h0], x), (snd.at[0, h1], rcv.at[0, h1], x),
                 (snd.at[1, h1], rcv.at[1, h1], y), (snd.at[1, h0], rcv.at[1, h0], y)]
        return [_remote(s, d, self.i_send.at[n, k], self.i_recv.at[n, k], to) for k, (s, d, to) in enumerate(pairs)]

    def _mine(self, a, chip, rows=None):
        src = a["stage"].at[chip] if a["staged"] else a["part"].at[chip, _me()[2]]
        mine, sib = (src[...], a["sib"][chip]) if rows is None else (src[rows, :], a["sib"][chip, rows, :])
        return mine.astype(F32) + sib.astype(F32)

    def start(self, which=None, chips=None):
        if chips is None:
            for cp in self._staging(which):
                cp.start()
        for chip in range(4) if chips is None else chips:
            for cp in self._first(which, chip):
                cp.start()

    def send_far(self, which=None):
        far = self._chip(ACROSS_BOTH)
        for cp in self._staging(which):
            cp.wait()
        for cp in self._first(which, far):
            cp.wait_recv()
        for n, a in self._pick(which):
            hops = self._hops(n, a)
            a["snd"][2] = self._mine(a, far).astype(a["snd"].dtype)
            hops[0].start()
            hops[1].start()

    def send_near(self, r, which=None):
        chip = self._chip(r)
        for cp in self._first(which, chip):
            cp.wait_recv()
        for n, a in self._pick(which):
            h0, h1 = self._halves(a)
            hops = self._hops(n, a)
            if r == ACROSS_X:
                a["snd"][0, h0, :] = self._mine(a, chip, h0).astype(a["snd"].dtype)
                hops[2].start()
            else:
                a["snd"][1, h1, :] = self._mine(a, chip, h1).astype(a["snd"].dtype)
                hops[4].start()

    def exchange(self, which=None):
        self.send_far(which)
        self.send_near(ACROSS_X, which)
        self.send_near(ACROSS_Y, which)

    def fold(self, which=None):
        across_x, across_y = self._chip(ACROSS_X), self._chip(ACROSS_Y)
        for n, a in self._pick(which):
            h0, h1 = self._halves(a)
            hops = self._hops(n, a)
            dtype = a["snd"].dtype
            hops[1].wait_recv()
            a["snd"][0, h1, :] = (self._mine(a, across_x, h1) + a["relay"][1].astype(F32)).astype(dtype)
            hops[3].start()
            hops[0].wait_recv()
            a["snd"][1, h0, :] = (self._mine(a, across_y, h0) + a["relay"][0].astype(F32)).astype(dtype)
            hops[5].start()

    def finish(self, which=None):
        home = self._chip(None)
        for cp in self._first(which, home):
            cp.wait_recv()
        for n, a in self._pick(which):
            hops = self._hops(n, a)
            a["out"][...] = self._mine(a, home)
            hops[2].wait_recv()
            hops[3].wait_recv()
            a["out"][...] += a["rcv"][0].astype(F32)
            hops[4].wait_recv()
            hops[5].wait_recv()
            a["out"][...] += a["rcv"][1].astype(F32)

    def wait_sends(self, which=None):
        for chip in range(4):
            for cp in self._first(which, chip):
                cp.wait_send()
        for n, a in self._pick(which):
            for cp in self._hops(n, a):
                cp.wait_send()


def _direct_exchange(src_of, dst_of, send_sems, recv_sems):
    me = _me()
    copies = [_remote(src_of(_peer(r)), dst_of(me), send_sems.at[r - 1], recv_sems.at[r - 1], _peer(r))
              for r in range(1, N_DEV)]
    for cp in copies:
        cp.start()
    return copies


def _wait_direct(copies):
    for cp in copies:
        cp.wait_recv()
    for cp in copies:
        cp.wait_send()


def _forward(x, c, w_ada, b_ada, small, ln_g, ln_b, mine, following):
    assert DEPTH == 2
    cols = w_ada.shape[2]
    shard = mine[0].shape[0]
    pair = 2 * shard

    def body(x_hbm, c_ref, wada_hbm, bada_ref, wpool_ref, pscale_ref, slng_ref, slnb_ref, wsgu_ref, bsgut_ref,
             lng_ref, lnb_ref, wint_hbm, wout_hbm, next_in_hbm, next_out_hbm,
             out0_ref, y0_ref, cdf0_ref, proj0_hbm, out1_ref, y1_ref, cdf1_ref, proj1_hbm,
             wint_keep, wout_keep, wint_next, wout_next, acts_ref, mod_ref,
             wint_v, wout_v, h_buf, proj_blk, proj_tile, halo_ref, x_ref, w_send, w_recv, w_local, p_sems,
             t_sems, in_sems, act_all, act_src, part, mod_recv, wada_ref, a_send, a_recv, m_send, m_recv,
             n_send, n_recv, n_local, f_sems):
        step = pl.program_id(0)
        chip_of = lambda dev: 2 * dev[0] + dev[1]

        def hosted():
            return [_TwoLevelGather(out, n_send.at[n], n_recv.at[n], src=src)
                    for n, (out, src) in enumerate(((wint_next, next_in_hbm), (wout_next, next_out_hbm)))]

        def hosted_own():
            me = _me()
            return [pltpu.make_async_copy(g.src, g._slot(me), n_local.at[n]) for n, g in enumerate(hosted())]

        def gathers():
            return (_TwoLevelGather(wint_v, w_send.at[0], w_recv.at[0], src=wint_hbm),
                    _TwoLevelGather(wout_v, w_send.at[1], w_recv.at[1], src=wout_hbm))

        def keeps():
            return [pltpu.make_async_copy(wint_v, wint_keep, w_local.at[2]),
                    pltpu.make_async_copy(wout_v, wout_keep, w_local.at[3])]

        def tile_read(t):
            slot = t % 2
            return pltpu.make_async_copy(proj0_hbm.at[pl.ds(pl.multiple_of(t * TM, TM), TM)], proj_tile.at[slot],
                                         t_sems.at[slot])

        def first_layer_start():
            writes = []

            def project(n, dev):
                first = pl.multiple_of(chip_of(dev) * pair, pair)
                if n >= 2:
                    writes[n - 2].wait()

                @pl.loop(0, N_TILES)
                def _(t):
                    rows = pl.ds(pl.multiple_of(t * TM, TM), TM)
                    proj_blk[n % 2, rows, :] = _dot_nt(h_buf[rows, :], wint_v[pl.ds(first, pair), :])

                cp = pltpu.make_async_copy(proj_blk.at[n % 2], proj0_hbm.at[:, pl.ds(first, pair)], p_sems.at[n % 2])
                cp.start()
                writes.append(cp)

            me = _me()
            halo_ref[...] = jnp.zeros_like(halo_ref)
            gather_in, gather_out = gathers()
            own_in = pltpu.make_async_copy(wint_hbm, gather_in._slot(me), w_local.at[0])
            own_out = pltpu.make_async_copy(wout_hbm, gather_out._slot(me), w_local.at[1])
            x_load = pltpu.make_async_copy(x_hbm, x_ref, in_sems.at[0])
            x_load.start()
            wada_load = pltpu.make_async_copy(wada_hbm, wada_ref, in_sems.at[1])
            wada_load.start()
            mine_index = _index(me)
            cval = c_ref[...]
            act_src[...] = jnp.zeros_like(act_src)
            act_src[0:1, :] = cval * jax.nn.sigmoid(cval)
            act_all[mine_index] = act_src[...]
            act_copies = _direct_exchange(lambda p: act_src, lambda m: act_all.at[_index(m)], a_send, a_recv)

            own_in.start()
            own_out.start()
            gather_in.send_first()

            _wait_direct(act_copies)
            acts = jnp.concatenate([act_all[j, 0:1, :] for j in range(N_DEV)], axis=0)
            acts_ref[...] = acts
            part[...] = jnp.zeros_like(part)
            wada_load.wait()
            for l in range(DEPTH):
                res = lax.dot_general(acts, wada_ref[l], (((1,), (0,)), ((), ())), preferred_element_type=F32,
                                      precision=lax.Precision.HIGHEST)
                for b in range(N_DEV):
                    part[b, l:l + 1, :] = res[b:b + 1, :]
            mod_recv[mine_index] = part[mine_index]
            mod_copies = _direct_exchange(lambda p: part.at[_index(p)], lambda m: mod_recv.at[_index(m)],
                                          m_send, m_recv)
            gather_in.send_second()
            gather_out.send_mine()

            _wait_direct(mod_copies)
            for l in range(DEPTH):
                for j in range(N_DEV):
                    sl = slice(j * cols, (j + 1) * cols)
                    mod_ref[l:l + 1, sl] = mod_recv[j, l:l + 1, :] + bada_ref[l:l + 1, sl]
            x_load.wait()
            shift = mod_ref[0:1, 0:D_MODEL]
            scale = mod_ref[0:1, D_MODEL:2 * D_MODEL]

            @pl.loop(0, N_TILES)
            def _(t):
                rows = pl.ds(pl.multiple_of(t * TM, TM), TM)
                xn, _ = _ln(x_ref[rows, :])
                h_buf[rows, :] = (xn * (1.0 + scale) + shift).astype(BF16)

            gather_in.relay()
            own_in.wait()
            gather_in.wait_sibling()
            project(0, me)
            gather_in.pass_near()
            gather_in.wait_passed(ACROSS_X)
            project(1, _peer(ACROSS_X))
            gather_out.relay()
            for cp in hosted_own():
                cp.start()
            for g in hosted():
                g.send_mine()
            gather_in.wait_passed(ACROSS_Y)
            project(2, _peer(ACROSS_Y))
            gather_in.pass_far()
            gather_in.wait_passed(ACROSS_BOTH)
            project(3, _peer(ACROSS_BOTH))

            gather_out.pass_on()
            gather_out.wait_rest()
            own_out.wait()
            for cp in keeps():
                cp.start()
            writes[2].wait()
            writes[3].wait()
            tile_read(0).start()

        def fetches():
            return [pltpu.make_async_copy(wint_next, wint_v, f_sems.at[0]),
                    pltpu.make_async_copy(wout_next, wout_v, f_sems.at[1])]

        def tile_write(t):
            slot = t % 2
            return pltpu.make_async_copy(proj_tile.at[slot], proj1_hbm.at[pl.ds(pl.multiple_of(t * TM, TM), TM)],
                                         t_sems.at[slot])

        def mix_and_close(layer, tile, rows, proj, out_ref, y_ref, cdf_ref):
            weights = _MixWeights(layer, wpool_ref, pscale_ref, slng_ref, slnb_ref, wsgu_ref, bsgut_ref)
            xt = x_ref[rows, :]
            gate = mod_ref[layer:layer + 1, 2 * D_MODEL:]
            cat, cdf_ref[...] = _mix_forward(proj, halo_ref[...], tile, weights)
            halo_ref[...] = proj[TM - HALO:, 0:D_POOL]
            if layer == 1:
                @pl.when(tile == 0)
                def _():
                    fetches()[1].wait()
            y = _dot(cat.astype(BF16), wout_v[...])
            y_ref[...] = y
            zn, _ = _ln(ALPHA * xt + gate * y)
            out = zn * lng_ref[layer:layer + 1, :] + lnb_ref[layer:layer + 1, :]
            out_ref[...] = out
            if layer == 0:
                x_ref[rows, :] = out

        def first_layer_tile(tile):
            @pl.when(tile + 1 < N_TILES)
            def _():
                tile_read(tile + 1).start()

            tile_read(tile).wait()
            rows = pl.ds(pl.multiple_of(tile * TM, TM), TM)
            mix_and_close(0, tile, rows, proj_tile[tile % 2], out0_ref, y0_ref, cdf0_ref)

        def second_layer_tile(tile):
            rows = pl.ds(pl.multiple_of(tile * TM, TM), TM)
            shift = mod_ref[1:2, 0:D_MODEL]
            scale = mod_ref[1:2, D_MODEL:2 * D_MODEL]
            xn, _ = _ln(x_ref[rows, :])
            h = (xn * (1.0 + scale) + shift).astype(BF16)

            @pl.when(tile >= 2)
            def _():
                tile_write(tile - 2).wait()

            proj_tile[tile % 2] = _dot_nt(h, wint_v[...])
            tile_write(tile).start()
            mix_and_close(1, tile, rows, proj_tile[tile % 2], out1_ref, y1_ref, cdf1_ref)

        @pl.when(step < N_TILES)
        def _():
            @pl.when(step == 0)
            def _():
                first_layer_start()

            @pl.when(step == 1)
            def _():
                for g in hosted():
                    g.relay()

            @pl.when(step == N_TILES // 2)
            def _():
                for g in hosted():
                    g.pass_near()

            @pl.when(step == N_TILES - 1)
            def _():
                for g in hosted():
                    g.pass_far()
                for g in gathers():
                    g.wait_sends()
                for cp in keeps() + hosted_own():
                    cp.wait()
                hosted()[0].wait_rest()
                fetches()[0].start()

            first_layer_tile(step)

        @pl.when(step >= N_TILES)
        def _():
            @pl.when(step == N_TILES)
            def _():
                halo_ref[...] = jnp.zeros_like(halo_ref)
                hosted()[1].wait_rest()
                fetches()[1].start()
                fetches()[0].wait()

            second_layer_tile(step - N_TILES)

            @pl.when(step == 2 * N_TILES - 1)
            def _():
                for g in hosted():
                    g.wait_sends()
                tile_write(N_TILES - 2).wait()
                tile_write(N_TILES - 1).wait()

    first = lambda w: pl.BlockSpec((TM, w), lambda i: (jnp.minimum(i, N_TILES - 1), 0))
    second = lambda w: pl.BlockSpec((TM, w), lambda i: (jnp.maximum(i - N_TILES, 0), 0))
    gather_sems = pltpu.SemaphoreType.DMA((2, GATHER_SEMS))
    seven = pltpu.SemaphoreType.DMA((7,))
    per_layer = [jax.ShapeDtypeStruct((SEQ, D_MODEL), F32), jax.ShapeDtypeStruct((SEQ, D_MODEL), F32),
                 jax.ShapeDtypeStruct((SEQ, 2 * D_SGU), F32), jax.ShapeDtypeStruct((SEQ, D_IN), F32)]
    gathered = [jax.ShapeDtypeStruct((D_IN, D_MODEL), BF16), jax.ShapeDtypeStruct((D_MODEL, D_MODEL), BF16)]
    res = pl.pallas_call(
        body,
        name="layers_fwd",
        grid=(DEPTH * N_TILES,),
        in_specs=[ANY, _const_spec(c.shape), ANY, _const_spec(b_ada.shape)] + [_const_spec(s) for s in SMALL_SPECS]
                 + [_const_spec((DEPTH, D_MODEL)), _const_spec((DEPTH, D_MODEL))] + [ANY] * 4,
        out_specs=[first(D_MODEL), first(D_MODEL), first(2 * D_SGU), ANY,
                   second(D_MODEL), second(D_MODEL), second(2 * D_SGU), ANY] + [ANY] * 4
                  + [_const_spec((N_DEV, D_MODEL)), _const_spec((DEPTH, 3 * D_MODEL))],
        out_shape=per_layer * 2 + gathered * 2 + [jax.ShapeDtypeStruct((N_DEV, D_MODEL), F32),
                                                  jax.ShapeDtypeStruct((DEPTH, 3 * D_MODEL), F32)],
        scratch_shapes=[pltpu.VMEM((D_IN, D_MODEL), BF16), pltpu.VMEM((D_MODEL, D_MODEL), BF16),
                        pltpu.VMEM((SEQ, D_MODEL), BF16), pltpu.VMEM((2, SEQ, pair), F32),
                        pltpu.VMEM((2, TM, D_IN), F32), pltpu.VMEM((HALO, D_POOL), F32),
                        pltpu.VMEM((SEQ, D_MODEL), F32),
                        gather_sems, gather_sems, pltpu.SemaphoreType.DMA((4,)), pltpu.SemaphoreType.DMA((2,)),
                        pltpu.SemaphoreType.DMA((2,)), pltpu.SemaphoreType.DMA((2,)),
                        pltpu.VMEM((N_DEV, 8, D_MODEL), F32), pltpu.VMEM((8, D_MODEL), F32),
                        pltpu.VMEM((N_DEV, 8, cols), F32), pltpu.VMEM((N_DEV, 8, cols), F32),
                        pltpu.VMEM(w_ada.shape, F32), seven, seven, seven, seven,
                        gather_sems, gather_sems, pltpu.SemaphoreType.DMA((2,)), pltpu.SemaphoreType.DMA((2,))],
        compiler_params=pltpu.CompilerParams(dimension_semantics=("arbitrary",), vmem_limit_bytes=VMEM_LIMIT),
    )(x, c, w_ada, b_ada, *small, ln_g, ln_b, *mine, *following)
    return res[0:4], res[4:8], res[8:10], res[10:12], res[12], res[13]


ADA_CHUNK = 256


def _grad_tail(dproj, h, cat, dy, small, dmod8, loss_lanes, w_ada, m_ada, v_ada, act_t, b_ada, m_bada, v_bada):
    shard_in, shard_out, shard_small = D_IN // N_DEV, D_MODEL // N_DEV, small.shape[2]
    cols = w_ada.shape[2]
    W_IN, W_OUT, SMALL = 0, 1, 2

    def body(dproj_hbm, h_hbm, cat_hbm, dy_hbm, small_hbm, dmod_ref, lanes_ref, wada_hbm, mada_hbm, vada_hbm,
             act_ref, bada_ref, mbada_ref, vbada_ref,
             gwin_ref, gwout_ref, stot_ref, loss_ref, gada_hbm, dada_hbm, nmada_hbm, nvada_hbm,
             gb_ref, db_ref, nmb_ref, nvb_ref,
             dproj_v, h_v, cat_v, dy_v, part_in, part_out, own_small, loss_src, loss_all, dmod_all, ada_in, ada_out,
             *rest):
        bufs, rest = rest[:13], rest[13:]
        load_sems, rs_sems = rest[0], rest[1:6]
        m_send, m_recv, g_send, g_recv, s_send, s_recv, ada_lsem, ada_ssem = rest[6:]
        mine = _index(_me())

        def update_ada():
            upper = (mine % 2) == 1

            def dmod_of(layer):
                rows = []
                for b in range(N_DEV):
                    r = dmod_all[b, pl.ds(4 * layer + mine // 2, 1), :]
                    rows.append(jnp.where(upper, r[:, cols:], r[:, :cols]))
                return jnp.concatenate(rows, axis=0)

            chunks = [(layer, c) for layer in range(DEPTH) for c in range(D_MODEL // ADA_CHUNK)]

            def loads(i):
                layer, c = chunks[i]
                rows = pl.ds(c * ADA_CHUNK, ADA_CHUNK)
                return [pltpu.make_async_copy(src.at[layer, rows], ada_in.at[i % 2, k], ada_lsem.at[i % 2, k])
                        for k, src in enumerate((wada_hbm, mada_hbm, vada_hbm))]

            def stores(i):
                layer, c = chunks[i]
                rows = pl.ds(c * ADA_CHUNK, ADA_CHUNK)
                return [pltpu.make_async_copy(ada_out.at[i % 2, k], dst.at[layer, rows], ada_ssem.at[i % 2, k])
                        for k, dst in enumerate((gada_hbm, dada_hbm, nmada_hbm, nvada_hbm))]

            for cp in loads(0):
                cp.start()
            dmods = {}
            for i, (layer, c) in enumerate(chunks):
                if i + 1 < len(chunks):
                    for cp in loads(i + 1):
                        cp.start()
                for cp in loads(i):
                    cp.wait()
                if i >= 2:
                    for cp in stores(i - 2):
                        cp.wait()
                if layer not in dmods:
                    dmods[layer] = dmod_of(layer)
                act = act_ref[pl.ds(c * ADA_CHUNK, ADA_CHUNK), :]
                g = act[:, 0:1] * dmods[layer][0:1, :]
                for b in range(1, N_DEV):
                    g = g + act[:, b:b + 1] * dmods[layer][b:b + 1, :]
                slot = i % 2
                delta, new_m, new_v = _adamw_math(ada_in[slot, 0], g, ada_in[slot, 1], ada_in[slot, 2])
                ada_out[slot, 0] = g
                ada_out[slot, 1] = delta
                ada_out[slot, 2] = new_m
                ada_out[slot, 3] = new_v
                for cp in stores(i):
                    cp.start()
            for i in (len(chunks) - 2, len(chunks) - 1):
                for cp in stores(i):
                    cp.wait()

            total = dmod_all[0]
            for b in range(1, N_DEV):
                total = total + dmod_all[b]
            width = total.shape[1]
            for layer in range(DEPTH):
                for q in range(4):
                    gb_ref[layer:layer + 1, q * width:(q + 1) * width] = total[4 * layer + q:4 * layer + q + 1, :]
            db_ref[...], nmb_ref[...], nvb_ref[...] = _adamw_math(bada_ref[...], gb_ref[...], mbada_ref[...],
                                                                  vbada_ref[...])

        order = (ACROSS_BOTH, ACROSS_X, ACROSS_Y, None)
        chips = [_ChipReduceScatter._chip(r) for r in order]
        loads = [pltpu.make_async_copy(s, d, load_sems.at[n]) for n, (s, d) in enumerate(
            ((cat_hbm, cat_v), (dy_hbm, dy_v), (h_hbm, h_v)))]
        loads += [pltpu.make_async_copy(dproj_hbm.at[:, pl.ds(pl.multiple_of(chip * 2 * shard_in, 2 * shard_in),
                                                             2 * shard_in)], dproj_v.at[n], load_sems.at[3 + n])
                  for n, chip in enumerate(chips)]
        for cp in loads[2:] + loads[:2]:
            cp.start()
        arrays = [dict(part=part_in, out=gwin_ref, staged=False, sib=bufs[0], snd=bufs[1], rcv=bufs[2], relay=bufs[3]),
                  dict(part=part_out, out=gwout_ref, staged=False, sib=bufs[4], snd=bufs[5], rcv=bufs[6],
                       relay=bufs[7]),
                  dict(part=small_hbm, out=own_small, staged=True, stage=bufs[8], sib=bufs[9], snd=bufs[10],
                       rcv=bufs[11], relay=bufs[12])]
        scatter = _ChipReduceScatter(arrays, *rs_sems)
        scatter.start([SMALL])
        dmod_all[mine] = dmod_ref[...]
        dmod_copies = _direct_exchange(lambda p: dmod_ref, lambda m: dmod_all.at[_index(m)], m_send, m_recv)
        loss_src[...] = jnp.full(loss_src.shape, (0.5 / D_MODEL) * jnp.sum(lanes_ref[...]), F32)
        loss_all[mine] = loss_src[...]
        loss_copies = _direct_exchange(lambda p: loss_src, lambda m: loss_all.at[_index(m)], s_send, s_recv)

        gather = _TwoLevelGather(stot_ref, g_send, g_recv)
        loads[2].wait()
        for n, chip in enumerate(chips):
            loads[3 + n].wait()
            res = _dot_tn(dproj_v[n], h_v[...]).astype(BF16)
            part_in[chip, 0] = res[:shard_in]
            part_in[chip, 1] = res[shard_in:]
            scatter.start([W_IN], chips=[chip])
            if n == 0:
                scatter.exchange([SMALL])
            if n == 1:
                scatter.send_far([W_IN])
                scatter.fold([SMALL])
            if n == 2:
                scatter.send_near(ACROSS_X, [W_IN])
                scatter.finish([SMALL])
                stot_ref[mine] = own_small[...]
                gather.send_mine()
            if n == 3:
                scatter.send_near(ACROSS_Y, [W_IN])

        loads[0].wait()
        loads[1].wait()
        for blk in range(2):
            res = _dot_tn(cat_v[:, blk * 512:(blk + 1) * 512], dy_v[...]).astype(BF16)
            for s in range(4):
                part_out[2 * blk + s // 2, s % 2] = res[s * shard_out:(s + 1) * shard_out]
        scatter.start([W_OUT])
        scatter.fold([W_IN])
        scatter.exchange([W_OUT])
        gather.relay()
        _wait_direct(dmod_copies)
        update_ada()
        scatter.fold([W_OUT])
        gather.pass_on()
        gather.wait_rest()
        _wait_direct(loss_copies)
        total = loss_all[0]
        for j in range(1, N_DEV):
            total = total + loss_all[j]
        loss_ref[...] = total
        scatter.finish([W_IN])
        scatter.finish([W_OUT])
        gather.wait_sends()
        scatter.wait_sends()

    buffers = _ChipReduceScatter.buffers
    comm_scratch = (buffers(shard_in, D_MODEL, BF16, staged=False) + buffers(shard_out, D_MODEL, BF16, staged=False)
                    + buffers(shard_small, 128, F32))
    comm_scratch += [pltpu.SemaphoreType.DMA((7,))] + _ChipReduceScatter.semaphores(3)
    comm_scratch += [pltpu.SemaphoreType.DMA((n,)) for n in (7, 7, GATHER_SEMS, GATHER_SEMS, 7, 7)]
    comm_scratch += [pltpu.SemaphoreType.DMA((2, 3)), pltpu.SemaphoreType.DMA((2, 4))]
    return pl.pallas_call(
        body,
        name="grad_tail",
        in_specs=[ANY] * 5 + [VMEM, VMEM] + [ANY] * 3 + [VMEM] * 4,
        out_specs=[VMEM] * 4 + [ANY] * 4 + [VMEM] * 4,
        out_shape=[jax.ShapeDtypeStruct((shard_in, D_MODEL), F32), jax.ShapeDtypeStruct((shard_out, D_MODEL), F32),
                   jax.ShapeDtypeStruct((N_DEV, shard_small, 128), F32), jax.ShapeDtypeStruct((8, 128), F32)]
                  + [jax.ShapeDtypeStruct(w_ada.shape, F32)] * 4 + [jax.ShapeDtypeStruct(b_ada.shape, F32)] * 4,
        scratch_shapes=[pltpu.VMEM((4, SEQ, 2 * shard_in), BF16), pltpu.VMEM(h.shape, BF16), pltpu.VMEM(cat.shape, BF16),
                        pltpu.VMEM(dy.shape, BF16), pltpu.VMEM((4, 2, shard_in, D_MODEL), BF16),
                        pltpu.VMEM((4, 2, shard_out, D_MODEL), BF16), pltpu.VMEM((shard_small, 128), F32),
                        pltpu.VMEM((8, 128), F32), pltpu.VMEM((N_DEV, 8, 128), F32),
                        pltpu.VMEM((N_DEV,) + dmod8.shape, F32), pltpu.VMEM((2, 3, ADA_CHUNK, cols), F32),
                        pltpu.VMEM((2, 4, ADA_CHUNK, cols), F32)] + comm_scratch,
        compiler_params=pltpu.CompilerParams(vmem_limit_bytes=VMEM_LIMIT),
    )(dproj, h, cat, dy, small, dmod8, loss_lanes, w_ada, m_ada, v_ada, act_t, b_ada, m_bada, v_bada)


SMALL_NAMES = ("w_pool", "w_sgu", "pool_scale", "sgu_ln_g", "sgu_ln_b", "b_sgu", "ln_g", "ln_b")
SMALL_ROWS = (512, 512, 4, 4, 4, 4, 8, 8)


def _adamw_small(g_packed, ws, ms, vs, name):
    n = len(SMALL_NAMES)

    def body(g_ref, *refs):
        w_refs, m_refs, v_refs = refs[:n], refs[n:2 * n], refs[2 * n:3 * n]
        outs = refs[3 * n:]

        def update(p, at, g):
            delta, new_m, new_v = _adamw_math(w_refs[p][at], g, m_refs[p][at], v_refs[p][at])
            outs[p][at] = g
            outs[n + p][at] = delta
            outs[2 * n + p][at] = new_m
            outs[3 * n + p][at] = new_v

        row = 0
        for p, r in enumerate(SMALL_ROWS):
            shape = ws[p].shape
            for layer in range(DEPTH):
                first = layer * PACK_ROWS + row
                if len(shape) == 4:
                    for k in range(shape[1]):
                        update(p, (layer, k), g_ref[first + k * shape[2]:first + (k + 1) * shape[2], :])
                elif len(shape) == 3:
                    update(p, (layer,), g_ref[first:first + r, :])
                else:
                    g = jnp.concatenate([g_ref[first + k:first + k + 1, :] for k in range(r)], axis=1)
                    update(p, (slice(layer, layer + 1), slice(None)), g)
            row += r

    res = pl.pallas_call(
        body,
        name=name,
        out_shape=[jax.ShapeDtypeStruct(w.shape, F32) for w in ws] * 4,
        compiler_params=pltpu.CompilerParams(vmem_limit_bytes=VMEM_LIMIT),
    )(g_packed, *ws, *ms, *vs)
    return res[:n], res[n:2 * n], res[2 * n:3 * n], res[3 * n:]


def kernel(x, c, w_ada, b_ada, w_in, w_pool, pool_scale, sgu_ln_g, sgu_ln_b, w_sgu, b_sgu, w_out, ln_g, ln_b, loss_target, m_w_ada, m_b_ada, m_w_in, m_w_pool, m_pool_scale, m_sgu_ln_g, m_sgu_ln_b, m_w_sgu, m_b_sgu, m_w_out, m_ln_g, m_ln_b, v_w_ada, v_b_ada, v_w_in, v_w_pool, v_pool_scale, v_sgu_ln_g, v_sgu_ln_b, v_w_sgu, v_b_sgu, v_w_out, v_ln_g, v_ln_b):
    small_w = dict(w_pool=w_pool, w_sgu=w_sgu, pool_scale=pool_scale, sgu_ln_g=sgu_ln_g, sgu_ln_b=sgu_ln_b,
                   b_sgu=b_sgu, ln_g=ln_g, ln_b=ln_b)
    small_m = dict(w_pool=m_w_pool, w_sgu=m_w_sgu, pool_scale=m_pool_scale, sgu_ln_g=m_sgu_ln_g,
                   sgu_ln_b=m_sgu_ln_b, b_sgu=m_b_sgu, ln_g=m_ln_g, ln_b=m_ln_b)
    small_v = dict(w_pool=v_w_pool, w_sgu=v_w_sgu, pool_scale=v_pool_scale, sgu_ln_g=v_sgu_ln_g,
                   sgu_ln_b=v_sgu_ln_b, b_sgu=v_b_sgu, ln_g=v_ln_g, ln_b=v_ln_b)

    wint_loc = jnp.transpose(w_in, (0, 2, 1)).astype(BF16)
    wout_loc = w_out.astype(BF16)
    small = (w_pool, pool_scale, sgu_ln_g, sgu_ln_b, w_sgu, jnp.transpose(b_sgu, (0, 2, 1)))
    (out0, y0, cdf0, proj0), (cur, y1, cdf1, proj1), gathered0, gathered1, act_all, mod = _forward(
        x[0], c, w_ada, b_ada, small, ln_g, ln_b, [wint_loc[0], wout_loc[0]], [wint_loc[1], wout_loc[1]])
    w_int, w_outf = [gathered0[0], gathered1[0]], [gathered0[1], gathered1[1]]
    acts = [(x[0], proj0, y0, cdf0), (out0, proj1, y1, cdf1)]

    shard_in, shard_out = D_IN // N_DEV, D_MODEL // N_DEV
    a, b = cur, loss_target[0]
    loss_lanes, carry, pending = None, (), []
    g_w_in_t, g_w_out = [None] * DEPTH, [None] * DEPTH
    for l in reversed(range(DEPTH)):
        dx, dproj, h, cat, dy, small_grads, dmod8, lanes, *shards = _layer_backward(
            l, a, b, *acts[l], mod, w_int[l], w_outf[l], small, ln_g, l == DEPTH - 1, f"layer_bwd_{l}",
            carry=carry, reduce=pending)
        if shards:
            g_w_in_t[l + 1], g_w_out[l + 1] = shards
        if l == DEPTH - 1:
            loss_lanes = lanes
        if l > 0:
            g_in, g_out = _grad_matmuls(dproj, h, cat, dy, f"grad_w_{l}")
            pending = [g_in.reshape(4, 2, shard_in, D_MODEL), g_out.reshape(4, 2, shard_out, D_MODEL)]
        carry = (small_grads, dmod8)
        a = b = dx

    (g_w_in_t[0], g_w_out[0], small_tot, loss_tile, g_w_ada, d_w_ada, nm_w_ada, nv_w_ada,
     g_b_ada, d_b_ada, nm_b_ada, nv_b_ada) = _grad_tail(
        dproj, h, cat, dy, small_grads.reshape(4, 2, DEPTH * PACK_ROWS // N_DEV, 128), dmod8, loss_lanes,
        w_ada, m_w_ada, v_w_ada, jnp.transpose(act_all), b_ada, m_b_ada, v_b_ada)
    loss = loss_tile[0, 0]

    flat = lambda t: t.reshape(-1, t.shape[-1])
    to_t = lambda t: flat(jnp.transpose(t, (0, 2, 1)))
    from_t = lambda t: jnp.transpose(t.reshape(DEPTH, shard_in, D_MODEL), (0, 2, 1))
    *updated, grad_x = _adamw(to_t(w_in), g_w_in_t, to_t(m_w_in), to_t(v_w_in), shard_in // 2, "adamw_w_in",
                              ride=[a])
    grad_x = grad_x[None]
    g_w_in, d_w_in, nm_w_in, nv_w_in = [from_t(t) for t in updated]
    gwout, d_w_out, nm_w_out, nv_w_out = [t.reshape(w_out.shape) for t in _adamw(
        flat(w_out), g_w_out, flat(m_w_out), flat(v_w_out), shard_out, "adamw_w_out")]
    small_out = _adamw_small(small_tot.reshape(DEPTH * PACK_ROWS, 128), [small_w[n] for n in SMALL_NAMES],
                             [small_m[n] for n in SMALL_NAMES], [small_v[n] for n in SMALL_NAMES], "adamw_small")
    gs, ds, ms, vs = [dict(zip(SMALL_NAMES, group)) for group in small_out]

    def ordered(w_ada_, b_ada_, w_in_, small, w_out_):
        return (w_ada_, b_ada_, w_in_, small["w_pool"], small["pool_scale"], small["sgu_ln_g"], small["sgu_ln_b"],
                small["w_sgu"], small["b_sgu"], w_out_, small["ln_g"], small["ln_b"])

    return (loss, grad_x,
            *ordered(g_w_ada, g_b_ada, g_w_in, gs, gwout),
            *ordered(d_w_ada, d_b_ada, d_w_in, ds, d_w_out),
            *ordered(nm_w_ada, nm_b_ada, nm_w_in, ms, nm_w_out),
            *ordered(nv_w_ada, nv_b_ada, nv_w_in, vs, nv_w_out))
```

```python
import jax
import jax.numpy as jnp
from jax import lax
from jax.experimental import pallas as pl
from jax.experimental.pallas import tpu as pltpu

F32 = jnp.float32
BF16 = jnp.bfloat16

D_MODEL = 1024
SEQ = 2048
DEPTH = 2
D_POOL = 512
D_SGU = 512
D_IN = 2560
N_GROUPS = 4
GROUP = 128
N_HEADS = 4
HEAD = 128
CHUNK = 128
WINDOWS = (2, 4, 8, 16)
ALPHA = (2.0 * DEPTH) ** 0.25
LN_EPS = 1e-5
N_DEV = 8

ADAM_LR = 0.001
ADAM_B1 = 0.9
ADAM_B2 = 0.999
ADAM_EPS = 1e-08
ADAM_WD = 0.01
ADAM_STEP = 10

TM = 256
HALO = 16
N_TILES = SEQ // TM
VMEM_LIMIT = 60 * 1024 * 1024

ROW_WPOOL = 0
ROW_WSGU = 512
ROW_PSCALE = 1024
ROW_SLNG = 1028
ROW_SLNB = 1032
ROW_BSGU = 1036
ROW_LNG = 1040
ROW_LNB = 1048
PACK_ROWS = 1088
DMOD_COLS = DEPTH * 3 * D_MODEL // 8

SQRT_HALF = 0.7071067811865476
INV_SQRT_2PI = 0.3989422804014327


def _ln(x):
    mu = jnp.mean(x, axis=-1, keepdims=True)
    xc = x - mu
    var = jnp.mean(xc * xc, axis=-1, keepdims=True)
    rstd = lax.rsqrt(var + LN_EPS)
    return xc * rstd, rstd


def _ln_bwd(dxn, xn, rstd):
    m1 = jnp.mean(dxn, axis=-1, keepdims=True)
    m2 = jnp.mean(dxn * xn, axis=-1, keepdims=True)
    return rstd * (dxn - m1 - xn * m2)


def _normal_cdf(x):
    return 0.5 * (1.0 + lax.erf(x * SQRT_HALF))


def _gelu_parts(x, cdf, with_grad):
    if not with_grad:
        return x * cdf, None
    return x * cdf, cdf + x * (INV_SQRT_2PI * jnp.exp(-0.5 * x * x))


def _silu_parts(x):
    s = jax.nn.sigmoid(x)
    return x * s, s * (1.0 + x * (1.0 - s))


def _dot(a, b):
    return lax.dot_general(a, b, (((1,), (0,)), ((), ())), preferred_element_type=F32)


def _dot_nt(a, b):
    return lax.dot_general(a, b, (((1,), (1,)), ((), ())), preferred_element_type=F32)


def _dot_tn(a, b):
    return lax.dot_general(a, b, (((0,), (0,)), ((), ())), preferred_element_type=F32)


def _row_index(tile):
    return tile * TM + lax.broadcasted_iota(jnp.int32, (TM, 1), 0)


def _window_sums(ext, forward):
    n = TM + HALO
    cur = ext
    outs = []
    for g in range(N_GROUPS):
        step = 1 << g
        cur = cur + pltpu.roll(cur, step if forward else n - step, 0)
        rows = cur[HALO:, :GROUP] if forward else cur[:TM, :GROUP]
        outs.append(rows)
        cur = cur[:, GROUP:] if g + 1 < N_GROUPS else None
    return outs


def _inverse_counts(rows):
    return [1.0 / jnp.minimum(rows + 1, w).astype(F32) for w in WINDOWS]


def _tril_bf16(w):
    t = lax.broadcasted_iota(jnp.int32, (CHUNK, CHUNK), 0)
    s = lax.broadcasted_iota(jnp.int32, (CHUNK, CHUNK), 1)
    return jnp.where(t >= s, w, 0.0).astype(BF16)


class _MixWeights:
    def __init__(self, layer, wpool_ref, pscale_ref, slng_ref, slnb_ref, wsgu_ref, bsgut_ref):
        self.layer = layer
        self.wpool_ref, self.pscale_ref, self.slng_ref, self.slnb_ref = wpool_ref, pscale_ref, slng_ref, slnb_ref
        self.wsgu_ref, self.bsgut_ref = wsgu_ref, bsgut_ref

    def pool(self, g):
        return self.wpool_ref[self.layer, g].astype(BF16)

    def pool_scale(self, g):
        return self.pscale_ref[self.layer:self.layer + 1, g * GROUP:(g + 1) * GROUP]

    def ln_gain(self, h):
        return self.slng_ref[self.layer, h:h + 1, :]

    def ln_bias(self, h):
        return self.slnb_ref[self.layer, h:h + 1, :]

    def mix(self, h):
        return _tril_bf16(self.wsgu_ref[self.layer, h])

    def mix_bias(self, h):
        return self.bsgut_ref[self.layer, :, h:h + 1]


SMALL_SPECS = ((DEPTH, N_GROUPS, GROUP, GROUP), (DEPTH, D_POOL), (DEPTH, N_HEADS, HEAD), (DEPTH, N_HEADS, HEAD),
               (DEPTH, N_HEADS, CHUNK, CHUNK), (DEPTH, CHUNK, N_HEADS))


def _mix_forward(proj, halo, tile, w, cdf=None):
    keep = cdf is not None
    rows = _row_index(tile)
    inv_counts = _inverse_counts(rows)
    xa = proj[:, 0:D_POOL]
    ga = proj[:, D_POOL:2 * D_POOL]
    sums = _window_sums(jnp.concatenate([halo, xa], axis=0), True)
    ga_act, ga_grad = _silu_parts(ga)
    pooled, pw, ya = [], [], []
    for g in range(N_GROUPS):
        sl = slice(g * GROUP, (g + 1) * GROUP)
        p = (sums[g] * inv_counts[g] - xa[:, sl]).astype(BF16)
        q = _dot(p, w.pool(g))
        pooled.append(p)
        pw.append(q)
        ya.append(q * w.pool_scale(g) * ga_act[:, sl])

    u = proj[:, 2 * D_POOL:2 * D_POOL + D_SGU]
    v = proj[:, 2 * D_POOL + D_SGU:2 * D_POOL + 2 * D_SGU]
    gb = proj[:, 2 * D_POOL + 2 * D_SGU:]
    gb_act, gb_grad = _silu_parts(gb)
    if cdf is None:
        cdf = jnp.concatenate([_normal_cdf(u), _normal_cdf(v)], axis=1)
    u_act, u_grad = _gelu_parts(u, cdf[:, :D_SGU], keep)
    v_act, v_grad = _gelu_parts(v, cdf[:, D_SGU:], keep)
    vn, vrstd, vln, mixed, yb = [], [], [], [], []
    for h in range(N_HEADS):
        sl = slice(h * HEAD, (h + 1) * HEAD)
        n_h, r_h = _ln(v_act[:, sl])
        l_h = (n_h * w.ln_gain(h) + w.ln_bias(h)).astype(BF16)
        w_h = w.mix(h)
        bias = w.mix_bias(h)
        m_h = jnp.concatenate(
            [_dot(w_h, l_h[k * CHUNK:(k + 1) * CHUNK]) + bias for k in range(TM // CHUNK)], axis=0)
        vn.append(n_h)
        vrstd.append(r_h)
        vln.append(l_h)
        mixed.append(m_h)
        yb.append(u_act[:, sl] * m_h * gb_act[:, sl])
    cat = jnp.concatenate(ya + yb, axis=1)
    if not keep:
        return cat, cdf
    return cat, dict(inv_counts=inv_counts, ga_act=ga_act, ga_grad=ga_grad, pooled=pooled, pw=pw, u_grad=u_grad,
                     v_grad=v_grad, u_act=u_act, gb_act=gb_act, gb_grad=gb_grad, vn=vn, vrstd=vrstd, vln=vln,
                     mixed=mixed)


def _const_spec(shape):
    nd = len(shape)
    return pl.BlockSpec(shape, lambda i: (0,) * nd)


VEC_LNG, VEC_LNB, VEC_POOL, VEC_SGU, VEC_SHIFT, VEC_SCALE, VEC_GATE, VEC_LOSS = range(8)


def _layer_backward(layer, a, b, x, proj, y, cdf, mod, w_int, w_outf, small, ln_g, is_last, name, carry=(),
                    reduce=()):
    n_red, n_carry = len(reduce), len(carry)
    base = layer * PACK_ROWS

    def body(a_ref, b_ref, x_ref, proj_ref, prev_ref, y_ref, cdf_ref, mod_ref, wint_ref, wout_ref, wpool_ref,
             pscale_ref, slng_ref, slnb_ref, wsgu_ref, bsgut_ref, lng_ref, *rest):
        weights = _MixWeights(layer, wpool_ref, pscale_ref, slng_ref, slnb_ref, wsgu_ref, bsgut_ref)
        carry_refs, rest = rest[:n_carry], rest[n_carry:]
        part_refs, rest = rest[:n_red], rest[n_red:]
        dx_ref, dproj_ref, h_ref, cat_ref, dy_ref, small_ref, dmod_ref, loss_ref = rest[:8]
        shard_refs, rest = rest[8:8 + n_red], rest[8 + n_red:]
        vec_ref, dmix_ref, halo_ref = rest[:3]
        step = pl.program_id(0)
        tile = N_TILES - 1 - step

        def scatter():
            bufs, sems = rest[3:3 + 5 * n_red], rest[3 + 5 * n_red:]
            arrays = [dict(part=part_refs[n], out=shard_refs[n], staged=True, stage=bufs[5 * n], sib=bufs[5 * n + 1],
                           snd=bufs[5 * n + 2], rcv=bufs[5 * n + 3], relay=bufs[5 * n + 4]) for n in range(n_red)]
            return _ChipReduceScatter(arrays, *sems)

        @pl.when(step == 0)
        def _():
            small_ref[...] = jnp.zeros_like(small_ref)
            dmod_ref[...] = jnp.zeros_like(dmod_ref)
            vec_ref[...] = jnp.zeros_like(vec_ref)
            dmix_ref[...] = jnp.zeros_like(dmix_ref)
            halo_ref[...] = jnp.zeros_like(halo_ref)
            if n_red:
                scatter().start()

        if n_red:
            @pl.when(step == 1)
            def _():
                scatter().exchange()

            @pl.when(step == N_TILES // 2)
            def _():
                scatter().fold()

        def acc(row, lo, val):
            hi = lo + val.shape[1]
            vec_ref[row:row + 1, lo:hi] += jnp.sum(val, axis=0, keepdims=True)

        xt = x_ref[...]
        yt = y_ref[...]
        shift = mod_ref[layer:layer + 1, 0:D_MODEL]
        scale = mod_ref[layer:layer + 1, D_MODEL:2 * D_MODEL]
        gate = mod_ref[layer:layer + 1, 2 * D_MODEL:]
        ln_gain = lng_ref[layer:layer + 1, :]

        zn, zrstd = _ln(ALPHA * xt + gate * yt)
        if is_last:
            diff = a_ref[...] - b_ref[...]
            acc(VEC_LOSS, 0, diff * diff)
            dout = diff * (1.0 / D_MODEL)
        else:
            dout = a_ref[...]
        acc(VEC_LNG, 0, dout * zn)
        acc(VEC_LNB, 0, dout)
        dz = _ln_bwd(dout * ln_gain, zn, zrstd)
        acc(VEC_GATE, 0, dz * yt)
        dy = (dz * gate).astype(BF16)
        dy_ref[...] = dy
        dcat = _dot_nt(dy, wout_ref[...])

        proj = proj_ref[...]
        prev = jnp.where(tile > 0, prev_ref[...], 0.0)
        cat, k = _mix_forward(proj, prev, tile, weights, cdf_ref[...])
        cat_ref[...] = cat.astype(BF16)

        dga, dq = [], []
        for g in range(N_GROUPS):
            sl = slice(g * GROUP, (g + 1) * GROUP)
            pscale = weights.pool_scale(g)
            dya = dcat[:, sl]
            dyp = dya * k["ga_act"][:, sl]
            dga.append(dya * k["pw"][g] * pscale * k["ga_grad"][:, sl])
            acc(VEC_POOL, g * GROUP, dyp * k["pw"][g])
            dpw = (dyp * pscale).astype(BF16)
            rows = pl.ds(base + ROW_WPOOL + g * GROUP, GROUP)
            small_ref[rows, :] += _dot_tn(k["pooled"][g], dpw)
            dq.append(_dot_nt(dpw, weights.pool(g)))
        dpooled = jnp.concatenate(dq, axis=1)
        scaled = jnp.concatenate([dq[g] * k["inv_counts"][g] for g in range(N_GROUPS)], axis=1)
        sums = _window_sums(jnp.concatenate([scaled, halo_ref[...]], axis=0), False)
        halo_ref[...] = scaled[0:HALO]
        dxa = jnp.concatenate(sums, axis=1) - dpooled

        du, dv, dgb = [], [], []
        for h in range(N_HEADS):
            sl = slice(h * HEAD, (h + 1) * HEAD)
            dyb = dcat[:, D_POOL + h * HEAD:D_POOL + (h + 1) * HEAD]
            m_h = k["mixed"][h]
            ug = k["u_act"][:, sl] * dyb
            du.append(dyb * m_h * k["gb_act"][:, sl] * k["u_grad"][:, sl])
            dgb.append(ug * m_h * k["gb_grad"][:, sl])
            dmixed = ug * k["gb_act"][:, sl]
            dmixed_bf = dmixed.astype(BF16)
            w_h = weights.mix(h)
            dvln_parts = []
            dmix_sum = dmix_ref[h]
            wsgu_rows = pl.ds(base + ROW_WSGU + h * CHUNK, CHUNK)
            dws = small_ref[wsgu_rows, :]
            for c in range(TM // CHUNK):
                cs = slice(c * CHUNK, (c + 1) * CHUNK)
                dmix_sum = dmix_sum + dmixed[cs]
                dws = dws + _dot_nt(dmixed_bf[cs], k["vln"][h][cs])
                dvln_parts.append(_dot_tn(w_h, dmixed_bf[cs]))
            dmix_ref[h] = dmix_sum
            small_ref[wsgu_rows, :] = dws
            dvln = jnp.concatenate(dvln_parts, axis=0)
            acc(VEC_SGU, h * HEAD, dvln * k["vn"][h])
            acc(VEC_SGU, D_SGU + h * HEAD, dvln)
            dvv = _ln_bwd(dvln * weights.ln_gain(h), k["vn"][h], k["vrstd"][h])
            dv.append(dvv * k["v_grad"][:, sl])

        dproj = jnp.concatenate([dxa] + dga + du + dv + dgb, axis=1).astype(BF16)
        dproj_ref[...] = dproj
        dh = _dot(dproj, wint_ref[...])

        xn, xrstd = _ln(xt)
        h_ref[...] = (xn * (1.0 + scale) + shift).astype(BF16)
        acc(VEC_SCALE, 0, dh * xn)
        acc(VEC_SHIFT, 0, dh)
        dx_ref[...] = _ln_bwd(dh * (1.0 + scale), xn, xrstd) + ALPHA * dz

        @pl.when(step == N_TILES - 1)
        def _():
            def put(row0, vec_row, lo, n):
                for r in range(n):
                    small_ref[base + row0 + r:base + row0 + r + 1, :] = (
                        vec_ref[vec_row:vec_row + 1, lo + r * 128:lo + (r + 1) * 128])

            put(ROW_PSCALE, VEC_POOL, 0, 4)
            put(ROW_SLNG, VEC_SGU, 0, 4)
            put(ROW_SLNB, VEC_SGU, D_SGU, 4)
            put(ROW_LNG, VEC_LNG, 0, 8)
            put(ROW_LNB, VEC_LNB, 0, 8)
            ones = jnp.ones((8, HEAD), F32)
            t = lax.broadcasted_iota(jnp.int32, (CHUNK, CHUNK), 0)
            s = lax.broadcasted_iota(jnp.int32, (CHUNK, CHUNK), 1)
            for h in range(N_HEADS):
                bias_rows = lax.dot_general(ones, dmix_ref[h], (((1,), (1,)), ((), ())),
                                            preferred_element_type=F32, precision=lax.Precision.HIGHEST)
                small_ref[base + ROW_BSGU + h:base + ROW_BSGU + h + 1, :] = bias_rows[0:1]
                rows = pl.ds(base + ROW_WSGU + h * CHUNK, CHUNK)
                small_ref[rows, :] = jnp.where(t >= s, small_ref[rows, :], 0.0)
            pieces = ((0, VEC_SHIFT, 0, 768),
                      (1, VEC_SHIFT, 768, 256), (1, VEC_SCALE, 0, 512),
                      (2, VEC_SCALE, 512, 512), (2, VEC_GATE, 0, 256),
                      (3, VEC_GATE, 256, 768))
            filled = [0] * 4
            for q, vec_row, lo, n in pieces:
                row = 4 * layer + q
                dmod_ref[row:row + 1, filled[q]:filled[q] + n] = vec_ref[vec_row:vec_row + 1, lo:lo + n]
                filled[q] += n
            if n_carry:
                for other in range(layer + 1, DEPTH):
                    rows = pl.ds(other * PACK_ROWS, PACK_ROWS)
                    small_ref[rows, :] = carry_refs[0][rows, :]
                    dmod_ref[4 * other:4 * other + 4, :] = carry_refs[1][4 * other:4 * other + 4, :]
            loss_ref[...] = vec_ref[VEC_LOSS:VEC_LOSS + 1, :]
            if n_red:
                scatter().finish()
                scatter().wait_sends()

    rev = lambda w: pl.BlockSpec((TM, w), lambda i: (N_TILES - 1 - i, 0))
    prev_spec = pl.BlockSpec(
        (HALO, D_POOL), lambda i: (jnp.maximum((N_TILES - 1 - i) * (TM // HALO) - 1, 0), 0))
    comm_scratch = []
    for p in reduce:
        comm_scratch += _ChipReduceScatter.buffers(p.shape[2], p.shape[3], p.dtype)
    if n_red:
        comm_scratch += _ChipReduceScatter.semaphores(n_red)
    return pl.pallas_call(
        body,
        name=name,
        grid=(N_TILES,),
        in_specs=[rev(D_MODEL), rev(D_MODEL) if is_last else pl.BlockSpec((TM, D_MODEL), lambda i: (0, 0)),
                  rev(D_MODEL), rev(D_IN), prev_spec, rev(D_MODEL), rev(2 * D_SGU),
                  _const_spec((DEPTH, 3 * D_MODEL)), _const_spec((D_IN, D_MODEL)), _const_spec((D_MODEL, D_MODEL))]
                 + [_const_spec(s) for s in SMALL_SPECS] + [_const_spec((DEPTH, D_MODEL))]
                 + [_const_spec(c.shape) for c in carry] + [ANY] * n_red,
        out_specs=[rev(D_MODEL), rev(D_IN), rev(D_MODEL), rev(D_MODEL), rev(D_MODEL),
                   _const_spec((DEPTH * PACK_ROWS, 128)), _const_spec((8, DMOD_COLS)), _const_spec((1, D_MODEL))]
                  + [_const_spec(p.shape[2:]) for p in reduce],
        out_shape=[jax.ShapeDtypeStruct((SEQ, D_MODEL), F32), jax.ShapeDtypeStruct((SEQ, D_IN), BF16),
                   jax.ShapeDtypeStruct((SEQ, D_MODEL), BF16), jax.ShapeDtypeStruct((SEQ, D_MODEL), BF16),
                   jax.ShapeDtypeStruct((SEQ, D_MODEL), BF16), jax.ShapeDtypeStruct((DEPTH * PACK_ROWS, 128), F32),
                   jax.ShapeDtypeStruct((8, DMOD_COLS), F32), jax.ShapeDtypeStruct((1, D_MODEL), F32)]
                  + [jax.ShapeDtypeStruct(p.shape[2:], F32) for p in reduce],
        scratch_shapes=[pltpu.VMEM((8, D_MODEL), F32), pltpu.VMEM((N_HEADS, CHUNK, HEAD), F32),
                        pltpu.VMEM((HALO, D_POOL), F32)] + comm_scratch,
        compiler_params=pltpu.CompilerParams(dimension_semantics=("arbitrary",), vmem_limit_bytes=VMEM_LIMIT),
    )(a, b, x, proj, proj, y, cdf, mod, w_int, w_outf, *small, ln_g, *carry, *reduce)


def _grad_matmuls(dproj, h, cat, dy, name):
    in_cols, out_cols = D_IN // 4, D_MODEL // 2
    in_steps = D_IN // in_cols

    def body(dproj_ref, h_ref, cat_ref, dy_ref, gin_ref, gout_ref):
        step = pl.program_id(0)

        @pl.when(step < in_steps)
        def _():
            gin_ref[...] = _dot_tn(dproj_ref[...], h_ref[...]).astype(BF16)

        @pl.when(step >= in_steps)
        def _():
            gout_ref[...] = _dot_tn(cat_ref[...], dy_ref[...]).astype(BF16)

    in_block = lambda j: jnp.minimum(j, in_steps - 1)
    out_block = lambda j: jnp.maximum(j - in_steps, 0)
    return pl.pallas_call(
        body,
        name=name,
        grid=(in_steps + D_MODEL // out_cols,),
        in_specs=[pl.BlockSpec((SEQ, in_cols), lambda j: (0, in_block(j))), _const_spec((SEQ, D_MODEL)),
                  pl.BlockSpec((SEQ, out_cols), lambda j: (0, out_block(j))), _const_spec((SEQ, D_MODEL))],
        out_specs=[pl.BlockSpec((in_cols, D_MODEL), lambda j: (in_block(j), 0)),
                   pl.BlockSpec((out_cols, D_MODEL), lambda j: (out_block(j), 0))],
        out_shape=[jax.ShapeDtypeStruct((D_IN, D_MODEL), BF16), jax.ShapeDtypeStruct((D_MODEL, D_MODEL), BF16)],
        compiler_params=pltpu.CompilerParams(dimension_semantics=("arbitrary",), vmem_limit_bytes=VMEM_LIMIT),
    )(dproj, h, cat, dy)


def _adamw_math(w, g, m, v):
    m = ADAM_B1 * m + (1.0 - ADAM_B1) * g
    v = ADAM_B2 * v + (1.0 - ADAM_B2) * (g * g)
    m_hat = m / (1.0 - ADAM_B1 ** ADAM_STEP)
    v_hat = v / (1.0 - ADAM_B2 ** ADAM_STEP)
    delta = -ADAM_LR * (m_hat / (jnp.sqrt(v_hat) + ADAM_EPS) + ADAM_WD * w)
    return delta, m, v


def _adamw(w, grads, m, v, block_rows, name, ride=()):
    rows, cols = grads[0].shape
    blocks = rows // block_rows
    n_ride = len(ride)

    def body(w_ref, m_ref, v_ref, *rest):
        g_refs, rest = rest[:DEPTH], rest[DEPTH:]
        ride_in, (g_ref, d_ref, nm_ref, nv_ref), ride_out = rest[:n_ride], rest[n_ride:n_ride + 4], rest[n_ride + 4:]
        for src, dst in zip(ride_in, ride_out):
            dst[...] = src[...]
        for layer in range(DEPTH):
            @pl.when(pl.program_id(0) == layer)
            def _():
                g = g_refs[layer][...]
                g_ref[...] = g
                d_ref[...], nm_ref[...], nv_ref[...] = _adamw_math(w_ref[...], g, m_ref[...], v_ref[...])

    def grad_spec(layer):
        return pl.BlockSpec((block_rows, cols),
                            lambda l, i: (jnp.where(l == layer, i, jnp.where(l < layer, 0, blocks - 1)), 0))

    spec = pl.BlockSpec((block_rows, cols), lambda l, i: (l * blocks + i, 0))
    ride_specs = [pl.BlockSpec((r.shape[0] // (DEPTH * blocks), r.shape[1]), lambda l, i: (l * blocks + i, 0))
                  for r in ride]
    return pl.pallas_call(
        body,
        name=name,
        grid=(DEPTH, blocks),
        in_specs=[spec] * 3 + [grad_spec(layer) for layer in range(DEPTH)] + ride_specs,
        out_specs=[spec] * 4 + ride_specs,
        out_shape=[jax.ShapeDtypeStruct(w.shape, F32)] * 4 + [jax.ShapeDtypeStruct(r.shape, r.dtype) for r in ride],
        compiler_params=pltpu.CompilerParams(dimension_semantics=("arbitrary", "arbitrary"),
                                             vmem_limit_bytes=VMEM_LIMIT),
    )(w, m, v, *grads, *ride)


MESH = pl.DeviceIdType.MESH
SIBLING = 1
ANY = pl.BlockSpec(memory_space=pl.ANY)
VMEM = pl.BlockSpec(memory_space=pltpu.VMEM)


def _me():
    return lax.axis_index("x"), lax.axis_index("y"), lax.axis_index("c")


def _peer(r):
    x, y, c = _me()
    return (1 - x if r & 4 else x, 1 - y if r & 2 else y, 1 - c if r & 1 else c)


def _index(dev):
    return 4 * dev[0] + 2 * dev[1] + dev[2]


def _remote(src, dst, send_sem, recv_sem, dev):
    return pltpu.make_async_remote_copy(src_ref=src, dst_ref=dst, send_sem=send_sem, recv_sem=recv_sem,
                                        device_id=dev, device_id_type=MESH)


ACROSS_X, ACROSS_Y, ACROSS_BOTH = 4, 2, 6
GATHER_SEMS = 11


class _TwoLevelGather:
    def __init__(self, out, send_sems, recv_sems, src=None):
        self.out, self.send_sems, self.recv_sems, self.src = out, send_sems, recv_sems, src
        self.rows = (out.shape[0] // N_DEV) if len(out.shape) == 2 else out.shape[1]
        self.half = self.rows // 2

    def _slot(self, block):
        if len(self.out.shape) == 2:
            return self.out.at[pl.ds(pl.multiple_of(_index(block) * self.rows, self.rows), self.rows)]
        return self.out.at[_index(block)]

    def _copy(self, k, block, part, to, src=None):
        slot = self._slot(block)
        if part is not None:
            rows = pl.ds(part * self.half, self.half)
            slot = slot.at[rows]
            src = None if src is None else src.at[rows]
        return _remote(slot if src is None else src, slot, self.send_sems.at[k], self.recv_sems.at[k], to)

    def _mine(self):
        me = _me()
        src = self._slot(me) if self.src is None else self.src
        x, y = _peer(ACROSS_X), _peer(ACROSS_Y)
        return [self._copy(1, me, 0, x, src), self._copy(3, me, 1, y, src), self._copy(0, me, None, _peer(SIBLING), src),
                self._copy(2, me, 1, x, src), self._copy(4, me, 0, y, src)]

    def _relayed(self):
        return [self._copy(5, _peer(ACROSS_X), 0, _peer(ACROSS_Y)), self._copy(6, _peer(ACROSS_Y), 1, _peer(ACROSS_X))]

    def _passed(self):
        sib, far = _peer(SIBLING), _peer(ACROSS_BOTH)
        return [self._copy(7, _peer(ACROSS_X), None, sib), self._copy(8, _peer(ACROSS_Y), None, sib),
                self._copy(9, far, 0, sib), self._copy(10, far, 1, sib)]

    def _arrival(self, k, r, part):
        return self._copy(k, _peer(r), part, _me())

    def send_first(self):
        for cp in self._mine()[:3]:
            cp.start()

    def send_second(self):
        for cp in self._mine()[3:]:
            cp.start()

    def send_mine(self):
        self.send_first()
        self.send_second()

    def relay(self):
        relayed = self._relayed()
        self._arrival(1, ACROSS_X, 0).wait_recv()
        relayed[0].start()
        self._arrival(3, ACROSS_Y, 1).wait_recv()
        relayed[1].start()

    def pass_near(self):
        passed = self._passed()
        self._arrival(2, ACROSS_X, 1).wait_recv()
        passed[0].start()
        self._arrival(4, ACROSS_Y, 0).wait_recv()
        passed[1].start()

    def pass_far(self):
        passed = self._passed()
        self._arrival(5, ACROSS_BOTH, 0).wait_recv()
        passed[2].start()
        self._arrival(6, ACROSS_BOTH, 1).wait_recv()
        passed[3].start()

    def pass_on(self):
        self.pass_near()
        self.pass_far()

    def wait_sibling(self):
        self._arrival(0, SIBLING, None).wait_recv()

    def wait_passed(self, r):
        if r == ACROSS_BOTH:
            self._arrival(9, r ^ SIBLING, 0).wait_recv()
            self._arrival(10, r ^ SIBLING, 1).wait_recv()
        else:
            self._arrival(7 if r == ACROSS_X else 8, r ^ SIBLING, None).wait_recv()

    def wait_rest(self):
        self.wait_sibling()
        for r in (ACROSS_X, ACROSS_Y, ACROSS_BOTH):
            self.wait_passed(r)

    def wait_sends(self):
        for cp in self._mine() + self._relayed() + self._passed():
            cp.wait_send()


class _ChipReduceScatter:
    SLOTS = 6

    def __init__(self, arrays, l_sem, d_send, d_recv, i_send, i_recv):
        self.arrays = arrays
        self.l_sem, self.d_send, self.d_recv, self.i_send, self.i_recv = l_sem, d_send, d_recv, i_send, i_recv

    @staticmethod
    def buffers(rows, cols, dtype, staged=True):
        stage = [pltpu.VMEM((4, rows, cols), dtype)] if staged else []
        return stage + [pltpu.VMEM((4, rows, cols), dtype), pltpu.VMEM((3, rows, cols), dtype),
                        pltpu.VMEM((2, rows, cols), dtype), pltpu.VMEM((2, rows // 2, cols), dtype)]

    @classmethod
    def semaphores(cls, n):
        return [pltpu.SemaphoreType.DMA((n,)), pltpu.SemaphoreType.DMA((n, 4)), pltpu.SemaphoreType.DMA((n, 4)),
                pltpu.SemaphoreType.DMA((n, cls.SLOTS)), pltpu.SemaphoreType.DMA((n, cls.SLOTS))]

    def _pick(self, which):
        return list(enumerate(self.arrays)) if which is None else [(n, self.arrays[n]) for n in which]

    @staticmethod
    def _chip(r):
        dev = _me() if r is None else _peer(r)
        return 2 * dev[0] + dev[1]

    def _staging(self, which):
        c = _me()[2]
        return [pltpu.make_async_copy(a["part"].at[pl.ds(0, 4), c], a["stage"], self.l_sem.at[n])
                for n, a in self._pick(which) if a["staged"]]

    def _first(self, which, chip):
        other = 1 - _me()[2]
        return [_remote(a["part"].at[chip, other], a["sib"].at[chip], self.d_send.at[n, chip], self.d_recv.at[n, chip],
                        _peer(SIBLING)) for n, a in self._pick(which)]

    @staticmethod
    def _halves(a):
        half = a["rcv"].shape[1] // 2
        return pl.ds(0, half), pl.ds(half, half)

    def _hops(self, n, a):
        h0, h1 = self._halves(a)
        x, y = _peer(ACROSS_X), _peer(ACROSS_Y)
        snd, rcv, relay = a["snd"], a["rcv"], a["relay"]
        pairs = [(snd.at[2, h0], relay.at[0], x), (snd.at[2, h1], relay.at[1], y),
                 (snd.at[0, h0], rcv.at[0, h0], x), (snd.at[0, h1], rcv.at[0, h1], x),
                 (snd.at[1, h1], rcv.at[1, h1], y), (snd.at[1, h0], rcv.at[1, h0], y)]
        return [_remote(s, d, self.i_send.at[n, k], self.i_recv.at[n, k], to) for k, (s, d, to) in enumerate(pairs)]

    def _mine(self, a, chip, rows=None):
        src = a["stage"].at[chip] if a["staged"] else a["part"].at[chip, _me()[2]]
        mine, sib = (src[...], a["sib"][chip]) if rows is None else (src[rows, :], a["sib"][chip, rows, :])
        return mine.astype(F32) + sib.astype(F32)

    def start(self, which=None, chips=None):
        if chips is None:
            for cp in self._staging(which):
                cp.start()
        for chip in range(4) if chips is None else chips:
            for cp in self._first(which, chip):
                cp.start()

    def send_far(self, which=None):
        far = self._chip(ACROSS_BOTH)
        for cp in self._staging(which):
            cp.wait()
        for cp in self._first(which, far):
            cp.wait_recv()
        for n, a in self._pick(which):
            hops = self._hops(n, a)
            a["snd"][2] = self._mine(a, far).astype(a["snd"].dtype)
            hops[0].start()
            hops[1].start()

    def send_near(self, r, which=None):
        chip = self._chip(r)
        for cp in self._first(which, chip):
            cp.wait_recv()
        for n, a in self._pick(which):
            h0, h1 = self._halves(a)
            hops = self._hops(n, a)
            if r == ACROSS_X:
                a["snd"][0, h0, :] = self._mine(a, chip, h0).astype(a["snd"].dtype)
                hops[2].start()
            else:
                a["snd"][1, h1, :] = self._mine(a, chip, h1).astype(a["snd"].dtype)
                hops[4].start()

    def exchange(self, which=None):
        self.send_far(which)
        self.send_near(ACROSS_X, which)
        self.send_near(ACROSS_Y, which)

    def fold(self, which=None):
        across_x, across_y = self._chip(ACROSS_X), self._chip(ACROSS_Y)
        for n, a in self._pick(which):
            h0, h1 = self._halves(a)
            hops = self._hops(n, a)
            dtype = a["snd"].dtype
            hops[1].wait_recv()
            a["snd"][0, h1, :] = (self._mine(a, across_x, h1) + a["relay"][1].astype(F32)).astype(dtype)
            hops[3].start()
            hops[0].wait_recv()
            a["snd"][1, h0, :] = (self._mine(a, across_y, h0) + a["relay"][0].astype(F32)).astype(dtype)
            hops[5].start()

    def finish(self, which=None):
        home = self._chip(None)
        for cp in self._first(which, home):
            cp.wait_recv()
        for n, a in self._pick(which):
            hops = self._hops(n, a)
            a["out"][...] = self._mine(a, home)
            hops[2].wait_recv()
            hops[3].wait_recv()
            a["out"][...] += a["rcv"][0].astype(F32)
            hops[4].wait_recv()
            hops[5].wait_recv()
            a["out"][...] += a["rcv"][1].astype(F32)

    def wait_sends(self, which=None):
        for chip in range(4):
            for cp in self._first(which, chip):
                cp.wait_send()
        for n, a in self._pick(which):
            for cp in self._hops(n, a):
                cp.wait_send()


def _direct_exchange(src_of, dst_of, send_sems, recv_sems):
    me = _me()
    copies = [_remote(src_of(_peer(r)), dst_of(me), send_sems.at[r - 1], recv_sems.at[r - 1], _peer(r))
              for r in range(1, N_DEV)]
    for cp in copies:
        cp.start()
    return copies


def _wait_direct(copies):
    for cp in copies:
        cp.wait_recv()
    for cp in copies:
        cp.wait_send()


def _forward(x, c, w_ada, b_ada, small, ln_g, ln_b, wint_mine, wout_mine):
    assert DEPTH == 2
    cols = w_ada.shape[2]
    shard = wint_mine.shape[1]
    pair = 2 * shard

    def body(x_hbm, c_ref, wada_hbm, bada_ref, wpool_ref, pscale_ref, slng_ref, slnb_ref, wsgu_ref, bsgut_ref,
             lng_ref, lnb_ref, wint_layers, wout_layers,
             out0_ref, y0_ref, cdf0_ref, proj0_hbm, out1_ref, y1_ref, cdf1_ref, proj1_hbm,
             wint_keep, wout_keep, wint_next, wout_next, acts_ref, mod_ref,
             wint_v, wout_v, h_buf, proj_blk, proj_tile, halo_ref, x_ref, w_send, w_recv, w_local, p_sems,
             t_sems, in_sems, act_all, act_src, part, mod_recv, wada_ref, a_send, a_recv, m_send, m_recv,
             n_send, n_recv, n_local, f_sems):
        step = pl.program_id(0)
        chip_of = lambda dev: 2 * dev[0] + dev[1]
        wint_hbm, next_in_hbm = wint_layers.at[0], wint_layers.at[1]
        wout_hbm, next_out_hbm = wout_layers.at[0], wout_layers.at[1]

        def hosted():
            return [_TwoLevelGather(out, n_send.at[n], n_recv.at[n], src=src)
                    for n, (out, src) in enumerate(((wint_next, next_in_hbm), (wout_next, next_out_hbm)))]

        def hosted_own():
            me = _me()
            return [pltpu.make_async_copy(g.src, g._slot(me), n_local.at[n]) for n, g in enumerate(hosted())]

        def gathers():
            return (_TwoLevelGather(wint_v, w_send.at[0], w_recv.at[0], src=wint_hbm),
                    _TwoLevelGather(wout_v, w_send.at[1], w_recv.at[1], src=wout_hbm))

        def keeps():
            return [pltpu.make_async_copy(wint_v, wint_keep, w_local.at[2]),
                    pltpu.make_async_copy(wout_v, wout_keep, w_local.at[3])]

        def tile_read(t):
            slot = t % 2
            return pltpu.make_async_copy(proj0_hbm.at[pl.ds(pl.multiple_of(t * TM, TM), TM)], proj_tile.at[slot],
                                         t_sems.at[slot])

        def first_layer_start():
            writes = []

            def project(n, dev):
                first = pl.multiple_of(chip_of(dev) * pair, pair)
                if n >= 2:
                    writes[n - 2].wait()

                @pl.loop(0, N_TILES)
                def _(t):
                    rows = pl.ds(pl.multiple_of(t * TM, TM), TM)
                    proj_blk[n % 2, rows, :] = _dot_nt(h_buf[rows, :], wint_v[pl.ds(first, pair), :])

                cp = pltpu.make_async_copy(proj_blk.at[n % 2], proj0_hbm.at[:, pl.ds(first, pair)], p_sems.at[n % 2])
                cp.start()
                writes.append(cp)

            me = _me()
            halo_ref[...] = jnp.zeros_like(halo_ref)
            gather_in, gather_out = gathers()
            own_in = pltpu.make_async_copy(wint_hbm, gather_in._slot(me), w_local.at[0])
            own_out = pltpu.make_async_copy(wout_hbm, gather_out._slot(me), w_local.at[1])
            x_load = pltpu.make_async_copy(x_hbm, x_ref, in_sems.at[0])
            x_load.start()
            wada_load = pltpu.make_async_copy(wada_hbm, wada_ref, in_sems.at[1])
            wada_load.start()
            mine_index = _index(me)
            cval = c_ref[...]
            act_src[...] = jnp.zeros_like(act_src)
            act_src[0:1, :] = cval * jax.nn.sigmoid(cval)
            act_all[mine_index] = act_src[...]
            act_copies = _direct_exchange(lambda p: act_src, lambda m: act_all.at[_index(m)], a_send, a_recv)

            own_in.start()
            own_out.start()
            gather_in.send_first()

            _wait_direct(act_copies)
            acts = jnp.concatenate([act_all[j, 0:1, :] for j in range(N_DEV)], axis=0)
            acts_ref[...] = acts
            part[...] = jnp.zeros_like(part)
            wada_load.wait()
            for l in range(DEPTH):
                res = lax.dot_general(acts, wada_ref[l], (((1,), (0,)), ((), ())), preferred_element_type=F32,
                                      precision=lax.Precision.HIGHEST)
                for b in range(N_DEV):
                    part[b, l:l + 1, :] = res[b:b + 1, :]
            mod_recv[mine_index] = part[mine_index]
            mod_copies = _direct_exchange(lambda p: part.at[_index(p)], lambda m: mod_recv.at[_index(m)],
                                          m_send, m_recv)
            gather_in.send_second()
            gather_out.send_mine()

            _wait_direct(mod_copies)
            for l in range(DEPTH):
                for j in range(N_DEV):
                    sl = slice(j * cols, (j + 1) * cols)
                    mod_ref[l:l + 1, sl] = mod_recv[j, l:l + 1, :] + bada_ref[l:l + 1, sl]
            x_load.wait()
            shift = mod_ref[0:1, 0:D_MODEL]
            scale = mod_ref[0:1, D_MODEL:2 * D_MODEL]

            @pl.loop(0, N_TILES)
            def _(t):
                rows = pl.ds(pl.multiple_of(t * TM, TM), TM)
                xn, _ = _ln(x_ref[rows, :])
                h_buf[rows, :] = (xn * (1.0 + scale) + shift).astype(BF16)

            gather_in.relay()
            own_in.wait()
            gather_in.wait_sibling()
            project(0, me)
            gather_in.pass_near()
            gather_in.wait_passed(ACROSS_X)
            project(1, _peer(ACROSS_X))
            gather_out.relay()
            for cp in hosted_own():
                cp.start()
            for g in hosted():
                g.send_mine()
            gather_in.wait_passed(ACROSS_Y)
            project(2, _peer(ACROSS_Y))
            gather_in.pass_far()
            gather_in.wait_passed(ACROSS_BOTH)
            project(3, _peer(ACROSS_BOTH))

            gather_out.pass_on()
            gather_out.wait_rest()
            own_out.wait()
            for cp in keeps():
                cp.start()
            writes[2].wait()
            writes[3].wait()
            tile_read(0).start()

        def fetches():
            return [pltpu.make_async_copy(wint_next, wint_v, f_sems.at[0]),
                    pltpu.make_async_copy(wout_next, wout_v, f_sems.at[1])]

        def tile_write(t):
            slot = t % 2
            return pltpu.make_async_copy(proj_tile.at[slot], proj1_hbm.at[pl.ds(pl.multiple_of(t * TM, TM), TM)],
                                         t_sems.at[slot])

        def mix_and_close(layer, tile, rows, proj, out_ref, y_ref, cdf_ref):
            weights = _MixWeights(layer, wpool_ref, pscale_ref, slng_ref, slnb_ref, wsgu_ref, bsgut_ref)
            xt = x_ref[rows, :]
            gate = mod_ref[layer:layer + 1, 2 * D_MODEL:]
            cat, cdf_ref[...] = _mix_forward(proj, halo_ref[...], tile, weights)
            halo_ref[...] = proj[TM - HALO:, 0:D_POOL]
            if layer == 1:
                @pl.when(tile == 0)
                def _():
                    fetches()[1].wait()
            y = _dot(cat.astype(BF16), wout_v[...])
            y_ref[...] = y
            zn, _ = _ln(ALPHA * xt + gate * y)
            out = zn * lng_ref[layer:layer + 1, :] + lnb_ref[layer:layer + 1, :]
            out_ref[...] = out
            if layer == 0:
                x_ref[rows, :] = out

        def first_layer_tile(tile):
            @pl.when(tile + 1 < N_TILES)
            def _():
                tile_read(tile + 1).start()

            tile_read(tile).wait()
            rows = pl.ds(pl.multiple_of(tile * TM, TM), TM)
            mix_and_close(0, tile, rows, proj_tile[tile % 2], out0_ref, y0_ref, cdf0_ref)

        def second_layer_tile(tile):
            rows = pl.ds(pl.multiple_of(tile * TM, TM), TM)
            shift = mod_ref[1:2, 0:D_MODEL]
            scale = mod_ref[1:2, D_MODEL:2 * D_MODEL]
            xn, _ = _ln(x_ref[rows, :])
            h = (xn * (1.0 + scale) + shift).astype(BF16)

            @pl.when(tile >= 2)
            def _():
                tile_write(tile - 2).wait()

            proj_tile[tile % 2] = _dot_nt(h, wint_v[...])
            tile_write(tile).start()
            mix_and_close(1, tile, rows, proj_tile[tile % 2], out1_ref, y1_ref, cdf1_ref)

        @pl.when(step < N_TILES)
        def _():
            @pl.when(step == 0)
            def _():
                first_layer_start()

            @pl.when(step == 1)
            def _():
                for g in hosted():
                    g.relay()

            @pl.when(step == N_TILES // 2)
            def _():
                for g in hosted():
                    g.pass_near()

            @pl.when(step == N_TILES - 1)
            def _():
                for g in hosted():
                    g.pass_far()
                for g in gathers():
                    g.wait_sends()
                for cp in keeps() + hosted_own():
                    cp.wait()
                hosted()[0].wait_rest()
                fetches()[0].start()

            first_layer_tile(step)

        @pl.when(step >= N_TILES)
        def _():
            @pl.when(step == N_TILES)
            def _():
                halo_ref[...] = jnp.zeros_like(halo_ref)
                hosted()[1].wait_rest()
                fetches()[1].start()
                fetches()[0].wait()

            second_layer_tile(step - N_TILES)

            @pl.when(step == 2 * N_TILES - 1)
            def _():
                for g in hosted():
                    g.wait_sends()
                tile_write(N_TILES - 2).wait()
                tile_write(N_TILES - 1).wait()

    first = lambda w: pl.BlockSpec((TM, w), lambda i: (jnp.minimum(i, N_TILES - 1), 0))
    second = lambda w: pl.BlockSpec((TM, w), lambda i: (jnp.maximum(i - N_TILES, 0), 0))
    gather_sems = pltpu.SemaphoreType.DMA((2, GATHER_SEMS))
    seven = pltpu.SemaphoreType.DMA((7,))
    per_layer = [jax.ShapeDtypeStruct((SEQ, D_MODEL), F32), jax.ShapeDtypeStruct((SEQ, D_MODEL), F32),
                 jax.ShapeDtypeStruct((SEQ, 2 * D_SGU), F32), jax.ShapeDtypeStruct((SEQ, D_IN), F32)]
    gathered = [jax.ShapeDtypeStruct((D_IN, D_MODEL), BF16), jax.ShapeDtypeStruct((D_MODEL, D_MODEL), BF16)]
    res = pl.pallas_call(
        body,
        name="layers_fwd",
        grid=(DEPTH * N_TILES,),
        in_specs=[ANY, _const_spec(c.shape), ANY, _const_spec(b_ada.shape)] + [_const_spec(s) for s in SMALL_SPECS]
                 + [_const_spec((DEPTH, D_MODEL)), _const_spec((DEPTH, D_MODEL))] + [ANY] * 2,
        out_specs=[first(D_MODEL), first(D_MODEL), first(2 * D_SGU), ANY,
                   second(D_MODEL), second(D_MODEL), second(2 * D_SGU), ANY] + [ANY] * 4
                  + [_const_spec((N_DEV, D_MODEL)), _const_spec((DEPTH, 3 * D_MODEL))],
        out_shape=per_layer * 2 + gathered * 2 + [jax.ShapeDtypeStruct((N_DEV, D_MODEL), F32),
                                                  jax.ShapeDtypeStruct((DEPTH, 3 * D_MODEL), F32)],
        scratch_shapes=[pltpu.VMEM((D_IN, D_MODEL), BF16), pltpu.VMEM((D_MODEL, D_MODEL), BF16),
                        pltpu.VMEM((SEQ, D_MODEL), BF16), pltpu.VMEM((2, SEQ, pair), F32),
                        pltpu.VMEM((2, TM, D_IN), F32), pltpu.VMEM((HALO, D_POOL), F32),
                        pltpu.VMEM((SEQ, D_MODEL), F32),
                        gather_sems, gather_sems, pltpu.SemaphoreType.DMA((4,)), pltpu.SemaphoreType.DMA((2,)),
                        pltpu.SemaphoreType.DMA((2,)), pltpu.SemaphoreType.DMA((2,)),
                        pltpu.VMEM((N_DEV, 8, D_MODEL), F32), pltpu.VMEM((8, D_MODEL), F32),
                        pltpu.VMEM((N_DEV, 8, cols), F32), pltpu.VMEM((N_DEV, 8, cols), F32),
                        pltpu.VMEM(w_ada.shape, F32), seven, seven, seven, seven,
                        gather_sems, gather_sems, pltpu.SemaphoreType.DMA((2,)), pltpu.SemaphoreType.DMA((2,))],
        compiler_params=pltpu.CompilerParams(dimension_semantics=("arbitrary",), vmem_limit_bytes=VMEM_LIMIT),
    )(x, c, w_ada, b_ada, *small, ln_g, ln_b, wint_mine, wout_mine)
    return res[0:4], res[4:8], res[8:10], res[10:12], res[12], res[13]


ADA_CHUNK = 256


def _grad_tail(dproj, h, cat, dy, small, dmod8, loss_lanes, w_ada, m_ada, v_ada, act_t, b_ada, m_bada, v_bada):
    shard_in, shard_out, shard_small = D_IN // N_DEV, D_MODEL // N_DEV, small.shape[2]
    cols = w_ada.shape[2]
    W_IN, W_OUT, SMALL = 0, 1, 2

    def body(dproj_hbm, h_hbm, cat_hbm, dy_hbm, small_hbm, dmod_ref, lanes_ref, wada_hbm, mada_hbm, vada_hbm,
             act_ref, bada_ref, mbada_ref, vbada_ref,
             gwin_ref, gwout_ref, stot_ref, loss_ref, gada_hbm, dada_hbm, nmada_hbm, nvada_hbm,
             gb_ref, db_ref, nmb_ref, nvb_ref,
             dproj_v, h_v, cat_v, dy_v, part_in, part_out, own_small, loss_src, loss_all, dmod_all, ada_in, ada_out,
             *rest):
        bufs, rest = rest[:13], rest[13:]
        load_sems, rs_sems = rest[0], rest[1:6]
        m_send, m_recv, g_send, g_recv, s_send, s_recv, ada_lsem, ada_ssem = rest[6:]
        mine = _index(_me())

        def update_ada():
            upper = (mine % 2) == 1

            def dmod_of(layer):
                rows = []
                for b in range(N_DEV):
                    r = dmod_all[b, pl.ds(4 * layer + mine // 2, 1), :]
                    rows.append(jnp.where(upper, r[:, cols:], r[:, :cols]))
                return jnp.concatenate(rows, axis=0)

            chunks = [(layer, c) for layer in range(DEPTH) for c in range(D_MODEL // ADA_CHUNK)]

            def loads(i):
                layer, c = chunks[i]
                rows = pl.ds(c * ADA_CHUNK, ADA_CHUNK)
                return [pltpu.make_async_copy(src.at[layer, rows], ada_in.at[i % 2, k], ada_lsem.at[i % 2, k])
                        for k, src in enumerate((wada_hbm, mada_hbm, vada_hbm))]

            def stores(i):
                layer, c = chunks[i]
                rows = pl.ds(c * ADA_CHUNK, ADA_CHUNK)
                return [pltpu.make_async_copy(ada_out.at[i % 2, k], dst.at[layer, rows], ada_ssem.at[i % 2, k])
                        for k, dst in enumerate((gada_hbm, dada_hbm, nmada_hbm, nvada_hbm))]

            for cp in loads(0):
                cp.start()
            dmods = {}
            for i, (layer, c) in enumerate(chunks):
                if i + 1 < len(chunks):
                    for cp in loads(i + 1):
                        cp.start()
                for cp in loads(i):
                    cp.wait()
                if i >= 2:
                    for cp in stores(i - 2):
                        cp.wait()
                if layer not in dmods:
                    dmods[layer] = dmod_of(layer)
                act = act_ref[pl.ds(c * ADA_CHUNK, ADA_CHUNK), :]
                g = act[:, 0:1] * dmods[layer][0:1, :]
                for b in range(1, N_DEV):
                    g = g + act[:, b:b + 1] * dmods[layer][b:b + 1, :]
                slot = i % 2
                delta, new_m, new_v = _adamw_math(ada_in[slot, 0], g, ada_in[slot, 1], ada_in[slot, 2])
                ada_out[slot, 0] = g
                ada_out[slot, 1] = delta
                ada_out[slot, 2] = new_m
                ada_out[slot, 3] = new_v
                for cp in stores(i):
                    cp.start()
            for i in (len(chunks) - 2, len(chunks) - 1):
                for cp in stores(i):
                    cp.wait()

            total = dmod_all[0]
            for b in range(1, N_DEV):
                total = total + dmod_all[b]
            width = total.shape[1]
            for layer in range(DEPTH):
                for q in range(4):
                    gb_ref[layer:layer + 1, q * width:(q + 1) * width] = total[4 * layer + q:4 * layer + q + 1, :]
            db_ref[...], nmb_ref[...], nvb_ref[...] = _adamw_math(bada_ref[...], gb_ref[...], mbada_ref[...],
                                                                  vbada_ref[...])

        order = (ACROSS_BOTH, ACROSS_X, ACROSS_Y, None)
        chips = [_ChipReduceScatter._chip(r) for r in order]
        loads = [pltpu.make_async_copy(s, d, load_sems.at[n]) for n, (s, d) in enumerate(
            ((cat_hbm, cat_v), (dy_hbm, dy_v), (h_hbm, h_v)))]
        loads += [pltpu.make_async_copy(dproj_hbm.at[:, pl.ds(pl.multiple_of(chip * 2 * shard_in, 2 * shard_in),
                                                             2 * shard_in)], dproj_v.at[n], load_sems.at[3 + n])
                  for n, chip in enumerate(chips)]
        for cp in loads[2:] + loads[:2]:
            cp.start()
        arrays = [dict(part=part_in, out=gwin_ref, staged=False, sib=bufs[0], snd=bufs[1], rcv=bufs[2], relay=bufs[3]),
                  dict(part=part_out, out=gwout_ref, staged=False, sib=bufs[4], snd=bufs[5], rcv=bufs[6],
                       relay=bufs[7]),
                  dict(part=small_hbm, out=own_small, staged=True, stage=bufs[8], sib=bufs[9], snd=bufs[10],
                       rcv=bufs[11], relay=bufs[12])]
        scatter = _ChipReduceScatter(arrays, *rs_sems)
        scatter.start([SMALL])
        dmod_all[mine] = dmod_ref[...]
        dmod_copies = _direct_exchange(lambda p: dmod_ref, lambda m: dmod_all.at[_index(m)], m_send, m_recv)
        loss_src[...] = jnp.full(loss_src.shape, (0.5 / D_MODEL) * jnp.sum(lanes_ref[...]), F32)
        loss_all[mine] = loss_src[...]
        loss_copies = _direct_exchange(lambda p: loss_src, lambda m: loss_all.at[_index(m)], s_send, s_recv)

        gather = _TwoLevelGather(stot_ref, g_send, g_recv)
        loads[2].wait()
        for n, chip in enumerate(chips):
            loads[3 + n].wait()
            res = _dot_tn(dproj_v[n], h_v[...]).astype(BF16)
            part_in[chip, 0] = res[:shard_in]
            part_in[chip, 1] = res[shard_in:]
            scatter.start([W_IN], chips=[chip])
            if n == 0:
                scatter.exchange([SMALL])
            if n == 1:
                scatter.send_far([W_IN])
                scatter.fold([SMALL])
            if n == 2:
                scatter.send_near(ACROSS_X, [W_IN])
                scatter.finish([SMALL])
                stot_ref[mine] = own_small[...]
                gather.send_mine()
            if n == 3:
                scatter.send_near(ACROSS_Y, [W_IN])

        loads[0].wait()
        loads[1].wait()
        for blk in range(2):
            res = _dot_tn(cat_v[:, blk * 512:(blk + 1) * 512], dy_v[...]).astype(BF16)
            for s in range(4):
                part_out[2 * blk + s // 2, s % 2] = res[s * shard_out:(s + 1) * shard_out]
        scatter.start([W_OUT])
        scatter.fold([W_IN])
        scatter.exchange([W_OUT])
        gather.relay()
        _wait_direct(dmod_copies)
        update_ada()
        scatter.fold([W_OUT])
        gather.pass_on()
        gather.wait_rest()
        _wait_direct(loss_copies)
        total = loss_all[0]
        for j in range(1, N_DEV):
            total = total + loss_all[j]
        loss_ref[...] = total
        scatter.finish([W_IN])
        scatter.finish([W_OUT])
        gather.wait_sends()
        scatter.wait_sends()

    buffers = _ChipReduceScatter.buffers
    comm_scratch = (buffers(shard_in, D_MODEL, BF16, staged=False) + buffers(shard_out, D_MODEL, BF16, staged=False)
                    + buffers(shard_small, 128, F32))
    comm_scratch += [pltpu.SemaphoreType.DMA((7,))] + _ChipReduceScatter.semaphores(3)
    comm_scratch += [pltpu.SemaphoreType.DMA((n,)) for n in (7, 7, GATHER_SEMS, GATHER_SEMS, 7, 7)]
    comm_scratch += [pltpu.SemaphoreType.DMA((2, 3)), pltpu.SemaphoreType.DMA((2, 4))]
    return pl.pallas_call(
        body,
        name="grad_tail",
        in_specs=[ANY] * 5 + [VMEM, VMEM] + [ANY] * 3 + [VMEM] * 4,
        out_specs=[VMEM] * 4 + [ANY] * 4 + [VMEM] * 4,
        out_shape=[jax.ShapeDtypeStruct((shard_in, D_MODEL), F32), jax.ShapeDtypeStruct((shard_out, D_MODEL), F32),
                   jax.ShapeDtypeStruct((N_DEV, shard_small, 128), F32), jax.ShapeDtypeStruct((8, 128), F32)]
                  + [jax.ShapeDtypeStruct(w_ada.shape, F32)] * 4 + [jax.ShapeDtypeStruct(b_ada.shape, F32)] * 4,
        scratch_shapes=[pltpu.VMEM((4, SEQ, 2 * shard_in), BF16), pltpu.VMEM(h.shape, BF16), pltpu.VMEM(cat.shape, BF16),
                        pltpu.VMEM(dy.shape, BF16), pltpu.VMEM((4, 2, shard_in, D_MODEL), BF16),
                        pltpu.VMEM((4, 2, shard_out, D_MODEL), BF16), pltpu.VMEM((shard_small, 128), F32),
                        pltpu.VMEM((8, 128), F32), pltpu.VMEM((N_DEV, 8, 128), F32),
                        pltpu.VMEM((N_DEV,) + dmod8.shape, F32), pltpu.VMEM((2, 3, ADA_CHUNK, cols), F32),
                        pltpu.VMEM((2, 4, ADA_CHUNK, cols), F32)] + comm_scratch,
        compiler_params=pltpu.CompilerParams(vmem_limit_bytes=VMEM_LIMIT),
    )(dproj, h, cat, dy, small, dmod8, loss_lanes, w_ada, m_ada, v_ada, act_t, b_ada, m_bada, v_bada)


SMALL_NAMES = ("w_pool", "w_sgu", "pool_scale", "sgu_ln_g", "sgu_ln_b", "b_sgu", "ln_g", "ln_b")
SMALL_ROWS = (512, 512, 4, 4, 4, 4, 8, 8)


def _adamw_small(g_packed, ws, ms, vs, name):
    n = len(SMALL_NAMES)

    def body(g_ref, *refs):
        w_refs, m_refs, v_refs = refs[:n], refs[n:2 * n], refs[2 * n:3 * n]
        outs = refs[3 * n:]

        def update(p, at, g):
            delta, new_m, new_v = _adamw_math(w_refs[p][at], g, m_refs[p][at], v_refs[p][at])
            outs[p][at] = g
            outs[n + p][at] = delta
            outs[2 * n + p][at] = new_m
            outs[3 * n + p][at] = new_v

        row = 0
        for p, r in enumerate(SMALL_ROWS):
            shape = ws[p].shape
            for layer in range(DEPTH):
                first = layer * PACK_ROWS + row
                if len(shape) == 4:
                    for k in range(shape[1]):
                        update(p, (layer, k), g_ref[first + k * shape[2]:first + (k + 1) * shape[2], :])
                elif len(shape) == 3:
                    update(p, (layer,), g_ref[first:first + r, :])
                else:
                    g = jnp.concatenate([g_ref[first + k:first + k + 1, :] for k in range(r)], axis=1)
                    update(p, (slice(layer, layer + 1), slice(None)), g)
            row += r

    res = pl.pallas_call(
        body,
        name=name,
        out_shape=[jax.ShapeDtypeStruct(w.shape, F32) for w in ws] * 4,
        compiler_params=pltpu.CompilerParams(vmem_limit_bytes=VMEM_LIMIT),
    )(g_packed, *ws, *ms, *vs)
    return res[:n], res[n:2 * n], res[2 * n:3 * n], res[3 * n:]


def kernel(x, c, w_ada, b_ada, w_in, w_pool, pool_scale, sgu_ln_g, sgu_ln_b, w_sgu, b_sgu, w_out, ln_g, ln_b, loss_target, m_w_ada, m_b_ada, m_w_in, m_w_pool, m_pool_scale, m_sgu_ln_g, m_sgu_ln_b, m_w_sgu, m_b_sgu, m_w_out, m_ln_g, m_ln_b, v_w_ada, v_b_ada, v_w_in, v_w_pool, v_pool_scale, v_sgu_ln_g, v_sgu_ln_b, v_w_sgu, v_b_sgu, v_w_out, v_ln_g, v_ln_b):
    small_w = dict(w_pool=w_pool, w_sgu=w_sgu, pool_scale=pool_scale, sgu_ln_g=sgu_ln_g, sgu_ln_b=sgu_ln_b,
                   b_sgu=b_sgu, ln_g=ln_g, ln_b=ln_b)
    small_m = dict(w_pool=m_w_pool, w_sgu=m_w_sgu, pool_scale=m_pool_scale, sgu_ln_g=m_sgu_ln_g,
                   sgu_ln_b=m_sgu_ln_b, b_sgu=m_b_sgu, ln_g=m_ln_g, ln_b=m_ln_b)
    small_v = dict(w_pool=v_w_pool, w_sgu=v_w_sgu, pool_scale=v_pool_scale, sgu_ln_g=v_sgu_ln_g,
                   sgu_ln_b=v_sgu_ln_b, b_sgu=v_b_sgu, ln_g=v_ln_g, ln_b=v_ln_b)

    wint_loc = jnp.transpose(w_in, (0, 2, 1)).astype(BF16)
    wout_loc = w_out.astype(BF16)
    small = (w_pool, pool_scale, sgu_ln_g, sgu_ln_b, w_sgu, jnp.transpose(b_sgu, (0, 2, 1)))
    (out0, y0, cdf0, proj0), (cur, y1, cdf1, proj1), gathered0, gathered1, act_all, mod = _forward(
        x[0], c, w_ada, b_ada, small, ln_g, ln_b, wint_loc, wout_loc)
    w_int, w_outf = [gathered0[0], gathered1[0]], [gathered0[1], gathered1[1]]
    acts = [(x[0], proj0, y0, cdf0), (out0, proj1, y1, cdf1)]

    shard_in, shard_out = D_IN // N_DEV, D_MODEL // N_DEV
    a, b = cur, loss_target[0]
    loss_lanes, carry, pending = None, (), []
    g_w_in_t, g_w_out = [None] * DEPTH, [None] * DEPTH
    for l in reversed(range(DEPTH)):
        dx, dproj, h, cat, dy, small_grads, dmod8, lanes, *shards = _layer_backward(
            l, a, b, *acts[l], mod, w_int[l], w_outf[l], small, ln_g, l == DEPTH - 1, f"layer_bwd_{l}",
            carry=carry, reduce=pending)
        if shards:
            g_w_in_t[l + 1], g_w_out[l + 1] = shards
        if l == DEPTH - 1:
            loss_lanes = lanes
        if l > 0:
            g_in, g_out = _grad_matmuls(dproj, h, cat, dy, f"grad_w_{l}")
            pending = [g_in.reshape(4, 2, shard_in, D_MODEL), g_out.reshape(4, 2, shard_out, D_MODEL)]
        carry = (small_grads, dmod8)
        a = b = dx

    (g_w_in_t[0], g_w_out[0], small_tot, loss_tile, g_w_ada, d_w_ada, nm_w_ada, nv_w_ada,
     g_b_ada, d_b_ada, nm_b_ada, nv_b_ada) = _grad_tail(
        dproj, h, cat, dy, small_grads.reshape(4, 2, DEPTH * PACK_ROWS // N_DEV, 128), dmod8, loss_lanes,
        w_ada, m_w_ada, v_w_ada, jnp.transpose(act_all), b_ada, m_b_ada, v_b_ada)
    loss = loss_tile[0, 0]

    flat = lambda t: t.reshape(-1, t.shape[-1])
    to_t = lambda t: flat(jnp.transpose(t, (0, 2, 1)))
    from_t = lambda t: jnp.transpose(t.reshape(DEPTH, shard_in, D_MODEL), (0, 2, 1))
    *updated, grad_x = _adamw(to_t(w_in), g_w_in_t, to_t(m_w_in), to_t(v_w_in), shard_in // 2, "adamw_w_in",
                              ride=[a])
    grad_x = grad_x[None]
    g_w_in, d_w_in, nm_w_in, nv_w_in = [from_t(t) for t in updated]
    gwout, d_w_out, nm_w_out, nv_w_out = [t.reshape(w_out.shape) for t in _adamw(
        flat(w_out), g_w_out, flat(m_w_out), flat(v_w_out), shard_out, "adamw_w_out")]
    small_out = _adamw_small(small_tot.reshape(DEPTH * PACK_ROWS, 128), [small_w[n] for n in SMALL_NAMES],
                             [small_m[n] for n in SMALL_NAMES], [small_v[n] for n in SMALL_NAMES], "adamw_small")
    gs, ds, ms, vs = [dict(zip(SMALL_NAMES, group)) for group in small_out]

    def ordered(w_ada_, b_ada_, w_in_, small, w_out_):
        return (w_ada_, b_ada_, w_in_, small["w_pool"], small["pool_scale"], small["sgu_ln_g"], small["sgu_ln_b"],
                small["w_sgu"], small["b_sgu"], w_out_, small["ln_g"], small["ln_b"])

    return (loss, grad_x,
            *ordered(g_w_ada, g_b_ada, g_w_in, gs, gwout),
            *ordered(d_w_ada, d_b_ada, d_w_in, ds, d_w_out),
            *ordered(nm_w_ada, nm_b_ada, nm_w_in, ms, nm_w_out),
            *ordered(nv_w_ada, nv_b_ada, nv_w_in, vs, nv_w_out))
```

```python
import jax
import jax.numpy as jnp
from jax import lax
from jax.experimental import pallas as pl
from jax.experimental.pallas import tpu as pltpu

F32 = jnp.float32
BF16 = jnp.bfloat16

D_MODEL = 1024
SEQ = 2048
DEPTH = 2
D_POOL = 512
D_SGU = 512
D_IN = 2560
N_GROUPS = 4
GROUP = 128
N_HEADS = 4
HEAD = 128
CHUNK = 128
WINDOWS = (2, 4, 8, 16)
ALPHA = (2.0 * DEPTH) ** 0.25
LN_EPS = 1e-5
N_DEV = 8

ADAM_LR = 0.001
ADAM_B1 = 0.9
ADAM_B2 = 0.999
ADAM_EPS = 1e-08
ADAM_WD = 0.01
ADAM_STEP = 10

TM = 256
HALO = 16
N_TILES = SEQ // TM
VMEM_LIMIT = 60 * 1024 * 1024

ROW_WPOOL = 0
ROW_WSGU = 512
ROW_PSCALE = 1024
ROW_SLNG = 1028
ROW_SLNB = 1032
ROW_BSGU = 1036
ROW_LNG = 1040
ROW_LNB = 1048
PACK_ROWS = 1088
DMOD_COLS = DEPTH * 3 * D_MODEL // 8

SQRT_HALF = 0.7071067811865476
INV_SQRT_2PI = 0.3989422804014327


def _ln(x):
    mu = jnp.mean(x, axis=-1, keepdims=True)
    xc = x - mu
    var = jnp.mean(xc * xc, axis=-1, keepdims=True)
    rstd = lax.rsqrt(var + LN_EPS)
    return xc * rstd, rstd


def _ln_bwd(dxn, xn, rstd):
    m1 = jnp.mean(dxn, axis=-1, keepdims=True)
    m2 = jnp.mean(dxn * xn, axis=-1, keepdims=True)
    return rstd * (dxn - m1 - xn * m2)


def _normal_cdf(x):
    return 0.5 * (1.0 + lax.erf(x * SQRT_HALF))


def _gelu_parts(x, cdf, with_grad):
    if not with_grad:
        return x * cdf, None
    return x * cdf, cdf + x * (INV_SQRT_2PI * jnp.exp(-0.5 * x * x))


def _silu_parts(x):
    s = jax.nn.sigmoid(x)
    return x * s, s * (1.0 + x * (1.0 - s))


def _dot(a, b):
    return lax.dot_general(a, b, (((1,), (0,)), ((), ())), preferred_element_type=F32)


def _dot_nt(a, b):
    return lax.dot_general(a, b, (((1,), (1,)), ((), ())), preferred_element_type=F32)


def _dot_tn(a, b):
    return lax.dot_general(a, b, (((0,), (0,)), ((), ())), preferred_element_type=F32)


def _row_index(tile):
    return tile * TM + lax.broadcasted_iota(jnp.int32, (TM, 1), 0)


def _window_sums(ext, forward):
    n = TM + HALO
    cur = ext
    outs = []
    for g in range(N_GROUPS):
        step = 1 << g
        cur = cur + pltpu.roll(cur, step if forward else n - step, 0)
        rows = cur[HALO:, :GROUP] if forward else cur[:TM, :GROUP]
        outs.append(rows)
        cur = cur[:, GROUP:] if g + 1 < N_GROUPS else None
    return outs


def _inverse_counts(rows):
    return [1.0 / jnp.minimum(rows + 1, w).astype(F32) for w in WINDOWS]


def _tril_bf16(w):
    t = lax.broadcasted_iota(jnp.int32, (CHUNK, CHUNK), 0)
    s = lax.broadcasted_iota(jnp.int32, (CHUNK, CHUNK), 1)
    return jnp.where(t >= s, w, 0.0).astype(BF16)


class _MixWeights:
    def __init__(self, layer, wpool_ref, pscale_ref, slng_ref, slnb_ref, wsgu_ref, bsgut_ref):
        self.layer = layer
        self.wpool_ref, self.pscale_ref, self.slng_ref, self.slnb_ref = wpool_ref, pscale_ref, slng_ref, slnb_ref
        self.wsgu_ref, self.bsgut_ref = wsgu_ref, bsgut_ref

    def pool(self, g):
        return self.wpool_ref[self.layer, g].astype(BF16)

    def pool_scale(self, g):
        return self.pscale_ref[self.layer:self.layer + 1, g * GROUP:(g + 1) * GROUP]

    def ln_gain(self, h):
        return self.slng_ref[self.layer, h:h + 1, :]

    def ln_bias(self, h):
        return self.slnb_ref[self.layer, h:h + 1, :]

    def mix(self, h):
        return _tril_bf16(self.wsgu_ref[self.layer, h])

    def mix_bias(self, h):
        return self.bsgut_ref[self.layer, :, h:h + 1]


SMALL_SPECS = ((DEPTH, N_GROUPS, GROUP, GROUP), (DEPTH, D_POOL), (DEPTH, N_HEADS, HEAD), (DEPTH, N_HEADS, HEAD),
               (DEPTH, N_HEADS, CHUNK, CHUNK), (DEPTH, CHUNK, N_HEADS))


def _mix_forward(proj, halo, tile, w, cdf=None):
    keep = cdf is not None
    rows = _row_index(tile)
    inv_counts = _inverse_counts(rows)
    xa = proj[:, 0:D_POOL]
    ga = proj[:, D_POOL:2 * D_POOL]
    sums = _window_sums(jnp.concatenate([halo, xa], axis=0), True)
    ga_act, ga_grad = _silu_parts(ga)
    pooled, pw, ya = [], [], []
    for g in range(N_GROUPS):
        sl = slice(g * GROUP, (g + 1) * GROUP)
        p = (sums[g] * inv_counts[g] - xa[:, sl]).astype(BF16)
        q = _dot(p, w.pool(g))
        pooled.append(p)
        pw.append(q)
        ya.append(q * w.pool_scale(g) * ga_act[:, sl])

    u = proj[:, 2 * D_POOL:2 * D_POOL + D_SGU]
    v = proj[:, 2 * D_POOL + D_SGU:2 * D_POOL + 2 * D_SGU]
    gb = proj[:, 2 * D_POOL + 2 * D_SGU:]
    gb_act, gb_grad = _silu_parts(gb)
    if cdf is None:
        cdf = jnp.concatenate([_normal_cdf(u), _normal_cdf(v)], axis=1)
    u_act, u_grad = _gelu_parts(u, cdf[:, :D_SGU], keep)
    v_act, v_grad = _gelu_parts(v, cdf[:, D_SGU:], keep)
    vn, vrstd, vln, mixed, yb = [], [], [], [], []
    for h in range(N_HEADS):
        sl = slice(h * HEAD, (h + 1) * HEAD)
        n_h, r_h = _ln(v_act[:, sl])
        l_h = (n_h * w.ln_gain(h) + w.ln_bias(h)).astype(BF16)
        w_h = w.mix(h)
        bias = w.mix_bias(h)
        m_h = jnp.concatenate(
            [_dot(w_h, l_h[k * CHUNK:(k + 1) * CHUNK]) + bias for k in range(TM // CHUNK)], axis=0)
        vn.append(n_h)
        vrstd.append(r_h)
        vln.append(l_h)
        mixed.append(m_h)
        yb.append(u_act[:, sl] * m_h * gb_act[:, sl])
    cat = jnp.concatenate(ya + yb, axis=1)
    if not keep:
        return cat, cdf
    return cat, dict(inv_counts=inv_counts, ga_act=ga_act, ga_grad=ga_grad, pooled=pooled, pw=pw, u_grad=u_grad,
                     v_grad=v_grad, u_act=u_act, gb_act=gb_act, gb_grad=gb_grad, vn=vn, vrstd=vrstd, vln=vln,
                     mixed=mixed)


def _const_spec(shape):
    nd = len(shape)
    return pl.BlockSpec(shape, lambda i: (0,) * nd)


VEC_LNG, VEC_LNB, VEC_POOL, VEC_SGU, VEC_SHIFT, VEC_SCALE, VEC_GATE, VEC_LOSS = range(8)


def _layer_backward(layer, a, b, x, proj, y, cdf, mod, w_int, w_outf, small, ln_g, is_last, name, carry=(),
                    reduce=()):
    n_red, n_carry = len(reduce), len(carry)
    base = layer * PACK_ROWS

    def body(a_ref, b_ref, x_ref, proj_ref, prev_ref, y_ref, cdf_ref, mod_ref, wint_ref, wout_ref, wpool_ref,
             pscale_ref, slng_ref, slnb_ref, wsgu_ref, bsgut_ref, lng_ref, *rest):
        weights = _MixWeights(layer, wpool_ref, pscale_ref, slng_ref, slnb_ref, wsgu_ref, bsgut_ref)
        carry_refs, rest = rest[:n_carry], rest[n_carry:]
        part_refs, rest = rest[:n_red], rest[n_red:]
        dx_ref, dproj_ref, h_ref, cat_ref, dy_ref, small_ref, dmod_ref, loss_ref = rest[:8]
        shard_refs, rest = rest[8:8 + n_red], rest[8 + n_red:]
        vec_ref, dmix_ref, halo_ref = rest[:3]
        step = pl.program_id(0)
        tile = N_TILES - 1 - step

        def scatter():
            bufs, sems = rest[3:3 + 5 * n_red], rest[3 + 5 * n_red:]
            arrays = [dict(part=part_refs[n], out=shard_refs[n], staged=True, stage=bufs[5 * n], sib=bufs[5 * n + 1],
                           snd=bufs[5 * n + 2], rcv=bufs[5 * n + 3], relay=bufs[5 * n + 4]) for n in range(n_red)]
            return _ChipReduceScatter(arrays, *sems)

        @pl.when(step == 0)
        def _():
            small_ref[...] = jnp.zeros_like(small_ref)
            dmod_ref[...] = jnp.zeros_like(dmod_ref)
            vec_ref[...] = jnp.zeros_like(vec_ref)
            dmix_ref[...] = jnp.zeros_like(dmix_ref)
            halo_ref[...] = jnp.zeros_like(halo_ref)
            if n_red:
                scatter().start()

        if n_red:
            @pl.when(step == 1)
            def _():
                scatter().exchange()

            @pl.when(step == N_TILES // 2)
            def _():
                scatter().fold()

        def acc(row, lo, val):
            hi = lo + val.shape[1]
            vec_ref[row:row + 1, lo:hi] += jnp.sum(val, axis=0, keepdims=True)

        xt = x_ref[...]
        yt = y_ref[...]
        shift = mod_ref[layer:layer + 1, 0:D_MODEL]
        scale = mod_ref[layer:layer + 1, D_MODEL:2 * D_MODEL]
        gate = mod_ref[layer:layer + 1, 2 * D_MODEL:]
        ln_gain = lng_ref[layer:layer + 1, :]

        zn, zrstd = _ln(ALPHA * xt + gate * yt)
        if is_last:
            diff = a_ref[...] - b_ref[...]
            acc(VEC_LOSS, 0, diff * diff)
            dout = diff * (1.0 / D_MODEL)
        else:
            dout = a_ref[...]
        acc(VEC_LNG, 0, dout * zn)
        acc(VEC_LNB, 0, dout)
        dz = _ln_bwd(dout * ln_gain, zn, zrstd)
        acc(VEC_GATE, 0, dz * yt)
        dy = (dz * gate).astype(BF16)
        dy_ref[...] = dy
        dcat = _dot_nt(dy, wout_ref[...])

        proj = proj_ref[...]
        prev = jnp.where(tile > 0, prev_ref[...], 0.0)
        cat, k = _mix_forward(proj, prev, tile, weights, cdf_ref[...])
        cat_ref[...] = cat.astype(BF16)

        dga, dq = [], []
        for g in range(N_GROUPS):
            sl = slice(g * GROUP, (g + 1) * GROUP)
            pscale = weights.pool_scale(g)
            dya = dcat[:, sl]
            dyp = dya * k["ga_act"][:, sl]
            dga.append(dya * k["pw"][g] * pscale * k["ga_grad"][:, sl])
            acc(VEC_POOL, g * GROUP, dyp * k["pw"][g])
            dpw = (dyp * pscale).astype(BF16)
            rows = pl.ds(base + ROW_WPOOL + g * GROUP, GROUP)
            small_ref[rows, :] += _dot_tn(k["pooled"][g], dpw)
            dq.append(_dot_nt(dpw, weights.pool(g)))
        dpooled = jnp.concatenate(dq, axis=1)
        scaled = jnp.concatenate([dq[g] * k["inv_counts"][g] for g in range(N_GROUPS)], axis=1)
        sums = _window_sums(jnp.concatenate([scaled, halo_ref[...]], axis=0), False)
        halo_ref[...] = scaled[0:HALO]
        dxa = jnp.concatenate(sums, axis=1) - dpooled

        du, dv, dgb = [], [], []
        for h in range(N_HEADS):
            sl = slice(h * HEAD, (h + 1) * HEAD)
            dyb = dcat[:, D_POOL + h * HEAD:D_POOL + (h + 1) * HEAD]
            m_h = k["mixed"][h]
            ug = k["u_act"][:, sl] * dyb
            du.append(dyb * m_h * k["gb_act"][:, sl] * k["u_grad"][:, sl])
            dgb.append(ug * m_h * k["gb_grad"][:, sl])
            dmixed = ug * k["gb_act"][:, sl]
            dmixed_bf = dmixed.astype(BF16)
            w_h = weights.mix(h)
            dvln_parts = []
            dmix_sum = dmix_ref[h]
            wsgu_rows = pl.ds(base + ROW_WSGU + h * CHUNK, CHUNK)
            dws = small_ref[wsgu_rows, :]
            for c in range(TM // CHUNK):
                cs = slice(c * CHUNK, (c + 1) * CHUNK)
                dmix_sum = dmix_sum + dmixed[cs]
                dws = dws + _dot_nt(dmixed_bf[cs], k["vln"][h][cs])
                dvln_parts.append(_dot_tn(w_h, dmixed_bf[cs]))
            dmix_ref[h] = dmix_sum
            small_ref[wsgu_rows, :] = dws
            dvln = jnp.concatenate(dvln_parts, axis=0)
            acc(VEC_SGU, h * HEAD, dvln * k["vn"][h])
            acc(VEC_SGU, D_SGU + h * HEAD, dvln)
            dvv = _ln_bwd(dvln * weights.ln_gain(h), k["vn"][h], k["vrstd"][h])
            dv.append(dvv * k["v_grad"][:, sl])

        dproj = jnp.concatenate([dxa] + dga + du + dv + dgb, axis=1).astype(BF16)
        dproj_ref[...] = dproj
        dh = _dot(dproj, wint_ref[...])

        xn, xrstd = _ln(xt)
        h_ref[...] = (xn * (1.0 + scale) + shift).astype(BF16)
        acc(VEC_SCALE, 0, dh * xn)
        acc(VEC_SHIFT, 0, dh)
        dx_ref[...] = _ln_bwd(dh * (1.0 + scale), xn, xrstd) + ALPHA * dz

        @pl.when(step == N_TILES - 1)
        def _():
            def put(row0, vec_row, lo, n):
                for r in range(n):
                    small_ref[base + row0 + r:base + row0 + r + 1, :] = (
                        vec_ref[vec_row:vec_row + 1, lo + r * 128:lo + (r + 1) * 128])

            put(ROW_PSCALE, VEC_POOL, 0, 4)
            put(ROW_SLNG, VEC_SGU, 0, 4)
            put(ROW_SLNB, VEC_SGU, D_SGU, 4)
            put(ROW_LNG, VEC_LNG, 0, 8)
            put(ROW_LNB, VEC_LNB, 0, 8)
            ones = jnp.ones((8, HEAD), F32)
            t = lax.broadcasted_iota(jnp.int32, (CHUNK, CHUNK), 0)
            s = lax.broadcasted_iota(jnp.int32, (CHUNK, CHUNK), 1)
            for h in range(N_HEADS):
                bias_rows = lax.dot_general(ones, dmix_ref[h], (((1,), (1,)), ((), ())),
                                            preferred_element_type=F32, precision=lax.Precision.HIGHEST)
                small_ref[base + ROW_BSGU + h:base + ROW_BSGU + h + 1, :] = bias_rows[0:1]
                rows = pl.ds(base + ROW_WSGU + h * CHUNK, CHUNK)
                small_ref[rows, :] = jnp.where(t >= s, small_ref[rows, :], 0.0)
            pieces = ((0, VEC_SHIFT, 0, 768),
                      (1, VEC_SHIFT, 768, 256), (1, VEC_SCALE, 0, 512),
                      (2, VEC_SCALE, 512, 512), (2, VEC_GATE, 0, 256),
                      (3, VEC_GATE, 256, 768))
            filled = [0] * 4
            for q, vec_row, lo, n in pieces:
                row = 4 * layer + q
                dmod_ref[row:row + 1, filled[q]:filled[q] + n] = vec_ref[vec_row:vec_row + 1, lo:lo + n]
                filled[q] += n
            if n_carry:
                for other in range(layer + 1, DEPTH):
                    rows = pl.ds(other * PACK_ROWS, PACK_ROWS)
                    small_ref[rows, :] = carry_refs[0][rows, :]
                    dmod_ref[4 * other:4 * other + 4, :] = carry_refs[1][4 * other:4 * other + 4, :]
            loss_ref[...] = vec_ref[VEC_LOSS:VEC_LOSS + 1, :]
            if n_red:
                scatter().finish()
                scatter().wait_sends()

    rev = lambda w: pl.BlockSpec((TM, w), lambda i: (N_TILES - 1 - i, 0))
    prev_spec = pl.BlockSpec(
        (HALO, D_POOL), lambda i: (jnp.maximum((N_TILES - 1 - i) * (TM // HALO) - 1, 0), 0))
    comm_scratch = []
    for p in reduce:
        comm_scratch += _ChipReduceScatter.buffers(p.shape[2], p.shape[3], p.dtype)
    if n_red:
        comm_scratch += _ChipReduceScatter.semaphores(n_red)
    return pl.pallas_call(
        body,
        name=name,
        grid=(N_TILES,),
        in_specs=[rev(D_MODEL), rev(D_MODEL) if is_last else pl.BlockSpec((TM, D_MODEL), lambda i: (0, 0)),
                  rev(D_MODEL), rev(D_IN), prev_spec, rev(D_MODEL), rev(2 * D_SGU),
                  _const_spec((DEPTH, 3 * D_MODEL)), _const_spec((D_IN, D_MODEL)), _const_spec((D_MODEL, D_MODEL))]
                 + [_const_spec(s) for s in SMALL_SPECS] + [_const_spec((DEPTH, D_MODEL))]
                 + [_const_spec(c.shape) for c in carry] + [ANY] * n_red,
        out_specs=[rev(D_MODEL), rev(D_IN), rev(D_MODEL), rev(D_MODEL), rev(D_MODEL),
                   _const_spec((DEPTH * PACK_ROWS, 128)), _const_spec((8, DMOD_COLS)), _const_spec((1, D_MODEL))]
                  + [_const_spec(p.shape[2:]) for p in reduce],
        out_shape=[jax.ShapeDtypeStruct((SEQ, D_MODEL), F32), jax.ShapeDtypeStruct((SEQ, D_IN), BF16),
                   jax.ShapeDtypeStruct((SEQ, D_MODEL), BF16), jax.ShapeDtypeStruct((SEQ, D_MODEL), BF16),
                   jax.ShapeDtypeStruct((SEQ, D_MODEL), BF16), jax.ShapeDtypeStruct((DEPTH * PACK_ROWS, 128), F32),
                   jax.ShapeDtypeStruct((8, DMOD_COLS), F32), jax.ShapeDtypeStruct((1, D_MODEL), F32)]
                  + [jax.ShapeDtypeStruct(p.shape[2:], F32) for p in reduce],
        scratch_shapes=[pltpu.VMEM((8, D_MODEL), F32), pltpu.VMEM((N_HEADS, CHUNK, HEAD), F32),
                        pltpu.VMEM((HALO, D_POOL), F32)] + comm_scratch,
        compiler_params=pltpu.CompilerParams(dimension_semantics=("arbitrary",), vmem_limit_bytes=VMEM_LIMIT),
    )(a, b, x, proj, proj, y, cdf, mod, w_int, w_outf, *small, ln_g, *carry, *reduce)


def _grad_matmuls(dproj, h, cat, dy, name):
    in_cols, out_cols = D_IN // 4, D_MODEL // 2
    in_steps = D_IN // in_cols

    def body(dproj_ref, h_ref, cat_ref, dy_ref, gin_ref, gout_ref):
        step = pl.program_id(0)

        @pl.when(step < in_steps)
        def _():
            gin_ref[...] = _dot_tn(dproj_ref[...], h_ref[...]).astype(BF16)

        @pl.when(step >= in_steps)
        def _():
            gout_ref[...] = _dot_tn(cat_ref[...], dy_ref[...]).astype(BF16)

    in_block = lambda j: jnp.minimum(j, in_steps - 1)
    out_block = lambda j: jnp.maximum(j - in_steps, 0)
    return pl.pallas_call(
        body,
        name=name,
        grid=(in_steps + D_MODEL // out_cols,),
        in_specs=[pl.BlockSpec((SEQ, in_cols), lambda j: (0, in_block(j))), _const_spec((SEQ, D_MODEL)),
                  pl.BlockSpec((SEQ, out_cols), lambda j: (0, out_block(j))), _const_spec((SEQ, D_MODEL))],
        out_specs=[pl.BlockSpec((in_cols, D_MODEL), lambda j: (in_block(j), 0)),
                   pl.BlockSpec((out_cols, D_MODEL), lambda j: (out_block(j), 0))],
        out_shape=[jax.ShapeDtypeStruct((D_IN, D_MODEL), BF16), jax.ShapeDtypeStruct((D_MODEL, D_MODEL), BF16)],
        compiler_params=pltpu.CompilerParams(dimension_semantics=("arbitrary",), vmem_limit_bytes=VMEM_LIMIT),
    )(dproj, h, cat, dy)


def _adamw_math(w, g, m, v):
    m = ADAM_B1 * m + (1.0 - ADAM_B1) * g
    v = ADAM_B2 * v + (1.0 - ADAM_B2) * (g * g)
    m_hat = m / (1.0 - ADAM_B1 ** ADAM_STEP)
    v_hat = v / (1.0 - ADAM_B2 ** ADAM_STEP)
    delta = -ADAM_LR * (m_hat / (jnp.sqrt(v_hat) + ADAM_EPS) + ADAM_WD * w)
    return delta, m, v


def _adamw(w, grads, m, v, block_rows, name, ride=()):
    rows, cols = grads[0].shape
    blocks = rows // block_rows
    n_ride = len(ride)

    def body(w_ref, m_ref, v_ref, *rest):
        g_refs, rest = rest[:DEPTH], rest[DEPTH:]
        ride_in, (g_ref, d_ref, nm_ref, nv_ref), ride_out = rest[:n_ride], rest[n_ride:n_ride + 4], rest[n_ride + 4:]
        for src, dst in zip(ride_in, ride_out):
            dst[...] = src[...]
        for layer in range(DEPTH):
            @pl.when(pl.program_id(0) == layer)
            def _():
                g = g_refs[layer][...]
                g_ref[...] = g
                d_ref[...], nm_ref[...], nv_ref[...] = _adamw_math(w_ref[...], g, m_ref[...], v_ref[...])

    def grad_spec(layer):
        return pl.BlockSpec((block_rows, cols),
                            lambda l, i: (jnp.where(l == layer, i, jnp.where(l < layer, 0, blocks - 1)), 0))

    spec = pl.BlockSpec((block_rows, cols), lambda l, i: (l * blocks + i, 0))
    ride_specs = [pl.BlockSpec((r.shape[0] // (DEPTH * blocks), r.shape[1]), lambda l, i: (l * blocks + i, 0))
                  for r in ride]
    return pl.pallas_call(
        body,
        name=name,
        grid=(DEPTH, blocks),
        in_specs=[spec] * 3 + [grad_spec(layer) for layer in range(DEPTH)] + ride_specs,
        out_specs=[spec] * 4 + ride_specs,
        out_shape=[jax.ShapeDtypeStruct(w.shape, F32)] * 4 + [jax.ShapeDtypeStruct(r.shape, r.dtype) for r in ride],
        compiler_params=pltpu.CompilerParams(dimension_semantics=("arbitrary", "arbitrary"),
                                             vmem_limit_bytes=VMEM_LIMIT),
    )(w, m, v, *grads, *ride)


MESH = pl.DeviceIdType.MESH
SIBLING = 1
ANY = pl.BlockSpec(memory_space=pl.ANY)
VMEM = pl.BlockSpec(memory_space=pltpu.VMEM)
BULK = 1


def _me():
    return lax.axis_index("x"), lax.axis_index("y"), lax.axis_index("c")


def _peer(r):
    x, y, c = _me()
    return (1 - x if r & 4 else x, 1 - y if r & 2 else y, 1 - c if r & 1 else c)


def _index(dev):
    return 4 * dev[0] + 2 * dev[1] + dev[2]


def _remote(src, dst, send_sem, recv_sem, dev):
    return pltpu.make_async_remote_copy(src_ref=src, dst_ref=dst, send_sem=send_sem, recv_sem=recv_sem,
                                        device_id=dev, device_id_type=MESH)


ACROSS_X, ACROSS_Y, ACROSS_BOTH = 4, 2, 6
GATHER_SEMS = 11


class _TwoLevelGather:
    def __init__(self, out, send_sems, recv_sems, src=None):
        self.out, self.send_sems, self.recv_sems, self.src = out, send_sems, recv_sems, src
        self.rows = (out.shape[0] // N_DEV) if len(out.shape) == 2 else out.shape[1]
        self.half = self.rows // 2

    def _slot(self, block):
        if len(self.out.shape) == 2:
            return self.out.at[pl.ds(pl.multiple_of(_index(block) * self.rows, self.rows), self.rows)]
        return self.out.at[_index(block)]

    def _copy(self, k, block, part, to, src=None):
        slot = self._slot(block)
        if part is not None:
            rows = pl.ds(part * self.half, self.half)
            slot = slot.at[rows]
            src = None if src is None else src.at[rows]
        return _remote(slot if src is None else src, slot, self.send_sems.at[k], self.recv_sems.at[k], to)

    def _mine(self):
        me = _me()
        src = self._slot(me) if self.src is None else self.src
        x, y = _peer(ACROSS_X), _peer(ACROSS_Y)
        return [self._copy(1, me, 0, x, src), self._copy(3, me, 1, y, src), self._copy(0, me, None, _peer(SIBLING), src),
                self._copy(2, me, 1, x, src), self._copy(4, me, 0, y, src)]

    def _relayed(self):
        return [self._copy(5, _peer(ACROSS_X), 0, _peer(ACROSS_Y)), self._copy(6, _peer(ACROSS_Y), 1, _peer(ACROSS_X))]

    def _passed(self):
        sib, far = _peer(SIBLING), _peer(ACROSS_BOTH)
        return [self._copy(7, _peer(ACROSS_X), None, sib), self._copy(8, _peer(ACROSS_Y), None, sib),
                self._copy(9, far, 0, sib), self._copy(10, far, 1, sib)]

    def _arrival(self, k, r, part):
        return self._copy(k, _peer(r), part, _me())

    def send_first(self):
        for cp in self._mine()[:3]:
            cp.start()

    def send_second(self):
        for cp in self._mine()[3:]:
            cp.start()

    def send_mine(self):
        self.send_first()
        self.send_second()

    def relay(self):
        relayed = self._relayed()
        self._arrival(1, ACROSS_X, 0).wait_recv()
        relayed[0].start()
        self._arrival(3, ACROSS_Y, 1).wait_recv()
        relayed[1].start()

    def pass_near(self):
        passed = self._passed()
        self._arrival(2, ACROSS_X, 1).wait_recv()
        passed[0].start()
        self._arrival(4, ACROSS_Y, 0).wait_recv()
        passed[1].start()

    def pass_far(self):
        passed = self._passed()
        self._arrival(5, ACROSS_BOTH, 0).wait_recv()
        passed[2].start()
        self._arrival(6, ACROSS_BOTH, 1).wait_recv()
        passed[3].start()

    def pass_on(self):
        self.pass_near()
        self.pass_far()

    def wait_sibling(self):
        self._arrival(0, SIBLING, None).wait_recv()

    def wait_passed(self, r):
        if r == ACROSS_BOTH:
            self._arrival(9, r ^ SIBLING, 0).wait_recv()
            self._arrival(10, r ^ SIBLING, 1).wait_recv()
        else:
            self._arrival(7 if r == ACROSS_X else 8, r ^ SIBLING, None).wait_recv()

    def wait_rest(self):
        self.wait_sibling()
        for r in (ACROSS_X, ACROSS_Y, ACROSS_BOTH):
            self.wait_passed(r)

    def wait_sends(self):
        for cp in self._mine() + self._relayed() + self._passed():
            cp.wait_send()


class _ChipReduceScatter:
    SLOTS = 6

    def __init__(self, arrays, l_sem, d_send, d_recv, i_send, i_recv):
        self.arrays = arrays
        self.l_sem, self.d_send, self.d_recv, self.i_send, self.i_recv = l_sem, d_send, d_recv, i_send, i_recv

    @staticmethod
    def buffers(rows, cols, dtype, staged=True):
        stage = [pltpu.VMEM((4, rows, cols), dtype)] if staged else []
        return stage + [pltpu.VMEM((4, rows, cols), dtype), pltpu.VMEM((3, rows, cols), dtype),
                        pltpu.VMEM((2, rows, cols), dtype), pltpu.VMEM((2, rows // 2, cols), dtype)]

    @classmethod
    def semaphores(cls, n):
        return [pltpu.SemaphoreType.DMA((n,)), pltpu.SemaphoreType.DMA((n, 4)), pltpu.SemaphoreType.DMA((n, 4)),
                pltpu.SemaphoreType.DMA((n, cls.SLOTS)), pltpu.SemaphoreType.DMA((n, cls.SLOTS))]

    def _pick(self, which):
        return list(enumerate(self.arrays)) if which is None else [(n, self.arrays[n]) for n in which]

    @staticmethod
    def _chip(r):
        dev = _me() if r is None else _peer(r)
        return 2 * dev[0] + dev[1]

    def _staging(self, which):
        c = _me()[2]
        return [pltpu.make_async_copy(a["part"].at[pl.ds(0, 4), c], a["stage"], self.l_sem.at[n])
                for n, a in self._pick(which) if a["staged"]]

    def _first(self, which, chip):
        other = 1 - _me()[2]
        return [_remote(a["part"].at[chip, other], a["sib"].at[chip], self.d_send.at[n, chip], self.d_recv.at[n, chip],
                        _peer(SIBLING)) for n, a in self._pick(which)]

    @staticmethod
    def _halves(a):
        half = a["rcv"].shape[1] // 2
        return pl.ds(0, half), pl.ds(half, half)

    def _hops(self, n, a):
        h0, h1 = self._halves(a)
        x, y = _peer(ACROSS_X), _peer(ACROSS_Y)
        snd, rcv, relay = a["snd"], a["rcv"], a["relay"]
        pairs = [(snd.at[2, h0], relay.at[0], x), (snd.at[2, h1], relay.at[1], y),
                 (snd.at[0, h0], rcv.at[0, h0], x), (snd.at[0, h1], rcv.at[0, h1], x),
                 (snd.at[1, h1], rcv.at[1, h1], y), (snd.at[1, h0], rcv.at[1, h0], y)]
        return [_remote(s, d, self.i_send.at[n, k], self.i_recv.at[n, k], to) for k, (s, d, to) in enumerate(pairs)]

    def _mine(self, a, chip, rows=None):
        src = a["stage"].at[chip] if a["staged"] else a["part"].at[chip, _me()[2]]
        mine, sib = (src[...], a["sib"][chip]) if rows is None else (src[rows, :], a["sib"][chip, rows, :])
        return mine.astype(F32) + sib.astype(F32)

    def start(self, which=None, chips=None):
        if chips is None:
            for cp in self._staging(which):
                cp.start()
        for chip in range(4) if chips is None else chips:
            for cp in self._first(which, chip):
                cp.start()

    def send_far(self, which=None):
        far = self._chip(ACROSS_BOTH)
        for cp in self._staging(which):
            cp.wait()
        for cp in self._first(which, far):
            cp.wait_recv()
        for n, a in self._pick(which):
            hops = self._hops(n, a)
            a["snd"][2] = self._mine(a, far).astype(a["snd"].dtype)
            hops[0].start()
            hops[1].start()

    def send_near(self, r, which=None):
        chip = self._chip(r)
        for cp in self._first(which, chip):
            cp.wait_recv()
        for n, a in self._pick(which):
            h0, h1 = self._halves(a)
            hops = self._hops(n, a)
            if r == ACROSS_X:
                a["snd"][0, h0, :] = self._mine(a, chip, h0).astype(a["snd"].dtype)
                hops[2].start()
            else:
                a["snd"][1, h1, :] = self._mine(a, chip, h1).astype(a["snd"].dtype)
                hops[4].start()

    def exchange(self, which=None):
        self.send_far(which)
        self.send_near(ACROSS_X, which)
        self.send_near(ACROSS_Y, which)

    def fold(self, which=None):
        across_x, across_y = self._chip(ACROSS_X), self._chip(ACROSS_Y)
        for n, a in self._pick(which):
            h0, h1 = self._halves(a)
            hops = self._hops(n, a)
            dtype = a["snd"].dtype
            hops[1].wait_recv()
            a["snd"][0, h1, :] = (self._mine(a, across_x, h1) + a["relay"][1].astype(F32)).astype(dtype)
            hops[3].start()
            hops[0].wait_recv()
            a["snd"][1, h0, :] = (self._mine(a, across_y, h0) + a["relay"][0].astype(F32)).astype(dtype)
            hops[5].start()

    def finish(self, which=None):
        home = self._chip(None)
        for cp in self._first(which, home):
            cp.wait_recv()
        for n, a in self._pick(which):
            hops = self._hops(n, a)
            a["out"][...] = self._mine(a, home)
            hops[2].wait_recv()
            hops[3].wait_recv()
            a["out"][...] += a["rcv"][0].astype(F32)
            hops[4].wait_recv()
            hops[5].wait_recv()
            a["out"][...] += a["rcv"][1].astype(F32)

    def wait_sends(self, which=None):
        for chip in range(4):
            for cp in self._first(which, chip):
                cp.wait_send()
        for n, a in self._pick(which):
            for cp in self._hops(n, a):
                cp.wait_send()


def _direct_exchange(src_of, dst_of, send_sems, recv_sems):
    me = _me()
    copies = [_remote(src_of(_peer(r)), dst_of(me), send_sems.at[r - 1], recv_sems.at[r - 1], _peer(r))
              for r in range(1, N_DEV)]
    for cp in copies:
        cp.start()
    return copies


def _wait_direct(copies):
    for cp in copies:
        cp.wait_recv()
    for cp in copies:
        cp.wait_send()


def _forward(x, c, w_ada, b_ada, small, ln_g, ln_b, mine, following):
    assert DEPTH == 2
    cols = w_ada.shape[2]
    shard = mine[0].shape[0]
    pair = 2 * shard

    def body(x_hbm, c_ref, wada_hbm, bada_ref, wpool_ref, pscale_ref, slng_ref, slnb_ref, wsgu_ref, bsgut_ref,
             lng_ref, lnb_ref, wint_hbm, wout_hbm, next_in_hbm, next_out_hbm,
             out0_ref, y0_ref, cdf0_ref, proj0_hbm, out1_ref, y1_ref, cdf1_ref, proj1_hbm,
             wint_keep, wout_keep, wint_next, wout_next, acts_ref, mod_ref,
             wint_v, wout_v, h_buf, proj_blk, proj_tile, halo_ref, x_ref, w_send, w_recv, w_local, p_sems,
             t_sems, in_sems, act_all, act_src, part, mod_recv, wada_ref, a_send, a_recv, m_send, m_recv,
             n_send, n_recv, n_local, f_sems):
        step = pl.program_id(0)
        chip_of = lambda dev: 2 * dev[0] + dev[1]

        def hosted():
            return [_TwoLevelGather(out, n_send.at[n], n_recv.at[n], src=src)
                    for n, (out, src) in enumerate(((wint_next, next_in_hbm), (wout_next, next_out_hbm)))]

        def hosted_own():
            me = _me()
            return [pltpu.make_async_copy(g.src, g._slot(me), n_local.at[n]) for n, g in enumerate(hosted())]

        def gathers():
            return (_TwoLevelGather(wint_v, w_send.at[0], w_recv.at[0], src=wint_hbm),
                    _TwoLevelGather(wout_v, w_send.at[1], w_recv.at[1], src=wout_hbm))

        def keeps():
            return [pltpu.make_async_copy(wint_v, wint_keep, w_local.at[2]),
                    pltpu.make_async_copy(wout_v, wout_keep, w_local.at[3])]

        def tile_read(t):
            slot = t % 2
            return pltpu.make_async_copy(proj0_hbm.at[pl.ds(pl.multiple_of(t * TM, TM), TM)], proj_tile.at[slot],
                                         t_sems.at[slot])

        def first_layer_start():
            writes = []

            def project(n, dev):
                first = pl.multiple_of(chip_of(dev) * pair, pair)
                if n >= 2:
                    writes[n - 2].wait()

                @pl.loop(0, N_TILES)
                def _(t):
                    rows = pl.ds(pl.multiple_of(t * TM, TM), TM)
                    proj_blk[n % 2, rows, :] = _dot_nt(h_buf[rows, :], wint_v[pl.ds(first, pair), :])

                cp = pltpu.make_async_copy(proj_blk.at[n % 2], proj0_hbm.at[:, pl.ds(first, pair)], p_sems.at[n % 2])
                cp.start(priority=BULK)
                writes.append(cp)

            me = _me()
            halo_ref[...] = jnp.zeros_like(halo_ref)
            gather_in, gather_out = gathers()
            own_in = pltpu.make_async_copy(wint_hbm, gather_in._slot(me), w_local.at[0])
            own_out = pltpu.make_async_copy(wout_hbm, gather_out._slot(me), w_local.at[1])
            x_load = pltpu.make_async_copy(x_hbm, x_ref, in_sems.at[0])
            x_load.start(priority=BULK)
            wada_load = pltpu.make_async_copy(wada_hbm, wada_ref, in_sems.at[1])
            wada_load.start(priority=BULK)
            mine_index = _index(me)
            cval = c_ref[...]
            act_src[...] = jnp.zeros_like(act_src)
            act_src[0:1, :] = cval * jax.nn.sigmoid(cval)
            act_all[mine_index] = act_src[...]
            act_copies = _direct_exchange(lambda p: act_src, lambda m: act_all.at[_index(m)], a_send, a_recv)

            own_in.start()
            own_out.start()
            gather_in.send_first()

            _wait_direct(act_copies)
            acts = jnp.concatenate([act_all[j, 0:1, :] for j in range(N_DEV)], axis=0)
            acts_ref[...] = acts
            part[...] = jnp.zeros_like(part)
            wada_load.wait()
            for l in range(DEPTH):
                res = lax.dot_general(acts, wada_ref[l], (((1,), (0,)), ((), ())), preferred_element_type=F32,
                                      precision=lax.Precision.HIGHEST)
                for b in range(N_DEV):
                    part[b, l:l + 1, :] = res[b:b + 1, :]
            mod_recv[mine_index] = part[mine_index]
            mod_copies = _direct_exchange(lambda p: part.at[_index(p)], lambda m: mod_recv.at[_index(m)],
                                          m_send, m_recv)
            gather_in.send_second()
            gather_out.send_mine()

            _wait_direct(mod_copies)
            for l in range(DEPTH):
                for j in range(N_DEV):
                    sl = slice(j * cols, (j + 1) * cols)
                    mod_ref[l:l + 1, sl] = mod_recv[j, l:l + 1, :] + bada_ref[l:l + 1, sl]
            x_load.wait()
            shift = mod_ref[0:1, 0:D_MODEL]
            scale = mod_ref[0:1, D_MODEL:2 * D_MODEL]

            @pl.loop(0, N_TILES)
            def _(t):
                rows = pl.ds(pl.multiple_of(t * TM, TM), TM)
                xn, _ = _ln(x_ref[rows, :])
                h_buf[rows, :] = (xn * (1.0 + scale) + shift).astype(BF16)

            gather_in.relay()
            own_in.wait()
            gather_in.wait_sibling()
            project(0, me)
            gather_in.pass_near()
            gather_in.wait_passed(ACROSS_X)
            project(1, _peer(ACROSS_X))
            gather_out.relay()
            for cp in hosted_own():
                cp.start()
            for g in hosted():
                g.send_mine()
            gather_in.wait_passed(ACROSS_Y)
            project(2, _peer(ACROSS_Y))
            gather_in.pass_far()
            gather_in.wait_passed(ACROSS_BOTH)
            project(3, _peer(ACROSS_BOTH))

            gather_out.pass_on()
            gather_out.wait_rest()
            own_out.wait()
            for cp in keeps():
                cp.start(priority=BULK)
            writes[2].wait()
            writes[3].wait()
            tile_read(0).start()

        def fetches():
            return [pltpu.make_async_copy(wint_next, wint_v, f_sems.at[0]),
                    pltpu.make_async_copy(wout_next, wout_v, f_sems.at[1])]

        def tile_write(t):
            slot = t % 2
            return pltpu.make_async_copy(proj_tile.at[slot], proj1_hbm.at[pl.ds(pl.multiple_of(t * TM, TM), TM)],
                                         t_sems.at[slot])

        def mix_and_close(layer, tile, rows, proj, out_ref, y_ref, cdf_ref):
            weights = _MixWeights(layer, wpool_ref, pscale_ref, slng_ref, slnb_ref, wsgu_ref, bsgut_ref)
            xt = x_ref[rows, :]
            gate = mod_ref[layer:layer + 1, 2 * D_MODEL:]
            cat, cdf_ref[...] = _mix_forward(proj, halo_ref[...], tile, weights)
            halo_ref[...] = proj[TM - HALO:, 0:D_POOL]
            if layer == 1:
                @pl.when(tile == 0)
                def _():
                    fetches()[1].wait()
            y = _dot(cat.astype(BF16), wout_v[...])
            y_ref[...] = y
            zn, _ = _ln(ALPHA * xt + gate * y)
            out = zn * lng_ref[layer:layer + 1, :] + lnb_ref[layer:layer + 1, :]
            out_ref[...] = out
            if layer == 0:
                x_ref[rows, :] = out

        def first_layer_tile(tile):
            @pl.when(tile + 1 < N_TILES)
            def _():
                tile_read(tile + 1).start()

            tile_read(tile).wait()
            rows = pl.ds(pl.multiple_of(tile * TM, TM), TM)
            mix_and_close(0, tile, rows, proj_tile[tile % 2], out0_ref, y0_ref, cdf0_ref)

        def second_layer_tile(tile):
            rows = pl.ds(pl.multiple_of(tile * TM, TM), TM)
            shift = mod_ref[1:2, 0:D_MODEL]
            scale = mod_ref[1:2, D_MODEL:2 * D_MODEL]
            xn, _ = _ln(x_ref[rows, :])
            h = (xn * (1.0 + scale) + shift).astype(BF16)

            @pl.when(tile >= 2)
            def _():
                tile_write(tile - 2).wait()

            proj_tile[tile % 2] = _dot_nt(h, wint_v[...])
            tile_write(tile).start()
            mix_and_close(1, tile, rows, proj_tile[tile % 2], out1_ref, y1_ref, cdf1_ref)

        @pl.when(step < N_TILES)
        def _():
            @pl.when(step == 0)
            def _():
                first_layer_start()

            @pl.when(step == 1)
            def _():
                for g in hosted():
                    g.relay()

            @pl.when(step == N_TILES // 2)
            def _():
                for g in hosted():
                    g.pass_near()

            @pl.when(step == N_TILES - 1)
            def _():
                for g in hosted():
                    g.pass_far()
                for g in gathers():
                    g.wait_sends()
                for cp in keeps() + hosted_own():
                    cp.wait()
                hosted()[0].wait_rest()
                fetches()[0].start()

            first_layer_tile(step)

        @pl.when(step >= N_TILES)
        def _():
            @pl.when(step == N_TILES)
            def _():
                halo_ref[...] = jnp.zeros_like(halo_ref)
                hosted()[1].wait_rest()
                fetches()[1].start()
                fetches()[0].wait()

            second_layer_tile(step - N_TILES)

            @pl.when(step == 2 * N_TILES - 1)
            def _():
                for g in hosted():
                    g.wait_sends()
                tile_write(N_TILES - 2).wait()
                tile_write(N_TILES - 1).wait()

    first = lambda w: pl.BlockSpec((TM, w), lambda i: (jnp.minimum(i, N_TILES - 1), 0))
    second = lambda w: pl.BlockSpec((TM, w), lambda i: (jnp.maximum(i - N_TILES, 0), 0))
    gather_sems = pltpu.SemaphoreType.DMA((2, GATHER_SEMS))
    seven = pltpu.SemaphoreType.DMA((7,))
    per_layer = [jax.ShapeDtypeStruct((SEQ, D_MODEL), F32), jax.ShapeDtypeStruct((SEQ, D_MODEL), F32),
                 jax.ShapeDtypeStruct((SEQ, 2 * D_SGU), F32), jax.ShapeDtypeStruct((SEQ, D_IN), F32)]
    gathered = [jax.ShapeDtypeStruct((D_IN, D_MODEL), BF16), jax.ShapeDtypeStruct((D_MODEL, D_MODEL), BF16)]
    res = pl.pallas_call(
        body,
        name="layers_fwd",
        grid=(DEPTH * N_TILES,),
        in_specs=[ANY, _const_spec(c.shape), ANY, _const_spec(b_ada.shape)] + [_const_spec(s) for s in SMALL_SPECS]
                 + [_const_spec((DEPTH, D_MODEL)), _const_spec((DEPTH, D_MODEL))] + [ANY] * 4,
        out_specs=[first(D_MODEL), first(D_MODEL), first(2 * D_SGU), ANY,
                   second(D_MODEL), second(D_MODEL), second(2 * D_SGU), ANY] + [ANY] * 4
                  + [_const_spec((N_DEV, D_MODEL)), _const_spec((DEPTH, 3 * D_MODEL))],
        out_shape=per_layer * 2 + gathered * 2 + [jax.ShapeDtypeStruct((N_DEV, D_MODEL), F32),
                                                  jax.ShapeDtypeStruct((DEPTH, 3 * D_MODEL), F32)],
        scratch_shapes=[pltpu.VMEM((D_IN, D_MODEL), BF16), pltpu.VMEM((D_MODEL, D_MODEL), BF16),
                        pltpu.VMEM((SEQ, D_MODEL), BF16), pltpu.VMEM((2, SEQ, pair), F32),
                        pltpu.VMEM((2, TM, D_IN), F32), pltpu.VMEM((HALO, D_POOL), F32),
                        pltpu.VMEM((SEQ, D_MODEL), F32),
                        gather_sems, gather_sems, pltpu.SemaphoreType.DMA((4,)), pltpu.SemaphoreType.DMA((2,)),
                        pltpu.SemaphoreType.DMA((2,)), pltpu.SemaphoreType.DMA((2,)),
                        pltpu.VMEM((N_DEV, 8, D_MODEL), F32), pltpu.VMEM((8, D_MODEL), F32),
                        pltpu.VMEM((N_DEV, 8, cols), F32), pltpu.VMEM((N_DEV, 8, cols), F32),
                        pltpu.VMEM(w_ada.shape, F32), seven, seven, seven, seven,
                        gather_sems, gather_sems, pltpu.SemaphoreType.DMA((2,)), pltpu.SemaphoreType.DMA((2,))],
        compiler_params=pltpu.CompilerParams(dimension_semantics=("arbitrary",), vmem_limit_bytes=VMEM_LIMIT),
    )(x, c, w_ada, b_ada, *small, ln_g, ln_b, *mine, *following)
    return res[0:4], res[4:8], res[8:10], res[10:12], res[12], res[13]


ADA_CHUNK = 256


def _grad_tail(dproj, h, cat, dy, small, dmod8, loss_lanes, w_ada, m_ada, v_ada, act_t, b_ada, m_bada, v_bada):
    shard_in, shard_out, shard_small = D_IN // N_DEV, D_MODEL // N_DEV, small.shape[2]
    cols = w_ada.shape[2]
    W_IN, W_OUT, SMALL = 0, 1, 2

    def body(dproj_hbm, h_hbm, cat_hbm, dy_hbm, small_hbm, dmod_ref, lanes_ref, wada_hbm, mada_hbm, vada_hbm,
             act_ref, bada_ref, mbada_ref, vbada_ref,
             gwin_ref, gwout_ref, stot_ref, loss_ref, gada_hbm, dada_hbm, nmada_hbm, nvada_hbm,
             gb_ref, db_ref, nmb_ref, nvb_ref,
             dproj_v, h_v, cat_v, dy_v, part_in, part_out, own_small, loss_src, loss_all, dmod_all, ada_in, ada_out,
             *rest):
        bufs, rest = rest[:13], rest[13:]
        load_sems, rs_sems = rest[0], rest[1:6]
        m_send, m_recv, g_send, g_recv, s_send, s_recv, ada_lsem, ada_ssem = rest[6:]
        mine = _index(_me())

        def update_ada():
            upper = (mine % 2) == 1

            def dmod_of(layer):
                rows = []
                for b in range(N_DEV):
                    r = dmod_all[b, pl.ds(4 * layer + mine // 2, 1), :]
                    rows.append(jnp.where(upper, r[:, cols:], r[:, :cols]))
                return jnp.concatenate(rows, axis=0)

            chunks = [(layer, c) for layer in range(DEPTH) for c in range(D_MODEL // ADA_CHUNK)]

            def loads(i):
                layer, c = chunks[i]
                rows = pl.ds(c * ADA_CHUNK, ADA_CHUNK)
                return [pltpu.make_async_copy(src.at[layer, rows], ada_in.at[i % 2, k], ada_lsem.at[i % 2, k])
                        for k, src in enumerate((wada_hbm, mada_hbm, vada_hbm))]

            def stores(i):
                layer, c = chunks[i]
                rows = pl.ds(c * ADA_CHUNK, ADA_CHUNK)
                return [pltpu.make_async_copy(ada_out.at[i % 2, k], dst.at[layer, rows], ada_ssem.at[i % 2, k])
                        for k, dst in enumerate((gada_hbm, dada_hbm, nmada_hbm, nvada_hbm))]

            for cp in loads(0):
                cp.start(priority=BULK)
            dmods = {}
            for i, (layer, c) in enumerate(chunks):
                if i + 1 < len(chunks):
                    for cp in loads(i + 1):
                        cp.start(priority=BULK)
                for cp in loads(i):
                    cp.wait()
                if i >= 2:
                    for cp in stores(i - 2):
                        cp.wait()
                if layer not in dmods:
                    dmods[layer] = dmod_of(layer)
                act = act_ref[pl.ds(c * ADA_CHUNK, ADA_CHUNK), :]
                g = act[:, 0:1] * dmods[layer][0:1, :]
                for b in range(1, N_DEV):
                    g = g + act[:, b:b + 1] * dmods[layer][b:b + 1, :]
                slot = i % 2
                delta, new_m, new_v = _adamw_math(ada_in[slot, 0], g, ada_in[slot, 1], ada_in[slot, 2])
                ada_out[slot, 0] = g
                ada_out[slot, 1] = delta
                ada_out[slot, 2] = new_m
                ada_out[slot, 3] = new_v
                for cp in stores(i):
                    cp.start(priority=BULK)
            for i in (len(chunks) - 2, len(chunks) - 1):
                for cp in stores(i):
                    cp.wait()

            total = dmod_all[0]
            for b in range(1, N_DEV):
                total = total + dmod_all[b]
            width = total.shape[1]
            for layer in range(DEPTH):
                for q in range(4):
                    gb_ref[layer:layer + 1, q * width:(q + 1) * width] = total[4 * layer + q:4 * layer + q + 1, :]
            db_ref[...], nmb_ref[...], nvb_ref[...] = _adamw_math(bada_ref[...], gb_ref[...], mbada_ref[...],
                                                                  vbada_ref[...])

        order = (ACROSS_BOTH, ACROSS_X, ACROSS_Y, None)
        chips = [_ChipReduceScatter._chip(r) for r in order]
        loads = [pltpu.make_async_copy(s, d, load_sems.at[n]) for n, (s, d) in enumerate(
            ((cat_hbm, cat_v), (dy_hbm, dy_v), (h_hbm, h_v)))]
        loads += [pltpu.make_async_copy(dproj_hbm.at[:, pl.ds(pl.multiple_of(chip * 2 * shard_in, 2 * shard_in),
                                                             2 * shard_in)], dproj_v.at[n], load_sems.at[3 + n])
                  for n, chip in enumerate(chips)]
        for cp in loads[2:] + loads[:2]:
            cp.start(priority=BULK)
        arrays = [dict(part=part_in, out=gwin_ref, staged=False, sib=bufs[0], snd=bufs[1], rcv=bufs[2], relay=bufs[3]),
                  dict(part=part_out, out=gwout_ref, staged=False, sib=bufs[4], snd=bufs[5], rcv=bufs[6],
                       relay=bufs[7]),
                  dict(part=small_hbm, out=own_small, staged=True, stage=bufs[8], sib=bufs[9], snd=bufs[10],
                       rcv=bufs[11], relay=bufs[12])]
        scatter = _ChipReduceScatter(arrays, *rs_sems)
        scatter.start([SMALL])
        dmod_all[mine] = dmod_ref[...]
        dmod_copies = _direct_exchange(lambda p: dmod_ref, lambda m: dmod_all.at[_index(m)], m_send, m_recv)
        loss_src[...] = jnp.full(loss_src.shape, (0.5 / D_MODEL) * jnp.sum(lanes_ref[...]), F32)
        loss_all[mine] = loss_src[...]
        loss_copies = _direct_exchange(lambda p: loss_src, lambda m: loss_all.at[_index(m)], s_send, s_recv)

        gather = _TwoLevelGather(stot_ref, g_send, g_recv)
        loads[2].wait()
        for n, chip in enumerate(chips):
            loads[3 + n].wait()
            res = _dot_tn(dproj_v[n], h_v[...]).astype(BF16)
            part_in[chip, 0] = res[:shard_in]
            part_in[chip, 1] = res[shard_in:]
            scatter.start([W_IN], chips=[chip])
            if n == 0:
                scatter.exchange([SMALL])
            if n == 1:
                scatter.send_far([W_IN])
                scatter.fold([SMALL])
            if n == 2:
                scatter.send_near(ACROSS_X, [W_IN])
                scatter.finish([SMALL])
                stot_ref[mine] = own_small[...]
                gather.send_mine()
            if n == 3:
                scatter.send_near(ACROSS_Y, [W_IN])

        loads[0].wait()
        loads[1].wait()
        for blk in range(2):
            res = _dot_tn(cat_v[:, blk * 512:(blk + 1) * 512], dy_v[...]).astype(BF16)
            for s in range(4):
                part_out[2 * blk + s // 2, s % 2] = res[s * shard_out:(s + 1) * shard_out]
        scatter.start([W_OUT])
        scatter.fold([W_IN])
        scatter.exchange([W_OUT])
        gather.relay()
        _wait_direct(dmod_copies)
        update_ada()
        scatter.fold([W_OUT])
        gather.pass_on()
        gather.wait_rest()
        _wait_direct(loss_copies)
        total = loss_all[0]
        for j in range(1, N_DEV):
            total = total + loss_all[j]
        loss_ref[...] = total
        scatter.finish([W_IN])
        scatter.finish([W_OUT])
        gather.wait_sends()
        scatter.wait_sends()

    buffers = _ChipReduceScatter.buffers
    comm_scratch = (buffers(shard_in, D_MODEL, BF16, staged=False) + buffers(shard_out, D_MODEL, BF16, staged=False)
                    + buffers(shard_small, 128, F32))
    comm_scratch += [pltpu.SemaphoreType.DMA((7,))] + _ChipReduceScatter.semaphores(3)
    comm_scratch += [pltpu.SemaphoreType.DMA((n,)) for n in (7, 7, GATHER_SEMS, GATHER_SEMS, 7, 7)]
    comm_scratch += [pltpu.SemaphoreType.DMA((2, 3)), pltpu.SemaphoreType.DMA((2, 4))]
    return pl.pallas_call(
        body,
        name="grad_tail",
        in_specs=[ANY] * 5 + [VMEM, VMEM] + [ANY] * 3 + [VMEM] * 4,
        out_specs=[VMEM] * 4 + [ANY] * 4 + [VMEM] * 4,
        out_shape=[jax.ShapeDtypeStruct((shard_in, D_MODEL), F32), jax.ShapeDtypeStruct((shard_out, D_MODEL), F32),
                   jax.ShapeDtypeStruct((N_DEV, shard_small, 128), F32), jax.ShapeDtypeStruct((8, 128), F32)]
                  + [jax.ShapeDtypeStruct(w_ada.shape, F32)] * 4 + [jax.ShapeDtypeStruct(b_ada.shape, F32)] * 4,
        scratch_shapes=[pltpu.VMEM((4, SEQ, 2 * shard_in), BF16), pltpu.VMEM(h.shape, BF16), pltpu.VMEM(cat.shape, BF16),
                        pltpu.VMEM(dy.shape, BF16), pltpu.VMEM((4, 2, shard_in, D_MODEL), BF16),
                        pltpu.VMEM((4, 2, shard_out, D_MODEL), BF16), pltpu.VMEM((shard_small, 128), F32),
                        pltpu.VMEM((8, 128), F32), pltpu.VMEM((N_DEV, 8, 128), F32),
                        pltpu.VMEM((N_DEV,) + dmod8.shape, F32), pltpu.VMEM((2, 3, ADA_CHUNK, cols), F32),
                        pltpu.VMEM((2, 4, ADA_CHUNK, cols), F32)] + comm_scratch,
        compiler_params=pltpu.CompilerParams(vmem_limit_bytes=VMEM_LIMIT),
    )(dproj, h, cat, dy, small, dmod8, loss_lanes, w_ada, m_ada, v_ada, act_t, b_ada, m_bada, v_bada)


SMALL_NAMES = ("w_pool", "w_sgu", "pool_scale", "sgu_ln_g", "sgu_ln_b", "b_sgu", "ln_g", "ln_b")
SMALL_ROWS = (512, 512, 4, 4, 4, 4, 8, 8)


def _adamw_small(g_packed, ws, ms, vs, name):
    n = len(SMALL_NAMES)

    def body(g_ref, *refs):
        w_refs, m_refs, v_refs = refs[:n], refs[n:2 * n], refs[2 * n:3 * n]
        outs = refs[3 * n:]

        def update(p, at, g):
            delta, new_m, new_v = _adamw_math(w_refs[p][at], g, m_refs[p][at], v_refs[p][at])
            outs[p][at] = g
            outs[n + p][at] = delta
            outs[2 * n + p][at] = new_m
            outs[3 * n + p][at] = new_v

        row = 0
        for p, r in enumerate(SMALL_ROWS):
            shape = ws[p].shape
            for layer in range(DEPTH):
                first = layer * PACK_ROWS + row
                if len(shape) == 4:
                    for k in range(shape[1]):
                        update(p, (layer, k), g_ref[first + k * shape[2]:first + (k + 1) * shape[2], :])
                elif len(shape) == 3:
                    update(p, (layer,), g_ref[first:first + r, :])
                else:
                    g = jnp.concatenate([g_ref[first + k:first + k + 1, :] for k in range(r)], axis=1)
                    update(p, (slice(layer, layer + 1), slice(None)), g)
            row += r

    res = pl.pallas_call(
        body,
        name=name,
        out_shape=[jax.ShapeDtypeStruct(w.shape, F32) for w in ws] * 4,
        compiler_params=pltpu.CompilerParams(vmem_limit_bytes=VMEM_LIMIT),
    )(g_packed, *ws, *ms, *vs)
    return res[:n], res[n:2 * n], res[2 * n:3 * n], res[3 * n:]


def kernel(x, c, w_ada, b_ada, w_in, w_pool, pool_scale, sgu_ln_g, sgu_ln_b, w_sgu, b_sgu, w_out, ln_g, ln_b, loss_target, m_w_ada, m_b_ada, m_w_in, m_w_pool, m_pool_scale, m_sgu_ln_g, m_sgu_ln_b, m_w_sgu, m_b_sgu, m_w_out, m_ln_g, m_ln_b, v_w_ada, v_b_ada, v_w_in, v_w_pool, v_pool_scale, v_sgu_ln_g, v_sgu_ln_b, v_w_sgu, v_b_sgu, v_w_out, v_ln_g, v_ln_b):
    small_w = dict(w_pool=w_pool, w_sgu=w_sgu, pool_scale=pool_scale, sgu_ln_g=sgu_ln_g, sgu_ln_b=sgu_ln_b,
                   b_sgu=b_sgu, ln_g=ln_g, ln_b=ln_b)
    small_m = dict(w_pool=m_w_pool, w_sgu=m_w_sgu, pool_scale=m_pool_scale, sgu_ln_g=m_sgu_ln_g,
                   sgu_ln_b=m_sgu_ln_b, b_sgu=m_b_sgu, ln_g=m_ln_g, ln_b=m_ln_b)
    small_v = dict(w_pool=v_w_pool, w_sgu=v_w_sgu, pool_scale=v_pool_scale, sgu_ln_g=v_sgu_ln_g,
                   sgu_ln_b=v_sgu_ln_b, b_sgu=v_b_sgu, ln_g=v_ln_g, ln_b=v_ln_b)

    wint_loc = jnp.transpose(w_in, (0, 2, 1)).astype(BF16)
    wout_loc = w_out.astype(BF16)
    small = (w_pool, pool_scale, sgu_ln_g, sgu_ln_b, w_sgu, jnp.transpose(b_sgu, (0, 2, 1)))
    (out0, y0, cdf0, proj0), (cur, y1, cdf1, proj1), gathered0, gathered1, act_all, mod = _forward(
        x[0], c, w_ada, b_ada, small, ln_g, ln_b, [wint_loc[0], wout_loc[0]], [wint_loc[1], wout_loc[1]])
    w_int, w_outf = [gathered0[0], gathered1[0]], [gathered0[1], gathered1[1]]
    acts = [(x[0], proj0, y0, cdf0), (out0, proj1, y1, cdf1)]

    shard_in, shard_out = D_IN // N_DEV, D_MODEL // N_DEV
    a, b = cur, loss_target[0]
    loss_lanes, carry, pending = None, (), []
    g_w_in_t, g_w_out = [None] * DEPTH, [None] * DEPTH
    for l in reversed(range(DEPTH)):
        dx, dproj, h, cat, dy, small_grads, dmod8, lanes, *shards = _layer_backward(
            l, a, b, *acts[l], mod, w_int[l], w_outf[l], small, ln_g, l == DEPTH - 1, f"layer_bwd_{l}",
            carry=carry, reduce=pending)
        if shards:
            g_w_in_t[l + 1], g_w_out[l + 1] = shards
        if l == DEPTH - 1:
            loss_lanes = lanes
        if l > 0:
            g_in, g_out = _grad_matmuls(dproj, h, cat, dy, f"grad_w_{l}")
            pending = [g_in.reshape(4, 2, shard_in, D_MODEL), g_out.reshape(4, 2, shard_out, D_MODEL)]
        carry = (small_grads, dmod8)
        a = b = dx

    (g_w_in_t[0], g_w_out[0], small_tot, loss_tile, g_w_ada, d_w_ada, nm_w_ada, nv_w_ada,
     g_b_ada, d_b_ada, nm_b_ada, nv_b_ada) = _grad_tail(
        dproj, h, cat, dy, small_grads.reshape(4, 2, DEPTH * PACK_ROWS // N_DEV, 128), dmod8, loss_lanes,
        w_ada, m_w_ada, v_w_ada, jnp.transpose(act_all), b_ada, m_b_ada, v_b_ada)
    loss = loss_tile[0, 0]

    flat = lambda t: t.reshape(-1, t.shape[-1])
    to_t = lambda t: flat(jnp.transpose(t, (0, 2, 1)))
    from_t = lambda t: jnp.transpose(t.reshape(DEPTH, shard_in, D_MODEL), (0, 2, 1))
    *updated, grad_x = _adamw(to_t(w_in), g_w_in_t, to_t(m_w_in), to_t(v_w_in), shard_in // 2, "adamw_w_in",
                              ride=[a])
    grad_x = grad_x[None]
    g_w_in, d_w_in, nm_w_in, nv_w_in = [from_t(t) for t in updated]
    gwout, d_w_out, nm_w_out, nv_w_out = [t.reshape(w_out.shape) for t in _adamw(
        flat(w_out), g_w_out, flat(m_w_out), flat(v_w_out), shard_out, "adamw_w_out")]
    small_out = _adamw_small(small_tot.reshape(DEPTH * PACK_ROWS, 128), [small_w[n] for n in SMALL_NAMES],
                             [small_m[n] for n in SMALL_NAMES], [small_v[n] for n in SMALL_NAMES], "adamw_small")
    gs, ds, ms, vs = [dict(zip(SMALL_NAMES, group)) for group in small_out]

    def ordered(w_ada_, b_ada_, w_in_, small, w_out_):
        return (w_ada_, b_ada_, w_in_, small["w_pool"], small["pool_scale"], small["sgu_ln_g"], small["sgu_ln_b"],
                small["w_sgu"], small["b_sgu"], w_out_, small["ln_g"], small["ln_b"])

    return (loss, grad_x,
            *ordered(g_w_ada, g_b_ada, g_w_in, gs, gwout),
            *ordered(d_w_ada, d_b_ada, d_w_in, ds, d_w_out),
            *ordered(nm_w_ada, nm_b_ada, nm_w_in, ms, nm_w_out),
            *ordered(nv_w_ada, nv_b_ada, nv_w_in, vs, nv_w_out))
```

```python
import jax
import jax.numpy as jnp
from jax import lax
from jax.experimental import pallas as pl
from jax.experimental.pallas import tpu as pltpu

F32 = jnp.float32
BF16 = jnp.bfloat16

D_MODEL = 1024
SEQ = 2048
DEPTH = 2
D_POOL = 512
D_SGU = 512
D_IN = 2560
N_GROUPS = 4
GROUP = 128
N_HEADS = 4
HEAD = 128
CHUNK = 128
WINDOWS = (2, 4, 8, 16)
ALPHA = (2.0 * DEPTH) ** 0.25
LN_EPS = 1e-5
N_DEV = 8

ADAM_LR = 0.001
ADAM_B1 = 0.9
ADAM_B2 = 0.999
ADAM_EPS = 1e-08
ADAM_WD = 0.01
ADAM_STEP = 10

TM = 256
HALO = 16
N_TILES = SEQ // TM
VMEM_LIMIT = 60 * 1024 * 1024

ROW_WPOOL = 0
ROW_WSGU = 512
ROW_PSCALE = 1024
ROW_SLNG = 1028
ROW_SLNB = 1032
ROW_BSGU = 1036
ROW_LNG = 1040
ROW_LNB = 1048
PACK_ROWS = 1088
DMOD_COLS = DEPTH * 3 * D_MODEL // 8

SQRT_HALF = 0.7071067811865476
INV_SQRT_2PI = 0.3989422804014327


def _ln(x):
    mu = jnp.mean(x, axis=-1, keepdims=True)
    xc = x - mu
    var = jnp.mean(xc * xc, axis=-1, keepdims=True)
    rstd = lax.rsqrt(var + LN_EPS)
    return xc * rstd, rstd


def _ln_bwd(dxn, xn, rstd):
    m1 = jnp.mean(dxn, axis=-1, keepdims=True)
    m2 = jnp.mean(dxn * xn, axis=-1, keepdims=True)
    return rstd * (dxn - m1 - xn * m2)


def _normal_cdf(x):
    return 0.5 * (1.0 + lax.erf(x * SQRT_HALF))


def _gelu_parts(x, cdf, with_grad):
    if not with_grad:
        return x * cdf, None
    return x * cdf, cdf + x * (INV_SQRT_2PI * jnp.exp(-0.5 * x * x))


def _silu_parts(x):
    s = jax.nn.sigmoid(x)
    return x * s, s * (1.0 + x * (1.0 - s))


def _dot(a, b):
    return lax.dot_general(a, b, (((1,), (0,)), ((), ())), preferred_element_type=F32)


def _dot_nt(a, b):
    return lax.dot_general(a, b, (((1,), (1,)), ((), ())), preferred_element_type=F32)


def _dot_tn(a, b):
    return lax.dot_general(a, b, (((0,), (0,)), ((), ())), preferred_element_type=F32)


def _row_index(tile):
    return tile * TM + lax.broadcasted_iota(jnp.int32, (TM, 1), 0)


def _window_sums(ext, forward):
    n = TM + HALO
    cur = ext
    outs = []
    for g in range(N_GROUPS):
        step = 1 << g
        cur = cur + pltpu.roll(cur, step if forward else n - step, 0)
        rows = cur[HALO:, :GROUP] if forward else cur[:TM, :GROUP]
        outs.append(rows)
        cur = cur[:, GROUP:] if g + 1 < N_GROUPS else None
    return outs


def _inverse_counts(rows):
    return [1.0 / jnp.minimum(rows + 1, w).astype(F32) for w in WINDOWS]


def _tril_bf16(w):
    t = lax.broadcasted_iota(jnp.int32, (CHUNK, CHUNK), 0)
    s = lax.broadcasted_iota(jnp.int32, (CHUNK, CHUNK), 1)
    return jnp.where(t >= s, w, 0.0).astype(BF16)


class _MixWeights:
    def __init__(self, layer, wpool_ref, pscale_ref, slng_ref, slnb_ref, wsgu_ref, bsgut_ref):
        self.layer = layer
        self.wpool_ref, self.pscale_ref, self.slng_ref, self.slnb_ref = wpool_ref, pscale_ref, slng_ref, slnb_ref
        self.wsgu_ref, self.bsgut_ref = wsgu_ref, bsgut_ref

    def pool(self, g):
        return self.wpool_ref[self.layer, g].astype(BF16)

    def pool_scale(self, g):
        return self.pscale_ref[self.layer:self.layer + 1, g * GROUP:(g + 1) * GROUP]

    def ln_gain(self, h):
        return self.slng_ref[self.layer, h:h + 1, :]

    def ln_bias(self, h):
        return self.slnb_ref[self.layer, h:h + 1, :]

    def mix(self, h):
        return _tril_bf16(self.wsgu_ref[self.layer, h])

    def mix_bias(self, h):
        return self.bsgut_ref[self.layer, :, h:h + 1]


SMALL_SPECS = ((DEPTH, N_GROUPS, GROUP, GROUP), (DEPTH, D_POOL), (DEPTH, N_HEADS, HEAD), (DEPTH, N_HEADS, HEAD),
               (DEPTH, N_HEADS, CHUNK, CHUNK), (DEPTH, CHUNK, N_HEADS))


def _mix_forward(proj, halo, tile, w, cdf=None):
    keep = cdf is not None
    rows = _row_index(tile)
    inv_counts = _inverse_counts(rows)
    xa = proj[:, 0:D_POOL]
    ga = proj[:, D_POOL:2 * D_POOL]
    sums = _window_sums(jnp.concatenate([halo, xa], axis=0), True)
    ga_act, ga_grad = _silu_parts(ga)
    pooled, pw, ya = [], [], []
    for g in range(N_GROUPS):
        sl = slice(g * GROUP, (g + 1) * GROUP)
        p = (sums[g] * inv_counts[g] - xa[:, sl]).astype(BF16)
        q = _dot(p, w.pool(g))
        pooled.append(p)
        pw.append(q)
        ya.append(q * w.pool_scale(g) * ga_act[:, sl])

    u = proj[:, 2 * D_POOL:2 * D_POOL + D_SGU]
    v = proj[:, 2 * D_POOL + D_SGU:2 * D_POOL + 2 * D_SGU]
    gb = proj[:, 2 * D_POOL + 2 * D_SGU:]
    gb_act, gb_grad = _silu_parts(gb)
    if cdf is None:
        cdf = jnp.concatenate([_normal_cdf(u), _normal_cdf(v)], axis=1)
    u_act, u_grad = _gelu_parts(u, cdf[:, :D_SGU], keep)
    v_act, v_grad = _gelu_parts(v, cdf[:, D_SGU:], keep)
    vn, vrstd, vln, mixed, yb = [], [], [], [], []
    for h in range(N_HEADS):
        sl = slice(h * HEAD, (h + 1) * HEAD)
        n_h, r_h = _ln(v_act[:, sl])
        l_h = (n_h * w.ln_gain(h) + w.ln_bias(h)).astype(BF16)
        w_h = w.mix(h)
        bias = w.mix_bias(h)
        m_h = jnp.concatenate(
            [_dot(w_h, l_h[k * CHUNK:(k + 1) * CHUNK]) + bias for k in range(TM // CHUNK)], axis=0)
        vn.append(n_h)
        vrstd.append(r_h)
        vln.append(l_h)
        mixed.append(m_h)
        yb.append(u_act[:, sl] * m_h * gb_act[:, sl])
    cat = jnp.concatenate(ya + yb, axis=1)
    if not keep:
        return cat, cdf
    return cat, dict(inv_counts=inv_counts, ga_act=ga_act, ga_grad=ga_grad, pooled=pooled, pw=pw, u_grad=u_grad,
                     v_grad=v_grad, u_act=u_act, gb_act=gb_act, gb_grad=gb_grad, vn=vn, vrstd=vrstd, vln=vln,
                     mixed=mixed)


def _const_spec(shape):
    nd = len(shape)
    return pl.BlockSpec(shape, lambda i: (0,) * nd)


VEC_LNG, VEC_LNB, VEC_POOL, VEC_SGU, VEC_SHIFT, VEC_SCALE, VEC_GATE, VEC_LOSS = range(8)


def _layer_backward(layer, a, b, x, proj, y, cdf, mod, w_int, w_outf, small, ln_g, is_last, name, carry=(),
                    reduce=()):
    n_red, n_carry = len(reduce), len(carry)
    base = layer * PACK_ROWS

    def body(a_ref, b_ref, x_ref, proj_ref, prev_ref, y_ref, cdf_ref, mod_ref, wint_ref, wout_ref, wpool_ref,
             pscale_ref, slng_ref, slnb_ref, wsgu_ref, bsgut_ref, lng_ref, *rest):
        weights = _MixWeights(layer, wpool_ref, pscale_ref, slng_ref, slnb_ref, wsgu_ref, bsgut_ref)
        carry_refs, rest = rest[:n_carry], rest[n_carry:]
        part_refs, rest = rest[:n_red], rest[n_red:]
        dx_ref, dproj_ref, h_ref, cat_ref, dy_ref, small_ref, dmod_ref, loss_ref = rest[:8]
        shard_refs, rest = rest[8:8 + n_red], rest[8 + n_red:]
        vec_ref, dmix_ref, halo_ref = rest[:3]
        step = pl.program_id(0)
        tile = N_TILES - 1 - step

        def scatter():
            bufs, sems = rest[3:3 + 5 * n_red], rest[3 + 5 * n_red:]
            arrays = [dict(part=part_refs[n], out=shard_refs[n], staged=True, stage=bufs[5 * n], sib=bufs[5 * n + 1],
                           snd=bufs[5 * n + 2], rcv=bufs[5 * n + 3], relay=bufs[5 * n + 4]) for n in range(n_red)]
            return _ChipReduceScatter(arrays, *sems)

        @pl.when(step == 0)
        def _():
            small_ref[...] = jnp.zeros_like(small_ref)
            dmod_ref[...] = jnp.zeros_like(dmod_ref)
            vec_ref[...] = jnp.zeros_like(vec_ref)
            dmix_ref[...] = jnp.zeros_like(dmix_ref)
            halo_ref[...] = jnp.zeros_like(halo_ref)
            if n_red:
                scatter().start()

        if n_red:
            @pl.when(step == 1)
            def _():
                scatter().exchange()

            @pl.when(step == N_TILES // 2)
            def _():
                scatter().fold()

        def acc(row, lo, val):
            hi = lo + val.shape[1]
            vec_ref[row:row + 1, lo:hi] += jnp.sum(val, axis=0, keepdims=True)

        xt = x_ref[...]
        yt = y_ref[...]
        shift = mod_ref[layer:layer + 1, 0:D_MODEL]
        scale = mod_ref[layer:layer + 1, D_MODEL:2 * D_MODEL]
        gate = mod_ref[layer:layer + 1, 2 * D_MODEL:]
        ln_gain = lng_ref[layer:layer + 1, :]

        zn, zrstd = _ln(ALPHA * xt + gate * yt)
        if is_last:
            diff = a_ref[...] - b_ref[...]
            acc(VEC_LOSS, 0, diff * diff)
            dout = diff * (1.0 / D_MODEL)
        else:
            dout = a_ref[...]
        acc(VEC_LNG, 0, dout * zn)
        acc(VEC_LNB, 0, dout)
        dz = _ln_bwd(dout * ln_gain, zn, zrstd)
        acc(VEC_GATE, 0, dz * yt)
        dy = (dz * gate).astype(BF16)
        dy_ref[...] = dy
        dcat = _dot_nt(dy, wout_ref[...])

        proj = proj_ref[...]
        prev = jnp.where(tile > 0, prev_ref[...], 0.0)
        cat, k = _mix_forward(proj, prev, tile, weights, cdf_ref[...])
        cat_ref[...] = cat.astype(BF16)

        dga, dq = [], []
        for g in range(N_GROUPS):
            sl = slice(g * GROUP, (g + 1) * GROUP)
            pscale = weights.pool_scale(g)
            dya = dcat[:, sl]
            dyp = dya * k["ga_act"][:, sl]
            dga.append(dya * k["pw"][g] * pscale * k["ga_grad"][:, sl])
            acc(VEC_POOL, g * GROUP, dyp * k["pw"][g])
            dpw = (dyp * pscale).astype(BF16)
            rows = pl.ds(base + ROW_WPOOL + g * GROUP, GROUP)
            small_ref[rows, :] += _dot_tn(k["pooled"][g], dpw)
            dq.append(_dot_nt(dpw, weights.pool(g)))
        dpooled = jnp.concatenate(dq, axis=1)
        scaled = jnp.concatenate([dq[g] * k["inv_counts"][g] for g in range(N_GROUPS)], axis=1)
        sums = _window_sums(jnp.concatenate([scaled, halo_ref[...]], axis=0), False)
        halo_ref[...] = scaled[0:HALO]
        dxa = jnp.concatenate(sums, axis=1) - dpooled

        du, dv, dgb = [], [], []
        for h in range(N_HEADS):
            sl = slice(h * HEAD, (h + 1) * HEAD)
            dyb = dcat[:, D_POOL + h * HEAD:D_POOL + (h + 1) * HEAD]
            m_h = k["mixed"][h]
            ug = k["u_act"][:, sl] * dyb
            du.append(dyb * m_h * k["gb_act"][:, sl] * k["u_grad"][:, sl])
            dgb.append(ug * m_h * k["gb_grad"][:, sl])
            dmixed = ug * k["gb_act"][:, sl]
            dmixed_bf = dmixed.astype(BF16)
            w_h = weights.mix(h)
            dvln_parts = []
            dmix_sum = dmix_ref[h]
            wsgu_rows = pl.ds(base + ROW_WSGU + h * CHUNK, CHUNK)
            dws = small_ref[wsgu_rows, :]
            for c in range(TM // CHUNK):
                cs = slice(c * CHUNK, (c + 1) * CHUNK)
                dmix_sum = dmix_sum + dmixed[cs]
                dws = dws + _dot_nt(dmixed_bf[cs], k["vln"][h][cs])
                dvln_parts.append(_dot_tn(w_h, dmixed_bf[cs]))
            dmix_ref[h] = dmix_sum
            small_ref[wsgu_rows, :] = dws
            dvln = jnp.concatenate(dvln_parts, axis=0)
            acc(VEC_SGU, h * HEAD, dvln * k["vn"][h])
            acc(VEC_SGU, D_SGU + h * HEAD, dvln)
            dvv = _ln_bwd(dvln * weights.ln_gain(h), k["vn"][h], k["vrstd"][h])
            dv.append(dvv * k["v_grad"][:, sl])

        dproj = jnp.concatenate([dxa] + dga + du + dv + dgb, axis=1).astype(BF16)
        dproj_ref[...] = dproj
        dh = _dot(dproj, wint_ref[...])

        xn, xrstd = _ln(xt)
        h_ref[...] = (xn * (1.0 + scale) + shift).astype(BF16)
        acc(VEC_SCALE, 0, dh * xn)
        acc(VEC_SHIFT, 0, dh)
        dx_ref[...] = _ln_bwd(dh * (1.0 + scale), xn, xrstd) + ALPHA * dz

        @pl.when(step == N_TILES - 1)
        def _():
            def put(row0, vec_row, lo, n):
                for r in range(n):
                    small_ref[base + row0 + r:base + row0 + r + 1, :] = (
                        vec_ref[vec_row:vec_row + 1, lo + r * 128:lo + (r + 1) * 128])

            put(ROW_PSCALE, VEC_POOL, 0, 4)
            put(ROW_SLNG, VEC_SGU, 0, 4)
            put(ROW_SLNB, VEC_SGU, D_SGU, 4)
            put(ROW_LNG, VEC_LNG, 0, 8)
            put(ROW_LNB, VEC_LNB, 0, 8)
            ones = jnp.ones((8, HEAD), F32)
            t = lax.broadcasted_iota(jnp.int32, (CHUNK, CHUNK), 0)
            s = lax.broadcasted_iota(jnp.int32, (CHUNK, CHUNK), 1)
            for h in range(N_HEADS):
                bias_rows = lax.dot_general(ones, dmix_ref[h], (((1,), (1,)), ((), ())),
                                            preferred_element_type=F32, precision=lax.Precision.HIGHEST)
                small_ref[base + ROW_BSGU + h:base + ROW_BSGU + h + 1, :] = bias_rows[0:1]
                rows = pl.ds(base + ROW_WSGU + h * CHUNK, CHUNK)
                small_ref[rows, :] = jnp.where(t >= s, small_ref[rows, :], 0.0)
            pieces = ((0, VEC_SHIFT, 0, 768),
                      (1, VEC_SHIFT, 768, 256), (1, VEC_SCALE, 0, 512),
                      (2, VEC_SCALE, 512, 512), (2, VEC_GATE, 0, 256),
                      (3, VEC_GATE, 256, 768))
            filled = [0] * 4
            for q, vec_row, lo, n in pieces:
                row = 4 * layer + q
                dmod_ref[row:row + 1, filled[q]:filled[q] + n] = vec_ref[vec_row:vec_row + 1, lo:lo + n]
                filled[q] += n
            if n_carry:
                for other in range(layer + 1, DEPTH):
                    rows = pl.ds(other * PACK_ROWS, PACK_ROWS)
                    small_ref[rows, :] = carry_refs[0][rows, :]
                    dmod_ref[4 * other:4 * other + 4, :] = carry_refs[1][4 * other:4 * other + 4, :]
            loss_ref[...] = vec_ref[VEC_LOSS:VEC_LOSS + 1, :]
            if n_red:
                scatter().finish()
                scatter().wait_sends()

    rev = lambda w: pl.BlockSpec((TM, w), lambda i: (N_TILES - 1 - i, 0))
    prev_spec = pl.BlockSpec(
        (HALO, D_POOL), lambda i: (jnp.maximum((N_TILES - 1 - i) * (TM // HALO) - 1, 0), 0))
    comm_scratch = []
    for p in reduce:
        comm_scratch += _ChipReduceScatter.buffers(p.shape[2], p.shape[3], p.dtype)
    if n_red:
        comm_scratch += _ChipReduceScatter.semaphores(n_red)
    return pl.pallas_call(
        body,
        name=name,
        grid=(N_TILES,),
        in_specs=[rev(D_MODEL), rev(D_MODEL) if is_last else pl.BlockSpec((TM, D_MODEL), lambda i: (0, 0)),
                  rev(D_MODEL), rev(D_IN), prev_spec, rev(D_MODEL), rev(2 * D_SGU),
                  _const_spec((DEPTH, 3 * D_MODEL)), _const_spec((D_IN, D_MODEL)), _const_spec((D_MODEL, D_MODEL))]
                 + [_const_spec(s) for s in SMALL_SPECS] + [_const_spec((DEPTH, D_MODEL))]
                 + [_const_spec(c.shape) for c in carry] + [ANY] * n_red,
        out_specs=[rev(D_MODEL), rev(D_IN), rev(D_MODEL), rev(D_MODEL), rev(D_MODEL),
                   _const_spec((DEPTH * PACK_ROWS, 128)), _const_spec((8, DMOD_COLS)), _const_spec((1, D_MODEL))]
                  + [_const_spec(p.shape[2:]) for p in reduce],
        out_shape=[jax.ShapeDtypeStruct((SEQ, D_MODEL), F32), jax.ShapeDtypeStruct((SEQ, D_IN), BF16),
                   jax.ShapeDtypeStruct((SEQ, D_MODEL), BF16), jax.ShapeDtypeStruct((SEQ, D_MODEL), BF16),
                   jax.ShapeDtypeStruct((SEQ, D_MODEL), BF16), jax.ShapeDtypeStruct((DEPTH * PACK_ROWS, 128), F32),
                   jax.ShapeDtypeStruct((8, DMOD_COLS), F32), jax.ShapeDtypeStruct((1, D_MODEL), F32)]
                  + [jax.ShapeDtypeStruct(p.shape[2:], F32) for p in reduce],
        scratch_shapes=[pltpu.VMEM((8, D_MODEL), F32), pltpu.VMEM((N_HEADS, CHUNK, HEAD), F32),
                        pltpu.VMEM((HALO, D_POOL), F32)] + comm_scratch,
        compiler_params=pltpu.CompilerParams(dimension_semantics=("arbitrary",), vmem_limit_bytes=VMEM_LIMIT),
    )(a, b, x, proj, proj, y, cdf, mod, w_int, w_outf, *small, ln_g, *carry, *reduce)


def _grad_matmuls(dproj, h, cat, dy, name):
    in_cols, out_cols = D_IN // 4, D_MODEL // 2
    in_steps = D_IN // in_cols

    def body(dproj_ref, h_ref, cat_ref, dy_ref, gin_ref, gout_ref):
        step = pl.program_id(0)

        @pl.when(step < in_steps)
        def _():
            gin_ref[...] = _dot_tn(dproj_ref[...], h_ref[...]).astype(BF16)

        @pl.when(step >= in_steps)
        def _():
            gout_ref[...] = _dot_tn(cat_ref[...], dy_ref[...]).astype(BF16)

    in_block = lambda j: jnp.minimum(j, in_steps - 1)
    out_block = lambda j: jnp.maximum(j - in_steps, 0)
    return pl.pallas_call(
        body,
        name=name,
        grid=(in_steps + D_MODEL // out_cols,),
        in_specs=[pl.BlockSpec((SEQ, in_cols), lambda j: (0, in_block(j))), _const_spec((SEQ, D_MODEL)),
                  pl.BlockSpec((SEQ, out_cols), lambda j: (0, out_block(j))), _const_spec((SEQ, D_MODEL))],
        out_specs=[pl.BlockSpec((in_cols, D_MODEL), lambda j: (in_block(j), 0)),
                   pl.BlockSpec((out_cols, D_MODEL), lambda j: (out_block(j), 0))],
        out_shape=[jax.ShapeDtypeStruct((D_IN, D_MODEL), BF16), jax.ShapeDtypeStruct((D_MODEL, D_MODEL), BF16)],
        compiler_params=pltpu.CompilerParams(dimension_semantics=("arbitrary",), vmem_limit_bytes=VMEM_LIMIT),
    )(dproj, h, cat, dy)


def _adamw_math(w, g, m, v):
    m = ADAM_B1 * m + (1.0 - ADAM_B1) * g
    v = ADAM_B2 * v + (1.0 - ADAM_B2) * (g * g)
    m_hat = m / (1.0 - ADAM_B1 ** ADAM_STEP)
    v_hat = v / (1.0 - ADAM_B2 ** ADAM_STEP)
    delta = -ADAM_LR * (m_hat / (jnp.sqrt(v_hat) + ADAM_EPS) + ADAM_WD * w)
    return delta, m, v


def _adamw(w, grads, m, v, block_rows, name, ride=()):
    rows, cols = grads[0].shape
    blocks = rows // block_rows
    n_ride = len(ride)

    def body(w_ref, m_ref, v_ref, *rest):
        g_refs, rest = rest[:DEPTH], rest[DEPTH:]
        ride_in, (g_ref, d_ref, nm_ref, nv_ref), ride_out = rest[:n_ride], rest[n_ride:n_ride + 4], rest[n_ride + 4:]
        for src, dst in zip(ride_in, ride_out):
            dst[...] = src[...]
        for layer in range(DEPTH):
            @pl.when(pl.program_id(0) == layer)
            def _():
                g = g_refs[layer][...]
                g_ref[...] = g
                d_ref[...], nm_ref[...], nv_ref[...] = _adamw_math(w_ref[...], g, m_ref[...], v_ref[...])

    def grad_spec(layer):
        return pl.BlockSpec((block_rows, cols),
                            lambda l, i: (jnp.where(l == layer, i, jnp.where(l < layer, 0, blocks - 1)), 0))

    spec = pl.BlockSpec((block_rows, cols), lambda l, i: (l * blocks + i, 0))
    ride_specs = [pl.BlockSpec((r.shape[0] // (DEPTH * blocks), r.shape[1]), lambda l, i: (l * blocks + i, 0))
                  for r in ride]
    return pl.pallas_call(
        body,
        name=name,
        grid=(DEPTH, blocks),
        in_specs=[spec] * 3 + [grad_spec(layer) for layer in range(DEPTH)] + ride_specs,
        out_specs=[spec] * 4 + ride_specs,
        out_shape=[jax.ShapeDtypeStruct(w.shape, F32)] * 4 + [jax.ShapeDtypeStruct(r.shape, r.dtype) for r in ride],
        compiler_params=pltpu.CompilerParams(dimension_semantics=("arbitrary", "arbitrary"),
                                             vmem_limit_bytes=VMEM_LIMIT),
    )(w, m, v, *grads, *ride)


MESH = pl.DeviceIdType.MESH
SIBLING = 1
ANY = pl.BlockSpec(memory_space=pl.ANY)
VMEM = pl.BlockSpec(memory_space=pltpu.VMEM)
BULK = 1


def _me():
    return lax.axis_index("x"), lax.axis_index("y"), lax.axis_index("c")


def _peer(r):
    x, y, c = _me()
    return (1 - x if r & 4 else x, 1 - y if r & 2 else y, 1 - c if r & 1 else c)


def _index(dev):
    return 4 * dev[0] + 2 * dev[1] + dev[2]


def _remote(src, dst, send_sem, recv_sem, dev):
    return pltpu.make_async_remote_copy(src_ref=src, dst_ref=dst, send_sem=send_sem, recv_sem=recv_sem,
                                        device_id=dev, device_id_type=MESH)


ACROSS_X, ACROSS_Y, ACROSS_BOTH = 4, 2, 6
GATHER_SEMS = 11


class _TwoLevelGather:
    def __init__(self, out, send_sems, recv_sems, src=None):
        self.out, self.send_sems, self.recv_sems, self.src = out, send_sems, recv_sems, src
        self.rows = (out.shape[0] // N_DEV) if len(out.shape) == 2 else out.shape[1]
        self.half = self.rows // 2

    def _slot(self, block):
        if len(self.out.shape) == 2:
            return self.out.at[pl.ds(pl.multiple_of(_index(block) * self.rows, self.rows), self.rows)]
        return self.out.at[_index(block)]

    def _copy(self, k, block, part, to, src=None):
        slot = self._slot(block)
        if part is not None:
            rows = pl.ds(part * self.half, self.half)
            slot = slot.at[rows]
            src = None if src is None else src.at[rows]
        return _remote(slot if src is None else src, slot, self.send_sems.at[k], self.recv_sems.at[k], to)

    def _mine(self):
        me = _me()
        src = self._slot(me) if self.src is None else self.src
        x, y = _peer(ACROSS_X), _peer(ACROSS_Y)
        return [self._copy(1, me, 0, x, src), self._copy(3, me, 1, y, src), self._copy(0, me, None, _peer(SIBLING), src),
                self._copy(2, me, 1, x, src), self._copy(4, me, 0, y, src)]

    def _relayed(self):
        return [self._copy(5, _peer(ACROSS_X), 0, _peer(ACROSS_Y)), self._copy(6, _peer(ACROSS_Y), 1, _peer(ACROSS_X))]

    def _passed(self):
        sib, far = _peer(SIBLING), _peer(ACROSS_BOTH)
        return [self._copy(7, _peer(ACROSS_X), None, sib), self._copy(8, _peer(ACROSS_Y), None, sib),
                self._copy(9, far, 0, sib), self._copy(10, far, 1, sib)]

    def _arrival(self, k, r, part):
        return self._copy(k, _peer(r), part, _me())

    def send_first(self):
        for cp in self._mine()[:3]:
            cp.start()

    def send_second(self):
        for cp in self._mine()[3:]:
            cp.start()

    def send_mine(self):
        self.send_first()
        self.send_second()

    def relay(self):
        relayed = self._relayed()
        self._arrival(1, ACROSS_X, 0).wait_recv()
        relayed[0].start()
        self._arrival(3, ACROSS_Y, 1).wait_recv()
        relayed[1].start()

    def pass_near(self):
        passed = self._passed()
        self._arrival(2, ACROSS_X, 1).wait_recv()
        passed[0].start()
        self._arrival(4, ACROSS_Y, 0).wait_recv()
        passed[1].start()

    def pass_far(self):
        passed = self._passed()
        self._arrival(5, ACROSS_BOTH, 0).wait_recv()
        passed[2].start()
        self._arrival(6, ACROSS_BOTH, 1).wait_recv()
        passed[3].start()

    def pass_on(self):
        self.pass_near()
        self.pass_far()

    def wait_sibling(self):
        self._arrival(0, SIBLING, None).wait_recv()

    def wait_passed(self, r):
        if r == ACROSS_BOTH:
            self._arrival(9, r ^ SIBLING, 0).wait_recv()
            self._arrival(10, r ^ SIBLING, 1).wait_recv()
        else:
            self._arrival(7 if r == ACROSS_X else 8, r ^ SIBLING, None).wait_recv()

    def wait_rest(self):
        self.wait_sibling()
        for r in (ACROSS_X, ACROSS_Y, ACROSS_BOTH):
            self.wait_passed(r)

    def wait_sends(self):
        for cp in self._mine() + self._relayed() + self._passed():
            cp.wait_send()


class _ChipReduceScatter:
    SLOTS = 6

    def __init__(self, arrays, l_sem, d_send, d_recv, i_send, i_recv):
        self.arrays = arrays
        self.l_sem, self.d_send, self.d_recv, self.i_send, self.i_recv = l_sem, d_send, d_recv, i_send, i_recv

    @staticmethod
    def buffers(rows, cols, dtype, staged=True):
        stage = [pltpu.VMEM((4, rows, cols), dtype)] if staged else []
        return stage + [pltpu.VMEM((4, rows, cols), dtype), pltpu.VMEM((3, rows, cols), dtype),
                        pltpu.VMEM((2, rows, cols), dtype), pltpu.VMEM((2, rows // 2, cols), dtype)]

    @classmethod
    def semaphores(cls, n):
        return [pltpu.SemaphoreType.DMA((n,)), pltpu.SemaphoreType.DMA((n, 4)), pltpu.SemaphoreType.DMA((n, 4)),
                pltpu.SemaphoreType.DMA((n, cls.SLOTS)), pltpu.SemaphoreType.DMA((n, cls.SLOTS))]

    def _pick(self, which):
        return list(enumerate(self.arrays)) if which is None else [(n, self.arrays[n]) for n in which]

    @staticmethod
    def _chip(r):
        dev = _me() if r is None else _peer(r)
        return 2 * dev[0] + dev[1]

    def _staging(self, which):
        c = _me()[2]
        return [pltpu.make_async_copy(a["part"].at[pl.ds(0, 4), c], a["stage"], self.l_sem.at[n])
                for n, a in self._pick(which) if a["staged"]]

    def _first(self, which, chip):
        other = 1 - _me()[2]
        return [_remote(a["part"].at[chip, other], a["sib"].at[chip], self.d_send.at[n, chip], self.d_recv.at[n, chip],
                        _peer(SIBLING)) for n, a in self._pick(which)]

    @staticmethod
    def _halves(a):
        half = a["rcv"].shape[1] // 2
        return pl.ds(0, half), pl.ds(half, half)

    def _hops(self, n, a):
        h0, h1 = self._halves(a)
        x, y = _peer(ACROSS_X), _peer(ACROSS_Y)
        snd, rcv, relay = a["snd"], a["rcv"], a["relay"]
        pairs = [(snd.at[2, h0], relay.at[0], x), (snd.at[2, h1], relay.at[1], y),
                 (snd.at[0, h0], rcv.at[0, h0], x), (snd.at[0, h1], rcv.at[0, h1], x),
                 (snd.at[1, h1], rcv.at[1, h1], y), (snd.at[1, h0], rcv.at[1, h0], y)]
        return [_remote(s, d, self.i_send.at[n, k], self.i_recv.at[n, k], to) for k, (s, d, to) in enumerate(pairs)]

    def _mine(self, a, chip, rows=None):
        src = a["stage"].at[chip] if a["staged"] else a["part"].at[chip, _me()[2]]
        mine, sib = (src[...], a["sib"][chip]) if rows is None else (src[rows, :], a["sib"][chip, rows, :])
        return mine.astype(F32) + sib.astype(F32)

    def start(self, which=None, chips=None):
        if chips is None:
            for cp in self._staging(which):
                cp.start()
        for chip in range(4) if chips is None else chips:
            for cp in self._first(which, chip):
                cp.start()

    def send_far(self, which=None):
        far = self._chip(ACROSS_BOTH)
        for cp in self._staging(which):
            cp.wait()
        for cp in self._first(which, far):
            cp.wait_recv()
        for n, a in self._pick(which):
            hops = self._hops(n, a)
            a["snd"][2] = self._mine(a, far).astype(a["snd"].dtype)
            hops[0].start()
            hops[1].start()

    def send_near(self, r, which=None):
        chip = self._chip(r)
        for cp in self._first(which, chip):
            cp.wait_recv()
        for n, a in self._pick(which):
            h0, h1 = self._halves(a)
            hops = self._hops(n, a)
            if r == ACROSS_X:
                a["snd"][0, h0, :] = self._mine(a, chip, h0).astype(a["snd"].dtype)
                hops[2].start()
            else:
                a["snd"][1, h1, :] = self._mine(a, chip, h1).astype(a["snd"].dtype)
                hops[4].start()

    def exchange(self, which=None):
        self.send_far(which)
        self.send_near(ACROSS_X, which)
        self.send_near(ACROSS_Y, which)

    def fold(self, which=None):
        across_x, across_y = self._chip(ACROSS_X), self._chip(ACROSS_Y)
        for n, a in self._pick(which):
            h0, h1 = self._halves(a)
            hops = self._hops(n, a)
            dtype = a["snd"].dtype
            hops[1].wait_recv()
            a["snd"][0, h1, :] = (self._mine(a, across_x, h1) + a["relay"][1].astype(F32)).astype(dtype)
            hops[3].start()
            hops[0].wait_recv()
            a["snd"][1, h0, :] = (self._mine(a, across_y, h0) + a["relay"][0].astype(F32)).astype(dtype)
            hops[5].start()

    def finish(self, which=None):
        home = self._chip(None)
        for cp in self._first(which, home):
            cp.wait_recv()
        for n, a in self._pick(which):
            hops = self._hops(n, a)
            a["out"][...] = self._mine(a, home)
            hops[2].wait_recv()
            hops[3].wait_recv()
            a["out"][...] += a["rcv"][0].astype(F32)
            hops[4].wait_recv()
            hops[5].wait_recv()
            a["out"][...] += a["rcv"][1].astype(F32)

    def wait_sends(self, which=None):
        for chip in range(4):
            for cp in self._first(which, chip):
                cp.wait_send()
        for n, a in self._pick(which):
            for cp in self._hops(n, a):
                cp.wait_send()


def _direct_exchange(src_of, dst_of, send_sems, recv_sems):
    me = _me()
    copies = [_remote(src_of(_peer(r)), dst_of(me), send_sems.at[r - 1], recv_sems.at[r - 1], _peer(r))
              for r in range(1, N_DEV)]
    for cp in copies:
        cp.start()
    return copies


def _wait_direct(copies):
    for cp in copies:
        cp.wait_recv()
    for cp in copies:
        cp.wait_send()


def _forward(x, c, w_ada, b_ada, small, ln_g, ln_b, mine, following):
    assert DEPTH == 2
    cols = w_ada.shape[2]
    shard = mine[0].shape[0]
    pair = 2 * shard

    def body(x_hbm, c_ref, wada_hbm, bada_ref, wpool_ref, pscale_ref, slng_ref, slnb_ref, wsgu_ref, bsgut_ref,
             lng_ref, lnb_ref, wint_hbm, wout_hbm, next_in_hbm, next_out_hbm,
             out0_ref, y0_ref, cdf0_ref, proj0_hbm, out1_ref, y1_ref, cdf1_ref, proj1_hbm,
             wint_keep, wout_keep, wint_next, wout_next, acts_ref, mod_ref,
             wint_v, wout_v, h_buf, proj_blk, proj_tile, halo_ref, x_ref, w_send, w_recv, w_local, p_sems,
             t_sems, in_sems, act_all, act_src, part, mod_recv, wada_ref, a_send, a_recv, m_send, m_recv,
             n_send, n_recv, n_local, f_sems):
        step = pl.program_id(0)
        chip_of = lambda dev: 2 * dev[0] + dev[1]

        def hosted():
            return [_TwoLevelGather(out, n_send.at[n], n_recv.at[n], src=src)
                    for n, (out, src) in enumerate(((wint_next, next_in_hbm), (wout_next, next_out_hbm)))]

        def hosted_own():
            me = _me()
            return [pltpu.make_async_copy(g.src, g._slot(me), n_local.at[n]) for n, g in enumerate(hosted())]

        def gathers():
            return (_TwoLevelGather(wint_v, w_send.at[0], w_recv.at[0], src=wint_hbm),
                    _TwoLevelGather(wout_v, w_send.at[1], w_recv.at[1], src=wout_hbm))

        def keeps():
            return [pltpu.make_async_copy(wint_v, wint_keep, w_local.at[2]),
                    pltpu.make_async_copy(wout_v, wout_keep, w_local.at[3])]

        def tile_read(t):
            slot = t % 2
            return pltpu.make_async_copy(proj0_hbm.at[pl.ds(pl.multiple_of(t * TM, TM), TM)], proj_tile.at[slot],
                                         t_sems.at[slot])

        def first_layer_start():
            writes = []

            def project(n, dev):
                first = pl.multiple_of(chip_of(dev) * pair, pair)
                if n >= 2:
                    writes[n - 2].wait()

                @pl.loop(0, N_TILES)
                def _(t):
                    rows = pl.ds(pl.multiple_of(t * TM, TM), TM)
                    proj_blk[n % 2, rows, :] = _dot_nt(h_buf[rows, :], wint_v[pl.ds(first, pair), :])

                cp = pltpu.make_async_copy(proj_blk.at[n % 2], proj0_hbm.at[:, pl.ds(first, pair)], p_sems.at[n % 2])
                cp.start(priority=BULK)
                writes.append(cp)

            me = _me()
            halo_ref[...] = jnp.zeros_like(halo_ref)
            gather_in, gather_out = gathers()
            own_in = pltpu.make_async_copy(wint_hbm, gather_in._slot(me), w_local.at[0])
            own_out = pltpu.make_async_copy(wout_hbm, gather_out._slot(me), w_local.at[1])
            x_load = pltpu.make_async_copy(x_hbm, x_ref, in_sems.at[0])
            x_load.start(priority=BULK)
            wada_load = pltpu.make_async_copy(wada_hbm, wada_ref, in_sems.at[1])
            wada_load.start(priority=BULK)
            mine_index = _index(me)
            cval = c_ref[...]
            act_src[...] = jnp.zeros_like(act_src)
            act_src[0:1, :] = cval * jax.nn.sigmoid(cval)
            act_all[mine_index] = act_src[...]
            act_copies = _direct_exchange(lambda p: act_src, lambda m: act_all.at[_index(m)], a_send, a_recv)

            own_in.start()
            own_out.start()
            gather_in.send_first()

            _wait_direct(act_copies)
            acts = jnp.concatenate([act_all[j, 0:1, :] for j in range(N_DEV)], axis=0)
            acts_ref[...] = acts
            part[...] = jnp.zeros_like(part)
            wada_load.wait()
            for l in range(DEPTH):
                res = lax.dot_general(acts, wada_ref[l], (((1,), (0,)), ((), ())), preferred_element_type=F32,
                                      precision=lax.Precision.HIGHEST)
                for b in range(N_DEV):
                    part[b, l:l + 1, :] = res[b:b + 1, :]
            mod_recv[mine_index] = part[mine_index]
            mod_copies = _direct_exchange(lambda p: part.at[_index(p)], lambda m: mod_recv.at[_index(m)],
                                          m_send, m_recv)
            gather_in.send_second()
            gather_out.send_mine()

            _wait_direct(mod_copies)
            for l in range(DEPTH):
                for j in range(N_DEV):
                    sl = slice(j * cols, (j + 1) * cols)
                    mod_ref[l:l + 1, sl] = mod_recv[j, l:l + 1, :] + bada_ref[l:l + 1, sl]
            x_load.wait()
            shift = mod_ref[0:1, 0:D_MODEL]
            scale = mod_ref[0:1, D_MODEL:2 * D_MODEL]

            @pl.loop(0, N_TILES)
            def _(t):
                rows = pl.ds(pl.multiple_of(t * TM, TM), TM)
                xn, _ = _ln(x_ref[rows, :])
                h_buf[rows, :] = (xn * (1.0 + scale) + shift).astype(BF16)

            gather_in.relay()
            own_in.wait()
            gather_in.wait_sibling()
            project(0, me)
            gather_in.pass_near()
            gather_in.wait_passed(ACROSS_X)
            project(1, _peer(ACROSS_X))
            gather_out.relay()
            for cp in hosted_own():
                cp.start()
            for g in hosted():
                g.send_mine()
            gather_in.wait_passed(ACROSS_Y)
            project(2, _peer(ACROSS_Y))
            gather_in.pass_far()
            gather_in.wait_passed(ACROSS_BOTH)
            project(3, _peer(ACROSS_BOTH))

            gather_out.pass_on()
            gather_out.wait_rest()
            own_out.wait()
            for cp in keeps():
                cp.start(priority=BULK)
            writes[2].wait()
            writes[3].wait()
            tile_read(0).start(priority=BULK)

        def fetches():
            return [pltpu.make_async_copy(wint_next, wint_v, f_sems.at[0]),
                    pltpu.make_async_copy(wout_next, wout_v, f_sems.at[1])]

        def tile_write(t):
            slot = t % 2
            return pltpu.make_async_copy(proj_tile.at[slot], proj1_hbm.at[pl.ds(pl.multiple_of(t * TM, TM), TM)],
                                         t_sems.at[slot])

        def mix_and_close(layer, tile, rows, proj, out_ref, y_ref, cdf_ref):
            weights = _MixWeights(layer, wpool_ref, pscale_ref, slng_ref, slnb_ref, wsgu_ref, bsgut_ref)
            xt = x_ref[rows, :]
            gate = mod_ref[layer:layer + 1, 2 * D_MODEL:]
            cat, cdf_ref[...] = _mix_forward(proj, halo_ref[...], tile, weights)
            halo_ref[...] = proj[TM - HALO:, 0:D_POOL]
            if layer == 1:
                @pl.when(tile == 0)
                def _():
                    fetches()[1].wait()
            y = _dot(cat.astype(BF16), wout_v[...])
            y_ref[...] = y
            zn, _ = _ln(ALPHA * xt + gate * y)
            out = zn * lng_ref[layer:layer + 1, :] + lnb_ref[layer:layer + 1, :]
            out_ref[...] = out
            if layer == 0:
                x_ref[rows, :] = out

        def first_layer_tile(tile):
            @pl.when(tile + 1 < N_TILES)
            def _():
                tile_read(tile + 1).start(priority=BULK)

            tile_read(tile).wait()
            rows = pl.ds(pl.multiple_of(tile * TM, TM), TM)
            mix_and_close(0, tile, rows, proj_tile[tile % 2], out0_ref, y0_ref, cdf0_ref)

        def second_layer_tile(tile):
            rows = pl.ds(pl.multiple_of(tile * TM, TM), TM)
            shift = mod_ref[1:2, 0:D_MODEL]
            scale = mod_ref[1:2, D_MODEL:2 * D_MODEL]
            xn, _ = _ln(x_ref[rows, :])
            h = (xn * (1.0 + scale) + shift).astype(BF16)

            @pl.when(tile >= 2)
            def _():
                tile_write(tile - 2).wait()

            proj_tile[tile % 2] = _dot_nt(h, wint_v[...])
            tile_write(tile).start()
            mix_and_close(1, tile, rows, proj_tile[tile % 2], out1_ref, y1_ref, cdf1_ref)

        @pl.when(step < N_TILES)
        def _():
            @pl.when(step == 0)
            def _():
                first_layer_start()

            @pl.when(step == 1)
            def _():
                for g in hosted():
                    g.relay()

            @pl.when(step == N_TILES // 2)
            def _():
                for g in hosted():
                    g.pass_near()

            @pl.when(step == N_TILES - 1)
            def _():
                for g in hosted():
                    g.pass_far()
                for g in gathers():
                    g.wait_sends()
                for cp in keeps() + hosted_own():
                    cp.wait()
                hosted()[0].wait_rest()
                fetches()[0].start(priority=BULK)

            first_layer_tile(step)

        @pl.when(step >= N_TILES)
        def _():
            @pl.when(step == N_TILES)
            def _():
                halo_ref[...] = jnp.zeros_like(halo_ref)
                hosted()[1].wait_rest()
                fetches()[1].start()
                fetches()[0].wait()

            second_layer_tile(step - N_TILES)

            @pl.when(step == 2 * N_TILES - 1)
            def _():
                for g in hosted():
                    g.wait_sends()
                tile_write(N_TILES - 2).wait()
                tile_write(N_TILES - 1).wait()

    first = lambda w: pl.BlockSpec((TM, w), lambda i: (jnp.minimum(i, N_TILES - 1), 0))
    second = lambda w: pl.BlockSpec((TM, w), lambda i: (jnp.maximum(i - N_TILES, 0), 0))
    gather_sems = pltpu.SemaphoreType.DMA((2, GATHER_SEMS))
    seven = pltpu.SemaphoreType.DMA((7,))
    per_layer = [jax.ShapeDtypeStruct((SEQ, D_MODEL), F32), jax.ShapeDtypeStruct((SEQ, D_MODEL), F32),
                 jax.ShapeDtypeStruct((SEQ, 2 * D_SGU), F32), jax.ShapeDtypeStruct((SEQ, D_IN), F32)]
    gathered = [jax.ShapeDtypeStruct((D_IN, D_MODEL), BF16), jax.ShapeDtypeStruct((D_MODEL, D_MODEL), BF16)]
    res = pl.pallas_call(
        body,
        name="layers_fwd",
        grid=(DEPTH * N_TILES,),
        in_specs=[ANY, _const_spec(c.shape), ANY, _const_spec(b_ada.shape)] + [_const_spec(s) for s in SMALL_SPECS]
                 + [_const_spec((DEPTH, D_MODEL)), _const_spec((DEPTH, D_MODEL))] + [ANY] * 4,
        out_specs=[first(D_MODEL), first(D_MODEL), first(2 * D_SGU), ANY,
                   second(D_MODEL), second(D_MODEL), second(2 * D_SGU), ANY] + [ANY] * 4
                  + [_const_spec((N_DEV, D_MODEL)), _const_spec((DEPTH, 3 * D_MODEL))],
        out_shape=per_layer * 2 + gathered * 2 + [jax.ShapeDtypeStruct((N_DEV, D_MODEL), F32),
                                                  jax.ShapeDtypeStruct((DEPTH, 3 * D_MODEL), F32)],
        scratch_shapes=[pltpu.VMEM((D_IN, D_MODEL), BF16), pltpu.VMEM((D_MODEL, D_MODEL), BF16),
                        pltpu.VMEM((SEQ, D_MODEL), BF16), pltpu.VMEM((2, SEQ, pair), F32),
                        pltpu.VMEM((2, TM, D_IN), F32), pltpu.VMEM((HALO, D_POOL), F32),
                        pltpu.VMEM((SEQ, D_MODEL), F32),
                        gather_sems, gather_sems, pltpu.SemaphoreType.DMA((4,)), pltpu.SemaphoreType.DMA((2,)),
                        pltpu.SemaphoreType.DMA((2,)), pltpu.SemaphoreType.DMA((2,)),
                        pltpu.VMEM((N_DEV, 8, D_MODEL), F32), pltpu.VMEM((8, D_MODEL), F32),
                        pltpu.VMEM((N_DEV, 8, cols), F32), pltpu.VMEM((N_DEV, 8, cols), F32),
                        pltpu.VMEM(w_ada.shape, F32), seven, seven, seven, seven,
                        gather_sems, gather_sems, pltpu.SemaphoreType.DMA((2,)), pltpu.SemaphoreType.DMA((2,))],
        compiler_params=pltpu.CompilerParams(dimension_semantics=("arbitrary",), vmem_limit_bytes=VMEM_LIMIT),
    )(x, c, w_ada, b_ada, *small, ln_g, ln_b, *mine, *following)
    return res[0:4], res[4:8], res[8:10], res[10:12], res[12], res[13]


ADA_CHUNK = 256


def _grad_tail(dproj, h, cat, dy, small, dmod8, loss_lanes, w_ada, m_ada, v_ada, act_t, b_ada, m_bada, v_bada):
    shard_in, shard_out, shard_small = D_IN // N_DEV, D_MODEL // N_DEV, small.shape[2]
    cols = w_ada.shape[2]
    W_IN, W_OUT, SMALL = 0, 1, 2

    def body(dproj_hbm, h_hbm, cat_hbm, dy_hbm, small_hbm, dmod_ref, lanes_ref, wada_hbm, mada_hbm, vada_hbm,
             act_ref, bada_ref, mbada_ref, vbada_ref,
             gwin_ref, gwout_ref, stot_ref, loss_ref, gada_hbm, dada_hbm, nmada_hbm, nvada_hbm,
             gb_ref, db_ref, nmb_ref, nvb_ref,
             dproj_v, h_v, cat_v, dy_v, part_in, part_out, own_small, loss_src, loss_all, dmod_all, ada_in, ada_out,
             *rest):
        bufs, rest = rest[:13], rest[13:]
        load_sems, rs_sems = rest[0], rest[1:6]
        m_send, m_recv, g_send, g_recv, s_send, s_recv, ada_lsem, ada_ssem = rest[6:]
        mine = _index(_me())

        def update_ada():
            upper = (mine % 2) == 1

            def dmod_of(layer):
                rows = []
                for b in range(N_DEV):
                    r = dmod_all[b, pl.ds(4 * layer + mine // 2, 1), :]
                    rows.append(jnp.where(upper, r[:, cols:], r[:, :cols]))
                return jnp.concatenate(rows, axis=0)

            chunks = [(layer, c) for layer in range(DEPTH) for c in range(D_MODEL // ADA_CHUNK)]

            def loads(i):
                layer, c = chunks[i]
                rows = pl.ds(c * ADA_CHUNK, ADA_CHUNK)
                return [pltpu.make_async_copy(src.at[layer, rows], ada_in.at[i % 2, k], ada_lsem.at[i % 2, k])
                        for k, src in enumerate((wada_hbm, mada_hbm, vada_hbm))]

            def stores(i):
                layer, c = chunks[i]
                rows = pl.ds(c * ADA_CHUNK, ADA_CHUNK)
                return [pltpu.make_async_copy(ada_out.at[i % 2, k], dst.at[layer, rows], ada_ssem.at[i % 2, k])
                        for k, dst in enumerate((gada_hbm, dada_hbm, nmada_hbm, nvada_hbm))]

            for cp in loads(0):
                cp.start(priority=BULK)
            dmods = {}
            for i, (layer, c) in enumerate(chunks):
                if i + 1 < len(chunks):
                    for cp in loads(i + 1):
                        cp.start(priority=BULK)
                for cp in loads(i):
                    cp.wait()
                if i >= 2:
                    for cp in stores(i - 2):
                        cp.wait()
                if layer not in dmods:
                    dmods[layer] = dmod_of(layer)
                act = act_ref[pl.ds(c * ADA_CHUNK, ADA_CHUNK), :]
                g = act[:, 0:1] * dmods[layer][0:1, :]
                for b in range(1, N_DEV):
                    g = g + act[:, b:b + 1] * dmods[layer][b:b + 1, :]
                slot = i % 2
                delta, new_m, new_v = _adamw_math(ada_in[slot, 0], g, ada_in[slot, 1], ada_in[slot, 2])
                ada_out[slot, 0] = g
                ada_out[slot, 1] = delta
                ada_out[slot, 2] = new_m
                ada_out[slot, 3] = new_v
                for cp in stores(i):
                    cp.start(priority=BULK)
            for i in (len(chunks) - 2, len(chunks) - 1):
                for cp in stores(i):
                    cp.wait()

            total = dmod_all[0]
            for b in range(1, N_DEV):
                total = total + dmod_all[b]
            width = total.shape[1]
            for layer in range(DEPTH):
                for q in range(4):
                    gb_ref[layer:layer + 1, q * width:(q + 1) * width] = total[4 * layer + q:4 * layer + q + 1, :]
            db_ref[...], nmb_ref[...], nvb_ref[...] = _adamw_math(bada_ref[...], gb_ref[...], mbada_ref[...],
                                                                  vbada_ref[...])

        order = (ACROSS_BOTH, ACROSS_X, ACROSS_Y, None)
        chips = [_ChipReduceScatter._chip(r) for r in order]
        loads = [pltpu.make_async_copy(s, d, load_sems.at[n]) for n, (s, d) in enumerate(
            ((cat_hbm, cat_v), (dy_hbm, dy_v), (h_hbm, h_v)))]
        loads += [pltpu.make_async_copy(dproj_hbm.at[:, pl.ds(pl.multiple_of(chip * 2 * shard_in, 2 * shard_in),
                                                             2 * shard_in)], dproj_v.at[n], load_sems.at[3 + n])
                  for n, chip in enumerate(chips)]
        for cp in loads[2:] + loads[:2]:
            cp.start(priority=BULK)
        arrays = [dict(part=part_in, out=gwin_ref, staged=False, sib=bufs[0], snd=bufs[1], rcv=bufs[2], relay=bufs[3]),
                  dict(part=part_out, out=gwout_ref, staged=False, sib=bufs[4], snd=bufs[5], rcv=bufs[6],
                       relay=bufs[7]),
                  dict(part=small_hbm, out=own_small, staged=True, stage=bufs[8], sib=bufs[9], snd=bufs[10],
                       rcv=bufs[11], relay=bufs[12])]
        scatter = _ChipReduceScatter(arrays, *rs_sems)
        scatter.start([SMALL])
        dmod_all[mine] = dmod_ref[...]
        dmod_copies = _direct_exchange(lambda p: dmod_ref, lambda m: dmod_all.at[_index(m)], m_send, m_recv)
        loss_src[...] = jnp.full(loss_src.shape, (0.5 / D_MODEL) * jnp.sum(lanes_ref[...]), F32)
        loss_all[mine] = loss_src[...]
        loss_copies = _direct_exchange(lambda p: loss_src, lambda m: loss_all.at[_index(m)], s_send, s_recv)

        gather = _TwoLevelGather(stot_ref, g_send, g_recv)
        loads[2].wait()
        for n, chip in enumerate(chips):
            loads[3 + n].wait()
            res = _dot_tn(dproj_v[n], h_v[...]).astype(BF16)
            part_in[chip, 0] = res[:shard_in]
            part_in[chip, 1] = res[shard_in:]
            scatter.start([W_IN], chips=[chip])
            if n == 0:
                scatter.exchange([SMALL])
            if n == 1:
                scatter.send_far([W_IN])
                scatter.fold([SMALL])
            if n == 2:
                scatter.send_near(ACROSS_X, [W_IN])
                scatter.finish([SMALL])
                stot_ref[mine] = own_small[...]
                gather.send_mine()
            if n == 3:
                scatter.send_near(ACROSS_Y, [W_IN])

        loads[0].wait()
        loads[1].wait()
        for blk in range(2):
            res = _dot_tn(cat_v[:, blk * 512:(blk + 1) * 512], dy_v[...]).astype(BF16)
            for s in range(4):
                part_out[2 * blk + s // 2, s % 2] = res[s * shard_out:(s + 1) * shard_out]
        scatter.start([W_OUT])
        scatter.fold([W_IN])
        scatter.exchange([W_OUT])
        gather.relay()
        _wait_direct(dmod_copies)
        update_ada()
        scatter.fold([W_OUT])
        gather.pass_on()
        gather.wait_rest()
        _wait_direct(loss_copies)
        total = loss_all[0]
        for j in range(1, N_DEV):
            total = total + loss_all[j]
        loss_ref[...] = total
        scatter.finish([W_IN])
        scatter.finish([W_OUT])
        gather.wait_sends()
        scatter.wait_sends()

    buffers = _ChipReduceScatter.buffers
    comm_scratch = (buffers(shard_in, D_MODEL, BF16, staged=False) + buffers(shard_out, D_MODEL, BF16, staged=False)
                    + buffers(shard_small, 128, F32))
    comm_scratch += [pltpu.SemaphoreType.DMA((7,))] + _ChipReduceScatter.semaphores(3)
    comm_scratch += [pltpu.SemaphoreType.DMA((n,)) for n in (7, 7, GATHER_SEMS, GATHER_SEMS, 7, 7)]
    comm_scratch += [pltpu.SemaphoreType.DMA((2, 3)), pltpu.SemaphoreType.DMA((2, 4))]
    return pl.pallas_call(
        body,
        name="grad_tail",
        in_specs=[ANY] * 5 + [VMEM, VMEM] + [ANY] * 3 + [VMEM] * 4,
        out_specs=[VMEM] * 4 + [ANY] * 4 + [VMEM] * 4,
        out_shape=[jax.ShapeDtypeStruct((shard_in, D_MODEL), F32), jax.ShapeDtypeStruct((shard_out, D_MODEL), F32),
                   jax.ShapeDtypeStruct((N_DEV, shard_small, 128), F32), jax.ShapeDtypeStruct((8, 128), F32)]
                  + [jax.ShapeDtypeStruct(w_ada.shape, F32)] * 4 + [jax.ShapeDtypeStruct(b_ada.shape, F32)] * 4,
        scratch_shapes=[pltpu.VMEM((4, SEQ, 2 * shard_in), BF16), pltpu.VMEM(h.shape, BF16), pltpu.VMEM(cat.shape, BF16),
                        pltpu.VMEM(dy.shape, BF16), pltpu.VMEM((4, 2, shard_in, D_MODEL), BF16),
                        pltpu.VMEM((4, 2, shard_out, D_MODEL), BF16), pltpu.VMEM((shard_small, 128), F32),
                        pltpu.VMEM((8, 128), F32), pltpu.VMEM((N_DEV, 8, 128), F32),
                        pltpu.VMEM((N_DEV,) + dmod8.shape, F32), pltpu.VMEM((2, 3, ADA_CHUNK, cols), F32),
                        pltpu.VMEM((2, 4, ADA_CHUNK, cols), F32)] + comm_scratch,
        compiler_params=pltpu.CompilerParams(vmem_limit_bytes=VMEM_LIMIT),
    )(dproj, h, cat, dy, small, dmod8, loss_lanes, w_ada, m_ada, v_ada, act_t, b_ada, m_bada, v_bada)


SMALL_NAMES = ("w_pool", "w_sgu", "pool_scale", "sgu_ln_g", "sgu_ln_b", "b_sgu", "ln_g", "ln_b")
SMALL_ROWS = (512, 512, 4, 4, 4, 4, 8, 8)


def _adamw_small(g_packed, ws, ms, vs, name):
    n = len(SMALL_NAMES)

    def body(g_ref, *refs):
        w_refs, m_refs, v_refs = refs[:n], refs[n:2 * n], refs[2 * n:3 * n]
        outs = refs[3 * n:]

        def update(p, at, g):
            delta, new_m, new_v = _adamw_math(w_refs[p][at], g, m_refs[p][at], v_refs[p][at])
            outs[p][at] = g
            outs[n + p][at] = delta
            outs[2 * n + p][at] = new_m
            outs[3 * n + p][at] = new_v

        row = 0
        for p, r in enumerate(SMALL_ROWS):
            shape = ws[p].shape
            for layer in range(DEPTH):
                first = layer * PACK_ROWS + row
                if len(shape) == 4:
                    for k in range(shape[1]):
                        update(p, (layer, k), g_ref[first + k * shape[2]:first + (k + 1) * shape[2], :])
                elif len(shape) == 3:
                    update(p, (layer,), g_ref[first:first + r, :])
                else:
                    g = jnp.concatenate([g_ref[first + k:first + k + 1, :] for k in range(r)], axis=1)
                    update(p, (slice(layer, layer + 1), slice(None)), g)
            row += r

    res = pl.pallas_call(
        body,
        name=name,
        out_shape=[jax.ShapeDtypeStruct(w.shape, F32) for w in ws] * 4,
        compiler_params=pltpu.CompilerParams(vmem_limit_bytes=VMEM_LIMIT),
    )(g_packed, *ws, *ms, *vs)
    return res[:n], res[n:2 * n], res[2 * n:3 * n], res[3 * n:]


def kernel(x, c, w_ada, b_ada, w_in, w_pool, pool_scale, sgu_ln_g, sgu_ln_b, w_sgu, b_sgu, w_out, ln_g, ln_b, loss_target, m_w_ada, m_b_ada, m_w_in, m_w_pool, m_pool_scale, m_sgu_ln_g, m_sgu_ln_b, m_w_sgu, m_b_sgu, m_w_out, m_ln_g, m_ln_b, v_w_ada, v_b_ada, v_w_in, v_w_pool, v_pool_scale, v_sgu_ln_g, v_sgu_ln_b, v_w_sgu, v_b_sgu, v_w_out, v_ln_g, v_ln_b):
    small_w = dict(w_pool=w_pool, w_sgu=w_sgu, pool_scale=pool_scale, sgu_ln_g=sgu_ln_g, sgu_ln_b=sgu_ln_b,
                   b_sgu=b_sgu, ln_g=ln_g, ln_b=ln_b)
    small_m = dict(w_pool=m_w_pool, w_sgu=m_w_sgu, pool_scale=m_pool_scale, sgu_ln_g=m_sgu_ln_g,
                   sgu_ln_b=m_sgu_ln_b, b_sgu=m_b_sgu, ln_g=m_ln_g, ln_b=m_ln_b)
    small_v = dict(w_pool=v_w_pool, w_sgu=v_w_sgu, pool_scale=v_pool_scale, sgu_ln_g=v_sgu_ln_g,
                   sgu_ln_b=v_sgu_ln_b, b_sgu=v_b_sgu, ln_g=v_ln_g, ln_b=v_ln_b)

    wint_loc = jnp.transpose(w_in, (0, 2, 1)).astype(BF16)
    wout_loc = w_out.astype(BF16)
    small = (w_pool, pool_scale, sgu_ln_g, sgu_ln_b, w_sgu, jnp.transpose(b_sgu, (0, 2, 1)))
    (out0, y0, cdf0, proj0), (cur, y1, cdf1, proj1), gathered0, gathered1, act_all, mod = _forward(
        x[0], c, w_ada, b_ada, small, ln_g, ln_b, [wint_loc[0], wout_loc[0]], [wint_loc[1], wout_loc[1]])
    w_int, w_outf = [gathered0[0], gathered1[0]], [gathered0[1], gathered1[1]]
    acts = [(x[0], proj0, y0, cdf0), (out0, proj1, y1, cdf1)]

    shard_in, shard_out = D_IN // N_DEV, D_MODEL // N_DEV
    a, b = cur, loss_target[0]
    loss_lanes, carry, pending = None, (), []
    g_w_in_t, g_w_out = [None] * DEPTH, [None] * DEPTH
    for l in reversed(range(DEPTH)):
        dx, dproj, h, cat, dy, small_grads, dmod8, lanes, *shards = _layer_backward(
            l, a, b, *acts[l], mod, w_int[l], w_outf[l], small, ln_g, l == DEPTH - 1, f"layer_bwd_{l}",
            carry=carry, reduce=pending)
        if shards:
            g_w_in_t[l + 1], g_w_out[l + 1] = shards
        if l == DEPTH - 1:
            loss_lanes = lanes
        if l > 0:
            g_in, g_out = _grad_matmuls(dproj, h, cat, dy, f"grad_w_{l}")
            pending = [g_in.reshape(4, 2, shard_in, D_MODEL), g_out.reshape(4, 2, shard_out, D_MODEL)]
        carry = (small_grads, dmod8)
        a = b = dx

    (g_w_in_t[0], g_w_out[0], small_tot, loss_tile, g_w_ada, d_w_ada, nm_w_ada, nv_w_ada,
     g_b_ada, d_b_ada, nm_b_ada, nv_b_ada) = _grad_tail(
        dproj, h, cat, dy, small_grads.reshape(4, 2, DEPTH * PACK_ROWS // N_DEV, 128), dmod8, loss_lanes,
        w_ada, m_w_ada, v_w_ada, jnp.transpose(act_all), b_ada, m_b_ada, v_b_ada)
    loss = loss_tile[0, 0]

    flat = lambda t: t.reshape(-1, t.shape[-1])
    to_t = lambda t: flat(jnp.transpose(t, (0, 2, 1)))
    from_t = lambda t: jnp.transpose(t.reshape(DEPTH, shard_in, D_MODEL), (0, 2, 1))
    *updated, grad_x = _adamw(to_t(w_in), g_w_in_t, to_t(m_w_in), to_t(v_w_in), shard_in // 2, "adamw_w_in",
                              ride=[a])
    grad_x = grad_x[None]
    g_w_in, d_w_in, nm_w_in, nv_w_in = [from_t(t) for t in updated]
    gwout, d_w_out, nm_w_out, nv_w_out = [t.reshape(w_out.shape) for t in _adamw(
        flat(w_out), g_w_out, flat(m_w_out), flat(v_w_out), shard_out, "adamw_w_out")]
    small_out = _adamw_small(small_tot.reshape(DEPTH * PACK_ROWS, 128), [small_w[n] for n in SMALL_NAMES],
                             [small_m[n] for n in SMALL_NAMES], [small_v[n] for n in SMALL_NAMES], "adamw_small")
    gs, ds, ms, vs = [dict(zip(SMALL_NAMES, group)) for group in small_out]

    def ordered(w_ada_, b_ada_, w_in_, small, w_out_):
        return (w_ada_, b_ada_, w_in_, small["w_pool"], small["pool_scale"], small["sgu_ln_g"], small["sgu_ln_b"],
                small["w_sgu"], small["b_sgu"], w_out_, small["ln_g"], small["ln_b"])

    return (loss, grad_x,
            *ordered(g_w_ada, g_b_ada, g_w_in, gs, gwout),
            *ordered(d_w_ada, d_b_ada, d_w_in, ds, d_w_out),
            *ordered(nm_w_ada, nm_b_ada, nm_w_in, ms, nm_w_out),
            *ordered(nv_w_ada, nv_b_ada, nv_w_in, vs, nv_w_out))
```
